```python
import math
import jax, jax.numpy as jnp
from jax import lax
import numpy as np

D_MODEL = 1024
BATCH = 8
SEQ = 8192
DEPTH = 1

HEAD_DIM = 64
HEADS_PER_GROUP = 4
DILATED_GROUPS = ((128, 1), (512, 4), (2048, 16))
N_ATTN_GROUPS = len(DILATED_GROUPS)
N_ATTN_HEADS = N_ATTN_GROUPS * HEADS_PER_GROUP
ATTN_WIDTH = N_ATTN_HEADS * HEAD_DIM
ATTN_OUT_WIDTH = HEADS_PER_GROUP * HEAD_DIM
BLOCK = 128
N_BUCKETS = 32
MAX_DISTANCE = 2048
NEG_INF = -1e30
SSM_GROUP = 16
SSM_WIDTH = 512
SSM_GROUPS = SSM_WIDTH // SSM_GROUP
SSM_STATE = 64
DT_MIN = 1e-3
DT_MAX = 1e-1
D_FF = 2816
EPS = 1e-6
IN_WIDTH = 3 * ATTN_WIDTH + SSM_WIDTH + 2 * D_MODEL

kernel_name = "hybrid_dilated_attn_s5_macaron"


def rms_norm(x, g):
    xf = x.astype(jnp.float32)
    y = xf * lax.rsqrt(jnp.mean(xf * xf, axis=-1, keepdims=True) + EPS)
    return (y * g.astype(jnp.float32)).astype(x.dtype)


def swiglu(h, w_gate, w_up, w_down):
    return (jax.nn.silu(h @ w_gate) * (h @ w_up)) @ w_down


def t5_bucket(dist):
    max_exact = N_BUCKETS // 2
    d = jnp.maximum(dist, 1).astype(jnp.float32)
    large = max_exact + (jnp.log(d / max_exact) / math.log(MAX_DISTANCE / max_exact)
                         * (N_BUCKETS - max_exact)).astype(jnp.int32)
    large = jnp.minimum(large, N_BUCKETS - 1)
    return jnp.where(dist < max_exact, dist, large)


def dilated_group_attention(q, k, v, bias_table_g, window, dilation):
    B, L, H, Dh = q.shape
    M = L // dilation
    n_steps = window // dilation
    nb = -(-M // BLOCK)
    Mp = nb * BLOCK

    def to_sub(t):
        t = t.reshape(B, M, dilation, H, Dh).transpose(0, 2, 1, 3, 4).reshape(B * dilation, M, H, Dh)
        t = jnp.pad(t, ((0, 0), (0, Mp - M), (0, 0), (0, 0)))
        return t.reshape(B * dilation, nb, BLOCK, H, Dh)

    def with_prev(t):
        prev = jnp.pad(t, ((0, 0), (1, 0), (0, 0), (0, 0), (0, 0)))[:, :-1]
        return jnp.concatenate([prev, t], axis=2)

    qs = to_sub(q).astype(jnp.float32)
    kb = with_prev(to_sub(k)).astype(jnp.float32)
    vb = with_prev(to_sub(v)).astype(jnp.float32)

    qi = jnp.arange(BLOCK)[:, None]
    kj = jnp.arange(2 * BLOCK)[None, :]
    steps = qi + BLOCK - kj
    band = (steps >= 0) & (steps <= n_steps)
    first_ok = (jnp.arange(nb)[:, None] > 0) | (kj >= BLOCK)
    mask = band[None] & first_ok[:, None, :]
    bucket = t5_bucket(jnp.maximum(steps, 0) * dilation)
    bias = bias_table_g.astype(jnp.float32)[bucket].transpose(2, 0, 1)

    logits = jnp.einsum('bnqhd,bnkhd->bhnqk', qs, kb) + bias[None, :, None]
    logits = jnp.where(mask[None, None], logits, NEG_INF)
    m = jnp.max(logits, axis=-1, keepdims=True)
    p = jnp.exp(logits - m)
    denom = jnp.sum(p, axis=-1)
    o = jnp.einsum('bhnqk,bnkhd->bnqhd', p, vb) / denom.transpose(0, 2, 3, 1)[..., None]
    lse = m[..., 0] + jnp.log(denom)

    o = o.reshape(B * dilation, Mp, H, Dh)[:, :M]
    o = o.reshape(B, dilation, M, H, Dh).transpose(0, 2, 1, 3, 4).reshape(B, L, H, Dh)
    lse = lse.transpose(0, 2, 3, 1).reshape(B * dilation, Mp, H)[:, :M]
    lse = lse.reshape(B, dilation, M, H).transpose(0, 2, 1, 3).reshape(B, L, H)
    return o, lse


def s5_mixer(u, a_re, a_im, log_dt, b_re, b_im, c_re, c_im, d_skip):
    B, L, _ = u.shape
    uf = u.astype(jnp.float32).reshape(B, L, SSM_GROUPS, SSM_GROUP)
    lam_re = a_re.astype(jnp.float32)
    lam_im = a_im.astype(jnp.float32)
    dt = jnp.exp(log_dt.astype(jnp.float32))[:, None]
    mag = jnp.exp(lam_re * dt)
    ab_re = mag * jnp.cos(lam_im * dt)
    ab_im = mag * jnp.sin(lam_im * dt)
    den = lam_re * lam_re + lam_im * lam_im
    xr = ab_re - 1.0
    coef_re = (xr * lam_re + ab_im * lam_im) / den
    coef_im = (ab_im * lam_re - xr * lam_im) / den
    br = b_re.astype(jnp.float32)
    bi = b_im.astype(jnp.float32)
    bb_re = coef_re[..., None] * br - coef_im[..., None] * bi
    bb_im = coef_re[..., None] * bi + coef_im[..., None] * br
    bu_re = jnp.einsum('gnc,blgc->lbgn', bb_re, uf)
    bu_im = jnp.einsum('gnc,blgc->lbgn', bb_im, uf)
    a_seq_re = jnp.broadcast_to(ab_re[None, None], (L, 1, SSM_GROUPS, SSM_STATE))
    a_seq_im = jnp.broadcast_to(ab_im[None, None], (L, 1, SSM_GROUPS, SSM_STATE))

    def combine(left, right):
        al_re, al_im, bl_re, bl_im = left
        ar_re, ar_im, brr, bri = right
        return (al_re * ar_re - al_im * ar_im,
                al_re * ar_im + al_im * ar_re,
                ar_re * bl_re - ar_im * bl_im + brr,
                ar_re * bl_im + ar_im * bl_re + bri)

    _, _, s_re, s_im = lax.associative_scan(combine, (a_seq_re, a_seq_im, bu_re, bu_im), axis=0)
    y = (jnp.einsum('gcn,lbgn->blgc', c_re.astype(jnp.float32), s_re)
         - jnp.einsum('gcn,lbgn->blgc', c_im.astype(jnp.float32), s_im)
         + d_skip.astype(jnp.float32).reshape(SSM_GROUPS, SSM_GROUP) * uf)
    return y.reshape(B, L, SSM_WIDTH).astype(u.dtype)


def _fwd_setup_inputs(seed: int = 0) -> dict:
    key = jax.random.key(seed)
    ks = iter(jax.random.split(key, 32))
    f32 = jnp.float32

    def nrm(shape, scale):
        return jax.random.normal(next(ks), shape, f32) * scale

    def gain(shape):
        return 1.0 + 0.05 * jax.random.normal(next(ks), shape, f32)

    L_ = DEPTH
    n_idx = jnp.arange(SSM_STATE, dtype=f32)
    return {
        "x": jax.random.normal(next(ks), (BATCH, SEQ, D_MODEL), f32),
        "ffn1_norm": gain((L_, D_MODEL)),
        "ffn1_w_gate": nrm((L_, D_MODEL, D_FF), D_MODEL ** -0.5),
        "ffn1_w_up": nrm((L_, D_MODEL, D_FF), D_MODEL ** -0.5),
        "ffn1_w_down": nrm((L_, D_FF, D_MODEL), D_FF ** -0.5),
        "mix_norm": gain((L_, D_MODEL)),
        "w_in": nrm((L_, D_MODEL, IN_WIDTH), D_MODEL ** -0.5),
        "gate_bias": nrm((L_, 2 * D_MODEL), 0.1),
        "rel_bias_table": nrm((N_BUCKETS, N_ATTN_HEADS), 0.5),
        "ssm_a_re": -0.5 + nrm((L_, SSM_GROUPS, SSM_STATE), 0.01),
        "ssm_a_im": math.pi * n_idx + nrm((L_, SSM_GROUPS, SSM_STATE), 0.01),
        "ssm_log_dt": jax.random.uniform(next(ks), (L_, SSM_GROUPS), f32,
                                         math.log(DT_MIN), math.log(DT_MAX)),
        "ssm_b_re": nrm((L_, SSM_GROUPS, SSM_STATE, SSM_GROUP), (2 * SSM_GROUP) ** -0.5),
        "ssm_b_im": nrm((L_, SSM_GROUPS, SSM_STATE, SSM_GROUP), (2 * SSM_GROUP) ** -0.5),
        "ssm_c_re": nrm((L_, SSM_GROUPS, SSM_GROUP, SSM_STATE), (2 * SSM_STATE) ** -0.5),
        "ssm_c_im": nrm((L_, SSM_GROUPS, SSM_GROUP, SSM_STATE), (2 * SSM_STATE) ** -0.5),
        "ssm_d": nrm((L_, SSM_WIDTH), 1.0),
        "ssm_w_glu": nrm((L_, SSM_WIDTH, 2 * SSM_WIDTH), SSM_WIDTH ** -0.5),
        "w_attn_branch": nrm((L_, ATTN_OUT_WIDTH, D_MODEL), ATTN_OUT_WIDTH ** -0.5),
        "w_ssm_branch": nrm((L_, SSM_WIDTH, D_MODEL), SSM_WIDTH ** -0.5),
        "w_out": nrm((L_, D_MODEL, D_MODEL), D_MODEL ** -0.5),
        "ffn2_norm": gain((L_, D_MODEL)),
        "ffn2_w_gate": nrm((L_, D_MODEL, D_FF), D_MODEL ** -0.5),
        "ffn2_w_up": nrm((L_, D_MODEL, D_FF), D_MODEL ** -0.5),
        "ffn2_w_down": nrm((L_, D_FF, D_MODEL), D_FF ** -0.5),
        "final_norm": gain((D_MODEL,)),
    }


def _fwd_reference(x, ffn1_norm, ffn1_w_gate, ffn1_w_up, ffn1_w_down, mix_norm, w_in, gate_bias,
              rel_bias_table, ssm_a_re, ssm_a_im, ssm_log_dt, ssm_b_re, ssm_b_im, ssm_c_re,
              ssm_c_im, ssm_d, ssm_w_glu, w_attn_branch, w_ssm_branch, w_out, ffn2_norm,
              ffn2_w_gate, ffn2_w_up, ffn2_w_down, final_norm):
    B, L, _ = x.shape
    scale = HEAD_DIM ** -0.5
    for l in range(DEPTH):
        x = x + 0.5 * swiglu(rms_norm(x, ffn1_norm[l]), ffn1_w_gate[l], ffn1_w_up[l], ffn1_w_down[l])

        h = rms_norm(x, mix_norm[l])
        z = h @ w_in[l]
        c0 = ATTN_WIDTH
        q = z[..., :c0].reshape(B, L, N_ATTN_HEADS, HEAD_DIM) * scale
        k = z[..., c0:2 * c0].reshape(B, L, N_ATTN_HEADS, HEAD_DIM)
        v = z[..., 2 * c0:3 * c0].reshape(B, L, N_ATTN_HEADS, HEAD_DIM)
        c1 = 3 * c0
        u = z[..., c1:c1 + SSM_WIDTH]
        c2 = c1 + SSM_WIDTH
        g_attn = jax.nn.sigmoid(z[..., c2:c2 + D_MODEL] + gate_bias[l, :D_MODEL])
        g_ssm = jax.nn.sigmoid(z[..., c2 + D_MODEL:] + gate_bias[l, D_MODEL:])

        outs, lses = [], []
        for g, (window, dilation) in enumerate(DILATED_GROUPS):
            hs = slice(g * HEADS_PER_GROUP, (g + 1) * HEADS_PER_GROUP)
            o_g, lse_g = dilated_group_attention(q[:, :, hs], k[:, :, hs], v[:, :, hs],
                                                 rel_bias_table[:, hs], window, dilation)
            outs.append(o_g)
            lses.append(lse_g)
        o_stack = jnp.stack(outs, axis=2)
        w_grp = jax.nn.softmax(jnp.stack(lses, axis=2), axis=2)
        o_attn = jnp.sum(w_grp[..., None] * o_stack, axis=2).reshape(B, L, ATTN_OUT_WIDTH)
        y_attn = o_attn.astype(x.dtype) @ w_attn_branch[l]

        y_s = jax.nn.gelu(s5_mixer(u, ssm_a_re[l], ssm_a_im[l], ssm_log_dt[l], ssm_b_re[l],
                                   ssm_b_im[l], ssm_c_re[l], ssm_c_im[l], ssm_d[l]))
        glu = y_s @ ssm_w_glu[l]
        y_s = glu[..., :SSM_WIDTH] * jax.nn.sigmoid(glu[..., SSM_WIDTH:])
        y_ssm = y_s @ w_ssm_branch[l]

        x = x + (g_attn * y_attn + g_ssm * y_ssm) @ w_out[l]

        x = x + 0.5 * swiglu(rms_norm(x, ffn2_norm[l]), ffn2_w_gate[l], ffn2_w_up[l], ffn2_w_down[l])
    return rms_norm(x, final_norm)


import jax as _jax
import jax.numpy as _jnp

TWIN_FORMAT = 'train_step'
FWD_PARAMS = ['x', 'ffn1_norm', 'ffn1_w_gate', 'ffn1_w_up', 'ffn1_w_down', 'mix_norm', 'w_in', 'gate_bias', 'rel_bias_table', 'ssm_a_re', 'ssm_a_im', 'ssm_log_dt', 'ssm_b_re', 'ssm_b_im', 'ssm_c_re', 'ssm_c_im', 'ssm_d', 'ssm_w_glu', 'w_attn_branch', 'w_ssm_branch', 'w_out', 'ffn2_norm', 'ffn2_w_gate', 'ffn2_w_up', 'ffn2_w_down', 'final_norm']
TWIN_WEIGHTS = ['ffn1_norm', 'ffn1_w_gate', 'ffn1_w_up', 'ffn1_w_down', 'mix_norm', 'w_in', 'gate_bias', 'rel_bias_table', 'ssm_a_re', 'ssm_a_im', 'ssm_log_dt', 'ssm_b_re', 'ssm_b_im', 'ssm_c_re', 'ssm_c_im', 'ssm_d', 'ssm_w_glu', 'w_attn_branch', 'w_ssm_branch', 'w_out', 'ffn2_norm', 'ffn2_w_gate', 'ffn2_w_up', 'ffn2_w_down', 'final_norm']
TWIN_DIFF_INPUT = 'x'
TWIN_INPUTS = ['x', 'ffn1_norm', 'ffn1_w_gate', 'ffn1_w_up', 'ffn1_w_down', 'mix_norm', 'w_in', 'gate_bias', 'rel_bias_table', 'ssm_a_re', 'ssm_a_im', 'ssm_log_dt', 'ssm_b_re', 'ssm_b_im', 'ssm_c_re', 'ssm_c_im', 'ssm_d', 'ssm_w_glu', 'w_attn_branch', 'w_ssm_branch', 'w_out', 'ffn2_norm', 'ffn2_w_gate', 'ffn2_w_up', 'ffn2_w_down', 'final_norm', 'loss_target', 'm_ffn1_norm', 'm_ffn1_w_gate', 'm_ffn1_w_up', 'm_ffn1_w_down', 'm_mix_norm', 'm_w_in', 'm_gate_bias', 'm_rel_bias_table', 'm_ssm_a_re', 'm_ssm_a_im', 'm_ssm_log_dt', 'm_ssm_b_re', 'm_ssm_b_im', 'm_ssm_c_re', 'm_ssm_c_im', 'm_ssm_d', 'm_ssm_w_glu', 'm_w_attn_branch', 'm_w_ssm_branch', 'm_w_out', 'm_ffn2_norm', 'm_ffn2_w_gate', 'm_ffn2_w_up', 'm_ffn2_w_down', 'm_final_norm', 'v_ffn1_norm', 'v_ffn1_w_gate', 'v_ffn1_w_up', 'v_ffn1_w_down', 'v_mix_norm', 'v_w_in', 'v_gate_bias', 'v_rel_bias_table', 'v_ssm_a_re', 'v_ssm_a_im', 'v_ssm_log_dt', 'v_ssm_b_re', 'v_ssm_b_im', 'v_ssm_c_re', 'v_ssm_c_im', 'v_ssm_d', 'v_ssm_w_glu', 'v_w_attn_branch', 'v_w_ssm_branch', 'v_w_out', 'v_ffn2_norm', 'v_ffn2_w_gate', 'v_ffn2_w_up', 'v_ffn2_w_down', 'v_final_norm']
TWIN_OUTPUTS = ['loss', 'grad_x', 'grad_ffn1_norm', 'grad_ffn1_w_gate', 'grad_ffn1_w_up', 'grad_ffn1_w_down', 'grad_mix_norm', 'grad_w_in', 'grad_gate_bias', 'grad_rel_bias_table', 'grad_ssm_a_re', 'grad_ssm_a_im', 'grad_ssm_log_dt', 'grad_ssm_b_re', 'grad_ssm_b_im', 'grad_ssm_c_re', 'grad_ssm_c_im', 'grad_ssm_d', 'grad_ssm_w_glu', 'grad_w_attn_branch', 'grad_w_ssm_branch', 'grad_w_out', 'grad_ffn2_norm', 'grad_ffn2_w_gate', 'grad_ffn2_w_up', 'grad_ffn2_w_down', 'grad_final_norm', 'delta_ffn1_norm', 'delta_ffn1_w_gate', 'delta_ffn1_w_up', 'delta_ffn1_w_down', 'delta_mix_norm', 'delta_w_in', 'delta_gate_bias', 'delta_rel_bias_table', 'delta_ssm_a_re', 'delta_ssm_a_im', 'delta_ssm_log_dt', 'delta_ssm_b_re', 'delta_ssm_b_im', 'delta_ssm_c_re', 'delta_ssm_c_im', 'delta_ssm_d', 'delta_ssm_w_glu', 'delta_w_attn_branch', 'delta_w_ssm_branch', 'delta_w_out', 'delta_ffn2_norm', 'delta_ffn2_w_gate', 'delta_ffn2_w_up', 'delta_ffn2_w_down', 'delta_final_norm', 'new_m_ffn1_norm', 'new_m_ffn1_w_gate', 'new_m_ffn1_w_up', 'new_m_ffn1_w_down', 'new_m_mix_norm', 'new_m_w_in', 'new_m_gate_bias', 'new_m_rel_bias_table', 'new_m_ssm_a_re', 'new_m_ssm_a_im', 'new_m_ssm_log_dt', 'new_m_ssm_b_re', 'new_m_ssm_b_im', 'new_m_ssm_c_re', 'new_m_ssm_c_im', 'new_m_ssm_d', 'new_m_ssm_w_glu', 'new_m_w_attn_branch', 'new_m_w_ssm_branch', 'new_m_w_out', 'new_m_ffn2_norm', 'new_m_ffn2_w_gate', 'new_m_ffn2_w_up', 'new_m_ffn2_w_down', 'new_m_final_norm', 'new_v_ffn1_norm', 'new_v_ffn1_w_gate', 'new_v_ffn1_w_up', 'new_v_ffn1_w_down', 'new_v_mix_norm', 'new_v_w_in', 'new_v_gate_bias', 'new_v_rel_bias_table', 'new_v_ssm_a_re', 'new_v_ssm_a_im', 'new_v_ssm_log_dt', 'new_v_ssm_b_re', 'new_v_ssm_b_im', 'new_v_ssm_c_re', 'new_v_ssm_c_im', 'new_v_ssm_d', 'new_v_ssm_w_glu', 'new_v_w_attn_branch', 'new_v_w_ssm_branch', 'new_v_w_out', 'new_v_ffn2_norm', 'new_v_ffn2_w_gate', 'new_v_ffn2_w_up', 'new_v_ffn2_w_down', 'new_v_final_norm']
TWIN_LEAF_KINDS = {'loss': 'loss', 'grad_x': 'grad_x', 'grad_ffn1_norm': 'grad_w', 'grad_ffn1_w_gate': 'grad_w', 'grad_ffn1_w_up': 'grad_w', 'grad_ffn1_w_down': 'grad_w', 'grad_mix_norm': 'grad_w', 'grad_w_in': 'grad_w', 'grad_gate_bias': 'grad_w', 'grad_rel_bias_table': 'grad_w', 'grad_ssm_a_re': 'grad_w', 'grad_ssm_a_im': 'grad_w', 'grad_ssm_log_dt': 'grad_w', 'grad_ssm_b_re': 'grad_w', 'grad_ssm_b_im': 'grad_w', 'grad_ssm_c_re': 'grad_w', 'grad_ssm_c_im': 'grad_w', 'grad_ssm_d': 'grad_w', 'grad_ssm_w_glu': 'grad_w', 'grad_w_attn_branch': 'grad_w', 'grad_w_ssm_branch': 'grad_w', 'grad_w_out': 'grad_w', 'grad_ffn2_norm': 'grad_w', 'grad_ffn2_w_gate': 'grad_w', 'grad_ffn2_w_up': 'grad_w', 'grad_ffn2_w_down': 'grad_w', 'grad_final_norm': 'grad_w', 'delta_ffn1_norm': 'delta_w', 'delta_ffn1_w_gate': 'delta_w', 'delta_ffn1_w_up': 'delta_w', 'delta_ffn1_w_down': 'delta_w', 'delta_mix_norm': 'delta_w', 'delta_w_in': 'delta_w', 'delta_gate_bias': 'delta_w', 'delta_rel_bias_table': 'delta_w', 'delta_ssm_a_re': 'delta_w', 'delta_ssm_a_im': 'delta_w', 'delta_ssm_log_dt': 'delta_w', 'delta_ssm_b_re': 'delta_w', 'delta_ssm_b_im': 'delta_w', 'delta_ssm_c_re': 'delta_w', 'delta_ssm_c_im': 'delta_w', 'delta_ssm_d': 'delta_w', 'delta_ssm_w_glu': 'delta_w', 'delta_w_attn_branch': 'delta_w', 'delta_w_ssm_branch': 'delta_w', 'delta_w_out': 'delta_w', 'delta_ffn2_norm': 'delta_w', 'delta_ffn2_w_gate': 'delta_w', 'delta_ffn2_w_up': 'delta_w', 'delta_ffn2_w_down': 'delta_w', 'delta_final_norm': 'delta_w', 'new_m_ffn1_norm': 'new_m', 'new_m_ffn1_w_gate': 'new_m', 'new_m_ffn1_w_up': 'new_m', 'new_m_ffn1_w_down': 'new_m', 'new_m_mix_norm': 'new_m', 'new_m_w_in': 'new_m', 'new_m_gate_bias': 'new_m', 'new_m_rel_bias_table': 'new_m', 'new_m_ssm_a_re': 'new_m', 'new_m_ssm_a_im': 'new_m', 'new_m_ssm_log_dt': 'new_m', 'new_m_ssm_b_re': 'new_m', 'new_m_ssm_b_im': 'new_m', 'new_m_ssm_c_re': 'new_m', 'new_m_ssm_c_im': 'new_m', 'new_m_ssm_d': 'new_m', 'new_m_ssm_w_glu': 'new_m', 'new_m_w_attn_branch': 'new_m', 'new_m_w_ssm_branch': 'new_m', 'new_m_w_out': 'new_m', 'new_m_ffn2_norm': 'new_m', 'new_m_ffn2_w_gate': 'new_m', 'new_m_ffn2_w_up': 'new_m', 'new_m_ffn2_w_down': 'new_m', 'new_m_final_norm': 'new_m', 'new_v_ffn1_norm': 'new_v', 'new_v_ffn1_w_gate': 'new_v', 'new_v_ffn1_w_up': 'new_v', 'new_v_ffn1_w_down': 'new_v', 'new_v_mix_norm': 'new_v', 'new_v_w_in': 'new_v', 'new_v_gate_bias': 'new_v', 'new_v_rel_bias_table': 'new_v', 'new_v_ssm_a_re': 'new_v', 'new_v_ssm_a_im': 'new_v', 'new_v_ssm_log_dt': 'new_v', 'new_v_ssm_b_re': 'new_v', 'new_v_ssm_b_im': 'new_v', 'new_v_ssm_c_re': 'new_v', 'new_v_ssm_c_im': 'new_v', 'new_v_ssm_d': 'new_v', 'new_v_ssm_w_glu': 'new_v', 'new_v_w_attn_branch': 'new_v', 'new_v_w_ssm_branch': 'new_v', 'new_v_w_out': 'new_v', 'new_v_ffn2_norm': 'new_v', 'new_v_ffn2_w_gate': 'new_v', 'new_v_ffn2_w_up': 'new_v', 'new_v_ffn2_w_down': 'new_v', 'new_v_final_norm': 'new_v'}


def _forward(args):
    return _fwd_reference(*[args[k] for k in FWD_PARAMS])


def _output_shape():
    def fwd():
        inp = _fwd_setup_inputs(0)
        return _fwd_reference(*[inp[k] for k in FWD_PARAMS])
    out = _jax.eval_shape(fwd)
    return out.shape, out.dtype

N_MICROBATCH = 1
ADAM_LR = 0.001
ADAM_B1 = 0.9
ADAM_B2 = 0.999
ADAM_EPS = 1e-08
ADAM_WD = 0.01
ADAM_STEP = 10
PER_EXAMPLE_BATCH_AXIS = {'x': 0, 'loss_target': 0}
SHARED_INPUTS = []
_WEIGHT_DTYPES = {'ffn1_norm': _jnp.float32, 'ffn1_w_gate': _jnp.float32, 'ffn1_w_up': _jnp.float32, 'ffn1_w_down': _jnp.float32, 'mix_norm': _jnp.float32, 'w_in': _jnp.float32, 'gate_bias': _jnp.float32, 'rel_bias_table': _jnp.float32, 'ssm_a_re': _jnp.float32, 'ssm_a_im': _jnp.float32, 'ssm_log_dt': _jnp.float32, 'ssm_b_re': _jnp.float32, 'ssm_b_im': _jnp.float32, 'ssm_c_re': _jnp.float32, 'ssm_c_im': _jnp.float32, 'ssm_d': _jnp.float32, 'ssm_w_glu': _jnp.float32, 'w_attn_branch': _jnp.float32, 'w_ssm_branch': _jnp.float32, 'w_out': _jnp.float32, 'ffn2_norm': _jnp.float32, 'ffn2_w_gate': _jnp.float32, 'ffn2_w_up': _jnp.float32, 'ffn2_w_down': _jnp.float32, 'final_norm': _jnp.float32}
MOMENT_SCALE = {'ffn1_norm': 1.172991e-01, 'ffn1_w_gate': 4.726329e-02, 'ffn1_w_up': 4.575034e-02, 'ffn1_w_down': 7.582980e-02, 'mix_norm': 6.784272e-02, 'w_in': 3.107027e-02, 'gate_bias': 1.457491e-02, 'rel_bias_table': 3.627878e-02, 'ssm_a_re': 3.665050e-03, 'ssm_a_im': 3.512392e-03, 'ssm_log_dt': 4.415779e+00, 'ssm_b_re': 2.315329e-03, 'ssm_b_im': 2.310863e-03, 'ssm_c_re': 4.902672e-03, 'ssm_c_im': 4.733011e-03, 'ssm_d': 7.876661e-02, 'ssm_w_glu': 5.257630e-02, 'w_attn_branch': 2.713962e-02, 'w_ssm_branch': 4.950365e-02, 'w_out': 5.251704e-02, 'ffn2_norm': 1.016382e-01, 'ffn2_w_gate': 4.287435e-02, 'ffn2_w_up': 4.189142e-02, 'ffn2_w_down': 6.971417e-02, 'final_norm': 6.403343e+01}


def _to_microbatches(a, axis):
    t = _jnp.moveaxis(a, axis, 0)
    t = t.reshape((N_MICROBATCH, t.shape[0] // N_MICROBATCH) + t.shape[1:])
    return _jnp.moveaxis(t, 1, axis + 1)


def setup_inputs(seed: int = 0) -> dict:
    inp = _fwd_setup_inputs(seed)
    key = _jax.random.fold_in(_jax.random.key(seed), 7919)
    shape, _ = _output_shape()
    out = dict(inp)
    out["loss_target"] = _jax.random.normal(_jax.random.fold_in(key, 0), shape, _jnp.float32)
    for i, name in enumerate(TWIN_WEIGHTS):
        w = inp[name].astype(_jnp.float32)
        if MOMENT_SCALE is None:
            s = _jnp.sqrt(_jnp.mean(_jnp.square(w)) + 1e-30)
        else:
            s = MOMENT_SCALE[name]
        km, kv = _jax.random.split(_jax.random.fold_in(key, i + 1))
        out[name] = w
        out["m_" + name] = s * _jax.random.normal(km, w.shape, _jnp.float32)
        out["v_" + name] = (s * s) * _jax.random.uniform(kv, w.shape, _jnp.float32, 0.5, 1.5)
    if N_MICROBATCH > 1:
        for name, axis in PER_EXAMPLE_BATCH_AXIS.items():
            out[name] = _to_microbatches(out[name], axis)
    return {'x': out['x'], 'ffn1_norm': out['ffn1_norm'], 'ffn1_w_gate': out['ffn1_w_gate'], 'ffn1_w_up': out['ffn1_w_up'], 'ffn1_w_down': out['ffn1_w_down'], 'mix_norm': out['mix_norm'], 'w_in': out['w_in'], 'gate_bias': out['gate_bias'], 'rel_bias_table': out['rel_bias_table'], 'ssm_a_re': out['ssm_a_re'], 'ssm_a_im': out['ssm_a_im'], 'ssm_log_dt': out['ssm_log_dt'], 'ssm_b_re': out['ssm_b_re'], 'ssm_b_im': out['ssm_b_im'], 'ssm_c_re': out['ssm_c_re'], 'ssm_c_im': out['ssm_c_im'], 'ssm_d': out['ssm_d'], 'ssm_w_glu': out['ssm_w_glu'], 'w_attn_branch': out['w_attn_branch'], 'w_ssm_branch': out['w_ssm_branch'], 'w_out': out['w_out'], 'ffn2_norm': out['ffn2_norm'], 'ffn2_w_gate': out['ffn2_w_gate'], 'ffn2_w_up': out['ffn2_w_up'], 'ffn2_w_down': out['ffn2_w_down'], 'final_norm': out['final_norm'], 'loss_target': out['loss_target'], 'm_ffn1_norm': out['m_ffn1_norm'], 'm_ffn1_w_gate': out['m_ffn1_w_gate'], 'm_ffn1_w_up': out['m_ffn1_w_up'], 'm_ffn1_w_down': out['m_ffn1_w_down'], 'm_mix_norm': out['m_mix_norm'], 'm_w_in': out['m_w_in'], 'm_gate_bias': out['m_gate_bias'], 'm_rel_bias_table': out['m_rel_bias_table'], 'm_ssm_a_re': out['m_ssm_a_re'], 'm_ssm_a_im': out['m_ssm_a_im'], 'm_ssm_log_dt': out['m_ssm_log_dt'], 'm_ssm_b_re': out['m_ssm_b_re'], 'm_ssm_b_im': out['m_ssm_b_im'], 'm_ssm_c_re': out['m_ssm_c_re'], 'm_ssm_c_im': out['m_ssm_c_im'], 'm_ssm_d': out['m_ssm_d'], 'm_ssm_w_glu': out['m_ssm_w_glu'], 'm_w_attn_branch': out['m_w_attn_branch'], 'm_w_ssm_branch': out['m_w_ssm_branch'], 'm_w_out': out['m_w_out'], 'm_ffn2_norm': out['m_ffn2_norm'], 'm_ffn2_w_gate': out['m_ffn2_w_gate'], 'm_ffn2_w_up': out['m_ffn2_w_up'], 'm_ffn2_w_down': out['m_ffn2_w_down'], 'm_final_norm': out['m_final_norm'], 'v_ffn1_norm': out['v_ffn1_norm'], 'v_ffn1_w_gate': out['v_ffn1_w_gate'], 'v_ffn1_w_up': out['v_ffn1_w_up'], 'v_ffn1_w_down': out['v_ffn1_w_down'], 'v_mix_norm': out['v_mix_norm'], 'v_w_in': out['v_w_in'], 'v_gate_bias': out['v_gate_bias'], 'v_rel_bias_table': out['v_rel_bias_table'], 'v_ssm_a_re': out['v_ssm_a_re'], 'v_ssm_a_im': out['v_ssm_a_im'], 'v_ssm_log_dt': out['v_ssm_log_dt'], 'v_ssm_b_re': out['v_ssm_b_re'], 'v_ssm_b_im': out['v_ssm_b_im'], 'v_ssm_c_re': out['v_ssm_c_re'], 'v_ssm_c_im': out['v_ssm_c_im'], 'v_ssm_d': out['v_ssm_d'], 'v_ssm_w_glu': out['v_ssm_w_glu'], 'v_w_attn_branch': out['v_w_attn_branch'], 'v_w_ssm_branch': out['v_w_ssm_branch'], 'v_w_out': out['v_w_out'], 'v_ffn2_norm': out['v_ffn2_norm'], 'v_ffn2_w_gate': out['v_ffn2_w_gate'], 'v_ffn2_w_up': out['v_ffn2_w_up'], 'v_ffn2_w_down': out['v_ffn2_w_down'], 'v_final_norm': out['v_final_norm']}


def _loss(weights, diff, rest, loss_target):
    with _jax.named_scope("forward"):
        args = {**rest, TWIN_DIFF_INPUT: diff, **{k: w.astype(_WEIGHT_DTYPES[k]) for k, w in weights.items()}}
        y = _forward(args)
    with _jax.named_scope("loss_head"):
        err = _jnp.square(y.astype(_jnp.float32) - loss_target)
        return 0.5 * _jnp.sum(_jnp.mean(err, axis=-1)) if err.ndim else 0.5 * err


def _adamw(w, g, m, v):
    m = ADAM_B1 * m + (1.0 - ADAM_B1) * g
    v = ADAM_B2 * v + (1.0 - ADAM_B2) * _jnp.square(g)
    m_hat = m / (1.0 - ADAM_B1 ** ADAM_STEP)
    v_hat = v / (1.0 - ADAM_B2 ** ADAM_STEP)
    delta = -ADAM_LR * (m_hat / (_jnp.sqrt(v_hat) + ADAM_EPS) + ADAM_WD * w)
    return delta, m, v


def reference(x, ffn1_norm, ffn1_w_gate, ffn1_w_up, ffn1_w_down, mix_norm, w_in, gate_bias, rel_bias_table, ssm_a_re, ssm_a_im, ssm_log_dt, ssm_b_re, ssm_b_im, ssm_c_re, ssm_c_im, ssm_d, ssm_w_glu, w_attn_branch, w_ssm_branch, w_out, ffn2_norm, ffn2_w_gate, ffn2_w_up, ffn2_w_down, final_norm, loss_target, m_ffn1_norm, m_ffn1_w_gate, m_ffn1_w_up, m_ffn1_w_down, m_mix_norm, m_w_in, m_gate_bias, m_rel_bias_table, m_ssm_a_re, m_ssm_a_im, m_ssm_log_dt, m_ssm_b_re, m_ssm_b_im, m_ssm_c_re, m_ssm_c_im, m_ssm_d, m_ssm_w_glu, m_w_attn_branch, m_w_ssm_branch, m_w_out, m_ffn2_norm, m_ffn2_w_gate, m_ffn2_w_up, m_ffn2_w_down, m_final_norm, v_ffn1_norm, v_ffn1_w_gate, v_ffn1_w_up, v_ffn1_w_down, v_mix_norm, v_w_in, v_gate_bias, v_rel_bias_table, v_ssm_a_re, v_ssm_a_im, v_ssm_log_dt, v_ssm_b_re, v_ssm_b_im, v_ssm_c_re, v_ssm_c_im, v_ssm_d, v_ssm_w_glu, v_w_attn_branch, v_w_ssm_branch, v_w_out, v_ffn2_norm, v_ffn2_w_gate, v_ffn2_w_up, v_ffn2_w_down, v_final_norm):
    given = dict(x=x, ffn1_norm=ffn1_norm, ffn1_w_gate=ffn1_w_gate, ffn1_w_up=ffn1_w_up, ffn1_w_down=ffn1_w_down, mix_norm=mix_norm, w_in=w_in, gate_bias=gate_bias, rel_bias_table=rel_bias_table, ssm_a_re=ssm_a_re, ssm_a_im=ssm_a_im, ssm_log_dt=ssm_log_dt, ssm_b_re=ssm_b_re, ssm_b_im=ssm_b_im, ssm_c_re=ssm_c_re, ssm_c_im=ssm_c_im, ssm_d=ssm_d, ssm_w_glu=ssm_w_glu, w_attn_branch=w_attn_branch, w_ssm_branch=w_ssm_branch, w_out=w_out, ffn2_norm=ffn2_norm, ffn2_w_gate=ffn2_w_gate, ffn2_w_up=ffn2_w_up, ffn2_w_down=ffn2_w_down, final_norm=final_norm, loss_target=loss_target, m_ffn1_norm=m_ffn1_norm, m_ffn1_w_gate=m_ffn1_w_gate, m_ffn1_w_up=m_ffn1_w_up, m_ffn1_w_down=m_ffn1_w_down, m_mix_norm=m_mix_norm, m_w_in=m_w_in, m_gate_bias=m_gate_bias, m_rel_bias_table=m_rel_bias_table, m_ssm_a_re=m_ssm_a_re, m_ssm_a_im=m_ssm_a_im, m_ssm_log_dt=m_ssm_log_dt, m_ssm_b_re=m_ssm_b_re, m_ssm_b_im=m_ssm_b_im, m_ssm_c_re=m_ssm_c_re, m_ssm_c_im=m_ssm_c_im, m_ssm_d=m_ssm_d, m_ssm_w_glu=m_ssm_w_glu, m_w_attn_branch=m_w_attn_branch, m_w_ssm_branch=m_w_ssm_branch, m_w_out=m_w_out, m_ffn2_norm=m_ffn2_norm, m_ffn2_w_gate=m_ffn2_w_gate, m_ffn2_w_up=m_ffn2_w_up, m_ffn2_w_down=m_ffn2_w_down, m_final_norm=m_final_norm, v_ffn1_norm=v_ffn1_norm, v_ffn1_w_gate=v_ffn1_w_gate, v_ffn1_w_up=v_ffn1_w_up, v_ffn1_w_down=v_ffn1_w_down, v_mix_norm=v_mix_norm, v_w_in=v_w_in, v_gate_bias=v_gate_bias, v_rel_bias_table=v_rel_bias_table, v_ssm_a_re=v_ssm_a_re, v_ssm_a_im=v_ssm_a_im, v_ssm_log_dt=v_ssm_log_dt, v_ssm_b_re=v_ssm_b_re, v_ssm_b_im=v_ssm_b_im, v_ssm_c_re=v_ssm_c_re, v_ssm_c_im=v_ssm_c_im, v_ssm_d=v_ssm_d, v_ssm_w_glu=v_ssm_w_glu, v_w_attn_branch=v_w_attn_branch, v_w_ssm_branch=v_w_ssm_branch, v_w_out=v_w_out, v_ffn2_norm=v_ffn2_norm, v_ffn2_w_gate=v_ffn2_w_gate, v_ffn2_w_up=v_ffn2_w_up, v_ffn2_w_down=v_ffn2_w_down, v_final_norm=v_final_norm)
    weights = {n: given[n] for n in TWIN_WEIGHTS}
    shared = {n: given[n] for n in SHARED_INPUTS}
    per_example = {n: given[n] for n in ['x']}
    grad_fn = _jax.value_and_grad(_loss, argnums=(0, 1))

    def one_microbatch(ex, loss_target):
        ex = dict(ex)
        diff = ex.pop(TWIN_DIFF_INPUT)
        return grad_fn(weights, diff, {**shared, **ex}, loss_target)

    if N_MICROBATCH == 1:
        loss, (grad_w, grad_x) = one_microbatch(per_example, given["loss_target"])
    else:
        def body(carry, xs):
            loss_sum, grad_sum = carry
            l_k, (gw_k, gx_k) = one_microbatch(xs[0], xs[1])
            with _jax.named_scope("update"):
                return (loss_sum + l_k, _jax.tree.map(_jnp.add, grad_sum, gw_k)), gx_k

        init = (_jnp.zeros((), _jnp.float32), _jax.tree.map(_jnp.zeros_like, weights))
        (loss, grad_w), grad_x = _jax.lax.scan(body, init, (per_example, given["loss_target"]))
    with _jax.named_scope("update"):
        delta_w, new_m, new_v = {}, {}, {}
        for n in TWIN_WEIGHTS:
            delta_w[n], new_m[n], new_v[n] = _adamw(weights[n], grad_w[n], given["m_" + n], given["v_" + n])
    return (loss, grad_x, *[grad_w[n] for n in TWIN_WEIGHTS], *[delta_w[n] for n in TWIN_WEIGHTS],
            *[new_m[n] for n in TWIN_WEIGHTS], *[new_v[n] for n in TWIN_WEIGHTS])
```

```python
import functools
import math

import jax
import jax.numpy as jnp
from jax import lax
from jax.experimental import pallas as pl
from jax.experimental.pallas import tpu as pltpu

F32 = jnp.float32
BF16 = jnp.bfloat16
MESH = pl.DeviceIdType.MESH

D_MODEL = 1024
D_FF = 2816
HEAD_DIM = 64
HEADS_PER_GROUP = 4
DILATIONS = (1, 4, 16)
WINDOW_STEPS = 128
ATTN_BLOCK = 128
GROUP_WIDTH = HEADS_PER_GROUP * HEAD_DIM
ATTN_WIDTH = 3 * GROUP_WIDTH
N_BUCKETS = 32
MAX_DISTANCE = 2048
NEG_INF = -1e30
SSM_WIDTH = 512
SSM_GROUP = 16
SSM_GROUPS = 32
SSM_STATE = 64
NS = SSM_GROUPS * SSM_STATE
EPS = 1e-6
IN_WIDTH = 3 * ATTN_WIDTH + SSM_WIDTH + 2 * D_MODEL
Q_SCALE = HEAD_DIM ** -0.5
N_SHARD = 4
FF_SHARD = D_FF // N_SHARD
ADAM_LR, ADAM_B1, ADAM_B2, ADAM_EPS, ADAM_WD, ADAM_STEP = 0.001, 0.9, 0.999, 1e-08, 0.01, 10

VMEM_LIMIT = 56 * 1024 * 1024
ROW_TILE = 512
FFN_BWD_TILE = 256
SSM_CHUNK = 256
SCAN_LANES = 512
PACK_W = 1024


def _params(**kw):
    return pltpu.CompilerParams(vmem_limit_bytes=VMEM_LIMIT, **kw)


def _dot(a, b):
    return jnp.dot(a, b, preferred_element_type=F32)


def _dot_nt(a, b):
    return lax.dot_general(a, b, (((1,), (1,)), ((), ())), preferred_element_type=F32)


def _dot_tn(a, b):
    return lax.dot_general(a, b, (((0,), (0,)), ((), ())), preferred_element_type=F32)


def _dot_exact(a, b):
    return jnp.dot(a, b, preferred_element_type=F32, precision=lax.Precision.HIGHEST)


def _dot_nt_exact(a, b):
    return lax.dot_general(a, b, (((1,), (1,)), ((), ())), preferred_element_type=F32,
                           precision=lax.Precision.HIGHEST)


def _rms(x):
    r = lax.rsqrt(jnp.mean(x * x, axis=-1, keepdims=True) + EPS)
    return r, x * r


def _rms_bwd(dh, g, r, xhat):
    dxh = dh * g
    return r * (dxh - xhat * jnp.mean(dxh * xhat, axis=-1, keepdims=True))


def _sigmoid(x):
    return 1.0 / (1.0 + jnp.exp(-x))


_GELU_C = math.sqrt(2.0 / math.pi)


def _gelu(x):
    return 0.5 * x * (1.0 + jnp.tanh(_GELU_C * (x + 0.044715 * x * x * x)))


def _gelu_grad(x):
    t = jnp.tanh(_GELU_C * (x + 0.044715 * x * x * x))
    return 0.5 * (1.0 + t) + 0.5 * x * (1.0 - t * t) * _GELU_C * (1.0 + 3 * 0.044715 * x * x)


def _whole():
    return pl.BlockSpec(memory_space=pltpu.VMEM)


def _row_tile(rows, cap):
    if rows <= cap:
        return rows
    return max(t for t in range(8, cap + 1, 8) if rows % t == 0)


def _rows(tm, w):
    return pl.BlockSpec((tm, w), lambda i: (i, 0))


def _acc_row(w):
    return pl.BlockSpec((1, w), lambda i: (0, 0))


def _ffn_fwd(x, g, wg, wu, wd, name):
    L = x.shape[0]
    tm = min(ROW_TILE, L)

    def body(x_ref, g_ref, wg_ref, wu_ref, wd_ref, xo_ref, a_ref, b_ref):
        xv = x_ref[...]
        r, xhat = _rms(xv)
        h = (xhat * g_ref[...]).astype(BF16)
        acc = jnp.zeros((tm, D_MODEL), F32)
        for j in range(N_SHARD):
            a = _dot(h, wg_ref[j])
            b = _dot(h, wu_ref[j])
            a_ref[j] = a.astype(BF16)
            b_ref[j] = b.astype(BF16)
            s = (a * _sigmoid(a) * b).astype(BF16)
            acc = acc + _dot(s, wd_ref[j])
        xo_ref[...] = xv + 0.5 * acc

    act = pl.BlockSpec((N_SHARD, tm, FF_SHARD), lambda i: (0, i, 0))
    return pl.pallas_call(
        body, name=name, grid=(L // tm,),
        in_specs=[_rows(tm, D_MODEL), _whole(), _whole(), _whole(), _whole()],
        out_specs=[_rows(tm, D_MODEL), act, act],
        out_shape=[jax.ShapeDtypeStruct((L, D_MODEL), F32),
                   jax.ShapeDtypeStruct((N_SHARD, L, FF_SHARD), BF16),
                   jax.ShapeDtypeStruct((N_SHARD, L, FF_SHARD), BF16)],
        compiler_params=_params(),
    )(x, g, wg, wu, wd)


def _ffn_bwd(dxo, x, g, a, b, wg, wu, wd, name):
    L = x.shape[0]
    tm = min(FFN_BWD_TILE, L)

    def body(dxo_ref, x_ref, g_ref, a_ref, b_ref, wg_ref, wu_ref, wd_ref,
             dxi_ref, da_ref, db_ref, s_ref, h_ref, do_ref, dg_ref):
        i = pl.program_id(0)
        xv = x_ref[...]
        gv = g_ref[...]
        r, xhat = _rms(xv)
        h_ref[...] = (xhat * gv).astype(BF16)
        dxo_v = dxo_ref[...]
        d_out = (0.5 * dxo_v).astype(BF16)
        do_ref[...] = d_out
        dh = jnp.zeros((tm, D_MODEL), F32)
        for j in range(N_SHARD):
            av = a_ref[j].astype(F32)
            bv = b_ref[j].astype(F32)
            sg = _sigmoid(av)
            sl = av * sg
            ds = _dot_nt(d_out, wd_ref[j])
            dbv = (ds * sl).astype(BF16)
            dav = (ds * bv * (sg * (1.0 + av * (1.0 - sg)))).astype(BF16)
            da_ref[j] = dav
            db_ref[j] = dbv
            s_ref[j] = (sl * bv).astype(BF16)
            dh = dh + _dot_nt(dav, wg_ref[j]) + _dot_nt(dbv, wu_ref[j])

        @pl.when(i == 0)
        def _():
            dg_ref[...] = jnp.zeros_like(dg_ref)

        dg_ref[...] += jnp.sum(dh * xhat, axis=0, keepdims=True)
        dxi_ref[...] = dxo_v + _rms_bwd(dh, gv, r, xhat)

    act = pl.BlockSpec((N_SHARD, tm, FF_SHARD), lambda i: (0, i, 0))
    act_shape = jax.ShapeDtypeStruct((N_SHARD, L, FF_SHARD), BF16)
    return pl.pallas_call(
        body, name=name, grid=(L // tm,),
        in_specs=[_rows(tm, D_MODEL), _rows(tm, D_MODEL), _whole(), act, act, _whole(), _whole(), _whole()],
        out_specs=[_rows(tm, D_MODEL), act, act, act, _rows(tm, D_MODEL), _rows(tm, D_MODEL), _acc_row(D_MODEL)],
        out_shape=[jax.ShapeDtypeStruct((L, D_MODEL), F32), act_shape, act_shape, act_shape,
                   jax.ShapeDtypeStruct((L, D_MODEL), BF16), jax.ShapeDtypeStruct((L, D_MODEL), BF16),
                   jax.ShapeDtypeStruct((1, D_MODEL), F32)],
        compiler_params=_params(),
    )(dxo, x, g, a, b, wg, wu, wd)


def _matmul_tn(a, b, name):
    ja, L, K = a.shape
    jb, _, N = b.shape
    J = max(ja, jb)
    tm = min(ROW_TILE, L)

    def body(a_ref, b_ref, o_ref):
        @pl.when(pl.program_id(1) == 0)
        def _():
            o_ref[...] = jnp.zeros_like(o_ref)

        o_ref[...] += _dot_tn(a_ref[...].astype(BF16), b_ref[...].astype(BF16))

    return pl.pallas_call(
        body, name=name, grid=(J, L // tm),
        in_specs=[pl.BlockSpec((None, tm, K), (lambda j, i: (j, i, 0)) if ja > 1 else (lambda j, i: (0, i, 0))),
                  pl.BlockSpec((None, tm, N), (lambda j, i: (j, i, 0)) if jb > 1 else (lambda j, i: (0, i, 0)))],
        out_specs=pl.BlockSpec((None, K, N), lambda j, i: (j, 0, 0)),
        out_shape=jax.ShapeDtypeStruct((J, K, N), F32),
        compiler_params=_params(),
    )(a, b)


def _loss_fwd_bwd(x, g, target):
    L = x.shape[0]
    tm = min(ROW_TILE, L)

    def body(x_ref, g_ref, t_ref, loss_ref, dx_ref, dg_ref):
        i = pl.program_id(0)
        xv = x_ref[...]
        gv = g_ref[...]
        r, xhat = _rms(xv)
        err = xhat * gv - t_ref[...]
        part = 0.5 * jnp.sum(jnp.sum(err * err, axis=1, keepdims=True) * (1.0 / D_MODEL), axis=0, keepdims=True)
        dy = err * (1.0 / D_MODEL)

        @pl.when(i == 0)
        def _():
            dg_ref[...] = jnp.zeros_like(dg_ref)
            loss_ref[...] = jnp.zeros_like(loss_ref)

        loss_ref[...] += jnp.broadcast_to(part, loss_ref.shape)
        dg_ref[...] += jnp.sum(dy * xhat, axis=0, keepdims=True)
        dx_ref[...] = _rms_bwd(dy, gv, r, xhat)

    return pl.pallas_call(
        body, name="loss_fwd_bwd", grid=(L // tm,),
        in_specs=[_rows(tm, D_MODEL), _whole(), _rows(tm, D_MODEL)],
        out_specs=[pl.BlockSpec((8, 128), lambda i: (0, 0)), _rows(tm, D_MODEL), _acc_row(D_MODEL)],
        out_shape=[jax.ShapeDtypeStruct((8, 128), F32), jax.ShapeDtypeStruct((L, D_MODEL), F32),
                   jax.ShapeDtypeStruct((1, D_MODEL), F32)],
        compiler_params=_params(),
    )(x, g, target)


_C_K = ATTN_WIDTH
_C_V = 2 * ATTN_WIDTH
_C_U = 3 * ATTN_WIDTH
_C_G = _C_U + SSM_WIDTH


def _mix_in_fwd(x, g, w_in, gate_bias):
    L = x.shape[0]
    tm = min(ROW_TILE, L)

    def body(x_ref, g_ref, w_ref, gb_ref, q_ref, k_ref, v_ref, u_ref, gate_ref):
        r, xhat = _rms(x_ref[...])
        h = (xhat * g_ref[...]).astype(BF16)
        q_ref[...] = (_dot(h, w_ref[:, 0:_C_K]) * Q_SCALE).astype(BF16)
        k_ref[...] = _dot(h, w_ref[:, _C_K:_C_V]).astype(BF16)
        v_ref[...] = _dot(h, w_ref[:, _C_V:_C_U]).astype(BF16)
        u_ref[...] = _dot(h, w_ref[:, _C_U:_C_G])
        gate_ref[...] = _sigmoid(_dot(h, w_ref[:, _C_G:IN_WIDTH]) + gb_ref[...])

    return pl.pallas_call(
        body, name="mix_in_fwd", grid=(L // tm,),
        in_specs=[_rows(tm, D_MODEL), _whole(), _whole(), _whole()],
        out_specs=[_rows(tm, ATTN_WIDTH), _rows(tm, ATTN_WIDTH), _rows(tm, ATTN_WIDTH),
                   _rows(tm, SSM_WIDTH), _rows(tm, 2 * D_MODEL)],
        out_shape=[jax.ShapeDtypeStruct((L, ATTN_WIDTH), BF16)] * 3
        + [jax.ShapeDtypeStruct((L, SSM_WIDTH), F32), jax.ShapeDtypeStruct((L, 2 * D_MODEL), F32)],
        compiler_params=_params(),
    )(x, g, w_in, gate_bias)


def _mix_in_bwd(dx2, x, g, dz, w_in):
    L = x.shape[0]
    tm = min(ROW_TILE, L)

    def body(dx2_ref, x_ref, g_ref, dz_ref, w_ref, dx1_ref, h_ref, dg_ref):
        i = pl.program_id(0)
        gv = g_ref[...]
        r, xhat = _rms(x_ref[...])
        h_ref[...] = (xhat * gv).astype(BF16)
        dh = _dot_nt(dz_ref[...], w_ref[...])

        @pl.when(i == 0)
        def _():
            dg_ref[...] = jnp.zeros_like(dg_ref)

        dg_ref[...] += jnp.sum(dh * xhat, axis=0, keepdims=True)
        dx1_ref[...] = dx2_ref[...] + _rms_bwd(dh, gv, r, xhat)

    return pl.pallas_call(
        body, name="mix_in_bwd", grid=(L // tm,),
        in_specs=[_rows(tm, D_MODEL), _rows(tm, D_MODEL), _whole(), _rows(tm, IN_WIDTH), _whole()],
        out_specs=[_rows(tm, D_MODEL), _rows(tm, D_MODEL), _acc_row(D_MODEL)],
        out_shape=[jax.ShapeDtypeStruct((L, D_MODEL), F32), jax.ShapeDtypeStruct((L, D_MODEL), BF16),
                   jax.ShapeDtypeStruct((1, D_MODEL), F32)],
        compiler_params=_params(),
    )(dx2, x, g, dz, w_in)


def _bucket_onehot():
    qi = jnp.arange(ATTN_BLOCK)[:, None]
    kj = jnp.arange(2 * ATTN_BLOCK)[None, :]
    steps = jnp.maximum(qi + ATTN_BLOCK - kj, 0)
    max_exact = N_BUCKETS // 2
    out = []
    for d in DILATIONS:
        dist = steps * d
        df = jnp.maximum(dist, 1).astype(F32)
        large = max_exact + (jnp.log(df / max_exact) / math.log(MAX_DISTANCE / max_exact)
                             * (N_BUCKETS - max_exact)).astype(jnp.int32)
        large = jnp.minimum(large, N_BUCKETS - 1)
        bucket = jnp.where(dist < max_exact, dist, large).reshape(-1)
        out.append((bucket[None, :] == jnp.arange(N_BUCKETS)[:, None]).astype(F32))
    return jnp.stack(out)


def _bias_expand(table_t, onehot):
    n = onehot.shape[-1]

    def body(t_ref, oh_ref, o_ref):
        o_ref[...] = _dot_exact(t_ref[...], oh_ref[...])

    return pl.pallas_call(
        body, name="bias_expand", grid=(3,),
        in_specs=[pl.BlockSpec((None, 8, N_BUCKETS), lambda g: (g, 0, 0)),
                  pl.BlockSpec((None, N_BUCKETS, n), lambda g: (g, 0, 0))],
        out_specs=pl.BlockSpec((None, 8, n), lambda g: (g, 0, 0)),
        out_shape=jax.ShapeDtypeStruct((3, 8, n), F32),
        compiler_params=_params(),
    )(table_t, onehot)


def _bias_reduce(dsum, onehot):
    n = onehot.shape[-1]

    def body(d_ref, oh_ref, o_ref):
        o_ref[...] = _dot_nt_exact(d_ref[...], oh_ref[...])

    return pl.pallas_call(
        body, name="bias_reduce", grid=(3,),
        in_specs=[pl.BlockSpec((None, 8, n), lambda g: (g, 0, 0)),
                  pl.BlockSpec((None, N_BUCKETS, n), lambda g: (g, 0, 0))],
        out_specs=pl.BlockSpec((None, 8, N_BUCKETS), lambda g: (g, 0, 0)),
        out_shape=jax.ShapeDtypeStruct((3, 8, N_BUCKETS), F32),
        compiler_params=_params(),
    )(dsum, onehot)


def _attn_masks(n):
    qi = lax.broadcasted_iota(jnp.int32, (ATTN_BLOCK, 2 * ATTN_BLOCK), 0)
    kj = lax.broadcasted_iota(jnp.int32, (ATTN_BLOCK, 2 * ATTN_BLOCK), 1)
    steps = qi + ATTN_BLOCK - kj
    valid = (steps >= 0) & (steps <= WINDOW_STEPS) & ((n > 0) | (kj >= ATTN_BLOCK))
    head_of_col = lax.broadcasted_iota(jnp.int32, (ATTN_BLOCK, GROUP_WIDTH), 1) // HEAD_DIM
    return valid, head_of_col


def _attn_fwd(q, k, v, bias, grp, name):
    L = q.shape[0]
    d = DILATIONS[grp]
    M = L // d
    nb = M // ATTN_BLOCK
    q2, k2, v2 = (t.reshape(M, d * ATTN_WIDTH) for t in (q, k, v))

    def body(q_ref, kp_ref, kc_ref, vp_ref, vc_ref, bias_ref, o_ref, lse_ref):
        n = pl.program_id(1)
        valid, head_of_col = _attn_masks(n)
        qb = q_ref[...]
        kk = jnp.concatenate([kp_ref[...], kc_ref[...]], axis=0)
        vv = jnp.concatenate([vp_ref[...], vc_ref[...]], axis=0)
        o_acc = jnp.zeros((ATTN_BLOCK, GROUP_WIDTH), F32)
        lse_acc = jnp.zeros((ATTN_BLOCK, GROUP_WIDTH), F32)
        for hh in range(HEADS_PER_GROUP):
            hm = head_of_col == hh
            qh = jnp.where(hm, qb, jnp.zeros_like(qb))
            logits = jnp.where(valid, _dot_nt(qh, kk) + bias_ref[hh], NEG_INF)
            m = jnp.max(logits, axis=1, keepdims=True)
            p = jnp.exp(logits - m)
            den = jnp.sum(p, axis=1, keepdims=True)
            oh = _dot(p.astype(BF16), vv) / den
            o_acc = jnp.where(hm, oh, o_acc)
            lse_acc = jnp.where(hm, m + jnp.log(den), lse_acc)
        o_ref[...] = o_acc
        lse_ref[...] = lse_acc

    cur = pl.BlockSpec((ATTN_BLOCK, GROUP_WIDTH), lambda r, n: (n, 3 * r + grp))
    prev = pl.BlockSpec((ATTN_BLOCK, GROUP_WIDTH), lambda r, n: (jnp.maximum(n - 1, 0), 3 * r + grp))
    out = pl.BlockSpec((ATTN_BLOCK, GROUP_WIDTH), lambda r, n: (n, r))
    o, lse = pl.pallas_call(
        body, name=name, grid=(d, nb),
        in_specs=[cur, prev, cur, prev, cur, pl.BlockSpec((HEADS_PER_GROUP, ATTN_BLOCK, 2 * ATTN_BLOCK), lambda r, n: (0, 0, 0))],
        out_specs=[out, out],
        out_shape=[jax.ShapeDtypeStruct((M, d * GROUP_WIDTH), F32)] * 2,
        compiler_params=_params(),
    )(q2, k2, k2, v2, v2, bias)
    return o.reshape(L, GROUP_WIDTH), lse.reshape(L, GROUP_WIDTH)


def _attn_bwd(q, k, v, do, lse, delta, bias, grp, name):
    L = q.shape[0]
    d = DILATIONS[grp]
    M = L // d
    nb = M // ATTN_BLOCK
    q2, k2, v2 = (t.reshape(M, d * ATTN_WIDTH) for t in (q, k, v))
    do2, lse2, dl2 = (t.reshape(M, d * GROUP_WIDTH) for t in (do, lse, delta))

    def body(q_ref, kp_ref, kc_ref, vp_ref, vc_ref, do_ref, lse_ref, dl_ref, bias_ref,
             dq_ref, dk_ref, dv_ref, dsum_ref, ck_ref, cv_ref):
        r = pl.program_id(0)
        n = pl.program_id(1)

        @pl.when((r == 0) & (n == 0))
        def _():
            dsum_ref[...] = jnp.zeros_like(dsum_ref)

        @pl.when(n == 0)
        def _():
            ck_ref[...] = jnp.zeros_like(ck_ref)
            cv_ref[...] = jnp.zeros_like(cv_ref)

        @pl.when(n < nb)
        def _():
            valid, head_of_col = _attn_masks(n)
            qb = q_ref[...]
            dob = do_ref[...]
            kk = jnp.concatenate([kp_ref[...], kc_ref[...]], axis=0)
            vv = jnp.concatenate([vp_ref[...], vc_ref[...]], axis=0)
            dq_acc = jnp.zeros((ATTN_BLOCK, GROUP_WIDTH), F32)
            dkk = jnp.zeros((2 * ATTN_BLOCK, GROUP_WIDTH), F32)
            dvv = jnp.zeros((2 * ATTN_BLOCK, GROUP_WIDTH), F32)
            for hh in range(HEADS_PER_GROUP):
                hm = head_of_col == hh
                c0 = hh * HEAD_DIM
                qh = jnp.where(hm, qb, jnp.zeros_like(qb))
                doh = jnp.where(hm, dob, jnp.zeros_like(dob))
                logits = jnp.where(valid, _dot_nt(qh, kk) + bias_ref[hh], NEG_INF)
                p = jnp.exp(logits - lse_ref[:, c0:c0 + 1])
                dp = _dot_nt(doh, vv)
                ds = p * (dp - dl_ref[:, c0:c0 + 1])
                dsum_ref[hh] += ds
                ds16 = ds.astype(BF16)
                dq_acc = jnp.where(hm, _dot(ds16, kk), dq_acc)
                dkk = dkk + _dot_tn(ds16, qh)
                dvv = dvv + _dot_tn(p.astype(BF16), doh)
            dq_ref[...] = (dq_acc * Q_SCALE).astype(BF16)
            dk_ref[...] = (ck_ref[...] + dkk[:ATTN_BLOCK]).astype(BF16)
            dv_ref[...] = (cv_ref[...] + dvv[:ATTN_BLOCK]).astype(BF16)
            ck_ref[...] = dkk[ATTN_BLOCK:]
            cv_ref[...] = dvv[ATTN_BLOCK:]

        @pl.when(n == nb)
        def _():
            dk_ref[...] = ck_ref[...].astype(BF16)
            dv_ref[...] = cv_ref[...].astype(BF16)

    def clamp(n):
        return jnp.minimum(n, nb - 1)

    blk = (ATTN_BLOCK, GROUP_WIDTH)
    cur = pl.BlockSpec(blk, lambda r, n: (clamp(n), 3 * r + grp))
    prev = pl.BlockSpec(blk, lambda r, n: (jnp.maximum(clamp(n) - 1, 0), 3 * r + grp))
    row = pl.BlockSpec(blk, lambda r, n: (clamp(n), r))
    lag = pl.BlockSpec(blk, lambda r, n: (jnp.maximum(n - 1, 0), r))
    full = pl.BlockSpec((HEADS_PER_GROUP, ATTN_BLOCK, 2 * ATTN_BLOCK), lambda r, n: (0, 0, 0))
    dq, dk, dv, dsum = pl.pallas_call(
        body, name=name, grid=(d, nb + 1),
        in_specs=[cur, prev, cur, prev, cur, row, row, row, full],
        out_specs=[row, lag, lag, full],
        out_shape=[jax.ShapeDtypeStruct((M, d * GROUP_WIDTH), BF16)] * 3
        + [jax.ShapeDtypeStruct((HEADS_PER_GROUP, ATTN_BLOCK, 2 * ATTN_BLOCK), F32)],
        scratch_shapes=[pltpu.VMEM(blk, F32), pltpu.VMEM(blk, F32)],
        compiler_params=_params(),
    )(q2, k2, k2, v2, v2, do2, lse2, dl2, bias)
    return dq.reshape(L, GROUP_WIDTH), dk.reshape(L, GROUP_WIDTH), dv.reshape(L, GROUP_WIDTH), dsum


def _disc_math(a_re, a_im, ldt, b_re, b_im):
    dt = jnp.exp(ldt)
    mag = jnp.exp(a_re * dt)
    ab_re = mag * jnp.cos(a_im * dt)
    ab_im = mag * jnp.sin(a_im * dt)
    den = a_re * a_re + a_im * a_im
    xr = ab_re - 1.0
    coef_re = (xr * a_re + ab_im * a_im) / den
    coef_im = (ab_im * a_re - xr * a_im) / den
    return ab_re, ab_im, coef_re * b_re - coef_im * b_im, coef_re * b_im + coef_im * b_re


def _block_diag_mask():
    row_g = lax.broadcasted_iota(jnp.int32, (SSM_WIDTH, 2 * NS), 0) // SSM_GROUP
    col = lax.broadcasted_iota(jnp.int32, (SSM_WIDTH, 2 * NS), 1)
    col_g = jnp.where(col >= NS, col - NS, col) // SSM_STATE
    return row_g == col_g


def _disc_fwd(a_re, a_im, ldt, b_re, b_im, c_re, c_im):
    def body(are_ref, aim_ref, ldt_ref, bre_ref, bim_ref, cre_ref, cim_ref, pw_ref, pwr_ref, bd_ref, cdt_ref):
        ab_re, ab_im, bb_re, bb_im = _disc_math(are_ref[...], aim_ref[...], ldt_ref[...], bre_ref[...], bim_ref[...])
        row = lax.broadcasted_iota(jnp.int32, (8, NS), 0)
        pr, pi = ab_re, ab_im
        t_re = jnp.zeros((8, NS), F32)
        t_im = jnp.zeros((8, NS), F32)
        u_re = jnp.zeros((8, NS), F32)
        u_im = jnp.zeros((8, NS), F32)
        for j in range(8):
            t_re = jnp.where(row == j, pr, t_re)
            t_im = jnp.where(row == j, pi, t_im)
            u_re = jnp.where(row == 7 - j, pr, u_re)
            u_im = jnp.where(row == 7 - j, pi, u_im)
            pr, pi = pr * ab_re - pi * ab_im, pr * ab_im + pi * ab_re
        pw_ref[0] = t_re
        pw_ref[1] = t_im
        pwr_ref[0] = u_re
        pwr_ref[1] = u_im
        mask = _block_diag_mask()
        zero = jnp.zeros((SSM_WIDTH, 2 * NS), F32)
        bfull = jnp.concatenate([jnp.concatenate([bb_re] * SSM_GROUPS, axis=0),
                                 jnp.concatenate([bb_im] * SSM_GROUPS, axis=0)], axis=1)
        bd_ref[...] = jnp.where(mask, bfull, zero).astype(BF16)
        cfull = jnp.concatenate([jnp.concatenate([cre_ref[...]] * SSM_GROUPS, axis=0),
                                 jnp.concatenate([-cim_ref[...]] * SSM_GROUPS, axis=0)], axis=1)
        cdt_ref[...] = jnp.where(mask, cfull, zero).astype(BF16)

    return pl.pallas_call(
        body, name="s5_disc_fwd",
        in_specs=[_whole()] * 7, out_specs=[_whole()] * 4,
        out_shape=[jax.ShapeDtypeStruct((2, 8, NS), F32), jax.ShapeDtypeStruct((2, 8, NS), F32),
                   jax.ShapeDtypeStruct((SSM_WIDTH, 2 * NS), BF16), jax.ShapeDtypeStruct((SSM_WIDTH, 2 * NS), BF16)],
        compiler_params=_params(),
    )(a_re, a_im, ldt, b_re, b_im, c_re, c_im)


def _disc_bwd(a_re, a_im, ldt, b_re, b_im, d_bd, d_cdt, d_ab, group_sum):
    def body(are_ref, aim_ref, ldt_ref, bre_ref, bim_ref, dbd_ref, dcdt_ref, dab_ref, gs_ref,
             dare_ref, daim_ref, dldt_ref, dbre_ref, dbim_ref, dcre_ref, dcim_ref):
        col = lax.broadcasted_iota(jnp.int32, (SSM_GROUP, 2 * NS), 1)
        col_g = jnp.where(col >= NS, col - NS, col) // SSM_STATE
        acc_b = jnp.zeros((SSM_GROUP, 2 * NS), F32)
        acc_c = jnp.zeros((SSM_GROUP, 2 * NS), F32)
        for g in range(SSM_GROUPS):
            rows = slice(g * SSM_GROUP, (g + 1) * SSM_GROUP)
            acc_b = acc_b + jnp.where(col_g == g, dbd_ref[rows, :], 0.0)
            acc_c = acc_c + jnp.where(col_g == g, dcdt_ref[rows, :], 0.0)
        dcre_ref[...] = acc_c[:, :NS]
        dcim_ref[...] = -acc_c[:, NS:]
        dab_re = jnp.sum(dab_ref[0], axis=0, keepdims=True)
        dab_im = jnp.sum(dab_ref[1], axis=0, keepdims=True)
        _, vjp = jax.vjp(_disc_math, are_ref[...], aim_ref[...], ldt_ref[...], bre_ref[...], bim_ref[...])
        d_are, d_aim, d_ldt, d_bre, d_bim = vjp((dab_re, dab_im, acc_b[:, :NS], acc_b[:, NS:]))
        dare_ref[...] = d_are
        daim_ref[...] = d_aim
        dbre_ref[...] = d_bre
        dbim_ref[...] = d_bim
        dldt_ref[...] = _dot_exact(jnp.broadcast_to(d_ldt, (8, NS)), gs_ref[...])

    vec = jax.ShapeDtypeStruct((1, NS), F32)
    mat = jax.ShapeDtypeStruct((SSM_GROUP, NS), F32)
    return pl.pallas_call(
        body, name="s5_disc_bwd",
        in_specs=[_whole()] * 9, out_specs=[_whole()] * 7,
        out_shape=[vec, vec, jax.ShapeDtypeStruct((8, 128), F32), mat, mat, mat, mat],
        compiler_params=_params(),
    )(a_re, a_im, ldt, b_re, b_im, d_bd, d_cdt, d_ab, group_sum)


def _scan_blocks(buf, pw_ref, carry_ref, n_blocks, reverse):
    row = lax.broadcasted_iota(jnp.int32, (8, SCAN_LANES), 0)
    for lc in range(NS // SCAN_LANES):
        re_cols = pl.ds(lc * SCAN_LANES, SCAN_LANES)
        im_cols = pl.ds(NS + lc * SCAN_LANES, SCAN_LANES)
        pr = pw_ref[0, :, re_cols]
        pi = pw_ref[1, :, re_cols]
        if reverse:
            pi = -pi
            base = [(7, 1), (6, 2), (4, 4)]
            coef = [(jnp.where(row < 8 - k, pr[j:j + 1], 0.0), jnp.where(row < 8 - k, pi[j:j + 1], 0.0), 8 - k)
                    for j, k in base]
        else:
            base = [(0, 1), (1, 2), (3, 4)]
            coef = [(jnp.where(row >= k, pr[j:j + 1], 0.0), jnp.where(row >= k, pi[j:j + 1], 0.0), k)
                    for j, k in base]

        def step(i, carry, pr=pr, pi=pi, coef=coef, re_cols=re_cols, im_cols=im_cols):
            cr, ci = carry
            blk = (n_blocks - 1 - i) if reverse else i
            rows = pl.ds(pl.multiple_of(blk * 8, 8), 8)
            xr = buf[rows, re_cols]
            xi = buf[rows, im_cols]
            for kr, ki, shift in coef:
                sr = pltpu.roll(xr, shift, 0)
                si = pltpu.roll(xi, shift, 0)
                xr, xi = xr + kr * sr - ki * si, xi + kr * si + ki * sr
            xr, xi = xr + pr * cr - pi * ci, xi + pr * ci + pi * cr
            buf[rows, re_cols] = xr
            buf[rows, im_cols] = xi
            edge = slice(0, 1) if reverse else slice(7, 8)
            return xr[edge], xi[edge]

        cr, ci = lax.fori_loop(0, n_blocks, step, (carry_ref[0:1, re_cols], carry_ref[0:1, im_cols]))
        carry_ref[0:1, re_cols] = cr
        carry_ref[0:1, im_cols] = ci


def _ssm_fwd(u, bd, cdt, d_skip, pw):
    L = u.shape[0]
    tc = min(SSM_CHUNK, L)

    def body(u_ref, bd_ref, cdt_ref, dsk_ref, pw_ref, y_ref, s_ref, carry_ref):
        @pl.when(pl.program_id(0) == 0)
        def _():
            carry_ref[...] = jnp.zeros_like(carry_ref)

        uv = u_ref[...]
        s_ref[...] = _dot(uv.astype(BF16), bd_ref[...])
        _scan_blocks(s_ref, pw_ref, carry_ref, tc // 8, reverse=False)
        y_ref[...] = _dot_nt(s_ref[...].astype(BF16), cdt_ref[...]) + dsk_ref[...] * uv

    return pl.pallas_call(
        body, name="s5_fwd", grid=(L // tc,),
        in_specs=[_rows(tc, SSM_WIDTH), _whole(), _whole(), _whole(), _whole()],
        out_specs=[_rows(tc, SSM_WIDTH), _rows(tc, 2 * NS)],
        out_shape=[jax.ShapeDtypeStruct((L, SSM_WIDTH), F32), jax.ShapeDtypeStruct((L, 2 * NS), F32)],
        scratch_shapes=[pltpu.VMEM((8, 2 * NS), F32)],
        compiler_params=_params(),
    )(u, bd, cdt, d_skip, pw)


def _ssm_bwd(dy, u, s, bd, cdt, d_skip, pwr):
    L = u.shape[0]
    tc = min(SSM_CHUNK, L)
    nc = L // tc
    blocks = tc // 8

    def body(dy_ref, u_ref, s_ref, sprev_ref, bd_ref, cdt_ref, dsk_ref, pwr_ref,
             du_ref, ddsk_ref, dbd_ref, dcdt_ref, dab_ref, g_ref, sx_ref, carry_ref):
        i = pl.program_id(0)

        @pl.when(i == 0)
        def _():
            carry_ref[...] = jnp.zeros_like(carry_ref)
            ddsk_ref[...] = jnp.zeros_like(ddsk_ref)
            dbd_ref[...] = jnp.zeros_like(dbd_ref)
            dcdt_ref[...] = jnp.zeros_like(dcdt_ref)
            dab_ref[...] = jnp.zeros_like(dab_ref)

        dyv = dy_ref[...]
        uv = u_ref[...]
        dy16 = dyv.astype(BF16)
        g_ref[...] = _dot(dy16, cdt_ref[...])
        _scan_blocks(g_ref, pwr_ref, carry_ref, blocks, reverse=True)
        g16 = g_ref[...].astype(BF16)
        du_ref[...] = _dot_nt(g16, bd_ref[...]) + dsk_ref[...] * dyv
        ddsk_ref[...] += jnp.sum(dyv * uv, axis=0, keepdims=True)
        dbd_ref[...] += _dot_tn(uv.astype(BF16), g16)
        dcdt_ref[...] += _dot_tn(dy16, s_ref[...].astype(BF16))

        sx_ref[pl.ds(8, tc), :] = s_ref[...]
        sx_ref[pl.ds(0, 8), :] = jnp.where(i == nc - 1, 0.0, sprev_ref[...])
        row = lax.broadcasted_iota(jnp.int32, (8, SCAN_LANES), 0)
        for lc in range(NS // SCAN_LANES):
            re_cols = pl.ds(lc * SCAN_LANES, SCAN_LANES)
            im_cols = pl.ds(NS + lc * SCAN_LANES, SCAN_LANES)

            def step(b, acc, re_cols=re_cols, im_cols=im_cols):
                ar, ai = acc
                off = pl.multiple_of(b * 8, 8)
                gr = g_ref[pl.ds(off, 8), re_cols]
                gi = g_ref[pl.ds(off, 8), im_cols]
                before = pl.ds(off, 8)
                here = pl.ds(off + 8, 8)
                sr = jnp.where(row == 0, sx_ref[before, re_cols][7:8], pltpu.roll(sx_ref[here, re_cols], 1, 0))
                si = jnp.where(row == 0, sx_ref[before, im_cols][7:8], pltpu.roll(sx_ref[here, im_cols], 1, 0))
                return ar + gr * sr + gi * si, ai + gi * sr - gr * si

            zero = jnp.zeros((8, SCAN_LANES), F32)
            ar, ai = lax.fori_loop(0, blocks, step, (zero, zero))
            dab_ref[0, :, re_cols] += ar
            dab_ref[1, :, re_cols] += ai

    rev = lambda i: (nc - 1 - i, 0)
    sprev = pl.BlockSpec((8, 2 * NS), lambda i: (jnp.maximum((nc - 1 - i) * blocks - 1, 0), 0))
    return pl.pallas_call(
        body, name="s5_bwd", grid=(nc,),
        in_specs=[pl.BlockSpec((tc, SSM_WIDTH), rev), pl.BlockSpec((tc, SSM_WIDTH), rev),
                  pl.BlockSpec((tc, 2 * NS), rev), sprev, _whole(), _whole(), _whole(), _whole()],
        out_specs=[pl.BlockSpec((tc, SSM_WIDTH), rev), _whole(), _whole(), _whole(), _whole()],
        out_shape=[jax.ShapeDtypeStruct((L, SSM_WIDTH), F32), jax.ShapeDtypeStruct((1, SSM_WIDTH), F32),
                   jax.ShapeDtypeStruct((SSM_WIDTH, 2 * NS), F32), jax.ShapeDtypeStruct((SSM_WIDTH, 2 * NS), F32),
                   jax.ShapeDtypeStruct((2, 8, NS), F32)],
        scratch_shapes=[pltpu.VMEM((tc, 2 * NS), F32), pltpu.VMEM((tc + 8, 2 * NS), F32), pltpu.VMEM((8, 2 * NS), F32)],
        compiler_params=_params(),
    )(dy, u, s, s, bd, cdt, d_skip, pwr)


def _branches(o_attn, y, gates, w_ab, w_glu, w_sb):
    ya = _dot(o_attn.astype(BF16), w_ab[...])
    gel = _gelu(y)
    glu = _dot(gel.astype(BF16), w_glu[...])
    p = glu[:, :SSM_WIDTH]
    sg = _sigmoid(glu[:, SSM_WIDTH:])
    ys2 = p * sg
    ysb = _dot(ys2.astype(BF16), w_sb[...])
    ga = gates[:, :D_MODEL]
    gs = gates[:, D_MODEL:]
    return ya, gel, p, sg, ys2, ysb, ga, gs


def _mix_out_fwd(x1, o_g, lse_g, y, gates, w_ab, w_glu, w_sb, w_out):
    L = x1.shape[0]
    tm = min(ROW_TILE, L)

    def body(x_ref, o0, o1, o2, l0, l1, l2, y_ref, gate_ref, wab_ref, wglu_ref, wsb_ref, wout_ref,
             x2_ref, oat_ref, lse_ref):
        la, lb, lc = l0[...], l1[...], l2[...]
        m = jnp.maximum(jnp.maximum(la, lb), lc)
        ea, eb, ec = jnp.exp(la - m), jnp.exp(lb - m), jnp.exp(lc - m)
        tot = ea + eb + ec
        o_attn = (ea * o0[...] + eb * o1[...] + ec * o2[...]) / tot
        oat_ref[...] = o_attn
        lse_ref[...] = m + jnp.log(tot)
        ya, _, _, _, _, ysb, ga, gs = _branches(o_attn, y_ref[...], gate_ref[...], wab_ref, wglu_ref, wsb_ref)
        mix = ga * ya + gs * ysb
        x2_ref[...] = x_ref[...] + _dot(mix.astype(BF16), wout_ref[...])

    grp = _rows(tm, GROUP_WIDTH)
    return pl.pallas_call(
        body, name="mix_out_fwd", grid=(L // tm,),
        in_specs=[_rows(tm, D_MODEL)] + [grp] * 6 + [_rows(tm, SSM_WIDTH), _rows(tm, 2 * D_MODEL)] + [_whole()] * 4,
        out_specs=[_rows(tm, D_MODEL), grp, grp],
        out_shape=[jax.ShapeDtypeStruct((L, D_MODEL), F32), jax.ShapeDtypeStruct((L, GROUP_WIDTH), F32),
                   jax.ShapeDtypeStruct((L, GROUP_WIDTH), F32)],
        compiler_params=_params(),
    )(x1, *o_g, *lse_g, y, gates, w_ab, w_glu, w_sb, w_out)


def _mix_out_bwd(dx2, o_attn, y, gates, w_ab, w_glu, w_sb, w_out, head_sum):
    L = dx2.shape[0]
    tm = min(ROW_TILE, L)

    def body(dx_ref, oat_ref, y_ref, gate_ref, wab_ref, wglu_ref, wsb_ref, wout_ref, hs_ref,
             do_ref, dl_ref, dy_ref, dgp_ref, mix_ref, dya_ref, dys_ref, ys2_ref, gel_ref, dglu_ref, dgb_ref):
        i = pl.program_id(0)
        o_attn = oat_ref[...]
        yv = y_ref[...]
        ya, gel, p, sg, ys2, ysb, ga, gs = _branches(o_attn, yv, gate_ref[...], wab_ref, wglu_ref, wsb_ref)
        mix_ref[...] = (ga * ya + gs * ysb).astype(BF16)
        ys2_ref[...] = ys2.astype(BF16)
        gel_ref[...] = gel.astype(BF16)
        dmix = _dot_nt(dx_ref[...].astype(BF16), wout_ref[...])
        dgp = jnp.concatenate([dmix * ya * ga * (1.0 - ga), dmix * ysb * gs * (1.0 - gs)], axis=1)
        dgp_ref[...] = dgp.astype(BF16)

        @pl.when(i == 0)
        def _():
            dgb_ref[...] = jnp.zeros_like(dgb_ref)

        dgb_ref[...] += jnp.sum(dgp, axis=0, keepdims=True)
        dya = (dmix * ga).astype(BF16)
        dys = (dmix * gs).astype(BF16)
        dya_ref[...] = dya
        dys_ref[...] = dys
        d_o = _dot_nt(dya, wab_ref[...])
        do_ref[...] = d_o.astype(BF16)
        dl_ref[...] = _dot_exact(d_o * o_attn, hs_ref[...])
        dys2 = _dot_nt(dys, wsb_ref[...])
        dglu = jnp.concatenate([dys2 * sg, dys2 * p * sg * (1.0 - sg)], axis=1).astype(BF16)
        dglu_ref[...] = dglu
        dy_ref[...] = _dot_nt(dglu, wglu_ref[...]) * _gelu_grad(yv)

    grp = _rows(tm, GROUP_WIDTH)
    wide = _rows(tm, D_MODEL)
    half = _rows(tm, SSM_WIDTH)
    sds = jax.ShapeDtypeStruct
    return pl.pallas_call(
        body, name="mix_out_bwd", grid=(L // tm,),
        in_specs=[wide, grp, half, _rows(tm, 2 * D_MODEL)] + [_whole()] * 5,
        out_specs=[grp, grp, half, _rows(tm, 2 * D_MODEL), wide, wide, wide, half, half, wide, _acc_row(2 * D_MODEL)],
        out_shape=[sds((L, GROUP_WIDTH), BF16), sds((L, GROUP_WIDTH), F32), sds((L, SSM_WIDTH), F32),
                   sds((L, 2 * D_MODEL), BF16), sds((L, D_MODEL), BF16), sds((L, D_MODEL), BF16),
                   sds((L, D_MODEL), BF16), sds((L, SSM_WIDTH), BF16), sds((L, SSM_WIDTH), BF16),
                   sds((L, D_MODEL), BF16), sds((1, 2 * D_MODEL), F32)],
        compiler_params=_params(),
    )(dx2, o_attn, y, gates, w_ab, w_glu, w_sb, w_out, head_sum)


def _adamw(w, g, m, v, name):
    R, C = w.shape
    tr = _row_tile(R, 256)

    def body(w_ref, g_ref, m_ref, v_ref, d_ref, mo_ref, vo_ref):
        gv = g_ref[...]
        mn = ADAM_B1 * m_ref[...] + (1.0 - ADAM_B1) * gv
        vn = ADAM_B2 * v_ref[...] + (1.0 - ADAM_B2) * (gv * gv)
        m_hat = mn / (1.0 - ADAM_B1 ** ADAM_STEP)
        v_hat = vn / (1.0 - ADAM_B2 ** ADAM_STEP)
        d_ref[...] = -ADAM_LR * (m_hat / (jnp.sqrt(v_hat) + ADAM_EPS) + ADAM_WD * w_ref[...])
        mo_ref[...] = mn
        vo_ref[...] = vn

    blk = pl.BlockSpec((tr, C), lambda i: (i, 0))
    return pl.pallas_call(
        body, name=name, grid=(R // tr,),
        in_specs=[blk] * 4, out_specs=[blk] * 3,
        out_shape=[jax.ShapeDtypeStruct((R, C), F32)] * 3,
        compiler_params=_params(),
    )(w, g, m, v)


def _sum_slots(x, name):
    S, R, C = x.shape
    tr = _row_tile(R, 512)

    def body(x_ref, o_ref):
        acc = x_ref[0]
        for k in range(1, S):
            acc = acc + x_ref[k]
        o_ref[...] = acc

    return pl.pallas_call(
        body, name=name, grid=(R // tr,),
        in_specs=[pl.BlockSpec((S, tr, C), lambda i: (0, i, 0))],
        out_specs=pl.BlockSpec((tr, C), lambda i: (i, 0)),
        out_shape=jax.ShapeDtypeStruct((R, C), F32),
        compiler_params=_params(),
    )(x)


def _add_halves(g, r1, name):
    S, R, C = g.shape
    H = R // 2
    tr = _row_tile(H, 512)
    hb = H // tr

    def body(c_ref, g_ref, r_ref, o_ref):
        o_ref[...] = g_ref[...] + r_ref[...]

    core = lax.axis_index("c").astype(jnp.int32).reshape(1)
    return pl.pallas_call(
        body, name=name,
        grid_spec=pltpu.PrefetchScalarGridSpec(
            num_scalar_prefetch=1, grid=(S, hb),
            in_specs=[pl.BlockSpec((None, tr, C), lambda j, i, c_ref: (j, c_ref[0] * hb + i, 0)),
                      pl.BlockSpec((None, tr, C), lambda j, i, c_ref: (j, i, 0))],
            out_specs=pl.BlockSpec((None, tr, C), lambda j, i, c_ref: (j, i, 0))),
        out_shape=jax.ShapeDtypeStruct((S, H, C), F32),
        compiler_params=_params(),
    )(core, g, r1)


_ANY = pl.BlockSpec(memory_space=pl.ANY)


def _place():
    x, y, c = lax.axis_index("x"), lax.axis_index("y"), lax.axis_index("c")
    chips = [(1 - x, y), (x, 1 - y), (1 - x, 1 - y)]
    return x, y, c, chips


def _gather_weights(shard):
    R, C = shard.shape
    H = R // 2

    def body(w_ref, out_ref, send_sems, recv_sems, local_sem):
        x, y, c, chips = _place()
        me = 2 * x + y
        sibling = (x, y, 1 - c)

        def half(chip_idx, core):
            return out_ref.at[chip_idx, pl.ds(core * H, H), :]

        mine = pltpu.make_async_copy(w_ref, out_ref.at[me], local_sem)
        mine.start()
        first = []
        for j, (cx, cy) in enumerate(chips):
            first.append(pltpu.make_async_remote_copy(
                src_ref=w_ref.at[pl.ds(c * H, H), :], dst_ref=half(me, c),
                send_sem=send_sems.at[j], recv_sem=recv_sems.at[j], device_id=(cx, cy, c), device_id_type=MESH))
        for cp in first:
            cp.start()
        passed = []
        for j, (cx, cy) in enumerate(chips):
            landed = half(2 * cx + cy, c)
            pltpu.make_async_remote_copy(
                src_ref=landed, dst_ref=landed, send_sem=send_sems.at[j], recv_sem=recv_sems.at[j],
                device_id=(cx, cy, c), device_id_type=MESH).wait_recv()
            fwd = pltpu.make_async_remote_copy(
                src_ref=landed, dst_ref=landed, send_sem=send_sems.at[3 + j], recv_sem=recv_sems.at[3 + j],
                device_id=sibling, device_id_type=MESH)
            fwd.start()
            passed.append(fwd)
        for j, (cx, cy) in enumerate(chips):
            other = half(2 * cx + cy, 1 - c)
            pltpu.make_async_remote_copy(
                src_ref=other, dst_ref=other, send_sem=send_sems.at[3 + j], recv_sem=recv_sems.at[3 + j],
                device_id=sibling, device_id_type=MESH).wait_recv()
        for cp in first + passed:
            cp.wait_send()
        mine.wait()

    return pl.pallas_call(
        body, name="gather_weights",
        in_specs=[_ANY], out_specs=_ANY,
        out_shape=jax.ShapeDtypeStruct((N_SHARD, R, C), shard.dtype),
        scratch_shapes=[pltpu.SemaphoreType.DMA((6,)), pltpu.SemaphoreType.DMA((6,)), pltpu.SemaphoreType.DMA],
    )(shard)


def _swap_halves(g):
    S, R, C = g.shape
    H = R // 2

    def body(g_ref, out_ref, send_sem, recv_sem):
        x, y, c, _ = _place()
        cp = pltpu.make_async_remote_copy(
            src_ref=g_ref.at[:, pl.ds((1 - c) * H, H), :], dst_ref=out_ref,
            send_sem=send_sem, recv_sem=recv_sem, device_id=(x, y, 1 - c), device_id_type=MESH)
        cp.start()
        cp.wait()

    return pl.pallas_call(
        body, name="reduce_swap_halves",
        in_specs=[_ANY], out_specs=_ANY,
        out_shape=jax.ShapeDtypeStruct((S, H, C), F32),
        scratch_shapes=[pltpu.SemaphoreType.DMA, pltpu.SemaphoreType.DMA],
    )(g)


def _exchange_chips(t):
    S, H, C = t.shape

    def body(t_ref, out_ref, send_sems, recv_sems, local_sem):
        x, y, c, chips = _place()
        me = 2 * x + y
        mine = pltpu.make_async_copy(t_ref.at[me], out_ref.at[me], local_sem)
        mine.start()
        sent = []
        for j, (cx, cy) in enumerate(chips):
            cp = pltpu.make_async_remote_copy(
                src_ref=t_ref.at[2 * cx + cy], dst_ref=out_ref.at[me],
                send_sem=send_sems.at[j], recv_sem=recv_sems.at[j], device_id=(cx, cy, c), device_id_type=MESH)
            cp.start()
            sent.append(cp)
        for j, (cx, cy) in enumerate(chips):
            slot = out_ref.at[2 * cx + cy]
            pltpu.make_async_remote_copy(
                src_ref=slot, dst_ref=slot, send_sem=send_sems.at[j], recv_sem=recv_sems.at[j],
                device_id=(cx, cy, c), device_id_type=MESH).wait_recv()
        for cp in sent:
            cp.wait_send()
        mine.wait()

    return pl.pallas_call(
        body, name="reduce_exchange_chips",
        in_specs=[_ANY], out_specs=_ANY,
        out_shape=jax.ShapeDtypeStruct((S, H, C), F32),
        scratch_shapes=[pltpu.SemaphoreType.DMA((3,)), pltpu.SemaphoreType.DMA((3,)), pltpu.SemaphoreType.DMA],
    )(t)


def _join_halves(f):
    H, C = f.shape

    def body(f_ref, out_ref, send_sem, recv_sem, local_sem):
        x, y, c, _ = _place()
        mine = pltpu.make_async_copy(f_ref, out_ref.at[pl.ds(c * H, H), :], local_sem)
        mine.start()
        cp = pltpu.make_async_remote_copy(
            src_ref=f_ref, dst_ref=out_ref.at[pl.ds(c * H, H), :],
            send_sem=send_sem, recv_sem=recv_sem, device_id=(x, y, 1 - c), device_id_type=MESH)
        cp.start()
        other = out_ref.at[pl.ds((1 - c) * H, H), :]
        pltpu.make_async_remote_copy(
            src_ref=other, dst_ref=other, send_sem=send_sem, recv_sem=recv_sem,
            device_id=(x, y, 1 - c), device_id_type=MESH).wait_recv()
        cp.wait_send()
        mine.wait()

    return pl.pallas_call(
        body, name="reduce_join_halves",
        in_specs=[_ANY], out_specs=_ANY,
        out_shape=jax.ShapeDtypeStruct((2 * H, C), F32),
        scratch_shapes=[pltpu.SemaphoreType.DMA, pltpu.SemaphoreType.DMA, pltpu.SemaphoreType.DMA],
    )(f)


def _gather_small(v):
    R, C = v.shape

    def body(v_ref, out_ref, send_sems, recv_sems, local_sem):
        x, y, c, _ = _place()
        me = 4 * x + 2 * y + c
        mine = pltpu.make_async_copy(v_ref, out_ref.at[me], local_sem)
        mine.start()
        flips = [(fx, fy, fc) for fx in (0, 1) for fy in (0, 1) for fc in (0, 1)][1:]
        peers = [((1 - x) if fx else x, (1 - y) if fy else y, (1 - c) if fc else c) for fx, fy, fc in flips]
        sent = []
        for j, peer in enumerate(peers):
            cp = pltpu.make_async_remote_copy(
                src_ref=v_ref, dst_ref=out_ref.at[me], send_sem=send_sems.at[j], recv_sem=recv_sems.at[j],
                device_id=peer, device_id_type=MESH)
            cp.start()
            sent.append(cp)
        for j, peer in enumerate(peers):
            slot = out_ref.at[4 * peer[0] + 2 * peer[1] + peer[2]]
            pltpu.make_async_remote_copy(
                src_ref=slot, dst_ref=slot, send_sem=send_sems.at[j], recv_sem=recv_sems.at[j],
                device_id=peer, device_id_type=MESH).wait_recv()
        for cp in sent:
            cp.wait_send()
        mine.wait()

    return pl.pallas_call(
        body, name="gather_small",
        in_specs=[_ANY], out_specs=_ANY,
        out_shape=jax.ShapeDtypeStruct((8, R, C), F32),
        scratch_shapes=[pltpu.SemaphoreType.DMA((7,)), pltpu.SemaphoreType.DMA((7,)), pltpu.SemaphoreType.DMA],
    )(v)


def _reduce_scatter(g):
    r1 = _swap_halves(g)
    t = _add_halves(g, r1, "reduce_add_cores")
    u = _exchange_chips(t)
    f = _sum_slots(u, "reduce_add_chips")
    return _join_halves(f)


BIG = ["ffn1_w_gate", "ffn1_w_up", "ffn1_w_down", "w_in", "ssm_w_glu", "w_attn_branch", "w_ssm_branch",
       "w_out", "ffn2_w_gate", "ffn2_w_up", "ffn2_w_down"]
SMALL = ["ffn1_norm", "mix_norm", "gate_bias", "rel_bias_table", "ssm_a_re", "ssm_a_im", "ssm_log_dt",
         "ssm_b_re", "ssm_b_im", "ssm_c_re", "ssm_c_im", "ssm_d", "ffn2_norm", "final_norm"]
ORDER = ["ffn1_norm", "ffn1_w_gate", "ffn1_w_up", "ffn1_w_down", "mix_norm", "w_in", "gate_bias", "rel_bias_table",
         "ssm_a_re", "ssm_a_im", "ssm_log_dt", "ssm_b_re", "ssm_b_im", "ssm_c_re", "ssm_c_im", "ssm_d",
         "ssm_w_glu", "w_attn_branch", "w_ssm_branch", "w_out", "ffn2_norm", "ffn2_w_gate", "ffn2_w_up",
         "ffn2_w_down", "final_norm"]


def _pack_rows(arrays):
    return jnp.concatenate([a.reshape(-1, PACK_W) for a in arrays], axis=0)


def _unpack_rows(packed, shapes):
    out, r0 = [], 0
    lead = packed.shape[:-2]
    for shp in shapes:
        n = math.prod(shp) // PACK_W
        out.append(packed[..., r0:r0 + n, :].reshape(lead + tuple(shp)))
        r0 += n
    return out


def _pack_small(arrays):
    rows = []
    for a in arrays:
        flat = a.reshape(-1).astype(F32)
        pad = (-flat.shape[0]) % 128
        rows.append(jnp.pad(flat, (0, pad)).reshape(-1, 128))
    packed = jnp.concatenate(rows, axis=0)
    return jnp.pad(packed, ((0, (-packed.shape[0]) % 8), (0, 0)))


def _unpack_small(packed, shapes):
    out, r0 = [], 0
    for shp in shapes:
        n = math.prod(shp)
        rows = -(-n // 128)
        out.append(packed[r0:r0 + rows].reshape(-1)[:n].reshape(shp))
        r0 += rows
    return out


def _local_step(x, target, w, small):
    L = x.shape[0]
    row = lambda v: v.reshape(1, -1)

    a_re, a_im = small["ssm_a_re"].reshape(1, NS), small["ssm_a_im"].reshape(1, NS)
    ldt = jnp.repeat(small["ssm_log_dt"].reshape(SSM_GROUPS), SSM_STATE).reshape(1, NS)
    to_cn = lambda b: b.reshape(SSM_GROUPS, SSM_STATE, SSM_GROUP).transpose(2, 0, 1).reshape(SSM_GROUP, NS)
    c_to_cn = lambda c: c.reshape(SSM_GROUPS, SSM_GROUP, SSM_STATE).transpose(1, 0, 2).reshape(SSM_GROUP, NS)
    b_re, b_im = to_cn(small["ssm_b_re"]), to_cn(small["ssm_b_im"])
    c_re, c_im = c_to_cn(small["ssm_c_re"]), c_to_cn(small["ssm_c_im"])
    d_skip = row(small["ssm_d"])
    pw, pwr, bd, cdt = _disc_fwd(a_re, a_im, ldt, b_re, b_im, c_re, c_im)

    onehot = _bucket_onehot()
    table_t = small["rel_bias_table"].T.reshape(3, HEADS_PER_GROUP, N_BUCKETS)
    table_t = jnp.pad(table_t, ((0, 0), (0, 8 - HEADS_PER_GROUP), (0, 0)))
    bias = _bias_expand(table_t, onehot)[:, :HEADS_PER_GROUP].reshape(3, HEADS_PER_GROUP, ATTN_BLOCK, 2 * ATTN_BLOCK)

    n1, nm, n2, nf = row(small["ffn1_norm"]), row(small["mix_norm"]), row(small["ffn2_norm"]), row(small["final_norm"])
    gate_bias = row(small["gate_bias"])

    x1, a1, b1 = _ffn_fwd(x, n1, w["ffn1_w_gate"], w["ffn1_w_up"], w["ffn1_w_down"], "ffn1_fwd")
    q, k, v, u, gates = _mix_in_fwd(x1, nm, w["w_in"], gate_bias)
    o_g, lse_g = [], []
    for grp in range(3):
        o, lse = _attn_fwd(q, k, v, bias[grp], grp, f"attn_fwd_{grp}")
        o_g.append(o)
        lse_g.append(lse)
    y, s = _ssm_fwd(u, bd, cdt, d_skip, pw)
    x2, o_attn, lse_tot = _mix_out_fwd(x1, o_g, lse_g, y, gates, w["w_attn_branch"], w["ssm_w_glu"],
                                       w["w_ssm_branch"], w["w_out"])
    x3, a2, b2 = _ffn_fwd(x2, n2, w["ffn2_w_gate"], w["ffn2_w_up"], w["ffn2_w_down"], "ffn2_fwd")
    loss_blk, dx3, d_nf = _loss_fwd_bwd(x3, nf, target)

    gw, gs = {}, {}
    gs["final_norm"] = d_nf

    dx2, da, db, sact, h, d_out, gs["ffn2_norm"] = _ffn_bwd(dx3, x2, n2, a2, b2, w["ffn2_w_gate"], w["ffn2_w_up"],
                                                            w["ffn2_w_down"], "ffn2_bwd")
    gw["ffn2_w_gate"] = _matmul_tn(h[None], da, "ffn2_dw_gate")
    gw["ffn2_w_up"] = _matmul_tn(h[None], db, "ffn2_dw_up")
    gw["ffn2_w_down"] = _matmul_tn(sact, d_out[None], "ffn2_dw_down")

    head_sum = (jnp.arange(GROUP_WIDTH)[:, None] // HEAD_DIM == jnp.arange(GROUP_WIDTH)[None, :] // HEAD_DIM).astype(F32)
    (d_o, delta, dy, dgp, mix, dya, dys, ys2, gel, dglu, gs["gate_bias"]) = _mix_out_bwd(
        dx2, o_attn, y, gates, w["w_attn_branch"], w["ssm_w_glu"], w["w_ssm_branch"], w["w_out"], head_sum)
    gw["w_out"] = _matmul_tn(mix[None], dx2[None], "dw_out")[0]
    gw["w_attn_branch"] = _matmul_tn(o_attn[None], dya[None], "dw_attn_branch")[0]
    gw["w_ssm_branch"] = _matmul_tn(ys2[None], dys[None], "dw_ssm_branch")[0]
    gw["ssm_w_glu"] = _matmul_tn(gel[None], dglu[None], "dw_glu")[0]

    dqs, dks, dvs, dsums = [], [], [], []
    for grp in range(3):
        dq, dk, dv, dsum = _attn_bwd(q, k, v, d_o, lse_tot, delta, bias[grp], grp, f"attn_bwd_{grp}")
        dqs.append(dq)
        dks.append(dk)
        dvs.append(dv)
        dsums.append(dsum.reshape(HEADS_PER_GROUP, -1))
    dsum_all = jnp.pad(jnp.stack(dsums), ((0, 0), (0, 8 - HEADS_PER_GROUP), (0, 0)))
    d_table = _bias_reduce(dsum_all, onehot)[:, :HEADS_PER_GROUP]
    gs["rel_bias_table"] = d_table.reshape(3 * HEADS_PER_GROUP, N_BUCKETS).T

    du, gs["ssm_d"], d_bd, d_cdt, d_ab = _ssm_bwd(dy, u, s, bd, cdt, d_skip, pwr)
    group_sum = (jnp.arange(NS)[:, None] // SSM_STATE == jnp.arange(128)[None, :]).astype(F32)
    d_are, d_aim, d_ldt, d_bre, d_bim, d_cre, d_cim = _disc_bwd(a_re, a_im, ldt, b_re, b_im, d_bd, d_cdt, d_ab, group_sum)
    gs["ssm_a_re"], gs["ssm_a_im"] = d_are, d_aim
    gs["ssm_log_dt"] = d_ldt[0, :SSM_GROUPS]
    from_cn = lambda t: t.reshape(SSM_GROUP, SSM_GROUPS, SSM_STATE).transpose(1, 2, 0)
    c_from_cn = lambda t: t.reshape(SSM_GROUP, SSM_GROUPS, SSM_STATE).transpose(1, 0, 2)
    gs["ssm_b_re"], gs["ssm_b_im"] = from_cn(d_bre), from_cn(d_bim)
    gs["ssm_c_re"], gs["ssm_c_im"] = c_from_cn(d_cre), c_from_cn(d_cim)

    dz = jnp.concatenate(dqs + dks + dvs + [du.astype(BF16), dgp], axis=1)
    dx1, hm, gs["mix_norm"] = _mix_in_bwd(dx2, x1, nm, dz, w["w_in"])
    gw["w_in"] = _matmul_tn(hm[None], dz[None], "dw_in")[0]

    dx0, da, db, sact, h, d_out, gs["ffn1_norm"] = _ffn_bwd(dx1, x, n1, a1, b1, w["ffn1_w_gate"], w["ffn1_w_up"],
                                                            w["ffn1_w_down"], "ffn1_bwd")
    gw["ffn1_w_gate"] = _matmul_tn(h[None], da, "ffn1_dw_gate")
    gw["ffn1_w_up"] = _matmul_tn(h[None], db, "ffn1_dw_up")
    gw["ffn1_w_down"] = _matmul_tn(sact, d_out[None], "ffn1_dw_down")
    return loss_blk, dx0, gw, gs


def _split_cols(g):
    K, N = g.shape
    return g.reshape(K, N_SHARD, N // N_SHARD).transpose(1, 0, 2)


def _join_cols(w):
    S, K, n = w.shape
    return w.transpose(1, 0, 2).reshape(K, S * n)


COL_SHARDED = ("w_in", "ssm_w_glu", "w_attn_branch", "w_ssm_branch")


def kernel(x, ffn1_norm, ffn1_w_gate, ffn1_w_up, ffn1_w_down, mix_norm, w_in, gate_bias, rel_bias_table, ssm_a_re, ssm_a_im, ssm_log_dt, ssm_b_re, ssm_b_im, ssm_c_re, ssm_c_im, ssm_d, ssm_w_glu, w_attn_branch, w_ssm_branch, w_out, ffn2_norm, ffn2_w_gate, ffn2_w_up, ffn2_w_down, final_norm, loss_target, m_ffn1_norm, m_ffn1_w_gate, m_ffn1_w_up, m_ffn1_w_down, m_mix_norm, m_w_in, m_gate_bias, m_rel_bias_table, m_ssm_a_re, m_ssm_a_im, m_ssm_log_dt, m_ssm_b_re, m_ssm_b_im, m_ssm_c_re, m_ssm_c_im, m_ssm_d, m_ssm_w_glu, m_w_attn_branch, m_w_ssm_branch, m_w_out, m_ffn2_norm, m_ffn2_w_gate, m_ffn2_w_up, m_ffn2_w_down, m_final_norm, v_ffn1_norm, v_ffn1_w_gate, v_ffn1_w_up, v_ffn1_w_down, v_mix_norm, v_w_in, v_gate_bias, v_rel_bias_table, v_ssm_a_re, v_ssm_a_im, v_ssm_log_dt, v_ssm_b_re, v_ssm_b_im, v_ssm_c_re, v_ssm_c_im, v_ssm_d, v_ssm_w_glu, v_w_attn_branch, v_w_ssm_branch, v_w_out, v_ffn2_norm, v_ffn2_w_gate, v_ffn2_w_up, v_ffn2_w_down, v_final_norm):
    args = dict(locals())
    weights = {n: args[n] for n in ORDER}
    moms = {n: args["m_" + n] for n in ORDER}
    vels = {n: args["v_" + n] for n in ORDER}

    shard2d = {n: weights[n].reshape(weights[n].shape[-2:]) for n in BIG}
    packed = _pack_rows([shard2d[n].astype(BF16) for n in BIG])
    gathered = _gather_weights(packed)
    full = dict(zip(BIG, _unpack_rows(gathered, [shard2d[n].shape for n in BIG])))
    for n in COL_SHARDED:
        full[n] = _join_cols(full[n])
    full["w_out"] = full["w_out"].reshape(D_MODEL, D_MODEL)

    small = {n: weights[n] for n in SMALL}
    loss_blk, grad_x, gw, gs = _local_step(x[0], loss_target[0], full, small)

    for n in COL_SHARDED:
        gw[n] = _split_cols(gw[n])
    gw["w_out"] = gw["w_out"].reshape(N_SHARD, D_MODEL // N_SHARD, D_MODEL)
    g_packed = jnp.concatenate([gw[n].reshape(N_SHARD, -1, PACK_W) for n in BIG], axis=1)
    g_sum = _reduce_scatter(g_packed)
    grads = dict(zip(BIG, _unpack_rows(g_sum, [shard2d[n].shape for n in BIG])))

    small_shapes = [weights[n].shape for n in SMALL]
    mine = _pack_small([gs[n] for n in SMALL] + [loss_blk[0:1, :]])
    total = _sum_slots(_gather_small(mine), "sum_small")
    small_grads = _unpack_small(total, small_shapes + [(128,)])
    loss = small_grads[-1][0]
    for n, g in zip(SMALL, small_grads[:-1]):
        grads[n] = g

    delta, new_m, new_v = {}, {}, {}
    for n in BIG:
        d, m, v = _adamw(shard2d[n], grads[n], moms[n].reshape(shard2d[n].shape), vels[n].reshape(shard2d[n].shape),
                         "adamw_" + n)
        shp = weights[n].shape
        delta[n], new_m[n], new_v[n] = d.reshape(shp), m.reshape(shp), v.reshape(shp)
        grads[n] = grads[n].reshape(shp)
    d, m, v = _adamw(_pack_small([weights[n] for n in SMALL]), _pack_small([grads[n] for n in SMALL]),
                     _pack_small([moms[n] for n in SMALL]), _pack_small([vels[n] for n in SMALL]), "adamw_small")
    for n, dd, mm, vv in zip(SMALL, _unpack_small(d, small_shapes), _unpack_small(m, small_shapes),
                             _unpack_small(v, small_shapes)):
        delta[n], new_m[n], new_v[n] = dd, mm, vv

    return (loss, grad_x[None], *[grads[n] for n in ORDER], *[delta[n] for n in ORDER],
            *[new_m[n] for n in ORDER], *[new_v[n] for n in ORDER])
```

```python
import functools
import math

import jax
import jax.numpy as jnp
from jax import lax
from jax.experimental import pallas as pl
from jax.experimental.pallas import tpu as pltpu

F32 = jnp.float32
BF16 = jnp.bfloat16
MESH = pl.DeviceIdType.MESH

D_MODEL = 1024
D_FF = 2816
HEAD_DIM = 64
HEADS_PER_GROUP = 4
DILATIONS = (1, 4, 16)
WINDOW_STEPS = 128
ATTN_BLOCK = 128
GROUP_WIDTH = HEADS_PER_GROUP * HEAD_DIM
ATTN_WIDTH = 3 * GROUP_WIDTH
N_BUCKETS = 32
MAX_DISTANCE = 2048
NEG_INF = -1e30
SSM_WIDTH = 512
SSM_GROUP = 16
SSM_GROUPS = 32
SSM_STATE = 64
NS = SSM_GROUPS * SSM_STATE
EPS = 1e-6
IN_WIDTH = 3 * ATTN_WIDTH + SSM_WIDTH + 2 * D_MODEL
Q_SCALE = HEAD_DIM ** -0.5
N_SHARD = 4
FF_SHARD = D_FF // N_SHARD
ADAM_LR, ADAM_B1, ADAM_B2, ADAM_EPS, ADAM_WD, ADAM_STEP = 0.001, 0.9, 0.999, 1e-08, 0.01, 10

LANES = 128
VMEM_LIMIT = 56 * 1024 * 1024
ROW_TILE = 512
FFN_BWD_TILE = 256
SSM_CHUNK = 256
SCAN_LANES = 512


def _params(**kw):
    return pltpu.CompilerParams(vmem_limit_bytes=VMEM_LIMIT, **kw)


def _dot(a, b):
    return jnp.dot(a, b, preferred_element_type=F32)


def _dot_nt(a, b):
    return lax.dot_general(a, b, (((1,), (1,)), ((), ())), preferred_element_type=F32)


def _dot_tn(a, b):
    return lax.dot_general(a, b, (((0,), (0,)), ((), ())), preferred_element_type=F32)


def _dot_exact(a, b):
    return jnp.dot(a, b, preferred_element_type=F32, precision=lax.Precision.HIGHEST)


def _dot_nt_exact(a, b):
    return lax.dot_general(a, b, (((1,), (1,)), ((), ())), preferred_element_type=F32,
                           precision=lax.Precision.HIGHEST)


def _rms(x):
    r = lax.rsqrt(jnp.mean(x * x, axis=-1, keepdims=True) + EPS)
    return r, x * r


def _rms_bwd(dh, g, r, xhat):
    dxh = dh * g
    return r * (dxh - xhat * jnp.mean(dxh * xhat, axis=-1, keepdims=True))


def _sigmoid(x):
    return 1.0 / (1.0 + jnp.exp(-x))


_GELU_C = math.sqrt(2.0 / math.pi)


def _gelu(x):
    return 0.5 * x * (1.0 + jnp.tanh(_GELU_C * (x + 0.044715 * x * x * x)))


def _gelu_grad(x):
    t = jnp.tanh(_GELU_C * (x + 0.044715 * x * x * x))
    return 0.5 * (1.0 + t) + 0.5 * x * (1.0 - t * t) * _GELU_C * (1.0 + 3 * 0.044715 * x * x)


def _whole():
    return pl.BlockSpec(memory_space=pltpu.VMEM)


def _row_tile(rows, cap):
    if rows <= cap:
        return rows
    return max(t for t in range(8, cap + 1, 8) if rows % t == 0)


def _rows(tm, w):
    return pl.BlockSpec((tm, w), lambda i: (i, 0))


def _acc_row(w):
    return pl.BlockSpec((1, w), lambda i: (0, 0))


def _ffn_fwd(x, g, wg, wu, wd, name):
    L = x.shape[0]
    tm = min(ROW_TILE, L)

    def body(x_ref, g_ref, wg_ref, wu_ref, wd_ref, xo_ref, a_ref, b_ref):
        xv = x_ref[...]
        r, xhat = _rms(xv)
        h = (xhat * g_ref[...]).astype(BF16)
        acc = jnp.zeros((tm, D_MODEL), F32)
        for j in range(N_SHARD):
            a = _dot(h, wg_ref[j])
            b = _dot(h, wu_ref[j])
            a_ref[j] = a.astype(BF16)
            b_ref[j] = b.astype(BF16)
            s = (a * _sigmoid(a) * b).astype(BF16)
            acc = acc + _dot(s, wd_ref[j])
        xo_ref[...] = xv + 0.5 * acc

    act = pl.BlockSpec((N_SHARD, tm, FF_SHARD), lambda i: (0, i, 0))
    return pl.pallas_call(
        body, name=name, grid=(L // tm,),
        in_specs=[_rows(tm, D_MODEL), _whole(), _whole(), _whole(), _whole()],
        out_specs=[_rows(tm, D_MODEL), act, act],
        out_shape=[jax.ShapeDtypeStruct((L, D_MODEL), F32),
                   jax.ShapeDtypeStruct((N_SHARD, L, FF_SHARD), BF16),
                   jax.ShapeDtypeStruct((N_SHARD, L, FF_SHARD), BF16)],
        compiler_params=_params(),
    )(x, g, wg, wu, wd)


def _ffn_bwd(dxo, x, g, a, b, wg, wu, wd, name):
    L = x.shape[0]
    tm = min(FFN_BWD_TILE, L)

    def body(dxo_ref, x_ref, g_ref, a_ref, b_ref, wg_ref, wu_ref, wd_ref,
             dxi_ref, da_ref, db_ref, s_ref, h_ref, do_ref, dg_ref):
        i = pl.program_id(0)
        xv = x_ref[...]
        gv = g_ref[...]
        r, xhat = _rms(xv)
        h_ref[...] = (xhat * gv).astype(BF16)
        dxo_v = dxo_ref[...]
        d_out = (0.5 * dxo_v).astype(BF16)
        do_ref[...] = d_out
        dh = jnp.zeros((tm, D_MODEL), F32)
        for j in range(N_SHARD):
            av = a_ref[j].astype(F32)
            bv = b_ref[j].astype(F32)
            sg = _sigmoid(av)
            sl = av * sg
            ds = _dot_nt(d_out, wd_ref[j])
            dbv = (ds * sl).astype(BF16)
            dav = (ds * bv * (sg * (1.0 + av * (1.0 - sg)))).astype(BF16)
            da_ref[j] = dav
            db_ref[j] = dbv
            s_ref[j] = (sl * bv).astype(BF16)
            dh = dh + _dot_nt(dav, wg_ref[j]) + _dot_nt(dbv, wu_ref[j])

        @pl.when(i == 0)
        def _():
            dg_ref[...] = jnp.zeros_like(dg_ref)

        dg_ref[...] += jnp.sum(dh * xhat, axis=0, keepdims=True)
        dxi_ref[...] = dxo_v + _rms_bwd(dh, gv, r, xhat)

    act = pl.BlockSpec((N_SHARD, tm, FF_SHARD), lambda i: (0, i, 0))
    act_shape = jax.ShapeDtypeStruct((N_SHARD, L, FF_SHARD), BF16)
    return pl.pallas_call(
        body, name=name, grid=(L // tm,),
        in_specs=[_rows(tm, D_MODEL), _rows(tm, D_MODEL), _whole(), act, act, _whole(), _whole(), _whole()],
        out_specs=[_rows(tm, D_MODEL), act, act, act, _rows(tm, D_MODEL), _rows(tm, D_MODEL), _acc_row(D_MODEL)],
        out_shape=[jax.ShapeDtypeStruct((L, D_MODEL), F32), act_shape, act_shape, act_shape,
                   jax.ShapeDtypeStruct((L, D_MODEL), BF16), jax.ShapeDtypeStruct((L, D_MODEL), BF16),
                   jax.ShapeDtypeStruct((1, D_MODEL), F32)],
        compiler_params=_params(),
    )(dxo, x, g, a, b, wg, wu, wd)


def _matmul_tn(a, b, name):
    ja, L, K = a.shape
    jb, _, N = b.shape
    J = max(ja, jb)
    tm = min(ROW_TILE, L)

    def body(a_ref, b_ref, o_ref):
        @pl.when(pl.program_id(1) == 0)
        def _():
            o_ref[...] = jnp.zeros_like(o_ref)

        o_ref[...] += _dot_tn(a_ref[...].astype(BF16), b_ref[...].astype(BF16))

    return pl.pallas_call(
        body, name=name, grid=(J, L // tm),
        in_specs=[pl.BlockSpec((None, tm, K), (lambda j, i: (j, i, 0)) if ja > 1 else (lambda j, i: (0, i, 0))),
                  pl.BlockSpec((None, tm, N), (lambda j, i: (j, i, 0)) if jb > 1 else (lambda j, i: (0, i, 0)))],
        out_specs=pl.BlockSpec((None, K, N), lambda j, i: (j, 0, 0)),
        out_shape=jax.ShapeDtypeStruct((J, K, N), F32),
        compiler_params=_params(),
    )(a, b)


def _loss_fwd_bwd(x, g, target):
    L = x.shape[0]
    tm = min(ROW_TILE, L)

    def body(x_ref, g_ref, t_ref, loss_ref, dx_ref, dg_ref):
        i = pl.program_id(0)
        xv = x_ref[...]
        gv = g_ref[...]
        r, xhat = _rms(xv)
        err = xhat * gv - t_ref[...]
        part = 0.5 * jnp.sum(jnp.sum(err * err, axis=1, keepdims=True) * (1.0 / D_MODEL), axis=0, keepdims=True)
        dy = err * (1.0 / D_MODEL)

        @pl.when(i == 0)
        def _():
            dg_ref[...] = jnp.zeros_like(dg_ref)
            loss_ref[...] = jnp.zeros_like(loss_ref)

        loss_ref[...] += jnp.broadcast_to(part, loss_ref.shape)
        dg_ref[...] += jnp.sum(dy * xhat, axis=0, keepdims=True)
        dx_ref[...] = _rms_bwd(dy, gv, r, xhat)

    return pl.pallas_call(
        body, name="loss_fwd_bwd", grid=(L // tm,),
        in_specs=[_rows(tm, D_MODEL), _whole(), _rows(tm, D_MODEL)],
        out_specs=[pl.BlockSpec((8, 128), lambda i: (0, 0)), _rows(tm, D_MODEL), _acc_row(D_MODEL)],
        out_shape=[jax.ShapeDtypeStruct((8, 128), F32), jax.ShapeDtypeStruct((L, D_MODEL), F32),
                   jax.ShapeDtypeStruct((1, D_MODEL), F32)],
        compiler_params=_params(),
    )(x, g, target)


_C_K = ATTN_WIDTH
_C_V = 2 * ATTN_WIDTH
_C_U = 3 * ATTN_WIDTH
_C_G = _C_U + SSM_WIDTH


def _residue_spec(d, tm):
    return pl.BlockSpec((d, tm // d, GROUP_WIDTH), lambda i: (0, i, 0))


def _residue_shape(d, L, dtype):
    return jax.ShapeDtypeStruct((d, L // d, GROUP_WIDTH), dtype)


def _residue_scratch(tm):
    return pltpu.VMEM((GROUP_WIDTH // LANES, tm, LANES), F32)


def _to_residues(val, out_ref, scr, d):
    if d == 1:
        out_ref[0] = val.astype(out_ref.dtype)
        return
    tm = val.shape[0]
    for half in range(GROUP_WIDTH // LANES):
        cols = slice(half * LANES, (half + 1) * LANES)
        scr[half] = val[:, cols]
        for r in range(d):
            out_ref[r, :, cols] = scr[half, pl.ds(r, tm // d, stride=d), :].astype(out_ref.dtype)


def _from_residues(ref, scr, d):
    if d == 1:
        return ref[0].astype(F32)
    rows = ref.shape[1]
    for half in range(GROUP_WIDTH // LANES):
        cols = slice(half * LANES, (half + 1) * LANES)
        for r in range(d):
            scr[half, pl.ds(r, rows, stride=d), :] = ref[r, :, cols].astype(F32)
    return jnp.concatenate([scr[half] for half in range(GROUP_WIDTH // LANES)], axis=1)


def _mix_in_fwd(x, g, w_in, gate_bias):
    L = x.shape[0]
    tm = min(ROW_TILE, L)

    def body(x_ref, g_ref, w_ref, gb_ref, *refs):
        qkv_refs, (u_ref, gate_ref, scr) = refs[:9], refs[9:]
        r, xhat = _rms(x_ref[...])
        h = (xhat * g_ref[...]).astype(BF16)
        for part, (c0, scale) in enumerate(((0, Q_SCALE), (_C_K, 1.0), (_C_V, 1.0))):
            z = _dot(h, w_ref[:, c0:c0 + ATTN_WIDTH]) * scale
            for grp, d in enumerate(DILATIONS):
                _to_residues(z[:, grp * GROUP_WIDTH:(grp + 1) * GROUP_WIDTH], qkv_refs[3 * part + grp], scr, d)
        u_ref[...] = _dot(h, w_ref[:, _C_U:_C_G])
        gate_ref[...] = _sigmoid(_dot(h, w_ref[:, _C_G:IN_WIDTH]) + gb_ref[...])

    return pl.pallas_call(
        body, name="mix_in_fwd", grid=(L // tm,),
        in_specs=[_rows(tm, D_MODEL), _whole(), _whole(), _whole()],
        out_specs=[_residue_spec(d, tm) for d in DILATIONS] * 3 + [_rows(tm, SSM_WIDTH), _rows(tm, 2 * D_MODEL)],
        out_shape=[_residue_shape(d, L, BF16) for d in DILATIONS] * 3
        + [jax.ShapeDtypeStruct((L, SSM_WIDTH), F32), jax.ShapeDtypeStruct((L, 2 * D_MODEL), F32)],
        scratch_shapes=[_residue_scratch(tm)],
        compiler_params=_params(),
    )(x, g, w_in, gate_bias)


def _mix_in_bwd(dx2, x, g, dqkv, du, dgp, w_in):
    L = x.shape[0]
    tm = min(ROW_TILE, L)

    def body(dx2_ref, x_ref, g_ref, *refs):
        piece_refs = refs[:9]
        du_ref, dgp_ref, w_ref, dx1_ref, h_ref, dz_ref, dg_ref, scr = refs[9:]
        i = pl.program_id(0)
        gv = g_ref[...]
        r, xhat = _rms(x_ref[...])
        h_ref[...] = (xhat * gv).astype(BF16)
        for part in range(3):
            for grp, d in enumerate(DILATIONS):
                c0 = part * ATTN_WIDTH + grp * GROUP_WIDTH
                dz_ref[:, c0:c0 + GROUP_WIDTH] = _from_residues(piece_refs[3 * part + grp], scr, d).astype(BF16)
        dz_ref[:, _C_U:_C_G] = du_ref[...].astype(BF16)
        dz_ref[:, _C_G:IN_WIDTH] = dgp_ref[...]
        dh = _dot_nt(dz_ref[...], w_ref[...])

        @pl.when(i == 0)
        def _():
            dg_ref[...] = jnp.zeros_like(dg_ref)

        dg_ref[...] += jnp.sum(dh * xhat, axis=0, keepdims=True)
        dx1_ref[...] = dx2_ref[...] + _rms_bwd(dh, gv, r, xhat)

    return pl.pallas_call(
        body, name="mix_in_bwd", grid=(L // tm,),
        in_specs=[_rows(tm, D_MODEL), _rows(tm, D_MODEL), _whole()] + [_residue_spec(d, tm) for d in DILATIONS] * 3
        + [_rows(tm, SSM_WIDTH), _rows(tm, 2 * D_MODEL), _whole()],
        out_specs=[_rows(tm, D_MODEL), _rows(tm, D_MODEL), _rows(tm, IN_WIDTH), _acc_row(D_MODEL)],
        out_shape=[jax.ShapeDtypeStruct((L, D_MODEL), F32), jax.ShapeDtypeStruct((L, D_MODEL), BF16),
                   jax.ShapeDtypeStruct((L, IN_WIDTH), BF16), jax.ShapeDtypeStruct((1, D_MODEL), F32)],
        scratch_shapes=[_residue_scratch(tm)],
        compiler_params=_params(),
    )(dx2, x, g, *dqkv, du, dgp, w_in)


def _bucket_onehot():
    qi = jnp.arange(ATTN_BLOCK)[:, None]
    kj = jnp.arange(2 * ATTN_BLOCK)[None, :]
    steps = jnp.maximum(qi + ATTN_BLOCK - kj, 0)
    max_exact = N_BUCKETS // 2
    out = []
    for d in DILATIONS:
        dist = steps * d
        df = jnp.maximum(dist, 1).astype(F32)
        large = max_exact + (jnp.log(df / max_exact) / math.log(MAX_DISTANCE / max_exact)
                             * (N_BUCKETS - max_exact)).astype(jnp.int32)
        large = jnp.minimum(large, N_BUCKETS - 1)
        bucket = jnp.where(dist < max_exact, dist, large).reshape(-1)
        out.append((bucket[None, :] == jnp.arange(N_BUCKETS)[:, None]).astype(F32))
    return jnp.stack(out)


def _bias_expand(table_t, onehot):
    n = onehot.shape[-1]

    def body(t_ref, oh_ref, o_ref):
        o_ref[...] = _dot_exact(t_ref[...], oh_ref[...])

    return pl.pallas_call(
        body, name="bias_expand", grid=(3,),
        in_specs=[pl.BlockSpec((None, 8, N_BUCKETS), lambda g: (g, 0, 0)),
                  pl.BlockSpec((None, N_BUCKETS, n), lambda g: (g, 0, 0))],
        out_specs=pl.BlockSpec((None, 8, n), lambda g: (g, 0, 0)),
        out_shape=jax.ShapeDtypeStruct((3, 8, n), F32),
        compiler_params=_params(),
    )(table_t, onehot)


def _bias_reduce(dsum, onehot):
    n = onehot.shape[-1]

    def body(d_ref, oh_ref, o_ref):
        o_ref[...] = _dot_nt_exact(d_ref[...], oh_ref[...])

    return pl.pallas_call(
        body, name="bias_reduce", grid=(3,),
        in_specs=[pl.BlockSpec((None, 8, n), lambda g: (g, 0, 0)),
                  pl.BlockSpec((None, N_BUCKETS, n), lambda g: (g, 0, 0))],
        out_specs=pl.BlockSpec((None, 8, N_BUCKETS), lambda g: (g, 0, 0)),
        out_shape=jax.ShapeDtypeStruct((3, 8, N_BUCKETS), F32),
        compiler_params=_params(),
    )(dsum, onehot)


def _attn_masks(n):
    qi = lax.broadcasted_iota(jnp.int32, (ATTN_BLOCK, 2 * ATTN_BLOCK), 0)
    kj = lax.broadcasted_iota(jnp.int32, (ATTN_BLOCK, 2 * ATTN_BLOCK), 1)
    steps = qi + ATTN_BLOCK - kj
    valid = (steps >= 0) & (steps <= WINDOW_STEPS) & ((n > 0) | (kj >= ATTN_BLOCK))
    head_of_col = lax.broadcasted_iota(jnp.int32, (ATTN_BLOCK, GROUP_WIDTH), 1) // HEAD_DIM
    return valid, head_of_col


def _attn_fwd(q, k, v, bias, name):
    d, M, _ = q.shape
    nb = M // ATTN_BLOCK

    def body(q_ref, kp_ref, kc_ref, vp_ref, vc_ref, bias_ref, o_ref, lse_ref):
        n = pl.program_id(1)
        valid, head_of_col = _attn_masks(n)
        qb = q_ref[...]
        kk = jnp.concatenate([kp_ref[...], kc_ref[...]], axis=0)
        vv = jnp.concatenate([vp_ref[...], vc_ref[...]], axis=0)
        o_acc = jnp.zeros((ATTN_BLOCK, GROUP_WIDTH), F32)
        lse_acc = jnp.zeros((ATTN_BLOCK, GROUP_WIDTH), F32)
        for hh in range(HEADS_PER_GROUP):
            hm = head_of_col == hh
            qh = jnp.where(hm, qb, jnp.zeros_like(qb))
            logits = jnp.where(valid, _dot_nt(qh, kk) + bias_ref[hh], NEG_INF)
            m = jnp.max(logits, axis=1, keepdims=True)
            p = jnp.exp(logits - m)
            den = jnp.sum(p, axis=1, keepdims=True)
            oh = _dot(p.astype(BF16), vv) / den
            o_acc = jnp.where(hm, oh, o_acc)
            lse_acc = jnp.where(hm, m + jnp.log(den), lse_acc)
        o_ref[...] = o_acc
        lse_ref[...] = lse_acc

    blk = (None, ATTN_BLOCK, GROUP_WIDTH)
    cur = pl.BlockSpec(blk, lambda r, n: (r, n, 0))
    prev = pl.BlockSpec(blk, lambda r, n: (r, jnp.maximum(n - 1, 0), 0))
    return pl.pallas_call(
        body, name=name, grid=(d, nb),
        in_specs=[cur, prev, cur, prev, cur, pl.BlockSpec((HEADS_PER_GROUP, ATTN_BLOCK, 2 * ATTN_BLOCK), lambda r, n: (0, 0, 0))],
        out_specs=[cur, cur],
        out_shape=[jax.ShapeDtypeStruct((d, M, GROUP_WIDTH), F32)] * 2,
        compiler_params=_params(),
    )(q, k, k, v, v, bias)


def _attn_bwd(q, k, v, do, lse, delta, bias, name):
    d, M, _ = q.shape
    nb = M // ATTN_BLOCK

    def body(q_ref, kp_ref, kc_ref, vp_ref, vc_ref, do_ref, lse_ref, dl_ref, bias_ref,
             dq_ref, dk_ref, dv_ref, dsum_ref, ck_ref, cv_ref):
        r = pl.program_id(0)
        n = pl.program_id(1)

        @pl.when((r == 0) & (n == 0))
        def _():
            dsum_ref[...] = jnp.zeros_like(dsum_ref)

        @pl.when(n == 0)
        def _():
            ck_ref[...] = jnp.zeros_like(ck_ref)
            cv_ref[...] = jnp.zeros_like(cv_ref)

        @pl.when(n < nb)
        def _():
            valid, head_of_col = _attn_masks(n)
            qb = q_ref[...]
            dob = do_ref[...]
            kk = jnp.concatenate([kp_ref[...], kc_ref[...]], axis=0)
            vv = jnp.concatenate([vp_ref[...], vc_ref[...]], axis=0)
            dq_acc = jnp.zeros((ATTN_BLOCK, GROUP_WIDTH), F32)
            dkk = jnp.zeros((2 * ATTN_BLOCK, GROUP_WIDTH), F32)
            dvv = jnp.zeros((2 * ATTN_BLOCK, GROUP_WIDTH), F32)
            for hh in range(HEADS_PER_GROUP):
                hm = head_of_col == hh
                c0 = hh * HEAD_DIM
                qh = jnp.where(hm, qb, jnp.zeros_like(qb))
                doh = jnp.where(hm, dob, jnp.zeros_like(dob))
                logits = jnp.where(valid, _dot_nt(qh, kk) + bias_ref[hh], NEG_INF)
                p = jnp.exp(logits - lse_ref[:, c0:c0 + 1])
                dp = _dot_nt(doh, vv)
                ds = p * (dp - dl_ref[:, c0:c0 + 1])
                dsum_ref[hh] += ds
                ds16 = ds.astype(BF16)
                dq_acc = jnp.where(hm, _dot(ds16, kk), dq_acc)
                dkk = dkk + _dot_tn(ds16, qh)
                dvv = dvv + _dot_tn(p.astype(BF16), doh)
            dq_ref[...] = (dq_acc * Q_SCALE).astype(BF16)
            dk_ref[...] = (ck_ref[...] + dkk[:ATTN_BLOCK]).astype(BF16)
            dv_ref[...] = (cv_ref[...] + dvv[:ATTN_BLOCK]).astype(BF16)
            ck_ref[...] = dkk[ATTN_BLOCK:]
            cv_ref[...] = dvv[ATTN_BLOCK:]

        @pl.when(n == nb)
        def _():
            dk_ref[...] = ck_ref[...].astype(BF16)
            dv_ref[...] = cv_ref[...].astype(BF16)

    def clamp(n):
        return jnp.minimum(n, nb - 1)

    blk = (None, ATTN_BLOCK, GROUP_WIDTH)
    cur = pl.BlockSpec(blk, lambda r, n: (r, clamp(n), 0))
    prev = pl.BlockSpec(blk, lambda r, n: (r, jnp.maximum(clamp(n) - 1, 0), 0))
    lag = pl.BlockSpec(blk, lambda r, n: (r, jnp.maximum(n - 1, 0), 0))
    full = pl.BlockSpec((HEADS_PER_GROUP, ATTN_BLOCK, 2 * ATTN_BLOCK), lambda r, n: (0, 0, 0))
    carry = pltpu.VMEM((ATTN_BLOCK, GROUP_WIDTH), F32)
    return pl.pallas_call(
        body, name=name, grid=(d, nb + 1),
        in_specs=[cur, prev, cur, prev, cur, cur, cur, cur, full],
        out_specs=[cur, lag, lag, full],
        out_shape=[jax.ShapeDtypeStruct((d, M, GROUP_WIDTH), BF16)] * 3
        + [jax.ShapeDtypeStruct((HEADS_PER_GROUP, ATTN_BLOCK, 2 * ATTN_BLOCK), F32)],
        scratch_shapes=[carry, carry],
        compiler_params=_params(),
    )(q, k, k, v, v, do, lse, delta, bias)


def _disc_math(a_re, a_im, ldt, b_re, b_im):
    dt = jnp.exp(ldt)
    mag = jnp.exp(a_re * dt)
    ab_re = mag * jnp.cos(a_im * dt)
    ab_im = mag * jnp.sin(a_im * dt)
    den = a_re * a_re + a_im * a_im
    xr = ab_re - 1.0
    coef_re = (xr * a_re + ab_im * a_im) / den
    coef_im = (ab_im * a_re - xr * a_im) / den
    return ab_re, ab_im, coef_re * b_re - coef_im * b_im, coef_re * b_im + coef_im * b_re


def _block_diag_mask():
    row_g = lax.broadcasted_iota(jnp.int32, (SSM_WIDTH, 2 * NS), 0) // SSM_GROUP
    col = lax.broadcasted_iota(jnp.int32, (SSM_WIDTH, 2 * NS), 1)
    col_g = jnp.where(col >= NS, col - NS, col) // SSM_STATE
    return row_g == col_g


def _disc_fwd(a_re, a_im, ldt, b_re, b_im, c_re, c_im):
    def body(are_ref, aim_ref, ldt_ref, bre_ref, bim_ref, cre_ref, cim_ref, pw_ref, pwr_ref, bd_ref, cdt_ref):
        ab_re, ab_im, bb_re, bb_im = _disc_math(are_ref[...], aim_ref[...], ldt_ref[...], bre_ref[...], bim_ref[...])
        row = lax.broadcasted_iota(jnp.int32, (8, NS), 0)
        pr, pi = ab_re, ab_im
        t_re = jnp.zeros((8, NS), F32)
        t_im = jnp.zeros((8, NS), F32)
        u_re = jnp.zeros((8, NS), F32)
        u_im = jnp.zeros((8, NS), F32)
        for j in range(8):
            t_re = jnp.where(row == j, pr, t_re)
            t_im = jnp.where(row == j, pi, t_im)
            u_re = jnp.where(row == 7 - j, pr, u_re)
            u_im = jnp.where(row == 7 - j, pi, u_im)
            pr, pi = pr * ab_re - pi * ab_im, pr * ab_im + pi * ab_re
        pw_ref[0] = t_re
        pw_ref[1] = t_im
        pwr_ref[0] = u_re
        pwr_ref[1] = u_im
        mask = _block_diag_mask()
        zero = jnp.zeros((SSM_WIDTH, 2 * NS), F32)
        bfull = jnp.concatenate([jnp.concatenate([bb_re] * SSM_GROUPS, axis=0),
                                 jnp.concatenate([bb_im] * SSM_GROUPS, axis=0)], axis=1)
        bd_ref[...] = jnp.where(mask, bfull, zero).astype(BF16)
        cfull = jnp.concatenate([jnp.concatenate([cre_ref[...]] * SSM_GROUPS, axis=0),
                                 jnp.concatenate([-cim_ref[...]] * SSM_GROUPS, axis=0)], axis=1)
        cdt_ref[...] = jnp.where(mask, cfull, zero).astype(BF16)

    return pl.pallas_call(
        body, name="s5_disc_fwd",
        in_specs=[_whole()] * 7, out_specs=[_whole()] * 4,
        out_shape=[jax.ShapeDtypeStruct((2, 8, NS), F32), jax.ShapeDtypeStruct((2, 8, NS), F32),
                   jax.ShapeDtypeStruct((SSM_WIDTH, 2 * NS), BF16), jax.ShapeDtypeStruct((SSM_WIDTH, 2 * NS), BF16)],
        compiler_params=_params(),
    )(a_re, a_im, ldt, b_re, b_im, c_re, c_im)


def _disc_bwd(a_re, a_im, ldt, b_re, b_im, d_bd, d_cdt, d_ab, group_sum):
    def body(are_ref, aim_ref, ldt_ref, bre_ref, bim_ref, dbd_ref, dcdt_ref, dab_ref, gs_ref,
             dare_ref, daim_ref, dldt_ref, dbre_ref, dbim_ref, dcre_ref, dcim_ref):
        col = lax.broadcasted_iota(jnp.int32, (SSM_GROUP, 2 * NS), 1)
        col_g = jnp.where(col >= NS, col - NS, col) // SSM_STATE
        acc_b = jnp.zeros((SSM_GROUP, 2 * NS), F32)
        acc_c = jnp.zeros((SSM_GROUP, 2 * NS), F32)
        for g in range(SSM_GROUPS):
            rows = slice(g * SSM_GROUP, (g + 1) * SSM_GROUP)
            acc_b = acc_b + jnp.where(col_g == g, dbd_ref[rows, :], 0.0)
            acc_c = acc_c + jnp.where(col_g == g, dcdt_ref[rows, :], 0.0)
        dcre_ref[...] = acc_c[:, :NS]
        dcim_ref[...] = -acc_c[:, NS:]
        dab_re = jnp.sum(dab_ref[0], axis=0, keepdims=True)
        dab_im = jnp.sum(dab_ref[1], axis=0, keepdims=True)
        _, vjp = jax.vjp(_disc_math, are_ref[...], aim_ref[...], ldt_ref[...], bre_ref[...], bim_ref[...])
        d_are, d_aim, d_ldt, d_bre, d_bim = vjp((dab_re, dab_im, acc_b[:, :NS], acc_b[:, NS:]))
        dare_ref[...] = d_are
        daim_ref[...] = d_aim
        dbre_ref[...] = d_bre
        dbim_ref[...] = d_bim
        dldt_ref[...] = _dot_exact(jnp.broadcast_to(d_ldt, (8, NS)), gs_ref[...])

    vec = jax.ShapeDtypeStruct((1, NS), F32)
    mat = jax.ShapeDtypeStruct((SSM_GROUP, NS), F32)
    return pl.pallas_call(
        body, name="s5_disc_bwd",
        in_specs=[_whole()] * 9, out_specs=[_whole()] * 7,
        out_shape=[vec, vec, jax.ShapeDtypeStruct((8, 128), F32), mat, mat, mat, mat],
        compiler_params=_params(),
    )(a_re, a_im, ldt, b_re, b_im, d_bd, d_cdt, d_ab, group_sum)


def _scan_blocks(buf, pw_ref, carry_ref, n_blocks, reverse):
    row = lax.broadcasted_iota(jnp.int32, (8, SCAN_LANES), 0)
    for lc in range(NS // SCAN_LANES):
        re_cols = pl.ds(lc * SCAN_LANES, SCAN_LANES)
        im_cols = pl.ds(NS + lc * SCAN_LANES, SCAN_LANES)
        pr = pw_ref[0, :, re_cols]
        pi = pw_ref[1, :, re_cols]
        if reverse:
            pi = -pi
            base = [(7, 1), (6, 2), (4, 4)]
            coef = [(jnp.where(row < 8 - k, pr[j:j + 1], 0.0), jnp.where(row < 8 - k, pi[j:j + 1], 0.0), 8 - k)
                    for j, k in base]
        else:
            base = [(0, 1), (1, 2), (3, 4)]
            coef = [(jnp.where(row >= k, pr[j:j + 1], 0.0), jnp.where(row >= k, pi[j:j + 1], 0.0), k)
                    for j, k in base]

        def step(i, carry, pr=pr, pi=pi, coef=coef, re_cols=re_cols, im_cols=im_cols):
            cr, ci = carry
            blk = (n_blocks - 1 - i) if reverse else i
            rows = pl.ds(pl.multiple_of(blk * 8, 8), 8)
            xr = buf[rows, re_cols]
            xi = buf[rows, im_cols]
            for kr, ki, shift in coef:
                sr = pltpu.roll(xr, shift, 0)
                si = pltpu.roll(xi, shift, 0)
                xr, xi = xr + kr * sr - ki * si, xi + kr * si + ki * sr
            xr, xi = xr + pr * cr - pi * ci, xi + pr * ci + pi * cr
            buf[rows, re_cols] = xr
            buf[rows, im_cols] = xi
            edge = slice(0, 1) if reverse else slice(7, 8)
            return xr[edge], xi[edge]

        cr, ci = lax.fori_loop(0, n_blocks, step, (carry_ref[0:1, re_cols], carry_ref[0:1, im_cols]))
        carry_ref[0:1, re_cols] = cr
        carry_ref[0:1, im_cols] = ci


def _ssm_fwd(u, bd, cdt, d_skip, pw):
    L = u.shape[0]
    tc = min(SSM_CHUNK, L)

    def body(u_ref, bd_ref, cdt_ref, dsk_ref, pw_ref, y_ref, s_ref, carry_ref):
        @pl.when(pl.program_id(0) == 0)
        def _():
            carry_ref[...] = jnp.zeros_like(carry_ref)

        uv = u_ref[...]
        s_ref[...] = _dot(uv.astype(BF16), bd_ref[...])
        _scan_blocks(s_ref, pw_ref, carry_ref, tc // 8, reverse=False)
        y_ref[...] = _dot_nt(s_ref[...].astype(BF16), cdt_ref[...]) + dsk_ref[...] * uv

    return pl.pallas_call(
        body, name="s5_fwd", grid=(L // tc,),
        in_specs=[_rows(tc, SSM_WIDTH), _whole(), _whole(), _whole(), _whole()],
        out_specs=[_rows(tc, SSM_WIDTH), _rows(tc, 2 * NS)],
        out_shape=[jax.ShapeDtypeStruct((L, SSM_WIDTH), F32), jax.ShapeDtypeStruct((L, 2 * NS), F32)],
        scratch_shapes=[pltpu.VMEM((8, 2 * NS), F32)],
        compiler_params=_params(),
    )(u, bd, cdt, d_skip, pw)


def _ssm_bwd(dy, u, s, bd, cdt, d_skip, pwr):
    L = u.shape[0]
    tc = min(SSM_CHUNK, L)
    nc = L // tc
    blocks = tc // 8

    def body(dy_ref, u_ref, s_ref, sprev_ref, bd_ref, cdt_ref, dsk_ref, pwr_ref,
             du_ref, ddsk_ref, dbd_ref, dcdt_ref, dab_ref, g_ref, sx_ref, carry_ref):
        i = pl.program_id(0)

        @pl.when(i == 0)
        def _():
            carry_ref[...] = jnp.zeros_like(carry_ref)
            ddsk_ref[...] = jnp.zeros_like(ddsk_ref)
            dbd_ref[...] = jnp.zeros_like(dbd_ref)
            dcdt_ref[...] = jnp.zeros_like(dcdt_ref)
            dab_ref[...] = jnp.zeros_like(dab_ref)

        dyv = dy_ref[...]
        uv = u_ref[...]
        dy16 = dyv.astype(BF16)
        g_ref[...] = _dot(dy16, cdt_ref[...])
        _scan_blocks(g_ref, pwr_ref, carry_ref, blocks, reverse=True)
        g16 = g_ref[...].astype(BF16)
        du_ref[...] = _dot_nt(g16, bd_ref[...]) + dsk_ref[...] * dyv
        ddsk_ref[...] += jnp.sum(dyv * uv, axis=0, keepdims=True)
        dbd_ref[...] += _dot_tn(uv.astype(BF16), g16)
        dcdt_ref[...] += _dot_tn(dy16, s_ref[...].astype(BF16))

        sx_ref[pl.ds(8, tc), :] = s_ref[...]
        sx_ref[pl.ds(0, 8), :] = jnp.where(i == nc - 1, 0.0, sprev_ref[...])
        row = lax.broadcasted_iota(jnp.int32, (8, SCAN_LANES), 0)
        for lc in range(NS // SCAN_LANES):
            re_cols = pl.ds(lc * SCAN_LANES, SCAN_LANES)
            im_cols = pl.ds(NS + lc * SCAN_LANES, SCAN_LANES)

            def step(b, acc, re_cols=re_cols, im_cols=im_cols):
                ar, ai = acc
                off = pl.multiple_of(b * 8, 8)
                gr = g_ref[pl.ds(off, 8), re_cols]
                gi = g_ref[pl.ds(off, 8), im_cols]
                before = pl.ds(off, 8)
                here = pl.ds(off + 8, 8)
                sr = jnp.where(row == 0, sx_ref[before, re_cols][7:8], pltpu.roll(sx_ref[here, re_cols], 1, 0))
                si = jnp.where(row == 0, sx_ref[before, im_cols][7:8], pltpu.roll(sx_ref[here, im_cols], 1, 0))
                return ar + gr * sr + gi * si, ai + gi * sr - gr * si

            zero = jnp.zeros((8, SCAN_LANES), F32)
            ar, ai = lax.fori_loop(0, blocks, step, (zero, zero))
            dab_ref[0, :, re_cols] += ar
            dab_ref[1, :, re_cols] += ai

    rev = lambda i: (nc - 1 - i, 0)
    sprev = pl.BlockSpec((8, 2 * NS), lambda i: (jnp.maximum((nc - 1 - i) * blocks - 1, 0), 0))
    return pl.pallas_call(
        body, name="s5_bwd", grid=(nc,),
        in_specs=[pl.BlockSpec((tc, SSM_WIDTH), rev), pl.BlockSpec((tc, SSM_WIDTH), rev),
                  pl.BlockSpec((tc, 2 * NS), rev), sprev, _whole(), _whole(), _whole(), _whole()],
        out_specs=[pl.BlockSpec((tc, SSM_WIDTH), rev), _whole(), _whole(), _whole(), _whole()],
        out_shape=[jax.ShapeDtypeStruct((L, SSM_WIDTH), F32), jax.ShapeDtypeStruct((1, SSM_WIDTH), F32),
                   jax.ShapeDtypeStruct((SSM_WIDTH, 2 * NS), F32), jax.ShapeDtypeStruct((SSM_WIDTH, 2 * NS), F32),
                   jax.ShapeDtypeStruct((2, 8, NS), F32)],
        scratch_shapes=[pltpu.VMEM((tc, 2 * NS), F32), pltpu.VMEM((tc + 8, 2 * NS), F32), pltpu.VMEM((8, 2 * NS), F32)],
        compiler_params=_params(),
    )(dy, u, s, s, bd, cdt, d_skip, pwr)


def _branches(o_attn, y, gates, w_ab, w_glu, w_sb):
    ya = _dot(o_attn.astype(BF16), w_ab[...])
    gel = _gelu(y)
    glu = _dot(gel.astype(BF16), w_glu[...])
    p = glu[:, :SSM_WIDTH]
    sg = _sigmoid(glu[:, SSM_WIDTH:])
    ys2 = p * sg
    ysb = _dot(ys2.astype(BF16), w_sb[...])
    ga = gates[:, :D_MODEL]
    gs = gates[:, D_MODEL:]
    return ya, gel, p, sg, ys2, ysb, ga, gs


def _mix_out_fwd(x1, o_g, lse_g, y, gates, w_ab, w_glu, w_sb, w_out):
    L = x1.shape[0]
    tm = min(ROW_TILE, L)

    def body(x_ref, o0, o1, o2, l0, l1, l2, y_ref, gate_ref, wab_ref, wglu_ref, wsb_ref, wout_ref,
             x2_ref, oat_ref, lse0, lse1, lse2, scr):
        la, lb, lc = (_from_residues(ref, scr, d) for ref, d in zip((l0, l1, l2), DILATIONS))
        m = jnp.maximum(jnp.maximum(la, lb), lc)
        ea, eb, ec = jnp.exp(la - m), jnp.exp(lb - m), jnp.exp(lc - m)
        tot = ea + eb + ec
        oa, ob, oc = (_from_residues(ref, scr, d) for ref, d in zip((o0, o1, o2), DILATIONS))
        o_attn = (ea * oa + eb * ob + ec * oc) / tot
        oat_ref[...] = o_attn
        lse = m + jnp.log(tot)
        for ref, d in zip((lse0, lse1, lse2), DILATIONS):
            _to_residues(lse, ref, scr, d)
        ya, _, _, _, _, ysb, ga, gs = _branches(o_attn, y_ref[...], gate_ref[...], wab_ref, wglu_ref, wsb_ref)
        mix = ga * ya + gs * ysb
        x2_ref[...] = x_ref[...] + _dot(mix.astype(BF16), wout_ref[...])

    res = [_residue_spec(d, tm) for d in DILATIONS]
    return pl.pallas_call(
        body, name="mix_out_fwd", grid=(L // tm,),
        in_specs=[_rows(tm, D_MODEL)] + res * 2 + [_rows(tm, SSM_WIDTH), _rows(tm, 2 * D_MODEL)] + [_whole()] * 4,
        out_specs=[_rows(tm, D_MODEL), _rows(tm, GROUP_WIDTH)] + res,
        out_shape=[jax.ShapeDtypeStruct((L, D_MODEL), F32), jax.ShapeDtypeStruct((L, GROUP_WIDTH), F32)]
        + [_residue_shape(d, L, F32) for d in DILATIONS],
        scratch_shapes=[_residue_scratch(tm)],
        compiler_params=_params(),
    )(x1, *o_g, *lse_g, y, gates, w_ab, w_glu, w_sb, w_out)


def _mix_out_bwd(dx2, o_attn, y, gates, w_ab, w_glu, w_sb, w_out, head_sum):
    L = dx2.shape[0]
    tm = min(ROW_TILE, L)

    def body(dx_ref, oat_ref, y_ref, gate_ref, wab_ref, wglu_ref, wsb_ref, wout_ref, hs_ref,
             do0, do1, do2, dl0, dl1, dl2, dy_ref, dgp_ref, mix_ref, dya_ref, dys_ref, ys2_ref, gel_ref, dglu_ref,
             dgb_ref, scr):
        i = pl.program_id(0)
        o_attn = oat_ref[...]
        yv = y_ref[...]
        ya, gel, p, sg, ys2, ysb, ga, gs = _branches(o_attn, yv, gate_ref[...], wab_ref, wglu_ref, wsb_ref)
        mix_ref[...] = (ga * ya + gs * ysb).astype(BF16)
        ys2_ref[...] = ys2.astype(BF16)
        gel_ref[...] = gel.astype(BF16)
        dmix = _dot_nt(dx_ref[...].astype(BF16), wout_ref[...])
        dgp = jnp.concatenate([dmix * ya * ga * (1.0 - ga), dmix * ysb * gs * (1.0 - gs)], axis=1)
        dgp_ref[...] = dgp.astype(BF16)

        @pl.when(i == 0)
        def _():
            dgb_ref[...] = jnp.zeros_like(dgb_ref)

        dgb_ref[...] += jnp.sum(dgp, axis=0, keepdims=True)
        dya = (dmix * ga).astype(BF16)
        dys = (dmix * gs).astype(BF16)
        dya_ref[...] = dya
        dys_ref[...] = dys
        d_o = _dot_nt(dya, wab_ref[...])
        delta = _dot_exact(d_o * o_attn, hs_ref[...])
        for do_ref, dl_ref, d in zip((do0, do1, do2), (dl0, dl1, dl2), DILATIONS):
            _to_residues(d_o, do_ref, scr, d)
            _to_residues(delta, dl_ref, scr, d)
        dys2 = _dot_nt(dys, wsb_ref[...])
        dglu = jnp.concatenate([dys2 * sg, dys2 * p * sg * (1.0 - sg)], axis=1).astype(BF16)
        dglu_ref[...] = dglu
        dy_ref[...] = _dot_nt(dglu, wglu_ref[...]) * _gelu_grad(yv)

    grp = _rows(tm, GROUP_WIDTH)
    wide = _rows(tm, D_MODEL)
    half = _rows(tm, SSM_WIDTH)
    res = [_residue_spec(d, tm) for d in DILATIONS]
    sds = jax.ShapeDtypeStruct
    return pl.pallas_call(
        body, name="mix_out_bwd", grid=(L // tm,),
        in_specs=[wide, grp, half, _rows(tm, 2 * D_MODEL)] + [_whole()] * 5,
        out_specs=res + res + [half, _rows(tm, 2 * D_MODEL), wide, wide, wide, half, half, wide, _acc_row(2 * D_MODEL)],
        out_shape=[_residue_shape(d, L, BF16) for d in DILATIONS] + [_residue_shape(d, L, F32) for d in DILATIONS]
        + [sds((L, SSM_WIDTH), F32),
           sds((L, 2 * D_MODEL), BF16), sds((L, D_MODEL), BF16), sds((L, D_MODEL), BF16),
           sds((L, D_MODEL), BF16), sds((L, SSM_WIDTH), BF16), sds((L, SSM_WIDTH), BF16),
           sds((L, D_MODEL), BF16), sds((1, 2 * D_MODEL), F32)],
        scratch_shapes=[_residue_scratch(tm)],
        compiler_params=_params(),
    )(dx2, o_attn, y, gates, w_ab, w_glu, w_sb, w_out, head_sum)


def _adamw(w, g, m, v, name):
    R, C = w.shape
    tr = _row_tile(R, 256)

    def body(w_ref, g_ref, m_ref, v_ref, d_ref, mo_ref, vo_ref):
        gv = g_ref[...]
        mn = ADAM_B1 * m_ref[...] + (1.0 - ADAM_B1) * gv
        vn = ADAM_B2 * v_ref[...] + (1.0 - ADAM_B2) * (gv * gv)
        m_hat = mn / (1.0 - ADAM_B1 ** ADAM_STEP)
        v_hat = vn / (1.0 - ADAM_B2 ** ADAM_STEP)
        d_ref[...] = -ADAM_LR * (m_hat / (jnp.sqrt(v_hat) + ADAM_EPS) + ADAM_WD * w_ref[...])
        mo_ref[...] = mn
        vo_ref[...] = vn

    blk = pl.BlockSpec((tr, C), lambda i: (i, 0))
    return pl.pallas_call(
        body, name=name, grid=(R // tr,),
        in_specs=[blk] * 4, out_specs=[blk] * 3,
        out_shape=[jax.ShapeDtypeStruct((R, C), F32)] * 3,
        compiler_params=_params(),
    )(w, g, m, v)


def _sum_slots(x, name):
    S, R, C = x.shape
    tr = _row_tile(R, 512)

    def body(x_ref, o_ref):
        acc = x_ref[0].astype(F32)
        for k in range(1, S):
            acc = acc + x_ref[k].astype(F32)
        o_ref[...] = acc

    return pl.pallas_call(
        body, name=name, grid=(R // tr,),
        in_specs=[pl.BlockSpec((S, tr, C), lambda i: (0, i, 0))],
        out_specs=pl.BlockSpec((tr, C), lambda i: (i, 0)),
        out_shape=jax.ShapeDtypeStruct((R, C), F32),
        compiler_params=_params(),
    )(x)


def _add_halves(g, r1, name):
    S, R, C = g.shape
    H = R // 2
    tr = _row_tile(H, 512)
    hb = H // tr

    def body(c_ref, g_ref, r_ref, o_ref):
        o_ref[...] = (g_ref[...] + r_ref[...]).astype(BF16)

    core = lax.axis_index("c").astype(jnp.int32).reshape(1)
    return pl.pallas_call(
        body, name=name,
        grid_spec=pltpu.PrefetchScalarGridSpec(
            num_scalar_prefetch=1, grid=(S, hb),
            in_specs=[pl.BlockSpec((None, tr, C), lambda j, i, c_ref: (j, c_ref[0] * hb + i, 0)),
                      pl.BlockSpec((None, tr, C), lambda j, i, c_ref: (j, i, 0))],
            out_specs=pl.BlockSpec((None, tr, C), lambda j, i, c_ref: (j, i, 0))),
        out_shape=jax.ShapeDtypeStruct((S, H, C), BF16),
        compiler_params=_params(),
    )(core, g, r1)


_ANY = pl.BlockSpec(memory_space=pl.ANY)


def _place():
    x, y, c = lax.axis_index("x"), lax.axis_index("y"), lax.axis_index("c")
    chips = [(1 - x, y), (x, 1 - y), (1 - x, 1 - y)]
    return x, y, c, chips


def _comm_call(body, name, ins, out_shapes, n_remote, n_local):
    return pl.pallas_call(
        body, name=name,
        in_specs=[_ANY] * len(ins), out_specs=[_ANY] * len(out_shapes), out_shape=out_shapes,
        scratch_shapes=[pltpu.SemaphoreType.DMA((n_remote,)), pltpu.SemaphoreType.DMA((n_remote,)),
                        pltpu.SemaphoreType.DMA((max(n_local, 1),))],
    )(*ins)


def _remote(src, dst, send_sems, recv_sems, k, device):
    return pltpu.make_async_remote_copy(src_ref=src, dst_ref=dst, send_sem=send_sems.at[k], recv_sem=recv_sems.at[k],
                                        device_id=device, device_id_type=MESH)


def _gather_weights(shards):
    n = len(shards)

    def body(*refs):
        w_refs, out_refs = refs[:n], refs[n:2 * n]
        send_sems, recv_sems, local_sems = refs[2 * n:]
        x, y, c, chips = _place()
        me = 2 * x + y
        sibling = (x, y, 1 - c)

        def half(k, chip_idx, core):
            H = shards[k].shape[0] // 2
            return out_refs[k].at[chip_idx, pl.ds(core * H, H), :]

        mine = [pltpu.make_async_copy(w_refs[k], out_refs[k].at[me], local_sems.at[k]) for k in range(n)]
        for cp in mine:
            cp.start()
        first = []
        for k in range(n):
            H = shards[k].shape[0] // 2
            for j, (cx, cy) in enumerate(chips):
                first.append(_remote(w_refs[k].at[pl.ds(c * H, H), :], half(k, me, c), send_sems, recv_sems,
                                     3 * k + j, (cx, cy, c)))
        for cp in first:
            cp.start()
        passed = []
        for k in range(n):
            for j, (cx, cy) in enumerate(chips):
                landed = half(k, 2 * cx + cy, c)
                _remote(landed, landed, send_sems, recv_sems, 3 * k + j, (cx, cy, c)).wait_recv()
                fwd = _remote(landed, landed, send_sems, recv_sems, 3 * n + 3 * k + j, sibling)
                fwd.start()
                passed.append(fwd)
        for k in range(n):
            for j, (cx, cy) in enumerate(chips):
                other = half(k, 2 * cx + cy, 1 - c)
                _remote(other, other, send_sems, recv_sems, 3 * n + 3 * k + j, sibling).wait_recv()
        for cp in first + passed:
            cp.wait_send()
        for cp in mine:
            cp.wait()

    return _comm_call(body, "gather_weights", shards,
                      [jax.ShapeDtypeStruct((N_SHARD,) + w.shape, w.dtype) for w in shards], 6 * n, n)


def _swap_halves(gs):
    n = len(gs)

    def body(*refs):
        g_refs, out_refs = refs[:n], refs[n:2 * n]
        send_sems, recv_sems, _ = refs[2 * n:]
        x, y, c, _ = _place()
        cps = []
        for k in range(n):
            H = gs[k].shape[1] // 2
            cp = _remote(g_refs[k].at[:, pl.ds((1 - c) * H, H), :], out_refs[k], send_sems, recv_sems, k, (x, y, 1 - c))
            cp.start()
            cps.append(cp)
        for cp in cps:
            cp.wait()

    return _comm_call(body, "reduce_swap_halves", gs,
                      [jax.ShapeDtypeStruct((g.shape[0], g.shape[1] // 2, g.shape[2]), g.dtype) for g in gs], n, 0)


def _exchange_chips(ts):
    n = len(ts)

    def body(*refs):
        t_refs, out_refs = refs[:n], refs[n:2 * n]
        send_sems, recv_sems, local_sems = refs[2 * n:]
        x, y, c, chips = _place()
        me = 2 * x + y
        mine = [pltpu.make_async_copy(t_refs[k].at[me], out_refs[k].at[me], local_sems.at[k]) for k in range(n)]
        for cp in mine:
            cp.start()
        sent = []
        for k in range(n):
            for j, (cx, cy) in enumerate(chips):
                cp = _remote(t_refs[k].at[2 * cx + cy], out_refs[k].at[me], send_sems, recv_sems, 3 * k + j, (cx, cy, c))
                cp.start()
                sent.append(cp)
        for k in range(n):
            for j, (cx, cy) in enumerate(chips):
                slot = out_refs[k].at[2 * cx + cy]
                _remote(slot, slot, send_sems, recv_sems, 3 * k + j, (cx, cy, c)).wait_recv()
        for cp in sent:
            cp.wait_send()
        for cp in mine:
            cp.wait()

    return _comm_call(body, "reduce_exchange_chips", ts, [jax.ShapeDtypeStruct(t.shape, t.dtype) for t in ts], 3 * n, n)


def _join_halves(fs):
    n = len(fs)

    def body(*refs):
        f_refs, out_refs = refs[:n], refs[n:2 * n]
        send_sems, recv_sems, local_sems = refs[2 * n:]
        x, y, c, _ = _place()
        mine, sent = [], []
        for k in range(n):
            H = fs[k].shape[0]
            here = out_refs[k].at[pl.ds(c * H, H), :]
            cp = pltpu.make_async_copy(f_refs[k], here, local_sems.at[k])
            cp.start()
            mine.append(cp)
            cp = _remote(f_refs[k], here, send_sems, recv_sems, k, (x, y, 1 - c))
            cp.start()
            sent.append(cp)
        for k in range(n):
            H = fs[k].shape[0]
            other = out_refs[k].at[pl.ds((1 - c) * H, H), :]
            _remote(other, other, send_sems, recv_sems, k, (x, y, 1 - c)).wait_recv()
        for cp in sent:
            cp.wait_send()
        for cp in mine:
            cp.wait()

    return _comm_call(body, "reduce_join_halves", fs,
                      [jax.ShapeDtypeStruct((2 * f.shape[0], f.shape[1]), f.dtype) for f in fs], n, n)


def _gather_small(v):
    R, C = v.shape

    def body(v_ref, out_ref, send_sems, recv_sems, local_sem):
        x, y, c, _ = _place()
        me = 4 * x + 2 * y + c
        mine = pltpu.make_async_copy(v_ref, out_ref.at[me], local_sem)
        mine.start()
        flips = [(fx, fy, fc) for fx in (0, 1) for fy in (0, 1) for fc in (0, 1)][1:]
        peers = [((1 - x) if fx else x, (1 - y) if fy else y, (1 - c) if fc else c) for fx, fy, fc in flips]
        sent = []
        for j, peer in enumerate(peers):
            cp = pltpu.make_async_remote_copy(
                src_ref=v_ref, dst_ref=out_ref.at[me], send_sem=send_sems.at[j], recv_sem=recv_sems.at[j],
                device_id=peer, device_id_type=MESH)
            cp.start()
            sent.append(cp)
        for j, peer in enumerate(peers):
            slot = out_ref.at[4 * peer[0] + 2 * peer[1] + peer[2]]
            pltpu.make_async_remote_copy(
                src_ref=slot, dst_ref=slot, send_sem=send_sems.at[j], recv_sem=recv_sems.at[j],
                device_id=peer, device_id_type=MESH).wait_recv()
        for cp in sent:
            cp.wait_send()
        mine.wait()

    return pl.pallas_call(
        body, name="gather_small",
        in_specs=[_ANY], out_specs=_ANY,
        out_shape=jax.ShapeDtypeStruct((8, R, C), F32),
        scratch_shapes=[pltpu.SemaphoreType.DMA((7,)), pltpu.SemaphoreType.DMA((7,)), pltpu.SemaphoreType.DMA],
    )(v)


def _reduce_scatter(gs, names):
    r1 = _swap_halves(gs)
    ts = [_add_halves(g, r, "reduce_add_cores_" + nm) for g, r, nm in zip(gs, r1, names)]
    us = _exchange_chips(ts)
    fs = [_sum_slots(u, "reduce_add_chips_" + nm) for u, nm in zip(us, names)]
    return _join_halves(fs)


BIG = ["ffn1_w_gate", "ffn1_w_up", "ffn1_w_down", "w_in", "ssm_w_glu", "w_attn_branch", "w_ssm_branch",
       "w_out", "ffn2_w_gate", "ffn2_w_up", "ffn2_w_down"]
SMALL = ["ffn1_norm", "mix_norm", "gate_bias", "rel_bias_table", "ssm_a_re", "ssm_a_im", "ssm_log_dt",
         "ssm_b_re", "ssm_b_im", "ssm_c_re", "ssm_c_im", "ssm_d", "ffn2_norm", "final_norm"]
ORDER = ["ffn1_norm", "ffn1_w_gate", "ffn1_w_up", "ffn1_w_down", "mix_norm", "w_in", "gate_bias", "rel_bias_table",
         "ssm_a_re", "ssm_a_im", "ssm_log_dt", "ssm_b_re", "ssm_b_im", "ssm_c_re", "ssm_c_im", "ssm_d",
         "ssm_w_glu", "w_attn_branch", "w_ssm_branch", "w_out", "ffn2_norm", "ffn2_w_gate", "ffn2_w_up",
         "ffn2_w_down", "final_norm"]


def _pack_small(arrays):
    rows = []
    for a in arrays:
        flat = a.reshape(-1).astype(F32)
        pad = (-flat.shape[0]) % 128
        rows.append(jnp.pad(flat, (0, pad)).reshape(-1, 128))
    packed = jnp.concatenate(rows, axis=0)
    return jnp.pad(packed, ((0, (-packed.shape[0]) % 8), (0, 0)))


def _unpack_small(packed, shapes):
    out, r0 = [], 0
    for shp in shapes:
        n = math.prod(shp)
        rows = -(-n // 128)
        out.append(packed[r0:r0 + rows].reshape(-1)[:n].reshape(shp))
        r0 += rows
    return out


def _local_step(x, target, w, small):
    L = x.shape[0]
    row = lambda v: v.reshape(1, -1)

    a_re, a_im = small["ssm_a_re"].reshape(1, NS), small["ssm_a_im"].reshape(1, NS)
    ldt = jnp.repeat(small["ssm_log_dt"].reshape(SSM_GROUPS), SSM_STATE).reshape(1, NS)
    to_cn = lambda b: b.reshape(SSM_GROUPS, SSM_STATE, SSM_GROUP).transpose(2, 0, 1).reshape(SSM_GROUP, NS)
    c_to_cn = lambda c: c.reshape(SSM_GROUPS, SSM_GROUP, SSM_STATE).transpose(1, 0, 2).reshape(SSM_GROUP, NS)
    b_re, b_im = to_cn(small["ssm_b_re"]), to_cn(small["ssm_b_im"])
    c_re, c_im = c_to_cn(small["ssm_c_re"]), c_to_cn(small["ssm_c_im"])
    d_skip = row(small["ssm_d"])
    pw, pwr, bd, cdt = _disc_fwd(a_re, a_im, ldt, b_re, b_im, c_re, c_im)

    onehot = _bucket_onehot()
    table_t = small["rel_bias_table"].T.reshape(3, HEADS_PER_GROUP, N_BUCKETS)
    table_t = jnp.pad(table_t, ((0, 0), (0, 8 - HEADS_PER_GROUP), (0, 0)))
    bias = _bias_expand(table_t, onehot)[:, :HEADS_PER_GROUP].reshape(3, HEADS_PER_GROUP, ATTN_BLOCK, 2 * ATTN_BLOCK)

    n1, nm, n2, nf = row(small["ffn1_norm"]), row(small["mix_norm"]), row(small["ffn2_norm"]), row(small["final_norm"])
    gate_bias = row(small["gate_bias"])

    x1, a1, b1 = _ffn_fwd(x, n1, w["ffn1_w_gate"], w["ffn1_w_up"], w["ffn1_w_down"], "ffn1_fwd")
    *qkv, u, gates = _mix_in_fwd(x1, nm, w["w_in"], gate_bias)
    q, k, v = qkv[0:3], qkv[3:6], qkv[6:9]
    o_g, lse_g = [], []
    for grp in range(3):
        o, lse = _attn_fwd(q[grp], k[grp], v[grp], bias[grp], f"attn_fwd_{grp}")
        o_g.append(o)
        lse_g.append(lse)
    y, s = _ssm_fwd(u, bd, cdt, d_skip, pw)
    x2, o_attn, *lse_tot = _mix_out_fwd(x1, o_g, lse_g, y, gates, w["w_attn_branch"], w["ssm_w_glu"],
                                        w["w_ssm_branch"], w["w_out"])
    x3, a2, b2 = _ffn_fwd(x2, n2, w["ffn2_w_gate"], w["ffn2_w_up"], w["ffn2_w_down"], "ffn2_fwd")
    loss_blk, dx3, d_nf = _loss_fwd_bwd(x3, nf, target)

    gw, gs = {}, {}
    gs["final_norm"] = d_nf

    dx2, da, db, sact, h, d_out, gs["ffn2_norm"] = _ffn_bwd(dx3, x2, n2, a2, b2, w["ffn2_w_gate"], w["ffn2_w_up"],
                                                            w["ffn2_w_down"], "ffn2_bwd")
    gw["ffn2_w_gate"] = _matmul_tn(h[None], da, "ffn2_dw_gate")
    gw["ffn2_w_up"] = _matmul_tn(h[None], db, "ffn2_dw_up")
    gw["ffn2_w_down"] = _matmul_tn(sact, d_out[None], "ffn2_dw_down")

    head_sum = (jnp.arange(GROUP_WIDTH)[:, None] // HEAD_DIM == jnp.arange(GROUP_WIDTH)[None, :] // HEAD_DIM).astype(F32)
    (*d_o_delta, dy, dgp, mix, dya, dys, ys2, gel, dglu, gs["gate_bias"]) = _mix_out_bwd(
        dx2, o_attn, y, gates, w["w_attn_branch"], w["ssm_w_glu"], w["w_ssm_branch"], w["w_out"], head_sum)
    d_o, delta = d_o_delta[0:3], d_o_delta[3:6]
    gw["w_out"] = _matmul_tn(mix[None], dx2[None], "dw_out")[0]
    gw["w_attn_branch"] = _matmul_tn(o_attn[None], dya[None], "dw_attn_branch")[0]
    gw["w_ssm_branch"] = _matmul_tn(ys2[None], dys[None], "dw_ssm_branch")[0]
    gw["ssm_w_glu"] = _matmul_tn(gel[None], dglu[None], "dw_glu")[0]

    dqs, dks, dvs, dsums = [], [], [], []
    for grp in range(3):
        dq, dk, dv, dsum = _attn_bwd(q[grp], k[grp], v[grp], d_o[grp], lse_tot[grp], delta[grp], bias[grp],
                                     f"attn_bwd_{grp}")
        dqs.append(dq)
        dks.append(dk)
        dvs.append(dv)
        dsums.append(dsum.reshape(HEADS_PER_GROUP, -1))
    dsum_all = jnp.pad(jnp.stack(dsums), ((0, 0), (0, 8 - HEADS_PER_GROUP), (0, 0)))
    d_table = _bias_reduce(dsum_all, onehot)[:, :HEADS_PER_GROUP]
    gs["rel_bias_table"] = d_table.reshape(3 * HEADS_PER_GROUP, N_BUCKETS).T

    du, gs["ssm_d"], d_bd, d_cdt, d_ab = _ssm_bwd(dy, u, s, bd, cdt, d_skip, pwr)
    group_sum = (jnp.arange(NS)[:, None] // SSM_STATE == jnp.arange(128)[None, :]).astype(F32)
    d_are, d_aim, d_ldt, d_bre, d_bim, d_cre, d_cim = _disc_bwd(a_re, a_im, ldt, b_re, b_im, d_bd, d_cdt, d_ab, group_sum)
    gs["ssm_a_re"], gs["ssm_a_im"] = d_are, d_aim
    gs["ssm_log_dt"] = d_ldt[0, :SSM_GROUPS]
    from_cn = lambda t: t.reshape(SSM_GROUP, SSM_GROUPS, SSM_STATE).transpose(1, 2, 0)
    c_from_cn = lambda t: t.reshape(SSM_GROUP, SSM_GROUPS, SSM_STATE).transpose(1, 0, 2)
    gs["ssm_b_re"], gs["ssm_b_im"] = from_cn(d_bre), from_cn(d_bim)
    gs["ssm_c_re"], gs["ssm_c_im"] = c_from_cn(d_cre), c_from_cn(d_cim)

    dx1, hm, dz, gs["mix_norm"] = _mix_in_bwd(dx2, x1, nm, dqs + dks + dvs, du, dgp, w["w_in"])
    gw["w_in"] = _matmul_tn(hm[None], dz[None], "dw_in")[0]

    dx0, da, db, sact, h, d_out, gs["ffn1_norm"] = _ffn_bwd(dx1, x, n1, a1, b1, w["ffn1_w_gate"], w["ffn1_w_up"],
                                                            w["ffn1_w_down"], "ffn1_bwd")
    gw["ffn1_w_gate"] = _matmul_tn(h[None], da, "ffn1_dw_gate")
    gw["ffn1_w_up"] = _matmul_tn(h[None], db, "ffn1_dw_up")
    gw["ffn1_w_down"] = _matmul_tn(sact, d_out[None], "ffn1_dw_down")
    return loss_blk, dx0, gw, gs


def _split_cols(g):
    K, N = g.shape
    return g.reshape(K, N_SHARD, N // N_SHARD).transpose(1, 0, 2)


def _join_cols(w):
    S, K, n = w.shape
    return w.transpose(1, 0, 2).reshape(K, S * n)


COL_SHARDED = ("w_in", "ssm_w_glu", "w_attn_branch", "w_ssm_branch")


def kernel(x, ffn1_norm, ffn1_w_gate, ffn1_w_up, ffn1_w_down, mix_norm, w_in, gate_bias, rel_bias_table, ssm_a_re, ssm_a_im, ssm_log_dt, ssm_b_re, ssm_b_im, ssm_c_re, ssm_c_im, ssm_d, ssm_w_glu, w_attn_branch, w_ssm_branch, w_out, ffn2_norm, ffn2_w_gate, ffn2_w_up, ffn2_w_down, final_norm, loss_target, m_ffn1_norm, m_ffn1_w_gate, m_ffn1_w_up, m_ffn1_w_down, m_mix_norm, m_w_in, m_gate_bias, m_rel_bias_table, m_ssm_a_re, m_ssm_a_im, m_ssm_log_dt, m_ssm_b_re, m_ssm_b_im, m_ssm_c_re, m_ssm_c_im, m_ssm_d, m_ssm_w_glu, m_w_attn_branch, m_w_ssm_branch, m_w_out, m_ffn2_norm, m_ffn2_w_gate, m_ffn2_w_up, m_ffn2_w_down, m_final_norm, v_ffn1_norm, v_ffn1_w_gate, v_ffn1_w_up, v_ffn1_w_down, v_mix_norm, v_w_in, v_gate_bias, v_rel_bias_table, v_ssm_a_re, v_ssm_a_im, v_ssm_log_dt, v_ssm_b_re, v_ssm_b_im, v_ssm_c_re, v_ssm_c_im, v_ssm_d, v_ssm_w_glu, v_w_attn_branch, v_w_ssm_branch, v_w_out, v_ffn2_norm, v_ffn2_w_gate, v_ffn2_w_up, v_ffn2_w_down, v_final_norm):
    args = dict(locals())
    weights = {n: args[n] for n in ORDER}
    moms = {n: args["m_" + n] for n in ORDER}
    vels = {n: args["v_" + n] for n in ORDER}

    shard2d = {n: weights[n].reshape(weights[n].shape[-2:]) for n in BIG}
    full = dict(zip(BIG, _gather_weights([shard2d[n].astype(BF16) for n in BIG])))
    for n in COL_SHARDED:
        full[n] = _join_cols(full[n])
    full["w_out"] = full["w_out"].reshape(D_MODEL, D_MODEL)

    small = {n: weights[n] for n in SMALL}
    loss_blk, grad_x, gw, gs = _local_step(x[0], loss_target[0], full, small)

    for n in COL_SHARDED:
        gw[n] = _split_cols(gw[n])
    gw["w_out"] = gw["w_out"].reshape(N_SHARD, D_MODEL // N_SHARD, D_MODEL)
    grads = dict(zip(BIG, _reduce_scatter([gw[n] for n in BIG], BIG)))

    small_shapes = [weights[n].shape for n in SMALL]
    mine = _pack_small([gs[n] for n in SMALL] + [loss_blk[0:1, :]])
    total = _sum_slots(_gather_small(mine), "sum_small")
    small_grads = _unpack_small(total, small_shapes + [(128,)])
    loss = small_grads[-1][0]
    for n, g in zip(SMALL, small_grads[:-1]):
        grads[n] = g

    delta, new_m, new_v = {}, {}, {}
    for n in BIG:
        d, m, v = _adamw(shard2d[n], grads[n], moms[n].reshape(shard2d[n].shape), vels[n].reshape(shard2d[n].shape),
                         "adamw_" + n)
        shp = weights[n].shape
        delta[n], new_m[n], new_v[n] = d.reshape(shp), m.reshape(shp), v.reshape(shp)
        grads[n] = grads[n].reshape(shp)
    d, m, v = _adamw(_pack_small([weights[n] for n in SMALL]), _pack_small([grads[n] for n in SMALL]),
                     _pack_small([moms[n] for n in SMALL]), _pack_small([vels[n] for n in SMALL]), "adamw_small")
    for n, dd, mm, vv in zip(SMALL, _unpack_small(d, small_shapes), _unpack_small(m, small_shapes),
                             _unpack_small(v, small_shapes)):
        delta[n], new_m[n], new_v[n] = dd, mm, vv

    return (loss, grad_x[None], *[grads[n] for n in ORDER], *[delta[n] for n in ORDER],
            *[new_m[n] for n in ORDER], *[new_v[n] for n in ORDER])
```

```python
import functools
import math

import jax
import jax.numpy as jnp
from jax import lax
from jax.experimental import pallas as pl
from jax.experimental.pallas import tpu as pltpu

F32 = jnp.float32
BF16 = jnp.bfloat16
MESH = pl.DeviceIdType.MESH

D_MODEL = 1024
D_FF = 2816
HEAD_DIM = 64
HEADS_PER_GROUP = 4
DILATIONS = (1, 4, 16)
WINDOW_STEPS = 128
ATTN_BLOCK = 128
GROUP_WIDTH = HEADS_PER_GROUP * HEAD_DIM
ATTN_WIDTH = 3 * GROUP_WIDTH
N_BUCKETS = 32
MAX_DISTANCE = 2048
NEG_INF = -1e30
SSM_WIDTH = 512
SSM_GROUP = 16
SSM_GROUPS = 32
SSM_STATE = 64
NS = SSM_GROUPS * SSM_STATE
EPS = 1e-6
IN_WIDTH = 3 * ATTN_WIDTH + SSM_WIDTH + 2 * D_MODEL
Q_SCALE = HEAD_DIM ** -0.5
N_SHARD = 4
FF_SHARD = D_FF // N_SHARD
ADAM_LR, ADAM_B1, ADAM_B2, ADAM_EPS, ADAM_WD, ADAM_STEP = 0.001, 0.9, 0.999, 1e-08, 0.01, 10

LANES = 128
VMEM_LIMIT = 56 * 1024 * 1024
ROW_TILE = 512
FFN_BWD_TILE = 256
SSM_CHUNK = 256
SCAN_LANES = 512
ADAMW_BLOCK_BYTES = 1 << 20
TN_VMEM_BUDGET = 40 * 1024 * 1024


def _params(**kw):
    return pltpu.CompilerParams(vmem_limit_bytes=VMEM_LIMIT, **kw)


def _dot(a, b):
    return jnp.dot(a, b, preferred_element_type=F32)


def _dot_nt(a, b):
    return lax.dot_general(a, b, (((1,), (1,)), ((), ())), preferred_element_type=F32)


def _dot_tn(a, b):
    return lax.dot_general(a, b, (((0,), (0,)), ((), ())), preferred_element_type=F32)


def _dot_exact(a, b):
    return jnp.dot(a, b, preferred_element_type=F32, precision=lax.Precision.HIGHEST)


def _dot_nt_exact(a, b):
    return lax.dot_general(a, b, (((1,), (1,)), ((), ())), preferred_element_type=F32,
                           precision=lax.Precision.HIGHEST)


def _rms(x):
    r = lax.rsqrt(jnp.mean(x * x, axis=-1, keepdims=True) + EPS)
    return r, x * r


def _rms_bwd(dh, g, r, xhat):
    dxh = dh * g
    return r * (dxh - xhat * jnp.mean(dxh * xhat, axis=-1, keepdims=True))


def _sigmoid(x):
    return 1.0 / (1.0 + jnp.exp(-x))


_GELU_C = math.sqrt(2.0 / math.pi)


def _gelu(x):
    return 0.5 * x * (1.0 + jnp.tanh(_GELU_C * (x + 0.044715 * x * x * x)))


def _gelu_grad(x):
    t = jnp.tanh(_GELU_C * (x + 0.044715 * x * x * x))
    return 0.5 * (1.0 + t) + 0.5 * x * (1.0 - t * t) * _GELU_C * (1.0 + 3 * 0.044715 * x * x)


def _whole():
    return pl.BlockSpec(memory_space=pltpu.VMEM)


def _row_tile(rows, cap):
    if rows <= cap:
        return rows
    return max(t for t in range(8, cap + 1, 8) if rows % t == 0)


def _rows(tm, w):
    return pl.BlockSpec((tm, w), lambda i: (i, 0))


def _acc_row(w):
    return pl.BlockSpec((1, w), lambda i: (0, 0))


def _ffn_fwd(x, g, wg, wu, wd, name):
    L = x.shape[0]
    tm = min(ROW_TILE, L)

    def body(x_ref, g_ref, wg_ref, wu_ref, wd_ref, xo_ref, a_ref, b_ref):
        xv = x_ref[...]
        r, xhat = _rms(xv)
        h = (xhat * g_ref[...]).astype(BF16)
        acc = jnp.zeros((tm, D_MODEL), F32)
        for j in range(N_SHARD):
            a = _dot(h, wg_ref[j])
            b = _dot(h, wu_ref[j])
            a_ref[j] = a.astype(BF16)
            b_ref[j] = b.astype(BF16)
            s = (a * _sigmoid(a) * b).astype(BF16)
            acc = acc + _dot(s, wd_ref[j])
        xo_ref[...] = xv + 0.5 * acc

    act = pl.BlockSpec((N_SHARD, tm, FF_SHARD), lambda i: (0, i, 0))
    return pl.pallas_call(
        body, name=name, grid=(L // tm,),
        in_specs=[_rows(tm, D_MODEL), _whole(), _whole(), _whole(), _whole()],
        out_specs=[_rows(tm, D_MODEL), act, act],
        out_shape=[jax.ShapeDtypeStruct((L, D_MODEL), F32),
                   jax.ShapeDtypeStruct((N_SHARD, L, FF_SHARD), BF16),
                   jax.ShapeDtypeStruct((N_SHARD, L, FF_SHARD), BF16)],
        compiler_params=_params(),
    )(x, g, wg, wu, wd)


def _ffn_bwd(dxo, x, g, a, b, wg, wu, wd, name):
    L = x.shape[0]
    tm = min(FFN_BWD_TILE, L)

    def body(dxo_ref, x_ref, g_ref, a_ref, b_ref, wg_ref, wu_ref, wd_ref,
             dxi_ref, da_ref, db_ref, s_ref, h_ref, do_ref, dg_ref):
        i = pl.program_id(0)
        xv = x_ref[...]
        gv = g_ref[...]
        r, xhat = _rms(xv)
        h_ref[...] = (xhat * gv).astype(BF16)
        dxo_v = dxo_ref[...]
        d_out = (0.5 * dxo_v).astype(BF16)
        do_ref[...] = d_out
        dh = jnp.zeros((tm, D_MODEL), F32)
        for j in range(N_SHARD):
            av = a_ref[j].astype(F32)
            bv = b_ref[j].astype(F32)
            sg = _sigmoid(av)
            sl = av * sg
            ds = _dot_nt(d_out, wd_ref[j])
            dbv = (ds * sl).astype(BF16)
            dav = (ds * bv * (sg * (1.0 + av * (1.0 - sg)))).astype(BF16)
            da_ref[j] = dav
            db_ref[j] = dbv
            s_ref[j] = (sl * bv).astype(BF16)
            dh = dh + _dot_nt(dav, wg_ref[j]) + _dot_nt(dbv, wu_ref[j])

        @pl.when(i == 0)
        def _():
            dg_ref[...] = jnp.zeros_like(dg_ref)

        dg_ref[...] += jnp.sum(dh * xhat, axis=0, keepdims=True)
        dxi_ref[...] = dxo_v + _rms_bwd(dh, gv, r, xhat)

    act = pl.BlockSpec((N_SHARD, tm, FF_SHARD), lambda i: (0, i, 0))
    act_shape = jax.ShapeDtypeStruct((N_SHARD, L, FF_SHARD), BF16)
    return pl.pallas_call(
        body, name=name, grid=(L // tm,),
        in_specs=[_rows(tm, D_MODEL), _rows(tm, D_MODEL), _whole(), act, act, _whole(), _whole(), _whole()],
        out_specs=[_rows(tm, D_MODEL), act, act, act, _rows(tm, D_MODEL), _rows(tm, D_MODEL), _acc_row(D_MODEL)],
        out_shape=[jax.ShapeDtypeStruct((L, D_MODEL), F32), act_shape, act_shape, act_shape,
                   jax.ShapeDtypeStruct((L, D_MODEL), BF16), jax.ShapeDtypeStruct((L, D_MODEL), BF16),
                   jax.ShapeDtypeStruct((1, D_MODEL), F32)],
        compiler_params=_params(),
    )(dxo, x, g, a, b, wg, wu, wd)


def _matmul_tn(a, b, name):
    ja, L, K = a.shape
    jb, _, N = b.shape
    J = max(ja, jb)
    splits = [s for s in (1, 2, 4, 8) if s == 1 or N % (s * LANES) == 0]
    nsplit = next((s for s in splits if 2 * K * (N // s) * 4 <= TN_VMEM_BUDGET // 2), splits[-1])
    nc = N // nsplit
    left = TN_VMEM_BUDGET - 2 * K * nc * 4
    row_bytes = 2 * (K * a.dtype.itemsize + nc * b.dtype.itemsize)
    tm = next((t for t in (2048, 1024, 512, 256) if L % t == 0 and t * row_bytes <= left), min(128, L))

    def body(a_ref, b_ref, o_ref):
        @pl.when(pl.program_id(2) == 0)
        def _():
            o_ref[...] = jnp.zeros_like(o_ref)

        o_ref[...] += _dot_tn(a_ref[...].astype(BF16), b_ref[...].astype(BF16))

    return pl.pallas_call(
        body, name=name, grid=(J, nsplit, L // tm),
        in_specs=[pl.BlockSpec((None, tm, K), (lambda j, s, i: (j, i, 0)) if ja > 1 else (lambda j, s, i: (0, i, 0))),
                  pl.BlockSpec((None, tm, nc), (lambda j, s, i: (j, i, s)) if jb > 1 else (lambda j, s, i: (0, i, s)))],
        out_specs=pl.BlockSpec((None, K, nc), lambda j, s, i: (j, 0, s)),
        out_shape=jax.ShapeDtypeStruct((J, K, N), F32),
        compiler_params=_params(),
    )(a, b)


def _loss_fwd_bwd(x, g, target):
    L = x.shape[0]
    tm = min(ROW_TILE, L)

    def body(x_ref, g_ref, t_ref, loss_ref, dx_ref, dg_ref):
        i = pl.program_id(0)
        xv = x_ref[...]
        gv = g_ref[...]
        r, xhat = _rms(xv)
        err = xhat * gv - t_ref[...]
        part = 0.5 * jnp.sum(jnp.sum(err * err, axis=1, keepdims=True) * (1.0 / D_MODEL), axis=0, keepdims=True)
        dy = err * (1.0 / D_MODEL)

        @pl.when(i == 0)
        def _():
            dg_ref[...] = jnp.zeros_like(dg_ref)
            loss_ref[...] = jnp.zeros_like(loss_ref)

        loss_ref[...] += jnp.broadcast_to(part, loss_ref.shape)
        dg_ref[...] += jnp.sum(dy * xhat, axis=0, keepdims=True)
        dx_ref[...] = _rms_bwd(dy, gv, r, xhat)

    return pl.pallas_call(
        body, name="loss_fwd_bwd", grid=(L // tm,),
        in_specs=[_rows(tm, D_MODEL), _whole(), _rows(tm, D_MODEL)],
        out_specs=[pl.BlockSpec((8, 128), lambda i: (0, 0)), _rows(tm, D_MODEL), _acc_row(D_MODEL)],
        out_shape=[jax.ShapeDtypeStruct((8, 128), F32), jax.ShapeDtypeStruct((L, D_MODEL), F32),
                   jax.ShapeDtypeStruct((1, D_MODEL), F32)],
        compiler_params=_params(),
    )(x, g, target)


_C_K = ATTN_WIDTH
_C_V = 2 * ATTN_WIDTH
_C_U = 3 * ATTN_WIDTH
_C_G = _C_U + SSM_WIDTH


def _residue_spec(d, tm):
    return pl.BlockSpec((d, tm // d, GROUP_WIDTH), lambda i: (0, i, 0))


def _residue_shape(d, L, dtype):
    return jax.ShapeDtypeStruct((d, L // d, GROUP_WIDTH), dtype)


def _residue_scratch(tm):
    return pltpu.VMEM((GROUP_WIDTH // LANES, tm, LANES), F32)


def _to_residues(val, out_ref, scr, d):
    if d == 1:
        out_ref[0] = val.astype(out_ref.dtype)
        return
    tm = val.shape[0]
    for half in range(GROUP_WIDTH // LANES):
        cols = slice(half * LANES, (half + 1) * LANES)
        scr[half] = val[:, cols]
        for r in range(d):
            out_ref[r, :, cols] = scr[half, pl.ds(r, tm // d, stride=d), :].astype(out_ref.dtype)


def _from_residues(ref, scr, d):
    if d == 1:
        return ref[0].astype(F32)
    rows = ref.shape[1]
    for half in range(GROUP_WIDTH // LANES):
        cols = slice(half * LANES, (half + 1) * LANES)
        for r in range(d):
            scr[half, pl.ds(r, rows, stride=d), :] = ref[r, :, cols].astype(F32)
    return jnp.concatenate([scr[half] for half in range(GROUP_WIDTH // LANES)], axis=1)


def _mix_in_fwd(x, g, w_in, gate_bias):
    L = x.shape[0]
    tm = min(ROW_TILE, L)

    def body(x_ref, g_ref, w_ref, gb_ref, *refs):
        qkv_refs, (u_ref, gate_ref, scr) = refs[:9], refs[9:]
        r, xhat = _rms(x_ref[...])
        h = (xhat * g_ref[...]).astype(BF16)
        for part, (c0, scale) in enumerate(((0, Q_SCALE), (_C_K, 1.0), (_C_V, 1.0))):
            z = _dot(h, w_ref[:, c0:c0 + ATTN_WIDTH]) * scale
            for grp, d in enumerate(DILATIONS):
                _to_residues(z[:, grp * GROUP_WIDTH:(grp + 1) * GROUP_WIDTH], qkv_refs[3 * part + grp], scr, d)
        u_ref[...] = _dot(h, w_ref[:, _C_U:_C_G])
        gate_ref[...] = _sigmoid(_dot(h, w_ref[:, _C_G:IN_WIDTH]) + gb_ref[...])

    return pl.pallas_call(
        body, name="mix_in_fwd", grid=(L // tm,),
        in_specs=[_rows(tm, D_MODEL), _whole(), _whole(), _whole()],
        out_specs=[_residue_spec(d, tm) for d in DILATIONS] * 3 + [_rows(tm, SSM_WIDTH), _rows(tm, 2 * D_MODEL)],
        out_shape=[_residue_shape(d, L, BF16) for d in DILATIONS] * 3
        + [jax.ShapeDtypeStruct((L, SSM_WIDTH), F32), jax.ShapeDtypeStruct((L, 2 * D_MODEL), F32)],
        scratch_shapes=[_residue_scratch(tm)],
        compiler_params=_params(),
    )(x, g, w_in, gate_bias)


def _mix_in_bwd(dx2, x, g, dqkv, du, dgp, w_in):
    L = x.shape[0]
    tm = min(ROW_TILE, L)

    def body(dx2_ref, x_ref, g_ref, *refs):
        piece_refs = refs[:9]
        du_ref, dgp_ref, w_ref, dx1_ref, h_ref, dz_ref, dg_ref, scr = refs[9:]
        i = pl.program_id(0)
        gv = g_ref[...]
        r, xhat = _rms(x_ref[...])
        h_ref[...] = (xhat * gv).astype(BF16)
        for part in range(3):
            for grp, d in enumerate(DILATIONS):
                c0 = part * ATTN_WIDTH + grp * GROUP_WIDTH
                dz_ref[:, c0:c0 + GROUP_WIDTH] = _from_residues(piece_refs[3 * part + grp], scr, d).astype(BF16)
        dz_ref[:, _C_U:_C_G] = du_ref[...].astype(BF16)
        dz_ref[:, _C_G:IN_WIDTH] = dgp_ref[...]
        dh = _dot_nt(dz_ref[...], w_ref[...])

        @pl.when(i == 0)
        def _():
            dg_ref[...] = jnp.zeros_like(dg_ref)

        dg_ref[...] += jnp.sum(dh * xhat, axis=0, keepdims=True)
        dx1_ref[...] = dx2_ref[...] + _rms_bwd(dh, gv, r, xhat)

    return pl.pallas_call(
        body, name="mix_in_bwd", grid=(L // tm,),
        in_specs=[_rows(tm, D_MODEL), _rows(tm, D_MODEL), _whole()] + [_residue_spec(d, tm) for d in DILATIONS] * 3
        + [_rows(tm, SSM_WIDTH), _rows(tm, 2 * D_MODEL), _whole()],
        out_specs=[_rows(tm, D_MODEL), _rows(tm, D_MODEL), _rows(tm, IN_WIDTH), _acc_row(D_MODEL)],
        out_shape=[jax.ShapeDtypeStruct((L, D_MODEL), F32), jax.ShapeDtypeStruct((L, D_MODEL), BF16),
                   jax.ShapeDtypeStruct((L, IN_WIDTH), BF16), jax.ShapeDtypeStruct((1, D_MODEL), F32)],
        scratch_shapes=[_residue_scratch(tm)],
        compiler_params=_params(),
    )(dx2, x, g, *dqkv, du, dgp, w_in)


def _bucket_onehot():
    qi = jnp.arange(ATTN_BLOCK)[:, None]
    kj = jnp.arange(2 * ATTN_BLOCK)[None, :]
    steps = jnp.maximum(qi + ATTN_BLOCK - kj, 0)
    max_exact = N_BUCKETS // 2
    out = []
    for d in DILATIONS:
        dist = steps * d
        df = jnp.maximum(dist, 1).astype(F32)
        large = max_exact + (jnp.log(df / max_exact) / math.log(MAX_DISTANCE / max_exact)
                             * (N_BUCKETS - max_exact)).astype(jnp.int32)
        large = jnp.minimum(large, N_BUCKETS - 1)
        bucket = jnp.where(dist < max_exact, dist, large).reshape(-1)
        out.append((bucket[None, :] == jnp.arange(N_BUCKETS)[:, None]).astype(F32))
    return jnp.stack(out)


def _bias_expand(table_t, onehot):
    n = onehot.shape[-1]

    def body(t_ref, oh_ref, o_ref):
        o_ref[...] = _dot_exact(t_ref[...], oh_ref[...])

    return pl.pallas_call(
        body, name="bias_expand", grid=(3,),
        in_specs=[pl.BlockSpec((None, 8, N_BUCKETS), lambda g: (g, 0, 0)),
                  pl.BlockSpec((None, N_BUCKETS, n), lambda g: (g, 0, 0))],
        out_specs=pl.BlockSpec((None, 8, n), lambda g: (g, 0, 0)),
        out_shape=jax.ShapeDtypeStruct((3, 8, n), F32),
        compiler_params=_params(),
    )(table_t, onehot)


def _bias_reduce(dsum, onehot):
    n = onehot.shape[-1]

    def body(d_ref, oh_ref, o_ref):
        o_ref[...] = _dot_nt_exact(d_ref[...], oh_ref[...])

    return pl.pallas_call(
        body, name="bias_reduce", grid=(3,),
        in_specs=[pl.BlockSpec((None, 8, n), lambda g: (g, 0, 0)),
                  pl.BlockSpec((None, N_BUCKETS, n), lambda g: (g, 0, 0))],
        out_specs=pl.BlockSpec((None, 8, N_BUCKETS), lambda g: (g, 0, 0)),
        out_shape=jax.ShapeDtypeStruct((3, 8, N_BUCKETS), F32),
        compiler_params=_params(),
    )(dsum, onehot)


def _attn_masks(n):
    qi = lax.broadcasted_iota(jnp.int32, (ATTN_BLOCK, 2 * ATTN_BLOCK), 0)
    kj = lax.broadcasted_iota(jnp.int32, (ATTN_BLOCK, 2 * ATTN_BLOCK), 1)
    steps = qi + ATTN_BLOCK - kj
    valid = (steps >= 0) & (steps <= WINDOW_STEPS) & ((n > 0) | (kj >= ATTN_BLOCK))
    head_of_col = lax.broadcasted_iota(jnp.int32, (ATTN_BLOCK, GROUP_WIDTH), 1) // HEAD_DIM
    return valid, head_of_col


def _attn_fwd(q, k, v, bias, name):
    d, M, _ = q.shape
    nb = M // ATTN_BLOCK

    def body(q_ref, kp_ref, kc_ref, vp_ref, vc_ref, bias_ref, o_ref, lse_ref):
        n = pl.program_id(1)
        valid, head_of_col = _attn_masks(n)
        qb = q_ref[...]
        kk = jnp.concatenate([kp_ref[...], kc_ref[...]], axis=0)
        vv = jnp.concatenate([vp_ref[...], vc_ref[...]], axis=0)
        o_acc = jnp.zeros((ATTN_BLOCK, GROUP_WIDTH), F32)
        lse_acc = jnp.zeros((ATTN_BLOCK, GROUP_WIDTH), F32)
        for hh in range(HEADS_PER_GROUP):
            hm = head_of_col == hh
            qh = jnp.where(hm, qb, jnp.zeros_like(qb))
            logits = jnp.where(valid, _dot_nt(qh, kk) + bias_ref[hh], NEG_INF)
            m = jnp.max(logits, axis=1, keepdims=True)
            p = jnp.exp(logits - m)
            den = jnp.sum(p, axis=1, keepdims=True)
            oh = _dot(p.astype(BF16), vv) / den
            o_acc = jnp.where(hm, oh, o_acc)
            lse_acc = jnp.where(hm, m + jnp.log(den), lse_acc)
        o_ref[...] = o_acc
        lse_ref[...] = lse_acc

    blk = (None, ATTN_BLOCK, GROUP_WIDTH)
    cur = pl.BlockSpec(blk, lambda r, n: (r, n, 0))
    prev = pl.BlockSpec(blk, lambda r, n: (r, jnp.maximum(n - 1, 0), 0))
    return pl.pallas_call(
        body, name=name, grid=(d, nb),
        in_specs=[cur, prev, cur, prev, cur, pl.BlockSpec((HEADS_PER_GROUP, ATTN_BLOCK, 2 * ATTN_BLOCK), lambda r, n: (0, 0, 0))],
        out_specs=[cur, cur],
        out_shape=[jax.ShapeDtypeStruct((d, M, GROUP_WIDTH), F32)] * 2,
        compiler_params=_params(),
    )(q, k, k, v, v, bias)


def _attn_bwd(q, k, v, do, lse, delta, bias, name):
    d, M, _ = q.shape
    nb = M // ATTN_BLOCK

    def body(q_ref, kp_ref, kc_ref, vp_ref, vc_ref, do_ref, lse_ref, dl_ref, bias_ref,
             dq_ref, dk_ref, dv_ref, dsum_ref, ck_ref, cv_ref):
        r = pl.program_id(0)
        n = pl.program_id(1)

        @pl.when((r == 0) & (n == 0))
        def _():
            dsum_ref[...] = jnp.zeros_like(dsum_ref)

        @pl.when(n == 0)
        def _():
            ck_ref[...] = jnp.zeros_like(ck_ref)
            cv_ref[...] = jnp.zeros_like(cv_ref)

        @pl.when(n < nb)
        def _():
            valid, head_of_col = _attn_masks(n)
            qb = q_ref[...]
            dob = do_ref[...]
            kk = jnp.concatenate([kp_ref[...], kc_ref[...]], axis=0)
            vv = jnp.concatenate([vp_ref[...], vc_ref[...]], axis=0)
            dq_acc = jnp.zeros((ATTN_BLOCK, GROUP_WIDTH), F32)
            dkk = jnp.zeros((2 * ATTN_BLOCK, GROUP_WIDTH), F32)
            dvv = jnp.zeros((2 * ATTN_BLOCK, GROUP_WIDTH), F32)
            for hh in range(HEADS_PER_GROUP):
                hm = head_of_col == hh
                c0 = hh * HEAD_DIM
                qh = jnp.where(hm, qb, jnp.zeros_like(qb))
                doh = jnp.where(hm, dob, jnp.zeros_like(dob))
                logits = jnp.where(valid, _dot_nt(qh, kk) + bias_ref[hh], NEG_INF)
                p = jnp.exp(logits - lse_ref[:, c0:c0 + 1])
                dp = _dot_nt(doh, vv)
                ds = p * (dp - dl_ref[:, c0:c0 + 1])
                dsum_ref[hh] += ds
                ds16 = ds.astype(BF16)
                dq_acc = jnp.where(hm, _dot(ds16, kk), dq_acc)
                dkk = dkk + _dot_tn(ds16, qh)
                dvv = dvv + _dot_tn(p.astype(BF16), doh)
            dq_ref[...] = (dq_acc * Q_SCALE).astype(BF16)
            dk_ref[...] = (ck_ref[...] + dkk[:ATTN_BLOCK]).astype(BF16)
            dv_ref[...] = (cv_ref[...] + dvv[:ATTN_BLOCK]).astype(BF16)
            ck_ref[...] = dkk[ATTN_BLOCK:]
            cv_ref[...] = dvv[ATTN_BLOCK:]

        @pl.when(n == nb)
        def _():
            dk_ref[...] = ck_ref[...].astype(BF16)
            dv_ref[...] = cv_ref[...].astype(BF16)

    def clamp(n):
        return jnp.minimum(n, nb - 1)

    blk = (None, ATTN_BLOCK, GROUP_WIDTH)
    cur = pl.BlockSpec(blk, lambda r, n: (r, clamp(n), 0))
    prev = pl.BlockSpec(blk, lambda r, n: (r, jnp.maximum(clamp(n) - 1, 0), 0))
    lag = pl.BlockSpec(blk, lambda r, n: (r, jnp.maximum(n - 1, 0), 0))
    full = pl.BlockSpec((HEADS_PER_GROUP, ATTN_BLOCK, 2 * ATTN_BLOCK), lambda r, n: (0, 0, 0))
    carry = pltpu.VMEM((ATTN_BLOCK, GROUP_WIDTH), F32)
    return pl.pallas_call(
        body, name=name, grid=(d, nb + 1),
        in_specs=[cur, prev, cur, prev, cur, cur, cur, cur, full],
        out_specs=[cur, lag, lag, full],
        out_shape=[jax.ShapeDtypeStruct((d, M, GROUP_WIDTH), BF16)] * 3
        + [jax.ShapeDtypeStruct((HEADS_PER_GROUP, ATTN_BLOCK, 2 * ATTN_BLOCK), F32)],
        scratch_shapes=[carry, carry],
        compiler_params=_params(),
    )(q, k, k, v, v, do, lse, delta, bias)


def _disc_math(a_re, a_im, ldt, b_re, b_im):
    dt = jnp.exp(ldt)
    mag = jnp.exp(a_re * dt)
    ab_re = mag * jnp.cos(a_im * dt)
    ab_im = mag * jnp.sin(a_im * dt)
    den = a_re * a_re + a_im * a_im
    xr = ab_re - 1.0
    coef_re = (xr * a_re + ab_im * a_im) / den
    coef_im = (ab_im * a_re - xr * a_im) / den
    return ab_re, ab_im, coef_re * b_re - coef_im * b_im, coef_re * b_im + coef_im * b_re


def _block_diag_mask():
    row_g = lax.broadcasted_iota(jnp.int32, (SSM_WIDTH, 2 * NS), 0) // SSM_GROUP
    col = lax.broadcasted_iota(jnp.int32, (SSM_WIDTH, 2 * NS), 1)
    col_g = jnp.where(col >= NS, col - NS, col) // SSM_STATE
    return row_g == col_g


def _disc_fwd(a_re, a_im, ldt, b_re, b_im, c_re, c_im):
    def body(are_ref, aim_ref, ldt_ref, bre_ref, bim_ref, cre_ref, cim_ref, pw_ref, pwr_ref, bd_ref, cdt_ref):
        ab_re, ab_im, bb_re, bb_im = _disc_math(are_ref[...], aim_ref[...], ldt_ref[...], bre_ref[...], bim_ref[...])
        row = lax.broadcasted_iota(jnp.int32, (8, NS), 0)
        pr, pi = ab_re, ab_im
        t_re = jnp.zeros((8, NS), F32)
        t_im = jnp.zeros((8, NS), F32)
        u_re = jnp.zeros((8, NS), F32)
        u_im = jnp.zeros((8, NS), F32)
        for j in range(8):
            t_re = jnp.where(row == j, pr, t_re)
            t_im = jnp.where(row == j, pi, t_im)
            u_re = jnp.where(row == 7 - j, pr, u_re)
            u_im = jnp.where(row == 7 - j, pi, u_im)
            pr, pi = pr * ab_re - pi * ab_im, pr * ab_im + pi * ab_re
        pw_ref[0] = t_re
        pw_ref[1] = t_im
        pwr_ref[0] = u_re
        pwr_ref[1] = u_im
        mask = _block_diag_mask()
        zero = jnp.zeros((SSM_WIDTH, 2 * NS), F32)
        bfull = jnp.concatenate([jnp.concatenate([bb_re] * SSM_GROUPS, axis=0),
                                 jnp.concatenate([bb_im] * SSM_GROUPS, axis=0)], axis=1)
        bd_ref[...] = jnp.where(mask, bfull, zero).astype(BF16)
        cfull = jnp.concatenate([jnp.concatenate([cre_ref[...]] * SSM_GROUPS, axis=0),
                                 jnp.concatenate([-cim_ref[...]] * SSM_GROUPS, axis=0)], axis=1)
        cdt_ref[...] = jnp.where(mask, cfull, zero).astype(BF16)

    return pl.pallas_call(
        body, name="s5_disc_fwd",
        in_specs=[_whole()] * 7, out_specs=[_whole()] * 4,
        out_shape=[jax.ShapeDtypeStruct((2, 8, NS), F32), jax.ShapeDtypeStruct((2, 8, NS), F32),
                   jax.ShapeDtypeStruct((SSM_WIDTH, 2 * NS), BF16), jax.ShapeDtypeStruct((SSM_WIDTH, 2 * NS), BF16)],
        compiler_params=_params(),
    )(a_re, a_im, ldt, b_re, b_im, c_re, c_im)


def _disc_bwd(a_re, a_im, ldt, b_re, b_im, d_bd, d_cdt, d_ab, group_sum):
    def body(are_ref, aim_ref, ldt_ref, bre_ref, bim_ref, dbd_ref, dcdt_ref, dab_ref, gs_ref,
             dare_ref, daim_ref, dldt_ref, dbre_ref, dbim_ref, dcre_ref, dcim_ref):
        col = lax.broadcasted_iota(jnp.int32, (SSM_GROUP, 2 * NS), 1)
        col_g = jnp.where(col >= NS, col - NS, col) // SSM_STATE
        acc_b = jnp.zeros((SSM_GROUP, 2 * NS), F32)
        acc_c = jnp.zeros((SSM_GROUP, 2 * NS), F32)
        for g in range(SSM_GROUPS):
            rows = slice(g * SSM_GROUP, (g + 1) * SSM_GROUP)
            acc_b = acc_b + jnp.where(col_g == g, dbd_ref[rows, :], 0.0)
            acc_c = acc_c + jnp.where(col_g == g, dcdt_ref[rows, :], 0.0)
        dcre_ref[...] = acc_c[:, :NS]
        dcim_ref[...] = -acc_c[:, NS:]
        dab_re = jnp.sum(dab_ref[0], axis=0, keepdims=True)
        dab_im = jnp.sum(dab_ref[1], axis=0, keepdims=True)
        _, vjp = jax.vjp(_disc_math, are_ref[...], aim_ref[...], ldt_ref[...], bre_ref[...], bim_ref[...])
        d_are, d_aim, d_ldt, d_bre, d_bim = vjp((dab_re, dab_im, acc_b[:, :NS], acc_b[:, NS:]))
        dare_ref[...] = d_are
        daim_ref[...] = d_aim
        dbre_ref[...] = d_bre
        dbim_ref[...] = d_bim
        dldt_ref[...] = _dot_exact(jnp.broadcast_to(d_ldt, (8, NS)), gs_ref[...])

    vec = jax.ShapeDtypeStruct((1, NS), F32)
    mat = jax.ShapeDtypeStruct((SSM_GROUP, NS), F32)
    return pl.pallas_call(
        body, name="s5_disc_bwd",
        in_specs=[_whole()] * 9, out_specs=[_whole()] * 7,
        out_shape=[vec, vec, jax.ShapeDtypeStruct((8, 128), F32), mat, mat, mat, mat],
        compiler_params=_params(),
    )(a_re, a_im, ldt, b_re, b_im, d_bd, d_cdt, d_ab, group_sum)


def _scan_blocks(buf, pw_ref, carry_ref, n_blocks, reverse):
    row = lax.broadcasted_iota(jnp.int32, (8, SCAN_LANES), 0)
    for lc in range(NS // SCAN_LANES):
        re_cols = pl.ds(lc * SCAN_LANES, SCAN_LANES)
        im_cols = pl.ds(NS + lc * SCAN_LANES, SCAN_LANES)
        pr = pw_ref[0, :, re_cols]
        pi = pw_ref[1, :, re_cols]
        if reverse:
            pi = -pi
            base = [(7, 1), (6, 2), (4, 4)]
            coef = [(jnp.where(row < 8 - k, pr[j:j + 1], 0.0), jnp.where(row < 8 - k, pi[j:j + 1], 0.0), 8 - k)
                    for j, k in base]
        else:
            base = [(0, 1), (1, 2), (3, 4)]
            coef = [(jnp.where(row >= k, pr[j:j + 1], 0.0), jnp.where(row >= k, pi[j:j + 1], 0.0), k)
                    for j, k in base]

        def step(i, carry, pr=pr, pi=pi, coef=coef, re_cols=re_cols, im_cols=im_cols):
            cr, ci = carry
            blk = (n_blocks - 1 - i) if reverse else i
            rows = pl.ds(pl.multiple_of(blk * 8, 8), 8)
            xr = buf[rows, re_cols]
            xi = buf[rows, im_cols]
            for kr, ki, shift in coef:
                sr = pltpu.roll(xr, shift, 0)
                si = pltpu.roll(xi, shift, 0)
                xr, xi = xr + kr * sr - ki * si, xi + kr * si + ki * sr
            xr, xi = xr + pr * cr - pi * ci, xi + pr * ci + pi * cr
            buf[rows, re_cols] = xr
            buf[rows, im_cols] = xi
            edge = slice(0, 1) if reverse else slice(7, 8)
            return xr[edge], xi[edge]

        cr, ci = lax.fori_loop(0, n_blocks, step, (carry_ref[0:1, re_cols], carry_ref[0:1, im_cols]))
        carry_ref[0:1, re_cols] = cr
        carry_ref[0:1, im_cols] = ci


_SUPER_GROUPS = 16
_SUPER_BLOCKS = [
    (slice(k * _SUPER_GROUPS * SSM_GROUP, (k + 1) * _SUPER_GROUPS * SSM_GROUP),
     [slice(half + k * _SUPER_GROUPS * SSM_STATE, half + (k + 1) * _SUPER_GROUPS * SSM_STATE) for half in (0, NS)])
    for k in range(SSM_GROUPS // _SUPER_GROUPS)]


def _ssm_fwd(u, bd, cdt, d_skip, pw):
    L = u.shape[0]
    tc = min(SSM_CHUNK, L)

    def body(u_ref, bd_ref, cdt_ref, dsk_ref, pw_ref, y_ref, s_ref, carry_ref):
        @pl.when(pl.program_id(0) == 0)
        def _():
            carry_ref[...] = jnp.zeros_like(carry_ref)

        uv = u_ref[...]
        u16 = uv.astype(BF16)
        for ch, states in _SUPER_BLOCKS:
            for st in states:
                s_ref[:, st] = _dot(u16[:, ch], bd_ref[ch, st])
        _scan_blocks(s_ref, pw_ref, carry_ref, tc // 8, reverse=False)
        for ch, states in _SUPER_BLOCKS:
            y_ref[:, ch] = (sum(_dot_nt(s_ref[:, st].astype(BF16), cdt_ref[ch, st]) for st in states)
                            + dsk_ref[:, ch] * uv[:, ch])

    return pl.pallas_call(
        body, name="s5_fwd", grid=(L // tc,),
        in_specs=[_rows(tc, SSM_WIDTH), _whole(), _whole(), _whole(), _whole()],
        out_specs=[_rows(tc, SSM_WIDTH), _rows(tc, 2 * NS)],
        out_shape=[jax.ShapeDtypeStruct((L, SSM_WIDTH), F32), jax.ShapeDtypeStruct((L, 2 * NS), F32)],
        scratch_shapes=[pltpu.VMEM((8, 2 * NS), F32)],
        compiler_params=_params(),
    )(u, bd, cdt, d_skip, pw)


def _ssm_bwd(dy, u, s, bd, cdt, d_skip, pwr):
    L = u.shape[0]
    tc = min(SSM_CHUNK, L)
    nc = L // tc
    blocks = tc // 8

    def body(dy_ref, u_ref, s_ref, sprev_ref, bd_ref, cdt_ref, dsk_ref, pwr_ref,
             du_ref, ddsk_ref, dbd_ref, dcdt_ref, dab_ref, g_ref, sx_ref, carry_ref):
        i = pl.program_id(0)

        @pl.when(i == 0)
        def _():
            carry_ref[...] = jnp.zeros_like(carry_ref)
            ddsk_ref[...] = jnp.zeros_like(ddsk_ref)
            dbd_ref[...] = jnp.zeros_like(dbd_ref)
            dcdt_ref[...] = jnp.zeros_like(dcdt_ref)
            dab_ref[...] = jnp.zeros_like(dab_ref)

        dyv = dy_ref[...]
        uv = u_ref[...]
        dy16 = dyv.astype(BF16)
        u16 = uv.astype(BF16)
        for ch, states in _SUPER_BLOCKS:
            for st in states:
                g_ref[:, st] = _dot(dy16[:, ch], cdt_ref[ch, st])
        _scan_blocks(g_ref, pwr_ref, carry_ref, blocks, reverse=True)
        ddsk_ref[...] += jnp.sum(dyv * uv, axis=0, keepdims=True)
        for ch, states in _SUPER_BLOCKS:
            du = dsk_ref[:, ch] * dyv[:, ch]
            for st in states:
                g16 = g_ref[:, st].astype(BF16)
                du = du + _dot_nt(g16, bd_ref[ch, st])
                dbd_ref[ch, st] += _dot_tn(u16[:, ch], g16)
                dcdt_ref[ch, st] += _dot_tn(dy16[:, ch], s_ref[:, st].astype(BF16))
            du_ref[:, ch] = du

        sx_ref[pl.ds(8, tc), :] = s_ref[...]
        sx_ref[pl.ds(0, 8), :] = jnp.where(i == nc - 1, 0.0, sprev_ref[...])
        row = lax.broadcasted_iota(jnp.int32, (8, SCAN_LANES), 0)
        for lc in range(NS // SCAN_LANES):
            re_cols = pl.ds(lc * SCAN_LANES, SCAN_LANES)
            im_cols = pl.ds(NS + lc * SCAN_LANES, SCAN_LANES)

            def step(b, acc, re_cols=re_cols, im_cols=im_cols):
                ar, ai = acc
                off = pl.multiple_of(b * 8, 8)
                gr = g_ref[pl.ds(off, 8), re_cols]
                gi = g_ref[pl.ds(off, 8), im_cols]
                before = pl.ds(off, 8)
                here = pl.ds(off + 8, 8)
                sr = jnp.where(row == 0, sx_ref[before, re_cols][7:8], pltpu.roll(sx_ref[here, re_cols], 1, 0))
                si = jnp.where(row == 0, sx_ref[before, im_cols][7:8], pltpu.roll(sx_ref[here, im_cols], 1, 0))
                return ar + gr * sr + gi * si, ai + gi * sr - gr * si

            zero = jnp.zeros((8, SCAN_LANES), F32)
            ar, ai = lax.fori_loop(0, blocks, step, (zero, zero))
            dab_ref[0, :, re_cols] += ar
            dab_ref[1, :, re_cols] += ai

    rev = lambda i: (nc - 1 - i, 0)
    sprev = pl.BlockSpec((8, 2 * NS), lambda i: (jnp.maximum((nc - 1 - i) * blocks - 1, 0), 0))
    return pl.pallas_call(
        body, name="s5_bwd", grid=(nc,),
        in_specs=[pl.BlockSpec((tc, SSM_WIDTH), rev), pl.BlockSpec((tc, SSM_WIDTH), rev),
                  pl.BlockSpec((tc, 2 * NS), rev), sprev, _whole(), _whole(), _whole(), _whole()],
        out_specs=[pl.BlockSpec((tc, SSM_WIDTH), rev), _whole(), _whole(), _whole(), _whole()],
        out_shape=[jax.ShapeDtypeStruct((L, SSM_WIDTH), F32), jax.ShapeDtypeStruct((1, SSM_WIDTH), F32),
                   jax.ShapeDtypeStruct((SSM_WIDTH, 2 * NS), F32), jax.ShapeDtypeStruct((SSM_WIDTH, 2 * NS), F32),
                   jax.ShapeDtypeStruct((2, 8, NS), F32)],
        scratch_shapes=[pltpu.VMEM((tc, 2 * NS), F32), pltpu.VMEM((tc + 8, 2 * NS), F32), pltpu.VMEM((8, 2 * NS), F32)],
        compiler_params=_params(),
    )(dy, u, s, s, bd, cdt, d_skip, pwr)


def _branches(o_attn, y, gates, w_ab, w_glu, w_sb):
    ya = _dot(o_attn.astype(BF16), w_ab[...])
    gel = _gelu(y)
    glu = _dot(gel.astype(BF16), w_glu[...])
    p = glu[:, :SSM_WIDTH]
    sg = _sigmoid(glu[:, SSM_WIDTH:])
    ys2 = p * sg
    ysb = _dot(ys2.astype(BF16), w_sb[...])
    ga = gates[:, :D_MODEL]
    gs = gates[:, D_MODEL:]
    return ya, gel, p, sg, ys2, ysb, ga, gs


def _mix_out_fwd(x1, o_g, lse_g, y, gates, w_ab, w_glu, w_sb, w_out):
    L = x1.shape[0]
    tm = min(ROW_TILE, L)

    def body(x_ref, o0, o1, o2, l0, l1, l2, y_ref, gate_ref, wab_ref, wglu_ref, wsb_ref, wout_ref,
             x2_ref, oat_ref, lse0, lse1, lse2, scr):
        la, lb, lc = (_from_residues(ref, scr, d) for ref, d in zip((l0, l1, l2), DILATIONS))
        m = jnp.maximum(jnp.maximum(la, lb), lc)
        ea, eb, ec = jnp.exp(la - m), jnp.exp(lb - m), jnp.exp(lc - m)
        tot = ea + eb + ec
        oa, ob, oc = (_from_residues(ref, scr, d) for ref, d in zip((o0, o1, o2), DILATIONS))
        o_attn = (ea * oa + eb * ob + ec * oc) / tot
        oat_ref[...] = o_attn
        lse = m + jnp.log(tot)
        for ref, d in zip((lse0, lse1, lse2), DILATIONS):
            _to_residues(lse, ref, scr, d)
        ya, _, _, _, _, ysb, ga, gs = _branches(o_attn, y_ref[...], gate_ref[...], wab_ref, wglu_ref, wsb_ref)
        mix = ga * ya + gs * ysb
        x2_ref[...] = x_ref[...] + _dot(mix.astype(BF16), wout_ref[...])

    res = [_residue_spec(d, tm) for d in DILATIONS]
    return pl.pallas_call(
        body, name="mix_out_fwd", grid=(L // tm,),
        in_specs=[_rows(tm, D_MODEL)] + res * 2 + [_rows(tm, SSM_WIDTH), _rows(tm, 2 * D_MODEL)] + [_whole()] * 4,
        out_specs=[_rows(tm, D_MODEL), _rows(tm, GROUP_WIDTH)] + res,
        out_shape=[jax.ShapeDtypeStruct((L, D_MODEL), F32), jax.ShapeDtypeStruct((L, GROUP_WIDTH), F32)]
        + [_residue_shape(d, L, F32) for d in DILATIONS],
        scratch_shapes=[_residue_scratch(tm)],
        compiler_params=_params(),
    )(x1, *o_g, *lse_g, y, gates, w_ab, w_glu, w_sb, w_out)


def _mix_out_bwd(dx2, o_attn, y, gates, w_ab, w_glu, w_sb, w_out, head_sum):
    L = dx2.shape[0]
    tm = min(ROW_TILE, L)

    def body(dx_ref, oat_ref, y_ref, gate_ref, wab_ref, wglu_ref, wsb_ref, wout_ref, hs_ref,
             do0, do1, do2, dl0, dl1, dl2, dy_ref, dgp_ref, mix_ref, dya_ref, dys_ref, ys2_ref, gel_ref, dglu_ref,
             dgb_ref, scr):
        i = pl.program_id(0)
        o_attn = oat_ref[...]
        yv = y_ref[...]
        ya, gel, p, sg, ys2, ysb, ga, gs = _branches(o_attn, yv, gate_ref[...], wab_ref, wglu_ref, wsb_ref)
        mix_ref[...] = (ga * ya + gs * ysb).astype(BF16)
        ys2_ref[...] = ys2.astype(BF16)
        gel_ref[...] = gel.astype(BF16)
        dmix = _dot_nt(dx_ref[...].astype(BF16), wout_ref[...])
        dgp = jnp.concatenate([dmix * ya * ga * (1.0 - ga), dmix * ysb * gs * (1.0 - gs)], axis=1)
        dgp_ref[...] = dgp.astype(BF16)

        @pl.when(i == 0)
        def _():
            dgb_ref[...] = jnp.zeros_like(dgb_ref)

        dgb_ref[...] += jnp.sum(dgp, axis=0, keepdims=True)
        dya = (dmix * ga).astype(BF16)
        dys = (dmix * gs).astype(BF16)
        dya_ref[...] = dya
        dys_ref[...] = dys
        d_o = _dot_nt(dya, wab_ref[...])
        delta = _dot_exact(d_o * o_attn, hs_ref[...])
        for do_ref, dl_ref, d in zip((do0, do1, do2), (dl0, dl1, dl2), DILATIONS):
            _to_residues(d_o, do_ref, scr, d)
            _to_residues(delta, dl_ref, scr, d)
        dys2 = _dot_nt(dys, wsb_ref[...])
        dglu = jnp.concatenate([dys2 * sg, dys2 * p * sg * (1.0 - sg)], axis=1).astype(BF16)
        dglu_ref[...] = dglu
        dy_ref[...] = _dot_nt(dglu, wglu_ref[...]) * _gelu_grad(yv)

    grp = _rows(tm, GROUP_WIDTH)
    wide = _rows(tm, D_MODEL)
    half = _rows(tm, SSM_WIDTH)
    res = [_residue_spec(d, tm) for d in DILATIONS]
    sds = jax.ShapeDtypeStruct
    return pl.pallas_call(
        body, name="mix_out_bwd", grid=(L // tm,),
        in_specs=[wide, grp, half, _rows(tm, 2 * D_MODEL)] + [_whole()] * 5,
        out_specs=res + res + [half, _rows(tm, 2 * D_MODEL), wide, wide, wide, half, half, wide, _acc_row(2 * D_MODEL)],
        out_shape=[_residue_shape(d, L, BF16) for d in DILATIONS] + [_residue_shape(d, L, F32) for d in DILATIONS]
        + [sds((L, SSM_WIDTH), F32),
           sds((L, 2 * D_MODEL), BF16), sds((L, D_MODEL), BF16), sds((L, D_MODEL), BF16),
           sds((L, D_MODEL), BF16), sds((L, SSM_WIDTH), BF16), sds((L, SSM_WIDTH), BF16),
           sds((L, D_MODEL), BF16), sds((1, 2 * D_MODEL), F32)],
        scratch_shapes=[_residue_scratch(tm)],
        compiler_params=_params(),
    )(dx2, o_attn, y, gates, w_ab, w_glu, w_sb, w_out, head_sum)


def _adamw(w, g, m, v, name):
    R, C = w.shape
    tr = _row_tile(R, max(8, ADAMW_BLOCK_BYTES // (4 * C)))

    def body(w_ref, g_ref, m_ref, v_ref, d_ref, mo_ref, vo_ref):
        gv = g_ref[...]
        mn = ADAM_B1 * m_ref[...] + (1.0 - ADAM_B1) * gv
        vn = ADAM_B2 * v_ref[...] + (1.0 - ADAM_B2) * (gv * gv)
        m_hat = mn / (1.0 - ADAM_B1 ** ADAM_STEP)
        v_hat = vn / (1.0 - ADAM_B2 ** ADAM_STEP)
        d_ref[...] = -ADAM_LR * (m_hat / (jnp.sqrt(v_hat) + ADAM_EPS) + ADAM_WD * w_ref[...])
        mo_ref[...] = mn
        vo_ref[...] = vn

    blk = pl.BlockSpec((tr, C), lambda i: (i, 0))
    return pl.pallas_call(
        body, name=name, grid=(R // tr,),
        in_specs=[blk] * 4, out_specs=[blk] * 3,
        out_shape=[jax.ShapeDtypeStruct((R, C), F32)] * 3,
        compiler_params=_params(),
    )(w, g, m, v)


def _sum_slots(x, name):
    S, R, C = x.shape
    tr = _row_tile(R, 512)

    def body(x_ref, o_ref):
        acc = x_ref[0].astype(F32)
        for k in range(1, S):
            acc = acc + x_ref[k].astype(F32)
        o_ref[...] = acc

    return pl.pallas_call(
        body, name=name, grid=(R // tr,),
        in_specs=[pl.BlockSpec((S, tr, C), lambda i: (0, i, 0))],
        out_specs=pl.BlockSpec((tr, C), lambda i: (i, 0)),
        out_shape=jax.ShapeDtypeStruct((R, C), F32),
        compiler_params=_params(),
    )(x)


def _sum_chips_into_half(u, name):
    S, H, C = u.shape
    tr = _row_tile(H, 512)
    hb = H // tr

    def body(c_ref, u_ref, o_ref):
        acc = u_ref[0].astype(F32)
        for k in range(1, S):
            acc = acc + u_ref[k].astype(F32)
        o_ref[...] = acc

    core = lax.axis_index("c").astype(jnp.int32).reshape(1)
    return pl.pallas_call(
        body, name=name,
        grid_spec=pltpu.PrefetchScalarGridSpec(
            num_scalar_prefetch=1, grid=(hb,),
            in_specs=[pl.BlockSpec((S, tr, C), lambda i, c_ref: (0, i, 0))],
            out_specs=pl.BlockSpec((tr, C), lambda i, c_ref: (c_ref[0] * hb + i, 0))),
        out_shape=jax.ShapeDtypeStruct((2 * H, C), F32),
        compiler_params=_params(),
    )(core, u)


def _add_halves(g, r1, name):
    S, R, C = g.shape
    H = R // 2
    tr = _row_tile(H, 512)
    hb = H // tr

    def body(c_ref, g_ref, r_ref, o_ref):
        o_ref[...] = (g_ref[...] + r_ref[...]).astype(BF16)

    core = lax.axis_index("c").astype(jnp.int32).reshape(1)
    return pl.pallas_call(
        body, name=name,
        grid_spec=pltpu.PrefetchScalarGridSpec(
            num_scalar_prefetch=1, grid=(S, hb),
            in_specs=[pl.BlockSpec((None, tr, C), lambda j, i, c_ref: (j, c_ref[0] * hb + i, 0)),
                      pl.BlockSpec((None, tr, C), lambda j, i, c_ref: (j, i, 0))],
            out_specs=pl.BlockSpec((None, tr, C), lambda j, i, c_ref: (j, i, 0))),
        out_shape=jax.ShapeDtypeStruct((S, H, C), BF16),
        compiler_params=_params(),
    )(core, g, r1)


_ANY = pl.BlockSpec(memory_space=pl.ANY)


def _place():
    x, y, c = lax.axis_index("x"), lax.axis_index("y"), lax.axis_index("c")
    chips = [(1 - x, y), (x, 1 - y), (1 - x, 1 - y)]
    return x, y, c, chips


def _comm_call(body, name, ins, out_shapes, n_remote, n_local):
    return pl.pallas_call(
        body, name=name,
        in_specs=[_ANY] * len(ins), out_specs=[_ANY] * len(out_shapes), out_shape=out_shapes,
        scratch_shapes=[pltpu.SemaphoreType.DMA((n_remote,)), pltpu.SemaphoreType.DMA((n_remote,)),
                        pltpu.SemaphoreType.DMA((max(n_local, 1),))],
    )(*ins)


def _remote(src, dst, send_sems, recv_sems, k, device):
    return pltpu.make_async_remote_copy(src_ref=src, dst_ref=dst, send_sem=send_sems.at[k], recv_sem=recv_sems.at[k],
                                        device_id=device, device_id_type=MESH)


def _gather_weights(shards):
    n = len(shards)

    def body(*refs):
        w_refs, out_refs = refs[:n], refs[n:2 * n]
        send_sems, recv_sems, local_sems = refs[2 * n:]
        x, y, c, chips = _place()
        me = 2 * x + y
        sibling = (x, y, 1 - c)

        def half(k, chip_idx, core):
            H = shards[k].shape[0] // 2
            return out_refs[k].at[chip_idx, pl.ds(core * H, H), :]

        mine = [pltpu.make_async_copy(w_refs[k], out_refs[k].at[me], local_sems.at[k]) for k in range(n)]
        for cp in mine:
            cp.start()
        first = []
        for k in range(n):
            H = shards[k].shape[0] // 2
            for j, (cx, cy) in enumerate(chips):
                first.append(_remote(w_refs[k].at[pl.ds(c * H, H), :], half(k, me, c), send_sems, recv_sems,
                                     3 * k + j, (cx, cy, c)))
        for cp in first:
            cp.start()
        passed = []
        for k in range(n):
            for j, (cx, cy) in enumerate(chips):
                landed = half(k, 2 * cx + cy, c)
                _remote(landed, landed, send_sems, recv_sems, 3 * k + j, (cx, cy, c)).wait_recv()
                fwd = _remote(landed, landed, send_sems, recv_sems, 3 * n + 3 * k + j, sibling)
                fwd.start()
                passed.append(fwd)
        for k in range(n):
            for j, (cx, cy) in enumerate(chips):
                other = half(k, 2 * cx + cy, 1 - c)
                _remote(other, other, send_sems, recv_sems, 3 * n + 3 * k + j, sibling).wait_recv()
        for cp in first + passed:
            cp.wait_send()
        for cp in mine:
            cp.wait()

    return _comm_call(body, "gather_weights", shards,
                      [jax.ShapeDtypeStruct((N_SHARD,) + w.shape, w.dtype) for w in shards], 6 * n, n)


def _swap_halves(gs):
    n = len(gs)

    def body(*refs):
        g_refs, out_refs = refs[:n], refs[n:2 * n]
        send_sems, recv_sems, _ = refs[2 * n:]
        x, y, c, _ = _place()
        cps = []
        for k in range(n):
            H = gs[k].shape[1] // 2
            cp = _remote(g_refs[k].at[:, pl.ds((1 - c) * H, H), :], out_refs[k], send_sems, recv_sems, k, (x, y, 1 - c))
            cp.start()
            cps.append(cp)
        for cp in cps:
            cp.wait()

    return _comm_call(body, "reduce_swap_halves", gs,
                      [jax.ShapeDtypeStruct((g.shape[0], g.shape[1] // 2, g.shape[2]), g.dtype) for g in gs], n, 0)


def _exchange_chips(ts):
    n = len(ts)

    def body(*refs):
        t_refs, out_refs = refs[:n], refs[n:2 * n]
        send_sems, recv_sems, local_sems = refs[2 * n:]
        x, y, c, chips = _place()
        me = 2 * x + y
        mine = [pltpu.make_async_copy(t_refs[k].at[me], out_refs[k].at[me], local_sems.at[k]) for k in range(n)]
        for cp in mine:
            cp.start()
        sent = []
        for k in range(n):
            for j, (cx, cy) in enumerate(chips):
                cp = _remote(t_refs[k].at[2 * cx + cy], out_refs[k].at[me], send_sems, recv_sems, 3 * k + j, (cx, cy, c))
                cp.start()
                sent.append(cp)
        for k in range(n):
            for j, (cx, cy) in enumerate(chips):
                slot = out_refs[k].at[2 * cx + cy]
                _remote(slot, slot, send_sems, recv_sems, 3 * k + j, (cx, cy, c)).wait_recv()
        for cp in sent:
            cp.wait_send()
        for cp in mine:
            cp.wait()

    return _comm_call(body, "reduce_exchange_chips", ts, [jax.ShapeDtypeStruct(t.shape, t.dtype) for t in ts], 3 * n, n)


def _join_halves(fs):
    n = len(fs)

    def body(*refs):
        out_refs = refs[n:2 * n]
        send_sems, recv_sems, _ = refs[2 * n:]
        x, y, c, _ = _place()
        sent = []
        for k in range(n):
            H = fs[k].shape[0] // 2
            here = out_refs[k].at[pl.ds(c * H, H), :]
            cp = _remote(here, here, send_sems, recv_sems, k, (x, y, 1 - c))
            cp.start()
            sent.append(cp)
        for k in range(n):
            H = fs[k].shape[0] // 2
            other = out_refs[k].at[pl.ds((1 - c) * H, H), :]
            _remote(other, other, send_sems, recv_sems, k, (x, y, 1 - c)).wait_recv()
        for cp in sent:
            cp.wait_send()

    return pl.pallas_call(
        body, name="reduce_join_halves",
        in_specs=[_ANY] * n, out_specs=[_ANY] * n,
        out_shape=[jax.ShapeDtypeStruct(f.shape, f.dtype) for f in fs],
        input_output_aliases={k: k for k in range(n)},
        scratch_shapes=[pltpu.SemaphoreType.DMA((n,)), pltpu.SemaphoreType.DMA((n,)), pltpu.SemaphoreType.DMA((1,))],
    )(*fs)


def _gather_small(v):
    R, C = v.shape

    def body(v_ref, out_ref, send_sems, recv_sems, local_sem):
        x, y, c, _ = _place()
        me = 4 * x + 2 * y + c
        mine = pltpu.make_async_copy(v_ref, out_ref.at[me], local_sem)
        mine.start()
        flips = [(fx, fy, fc) for fx in (0, 1) for fy in (0, 1) for fc in (0, 1)][1:]
        peers = [((1 - x) if fx else x, (1 - y) if fy else y, (1 - c) if fc else c) for fx, fy, fc in flips]
        sent = []
        for j, peer in enumerate(peers):
            cp = pltpu.make_async_remote_copy(
                src_ref=v_ref, dst_ref=out_ref.at[me], send_sem=send_sems.at[j], recv_sem=recv_sems.at[j],
                device_id=peer, device_id_type=MESH)
            cp.start()
            sent.append(cp)
        for j, peer in enumerate(peers):
            slot = out_ref.at[4 * peer[0] + 2 * peer[1] + peer[2]]
            pltpu.make_async_remote_copy(
                src_ref=slot, dst_ref=slot, send_sem=send_sems.at[j], recv_sem=recv_sems.at[j],
                device_id=peer, device_id_type=MESH).wait_recv()
        for cp in sent:
            cp.wait_send()
        mine.wait()

    return pl.pallas_call(
        body, name="gather_small",
        in_specs=[_ANY], out_specs=_ANY,
        out_shape=jax.ShapeDtypeStruct((8, R, C), F32),
        scratch_shapes=[pltpu.SemaphoreType.DMA((7,)), pltpu.SemaphoreType.DMA((7,)), pltpu.SemaphoreType.DMA],
    )(v)


def _reduce_scatter(gs, names):
    r1 = _swap_halves(gs)
    ts = [_add_halves(g, r, "reduce_add_cores_" + nm) for g, r, nm in zip(gs, r1, names)]
    us = _exchange_chips(ts)
    fs = [_sum_chips_into_half(u, "reduce_add_chips_" + nm) for u, nm in zip(us, names)]
    return _join_halves(fs)


BIG = ["ffn1_w_gate", "ffn1_w_up", "ffn1_w_down", "w_in", "ssm_w_glu", "w_attn_branch", "w_ssm_branch",
       "w_out", "ffn2_w_gate", "ffn2_w_up", "ffn2_w_down"]
SMALL = ["ffn1_norm", "mix_norm", "gate_bias", "rel_bias_table", "ssm_a_re", "ssm_a_im", "ssm_log_dt",
         "ssm_b_re", "ssm_b_im", "ssm_c_re", "ssm_c_im", "ssm_d", "ffn2_norm", "final_norm"]
ORDER = ["ffn1_norm", "ffn1_w_gate", "ffn1_w_up", "ffn1_w_down", "mix_norm", "w_in", "gate_bias", "rel_bias_table",
         "ssm_a_re", "ssm_a_im", "ssm_log_dt", "ssm_b_re", "ssm_b_im", "ssm_c_re", "ssm_c_im", "ssm_d",
         "ssm_w_glu", "w_attn_branch", "w_ssm_branch", "w_out", "ffn2_norm", "ffn2_w_gate", "ffn2_w_up",
         "ffn2_w_down", "final_norm"]


def _pack_small(arrays):
    rows = []
    for a in arrays:
        flat = a.reshape(-1).astype(F32)
        pad = (-flat.shape[0]) % 128
        rows.append(jnp.pad(flat, (0, pad)).reshape(-1, 128))
    packed = jnp.concatenate(rows, axis=0)
    return jnp.pad(packed, ((0, (-packed.shape[0]) % 8), (0, 0)))


def _unpack_small(packed, shapes):
    out, r0 = [], 0
    for shp in shapes:
        n = math.prod(shp)
        rows = -(-n // 128)
        out.append(packed[r0:r0 + rows].reshape(-1)[:n].reshape(shp))
        r0 += rows
    return out


def _local_step(x, target, w, small):
    L = x.shape[0]
    row = lambda v: v.reshape(1, -1)

    a_re, a_im = small["ssm_a_re"].reshape(1, NS), small["ssm_a_im"].reshape(1, NS)
    ldt = jnp.repeat(small["ssm_log_dt"].reshape(SSM_GROUPS), SSM_STATE).reshape(1, NS)
    to_cn = lambda b: b.reshape(SSM_GROUPS, SSM_STATE, SSM_GROUP).transpose(2, 0, 1).reshape(SSM_GROUP, NS)
    c_to_cn = lambda c: c.reshape(SSM_GROUPS, SSM_GROUP, SSM_STATE).transpose(1, 0, 2).reshape(SSM_GROUP, NS)
    b_re, b_im = to_cn(small["ssm_b_re"]), to_cn(small["ssm_b_im"])
    c_re, c_im = c_to_cn(small["ssm_c_re"]), c_to_cn(small["ssm_c_im"])
    d_skip = row(small["ssm_d"])
    pw, pwr, bd, cdt = _disc_fwd(a_re, a_im, ldt, b_re, b_im, c_re, c_im)

    onehot = _bucket_onehot()
    table_t = small["rel_bias_table"].T.reshape(3, HEADS_PER_GROUP, N_BUCKETS)
    table_t = jnp.pad(table_t, ((0, 0), (0, 8 - HEADS_PER_GROUP), (0, 0)))
    bias = _bias_expand(table_t, onehot)[:, :HEADS_PER_GROUP].reshape(3, HEADS_PER_GROUP, ATTN_BLOCK, 2 * ATTN_BLOCK)

    n1, nm, n2, nf = row(small["ffn1_norm"]), row(small["mix_norm"]), row(small["ffn2_norm"]), row(small["final_norm"])
    gate_bias = row(small["gate_bias"])

    x1, a1, b1 = _ffn_fwd(x, n1, w["ffn1_w_gate"], w["ffn1_w_up"], w["ffn1_w_down"], "ffn1_fwd")
    *qkv, u, gates = _mix_in_fwd(x1, nm, w["w_in"], gate_bias)
    q, k, v = qkv[0:3], qkv[3:6], qkv[6:9]
    o_g, lse_g = [], []
    for grp in range(3):
        o, lse = _attn_fwd(q[grp], k[grp], v[grp], bias[grp], f"attn_fwd_{grp}")
        o_g.append(o)
        lse_g.append(lse)
    y, s = _ssm_fwd(u, bd, cdt, d_skip, pw)
    x2, o_attn, *lse_tot = _mix_out_fwd(x1, o_g, lse_g, y, gates, w["w_attn_branch"], w["ssm_w_glu"],
                                        w["w_ssm_branch"], w["w_out"])
    x3, a2, b2 = _ffn_fwd(x2, n2, w["ffn2_w_gate"], w["ffn2_w_up"], w["ffn2_w_down"], "ffn2_fwd")
    loss_blk, dx3, d_nf = _loss_fwd_bwd(x3, nf, target)

    gw, gs = {}, {}
    gs["final_norm"] = d_nf

    dx2, da, db, sact, h, d_out, gs["ffn2_norm"] = _ffn_bwd(dx3, x2, n2, a2, b2, w["ffn2_w_gate"], w["ffn2_w_up"],
                                                            w["ffn2_w_down"], "ffn2_bwd")
    gw["ffn2_w_gate"] = _matmul_tn(h[None], da, "ffn2_dw_gate")
    gw["ffn2_w_up"] = _matmul_tn(h[None], db, "ffn2_dw_up")
    gw["ffn2_w_down"] = _matmul_tn(sact, d_out[None], "ffn2_dw_down")

    head_sum = (jnp.arange(GROUP_WIDTH)[:, None] // HEAD_DIM == jnp.arange(GROUP_WIDTH)[None, :] // HEAD_DIM).astype(F32)
    (*d_o_delta, dy, dgp, mix, dya, dys, ys2, gel, dglu, gs["gate_bias"]) = _mix_out_bwd(
        dx2, o_attn, y, gates, w["w_attn_branch"], w["ssm_w_glu"], w["w_ssm_branch"], w["w_out"], head_sum)
    d_o, delta = d_o_delta[0:3], d_o_delta[3:6]
    gw["w_out"] = _matmul_tn(mix[None], dx2[None], "dw_out")[0]
    gw["w_attn_branch"] = _matmul_tn(o_attn[None], dya[None], "dw_attn_branch")[0]
    gw["w_ssm_branch"] = _matmul_tn(ys2[None], dys[None], "dw_ssm_branch")[0]
    gw["ssm_w_glu"] = _matmul_tn(gel[None], dglu[None], "dw_glu")[0]

    dqs, dks, dvs, dsums = [], [], [], []
    for grp in range(3):
        dq, dk, dv, dsum = _attn_bwd(q[grp], k[grp], v[grp], d_o[grp], lse_tot[grp], delta[grp], bias[grp],
                                     f"attn_bwd_{grp}")
        dqs.append(dq)
        dks.append(dk)
        dvs.append(dv)
        dsums.append(dsum.reshape(HEADS_PER_GROUP, -1))
    dsum_all = jnp.pad(jnp.stack(dsums), ((0, 0), (0, 8 - HEADS_PER_GROUP), (0, 0)))
    d_table = _bias_reduce(dsum_all, onehot)[:, :HEADS_PER_GROUP]
    gs["rel_bias_table"] = d_table.reshape(3 * HEADS_PER_GROUP, N_BUCKETS).T

    du, gs["ssm_d"], d_bd, d_cdt, d_ab = _ssm_bwd(dy, u, s, bd, cdt, d_skip, pwr)
    group_sum = (jnp.arange(NS)[:, None] // SSM_STATE == jnp.arange(128)[None, :]).astype(F32)
    d_are, d_aim, d_ldt, d_bre, d_bim, d_cre, d_cim = _disc_bwd(a_re, a_im, ldt, b_re, b_im, d_bd, d_cdt, d_ab, group_sum)
    gs["ssm_a_re"], gs["ssm_a_im"] = d_are, d_aim
    gs["ssm_log_dt"] = d_ldt[0, :SSM_GROUPS]
    from_cn = lambda t: t.reshape(SSM_GROUP, SSM_GROUPS, SSM_STATE).transpose(1, 2, 0)
    c_from_cn = lambda t: t.reshape(SSM_GROUP, SSM_GROUPS, SSM_STATE).transpose(1, 0, 2)
    gs["ssm_b_re"], gs["ssm_b_im"] = from_cn(d_bre), from_cn(d_bim)
    gs["ssm_c_re"], gs["ssm_c_im"] = c_from_cn(d_cre), c_from_cn(d_cim)

    dx1, hm, dz, gs["mix_norm"] = _mix_in_bwd(dx2, x1, nm, dqs + dks + dvs, du, dgp, w["w_in"])
    gw["w_in"] = _matmul_tn(hm[None], dz[None], "dw_in")[0]

    dx0, da, db, sact, h, d_out, gs["ffn1_norm"] = _ffn_bwd(dx1, x, n1, a1, b1, w["ffn1_w_gate"], w["ffn1_w_up"],
                                                            w["ffn1_w_down"], "ffn1_bwd")
    gw["ffn1_w_gate"] = _matmul_tn(h[None], da, "ffn1_dw_gate")
    gw["ffn1_w_up"] = _matmul_tn(h[None], db, "ffn1_dw_up")
    gw["ffn1_w_down"] = _matmul_tn(sact, d_out[None], "ffn1_dw_down")
    return loss_blk, dx0, gw, gs


def _split_cols(g):
    K, N = g.shape
    return g.reshape(K, N_SHARD, N // N_SHARD).transpose(1, 0, 2)


def _join_cols(w):
    S, K, n = w.shape
    return w.transpose(1, 0, 2).reshape(K, S * n)


COL_SHARDED = ("w_in", "ssm_w_glu", "w_attn_branch", "w_ssm_branch")


def kernel(x, ffn1_norm, ffn1_w_gate, ffn1_w_up, ffn1_w_down, mix_norm, w_in, gate_bias, rel_bias_table, ssm_a_re, ssm_a_im, ssm_log_dt, ssm_b_re, ssm_b_im, ssm_c_re, ssm_c_im, ssm_d, ssm_w_glu, w_attn_branch, w_ssm_branch, w_out, ffn2_norm, ffn2_w_gate, ffn2_w_up, ffn2_w_down, final_norm, loss_target, m_ffn1_norm, m_ffn1_w_gate, m_ffn1_w_up, m_ffn1_w_down, m_mix_norm, m_w_in, m_gate_bias, m_rel_bias_table, m_ssm_a_re, m_ssm_a_im, m_ssm_log_dt, m_ssm_b_re, m_ssm_b_im, m_ssm_c_re, m_ssm_c_im, m_ssm_d, m_ssm_w_glu, m_w_attn_branch, m_w_ssm_branch, m_w_out, m_ffn2_norm, m_ffn2_w_gate, m_ffn2_w_up, m_ffn2_w_down, m_final_norm, v_ffn1_norm, v_ffn1_w_gate, v_ffn1_w_up, v_ffn1_w_down, v_mix_norm, v_w_in, v_gate_bias, v_rel_bias_table, v_ssm_a_re, v_ssm_a_im, v_ssm_log_dt, v_ssm_b_re, v_ssm_b_im, v_ssm_c_re, v_ssm_c_im, v_ssm_d, v_ssm_w_glu, v_w_attn_branch, v_w_ssm_branch, v_w_out, v_ffn2_norm, v_ffn2_w_gate, v_ffn2_w_up, v_ffn2_w_down, v_final_norm):
    args = dict(locals())
    weights = {n: args[n] for n in ORDER}
    moms = {n: args["m_" + n] for n in ORDER}
    vels = {n: args["v_" + n] for n in ORDER}

    shard2d = {n: weights[n].reshape(weights[n].shape[-2:]) for n in BIG}
    full = dict(zip(BIG, _gather_weights([shard2d[n].astype(BF16) for n in BIG])))
    for n in COL_SHARDED:
        full[n] = _join_cols(full[n])
    full["w_out"] = full["w_out"].reshape(D_MODEL, D_MODEL)

    small = {n: weights[n] for n in SMALL}
    loss_blk, grad_x, gw, gs = _local_step(x[0], loss_target[0], full, small)

    for n in COL_SHARDED:
        gw[n] = _split_cols(gw[n])
    gw["w_out"] = gw["w_out"].reshape(N_SHARD, D_MODEL // N_SHARD, D_MODEL)
    grads = dict(zip(BIG, _reduce_scatter([gw[n] for n in BIG], BIG)))

    small_shapes = [weights[n].shape for n in SMALL]
    mine = _pack_small([gs[n] for n in SMALL] + [loss_blk[0:1, :]])
    total = _sum_slots(_gather_small(mine), "sum_small")
    small_grads = _unpack_small(total, small_shapes + [(128,)])
    loss = small_grads[-1][0]
    for n, g in zip(SMALL, small_grads[:-1]):
        grads[n] = g

    delta, new_m, new_v = {}, {}, {}
    for n in BIG:
        d, m, v = _adamw(shard2d[n], grads[n], moms[n].reshape(shard2d[n].shape), vels[n].reshape(shard2d[n].shape),
                         "adamw_" + n)
        shp = weights[n].shape
        delta[n], new_m[n], new_v[n] = d.reshape(shp), m.reshape(shp), v.reshape(shp)
        grads[n] = grads[n].reshape(shp)
    d, m, v = _adamw(_pack_small([weights[n] for n in SMALL]), _pack_small([grads[n] for n in SMALL]),
                     _pack_small([moms[n] for n in SMALL]), _pack_small([vels[n] for n in SMALL]), "adamw_small")
    for n, dd, mm, vv in zip(SMALL, _unpack_small(d, small_shapes), _unpack_small(m, small_shapes),
                             _unpack_small(v, small_shapes)):
        delta[n], new_m[n], new_v[n] = dd, mm, vv

    return (loss, grad_x[None], *[grads[n] for n in ORDER], *[delta[n] for n in ORDER],
            *[new_m[n] for n in ORDER], *[new_v[n] for n in ORDER])
```

```python
import functools
import math

import jax
import jax.numpy as jnp
from jax import lax
from jax.experimental import pallas as pl
from jax.experimental.pallas import tpu as pltpu

F32 = jnp.float32
BF16 = jnp.bfloat16
MESH = pl.DeviceIdType.MESH

D_MODEL = 1024
D_FF = 2816
HEAD_DIM = 64
HEADS_PER_GROUP = 4
DILATIONS = (1, 4, 16)
WINDOW_STEPS = 128
ATTN_BLOCK = 128
ATTN_QB = 4
GROUP_WIDTH = HEADS_PER_GROUP * HEAD_DIM
ATTN_WIDTH = 3 * GROUP_WIDTH
N_BUCKETS = 32
MAX_DISTANCE = 2048
NEG_INF = -1e30
SSM_WIDTH = 512
SSM_GROUP = 16
SSM_GROUPS = 32
SSM_STATE = 64
NS = SSM_GROUPS * SSM_STATE
EPS = 1e-6
IN_WIDTH = 3 * ATTN_WIDTH + SSM_WIDTH + 2 * D_MODEL
Q_SCALE = HEAD_DIM ** -0.5
N_SHARD = 4
FF_SHARD = D_FF // N_SHARD
ADAM_LR, ADAM_B1, ADAM_B2, ADAM_EPS, ADAM_WD, ADAM_STEP = 0.001, 0.9, 0.999, 1e-08, 0.01, 10

LANES = 128
VMEM_LIMIT = 56 * 1024 * 1024
ROW_TILE = 512
FFN_BWD_TILE = 256
SSM_CHUNK = 256
SCAN_LANES = 512
ADAMW_BLOCK_BYTES = 1 << 20
TN_VMEM_BUDGET = 40 * 1024 * 1024


def _params(**kw):
    return pltpu.CompilerParams(vmem_limit_bytes=VMEM_LIMIT, **kw)


def _dot(a, b):
    return jnp.dot(a, b, preferred_element_type=F32)


def _dot_nt(a, b):
    return lax.dot_general(a, b, (((1,), (1,)), ((), ())), preferred_element_type=F32)


def _dot_tn(a, b):
    return lax.dot_general(a, b, (((0,), (0,)), ((), ())), preferred_element_type=F32)


def _dot_exact(a, b):
    return jnp.dot(a, b, preferred_element_type=F32, precision=lax.Precision.HIGHEST)


def _dot_nt_exact(a, b):
    return lax.dot_general(a, b, (((1,), (1,)), ((), ())), preferred_element_type=F32,
                           precision=lax.Precision.HIGHEST)


def _rms(x):
    r = lax.rsqrt(jnp.mean(x * x, axis=-1, keepdims=True) + EPS)
    return r, x * r


def _rms_bwd(dh, g, r, xhat):
    dxh = dh * g
    return r * (dxh - xhat * jnp.mean(dxh * xhat, axis=-1, keepdims=True))


def _sigmoid(x):
    return 1.0 / (1.0 + jnp.exp(-x))


_GELU_C = math.sqrt(2.0 / math.pi)


def _gelu(x):
    return 0.5 * x * (1.0 + jnp.tanh(_GELU_C * (x + 0.044715 * x * x * x)))


def _gelu_grad(x):
    t = jnp.tanh(_GELU_C * (x + 0.044715 * x * x * x))
    return 0.5 * (1.0 + t) + 0.5 * x * (1.0 - t * t) * _GELU_C * (1.0 + 3 * 0.044715 * x * x)


def _whole():
    return pl.BlockSpec(memory_space=pltpu.VMEM)


def _row_tile(rows, cap):
    if rows <= cap:
        return rows
    return max(t for t in range(8, cap + 1, 8) if rows % t == 0)


def _rows(tm, w):
    return pl.BlockSpec((tm, w), lambda i: (i, 0))


def _acc_row(w):
    return pl.BlockSpec((1, w), lambda i: (0, 0))


def _ffn_fwd(x, g, wg, wu, wd, name):
    L = x.shape[0]
    tm = min(ROW_TILE, L)

    def body(x_ref, g_ref, wg_ref, wu_ref, wd_ref, xo_ref, a_ref, b_ref):
        xv = x_ref[...]
        r, xhat = _rms(xv)
        h = (xhat * g_ref[...]).astype(BF16)
        acc = jnp.zeros((tm, D_MODEL), F32)
        for j in range(N_SHARD):
            a = _dot(h, wg_ref[j])
            b = _dot(h, wu_ref[j])
            a_ref[j] = a.astype(BF16)
            b_ref[j] = b.astype(BF16)
            s = (a * _sigmoid(a) * b).astype(BF16)
            acc = acc + _dot(s, wd_ref[j])
        xo_ref[...] = xv + 0.5 * acc

    act = pl.BlockSpec((N_SHARD, tm, FF_SHARD), lambda i: (0, i, 0))
    return pl.pallas_call(
        body, name=name, grid=(L // tm,),
        in_specs=[_rows(tm, D_MODEL), _whole(), _whole(), _whole(), _whole()],
        out_specs=[_rows(tm, D_MODEL), act, act],
        out_shape=[jax.ShapeDtypeStruct((L, D_MODEL), F32),
                   jax.ShapeDtypeStruct((N_SHARD, L, FF_SHARD), BF16),
                   jax.ShapeDtypeStruct((N_SHARD, L, FF_SHARD), BF16)],
        compiler_params=_params(),
    )(x, g, wg, wu, wd)


def _ffn_bwd(dxo, x, g, a, b, wg, wu, wd, name):
    L = x.shape[0]
    tm = min(FFN_BWD_TILE, L)

    def body(dxo_ref, x_ref, g_ref, a_ref, b_ref, wg_ref, wu_ref, wd_ref,
             dxi_ref, da_ref, db_ref, s_ref, h_ref, do_ref, dg_ref):
        i = pl.program_id(0)
        xv = x_ref[...]
        gv = g_ref[...]
        r, xhat = _rms(xv)
        h_ref[...] = (xhat * gv).astype(BF16)
        dxo_v = dxo_ref[...]
        d_out = (0.5 * dxo_v).astype(BF16)
        do_ref[...] = d_out
        dh = jnp.zeros((tm, D_MODEL), F32)
        for j in range(N_SHARD):
            av = a_ref[j].astype(F32)
            bv = b_ref[j].astype(F32)
            sg = _sigmoid(av)
            sl = av * sg
            ds = _dot_nt(d_out, wd_ref[j])
            dbv = (ds * sl).astype(BF16)
            dav = (ds * bv * (sg * (1.0 + av * (1.0 - sg)))).astype(BF16)
            da_ref[j] = dav
            db_ref[j] = dbv
            s_ref[j] = (sl * bv).astype(BF16)
            dh = dh + _dot_nt(dav, wg_ref[j]) + _dot_nt(dbv, wu_ref[j])

        @pl.when(i == 0)
        def _():
            dg_ref[...] = jnp.zeros_like(dg_ref)

        dg_ref[...] += jnp.sum(dh * xhat, axis=0, keepdims=True)
        dxi_ref[...] = dxo_v + _rms_bwd(dh, gv, r, xhat)

    act = pl.BlockSpec((N_SHARD, tm, FF_SHARD), lambda i: (0, i, 0))
    act_shape = jax.ShapeDtypeStruct((N_SHARD, L, FF_SHARD), BF16)
    return pl.pallas_call(
        body, name=name, grid=(L // tm,),
        in_specs=[_rows(tm, D_MODEL), _rows(tm, D_MODEL), _whole(), act, act, _whole(), _whole(), _whole()],
        out_specs=[_rows(tm, D_MODEL), act, act, act, _rows(tm, D_MODEL), _rows(tm, D_MODEL), _acc_row(D_MODEL)],
        out_shape=[jax.ShapeDtypeStruct((L, D_MODEL), F32), act_shape, act_shape, act_shape,
                   jax.ShapeDtypeStruct((L, D_MODEL), BF16), jax.ShapeDtypeStruct((L, D_MODEL), BF16),
                   jax.ShapeDtypeStruct((1, D_MODEL), F32)],
        compiler_params=_params(),
    )(dxo, x, g, a, b, wg, wu, wd)


def _matmul_tn(a, b, name):
    ja, L, K = a.shape
    jb, _, N = b.shape
    J = max(ja, jb)
    splits = [s for s in (1, 2, 4, 8) if s == 1 or N % (s * LANES) == 0]
    nsplit = next((s for s in splits if 2 * K * (N // s) * 4 <= TN_VMEM_BUDGET // 2), splits[-1])
    nc = N // nsplit
    left = TN_VMEM_BUDGET - 2 * K * nc * 4
    row_bytes = 2 * (K * a.dtype.itemsize + nc * b.dtype.itemsize)
    tm = next((t for t in (2048, 1024, 512, 256) if L % t == 0 and t * row_bytes <= left), min(128, L))

    def body(a_ref, b_ref, o_ref):
        @pl.when(pl.program_id(2) == 0)
        def _():
            o_ref[...] = jnp.zeros_like(o_ref)

        o_ref[...] += _dot_tn(a_ref[...].astype(BF16), b_ref[...].astype(BF16))

    return pl.pallas_call(
        body, name=name, grid=(J, nsplit, L // tm),
        in_specs=[pl.BlockSpec((None, tm, K), (lambda j, s, i: (j, i, 0)) if ja > 1 else (lambda j, s, i: (0, i, 0))),
                  pl.BlockSpec((None, tm, nc), (lambda j, s, i: (j, i, s)) if jb > 1 else (lambda j, s, i: (0, i, s)))],
        out_specs=pl.BlockSpec((None, K, nc), lambda j, s, i: (j, 0, s)),
        out_shape=jax.ShapeDtypeStruct((J, K, N), F32),
        compiler_params=_params(),
    )(a, b)


def _loss_fwd_bwd(x, g, target):
    L = x.shape[0]
    tm = min(ROW_TILE, L)

    def body(x_ref, g_ref, t_ref, loss_ref, dx_ref, dg_ref):
        i = pl.program_id(0)
        xv = x_ref[...]
        gv = g_ref[...]
        r, xhat = _rms(xv)
        err = xhat * gv - t_ref[...]
        part = 0.5 * jnp.sum(jnp.sum(err * err, axis=1, keepdims=True) * (1.0 / D_MODEL), axis=0, keepdims=True)
        dy = err * (1.0 / D_MODEL)

        @pl.when(i == 0)
        def _():
            dg_ref[...] = jnp.zeros_like(dg_ref)
            loss_ref[...] = jnp.zeros_like(loss_ref)

        loss_ref[...] += jnp.broadcast_to(part, loss_ref.shape)
        dg_ref[...] += jnp.sum(dy * xhat, axis=0, keepdims=True)
        dx_ref[...] = _rms_bwd(dy, gv, r, xhat)

    return pl.pallas_call(
        body, name="loss_fwd_bwd", grid=(L // tm,),
        in_specs=[_rows(tm, D_MODEL), _whole(), _rows(tm, D_MODEL)],
        out_specs=[pl.BlockSpec((8, 128), lambda i: (0, 0)), _rows(tm, D_MODEL), _acc_row(D_MODEL)],
        out_shape=[jax.ShapeDtypeStruct((8, 128), F32), jax.ShapeDtypeStruct((L, D_MODEL), F32),
                   jax.ShapeDtypeStruct((1, D_MODEL), F32)],
        compiler_params=_params(),
    )(x, g, target)


_C_K = ATTN_WIDTH
_C_V = 2 * ATTN_WIDTH
_C_U = 3 * ATTN_WIDTH
_C_G = _C_U + SSM_WIDTH


def _residue_spec(d, tm):
    return pl.BlockSpec((d, tm // d, GROUP_WIDTH), lambda i: (0, i, 0))


def _residue_shape(d, L, dtype):
    return jax.ShapeDtypeStruct((d, L // d, GROUP_WIDTH), dtype)


def _residue_scratch(tm):
    return pltpu.VMEM((GROUP_WIDTH // LANES, tm, LANES), F32)


def _to_residues(val, out_ref, scr, d):
    if d == 1:
        out_ref[0] = val.astype(out_ref.dtype)
        return
    tm = val.shape[0]
    for half in range(GROUP_WIDTH // LANES):
        cols = slice(half * LANES, (half + 1) * LANES)
        scr[half] = val[:, cols]
        for r in range(d):
            out_ref[r, :, cols] = scr[half, pl.ds(r, tm // d, stride=d), :].astype(out_ref.dtype)


def _from_residues(ref, scr, d):
    if d == 1:
        return ref[0].astype(F32)
    rows = ref.shape[1]
    for half in range(GROUP_WIDTH // LANES):
        cols = slice(half * LANES, (half + 1) * LANES)
        for r in range(d):
            scr[half, pl.ds(r, rows, stride=d), :] = ref[r, :, cols].astype(F32)
    return jnp.concatenate([scr[half] for half in range(GROUP_WIDTH // LANES)], axis=1)


def _mix_in_fwd(x, g, w_in, gate_bias):
    L = x.shape[0]
    tm = min(ROW_TILE, L)

    def body(x_ref, g_ref, w_ref, gb_ref, *refs):
        qkv_refs, (u_ref, gate_ref, scr) = refs[:9], refs[9:]
        r, xhat = _rms(x_ref[...])
        h = (xhat * g_ref[...]).astype(BF16)
        for part, (c0, scale) in enumerate(((0, Q_SCALE), (_C_K, 1.0), (_C_V, 1.0))):
            z = _dot(h, w_ref[:, c0:c0 + ATTN_WIDTH]) * scale
            for grp, d in enumerate(DILATIONS):
                _to_residues(z[:, grp * GROUP_WIDTH:(grp + 1) * GROUP_WIDTH], qkv_refs[3 * part + grp], scr, d)
        u_ref[...] = _dot(h, w_ref[:, _C_U:_C_G])
        gate_ref[...] = _sigmoid(_dot(h, w_ref[:, _C_G:IN_WIDTH]) + gb_ref[...])

    return pl.pallas_call(
        body, name="mix_in_fwd", grid=(L // tm,),
        in_specs=[_rows(tm, D_MODEL), _whole(), _whole(), _whole()],
        out_specs=[_residue_spec(d, tm) for d in DILATIONS] * 3 + [_rows(tm, SSM_WIDTH), _rows(tm, 2 * D_MODEL)],
        out_shape=[_residue_shape(d, L, BF16) for d in DILATIONS] * 3
        + [jax.ShapeDtypeStruct((L, SSM_WIDTH), F32), jax.ShapeDtypeStruct((L, 2 * D_MODEL), F32)],
        scratch_shapes=[_residue_scratch(tm)],
        compiler_params=_params(),
    )(x, g, w_in, gate_bias)


def _mix_in_bwd(dx2, x, g, dqkv, du, dgp, w_in):
    L = x.shape[0]
    tm = min(ROW_TILE, L)

    def body(dx2_ref, x_ref, g_ref, *refs):
        piece_refs = refs[:9]
        du_ref, dgp_ref, w_ref, dx1_ref, h_ref, dz_ref, dg_ref, scr = refs[9:]
        i = pl.program_id(0)
        gv = g_ref[...]
        r, xhat = _rms(x_ref[...])
        h_ref[...] = (xhat * gv).astype(BF16)
        for part in range(3):
            for grp, d in enumerate(DILATIONS):
                c0 = part * ATTN_WIDTH + grp * GROUP_WIDTH
                dz_ref[:, c0:c0 + GROUP_WIDTH] = _from_residues(piece_refs[3 * part + grp], scr, d).astype(BF16)
        dz_ref[:, _C_U:_C_G] = du_ref[...].astype(BF16)
        dz_ref[:, _C_G:IN_WIDTH] = dgp_ref[...]
        dh = _dot_nt(dz_ref[...], w_ref[...])

        @pl.when(i == 0)
        def _():
            dg_ref[...] = jnp.zeros_like(dg_ref)

        dg_ref[...] += jnp.sum(dh * xhat, axis=0, keepdims=True)
        dx1_ref[...] = dx2_ref[...] + _rms_bwd(dh, gv, r, xhat)

    return pl.pallas_call(
        body, name="mix_in_bwd", grid=(L // tm,),
        in_specs=[_rows(tm, D_MODEL), _rows(tm, D_MODEL), _whole()] + [_residue_spec(d, tm) for d in DILATIONS] * 3
        + [_rows(tm, SSM_WIDTH), _rows(tm, 2 * D_MODEL), _whole()],
        out_specs=[_rows(tm, D_MODEL), _rows(tm, D_MODEL), _rows(tm, IN_WIDTH), _acc_row(D_MODEL)],
        out_shape=[jax.ShapeDtypeStruct((L, D_MODEL), F32), jax.ShapeDtypeStruct((L, D_MODEL), BF16),
                   jax.ShapeDtypeStruct((L, IN_WIDTH), BF16), jax.ShapeDtypeStruct((1, D_MODEL), F32)],
        scratch_shapes=[_residue_scratch(tm)],
        compiler_params=_params(),
    )(dx2, x, g, *dqkv, du, dgp, w_in)


def _bucket_onehot():
    qi = jnp.arange(ATTN_BLOCK)[:, None]
    kj = jnp.arange(2 * ATTN_BLOCK)[None, :]
    steps = jnp.maximum(qi + ATTN_BLOCK - kj, 0)
    max_exact = N_BUCKETS // 2
    out = []
    for d in DILATIONS:
        dist = steps * d
        df = jnp.maximum(dist, 1).astype(F32)
        large = max_exact + (jnp.log(df / max_exact) / math.log(MAX_DISTANCE / max_exact)
                             * (N_BUCKETS - max_exact)).astype(jnp.int32)
        large = jnp.minimum(large, N_BUCKETS - 1)
        bucket = jnp.where(dist < max_exact, dist, large).reshape(-1)
        out.append((bucket[None, :] == jnp.arange(N_BUCKETS)[:, None]).astype(F32))
    return jnp.stack(out)


def _bias_expand(table_t, onehot):
    n = onehot.shape[-1]

    def body(t_ref, oh_ref, o_ref):
        bias = _dot_exact(t_ref[...], oh_ref[...])
        col = lax.broadcasted_iota(jnp.int32, (8, n), 1)
        qi = col // (2 * ATTN_BLOCK)
        kj = col - qi * (2 * ATTN_BLOCK)
        steps = qi + ATTN_BLOCK - kj
        band = (steps >= 0) & (steps <= WINDOW_STEPS)
        o_ref[0] = jnp.where(band & (kj >= ATTN_BLOCK), bias, NEG_INF)
        o_ref[1] = jnp.where(band, bias, NEG_INF)

    return pl.pallas_call(
        body, name="bias_expand", grid=(3,),
        in_specs=[pl.BlockSpec((None, 8, N_BUCKETS), lambda g: (g, 0, 0)),
                  pl.BlockSpec((None, N_BUCKETS, n), lambda g: (g, 0, 0))],
        out_specs=pl.BlockSpec((None, 2, 8, n), lambda g: (g, 0, 0, 0)),
        out_shape=jax.ShapeDtypeStruct((3, 2, 8, n), F32),
        compiler_params=_params(),
    )(table_t, onehot)


def _bias_reduce(dsum, onehot):
    n = onehot.shape[-1]

    def body(d_ref, oh_ref, o_ref):
        o_ref[...] = _dot_nt_exact(d_ref[...], oh_ref[...])

    return pl.pallas_call(
        body, name="bias_reduce", grid=(3,),
        in_specs=[pl.BlockSpec((None, 8, n), lambda g: (g, 0, 0)),
                  pl.BlockSpec((None, N_BUCKETS, n), lambda g: (g, 0, 0))],
        out_specs=pl.BlockSpec((None, 8, N_BUCKETS), lambda g: (g, 0, 0)),
        out_shape=jax.ShapeDtypeStruct((3, 8, N_BUCKETS), F32),
        compiler_params=_params(),
    )(dsum, onehot)


def _head_of_col(rows):
    return lax.broadcasted_iota(jnp.int32, (rows, GROUP_WIDTH), 1) // HEAD_DIM


def _attn_specs(qb):
    rows = qb * ATTN_BLOCK
    cur = pl.BlockSpec((None, rows, GROUP_WIDTH), lambda r, n: (r, n, 0))
    prev = pl.BlockSpec((None, ATTN_BLOCK, GROUP_WIDTH), lambda r, n: (r, jnp.maximum(n * qb - 1, 0), 0))
    bias = pl.BlockSpec((2, HEADS_PER_GROUP, ATTN_BLOCK, 2 * ATTN_BLOCK), lambda r, n: (0, 0, 0, 0))
    return cur, prev, bias


def _attn_fwd(q, k, v, bias, name):
    d, M, _ = q.shape
    nb = M // ATTN_BLOCK
    qb = min(ATTN_QB, nb)

    def body(q_ref, kp_ref, kc_ref, vp_ref, vc_ref, bias_ref, o_ref, lse_ref):
        n = pl.program_id(1)
        q_head = _head_of_col(ATTN_BLOCK)
        kv_head = _head_of_col(2 * ATTN_BLOCK)
        kwin = jnp.concatenate([kp_ref[...], kc_ref[...]], axis=0)
        vwin = jnp.concatenate([vp_ref[...], vc_ref[...]], axis=0)
        for b in range(qb):
            rows = slice(b * ATTN_BLOCK, (b + 1) * ATTN_BLOCK)
            window = slice(b * ATTN_BLOCK, (b + 2) * ATTN_BLOCK)
            variant = jnp.minimum(n, 1) if b == 0 else 1
            qv = q_ref[rows, :]
            kk = kwin[window]
            vv = vwin[window]
            o_acc = jnp.zeros((ATTN_BLOCK, GROUP_WIDTH), F32)
            lse_acc = jnp.zeros((ATTN_BLOCK, GROUP_WIDTH), F32)
            for hh in range(HEADS_PER_GROUP):
                hm = q_head == hh
                qh = jnp.where(hm, qv, jnp.zeros_like(qv))
                logits = _dot_nt(qh, kk) + bias_ref[variant, hh]
                m = jnp.max(logits, axis=1, keepdims=True)
                p = jnp.exp(logits - m)
                vh = jnp.where(kv_head == hh, vv, jnp.ones_like(vv))
                pv = _dot(p.astype(BF16), vh)
                c_sum = ((hh + 1) % HEADS_PER_GROUP) * HEAD_DIM
                den = pv[:, c_sum:c_sum + 1]
                o_acc = jnp.where(hm, pv * (1.0 / den), o_acc)
                lse_acc = jnp.where(hm, m + jnp.log(den), lse_acc)
            o_ref[rows, :] = o_acc
            lse_ref[rows, :] = lse_acc

    cur, prev, full = _attn_specs(qb)
    return pl.pallas_call(
        body, name=name, grid=(d, nb // qb),
        in_specs=[cur, prev, cur, prev, cur, full],
        out_specs=[cur, cur],
        out_shape=[jax.ShapeDtypeStruct((d, M, GROUP_WIDTH), F32)] * 2,
        compiler_params=_params(),
    )(q, k, k, v, v, bias)


def _attn_bwd(q, k, v, do, lse, delta, bias, name):
    d, M, _ = q.shape
    nb = M // ATTN_BLOCK
    qb = min(ATTN_QB, nb)
    ns = nb // qb
    rows_q = qb * ATTN_BLOCK
    last = slice(rows_q - ATTN_BLOCK, rows_q)

    def body(q_ref, kp_ref, kc_ref, vp_ref, vc_ref, do_ref, lse_ref, dl_ref, bias_ref,
             dq_ref, dk_ref, dv_ref, dsum_ref, pk_ref, pv_ref, wk_ref, wv_ref):
        r = pl.program_id(0)
        n = pl.program_id(1)

        @pl.when((r == 0) & (n == 0))
        def _():
            dsum_ref[...] = jnp.zeros_like(dsum_ref)

        @pl.when(n == 0)
        def _():
            pk_ref[...] = jnp.zeros_like(pk_ref)
            pv_ref[...] = jnp.zeros_like(pv_ref)

        @pl.when(n < ns)
        def _():
            q_head = _head_of_col(ATTN_BLOCK)
            kwin = jnp.concatenate([kp_ref[...], kc_ref[...]], axis=0)
            vwin = jnp.concatenate([vp_ref[...], vc_ref[...]], axis=0)
            wk_ref[...] = jnp.zeros_like(wk_ref)
            wv_ref[...] = jnp.zeros_like(wv_ref)
            for b in range(qb):
                rows = slice(b * ATTN_BLOCK, (b + 1) * ATTN_BLOCK)
                window = slice(b * ATTN_BLOCK, (b + 2) * ATTN_BLOCK)
                variant = jnp.minimum(n, 1) if b == 0 else 1
                qv = q_ref[rows, :]
                dov = do_ref[rows, :]
                kk = kwin[window]
                vv = vwin[window]
                dq_acc = jnp.zeros((ATTN_BLOCK, GROUP_WIDTH), F32)
                dkk = jnp.zeros((2 * ATTN_BLOCK, GROUP_WIDTH), F32)
                dvv = jnp.zeros((2 * ATTN_BLOCK, GROUP_WIDTH), F32)
                for hh in range(HEADS_PER_GROUP):
                    hm = q_head == hh
                    c0 = hh * HEAD_DIM
                    qh = jnp.where(hm, qv, jnp.zeros_like(qv))
                    doh = jnp.where(hm, dov, jnp.zeros_like(dov))
                    logits = _dot_nt(qh, kk) + bias_ref[variant, hh]
                    p = jnp.exp(logits - lse_ref[rows, c0:c0 + 1])
                    dp = _dot_nt(doh, vv)
                    ds = p * (dp - dl_ref[rows, c0:c0 + 1])
                    dsum_ref[hh] += ds
                    ds16 = ds.astype(BF16)
                    dq_acc = jnp.where(hm, _dot(ds16, kk), dq_acc)
                    dkk = dkk + _dot_tn(ds16, qh)
                    dvv = dvv + _dot_tn(p.astype(BF16), doh)
                dq_ref[rows, :] = (dq_acc * Q_SCALE).astype(BF16)
                wk_ref[window, :] += dkk
                wv_ref[window, :] += dvv
            for out_ref, part_ref, win_ref in ((dk_ref, pk_ref, wk_ref), (dv_ref, pv_ref, wv_ref)):
                if qb > 1:
                    out_ref[0:rows_q - ATTN_BLOCK, :] = part_ref[0:rows_q - ATTN_BLOCK, :].astype(BF16)
                out_ref[last, :] = (part_ref[last, :] + win_ref[0:ATTN_BLOCK, :]).astype(BF16)
                part_ref[...] = win_ref[ATTN_BLOCK:, :]

        @pl.when(n == ns)
        def _():
            dk_ref[...] = pk_ref[...].astype(BF16)
            dv_ref[...] = pv_ref[...].astype(BF16)

    def clamp(n):
        return jnp.minimum(n, ns - 1)

    cur = pl.BlockSpec((None, rows_q, GROUP_WIDTH), lambda r, n: (r, clamp(n), 0))
    prev = pl.BlockSpec((None, ATTN_BLOCK, GROUP_WIDTH), lambda r, n: (r, jnp.maximum(clamp(n) * qb - 1, 0), 0))
    lag = pl.BlockSpec((None, rows_q, GROUP_WIDTH), lambda r, n: (r, jnp.maximum(n - 1, 0), 0))
    full = pl.BlockSpec((2, HEADS_PER_GROUP, ATTN_BLOCK, 2 * ATTN_BLOCK), lambda r, n: (0, 0, 0, 0))
    acc = pl.BlockSpec((HEADS_PER_GROUP, ATTN_BLOCK, 2 * ATTN_BLOCK), lambda r, n: (0, 0, 0))
    return pl.pallas_call(
        body, name=name, grid=(d, ns + 1),
        in_specs=[cur, prev, cur, prev, cur, cur, cur, cur, full],
        out_specs=[cur, lag, lag, acc],
        out_shape=[jax.ShapeDtypeStruct((d, M, GROUP_WIDTH), BF16)] * 3
        + [jax.ShapeDtypeStruct((HEADS_PER_GROUP, ATTN_BLOCK, 2 * ATTN_BLOCK), F32)],
        scratch_shapes=[pltpu.VMEM((rows_q, GROUP_WIDTH), F32), pltpu.VMEM((rows_q, GROUP_WIDTH), F32),
                        pltpu.VMEM((rows_q + ATTN_BLOCK, GROUP_WIDTH), F32),
                        pltpu.VMEM((rows_q + ATTN_BLOCK, GROUP_WIDTH), F32)],
        compiler_params=_params(),
    )(q, k, k, v, v, do, lse, delta, bias)


def _disc_math(a_re, a_im, ldt, b_re, b_im):
    dt = jnp.exp(ldt)
    mag = jnp.exp(a_re * dt)
    ab_re = mag * jnp.cos(a_im * dt)
    ab_im = mag * jnp.sin(a_im * dt)
    den = a_re * a_re + a_im * a_im
    xr = ab_re - 1.0
    coef_re = (xr * a_re + ab_im * a_im) / den
    coef_im = (ab_im * a_re - xr * a_im) / den
    return ab_re, ab_im, coef_re * b_re - coef_im * b_im, coef_re * b_im + coef_im * b_re


def _block_diag_mask():
    row_g = lax.broadcasted_iota(jnp.int32, (SSM_WIDTH, 2 * NS), 0) // SSM_GROUP
    col = lax.broadcasted_iota(jnp.int32, (SSM_WIDTH, 2 * NS), 1)
    col_g = jnp.where(col >= NS, col - NS, col) // SSM_STATE
    return row_g == col_g


def _disc_fwd(a_re, a_im, ldt, b_re, b_im, c_re, c_im):
    def body(are_ref, aim_ref, ldt_ref, bre_ref, bim_ref, cre_ref, cim_ref, pw_ref, pwr_ref, bd_ref, cdt_ref):
        ab_re, ab_im, bb_re, bb_im = _disc_math(are_ref[...], aim_ref[...], ldt_ref[...], bre_ref[...], bim_ref[...])
        row = lax.broadcasted_iota(jnp.int32, (8, NS), 0)
        pr, pi = ab_re, ab_im
        t_re = jnp.zeros((8, NS), F32)
        t_im = jnp.zeros((8, NS), F32)
        u_re = jnp.zeros((8, NS), F32)
        u_im = jnp.zeros((8, NS), F32)
        for j in range(8):
            t_re = jnp.where(row == j, pr, t_re)
            t_im = jnp.where(row == j, pi, t_im)
            u_re = jnp.where(row == 7 - j, pr, u_re)
            u_im = jnp.where(row == 7 - j, pi, u_im)
            pr, pi = pr * ab_re - pi * ab_im, pr * ab_im + pi * ab_re
        pw_ref[0] = t_re
        pw_ref[1] = t_im
        pwr_ref[0] = u_re
        pwr_ref[1] = u_im
        mask = _block_diag_mask()
        zero = jnp.zeros((SSM_WIDTH, 2 * NS), F32)
        bfull = jnp.concatenate([jnp.concatenate([bb_re] * SSM_GROUPS, axis=0),
                                 jnp.concatenate([bb_im] * SSM_GROUPS, axis=0)], axis=1)
        bd_ref[...] = jnp.where(mask, bfull, zero).astype(BF16)
        cfull = jnp.concatenate([jnp.concatenate([cre_ref[...]] * SSM_GROUPS, axis=0),
                                 jnp.concatenate([-cim_ref[...]] * SSM_GROUPS, axis=0)], axis=1)
        cdt_ref[...] = jnp.where(mask, cfull, zero).astype(BF16)

    return pl.pallas_call(
        body, name="s5_disc_fwd",
        in_specs=[_whole()] * 7, out_specs=[_whole()] * 4,
        out_shape=[jax.ShapeDtypeStruct((2, 8, NS), F32), jax.ShapeDtypeStruct((2, 8, NS), F32),
                   jax.ShapeDtypeStruct((SSM_WIDTH, 2 * NS), BF16), jax.ShapeDtypeStruct((SSM_WIDTH, 2 * NS), BF16)],
        compiler_params=_params(),
    )(a_re, a_im, ldt, b_re, b_im, c_re, c_im)


def _disc_bwd(a_re, a_im, ldt, b_re, b_im, d_bd, d_cdt, d_ab, group_sum):
    def body(are_ref, aim_ref, ldt_ref, bre_ref, bim_ref, dbd_ref, dcdt_ref, dab_ref, gs_ref,
             dare_ref, daim_ref, dldt_ref, dbre_ref, dbim_ref, dcre_ref, dcim_ref):
        col = lax.broadcasted_iota(jnp.int32, (SSM_GROUP, 2 * NS), 1)
        col_g = jnp.where(col >= NS, col - NS, col) // SSM_STATE
        acc_b = jnp.zeros((SSM_GROUP, 2 * NS), F32)
        acc_c = jnp.zeros((SSM_GROUP, 2 * NS), F32)
        for g in range(SSM_GROUPS):
            rows = slice(g * SSM_GROUP, (g + 1) * SSM_GROUP)
            acc_b = acc_b + jnp.where(col_g == g, dbd_ref[rows, :], 0.0)
            acc_c = acc_c + jnp.where(col_g == g, dcdt_ref[rows, :], 0.0)
        dcre_ref[...] = acc_c[:, :NS]
        dcim_ref[...] = -acc_c[:, NS:]
        dab_re = jnp.sum(dab_ref[0], axis=0, keepdims=True)
        dab_im = jnp.sum(dab_ref[1], axis=0, keepdims=True)
        _, vjp = jax.vjp(_disc_math, are_ref[...], aim_ref[...], ldt_ref[...], bre_ref[...], bim_ref[...])
        d_are, d_aim, d_ldt, d_bre, d_bim = vjp((dab_re, dab_im, acc_b[:, :NS], acc_b[:, NS:]))
        dare_ref[...] = d_are
        daim_ref[...] = d_aim
        dbre_ref[...] = d_bre
        dbim_ref[...] = d_bim
        dldt_ref[...] = _dot_exact(jnp.broadcast_to(d_ldt, (8, NS)), gs_ref[...])

    vec = jax.ShapeDtypeStruct((1, NS), F32)
    mat = jax.ShapeDtypeStruct((SSM_GROUP, NS), F32)
    return pl.pallas_call(
        body, name="s5_disc_bwd",
        in_specs=[_whole()] * 9, out_specs=[_whole()] * 7,
        out_shape=[vec, vec, jax.ShapeDtypeStruct((8, 128), F32), mat, mat, mat, mat],
        compiler_params=_params(),
    )(a_re, a_im, ldt, b_re, b_im, d_bd, d_cdt, d_ab, group_sum)


def _scan_blocks(buf, pw_ref, carry_ref, n_blocks, reverse):
    row = lax.broadcasted_iota(jnp.int32, (8, SCAN_LANES), 0)
    for lc in range(NS // SCAN_LANES):
        re_cols = pl.ds(lc * SCAN_LANES, SCAN_LANES)
        im_cols = pl.ds(NS + lc * SCAN_LANES, SCAN_LANES)
        pr = pw_ref[0, :, re_cols]
        pi = pw_ref[1, :, re_cols]
        if reverse:
            pi = -pi
            base = [(7, 1), (6, 2), (4, 4)]
            coef = [(jnp.where(row < 8 - k, pr[j:j + 1], 0.0), jnp.where(row < 8 - k, pi[j:j + 1], 0.0), 8 - k)
                    for j, k in base]
        else:
            base = [(0, 1), (1, 2), (3, 4)]
            coef = [(jnp.where(row >= k, pr[j:j + 1], 0.0), jnp.where(row >= k, pi[j:j + 1], 0.0), k)
                    for j, k in base]

        def step(i, carry, pr=pr, pi=pi, coef=coef, re_cols=re_cols, im_cols=im_cols):
            cr, ci = carry
            blk = (n_blocks - 1 - i) if reverse else i
            rows = pl.ds(pl.multiple_of(blk * 8, 8), 8)
            xr = buf[rows, re_cols]
            xi = buf[rows, im_cols]
            for kr, ki, shift in coef:
                sr = pltpu.roll(xr, shift, 0)
                si = pltpu.roll(xi, shift, 0)
                xr, xi = xr + kr * sr - ki * si, xi + kr * si + ki * sr
            xr, xi = xr + pr * cr - pi * ci, xi + pr * ci + pi * cr
            buf[rows, re_cols] = xr
            buf[rows, im_cols] = xi
            edge = slice(0, 1) if reverse else slice(7, 8)
            return xr[edge], xi[edge]

        cr, ci = lax.fori_loop(0, n_blocks, step, (carry_ref[0:1, re_cols], carry_ref[0:1, im_cols]))
        carry_ref[0:1, re_cols] = cr
        carry_ref[0:1, im_cols] = ci


_SUPER_GROUPS = 16
_SUPER_BLOCKS = [
    (slice(k * _SUPER_GROUPS * SSM_GROUP, (k + 1) * _SUPER_GROUPS * SSM_GROUP),
     [slice(half + k * _SUPER_GROUPS * SSM_STATE, half + (k + 1) * _SUPER_GROUPS * SSM_STATE) for half in (0, NS)])
    for k in range(SSM_GROUPS // _SUPER_GROUPS)]


def _ssm_fwd(u, bd, cdt, d_skip, pw):
    L = u.shape[0]
    tc = min(SSM_CHUNK, L)

    def body(u_ref, bd_ref, cdt_ref, dsk_ref, pw_ref, y_ref, s_ref, carry_ref):
        @pl.when(pl.program_id(0) == 0)
        def _():
            carry_ref[...] = jnp.zeros_like(carry_ref)

        uv = u_ref[...]
        u16 = uv.astype(BF16)
        for ch, states in _SUPER_BLOCKS:
            for st in states:
                s_ref[:, st] = _dot(u16[:, ch], bd_ref[ch, st])
        _scan_blocks(s_ref, pw_ref, carry_ref, tc // 8, reverse=False)
        for ch, states in _SUPER_BLOCKS:
            y_ref[:, ch] = (sum(_dot_nt(s_ref[:, st].astype(BF16), cdt_ref[ch, st]) for st in states)
                            + dsk_ref[:, ch] * uv[:, ch])

    return pl.pallas_call(
        body, name="s5_fwd", grid=(L // tc,),
        in_specs=[_rows(tc, SSM_WIDTH), _whole(), _whole(), _whole(), _whole()],
        out_specs=[_rows(tc, SSM_WIDTH), _rows(tc, 2 * NS)],
        out_shape=[jax.ShapeDtypeStruct((L, SSM_WIDTH), F32), jax.ShapeDtypeStruct((L, 2 * NS), F32)],
        scratch_shapes=[pltpu.VMEM((8, 2 * NS), F32)],
        compiler_params=_params(),
    )(u, bd, cdt, d_skip, pw)


def _ssm_bwd(dy, u, s, bd, cdt, d_skip, pwr):
    L = u.shape[0]
    tc = min(SSM_CHUNK, L)
    nc = L // tc
    blocks = tc // 8

    def body(dy_ref, u_ref, s_ref, sprev_ref, bd_ref, cdt_ref, dsk_ref, pwr_ref,
             du_ref, ddsk_ref, dbd_ref, dcdt_ref, dab_ref, g_ref, sx_ref, carry_ref):
        i = pl.program_id(0)

        @pl.when(i == 0)
        def _():
            carry_ref[...] = jnp.zeros_like(carry_ref)
            ddsk_ref[...] = jnp.zeros_like(ddsk_ref)
            dbd_ref[...] = jnp.zeros_like(dbd_ref)
            dcdt_ref[...] = jnp.zeros_like(dcdt_ref)
            dab_ref[...] = jnp.zeros_like(dab_ref)

        dyv = dy_ref[...]
        uv = u_ref[...]
        dy16 = dyv.astype(BF16)
        u16 = uv.astype(BF16)
        for ch, states in _SUPER_BLOCKS:
            for st in states:
                g_ref[:, st] = _dot(dy16[:, ch], cdt_ref[ch, st])
        _scan_blocks(g_ref, pwr_ref, carry_ref, blocks, reverse=True)
        ddsk_ref[...] += jnp.sum(dyv * uv, axis=0, keepdims=True)
        for ch, states in _SUPER_BLOCKS:
            du = dsk_ref[:, ch] * dyv[:, ch]
            for st in states:
                g16 = g_ref[:, st].astype(BF16)
                du = du + _dot_nt(g16, bd_ref[ch, st])
                dbd_ref[ch, st] += _dot_tn(u16[:, ch], g16)
                dcdt_ref[ch, st] += _dot_tn(dy16[:, ch], s_ref[:, st].astype(BF16))
            du_ref[:, ch] = du

        sx_ref[pl.ds(8, tc), :] = s_ref[...]
        sx_ref[pl.ds(0, 8), :] = jnp.where(i == nc - 1, 0.0, sprev_ref[...])
        row = lax.broadcasted_iota(jnp.int32, (8, SCAN_LANES), 0)
        for lc in range(NS // SCAN_LANES):
            re_cols = pl.ds(lc * SCAN_LANES, SCAN_LANES)
            im_cols = pl.ds(NS + lc * SCAN_LANES, SCAN_LANES)

            def step(b, acc, re_cols=re_cols, im_cols=im_cols):
                ar, ai = acc
                off = pl.multiple_of(b * 8, 8)
                gr = g_ref[pl.ds(off, 8), re_cols]
                gi = g_ref[pl.ds(off, 8), im_cols]
                before = pl.ds(off, 8)
                here = pl.ds(off + 8, 8)
                sr = jnp.where(row == 0, sx_ref[before, re_cols][7:8], pltpu.roll(sx_ref[here, re_cols], 1, 0))
                si = jnp.where(row == 0, sx_ref[before, im_cols][7:8], pltpu.roll(sx_ref[here, im_cols], 1, 0))
                return ar + gr * sr + gi * si, ai + gi * sr - gr * si

            zero = jnp.zeros((8, SCAN_LANES), F32)
            ar, ai = lax.fori_loop(0, blocks, step, (zero, zero))
            dab_ref[0, :, re_cols] += ar
            dab_ref[1, :, re_cols] += ai

    rev = lambda i: (nc - 1 - i, 0)
    sprev = pl.BlockSpec((8, 2 * NS), lambda i: (jnp.maximum((nc - 1 - i) * blocks - 1, 0), 0))
    return pl.pallas_call(
        body, name="s5_bwd", grid=(nc,),
        in_specs=[pl.BlockSpec((tc, SSM_WIDTH), rev), pl.BlockSpec((tc, SSM_WIDTH), rev),
                  pl.BlockSpec((tc, 2 * NS), rev), sprev, _whole(), _whole(), _whole(), _whole()],
        out_specs=[pl.BlockSpec((tc, SSM_WIDTH), rev), _whole(), _whole(), _whole(), _whole()],
        out_shape=[jax.ShapeDtypeStruct((L, SSM_WIDTH), F32), jax.ShapeDtypeStruct((1, SSM_WIDTH), F32),
                   jax.ShapeDtypeStruct((SSM_WIDTH, 2 * NS), F32), jax.ShapeDtypeStruct((SSM_WIDTH, 2 * NS), F32),
                   jax.ShapeDtypeStruct((2, 8, NS), F32)],
        scratch_shapes=[pltpu.VMEM((tc, 2 * NS), F32), pltpu.VMEM((tc + 8, 2 * NS), F32), pltpu.VMEM((8, 2 * NS), F32)],
        compiler_params=_params(),
    )(dy, u, s, s, bd, cdt, d_skip, pwr)


def _branches(o_attn, y, gates, w_ab, w_glu, w_sb):
    ya = _dot(o_attn.astype(BF16), w_ab[...])
    gel = _gelu(y)
    glu = _dot(gel.astype(BF16), w_glu[...])
    p = glu[:, :SSM_WIDTH]
    sg = _sigmoid(glu[:, SSM_WIDTH:])
    ys2 = p * sg
    ysb = _dot(ys2.astype(BF16), w_sb[...])
    ga = gates[:, :D_MODEL]
    gs = gates[:, D_MODEL:]
    return ya, gel, p, sg, ys2, ysb, ga, gs


def _mix_out_fwd(x1, o_g, lse_g, y, gates, w_ab, w_glu, w_sb, w_out):
    L = x1.shape[0]
    tm = min(ROW_TILE, L)

    def body(x_ref, o0, o1, o2, l0, l1, l2, y_ref, gate_ref, wab_ref, wglu_ref, wsb_ref, wout_ref,
             x2_ref, oat_ref, lse0, lse1, lse2, scr):
        la, lb, lc = (_from_residues(ref, scr, d) for ref, d in zip((l0, l1, l2), DILATIONS))
        m = jnp.maximum(jnp.maximum(la, lb), lc)
        ea, eb, ec = jnp.exp(la - m), jnp.exp(lb - m), jnp.exp(lc - m)
        tot = ea + eb + ec
        oa, ob, oc = (_from_residues(ref, scr, d) for ref, d in zip((o0, o1, o2), DILATIONS))
        o_attn = (ea * oa + eb * ob + ec * oc) / tot
        oat_ref[...] = o_attn
        lse = m + jnp.log(tot)
        for ref, d in zip((lse0, lse1, lse2), DILATIONS):
            _to_residues(lse, ref, scr, d)
        ya, _, _, _, _, ysb, ga, gs = _branches(o_attn, y_ref[...], gate_ref[...], wab_ref, wglu_ref, wsb_ref)
        mix = ga * ya + gs * ysb
        x2_ref[...] = x_ref[...] + _dot(mix.astype(BF16), wout_ref[...])

    res = [_residue_spec(d, tm) for d in DILATIONS]
    return pl.pallas_call(
        body, name="mix_out_fwd", grid=(L // tm,),
        in_specs=[_rows(tm, D_MODEL)] + res * 2 + [_rows(tm, SSM_WIDTH), _rows(tm, 2 * D_MODEL)] + [_whole()] * 4,
        out_specs=[_rows(tm, D_MODEL), _rows(tm, GROUP_WIDTH)] + res,
        out_shape=[jax.ShapeDtypeStruct((L, D_MODEL), F32), jax.ShapeDtypeStruct((L, GROUP_WIDTH), F32)]
        + [_residue_shape(d, L, F32) for d in DILATIONS],
        scratch_shapes=[_residue_scratch(tm)],
        compiler_params=_params(),
    )(x1, *o_g, *lse_g, y, gates, w_ab, w_glu, w_sb, w_out)


def _mix_out_bwd(dx2, o_attn, y, gates, w_ab, w_glu, w_sb, w_out, head_sum):
    L = dx2.shape[0]
    tm = min(ROW_TILE, L)

    def body(dx_ref, oat_ref, y_ref, gate_ref, wab_ref, wglu_ref, wsb_ref, wout_ref, hs_ref,
             do0, do1, do2, dl0, dl1, dl2, dy_ref, dgp_ref, mix_ref, dya_ref, dys_ref, ys2_ref, gel_ref, dglu_ref,
             dgb_ref, scr):
        i = pl.program_id(0)
        o_attn = oat_ref[...]
        yv = y_ref[...]
        ya, gel, p, sg, ys2, ysb, ga, gs = _branches(o_attn, yv, gate_ref[...], wab_ref, wglu_ref, wsb_ref)
        mix_ref[...] = (ga * ya + gs * ysb).astype(BF16)
        ys2_ref[...] = ys2.astype(BF16)
        gel_ref[...] = gel.astype(BF16)
        dmix = _dot_nt(dx_ref[...].astype(BF16), wout_ref[...])
        dgp = jnp.concatenate([dmix * ya * ga * (1.0 - ga), dmix * ysb * gs * (1.0 - gs)], axis=1)
        dgp_ref[...] = dgp.astype(BF16)

        @pl.when(i == 0)
        def _():
            dgb_ref[...] = jnp.zeros_like(dgb_ref)

        dgb_ref[...] += jnp.sum(dgp, axis=0, keepdims=True)
        dya = (dmix * ga).astype(BF16)
        dys = (dmix * gs).astype(BF16)
        dya_ref[...] = dya
        dys_ref[...] = dys
        d_o = _dot_nt(dya, wab_ref[...])
        delta = _dot_exact(d_o * o_attn, hs_ref[...])
        for do_ref, dl_ref, d in zip((do0, do1, do2), (dl0, dl1, dl2), DILATIONS):
            _to_residues(d_o, do_ref, scr, d)
            _to_residues(delta, dl_ref, scr, d)
        dys2 = _dot_nt(dys, wsb_ref[...])
        dglu = jnp.concatenate([dys2 * sg, dys2 * p * sg * (1.0 - sg)], axis=1).astype(BF16)
        dglu_ref[...] = dglu
        dy_ref[...] = _dot_nt(dglu, wglu_ref[...]) * _gelu_grad(yv)

    grp = _rows(tm, GROUP_WIDTH)
    wide = _rows(tm, D_MODEL)
    half = _rows(tm, SSM_WIDTH)
    res = [_residue_spec(d, tm) for d in DILATIONS]
    sds = jax.ShapeDtypeStruct
    return pl.pallas_call(
        body, name="mix_out_bwd", grid=(L // tm,),
        in_specs=[wide, grp, half, _rows(tm, 2 * D_MODEL)] + [_whole()] * 5,
        out_specs=res + res + [half, _rows(tm, 2 * D_MODEL), wide, wide, wide, half, half, wide, _acc_row(2 * D_MODEL)],
        out_shape=[_residue_shape(d, L, BF16) for d in DILATIONS] + [_residue_shape(d, L, F32) for d in DILATIONS]
        + [sds((L, SSM_WIDTH), F32),
           sds((L, 2 * D_MODEL), BF16), sds((L, D_MODEL), BF16), sds((L, D_MODEL), BF16),
           sds((L, D_MODEL), BF16), sds((L, SSM_WIDTH), BF16), sds((L, SSM_WIDTH), BF16),
           sds((L, D_MODEL), BF16), sds((1, 2 * D_MODEL), F32)],
        scratch_shapes=[_residue_scratch(tm)],
        compiler_params=_params(),
    )(dx2, o_attn, y, gates, w_ab, w_glu, w_sb, w_out, head_sum)


def _adamw(w, g, m, v, name):
    R, C = w.shape
    tr = _row_tile(R, max(8, ADAMW_BLOCK_BYTES // (4 * C)))

    def body(w_ref, g_ref, m_ref, v_ref, d_ref, mo_ref, vo_ref):
        gv = g_ref[...]
        mn = ADAM_B1 * m_ref[...] + (1.0 - ADAM_B1) * gv
        vn = ADAM_B2 * v_ref[...] + (1.0 - ADAM_B2) * (gv * gv)
        m_hat = mn / (1.0 - ADAM_B1 ** ADAM_STEP)
        v_hat = vn / (1.0 - ADAM_B2 ** ADAM_STEP)
        d_ref[...] = -ADAM_LR * (m_hat / (jnp.sqrt(v_hat) + ADAM_EPS) + ADAM_WD * w_ref[...])
        mo_ref[...] = mn
        vo_ref[...] = vn

    blk = pl.BlockSpec((tr, C), lambda i: (i, 0))
    return pl.pallas_call(
        body, name=name, grid=(R // tr,),
        in_specs=[blk] * 4, out_specs=[blk] * 3,
        out_shape=[jax.ShapeDtypeStruct((R, C), F32)] * 3,
        compiler_params=_params(),
    )(w, g, m, v)


def _sum_slots(x, name):
    S, R, C = x.shape
    tr = _row_tile(R, 512)

    def body(x_ref, o_ref):
        acc = x_ref[0].astype(F32)
        for k in range(1, S):
            acc = acc + x_ref[k].astype(F32)
        o_ref[...] = acc

    return pl.pallas_call(
        body, name=name, grid=(R // tr,),
        in_specs=[pl.BlockSpec((S, tr, C), lambda i: (0, i, 0))],
        out_specs=pl.BlockSpec((tr, C), lambda i: (i, 0)),
        out_shape=jax.ShapeDtypeStruct((R, C), F32),
        compiler_params=_params(),
    )(x)


def _sum_chips_into_half(u, name):
    S, H, C = u.shape
    tr = _row_tile(H, 512)
    hb = H // tr

    def body(c_ref, u_ref, o_ref):
        acc = u_ref[0].astype(F32)
        for k in range(1, S):
            acc = acc + u_ref[k].astype(F32)
        o_ref[...] = acc

    core = lax.axis_index("c").astype(jnp.int32).reshape(1)
    return pl.pallas_call(
        body, name=name,
        grid_spec=pltpu.PrefetchScalarGridSpec(
            num_scalar_prefetch=1, grid=(hb,),
            in_specs=[pl.BlockSpec((S, tr, C), lambda i, c_ref: (0, i, 0))],
            out_specs=pl.BlockSpec((tr, C), lambda i, c_ref: (c_ref[0] * hb + i, 0))),
        out_shape=jax.ShapeDtypeStruct((2 * H, C), F32),
        compiler_params=_params(),
    )(core, u)


def _add_halves(g, r1, name):
    S, R, C = g.shape
    H = R // 2
    tr = _row_tile(H, 512)
    hb = H // tr

    def body(c_ref, g_ref, r_ref, o_ref):
        o_ref[...] = (g_ref[...] + r_ref[...]).astype(BF16)

    core = lax.axis_index("c").astype(jnp.int32).reshape(1)
    return pl.pallas_call(
        body, name=name,
        grid_spec=pltpu.PrefetchScalarGridSpec(
            num_scalar_prefetch=1, grid=(S, hb),
            in_specs=[pl.BlockSpec((None, tr, C), lambda j, i, c_ref: (j, c_ref[0] * hb + i, 0)),
                      pl.BlockSpec((None, tr, C), lambda j, i, c_ref: (j, i, 0))],
            out_specs=pl.BlockSpec((None, tr, C), lambda j, i, c_ref: (j, i, 0))),
        out_shape=jax.ShapeDtypeStruct((S, H, C), BF16),
        compiler_params=_params(),
    )(core, g, r1)


_ANY = pl.BlockSpec(memory_space=pl.ANY)


def _place():
    x, y, c = lax.axis_index("x"), lax.axis_index("y"), lax.axis_index("c")
    chips = [(1 - x, y), (x, 1 - y), (1 - x, 1 - y)]
    return x, y, c, chips


def _comm_call(body, name, ins, out_shapes, n_remote, n_local):
    return pl.pallas_call(
        body, name=name,
        in_specs=[_ANY] * len(ins), out_specs=[_ANY] * len(out_shapes), out_shape=out_shapes,
        scratch_shapes=[pltpu.SemaphoreType.DMA((n_remote,)), pltpu.SemaphoreType.DMA((n_remote,)),
                        pltpu.SemaphoreType.DMA((max(n_local, 1),))],
    )(*ins)


def _remote(src, dst, send_sems, recv_sems, k, device):
    return pltpu.make_async_remote_copy(src_ref=src, dst_ref=dst, send_sem=send_sems.at[k], recv_sem=recv_sems.at[k],
                                        device_id=device, device_id_type=MESH)


def _gather_weights(shards):
    n = len(shards)

    def body(*refs):
        w_refs, out_refs = refs[:n], refs[n:2 * n]
        send_sems, recv_sems, local_sems = refs[2 * n:]
        x, y, c, chips = _place()
        me = 2 * x + y
        sibling = (x, y, 1 - c)

        def half(k, chip_idx, core):
            H = shards[k].shape[0] // 2
            return out_refs[k].at[chip_idx, pl.ds(core * H, H), :]

        mine = [_remote(w_refs[k], out_refs[k].at[me], send_sems, recv_sems, 6 * n + k, sibling) for k in range(n)]
        for cp in mine:
            cp.start()
        first = []
        for k in range(n):
            H = shards[k].shape[0] // 2
            for j, (cx, cy) in enumerate(chips):
                first.append(_remote(w_refs[k].at[pl.ds(c * H, H), :], half(k, me, c), send_sems, recv_sems,
                                     3 * k + j, (cx, cy, c)))
        for cp in first:
            cp.start()
        passed = []
        for k in range(n):
            for j, (cx, cy) in enumerate(chips):
                landed = half(k, 2 * cx + cy, c)
                _remote(landed, landed, send_sems, recv_sems, 3 * k + j, (cx, cy, c)).wait_recv()
                fwd = _remote(landed, landed, send_sems, recv_sems, 3 * n + 3 * k + j, sibling)
                fwd.start()
                passed.append(fwd)
        for k in range(n):
            for j, (cx, cy) in enumerate(chips):
                other = half(k, 2 * cx + cy, 1 - c)
                _remote(other, other, send_sems, recv_sems, 3 * n + 3 * k + j, sibling).wait_recv()
        for cp in mine:
            cp.wait_recv()
        for cp in first + passed + mine:
            cp.wait_send()

    return _comm_call(body, "gather_weights", shards,
                      [jax.ShapeDtypeStruct((N_SHARD,) + w.shape, w.dtype) for w in shards], 7 * n, 0)


def _swap_halves(gs):
    n = len(gs)

    def body(*refs):
        g_refs, out_refs = refs[:n], refs[n:2 * n]
        send_sems, recv_sems, _ = refs[2 * n:]
        x, y, c, _ = _place()
        cps = []
        for k in range(n):
            H = gs[k].shape[1] // 2
            cp = _remote(g_refs[k].at[:, pl.ds((1 - c) * H, H), :], out_refs[k], send_sems, recv_sems, k, (x, y, 1 - c))
            cp.start()
            cps.append(cp)
        for cp in cps:
            cp.wait()

    return _comm_call(body, "reduce_swap_halves", gs,
                      [jax.ShapeDtypeStruct((g.shape[0], g.shape[1] // 2, g.shape[2]), g.dtype) for g in gs], n, 0)


def _exchange_chips(ts):
    n = len(ts)

    def body(*refs):
        t_refs, out_refs = refs[:n], refs[n:2 * n]
        send_sems, recv_sems, local_sems = refs[2 * n:]
        x, y, c, chips = _place()
        me = 2 * x + y
        mine = [pltpu.make_async_copy(t_refs[k].at[me], out_refs[k].at[me], local_sems.at[k]) for k in range(n)]
        for cp in mine:
            cp.start()
        sent = []
        for k in range(n):
            for j, (cx, cy) in enumerate(chips):
                cp = _remote(t_refs[k].at[2 * cx + cy], out_refs[k].at[me], send_sems, recv_sems, 3 * k + j, (cx, cy, c))
                cp.start()
                sent.append(cp)
        for k in range(n):
            for j, (cx, cy) in enumerate(chips):
                slot = out_refs[k].at[2 * cx + cy]
                _remote(slot, slot, send_sems, recv_sems, 3 * k + j, (cx, cy, c)).wait_recv()
        for cp in sent:
            cp.wait_send()
        for cp in mine:
            cp.wait()

    return _comm_call(body, "reduce_exchange_chips", ts, [jax.ShapeDtypeStruct(t.shape, t.dtype) for t in ts], 3 * n, n)


def _join_halves(fs):
    n = len(fs)

    def body(*refs):
        out_refs = refs[n:2 * n]
        send_sems, recv_sems, _ = refs[2 * n:]
        x, y, c, _ = _place()
        sent = []
        for k in range(n):
            H = fs[k].shape[0] // 2
            here = out_refs[k].at[pl.ds(c * H, H), :]
            cp = _remote(here, here, send_sems, recv_sems, k, (x, y, 1 - c))
            cp.start()
            sent.append(cp)
        for k in range(n):
            H = fs[k].shape[0] // 2
            other = out_refs[k].at[pl.ds((1 - c) * H, H), :]
            _remote(other, other, send_sems, recv_sems, k, (x, y, 1 - c)).wait_recv()
        for cp in sent:
            cp.wait_send()

    return pl.pallas_call(
        body, name="reduce_join_halves",
        in_specs=[_ANY] * n, out_specs=[_ANY] * n,
        out_shape=[jax.ShapeDtypeStruct(f.shape, f.dtype) for f in fs],
        input_output_aliases={k: k for k in range(n)},
        scratch_shapes=[pltpu.SemaphoreType.DMA((n,)), pltpu.SemaphoreType.DMA((n,)), pltpu.SemaphoreType.DMA((1,))],
    )(*fs)


def _gather_small(v):
    R, C = v.shape

    def body(v_ref, out_ref, send_sems, recv_sems, local_sem):
        x, y, c, _ = _place()
        me = 4 * x + 2 * y + c
        mine = pltpu.make_async_copy(v_ref, out_ref.at[me], local_sem)
        mine.start()
        flips = [(fx, fy, fc) for fx in (0, 1) for fy in (0, 1) for fc in (0, 1)][1:]
        peers = [((1 - x) if fx else x, (1 - y) if fy else y, (1 - c) if fc else c) for fx, fy, fc in flips]
        sent = []
        for j, peer in enumerate(peers):
            cp = pltpu.make_async_remote_copy(
                src_ref=v_ref, dst_ref=out_ref.at[me], send_sem=send_sems.at[j], recv_sem=recv_sems.at[j],
                device_id=peer, device_id_type=MESH)
            cp.start()
            sent.append(cp)
        for j, peer in enumerate(peers):
            slot = out_ref.at[4 * peer[0] + 2 * peer[1] + peer[2]]
            pltpu.make_async_remote_copy(
                src_ref=slot, dst_ref=slot, send_sem=send_sems.at[j], recv_sem=recv_sems.at[j],
                device_id=peer, device_id_type=MESH).wait_recv()
        for cp in sent:
            cp.wait_send()
        mine.wait()

    return pl.pallas_call(
        body, name="gather_small",
        in_specs=[_ANY], out_specs=_ANY,
        out_shape=jax.ShapeDtypeStruct((8, R, C), F32),
        scratch_shapes=[pltpu.SemaphoreType.DMA((7,)), pltpu.SemaphoreType.DMA((7,)), pltpu.SemaphoreType.DMA],
    )(v)


def _reduce_scatter(gs, names):
    r1 = _swap_halves(gs)
    ts = [_add_halves(g, r, "reduce_add_cores_" + nm) for g, r, nm in zip(gs, r1, names)]
    us = _exchange_chips(ts)
    fs = [_sum_chips_into_half(u, "reduce_add_chips_" + nm) for u, nm in zip(us, names)]
    return _join_halves(fs)


BIG = ["ffn1_w_gate", "ffn1_w_up", "ffn1_w_down", "w_in", "ssm_w_glu", "w_attn_branch", "w_ssm_branch",
       "w_out", "ffn2_w_gate", "ffn2_w_up", "ffn2_w_down"]
SMALL = ["ffn1_norm", "mix_norm", "gate_bias", "rel_bias_table", "ssm_a_re", "ssm_a_im", "ssm_log_dt",
         "ssm_b_re", "ssm_b_im", "ssm_c_re", "ssm_c_im", "ssm_d", "ffn2_norm", "final_norm"]
ORDER = ["ffn1_norm", "ffn1_w_gate", "ffn1_w_up", "ffn1_w_down", "mix_norm", "w_in", "gate_bias", "rel_bias_table",
         "ssm_a_re", "ssm_a_im", "ssm_log_dt", "ssm_b_re", "ssm_b_im", "ssm_c_re", "ssm_c_im", "ssm_d",
         "ssm_w_glu", "w_attn_branch", "w_ssm_branch", "w_out", "ffn2_norm", "ffn2_w_gate", "ffn2_w_up",
         "ffn2_w_down", "final_norm"]


_SMALL_TILE = 8 * LANES


def _pack_small(arrays):
    rows = []
    for a in arrays:
        flat = a.reshape(-1).astype(F32)
        rows.append(jnp.pad(flat, (0, (-flat.shape[0]) % _SMALL_TILE)).reshape(-1, LANES))
    return jnp.concatenate(rows, axis=0)


def _unpack_small(packed, shapes):
    out, r0 = [], 0
    for shp in shapes:
        n = math.prod(shp)
        rows = 8 * -(-n // _SMALL_TILE)
        out.append(packed[r0:r0 + rows].reshape(-1)[:n].reshape(shp))
        r0 += rows
    return out


def _local_step(x, target, w, small):
    L = x.shape[0]
    row = lambda v: v.reshape(1, -1)

    a_re, a_im = small["ssm_a_re"].reshape(1, NS), small["ssm_a_im"].reshape(1, NS)
    ldt = jnp.repeat(small["ssm_log_dt"].reshape(SSM_GROUPS), SSM_STATE).reshape(1, NS)
    to_cn = lambda b: b.reshape(SSM_GROUPS, SSM_STATE, SSM_GROUP).transpose(2, 0, 1).reshape(SSM_GROUP, NS)
    c_to_cn = lambda c: c.reshape(SSM_GROUPS, SSM_GROUP, SSM_STATE).transpose(1, 0, 2).reshape(SSM_GROUP, NS)
    b_re, b_im = to_cn(small["ssm_b_re"]), to_cn(small["ssm_b_im"])
    c_re, c_im = c_to_cn(small["ssm_c_re"]), c_to_cn(small["ssm_c_im"])
    d_skip = row(small["ssm_d"])
    pw, pwr, bd, cdt = _disc_fwd(a_re, a_im, ldt, b_re, b_im, c_re, c_im)

    onehot = _bucket_onehot()
    table_t = small["rel_bias_table"].T.reshape(3, HEADS_PER_GROUP, N_BUCKETS)
    table_t = jnp.pad(table_t, ((0, 0), (0, 8 - HEADS_PER_GROUP), (0, 0)))
    bias = _bias_expand(table_t, onehot)[:, :, :HEADS_PER_GROUP].reshape(
        3, 2, HEADS_PER_GROUP, ATTN_BLOCK, 2 * ATTN_BLOCK)

    n1, nm, n2, nf = row(small["ffn1_norm"]), row(small["mix_norm"]), row(small["ffn2_norm"]), row(small["final_norm"])
    gate_bias = row(small["gate_bias"])

    x1, a1, b1 = _ffn_fwd(x, n1, w["ffn1_w_gate"], w["ffn1_w_up"], w["ffn1_w_down"], "ffn1_fwd")
    *qkv, u, gates = _mix_in_fwd(x1, nm, w["w_in"], gate_bias)
    q, k, v = qkv[0:3], qkv[3:6], qkv[6:9]
    o_g, lse_g = [], []
    for grp in range(3):
        o, lse = _attn_fwd(q[grp], k[grp], v[grp], bias[grp], f"attn_fwd_{grp}")
        o_g.append(o)
        lse_g.append(lse)
    y, s = _ssm_fwd(u, bd, cdt, d_skip, pw)
    x2, o_attn, *lse_tot = _mix_out_fwd(x1, o_g, lse_g, y, gates, w["w_attn_branch"], w["ssm_w_glu"],
                                        w["w_ssm_branch"], w["w_out"])
    x3, a2, b2 = _ffn_fwd(x2, n2, w["ffn2_w_gate"], w["ffn2_w_up"], w["ffn2_w_down"], "ffn2_fwd")
    loss_blk, dx3, d_nf = _loss_fwd_bwd(x3, nf, target)

    gw, gs = {}, {}
    gs["final_norm"] = d_nf

    dx2, da, db, sact, h, d_out, gs["ffn2_norm"] = _ffn_bwd(dx3, x2, n2, a2, b2, w["ffn2_w_gate"], w["ffn2_w_up"],
                                                            w["ffn2_w_down"], "ffn2_bwd")
    gw["ffn2_w_gate"] = _matmul_tn(h[None], da, "ffn2_dw_gate")
    gw["ffn2_w_up"] = _matmul_tn(h[None], db, "ffn2_dw_up")
    gw["ffn2_w_down"] = _matmul_tn(sact, d_out[None], "ffn2_dw_down")

    head_sum = (jnp.arange(GROUP_WIDTH)[:, None] // HEAD_DIM == jnp.arange(GROUP_WIDTH)[None, :] // HEAD_DIM).astype(F32)
    (*d_o_delta, dy, dgp, mix, dya, dys, ys2, gel, dglu, gs["gate_bias"]) = _mix_out_bwd(
        dx2, o_attn, y, gates, w["w_attn_branch"], w["ssm_w_glu"], w["w_ssm_branch"], w["w_out"], head_sum)
    d_o, delta = d_o_delta[0:3], d_o_delta[3:6]
    gw["w_out"] = _matmul_tn(mix[None], dx2[None], "dw_out")[0]
    gw["w_attn_branch"] = _matmul_tn(o_attn[None], dya[None], "dw_attn_branch")[0]
    gw["w_ssm_branch"] = _matmul_tn(ys2[None], dys[None], "dw_ssm_branch")[0]
    gw["ssm_w_glu"] = _matmul_tn(gel[None], dglu[None], "dw_glu")[0]

    dqs, dks, dvs, dsums = [], [], [], []
    for grp in range(3):
        dq, dk, dv, dsum = _attn_bwd(q[grp], k[grp], v[grp], d_o[grp], lse_tot[grp], delta[grp], bias[grp],
                                     f"attn_bwd_{grp}")
        dqs.append(dq)
        dks.append(dk)
        dvs.append(dv)
        dsums.append(dsum.reshape(HEADS_PER_GROUP, -1))
    dsum_all = jnp.pad(jnp.stack(dsums), ((0, 0), (0, 8 - HEADS_PER_GROUP), (0, 0)))
    d_table = _bias_reduce(dsum_all, onehot)[:, :HEADS_PER_GROUP]
    gs["rel_bias_table"] = d_table.reshape(3 * HEADS_PER_GROUP, N_BUCKETS).T

    du, gs["ssm_d"], d_bd, d_cdt, d_ab = _ssm_bwd(dy, u, s, bd, cdt, d_skip, pwr)
    group_sum = (jnp.arange(NS)[:, None] // SSM_STATE == jnp.arange(128)[None, :]).astype(F32)
    d_are, d_aim, d_ldt, d_bre, d_bim, d_cre, d_cim = _disc_bwd(a_re, a_im, ldt, b_re, b_im, d_bd, d_cdt, d_ab, group_sum)
    gs["ssm_a_re"], gs["ssm_a_im"] = d_are, d_aim
    gs["ssm_log_dt"] = d_ldt[0, :SSM_GROUPS]
    from_cn = lambda t: t.reshape(SSM_GROUP, SSM_GROUPS, SSM_STATE).transpose(1, 2, 0)
    c_from_cn = lambda t: t.reshape(SSM_GROUP, SSM_GROUPS, SSM_STATE).transpose(1, 0, 2)
    gs["ssm_b_re"], gs["ssm_b_im"] = from_cn(d_bre), from_cn(d_bim)
    gs["ssm_c_re"], gs["ssm_c_im"] = c_from_cn(d_cre), c_from_cn(d_cim)

    dx1, hm, dz, gs["mix_norm"] = _mix_in_bwd(dx2, x1, nm, dqs + dks + dvs, du, dgp, w["w_in"])
    gw["w_in"] = _matmul_tn(hm[None], dz[None], "dw_in")[0]

    dx0, da, db, sact, h, d_out, gs["ffn1_norm"] = _ffn_bwd(dx1, x, n1, a1, b1, w["ffn1_w_gate"], w["ffn1_w_up"],
                                                            w["ffn1_w_down"], "ffn1_bwd")
    gw["ffn1_w_gate"] = _matmul_tn(h[None], da, "ffn1_dw_gate")
    gw["ffn1_w_up"] = _matmul_tn(h[None], db, "ffn1_dw_up")
    gw["ffn1_w_down"] = _matmul_tn(sact, d_out[None], "ffn1_dw_down")
    return loss_blk, dx0, gw, gs


def _split_cols(g):
    K, N = g.shape
    return g.reshape(K, N_SHARD, N // N_SHARD).transpose(1, 0, 2)


def _join_cols(w):
    S, K, n = w.shape
    return w.transpose(1, 0, 2).reshape(K, S * n)


COL_SHARDED = ("w_in", "ssm_w_glu", "w_attn_branch", "w_ssm_branch")


def kernel(x, ffn1_norm, ffn1_w_gate, ffn1_w_up, ffn1_w_down, mix_norm, w_in, gate_bias, rel_bias_table, ssm_a_re, ssm_a_im, ssm_log_dt, ssm_b_re, ssm_b_im, ssm_c_re, ssm_c_im, ssm_d, ssm_w_glu, w_attn_branch, w_ssm_branch, w_out, ffn2_norm, ffn2_w_gate, ffn2_w_up, ffn2_w_down, final_norm, loss_target, m_ffn1_norm, m_ffn1_w_gate, m_ffn1_w_up, m_ffn1_w_down, m_mix_norm, m_w_in, m_gate_bias, m_rel_bias_table, m_ssm_a_re, m_ssm_a_im, m_ssm_log_dt, m_ssm_b_re, m_ssm_b_im, m_ssm_c_re, m_ssm_c_im, m_ssm_d, m_ssm_w_glu, m_w_attn_branch, m_w_ssm_branch, m_w_out, m_ffn2_norm, m_ffn2_w_gate, m_ffn2_w_up, m_ffn2_w_down, m_final_norm, v_ffn1_norm, v_ffn1_w_gate, v_ffn1_w_up, v_ffn1_w_down, v_mix_norm, v_w_in, v_gate_bias, v_rel_bias_table, v_ssm_a_re, v_ssm_a_im, v_ssm_log_dt, v_ssm_b_re, v_ssm_b_im, v_ssm_c_re, v_ssm_c_im, v_ssm_d, v_ssm_w_glu, v_w_attn_branch, v_w_ssm_branch, v_w_out, v_ffn2_norm, v_ffn2_w_gate, v_ffn2_w_up, v_ffn2_w_down, v_final_norm):
    args = dict(locals())
    weights = {n: args[n] for n in ORDER}
    moms = {n: args["m_" + n] for n in ORDER}
    vels = {n: args["v_" + n] for n in ORDER}

    shard2d = {n: weights[n].reshape(weights[n].shape[-2:]) for n in BIG}
    full = dict(zip(BIG, _gather_weights([shard2d[n].astype(BF16) for n in BIG])))
    for n in COL_SHARDED:
        full[n] = _join_cols(full[n])
    full["w_out"] = full["w_out"].reshape(D_MODEL, D_MODEL)

    small = {n: weights[n] for n in SMALL}
    loss_blk, grad_x, gw, gs = _local_step(x[0], loss_target[0], full, small)

    for n in COL_SHARDED:
        gw[n] = _split_cols(gw[n])
    gw["w_out"] = gw["w_out"].reshape(N_SHARD, D_MODEL // N_SHARD, D_MODEL)
    grads = dict(zip(BIG, _reduce_scatter([gw[n] for n in BIG], BIG)))

    small_shapes = [weights[n].shape for n in SMALL]
    mine = _pack_small([gs[n] for n in SMALL] + [loss_blk[0:1, :]])
    total = _sum_slots(_gather_small(mine), "sum_small")
    small_grads = _unpack_small(total, small_shapes + [(128,)])
    loss = small_grads[-1][0]
    for n, g in zip(SMALL, small_grads[:-1]):
        grads[n] = g

    delta, new_m, new_v = {}, {}, {}
    for n in BIG:
        d, m, v = _adamw(shard2d[n], grads[n], moms[n].reshape(shard2d[n].shape), vels[n].reshape(shard2d[n].shape),
                         "adamw_" + n)
        shp = weights[n].shape
        delta[n], new_m[n], new_v[n] = d.reshape(shp), m.reshape(shp), v.reshape(shp)
        grads[n] = grads[n].reshape(shp)
    d, m, v = _adamw(_pack_small([weights[n] for n in SMALL]), _pack_small([grads[n] for n in SMALL]),
                     _pack_small([moms[n] for n in SMALL]), _pack_small([vels[n] for n in SMALL]), "adamw_small")
    for n, dd, mm, vv in zip(SMALL, _unpack_small(d, small_shapes), _unpack_small(m, small_shapes),
                             _unpack_small(v, small_shapes)):
        delta[n], new_m[n], new_v[n] = dd, mm, vv

    return (loss, grad_x[None], *[grads[n] for n in ORDER], *[delta[n] for n in ORDER],
            *[new_m[n] for n in ORDER], *[new_v[n] for n in ORDER])
```

```python
import functools
import math

import jax
import jax.numpy as jnp
from jax import lax
from jax.experimental import pallas as pl
from jax.experimental.pallas import tpu as pltpu
from jax.experimental.pallas import tpu_sc as plsc

F32 = jnp.float32
BF16 = jnp.bfloat16
MESH = pl.DeviceIdType.MESH

D_MODEL = 1024
D_FF = 2816
HEAD_DIM = 64
HEADS_PER_GROUP = 4
DILATIONS = (1, 4, 16)
WINDOW_STEPS = 128
ATTN_BLOCK = 128
ATTN_QB = 4
GROUP_WIDTH = HEADS_PER_GROUP * HEAD_DIM
ATTN_WIDTH = 3 * GROUP_WIDTH
N_BUCKETS = 32
MAX_DISTANCE = 2048
NEG_INF = -1e30
SSM_WIDTH = 512
SSM_GROUP = 16
SSM_GROUPS = 32
SSM_STATE = 64
NS = SSM_GROUPS * SSM_STATE
EPS = 1e-6
IN_WIDTH = 3 * ATTN_WIDTH + SSM_WIDTH + 2 * D_MODEL
Q_SCALE = HEAD_DIM ** -0.5
N_SHARD = 4
FF_SHARD = D_FF // N_SHARD
ADAM_LR, ADAM_B1, ADAM_B2, ADAM_EPS, ADAM_WD, ADAM_STEP = 0.001, 0.9, 0.999, 1e-08, 0.01, 10

LANES = 128
VMEM_LIMIT = 56 * 1024 * 1024
ROW_TILE = 512
FFN_BWD_TILE = 256
SSM_CHUNK = 256
SCAN_LANES = 512
ADAMW_BLOCK_BYTES = 1 << 20
TN_VMEM_BUDGET = 40 * 1024 * 1024
GATHER_COLLECTIVE_ID = 0


def _params(**kw):
    return pltpu.CompilerParams(vmem_limit_bytes=VMEM_LIMIT, **kw)


def _dot(a, b):
    return jnp.dot(a, b, preferred_element_type=F32)


def _dot_nt(a, b):
    return lax.dot_general(a, b, (((1,), (1,)), ((), ())), preferred_element_type=F32)


def _dot_tn(a, b):
    return lax.dot_general(a, b, (((0,), (0,)), ((), ())), preferred_element_type=F32)


def _dot_exact(a, b):
    return jnp.dot(a, b, preferred_element_type=F32, precision=lax.Precision.HIGHEST)


def _dot_nt_exact(a, b):
    return lax.dot_general(a, b, (((1,), (1,)), ((), ())), preferred_element_type=F32,
                           precision=lax.Precision.HIGHEST)


def _rms(x):
    r = lax.rsqrt(jnp.mean(x * x, axis=-1, keepdims=True) + EPS)
    return r, x * r


def _rms_bwd(dh, g, r, xhat):
    dxh = dh * g
    return r * (dxh - xhat * jnp.mean(dxh * xhat, axis=-1, keepdims=True))


def _sigmoid(x):
    return 1.0 / (1.0 + jnp.exp(-x))


_GELU_C = math.sqrt(2.0 / math.pi)


def _gelu(x):
    return 0.5 * x * (1.0 + jnp.tanh(_GELU_C * (x + 0.044715 * x * x * x)))


def _gelu_grad(x):
    t = jnp.tanh(_GELU_C * (x + 0.044715 * x * x * x))
    return 0.5 * (1.0 + t) + 0.5 * x * (1.0 - t * t) * _GELU_C * (1.0 + 3 * 0.044715 * x * x)


def _whole():
    return pl.BlockSpec(memory_space=pltpu.VMEM)


def _row_tile(rows, cap):
    if rows <= cap:
        return rows
    return max(t for t in range(8, cap + 1, 8) if rows % t == 0)


def _rows(tm, w):
    return pl.BlockSpec((tm, w), lambda i: (i, 0))


def _acc_row(w):
    return pl.BlockSpec((1, w), lambda i: (0, 0))


def _ffn_fwd(x, g, wg, wu, wd, name):
    L = x.shape[0]
    tm = min(ROW_TILE, L)

    def body(x_ref, g_ref, wg_ref, wu_ref, wd_ref, xo_ref, a_ref, b_ref):
        xv = x_ref[...]
        r, xhat = _rms(xv)
        h = (xhat * g_ref[...]).astype(BF16)
        acc = jnp.zeros((tm, D_MODEL), F32)
        for j in range(N_SHARD):
            a = _dot(h, wg_ref[j])
            b = _dot(h, wu_ref[j])
            a_ref[j] = a.astype(BF16)
            b_ref[j] = b.astype(BF16)
            s = (a * _sigmoid(a) * b).astype(BF16)
            acc = acc + _dot(s, wd_ref[j])
        xo_ref[...] = xv + 0.5 * acc

    act = pl.BlockSpec((N_SHARD, tm, FF_SHARD), lambda i: (0, i, 0))
    return pl.pallas_call(
        body, name=name, grid=(L // tm,),
        in_specs=[_rows(tm, D_MODEL), _whole(), _whole(), _whole(), _whole()],
        out_specs=[_rows(tm, D_MODEL), act, act],
        out_shape=[jax.ShapeDtypeStruct((L, D_MODEL), F32),
                   jax.ShapeDtypeStruct((N_SHARD, L, FF_SHARD), BF16),
                   jax.ShapeDtypeStruct((N_SHARD, L, FF_SHARD), BF16)],
        compiler_params=_params(),
    )(x, g, wg, wu, wd)


def _ffn_bwd(dxo, x, g, a, b, wg, wu, wd, name):
    L = x.shape[0]
    tm = min(FFN_BWD_TILE, L)

    def body(dxo_ref, x_ref, g_ref, a_ref, b_ref, wg_ref, wu_ref, wd_ref,
             dxi_ref, da_ref, db_ref, s_ref, h_ref, do_ref, dg_ref):
        i = pl.program_id(0)
        xv = x_ref[...]
        gv = g_ref[...]
        r, xhat = _rms(xv)
        h_ref[...] = (xhat * gv).astype(BF16)
        dxo_v = dxo_ref[...]
        d_out = (0.5 * dxo_v).astype(BF16)
        do_ref[...] = d_out
        dh = jnp.zeros((tm, D_MODEL), F32)
        for j in range(N_SHARD):
            av = a_ref[j].astype(F32)
            bv = b_ref[j].astype(F32)
            sg = _sigmoid(av)
            sl = av * sg
            ds = _dot_nt(d_out, wd_ref[j])
            dbv = (ds * sl).astype(BF16)
            dav = (ds * bv * (sg * (1.0 + av * (1.0 - sg)))).astype(BF16)
            da_ref[j] = dav
            db_ref[j] = dbv
            s_ref[j] = (sl * bv).astype(BF16)
            dh = dh + _dot_nt(dav, wg_ref[j]) + _dot_nt(dbv, wu_ref[j])

        @pl.when(i == 0)
        def _():
            dg_ref[...] = jnp.zeros_like(dg_ref)

        dg_ref[...] += jnp.sum(dh * xhat, axis=0, keepdims=True)
        dxi_ref[...] = dxo_v + _rms_bwd(dh, gv, r, xhat)

    act = pl.BlockSpec((N_SHARD, tm, FF_SHARD), lambda i: (0, i, 0))
    act_shape = jax.ShapeDtypeStruct((N_SHARD, L, FF_SHARD), BF16)
    return pl.pallas_call(
        body, name=name, grid=(L // tm,),
        in_specs=[_rows(tm, D_MODEL), _rows(tm, D_MODEL), _whole(), act, act, _whole(), _whole(), _whole()],
        out_specs=[_rows(tm, D_MODEL), act, act, act, _rows(tm, D_MODEL), _rows(tm, D_MODEL), _acc_row(D_MODEL)],
        out_shape=[jax.ShapeDtypeStruct((L, D_MODEL), F32), act_shape, act_shape, act_shape,
                   jax.ShapeDtypeStruct((L, D_MODEL), BF16), jax.ShapeDtypeStruct((L, D_MODEL), BF16),
                   jax.ShapeDtypeStruct((1, D_MODEL), F32)],
        compiler_params=_params(),
    )(dxo, x, g, a, b, wg, wu, wd)


def _matmul_tn(a, b, name):
    ja, L, K = a.shape
    jb, _, N = b.shape
    J = max(ja, jb)
    splits = [s for s in (1, 2, 4, 8) if s == 1 or N % (s * LANES) == 0]
    nsplit = next((s for s in splits if 2 * K * (N // s) * 4 <= TN_VMEM_BUDGET // 2), splits[-1])
    nc = N // nsplit
    left = TN_VMEM_BUDGET - 2 * K * nc * 4
    row_bytes = 2 * (K * a.dtype.itemsize + nc * b.dtype.itemsize)
    tm = next((t for t in (2048, 1024, 512, 256) if L % t == 0 and t * row_bytes <= left), min(128, L))

    def body(a_ref, b_ref, o_ref):
        @pl.when(pl.program_id(2) == 0)
        def _():
            o_ref[...] = jnp.zeros_like(o_ref)

        o_ref[...] += _dot_tn(a_ref[...].astype(BF16), b_ref[...].astype(BF16))

    return pl.pallas_call(
        body, name=name, grid=(J, nsplit, L // tm),
        in_specs=[pl.BlockSpec((None, tm, K), (lambda j, s, i: (j, i, 0)) if ja > 1 else (lambda j, s, i: (0, i, 0))),
                  pl.BlockSpec((None, tm, nc), (lambda j, s, i: (j, i, s)) if jb > 1 else (lambda j, s, i: (0, i, s)))],
        out_specs=pl.BlockSpec((None, K, nc), lambda j, s, i: (j, 0, s)),
        out_shape=jax.ShapeDtypeStruct((J, K, N), F32),
        compiler_params=_params(),
    )(a, b)


def _loss_fwd_bwd(x, g, target):
    L = x.shape[0]
    tm = min(ROW_TILE, L)

    def body(x_ref, g_ref, t_ref, loss_ref, dx_ref, dg_ref):
        i = pl.program_id(0)
        xv = x_ref[...]
        gv = g_ref[...]
        r, xhat = _rms(xv)
        err = xhat * gv - t_ref[...]
        part = 0.5 * jnp.sum(jnp.sum(err * err, axis=1, keepdims=True) * (1.0 / D_MODEL), axis=0, keepdims=True)
        dy = err * (1.0 / D_MODEL)

        @pl.when(i == 0)
        def _():
            dg_ref[...] = jnp.zeros_like(dg_ref)
            loss_ref[...] = jnp.zeros_like(loss_ref)

        loss_ref[...] += jnp.broadcast_to(part, loss_ref.shape)
        dg_ref[...] += jnp.sum(dy * xhat, axis=0, keepdims=True)
        dx_ref[...] = _rms_bwd(dy, gv, r, xhat)

    return pl.pallas_call(
        body, name="loss_fwd_bwd", grid=(L // tm,),
        in_specs=[_rows(tm, D_MODEL), _whole(), _rows(tm, D_MODEL)],
        out_specs=[pl.BlockSpec((8, 128), lambda i: (0, 0)), _rows(tm, D_MODEL), _acc_row(D_MODEL)],
        out_shape=[jax.ShapeDtypeStruct((8, 128), F32), jax.ShapeDtypeStruct((L, D_MODEL), F32),
                   jax.ShapeDtypeStruct((1, D_MODEL), F32)],
        compiler_params=_params(),
    )(x, g, target)


_C_K = ATTN_WIDTH
_C_V = 2 * ATTN_WIDTH
_C_U = 3 * ATTN_WIDTH
_C_G = _C_U + SSM_WIDTH


def _residue_spec(d, tm):
    return pl.BlockSpec((d, tm // d, GROUP_WIDTH), lambda i: (0, i, 0))


def _residue_shape(d, L, dtype):
    return jax.ShapeDtypeStruct((d, L // d, GROUP_WIDTH), dtype)


def _residue_scratch(tm):
    return pltpu.VMEM((GROUP_WIDTH // LANES, tm, LANES), F32)


def _to_residues(val, out_ref, scr, d):
    if d == 1:
        out_ref[0] = val.astype(out_ref.dtype)
        return
    tm = val.shape[0]
    for half in range(GROUP_WIDTH // LANES):
        cols = slice(half * LANES, (half + 1) * LANES)
        scr[half] = val[:, cols]
        for r in range(d):
            out_ref[r, :, cols] = scr[half, pl.ds(r, tm // d, stride=d), :].astype(out_ref.dtype)


def _from_residues(ref, scr, d):
    if d == 1:
        return ref[0].astype(F32)
    rows = ref.shape[1]
    for half in range(GROUP_WIDTH // LANES):
        cols = slice(half * LANES, (half + 1) * LANES)
        for r in range(d):
            scr[half, pl.ds(r, rows, stride=d), :] = ref[r, :, cols].astype(F32)
    return jnp.concatenate([scr[half] for half in range(GROUP_WIDTH // LANES)], axis=1)


def _mix_in_fwd(x, g, w_in, gate_bias):
    L = x.shape[0]
    tm = min(ROW_TILE, L)

    def body(x_ref, g_ref, w_ref, gb_ref, *refs):
        qkv_refs, (u_ref, gate_ref, scr) = refs[:9], refs[9:]
        r, xhat = _rms(x_ref[...])
        h = (xhat * g_ref[...]).astype(BF16)
        for part, (c0, scale) in enumerate(((0, Q_SCALE), (_C_K, 1.0), (_C_V, 1.0))):
            z = _dot(h, w_ref[:, c0:c0 + ATTN_WIDTH]) * scale
            for grp, d in enumerate(DILATIONS):
                _to_residues(z[:, grp * GROUP_WIDTH:(grp + 1) * GROUP_WIDTH], qkv_refs[3 * part + grp], scr, d)
        u_ref[...] = _dot(h, w_ref[:, _C_U:_C_G])
        gate_ref[...] = _sigmoid(_dot(h, w_ref[:, _C_G:IN_WIDTH]) + gb_ref[...])

    return pl.pallas_call(
        body, name="mix_in_fwd", grid=(L // tm,),
        in_specs=[_rows(tm, D_MODEL), _whole(), _whole(), _whole()],
        out_specs=[_residue_spec(d, tm) for d in DILATIONS] * 3 + [_rows(tm, SSM_WIDTH), _rows(tm, 2 * D_MODEL)],
        out_shape=[_residue_shape(d, L, BF16) for d in DILATIONS] * 3
        + [jax.ShapeDtypeStruct((L, SSM_WIDTH), F32), jax.ShapeDtypeStruct((L, 2 * D_MODEL), F32)],
        scratch_shapes=[_residue_scratch(tm)],
        compiler_params=_params(),
    )(x, g, w_in, gate_bias)


def _mix_in_bwd(dx2, x, g, dqkv, du, dgp, w_in):
    L = x.shape[0]
    tm = min(ROW_TILE, L)

    def body(dx2_ref, x_ref, g_ref, *refs):
        piece_refs = refs[:9]
        du_ref, dgp_ref, w_ref, dx1_ref, h_ref, dz_ref, dg_ref, scr = refs[9:]
        i = pl.program_id(0)
        gv = g_ref[...]
        r, xhat = _rms(x_ref[...])
        h_ref[...] = (xhat * gv).astype(BF16)
        for part in range(3):
            for grp, d in enumerate(DILATIONS):
                c0 = part * ATTN_WIDTH + grp * GROUP_WIDTH
                dz_ref[:, c0:c0 + GROUP_WIDTH] = _from_residues(piece_refs[3 * part + grp], scr, d).astype(BF16)
        dz_ref[:, _C_U:_C_G] = du_ref[...].astype(BF16)
        dz_ref[:, _C_G:IN_WIDTH] = dgp_ref[...]
        dh = _dot_nt(dz_ref[...], w_ref[...])

        @pl.when(i == 0)
        def _():
            dg_ref[...] = jnp.zeros_like(dg_ref)

        dg_ref[...] += jnp.sum(dh * xhat, axis=0, keepdims=True)
        dx1_ref[...] = dx2_ref[...] + _rms_bwd(dh, gv, r, xhat)

    return pl.pallas_call(
        body, name="mix_in_bwd", grid=(L // tm,),
        in_specs=[_rows(tm, D_MODEL), _rows(tm, D_MODEL), _whole()] + [_residue_spec(d, tm) for d in DILATIONS] * 3
        + [_rows(tm, SSM_WIDTH), _rows(tm, 2 * D_MODEL), _whole()],
        out_specs=[_rows(tm, D_MODEL), _rows(tm, D_MODEL), _rows(tm, IN_WIDTH), _acc_row(D_MODEL)],
        out_shape=[jax.ShapeDtypeStruct((L, D_MODEL), F32), jax.ShapeDtypeStruct((L, D_MODEL), BF16),
                   jax.ShapeDtypeStruct((L, IN_WIDTH), BF16), jax.ShapeDtypeStruct((1, D_MODEL), F32)],
        scratch_shapes=[_residue_scratch(tm)],
        compiler_params=_params(),
    )(dx2, x, g, *dqkv, du, dgp, w_in)


def _bucket_onehot():
    qi = jnp.arange(ATTN_BLOCK)[:, None]
    kj = jnp.arange(2 * ATTN_BLOCK)[None, :]
    steps = jnp.maximum(qi + ATTN_BLOCK - kj, 0)
    max_exact = N_BUCKETS // 2
    out = []
    for d in DILATIONS:
        dist = steps * d
        df = jnp.maximum(dist, 1).astype(F32)
        large = max_exact + (jnp.log(df / max_exact) / math.log(MAX_DISTANCE / max_exact)
                             * (N_BUCKETS - max_exact)).astype(jnp.int32)
        large = jnp.minimum(large, N_BUCKETS - 1)
        bucket = jnp.where(dist < max_exact, dist, large).reshape(-1)
        out.append((bucket[None, :] == jnp.arange(N_BUCKETS)[:, None]).astype(F32))
    return jnp.stack(out)


def _bias_expand(table_t, onehot):
    n = onehot.shape[-1]

    def body(t_ref, oh_ref, o_ref):
        bias = _dot_exact(t_ref[...], oh_ref[...])
        col = lax.broadcasted_iota(jnp.int32, (8, n), 1)
        qi = col // (2 * ATTN_BLOCK)
        kj = col - qi * (2 * ATTN_BLOCK)
        steps = qi + ATTN_BLOCK - kj
        band = (steps >= 0) & (steps <= WINDOW_STEPS)
        o_ref[0] = jnp.where(band & (kj >= ATTN_BLOCK), bias, NEG_INF)
        o_ref[1] = jnp.where(band, bias, NEG_INF)

    return pl.pallas_call(
        body, name="bias_expand", grid=(3,),
        in_specs=[pl.BlockSpec((None, 8, N_BUCKETS), lambda g: (g, 0, 0)),
                  pl.BlockSpec((None, N_BUCKETS, n), lambda g: (g, 0, 0))],
        out_specs=pl.BlockSpec((None, 2, 8, n), lambda g: (g, 0, 0, 0)),
        out_shape=jax.ShapeDtypeStruct((3, 2, 8, n), F32),
        compiler_params=_params(),
    )(table_t, onehot)


def _bias_reduce(dsum, onehot):
    n = onehot.shape[-1]

    def body(d_ref, oh_ref, o_ref):
        o_ref[...] = _dot_nt_exact(d_ref[...], oh_ref[...])

    return pl.pallas_call(
        body, name="bias_reduce", grid=(3,),
        in_specs=[pl.BlockSpec((None, 8, n), lambda g: (g, 0, 0)),
                  pl.BlockSpec((None, N_BUCKETS, n), lambda g: (g, 0, 0))],
        out_specs=pl.BlockSpec((None, 8, N_BUCKETS), lambda g: (g, 0, 0)),
        out_shape=jax.ShapeDtypeStruct((3, 8, N_BUCKETS), F32),
        compiler_params=_params(),
    )(dsum, onehot)


def _head_of_col(rows):
    return lax.broadcasted_iota(jnp.int32, (rows, GROUP_WIDTH), 1) // HEAD_DIM


def _attn_specs(qb):
    rows = qb * ATTN_BLOCK
    cur = pl.BlockSpec((None, rows, GROUP_WIDTH), lambda r, n: (r, n, 0))
    prev = pl.BlockSpec((None, ATTN_BLOCK, GROUP_WIDTH), lambda r, n: (r, jnp.maximum(n * qb - 1, 0), 0))
    bias = pl.BlockSpec((2, HEADS_PER_GROUP, ATTN_BLOCK, 2 * ATTN_BLOCK), lambda r, n: (0, 0, 0, 0))
    return cur, prev, bias


def _attn_fwd(q, k, v, bias, name):
    d, M, _ = q.shape
    nb = M // ATTN_BLOCK
    qb = min(ATTN_QB, nb)

    def body(q_ref, kp_ref, kc_ref, vp_ref, vc_ref, bias_ref, o_ref, lse_ref):
        n = pl.program_id(1)
        q_head = _head_of_col(ATTN_BLOCK)
        kv_head = _head_of_col(2 * ATTN_BLOCK)
        kwin = jnp.concatenate([kp_ref[...], kc_ref[...]], axis=0)
        vwin = jnp.concatenate([vp_ref[...], vc_ref[...]], axis=0)
        for b in range(qb):
            rows = slice(b * ATTN_BLOCK, (b + 1) * ATTN_BLOCK)
            window = slice(b * ATTN_BLOCK, (b + 2) * ATTN_BLOCK)
            variant = jnp.minimum(n, 1) if b == 0 else 1
            qv = q_ref[rows, :]
            kk = kwin[window]
            vv = vwin[window]
            o_acc = jnp.zeros((ATTN_BLOCK, GROUP_WIDTH), F32)
            lse_acc = jnp.zeros((ATTN_BLOCK, GROUP_WIDTH), F32)
            for hh in range(HEADS_PER_GROUP):
                hm = q_head == hh
                qh = jnp.where(hm, qv, jnp.zeros_like(qv))
                logits = _dot_nt(qh, kk) + bias_ref[variant, hh]
                m = jnp.max(logits, axis=1, keepdims=True)
                p = jnp.exp(logits - m)
                vh = jnp.where(kv_head == hh, vv, jnp.ones_like(vv))
                pv = _dot(p.astype(BF16), vh)
                c_sum = ((hh + 1) % HEADS_PER_GROUP) * HEAD_DIM
                den = pv[:, c_sum:c_sum + 1]
                o_acc = jnp.where(hm, pv * (1.0 / den), o_acc)
                lse_acc = jnp.where(hm, m + jnp.log(den), lse_acc)
            o_ref[rows, :] = o_acc
            lse_ref[rows, :] = lse_acc

    cur, prev, full = _attn_specs(qb)
    return pl.pallas_call(
        body, name=name, grid=(d, nb // qb),
        in_specs=[cur, prev, cur, prev, cur, full],
        out_specs=[cur, cur],
        out_shape=[jax.ShapeDtypeStruct((d, M, GROUP_WIDTH), F32)] * 2,
        compiler_params=_params(),
    )(q, k, k, v, v, bias)


def _attn_bwd(q, k, v, do, lse, delta, bias, name):
    d, M, _ = q.shape
    nb = M // ATTN_BLOCK
    qb = min(ATTN_QB, nb)
    ns = nb // qb
    rows_q = qb * ATTN_BLOCK
    last = slice(rows_q - ATTN_BLOCK, rows_q)

    def body(q_ref, kp_ref, kc_ref, vp_ref, vc_ref, do_ref, lse_ref, dl_ref, bias_ref,
             dq_ref, dk_ref, dv_ref, dsum_ref, pk_ref, pv_ref, wk_ref, wv_ref):
        r = pl.program_id(0)
        n = pl.program_id(1)

        @pl.when((r == 0) & (n == 0))
        def _():
            dsum_ref[...] = jnp.zeros_like(dsum_ref)

        @pl.when(n == 0)
        def _():
            pk_ref[...] = jnp.zeros_like(pk_ref)
            pv_ref[...] = jnp.zeros_like(pv_ref)

        @pl.when(n < ns)
        def _():
            q_head = _head_of_col(ATTN_BLOCK)
            kwin = jnp.concatenate([kp_ref[...], kc_ref[...]], axis=0)
            vwin = jnp.concatenate([vp_ref[...], vc_ref[...]], axis=0)
            wk_ref[...] = jnp.zeros_like(wk_ref)
            wv_ref[...] = jnp.zeros_like(wv_ref)
            for b in range(qb):
                rows = slice(b * ATTN_BLOCK, (b + 1) * ATTN_BLOCK)
                window = slice(b * ATTN_BLOCK, (b + 2) * ATTN_BLOCK)
                variant = jnp.minimum(n, 1) if b == 0 else 1
                qv = q_ref[rows, :]
                dov = do_ref[rows, :]
                kk = kwin[window]
                vv = vwin[window]
                dq_acc = jnp.zeros((ATTN_BLOCK, GROUP_WIDTH), F32)
                dkk = jnp.zeros((2 * ATTN_BLOCK, GROUP_WIDTH), F32)
                dvv = jnp.zeros((2 * ATTN_BLOCK, GROUP_WIDTH), F32)
                for hh in range(HEADS_PER_GROUP):
                    hm = q_head == hh
                    c0 = hh * HEAD_DIM
                    qh = jnp.where(hm, qv, jnp.zeros_like(qv))
                    doh = jnp.where(hm, dov, jnp.zeros_like(dov))
                    logits = _dot_nt(qh, kk) + bias_ref[variant, hh]
                    p = jnp.exp(logits - lse_ref[rows, c0:c0 + 1])
                    dp = _dot_nt(doh, vv)
                    ds = p * (dp - dl_ref[rows, c0:c0 + 1])
                    dsum_ref[hh] += ds
                    ds16 = ds.astype(BF16)
                    dq_acc = jnp.where(hm, _dot(ds16, kk), dq_acc)
                    dkk = dkk + _dot_tn(ds16, qh)
                    dvv = dvv + _dot_tn(p.astype(BF16), doh)
                dq_ref[rows, :] = (dq_acc * Q_SCALE).astype(BF16)
                wk_ref[window, :] += dkk
                wv_ref[window, :] += dvv
            for out_ref, part_ref, win_ref in ((dk_ref, pk_ref, wk_ref), (dv_ref, pv_ref, wv_ref)):
                if qb > 1:
                    out_ref[0:rows_q - ATTN_BLOCK, :] = part_ref[0:rows_q - ATTN_BLOCK, :].astype(BF16)
                out_ref[last, :] = (part_ref[last, :] + win_ref[0:ATTN_BLOCK, :]).astype(BF16)
                part_ref[...] = win_ref[ATTN_BLOCK:, :]

        @pl.when(n == ns)
        def _():
            dk_ref[...] = pk_ref[...].astype(BF16)
            dv_ref[...] = pv_ref[...].astype(BF16)

    def clamp(n):
        return jnp.minimum(n, ns - 1)

    cur = pl.BlockSpec((None, rows_q, GROUP_WIDTH), lambda r, n: (r, clamp(n), 0))
    prev = pl.BlockSpec((None, ATTN_BLOCK, GROUP_WIDTH), lambda r, n: (r, jnp.maximum(clamp(n) * qb - 1, 0), 0))
    lag = pl.BlockSpec((None, rows_q, GROUP_WIDTH), lambda r, n: (r, jnp.maximum(n - 1, 0), 0))
    full = pl.BlockSpec((2, HEADS_PER_GROUP, ATTN_BLOCK, 2 * ATTN_BLOCK), lambda r, n: (0, 0, 0, 0))
    acc = pl.BlockSpec((HEADS_PER_GROUP, ATTN_BLOCK, 2 * ATTN_BLOCK), lambda r, n: (0, 0, 0))
    return pl.pallas_call(
        body, name=name, grid=(d, ns + 1),
        in_specs=[cur, prev, cur, prev, cur, cur, cur, cur, full],
        out_specs=[cur, lag, lag, acc],
        out_shape=[jax.ShapeDtypeStruct((d, M, GROUP_WIDTH), BF16)] * 3
        + [jax.ShapeDtypeStruct((HEADS_PER_GROUP, ATTN_BLOCK, 2 * ATTN_BLOCK), F32)],
        scratch_shapes=[pltpu.VMEM((rows_q, GROUP_WIDTH), F32), pltpu.VMEM((rows_q, GROUP_WIDTH), F32),
                        pltpu.VMEM((rows_q + ATTN_BLOCK, GROUP_WIDTH), F32),
                        pltpu.VMEM((rows_q + ATTN_BLOCK, GROUP_WIDTH), F32)],
        compiler_params=_params(),
    )(q, k, k, v, v, do, lse, delta, bias)


def _disc_math(a_re, a_im, ldt, b_re, b_im):
    dt = jnp.exp(ldt)
    mag = jnp.exp(a_re * dt)
    ab_re = mag * jnp.cos(a_im * dt)
    ab_im = mag * jnp.sin(a_im * dt)
    den = a_re * a_re + a_im * a_im
    xr = ab_re - 1.0
    coef_re = (xr * a_re + ab_im * a_im) / den
    coef_im = (ab_im * a_re - xr * a_im) / den
    return ab_re, ab_im, coef_re * b_re - coef_im * b_im, coef_re * b_im + coef_im * b_re


def _block_diag_mask():
    row_g = lax.broadcasted_iota(jnp.int32, (SSM_WIDTH, 2 * NS), 0) // SSM_GROUP
    col = lax.broadcasted_iota(jnp.int32, (SSM_WIDTH, 2 * NS), 1)
    col_g = jnp.where(col >= NS, col - NS, col) // SSM_STATE
    return row_g == col_g


def _disc_fwd(a_re, a_im, ldt, b_re, b_im, c_re, c_im):
    def body(are_ref, aim_ref, ldt_ref, bre_ref, bim_ref, cre_ref, cim_ref, pw_ref, pwr_ref, bd_ref, cdt_ref):
        ab_re, ab_im, bb_re, bb_im = _disc_math(are_ref[...], aim_ref[...], ldt_ref[...], bre_ref[...], bim_ref[...])
        row = lax.broadcasted_iota(jnp.int32, (8, NS), 0)
        pr, pi = ab_re, ab_im
        t_re = jnp.zeros((8, NS), F32)
        t_im = jnp.zeros((8, NS), F32)
        u_re = jnp.zeros((8, NS), F32)
        u_im = jnp.zeros((8, NS), F32)
        for j in range(8):
            t_re = jnp.where(row == j, pr, t_re)
            t_im = jnp.where(row == j, pi, t_im)
            u_re = jnp.where(row == 7 - j, pr, u_re)
            u_im = jnp.where(row == 7 - j, pi, u_im)
            pr, pi = pr * ab_re - pi * ab_im, pr * ab_im + pi * ab_re
        pw_ref[0] = t_re
        pw_ref[1] = t_im
        pwr_ref[0] = u_re
        pwr_ref[1] = u_im
        mask = _block_diag_mask()
        zero = jnp.zeros((SSM_WIDTH, 2 * NS), F32)
        bfull = jnp.concatenate([jnp.concatenate([bb_re] * SSM_GROUPS, axis=0),
                                 jnp.concatenate([bb_im] * SSM_GROUPS, axis=0)], axis=1)
        bd_ref[...] = jnp.where(mask, bfull, zero).astype(BF16)
        cfull = jnp.concatenate([jnp.concatenate([cre_ref[...]] * SSM_GROUPS, axis=0),
                                 jnp.concatenate([-cim_ref[...]] * SSM_GROUPS, axis=0)], axis=1)
        cdt_ref[...] = jnp.where(mask, cfull, zero).astype(BF16)

    return pl.pallas_call(
        body, name="s5_disc_fwd",
        in_specs=[_whole()] * 7, out_specs=[_whole()] * 4,
        out_shape=[jax.ShapeDtypeStruct((2, 8, NS), F32), jax.ShapeDtypeStruct((2, 8, NS), F32),
                   jax.ShapeDtypeStruct((SSM_WIDTH, 2 * NS), BF16), jax.ShapeDtypeStruct((SSM_WIDTH, 2 * NS), BF16)],
        compiler_params=_params(),
    )(a_re, a_im, ldt, b_re, b_im, c_re, c_im)


def _disc_bwd(a_re, a_im, ldt, b_re, b_im, d_bd, d_cdt, d_ab, group_sum):
    def body(are_ref, aim_ref, ldt_ref, bre_ref, bim_ref, dbd_ref, dcdt_ref, dab_ref, gs_ref,
             dare_ref, daim_ref, dldt_ref, dbre_ref, dbim_ref, dcre_ref, dcim_ref):
        col = lax.broadcasted_iota(jnp.int32, (SSM_GROUP, 2 * NS), 1)
        col_g = jnp.where(col >= NS, col - NS, col) // SSM_STATE
        acc_b = jnp.zeros((SSM_GROUP, 2 * NS), F32)
        acc_c = jnp.zeros((SSM_GROUP, 2 * NS), F32)
        for g in range(SSM_GROUPS):
            rows = slice(g * SSM_GROUP, (g + 1) * SSM_GROUP)
            acc_b = acc_b + jnp.where(col_g == g, dbd_ref[rows, :], 0.0)
            acc_c = acc_c + jnp.where(col_g == g, dcdt_ref[rows, :], 0.0)
        dcre_ref[...] = acc_c[:, :NS]
        dcim_ref[...] = -acc_c[:, NS:]
        dab_re = jnp.sum(dab_ref[0], axis=0, keepdims=True)
        dab_im = jnp.sum(dab_ref[1], axis=0, keepdims=True)
        _, vjp = jax.vjp(_disc_math, are_ref[...], aim_ref[...], ldt_ref[...], bre_ref[...], bim_ref[...])
        d_are, d_aim, d_ldt, d_bre, d_bim = vjp((dab_re, dab_im, acc_b[:, :NS], acc_b[:, NS:]))
        dare_ref[...] = d_are
        daim_ref[...] = d_aim
        dbre_ref[...] = d_bre
        dbim_ref[...] = d_bim
        dldt_ref[...] = _dot_exact(jnp.broadcast_to(d_ldt, (8, NS)), gs_ref[...])

    vec = jax.ShapeDtypeStruct((1, NS), F32)
    mat = jax.ShapeDtypeStruct((SSM_GROUP, NS), F32)
    return pl.pallas_call(
        body, name="s5_disc_bwd",
        in_specs=[_whole()] * 9, out_specs=[_whole()] * 7,
        out_shape=[vec, vec, jax.ShapeDtypeStruct((8, 128), F32), mat, mat, mat, mat],
        compiler_params=_params(),
    )(a_re, a_im, ldt, b_re, b_im, d_bd, d_cdt, d_ab, group_sum)


def _scan_blocks(buf, pw_ref, carry_ref, n_blocks, reverse):
    row = lax.broadcasted_iota(jnp.int32, (8, SCAN_LANES), 0)
    for lc in range(NS // SCAN_LANES):
        re_cols = pl.ds(lc * SCAN_LANES, SCAN_LANES)
        im_cols = pl.ds(NS + lc * SCAN_LANES, SCAN_LANES)
        pr = pw_ref[0, :, re_cols]
        pi = pw_ref[1, :, re_cols]
        if reverse:
            pi = -pi
            base = [(7, 1), (6, 2), (4, 4)]
            coef = [(jnp.where(row < 8 - k, pr[j:j + 1], 0.0), jnp.where(row < 8 - k, pi[j:j + 1], 0.0), 8 - k)
                    for j, k in base]
        else:
            base = [(0, 1), (1, 2), (3, 4)]
            coef = [(jnp.where(row >= k, pr[j:j + 1], 0.0), jnp.where(row >= k, pi[j:j + 1], 0.0), k)
                    for j, k in base]

        def step(i, carry, pr=pr, pi=pi, coef=coef, re_cols=re_cols, im_cols=im_cols):
            cr, ci = carry
            blk = (n_blocks - 1 - i) if reverse else i
            rows = pl.ds(pl.multiple_of(blk * 8, 8), 8)
            xr = buf[rows, re_cols]
            xi = buf[rows, im_cols]
            for kr, ki, shift in coef:
                sr = pltpu.roll(xr, shift, 0)
                si = pltpu.roll(xi, shift, 0)
                xr, xi = xr + kr * sr - ki * si, xi + kr * si + ki * sr
            xr, xi = xr + pr * cr - pi * ci, xi + pr * ci + pi * cr
            buf[rows, re_cols] = xr
            buf[rows, im_cols] = xi
            edge = slice(0, 1) if reverse else slice(7, 8)
            return xr[edge], xi[edge]

        cr, ci = lax.fori_loop(0, n_blocks, step, (carry_ref[0:1, re_cols], carry_ref[0:1, im_cols]))
        carry_ref[0:1, re_cols] = cr
        carry_ref[0:1, im_cols] = ci


_SUPER_GROUPS = 16
_SUPER_BLOCKS = [
    (slice(k * _SUPER_GROUPS * SSM_GROUP, (k + 1) * _SUPER_GROUPS * SSM_GROUP),
     [slice(half + k * _SUPER_GROUPS * SSM_STATE, half + (k + 1) * _SUPER_GROUPS * SSM_STATE) for half in (0, NS)])
    for k in range(SSM_GROUPS // _SUPER_GROUPS)]


def _ssm_fwd(u, bd, cdt, d_skip, pw):
    L = u.shape[0]
    tc = min(SSM_CHUNK, L)

    def body(u_ref, bd_ref, cdt_ref, dsk_ref, pw_ref, y_ref, s_ref, carry_ref):
        @pl.when(pl.program_id(0) == 0)
        def _():
            carry_ref[...] = jnp.zeros_like(carry_ref)

        uv = u_ref[...]
        u16 = uv.astype(BF16)
        for ch, states in _SUPER_BLOCKS:
            for st in states:
                s_ref[:, st] = _dot(u16[:, ch], bd_ref[ch, st])
        _scan_blocks(s_ref, pw_ref, carry_ref, tc // 8, reverse=False)
        for ch, states in _SUPER_BLOCKS:
            y_ref[:, ch] = (sum(_dot_nt(s_ref[:, st].astype(BF16), cdt_ref[ch, st]) for st in states)
                            + dsk_ref[:, ch] * uv[:, ch])

    return pl.pallas_call(
        body, name="s5_fwd", grid=(L // tc,),
        in_specs=[_rows(tc, SSM_WIDTH), _whole(), _whole(), _whole(), _whole()],
        out_specs=[_rows(tc, SSM_WIDTH), _rows(tc, 2 * NS)],
        out_shape=[jax.ShapeDtypeStruct((L, SSM_WIDTH), F32), jax.ShapeDtypeStruct((L, 2 * NS), F32)],
        scratch_shapes=[pltpu.VMEM((8, 2 * NS), F32)],
        compiler_params=_params(),
    )(u, bd, cdt, d_skip, pw)


def _ssm_bwd(dy, u, s, bd, cdt, d_skip, pwr):
    L = u.shape[0]
    tc = min(SSM_CHUNK, L)
    nc = L // tc
    blocks = tc // 8

    def body(dy_ref, u_ref, s_ref, sprev_ref, bd_ref, cdt_ref, dsk_ref, pwr_ref,
             du_ref, ddsk_ref, dbd_ref, dcdt_ref, dab_ref, g_ref, sx_ref, carry_ref):
        i = pl.program_id(0)

        @pl.when(i == 0)
        def _():
            carry_ref[...] = jnp.zeros_like(carry_ref)
            ddsk_ref[...] = jnp.zeros_like(ddsk_ref)
            dbd_ref[...] = jnp.zeros_like(dbd_ref)
            dcdt_ref[...] = jnp.zeros_like(dcdt_ref)
            dab_ref[...] = jnp.zeros_like(dab_ref)

        dyv = dy_ref[...]
        uv = u_ref[...]
        dy16 = dyv.astype(BF16)
        u16 = uv.astype(BF16)
        for ch, states in _SUPER_BLOCKS:
            for st in states:
                g_ref[:, st] = _dot(dy16[:, ch], cdt_ref[ch, st])
        _scan_blocks(g_ref, pwr_ref, carry_ref, blocks, reverse=True)
        ddsk_ref[...] += jnp.sum(dyv * uv, axis=0, keepdims=True)
        for ch, states in _SUPER_BLOCKS:
            du = dsk_ref[:, ch] * dyv[:, ch]
            for st in states:
                g16 = g_ref[:, st].astype(BF16)
                du = du + _dot_nt(g16, bd_ref[ch, st])
                dbd_ref[ch, st] += _dot_tn(u16[:, ch], g16)
                dcdt_ref[ch, st] += _dot_tn(dy16[:, ch], s_ref[:, st].astype(BF16))
            du_ref[:, ch] = du

        sx_ref[pl.ds(8, tc), :] = s_ref[...]
        sx_ref[pl.ds(0, 8), :] = jnp.where(i == nc - 1, 0.0, sprev_ref[...])
        row = lax.broadcasted_iota(jnp.int32, (8, SCAN_LANES), 0)
        for lc in range(NS // SCAN_LANES):
            re_cols = pl.ds(lc * SCAN_LANES, SCAN_LANES)
            im_cols = pl.ds(NS + lc * SCAN_LANES, SCAN_LANES)

            def step(b, acc, re_cols=re_cols, im_cols=im_cols):
                ar, ai = acc
                off = pl.multiple_of(b * 8, 8)
                gr = g_ref[pl.ds(off, 8), re_cols]
                gi = g_ref[pl.ds(off, 8), im_cols]
                before = pl.ds(off, 8)
                here = pl.ds(off + 8, 8)
                sr = jnp.where(row == 0, sx_ref[before, re_cols][7:8], pltpu.roll(sx_ref[here, re_cols], 1, 0))
                si = jnp.where(row == 0, sx_ref[before, im_cols][7:8], pltpu.roll(sx_ref[here, im_cols], 1, 0))
                return ar + gr * sr + gi * si, ai + gi * sr - gr * si

            zero = jnp.zeros((8, SCAN_LANES), F32)
            ar, ai = lax.fori_loop(0, blocks, step, (zero, zero))
            dab_ref[0, :, re_cols] += ar
            dab_ref[1, :, re_cols] += ai

    rev = lambda i: (nc - 1 - i, 0)
    sprev = pl.BlockSpec((8, 2 * NS), lambda i: (jnp.maximum((nc - 1 - i) * blocks - 1, 0), 0))
    return pl.pallas_call(
        body, name="s5_bwd", grid=(nc,),
        in_specs=[pl.BlockSpec((tc, SSM_WIDTH), rev), pl.BlockSpec((tc, SSM_WIDTH), rev),
                  pl.BlockSpec((tc, 2 * NS), rev), sprev, _whole(), _whole(), _whole(), _whole()],
        out_specs=[pl.BlockSpec((tc, SSM_WIDTH), rev), _whole(), _whole(), _whole(), _whole()],
        out_shape=[jax.ShapeDtypeStruct((L, SSM_WIDTH), F32), jax.ShapeDtypeStruct((1, SSM_WIDTH), F32),
                   jax.ShapeDtypeStruct((SSM_WIDTH, 2 * NS), F32), jax.ShapeDtypeStruct((SSM_WIDTH, 2 * NS), F32),
                   jax.ShapeDtypeStruct((2, 8, NS), F32)],
        scratch_shapes=[pltpu.VMEM((tc, 2 * NS), F32), pltpu.VMEM((tc + 8, 2 * NS), F32), pltpu.VMEM((8, 2 * NS), F32)],
        compiler_params=_params(),
    )(dy, u, s, s, bd, cdt, d_skip, pwr)


def _branches(o_attn, y, gates, w_ab, w_glu, w_sb):
    ya = _dot(o_attn.astype(BF16), w_ab[...])
    gel = _gelu(y)
    glu = _dot(gel.astype(BF16), w_glu[...])
    p = glu[:, :SSM_WIDTH]
    sg = _sigmoid(glu[:, SSM_WIDTH:])
    ys2 = p * sg
    ysb = _dot(ys2.astype(BF16), w_sb[...])
    ga = gates[:, :D_MODEL]
    gs = gates[:, D_MODEL:]
    return ya, gel, p, sg, ys2, ysb, ga, gs


def _mix_out_fwd(x1, o_g, lse_g, y, gates, w_ab, w_glu, w_sb, w_out):
    L = x1.shape[0]
    tm = min(ROW_TILE, L)

    def body(x_ref, o0, o1, o2, l0, l1, l2, y_ref, gate_ref, wab_ref, wglu_ref, wsb_ref, wout_ref,
             x2_ref, oat_ref, lse0, lse1, lse2, scr):
        la, lb, lc = (_from_residues(ref, scr, d) for ref, d in zip((l0, l1, l2), DILATIONS))
        m = jnp.maximum(jnp.maximum(la, lb), lc)
        ea, eb, ec = jnp.exp(la - m), jnp.exp(lb - m), jnp.exp(lc - m)
        tot = ea + eb + ec
        oa, ob, oc = (_from_residues(ref, scr, d) for ref, d in zip((o0, o1, o2), DILATIONS))
        o_attn = (ea * oa + eb * ob + ec * oc) / tot
        oat_ref[...] = o_attn
        lse = m + jnp.log(tot)
        for ref, d in zip((lse0, lse1, lse2), DILATIONS):
            _to_residues(lse, ref, scr, d)
        ya, _, _, _, _, ysb, ga, gs = _branches(o_attn, y_ref[...], gate_ref[...], wab_ref, wglu_ref, wsb_ref)
        mix = ga * ya + gs * ysb
        x2_ref[...] = x_ref[...] + _dot(mix.astype(BF16), wout_ref[...])

    res = [_residue_spec(d, tm) for d in DILATIONS]
    return pl.pallas_call(
        body, name="mix_out_fwd", grid=(L // tm,),
        in_specs=[_rows(tm, D_MODEL)] + res * 2 + [_rows(tm, SSM_WIDTH), _rows(tm, 2 * D_MODEL)] + [_whole()] * 4,
        out_specs=[_rows(tm, D_MODEL), _rows(tm, GROUP_WIDTH)] + res,
        out_shape=[jax.ShapeDtypeStruct((L, D_MODEL), F32), jax.ShapeDtypeStruct((L, GROUP_WIDTH), F32)]
        + [_residue_shape(d, L, F32) for d in DILATIONS],
        scratch_shapes=[_residue_scratch(tm)],
        compiler_params=_params(),
    )(x1, *o_g, *lse_g, y, gates, w_ab, w_glu, w_sb, w_out)


def _mix_out_bwd(dx2, o_attn, y, gates, w_ab, w_glu, w_sb, w_out, head_sum):
    L = dx2.shape[0]
    tm = min(ROW_TILE, L)

    def body(dx_ref, oat_ref, y_ref, gate_ref, wab_ref, wglu_ref, wsb_ref, wout_ref, hs_ref,
             do0, do1, do2, dl0, dl1, dl2, dy_ref, dgp_ref, mix_ref, dya_ref, dys_ref, ys2_ref, gel_ref, dglu_ref,
             dgb_ref, scr):
        i = pl.program_id(0)
        o_attn = oat_ref[...]
        yv = y_ref[...]
        ya, gel, p, sg, ys2, ysb, ga, gs = _branches(o_attn, yv, gate_ref[...], wab_ref, wglu_ref, wsb_ref)
        mix_ref[...] = (ga * ya + gs * ysb).astype(BF16)
        ys2_ref[...] = ys2.astype(BF16)
        gel_ref[...] = gel.astype(BF16)
        dmix = _dot_nt(dx_ref[...].astype(BF16), wout_ref[...])
        dgp = jnp.concatenate([dmix * ya * ga * (1.0 - ga), dmix * ysb * gs * (1.0 - gs)], axis=1)
        dgp_ref[...] = dgp.astype(BF16)

        @pl.when(i == 0)
        def _():
            dgb_ref[...] = jnp.zeros_like(dgb_ref)

        dgb_ref[...] += jnp.sum(dgp, axis=0, keepdims=True)
        dya = (dmix * ga).astype(BF16)
        dys = (dmix * gs).astype(BF16)
        dya_ref[...] = dya
        dys_ref[...] = dys
        d_o = _dot_nt(dya, wab_ref[...])
        delta = _dot_exact(d_o * o_attn, hs_ref[...])
        for do_ref, dl_ref, d in zip((do0, do1, do2), (dl0, dl1, dl2), DILATIONS):
            _to_residues(d_o, do_ref, scr, d)
            _to_residues(delta, dl_ref, scr, d)
        dys2 = _dot_nt(dys, wsb_ref[...])
        dglu = jnp.concatenate([dys2 * sg, dys2 * p * sg * (1.0 - sg)], axis=1).astype(BF16)
        dglu_ref[...] = dglu
        dy_ref[...] = _dot_nt(dglu, wglu_ref[...]) * _gelu_grad(yv)

    grp = _rows(tm, GROUP_WIDTH)
    wide = _rows(tm, D_MODEL)
    half = _rows(tm, SSM_WIDTH)
    res = [_residue_spec(d, tm) for d in DILATIONS]
    sds = jax.ShapeDtypeStruct
    return pl.pallas_call(
        body, name="mix_out_bwd", grid=(L // tm,),
        in_specs=[wide, grp, half, _rows(tm, 2 * D_MODEL)] + [_whole()] * 5,
        out_specs=res + res + [half, _rows(tm, 2 * D_MODEL), wide, wide, wide, half, half, wide, _acc_row(2 * D_MODEL)],
        out_shape=[_residue_shape(d, L, BF16) for d in DILATIONS] + [_residue_shape(d, L, F32) for d in DILATIONS]
        + [sds((L, SSM_WIDTH), F32),
           sds((L, 2 * D_MODEL), BF16), sds((L, D_MODEL), BF16), sds((L, D_MODEL), BF16),
           sds((L, D_MODEL), BF16), sds((L, SSM_WIDTH), BF16), sds((L, SSM_WIDTH), BF16),
           sds((L, D_MODEL), BF16), sds((1, 2 * D_MODEL), F32)],
        scratch_shapes=[_residue_scratch(tm)],
        compiler_params=_params(),
    )(dx2, o_attn, y, gates, w_ab, w_glu, w_sb, w_out, head_sum)


def _adamw(w, g, m, v, name):
    R, C = w.shape
    tr = _row_tile(R, max(8, ADAMW_BLOCK_BYTES // (4 * C)))

    def body(w_ref, g_ref, m_ref, v_ref, d_ref, mo_ref, vo_ref):
        gv = g_ref[...]
        mn = ADAM_B1 * m_ref[...] + (1.0 - ADAM_B1) * gv
        vn = ADAM_B2 * v_ref[...] + (1.0 - ADAM_B2) * (gv * gv)
        m_hat = mn / (1.0 - ADAM_B1 ** ADAM_STEP)
        v_hat = vn / (1.0 - ADAM_B2 ** ADAM_STEP)
        d_ref[...] = -ADAM_LR * (m_hat / (jnp.sqrt(v_hat) + ADAM_EPS) + ADAM_WD * w_ref[...])
        mo_ref[...] = mn
        vo_ref[...] = vn

    blk = pl.BlockSpec((tr, C), lambda i: (i, 0))
    return pl.pallas_call(
        body, name=name, grid=(R // tr,),
        in_specs=[blk] * 4, out_specs=[blk] * 3,
        out_shape=[jax.ShapeDtypeStruct((R, C), F32)] * 3,
        compiler_params=_params(),
    )(w, g, m, v)


def _sum_slots(x, name):
    S, R, C = x.shape
    tr = _row_tile(R, 512)

    def body(x_ref, o_ref):
        acc = x_ref[0].astype(F32)
        for k in range(1, S):
            acc = acc + x_ref[k].astype(F32)
        o_ref[...] = acc

    return pl.pallas_call(
        body, name=name, grid=(R // tr,),
        in_specs=[pl.BlockSpec((S, tr, C), lambda i: (0, i, 0))],
        out_specs=pl.BlockSpec((tr, C), lambda i: (i, 0)),
        out_shape=jax.ShapeDtypeStruct((R, C), F32),
        compiler_params=_params(),
    )(x)


def _sum_chips_into_half(u, name):
    S, H, C = u.shape
    tr = _row_tile(H, 512)
    hb = H // tr

    def body(c_ref, u_ref, o_ref):
        acc = u_ref[0].astype(F32)
        for k in range(1, S):
            acc = acc + u_ref[k].astype(F32)
        o_ref[...] = acc

    core = lax.axis_index("c").astype(jnp.int32).reshape(1)
    return pl.pallas_call(
        body, name=name,
        grid_spec=pltpu.PrefetchScalarGridSpec(
            num_scalar_prefetch=1, grid=(hb,),
            in_specs=[pl.BlockSpec((S, tr, C), lambda i, c_ref: (0, i, 0))],
            out_specs=pl.BlockSpec((tr, C), lambda i, c_ref: (c_ref[0] * hb + i, 0))),
        out_shape=jax.ShapeDtypeStruct((2 * H, C), F32),
        compiler_params=_params(),
    )(core, u)


def _add_halves(g, r1, name):
    S, R, C = g.shape
    H = R // 2
    tr = _row_tile(H, 512)
    hb = H // tr

    def body(c_ref, g_ref, r_ref, o_ref):
        o_ref[...] = (g_ref[...] + r_ref[...]).astype(BF16)

    core = lax.axis_index("c").astype(jnp.int32).reshape(1)
    return pl.pallas_call(
        body, name=name,
        grid_spec=pltpu.PrefetchScalarGridSpec(
            num_scalar_prefetch=1, grid=(S, hb),
            in_specs=[pl.BlockSpec((None, tr, C), lambda j, i, c_ref: (j, c_ref[0] * hb + i, 0)),
                      pl.BlockSpec((None, tr, C), lambda j, i, c_ref: (j, i, 0))],
            out_specs=pl.BlockSpec((None, tr, C), lambda j, i, c_ref: (j, i, 0))),
        out_shape=jax.ShapeDtypeStruct((S, H, C), BF16),
        compiler_params=_params(),
    )(core, g, r1)


_ANY = pl.BlockSpec(memory_space=pl.ANY)


def _place():
    x, y, c = lax.axis_index("x"), lax.axis_index("y"), lax.axis_index("c")
    chips = [(1 - x, y), (x, 1 - y), (1 - x, 1 - y)]
    return x, y, c, chips


def _comm_call(body, name, ins, out_shapes, n_remote, n_local):
    return pl.pallas_call(
        body, name=name,
        in_specs=[_ANY] * len(ins), out_specs=[_ANY] * len(out_shapes), out_shape=out_shapes,
        scratch_shapes=[pltpu.SemaphoreType.DMA((n_remote,)), pltpu.SemaphoreType.DMA((n_remote,)),
                        pltpu.SemaphoreType.DMA((max(n_local, 1),))],
    )(*ins)


def _remote(src, dst, send_sems, recv_sems, k, device):
    return pltpu.make_async_remote_copy(src_ref=src, dst_ref=dst, send_sem=send_sems.at[k], recv_sem=recv_sems.at[k],
                                        device_id=device, device_id_type=MESH)


def _gather_body(shapes, handshake):
    n = len(shapes)

    def body(*refs):
        w_refs, out_refs = refs[:n], refs[n:2 * n]
        send_sems, recv_sems = refs[2 * n:2 * n + 2]
        x, y, c, chips = _place()
        me = 2 * x + y
        sibling = (x, y, 1 - c)
        if handshake:
            barrier = pltpu.get_barrier_semaphore()
            for peer in [sibling] + [(cx, cy, c) for cx, cy in chips]:
                pl.semaphore_signal(barrier, inc=1, device_id=peer, device_id_type=MESH)
            pl.semaphore_wait(barrier, 1 + len(chips))

        def half(k, chip_idx, core):
            H = shapes[k][0] // 2
            return out_refs[k].at[chip_idx, pl.ds(core * H, H), :]

        mine = [_remote(w_refs[k], out_refs[k].at[me], send_sems, recv_sems, 6 * n + k, sibling) for k in range(n)]
        for cp in mine:
            cp.start()
        first = []
        for k in range(n):
            H = shapes[k][0] // 2
            for j, (cx, cy) in enumerate(chips):
                first.append(_remote(w_refs[k].at[pl.ds(c * H, H), :], half(k, me, c), send_sems, recv_sems,
                                     3 * k + j, (cx, cy, c)))
        for cp in first:
            cp.start()
        passed = []
        for k in range(n):
            for j, (cx, cy) in enumerate(chips):
                landed = half(k, 2 * cx + cy, c)
                _remote(landed, landed, send_sems, recv_sems, 3 * k + j, (cx, cy, c)).wait_recv()
                fwd = _remote(landed, landed, send_sems, recv_sems, 3 * n + 3 * k + j, sibling)
                fwd.start()
                passed.append(fwd)
        for k in range(n):
            for j, (cx, cy) in enumerate(chips):
                other = half(k, 2 * cx + cy, 1 - c)
                _remote(other, other, send_sems, recv_sems, 3 * n + 3 * k + j, sibling).wait_recv()
        for cp in mine:
            cp.wait_recv()
        for cp in first + passed + mine:
            cp.wait_send()

    return body


def _gather_weights(shards, name):
    n = len(shards)
    return _comm_call(_gather_body([w.shape for w in shards], handshake=False), name, shards,
                      [jax.ShapeDtypeStruct((N_SHARD,) + w.shape, w.dtype) for w in shards], 7 * n, 0)


def _gather_weights_behind(shards, name, collective_id):
    n = len(shards)
    hbm = pltpu.MemorySpace.HBM
    w_refs = [jax.new_ref(w, memory_space=hbm) for w in shards]
    out_refs = [jax.empty_ref(jax.ShapeDtypeStruct((N_SHARD,) + w.shape, w.dtype), memory_space=hbm) for w in shards]
    body = _gather_body([w.shape for w in shards], handshake=True)

    @pl.kernel(mesh=plsc.ScalarSubcoreMesh(axis_name="sequencer", num_cores=1), name=name,
               scratch_types=(pltpu.SemaphoreType.DMA((7 * n,)), pltpu.SemaphoreType.DMA((7 * n,))),
               compiler_params=pltpu.CompilerParams(collective_id=collective_id))
    def launch(send_sems, recv_sems):
        body(*w_refs, *out_refs, send_sems, recv_sems)

    launch()
    return [r[...] for r in out_refs]


def _swap_halves(gs):
    n = len(gs)

    def body(*refs):
        g_refs, out_refs = refs[:n], refs[n:2 * n]
        send_sems, recv_sems, _ = refs[2 * n:]
        x, y, c, _ = _place()
        cps = []
        for k in range(n):
            H = gs[k].shape[1] // 2
            cp = _remote(g_refs[k].at[:, pl.ds((1 - c) * H, H), :], out_refs[k], send_sems, recv_sems, k, (x, y, 1 - c))
            cp.start()
            cps.append(cp)
        for cp in cps:
            cp.wait()

    return _comm_call(body, "reduce_swap_halves", gs,
                      [jax.ShapeDtypeStruct((g.shape[0], g.shape[1] // 2, g.shape[2]), g.dtype) for g in gs], n, 0)


def _exchange_chips(ts):
    n = len(ts)

    def body(*refs):
        t_refs, out_refs = refs[:n], refs[n:2 * n]
        send_sems, recv_sems, local_sems = refs[2 * n:]
        x, y, c, chips = _place()
        me = 2 * x + y
        mine = [pltpu.make_async_copy(t_refs[k].at[me], out_refs[k].at[me], local_sems.at[k]) for k in range(n)]
        for cp in mine:
            cp.start()
        sent = []
        for k in range(n):
            for j, (cx, cy) in enumerate(chips):
                cp = _remote(t_refs[k].at[2 * cx + cy], out_refs[k].at[me], send_sems, recv_sems, 3 * k + j, (cx, cy, c))
                cp.start()
                sent.append(cp)
        for k in range(n):
            for j, (cx, cy) in enumerate(chips):
                slot = out_refs[k].at[2 * cx + cy]
                _remote(slot, slot, send_sems, recv_sems, 3 * k + j, (cx, cy, c)).wait_recv()
        for cp in sent:
            cp.wait_send()
        for cp in mine:
            cp.wait()

    return _comm_call(body, "reduce_exchange_chips", ts, [jax.ShapeDtypeStruct(t.shape, t.dtype) for t in ts], 3 * n, n)


def _join_halves(fs):
    n = len(fs)

    def body(*refs):
        out_refs = refs[n:2 * n]
        send_sems, recv_sems, _ = refs[2 * n:]
        x, y, c, _ = _place()
        sent = []
        for k in range(n):
            H = fs[k].shape[0] // 2
            here = out_refs[k].at[pl.ds(c * H, H), :]
            cp = _remote(here, here, send_sems, recv_sems, k, (x, y, 1 - c))
            cp.start()
            sent.append(cp)
        for k in range(n):
            H = fs[k].shape[0] // 2
            other = out_refs[k].at[pl.ds((1 - c) * H, H), :]
            _remote(other, other, send_sems, recv_sems, k, (x, y, 1 - c)).wait_recv()
        for cp in sent:
            cp.wait_send()

    return pl.pallas_call(
        body, name="reduce_join_halves",
        in_specs=[_ANY] * n, out_specs=[_ANY] * n,
        out_shape=[jax.ShapeDtypeStruct(f.shape, f.dtype) for f in fs],
        input_output_aliases={k: k for k in range(n)},
        scratch_shapes=[pltpu.SemaphoreType.DMA((n,)), pltpu.SemaphoreType.DMA((n,)), pltpu.SemaphoreType.DMA((1,))],
    )(*fs)


def _gather_small(v):
    R, C = v.shape

    def body(v_ref, out_ref, send_sems, recv_sems, local_sem):
        x, y, c, _ = _place()
        me = 4 * x + 2 * y + c
        mine = pltpu.make_async_copy(v_ref, out_ref.at[me], local_sem)
        mine.start()
        flips = [(fx, fy, fc) for fx in (0, 1) for fy in (0, 1) for fc in (0, 1)][1:]
        peers = [((1 - x) if fx else x, (1 - y) if fy else y, (1 - c) if fc else c) for fx, fy, fc in flips]
        sent = []
        for j, peer in enumerate(peers):
            cp = pltpu.make_async_remote_copy(
                src_ref=v_ref, dst_ref=out_ref.at[me], send_sem=send_sems.at[j], recv_sem=recv_sems.at[j],
                device_id=peer, device_id_type=MESH)
            cp.start()
            sent.append(cp)
        for j, peer in enumerate(peers):
            slot = out_ref.at[4 * peer[0] + 2 * peer[1] + peer[2]]
            pltpu.make_async_remote_copy(
                src_ref=slot, dst_ref=slot, send_sem=send_sems.at[j], recv_sem=recv_sems.at[j],
                device_id=peer, device_id_type=MESH).wait_recv()
        for cp in sent:
            cp.wait_send()
        mine.wait()

    return pl.pallas_call(
        body, name="gather_small",
        in_specs=[_ANY], out_specs=_ANY,
        out_shape=jax.ShapeDtypeStruct((8, R, C), F32),
        scratch_shapes=[pltpu.SemaphoreType.DMA((7,)), pltpu.SemaphoreType.DMA((7,)), pltpu.SemaphoreType.DMA],
    )(v)


def _reduce_scatter(gs, names):
    r1 = _swap_halves(gs)
    ts = [_add_halves(g, r, "reduce_add_cores_" + nm) for g, r, nm in zip(gs, r1, names)]
    us = _exchange_chips(ts)
    fs = [_sum_chips_into_half(u, "reduce_add_chips_" + nm) for u, nm in zip(us, names)]
    return _join_halves(fs)


BIG = ["ffn1_w_gate", "ffn1_w_up", "ffn1_w_down", "w_in", "ssm_w_glu", "w_attn_branch", "w_ssm_branch",
       "w_out", "ffn2_w_gate", "ffn2_w_up", "ffn2_w_down"]
SMALL = ["ffn1_norm", "mix_norm", "gate_bias", "rel_bias_table", "ssm_a_re", "ssm_a_im", "ssm_log_dt",
         "ssm_b_re", "ssm_b_im", "ssm_c_re", "ssm_c_im", "ssm_d", "ffn2_norm", "final_norm"]
ORDER = ["ffn1_norm", "ffn1_w_gate", "ffn1_w_up", "ffn1_w_down", "mix_norm", "w_in", "gate_bias", "rel_bias_table",
         "ssm_a_re", "ssm_a_im", "ssm_log_dt", "ssm_b_re", "ssm_b_im", "ssm_c_re", "ssm_c_im", "ssm_d",
         "ssm_w_glu", "w_attn_branch", "w_ssm_branch", "w_out", "ffn2_norm", "ffn2_w_gate", "ffn2_w_up",
         "ffn2_w_down", "final_norm"]


_SMALL_TILE = 8 * LANES


def _pack_small(arrays):
    rows = []
    for a in arrays:
        flat = a.reshape(-1).astype(F32)
        rows.append(jnp.pad(flat, (0, (-flat.shape[0]) % _SMALL_TILE)).reshape(-1, LANES))
    return jnp.concatenate(rows, axis=0)


def _unpack_small(packed, shapes):
    out, r0 = [], 0
    for shp in shapes:
        n = math.prod(shp)
        rows = 8 * -(-n // _SMALL_TILE)
        out.append(packed[r0:r0 + rows].reshape(-1)[:n].reshape(shp))
        r0 += rows
    return out


def _local_step(x, target, w, small):
    L = x.shape[0]
    row = lambda v: v.reshape(1, -1)

    a_re, a_im = small["ssm_a_re"].reshape(1, NS), small["ssm_a_im"].reshape(1, NS)
    ldt = jnp.repeat(small["ssm_log_dt"].reshape(SSM_GROUPS), SSM_STATE).reshape(1, NS)
    to_cn = lambda b: b.reshape(SSM_GROUPS, SSM_STATE, SSM_GROUP).transpose(2, 0, 1).reshape(SSM_GROUP, NS)
    c_to_cn = lambda c: c.reshape(SSM_GROUPS, SSM_GROUP, SSM_STATE).transpose(1, 0, 2).reshape(SSM_GROUP, NS)
    b_re, b_im = to_cn(small["ssm_b_re"]), to_cn(small["ssm_b_im"])
    c_re, c_im = c_to_cn(small["ssm_c_re"]), c_to_cn(small["ssm_c_im"])
    d_skip = row(small["ssm_d"])
    pw, pwr, bd, cdt = _disc_fwd(a_re, a_im, ldt, b_re, b_im, c_re, c_im)

    onehot = _bucket_onehot()
    table_t = small["rel_bias_table"].T.reshape(3, HEADS_PER_GROUP, N_BUCKETS)
    table_t = jnp.pad(table_t, ((0, 0), (0, 8 - HEADS_PER_GROUP), (0, 0)))
    bias = _bias_expand(table_t, onehot)[:, :, :HEADS_PER_GROUP].reshape(
        3, 2, HEADS_PER_GROUP, ATTN_BLOCK, 2 * ATTN_BLOCK)

    n1, nm, n2, nf = row(small["ffn1_norm"]), row(small["mix_norm"]), row(small["ffn2_norm"]), row(small["final_norm"])
    gate_bias = row(small["gate_bias"])

    x1, a1, b1 = _ffn_fwd(x, n1, w["ffn1_w_gate"], w["ffn1_w_up"], w["ffn1_w_down"], "ffn1_fwd")
    *qkv, u, gates = _mix_in_fwd(x1, nm, w["w_in"], gate_bias)
    q, k, v = qkv[0:3], qkv[3:6], qkv[6:9]
    o_g, lse_g = [], []
    for grp in range(3):
        o, lse = _attn_fwd(q[grp], k[grp], v[grp], bias[grp], f"attn_fwd_{grp}")
        o_g.append(o)
        lse_g.append(lse)
    y, s = _ssm_fwd(u, bd, cdt, d_skip, pw)
    x2, o_attn, *lse_tot = _mix_out_fwd(x1, o_g, lse_g, y, gates, w["w_attn_branch"], w["ssm_w_glu"],
                                        w["w_ssm_branch"], w["w_out"])
    x3, a2, b2 = _ffn_fwd(x2, n2, w["ffn2_w_gate"], w["ffn2_w_up"], w["ffn2_w_down"], "ffn2_fwd")
    loss_blk, dx3, d_nf = _loss_fwd_bwd(x3, nf, target)

    gw, gs = {}, {}
    gs["final_norm"] = d_nf

    dx2, da, db, sact, h, d_out, gs["ffn2_norm"] = _ffn_bwd(dx3, x2, n2, a2, b2, w["ffn2_w_gate"], w["ffn2_w_up"],
                                                            w["ffn2_w_down"], "ffn2_bwd")
    gw["ffn2_w_gate"] = _matmul_tn(h[None], da, "ffn2_dw_gate")
    gw["ffn2_w_up"] = _matmul_tn(h[None], db, "ffn2_dw_up")
    gw["ffn2_w_down"] = _matmul_tn(sact, d_out[None], "ffn2_dw_down")

    head_sum = (jnp.arange(GROUP_WIDTH)[:, None] // HEAD_DIM == jnp.arange(GROUP_WIDTH)[None, :] // HEAD_DIM).astype(F32)
    (*d_o_delta, dy, dgp, mix, dya, dys, ys2, gel, dglu, gs["gate_bias"]) = _mix_out_bwd(
        dx2, o_attn, y, gates, w["w_attn_branch"], w["ssm_w_glu"], w["w_ssm_branch"], w["w_out"], head_sum)
    d_o, delta = d_o_delta[0:3], d_o_delta[3:6]
    gw["w_out"] = _matmul_tn(mix[None], dx2[None], "dw_out")[0]
    gw["w_attn_branch"] = _matmul_tn(o_attn[None], dya[None], "dw_attn_branch")[0]
    gw["w_ssm_branch"] = _matmul_tn(ys2[None], dys[None], "dw_ssm_branch")[0]
    gw["ssm_w_glu"] = _matmul_tn(gel[None], dglu[None], "dw_glu")[0]

    dqs, dks, dvs, dsums = [], [], [], []
    for grp in range(3):
        dq, dk, dv, dsum = _attn_bwd(q[grp], k[grp], v[grp], d_o[grp], lse_tot[grp], delta[grp], bias[grp],
                                     f"attn_bwd_{grp}")
        dqs.append(dq)
        dks.append(dk)
        dvs.append(dv)
        dsums.append(dsum.reshape(HEADS_PER_GROUP, -1))
    dsum_all = jnp.pad(jnp.stack(dsums), ((0, 0), (0, 8 - HEADS_PER_GROUP), (0, 0)))
    d_table = _bias_reduce(dsum_all, onehot)[:, :HEADS_PER_GROUP]
    gs["rel_bias_table"] = d_table.reshape(3 * HEADS_PER_GROUP, N_BUCKETS).T

    du, gs["ssm_d"], d_bd, d_cdt, d_ab = _ssm_bwd(dy, u, s, bd, cdt, d_skip, pwr)
    group_sum = (jnp.arange(NS)[:, None] // SSM_STATE == jnp.arange(128)[None, :]).astype(F32)
    d_are, d_aim, d_ldt, d_bre, d_bim, d_cre, d_cim = _disc_bwd(a_re, a_im, ldt, b_re, b_im, d_bd, d_cdt, d_ab, group_sum)
    gs["ssm_a_re"], gs["ssm_a_im"] = d_are, d_aim
    gs["ssm_log_dt"] = d_ldt[0, :SSM_GROUPS]
    from_cn = lambda t: t.reshape(SSM_GROUP, SSM_GROUPS, SSM_STATE).transpose(1, 2, 0)
    c_from_cn = lambda t: t.reshape(SSM_GROUP, SSM_GROUPS, SSM_STATE).transpose(1, 0, 2)
    gs["ssm_b_re"], gs["ssm_b_im"] = from_cn(d_bre), from_cn(d_bim)
    gs["ssm_c_re"], gs["ssm_c_im"] = c_from_cn(d_cre), c_from_cn(d_cim)

    dx1, hm, dz, gs["mix_norm"] = _mix_in_bwd(dx2, x1, nm, dqs + dks + dvs, du, dgp, w["w_in"])
    gw["w_in"] = _matmul_tn(hm[None], dz[None], "dw_in")[0]

    dx0, da, db, sact, h, d_out, gs["ffn1_norm"] = _ffn_bwd(dx1, x, n1, a1, b1, w["ffn1_w_gate"], w["ffn1_w_up"],
                                                            w["ffn1_w_down"], "ffn1_bwd")
    gw["ffn1_w_gate"] = _matmul_tn(h[None], da, "ffn1_dw_gate")
    gw["ffn1_w_up"] = _matmul_tn(h[None], db, "ffn1_dw_up")
    gw["ffn1_w_down"] = _matmul_tn(sact, d_out[None], "ffn1_dw_down")
    return loss_blk, dx0, gw, gs


def _split_cols(g):
    K, N = g.shape
    return g.reshape(K, N_SHARD, N // N_SHARD).transpose(1, 0, 2)


def _join_cols(w):
    S, K, n = w.shape
    return w.transpose(1, 0, 2).reshape(K, S * n)


COL_SHARDED = ("w_in", "ssm_w_glu", "w_attn_branch", "w_ssm_branch")


def kernel(x, ffn1_norm, ffn1_w_gate, ffn1_w_up, ffn1_w_down, mix_norm, w_in, gate_bias, rel_bias_table, ssm_a_re, ssm_a_im, ssm_log_dt, ssm_b_re, ssm_b_im, ssm_c_re, ssm_c_im, ssm_d, ssm_w_glu, w_attn_branch, w_ssm_branch, w_out, ffn2_norm, ffn2_w_gate, ffn2_w_up, ffn2_w_down, final_norm, loss_target, m_ffn1_norm, m_ffn1_w_gate, m_ffn1_w_up, m_ffn1_w_down, m_mix_norm, m_w_in, m_gate_bias, m_rel_bias_table, m_ssm_a_re, m_ssm_a_im, m_ssm_log_dt, m_ssm_b_re, m_ssm_b_im, m_ssm_c_re, m_ssm_c_im, m_ssm_d, m_ssm_w_glu, m_w_attn_branch, m_w_ssm_branch, m_w_out, m_ffn2_norm, m_ffn2_w_gate, m_ffn2_w_up, m_ffn2_w_down, m_final_norm, v_ffn1_norm, v_ffn1_w_gate, v_ffn1_w_up, v_ffn1_w_down, v_mix_norm, v_w_in, v_gate_bias, v_rel_bias_table, v_ssm_a_re, v_ssm_a_im, v_ssm_log_dt, v_ssm_b_re, v_ssm_b_im, v_ssm_c_re, v_ssm_c_im, v_ssm_d, v_ssm_w_glu, v_w_attn_branch, v_w_ssm_branch, v_w_out, v_ffn2_norm, v_ffn2_w_gate, v_ffn2_w_up, v_ffn2_w_down, v_final_norm):
    args = dict(locals())
    weights = {n: args[n] for n in ORDER}
    moms = {n: args["m_" + n] for n in ORDER}
    vels = {n: args["v_" + n] for n in ORDER}

    shard2d = {n: weights[n].reshape(weights[n].shape[-2:]) for n in BIG}
    first, rest = BIG[:3], BIG[3:]
    full = dict(zip(first, _gather_weights([shard2d[n].astype(BF16) for n in first], "gather_ffn1_weights")))
    full.update(zip(rest, _gather_weights_behind([shard2d[n].astype(BF16) for n in rest], "gather_later_weights",
                                                 GATHER_COLLECTIVE_ID)))
    for n in COL_SHARDED:
        full[n] = _join_cols(full[n])
    full["w_out"] = full["w_out"].reshape(D_MODEL, D_MODEL)

    small = {n: weights[n] for n in SMALL}
    loss_blk, grad_x, gw, gs = _local_step(x[0], loss_target[0], full, small)

    for n in COL_SHARDED:
        gw[n] = _split_cols(gw[n])
    gw["w_out"] = gw["w_out"].reshape(N_SHARD, D_MODEL // N_SHARD, D_MODEL)
    grads = dict(zip(BIG, _reduce_scatter([gw[n] for n in BIG], BIG)))

    small_shapes = [weights[n].shape for n in SMALL]
    mine = _pack_small([gs[n] for n in SMALL] + [loss_blk[0:1, :]])
    total = _sum_slots(_gather_small(mine), "sum_small")
    small_grads = _unpack_small(total, small_shapes + [(128,)])
    loss = small_grads[-1][0]
    for n, g in zip(SMALL, small_grads[:-1]):
        grads[n] = g

    delta, new_m, new_v = {}, {}, {}
    for n in BIG:
        d, m, v = _adamw(shard2d[n], grads[n], moms[n].reshape(shard2d[n].shape), vels[n].reshape(shard2d[n].shape),
                         "adamw_" + n)
        shp = weights[n].shape
        delta[n], new_m[n], new_v[n] = d.reshape(shp), m.reshape(shp), v.reshape(shp)
        grads[n] = grads[n].reshape(shp)
    d, m, v = _adamw(_pack_small([weights[n] for n in SMALL]), _pack_small([grads[n] for n in SMALL]),
                     _pack_small([moms[n] for n in SMALL]), _pack_small([vels[n] for n in SMALL]), "adamw_small")
    for n, dd, mm, vv in zip(SMALL, _unpack_small(d, small_shapes), _unpack_small(m, small_shapes),
                             _unpack_small(v, small_shapes)):
        delta[n], new_m[n], new_v[n] = dd, mm, vv

    return (loss, grad_x[None], *[grads[n] for n in ORDER], *[delta[n] for n in ORDER],
            *[new_m[n] for n in ORDER], *[new_v[n] for n in ORDER])
```

```python
import functools
import math

import jax
import jax.numpy as jnp
from jax import lax
from jax.experimental import pallas as pl
from jax.experimental.pallas import tpu as pltpu
from jax.experimental.pallas import tpu_sc as plsc

F32 = jnp.float32
BF16 = jnp.bfloat16
MESH = pl.DeviceIdType.MESH

D_MODEL = 1024
D_FF = 2816
HEAD_DIM = 64
HEADS_PER_GROUP = 4
DILATIONS = (1, 4, 16)
WINDOW_STEPS = 128
ATTN_BLOCK = 128
ATTN_QB = 4
GROUP_WIDTH = HEADS_PER_GROUP * HEAD_DIM
ATTN_WIDTH = 3 * GROUP_WIDTH
N_BUCKETS = 32
MAX_DISTANCE = 2048
NEG_INF = -1e30
SSM_WIDTH = 512
SSM_GROUP = 16
SSM_GROUPS = 32
SSM_STATE = 64
NS = SSM_GROUPS * SSM_STATE
EPS = 1e-6
IN_WIDTH = 3 * ATTN_WIDTH + SSM_WIDTH + 2 * D_MODEL
Q_SCALE = HEAD_DIM ** -0.5
N_SHARD = 4
FF_SHARD = D_FF // N_SHARD
ADAM_LR, ADAM_B1, ADAM_B2, ADAM_EPS, ADAM_WD, ADAM_STEP = 0.001, 0.9, 0.999, 1e-08, 0.01, 10

LANES = 128
VMEM_LIMIT = 56 * 1024 * 1024
ROW_TILE = 512
FFN_BWD_TILE = 256
SSM_CHUNK = 256
SCAN_LANES = 512
ADAMW_BLOCK_BYTES = 1 << 20
TN_VMEM_BUDGET = 40 * 1024 * 1024
REDUCE_GROUPS = {
    "ffn2": ["ffn2_w_gate", "ffn2_w_up", "ffn2_w_down"],
    "mixer": ["w_out", "w_attn_branch", "w_ssm_branch", "ssm_w_glu"],
    "w_in": ["w_in"],
    "ffn1": ["ffn1_w_gate", "ffn1_w_up", "ffn1_w_down"],
}
COLLECTIVE_IDS = {name: i for i, name in enumerate(
    ["gather"] + [stage + "_" + tag for tag in REDUCE_GROUPS for stage in ("swap", "exchange")])}


def _params(**kw):
    return pltpu.CompilerParams(vmem_limit_bytes=VMEM_LIMIT, **kw)


def _dot(a, b):
    return jnp.dot(a, b, preferred_element_type=F32)


def _dot_nt(a, b):
    return lax.dot_general(a, b, (((1,), (1,)), ((), ())), preferred_element_type=F32)


def _dot_tn(a, b):
    return lax.dot_general(a, b, (((0,), (0,)), ((), ())), preferred_element_type=F32)


def _dot_exact(a, b):
    return jnp.dot(a, b, preferred_element_type=F32, precision=lax.Precision.HIGHEST)


def _dot_nt_exact(a, b):
    return lax.dot_general(a, b, (((1,), (1,)), ((), ())), preferred_element_type=F32,
                           precision=lax.Precision.HIGHEST)


def _rms(x):
    r = lax.rsqrt(jnp.mean(x * x, axis=-1, keepdims=True) + EPS)
    return r, x * r


def _rms_bwd(dh, g, r, xhat):
    dxh = dh * g
    return r * (dxh - xhat * jnp.mean(dxh * xhat, axis=-1, keepdims=True))


def _sigmoid(x):
    return 1.0 / (1.0 + jnp.exp(-x))


_GELU_C = math.sqrt(2.0 / math.pi)


def _gelu(x):
    return 0.5 * x * (1.0 + jnp.tanh(_GELU_C * (x + 0.044715 * x * x * x)))


def _gelu_grad(x):
    t = jnp.tanh(_GELU_C * (x + 0.044715 * x * x * x))
    return 0.5 * (1.0 + t) + 0.5 * x * (1.0 - t * t) * _GELU_C * (1.0 + 3 * 0.044715 * x * x)


def _whole():
    return pl.BlockSpec(memory_space=pltpu.VMEM)


def _row_tile(rows, cap):
    if rows <= cap:
        return rows
    return max(t for t in range(8, cap + 1, 8) if rows % t == 0)


def _rows(tm, w):
    return pl.BlockSpec((tm, w), lambda i: (i, 0))


def _acc_row(w):
    return pl.BlockSpec((1, w), lambda i: (0, 0))


def _ffn_fwd(x, g, wg, wu, wd, name):
    L = x.shape[0]
    tm = min(ROW_TILE, L)

    def body(x_ref, g_ref, wg_ref, wu_ref, wd_ref, xo_ref, a_ref, b_ref):
        xv = x_ref[...]
        r, xhat = _rms(xv)
        h = (xhat * g_ref[...]).astype(BF16)
        acc = jnp.zeros((tm, D_MODEL), F32)
        for j in range(N_SHARD):
            a = _dot(h, wg_ref[j])
            b = _dot(h, wu_ref[j])
            a_ref[j] = a.astype(BF16)
            b_ref[j] = b.astype(BF16)
            s = (a * _sigmoid(a) * b).astype(BF16)
            acc = acc + _dot(s, wd_ref[j])
        xo_ref[...] = xv + 0.5 * acc

    act = pl.BlockSpec((N_SHARD, tm, FF_SHARD), lambda i: (0, i, 0))
    return pl.pallas_call(
        body, name=name, grid=(L // tm,),
        in_specs=[_rows(tm, D_MODEL), _whole(), _whole(), _whole(), _whole()],
        out_specs=[_rows(tm, D_MODEL), act, act],
        out_shape=[jax.ShapeDtypeStruct((L, D_MODEL), F32),
                   jax.ShapeDtypeStruct((N_SHARD, L, FF_SHARD), BF16),
                   jax.ShapeDtypeStruct((N_SHARD, L, FF_SHARD), BF16)],
        compiler_params=_params(),
    )(x, g, wg, wu, wd)


def _ffn_bwd(dxo, x, g, a, b, wg, wu, wd, name):
    L = x.shape[0]
    tm = min(FFN_BWD_TILE, L)

    def body(dxo_ref, x_ref, g_ref, a_ref, b_ref, wg_ref, wu_ref, wd_ref,
             dxi_ref, da_ref, db_ref, s_ref, h_ref, do_ref, dg_ref):
        i = pl.program_id(0)
        xv = x_ref[...]
        gv = g_ref[...]
        r, xhat = _rms(xv)
        h_ref[...] = (xhat * gv).astype(BF16)
        dxo_v = dxo_ref[...]
        d_out = (0.5 * dxo_v).astype(BF16)
        do_ref[...] = d_out
        dh = jnp.zeros((tm, D_MODEL), F32)
        for j in range(N_SHARD):
            av = a_ref[j].astype(F32)
            bv = b_ref[j].astype(F32)
            sg = _sigmoid(av)
            sl = av * sg
            ds = _dot_nt(d_out, wd_ref[j])
            dbv = (ds * sl).astype(BF16)
            dav = (ds * bv * (sg * (1.0 + av * (1.0 - sg)))).astype(BF16)
            da_ref[j] = dav
            db_ref[j] = dbv
            s_ref[j] = (sl * bv).astype(BF16)
            dh = dh + _dot_nt(dav, wg_ref[j]) + _dot_nt(dbv, wu_ref[j])

        @pl.when(i == 0)
        def _():
            dg_ref[...] = jnp.zeros_like(dg_ref)

        dg_ref[...] += jnp.sum(dh * xhat, axis=0, keepdims=True)
        dxi_ref[...] = dxo_v + _rms_bwd(dh, gv, r, xhat)

    act = pl.BlockSpec((N_SHARD, tm, FF_SHARD), lambda i: (0, i, 0))
    act_shape = jax.ShapeDtypeStruct((N_SHARD, L, FF_SHARD), BF16)
    return pl.pallas_call(
        body, name=name, grid=(L // tm,),
        in_specs=[_rows(tm, D_MODEL), _rows(tm, D_MODEL), _whole(), act, act, _whole(), _whole(), _whole()],
        out_specs=[_rows(tm, D_MODEL), act, act, act, _rows(tm, D_MODEL), _rows(tm, D_MODEL), _acc_row(D_MODEL)],
        out_shape=[jax.ShapeDtypeStruct((L, D_MODEL), F32), act_shape, act_shape, act_shape,
                   jax.ShapeDtypeStruct((L, D_MODEL), BF16), jax.ShapeDtypeStruct((L, D_MODEL), BF16),
                   jax.ShapeDtypeStruct((1, D_MODEL), F32)],
        compiler_params=_params(),
    )(dxo, x, g, a, b, wg, wu, wd)


def _matmul_tn(a, b, name):
    ja, L, K = a.shape
    jb, _, N = b.shape
    J = max(ja, jb)
    splits = [s for s in (1, 2, 4, 8) if s == 1 or N % (s * LANES) == 0]
    nsplit = next((s for s in splits if 2 * K * (N // s) * 4 <= TN_VMEM_BUDGET // 2), splits[-1])
    nc = N // nsplit
    left = TN_VMEM_BUDGET - 2 * K * nc * 4
    row_bytes = 2 * (K * a.dtype.itemsize + nc * b.dtype.itemsize)
    tm = next((t for t in (2048, 1024, 512, 256) if L % t == 0 and t * row_bytes <= left), min(128, L))

    def body(a_ref, b_ref, o_ref):
        @pl.when(pl.program_id(2) == 0)
        def _():
            o_ref[...] = jnp.zeros_like(o_ref)

        o_ref[...] += _dot_tn(a_ref[...].astype(BF16), b_ref[...].astype(BF16))

    return pl.pallas_call(
        body, name=name, grid=(J, nsplit, L // tm),
        in_specs=[pl.BlockSpec((None, tm, K), (lambda j, s, i: (j, i, 0)) if ja > 1 else (lambda j, s, i: (0, i, 0))),
                  pl.BlockSpec((None, tm, nc), (lambda j, s, i: (j, i, s)) if jb > 1 else (lambda j, s, i: (0, i, s)))],
        out_specs=pl.BlockSpec((None, K, nc), lambda j, s, i: (j, 0, s)),
        out_shape=jax.ShapeDtypeStruct((J, K, N), F32),
        compiler_params=_params(),
    )(a, b)


def _loss_fwd_bwd(x, g, target):
    L = x.shape[0]
    tm = min(ROW_TILE, L)

    def body(x_ref, g_ref, t_ref, loss_ref, dx_ref, dg_ref):
        i = pl.program_id(0)
        xv = x_ref[...]
        gv = g_ref[...]
        r, xhat = _rms(xv)
        err = xhat * gv - t_ref[...]
        part = 0.5 * jnp.sum(jnp.sum(err * err, axis=1, keepdims=True) * (1.0 / D_MODEL), axis=0, keepdims=True)
        dy = err * (1.0 / D_MODEL)

        @pl.when(i == 0)
        def _():
            dg_ref[...] = jnp.zeros_like(dg_ref)
            loss_ref[...] = jnp.zeros_like(loss_ref)

        loss_ref[...] += jnp.broadcast_to(part, loss_ref.shape)
        dg_ref[...] += jnp.sum(dy * xhat, axis=0, keepdims=True)
        dx_ref[...] = _rms_bwd(dy, gv, r, xhat)

    return pl.pallas_call(
        body, name="loss_fwd_bwd", grid=(L // tm,),
        in_specs=[_rows(tm, D_MODEL), _whole(), _rows(tm, D_MODEL)],
        out_specs=[pl.BlockSpec((8, 128), lambda i: (0, 0)), _rows(tm, D_MODEL), _acc_row(D_MODEL)],
        out_shape=[jax.ShapeDtypeStruct((8, 128), F32), jax.ShapeDtypeStruct((L, D_MODEL), F32),
                   jax.ShapeDtypeStruct((1, D_MODEL), F32)],
        compiler_params=_params(),
    )(x, g, target)


_C_K = ATTN_WIDTH
_C_V = 2 * ATTN_WIDTH
_C_U = 3 * ATTN_WIDTH
_C_G = _C_U + SSM_WIDTH


def _residue_spec(d, tm):
    return pl.BlockSpec((d, tm // d, GROUP_WIDTH), lambda i: (0, i, 0))


def _residue_shape(d, L, dtype):
    return jax.ShapeDtypeStruct((d, L // d, GROUP_WIDTH), dtype)


def _residue_scratch(tm):
    return pltpu.VMEM((GROUP_WIDTH // LANES, tm, LANES), F32)


def _to_residues(val, out_ref, scr, d):
    if d == 1:
        out_ref[0] = val.astype(out_ref.dtype)
        return
    tm = val.shape[0]
    for half in range(GROUP_WIDTH // LANES):
        cols = slice(half * LANES, (half + 1) * LANES)
        scr[half] = val[:, cols]
        for r in range(d):
            out_ref[r, :, cols] = scr[half, pl.ds(r, tm // d, stride=d), :].astype(out_ref.dtype)


def _from_residues(ref, scr, d):
    if d == 1:
        return ref[0].astype(F32)
    rows = ref.shape[1]
    for half in range(GROUP_WIDTH // LANES):
        cols = slice(half * LANES, (half + 1) * LANES)
        for r in range(d):
            scr[half, pl.ds(r, rows, stride=d), :] = ref[r, :, cols].astype(F32)
    return jnp.concatenate([scr[half] for half in range(GROUP_WIDTH // LANES)], axis=1)


def _mix_in_fwd(x, g, w_in, gate_bias):
    L = x.shape[0]
    tm = min(ROW_TILE, L)

    def body(x_ref, g_ref, w_ref, gb_ref, *refs):
        qkv_refs, (u_ref, gate_ref, scr) = refs[:9], refs[9:]
        r, xhat = _rms(x_ref[...])
        h = (xhat * g_ref[...]).astype(BF16)
        for part, (c0, scale) in enumerate(((0, Q_SCALE), (_C_K, 1.0), (_C_V, 1.0))):
            z = _dot(h, w_ref[:, c0:c0 + ATTN_WIDTH]) * scale
            for grp, d in enumerate(DILATIONS):
                _to_residues(z[:, grp * GROUP_WIDTH:(grp + 1) * GROUP_WIDTH], qkv_refs[3 * part + grp], scr, d)
        u_ref[...] = _dot(h, w_ref[:, _C_U:_C_G])
        gate_ref[...] = _sigmoid(_dot(h, w_ref[:, _C_G:IN_WIDTH]) + gb_ref[...])

    return pl.pallas_call(
        body, name="mix_in_fwd", grid=(L // tm,),
        in_specs=[_rows(tm, D_MODEL), _whole(), _whole(), _whole()],
        out_specs=[_residue_spec(d, tm) for d in DILATIONS] * 3 + [_rows(tm, SSM_WIDTH), _rows(tm, 2 * D_MODEL)],
        out_shape=[_residue_shape(d, L, BF16) for d in DILATIONS] * 3
        + [jax.ShapeDtypeStruct((L, SSM_WIDTH), F32), jax.ShapeDtypeStruct((L, 2 * D_MODEL), F32)],
        scratch_shapes=[_residue_scratch(tm)],
        compiler_params=_params(),
    )(x, g, w_in, gate_bias)


def _mix_in_bwd(dx2, x, g, dqkv, du, dgp, w_in):
    L = x.shape[0]
    tm = min(ROW_TILE, L)

    def body(dx2_ref, x_ref, g_ref, *refs):
        piece_refs = refs[:9]
        du_ref, dgp_ref, w_ref, dx1_ref, h_ref, dz_ref, dg_ref, scr = refs[9:]
        i = pl.program_id(0)
        gv = g_ref[...]
        r, xhat = _rms(x_ref[...])
        h_ref[...] = (xhat * gv).astype(BF16)
        for part in range(3):
            for grp, d in enumerate(DILATIONS):
                c0 = part * ATTN_WIDTH + grp * GROUP_WIDTH
                dz_ref[:, c0:c0 + GROUP_WIDTH] = _from_residues(piece_refs[3 * part + grp], scr, d).astype(BF16)
        dz_ref[:, _C_U:_C_G] = du_ref[...].astype(BF16)
        dz_ref[:, _C_G:IN_WIDTH] = dgp_ref[...]
        dh = _dot_nt(dz_ref[...], w_ref[...])

        @pl.when(i == 0)
        def _():
            dg_ref[...] = jnp.zeros_like(dg_ref)

        dg_ref[...] += jnp.sum(dh * xhat, axis=0, keepdims=True)
        dx1_ref[...] = dx2_ref[...] + _rms_bwd(dh, gv, r, xhat)

    return pl.pallas_call(
        body, name="mix_in_bwd", grid=(L // tm,),
        in_specs=[_rows(tm, D_MODEL), _rows(tm, D_MODEL), _whole()] + [_residue_spec(d, tm) for d in DILATIONS] * 3
        + [_rows(tm, SSM_WIDTH), _rows(tm, 2 * D_MODEL), _whole()],
        out_specs=[_rows(tm, D_MODEL), _rows(tm, D_MODEL), _rows(tm, IN_WIDTH), _acc_row(D_MODEL)],
        out_shape=[jax.ShapeDtypeStruct((L, D_MODEL), F32), jax.ShapeDtypeStruct((L, D_MODEL), BF16),
                   jax.ShapeDtypeStruct((L, IN_WIDTH), BF16), jax.ShapeDtypeStruct((1, D_MODEL), F32)],
        scratch_shapes=[_residue_scratch(tm)],
        compiler_params=_params(),
    )(dx2, x, g, *dqkv, du, dgp, w_in)


def _bucket_onehot():
    qi = jnp.arange(ATTN_BLOCK)[:, None]
    kj = jnp.arange(2 * ATTN_BLOCK)[None, :]
    steps = jnp.maximum(qi + ATTN_BLOCK - kj, 0)
    max_exact = N_BUCKETS // 2
    out = []
    for d in DILATIONS:
        dist = steps * d
        df = jnp.maximum(dist, 1).astype(F32)
        large = max_exact + (jnp.log(df / max_exact) / math.log(MAX_DISTANCE / max_exact)
                             * (N_BUCKETS - max_exact)).astype(jnp.int32)
        large = jnp.minimum(large, N_BUCKETS - 1)
        bucket = jnp.where(dist < max_exact, dist, large).reshape(-1)
        out.append((bucket[None, :] == jnp.arange(N_BUCKETS)[:, None]).astype(F32))
    return jnp.stack(out)


def _bias_expand(table_t, onehot):
    n = onehot.shape[-1]

    def body(t_ref, oh_ref, o_ref):
        bias = _dot_exact(t_ref[...], oh_ref[...])
        col = lax.broadcasted_iota(jnp.int32, (8, n), 1)
        qi = col // (2 * ATTN_BLOCK)
        kj = col - qi * (2 * ATTN_BLOCK)
        steps = qi + ATTN_BLOCK - kj
        band = (steps >= 0) & (steps <= WINDOW_STEPS)
        o_ref[0] = jnp.where(band & (kj >= ATTN_BLOCK), bias, NEG_INF)
        o_ref[1] = jnp.where(band, bias, NEG_INF)

    return pl.pallas_call(
        body, name="bias_expand", grid=(3,),
        in_specs=[pl.BlockSpec((None, 8, N_BUCKETS), lambda g: (g, 0, 0)),
                  pl.BlockSpec((None, N_BUCKETS, n), lambda g: (g, 0, 0))],
        out_specs=pl.BlockSpec((None, 2, 8, n), lambda g: (g, 0, 0, 0)),
        out_shape=jax.ShapeDtypeStruct((3, 2, 8, n), F32),
        compiler_params=_params(),
    )(table_t, onehot)


def _bias_reduce(dsum, onehot):
    n = onehot.shape[-1]

    def body(d_ref, oh_ref, o_ref):
        o_ref[...] = _dot_nt_exact(d_ref[...], oh_ref[...])

    return pl.pallas_call(
        body, name="bias_reduce", grid=(3,),
        in_specs=[pl.BlockSpec((None, 8, n), lambda g: (g, 0, 0)),
                  pl.BlockSpec((None, N_BUCKETS, n), lambda g: (g, 0, 0))],
        out_specs=pl.BlockSpec((None, 8, N_BUCKETS), lambda g: (g, 0, 0)),
        out_shape=jax.ShapeDtypeStruct((3, 8, N_BUCKETS), F32),
        compiler_params=_params(),
    )(dsum, onehot)


def _head_of_col(rows):
    return lax.broadcasted_iota(jnp.int32, (rows, GROUP_WIDTH), 1) // HEAD_DIM


def _attn_specs(qb):
    rows = qb * ATTN_BLOCK
    cur = pl.BlockSpec((None, rows, GROUP_WIDTH), lambda r, n: (r, n, 0))
    prev = pl.BlockSpec((None, ATTN_BLOCK, GROUP_WIDTH), lambda r, n: (r, jnp.maximum(n * qb - 1, 0), 0))
    bias = pl.BlockSpec((2, HEADS_PER_GROUP, ATTN_BLOCK, 2 * ATTN_BLOCK), lambda r, n: (0, 0, 0, 0))
    return cur, prev, bias


def _attn_fwd(q, k, v, bias, name):
    d, M, _ = q.shape
    nb = M // ATTN_BLOCK
    qb = min(ATTN_QB, nb)

    def body(q_ref, kp_ref, kc_ref, vp_ref, vc_ref, bias_ref, o_ref, lse_ref):
        n = pl.program_id(1)
        q_head = _head_of_col(ATTN_BLOCK)
        kv_head = _head_of_col(2 * ATTN_BLOCK)
        kwin = jnp.concatenate([kp_ref[...], kc_ref[...]], axis=0)
        vwin = jnp.concatenate([vp_ref[...], vc_ref[...]], axis=0)
        for b in range(qb):
            rows = slice(b * ATTN_BLOCK, (b + 1) * ATTN_BLOCK)
            window = slice(b * ATTN_BLOCK, (b + 2) * ATTN_BLOCK)
            variant = jnp.minimum(n, 1) if b == 0 else 1
            qv = q_ref[rows, :]
            kk = kwin[window]
            vv = vwin[window]
            o_acc = jnp.zeros((ATTN_BLOCK, GROUP_WIDTH), F32)
            lse_acc = jnp.zeros((ATTN_BLOCK, GROUP_WIDTH), F32)
            for hh in range(HEADS_PER_GROUP):
                hm = q_head == hh
                qh = jnp.where(hm, qv, jnp.zeros_like(qv))
                logits = _dot_nt(qh, kk) + bias_ref[variant, hh]
                m = jnp.max(logits, axis=1, keepdims=True)
                p = jnp.exp(logits - m)
                vh = jnp.where(kv_head == hh, vv, jnp.ones_like(vv))
                pv = _dot(p.astype(BF16), vh)
                c_sum = ((hh + 1) % HEADS_PER_GROUP) * HEAD_DIM
                den = pv[:, c_sum:c_sum + 1]
                o_acc = jnp.where(hm, pv * (1.0 / den), o_acc)
                lse_acc = jnp.where(hm, m + jnp.log(den), lse_acc)
            o_ref[rows, :] = o_acc
            lse_ref[rows, :] = lse_acc

    cur, prev, full = _attn_specs(qb)
    return pl.pallas_call(
        body, name=name, grid=(d, nb // qb),
        in_specs=[cur, prev, cur, prev, cur, full],
        out_specs=[cur, cur],
        out_shape=[jax.ShapeDtypeStruct((d, M, GROUP_WIDTH), F32)] * 2,
        compiler_params=_params(),
    )(q, k, k, v, v, bias)


def _attn_bwd(q, k, v, do, lse, delta, bias, name):
    d, M, _ = q.shape
    nb = M // ATTN_BLOCK
    qb = min(ATTN_QB, nb)
    ns = nb // qb
    rows_q = qb * ATTN_BLOCK
    last = slice(rows_q - ATTN_BLOCK, rows_q)

    def body(q_ref, kp_ref, kc_ref, vp_ref, vc_ref, do_ref, lse_ref, dl_ref, bias_ref,
             dq_ref, dk_ref, dv_ref, dsum_ref, pk_ref, pv_ref, wk_ref, wv_ref):
        r = pl.program_id(0)
        n = pl.program_id(1)

        @pl.when((r == 0) & (n == 0))
        def _():
            dsum_ref[...] = jnp.zeros_like(dsum_ref)

        @pl.when(n == 0)
        def _():
            pk_ref[...] = jnp.zeros_like(pk_ref)
            pv_ref[...] = jnp.zeros_like(pv_ref)

        @pl.when(n < ns)
        def _():
            q_head = _head_of_col(ATTN_BLOCK)
            kwin = jnp.concatenate([kp_ref[...], kc_ref[...]], axis=0)
            vwin = jnp.concatenate([vp_ref[...], vc_ref[...]], axis=0)
            wk_ref[...] = jnp.zeros_like(wk_ref)
            wv_ref[...] = jnp.zeros_like(wv_ref)
            for b in range(qb):
                rows = slice(b * ATTN_BLOCK, (b + 1) * ATTN_BLOCK)
                window = slice(b * ATTN_BLOCK, (b + 2) * ATTN_BLOCK)
                variant = jnp.minimum(n, 1) if b == 0 else 1
                qv = q_ref[rows, :]
                dov = do_ref[rows, :]
                kk = kwin[window]
                vv = vwin[window]
                dq_acc = jnp.zeros((ATTN_BLOCK, GROUP_WIDTH), F32)
                dkk = jnp.zeros((2 * ATTN_BLOCK, GROUP_WIDTH), F32)
                dvv = jnp.zeros((2 * ATTN_BLOCK, GROUP_WIDTH), F32)
                for hh in range(HEADS_PER_GROUP):
                    hm = q_head == hh
                    c0 = hh * HEAD_DIM
                    qh = jnp.where(hm, qv, jnp.zeros_like(qv))
                    doh = jnp.where(hm, dov, jnp.zeros_like(dov))
                    logits = _dot_nt(qh, kk) + bias_ref[variant, hh]
                    p = jnp.exp(logits - lse_ref[rows, c0:c0 + 1])
                    dp = _dot_nt(doh, vv)
                    ds = p * (dp - dl_ref[rows, c0:c0 + 1])
                    dsum_ref[hh] += ds
                    ds16 = ds.astype(BF16)
                    dq_acc = jnp.where(hm, _dot(ds16, kk), dq_acc)
                    dkk = dkk + _dot_tn(ds16, qh)
                    dvv = dvv + _dot_tn(p.astype(BF16), doh)
                dq_ref[rows, :] = (dq_acc * Q_SCALE).astype(BF16)
                wk_ref[window, :] += dkk
                wv_ref[window, :] += dvv
            for out_ref, part_ref, win_ref in ((dk_ref, pk_ref, wk_ref), (dv_ref, pv_ref, wv_ref)):
                if qb > 1:
                    out_ref[0:rows_q - ATTN_BLOCK, :] = part_ref[0:rows_q - ATTN_BLOCK, :].astype(BF16)
                out_ref[last, :] = (part_ref[last, :] + win_ref[0:ATTN_BLOCK, :]).astype(BF16)
                part_ref[...] = win_ref[ATTN_BLOCK:, :]

        @pl.when(n == ns)
        def _():
            dk_ref[...] = pk_ref[...].astype(BF16)
            dv_ref[...] = pv_ref[...].astype(BF16)

    def clamp(n):
        return jnp.minimum(n, ns - 1)

    cur = pl.BlockSpec((None, rows_q, GROUP_WIDTH), lambda r, n: (r, clamp(n), 0))
    prev = pl.BlockSpec((None, ATTN_BLOCK, GROUP_WIDTH), lambda r, n: (r, jnp.maximum(clamp(n) * qb - 1, 0), 0))
    lag = pl.BlockSpec((None, rows_q, GROUP_WIDTH), lambda r, n: (r, jnp.maximum(n - 1, 0), 0))
    full = pl.BlockSpec((2, HEADS_PER_GROUP, ATTN_BLOCK, 2 * ATTN_BLOCK), lambda r, n: (0, 0, 0, 0))
    acc = pl.BlockSpec((HEADS_PER_GROUP, ATTN_BLOCK, 2 * ATTN_BLOCK), lambda r, n: (0, 0, 0))
    return pl.pallas_call(
        body, name=name, grid=(d, ns + 1),
        in_specs=[cur, prev, cur, prev, cur, cur, cur, cur, full],
        out_specs=[cur, lag, lag, acc],
        out_shape=[jax.ShapeDtypeStruct((d, M, GROUP_WIDTH), BF16)] * 3
        + [jax.ShapeDtypeStruct((HEADS_PER_GROUP, ATTN_BLOCK, 2 * ATTN_BLOCK), F32)],
        scratch_shapes=[pltpu.VMEM((rows_q, GROUP_WIDTH), F32), pltpu.VMEM((rows_q, GROUP_WIDTH), F32),
                        pltpu.VMEM((rows_q + ATTN_BLOCK, GROUP_WIDTH), F32),
                        pltpu.VMEM((rows_q + ATTN_BLOCK, GROUP_WIDTH), F32)],
        compiler_params=_params(),
    )(q, k, k, v, v, do, lse, delta, bias)


def _disc_math(a_re, a_im, ldt, b_re, b_im):
    dt = jnp.exp(ldt)
    mag = jnp.exp(a_re * dt)
    ab_re = mag * jnp.cos(a_im * dt)
    ab_im = mag * jnp.sin(a_im * dt)
    den = a_re * a_re + a_im * a_im
    xr = ab_re - 1.0
    coef_re = (xr * a_re + ab_im * a_im) / den
    coef_im = (ab_im * a_re - xr * a_im) / den
    return ab_re, ab_im, coef_re * b_re - coef_im * b_im, coef_re * b_im + coef_im * b_re


def _block_diag_mask():
    row_g = lax.broadcasted_iota(jnp.int32, (SSM_WIDTH, 2 * NS), 0) // SSM_GROUP
    col = lax.broadcasted_iota(jnp.int32, (SSM_WIDTH, 2 * NS), 1)
    col_g = jnp.where(col >= NS, col - NS, col) // SSM_STATE
    return row_g == col_g


def _disc_fwd(a_re, a_im, ldt, b_re, b_im, c_re, c_im):
    def body(are_ref, aim_ref, ldt_ref, bre_ref, bim_ref, cre_ref, cim_ref, pw_ref, pwr_ref, bd_ref, cdt_ref):
        ab_re, ab_im, bb_re, bb_im = _disc_math(are_ref[...], aim_ref[...], ldt_ref[...], bre_ref[...], bim_ref[...])
        row = lax.broadcasted_iota(jnp.int32, (8, NS), 0)
        pr, pi = ab_re, ab_im
        t_re = jnp.zeros((8, NS), F32)
        t_im = jnp.zeros((8, NS), F32)
        u_re = jnp.zeros((8, NS), F32)
        u_im = jnp.zeros((8, NS), F32)
        for j in range(8):
            t_re = jnp.where(row == j, pr, t_re)
            t_im = jnp.where(row == j, pi, t_im)
            u_re = jnp.where(row == 7 - j, pr, u_re)
            u_im = jnp.where(row == 7 - j, pi, u_im)
            pr, pi = pr * ab_re - pi * ab_im, pr * ab_im + pi * ab_re
        pw_ref[0] = t_re
        pw_ref[1] = t_im
        pwr_ref[0] = u_re
        pwr_ref[1] = u_im
        mask = _block_diag_mask()
        zero = jnp.zeros((SSM_WIDTH, 2 * NS), F32)
        bfull = jnp.concatenate([jnp.concatenate([bb_re] * SSM_GROUPS, axis=0),
                                 jnp.concatenate([bb_im] * SSM_GROUPS, axis=0)], axis=1)
        bd_ref[...] = jnp.where(mask, bfull, zero).astype(BF16)
        cfull = jnp.concatenate([jnp.concatenate([cre_ref[...]] * SSM_GROUPS, axis=0),
                                 jnp.concatenate([-cim_ref[...]] * SSM_GROUPS, axis=0)], axis=1)
        cdt_ref[...] = jnp.where(mask, cfull, zero).astype(BF16)

    return pl.pallas_call(
        body, name="s5_disc_fwd",
        in_specs=[_whole()] * 7, out_specs=[_whole()] * 4,
        out_shape=[jax.ShapeDtypeStruct((2, 8, NS), F32), jax.ShapeDtypeStruct((2, 8, NS), F32),
                   jax.ShapeDtypeStruct((SSM_WIDTH, 2 * NS), BF16), jax.ShapeDtypeStruct((SSM_WIDTH, 2 * NS), BF16)],
        compiler_params=_params(),
    )(a_re, a_im, ldt, b_re, b_im, c_re, c_im)


def _disc_bwd(a_re, a_im, ldt, b_re, b_im, d_bd, d_cdt, d_ab, group_sum):
    def body(are_ref, aim_ref, ldt_ref, bre_ref, bim_ref, dbd_ref, dcdt_ref, dab_ref, gs_ref,
             dare_ref, daim_ref, dldt_ref, dbre_ref, dbim_ref, dcre_ref, dcim_ref):
        col = lax.broadcasted_iota(jnp.int32, (SSM_GROUP, 2 * NS), 1)
        col_g = jnp.where(col >= NS, col - NS, col) // SSM_STATE
        acc_b = jnp.zeros((SSM_GROUP, 2 * NS), F32)
        acc_c = jnp.zeros((SSM_GROUP, 2 * NS), F32)
        for g in range(SSM_GROUPS):
            rows = slice(g * SSM_GROUP, (g + 1) * SSM_GROUP)
            acc_b = acc_b + jnp.where(col_g == g, dbd_ref[rows, :], 0.0)
            acc_c = acc_c + jnp.where(col_g == g, dcdt_ref[rows, :], 0.0)
        dcre_ref[...] = acc_c[:, :NS]
        dcim_ref[...] = -acc_c[:, NS:]
        dab_re = jnp.sum(dab_ref[0], axis=0, keepdims=True)
        dab_im = jnp.sum(dab_ref[1], axis=0, keepdims=True)
        _, vjp = jax.vjp(_disc_math, are_ref[...], aim_ref[...], ldt_ref[...], bre_ref[...], bim_ref[...])
        d_are, d_aim, d_ldt, d_bre, d_bim = vjp((dab_re, dab_im, acc_b[:, :NS], acc_b[:, NS:]))
        dare_ref[...] = d_are
        daim_ref[...] = d_aim
        dbre_ref[...] = d_bre
        dbim_ref[...] = d_bim
        dldt_ref[...] = _dot_exact(jnp.broadcast_to(d_ldt, (8, NS)), gs_ref[...])

    vec = jax.ShapeDtypeStruct((1, NS), F32)
    mat = jax.ShapeDtypeStruct((SSM_GROUP, NS), F32)
    return pl.pallas_call(
        body, name="s5_disc_bwd",
        in_specs=[_whole()] * 9, out_specs=[_whole()] * 7,
        out_shape=[vec, vec, jax.ShapeDtypeStruct((8, 128), F32), mat, mat, mat, mat],
        compiler_params=_params(),
    )(a_re, a_im, ldt, b_re, b_im, d_bd, d_cdt, d_ab, group_sum)


def _scan_blocks(buf, pw_ref, carry_ref, n_blocks, reverse):
    row = lax.broadcasted_iota(jnp.int32, (8, SCAN_LANES), 0)
    for lc in range(NS // SCAN_LANES):
        re_cols = pl.ds(lc * SCAN_LANES, SCAN_LANES)
        im_cols = pl.ds(NS + lc * SCAN_LANES, SCAN_LANES)
        pr = pw_ref[0, :, re_cols]
        pi = pw_ref[1, :, re_cols]
        if reverse:
            pi = -pi
            base = [(7, 1), (6, 2), (4, 4)]
            coef = [(jnp.where(row < 8 - k, pr[j:j + 1], 0.0), jnp.where(row < 8 - k, pi[j:j + 1], 0.0), 8 - k)
                    for j, k in base]
        else:
            base = [(0, 1), (1, 2), (3, 4)]
            coef = [(jnp.where(row >= k, pr[j:j + 1], 0.0), jnp.where(row >= k, pi[j:j + 1], 0.0), k)
                    for j, k in base]

        def step(i, carry, pr=pr, pi=pi, coef=coef, re_cols=re_cols, im_cols=im_cols):
            cr, ci = carry
            blk = (n_blocks - 1 - i) if reverse else i
            rows = pl.ds(pl.multiple_of(blk * 8, 8), 8)
            xr = buf[rows, re_cols]
            xi = buf[rows, im_cols]
            for kr, ki, shift in coef:
                sr = pltpu.roll(xr, shift, 0)
                si = pltpu.roll(xi, shift, 0)
                xr, xi = xr + kr * sr - ki * si, xi + kr * si + ki * sr
            xr, xi = xr + pr * cr - pi * ci, xi + pr * ci + pi * cr
            buf[rows, re_cols] = xr
            buf[rows, im_cols] = xi
            edge = slice(0, 1) if reverse else slice(7, 8)
            return xr[edge], xi[edge]

        cr, ci = lax.fori_loop(0, n_blocks, step, (carry_ref[0:1, re_cols], carry_ref[0:1, im_cols]))
        carry_ref[0:1, re_cols] = cr
        carry_ref[0:1, im_cols] = ci


_SUPER_GROUPS = 16
_SUPER_BLOCKS = [
    (slice(k * _SUPER_GROUPS * SSM_GROUP, (k + 1) * _SUPER_GROUPS * SSM_GROUP),
     [slice(half + k * _SUPER_GROUPS * SSM_STATE, half + (k + 1) * _SUPER_GROUPS * SSM_STATE) for half in (0, NS)])
    for k in range(SSM_GROUPS // _SUPER_GROUPS)]


def _ssm_fwd(u, bd, cdt, d_skip, pw):
    L = u.shape[0]
    tc = min(SSM_CHUNK, L)

    def body(u_ref, bd_ref, cdt_ref, dsk_ref, pw_ref, y_ref, s_ref, carry_ref):
        @pl.when(pl.program_id(0) == 0)
        def _():
            carry_ref[...] = jnp.zeros_like(carry_ref)

        uv = u_ref[...]
        u16 = uv.astype(BF16)
        for ch, states in _SUPER_BLOCKS:
            for st in states:
                s_ref[:, st] = _dot(u16[:, ch], bd_ref[ch, st])
        _scan_blocks(s_ref, pw_ref, carry_ref, tc // 8, reverse=False)
        for ch, states in _SUPER_BLOCKS:
            y_ref[:, ch] = (sum(_dot_nt(s_ref[:, st].astype(BF16), cdt_ref[ch, st]) for st in states)
                            + dsk_ref[:, ch] * uv[:, ch])

    return pl.pallas_call(
        body, name="s5_fwd", grid=(L // tc,),
        in_specs=[_rows(tc, SSM_WIDTH), _whole(), _whole(), _whole(), _whole()],
        out_specs=[_rows(tc, SSM_WIDTH), _rows(tc, 2 * NS)],
        out_shape=[jax.ShapeDtypeStruct((L, SSM_WIDTH), F32), jax.ShapeDtypeStruct((L, 2 * NS), F32)],
        scratch_shapes=[pltpu.VMEM((8, 2 * NS), F32)],
        compiler_params=_params(),
    )(u, bd, cdt, d_skip, pw)


def _ssm_bwd(dy, u, s, bd, cdt, d_skip, pwr):
    L = u.shape[0]
    tc = min(SSM_CHUNK, L)
    nc = L // tc
    blocks = tc // 8

    def body(dy_ref, u_ref, s_ref, sprev_ref, bd_ref, cdt_ref, dsk_ref, pwr_ref,
             du_ref, ddsk_ref, dbd_ref, dcdt_ref, dab_ref, g_ref, sx_ref, carry_ref):
        i = pl.program_id(0)

        @pl.when(i == 0)
        def _():
            carry_ref[...] = jnp.zeros_like(carry_ref)
            ddsk_ref[...] = jnp.zeros_like(ddsk_ref)
            dbd_ref[...] = jnp.zeros_like(dbd_ref)
            dcdt_ref[...] = jnp.zeros_like(dcdt_ref)
            dab_ref[...] = jnp.zeros_like(dab_ref)

        dyv = dy_ref[...]
        uv = u_ref[...]
        dy16 = dyv.astype(BF16)
        u16 = uv.astype(BF16)
        for ch, states in _SUPER_BLOCKS:
            for st in states:
                g_ref[:, st] = _dot(dy16[:, ch], cdt_ref[ch, st])
        _scan_blocks(g_ref, pwr_ref, carry_ref, blocks, reverse=True)
        ddsk_ref[...] += jnp.sum(dyv * uv, axis=0, keepdims=True)
        for ch, states in _SUPER_BLOCKS:
            du = dsk_ref[:, ch] * dyv[:, ch]
            for st in states:
                g16 = g_ref[:, st].astype(BF16)
                du = du + _dot_nt(g16, bd_ref[ch, st])
                dbd_ref[ch, st] += _dot_tn(u16[:, ch], g16)
                dcdt_ref[ch, st] += _dot_tn(dy16[:, ch], s_ref[:, st].astype(BF16))
            du_ref[:, ch] = du

        sx_ref[pl.ds(8, tc), :] = s_ref[...]
        sx_ref[pl.ds(0, 8), :] = jnp.where(i == nc - 1, 0.0, sprev_ref[...])
        row = lax.broadcasted_iota(jnp.int32, (8, SCAN_LANES), 0)
        for lc in range(NS // SCAN_LANES):
            re_cols = pl.ds(lc * SCAN_LANES, SCAN_LANES)
            im_cols = pl.ds(NS + lc * SCAN_LANES, SCAN_LANES)

            def step(b, acc, re_cols=re_cols, im_cols=im_cols):
                ar, ai = acc
                off = pl.multiple_of(b * 8, 8)
                gr = g_ref[pl.ds(off, 8), re_cols]
                gi = g_ref[pl.ds(off, 8), im_cols]
                before = pl.ds(off, 8)
                here = pl.ds(off + 8, 8)
                sr = jnp.where(row == 0, sx_ref[before, re_cols][7:8], pltpu.roll(sx_ref[here, re_cols], 1, 0))
                si = jnp.where(row == 0, sx_ref[before, im_cols][7:8], pltpu.roll(sx_ref[here, im_cols], 1, 0))
                return ar + gr * sr + gi * si, ai + gi * sr - gr * si

            zero = jnp.zeros((8, SCAN_LANES), F32)
            ar, ai = lax.fori_loop(0, blocks, step, (zero, zero))
            dab_ref[0, :, re_cols] += ar
            dab_ref[1, :, re_cols] += ai

    rev = lambda i: (nc - 1 - i, 0)
    sprev = pl.BlockSpec((8, 2 * NS), lambda i: (jnp.maximum((nc - 1 - i) * blocks - 1, 0), 0))
    return pl.pallas_call(
        body, name="s5_bwd", grid=(nc,),
        in_specs=[pl.BlockSpec((tc, SSM_WIDTH), rev), pl.BlockSpec((tc, SSM_WIDTH), rev),
                  pl.BlockSpec((tc, 2 * NS), rev), sprev, _whole(), _whole(), _whole(), _whole()],
        out_specs=[pl.BlockSpec((tc, SSM_WIDTH), rev), _whole(), _whole(), _whole(), _whole()],
        out_shape=[jax.ShapeDtypeStruct((L, SSM_WIDTH), F32), jax.ShapeDtypeStruct((1, SSM_WIDTH), F32),
                   jax.ShapeDtypeStruct((SSM_WIDTH, 2 * NS), F32), jax.ShapeDtypeStruct((SSM_WIDTH, 2 * NS), F32),
                   jax.ShapeDtypeStruct((2, 8, NS), F32)],
        scratch_shapes=[pltpu.VMEM((tc, 2 * NS), F32), pltpu.VMEM((tc + 8, 2 * NS), F32), pltpu.VMEM((8, 2 * NS), F32)],
        compiler_params=_params(),
    )(dy, u, s, s, bd, cdt, d_skip, pwr)


def _branches(o_attn, y, gates, w_ab, w_glu, w_sb):
    ya = _dot(o_attn.astype(BF16), w_ab[...])
    gel = _gelu(y)
    glu = _dot(gel.astype(BF16), w_glu[...])
    p = glu[:, :SSM_WIDTH]
    sg = _sigmoid(glu[:, SSM_WIDTH:])
    ys2 = p * sg
    ysb = _dot(ys2.astype(BF16), w_sb[...])
    ga = gates[:, :D_MODEL]
    gs = gates[:, D_MODEL:]
    return ya, gel, p, sg, ys2, ysb, ga, gs


def _mix_out_fwd(x1, o_g, lse_g, y, gates, w_ab, w_glu, w_sb, w_out):
    L = x1.shape[0]
    tm = min(ROW_TILE, L)

    def body(x_ref, o0, o1, o2, l0, l1, l2, y_ref, gate_ref, wab_ref, wglu_ref, wsb_ref, wout_ref,
             x2_ref, oat_ref, lse0, lse1, lse2, scr):
        la, lb, lc = (_from_residues(ref, scr, d) for ref, d in zip((l0, l1, l2), DILATIONS))
        m = jnp.maximum(jnp.maximum(la, lb), lc)
        ea, eb, ec = jnp.exp(la - m), jnp.exp(lb - m), jnp.exp(lc - m)
        tot = ea + eb + ec
        oa, ob, oc = (_from_residues(ref, scr, d) for ref, d in zip((o0, o1, o2), DILATIONS))
        o_attn = (ea * oa + eb * ob + ec * oc) / tot
        oat_ref[...] = o_attn
        lse = m + jnp.log(tot)
        for ref, d in zip((lse0, lse1, lse2), DILATIONS):
            _to_residues(lse, ref, scr, d)
        ya, _, _, _, _, ysb, ga, gs = _branches(o_attn, y_ref[...], gate_ref[...], wab_ref, wglu_ref, wsb_ref)
        mix = ga * ya + gs * ysb
        x2_ref[...] = x_ref[...] + _dot(mix.astype(BF16), wout_ref[...])

    res = [_residue_spec(d, tm) for d in DILATIONS]
    return pl.pallas_call(
        body, name="mix_out_fwd", grid=(L // tm,),
        in_specs=[_rows(tm, D_MODEL)] + res * 2 + [_rows(tm, SSM_WIDTH), _rows(tm, 2 * D_MODEL)] + [_whole()] * 4,
        out_specs=[_rows(tm, D_MODEL), _rows(tm, GROUP_WIDTH)] + res,
        out_shape=[jax.ShapeDtypeStruct((L, D_MODEL), F32), jax.ShapeDtypeStruct((L, GROUP_WIDTH), F32)]
        + [_residue_shape(d, L, F32) for d in DILATIONS],
        scratch_shapes=[_residue_scratch(tm)],
        compiler_params=_params(),
    )(x1, *o_g, *lse_g, y, gates, w_ab, w_glu, w_sb, w_out)


def _mix_out_bwd(dx2, o_attn, y, gates, w_ab, w_glu, w_sb, w_out, head_sum):
    L = dx2.shape[0]
    tm = min(ROW_TILE, L)

    def body(dx_ref, oat_ref, y_ref, gate_ref, wab_ref, wglu_ref, wsb_ref, wout_ref, hs_ref,
             do0, do1, do2, dl0, dl1, dl2, dy_ref, dgp_ref, mix_ref, dya_ref, dys_ref, ys2_ref, gel_ref, dglu_ref,
             dgb_ref, scr):
        i = pl.program_id(0)
        o_attn = oat_ref[...]
        yv = y_ref[...]
        ya, gel, p, sg, ys2, ysb, ga, gs = _branches(o_attn, yv, gate_ref[...], wab_ref, wglu_ref, wsb_ref)
        mix_ref[...] = (ga * ya + gs * ysb).astype(BF16)
        ys2_ref[...] = ys2.astype(BF16)
        gel_ref[...] = gel.astype(BF16)
        dmix = _dot_nt(dx_ref[...].astype(BF16), wout_ref[...])
        dgp = jnp.concatenate([dmix * ya * ga * (1.0 - ga), dmix * ysb * gs * (1.0 - gs)], axis=1)
        dgp_ref[...] = dgp.astype(BF16)

        @pl.when(i == 0)
        def _():
            dgb_ref[...] = jnp.zeros_like(dgb_ref)

        dgb_ref[...] += jnp.sum(dgp, axis=0, keepdims=True)
        dya = (dmix * ga).astype(BF16)
        dys = (dmix * gs).astype(BF16)
        dya_ref[...] = dya
        dys_ref[...] = dys
        d_o = _dot_nt(dya, wab_ref[...])
        delta = _dot_exact(d_o * o_attn, hs_ref[...])
        for do_ref, dl_ref, d in zip((do0, do1, do2), (dl0, dl1, dl2), DILATIONS):
            _to_residues(d_o, do_ref, scr, d)
            _to_residues(delta, dl_ref, scr, d)
        dys2 = _dot_nt(dys, wsb_ref[...])
        dglu = jnp.concatenate([dys2 * sg, dys2 * p * sg * (1.0 - sg)], axis=1).astype(BF16)
        dglu_ref[...] = dglu
        dy_ref[...] = _dot_nt(dglu, wglu_ref[...]) * _gelu_grad(yv)

    grp = _rows(tm, GROUP_WIDTH)
    wide = _rows(tm, D_MODEL)
    half = _rows(tm, SSM_WIDTH)
    res = [_residue_spec(d, tm) for d in DILATIONS]
    sds = jax.ShapeDtypeStruct
    return pl.pallas_call(
        body, name="mix_out_bwd", grid=(L // tm,),
        in_specs=[wide, grp, half, _rows(tm, 2 * D_MODEL)] + [_whole()] * 5,
        out_specs=res + res + [half, _rows(tm, 2 * D_MODEL), wide, wide, wide, half, half, wide, _acc_row(2 * D_MODEL)],
        out_shape=[_residue_shape(d, L, BF16) for d in DILATIONS] + [_residue_shape(d, L, F32) for d in DILATIONS]
        + [sds((L, SSM_WIDTH), F32),
           sds((L, 2 * D_MODEL), BF16), sds((L, D_MODEL), BF16), sds((L, D_MODEL), BF16),
           sds((L, D_MODEL), BF16), sds((L, SSM_WIDTH), BF16), sds((L, SSM_WIDTH), BF16),
           sds((L, D_MODEL), BF16), sds((1, 2 * D_MODEL), F32)],
        scratch_shapes=[_residue_scratch(tm)],
        compiler_params=_params(),
    )(dx2, o_attn, y, gates, w_ab, w_glu, w_sb, w_out, head_sum)


def _adamw(w, g, m, v, name):
    R, C = w.shape
    tr = _row_tile(R, max(8, ADAMW_BLOCK_BYTES // (4 * C)))

    def body(w_ref, g_ref, m_ref, v_ref, d_ref, mo_ref, vo_ref):
        gv = g_ref[...]
        mn = ADAM_B1 * m_ref[...] + (1.0 - ADAM_B1) * gv
        vn = ADAM_B2 * v_ref[...] + (1.0 - ADAM_B2) * (gv * gv)
        m_hat = mn / (1.0 - ADAM_B1 ** ADAM_STEP)
        v_hat = vn / (1.0 - ADAM_B2 ** ADAM_STEP)
        d_ref[...] = -ADAM_LR * (m_hat / (jnp.sqrt(v_hat) + ADAM_EPS) + ADAM_WD * w_ref[...])
        mo_ref[...] = mn
        vo_ref[...] = vn

    blk = pl.BlockSpec((tr, C), lambda i: (i, 0))
    return pl.pallas_call(
        body, name=name, grid=(R // tr,),
        in_specs=[blk] * 4, out_specs=[blk] * 3,
        out_shape=[jax.ShapeDtypeStruct((R, C), F32)] * 3,
        compiler_params=_params(),
    )(w, g, m, v)


def _sum_slots(x, name):
    S, R, C = x.shape
    tr = _row_tile(R, 512)

    def body(x_ref, o_ref):
        acc = x_ref[0].astype(F32)
        for k in range(1, S):
            acc = acc + x_ref[k].astype(F32)
        o_ref[...] = acc

    return pl.pallas_call(
        body, name=name, grid=(R // tr,),
        in_specs=[pl.BlockSpec((S, tr, C), lambda i: (0, i, 0))],
        out_specs=pl.BlockSpec((tr, C), lambda i: (i, 0)),
        out_shape=jax.ShapeDtypeStruct((R, C), F32),
        compiler_params=_params(),
    )(x)


def _sum_chips_into_half(u, t, name):
    S, H, C = u.shape
    tr = _row_tile(H, 512)
    hb = H // tr

    def body(s_ref, t_ref, a_ref, b_ref, c_ref, o_ref):
        me = s_ref[1]
        others = (a_ref[...], b_ref[...], c_ref[...])
        acc = None
        for chip in range(S):
            below = others[min(chip, S - 2)]
            above = others[max(chip - 1, 0)]
            term = jnp.where(me == chip, t_ref[...], jnp.where(me > chip, below, above)).astype(F32)
            acc = term if acc is None else acc + term
        o_ref[...] = acc

    x, y, c = lax.axis_index("x"), lax.axis_index("y"), lax.axis_index("c")
    me = 2 * x + y
    scalars = jnp.stack([c, me] + [j + (j >= me).astype(jnp.int32) for j in range(S - 1)]).astype(jnp.int32)
    blk = (None, tr, C)
    return pl.pallas_call(
        body, name=name,
        grid_spec=pltpu.PrefetchScalarGridSpec(
            num_scalar_prefetch=1, grid=(hb,),
            in_specs=[pl.BlockSpec(blk, lambda i, s: (s[1], i, 0))]
            + [pl.BlockSpec(blk, functools.partial(lambda j, i, s: (s[2 + j], i, 0), j)) for j in range(S - 1)],
            out_specs=pl.BlockSpec((tr, C), lambda i, s: (s[0] * hb + i, 0))),
        out_shape=jax.ShapeDtypeStruct((2 * H, C), F32),
        compiler_params=_params(),
    )(scalars, t, u, u, u)


def _add_halves(g, r1, name):
    S, R, C = g.shape
    H = R // 2
    tr = _row_tile(H, 512)
    hb = H // tr

    def body(c_ref, g_ref, r_ref, o_ref):
        o_ref[...] = (g_ref[...] + r_ref[...]).astype(BF16)

    core = lax.axis_index("c").astype(jnp.int32).reshape(1)
    return pl.pallas_call(
        body, name=name,
        grid_spec=pltpu.PrefetchScalarGridSpec(
            num_scalar_prefetch=1, grid=(S, hb),
            in_specs=[pl.BlockSpec((None, tr, C), lambda j, i, c_ref: (j, c_ref[0] * hb + i, 0)),
                      pl.BlockSpec((None, tr, C), lambda j, i, c_ref: (j, i, 0))],
            out_specs=pl.BlockSpec((None, tr, C), lambda j, i, c_ref: (j, i, 0))),
        out_shape=jax.ShapeDtypeStruct((S, H, C), BF16),
        compiler_params=_params(),
    )(core, g, r1)


_ANY = pl.BlockSpec(memory_space=pl.ANY)


def _place():
    x, y, c = lax.axis_index("x"), lax.axis_index("y"), lax.axis_index("c")
    chips = [(1 - x, y), (x, 1 - y), (1 - x, 1 - y)]
    return x, y, c, chips


def _comm_call(body, name, ins, out_shapes, n_remote, n_local):
    return pl.pallas_call(
        body, name=name,
        in_specs=[_ANY] * len(ins), out_specs=[_ANY] * len(out_shapes), out_shape=out_shapes,
        scratch_shapes=[pltpu.SemaphoreType.DMA((n_remote,)), pltpu.SemaphoreType.DMA((n_remote,)),
                        pltpu.SemaphoreType.DMA((max(n_local, 1),))],
    )(*ins)


def _remote(src, dst, send_sems, recv_sems, k, device):
    return pltpu.make_async_remote_copy(src_ref=src, dst_ref=dst, send_sem=send_sems.at[k], recv_sem=recv_sems.at[k],
                                        device_id=device, device_id_type=MESH)


def _gather_body(shapes, handshake):
    n = len(shapes)

    def body(*refs):
        w_refs, out_refs = refs[:n], refs[n:2 * n]
        send_sems, recv_sems = refs[2 * n:2 * n + 2]
        x, y, c, chips = _place()
        me = 2 * x + y
        sibling = (x, y, 1 - c)
        if handshake:
            _handshake([sibling] + [(cx, cy, c) for cx, cy in chips])

        def half(k, chip_idx, core):
            H = shapes[k][0] // 2
            return out_refs[k].at[chip_idx, pl.ds(core * H, H), :]

        mine = [_remote(w_refs[k], out_refs[k].at[me], send_sems, recv_sems, 6 * n + k, sibling) for k in range(n)]
        for cp in mine:
            cp.start()
        first = []
        for k in range(n):
            H = shapes[k][0] // 2
            for j, (cx, cy) in enumerate(chips):
                first.append(_remote(w_refs[k].at[pl.ds(c * H, H), :], half(k, me, c), send_sems, recv_sems,
                                     3 * k + j, (cx, cy, c)))
        for cp in first:
            cp.start()
        passed = []
        for k in range(n):
            for j, (cx, cy) in enumerate(chips):
                landed = half(k, 2 * cx + cy, c)
                _remote(landed, landed, send_sems, recv_sems, 3 * k + j, (cx, cy, c)).wait_recv()
                fwd = _remote(landed, landed, send_sems, recv_sems, 3 * n + 3 * k + j, sibling)
                fwd.start()
                passed.append(fwd)
        for k in range(n):
            for j, (cx, cy) in enumerate(chips):
                other = half(k, 2 * cx + cy, 1 - c)
                _remote(other, other, send_sems, recv_sems, 3 * n + 3 * k + j, sibling).wait_recv()
        for cp in mine:
            cp.wait_recv()
        for cp in first + passed + mine:
            cp.wait_send()

    return body


def _gather_weights(shards, name):
    n = len(shards)
    return _comm_call(_gather_body([w.shape for w in shards], handshake=False), name, shards,
                      [jax.ShapeDtypeStruct((N_SHARD,) + w.shape, w.dtype) for w in shards], 7 * n, 0)


def _gather_weights_behind(shards, name, collective_id):
    return _sequenced(_gather_body([w.shape for w in shards], handshake=True), name, shards,
                      [jax.ShapeDtypeStruct((N_SHARD,) + w.shape, w.dtype) for w in shards], 7 * len(shards),
                      collective_id)


def _handshake(peers):
    barrier = pltpu.get_barrier_semaphore()
    for peer in peers:
        pl.semaphore_signal(barrier, inc=1, device_id=peer, device_id_type=MESH)
    pl.semaphore_wait(barrier, len(peers))


def _sequenced(body, name, ins, out_shapes, n_sems, collective_id):
    hbm = pltpu.MemorySpace.HBM
    in_refs = [jax.new_ref(a, memory_space=hbm) for a in ins]
    out_refs = [jax.empty_ref(s, memory_space=hbm) for s in out_shapes]

    @pl.kernel(mesh=plsc.ScalarSubcoreMesh(axis_name="sequencer", num_cores=1), name=name,
               scratch_types=(pltpu.SemaphoreType.DMA((n_sems,)), pltpu.SemaphoreType.DMA((n_sems,))),
               compiler_params=pltpu.CompilerParams(collective_id=collective_id))
    def launch(send_sems, recv_sems):
        body(*in_refs, *out_refs, send_sems, recv_sems)

    launch()
    return [r[...] for r in out_refs]


def _swap_halves(gs, name, collective_id):
    n = len(gs)

    def body(*refs):
        g_refs, out_refs = refs[:n], refs[n:2 * n]
        send_sems, recv_sems = refs[2 * n:]
        x, y, c, _ = _place()
        _handshake([(x, y, 1 - c)])
        cps = []
        for k in range(n):
            H = gs[k].shape[1] // 2
            cp = _remote(g_refs[k].at[:, pl.ds((1 - c) * H, H), :], out_refs[k], send_sems, recv_sems, k, (x, y, 1 - c))
            cp.start()
            cps.append(cp)
        for cp in cps:
            cp.wait()

    return _sequenced(body, name, gs, [jax.ShapeDtypeStruct((g.shape[0], g.shape[1] // 2, g.shape[2]), g.dtype)
                                       for g in gs], n, collective_id)


def _exchange_chips(ts, name, collective_id):
    n = len(ts)

    def body(*refs):
        t_refs, out_refs = refs[:n], refs[n:2 * n]
        send_sems, recv_sems = refs[2 * n:]
        x, y, c, chips = _place()
        me = 2 * x + y
        _handshake([(cx, cy, c) for cx, cy in chips])
        sent = []
        for k in range(n):
            for j, (cx, cy) in enumerate(chips):
                cp = _remote(t_refs[k].at[2 * cx + cy], out_refs[k].at[me], send_sems, recv_sems, 3 * k + j, (cx, cy, c))
                cp.start()
                sent.append(cp)
        for k in range(n):
            for j, (cx, cy) in enumerate(chips):
                slot = out_refs[k].at[2 * cx + cy]
                _remote(slot, slot, send_sems, recv_sems, 3 * k + j, (cx, cy, c)).wait_recv()
        for cp in sent:
            cp.wait_send()

    return _sequenced(body, name, ts, [jax.ShapeDtypeStruct(t.shape, t.dtype) for t in ts], 3 * n, collective_id)


def _join_halves(fs, name):
    n = len(fs)

    def body(*refs):
        out_refs = refs[n:2 * n]
        send_sems, recv_sems, _ = refs[2 * n:]
        x, y, c, _ = _place()
        sent = []
        for k in range(n):
            H = fs[k].shape[0] // 2
            here = out_refs[k].at[pl.ds(c * H, H), :]
            cp = _remote(here, here, send_sems, recv_sems, k, (x, y, 1 - c))
            cp.start()
            sent.append(cp)
        for k in range(n):
            H = fs[k].shape[0] // 2
            other = out_refs[k].at[pl.ds((1 - c) * H, H), :]
            _remote(other, other, send_sems, recv_sems, k, (x, y, 1 - c)).wait_recv()
        for cp in sent:
            cp.wait_send()

    return pl.pallas_call(
        body, name=name,
        in_specs=[_ANY] * n, out_specs=[_ANY] * n,
        out_shape=[jax.ShapeDtypeStruct(f.shape, f.dtype) for f in fs],
        input_output_aliases={k: k for k in range(n)},
        scratch_shapes=[pltpu.SemaphoreType.DMA((n,)), pltpu.SemaphoreType.DMA((n,)), pltpu.SemaphoreType.DMA((1,))],
    )(*fs)


def _gather_small(v):
    R, C = v.shape

    def body(v_ref, out_ref, send_sems, recv_sems, local_sem):
        x, y, c, _ = _place()
        me = 4 * x + 2 * y + c
        mine = pltpu.make_async_copy(v_ref, out_ref.at[me], local_sem)
        mine.start()
        flips = [(fx, fy, fc) for fx in (0, 1) for fy in (0, 1) for fc in (0, 1)][1:]
        peers = [((1 - x) if fx else x, (1 - y) if fy else y, (1 - c) if fc else c) for fx, fy, fc in flips]
        sent = []
        for j, peer in enumerate(peers):
            cp = pltpu.make_async_remote_copy(
                src_ref=v_ref, dst_ref=out_ref.at[me], send_sem=send_sems.at[j], recv_sem=recv_sems.at[j],
                device_id=peer, device_id_type=MESH)
            cp.start()
            sent.append(cp)
        for j, peer in enumerate(peers):
            slot = out_ref.at[4 * peer[0] + 2 * peer[1] + peer[2]]
            pltpu.make_async_remote_copy(
                src_ref=slot, dst_ref=slot, send_sem=send_sems.at[j], recv_sem=recv_sems.at[j],
                device_id=peer, device_id_type=MESH).wait_recv()
        for cp in sent:
            cp.wait_send()
        mine.wait()

    return pl.pallas_call(
        body, name="gather_small",
        in_specs=[_ANY], out_specs=_ANY,
        out_shape=jax.ShapeDtypeStruct((8, R, C), F32),
        scratch_shapes=[pltpu.SemaphoreType.DMA((7,)), pltpu.SemaphoreType.DMA((7,)), pltpu.SemaphoreType.DMA],
    )(v)


def _reduce_scatter(gs, names, tag, collective_ids):
    r1 = _swap_halves(gs, "reduce_swap_" + tag, collective_ids[0])
    ts = [_add_halves(g, r, "reduce_add_cores_" + nm) for g, r, nm in zip(gs, r1, names)]
    us = _exchange_chips(ts, "reduce_exchange_" + tag, collective_ids[1])
    fs = [_sum_chips_into_half(u, t, "reduce_add_chips_" + nm) for u, t, nm in zip(us, ts, names)]
    return _join_halves(fs, "reduce_join_" + tag)


BIG = ["ffn1_w_gate", "ffn1_w_up", "ffn1_w_down", "w_in", "ssm_w_glu", "w_attn_branch", "w_ssm_branch",
       "w_out", "ffn2_w_gate", "ffn2_w_up", "ffn2_w_down"]
SMALL = ["ffn1_norm", "mix_norm", "gate_bias", "rel_bias_table", "ssm_a_re", "ssm_a_im", "ssm_log_dt",
         "ssm_b_re", "ssm_b_im", "ssm_c_re", "ssm_c_im", "ssm_d", "ffn2_norm", "final_norm"]
ORDER = ["ffn1_norm", "ffn1_w_gate", "ffn1_w_up", "ffn1_w_down", "mix_norm", "w_in", "gate_bias", "rel_bias_table",
         "ssm_a_re", "ssm_a_im", "ssm_log_dt", "ssm_b_re", "ssm_b_im", "ssm_c_re", "ssm_c_im", "ssm_d",
         "ssm_w_glu", "w_attn_branch", "w_ssm_branch", "w_out", "ffn2_norm", "ffn2_w_gate", "ffn2_w_up",
         "ffn2_w_down", "final_norm"]


_SMALL_TILE = 8 * LANES


def _pack_small(arrays):
    rows = []
    for a in arrays:
        flat = a.reshape(-1).astype(F32)
        rows.append(jnp.pad(flat, (0, (-flat.shape[0]) % _SMALL_TILE)).reshape(-1, LANES))
    return jnp.concatenate(rows, axis=0)


def _unpack_small(packed, shapes):
    out, r0 = [], 0
    for shp in shapes:
        n = math.prod(shp)
        rows = 8 * -(-n // _SMALL_TILE)
        out.append(packed[r0:r0 + rows].reshape(-1)[:n].reshape(shp))
        r0 += rows
    return out


def _local_step(x, target, w, small):
    L = x.shape[0]
    row = lambda v: v.reshape(1, -1)

    a_re, a_im = small["ssm_a_re"].reshape(1, NS), small["ssm_a_im"].reshape(1, NS)
    ldt = jnp.repeat(small["ssm_log_dt"].reshape(SSM_GROUPS), SSM_STATE).reshape(1, NS)
    to_cn = lambda b: b.reshape(SSM_GROUPS, SSM_STATE, SSM_GROUP).transpose(2, 0, 1).reshape(SSM_GROUP, NS)
    c_to_cn = lambda c: c.reshape(SSM_GROUPS, SSM_GROUP, SSM_STATE).transpose(1, 0, 2).reshape(SSM_GROUP, NS)
    b_re, b_im = to_cn(small["ssm_b_re"]), to_cn(small["ssm_b_im"])
    c_re, c_im = c_to_cn(small["ssm_c_re"]), c_to_cn(small["ssm_c_im"])
    d_skip = row(small["ssm_d"])
    pw, pwr, bd, cdt = _disc_fwd(a_re, a_im, ldt, b_re, b_im, c_re, c_im)

    onehot = _bucket_onehot()
    table_t = small["rel_bias_table"].T.reshape(3, HEADS_PER_GROUP, N_BUCKETS)
    table_t = jnp.pad(table_t, ((0, 0), (0, 8 - HEADS_PER_GROUP), (0, 0)))
    bias = _bias_expand(table_t, onehot)[:, :, :HEADS_PER_GROUP].reshape(
        3, 2, HEADS_PER_GROUP, ATTN_BLOCK, 2 * ATTN_BLOCK)

    n1, nm, n2, nf = row(small["ffn1_norm"]), row(small["mix_norm"]), row(small["ffn2_norm"]), row(small["final_norm"])
    gate_bias = row(small["gate_bias"])

    x1, a1, b1 = _ffn_fwd(x, n1, w["ffn1_w_gate"], w["ffn1_w_up"], w["ffn1_w_down"], "ffn1_fwd")
    *qkv, u, gates = _mix_in_fwd(x1, nm, w["w_in"], gate_bias)
    q, k, v = qkv[0:3], qkv[3:6], qkv[6:9]
    o_g, lse_g = [], []
    for grp in range(3):
        o, lse = _attn_fwd(q[grp], k[grp], v[grp], bias[grp], f"attn_fwd_{grp}")
        o_g.append(o)
        lse_g.append(lse)
    y, s = _ssm_fwd(u, bd, cdt, d_skip, pw)
    x2, o_attn, *lse_tot = _mix_out_fwd(x1, o_g, lse_g, y, gates, w["w_attn_branch"], w["ssm_w_glu"],
                                        w["w_ssm_branch"], w["w_out"])
    x3, a2, b2 = _ffn_fwd(x2, n2, w["ffn2_w_gate"], w["ffn2_w_up"], w["ffn2_w_down"], "ffn2_fwd")
    loss_blk, dx3, d_nf = _loss_fwd_bwd(x3, nf, target)

    gw, gs = {}, {}
    gs["final_norm"] = d_nf

    dx2, da, db, sact, h, d_out, gs["ffn2_norm"] = _ffn_bwd(dx3, x2, n2, a2, b2, w["ffn2_w_gate"], w["ffn2_w_up"],
                                                            w["ffn2_w_down"], "ffn2_bwd")
    gw["ffn2_w_gate"] = _matmul_tn(h[None], da, "ffn2_dw_gate")
    gw["ffn2_w_up"] = _matmul_tn(h[None], db, "ffn2_dw_up")
    gw["ffn2_w_down"] = _matmul_tn(sact, d_out[None], "ffn2_dw_down")

    head_sum = (jnp.arange(GROUP_WIDTH)[:, None] // HEAD_DIM == jnp.arange(GROUP_WIDTH)[None, :] // HEAD_DIM).astype(F32)
    (*d_o_delta, dy, dgp, mix, dya, dys, ys2, gel, dglu, gs["gate_bias"]) = _mix_out_bwd(
        dx2, o_attn, y, gates, w["w_attn_branch"], w["ssm_w_glu"], w["w_ssm_branch"], w["w_out"], head_sum)
    d_o, delta = d_o_delta[0:3], d_o_delta[3:6]
    gw["w_out"] = _matmul_tn(mix[None], dx2[None], "dw_out")[0]
    gw["w_attn_branch"] = _matmul_tn(o_attn[None], dya[None], "dw_attn_branch")[0]
    gw["w_ssm_branch"] = _matmul_tn(ys2[None], dys[None], "dw_ssm_branch")[0]
    gw["ssm_w_glu"] = _matmul_tn(gel[None], dglu[None], "dw_glu")[0]

    dqs, dks, dvs, dsums = [], [], [], []
    for grp in range(3):
        dq, dk, dv, dsum = _attn_bwd(q[grp], k[grp], v[grp], d_o[grp], lse_tot[grp], delta[grp], bias[grp],
                                     f"attn_bwd_{grp}")
        dqs.append(dq)
        dks.append(dk)
        dvs.append(dv)
        dsums.append(dsum.reshape(HEADS_PER_GROUP, -1))
    dsum_all = jnp.pad(jnp.stack(dsums), ((0, 0), (0, 8 - HEADS_PER_GROUP), (0, 0)))
    d_table = _bias_reduce(dsum_all, onehot)[:, :HEADS_PER_GROUP]
    gs["rel_bias_table"] = d_table.reshape(3 * HEADS_PER_GROUP, N_BUCKETS).T

    du, gs["ssm_d"], d_bd, d_cdt, d_ab = _ssm_bwd(dy, u, s, bd, cdt, d_skip, pwr)
    group_sum = (jnp.arange(NS)[:, None] // SSM_STATE == jnp.arange(128)[None, :]).astype(F32)
    d_are, d_aim, d_ldt, d_bre, d_bim, d_cre, d_cim = _disc_bwd(a_re, a_im, ldt, b_re, b_im, d_bd, d_cdt, d_ab, group_sum)
    gs["ssm_a_re"], gs["ssm_a_im"] = d_are, d_aim
    gs["ssm_log_dt"] = d_ldt[0, :SSM_GROUPS]
    from_cn = lambda t: t.reshape(SSM_GROUP, SSM_GROUPS, SSM_STATE).transpose(1, 2, 0)
    c_from_cn = lambda t: t.reshape(SSM_GROUP, SSM_GROUPS, SSM_STATE).transpose(1, 0, 2)
    gs["ssm_b_re"], gs["ssm_b_im"] = from_cn(d_bre), from_cn(d_bim)
    gs["ssm_c_re"], gs["ssm_c_im"] = c_from_cn(d_cre), c_from_cn(d_cim)

    dx1, hm, dz, gs["mix_norm"] = _mix_in_bwd(dx2, x1, nm, dqs + dks + dvs, du, dgp, w["w_in"])
    gw["w_in"] = _matmul_tn(hm[None], dz[None], "dw_in")[0]

    dx0, da, db, sact, h, d_out, gs["ffn1_norm"] = _ffn_bwd(dx1, x, n1, a1, b1, w["ffn1_w_gate"], w["ffn1_w_up"],
                                                            w["ffn1_w_down"], "ffn1_bwd")
    gw["ffn1_w_gate"] = _matmul_tn(h[None], da, "ffn1_dw_gate")
    gw["ffn1_w_up"] = _matmul_tn(h[None], db, "ffn1_dw_up")
    gw["ffn1_w_down"] = _matmul_tn(sact, d_out[None], "ffn1_dw_down")
    return loss_blk, dx0, gw, gs


def _split_cols(g):
    K, N = g.shape
    return g.reshape(K, N_SHARD, N // N_SHARD).transpose(1, 0, 2)


def _join_cols(w):
    S, K, n = w.shape
    return w.transpose(1, 0, 2).reshape(K, S * n)


COL_SHARDED = ("w_in", "ssm_w_glu", "w_attn_branch", "w_ssm_branch")


def kernel(x, ffn1_norm, ffn1_w_gate, ffn1_w_up, ffn1_w_down, mix_norm, w_in, gate_bias, rel_bias_table, ssm_a_re, ssm_a_im, ssm_log_dt, ssm_b_re, ssm_b_im, ssm_c_re, ssm_c_im, ssm_d, ssm_w_glu, w_attn_branch, w_ssm_branch, w_out, ffn2_norm, ffn2_w_gate, ffn2_w_up, ffn2_w_down, final_norm, loss_target, m_ffn1_norm, m_ffn1_w_gate, m_ffn1_w_up, m_ffn1_w_down, m_mix_norm, m_w_in, m_gate_bias, m_rel_bias_table, m_ssm_a_re, m_ssm_a_im, m_ssm_log_dt, m_ssm_b_re, m_ssm_b_im, m_ssm_c_re, m_ssm_c_im, m_ssm_d, m_ssm_w_glu, m_w_attn_branch, m_w_ssm_branch, m_w_out, m_ffn2_norm, m_ffn2_w_gate, m_ffn2_w_up, m_ffn2_w_down, m_final_norm, v_ffn1_norm, v_ffn1_w_gate, v_ffn1_w_up, v_ffn1_w_down, v_mix_norm, v_w_in, v_gate_bias, v_rel_bias_table, v_ssm_a_re, v_ssm_a_im, v_ssm_log_dt, v_ssm_b_re, v_ssm_b_im, v_ssm_c_re, v_ssm_c_im, v_ssm_d, v_ssm_w_glu, v_w_attn_branch, v_w_ssm_branch, v_w_out, v_ffn2_norm, v_ffn2_w_gate, v_ffn2_w_up, v_ffn2_w_down, v_final_norm):
    args = dict(locals())
    weights = {n: args[n] for n in ORDER}
    moms = {n: args["m_" + n] for n in ORDER}
    vels = {n: args["v_" + n] for n in ORDER}

    shard2d = {n: weights[n].reshape(weights[n].shape[-2:]) for n in BIG}
    first, rest = BIG[:3], BIG[3:]
    ffn1_full = _gather_weights([shard2d[n].astype(BF16) for n in first], "gather_ffn1_weights")
    ffn1_full, later = lax.optimization_barrier((ffn1_full, [shard2d[n].astype(BF16) for n in rest]))
    full = dict(zip(first, ffn1_full))
    full.update(zip(rest, _gather_weights_behind(later, "gather_later_weights", COLLECTIVE_IDS["gather"])))
    for n in COL_SHARDED:
        full[n] = _join_cols(full[n])
    full["w_out"] = full["w_out"].reshape(D_MODEL, D_MODEL)

    small = {n: weights[n] for n in SMALL}
    loss_blk, grad_x, gw, gs = _local_step(x[0], loss_target[0], full, small)

    for n in COL_SHARDED:
        gw[n] = _split_cols(gw[n])
    gw["w_out"] = gw["w_out"].reshape(N_SHARD, D_MODEL // N_SHARD, D_MODEL)
    grads = {}
    for tag, names in REDUCE_GROUPS.items():
        grads.update(zip(names, _reduce_scatter([gw[n] for n in names], names, tag,
                                                (COLLECTIVE_IDS["swap_" + tag], COLLECTIVE_IDS["exchange_" + tag]))))

    small_shapes = [weights[n].shape for n in SMALL]
    mine = _pack_small([gs[n] for n in SMALL] + [loss_blk[0:1, :]])
    total = _sum_slots(_gather_small(mine), "sum_small")
    small_grads = _unpack_small(total, small_shapes + [(128,)])
    loss = small_grads[-1][0]
    for n, g in zip(SMALL, small_grads[:-1]):
        grads[n] = g

    delta, new_m, new_v = {}, {}, {}
    for n in BIG:
        d, m, v = _adamw(shard2d[n], grads[n], moms[n].reshape(shard2d[n].shape), vels[n].reshape(shard2d[n].shape),
                         "adamw_" + n)
        shp = weights[n].shape
        delta[n], new_m[n], new_v[n] = d.reshape(shp), m.reshape(shp), v.reshape(shp)
        grads[n] = grads[n].reshape(shp)
    d, m, v = _adamw(_pack_small([weights[n] for n in SMALL]), _pack_small([grads[n] for n in SMALL]),
                     _pack_small([moms[n] for n in SMALL]), _pack_small([vels[n] for n in SMALL]), "adamw_small")
    for n, dd, mm, vv in zip(SMALL, _unpack_small(d, small_shapes), _unpack_small(m, small_shapes),
                             _unpack_small(v, small_shapes)):
        delta[n], new_m[n], new_v[n] = dd, mm, vv

    return (loss, grad_x[None], *[grads[n] for n in ORDER], *[delta[n] for n in ORDER],
            *[new_m[n] for n in ORDER], *[new_v[n] for n in ORDER])
```

```python
import functools
import math

import jax
import jax.numpy as jnp
from jax import lax
from jax.experimental import pallas as pl
from jax.experimental.pallas import tpu as pltpu
from jax.experimental.pallas import tpu_sc as plsc

F32 = jnp.float32
BF16 = jnp.bfloat16
MESH = pl.DeviceIdType.MESH

D_MODEL = 1024
D_FF = 2816
HEAD_DIM = 64
HEADS_PER_GROUP = 4
DILATIONS = (1, 4, 16)
WINDOW_STEPS = 128
ATTN_BLOCK = 128
ATTN_QB = 4
GROUP_WIDTH = HEADS_PER_GROUP * HEAD_DIM
ATTN_WIDTH = 3 * GROUP_WIDTH
N_BUCKETS = 32
MAX_DISTANCE = 2048
NEG_INF = -1e30
SSM_WIDTH = 512
SSM_GROUP = 16
SSM_GROUPS = 32
SSM_STATE = 64
NS = SSM_GROUPS * SSM_STATE
EPS = 1e-6
IN_WIDTH = 3 * ATTN_WIDTH + SSM_WIDTH + 2 * D_MODEL
Q_SCALE = HEAD_DIM ** -0.5
N_SHARD = 4
FF_SHARD = D_FF // N_SHARD
ADAM_LR, ADAM_B1, ADAM_B2, ADAM_EPS, ADAM_WD, ADAM_STEP = 0.001, 0.9, 0.999, 1e-08, 0.01, 10

LANES = 128
VMEM_LIMIT = 56 * 1024 * 1024
ROW_TILE = 512
FFN_BWD_TILE = 256
SSM_CHUNK = 256
SCAN_LANES = 512
ADAMW_BLOCK_BYTES = 1 << 20
TN_VMEM_BUDGET = 40 * 1024 * 1024
REDUCE_GROUPS = {
    "ffn2": ["ffn2_w_gate", "ffn2_w_up", "ffn2_w_down"],
    "mixer": ["w_out", "w_attn_branch", "w_ssm_branch", "ssm_w_glu"],
    "w_in": ["w_in"],
    "ffn1": ["ffn1_w_gate", "ffn1_w_up", "ffn1_w_down"],
}
COLLECTIVE_IDS = {name: i for i, name in enumerate(
    ["gather", "gather_small"] + [stage + "_" + tag for tag in REDUCE_GROUPS for stage in ("swap", "exchange")])}


def _params(**kw):
    return pltpu.CompilerParams(vmem_limit_bytes=VMEM_LIMIT, **kw)


def _dot(a, b):
    return jnp.dot(a, b, preferred_element_type=F32)


def _dot_nt(a, b):
    return lax.dot_general(a, b, (((1,), (1,)), ((), ())), preferred_element_type=F32)


def _dot_tn(a, b):
    return lax.dot_general(a, b, (((0,), (0,)), ((), ())), preferred_element_type=F32)


def _dot_exact(a, b):
    return jnp.dot(a, b, preferred_element_type=F32, precision=lax.Precision.HIGHEST)


def _dot_nt_exact(a, b):
    return lax.dot_general(a, b, (((1,), (1,)), ((), ())), preferred_element_type=F32,
                           precision=lax.Precision.HIGHEST)


def _rms(x):
    r = lax.rsqrt(jnp.mean(x * x, axis=-1, keepdims=True) + EPS)
    return r, x * r


def _rms_bwd(dh, g, r, xhat):
    dxh = dh * g
    return r * (dxh - xhat * jnp.mean(dxh * xhat, axis=-1, keepdims=True))


def _sigmoid(x):
    return 1.0 / (1.0 + jnp.exp(-x))


_GELU_C = math.sqrt(2.0 / math.pi)


def _gelu(x):
    return 0.5 * x * (1.0 + jnp.tanh(_GELU_C * (x + 0.044715 * x * x * x)))


def _gelu_grad(x):
    t = jnp.tanh(_GELU_C * (x + 0.044715 * x * x * x))
    return 0.5 * (1.0 + t) + 0.5 * x * (1.0 - t * t) * _GELU_C * (1.0 + 3 * 0.044715 * x * x)


def _whole():
    return pl.BlockSpec(memory_space=pltpu.VMEM)


def _row_tile(rows, cap):
    if rows <= cap:
        return rows
    return max(t for t in range(8, cap + 1, 8) if rows % t == 0)


def _rows(tm, w):
    return pl.BlockSpec((tm, w), lambda i: (i, 0))


def _acc_row(w):
    return pl.BlockSpec((1, w), lambda i: (0, 0))


def _ffn_fwd(x, g, wg, wu, wd, name):
    L = x.shape[0]
    tm = min(ROW_TILE, L)

    def body(x_ref, g_ref, wg_ref, wu_ref, wd_ref, xo_ref, a_ref, b_ref):
        xv = x_ref[...]
        r, xhat = _rms(xv)
        h = (xhat * g_ref[...]).astype(BF16)
        acc = jnp.zeros((tm, D_MODEL), F32)
        for j in range(N_SHARD):
            a = _dot(h, wg_ref[j])
            b = _dot(h, wu_ref[j])
            a_ref[j] = a.astype(BF16)
            b_ref[j] = b.astype(BF16)
            s = (a * _sigmoid(a) * b).astype(BF16)
            acc = acc + _dot(s, wd_ref[j])
        xo_ref[...] = xv + 0.5 * acc

    act = pl.BlockSpec((N_SHARD, tm, FF_SHARD), lambda i: (0, i, 0))
    return pl.pallas_call(
        body, name=name, grid=(L // tm,),
        in_specs=[_rows(tm, D_MODEL), _whole(), _whole(), _whole(), _whole()],
        out_specs=[_rows(tm, D_MODEL), act, act],
        out_shape=[jax.ShapeDtypeStruct((L, D_MODEL), F32),
                   jax.ShapeDtypeStruct((N_SHARD, L, FF_SHARD), BF16),
                   jax.ShapeDtypeStruct((N_SHARD, L, FF_SHARD), BF16)],
        compiler_params=_params(),
    )(x, g, wg, wu, wd)


def _ffn_bwd(dxo, x, g, a, b, wg, wu, wd, name):
    L = x.shape[0]
    tm = min(FFN_BWD_TILE, L)

    def body(dxo_ref, x_ref, g_ref, a_ref, b_ref, wg_ref, wu_ref, wd_ref,
             dxi_ref, da_ref, db_ref, s_ref, h_ref, do_ref, dg_ref):
        i = pl.program_id(0)
        xv = x_ref[...]
        gv = g_ref[...]
        r, xhat = _rms(xv)
        h_ref[...] = (xhat * gv).astype(BF16)
        dxo_v = dxo_ref[...]
        d_out = (0.5 * dxo_v).astype(BF16)
        do_ref[...] = d_out
        dh = jnp.zeros((tm, D_MODEL), F32)
        for j in range(N_SHARD):
            av = a_ref[j].astype(F32)
            bv = b_ref[j].astype(F32)
            sg = _sigmoid(av)
            sl = av * sg
            ds = _dot_nt(d_out, wd_ref[j])
            dbv = (ds * sl).astype(BF16)
            dav = (ds * bv * (sg * (1.0 + av * (1.0 - sg)))).astype(BF16)
            da_ref[j] = dav
            db_ref[j] = dbv
            s_ref[j] = (sl * bv).astype(BF16)
            dh = dh + _dot_nt(dav, wg_ref[j]) + _dot_nt(dbv, wu_ref[j])

        @pl.when(i == 0)
        def _():
            dg_ref[...] = jnp.zeros_like(dg_ref)

        dg_ref[...] += jnp.sum(dh * xhat, axis=0, keepdims=True)
        dxi_ref[...] = dxo_v + _rms_bwd(dh, gv, r, xhat)

    act = pl.BlockSpec((N_SHARD, tm, FF_SHARD), lambda i: (0, i, 0))
    act_shape = jax.ShapeDtypeStruct((N_SHARD, L, FF_SHARD), BF16)
    return pl.pallas_call(
        body, name=name, grid=(L // tm,),
        in_specs=[_rows(tm, D_MODEL), _rows(tm, D_MODEL), _whole(), act, act, _whole(), _whole(), _whole()],
        out_specs=[_rows(tm, D_MODEL), act, act, act, _rows(tm, D_MODEL), _rows(tm, D_MODEL), _acc_row(D_MODEL)],
        out_shape=[jax.ShapeDtypeStruct((L, D_MODEL), F32), act_shape, act_shape, act_shape,
                   jax.ShapeDtypeStruct((L, D_MODEL), BF16), jax.ShapeDtypeStruct((L, D_MODEL), BF16),
                   jax.ShapeDtypeStruct((1, D_MODEL), F32)],
        compiler_params=_params(),
    )(dxo, x, g, a, b, wg, wu, wd)


def _matmul_tn(a, b, name):
    ja, L, K = a.shape
    jb, _, N = b.shape
    J = max(ja, jb)
    splits = [s for s in (1, 2, 4, 8) if s == 1 or N % (s * LANES) == 0]
    nsplit = next((s for s in splits if 2 * K * (N // s) * 4 <= TN_VMEM_BUDGET // 2), splits[-1])
    nc = N // nsplit
    left = TN_VMEM_BUDGET - 2 * K * nc * 4
    row_bytes = 2 * (K * a.dtype.itemsize + nc * b.dtype.itemsize)
    tm = next((t for t in (2048, 1024, 512, 256) if L % t == 0 and t * row_bytes <= left), min(128, L))

    def body(a_ref, b_ref, o_ref):
        @pl.when(pl.program_id(2) == 0)
        def _():
            o_ref[...] = jnp.zeros_like(o_ref)

        o_ref[...] += _dot_tn(a_ref[...].astype(BF16), b_ref[...].astype(BF16))

    return pl.pallas_call(
        body, name=name, grid=(J, nsplit, L // tm),
        in_specs=[pl.BlockSpec((None, tm, K), (lambda j, s, i: (j, i, 0)) if ja > 1 else (lambda j, s, i: (0, i, 0))),
                  pl.BlockSpec((None, tm, nc), (lambda j, s, i: (j, i, s)) if jb > 1 else (lambda j, s, i: (0, i, s)))],
        out_specs=pl.BlockSpec((None, K, nc), lambda j, s, i: (j, 0, s)),
        out_shape=jax.ShapeDtypeStruct((J, K, N), F32),
        compiler_params=_params(),
    )(a, b)


def _loss_fwd_bwd(x, g, target):
    L = x.shape[0]
    tm = min(ROW_TILE, L)

    def body(x_ref, g_ref, t_ref, loss_ref, dx_ref, dg_ref):
        i = pl.program_id(0)
        xv = x_ref[...]
        gv = g_ref[...]
        r, xhat = _rms(xv)
        err = xhat * gv - t_ref[...]
        part = 0.5 * jnp.sum(jnp.sum(err * err, axis=1, keepdims=True) * (1.0 / D_MODEL), axis=0, keepdims=True)
        dy = err * (1.0 / D_MODEL)

        @pl.when(i == 0)
        def _():
            dg_ref[...] = jnp.zeros_like(dg_ref)
            loss_ref[...] = jnp.zeros_like(loss_ref)

        loss_ref[...] += jnp.broadcast_to(part, loss_ref.shape)
        dg_ref[...] += jnp.sum(dy * xhat, axis=0, keepdims=True)
        dx_ref[...] = _rms_bwd(dy, gv, r, xhat)

    return pl.pallas_call(
        body, name="loss_fwd_bwd", grid=(L // tm,),
        in_specs=[_rows(tm, D_MODEL), _whole(), _rows(tm, D_MODEL)],
        out_specs=[pl.BlockSpec((8, 128), lambda i: (0, 0)), _rows(tm, D_MODEL), _acc_row(D_MODEL)],
        out_shape=[jax.ShapeDtypeStruct((8, 128), F32), jax.ShapeDtypeStruct((L, D_MODEL), F32),
                   jax.ShapeDtypeStruct((1, D_MODEL), F32)],
        compiler_params=_params(),
    )(x, g, target)


_C_K = ATTN_WIDTH
_C_V = 2 * ATTN_WIDTH
_C_U = 3 * ATTN_WIDTH
_C_G = _C_U + SSM_WIDTH


def _residue_spec(d, tm):
    return pl.BlockSpec((d, tm // d, GROUP_WIDTH), lambda i: (0, i, 0))


def _residue_shape(d, L, dtype):
    return jax.ShapeDtypeStruct((d, L // d, GROUP_WIDTH), dtype)


def _residue_scratch(tm):
    return pltpu.VMEM((GROUP_WIDTH // LANES, tm, LANES), F32)


def _to_residues(val, out_ref, scr, d):
    if d == 1:
        out_ref[0] = val.astype(out_ref.dtype)
        return
    tm = val.shape[0]
    for half in range(GROUP_WIDTH // LANES):
        cols = slice(half * LANES, (half + 1) * LANES)
        scr[half] = val[:, cols]
        for r in range(d):
            out_ref[r, :, cols] = scr[half, pl.ds(r, tm // d, stride=d), :].astype(out_ref.dtype)


def _from_residues(ref, scr, d):
    if d == 1:
        return ref[0].astype(F32)
    rows = ref.shape[1]
    for half in range(GROUP_WIDTH // LANES):
        cols = slice(half * LANES, (half + 1) * LANES)
        for r in range(d):
            scr[half, pl.ds(r, rows, stride=d), :] = ref[r, :, cols].astype(F32)
    return jnp.concatenate([scr[half] for half in range(GROUP_WIDTH // LANES)], axis=1)


def _mix_in_fwd(x, g, w_in, gate_bias):
    L = x.shape[0]
    tm = min(ROW_TILE, L)

    def body(x_ref, g_ref, w_ref, gb_ref, *refs):
        qkv_refs, (u_ref, gate_ref, scr) = refs[:9], refs[9:]
        r, xhat = _rms(x_ref[...])
        h = (xhat * g_ref[...]).astype(BF16)
        for part, (c0, scale) in enumerate(((0, Q_SCALE), (_C_K, 1.0), (_C_V, 1.0))):
            z = _dot(h, w_ref[:, c0:c0 + ATTN_WIDTH]) * scale
            for grp, d in enumerate(DILATIONS):
                _to_residues(z[:, grp * GROUP_WIDTH:(grp + 1) * GROUP_WIDTH], qkv_refs[3 * part + grp], scr, d)
        u_ref[...] = _dot(h, w_ref[:, _C_U:_C_G])
        gate_ref[...] = _sigmoid(_dot(h, w_ref[:, _C_G:IN_WIDTH]) + gb_ref[...])

    return pl.pallas_call(
        body, name="mix_in_fwd", grid=(L // tm,),
        in_specs=[_rows(tm, D_MODEL), _whole(), _whole(), _whole()],
        out_specs=[_residue_spec(d, tm) for d in DILATIONS] * 3 + [_rows(tm, SSM_WIDTH), _rows(tm, 2 * D_MODEL)],
        out_shape=[_residue_shape(d, L, BF16) for d in DILATIONS] * 3
        + [jax.ShapeDtypeStruct((L, SSM_WIDTH), F32), jax.ShapeDtypeStruct((L, 2 * D_MODEL), F32)],
        scratch_shapes=[_residue_scratch(tm)],
        compiler_params=_params(),
    )(x, g, w_in, gate_bias)


def _mix_in_bwd(dx2, x, g, dqkv, du, dgp, w_in):
    L = x.shape[0]
    tm = min(ROW_TILE, L)

    def body(dx2_ref, x_ref, g_ref, *refs):
        piece_refs = refs[:9]
        du_ref, dgp_ref, w_ref, dx1_ref, h_ref, dz_ref, dg_ref, scr = refs[9:]
        i = pl.program_id(0)
        gv = g_ref[...]
        r, xhat = _rms(x_ref[...])
        h_ref[...] = (xhat * gv).astype(BF16)
        for part in range(3):
            for grp, d in enumerate(DILATIONS):
                c0 = part * ATTN_WIDTH + grp * GROUP_WIDTH
                dz_ref[:, c0:c0 + GROUP_WIDTH] = _from_residues(piece_refs[3 * part + grp], scr, d).astype(BF16)
        dz_ref[:, _C_U:_C_G] = du_ref[...].astype(BF16)
        dz_ref[:, _C_G:IN_WIDTH] = dgp_ref[...]
        dh = _dot_nt(dz_ref[...], w_ref[...])

        @pl.when(i == 0)
        def _():
            dg_ref[...] = jnp.zeros_like(dg_ref)

        dg_ref[...] += jnp.sum(dh * xhat, axis=0, keepdims=True)
        dx1_ref[...] = dx2_ref[...] + _rms_bwd(dh, gv, r, xhat)

    return pl.pallas_call(
        body, name="mix_in_bwd", grid=(L // tm,),
        in_specs=[_rows(tm, D_MODEL), _rows(tm, D_MODEL), _whole()] + [_residue_spec(d, tm) for d in DILATIONS] * 3
        + [_rows(tm, SSM_WIDTH), _rows(tm, 2 * D_MODEL), _whole()],
        out_specs=[_rows(tm, D_MODEL), _rows(tm, D_MODEL), _rows(tm, IN_WIDTH), _acc_row(D_MODEL)],
        out_shape=[jax.ShapeDtypeStruct((L, D_MODEL), F32), jax.ShapeDtypeStruct((L, D_MODEL), BF16),
                   jax.ShapeDtypeStruct((L, IN_WIDTH), BF16), jax.ShapeDtypeStruct((1, D_MODEL), F32)],
        scratch_shapes=[_residue_scratch(tm)],
        compiler_params=_params(),
    )(dx2, x, g, *dqkv, du, dgp, w_in)


def _bucket_onehot():
    qi = jnp.arange(ATTN_BLOCK)[:, None]
    kj = jnp.arange(2 * ATTN_BLOCK)[None, :]
    steps = jnp.maximum(qi + ATTN_BLOCK - kj, 0)
    max_exact = N_BUCKETS // 2
    out = []
    for d in DILATIONS:
        dist = steps * d
        df = jnp.maximum(dist, 1).astype(F32)
        large = max_exact + (jnp.log(df / max_exact) / math.log(MAX_DISTANCE / max_exact)
                             * (N_BUCKETS - max_exact)).astype(jnp.int32)
        large = jnp.minimum(large, N_BUCKETS - 1)
        bucket = jnp.where(dist < max_exact, dist, large).reshape(-1)
        out.append((bucket[None, :] == jnp.arange(N_BUCKETS)[:, None]).astype(F32))
    return jnp.stack(out)


def _bias_expand(table_t, onehot):
    n = onehot.shape[-1]

    def body(t_ref, oh_ref, o_ref):
        bias = _dot_exact(t_ref[...], oh_ref[...])
        col = lax.broadcasted_iota(jnp.int32, (8, n), 1)
        qi = col // (2 * ATTN_BLOCK)
        kj = col - qi * (2 * ATTN_BLOCK)
        steps = qi + ATTN_BLOCK - kj
        band = (steps >= 0) & (steps <= WINDOW_STEPS)
        o_ref[0] = jnp.where(band & (kj >= ATTN_BLOCK), bias, NEG_INF)
        o_ref[1] = jnp.where(band, bias, NEG_INF)

    return pl.pallas_call(
        body, name="bias_expand", grid=(3,),
        in_specs=[pl.BlockSpec((None, 8, N_BUCKETS), lambda g: (g, 0, 0)),
                  pl.BlockSpec((None, N_BUCKETS, n), lambda g: (g, 0, 0))],
        out_specs=pl.BlockSpec((None, 2, 8, n), lambda g: (g, 0, 0, 0)),
        out_shape=jax.ShapeDtypeStruct((3, 2, 8, n), F32),
        compiler_params=_params(),
    )(table_t, onehot)


def _bias_reduce(dsum, onehot):
    n = onehot.shape[-1]

    def body(d_ref, oh_ref, o_ref):
        o_ref[...] = _dot_nt_exact(d_ref[...], oh_ref[...])

    return pl.pallas_call(
        body, name="bias_reduce", grid=(3,),
        in_specs=[pl.BlockSpec((None, 8, n), lambda g: (g, 0, 0)),
                  pl.BlockSpec((None, N_BUCKETS, n), lambda g: (g, 0, 0))],
        out_specs=pl.BlockSpec((None, 8, N_BUCKETS), lambda g: (g, 0, 0)),
        out_shape=jax.ShapeDtypeStruct((3, 8, N_BUCKETS), F32),
        compiler_params=_params(),
    )(dsum, onehot)


def _head_of_col(rows):
    return lax.broadcasted_iota(jnp.int32, (rows, GROUP_WIDTH), 1) // HEAD_DIM


def _attn_specs(qb):
    rows = qb * ATTN_BLOCK
    cur = pl.BlockSpec((None, rows, GROUP_WIDTH), lambda r, n: (r, n, 0))
    prev = pl.BlockSpec((None, ATTN_BLOCK, GROUP_WIDTH), lambda r, n: (r, jnp.maximum(n * qb - 1, 0), 0))
    bias = pl.BlockSpec((2, HEADS_PER_GROUP, ATTN_BLOCK, 2 * ATTN_BLOCK), lambda r, n: (0, 0, 0, 0))
    return cur, prev, bias


def _attn_fwd(q, k, v, bias, name):
    d, M, _ = q.shape
    nb = M // ATTN_BLOCK
    qb = min(ATTN_QB, nb)

    def body(q_ref, kp_ref, kc_ref, vp_ref, vc_ref, bias_ref, o_ref, lse_ref):
        n = pl.program_id(1)
        q_head = _head_of_col(ATTN_BLOCK)
        kv_head = _head_of_col(2 * ATTN_BLOCK)
        kwin = jnp.concatenate([kp_ref[...], kc_ref[...]], axis=0)
        vwin = jnp.concatenate([vp_ref[...], vc_ref[...]], axis=0)
        for b in range(qb):
            rows = slice(b * ATTN_BLOCK, (b + 1) * ATTN_BLOCK)
            window = slice(b * ATTN_BLOCK, (b + 2) * ATTN_BLOCK)
            variant = jnp.minimum(n, 1) if b == 0 else 1
            qv = q_ref[rows, :]
            kk = kwin[window]
            vv = vwin[window]
            o_acc = jnp.zeros((ATTN_BLOCK, GROUP_WIDTH), F32)
            lse_acc = jnp.zeros((ATTN_BLOCK, GROUP_WIDTH), F32)
            for hh in range(HEADS_PER_GROUP):
                hm = q_head == hh
                qh = jnp.where(hm, qv, jnp.zeros_like(qv))
                logits = _dot_nt(qh, kk) + bias_ref[variant, hh]
                m = jnp.max(logits, axis=1, keepdims=True)
                p = jnp.exp(logits - m)
                vh = jnp.where(kv_head == hh, vv, jnp.ones_like(vv))
                pv = _dot(p.astype(BF16), vh)
                c_sum = ((hh + 1) % HEADS_PER_GROUP) * HEAD_DIM
                den = pv[:, c_sum:c_sum + 1]
                o_acc = jnp.where(hm, pv * (1.0 / den), o_acc)
                lse_acc = jnp.where(hm, m + jnp.log(den), lse_acc)
            o_ref[rows, :] = o_acc
            lse_ref[rows, :] = lse_acc

    cur, prev, full = _attn_specs(qb)
    return pl.pallas_call(
        body, name=name, grid=(d, nb // qb),
        in_specs=[cur, prev, cur, prev, cur, full],
        out_specs=[cur, cur],
        out_shape=[jax.ShapeDtypeStruct((d, M, GROUP_WIDTH), F32)] * 2,
        compiler_params=_params(),
    )(q, k, k, v, v, bias)


def _attn_bwd(q, k, v, do, lse, delta, bias, name):
    d, M, _ = q.shape
    nb = M // ATTN_BLOCK
    qb = min(ATTN_QB, nb)
    ns = nb // qb
    rows_q = qb * ATTN_BLOCK
    last = slice(rows_q - ATTN_BLOCK, rows_q)

    def body(q_ref, kp_ref, kc_ref, vp_ref, vc_ref, do_ref, lse_ref, dl_ref, bias_ref,
             dq_ref, dk_ref, dv_ref, dsum_ref, pk_ref, pv_ref, wk_ref, wv_ref):
        r = pl.program_id(0)
        n = pl.program_id(1)

        @pl.when((r == 0) & (n == 0))
        def _():
            dsum_ref[...] = jnp.zeros_like(dsum_ref)

        @pl.when(n == 0)
        def _():
            pk_ref[...] = jnp.zeros_like(pk_ref)
            pv_ref[...] = jnp.zeros_like(pv_ref)

        @pl.when(n < ns)
        def _():
            q_head = _head_of_col(ATTN_BLOCK)
            kwin = jnp.concatenate([kp_ref[...], kc_ref[...]], axis=0)
            vwin = jnp.concatenate([vp_ref[...], vc_ref[...]], axis=0)
            wk_ref[...] = jnp.zeros_like(wk_ref)
            wv_ref[...] = jnp.zeros_like(wv_ref)
            for b in range(qb):
                rows = slice(b * ATTN_BLOCK, (b + 1) * ATTN_BLOCK)
                window = slice(b * ATTN_BLOCK, (b + 2) * ATTN_BLOCK)
                variant = jnp.minimum(n, 1) if b == 0 else 1
                qv = q_ref[rows, :]
                dov = do_ref[rows, :]
                kk = kwin[window]
                vv = vwin[window]
                dq_acc = jnp.zeros((ATTN_BLOCK, GROUP_WIDTH), F32)
                dkk = jnp.zeros((2 * ATTN_BLOCK, GROUP_WIDTH), F32)
                dvv = jnp.zeros((2 * ATTN_BLOCK, GROUP_WIDTH), F32)
                for hh in range(HEADS_PER_GROUP):
                    hm = q_head == hh
                    c0 = hh * HEAD_DIM
                    qh = jnp.where(hm, qv, jnp.zeros_like(qv))
                    doh = jnp.where(hm, dov, jnp.zeros_like(dov))
                    logits = _dot_nt(qh, kk) + bias_ref[variant, hh]
                    p = jnp.exp(logits - lse_ref[rows, c0:c0 + 1])
                    dp = _dot_nt(doh, vv)
                    ds = p * (dp - dl_ref[rows, c0:c0 + 1])
                    dsum_ref[hh] += ds
                    ds16 = ds.astype(BF16)
                    dq_acc = jnp.where(hm, _dot(ds16, kk), dq_acc)
                    dkk = dkk + _dot_tn(ds16, qh)
                    dvv = dvv + _dot_tn(p.astype(BF16), doh)
                dq_ref[rows, :] = (dq_acc * Q_SCALE).astype(BF16)
                wk_ref[window, :] += dkk
                wv_ref[window, :] += dvv
            for out_ref, part_ref, win_ref in ((dk_ref, pk_ref, wk_ref), (dv_ref, pv_ref, wv_ref)):
                if qb > 1:
                    out_ref[0:rows_q - ATTN_BLOCK, :] = part_ref[0:rows_q - ATTN_BLOCK, :].astype(BF16)
                out_ref[last, :] = (part_ref[last, :] + win_ref[0:ATTN_BLOCK, :]).astype(BF16)
                part_ref[...] = win_ref[ATTN_BLOCK:, :]

        @pl.when(n == ns)
        def _():
            dk_ref[...] = pk_ref[...].astype(BF16)
            dv_ref[...] = pv_ref[...].astype(BF16)

    def clamp(n):
        return jnp.minimum(n, ns - 1)

    cur = pl.BlockSpec((None, rows_q, GROUP_WIDTH), lambda r, n: (r, clamp(n), 0))
    prev = pl.BlockSpec((None, ATTN_BLOCK, GROUP_WIDTH), lambda r, n: (r, jnp.maximum(clamp(n) * qb - 1, 0), 0))
    lag = pl.BlockSpec((None, rows_q, GROUP_WIDTH), lambda r, n: (r, jnp.maximum(n - 1, 0), 0))
    full = pl.BlockSpec((2, HEADS_PER_GROUP, ATTN_BLOCK, 2 * ATTN_BLOCK), lambda r, n: (0, 0, 0, 0))
    acc = pl.BlockSpec((HEADS_PER_GROUP, ATTN_BLOCK, 2 * ATTN_BLOCK), lambda r, n: (0, 0, 0))
    return pl.pallas_call(
        body, name=name, grid=(d, ns + 1),
        in_specs=[cur, prev, cur, prev, cur, cur, cur, cur, full],
        out_specs=[cur, lag, lag, acc],
        out_shape=[jax.ShapeDtypeStruct((d, M, GROUP_WIDTH), BF16)] * 3
        + [jax.ShapeDtypeStruct((HEADS_PER_GROUP, ATTN_BLOCK, 2 * ATTN_BLOCK), F32)],
        scratch_shapes=[pltpu.VMEM((rows_q, GROUP_WIDTH), F32), pltpu.VMEM((rows_q, GROUP_WIDTH), F32),
                        pltpu.VMEM((rows_q + ATTN_BLOCK, GROUP_WIDTH), F32),
                        pltpu.VMEM((rows_q + ATTN_BLOCK, GROUP_WIDTH), F32)],
        compiler_params=_params(),
    )(q, k, k, v, v, do, lse, delta, bias)


def _disc_math(a_re, a_im, ldt, b_re, b_im):
    dt = jnp.exp(ldt)
    mag = jnp.exp(a_re * dt)
    ab_re = mag * jnp.cos(a_im * dt)
    ab_im = mag * jnp.sin(a_im * dt)
    den = a_re * a_re + a_im * a_im
    xr = ab_re - 1.0
    coef_re = (xr * a_re + ab_im * a_im) / den
    coef_im = (ab_im * a_re - xr * a_im) / den
    return ab_re, ab_im, coef_re * b_re - coef_im * b_im, coef_re * b_im + coef_im * b_re


def _block_diag_mask():
    row_g = lax.broadcasted_iota(jnp.int32, (SSM_WIDTH, 2 * NS), 0) // SSM_GROUP
    col = lax.broadcasted_iota(jnp.int32, (SSM_WIDTH, 2 * NS), 1)
    col_g = jnp.where(col >= NS, col - NS, col) // SSM_STATE
    return row_g == col_g


def _disc_fwd(a_re, a_im, ldt, b_re, b_im, c_re, c_im):
    def body(are_ref, aim_ref, ldt_ref, bre_ref, bim_ref, cre_ref, cim_ref, pw_ref, pwr_ref, bd_ref, cdt_ref):
        ab_re, ab_im, bb_re, bb_im = _disc_math(are_ref[...], aim_ref[...], ldt_ref[...], bre_ref[...], bim_ref[...])
        row = lax.broadcasted_iota(jnp.int32, (8, NS), 0)
        pr, pi = ab_re, ab_im
        t_re = jnp.zeros((8, NS), F32)
        t_im = jnp.zeros((8, NS), F32)
        u_re = jnp.zeros((8, NS), F32)
        u_im = jnp.zeros((8, NS), F32)
        for j in range(8):
            t_re = jnp.where(row == j, pr, t_re)
            t_im = jnp.where(row == j, pi, t_im)
            u_re = jnp.where(row == 7 - j, pr, u_re)
            u_im = jnp.where(row == 7 - j, pi, u_im)
            pr, pi = pr * ab_re - pi * ab_im, pr * ab_im + pi * ab_re
        pw_ref[0] = t_re
        pw_ref[1] = t_im
        pwr_ref[0] = u_re
        pwr_ref[1] = u_im
        mask = _block_diag_mask()
        zero = jnp.zeros((SSM_WIDTH, 2 * NS), F32)
        bfull = jnp.concatenate([jnp.concatenate([bb_re] * SSM_GROUPS, axis=0),
                                 jnp.concatenate([bb_im] * SSM_GROUPS, axis=0)], axis=1)
        bd_ref[...] = jnp.where(mask, bfull, zero).astype(BF16)
        cfull = jnp.concatenate([jnp.concatenate([cre_ref[...]] * SSM_GROUPS, axis=0),
                                 jnp.concatenate([-cim_ref[...]] * SSM_GROUPS, axis=0)], axis=1)
        cdt_ref[...] = jnp.where(mask, cfull, zero).astype(BF16)

    return pl.pallas_call(
        body, name="s5_disc_fwd",
        in_specs=[_whole()] * 7, out_specs=[_whole()] * 4,
        out_shape=[jax.ShapeDtypeStruct((2, 8, NS), F32), jax.ShapeDtypeStruct((2, 8, NS), F32),
                   jax.ShapeDtypeStruct((SSM_WIDTH, 2 * NS), BF16), jax.ShapeDtypeStruct((SSM_WIDTH, 2 * NS), BF16)],
        compiler_params=_params(),
    )(a_re, a_im, ldt, b_re, b_im, c_re, c_im)


def _disc_bwd(a_re, a_im, ldt, b_re, b_im, d_bd, d_cdt, d_ab, group_sum):
    def body(are_ref, aim_ref, ldt_ref, bre_ref, bim_ref, dbd_ref, dcdt_ref, dab_ref, gs_ref,
             dare_ref, daim_ref, dldt_ref, dbre_ref, dbim_ref, dcre_ref, dcim_ref):
        col = lax.broadcasted_iota(jnp.int32, (SSM_GROUP, 2 * NS), 1)
        col_g = jnp.where(col >= NS, col - NS, col) // SSM_STATE
        acc_b = jnp.zeros((SSM_GROUP, 2 * NS), F32)
        acc_c = jnp.zeros((SSM_GROUP, 2 * NS), F32)
        for g in range(SSM_GROUPS):
            rows = slice(g * SSM_GROUP, (g + 1) * SSM_GROUP)
            acc_b = acc_b + jnp.where(col_g == g, dbd_ref[rows, :], 0.0)
            acc_c = acc_c + jnp.where(col_g == g, dcdt_ref[rows, :], 0.0)
        dcre_ref[...] = acc_c[:, :NS]
        dcim_ref[...] = -acc_c[:, NS:]
        dab_re = jnp.sum(dab_ref[0], axis=0, keepdims=True)
        dab_im = jnp.sum(dab_ref[1], axis=0, keepdims=True)
        _, vjp = jax.vjp(_disc_math, are_ref[...], aim_ref[...], ldt_ref[...], bre_ref[...], bim_ref[...])
        d_are, d_aim, d_ldt, d_bre, d_bim = vjp((dab_re, dab_im, acc_b[:, :NS], acc_b[:, NS:]))
        dare_ref[...] = d_are
        daim_ref[...] = d_aim
        dbre_ref[...] = d_bre
        dbim_ref[...] = d_bim
        dldt_ref[...] = _dot_exact(jnp.broadcast_to(d_ldt, (8, NS)), gs_ref[...])

    vec = jax.ShapeDtypeStruct((1, NS), F32)
    mat = jax.ShapeDtypeStruct((SSM_GROUP, NS), F32)
    return pl.pallas_call(
        body, name="s5_disc_bwd",
        in_specs=[_whole()] * 9, out_specs=[_whole()] * 7,
        out_shape=[vec, vec, jax.ShapeDtypeStruct((8, 128), F32), mat, mat, mat, mat],
        compiler_params=_params(),
    )(a_re, a_im, ldt, b_re, b_im, d_bd, d_cdt, d_ab, group_sum)


def _scan_blocks(buf, pw_ref, carry_ref, n_blocks, reverse):
    row = lax.broadcasted_iota(jnp.int32, (8, SCAN_LANES), 0)
    for lc in range(NS // SCAN_LANES):
        re_cols = pl.ds(lc * SCAN_LANES, SCAN_LANES)
        im_cols = pl.ds(NS + lc * SCAN_LANES, SCAN_LANES)
        pr = pw_ref[0, :, re_cols]
        pi = pw_ref[1, :, re_cols]
        if reverse:
            pi = -pi
            base = [(7, 1), (6, 2), (4, 4)]
            coef = [(jnp.where(row < 8 - k, pr[j:j + 1], 0.0), jnp.where(row < 8 - k, pi[j:j + 1], 0.0), 8 - k)
                    for j, k in base]
        else:
            base = [(0, 1), (1, 2), (3, 4)]
            coef = [(jnp.where(row >= k, pr[j:j + 1], 0.0), jnp.where(row >= k, pi[j:j + 1], 0.0), k)
                    for j, k in base]

        def step(i, carry, pr=pr, pi=pi, coef=coef, re_cols=re_cols, im_cols=im_cols):
            cr, ci = carry
            blk = (n_blocks - 1 - i) if reverse else i
            rows = pl.ds(pl.multiple_of(blk * 8, 8), 8)
            xr = buf[rows, re_cols]
            xi = buf[rows, im_cols]
            for kr, ki, shift in coef:
                sr = pltpu.roll(xr, shift, 0)
                si = pltpu.roll(xi, shift, 0)
                xr, xi = xr + kr * sr - ki * si, xi + kr * si + ki * sr
            xr, xi = xr + pr * cr - pi * ci, xi + pr * ci + pi * cr
            buf[rows, re_cols] = xr
            buf[rows, im_cols] = xi
            edge = slice(0, 1) if reverse else slice(7, 8)
            return xr[edge], xi[edge]

        cr, ci = lax.fori_loop(0, n_blocks, step, (carry_ref[0:1, re_cols], carry_ref[0:1, im_cols]))
        carry_ref[0:1, re_cols] = cr
        carry_ref[0:1, im_cols] = ci


_SUPER_GROUPS = 16
_SUPER_BLOCKS = [
    (slice(k * _SUPER_GROUPS * SSM_GROUP, (k + 1) * _SUPER_GROUPS * SSM_GROUP),
     [slice(half + k * _SUPER_GROUPS * SSM_STATE, half + (k + 1) * _SUPER_GROUPS * SSM_STATE) for half in (0, NS)])
    for k in range(SSM_GROUPS // _SUPER_GROUPS)]


def _ssm_fwd(u, bd, cdt, d_skip, pw):
    L = u.shape[0]
    tc = min(SSM_CHUNK, L)

    def body(u_ref, bd_ref, cdt_ref, dsk_ref, pw_ref, y_ref, s_ref, carry_ref):
        @pl.when(pl.program_id(0) == 0)
        def _():
            carry_ref[...] = jnp.zeros_like(carry_ref)

        uv = u_ref[...]
        u16 = uv.astype(BF16)
        for ch, states in _SUPER_BLOCKS:
            for st in states:
                s_ref[:, st] = _dot(u16[:, ch], bd_ref[ch, st])
        _scan_blocks(s_ref, pw_ref, carry_ref, tc // 8, reverse=False)
        for ch, states in _SUPER_BLOCKS:
            y_ref[:, ch] = (sum(_dot_nt(s_ref[:, st].astype(BF16), cdt_ref[ch, st]) for st in states)
                            + dsk_ref[:, ch] * uv[:, ch])

    return pl.pallas_call(
        body, name="s5_fwd", grid=(L // tc,),
        in_specs=[_rows(tc, SSM_WIDTH), _whole(), _whole(), _whole(), _whole()],
        out_specs=[_rows(tc, SSM_WIDTH), _rows(tc, 2 * NS)],
        out_shape=[jax.ShapeDtypeStruct((L, SSM_WIDTH), F32), jax.ShapeDtypeStruct((L, 2 * NS), F32)],
        scratch_shapes=[pltpu.VMEM((8, 2 * NS), F32)],
        compiler_params=_params(),
    )(u, bd, cdt, d_skip, pw)


def _ssm_bwd(dy, u, s, bd, cdt, d_skip, pwr):
    L = u.shape[0]
    tc = min(SSM_CHUNK, L)
    nc = L // tc
    blocks = tc // 8

    def body(dy_ref, u_ref, s_ref, sprev_ref, bd_ref, cdt_ref, dsk_ref, pwr_ref,
             du_ref, ddsk_ref, dbd_ref, dcdt_ref, dab_ref, g_ref, sx_ref, carry_ref):
        i = pl.program_id(0)

        @pl.when(i == 0)
        def _():
            carry_ref[...] = jnp.zeros_like(carry_ref)
            ddsk_ref[...] = jnp.zeros_like(ddsk_ref)
            dbd_ref[...] = jnp.zeros_like(dbd_ref)
            dcdt_ref[...] = jnp.zeros_like(dcdt_ref)
            dab_ref[...] = jnp.zeros_like(dab_ref)

        dyv = dy_ref[...]
        uv = u_ref[...]
        dy16 = dyv.astype(BF16)
        u16 = uv.astype(BF16)
        for ch, states in _SUPER_BLOCKS:
            for st in states:
                g_ref[:, st] = _dot(dy16[:, ch], cdt_ref[ch, st])
        _scan_blocks(g_ref, pwr_ref, carry_ref, blocks, reverse=True)
        ddsk_ref[...] += jnp.sum(dyv * uv, axis=0, keepdims=True)
        for ch, states in _SUPER_BLOCKS:
            du = dsk_ref[:, ch] * dyv[:, ch]
            for st in states:
                g16 = g_ref[:, st].astype(BF16)
                du = du + _dot_nt(g16, bd_ref[ch, st])
                dbd_ref[ch, st] += _dot_tn(u16[:, ch], g16)
                dcdt_ref[ch, st] += _dot_tn(dy16[:, ch], s_ref[:, st].astype(BF16))
            du_ref[:, ch] = du

        sx_ref[pl.ds(8, tc), :] = s_ref[...]
        sx_ref[pl.ds(0, 8), :] = jnp.where(i == nc - 1, 0.0, sprev_ref[...])
        row = lax.broadcasted_iota(jnp.int32, (8, SCAN_LANES), 0)
        for lc in range(NS // SCAN_LANES):
            re_cols = pl.ds(lc * SCAN_LANES, SCAN_LANES)
            im_cols = pl.ds(NS + lc * SCAN_LANES, SCAN_LANES)

            def step(b, acc, re_cols=re_cols, im_cols=im_cols):
                ar, ai = acc
                off = pl.multiple_of(b * 8, 8)
                gr = g_ref[pl.ds(off, 8), re_cols]
                gi = g_ref[pl.ds(off, 8), im_cols]
                before = pl.ds(off, 8)
                here = pl.ds(off + 8, 8)
                sr = jnp.where(row == 0, sx_ref[before, re_cols][7:8], pltpu.roll(sx_ref[here, re_cols], 1, 0))
                si = jnp.where(row == 0, sx_ref[before, im_cols][7:8], pltpu.roll(sx_ref[here, im_cols], 1, 0))
                return ar + gr * sr + gi * si, ai + gi * sr - gr * si

            zero = jnp.zeros((8, SCAN_LANES), F32)
            ar, ai = lax.fori_loop(0, blocks, step, (zero, zero))
            dab_ref[0, :, re_cols] += ar
            dab_ref[1, :, re_cols] += ai

    rev = lambda i: (nc - 1 - i, 0)
    sprev = pl.BlockSpec((8, 2 * NS), lambda i: (jnp.maximum((nc - 1 - i) * blocks - 1, 0), 0))
    return pl.pallas_call(
        body, name="s5_bwd", grid=(nc,),
        in_specs=[pl.BlockSpec((tc, SSM_WIDTH), rev), pl.BlockSpec((tc, SSM_WIDTH), rev),
                  pl.BlockSpec((tc, 2 * NS), rev), sprev, _whole(), _whole(), _whole(), _whole()],
        out_specs=[pl.BlockSpec((tc, SSM_WIDTH), rev), _whole(), _whole(), _whole(), _whole()],
        out_shape=[jax.ShapeDtypeStruct((L, SSM_WIDTH), F32), jax.ShapeDtypeStruct((1, SSM_WIDTH), F32),
                   jax.ShapeDtypeStruct((SSM_WIDTH, 2 * NS), F32), jax.ShapeDtypeStruct((SSM_WIDTH, 2 * NS), F32),
                   jax.ShapeDtypeStruct((2, 8, NS), F32)],
        scratch_shapes=[pltpu.VMEM((tc, 2 * NS), F32), pltpu.VMEM((tc + 8, 2 * NS), F32), pltpu.VMEM((8, 2 * NS), F32)],
        compiler_params=_params(),
    )(dy, u, s, s, bd, cdt, d_skip, pwr)


def _branches(o_attn, y, gates, w_ab, w_glu, w_sb):
    ya = _dot(o_attn.astype(BF16), w_ab[...])
    gel = _gelu(y)
    glu = _dot(gel.astype(BF16), w_glu[...])
    p = glu[:, :SSM_WIDTH]
    sg = _sigmoid(glu[:, SSM_WIDTH:])
    ys2 = p * sg
    ysb = _dot(ys2.astype(BF16), w_sb[...])
    ga = gates[:, :D_MODEL]
    gs = gates[:, D_MODEL:]
    return ya, gel, p, sg, ys2, ysb, ga, gs


def _mix_out_fwd(x1, o_g, lse_g, y, gates, w_ab, w_glu, w_sb, w_out):
    L = x1.shape[0]
    tm = min(ROW_TILE, L)

    def body(x_ref, o0, o1, o2, l0, l1, l2, y_ref, gate_ref, wab_ref, wglu_ref, wsb_ref, wout_ref,
             x2_ref, oat_ref, lse0, lse1, lse2, scr):
        la, lb, lc = (_from_residues(ref, scr, d) for ref, d in zip((l0, l1, l2), DILATIONS))
        m = jnp.maximum(jnp.maximum(la, lb), lc)
        ea, eb, ec = jnp.exp(la - m), jnp.exp(lb - m), jnp.exp(lc - m)
        tot = ea + eb + ec
        oa, ob, oc = (_from_residues(ref, scr, d) for ref, d in zip((o0, o1, o2), DILATIONS))
        o_attn = (ea * oa + eb * ob + ec * oc) / tot
        oat_ref[...] = o_attn
        lse = m + jnp.log(tot)
        for ref, d in zip((lse0, lse1, lse2), DILATIONS):
            _to_residues(lse, ref, scr, d)
        ya, _, _, _, _, ysb, ga, gs = _branches(o_attn, y_ref[...], gate_ref[...], wab_ref, wglu_ref, wsb_ref)
        mix = ga * ya + gs * ysb
        x2_ref[...] = x_ref[...] + _dot(mix.astype(BF16), wout_ref[...])

    res = [_residue_spec(d, tm) for d in DILATIONS]
    return pl.pallas_call(
        body, name="mix_out_fwd", grid=(L // tm,),
        in_specs=[_rows(tm, D_MODEL)] + res * 2 + [_rows(tm, SSM_WIDTH), _rows(tm, 2 * D_MODEL)] + [_whole()] * 4,
        out_specs=[_rows(tm, D_MODEL), _rows(tm, GROUP_WIDTH)] + res,
        out_shape=[jax.ShapeDtypeStruct((L, D_MODEL), F32), jax.ShapeDtypeStruct((L, GROUP_WIDTH), F32)]
        + [_residue_shape(d, L, F32) for d in DILATIONS],
        scratch_shapes=[_residue_scratch(tm)],
        compiler_params=_params(),
    )(x1, *o_g, *lse_g, y, gates, w_ab, w_glu, w_sb, w_out)


def _mix_out_bwd(dx2, o_attn, y, gates, w_ab, w_glu, w_sb, w_out, head_sum):
    L = dx2.shape[0]
    tm = min(ROW_TILE, L)

    def body(dx_ref, oat_ref, y_ref, gate_ref, wab_ref, wglu_ref, wsb_ref, wout_ref, hs_ref,
             do0, do1, do2, dl0, dl1, dl2, dy_ref, dgp_ref, mix_ref, dya_ref, dys_ref, ys2_ref, gel_ref, dglu_ref,
             dgb_ref, scr):
        i = pl.program_id(0)
        o_attn = oat_ref[...]
        yv = y_ref[...]
        ya, gel, p, sg, ys2, ysb, ga, gs = _branches(o_attn, yv, gate_ref[...], wab_ref, wglu_ref, wsb_ref)
        mix_ref[...] = (ga * ya + gs * ysb).astype(BF16)
        ys2_ref[...] = ys2.astype(BF16)
        gel_ref[...] = gel.astype(BF16)
        dmix = _dot_nt(dx_ref[...].astype(BF16), wout_ref[...])
        dgp = jnp.concatenate([dmix * ya * ga * (1.0 - ga), dmix * ysb * gs * (1.0 - gs)], axis=1)
        dgp_ref[...] = dgp.astype(BF16)

        @pl.when(i == 0)
        def _():
            dgb_ref[...] = jnp.zeros_like(dgb_ref)

        dgb_ref[...] += jnp.sum(dgp, axis=0, keepdims=True)
        dya = (dmix * ga).astype(BF16)
        dys = (dmix * gs).astype(BF16)
        dya_ref[...] = dya
        dys_ref[...] = dys
        d_o = _dot_nt(dya, wab_ref[...])
        delta = _dot_exact(d_o * o_attn, hs_ref[...])
        for do_ref, dl_ref, d in zip((do0, do1, do2), (dl0, dl1, dl2), DILATIONS):
            _to_residues(d_o, do_ref, scr, d)
            _to_residues(delta, dl_ref, scr, d)
        dys2 = _dot_nt(dys, wsb_ref[...])
        dglu = jnp.concatenate([dys2 * sg, dys2 * p * sg * (1.0 - sg)], axis=1).astype(BF16)
        dglu_ref[...] = dglu
        dy_ref[...] = _dot_nt(dglu, wglu_ref[...]) * _gelu_grad(yv)

    grp = _rows(tm, GROUP_WIDTH)
    wide = _rows(tm, D_MODEL)
    half = _rows(tm, SSM_WIDTH)
    res = [_residue_spec(d, tm) for d in DILATIONS]
    sds = jax.ShapeDtypeStruct
    return pl.pallas_call(
        body, name="mix_out_bwd", grid=(L // tm,),
        in_specs=[wide, grp, half, _rows(tm, 2 * D_MODEL)] + [_whole()] * 5,
        out_specs=res + res + [half, _rows(tm, 2 * D_MODEL), wide, wide, wide, half, half, wide, _acc_row(2 * D_MODEL)],
        out_shape=[_residue_shape(d, L, BF16) for d in DILATIONS] + [_residue_shape(d, L, F32) for d in DILATIONS]
        + [sds((L, SSM_WIDTH), F32),
           sds((L, 2 * D_MODEL), BF16), sds((L, D_MODEL), BF16), sds((L, D_MODEL), BF16),
           sds((L, D_MODEL), BF16), sds((L, SSM_WIDTH), BF16), sds((L, SSM_WIDTH), BF16),
           sds((L, D_MODEL), BF16), sds((1, 2 * D_MODEL), F32)],
        scratch_shapes=[_residue_scratch(tm)],
        compiler_params=_params(),
    )(dx2, o_attn, y, gates, w_ab, w_glu, w_sb, w_out, head_sum)


def _adamw(w, g, m, v, name):
    R, C = w.shape
    tr = _row_tile(R, max(8, ADAMW_BLOCK_BYTES // (4 * C)))

    def body(w_ref, g_ref, m_ref, v_ref, d_ref, mo_ref, vo_ref):
        gv = g_ref[...]
        mn = ADAM_B1 * m_ref[...] + (1.0 - ADAM_B1) * gv
        vn = ADAM_B2 * v_ref[...] + (1.0 - ADAM_B2) * (gv * gv)
        m_hat = mn / (1.0 - ADAM_B1 ** ADAM_STEP)
        v_hat = vn / (1.0 - ADAM_B2 ** ADAM_STEP)
        d_ref[...] = -ADAM_LR * (m_hat / (jnp.sqrt(v_hat) + ADAM_EPS) + ADAM_WD * w_ref[...])
        mo_ref[...] = mn
        vo_ref[...] = vn

    blk = pl.BlockSpec((tr, C), lambda i: (i, 0))
    return pl.pallas_call(
        body, name=name, grid=(R // tr,),
        in_specs=[blk] * 4, out_specs=[blk] * 3,
        out_shape=[jax.ShapeDtypeStruct((R, C), F32)] * 3,
        compiler_params=_params(),
    )(w, g, m, v)


def _sum_chips_into_half(u, t, name):
    S, H, C = u.shape
    tr = _row_tile(H, 512)
    hb = H // tr

    def body(s_ref, t_ref, a_ref, b_ref, c_ref, o_ref):
        me = s_ref[1]
        others = (a_ref[...], b_ref[...], c_ref[...])
        acc = None
        for chip in range(S):
            below = others[min(chip, S - 2)]
            above = others[max(chip - 1, 0)]
            term = jnp.where(me == chip, t_ref[...], jnp.where(me > chip, below, above)).astype(F32)
            acc = term if acc is None else acc + term
        o_ref[...] = acc

    x, y, c = lax.axis_index("x"), lax.axis_index("y"), lax.axis_index("c")
    me = 2 * x + y
    scalars = jnp.stack([c, me] + [j + (j >= me).astype(jnp.int32) for j in range(S - 1)]).astype(jnp.int32)
    blk = (None, tr, C)
    return pl.pallas_call(
        body, name=name,
        grid_spec=pltpu.PrefetchScalarGridSpec(
            num_scalar_prefetch=1, grid=(hb,),
            in_specs=[pl.BlockSpec(blk, lambda i, s: (s[1], i, 0))]
            + [pl.BlockSpec(blk, functools.partial(lambda j, i, s: (s[2 + j], i, 0), j)) for j in range(S - 1)],
            out_specs=pl.BlockSpec((tr, C), lambda i, s: (s[0] * hb + i, 0))),
        out_shape=jax.ShapeDtypeStruct((2 * H, C), F32),
        compiler_params=_params(),
    )(scalars, t, u, u, u)


def _add_halves(g, r1, name):
    S, R, C = g.shape
    H = R // 2
    tr = _row_tile(H, 512)
    hb = H // tr

    def body(c_ref, g_ref, r_ref, o_ref):
        o_ref[...] = (g_ref[...] + r_ref[...]).astype(BF16)

    core = lax.axis_index("c").astype(jnp.int32).reshape(1)
    return pl.pallas_call(
        body, name=name,
        grid_spec=pltpu.PrefetchScalarGridSpec(
            num_scalar_prefetch=1, grid=(S, hb),
            in_specs=[pl.BlockSpec((None, tr, C), lambda j, i, c_ref: (j, c_ref[0] * hb + i, 0)),
                      pl.BlockSpec((None, tr, C), lambda j, i, c_ref: (j, i, 0))],
            out_specs=pl.BlockSpec((None, tr, C), lambda j, i, c_ref: (j, i, 0))),
        out_shape=jax.ShapeDtypeStruct((S, H, C), BF16),
        compiler_params=_params(),
    )(core, g, r1)


_ANY = pl.BlockSpec(memory_space=pl.ANY)


def _place():
    x, y, c = lax.axis_index("x"), lax.axis_index("y"), lax.axis_index("c")
    chips = [(1 - x, y), (x, 1 - y), (1 - x, 1 - y)]
    return x, y, c, chips


def _comm_call(body, name, ins, out_shapes, n_remote, n_local):
    return pl.pallas_call(
        body, name=name,
        in_specs=[_ANY] * len(ins), out_specs=[_ANY] * len(out_shapes), out_shape=out_shapes,
        scratch_shapes=[pltpu.SemaphoreType.DMA((n_remote,)), pltpu.SemaphoreType.DMA((n_remote,)),
                        pltpu.SemaphoreType.DMA((max(n_local, 1),))],
    )(*ins)


def _remote(src, dst, send_sems, recv_sems, k, device):
    return pltpu.make_async_remote_copy(src_ref=src, dst_ref=dst, send_sem=send_sems.at[k], recv_sem=recv_sems.at[k],
                                        device_id=device, device_id_type=MESH)


def _gather_body(shapes, handshake):
    n = len(shapes)

    def body(*refs):
        w_refs, out_refs = refs[:n], refs[n:2 * n]
        send_sems, recv_sems = refs[2 * n:2 * n + 2]
        x, y, c, chips = _place()
        me = 2 * x + y
        sibling = (x, y, 1 - c)
        if handshake:
            _handshake([sibling] + [(cx, cy, c) for cx, cy in chips])

        def half(k, chip_idx, core):
            H = shapes[k][0] // 2
            return out_refs[k].at[chip_idx, pl.ds(core * H, H), :]

        mine = [_remote(w_refs[k], out_refs[k].at[me], send_sems, recv_sems, 6 * n + k, sibling) for k in range(n)]
        for cp in mine:
            cp.start()
        first = []
        for k in range(n):
            H = shapes[k][0] // 2
            for j, (cx, cy) in enumerate(chips):
                first.append(_remote(w_refs[k].at[pl.ds(c * H, H), :], half(k, me, c), send_sems, recv_sems,
                                     3 * k + j, (cx, cy, c)))
        for cp in first:
            cp.start()
        passed = []
        for k in range(n):
            for j, (cx, cy) in enumerate(chips):
                landed = half(k, 2 * cx + cy, c)
                _remote(landed, landed, send_sems, recv_sems, 3 * k + j, (cx, cy, c)).wait_recv()
                fwd = _remote(landed, landed, send_sems, recv_sems, 3 * n + 3 * k + j, sibling)
                fwd.start()
                passed.append(fwd)
        for k in range(n):
            for j, (cx, cy) in enumerate(chips):
                other = half(k, 2 * cx + cy, 1 - c)
                _remote(other, other, send_sems, recv_sems, 3 * n + 3 * k + j, sibling).wait_recv()
        for cp in mine:
            cp.wait_recv()
        for cp in first + passed + mine:
            cp.wait_send()

    return body


def _gather_weights(shards, name):
    n = len(shards)
    return _comm_call(_gather_body([w.shape for w in shards], handshake=False), name, shards,
                      [jax.ShapeDtypeStruct((N_SHARD,) + w.shape, w.dtype) for w in shards], 7 * n, 0)


def _gather_weights_behind(shards, name, collective_id):
    return _sequenced(_gather_body([w.shape for w in shards], handshake=True), name, shards,
                      [jax.ShapeDtypeStruct((N_SHARD,) + w.shape, w.dtype) for w in shards], 7 * len(shards),
                      collective_id)


def _handshake(peers):
    barrier = pltpu.get_barrier_semaphore()
    for peer in peers:
        pl.semaphore_signal(barrier, inc=1, device_id=peer, device_id_type=MESH)
    pl.semaphore_wait(barrier, len(peers))


def _sequenced(body, name, ins, out_shapes, n_sems, collective_id):
    return pl.kernel(
        body, out_type=list(out_shapes), mesh=plsc.ScalarSubcoreMesh(axis_name="sequencer", num_cores=1), name=name,
        scratch_types=(pltpu.SemaphoreType.DMA((n_sems,)), pltpu.SemaphoreType.DMA((n_sems,))),
        compiler_params=pltpu.CompilerParams(collective_id=collective_id))(*ins)


def _swap_halves(gs, name, collective_id):
    n = len(gs)

    def body(*refs):
        g_refs, out_refs = refs[:n], refs[n:2 * n]
        send_sems, recv_sems = refs[2 * n:]
        x, y, c, _ = _place()
        _handshake([(x, y, 1 - c)])
        cps = []
        for k in range(n):
            H = gs[k].shape[1] // 2
            cp = _remote(g_refs[k].at[:, pl.ds((1 - c) * H, H), :], out_refs[k], send_sems, recv_sems, k, (x, y, 1 - c))
            cp.start()
            cps.append(cp)
        for cp in cps:
            cp.wait()

    return _sequenced(body, name, gs, [jax.ShapeDtypeStruct((g.shape[0], g.shape[1] // 2, g.shape[2]), g.dtype)
                                       for g in gs], n, collective_id)


def _exchange_chips(ts, name, collective_id):
    n = len(ts)

    def body(*refs):
        t_refs, out_refs = refs[:n], refs[n:2 * n]
        send_sems, recv_sems = refs[2 * n:]
        x, y, c, chips = _place()
        me = 2 * x + y
        _handshake([(cx, cy, c) for cx, cy in chips])
        sent = []
        for k in range(n):
            for j, (cx, cy) in enumerate(chips):
                cp = _remote(t_refs[k].at[2 * cx + cy], out_refs[k].at[me], send_sems, recv_sems, 3 * k + j, (cx, cy, c))
                cp.start()
                sent.append(cp)
        for k in range(n):
            for j, (cx, cy) in enumerate(chips):
                slot = out_refs[k].at[2 * cx + cy]
                _remote(slot, slot, send_sems, recv_sems, 3 * k + j, (cx, cy, c)).wait_recv()
        for cp in sent:
            cp.wait_send()

    return _sequenced(body, name, ts, [jax.ShapeDtypeStruct(t.shape, t.dtype) for t in ts], 3 * n, collective_id)


def _join_halves(fs, name):
    n = len(fs)

    def body(*refs):
        out_refs = refs[n:2 * n]
        send_sems, recv_sems, _ = refs[2 * n:]
        x, y, c, _ = _place()
        sent = []
        for k in range(n):
            H = fs[k].shape[0] // 2
            here = out_refs[k].at[pl.ds(c * H, H), :]
            cp = _remote(here, here, send_sems, recv_sems, k, (x, y, 1 - c))
            cp.start()
            sent.append(cp)
        for k in range(n):
            H = fs[k].shape[0] // 2
            other = out_refs[k].at[pl.ds((1 - c) * H, H), :]
            _remote(other, other, send_sems, recv_sems, k, (x, y, 1 - c)).wait_recv()
        for cp in sent:
            cp.wait_send()

    return pl.pallas_call(
        body, name=name,
        in_specs=[_ANY] * n, out_specs=[_ANY] * n,
        out_shape=[jax.ShapeDtypeStruct(f.shape, f.dtype) for f in fs],
        input_output_aliases={k: k for k in range(n)},
        scratch_shapes=[pltpu.SemaphoreType.DMA((n,)), pltpu.SemaphoreType.DMA((n,)), pltpu.SemaphoreType.DMA((1,))],
    )(*fs)


def _gather_small(v):
    R, C = v.shape

    def body(v_ref, out_ref, send_sems, recv_sems):
        x, y, c, _ = _place()
        me = 4 * x + 2 * y + c
        flips = [(fx, fy, fc) for fx in (0, 1) for fy in (0, 1) for fc in (0, 1)][1:]
        peers = [((1 - x) if fx else x, (1 - y) if fy else y, (1 - c) if fc else c) for fx, fy, fc in flips]
        _handshake(peers)
        sent = []
        for j, peer in enumerate(peers):
            cp = _remote(v_ref, out_ref.at[me], send_sems, recv_sems, j, peer)
            cp.start()
            sent.append(cp)
        for j, peer in enumerate(peers):
            slot = out_ref.at[4 * peer[0] + 2 * peer[1] + peer[2]]
            _remote(slot, slot, send_sems, recv_sems, j, peer).wait_recv()
        for cp in sent:
            cp.wait_send()

    return _sequenced(body, "gather_small", [v], [jax.ShapeDtypeStruct((8, R, C), F32)], 7,
                      COLLECTIVE_IDS["gather_small"])[0]


def _sum_devices(x, own, name):
    S, R, C = x.shape
    tr = _row_tile(R, 2048)

    def body(s_ref, x_ref, own_ref, o_ref):
        me = s_ref[0]
        acc = None
        for k in range(S):
            term = jnp.where(me == k, own_ref[...], x_ref[k])
            acc = term if acc is None else acc + term
        o_ref[...] = acc

    x_, y_, c_ = lax.axis_index("x"), lax.axis_index("y"), lax.axis_index("c")
    me = (4 * x_ + 2 * y_ + c_).astype(jnp.int32).reshape(1)
    return pl.pallas_call(
        body, name=name,
        grid_spec=pltpu.PrefetchScalarGridSpec(
            num_scalar_prefetch=1, grid=(R // tr,),
            in_specs=[pl.BlockSpec((S, tr, C), lambda i, s: (0, i, 0)), pl.BlockSpec((tr, C), lambda i, s: (i, 0))],
            out_specs=pl.BlockSpec((tr, C), lambda i, s: (i, 0))),
        out_shape=jax.ShapeDtypeStruct((R, C), F32),
        compiler_params=_params(),
    )(me, x, own)


def _after(earlier, arrays):
    return lax.optimization_barrier((earlier, arrays))


def _reduce_exchange(gs, names, tag, earlier):
    earlier, gs = _after(earlier, gs)
    r1 = _swap_halves(gs, "reduce_swap_" + tag, COLLECTIVE_IDS["swap_" + tag])
    ts = [_add_halves(g, r, "reduce_add_cores_" + nm) for g, r, nm in zip(gs, r1, names)]
    us = _exchange_chips(ts, "reduce_exchange_" + tag, COLLECTIVE_IDS["exchange_" + tag])
    return us, ts, earlier


def _reduce_finish(us, ts, names, tag):
    fs = [_sum_chips_into_half(u, t, "reduce_add_chips_" + nm) for u, t, nm in zip(us, ts, names)]
    return _join_halves(fs, "reduce_join_" + tag)


BIG = ["ffn1_w_gate", "ffn1_w_up", "ffn1_w_down", "w_in", "ssm_w_glu", "w_attn_branch", "w_ssm_branch",
       "w_out", "ffn2_w_gate", "ffn2_w_up", "ffn2_w_down"]
SMALL = ["ffn1_norm", "mix_norm", "gate_bias", "rel_bias_table", "ssm_a_re", "ssm_a_im", "ssm_log_dt",
         "ssm_b_re", "ssm_b_im", "ssm_c_re", "ssm_c_im", "ssm_d", "ffn2_norm", "final_norm"]
ORDER = ["ffn1_norm", "ffn1_w_gate", "ffn1_w_up", "ffn1_w_down", "mix_norm", "w_in", "gate_bias", "rel_bias_table",
         "ssm_a_re", "ssm_a_im", "ssm_log_dt", "ssm_b_re", "ssm_b_im", "ssm_c_re", "ssm_c_im", "ssm_d",
         "ssm_w_glu", "w_attn_branch", "w_ssm_branch", "w_out", "ffn2_norm", "ffn2_w_gate", "ffn2_w_up",
         "ffn2_w_down", "final_norm"]


_SMALL_TILE = 8 * LANES


def _pack_small(arrays):
    rows = []
    for a in arrays:
        flat = a.reshape(-1).astype(F32)
        rows.append(jnp.pad(flat, (0, (-flat.shape[0]) % _SMALL_TILE)).reshape(-1, LANES))
    return jnp.concatenate(rows, axis=0)


def _unpack_small(packed, shapes):
    out, r0 = [], 0
    for shp in shapes:
        n = math.prod(shp)
        rows = 8 * -(-n // _SMALL_TILE)
        out.append(packed[r0:r0 + rows].reshape(-1)[:n].reshape(shp))
        r0 += rows
    return out


def _local_step(x, target, w, small):
    L = x.shape[0]
    row = lambda v: v.reshape(1, -1)

    a_re, a_im = small["ssm_a_re"].reshape(1, NS), small["ssm_a_im"].reshape(1, NS)
    ldt = jnp.repeat(small["ssm_log_dt"].reshape(SSM_GROUPS), SSM_STATE).reshape(1, NS)
    to_cn = lambda b: b.reshape(SSM_GROUPS, SSM_STATE, SSM_GROUP).transpose(2, 0, 1).reshape(SSM_GROUP, NS)
    c_to_cn = lambda c: c.reshape(SSM_GROUPS, SSM_GROUP, SSM_STATE).transpose(1, 0, 2).reshape(SSM_GROUP, NS)
    b_re, b_im = to_cn(small["ssm_b_re"]), to_cn(small["ssm_b_im"])
    c_re, c_im = c_to_cn(small["ssm_c_re"]), c_to_cn(small["ssm_c_im"])
    d_skip = row(small["ssm_d"])
    pw, pwr, bd, cdt = _disc_fwd(a_re, a_im, ldt, b_re, b_im, c_re, c_im)

    onehot = _bucket_onehot()
    table_t = small["rel_bias_table"].T.reshape(3, HEADS_PER_GROUP, N_BUCKETS)
    table_t = jnp.pad(table_t, ((0, 0), (0, 8 - HEADS_PER_GROUP), (0, 0)))
    bias = _bias_expand(table_t, onehot)[:, :, :HEADS_PER_GROUP].reshape(
        3, 2, HEADS_PER_GROUP, ATTN_BLOCK, 2 * ATTN_BLOCK)

    n1, nm, n2, nf = row(small["ffn1_norm"]), row(small["mix_norm"]), row(small["ffn2_norm"]), row(small["final_norm"])
    gate_bias = row(small["gate_bias"])

    x1, a1, b1 = _ffn_fwd(x, n1, w["ffn1_w_gate"], w["ffn1_w_up"], w["ffn1_w_down"], "ffn1_fwd")
    *qkv, u, gates = _mix_in_fwd(x1, nm, w["w_in"], gate_bias)
    q, k, v = qkv[0:3], qkv[3:6], qkv[6:9]
    o_g, lse_g = [], []
    for grp in range(3):
        o, lse = _attn_fwd(q[grp], k[grp], v[grp], bias[grp], f"attn_fwd_{grp}")
        o_g.append(o)
        lse_g.append(lse)
    y, s = _ssm_fwd(u, bd, cdt, d_skip, pw)
    x2, o_attn, *lse_tot = _mix_out_fwd(x1, o_g, lse_g, y, gates, w["w_attn_branch"], w["ssm_w_glu"],
                                        w["w_ssm_branch"], w["w_out"])
    x3, a2, b2 = _ffn_fwd(x2, n2, w["ffn2_w_gate"], w["ffn2_w_up"], w["ffn2_w_down"], "ffn2_fwd")
    loss_blk, dx3, d_nf = _loss_fwd_bwd(x3, nf, target)

    gw, gs = {}, {}
    gs["final_norm"] = d_nf

    dx2, da, db, sact, h, d_out, gs["ffn2_norm"] = _ffn_bwd(dx3, x2, n2, a2, b2, w["ffn2_w_gate"], w["ffn2_w_up"],
                                                            w["ffn2_w_down"], "ffn2_bwd")
    gw["ffn2_w_gate"] = _matmul_tn(h[None], da, "ffn2_dw_gate")
    gw["ffn2_w_up"] = _matmul_tn(h[None], db, "ffn2_dw_up")
    gw["ffn2_w_down"] = _matmul_tn(sact, d_out[None], "ffn2_dw_down")

    head_sum = (jnp.arange(GROUP_WIDTH)[:, None] // HEAD_DIM == jnp.arange(GROUP_WIDTH)[None, :] // HEAD_DIM).astype(F32)
    (*d_o_delta, dy, dgp, mix, dya, dys, ys2, gel, dglu, gs["gate_bias"]) = _mix_out_bwd(
        dx2, o_attn, y, gates, w["w_attn_branch"], w["ssm_w_glu"], w["w_ssm_branch"], w["w_out"], head_sum)
    d_o, delta = d_o_delta[0:3], d_o_delta[3:6]
    gw["w_out"] = _matmul_tn(mix[None], dx2[None], "dw_out")[0]
    gw["w_attn_branch"] = _matmul_tn(o_attn[None], dya[None], "dw_attn_branch")[0]
    gw["w_ssm_branch"] = _matmul_tn(ys2[None], dys[None], "dw_ssm_branch")[0]
    gw["ssm_w_glu"] = _matmul_tn(gel[None], dglu[None], "dw_glu")[0]

    dqs, dks, dvs, dsums = [], [], [], []
    for grp in range(3):
        dq, dk, dv, dsum = _attn_bwd(q[grp], k[grp], v[grp], d_o[grp], lse_tot[grp], delta[grp], bias[grp],
                                     f"attn_bwd_{grp}")
        dqs.append(dq)
        dks.append(dk)
        dvs.append(dv)
        dsums.append(dsum.reshape(HEADS_PER_GROUP, -1))
    dsum_all = jnp.pad(jnp.stack(dsums), ((0, 0), (0, 8 - HEADS_PER_GROUP), (0, 0)))
    d_table = _bias_reduce(dsum_all, onehot)[:, :HEADS_PER_GROUP]
    gs["rel_bias_table"] = d_table.reshape(3 * HEADS_PER_GROUP, N_BUCKETS).T

    du, gs["ssm_d"], d_bd, d_cdt, d_ab = _ssm_bwd(dy, u, s, bd, cdt, d_skip, pwr)
    group_sum = (jnp.arange(NS)[:, None] // SSM_STATE == jnp.arange(128)[None, :]).astype(F32)
    d_are, d_aim, d_ldt, d_bre, d_bim, d_cre, d_cim = _disc_bwd(a_re, a_im, ldt, b_re, b_im, d_bd, d_cdt, d_ab, group_sum)
    gs["ssm_a_re"], gs["ssm_a_im"] = d_are, d_aim
    gs["ssm_log_dt"] = d_ldt[0, :SSM_GROUPS]
    from_cn = lambda t: t.reshape(SSM_GROUP, SSM_GROUPS, SSM_STATE).transpose(1, 2, 0)
    c_from_cn = lambda t: t.reshape(SSM_GROUP, SSM_GROUPS, SSM_STATE).transpose(1, 0, 2)
    gs["ssm_b_re"], gs["ssm_b_im"] = from_cn(d_bre), from_cn(d_bim)
    gs["ssm_c_re"], gs["ssm_c_im"] = c_from_cn(d_cre), c_from_cn(d_cim)

    dx1, hm, dz, gs["mix_norm"] = _mix_in_bwd(dx2, x1, nm, dqs + dks + dvs, du, dgp, w["w_in"])
    gw["w_in"] = _matmul_tn(hm[None], dz[None], "dw_in")[0]

    dx0, da, db, sact, h, d_out, gs["ffn1_norm"] = _ffn_bwd(dx1, x, n1, a1, b1, w["ffn1_w_gate"], w["ffn1_w_up"],
                                                            w["ffn1_w_down"], "ffn1_bwd")
    gw["ffn1_w_gate"] = _matmul_tn(h[None], da, "ffn1_dw_gate")
    gw["ffn1_w_up"] = _matmul_tn(h[None], db, "ffn1_dw_up")
    gw["ffn1_w_down"] = _matmul_tn(sact, d_out[None], "ffn1_dw_down")
    return loss_blk, dx0, gw, gs


def _split_cols(g):
    K, N = g.shape
    return g.reshape(K, N_SHARD, N // N_SHARD).transpose(1, 0, 2)


def _join_cols(w):
    S, K, n = w.shape
    return w.transpose(1, 0, 2).reshape(K, S * n)


COL_SHARDED = ("w_in", "ssm_w_glu", "w_attn_branch", "w_ssm_branch")


def kernel(x, ffn1_norm, ffn1_w_gate, ffn1_w_up, ffn1_w_down, mix_norm, w_in, gate_bias, rel_bias_table, ssm_a_re, ssm_a_im, ssm_log_dt, ssm_b_re, ssm_b_im, ssm_c_re, ssm_c_im, ssm_d, ssm_w_glu, w_attn_branch, w_ssm_branch, w_out, ffn2_norm, ffn2_w_gate, ffn2_w_up, ffn2_w_down, final_norm, loss_target, m_ffn1_norm, m_ffn1_w_gate, m_ffn1_w_up, m_ffn1_w_down, m_mix_norm, m_w_in, m_gate_bias, m_rel_bias_table, m_ssm_a_re, m_ssm_a_im, m_ssm_log_dt, m_ssm_b_re, m_ssm_b_im, m_ssm_c_re, m_ssm_c_im, m_ssm_d, m_ssm_w_glu, m_w_attn_branch, m_w_ssm_branch, m_w_out, m_ffn2_norm, m_ffn2_w_gate, m_ffn2_w_up, m_ffn2_w_down, m_final_norm, v_ffn1_norm, v_ffn1_w_gate, v_ffn1_w_up, v_ffn1_w_down, v_mix_norm, v_w_in, v_gate_bias, v_rel_bias_table, v_ssm_a_re, v_ssm_a_im, v_ssm_log_dt, v_ssm_b_re, v_ssm_b_im, v_ssm_c_re, v_ssm_c_im, v_ssm_d, v_ssm_w_glu, v_w_attn_branch, v_w_ssm_branch, v_w_out, v_ffn2_norm, v_ffn2_w_gate, v_ffn2_w_up, v_ffn2_w_down, v_final_norm):
    args = dict(locals())
    weights = {n: args[n] for n in ORDER}
    moms = {n: args["m_" + n] for n in ORDER}
    vels = {n: args["v_" + n] for n in ORDER}

    shard2d = {n: weights[n].reshape(weights[n].shape[-2:]) for n in BIG}
    first, rest = BIG[:3], BIG[3:]
    full = dict(zip(first, _gather_weights([shard2d[n].astype(BF16) for n in first], "gather_ffn1_weights")))
    later_full = _gather_weights_behind([shard2d[n].astype(BF16) for n in rest], "gather_later_weights",
                                        COLLECTIVE_IDS["gather"])
    full.update(zip(rest, later_full))
    for n in COL_SHARDED:
        full[n] = _join_cols(full[n])
    full["w_out"] = full["w_out"].reshape(D_MODEL, D_MODEL)

    small = {n: weights[n] for n in SMALL}
    loss_blk, grad_x, gw, gs = _local_step(x[0], loss_target[0], full, small)

    for n in COL_SHARDED:
        gw[n] = _split_cols(gw[n])
    gw["w_out"] = gw["w_out"].reshape(N_SHARD, D_MODEL // N_SHARD, D_MODEL)
    grads, exchanged = {}, {}
    earlier = []
    small_shapes = [weights[n].shape for n in SMALL]
    for tag, names in REDUCE_GROUPS.items():
        if tag == "ffn1":
            earlier, (mine,) = _after(earlier, [_pack_small([gs[n] for n in SMALL] + [loss_blk[0:1, :]])])
            others = _gather_small(mine)
            earlier = [others]
        us, ts, _ = _reduce_exchange([gw[n] for n in names], names, tag, earlier)
        exchanged[tag] = (us, ts)
        earlier = us
    for tag, names in REDUCE_GROUPS.items():
        grads.update(zip(names, _reduce_finish(*exchanged[tag], names, tag)))
    total = _sum_devices(others, mine, "sum_small")
    small_grads = _unpack_small(total, small_shapes + [(128,)])
    loss = small_grads[-1][0]
    for n, g in zip(SMALL, small_grads[:-1]):
        grads[n] = g

    delta, new_m, new_v = {}, {}, {}
    for n in BIG:
        d, m, v = _adamw(shard2d[n], grads[n], moms[n].reshape(shard2d[n].shape), vels[n].reshape(shard2d[n].shape),
                         "adamw_" + n)
        shp = weights[n].shape
        delta[n], new_m[n], new_v[n] = d.reshape(shp), m.reshape(shp), v.reshape(shp)
        grads[n] = grads[n].reshape(shp)
    d, m, v = _adamw(_pack_small([weights[n] for n in SMALL]), _pack_small([grads[n] for n in SMALL]),
                     _pack_small([moms[n] for n in SMALL]), _pack_small([vels[n] for n in SMALL]), "adamw_small")
    for n, dd, mm, vv in zip(SMALL, _unpack_small(d, small_shapes), _unpack_small(m, small_shapes),
                             _unpack_small(v, small_shapes)):
        delta[n], new_m[n], new_v[n] = dd, mm, vv

    return (loss, grad_x[None], *[grads[n] for n in ORDER], *[delta[n] for n in ORDER],
            *[new_m[n] for n in ORDER], *[new_v[n] for n in ORDER])
```

```python
import functools
import math

import jax
import jax.numpy as jnp
from jax import lax
from jax.experimental import pallas as pl
from jax.experimental.pallas import tpu as pltpu
from jax.experimental.pallas import tpu_sc as plsc

F32 = jnp.float32
BF16 = jnp.bfloat16
MESH = pl.DeviceIdType.MESH

D_MODEL = 1024
D_FF = 2816
HEAD_DIM = 64
HEADS_PER_GROUP = 4
DILATIONS = (1, 4, 16)
WINDOW_STEPS = 128
ATTN_BLOCK = 128
ATTN_QB = 4
GROUP_WIDTH = HEADS_PER_GROUP * HEAD_DIM
ATTN_WIDTH = 3 * GROUP_WIDTH
N_BUCKETS = 32
MAX_DISTANCE = 2048
NEG_INF = -1e30
SSM_WIDTH = 512
SSM_GROUP = 16
SSM_GROUPS = 32
SSM_STATE = 64
NS = SSM_GROUPS * SSM_STATE
EPS = 1e-6
IN_WIDTH = 3 * ATTN_WIDTH + SSM_WIDTH + 2 * D_MODEL
Q_SCALE = HEAD_DIM ** -0.5
N_SHARD = 4
FF_SHARD = D_FF // N_SHARD
ADAM_LR, ADAM_B1, ADAM_B2, ADAM_EPS, ADAM_WD, ADAM_STEP = 0.001, 0.9, 0.999, 1e-08, 0.01, 10

LANES = 128
VMEM_LIMIT = 56 * 1024 * 1024
ROW_TILE = 512
FFN_BWD_TILE = 256
SSM_CHUNK = 256
SCAN_LANES = 512
ADAMW_BLOCK_BYTES = 1 << 20
TN_VMEM_BUDGET = 40 * 1024 * 1024
REDUCE_GROUPS = {
    "ffn2": ["ffn2_w_gate", "ffn2_w_up", "ffn2_w_down"],
    "mixer": ["w_out", "w_attn_branch", "w_ssm_branch", "ssm_w_glu"],
    "w_in": ["w_in"],
    "ffn1": ["ffn1_w_gate", "ffn1_w_up", "ffn1_w_down"],
}
COLLECTIVE_IDS = {name: i for i, name in enumerate(
    ["gather", "gather_small"] + [stage + "_" + tag for tag in REDUCE_GROUPS for stage in ("swap", "exchange")])}


def _params(**kw):
    return pltpu.CompilerParams(vmem_limit_bytes=VMEM_LIMIT, **kw)


def _dot(a, b):
    return jnp.dot(a, b, preferred_element_type=F32)


def _dot_nt(a, b):
    return lax.dot_general(a, b, (((1,), (1,)), ((), ())), preferred_element_type=F32)


def _dot_tn(a, b):
    return lax.dot_general(a, b, (((0,), (0,)), ((), ())), preferred_element_type=F32)


def _dot_exact(a, b):
    return jnp.dot(a, b, preferred_element_type=F32, precision=lax.Precision.HIGHEST)


def _dot_nt_exact(a, b):
    return lax.dot_general(a, b, (((1,), (1,)), ((), ())), preferred_element_type=F32,
                           precision=lax.Precision.HIGHEST)


def _rms(x):
    r = lax.rsqrt(jnp.mean(x * x, axis=-1, keepdims=True) + EPS)
    return r, x * r


def _rms_bwd(dh, g, r, xhat):
    dxh = dh * g
    return r * (dxh - xhat * jnp.mean(dxh * xhat, axis=-1, keepdims=True))


def _sigmoid(x):
    return 1.0 / (1.0 + jnp.exp(-x))


_GELU_C = math.sqrt(2.0 / math.pi)


def _gelu(x):
    return 0.5 * x * (1.0 + jnp.tanh(_GELU_C * (x + 0.044715 * x * x * x)))


def _gelu_grad(x):
    t = jnp.tanh(_GELU_C * (x + 0.044715 * x * x * x))
    return 0.5 * (1.0 + t) + 0.5 * x * (1.0 - t * t) * _GELU_C * (1.0 + 3 * 0.044715 * x * x)


def _whole():
    return pl.BlockSpec(memory_space=pltpu.VMEM)


def _row_tile(rows, cap):
    if rows <= cap:
        return rows
    return max(t for t in range(8, cap + 1, 8) if rows % t == 0)


def _rows(tm, w):
    return pl.BlockSpec((tm, w), lambda i: (i, 0))


def _acc_row(w):
    return pl.BlockSpec((1, w), lambda i: (0, 0))


def _ffn_fwd(x, g, wg, wu, wd, name):
    L = x.shape[0]
    tm = min(ROW_TILE, L)

    def body(x_ref, g_ref, wg_ref, wu_ref, wd_ref, xo_ref, a_ref, b_ref):
        xv = x_ref[...]
        r, xhat = _rms(xv)
        h = (xhat * g_ref[...]).astype(BF16)
        acc = jnp.zeros((tm, D_MODEL), F32)
        for j in range(N_SHARD):
            a = _dot(h, wg_ref[j])
            b = _dot(h, wu_ref[j])
            a_ref[j] = a.astype(BF16)
            b_ref[j] = b.astype(BF16)
            s = (a * _sigmoid(a) * b).astype(BF16)
            acc = acc + _dot(s, wd_ref[j])
        xo_ref[...] = xv + 0.5 * acc

    act = pl.BlockSpec((N_SHARD, tm, FF_SHARD), lambda i: (0, i, 0))
    return pl.pallas_call(
        body, name=name, grid=(L // tm,),
        in_specs=[_rows(tm, D_MODEL), _whole(), _whole(), _whole(), _whole()],
        out_specs=[_rows(tm, D_MODEL), act, act],
        out_shape=[jax.ShapeDtypeStruct((L, D_MODEL), F32),
                   jax.ShapeDtypeStruct((N_SHARD, L, FF_SHARD), BF16),
                   jax.ShapeDtypeStruct((N_SHARD, L, FF_SHARD), BF16)],
        compiler_params=_params(),
    )(x, g, wg, wu, wd)


def _ffn_bwd(dxo, x, g, a, b, wg, wu, wd, name):
    L = x.shape[0]
    tm = min(FFN_BWD_TILE, L)

    def body(dxo_ref, x_ref, g_ref, a_ref, b_ref, wg_ref, wu_ref, wd_ref,
             dxi_ref, da_ref, db_ref, s_ref, h_ref, do_ref, dg_ref):
        i = pl.program_id(0)
        xv = x_ref[...]
        gv = g_ref[...]
        r, xhat = _rms(xv)
        h_ref[...] = (xhat * gv).astype(BF16)
        dxo_v = dxo_ref[...]
        d_out = (0.5 * dxo_v).astype(BF16)
        do_ref[...] = d_out
        dh = jnp.zeros((tm, D_MODEL), F32)
        for j in range(N_SHARD):
            av = a_ref[j].astype(F32)
            bv = b_ref[j].astype(F32)
            sg = _sigmoid(av)
            sl = av * sg
            ds = _dot_nt(d_out, wd_ref[j])
            dbv = (ds * sl).astype(BF16)
            dav = (ds * bv * (sg * (1.0 + av * (1.0 - sg)))).astype(BF16)
            da_ref[j] = dav
            db_ref[j] = dbv
            s_ref[j] = (sl * bv).astype(BF16)
            dh = dh + _dot_nt(dav, wg_ref[j]) + _dot_nt(dbv, wu_ref[j])

        @pl.when(i == 0)
        def _():
            dg_ref[...] = jnp.zeros_like(dg_ref)

        dg_ref[...] += jnp.sum(dh * xhat, axis=0, keepdims=True)
        dxi_ref[...] = dxo_v + _rms_bwd(dh, gv, r, xhat)

    act = pl.BlockSpec((N_SHARD, tm, FF_SHARD), lambda i: (0, i, 0))
    act_shape = jax.ShapeDtypeStruct((N_SHARD, L, FF_SHARD), BF16)
    return pl.pallas_call(
        body, name=name, grid=(L // tm,),
        in_specs=[_rows(tm, D_MODEL), _rows(tm, D_MODEL), _whole(), act, act, _whole(), _whole(), _whole()],
        out_specs=[_rows(tm, D_MODEL), act, act, act, _rows(tm, D_MODEL), _rows(tm, D_MODEL), _acc_row(D_MODEL)],
        out_shape=[jax.ShapeDtypeStruct((L, D_MODEL), F32), act_shape, act_shape, act_shape,
                   jax.ShapeDtypeStruct((L, D_MODEL), BF16), jax.ShapeDtypeStruct((L, D_MODEL), BF16),
                   jax.ShapeDtypeStruct((1, D_MODEL), F32)],
        compiler_params=_params(),
    )(dxo, x, g, a, b, wg, wu, wd)


def _matmul_tn(a, b, name):
    ja, L, K = a.shape
    jb, _, N = b.shape
    J = max(ja, jb)
    splits = [s for s in (1, 2, 4, 8) if s == 1 or N % (s * LANES) == 0]
    nsplit = next((s for s in splits if 2 * K * (N // s) * 4 <= TN_VMEM_BUDGET // 2), splits[-1])
    nc = N // nsplit
    left = TN_VMEM_BUDGET - 2 * K * nc * 4
    row_bytes = 2 * (K * a.dtype.itemsize + nc * b.dtype.itemsize)
    tm = next((t for t in (2048, 1024, 512, 256) if L % t == 0 and t * row_bytes <= left), min(128, L))

    def body(a_ref, b_ref, o_ref):
        @pl.when(pl.program_id(2) == 0)
        def _():
            o_ref[...] = jnp.zeros_like(o_ref)

        o_ref[...] += _dot_tn(a_ref[...].astype(BF16), b_ref[...].astype(BF16))

    return pl.pallas_call(
        body, name=name, grid=(J, nsplit, L // tm),
        in_specs=[pl.BlockSpec((None, tm, K), (lambda j, s, i: (j, i, 0)) if ja > 1 else (lambda j, s, i: (0, i, 0))),
                  pl.BlockSpec((None, tm, nc), (lambda j, s, i: (j, i, s)) if jb > 1 else (lambda j, s, i: (0, i, s)))],
        out_specs=pl.BlockSpec((None, K, nc), lambda j, s, i: (j, 0, s)),
        out_shape=jax.ShapeDtypeStruct((J, K, N), F32),
        compiler_params=_params(),
    )(a, b)


def _loss_fwd_bwd(x, g, target):
    L = x.shape[0]
    tm = min(ROW_TILE, L)

    def body(x_ref, g_ref, t_ref, loss_ref, dx_ref, dg_ref):
        i = pl.program_id(0)
        xv = x_ref[...]
        gv = g_ref[...]
        r, xhat = _rms(xv)
        err = xhat * gv - t_ref[...]
        part = 0.5 * jnp.sum(jnp.sum(err * err, axis=1, keepdims=True) * (1.0 / D_MODEL), axis=0, keepdims=True)
        dy = err * (1.0 / D_MODEL)

        @pl.when(i == 0)
        def _():
            dg_ref[...] = jnp.zeros_like(dg_ref)
            loss_ref[...] = jnp.zeros_like(loss_ref)

        loss_ref[...] += jnp.broadcast_to(part, loss_ref.shape)
        dg_ref[...] += jnp.sum(dy * xhat, axis=0, keepdims=True)
        dx_ref[...] = _rms_bwd(dy, gv, r, xhat)

    return pl.pallas_call(
        body, name="loss_fwd_bwd", grid=(L // tm,),
        in_specs=[_rows(tm, D_MODEL), _whole(), _rows(tm, D_MODEL)],
        out_specs=[pl.BlockSpec((8, 128), lambda i: (0, 0)), _rows(tm, D_MODEL), _acc_row(D_MODEL)],
        out_shape=[jax.ShapeDtypeStruct((8, 128), F32), jax.ShapeDtypeStruct((L, D_MODEL), F32),
                   jax.ShapeDtypeStruct((1, D_MODEL), F32)],
        compiler_params=_params(),
    )(x, g, target)


_C_K = ATTN_WIDTH
_C_V = 2 * ATTN_WIDTH
_C_U = 3 * ATTN_WIDTH
_C_G = _C_U + SSM_WIDTH


def _residue_spec(d, tm):
    return pl.BlockSpec((d, tm // d, GROUP_WIDTH), lambda i: (0, i, 0))


def _residue_shape(d, L, dtype):
    return jax.ShapeDtypeStruct((d, L // d, GROUP_WIDTH), dtype)


def _residue_scratch(tm):
    return pltpu.VMEM((GROUP_WIDTH // LANES, tm, LANES), F32)


def _to_residues(val, out_ref, scr, d):
    if d == 1:
        out_ref[0] = val.astype(out_ref.dtype)
        return
    tm = val.shape[0]
    for half in range(GROUP_WIDTH // LANES):
        cols = slice(half * LANES, (half + 1) * LANES)
        scr[half] = val[:, cols]
        for r in range(d):
            out_ref[r, :, cols] = scr[half, pl.ds(r, tm // d, stride=d), :].astype(out_ref.dtype)


def _from_residues(ref, scr, d):
    if d == 1:
        return ref[0].astype(F32)
    rows = ref.shape[1]
    for half in range(GROUP_WIDTH // LANES):
        cols = slice(half * LANES, (half + 1) * LANES)
        for r in range(d):
            scr[half, pl.ds(r, rows, stride=d), :] = ref[r, :, cols].astype(F32)
    return jnp.concatenate([scr[half] for half in range(GROUP_WIDTH // LANES)], axis=1)


def _mix_in_fwd(x, g, w_in, gate_bias):
    L = x.shape[0]
    tm = min(ROW_TILE, L)

    def body(x_ref, g_ref, w_ref, gb_ref, *refs):
        qkv_refs, (u_ref, gate_ref, scr) = refs[:9], refs[9:]
        r, xhat = _rms(x_ref[...])
        h = (xhat * g_ref[...]).astype(BF16)
        for part, (c0, scale) in enumerate(((0, Q_SCALE), (_C_K, 1.0), (_C_V, 1.0))):
            z = _dot(h, w_ref[:, c0:c0 + ATTN_WIDTH]) * scale
            for grp, d in enumerate(DILATIONS):
                _to_residues(z[:, grp * GROUP_WIDTH:(grp + 1) * GROUP_WIDTH], qkv_refs[3 * part + grp], scr, d)
        u_ref[...] = _dot(h, w_ref[:, _C_U:_C_G])
        gate_ref[...] = _sigmoid(_dot(h, w_ref[:, _C_G:IN_WIDTH]) + gb_ref[...])

    return pl.pallas_call(
        body, name="mix_in_fwd", grid=(L // tm,),
        in_specs=[_rows(tm, D_MODEL), _whole(), _whole(), _whole()],
        out_specs=[_residue_spec(d, tm) for d in DILATIONS] * 3 + [_rows(tm, SSM_WIDTH), _rows(tm, 2 * D_MODEL)],
        out_shape=[_residue_shape(d, L, BF16) for d in DILATIONS] * 3
        + [jax.ShapeDtypeStruct((L, SSM_WIDTH), F32), jax.ShapeDtypeStruct((L, 2 * D_MODEL), F32)],
        scratch_shapes=[_residue_scratch(tm)],
        compiler_params=_params(),
    )(x, g, w_in, gate_bias)


def _mix_in_bwd(dx2, x, g, dqkv, du, dgp, w_in):
    L = x.shape[0]
    tm = min(ROW_TILE, L)

    def body(dx2_ref, x_ref, g_ref, *refs):
        piece_refs = refs[:9]
        du_ref, dgp_ref, w_ref, dx1_ref, h_ref, dz_ref, dg_ref, scr = refs[9:]
        i = pl.program_id(0)
        gv = g_ref[...]
        r, xhat = _rms(x_ref[...])
        h_ref[...] = (xhat * gv).astype(BF16)
        for part in range(3):
            for grp, d in enumerate(DILATIONS):
                c0 = part * ATTN_WIDTH + grp * GROUP_WIDTH
                dz_ref[:, c0:c0 + GROUP_WIDTH] = _from_residues(piece_refs[3 * part + grp], scr, d).astype(BF16)
        dz_ref[:, _C_U:_C_G] = du_ref[...].astype(BF16)
        dz_ref[:, _C_G:IN_WIDTH] = dgp_ref[...]
        dh = _dot_nt(dz_ref[...], w_ref[...])

        @pl.when(i == 0)
        def _():
            dg_ref[...] = jnp.zeros_like(dg_ref)

        dg_ref[...] += jnp.sum(dh * xhat, axis=0, keepdims=True)
        dx1_ref[...] = dx2_ref[...] + _rms_bwd(dh, gv, r, xhat)

    return pl.pallas_call(
        body, name="mix_in_bwd", grid=(L // tm,),
        in_specs=[_rows(tm, D_MODEL), _rows(tm, D_MODEL), _whole()] + [_residue_spec(d, tm) for d in DILATIONS] * 3
        + [_rows(tm, SSM_WIDTH), _rows(tm, 2 * D_MODEL), _whole()],
        out_specs=[_rows(tm, D_MODEL), _rows(tm, D_MODEL), _rows(tm, IN_WIDTH), _acc_row(D_MODEL)],
        out_shape=[jax.ShapeDtypeStruct((L, D_MODEL), F32), jax.ShapeDtypeStruct((L, D_MODEL), BF16),
                   jax.ShapeDtypeStruct((L, IN_WIDTH), BF16), jax.ShapeDtypeStruct((1, D_MODEL), F32)],
        scratch_shapes=[_residue_scratch(tm)],
        compiler_params=_params(),
    )(dx2, x, g, *dqkv, du, dgp, w_in)


def _bucket_onehot():
    qi = jnp.arange(ATTN_BLOCK)[:, None]
    kj = jnp.arange(2 * ATTN_BLOCK)[None, :]
    steps = jnp.maximum(qi + ATTN_BLOCK - kj, 0)
    max_exact = N_BUCKETS // 2
    out = []
    for d in DILATIONS:
        dist = steps * d
        df = jnp.maximum(dist, 1).astype(F32)
        large = max_exact + (jnp.log(df / max_exact) / math.log(MAX_DISTANCE / max_exact)
                             * (N_BUCKETS - max_exact)).astype(jnp.int32)
        large = jnp.minimum(large, N_BUCKETS - 1)
        bucket = jnp.where(dist < max_exact, dist, large).reshape(-1)
        out.append((bucket[None, :] == jnp.arange(N_BUCKETS)[:, None]).astype(F32))
    return jnp.stack(out)


def _bias_expand(table_t, onehot):
    n = onehot.shape[-1]

    def body(t_ref, oh_ref, o_ref):
        bias = _dot_exact(t_ref[...], oh_ref[...])
        col = lax.broadcasted_iota(jnp.int32, (8, n), 1)
        qi = col // (2 * ATTN_BLOCK)
        kj = col - qi * (2 * ATTN_BLOCK)
        steps = qi + ATTN_BLOCK - kj
        band = (steps >= 0) & (steps <= WINDOW_STEPS)
        o_ref[0] = jnp.where(band & (kj >= ATTN_BLOCK), bias, NEG_INF)
        o_ref[1] = jnp.where(band, bias, NEG_INF)

    return pl.pallas_call(
        body, name="bias_expand", grid=(3,),
        in_specs=[pl.BlockSpec((None, 8, N_BUCKETS), lambda g: (g, 0, 0)),
                  pl.BlockSpec((None, N_BUCKETS, n), lambda g: (g, 0, 0))],
        out_specs=pl.BlockSpec((None, 2, 8, n), lambda g: (g, 0, 0, 0)),
        out_shape=jax.ShapeDtypeStruct((3, 2, 8, n), F32),
        compiler_params=_params(),
    )(table_t, onehot)


def _bias_reduce(dsum, onehot):
    n = onehot.shape[-1]

    def body(d_ref, oh_ref, o_ref):
        o_ref[...] = _dot_nt_exact(d_ref[...], oh_ref[...])

    return pl.pallas_call(
        body, name="bias_reduce", grid=(3,),
        in_specs=[pl.BlockSpec((None, 8, n), lambda g: (g, 0, 0)),
                  pl.BlockSpec((None, N_BUCKETS, n), lambda g: (g, 0, 0))],
        out_specs=pl.BlockSpec((None, 8, N_BUCKETS), lambda g: (g, 0, 0)),
        out_shape=jax.ShapeDtypeStruct((3, 8, N_BUCKETS), F32),
        compiler_params=_params(),
    )(dsum, onehot)


def _head_of_col(rows):
    return lax.broadcasted_iota(jnp.int32, (rows, GROUP_WIDTH), 1) // HEAD_DIM


def _attn_specs(qb):
    rows = qb * ATTN_BLOCK
    cur = pl.BlockSpec((None, rows, GROUP_WIDTH), lambda r, n: (r, n, 0))
    prev = pl.BlockSpec((None, ATTN_BLOCK, GROUP_WIDTH), lambda r, n: (r, jnp.maximum(n * qb - 1, 0), 0))
    bias = pl.BlockSpec((2, HEADS_PER_GROUP, ATTN_BLOCK, 2 * ATTN_BLOCK), lambda r, n: (0, 0, 0, 0))
    return cur, prev, bias


def _attn_fwd(q, k, v, bias, name):
    d, M, _ = q.shape
    nb = M // ATTN_BLOCK
    qb = min(ATTN_QB, nb)

    def body(q_ref, kp_ref, kc_ref, vp_ref, vc_ref, bias_ref, o_ref, lse_ref):
        n = pl.program_id(1)
        q_head = _head_of_col(ATTN_BLOCK)
        kv_head = _head_of_col(2 * ATTN_BLOCK)
        kwin = jnp.concatenate([kp_ref[...], kc_ref[...]], axis=0)
        vwin = jnp.concatenate([vp_ref[...], vc_ref[...]], axis=0)
        for b in range(qb):
            rows = slice(b * ATTN_BLOCK, (b + 1) * ATTN_BLOCK)
            window = slice(b * ATTN_BLOCK, (b + 2) * ATTN_BLOCK)
            variant = jnp.minimum(n, 1) if b == 0 else 1
            qv = q_ref[rows, :]
            kk = kwin[window]
            vv = vwin[window]
            o_acc = jnp.zeros((ATTN_BLOCK, GROUP_WIDTH), F32)
            lse_acc = jnp.zeros((ATTN_BLOCK, GROUP_WIDTH), F32)
            for hh in range(HEADS_PER_GROUP):
                hm = q_head == hh
                qh = jnp.where(hm, qv, jnp.zeros_like(qv))
                logits = _dot_nt(qh, kk) + bias_ref[variant, hh]
                m = jnp.max(logits, axis=1, keepdims=True)
                p = jnp.exp(logits - m)
                vh = jnp.where(kv_head == hh, vv, jnp.ones_like(vv))
                pv = _dot(p.astype(BF16), vh)
                c_sum = ((hh + 1) % HEADS_PER_GROUP) * HEAD_DIM
                den = pv[:, c_sum:c_sum + 1]
                o_acc = jnp.where(hm, pv * (1.0 / den), o_acc)
                lse_acc = jnp.where(hm, m + jnp.log(den), lse_acc)
            o_ref[rows, :] = o_acc
            lse_ref[rows, :] = lse_acc

    cur, prev, full = _attn_specs(qb)
    return pl.pallas_call(
        body, name=name, grid=(d, nb // qb),
        in_specs=[cur, prev, cur, prev, cur, full],
        out_specs=[cur, cur],
        out_shape=[jax.ShapeDtypeStruct((d, M, GROUP_WIDTH), F32)] * 2,
        compiler_params=_params(),
    )(q, k, k, v, v, bias)


def _attn_bwd(q, k, v, do, lse, delta, bias, name):
    d, M, _ = q.shape
    nb = M // ATTN_BLOCK
    qb = min(ATTN_QB, nb)
    ns = nb // qb
    rows_q = qb * ATTN_BLOCK
    last = slice(rows_q - ATTN_BLOCK, rows_q)

    def body(q_ref, kp_ref, kc_ref, vp_ref, vc_ref, do_ref, lse_ref, dl_ref, bias_ref,
             dq_ref, dk_ref, dv_ref, dsum_ref, pk_ref, pv_ref, wk_ref, wv_ref):
        r = pl.program_id(0)
        n = pl.program_id(1)

        @pl.when((r == 0) & (n == 0))
        def _():
            dsum_ref[...] = jnp.zeros_like(dsum_ref)

        @pl.when(n == 0)
        def _():
            pk_ref[...] = jnp.zeros_like(pk_ref)
            pv_ref[...] = jnp.zeros_like(pv_ref)

        @pl.when(n < ns)
        def _():
            q_head = _head_of_col(ATTN_BLOCK)
            kwin = jnp.concatenate([kp_ref[...], kc_ref[...]], axis=0)
            vwin = jnp.concatenate([vp_ref[...], vc_ref[...]], axis=0)
            wk_ref[...] = jnp.zeros_like(wk_ref)
            wv_ref[...] = jnp.zeros_like(wv_ref)
            for b in range(qb):
                rows = slice(b * ATTN_BLOCK, (b + 1) * ATTN_BLOCK)
                window = slice(b * ATTN_BLOCK, (b + 2) * ATTN_BLOCK)
                variant = jnp.minimum(n, 1) if b == 0 else 1
                qv = q_ref[rows, :]
                dov = do_ref[rows, :]
                kk = kwin[window]
                vv = vwin[window]
                dq_acc = jnp.zeros((ATTN_BLOCK, GROUP_WIDTH), F32)
                dkk = jnp.zeros((2 * ATTN_BLOCK, GROUP_WIDTH), F32)
                dvv = jnp.zeros((2 * ATTN_BLOCK, GROUP_WIDTH), F32)
                for hh in range(HEADS_PER_GROUP):
                    hm = q_head == hh
                    c0 = hh * HEAD_DIM
                    qh = jnp.where(hm, qv, jnp.zeros_like(qv))
                    doh = jnp.where(hm, dov, jnp.zeros_like(dov))
                    logits = _dot_nt(qh, kk) + bias_ref[variant, hh]
                    p = jnp.exp(logits - lse_ref[rows, c0:c0 + 1])
                    dp = _dot_nt(doh, vv)
                    ds = p * (dp - dl_ref[rows, c0:c0 + 1])
                    dsum_ref[hh] += ds
                    ds16 = ds.astype(BF16)
                    dq_acc = jnp.where(hm, _dot(ds16, kk), dq_acc)
                    dkk = dkk + _dot_tn(ds16, qh)
                    dvv = dvv + _dot_tn(p.astype(BF16), doh)
                dq_ref[rows, :] = (dq_acc * Q_SCALE).astype(BF16)
                wk_ref[window, :] += dkk
                wv_ref[window, :] += dvv
            for out_ref, part_ref, win_ref in ((dk_ref, pk_ref, wk_ref), (dv_ref, pv_ref, wv_ref)):
                if qb > 1:
                    out_ref[0:rows_q - ATTN_BLOCK, :] = part_ref[0:rows_q - ATTN_BLOCK, :].astype(BF16)
                out_ref[last, :] = (part_ref[last, :] + win_ref[0:ATTN_BLOCK, :]).astype(BF16)
                part_ref[...] = win_ref[ATTN_BLOCK:, :]

        @pl.when(n == ns)
        def _():
            dk_ref[...] = pk_ref[...].astype(BF16)
            dv_ref[...] = pv_ref[...].astype(BF16)

    def clamp(n):
        return jnp.minimum(n, ns - 1)

    cur = pl.BlockSpec((None, rows_q, GROUP_WIDTH), lambda r, n: (r, clamp(n), 0))
    prev = pl.BlockSpec((None, ATTN_BLOCK, GROUP_WIDTH), lambda r, n: (r, jnp.maximum(clamp(n) * qb - 1, 0), 0))
    lag = pl.BlockSpec((None, rows_q, GROUP_WIDTH), lambda r, n: (r, jnp.maximum(n - 1, 0), 0))
    full = pl.BlockSpec((2, HEADS_PER_GROUP, ATTN_BLOCK, 2 * ATTN_BLOCK), lambda r, n: (0, 0, 0, 0))
    acc = pl.BlockSpec((HEADS_PER_GROUP, ATTN_BLOCK, 2 * ATTN_BLOCK), lambda r, n: (0, 0, 0))
    return pl.pallas_call(
        body, name=name, grid=(d, ns + 1),
        in_specs=[cur, prev, cur, prev, cur, cur, cur, cur, full],
        out_specs=[cur, lag, lag, acc],
        out_shape=[jax.ShapeDtypeStruct((d, M, GROUP_WIDTH), BF16)] * 3
        + [jax.ShapeDtypeStruct((HEADS_PER_GROUP, ATTN_BLOCK, 2 * ATTN_BLOCK), F32)],
        scratch_shapes=[pltpu.VMEM((rows_q, GROUP_WIDTH), F32), pltpu.VMEM((rows_q, GROUP_WIDTH), F32),
                        pltpu.VMEM((rows_q + ATTN_BLOCK, GROUP_WIDTH), F32),
                        pltpu.VMEM((rows_q + ATTN_BLOCK, GROUP_WIDTH), F32)],
        compiler_params=_params(),
    )(q, k, k, v, v, do, lse, delta, bias)


def _disc_math(a_re, a_im, ldt, b_re, b_im):
    dt = jnp.exp(ldt)
    mag = jnp.exp(a_re * dt)
    ab_re = mag * jnp.cos(a_im * dt)
    ab_im = mag * jnp.sin(a_im * dt)
    den = a_re * a_re + a_im * a_im
    xr = ab_re - 1.0
    coef_re = (xr * a_re + ab_im * a_im) / den
    coef_im = (ab_im * a_re - xr * a_im) / den
    return ab_re, ab_im, coef_re * b_re - coef_im * b_im, coef_re * b_im + coef_im * b_re


def _block_diag_mask():
    row_g = lax.broadcasted_iota(jnp.int32, (SSM_WIDTH, 2 * NS), 0) // SSM_GROUP
    col = lax.broadcasted_iota(jnp.int32, (SSM_WIDTH, 2 * NS), 1)
    col_g = jnp.where(col >= NS, col - NS, col) // SSM_STATE
    return row_g == col_g


def _disc_fwd(a_re, a_im, ldt, b_re, b_im, c_re, c_im):
    def body(are_ref, aim_ref, ldt_ref, bre_ref, bim_ref, cre_ref, cim_ref, pw_ref, pwr_ref, bd_ref, cdt_ref):
        ab_re, ab_im, bb_re, bb_im = _disc_math(are_ref[...], aim_ref[...], ldt_ref[...], bre_ref[...], bim_ref[...])
        row = lax.broadcasted_iota(jnp.int32, (8, NS), 0)
        pr, pi = ab_re, ab_im
        t_re = jnp.zeros((8, NS), F32)
        t_im = jnp.zeros((8, NS), F32)
        u_re = jnp.zeros((8, NS), F32)
        u_im = jnp.zeros((8, NS), F32)
        for j in range(8):
            t_re = jnp.where(row == j, pr, t_re)
            t_im = jnp.where(row == j, pi, t_im)
            u_re = jnp.where(row == 7 - j, pr, u_re)
            u_im = jnp.where(row == 7 - j, pi, u_im)
            pr, pi = pr * ab_re - pi * ab_im, pr * ab_im + pi * ab_re
        pw_ref[0] = t_re
        pw_ref[1] = t_im
        pwr_ref[0] = u_re
        pwr_ref[1] = u_im
        mask = _block_diag_mask()
        zero = jnp.zeros((SSM_WIDTH, 2 * NS), F32)
        bfull = jnp.concatenate([jnp.concatenate([bb_re] * SSM_GROUPS, axis=0),
                                 jnp.concatenate([bb_im] * SSM_GROUPS, axis=0)], axis=1)
        bd_ref[...] = jnp.where(mask, bfull, zero).astype(BF16)
        cfull = jnp.concatenate([jnp.concatenate([cre_ref[...]] * SSM_GROUPS, axis=0),
                                 jnp.concatenate([-cim_ref[...]] * SSM_GROUPS, axis=0)], axis=1)
        cdt_ref[...] = jnp.where(mask, cfull, zero).astype(BF16)

    return pl.pallas_call(
        body, name="s5_disc_fwd",
        in_specs=[_whole()] * 7, out_specs=[_whole()] * 4,
        out_shape=[jax.ShapeDtypeStruct((2, 8, NS), F32), jax.ShapeDtypeStruct((2, 8, NS), F32),
                   jax.ShapeDtypeStruct((SSM_WIDTH, 2 * NS), BF16), jax.ShapeDtypeStruct((SSM_WIDTH, 2 * NS), BF16)],
        compiler_params=_params(),
    )(a_re, a_im, ldt, b_re, b_im, c_re, c_im)


def _disc_bwd(a_re, a_im, ldt, b_re, b_im, d_bd, d_cdt, d_ab, group_sum):
    def body(are_ref, aim_ref, ldt_ref, bre_ref, bim_ref, dbd_ref, dcdt_ref, dab_ref, gs_ref,
             dare_ref, daim_ref, dldt_ref, dbre_ref, dbim_ref, dcre_ref, dcim_ref):
        col = lax.broadcasted_iota(jnp.int32, (SSM_GROUP, 2 * NS), 1)
        col_g = jnp.where(col >= NS, col - NS, col) // SSM_STATE
        acc_b = jnp.zeros((SSM_GROUP, 2 * NS), F32)
        acc_c = jnp.zeros((SSM_GROUP, 2 * NS), F32)
        for g in range(SSM_GROUPS):
            rows = slice(g * SSM_GROUP, (g + 1) * SSM_GROUP)
            acc_b = acc_b + jnp.where(col_g == g, dbd_ref[rows, :], 0.0)
            acc_c = acc_c + jnp.where(col_g == g, dcdt_ref[rows, :], 0.0)
        dcre_ref[...] = acc_c[:, :NS]
        dcim_ref[...] = -acc_c[:, NS:]
        dab_re = jnp.sum(dab_ref[0], axis=0, keepdims=True)
        dab_im = jnp.sum(dab_ref[1], axis=0, keepdims=True)
        _, vjp = jax.vjp(_disc_math, are_ref[...], aim_ref[...], ldt_ref[...], bre_ref[...], bim_ref[...])
        d_are, d_aim, d_ldt, d_bre, d_bim = vjp((dab_re, dab_im, acc_b[:, :NS], acc_b[:, NS:]))
        dare_ref[...] = d_are
        daim_ref[...] = d_aim
        dbre_ref[...] = d_bre
        dbim_ref[...] = d_bim
        dldt_ref[...] = _dot_exact(jnp.broadcast_to(d_ldt, (8, NS)), gs_ref[...])

    vec = jax.ShapeDtypeStruct((1, NS), F32)
    mat = jax.ShapeDtypeStruct((SSM_GROUP, NS), F32)
    return pl.pallas_call(
        body, name="s5_disc_bwd",
        in_specs=[_whole()] * 9, out_specs=[_whole()] * 7,
        out_shape=[vec, vec, jax.ShapeDtypeStruct((8, 128), F32), mat, mat, mat, mat],
        compiler_params=_params(),
    )(a_re, a_im, ldt, b_re, b_im, d_bd, d_cdt, d_ab, group_sum)


def _scan_blocks(buf, pw_ref, carry_ref, n_blocks, reverse):
    row = lax.broadcasted_iota(jnp.int32, (8, SCAN_LANES), 0)
    for lc in range(NS // SCAN_LANES):
        re_cols = pl.ds(lc * SCAN_LANES, SCAN_LANES)
        im_cols = pl.ds(NS + lc * SCAN_LANES, SCAN_LANES)
        pr = pw_ref[0, :, re_cols]
        pi = pw_ref[1, :, re_cols]
        if reverse:
            pi = -pi
            base = [(7, 1), (6, 2), (4, 4)]
            coef = [(jnp.where(row < 8 - k, pr[j:j + 1], 0.0), jnp.where(row < 8 - k, pi[j:j + 1], 0.0), 8 - k)
                    for j, k in base]
        else:
            base = [(0, 1), (1, 2), (3, 4)]
            coef = [(jnp.where(row >= k, pr[j:j + 1], 0.0), jnp.where(row >= k, pi[j:j + 1], 0.0), k)
                    for j, k in base]

        def step(i, carry, pr=pr, pi=pi, coef=coef, re_cols=re_cols, im_cols=im_cols):
            cr, ci = carry
            blk = (n_blocks - 1 - i) if reverse else i
            rows = pl.ds(pl.multiple_of(blk * 8, 8), 8)
            xr = buf[rows, re_cols]
            xi = buf[rows, im_cols]
            for kr, ki, shift in coef:
                sr = pltpu.roll(xr, shift, 0)
                si = pltpu.roll(xi, shift, 0)
                xr, xi = xr + kr * sr - ki * si, xi + kr * si + ki * sr
            xr, xi = xr + pr * cr - pi * ci, xi + pr * ci + pi * cr
            buf[rows, re_cols] = xr
            buf[rows, im_cols] = xi
            edge = slice(0, 1) if reverse else slice(7, 8)
            return xr[edge], xi[edge]

        cr, ci = lax.fori_loop(0, n_blocks, step, (carry_ref[0:1, re_cols], carry_ref[0:1, im_cols]))
        carry_ref[0:1, re_cols] = cr
        carry_ref[0:1, im_cols] = ci


_SUPER_GROUPS = 16
_SUPER_BLOCKS = [
    (slice(k * _SUPER_GROUPS * SSM_GROUP, (k + 1) * _SUPER_GROUPS * SSM_GROUP),
     [slice(half + k * _SUPER_GROUPS * SSM_STATE, half + (k + 1) * _SUPER_GROUPS * SSM_STATE) for half in (0, NS)])
    for k in range(SSM_GROUPS // _SUPER_GROUPS)]


def _ssm_fwd(u, bd, cdt, d_skip, pw):
    L = u.shape[0]
    tc = min(SSM_CHUNK, L)

    def body(u_ref, bd_ref, cdt_ref, dsk_ref, pw_ref, y_ref, s_ref, carry_ref):
        @pl.when(pl.program_id(0) == 0)
        def _():
            carry_ref[...] = jnp.zeros_like(carry_ref)

        uv = u_ref[...]
        u16 = uv.astype(BF16)
        for ch, states in _SUPER_BLOCKS:
            for st in states:
                s_ref[:, st] = _dot(u16[:, ch], bd_ref[ch, st])
        _scan_blocks(s_ref, pw_ref, carry_ref, tc // 8, reverse=False)
        for ch, states in _SUPER_BLOCKS:
            y_ref[:, ch] = (sum(_dot_nt(s_ref[:, st].astype(BF16), cdt_ref[ch, st]) for st in states)
                            + dsk_ref[:, ch] * uv[:, ch])

    return pl.pallas_call(
        body, name="s5_fwd", grid=(L // tc,),
        in_specs=[_rows(tc, SSM_WIDTH), _whole(), _whole(), _whole(), _whole()],
        out_specs=[_rows(tc, SSM_WIDTH), _rows(tc, 2 * NS)],
        out_shape=[jax.ShapeDtypeStruct((L, SSM_WIDTH), F32), jax.ShapeDtypeStruct((L, 2 * NS), F32)],
        scratch_shapes=[pltpu.VMEM((8, 2 * NS), F32)],
        compiler_params=_params(),
    )(u, bd, cdt, d_skip, pw)


def _ssm_bwd(dy, u, s, bd, cdt, d_skip, pwr):
    L = u.shape[0]
    tc = min(SSM_CHUNK, L)
    nc = L // tc
    blocks = tc // 8

    def body(dy_ref, u_ref, s_ref, sprev_ref, bd_ref, cdt_ref, dsk_ref, pwr_ref,
             du_ref, ddsk_ref, dbd_ref, dcdt_ref, dab_ref, g_ref, sx_ref, carry_ref):
        i = pl.program_id(0)

        @pl.when(i == 0)
        def _():
            carry_ref[...] = jnp.zeros_like(carry_ref)
            ddsk_ref[...] = jnp.zeros_like(ddsk_ref)
            dbd_ref[...] = jnp.zeros_like(dbd_ref)
            dcdt_ref[...] = jnp.zeros_like(dcdt_ref)
            dab_ref[...] = jnp.zeros_like(dab_ref)

        dyv = dy_ref[...]
        uv = u_ref[...]
        dy16 = dyv.astype(BF16)
        u16 = uv.astype(BF16)
        for ch, states in _SUPER_BLOCKS:
            for st in states:
                g_ref[:, st] = _dot(dy16[:, ch], cdt_ref[ch, st])
        _scan_blocks(g_ref, pwr_ref, carry_ref, blocks, reverse=True)
        ddsk_ref[...] += jnp.sum(dyv * uv, axis=0, keepdims=True)
        for ch, states in _SUPER_BLOCKS:
            du = dsk_ref[:, ch] * dyv[:, ch]
            for st in states:
                g16 = g_ref[:, st].astype(BF16)
                du = du + _dot_nt(g16, bd_ref[ch, st])
                dbd_ref[ch, st] += _dot_tn(u16[:, ch], g16)
                dcdt_ref[ch, st] += _dot_tn(dy16[:, ch], s_ref[:, st].astype(BF16))
            du_ref[:, ch] = du

        sx_ref[pl.ds(8, tc), :] = s_ref[...]
        sx_ref[pl.ds(0, 8), :] = jnp.where(i == nc - 1, 0.0, sprev_ref[...])
        row = lax.broadcasted_iota(jnp.int32, (8, SCAN_LANES), 0)
        for lc in range(NS // SCAN_LANES):
            re_cols = pl.ds(lc * SCAN_LANES, SCAN_LANES)
            im_cols = pl.ds(NS + lc * SCAN_LANES, SCAN_LANES)

            def step(b, acc, re_cols=re_cols, im_cols=im_cols):
                ar, ai = acc
                off = pl.multiple_of(b * 8, 8)
                gr = g_ref[pl.ds(off, 8), re_cols]
                gi = g_ref[pl.ds(off, 8), im_cols]
                before = pl.ds(off, 8)
                here = pl.ds(off + 8, 8)
                sr = jnp.where(row == 0, sx_ref[before, re_cols][7:8], pltpu.roll(sx_ref[here, re_cols], 1, 0))
                si = jnp.where(row == 0, sx_ref[before, im_cols][7:8], pltpu.roll(sx_ref[here, im_cols], 1, 0))
                return ar + gr * sr + gi * si, ai + gi * sr - gr * si

            zero = jnp.zeros((8, SCAN_LANES), F32)
            ar, ai = lax.fori_loop(0, blocks, step, (zero, zero))
            dab_ref[0, :, re_cols] += ar
            dab_ref[1, :, re_cols] += ai

    rev = lambda i: (nc - 1 - i, 0)
    sprev = pl.BlockSpec((8, 2 * NS), lambda i: (jnp.maximum((nc - 1 - i) * blocks - 1, 0), 0))
    return pl.pallas_call(
        body, name="s5_bwd", grid=(nc,),
        in_specs=[pl.BlockSpec((tc, SSM_WIDTH), rev), pl.BlockSpec((tc, SSM_WIDTH), rev),
                  pl.BlockSpec((tc, 2 * NS), rev), sprev, _whole(), _whole(), _whole(), _whole()],
        out_specs=[pl.BlockSpec((tc, SSM_WIDTH), rev), _whole(), _whole(), _whole(), _whole()],
        out_shape=[jax.ShapeDtypeStruct((L, SSM_WIDTH), F32), jax.ShapeDtypeStruct((1, SSM_WIDTH), F32),
                   jax.ShapeDtypeStruct((SSM_WIDTH, 2 * NS), F32), jax.ShapeDtypeStruct((SSM_WIDTH, 2 * NS), F32),
                   jax.ShapeDtypeStruct((2, 8, NS), F32)],
        scratch_shapes=[pltpu.VMEM((tc, 2 * NS), F32), pltpu.VMEM((tc + 8, 2 * NS), F32), pltpu.VMEM((8, 2 * NS), F32)],
        compiler_params=_params(),
    )(dy, u, s, s, bd, cdt, d_skip, pwr)


def _branches(o_attn, y, gates, w_ab, w_glu, w_sb):
    ya = _dot(o_attn.astype(BF16), w_ab[...])
    gel = _gelu(y)
    glu = _dot(gel.astype(BF16), w_glu[...])
    p = glu[:, :SSM_WIDTH]
    sg = _sigmoid(glu[:, SSM_WIDTH:])
    ys2 = p * sg
    ysb = _dot(ys2.astype(BF16), w_sb[...])
    ga = gates[:, :D_MODEL]
    gs = gates[:, D_MODEL:]
    return ya, gel, p, sg, ys2, ysb, ga, gs


def _mix_out_fwd(x1, o_g, lse_g, y, gates, w_ab, w_glu, w_sb, w_out):
    L = x1.shape[0]
    tm = min(ROW_TILE, L)

    def body(x_ref, o0, o1, o2, l0, l1, l2, y_ref, gate_ref, wab_ref, wglu_ref, wsb_ref, wout_ref,
             x2_ref, oat_ref, lse0, lse1, lse2, scr):
        la, lb, lc = (_from_residues(ref, scr, d) for ref, d in zip((l0, l1, l2), DILATIONS))
        m = jnp.maximum(jnp.maximum(la, lb), lc)
        ea, eb, ec = jnp.exp(la - m), jnp.exp(lb - m), jnp.exp(lc - m)
        tot = ea + eb + ec
        oa, ob, oc = (_from_residues(ref, scr, d) for ref, d in zip((o0, o1, o2), DILATIONS))
        o_attn = (ea * oa + eb * ob + ec * oc) / tot
        oat_ref[...] = o_attn
        lse = m + jnp.log(tot)
        for ref, d in zip((lse0, lse1, lse2), DILATIONS):
            _to_residues(lse, ref, scr, d)
        ya, _, _, _, _, ysb, ga, gs = _branches(o_attn, y_ref[...], gate_ref[...], wab_ref, wglu_ref, wsb_ref)
        mix = ga * ya + gs * ysb
        x2_ref[...] = x_ref[...] + _dot(mix.astype(BF16), wout_ref[...])

    res = [_residue_spec(d, tm) for d in DILATIONS]
    return pl.pallas_call(
        body, name="mix_out_fwd", grid=(L // tm,),
        in_specs=[_rows(tm, D_MODEL)] + res * 2 + [_rows(tm, SSM_WIDTH), _rows(tm, 2 * D_MODEL)] + [_whole()] * 4,
        out_specs=[_rows(tm, D_MODEL), _rows(tm, GROUP_WIDTH)] + res,
        out_shape=[jax.ShapeDtypeStruct((L, D_MODEL), F32), jax.ShapeDtypeStruct((L, GROUP_WIDTH), F32)]
        + [_residue_shape(d, L, F32) for d in DILATIONS],
        scratch_shapes=[_residue_scratch(tm)],
        compiler_params=_params(),
    )(x1, *o_g, *lse_g, y, gates, w_ab, w_glu, w_sb, w_out)


def _mix_out_bwd(dx2, o_attn, y, gates, w_ab, w_glu, w_sb, w_out, head_sum):
    L = dx2.shape[0]
    tm = min(ROW_TILE, L)

    def body(dx_ref, oat_ref, y_ref, gate_ref, wab_ref, wglu_ref, wsb_ref, wout_ref, hs_ref,
             do0, do1, do2, dl0, dl1, dl2, dy_ref, dgp_ref, mix_ref, dya_ref, dys_ref, ys2_ref, gel_ref, dglu_ref,
             dgb_ref, scr):
        i = pl.program_id(0)
        o_attn = oat_ref[...]
        yv = y_ref[...]
        ya, gel, p, sg, ys2, ysb, ga, gs = _branches(o_attn, yv, gate_ref[...], wab_ref, wglu_ref, wsb_ref)
        mix_ref[...] = (ga * ya + gs * ysb).astype(BF16)
        ys2_ref[...] = ys2.astype(BF16)
        gel_ref[...] = gel.astype(BF16)
        dmix = _dot_nt(dx_ref[...].astype(BF16), wout_ref[...])
        dgp = jnp.concatenate([dmix * ya * ga * (1.0 - ga), dmix * ysb * gs * (1.0 - gs)], axis=1)
        dgp_ref[...] = dgp.astype(BF16)

        @pl.when(i == 0)
        def _():
            dgb_ref[...] = jnp.zeros_like(dgb_ref)

        dgb_ref[...] += jnp.sum(dgp, axis=0, keepdims=True)
        dya = (dmix * ga).astype(BF16)
        dys = (dmix * gs).astype(BF16)
        dya_ref[...] = dya
        dys_ref[...] = dys
        d_o = _dot_nt(dya, wab_ref[...])
        delta = _dot_exact(d_o * o_attn, hs_ref[...])
        for do_ref, dl_ref, d in zip((do0, do1, do2), (dl0, dl1, dl2), DILATIONS):
            _to_residues(d_o, do_ref, scr, d)
            _to_residues(delta, dl_ref, scr, d)
        dys2 = _dot_nt(dys, wsb_ref[...])
        dglu = jnp.concatenate([dys2 * sg, dys2 * p * sg * (1.0 - sg)], axis=1).astype(BF16)
        dglu_ref[...] = dglu
        dy_ref[...] = _dot_nt(dglu, wglu_ref[...]) * _gelu_grad(yv)

    grp = _rows(tm, GROUP_WIDTH)
    wide = _rows(tm, D_MODEL)
    half = _rows(tm, SSM_WIDTH)
    res = [_residue_spec(d, tm) for d in DILATIONS]
    sds = jax.ShapeDtypeStruct
    return pl.pallas_call(
        body, name="mix_out_bwd", grid=(L // tm,),
        in_specs=[wide, grp, half, _rows(tm, 2 * D_MODEL)] + [_whole()] * 5,
        out_specs=res + res + [half, _rows(tm, 2 * D_MODEL), wide, wide, wide, half, half, wide, _acc_row(2 * D_MODEL)],
        out_shape=[_residue_shape(d, L, BF16) for d in DILATIONS] + [_residue_shape(d, L, F32) for d in DILATIONS]
        + [sds((L, SSM_WIDTH), F32),
           sds((L, 2 * D_MODEL), BF16), sds((L, D_MODEL), BF16), sds((L, D_MODEL), BF16),
           sds((L, D_MODEL), BF16), sds((L, SSM_WIDTH), BF16), sds((L, SSM_WIDTH), BF16),
           sds((L, D_MODEL), BF16), sds((1, 2 * D_MODEL), F32)],
        scratch_shapes=[_residue_scratch(tm)],
        compiler_params=_params(),
    )(dx2, o_attn, y, gates, w_ab, w_glu, w_sb, w_out, head_sum)


def _adamw(w, g, m, v, name):
    R, C = w.shape
    tr = _row_tile(R, max(8, ADAMW_BLOCK_BYTES // (4 * C)))

    def body(w_ref, g_ref, m_ref, v_ref, d_ref, mo_ref, vo_ref):
        gv = g_ref[...]
        mn = ADAM_B1 * m_ref[...] + (1.0 - ADAM_B1) * gv
        vn = ADAM_B2 * v_ref[...] + (1.0 - ADAM_B2) * (gv * gv)
        m_hat = mn / (1.0 - ADAM_B1 ** ADAM_STEP)
        v_hat = vn / (1.0 - ADAM_B2 ** ADAM_STEP)
        d_ref[...] = -ADAM_LR * (m_hat / (jnp.sqrt(v_hat) + ADAM_EPS) + ADAM_WD * w_ref[...])
        mo_ref[...] = mn
        vo_ref[...] = vn

    blk = pl.BlockSpec((tr, C), lambda i: (i, 0))
    return pl.pallas_call(
        body, name=name, grid=(R // tr,),
        in_specs=[blk] * 4, out_specs=[blk] * 3,
        out_shape=[jax.ShapeDtypeStruct((R, C), F32)] * 3,
        compiler_params=_params(),
    )(w, g, m, v)


def _sum_chips_into_half(u, t, name):
    S, H, C = u.shape
    tr = _row_tile(H, 512)
    hb = H // tr

    def body(s_ref, t_ref, a_ref, b_ref, c_ref, o_ref):
        me = s_ref[1]
        others = (a_ref[...], b_ref[...], c_ref[...])
        acc = None
        for chip in range(S):
            below = others[min(chip, S - 2)]
            above = others[max(chip - 1, 0)]
            term = jnp.where(me == chip, t_ref[...], jnp.where(me > chip, below, above)).astype(F32)
            acc = term if acc is None else acc + term
        o_ref[...] = acc

    x, y, c = lax.axis_index("x"), lax.axis_index("y"), lax.axis_index("c")
    me = 2 * x + y
    scalars = jnp.stack([c, me] + [j + (j >= me).astype(jnp.int32) for j in range(S - 1)]).astype(jnp.int32)
    blk = (None, tr, C)
    return pl.pallas_call(
        body, name=name,
        grid_spec=pltpu.PrefetchScalarGridSpec(
            num_scalar_prefetch=1, grid=(hb,),
            in_specs=[pl.BlockSpec(blk, lambda i, s: (s[1], i, 0))]
            + [pl.BlockSpec(blk, functools.partial(lambda j, i, s: (s[2 + j], i, 0), j)) for j in range(S - 1)],
            out_specs=pl.BlockSpec((tr, C), lambda i, s: (s[0] * hb + i, 0))),
        out_shape=jax.ShapeDtypeStruct((2 * H, C), F32),
        compiler_params=_params(),
    )(scalars, t, u, u, u)


def _add_halves(g, r1, name):
    S, R, C = g.shape
    H = R // 2
    tr = _row_tile(H, 512)
    hb = H // tr

    def body(c_ref, g_ref, r_ref, o_ref):
        o_ref[...] = (g_ref[...] + r_ref[...]).astype(BF16)

    core = lax.axis_index("c").astype(jnp.int32).reshape(1)
    return pl.pallas_call(
        body, name=name,
        grid_spec=pltpu.PrefetchScalarGridSpec(
            num_scalar_prefetch=1, grid=(S, hb),
            in_specs=[pl.BlockSpec((None, tr, C), lambda j, i, c_ref: (j, c_ref[0] * hb + i, 0)),
                      pl.BlockSpec((None, tr, C), lambda j, i, c_ref: (j, i, 0))],
            out_specs=pl.BlockSpec((None, tr, C), lambda j, i, c_ref: (j, i, 0))),
        out_shape=jax.ShapeDtypeStruct((S, H, C), BF16),
        compiler_params=_params(),
    )(core, g, r1)


_ANY = pl.BlockSpec(memory_space=pl.ANY)


def _place():
    x, y, c = lax.axis_index("x"), lax.axis_index("y"), lax.axis_index("c")
    chips = [(1 - x, y), (x, 1 - y), (1 - x, 1 - y)]
    return x, y, c, chips


def _comm_call(body, name, ins, out_shapes, n_remote, n_local):
    return pl.pallas_call(
        body, name=name,
        in_specs=[_ANY] * len(ins), out_specs=[_ANY] * len(out_shapes), out_shape=out_shapes,
        scratch_shapes=[pltpu.SemaphoreType.DMA((n_remote,)), pltpu.SemaphoreType.DMA((n_remote,)),
                        pltpu.SemaphoreType.DMA((max(n_local, 1),))],
    )(*ins)


def _remote(src, dst, send_sems, recv_sems, k, device):
    return pltpu.make_async_remote_copy(src_ref=src, dst_ref=dst, send_sem=send_sems.at[k], recv_sem=recv_sems.at[k],
                                        device_id=device, device_id_type=MESH)


def _gather_body(shapes, handshake):
    n = len(shapes)

    def body(*refs):
        w_refs, out_refs = refs[:n], refs[n:2 * n]
        send_sems, recv_sems = refs[2 * n:2 * n + 2]
        x, y, c, chips = _place()
        me = 2 * x + y
        sibling = (x, y, 1 - c)
        if handshake:
            _handshake([sibling] + [(cx, cy, c) for cx, cy in chips])

        def half(k, chip_idx, core):
            H = shapes[k][0] // 2
            return out_refs[k].at[chip_idx, pl.ds(core * H, H), :]

        mine = [_remote(w_refs[k], out_refs[k].at[me], send_sems, recv_sems, 6 * n + k, sibling) for k in range(n)]
        for cp in mine:
            cp.start()
        first = []
        for k in range(n):
            H = shapes[k][0] // 2
            for j, (cx, cy) in enumerate(chips):
                first.append(_remote(w_refs[k].at[pl.ds(c * H, H), :], half(k, me, c), send_sems, recv_sems,
                                     3 * k + j, (cx, cy, c)))
        for cp in first:
            cp.start()
        passed = []
        for k in range(n):
            for j, (cx, cy) in enumerate(chips):
                landed = half(k, 2 * cx + cy, c)
                _remote(landed, landed, send_sems, recv_sems, 3 * k + j, (cx, cy, c)).wait_recv()
                fwd = _remote(landed, landed, send_sems, recv_sems, 3 * n + 3 * k + j, sibling)
                fwd.start()
                passed.append(fwd)
        for k in range(n):
            for j, (cx, cy) in enumerate(chips):
                other = half(k, 2 * cx + cy, 1 - c)
                _remote(other, other, send_sems, recv_sems, 3 * n + 3 * k + j, sibling).wait_recv()
        for cp in mine:
            cp.wait_recv()
        for cp in first + passed + mine:
            cp.wait_send()

    return body


def _gather_weights(shards, name):
    n = len(shards)
    return _comm_call(_gather_body([w.shape for w in shards], handshake=False), name, shards,
                      [jax.ShapeDtypeStruct((N_SHARD,) + w.shape, w.dtype) for w in shards], 7 * n, 0)


def _gather_weights_behind(shards, name, collective_id):
    return _sequenced(_gather_body([w.shape for w in shards], handshake=True), name, shards,
                      [jax.ShapeDtypeStruct((N_SHARD,) + w.shape, w.dtype) for w in shards], 7 * len(shards),
                      collective_id)


def _handshake(peers):
    barrier = pltpu.get_barrier_semaphore()
    for peer in peers:
        pl.semaphore_signal(barrier, inc=1, device_id=peer, device_id_type=MESH)
    pl.semaphore_wait(barrier, len(peers))


def _sequenced(body, name, ins, out_shapes, n_sems, collective_id):
    return pl.kernel(
        body, out_type=list(out_shapes), mesh=plsc.ScalarSubcoreMesh(axis_name="sequencer", num_cores=1), name=name,
        scratch_types=(pltpu.SemaphoreType.DMA((n_sems,)), pltpu.SemaphoreType.DMA((n_sems,))),
        compiler_params=pltpu.CompilerParams(collective_id=collective_id))(*ins)


def _swap_halves(gs, name, collective_id):
    n = len(gs)

    def body(*refs):
        g_refs, out_refs = refs[:n], refs[n:2 * n]
        send_sems, recv_sems = refs[2 * n:]
        x, y, c, _ = _place()
        _handshake([(x, y, 1 - c)])
        cps = []
        for k in range(n):
            H = gs[k].shape[1] // 2
            cp = _remote(g_refs[k].at[:, pl.ds((1 - c) * H, H), :], out_refs[k], send_sems, recv_sems, k, (x, y, 1 - c))
            cp.start()
            cps.append(cp)
        for cp in cps:
            cp.wait()

    return _sequenced(body, name, gs, [jax.ShapeDtypeStruct((g.shape[0], g.shape[1] // 2, g.shape[2]), g.dtype)
                                       for g in gs], n, collective_id)


def _exchange_chips(ts, name, collective_id):
    n = len(ts)

    def body(*refs):
        t_refs, out_refs = refs[:n], refs[n:2 * n]
        send_sems, recv_sems = refs[2 * n:]
        x, y, c, chips = _place()
        me = 2 * x + y
        _handshake([(cx, cy, c) for cx, cy in chips])
        sent = []
        for k in range(n):
            for j, (cx, cy) in enumerate(chips):
                cp = _remote(t_refs[k].at[2 * cx + cy], out_refs[k].at[me], send_sems, recv_sems, 3 * k + j, (cx, cy, c))
                cp.start()
                sent.append(cp)
        for k in range(n):
            for j, (cx, cy) in enumerate(chips):
                slot = out_refs[k].at[2 * cx + cy]
                _remote(slot, slot, send_sems, recv_sems, 3 * k + j, (cx, cy, c)).wait_recv()
        for cp in sent:
            cp.wait_send()

    return _sequenced(body, name, ts, [jax.ShapeDtypeStruct(t.shape, t.dtype) for t in ts], 3 * n, collective_id)


def _join_halves(fs, name):
    n = len(fs)

    def body(*refs):
        out_refs = refs[n:2 * n]
        send_sems, recv_sems, _ = refs[2 * n:]
        x, y, c, _ = _place()
        sent = []
        for k in range(n):
            H = fs[k].shape[0] // 2
            here = out_refs[k].at[pl.ds(c * H, H), :]
            cp = _remote(here, here, send_sems, recv_sems, k, (x, y, 1 - c))
            cp.start()
            sent.append(cp)
        for k in range(n):
            H = fs[k].shape[0] // 2
            other = out_refs[k].at[pl.ds((1 - c) * H, H), :]
            _remote(other, other, send_sems, recv_sems, k, (x, y, 1 - c)).wait_recv()
        for cp in sent:
            cp.wait_send()

    return pl.pallas_call(
        body, name=name,
        in_specs=[_ANY] * n, out_specs=[_ANY] * n,
        out_shape=[jax.ShapeDtypeStruct(f.shape, f.dtype) for f in fs],
        input_output_aliases={k: k for k in range(n)},
        scratch_shapes=[pltpu.SemaphoreType.DMA((n,)), pltpu.SemaphoreType.DMA((n,)), pltpu.SemaphoreType.DMA((1,))],
    )(*fs)


def _gather_small(v):
    R, C = v.shape

    def body(v_ref, out_ref, send_sems, recv_sems):
        x, y, c, _ = _place()
        me = 4 * x + 2 * y + c
        flips = [(fx, fy, fc) for fx in (0, 1) for fy in (0, 1) for fc in (0, 1)][1:]
        peers = [((1 - x) if fx else x, (1 - y) if fy else y, (1 - c) if fc else c) for fx, fy, fc in flips]
        _handshake(peers)
        sent = []
        for j, peer in enumerate(peers):
            cp = _remote(v_ref, out_ref.at[me], send_sems, recv_sems, j, peer)
            cp.start()
            sent.append(cp)
        for j, peer in enumerate(peers):
            slot = out_ref.at[4 * peer[0] + 2 * peer[1] + peer[2]]
            _remote(slot, slot, send_sems, recv_sems, j, peer).wait_recv()
        for cp in sent:
            cp.wait_send()

    return _sequenced(body, "gather_small", [v], [jax.ShapeDtypeStruct((8, R, C), F32)], 7,
                      COLLECTIVE_IDS["gather_small"])[0]


def _sum_devices(x, own, name):
    S, R, C = x.shape
    tr = _row_tile(R, 2048)

    def body(s_ref, x_ref, own_ref, o_ref):
        me = s_ref[0]
        acc = None
        for k in range(S):
            term = jnp.where(me == k, own_ref[...], x_ref[k])
            acc = term if acc is None else acc + term
        o_ref[...] = acc

    x_, y_, c_ = lax.axis_index("x"), lax.axis_index("y"), lax.axis_index("c")
    me = (4 * x_ + 2 * y_ + c_).astype(jnp.int32).reshape(1)
    return pl.pallas_call(
        body, name=name,
        grid_spec=pltpu.PrefetchScalarGridSpec(
            num_scalar_prefetch=1, grid=(R // tr,),
            in_specs=[pl.BlockSpec((S, tr, C), lambda i, s: (0, i, 0)), pl.BlockSpec((tr, C), lambda i, s: (i, 0))],
            out_specs=pl.BlockSpec((tr, C), lambda i, s: (i, 0))),
        out_shape=jax.ShapeDtypeStruct((R, C), F32),
        compiler_params=_params(),
    )(me, x, own)


def _after(earlier, arrays):
    return lax.optimization_barrier((earlier, arrays))


def _reduce_exchange(gs, names, tag, earlier):
    earlier, gs = _after(earlier, gs)
    r1 = _swap_halves(gs, "reduce_swap_" + tag, COLLECTIVE_IDS["swap_" + tag])
    ts = [_add_halves(g, r, "reduce_add_cores_" + nm) for g, r, nm in zip(gs, r1, names)]
    us = _exchange_chips(ts, "reduce_exchange_" + tag, COLLECTIVE_IDS["exchange_" + tag])
    return us, ts, earlier


def _reduce_finish(us, ts, names, tag):
    fs = [_sum_chips_into_half(u, t, "reduce_add_chips_" + nm) for u, t, nm in zip(us, ts, names)]
    return _join_halves(fs, "reduce_join_" + tag)


BIG = ["ffn1_w_gate", "ffn1_w_up", "ffn1_w_down", "w_in", "ssm_w_glu", "w_attn_branch", "w_ssm_branch",
       "w_out", "ffn2_w_gate", "ffn2_w_up", "ffn2_w_down"]
SMALL = ["ffn1_norm", "mix_norm", "gate_bias", "rel_bias_table", "ssm_a_re", "ssm_a_im", "ssm_log_dt",
         "ssm_b_re", "ssm_b_im", "ssm_c_re", "ssm_c_im", "ssm_d", "ffn2_norm", "final_norm"]
ORDER = ["ffn1_norm", "ffn1_w_gate", "ffn1_w_up", "ffn1_w_down", "mix_norm", "w_in", "gate_bias", "rel_bias_table",
         "ssm_a_re", "ssm_a_im", "ssm_log_dt", "ssm_b_re", "ssm_b_im", "ssm_c_re", "ssm_c_im", "ssm_d",
         "ssm_w_glu", "w_attn_branch", "w_ssm_branch", "w_out", "ffn2_norm", "ffn2_w_gate", "ffn2_w_up",
         "ffn2_w_down", "final_norm"]


_SMALL_TILE = 8 * LANES


def _pack_small(arrays):
    rows = []
    for a in arrays:
        flat = a.reshape(-1).astype(F32)
        rows.append(jnp.pad(flat, (0, (-flat.shape[0]) % _SMALL_TILE)).reshape(-1, LANES))
    return jnp.concatenate(rows, axis=0)


def _unpack_small(packed, shapes):
    out, r0 = [], 0
    for shp in shapes:
        n = math.prod(shp)
        rows = 8 * -(-n // _SMALL_TILE)
        out.append(packed[r0:r0 + rows].reshape(-1)[:n].reshape(shp))
        r0 += rows
    return out


def _split_cols(g):
    K, N = g.shape
    return g.reshape(K, N_SHARD, N // N_SHARD).transpose(1, 0, 2)


def _join_cols(w):
    S, K, n = w.shape
    return w.transpose(1, 0, 2).reshape(K, S * n)


COL_SHARDED = ("w_in", "ssm_w_glu", "w_attn_branch", "w_ssm_branch")


class _GradSync:
    def __init__(self, weights, moms, vels):
        self.weights, self.moms, self.vels = weights, moms, vels
        self.grads, self.delta, self.new_m, self.new_v = {}, {}, {}, {}
        self.loss = None
        self._earlier = []
        self._exchanged = {}

    def grads_ready(self, tag, gw):
        names = REDUCE_GROUPS[tag]
        gs = []
        for n in names:
            g = gw[n]
            if n in COL_SHARDED:
                g = _split_cols(g)
            elif n == "w_out":
                g = g.reshape(N_SHARD, D_MODEL // N_SHARD, D_MODEL)
            gs.append(g)
        us, ts, _ = _reduce_exchange(gs, names, tag, self._earlier)
        self._exchanged[tag] = (us, ts)
        self._earlier = us

    def small_ready(self, gs, loss_blk):
        _, (mine,) = _after(self._earlier, [_pack_small([gs[n] for n in SMALL] + [loss_blk[0:1, :]])])
        others = _gather_small(mine)
        self._exchanged["small"] = (others, mine)
        self._earlier = [others]

    def finish(self, tag):
        made = []
        if tag == "small":
            others, mine = self._exchanged[tag]
            shapes = [self.weights[n].shape for n in SMALL]
            total = _unpack_small(_sum_devices(others, mine, "sum_small"), shapes + [(128,)])
            self.loss = total[-1][0]
            self.grads.update(zip(SMALL, total[:-1]))
            packed = [_pack_small([src[n] for n in SMALL]) for src in (self.weights, self.grads, self.moms, self.vels)]
            for dst, res in zip((self.delta, self.new_m, self.new_v), _adamw(*packed, "adamw_small")):
                dst.update(zip(SMALL, _unpack_small(res, shapes)))
            for n in SMALL:
                made += [self.grads[n], self.delta[n], self.new_m[n], self.new_v[n]]
            return made + [self.loss]
        names = REDUCE_GROUPS[tag]
        us, ts = self._exchanged[tag]
        for n, g in zip(names, _reduce_finish(us, ts, names, tag)):
            shp = self.weights[n].shape
            two_d = shp[-2:]
            d, m, v = _adamw(self.weights[n].reshape(two_d), g, self.moms[n].reshape(two_d),
                             self.vels[n].reshape(two_d), "adamw_" + n)
            self.grads[n], self.delta[n] = g.reshape(shp), d.reshape(shp)
            self.new_m[n], self.new_v[n] = m.reshape(shp), v.reshape(shp)
            made += [self.grads[n], self.delta[n], self.new_m[n], self.new_v[n]]
        return made

    def before(self, tags, array):
        made = [a for tag in tags for a in self.finish(tag)]
        return _after(made, [array])[1][0]


def _local_step(x, target, w, small, sync):
    L = x.shape[0]
    row = lambda v: v.reshape(1, -1)

    a_re, a_im = small["ssm_a_re"].reshape(1, NS), small["ssm_a_im"].reshape(1, NS)
    ldt = jnp.repeat(small["ssm_log_dt"].reshape(SSM_GROUPS), SSM_STATE).reshape(1, NS)
    to_cn = lambda b: b.reshape(SSM_GROUPS, SSM_STATE, SSM_GROUP).transpose(2, 0, 1).reshape(SSM_GROUP, NS)
    c_to_cn = lambda c: c.reshape(SSM_GROUPS, SSM_GROUP, SSM_STATE).transpose(1, 0, 2).reshape(SSM_GROUP, NS)
    b_re, b_im = to_cn(small["ssm_b_re"]), to_cn(small["ssm_b_im"])
    c_re, c_im = c_to_cn(small["ssm_c_re"]), c_to_cn(small["ssm_c_im"])
    d_skip = row(small["ssm_d"])
    pw, pwr, bd, cdt = _disc_fwd(a_re, a_im, ldt, b_re, b_im, c_re, c_im)

    onehot = _bucket_onehot()
    table_t = small["rel_bias_table"].T.reshape(3, HEADS_PER_GROUP, N_BUCKETS)
    table_t = jnp.pad(table_t, ((0, 0), (0, 8 - HEADS_PER_GROUP), (0, 0)))
    bias = _bias_expand(table_t, onehot)[:, :, :HEADS_PER_GROUP].reshape(
        3, 2, HEADS_PER_GROUP, ATTN_BLOCK, 2 * ATTN_BLOCK)

    n1, nm, n2, nf = row(small["ffn1_norm"]), row(small["mix_norm"]), row(small["ffn2_norm"]), row(small["final_norm"])
    gate_bias = row(small["gate_bias"])

    x1, a1, b1 = _ffn_fwd(x, n1, w["ffn1_w_gate"], w["ffn1_w_up"], w["ffn1_w_down"], "ffn1_fwd")
    *qkv, u, gates = _mix_in_fwd(x1, nm, w["w_in"], gate_bias)
    q, k, v = qkv[0:3], qkv[3:6], qkv[6:9]
    o_g, lse_g = [], []
    for grp in range(3):
        o, lse = _attn_fwd(q[grp], k[grp], v[grp], bias[grp], f"attn_fwd_{grp}")
        o_g.append(o)
        lse_g.append(lse)
    y, s = _ssm_fwd(u, bd, cdt, d_skip, pw)
    x2, o_attn, *lse_tot = _mix_out_fwd(x1, o_g, lse_g, y, gates, w["w_attn_branch"], w["ssm_w_glu"],
                                        w["w_ssm_branch"], w["w_out"])
    x3, a2, b2 = _ffn_fwd(x2, n2, w["ffn2_w_gate"], w["ffn2_w_up"], w["ffn2_w_down"], "ffn2_fwd")
    loss_blk, dx3, d_nf = _loss_fwd_bwd(x3, nf, target)

    gw, gs = {}, {}
    gs["final_norm"] = d_nf

    dx2, da, db, sact, h, d_out, gs["ffn2_norm"] = _ffn_bwd(dx3, x2, n2, a2, b2, w["ffn2_w_gate"], w["ffn2_w_up"],
                                                            w["ffn2_w_down"], "ffn2_bwd")
    gw["ffn2_w_gate"] = _matmul_tn(h[None], da, "ffn2_dw_gate")
    gw["ffn2_w_up"] = _matmul_tn(h[None], db, "ffn2_dw_up")
    gw["ffn2_w_down"] = _matmul_tn(sact, d_out[None], "ffn2_dw_down")
    sync.grads_ready("ffn2", gw)

    head_sum = (jnp.arange(GROUP_WIDTH)[:, None] // HEAD_DIM == jnp.arange(GROUP_WIDTH)[None, :] // HEAD_DIM).astype(F32)
    (*d_o_delta, dy, dgp, mix, dya, dys, ys2, gel, dglu, gs["gate_bias"]) = _mix_out_bwd(
        dx2, o_attn, y, gates, w["w_attn_branch"], w["ssm_w_glu"], w["w_ssm_branch"], w["w_out"], head_sum)
    d_o, delta = d_o_delta[0:3], d_o_delta[3:6]
    gw["w_out"] = _matmul_tn(mix[None], dx2[None], "dw_out")[0]
    gw["w_attn_branch"] = _matmul_tn(o_attn[None], dya[None], "dw_attn_branch")[0]
    gw["w_ssm_branch"] = _matmul_tn(ys2[None], dys[None], "dw_ssm_branch")[0]
    gw["ssm_w_glu"] = _matmul_tn(gel[None], dglu[None], "dw_glu")[0]
    sync.grads_ready("mixer", gw)

    dqs, dks, dvs, dsums = [], [], [], []
    for grp in range(3):
        dq, dk, dv, dsum = _attn_bwd(q[grp], k[grp], v[grp], d_o[grp], lse_tot[grp], delta[grp], bias[grp],
                                     f"attn_bwd_{grp}")
        dqs.append(dq)
        dks.append(dk)
        dvs.append(dv)
        dsums.append(dsum.reshape(HEADS_PER_GROUP, -1))
    dsum_all = jnp.pad(jnp.stack(dsums), ((0, 0), (0, 8 - HEADS_PER_GROUP), (0, 0)))
    d_table = _bias_reduce(dsum_all, onehot)[:, :HEADS_PER_GROUP]
    gs["rel_bias_table"] = d_table.reshape(3 * HEADS_PER_GROUP, N_BUCKETS).T

    du, gs["ssm_d"], d_bd, d_cdt, d_ab = _ssm_bwd(dy, u, s, bd, cdt, d_skip, pwr)
    group_sum = (jnp.arange(NS)[:, None] // SSM_STATE == jnp.arange(128)[None, :]).astype(F32)
    d_are, d_aim, d_ldt, d_bre, d_bim, d_cre, d_cim = _disc_bwd(a_re, a_im, ldt, b_re, b_im, d_bd, d_cdt, d_ab, group_sum)
    gs["ssm_a_re"], gs["ssm_a_im"] = d_are, d_aim
    gs["ssm_log_dt"] = d_ldt[0, :SSM_GROUPS]
    from_cn = lambda t: t.reshape(SSM_GROUP, SSM_GROUPS, SSM_STATE).transpose(1, 2, 0)
    c_from_cn = lambda t: t.reshape(SSM_GROUP, SSM_GROUPS, SSM_STATE).transpose(1, 0, 2)
    gs["ssm_b_re"], gs["ssm_b_im"] = from_cn(d_bre), from_cn(d_bim)
    gs["ssm_c_re"], gs["ssm_c_im"] = c_from_cn(d_cre), c_from_cn(d_cim)

    dx2 = sync.before(["ffn2"], dx2)
    dx1, hm, dz, gs["mix_norm"] = _mix_in_bwd(dx2, x1, nm, dqs + dks + dvs, du, dgp, w["w_in"])
    gw["w_in"] = _matmul_tn(hm[None], dz[None], "dw_in")[0]
    sync.grads_ready("w_in", gw)

    dx1 = sync.before(["mixer"], dx1)
    dx0, da, db, sact, h, d_out, gs["ffn1_norm"] = _ffn_bwd(dx1, x, n1, a1, b1, w["ffn1_w_gate"], w["ffn1_w_up"],
                                                            w["ffn1_w_down"], "ffn1_bwd")
    sync.small_ready(gs, loss_blk)
    h = sync.before(["w_in"], h)
    gw["ffn1_w_gate"] = _matmul_tn(h[None], da, "ffn1_dw_gate")
    gw["ffn1_w_up"] = _matmul_tn(h[None], db, "ffn1_dw_up")
    sact = sync.before(["small"], sact)
    gw["ffn1_w_down"] = _matmul_tn(sact, d_out[None], "ffn1_dw_down")
    sync.grads_ready("ffn1", gw)
    return dx0


def kernel(x, ffn1_norm, ffn1_w_gate, ffn1_w_up, ffn1_w_down, mix_norm, w_in, gate_bias, rel_bias_table, ssm_a_re, ssm_a_im, ssm_log_dt, ssm_b_re, ssm_b_im, ssm_c_re, ssm_c_im, ssm_d, ssm_w_glu, w_attn_branch, w_ssm_branch, w_out, ffn2_norm, ffn2_w_gate, ffn2_w_up, ffn2_w_down, final_norm, loss_target, m_ffn1_norm, m_ffn1_w_gate, m_ffn1_w_up, m_ffn1_w_down, m_mix_norm, m_w_in, m_gate_bias, m_rel_bias_table, m_ssm_a_re, m_ssm_a_im, m_ssm_log_dt, m_ssm_b_re, m_ssm_b_im, m_ssm_c_re, m_ssm_c_im, m_ssm_d, m_ssm_w_glu, m_w_attn_branch, m_w_ssm_branch, m_w_out, m_ffn2_norm, m_ffn2_w_gate, m_ffn2_w_up, m_ffn2_w_down, m_final_norm, v_ffn1_norm, v_ffn1_w_gate, v_ffn1_w_up, v_ffn1_w_down, v_mix_norm, v_w_in, v_gate_bias, v_rel_bias_table, v_ssm_a_re, v_ssm_a_im, v_ssm_log_dt, v_ssm_b_re, v_ssm_b_im, v_ssm_c_re, v_ssm_c_im, v_ssm_d, v_ssm_w_glu, v_w_attn_branch, v_w_ssm_branch, v_w_out, v_ffn2_norm, v_ffn2_w_gate, v_ffn2_w_up, v_ffn2_w_down, v_final_norm):
    args = dict(locals())
    weights = {n: args[n] for n in ORDER}
    moms = {n: args["m_" + n] for n in ORDER}
    vels = {n: args["v_" + n] for n in ORDER}

    shard2d = {n: weights[n].reshape(weights[n].shape[-2:]) for n in BIG}
    first, rest = BIG[:3], BIG[3:]
    full = dict(zip(first, _gather_weights([shard2d[n].astype(BF16) for n in first], "gather_ffn1_weights")))
    later_full = _gather_weights_behind([shard2d[n].astype(BF16) for n in rest], "gather_later_weights",
                                        COLLECTIVE_IDS["gather"])
    full.update(zip(rest, later_full))
    for n in COL_SHARDED:
        full[n] = _join_cols(full[n])
    full["w_out"] = full["w_out"].reshape(D_MODEL, D_MODEL)

    small = {n: weights[n] for n in SMALL}
    sync = _GradSync(weights, moms, vels)
    grad_x = _local_step(x[0], loss_target[0], full, small, sync)
    sync.finish("ffn1")
    return (sync.loss, grad_x[None], *[sync.grads[n] for n in ORDER], *[sync.delta[n] for n in ORDER],
            *[sync.new_m[n] for n in ORDER], *[sync.new_v[n] for n in ORDER])
```

```python
import functools
import math

import jax
import jax.numpy as jnp
from jax import lax
from jax.experimental import pallas as pl
from jax.experimental.pallas import tpu as pltpu
from jax.experimental.pallas import tpu_sc as plsc

F32 = jnp.float32
BF16 = jnp.bfloat16
MESH = pl.DeviceIdType.MESH

D_MODEL = 1024
D_FF = 2816
HEAD_DIM = 64
HEADS_PER_GROUP = 4
DILATIONS = (1, 4, 16)
WINDOW_STEPS = 128
ATTN_BLOCK = 128
ATTN_QB = 4
GROUP_WIDTH = HEADS_PER_GROUP * HEAD_DIM
ATTN_WIDTH = 3 * GROUP_WIDTH
N_BUCKETS = 32
MAX_DISTANCE = 2048
NEG_INF = -1e30
SSM_WIDTH = 512
SSM_GROUP = 16
SSM_GROUPS = 32
SSM_STATE = 64
NS = SSM_GROUPS * SSM_STATE
EPS = 1e-6
IN_WIDTH = 3 * ATTN_WIDTH + SSM_WIDTH + 2 * D_MODEL
Q_SCALE = HEAD_DIM ** -0.5
N_SHARD = 4
FF_SHARD = D_FF // N_SHARD
ADAM_LR, ADAM_B1, ADAM_B2, ADAM_EPS, ADAM_WD, ADAM_STEP = 0.001, 0.9, 0.999, 1e-08, 0.01, 10

LANES = 128
VMEM_LIMIT = 56 * 1024 * 1024
ROW_TILE = 512
FFN_BWD_TILE = 256
SSM_CHUNK = 256
SCAN_LANES = 512
ADAMW_BLOCK_BYTES = 1 << 20
TN_VMEM_BUDGET = 40 * 1024 * 1024
REDUCE_GROUPS = {
    "ffn2": ["ffn2_w_gate", "ffn2_w_up", "ffn2_w_down"],
    "mixer": ["w_out", "w_attn_branch", "w_ssm_branch", "ssm_w_glu"],
    "w_in": ["w_in"],
    "ffn1": ["ffn1_w_gate", "ffn1_w_up", "ffn1_w_down"],
}
COLLECTIVE_IDS = {name: i for i, name in enumerate(
    ["gather_small"] + [stage + "_" + tag for tag in REDUCE_GROUPS for stage in ("swap", "exchange")])}


def _params(**kw):
    return pltpu.CompilerParams(vmem_limit_bytes=VMEM_LIMIT, **kw)


def _dot(a, b):
    return jnp.dot(a, b, preferred_element_type=F32)


def _dot_nt(a, b):
    return lax.dot_general(a, b, (((1,), (1,)), ((), ())), preferred_element_type=F32)


def _dot_tn(a, b):
    return lax.dot_general(a, b, (((0,), (0,)), ((), ())), preferred_element_type=F32)


def _dot_exact(a, b):
    return jnp.dot(a, b, preferred_element_type=F32, precision=lax.Precision.HIGHEST)


def _dot_nt_exact(a, b):
    return lax.dot_general(a, b, (((1,), (1,)), ((), ())), preferred_element_type=F32,
                           precision=lax.Precision.HIGHEST)


def _rms(x):
    r = lax.rsqrt(jnp.mean(x * x, axis=-1, keepdims=True) + EPS)
    return r, x * r


def _rms_bwd(dh, g, r, xhat):
    dxh = dh * g
    return r * (dxh - xhat * jnp.mean(dxh * xhat, axis=-1, keepdims=True))


def _sigmoid(x):
    return 1.0 / (1.0 + jnp.exp(-x))


_GELU_C = math.sqrt(2.0 / math.pi)


def _gelu(x):
    return 0.5 * x * (1.0 + jnp.tanh(_GELU_C * (x + 0.044715 * x * x * x)))


def _gelu_grad(x):
    t = jnp.tanh(_GELU_C * (x + 0.044715 * x * x * x))
    return 0.5 * (1.0 + t) + 0.5 * x * (1.0 - t * t) * _GELU_C * (1.0 + 3 * 0.044715 * x * x)


def _whole():
    return pl.BlockSpec(memory_space=pltpu.VMEM)


def _row_tile(rows, cap):
    if rows <= cap:
        return rows
    return max(t for t in range(8, cap + 1, 8) if rows % t == 0)


def _rows(tm, w):
    return pl.BlockSpec((tm, w), lambda i: (i, 0))


def _acc_row(w):
    return pl.BlockSpec((1, w), lambda i: (0, 0))


def _ffn_fwd(x, g, wg, wu, wd, name, carried=()):
    L = x.shape[0]
    tm = min(ROW_TILE, L)
    n = len(carried)
    steps = L // tm

    def body(x_ref, g_ref, wg_ref, wu_ref, wd_ref, *refs):
        shard_refs, (xo_ref, a_ref, b_ref), full_refs, sems = refs[:n], refs[n:n + 3], refs[n + 3:2 * n + 3], refs[2 * n + 3:]
        if n:
            start, finish = _gather_parts([w.shape for w in carried], shard_refs, full_refs, *sems)
            pl.when(pl.program_id(0) == 0)(start)
        xv = x_ref[...]
        r, xhat = _rms(xv)
        h = (xhat * g_ref[...]).astype(BF16)
        acc = jnp.zeros((tm, D_MODEL), F32)
        for j in range(N_SHARD):
            a = _dot(h, wg_ref[j])
            b = _dot(h, wu_ref[j])
            a_ref[j] = a.astype(BF16)
            b_ref[j] = b.astype(BF16)
            s = (a * _sigmoid(a) * b).astype(BF16)
            acc = acc + _dot(s, wd_ref[j])
        xo_ref[...] = xv + 0.5 * acc
        if n:
            pl.when(pl.program_id(0) == steps - 1)(finish)

    act = pl.BlockSpec((N_SHARD, tm, FF_SHARD), lambda i: (0, i, 0))
    return pl.pallas_call(
        body, name=name, grid=(steps,),
        in_specs=[_rows(tm, D_MODEL), _whole(), _whole(), _whole(), _whole()] + [_ANY] * n,
        out_specs=[_rows(tm, D_MODEL), act, act] + [_ANY] * n,
        out_shape=[jax.ShapeDtypeStruct((L, D_MODEL), F32),
                   jax.ShapeDtypeStruct((N_SHARD, L, FF_SHARD), BF16),
                   jax.ShapeDtypeStruct((N_SHARD, L, FF_SHARD), BF16)]
        + [jax.ShapeDtypeStruct((N_SHARD,) + w.shape, w.dtype) for w in carried],
        scratch_shapes=[pltpu.SemaphoreType.DMA((7 * n,)), pltpu.SemaphoreType.DMA((7 * n,))] if n else [],
        compiler_params=_params(),
    )(x, g, wg, wu, wd, *carried)


def _ffn_bwd(dxo, x, g, a, b, wg, wu, wd, name):
    L = x.shape[0]
    tm = min(FFN_BWD_TILE, L)

    def body(dxo_ref, x_ref, g_ref, a_ref, b_ref, wg_ref, wu_ref, wd_ref,
             dxi_ref, da_ref, db_ref, s_ref, h_ref, do_ref, dg_ref):
        i = pl.program_id(0)
        xv = x_ref[...]
        gv = g_ref[...]
        r, xhat = _rms(xv)
        h_ref[...] = (xhat * gv).astype(BF16)
        dxo_v = dxo_ref[...]
        d_out = (0.5 * dxo_v).astype(BF16)
        do_ref[...] = d_out
        dh = jnp.zeros((tm, D_MODEL), F32)
        for j in range(N_SHARD):
            av = a_ref[j].astype(F32)
            bv = b_ref[j].astype(F32)
            sg = _sigmoid(av)
            sl = av * sg
            ds = _dot_nt(d_out, wd_ref[j])
            dbv = (ds * sl).astype(BF16)
            dav = (ds * bv * (sg * (1.0 + av * (1.0 - sg)))).astype(BF16)
            da_ref[j] = dav
            db_ref[j] = dbv
            s_ref[j] = (sl * bv).astype(BF16)
            dh = dh + _dot_nt(dav, wg_ref[j]) + _dot_nt(dbv, wu_ref[j])

        @pl.when(i == 0)
        def _():
            dg_ref[...] = jnp.zeros_like(dg_ref)

        dg_ref[...] += jnp.sum(dh * xhat, axis=0, keepdims=True)
        dxi_ref[...] = dxo_v + _rms_bwd(dh, gv, r, xhat)

    act = pl.BlockSpec((N_SHARD, tm, FF_SHARD), lambda i: (0, i, 0))
    act_shape = jax.ShapeDtypeStruct((N_SHARD, L, FF_SHARD), BF16)
    return pl.pallas_call(
        body, name=name, grid=(L // tm,),
        in_specs=[_rows(tm, D_MODEL), _rows(tm, D_MODEL), _whole(), act, act, _whole(), _whole(), _whole()],
        out_specs=[_rows(tm, D_MODEL), act, act, act, _rows(tm, D_MODEL), _rows(tm, D_MODEL), _acc_row(D_MODEL)],
        out_shape=[jax.ShapeDtypeStruct((L, D_MODEL), F32), act_shape, act_shape, act_shape,
                   jax.ShapeDtypeStruct((L, D_MODEL), BF16), jax.ShapeDtypeStruct((L, D_MODEL), BF16),
                   jax.ShapeDtypeStruct((1, D_MODEL), F32)],
        compiler_params=_params(),
    )(dxo, x, g, a, b, wg, wu, wd)


def _matmul_tn(a, b, name):
    ja, L, K = a.shape
    jb, _, N = b.shape
    J = max(ja, jb)
    splits = [s for s in (1, 2, 4, 8) if s == 1 or N % (s * LANES) == 0]
    nsplit = next((s for s in splits if 2 * K * (N // s) * 4 <= TN_VMEM_BUDGET // 2), splits[-1])
    nc = N // nsplit
    left = TN_VMEM_BUDGET - 2 * K * nc * 4
    row_bytes = 2 * (K * a.dtype.itemsize + nc * b.dtype.itemsize)
    tm = next((t for t in (2048, 1024, 512, 256) if L % t == 0 and t * row_bytes <= left), min(128, L))

    def body(a_ref, b_ref, o_ref):
        @pl.when(pl.program_id(2) == 0)
        def _():
            o_ref[...] = jnp.zeros_like(o_ref)

        o_ref[...] += _dot_tn(a_ref[...].astype(BF16), b_ref[...].astype(BF16))

    return pl.pallas_call(
        body, name=name, grid=(J, nsplit, L // tm),
        in_specs=[pl.BlockSpec((None, tm, K), (lambda j, s, i: (j, i, 0)) if ja > 1 else (lambda j, s, i: (0, i, 0))),
                  pl.BlockSpec((None, tm, nc), (lambda j, s, i: (j, i, s)) if jb > 1 else (lambda j, s, i: (0, i, s)))],
        out_specs=pl.BlockSpec((None, K, nc), lambda j, s, i: (j, 0, s)),
        out_shape=jax.ShapeDtypeStruct((J, K, N), F32),
        compiler_params=_params(),
    )(a, b)


def _loss_fwd_bwd(x, g, target):
    L = x.shape[0]
    tm = min(ROW_TILE, L)

    def body(x_ref, g_ref, t_ref, loss_ref, dx_ref, dg_ref):
        i = pl.program_id(0)
        xv = x_ref[...]
        gv = g_ref[...]
        r, xhat = _rms(xv)
        err = xhat * gv - t_ref[...]
        part = 0.5 * jnp.sum(jnp.sum(err * err, axis=1, keepdims=True) * (1.0 / D_MODEL), axis=0, keepdims=True)
        dy = err * (1.0 / D_MODEL)

        @pl.when(i == 0)
        def _():
            dg_ref[...] = jnp.zeros_like(dg_ref)
            loss_ref[...] = jnp.zeros_like(loss_ref)

        loss_ref[...] += jnp.broadcast_to(part, loss_ref.shape)
        dg_ref[...] += jnp.sum(dy * xhat, axis=0, keepdims=True)
        dx_ref[...] = _rms_bwd(dy, gv, r, xhat)

    return pl.pallas_call(
        body, name="loss_fwd_bwd", grid=(L // tm,),
        in_specs=[_rows(tm, D_MODEL), _whole(), _rows(tm, D_MODEL)],
        out_specs=[pl.BlockSpec((8, 128), lambda i: (0, 0)), _rows(tm, D_MODEL), _acc_row(D_MODEL)],
        out_shape=[jax.ShapeDtypeStruct((8, 128), F32), jax.ShapeDtypeStruct((L, D_MODEL), F32),
                   jax.ShapeDtypeStruct((1, D_MODEL), F32)],
        compiler_params=_params(),
    )(x, g, target)


_C_K = ATTN_WIDTH
_C_V = 2 * ATTN_WIDTH
_C_U = 3 * ATTN_WIDTH
_C_G = _C_U + SSM_WIDTH


def _residue_spec(d, tm):
    return pl.BlockSpec((d, tm // d, GROUP_WIDTH), lambda i: (0, i, 0))


def _residue_shape(d, L, dtype):
    return jax.ShapeDtypeStruct((d, L // d, GROUP_WIDTH), dtype)


def _residue_scratch(tm):
    return pltpu.VMEM((GROUP_WIDTH // LANES, tm, LANES), F32)


def _to_residues(val, out_ref, scr, d):
    if d == 1:
        out_ref[0] = val.astype(out_ref.dtype)
        return
    tm = val.shape[0]
    for half in range(GROUP_WIDTH // LANES):
        cols = slice(half * LANES, (half + 1) * LANES)
        scr[half] = val[:, cols]
        for r in range(d):
            out_ref[r, :, cols] = scr[half, pl.ds(r, tm // d, stride=d), :].astype(out_ref.dtype)


def _from_residues(ref, scr, d):
    if d == 1:
        return ref[0].astype(F32)
    rows = ref.shape[1]
    for half in range(GROUP_WIDTH // LANES):
        cols = slice(half * LANES, (half + 1) * LANES)
        for r in range(d):
            scr[half, pl.ds(r, rows, stride=d), :] = ref[r, :, cols].astype(F32)
    return jnp.concatenate([scr[half] for half in range(GROUP_WIDTH // LANES)], axis=1)


def _mix_in_fwd(x, g, w_in, gate_bias):
    L = x.shape[0]
    tm = min(ROW_TILE, L)

    def body(x_ref, g_ref, w_ref, gb_ref, *refs):
        qkv_refs, (u_ref, gate_ref, scr) = refs[:9], refs[9:]
        r, xhat = _rms(x_ref[...])
        h = (xhat * g_ref[...]).astype(BF16)
        for part, (c0, scale) in enumerate(((0, Q_SCALE), (_C_K, 1.0), (_C_V, 1.0))):
            z = _dot(h, w_ref[:, c0:c0 + ATTN_WIDTH]) * scale
            for grp, d in enumerate(DILATIONS):
                _to_residues(z[:, grp * GROUP_WIDTH:(grp + 1) * GROUP_WIDTH], qkv_refs[3 * part + grp], scr, d)
        u_ref[...] = _dot(h, w_ref[:, _C_U:_C_G])
        gate_ref[...] = _sigmoid(_dot(h, w_ref[:, _C_G:IN_WIDTH]) + gb_ref[...])

    return pl.pallas_call(
        body, name="mix_in_fwd", grid=(L // tm,),
        in_specs=[_rows(tm, D_MODEL), _whole(), _whole(), _whole()],
        out_specs=[_residue_spec(d, tm) for d in DILATIONS] * 3 + [_rows(tm, SSM_WIDTH), _rows(tm, 2 * D_MODEL)],
        out_shape=[_residue_shape(d, L, BF16) for d in DILATIONS] * 3
        + [jax.ShapeDtypeStruct((L, SSM_WIDTH), F32), jax.ShapeDtypeStruct((L, 2 * D_MODEL), F32)],
        scratch_shapes=[_residue_scratch(tm)],
        compiler_params=_params(),
    )(x, g, w_in, gate_bias)


def _mix_in_bwd(dx2, x, g, dqkv, du, dgp, w_in):
    L = x.shape[0]
    tm = min(ROW_TILE, L)

    def body(dx2_ref, x_ref, g_ref, *refs):
        piece_refs = refs[:9]
        du_ref, dgp_ref, w_ref, dx1_ref, h_ref, dz_ref, dg_ref, scr = refs[9:]
        i = pl.program_id(0)
        gv = g_ref[...]
        r, xhat = _rms(x_ref[...])
        h_ref[...] = (xhat * gv).astype(BF16)
        for part in range(3):
            for grp, d in enumerate(DILATIONS):
                c0 = part * ATTN_WIDTH + grp * GROUP_WIDTH
                dz_ref[:, c0:c0 + GROUP_WIDTH] = _from_residues(piece_refs[3 * part + grp], scr, d).astype(BF16)
        dz_ref[:, _C_U:_C_G] = du_ref[...].astype(BF16)
        dz_ref[:, _C_G:IN_WIDTH] = dgp_ref[...]
        dh = _dot_nt(dz_ref[...], w_ref[...])

        @pl.when(i == 0)
        def _():
            dg_ref[...] = jnp.zeros_like(dg_ref)

        dg_ref[...] += jnp.sum(dh * xhat, axis=0, keepdims=True)
        dx1_ref[...] = dx2_ref[...] + _rms_bwd(dh, gv, r, xhat)

    return pl.pallas_call(
        body, name="mix_in_bwd", grid=(L // tm,),
        in_specs=[_rows(tm, D_MODEL), _rows(tm, D_MODEL), _whole()] + [_residue_spec(d, tm) for d in DILATIONS] * 3
        + [_rows(tm, SSM_WIDTH), _rows(tm, 2 * D_MODEL), _whole()],
        out_specs=[_rows(tm, D_MODEL), _rows(tm, D_MODEL), _rows(tm, IN_WIDTH), _acc_row(D_MODEL)],
        out_shape=[jax.ShapeDtypeStruct((L, D_MODEL), F32), jax.ShapeDtypeStruct((L, D_MODEL), BF16),
                   jax.ShapeDtypeStruct((L, IN_WIDTH), BF16), jax.ShapeDtypeStruct((1, D_MODEL), F32)],
        scratch_shapes=[_residue_scratch(tm)],
        compiler_params=_params(),
    )(dx2, x, g, *dqkv, du, dgp, w_in)


def _bucket_onehot():
    qi = jnp.arange(ATTN_BLOCK)[:, None]
    kj = jnp.arange(2 * ATTN_BLOCK)[None, :]
    steps = jnp.maximum(qi + ATTN_BLOCK - kj, 0)
    max_exact = N_BUCKETS // 2
    out = []
    for d in DILATIONS:
        dist = steps * d
        df = jnp.maximum(dist, 1).astype(F32)
        large = max_exact + (jnp.log(df / max_exact) / math.log(MAX_DISTANCE / max_exact)
                             * (N_BUCKETS - max_exact)).astype(jnp.int32)
        large = jnp.minimum(large, N_BUCKETS - 1)
        bucket = jnp.where(dist < max_exact, dist, large).reshape(-1)
        out.append((bucket[None, :] == jnp.arange(N_BUCKETS)[:, None]).astype(F32))
    return jnp.stack(out)


def _bias_expand(table_t, onehot):
    n = onehot.shape[-1]

    def body(t_ref, oh_ref, o_ref):
        bias = _dot_exact(t_ref[...], oh_ref[...])
        col = lax.broadcasted_iota(jnp.int32, (8, n), 1)
        qi = col // (2 * ATTN_BLOCK)
        kj = col - qi * (2 * ATTN_BLOCK)
        steps = qi + ATTN_BLOCK - kj
        band = (steps >= 0) & (steps <= WINDOW_STEPS)
        o_ref[0] = jnp.where(band & (kj >= ATTN_BLOCK), bias, NEG_INF)
        o_ref[1] = jnp.where(band, bias, NEG_INF)

    return pl.pallas_call(
        body, name="bias_expand", grid=(3,),
        in_specs=[pl.BlockSpec((None, 8, N_BUCKETS), lambda g: (g, 0, 0)),
                  pl.BlockSpec((None, N_BUCKETS, n), lambda g: (g, 0, 0))],
        out_specs=pl.BlockSpec((None, 2, 8, n), lambda g: (g, 0, 0, 0)),
        out_shape=jax.ShapeDtypeStruct((3, 2, 8, n), F32),
        compiler_params=_params(),
    )(table_t, onehot)


def _bias_reduce(dsum, onehot):
    n = onehot.shape[-1]

    def body(d_ref, oh_ref, o_ref):
        o_ref[...] = _dot_nt_exact(d_ref[...], oh_ref[...])

    return pl.pallas_call(
        body, name="bias_reduce", grid=(3,),
        in_specs=[pl.BlockSpec((None, 8, n), lambda g: (g, 0, 0)),
                  pl.BlockSpec((None, N_BUCKETS, n), lambda g: (g, 0, 0))],
        out_specs=pl.BlockSpec((None, 8, N_BUCKETS), lambda g: (g, 0, 0)),
        out_shape=jax.ShapeDtypeStruct((3, 8, N_BUCKETS), F32),
        compiler_params=_params(),
    )(dsum, onehot)


def _head_of_col(rows):
    return lax.broadcasted_iota(jnp.int32, (rows, GROUP_WIDTH), 1) // HEAD_DIM


def _attn_specs(qb):
    rows = qb * ATTN_BLOCK
    cur = pl.BlockSpec((None, rows, GROUP_WIDTH), lambda r, n: (r, n, 0))
    prev = pl.BlockSpec((None, ATTN_BLOCK, GROUP_WIDTH), lambda r, n: (r, jnp.maximum(n * qb - 1, 0), 0))
    bias = pl.BlockSpec((2, HEADS_PER_GROUP, ATTN_BLOCK, 2 * ATTN_BLOCK), lambda r, n: (0, 0, 0, 0))
    return cur, prev, bias


def _attn_fwd(q, k, v, bias, name):
    d, M, _ = q.shape
    nb = M // ATTN_BLOCK
    qb = min(ATTN_QB, nb)

    def body(q_ref, kp_ref, kc_ref, vp_ref, vc_ref, bias_ref, o_ref, lse_ref):
        n = pl.program_id(1)
        q_head = _head_of_col(ATTN_BLOCK)
        kv_head = _head_of_col(2 * ATTN_BLOCK)
        kwin = jnp.concatenate([kp_ref[...], kc_ref[...]], axis=0)
        vwin = jnp.concatenate([vp_ref[...], vc_ref[...]], axis=0)
        for b in range(qb):
            rows = slice(b * ATTN_BLOCK, (b + 1) * ATTN_BLOCK)
            window = slice(b * ATTN_BLOCK, (b + 2) * ATTN_BLOCK)
            variant = jnp.minimum(n, 1) if b == 0 else 1
            qv = q_ref[rows, :]
            kk = kwin[window]
            vv = vwin[window]
            o_acc = jnp.zeros((ATTN_BLOCK, GROUP_WIDTH), F32)
            lse_acc = jnp.zeros((ATTN_BLOCK, GROUP_WIDTH), F32)
            for hh in range(HEADS_PER_GROUP):
                hm = q_head == hh
                qh = jnp.where(hm, qv, jnp.zeros_like(qv))
                logits = _dot_nt(qh, kk) + bias_ref[variant, hh]
                m = jnp.max(logits, axis=1, keepdims=True)
                p = jnp.exp(logits - m)
                vh = jnp.where(kv_head == hh, vv, jnp.ones_like(vv))
                pv = _dot(p.astype(BF16), vh)
                c_sum = ((hh + 1) % HEADS_PER_GROUP) * HEAD_DIM
                den = pv[:, c_sum:c_sum + 1]
                o_acc = jnp.where(hm, pv * (1.0 / den), o_acc)
                lse_acc = jnp.where(hm, m + jnp.log(den), lse_acc)
            o_ref[rows, :] = o_acc
            lse_ref[rows, :] = lse_acc

    cur, prev, full = _attn_specs(qb)
    return pl.pallas_call(
        body, name=name, grid=(d, nb // qb),
        in_specs=[cur, prev, cur, prev, cur, full],
        out_specs=[cur, cur],
        out_shape=[jax.ShapeDtypeStruct((d, M, GROUP_WIDTH), F32)] * 2,
        compiler_params=_params(),
    )(q, k, k, v, v, bias)


def _attn_bwd(q, k, v, do, lse, delta, bias, name):
    d, M, _ = q.shape
    nb = M // ATTN_BLOCK
    qb = min(ATTN_QB, nb)
    ns = nb // qb
    rows_q = qb * ATTN_BLOCK
    last = slice(rows_q - ATTN_BLOCK, rows_q)

    def body(q_ref, kp_ref, kc_ref, vp_ref, vc_ref, do_ref, lse_ref, dl_ref, bias_ref,
             dq_ref, dk_ref, dv_ref, dsum_ref, pk_ref, pv_ref, wk_ref, wv_ref):
        r = pl.program_id(0)
        n = pl.program_id(1)

        @pl.when((r == 0) & (n == 0))
        def _():
            dsum_ref[...] = jnp.zeros_like(dsum_ref)

        @pl.when(n == 0)
        def _():
            pk_ref[...] = jnp.zeros_like(pk_ref)
            pv_ref[...] = jnp.zeros_like(pv_ref)

        @pl.when(n < ns)
        def _():
            q_head = _head_of_col(ATTN_BLOCK)
            kwin = jnp.concatenate([kp_ref[...], kc_ref[...]], axis=0)
            vwin = jnp.concatenate([vp_ref[...], vc_ref[...]], axis=0)
            wk_ref[...] = jnp.zeros_like(wk_ref)
            wv_ref[...] = jnp.zeros_like(wv_ref)
            for b in range(qb):
                rows = slice(b * ATTN_BLOCK, (b + 1) * ATTN_BLOCK)
                window = slice(b * ATTN_BLOCK, (b + 2) * ATTN_BLOCK)
                variant = jnp.minimum(n, 1) if b == 0 else 1
                qv = q_ref[rows, :]
                dov = do_ref[rows, :]
                kk = kwin[window]
                vv = vwin[window]
                dq_acc = jnp.zeros((ATTN_BLOCK, GROUP_WIDTH), F32)
                dkk = jnp.zeros((2 * ATTN_BLOCK, GROUP_WIDTH), F32)
                dvv = jnp.zeros((2 * ATTN_BLOCK, GROUP_WIDTH), F32)
                for hh in range(HEADS_PER_GROUP):
                    hm = q_head == hh
                    c0 = hh * HEAD_DIM
                    qh = jnp.where(hm, qv, jnp.zeros_like(qv))
                    doh = jnp.where(hm, dov, jnp.zeros_like(dov))
                    logits = _dot_nt(qh, kk) + bias_ref[variant, hh]
                    p = jnp.exp(logits - lse_ref[rows, c0:c0 + 1])
                    dp = _dot_nt(doh, vv)
                    ds = p * (dp - dl_ref[rows, c0:c0 + 1])
                    dsum_ref[hh] += ds
                    ds16 = ds.astype(BF16)
                    dq_acc = jnp.where(hm, _dot(ds16, kk), dq_acc)
                    dkk = dkk + _dot_tn(ds16, qh)
                    dvv = dvv + _dot_tn(p.astype(BF16), doh)
                dq_ref[rows, :] = (dq_acc * Q_SCALE).astype(BF16)
                wk_ref[window, :] += dkk
                wv_ref[window, :] += dvv
            for out_ref, part_ref, win_ref in ((dk_ref, pk_ref, wk_ref), (dv_ref, pv_ref, wv_ref)):
                if qb > 1:
                    out_ref[0:rows_q - ATTN_BLOCK, :] = part_ref[0:rows_q - ATTN_BLOCK, :].astype(BF16)
                out_ref[last, :] = (part_ref[last, :] + win_ref[0:ATTN_BLOCK, :]).astype(BF16)
                part_ref[...] = win_ref[ATTN_BLOCK:, :]

        @pl.when(n == ns)
        def _():
            dk_ref[...] = pk_ref[...].astype(BF16)
            dv_ref[...] = pv_ref[...].astype(BF16)

    def clamp(n):
        return jnp.minimum(n, ns - 1)

    cur = pl.BlockSpec((None, rows_q, GROUP_WIDTH), lambda r, n: (r, clamp(n), 0))
    prev = pl.BlockSpec((None, ATTN_BLOCK, GROUP_WIDTH), lambda r, n: (r, jnp.maximum(clamp(n) * qb - 1, 0), 0))
    lag = pl.BlockSpec((None, rows_q, GROUP_WIDTH), lambda r, n: (r, jnp.maximum(n - 1, 0), 0))
    full = pl.BlockSpec((2, HEADS_PER_GROUP, ATTN_BLOCK, 2 * ATTN_BLOCK), lambda r, n: (0, 0, 0, 0))
    acc = pl.BlockSpec((HEADS_PER_GROUP, ATTN_BLOCK, 2 * ATTN_BLOCK), lambda r, n: (0, 0, 0))
    return pl.pallas_call(
        body, name=name, grid=(d, ns + 1),
        in_specs=[cur, prev, cur, prev, cur, cur, cur, cur, full],
        out_specs=[cur, lag, lag, acc],
        out_shape=[jax.ShapeDtypeStruct((d, M, GROUP_WIDTH), BF16)] * 3
        + [jax.ShapeDtypeStruct((HEADS_PER_GROUP, ATTN_BLOCK, 2 * ATTN_BLOCK), F32)],
        scratch_shapes=[pltpu.VMEM((rows_q, GROUP_WIDTH), F32), pltpu.VMEM((rows_q, GROUP_WIDTH), F32),
                        pltpu.VMEM((rows_q + ATTN_BLOCK, GROUP_WIDTH), F32),
                        pltpu.VMEM((rows_q + ATTN_BLOCK, GROUP_WIDTH), F32)],
        compiler_params=_params(),
    )(q, k, k, v, v, do, lse, delta, bias)


def _disc_math(a_re, a_im, ldt, b_re, b_im):
    dt = jnp.exp(ldt)
    mag = jnp.exp(a_re * dt)
    ab_re = mag * jnp.cos(a_im * dt)
    ab_im = mag * jnp.sin(a_im * dt)
    den = a_re * a_re + a_im * a_im
    xr = ab_re - 1.0
    coef_re = (xr * a_re + ab_im * a_im) / den
    coef_im = (ab_im * a_re - xr * a_im) / den
    return ab_re, ab_im, coef_re * b_re - coef_im * b_im, coef_re * b_im + coef_im * b_re


def _block_diag_mask():
    row_g = lax.broadcasted_iota(jnp.int32, (SSM_WIDTH, 2 * NS), 0) // SSM_GROUP
    col = lax.broadcasted_iota(jnp.int32, (SSM_WIDTH, 2 * NS), 1)
    col_g = jnp.where(col >= NS, col - NS, col) // SSM_STATE
    return row_g == col_g


def _disc_fwd(a_re, a_im, ldt, b_re, b_im, c_re, c_im):
    def body(are_ref, aim_ref, ldt_ref, bre_ref, bim_ref, cre_ref, cim_ref, pw_ref, pwr_ref, bd_ref, cdt_ref):
        ab_re, ab_im, bb_re, bb_im = _disc_math(are_ref[...], aim_ref[...], ldt_ref[...], bre_ref[...], bim_ref[...])
        row = lax.broadcasted_iota(jnp.int32, (8, NS), 0)
        pr, pi = ab_re, ab_im
        t_re = jnp.zeros((8, NS), F32)
        t_im = jnp.zeros((8, NS), F32)
        u_re = jnp.zeros((8, NS), F32)
        u_im = jnp.zeros((8, NS), F32)
        for j in range(8):
            t_re = jnp.where(row == j, pr, t_re)
            t_im = jnp.where(row == j, pi, t_im)
            u_re = jnp.where(row == 7 - j, pr, u_re)
            u_im = jnp.where(row == 7 - j, pi, u_im)
            pr, pi = pr * ab_re - pi * ab_im, pr * ab_im + pi * ab_re
        pw_ref[0] = t_re
        pw_ref[1] = t_im
        pwr_ref[0] = u_re
        pwr_ref[1] = u_im
        mask = _block_diag_mask()
        zero = jnp.zeros((SSM_WIDTH, 2 * NS), F32)
        bfull = jnp.concatenate([jnp.concatenate([bb_re] * SSM_GROUPS, axis=0),
                                 jnp.concatenate([bb_im] * SSM_GROUPS, axis=0)], axis=1)
        bd_ref[...] = jnp.where(mask, bfull, zero).astype(BF16)
        cfull = jnp.concatenate([jnp.concatenate([cre_ref[...]] * SSM_GROUPS, axis=0),
                                 jnp.concatenate([-cim_ref[...]] * SSM_GROUPS, axis=0)], axis=1)
        cdt_ref[...] = jnp.where(mask, cfull, zero).astype(BF16)

    return pl.pallas_call(
        body, name="s5_disc_fwd",
        in_specs=[_whole()] * 7, out_specs=[_whole()] * 4,
        out_shape=[jax.ShapeDtypeStruct((2, 8, NS), F32), jax.ShapeDtypeStruct((2, 8, NS), F32),
                   jax.ShapeDtypeStruct((SSM_WIDTH, 2 * NS), BF16), jax.ShapeDtypeStruct((SSM_WIDTH, 2 * NS), BF16)],
        compiler_params=_params(),
    )(a_re, a_im, ldt, b_re, b_im, c_re, c_im)


def _disc_bwd(a_re, a_im, ldt, b_re, b_im, d_bd, d_cdt, d_ab, group_sum):
    def body(are_ref, aim_ref, ldt_ref, bre_ref, bim_ref, dbd_ref, dcdt_ref, dab_ref, gs_ref,
             dare_ref, daim_ref, dldt_ref, dbre_ref, dbim_ref, dcre_ref, dcim_ref):
        col = lax.broadcasted_iota(jnp.int32, (SSM_GROUP, 2 * NS), 1)
        col_g = jnp.where(col >= NS, col - NS, col) // SSM_STATE
        acc_b = jnp.zeros((SSM_GROUP, 2 * NS), F32)
        acc_c = jnp.zeros((SSM_GROUP, 2 * NS), F32)
        for g in range(SSM_GROUPS):
            rows = slice(g * SSM_GROUP, (g + 1) * SSM_GROUP)
            acc_b = acc_b + jnp.where(col_g == g, dbd_ref[rows, :], 0.0)
            acc_c = acc_c + jnp.where(col_g == g, dcdt_ref[rows, :], 0.0)
        dcre_ref[...] = acc_c[:, :NS]
        dcim_ref[...] = -acc_c[:, NS:]
        dab_re = jnp.sum(dab_ref[0], axis=0, keepdims=True)
        dab_im = jnp.sum(dab_ref[1], axis=0, keepdims=True)
        _, vjp = jax.vjp(_disc_math, are_ref[...], aim_ref[...], ldt_ref[...], bre_ref[...], bim_ref[...])
        d_are, d_aim, d_ldt, d_bre, d_bim = vjp((dab_re, dab_im, acc_b[:, :NS], acc_b[:, NS:]))
        dare_ref[...] = d_are
        daim_ref[...] = d_aim
        dbre_ref[...] = d_bre
        dbim_ref[...] = d_bim
        dldt_ref[...] = _dot_exact(jnp.broadcast_to(d_ldt, (8, NS)), gs_ref[...])

    vec = jax.ShapeDtypeStruct((1, NS), F32)
    mat = jax.ShapeDtypeStruct((SSM_GROUP, NS), F32)
    return pl.pallas_call(
        body, name="s5_disc_bwd",
        in_specs=[_whole()] * 9, out_specs=[_whole()] * 7,
        out_shape=[vec, vec, jax.ShapeDtypeStruct((8, 128), F32), mat, mat, mat, mat],
        compiler_params=_params(),
    )(a_re, a_im, ldt, b_re, b_im, d_bd, d_cdt, d_ab, group_sum)


def _scan_blocks(buf, pw_ref, carry_ref, n_blocks, reverse):
    row = lax.broadcasted_iota(jnp.int32, (8, SCAN_LANES), 0)
    for lc in range(NS // SCAN_LANES):
        re_cols = pl.ds(lc * SCAN_LANES, SCAN_LANES)
        im_cols = pl.ds(NS + lc * SCAN_LANES, SCAN_LANES)
        pr = pw_ref[0, :, re_cols]
        pi = pw_ref[1, :, re_cols]
        if reverse:
            pi = -pi
            base = [(7, 1), (6, 2), (4, 4)]
            coef = [(jnp.where(row < 8 - k, pr[j:j + 1], 0.0), jnp.where(row < 8 - k, pi[j:j + 1], 0.0), 8 - k)
                    for j, k in base]
        else:
            base = [(0, 1), (1, 2), (3, 4)]
            coef = [(jnp.where(row >= k, pr[j:j + 1], 0.0), jnp.where(row >= k, pi[j:j + 1], 0.0), k)
                    for j, k in base]

        def step(i, carry, pr=pr, pi=pi, coef=coef, re_cols=re_cols, im_cols=im_cols):
            cr, ci = carry
            blk = (n_blocks - 1 - i) if reverse else i
            rows = pl.ds(pl.multiple_of(blk * 8, 8), 8)
            xr = buf[rows, re_cols]
            xi = buf[rows, im_cols]
            for kr, ki, shift in coef:
                sr = pltpu.roll(xr, shift, 0)
                si = pltpu.roll(xi, shift, 0)
                xr, xi = xr + kr * sr - ki * si, xi + kr * si + ki * sr
            xr, xi = xr + pr * cr - pi * ci, xi + pr * ci + pi * cr
            buf[rows, re_cols] = xr
            buf[rows, im_cols] = xi
            edge = slice(0, 1) if reverse else slice(7, 8)
            return xr[edge], xi[edge]

        cr, ci = lax.fori_loop(0, n_blocks, step, (carry_ref[0:1, re_cols], carry_ref[0:1, im_cols]))
        carry_ref[0:1, re_cols] = cr
        carry_ref[0:1, im_cols] = ci


_SUPER_GROUPS = 16
_SUPER_BLOCKS = [
    (slice(k * _SUPER_GROUPS * SSM_GROUP, (k + 1) * _SUPER_GROUPS * SSM_GROUP),
     [slice(half + k * _SUPER_GROUPS * SSM_STATE, half + (k + 1) * _SUPER_GROUPS * SSM_STATE) for half in (0, NS)])
    for k in range(SSM_GROUPS // _SUPER_GROUPS)]


def _ssm_fwd(u, bd, cdt, d_skip, pw):
    L = u.shape[0]
    tc = min(SSM_CHUNK, L)

    def body(u_ref, bd_ref, cdt_ref, dsk_ref, pw_ref, y_ref, s_ref, carry_ref):
        @pl.when(pl.program_id(0) == 0)
        def _():
            carry_ref[...] = jnp.zeros_like(carry_ref)

        uv = u_ref[...]
        u16 = uv.astype(BF16)
        for ch, states in _SUPER_BLOCKS:
            for st in states:
                s_ref[:, st] = _dot(u16[:, ch], bd_ref[ch, st])
        _scan_blocks(s_ref, pw_ref, carry_ref, tc // 8, reverse=False)
        for ch, states in _SUPER_BLOCKS:
            y_ref[:, ch] = (sum(_dot_nt(s_ref[:, st].astype(BF16), cdt_ref[ch, st]) for st in states)
                            + dsk_ref[:, ch] * uv[:, ch])

    return pl.pallas_call(
        body, name="s5_fwd", grid=(L // tc,),
        in_specs=[_rows(tc, SSM_WIDTH), _whole(), _whole(), _whole(), _whole()],
        out_specs=[_rows(tc, SSM_WIDTH), _rows(tc, 2 * NS)],
        out_shape=[jax.ShapeDtypeStruct((L, SSM_WIDTH), F32), jax.ShapeDtypeStruct((L, 2 * NS), F32)],
        scratch_shapes=[pltpu.VMEM((8, 2 * NS), F32)],
        compiler_params=_params(),
    )(u, bd, cdt, d_skip, pw)


def _ssm_bwd(dy, u, s, bd, cdt, d_skip, pwr):
    L = u.shape[0]
    tc = min(SSM_CHUNK, L)
    nc = L // tc
    blocks = tc // 8

    def body(dy_ref, u_ref, s_ref, sprev_ref, bd_ref, cdt_ref, dsk_ref, pwr_ref,
             du_ref, ddsk_ref, dbd_ref, dcdt_ref, dab_ref, g_ref, sx_ref, carry_ref):
        i = pl.program_id(0)

        @pl.when(i == 0)
        def _():
            carry_ref[...] = jnp.zeros_like(carry_ref)
            ddsk_ref[...] = jnp.zeros_like(ddsk_ref)
            dbd_ref[...] = jnp.zeros_like(dbd_ref)
            dcdt_ref[...] = jnp.zeros_like(dcdt_ref)
            dab_ref[...] = jnp.zeros_like(dab_ref)

        dyv = dy_ref[...]
        uv = u_ref[...]
        dy16 = dyv.astype(BF16)
        u16 = uv.astype(BF16)
        for ch, states in _SUPER_BLOCKS:
            for st in states:
                g_ref[:, st] = _dot(dy16[:, ch], cdt_ref[ch, st])
        _scan_blocks(g_ref, pwr_ref, carry_ref, blocks, reverse=True)
        ddsk_ref[...] += jnp.sum(dyv * uv, axis=0, keepdims=True)
        for ch, states in _SUPER_BLOCKS:
            du = dsk_ref[:, ch] * dyv[:, ch]
            for st in states:
                g16 = g_ref[:, st].astype(BF16)
                du = du + _dot_nt(g16, bd_ref[ch, st])
                dbd_ref[ch, st] += _dot_tn(u16[:, ch], g16)
                dcdt_ref[ch, st] += _dot_tn(dy16[:, ch], s_ref[:, st].astype(BF16))
            du_ref[:, ch] = du

        sx_ref[pl.ds(8, tc), :] = s_ref[...]
        sx_ref[pl.ds(0, 8), :] = jnp.where(i == nc - 1, 0.0, sprev_ref[...])
        row = lax.broadcasted_iota(jnp.int32, (8, SCAN_LANES), 0)
        for lc in range(NS // SCAN_LANES):
            re_cols = pl.ds(lc * SCAN_LANES, SCAN_LANES)
            im_cols = pl.ds(NS + lc * SCAN_LANES, SCAN_LANES)

            def step(b, acc, re_cols=re_cols, im_cols=im_cols):
                ar, ai = acc
                off = pl.multiple_of(b * 8, 8)
                gr = g_ref[pl.ds(off, 8), re_cols]
                gi = g_ref[pl.ds(off, 8), im_cols]
                before = pl.ds(off, 8)
                here = pl.ds(off + 8, 8)
                sr = jnp.where(row == 0, sx_ref[before, re_cols][7:8], pltpu.roll(sx_ref[here, re_cols], 1, 0))
                si = jnp.where(row == 0, sx_ref[before, im_cols][7:8], pltpu.roll(sx_ref[here, im_cols], 1, 0))
                return ar + gr * sr + gi * si, ai + gi * sr - gr * si

            zero = jnp.zeros((8, SCAN_LANES), F32)
            ar, ai = lax.fori_loop(0, blocks, step, (zero, zero))
            dab_ref[0, :, re_cols] += ar
            dab_ref[1, :, re_cols] += ai

    rev = lambda i: (nc - 1 - i, 0)
    sprev = pl.BlockSpec((8, 2 * NS), lambda i: (jnp.maximum((nc - 1 - i) * blocks - 1, 0), 0))
    return pl.pallas_call(
        body, name="s5_bwd", grid=(nc,),
        in_specs=[pl.BlockSpec((tc, SSM_WIDTH), rev), pl.BlockSpec((tc, SSM_WIDTH), rev),
                  pl.BlockSpec((tc, 2 * NS), rev), sprev, _whole(), _whole(), _whole(), _whole()],
        out_specs=[pl.BlockSpec((tc, SSM_WIDTH), rev), _whole(), _whole(), _whole(), _whole()],
        out_shape=[jax.ShapeDtypeStruct((L, SSM_WIDTH), F32), jax.ShapeDtypeStruct((1, SSM_WIDTH), F32),
                   jax.ShapeDtypeStruct((SSM_WIDTH, 2 * NS), F32), jax.ShapeDtypeStruct((SSM_WIDTH, 2 * NS), F32),
                   jax.ShapeDtypeStruct((2, 8, NS), F32)],
        scratch_shapes=[pltpu.VMEM((tc, 2 * NS), F32), pltpu.VMEM((tc + 8, 2 * NS), F32), pltpu.VMEM((8, 2 * NS), F32)],
        compiler_params=_params(),
    )(dy, u, s, s, bd, cdt, d_skip, pwr)


def _branches(o_attn, y, gates, w_ab, w_glu, w_sb):
    ya = _dot(o_attn.astype(BF16), w_ab[...])
    gel = _gelu(y)
    glu = _dot(gel.astype(BF16), w_glu[...])
    p = glu[:, :SSM_WIDTH]
    sg = _sigmoid(glu[:, SSM_WIDTH:])
    ys2 = p * sg
    ysb = _dot(ys2.astype(BF16), w_sb[...])
    ga = gates[:, :D_MODEL]
    gs = gates[:, D_MODEL:]
    return ya, gel, p, sg, ys2, ysb, ga, gs


def _mix_out_fwd(x1, o_g, lse_g, y, gates, w_ab, w_glu, w_sb, w_out):
    L = x1.shape[0]
    tm = min(ROW_TILE, L)

    def body(x_ref, o0, o1, o2, l0, l1, l2, y_ref, gate_ref, wab_ref, wglu_ref, wsb_ref, wout_ref,
             x2_ref, oat_ref, lse0, lse1, lse2, scr):
        la, lb, lc = (_from_residues(ref, scr, d) for ref, d in zip((l0, l1, l2), DILATIONS))
        m = jnp.maximum(jnp.maximum(la, lb), lc)
        ea, eb, ec = jnp.exp(la - m), jnp.exp(lb - m), jnp.exp(lc - m)
        tot = ea + eb + ec
        oa, ob, oc = (_from_residues(ref, scr, d) for ref, d in zip((o0, o1, o2), DILATIONS))
        o_attn = (ea * oa + eb * ob + ec * oc) / tot
        oat_ref[...] = o_attn
        lse = m + jnp.log(tot)
        for ref, d in zip((lse0, lse1, lse2), DILATIONS):
            _to_residues(lse, ref, scr, d)
        ya, _, _, _, _, ysb, ga, gs = _branches(o_attn, y_ref[...], gate_ref[...], wab_ref, wglu_ref, wsb_ref)
        mix = ga * ya + gs * ysb
        x2_ref[...] = x_ref[...] + _dot(mix.astype(BF16), wout_ref[...])

    res = [_residue_spec(d, tm) for d in DILATIONS]
    return pl.pallas_call(
        body, name="mix_out_fwd", grid=(L // tm,),
        in_specs=[_rows(tm, D_MODEL)] + res * 2 + [_rows(tm, SSM_WIDTH), _rows(tm, 2 * D_MODEL)] + [_whole()] * 4,
        out_specs=[_rows(tm, D_MODEL), _rows(tm, GROUP_WIDTH)] + res,
        out_shape=[jax.ShapeDtypeStruct((L, D_MODEL), F32), jax.ShapeDtypeStruct((L, GROUP_WIDTH), F32)]
        + [_residue_shape(d, L, F32) for d in DILATIONS],
        scratch_shapes=[_residue_scratch(tm)],
        compiler_params=_params(),
    )(x1, *o_g, *lse_g, y, gates, w_ab, w_glu, w_sb, w_out)


def _mix_out_bwd(dx2, o_attn, y, gates, w_ab, w_glu, w_sb, w_out, head_sum):
    L = dx2.shape[0]
    tm = min(ROW_TILE, L)

    def body(dx_ref, oat_ref, y_ref, gate_ref, wab_ref, wglu_ref, wsb_ref, wout_ref, hs_ref,
             do0, do1, do2, dl0, dl1, dl2, dy_ref, dgp_ref, mix_ref, dya_ref, dys_ref, ys2_ref, gel_ref, dglu_ref,
             dgb_ref, scr):
        i = pl.program_id(0)
        o_attn = oat_ref[...]
        yv = y_ref[...]
        ya, gel, p, sg, ys2, ysb, ga, gs = _branches(o_attn, yv, gate_ref[...], wab_ref, wglu_ref, wsb_ref)
        mix_ref[...] = (ga * ya + gs * ysb).astype(BF16)
        ys2_ref[...] = ys2.astype(BF16)
        gel_ref[...] = gel.astype(BF16)
        dmix = _dot_nt(dx_ref[...].astype(BF16), wout_ref[...])
        dgp = jnp.concatenate([dmix * ya * ga * (1.0 - ga), dmix * ysb * gs * (1.0 - gs)], axis=1)
        dgp_ref[...] = dgp.astype(BF16)

        @pl.when(i == 0)
        def _():
            dgb_ref[...] = jnp.zeros_like(dgb_ref)

        dgb_ref[...] += jnp.sum(dgp, axis=0, keepdims=True)
        dya = (dmix * ga).astype(BF16)
        dys = (dmix * gs).astype(BF16)
        dya_ref[...] = dya
        dys_ref[...] = dys
        d_o = _dot_nt(dya, wab_ref[...])
        delta = _dot_exact(d_o * o_attn, hs_ref[...])
        for do_ref, dl_ref, d in zip((do0, do1, do2), (dl0, dl1, dl2), DILATIONS):
            _to_residues(d_o, do_ref, scr, d)
            _to_residues(delta, dl_ref, scr, d)
        dys2 = _dot_nt(dys, wsb_ref[...])
        dglu = jnp.concatenate([dys2 * sg, dys2 * p * sg * (1.0 - sg)], axis=1).astype(BF16)
        dglu_ref[...] = dglu
        dy_ref[...] = _dot_nt(dglu, wglu_ref[...]) * _gelu_grad(yv)

    grp = _rows(tm, GROUP_WIDTH)
    wide = _rows(tm, D_MODEL)
    half = _rows(tm, SSM_WIDTH)
    res = [_residue_spec(d, tm) for d in DILATIONS]
    sds = jax.ShapeDtypeStruct
    return pl.pallas_call(
        body, name="mix_out_bwd", grid=(L // tm,),
        in_specs=[wide, grp, half, _rows(tm, 2 * D_MODEL)] + [_whole()] * 5,
        out_specs=res + res + [half, _rows(tm, 2 * D_MODEL), wide, wide, wide, half, half, wide, _acc_row(2 * D_MODEL)],
        out_shape=[_residue_shape(d, L, BF16) for d in DILATIONS] + [_residue_shape(d, L, F32) for d in DILATIONS]
        + [sds((L, SSM_WIDTH), F32),
           sds((L, 2 * D_MODEL), BF16), sds((L, D_MODEL), BF16), sds((L, D_MODEL), BF16),
           sds((L, D_MODEL), BF16), sds((L, SSM_WIDTH), BF16), sds((L, SSM_WIDTH), BF16),
           sds((L, D_MODEL), BF16), sds((1, 2 * D_MODEL), F32)],
        scratch_shapes=[_residue_scratch(tm)],
        compiler_params=_params(),
    )(dx2, o_attn, y, gates, w_ab, w_glu, w_sb, w_out, head_sum)


def _adamw(w, g, m, v, name):
    R, C = w.shape
    tr = _row_tile(R, max(8, ADAMW_BLOCK_BYTES // (4 * C)))

    def body(w_ref, g_ref, m_ref, v_ref, d_ref, mo_ref, vo_ref):
        gv = g_ref[...]
        mn = ADAM_B1 * m_ref[...] + (1.0 - ADAM_B1) * gv
        vn = ADAM_B2 * v_ref[...] + (1.0 - ADAM_B2) * (gv * gv)
        m_hat = mn / (1.0 - ADAM_B1 ** ADAM_STEP)
        v_hat = vn / (1.0 - ADAM_B2 ** ADAM_STEP)
        d_ref[...] = -ADAM_LR * (m_hat / (jnp.sqrt(v_hat) + ADAM_EPS) + ADAM_WD * w_ref[...])
        mo_ref[...] = mn
        vo_ref[...] = vn

    blk = pl.BlockSpec((tr, C), lambda i: (i, 0))
    return pl.pallas_call(
        body, name=name, grid=(R // tr,),
        in_specs=[blk] * 4, out_specs=[blk] * 3,
        out_shape=[jax.ShapeDtypeStruct((R, C), F32)] * 3,
        compiler_params=_params(),
    )(w, g, m, v)


def _sum_chips_into_half(u, t, name):
    S, H, C = u.shape
    tr = _row_tile(H, 512)
    hb = H // tr

    def body(s_ref, t_ref, a_ref, b_ref, c_ref, o_ref):
        me = s_ref[1]
        others = (a_ref[...], b_ref[...], c_ref[...])
        acc = None
        for chip in range(S):
            below = others[min(chip, S - 2)]
            above = others[max(chip - 1, 0)]
            term = jnp.where(me == chip, t_ref[...], jnp.where(me > chip, below, above)).astype(F32)
            acc = term if acc is None else acc + term
        o_ref[...] = acc

    x, y, c = lax.axis_index("x"), lax.axis_index("y"), lax.axis_index("c")
    me = 2 * x + y
    scalars = jnp.stack([c, me] + [j + (j >= me).astype(jnp.int32) for j in range(S - 1)]).astype(jnp.int32)
    blk = (None, tr, C)
    return pl.pallas_call(
        body, name=name,
        grid_spec=pltpu.PrefetchScalarGridSpec(
            num_scalar_prefetch=1, grid=(hb,),
            in_specs=[pl.BlockSpec(blk, lambda i, s: (s[1], i, 0))]
            + [pl.BlockSpec(blk, functools.partial(lambda j, i, s: (s[2 + j], i, 0), j)) for j in range(S - 1)],
            out_specs=pl.BlockSpec((tr, C), lambda i, s: (s[0] * hb + i, 0))),
        out_shape=jax.ShapeDtypeStruct((2 * H, C), F32),
        compiler_params=_params(),
    )(scalars, t, u, u, u)


def _add_halves(g, r1, name):
    S, R, C = g.shape
    H = R // 2
    tr = _row_tile(H, 512)
    hb = H // tr

    def body(c_ref, g_ref, r_ref, o_ref):
        o_ref[...] = (g_ref[...] + r_ref[...]).astype(BF16)

    core = lax.axis_index("c").astype(jnp.int32).reshape(1)
    return pl.pallas_call(
        body, name=name,
        grid_spec=pltpu.PrefetchScalarGridSpec(
            num_scalar_prefetch=1, grid=(S, hb),
            in_specs=[pl.BlockSpec((None, tr, C), lambda j, i, c_ref: (j, c_ref[0] * hb + i, 0)),
                      pl.BlockSpec((None, tr, C), lambda j, i, c_ref: (j, i, 0))],
            out_specs=pl.BlockSpec((None, tr, C), lambda j, i, c_ref: (j, i, 0))),
        out_shape=jax.ShapeDtypeStruct((S, H, C), BF16),
        compiler_params=_params(),
    )(core, g, r1)


_ANY = pl.BlockSpec(memory_space=pl.ANY)


def _place():
    x, y, c = lax.axis_index("x"), lax.axis_index("y"), lax.axis_index("c")
    chips = [(1 - x, y), (x, 1 - y), (1 - x, 1 - y)]
    return x, y, c, chips


def _comm_call(body, name, ins, out_shapes, n_remote, n_local):
    return pl.pallas_call(
        body, name=name,
        in_specs=[_ANY] * len(ins), out_specs=[_ANY] * len(out_shapes), out_shape=out_shapes,
        scratch_shapes=[pltpu.SemaphoreType.DMA((n_remote,)), pltpu.SemaphoreType.DMA((n_remote,)),
                        pltpu.SemaphoreType.DMA((max(n_local, 1),))],
    )(*ins)


def _remote(src, dst, send_sems, recv_sems, k, device):
    return pltpu.make_async_remote_copy(src_ref=src, dst_ref=dst, send_sem=send_sems.at[k], recv_sem=recv_sems.at[k],
                                        device_id=device, device_id_type=MESH)


def _gather_parts(shapes, w_refs, out_refs, send_sems, recv_sems):
    n = len(shapes)
    x, y, c, chips = _place()
    me = 2 * x + y
    sibling = (x, y, 1 - c)

    def half(k, chip_idx, core):
        H = shapes[k][0] // 2
        return out_refs[k].at[chip_idx, pl.ds(core * H, H), :]

    mine = [_remote(w_refs[k], out_refs[k].at[me], send_sems, recv_sems, 6 * n + k, sibling) for k in range(n)]
    first = []
    for k in range(n):
        H = shapes[k][0] // 2
        for j, (cx, cy) in enumerate(chips):
            first.append(_remote(w_refs[k].at[pl.ds(c * H, H), :], half(k, me, c), send_sems, recv_sems,
                                 3 * k + j, (cx, cy, c)))

    def start():
        for cp in mine + first:
            cp.start()

    def finish():
        passed = []
        for k in range(n):
            for j, (cx, cy) in enumerate(chips):
                landed = half(k, 2 * cx + cy, c)
                _remote(landed, landed, send_sems, recv_sems, 3 * k + j, (cx, cy, c)).wait_recv()
                fwd = _remote(landed, landed, send_sems, recv_sems, 3 * n + 3 * k + j, sibling)
                fwd.start()
                passed.append(fwd)
        for k in range(n):
            for j, (cx, cy) in enumerate(chips):
                other = half(k, 2 * cx + cy, 1 - c)
                _remote(other, other, send_sems, recv_sems, 3 * n + 3 * k + j, sibling).wait_recv()
        for cp in mine:
            cp.wait_recv()
        for cp in first + passed + mine:
            cp.wait_send()

    return start, finish


def _gather_weights(shards, name):
    n = len(shards)

    def body(*refs):
        start, finish = _gather_parts([w.shape for w in shards], refs[:n], refs[n:2 * n], *refs[2 * n:2 * n + 2])
        start()
        finish()

    return _comm_call(body, name, shards,
                      [jax.ShapeDtypeStruct((N_SHARD,) + w.shape, w.dtype) for w in shards], 7 * n, 0)


def _handshake(peers):
    barrier = pltpu.get_barrier_semaphore()
    for peer in peers:
        pl.semaphore_signal(barrier, inc=1, device_id=peer, device_id_type=MESH)
    pl.semaphore_wait(barrier, len(peers))


def _sequenced(body, name, ins, out_shapes, n_sems, collective_id):
    return pl.kernel(
        body, out_type=list(out_shapes), mesh=plsc.ScalarSubcoreMesh(axis_name="sequencer", num_cores=1), name=name,
        scratch_types=(pltpu.SemaphoreType.DMA((n_sems,)), pltpu.SemaphoreType.DMA((n_sems,))),
        compiler_params=pltpu.CompilerParams(collective_id=collective_id))(*ins)


def _swap_halves(gs, name, collective_id):
    n = len(gs)

    def body(*refs):
        g_refs, out_refs = refs[:n], refs[n:2 * n]
        send_sems, recv_sems = refs[2 * n:]
        x, y, c, _ = _place()
        _handshake([(x, y, 1 - c)])
        cps = []
        for k in range(n):
            H = gs[k].shape[1] // 2
            cp = _remote(g_refs[k].at[:, pl.ds((1 - c) * H, H), :], out_refs[k], send_sems, recv_sems, k, (x, y, 1 - c))
            cp.start()
            cps.append(cp)
        for cp in cps:
            cp.wait()

    return _sequenced(body, name, gs, [jax.ShapeDtypeStruct((g.shape[0], g.shape[1] // 2, g.shape[2]), g.dtype)
                                       for g in gs], n, collective_id)


def _exchange_chips(ts, name, collective_id):
    n = len(ts)

    def body(*refs):
        t_refs, out_refs = refs[:n], refs[n:2 * n]
        send_sems, recv_sems = refs[2 * n:]
        x, y, c, chips = _place()
        me = 2 * x + y
        _handshake([(cx, cy, c) for cx, cy in chips])
        sent = []
        for k in range(n):
            for j, (cx, cy) in enumerate(chips):
                cp = _remote(t_refs[k].at[2 * cx + cy], out_refs[k].at[me], send_sems, recv_sems, 3 * k + j, (cx, cy, c))
                cp.start()
                sent.append(cp)
        for k in range(n):
            for j, (cx, cy) in enumerate(chips):
                slot = out_refs[k].at[2 * cx + cy]
                _remote(slot, slot, send_sems, recv_sems, 3 * k + j, (cx, cy, c)).wait_recv()
        for cp in sent:
            cp.wait_send()

    return _sequenced(body, name, ts, [jax.ShapeDtypeStruct(t.shape, t.dtype) for t in ts], 3 * n, collective_id)


def _join_halves(fs, name):
    n = len(fs)

    def body(*refs):
        out_refs = refs[n:2 * n]
        send_sems, recv_sems, _ = refs[2 * n:]
        x, y, c, _ = _place()
        sent = []
        for k in range(n):
            H = fs[k].shape[0] // 2
            here = out_refs[k].at[pl.ds(c * H, H), :]
            cp = _remote(here, here, send_sems, recv_sems, k, (x, y, 1 - c))
            cp.start()
            sent.append(cp)
        for k in range(n):
            H = fs[k].shape[0] // 2
            other = out_refs[k].at[pl.ds((1 - c) * H, H), :]
            _remote(other, other, send_sems, recv_sems, k, (x, y, 1 - c)).wait_recv()
        for cp in sent:
            cp.wait_send()

    return pl.pallas_call(
        body, name=name,
        in_specs=[_ANY] * n, out_specs=[_ANY] * n,
        out_shape=[jax.ShapeDtypeStruct(f.shape, f.dtype) for f in fs],
        input_output_aliases={k: k for k in range(n)},
        scratch_shapes=[pltpu.SemaphoreType.DMA((n,)), pltpu.SemaphoreType.DMA((n,)), pltpu.SemaphoreType.DMA((1,))],
    )(*fs)


def _gather_small(v):
    R, C = v.shape

    def body(v_ref, out_ref, send_sems, recv_sems):
        x, y, c, _ = _place()
        me = 4 * x + 2 * y + c
        flips = [(fx, fy, fc) for fx in (0, 1) for fy in (0, 1) for fc in (0, 1)][1:]
        peers = [((1 - x) if fx else x, (1 - y) if fy else y, (1 - c) if fc else c) for fx, fy, fc in flips]
        _handshake(peers)
        sent = []
        for j, peer in enumerate(peers):
            cp = _remote(v_ref, out_ref.at[me], send_sems, recv_sems, j, peer)
            cp.start()
            sent.append(cp)
        for j, peer in enumerate(peers):
            slot = out_ref.at[4 * peer[0] + 2 * peer[1] + peer[2]]
            _remote(slot, slot, send_sems, recv_sems, j, peer).wait_recv()
        for cp in sent:
            cp.wait_send()

    return _sequenced(body, "gather_small", [v], [jax.ShapeDtypeStruct((8, R, C), F32)], 7,
                      COLLECTIVE_IDS["gather_small"])[0]


def _sum_devices(x, own, name):
    S, R, C = x.shape
    tr = _row_tile(R, 2048)

    def body(s_ref, x_ref, own_ref, o_ref):
        me = s_ref[0]
        acc = None
        for k in range(S):
            term = jnp.where(me == k, own_ref[...], x_ref[k])
            acc = term if acc is None else acc + term
        o_ref[...] = acc

    x_, y_, c_ = lax.axis_index("x"), lax.axis_index("y"), lax.axis_index("c")
    me = (4 * x_ + 2 * y_ + c_).astype(jnp.int32).reshape(1)
    return pl.pallas_call(
        body, name=name,
        grid_spec=pltpu.PrefetchScalarGridSpec(
            num_scalar_prefetch=1, grid=(R // tr,),
            in_specs=[pl.BlockSpec((S, tr, C), lambda i, s: (0, i, 0)), pl.BlockSpec((tr, C), lambda i, s: (i, 0))],
            out_specs=pl.BlockSpec((tr, C), lambda i, s: (i, 0))),
        out_shape=jax.ShapeDtypeStruct((R, C), F32),
        compiler_params=_params(),
    )(me, x, own)


def _after(earlier, arrays):
    return lax.optimization_barrier((earlier, arrays))


def _reduce_exchange(gs, names, tag, earlier):
    earlier, gs = _after(earlier, gs)
    r1 = _swap_halves(gs, "reduce_swap_" + tag, COLLECTIVE_IDS["swap_" + tag])
    ts = [_add_halves(g, r, "reduce_add_cores_" + nm) for g, r, nm in zip(gs, r1, names)]
    us = _exchange_chips(ts, "reduce_exchange_" + tag, COLLECTIVE_IDS["exchange_" + tag])
    return us, ts, earlier


def _reduce_finish(us, ts, names, tag):
    fs = [_sum_chips_into_half(u, t, "reduce_add_chips_" + nm) for u, t, nm in zip(us, ts, names)]
    return _join_halves(fs, "reduce_join_" + tag)


BIG = ["ffn1_w_gate", "ffn1_w_up", "ffn1_w_down", "w_in", "ssm_w_glu", "w_attn_branch", "w_ssm_branch",
       "w_out", "ffn2_w_gate", "ffn2_w_up", "ffn2_w_down"]
SMALL = ["ffn1_norm", "mix_norm", "gate_bias", "rel_bias_table", "ssm_a_re", "ssm_a_im", "ssm_log_dt",
         "ssm_b_re", "ssm_b_im", "ssm_c_re", "ssm_c_im", "ssm_d", "ffn2_norm", "final_norm"]
ORDER = ["ffn1_norm", "ffn1_w_gate", "ffn1_w_up", "ffn1_w_down", "mix_norm", "w_in", "gate_bias", "rel_bias_table",
         "ssm_a_re", "ssm_a_im", "ssm_log_dt", "ssm_b_re", "ssm_b_im", "ssm_c_re", "ssm_c_im", "ssm_d",
         "ssm_w_glu", "w_attn_branch", "w_ssm_branch", "w_out", "ffn2_norm", "ffn2_w_gate", "ffn2_w_up",
         "ffn2_w_down", "final_norm"]


_SMALL_TILE = 8 * LANES


def _pack_small(arrays):
    rows = []
    for a in arrays:
        flat = a.reshape(-1).astype(F32)
        rows.append(jnp.pad(flat, (0, (-flat.shape[0]) % _SMALL_TILE)).reshape(-1, LANES))
    return jnp.concatenate(rows, axis=0)


def _unpack_small(packed, shapes):
    out, r0 = [], 0
    for shp in shapes:
        n = math.prod(shp)
        rows = 8 * -(-n // _SMALL_TILE)
        out.append(packed[r0:r0 + rows].reshape(-1)[:n].reshape(shp))
        r0 += rows
    return out


def _split_cols(g):
    K, N = g.shape
    return g.reshape(K, N_SHARD, N // N_SHARD).transpose(1, 0, 2)


def _join_cols(w):
    S, K, n = w.shape
    return w.transpose(1, 0, 2).reshape(K, S * n)


COL_SHARDED = ("w_in", "ssm_w_glu", "w_attn_branch", "w_ssm_branch")


class _GradSync:
    def __init__(self, weights, moms, vels):
        self.weights, self.moms, self.vels = weights, moms, vels
        self.grads, self.delta, self.new_m, self.new_v = {}, {}, {}, {}
        self.loss = None
        self._earlier = []
        self._exchanged = {}

    def grads_ready(self, tag, gw):
        names = REDUCE_GROUPS[tag]
        gs = []
        for n in names:
            g = gw[n]
            if n in COL_SHARDED:
                g = _split_cols(g)
            elif n == "w_out":
                g = g.reshape(N_SHARD, D_MODEL // N_SHARD, D_MODEL)
            gs.append(g)
        us, ts, _ = _reduce_exchange(gs, names, tag, self._earlier)
        self._exchanged[tag] = (us, ts)
        self._earlier = us

    def small_ready(self, gs, loss_blk):
        _, (mine,) = _after(self._earlier, [_pack_small([gs[n] for n in SMALL] + [loss_blk[0:1, :]])])
        others = _gather_small(mine)
        self._exchanged["small"] = (others, mine)
        self._earlier = [others]

    def finish(self, tag):
        made = []
        if tag == "small":
            others, mine = self._exchanged[tag]
            shapes = [self.weights[n].shape for n in SMALL]
            total = _unpack_small(_sum_devices(others, mine, "sum_small"), shapes + [(128,)])
            self.loss = total[-1][0]
            self.grads.update(zip(SMALL, total[:-1]))
            packed = [_pack_small([src[n] for n in SMALL]) for src in (self.weights, self.grads, self.moms, self.vels)]
            for dst, res in zip((self.delta, self.new_m, self.new_v), _adamw(*packed, "adamw_small")):
                dst.update(zip(SMALL, _unpack_small(res, shapes)))
            for n in SMALL:
                made += [self.grads[n], self.delta[n], self.new_m[n], self.new_v[n]]
            return made + [self.loss]
        names = REDUCE_GROUPS[tag]
        us, ts = self._exchanged[tag]
        for n, g in zip(names, _reduce_finish(us, ts, names, tag)):
            shp = self.weights[n].shape
            two_d = shp[-2:]
            d, m, v = _adamw(self.weights[n].reshape(two_d), g, self.moms[n].reshape(two_d),
                             self.vels[n].reshape(two_d), "adamw_" + n)
            self.grads[n], self.delta[n] = g.reshape(shp), d.reshape(shp)
            self.new_m[n], self.new_v[n] = m.reshape(shp), v.reshape(shp)
            made += [self.grads[n], self.delta[n], self.new_m[n], self.new_v[n]]
        return made

    def finish_all(self):
        for tag in ("ffn2", "mixer", "w_in", "small", "ffn1"):
            self.finish(tag)


def _local_step(x, target, w, later, small, sync):
    L = x.shape[0]
    row = lambda v: v.reshape(1, -1)

    a_re, a_im = small["ssm_a_re"].reshape(1, NS), small["ssm_a_im"].reshape(1, NS)
    ldt = jnp.repeat(small["ssm_log_dt"].reshape(SSM_GROUPS), SSM_STATE).reshape(1, NS)
    to_cn = lambda b: b.reshape(SSM_GROUPS, SSM_STATE, SSM_GROUP).transpose(2, 0, 1).reshape(SSM_GROUP, NS)
    c_to_cn = lambda c: c.reshape(SSM_GROUPS, SSM_GROUP, SSM_STATE).transpose(1, 0, 2).reshape(SSM_GROUP, NS)
    b_re, b_im = to_cn(small["ssm_b_re"]), to_cn(small["ssm_b_im"])
    c_re, c_im = c_to_cn(small["ssm_c_re"]), c_to_cn(small["ssm_c_im"])
    d_skip = row(small["ssm_d"])
    pw, pwr, bd, cdt = _disc_fwd(a_re, a_im, ldt, b_re, b_im, c_re, c_im)

    onehot = _bucket_onehot()
    table_t = small["rel_bias_table"].T.reshape(3, HEADS_PER_GROUP, N_BUCKETS)
    table_t = jnp.pad(table_t, ((0, 0), (0, 8 - HEADS_PER_GROUP), (0, 0)))
    bias = _bias_expand(table_t, onehot)[:, :, :HEADS_PER_GROUP].reshape(
        3, 2, HEADS_PER_GROUP, ATTN_BLOCK, 2 * ATTN_BLOCK)

    n1, nm, n2, nf = row(small["ffn1_norm"]), row(small["mix_norm"]), row(small["ffn2_norm"]), row(small["final_norm"])
    gate_bias = row(small["gate_bias"])

    x1, a1, b1, *later_full = _ffn_fwd(x, n1, w["ffn1_w_gate"], w["ffn1_w_up"], w["ffn1_w_down"], "ffn1_fwd",
                                       carried=list(later.values()))
    w = dict(w, **dict(zip(later, later_full)))
    for n in COL_SHARDED:
        w[n] = _join_cols(w[n])
    w["w_out"] = w["w_out"].reshape(D_MODEL, D_MODEL)
    *qkv, u, gates = _mix_in_fwd(x1, nm, w["w_in"], gate_bias)
    q, k, v = qkv[0:3], qkv[3:6], qkv[6:9]
    o_g, lse_g = [], []
    for grp in range(3):
        o, lse = _attn_fwd(q[grp], k[grp], v[grp], bias[grp], f"attn_fwd_{grp}")
        o_g.append(o)
        lse_g.append(lse)
    y, s = _ssm_fwd(u, bd, cdt, d_skip, pw)
    x2, o_attn, *lse_tot = _mix_out_fwd(x1, o_g, lse_g, y, gates, w["w_attn_branch"], w["ssm_w_glu"],
                                        w["w_ssm_branch"], w["w_out"])
    x3, a2, b2 = _ffn_fwd(x2, n2, w["ffn2_w_gate"], w["ffn2_w_up"], w["ffn2_w_down"], "ffn2_fwd")
    loss_blk, dx3, d_nf = _loss_fwd_bwd(x3, nf, target)

    gw, gs = {}, {}
    gs["final_norm"] = d_nf

    dx2, da, db, sact, h, d_out, gs["ffn2_norm"] = _ffn_bwd(dx3, x2, n2, a2, b2, w["ffn2_w_gate"], w["ffn2_w_up"],
                                                            w["ffn2_w_down"], "ffn2_bwd")
    gw["ffn2_w_gate"] = _matmul_tn(h[None], da, "ffn2_dw_gate")
    gw["ffn2_w_up"] = _matmul_tn(h[None], db, "ffn2_dw_up")
    gw["ffn2_w_down"] = _matmul_tn(sact, d_out[None], "ffn2_dw_down")
    sync.grads_ready("ffn2", gw)

    head_sum = (jnp.arange(GROUP_WIDTH)[:, None] // HEAD_DIM == jnp.arange(GROUP_WIDTH)[None, :] // HEAD_DIM).astype(F32)
    (*d_o_delta, dy, dgp, mix, dya, dys, ys2, gel, dglu, gs["gate_bias"]) = _mix_out_bwd(
        dx2, o_attn, y, gates, w["w_attn_branch"], w["ssm_w_glu"], w["w_ssm_branch"], w["w_out"], head_sum)
    d_o, delta = d_o_delta[0:3], d_o_delta[3:6]
    gw["w_out"] = _matmul_tn(mix[None], dx2[None], "dw_out")[0]
    gw["w_attn_branch"] = _matmul_tn(o_attn[None], dya[None], "dw_attn_branch")[0]
    gw["w_ssm_branch"] = _matmul_tn(ys2[None], dys[None], "dw_ssm_branch")[0]
    gw["ssm_w_glu"] = _matmul_tn(gel[None], dglu[None], "dw_glu")[0]
    sync.grads_ready("mixer", gw)

    dqs, dks, dvs, dsums = [], [], [], []
    for grp in range(3):
        dq, dk, dv, dsum = _attn_bwd(q[grp], k[grp], v[grp], d_o[grp], lse_tot[grp], delta[grp], bias[grp],
                                     f"attn_bwd_{grp}")
        dqs.append(dq)
        dks.append(dk)
        dvs.append(dv)
        dsums.append(dsum.reshape(HEADS_PER_GROUP, -1))
    dsum_all = jnp.pad(jnp.stack(dsums), ((0, 0), (0, 8 - HEADS_PER_GROUP), (0, 0)))
    d_table = _bias_reduce(dsum_all, onehot)[:, :HEADS_PER_GROUP]
    gs["rel_bias_table"] = d_table.reshape(3 * HEADS_PER_GROUP, N_BUCKETS).T

    du, gs["ssm_d"], d_bd, d_cdt, d_ab = _ssm_bwd(dy, u, s, bd, cdt, d_skip, pwr)
    group_sum = (jnp.arange(NS)[:, None] // SSM_STATE == jnp.arange(128)[None, :]).astype(F32)
    d_are, d_aim, d_ldt, d_bre, d_bim, d_cre, d_cim = _disc_bwd(a_re, a_im, ldt, b_re, b_im, d_bd, d_cdt, d_ab, group_sum)
    gs["ssm_a_re"], gs["ssm_a_im"] = d_are, d_aim
    gs["ssm_log_dt"] = d_ldt[0, :SSM_GROUPS]
    from_cn = lambda t: t.reshape(SSM_GROUP, SSM_GROUPS, SSM_STATE).transpose(1, 2, 0)
    c_from_cn = lambda t: t.reshape(SSM_GROUP, SSM_GROUPS, SSM_STATE).transpose(1, 0, 2)
    gs["ssm_b_re"], gs["ssm_b_im"] = from_cn(d_bre), from_cn(d_bim)
    gs["ssm_c_re"], gs["ssm_c_im"] = c_from_cn(d_cre), c_from_cn(d_cim)

    dx1, hm, dz, gs["mix_norm"] = _mix_in_bwd(dx2, x1, nm, dqs + dks + dvs, du, dgp, w["w_in"])
    gw["w_in"] = _matmul_tn(hm[None], dz[None], "dw_in")[0]
    sync.grads_ready("w_in", gw)

    dx0, da, db, sact, h, d_out, gs["ffn1_norm"] = _ffn_bwd(dx1, x, n1, a1, b1, w["ffn1_w_gate"], w["ffn1_w_up"],
                                                            w["ffn1_w_down"], "ffn1_bwd")
    sync.small_ready(gs, loss_blk)
    gw["ffn1_w_gate"] = _matmul_tn(h[None], da, "ffn1_dw_gate")
    gw["ffn1_w_up"] = _matmul_tn(h[None], db, "ffn1_dw_up")
    gw["ffn1_w_down"] = _matmul_tn(sact, d_out[None], "ffn1_dw_down")
    sync.grads_ready("ffn1", gw)
    return dx0


def kernel(x, ffn1_norm, ffn1_w_gate, ffn1_w_up, ffn1_w_down, mix_norm, w_in, gate_bias, rel_bias_table, ssm_a_re, ssm_a_im, ssm_log_dt, ssm_b_re, ssm_b_im, ssm_c_re, ssm_c_im, ssm_d, ssm_w_glu, w_attn_branch, w_ssm_branch, w_out, ffn2_norm, ffn2_w_gate, ffn2_w_up, ffn2_w_down, final_norm, loss_target, m_ffn1_norm, m_ffn1_w_gate, m_ffn1_w_up, m_ffn1_w_down, m_mix_norm, m_w_in, m_gate_bias, m_rel_bias_table, m_ssm_a_re, m_ssm_a_im, m_ssm_log_dt, m_ssm_b_re, m_ssm_b_im, m_ssm_c_re, m_ssm_c_im, m_ssm_d, m_ssm_w_glu, m_w_attn_branch, m_w_ssm_branch, m_w_out, m_ffn2_norm, m_ffn2_w_gate, m_ffn2_w_up, m_ffn2_w_down, m_final_norm, v_ffn1_norm, v_ffn1_w_gate, v_ffn1_w_up, v_ffn1_w_down, v_mix_norm, v_w_in, v_gate_bias, v_rel_bias_table, v_ssm_a_re, v_ssm_a_im, v_ssm_log_dt, v_ssm_b_re, v_ssm_b_im, v_ssm_c_re, v_ssm_c_im, v_ssm_d, v_ssm_w_glu, v_w_attn_branch, v_w_ssm_branch, v_w_out, v_ffn2_norm, v_ffn2_w_gate, v_ffn2_w_up, v_ffn2_w_down, v_final_norm):
    args = dict(locals())
    weights = {n: args[n] for n in ORDER}
    moms = {n: args["m_" + n] for n in ORDER}
    vels = {n: args["v_" + n] for n in ORDER}

    shard2d = {n: weights[n].reshape(weights[n].shape[-2:]) for n in BIG}
    first, rest = BIG[:3], BIG[3:]
    full = dict(zip(first, _gather_weights([shard2d[n].astype(BF16) for n in first], "gather_ffn1_weights")))
    later = {n: shard2d[n].astype(BF16) for n in rest}

    small = {n: weights[n] for n in SMALL}
    sync = _GradSync(weights, moms, vels)
    grad_x = _local_step(x[0], loss_target[0], full, later, small, sync)
    sync.finish_all()
    return (sync.loss, grad_x[None], *[sync.grads[n] for n in ORDER], *[sync.delta[n] for n in ORDER],
            *[sync.new_m[n] for n in ORDER], *[sync.new_v[n] for n in ORDER])
```

```python
import functools
import math

import jax
import jax.numpy as jnp
from jax import lax
from jax.experimental import pallas as pl
from jax.experimental.pallas import tpu as pltpu
from jax.experimental.pallas import tpu_sc as plsc

F32 = jnp.float32
BF16 = jnp.bfloat16
MESH = pl.DeviceIdType.MESH

D_MODEL = 1024
D_FF = 2816
HEAD_DIM = 64
HEADS_PER_GROUP = 4
DILATIONS = (1, 4, 16)
WINDOW_STEPS = 128
ATTN_BLOCK = 128
ATTN_QB = 4
GROUP_WIDTH = HEADS_PER_GROUP * HEAD_DIM
ATTN_WIDTH = 3 * GROUP_WIDTH
N_BUCKETS = 32
MAX_DISTANCE = 2048
NEG_INF = -1e30
SSM_WIDTH = 512
SSM_GROUP = 16
SSM_GROUPS = 32
SSM_STATE = 64
NS = SSM_GROUPS * SSM_STATE
EPS = 1e-6
IN_WIDTH = 3 * ATTN_WIDTH + SSM_WIDTH + 2 * D_MODEL
Q_SCALE = HEAD_DIM ** -0.5
N_SHARD = 4
FF_SHARD = D_FF // N_SHARD
ADAM_LR, ADAM_B1, ADAM_B2, ADAM_EPS, ADAM_WD, ADAM_STEP = 0.001, 0.9, 0.999, 1e-08, 0.01, 10

LANES = 128
VMEM_LIMIT = 56 * 1024 * 1024
ROW_TILE = 512
FFN_BWD_TILE = 256
SSM_CHUNK = 256
SCAN_LANES = 512
ADAMW_BLOCK_BYTES = 1 << 20
TN_VMEM_BUDGET = 40 * 1024 * 1024
REDUCE_GROUPS = {
    "ffn2": ["ffn2_w_gate", "ffn2_w_up", "ffn2_w_down"],
    "mixer": ["w_out", "w_attn_branch", "w_ssm_branch", "ssm_w_glu"],
    "w_in": ["w_in"],
    "ffn1": ["ffn1_w_gate", "ffn1_w_up", "ffn1_w_down"],
}
COLLECTIVE_IDS = {name: i for i, name in enumerate(
    ["gather_small"] + [stage + "_" + tag for tag in REDUCE_GROUPS for stage in ("swap", "exchange")])}


def _params(**kw):
    return pltpu.CompilerParams(vmem_limit_bytes=VMEM_LIMIT, **kw)


def _dot(a, b):
    return jnp.dot(a, b, preferred_element_type=F32)


def _dot_nt(a, b):
    return lax.dot_general(a, b, (((1,), (1,)), ((), ())), preferred_element_type=F32)


def _dot_tn(a, b):
    return lax.dot_general(a, b, (((0,), (0,)), ((), ())), preferred_element_type=F32)


def _dot_exact(a, b):
    return jnp.dot(a, b, preferred_element_type=F32, precision=lax.Precision.HIGHEST)


def _dot_nt_exact(a, b):
    return lax.dot_general(a, b, (((1,), (1,)), ((), ())), preferred_element_type=F32,
                           precision=lax.Precision.HIGHEST)


def _rms(x):
    r = lax.rsqrt(jnp.mean(x * x, axis=-1, keepdims=True) + EPS)
    return r, x * r


def _rms_bwd(dh, g, r, xhat):
    dxh = dh * g
    return r * (dxh - xhat * jnp.mean(dxh * xhat, axis=-1, keepdims=True))


def _sigmoid(x):
    return 1.0 / (1.0 + jnp.exp(-x))


_GELU_C = math.sqrt(2.0 / math.pi)


def _gelu(x):
    return 0.5 * x * (1.0 + jnp.tanh(_GELU_C * (x + 0.044715 * x * x * x)))


def _gelu_grad(x):
    t = jnp.tanh(_GELU_C * (x + 0.044715 * x * x * x))
    return 0.5 * (1.0 + t) + 0.5 * x * (1.0 - t * t) * _GELU_C * (1.0 + 3 * 0.044715 * x * x)


def _whole():
    return pl.BlockSpec(memory_space=pltpu.VMEM)


def _row_tile(rows, cap):
    if rows <= cap:
        return rows
    return max(t for t in range(8, cap + 1, 8) if rows % t == 0)


def _rows(tm, w):
    return pl.BlockSpec((tm, w), lambda i: (i, 0))


def _acc_row(w):
    return pl.BlockSpec((1, w), lambda i: (0, 0))


def _ffn_fwd(x, g, wg, wu, wd, name, carried=()):
    L = x.shape[0]
    tm = min(ROW_TILE, L)
    n = len(carried)
    steps = L // tm

    def body(x_ref, g_ref, wg_ref, wu_ref, wd_ref, *refs):
        shard_refs, (xo_ref, a_ref, b_ref), full_refs, sems = refs[:n], refs[n:n + 3], refs[n + 3:2 * n + 3], refs[2 * n + 3:]
        if n:
            start, finish = _gather_parts([w.shape for w in carried], shard_refs, full_refs, *sems)
            pl.when(pl.program_id(0) == 0)(start)
        xv = x_ref[...]
        r, xhat = _rms(xv)
        h = (xhat * g_ref[...]).astype(BF16)
        acc = jnp.zeros((tm, D_MODEL), F32)
        for j in range(N_SHARD):
            a = _dot_nt(h, wg_ref[j])
            b = _dot_nt(h, wu_ref[j])
            a_ref[j] = a.astype(BF16)
            b_ref[j] = b.astype(BF16)
            s = (a * _sigmoid(a) * b).astype(BF16)
            acc = acc + _dot(s, wd_ref[j])
        xo_ref[...] = xv + 0.5 * acc
        if n:
            pl.when(pl.program_id(0) == steps - 1)(finish)

    act = pl.BlockSpec((N_SHARD, tm, FF_SHARD), lambda i: (0, i, 0))
    return pl.pallas_call(
        body, name=name, grid=(steps,),
        in_specs=[_rows(tm, D_MODEL), _whole(), _whole(), _whole(), _whole()] + [_ANY] * n,
        out_specs=[_rows(tm, D_MODEL), act, act] + [_ANY] * n,
        out_shape=[jax.ShapeDtypeStruct((L, D_MODEL), F32),
                   jax.ShapeDtypeStruct((N_SHARD, L, FF_SHARD), BF16),
                   jax.ShapeDtypeStruct((N_SHARD, L, FF_SHARD), BF16)]
        + [jax.ShapeDtypeStruct((N_SHARD,) + w.shape, w.dtype) for w in carried],
        scratch_shapes=[pltpu.SemaphoreType.DMA((7 * n,)), pltpu.SemaphoreType.DMA((7 * n,))] if n else [],
        compiler_params=_params(),
    )(x, g, wg, wu, wd, *carried)


def _ffn_bwd(dxo, x, g, a, b, wg, wu, wd, name):
    L = x.shape[0]
    tm = min(FFN_BWD_TILE, L)

    def body(dxo_ref, x_ref, g_ref, a_ref, b_ref, wg_ref, wu_ref, wd_ref,
             dxi_ref, da_ref, db_ref, s_ref, h_ref, do_ref, dg_ref):
        i = pl.program_id(0)
        xv = x_ref[...]
        gv = g_ref[...]
        r, xhat = _rms(xv)
        h_ref[...] = (xhat * gv).astype(BF16)
        dxo_v = dxo_ref[...]
        d_out = (0.5 * dxo_v).astype(BF16)
        do_ref[...] = d_out
        dh = jnp.zeros((tm, D_MODEL), F32)
        for j in range(N_SHARD):
            av = a_ref[j].astype(F32)
            bv = b_ref[j].astype(F32)
            sg = _sigmoid(av)
            sl = av * sg
            ds = _dot_nt(d_out, wd_ref[j])
            dbv = (ds * sl).astype(BF16)
            dav = (ds * bv * (sg * (1.0 + av * (1.0 - sg)))).astype(BF16)
            da_ref[j] = dav
            db_ref[j] = dbv
            s_ref[j] = (sl * bv).astype(BF16)
            dh = dh + _dot(dav, wg_ref[j]) + _dot(dbv, wu_ref[j])

        @pl.when(i == 0)
        def _():
            dg_ref[...] = jnp.zeros_like(dg_ref)

        dg_ref[...] += jnp.sum(dh * xhat, axis=0, keepdims=True)
        dxi_ref[...] = dxo_v + _rms_bwd(dh, gv, r, xhat)

    act = pl.BlockSpec((N_SHARD, tm, FF_SHARD), lambda i: (0, i, 0))
    act_shape = jax.ShapeDtypeStruct((N_SHARD, L, FF_SHARD), BF16)
    return pl.pallas_call(
        body, name=name, grid=(L // tm,),
        in_specs=[_rows(tm, D_MODEL), _rows(tm, D_MODEL), _whole(), act, act, _whole(), _whole(), _whole()],
        out_specs=[_rows(tm, D_MODEL), act, act, act, _rows(tm, D_MODEL), _rows(tm, D_MODEL), _acc_row(D_MODEL)],
        out_shape=[jax.ShapeDtypeStruct((L, D_MODEL), F32), act_shape, act_shape, act_shape,
                   jax.ShapeDtypeStruct((L, D_MODEL), BF16), jax.ShapeDtypeStruct((L, D_MODEL), BF16),
                   jax.ShapeDtypeStruct((1, D_MODEL), F32)],
        compiler_params=_params(),
    )(dxo, x, g, a, b, wg, wu, wd)


def _matmul_tn(a, b, name):
    ja, L, K = a.shape
    jb, _, N = b.shape
    J = max(ja, jb)
    splits = [s for s in (1, 2, 4, 8) if s == 1 or N % (s * LANES) == 0]
    nsplit = next((s for s in splits if 2 * K * (N // s) * 4 <= TN_VMEM_BUDGET // 2), splits[-1])
    nc = N // nsplit
    left = TN_VMEM_BUDGET - 2 * K * nc * 4
    row_bytes = 2 * (K * a.dtype.itemsize + nc * b.dtype.itemsize)
    tm = next((t for t in (2048, 1024, 512, 256) if L % t == 0 and t * row_bytes <= left), min(128, L))

    def body(a_ref, b_ref, o_ref):
        @pl.when(pl.program_id(2) == 0)
        def _():
            o_ref[...] = jnp.zeros_like(o_ref)

        o_ref[...] += _dot_tn(a_ref[...].astype(BF16), b_ref[...].astype(BF16))

    return pl.pallas_call(
        body, name=name, grid=(J, nsplit, L // tm),
        in_specs=[pl.BlockSpec((None, tm, K), (lambda j, s, i: (j, i, 0)) if ja > 1 else (lambda j, s, i: (0, i, 0))),
                  pl.BlockSpec((None, tm, nc), (lambda j, s, i: (j, i, s)) if jb > 1 else (lambda j, s, i: (0, i, s)))],
        out_specs=pl.BlockSpec((None, K, nc), lambda j, s, i: (j, 0, s)),
        out_shape=jax.ShapeDtypeStruct((J, K, N), F32),
        compiler_params=_params(),
    )(a, b)


def _loss_fwd_bwd(x, g, target):
    L = x.shape[0]
    tm = min(ROW_TILE, L)

    def body(x_ref, g_ref, t_ref, loss_ref, dx_ref, dg_ref):
        i = pl.program_id(0)
        xv = x_ref[...]
        gv = g_ref[...]
        r, xhat = _rms(xv)
        err = xhat * gv - t_ref[...]
        part = 0.5 * jnp.sum(jnp.sum(err * err, axis=1, keepdims=True) * (1.0 / D_MODEL), axis=0, keepdims=True)
        dy = err * (1.0 / D_MODEL)

        @pl.when(i == 0)
        def _():
            dg_ref[...] = jnp.zeros_like(dg_ref)
            loss_ref[...] = jnp.zeros_like(loss_ref)

        loss_ref[...] += jnp.broadcast_to(part, loss_ref.shape)
        dg_ref[...] += jnp.sum(dy * xhat, axis=0, keepdims=True)
        dx_ref[...] = _rms_bwd(dy, gv, r, xhat)

    return pl.pallas_call(
        body, name="loss_fwd_bwd", grid=(L // tm,),
        in_specs=[_rows(tm, D_MODEL), _whole(), _rows(tm, D_MODEL)],
        out_specs=[pl.BlockSpec((8, 128), lambda i: (0, 0)), _rows(tm, D_MODEL), _acc_row(D_MODEL)],
        out_shape=[jax.ShapeDtypeStruct((8, 128), F32), jax.ShapeDtypeStruct((L, D_MODEL), F32),
                   jax.ShapeDtypeStruct((1, D_MODEL), F32)],
        compiler_params=_params(),
    )(x, g, target)


_C_K = ATTN_WIDTH
_C_V = 2 * ATTN_WIDTH
_C_U = 3 * ATTN_WIDTH
_C_G = _C_U + SSM_WIDTH


def _residue_spec(d, tm):
    return pl.BlockSpec((d, tm // d, GROUP_WIDTH), lambda i: (0, i, 0))


def _residue_shape(d, L, dtype):
    return jax.ShapeDtypeStruct((d, L // d, GROUP_WIDTH), dtype)


def _residue_scratch(tm):
    return pltpu.VMEM((GROUP_WIDTH // LANES, tm, LANES), F32)


def _to_residues(val, out_ref, scr, d):
    if d == 1:
        out_ref[0] = val.astype(out_ref.dtype)
        return
    tm = val.shape[0]
    for half in range(GROUP_WIDTH // LANES):
        cols = slice(half * LANES, (half + 1) * LANES)
        scr[half] = val[:, cols]
        for r in range(d):
            out_ref[r, :, cols] = scr[half, pl.ds(r, tm // d, stride=d), :].astype(out_ref.dtype)


def _from_residues(ref, scr, d):
    if d == 1:
        return ref[0].astype(F32)
    rows = ref.shape[1]
    for half in range(GROUP_WIDTH // LANES):
        cols = slice(half * LANES, (half + 1) * LANES)
        for r in range(d):
            scr[half, pl.ds(r, rows, stride=d), :] = ref[r, :, cols].astype(F32)
    return jnp.concatenate([scr[half] for half in range(GROUP_WIDTH // LANES)], axis=1)


def _mix_in_fwd(x, g, w_in, gate_bias):
    L = x.shape[0]
    tm = min(ROW_TILE, L)

    def body(x_ref, g_ref, w_ref, gb_ref, *refs):
        qkv_refs, (u_ref, gate_ref, scr) = refs[:9], refs[9:]
        r, xhat = _rms(x_ref[...])
        h = (xhat * g_ref[...]).astype(BF16)
        for part, (c0, scale) in enumerate(((0, Q_SCALE), (_C_K, 1.0), (_C_V, 1.0))):
            z = _dot_nt(h, w_ref[c0:c0 + ATTN_WIDTH, :]) * scale
            for grp, d in enumerate(DILATIONS):
                _to_residues(z[:, grp * GROUP_WIDTH:(grp + 1) * GROUP_WIDTH], qkv_refs[3 * part + grp], scr, d)
        u_ref[...] = _dot_nt(h, w_ref[_C_U:_C_G, :])
        gate_ref[...] = _sigmoid(_dot_nt(h, w_ref[_C_G:IN_WIDTH, :]) + gb_ref[...])

    return pl.pallas_call(
        body, name="mix_in_fwd", grid=(L // tm,),
        in_specs=[_rows(tm, D_MODEL), _whole(), _whole(), _whole()],
        out_specs=[_residue_spec(d, tm) for d in DILATIONS] * 3 + [_rows(tm, SSM_WIDTH), _rows(tm, 2 * D_MODEL)],
        out_shape=[_residue_shape(d, L, BF16) for d in DILATIONS] * 3
        + [jax.ShapeDtypeStruct((L, SSM_WIDTH), F32), jax.ShapeDtypeStruct((L, 2 * D_MODEL), F32)],
        scratch_shapes=[_residue_scratch(tm)],
        compiler_params=_params(),
    )(x, g, w_in, gate_bias)


def _mix_in_bwd(dx2, x, g, dqkv, du, dgp, w_in):
    L = x.shape[0]
    tm = min(ROW_TILE, L)

    def body(dx2_ref, x_ref, g_ref, *refs):
        piece_refs = refs[:9]
        du_ref, dgp_ref, w_ref, dx1_ref, h_ref, dz_ref, dg_ref, scr = refs[9:]
        i = pl.program_id(0)
        gv = g_ref[...]
        r, xhat = _rms(x_ref[...])
        h_ref[...] = (xhat * gv).astype(BF16)
        for part in range(3):
            for grp, d in enumerate(DILATIONS):
                c0 = part * ATTN_WIDTH + grp * GROUP_WIDTH
                dz_ref[:, c0:c0 + GROUP_WIDTH] = _from_residues(piece_refs[3 * part + grp], scr, d).astype(BF16)
        dz_ref[:, _C_U:_C_G] = du_ref[...].astype(BF16)
        dz_ref[:, _C_G:IN_WIDTH] = dgp_ref[...]
        dh = _dot(dz_ref[...], w_ref[...])

        @pl.when(i == 0)
        def _():
            dg_ref[...] = jnp.zeros_like(dg_ref)

        dg_ref[...] += jnp.sum(dh * xhat, axis=0, keepdims=True)
        dx1_ref[...] = dx2_ref[...] + _rms_bwd(dh, gv, r, xhat)

    return pl.pallas_call(
        body, name="mix_in_bwd", grid=(L // tm,),
        in_specs=[_rows(tm, D_MODEL), _rows(tm, D_MODEL), _whole()] + [_residue_spec(d, tm) for d in DILATIONS] * 3
        + [_rows(tm, SSM_WIDTH), _rows(tm, 2 * D_MODEL), _whole()],
        out_specs=[_rows(tm, D_MODEL), _rows(tm, D_MODEL), _rows(tm, IN_WIDTH), _acc_row(D_MODEL)],
        out_shape=[jax.ShapeDtypeStruct((L, D_MODEL), F32), jax.ShapeDtypeStruct((L, D_MODEL), BF16),
                   jax.ShapeDtypeStruct((L, IN_WIDTH), BF16), jax.ShapeDtypeStruct((1, D_MODEL), F32)],
        scratch_shapes=[_residue_scratch(tm)],
        compiler_params=_params(),
    )(dx2, x, g, *dqkv, du, dgp, w_in)


def _bucket_onehot():
    qi = jnp.arange(ATTN_BLOCK)[:, None]
    kj = jnp.arange(2 * ATTN_BLOCK)[None, :]
    steps = jnp.maximum(qi + ATTN_BLOCK - kj, 0)
    max_exact = N_BUCKETS // 2
    out = []
    for d in DILATIONS:
        dist = steps * d
        df = jnp.maximum(dist, 1).astype(F32)
        large = max_exact + (jnp.log(df / max_exact) / math.log(MAX_DISTANCE / max_exact)
                             * (N_BUCKETS - max_exact)).astype(jnp.int32)
        large = jnp.minimum(large, N_BUCKETS - 1)
        bucket = jnp.where(dist < max_exact, dist, large).reshape(-1)
        out.append((bucket[None, :] == jnp.arange(N_BUCKETS)[:, None]).astype(F32))
    return jnp.stack(out)


def _bias_expand(table_t, onehot):
    n = onehot.shape[-1]

    def body(t_ref, oh_ref, o_ref):
        bias = _dot_exact(t_ref[...], oh_ref[...])
        col = lax.broadcasted_iota(jnp.int32, (8, n), 1)
        qi = col // (2 * ATTN_BLOCK)
        kj = col - qi * (2 * ATTN_BLOCK)
        steps = qi + ATTN_BLOCK - kj
        band = (steps >= 0) & (steps <= WINDOW_STEPS)
        o_ref[0] = jnp.where(band & (kj >= ATTN_BLOCK), bias, NEG_INF)
        o_ref[1] = jnp.where(band, bias, NEG_INF)

    return pl.pallas_call(
        body, name="bias_expand", grid=(3,),
        in_specs=[pl.BlockSpec((None, 8, N_BUCKETS), lambda g: (g, 0, 0)),
                  pl.BlockSpec((None, N_BUCKETS, n), lambda g: (g, 0, 0))],
        out_specs=pl.BlockSpec((None, 2, 8, n), lambda g: (g, 0, 0, 0)),
        out_shape=jax.ShapeDtypeStruct((3, 2, 8, n), F32),
        compiler_params=_params(),
    )(table_t, onehot)


def _bias_reduce(dsum, onehot):
    n = onehot.shape[-1]

    def body(d_ref, oh_ref, o_ref):
        o_ref[...] = _dot_nt_exact(d_ref[...], oh_ref[...])

    return pl.pallas_call(
        body, name="bias_reduce", grid=(3,),
        in_specs=[pl.BlockSpec((None, 8, n), lambda g: (g, 0, 0)),
                  pl.BlockSpec((None, N_BUCKETS, n), lambda g: (g, 0, 0))],
        out_specs=pl.BlockSpec((None, 8, N_BUCKETS), lambda g: (g, 0, 0)),
        out_shape=jax.ShapeDtypeStruct((3, 8, N_BUCKETS), F32),
        compiler_params=_params(),
    )(dsum, onehot)


def _head_of_col(rows):
    return lax.broadcasted_iota(jnp.int32, (rows, GROUP_WIDTH), 1) // HEAD_DIM


def _attn_specs(qb):
    rows = qb * ATTN_BLOCK
    cur = pl.BlockSpec((None, rows, GROUP_WIDTH), lambda r, n: (r, n, 0))
    prev = pl.BlockSpec((None, ATTN_BLOCK, GROUP_WIDTH), lambda r, n: (r, jnp.maximum(n * qb - 1, 0), 0))
    bias = pl.BlockSpec((2, HEADS_PER_GROUP, ATTN_BLOCK, 2 * ATTN_BLOCK), lambda r, n: (0, 0, 0, 0))
    return cur, prev, bias


def _attn_fwd(q, k, v, bias, name):
    d, M, _ = q.shape
    nb = M // ATTN_BLOCK
    qb = min(ATTN_QB, nb)

    def body(q_ref, kp_ref, kc_ref, vp_ref, vc_ref, bias_ref, o_ref, lse_ref):
        n = pl.program_id(1)
        q_head = _head_of_col(ATTN_BLOCK)
        kv_head = _head_of_col(2 * ATTN_BLOCK)
        kwin = jnp.concatenate([kp_ref[...], kc_ref[...]], axis=0)
        vwin = jnp.concatenate([vp_ref[...], vc_ref[...]], axis=0)
        for b in range(qb):
            rows = slice(b * ATTN_BLOCK, (b + 1) * ATTN_BLOCK)
            window = slice(b * ATTN_BLOCK, (b + 2) * ATTN_BLOCK)
            variant = jnp.minimum(n, 1) if b == 0 else 1
            qv = q_ref[rows, :]
            kk = kwin[window]
            vv = vwin[window]
            o_acc = jnp.zeros((ATTN_BLOCK, GROUP_WIDTH), F32)
            lse_acc = jnp.zeros((ATTN_BLOCK, GROUP_WIDTH), F32)
            for hh in range(HEADS_PER_GROUP):
                hm = q_head == hh
                qh = jnp.where(hm, qv, jnp.zeros_like(qv))
                logits = _dot_nt(qh, kk) + bias_ref[variant, hh]
                m = jnp.max(logits, axis=1, keepdims=True)
                p = jnp.exp(logits - m)
                vh = jnp.where(kv_head == hh, vv, jnp.ones_like(vv))
                pv = _dot(p.astype(BF16), vh)
                c_sum = ((hh + 1) % HEADS_PER_GROUP) * HEAD_DIM
                den = pv[:, c_sum:c_sum + 1]
                o_acc = jnp.where(hm, pv * (1.0 / den), o_acc)
                lse_acc = jnp.where(hm, m + jnp.log(den), lse_acc)
            o_ref[rows, :] = o_acc
            lse_ref[rows, :] = lse_acc

    cur, prev, full = _attn_specs(qb)
    return pl.pallas_call(
        body, name=name, grid=(d, nb // qb),
        in_specs=[cur, prev, cur, prev, cur, full],
        out_specs=[cur, cur],
        out_shape=[jax.ShapeDtypeStruct((d, M, GROUP_WIDTH), F32)] * 2,
        compiler_params=_params(),
    )(q, k, k, v, v, bias)


def _attn_bwd(q, k, v, do, lse, delta, bias, name):
    d, M, _ = q.shape
    nb = M // ATTN_BLOCK
    qb = min(ATTN_QB, nb)
    ns = nb // qb
    rows_q = qb * ATTN_BLOCK
    last = slice(rows_q - ATTN_BLOCK, rows_q)

    def body(q_ref, kp_ref, kc_ref, vp_ref, vc_ref, do_ref, lse_ref, dl_ref, bias_ref,
             dq_ref, dk_ref, dv_ref, dsum_ref, pk_ref, pv_ref, wk_ref, wv_ref):
        r = pl.program_id(0)
        n = pl.program_id(1)

        @pl.when((r == 0) & (n == 0))
        def _():
            dsum_ref[...] = jnp.zeros_like(dsum_ref)

        @pl.when(n == 0)
        def _():
            pk_ref[...] = jnp.zeros_like(pk_ref)
            pv_ref[...] = jnp.zeros_like(pv_ref)

        @pl.when(n < ns)
        def _():
            q_head = _head_of_col(ATTN_BLOCK)
            kwin = jnp.concatenate([kp_ref[...], kc_ref[...]], axis=0)
            vwin = jnp.concatenate([vp_ref[...], vc_ref[...]], axis=0)
            wk_ref[...] = jnp.zeros_like(wk_ref)
            wv_ref[...] = jnp.zeros_like(wv_ref)
            for b in range(qb):
                rows = slice(b * ATTN_BLOCK, (b + 1) * ATTN_BLOCK)
                window = slice(b * ATTN_BLOCK, (b + 2) * ATTN_BLOCK)
                variant = jnp.minimum(n, 1) if b == 0 else 1
                qv = q_ref[rows, :]
                dov = do_ref[rows, :]
                kk = kwin[window]
                vv = vwin[window]
                dq_acc = jnp.zeros((ATTN_BLOCK, GROUP_WIDTH), F32)
                dkk = jnp.zeros((2 * ATTN_BLOCK, GROUP_WIDTH), F32)
                dvv = jnp.zeros((2 * ATTN_BLOCK, GROUP_WIDTH), F32)
                for hh in range(HEADS_PER_GROUP):
                    hm = q_head == hh
                    c0 = hh * HEAD_DIM
                    qh = jnp.where(hm, qv, jnp.zeros_like(qv))
                    doh = jnp.where(hm, dov, jnp.zeros_like(dov))
                    logits = _dot_nt(qh, kk) + bias_ref[variant, hh]
                    p = jnp.exp(logits - lse_ref[rows, c0:c0 + 1])
                    dp = _dot_nt(doh, vv)
                    ds = p * (dp - dl_ref[rows, c0:c0 + 1])
                    dsum_ref[hh] += ds
                    ds16 = ds.astype(BF16)
                    dq_acc = jnp.where(hm, _dot(ds16, kk), dq_acc)
                    dkk = dkk + _dot_tn(ds16, qh)
                    dvv = dvv + _dot_tn(p.astype(BF16), doh)
                dq_ref[rows, :] = (dq_acc * Q_SCALE).astype(BF16)
                wk_ref[window, :] += dkk
                wv_ref[window, :] += dvv
            for out_ref, part_ref, win_ref in ((dk_ref, pk_ref, wk_ref), (dv_ref, pv_ref, wv_ref)):
                if qb > 1:
                    out_ref[0:rows_q - ATTN_BLOCK, :] = part_ref[0:rows_q - ATTN_BLOCK, :].astype(BF16)
                out_ref[last, :] = (part_ref[last, :] + win_ref[0:ATTN_BLOCK, :]).astype(BF16)
                part_ref[...] = win_ref[ATTN_BLOCK:, :]

        @pl.when(n == ns)
        def _():
            dk_ref[...] = pk_ref[...].astype(BF16)
            dv_ref[...] = pv_ref[...].astype(BF16)

    def clamp(n):
        return jnp.minimum(n, ns - 1)

    cur = pl.BlockSpec((None, rows_q, GROUP_WIDTH), lambda r, n: (r, clamp(n), 0))
    prev = pl.BlockSpec((None, ATTN_BLOCK, GROUP_WIDTH), lambda r, n: (r, jnp.maximum(clamp(n) * qb - 1, 0), 0))
    lag = pl.BlockSpec((None, rows_q, GROUP_WIDTH), lambda r, n: (r, jnp.maximum(n - 1, 0), 0))
    full = pl.BlockSpec((2, HEADS_PER_GROUP, ATTN_BLOCK, 2 * ATTN_BLOCK), lambda r, n: (0, 0, 0, 0))
    acc = pl.BlockSpec((HEADS_PER_GROUP, ATTN_BLOCK, 2 * ATTN_BLOCK), lambda r, n: (0, 0, 0))
    return pl.pallas_call(
        body, name=name, grid=(d, ns + 1),
        in_specs=[cur, prev, cur, prev, cur, cur, cur, cur, full],
        out_specs=[cur, lag, lag, acc],
        out_shape=[jax.ShapeDtypeStruct((d, M, GROUP_WIDTH), BF16)] * 3
        + [jax.ShapeDtypeStruct((HEADS_PER_GROUP, ATTN_BLOCK, 2 * ATTN_BLOCK), F32)],
        scratch_shapes=[pltpu.VMEM((rows_q, GROUP_WIDTH), F32), pltpu.VMEM((rows_q, GROUP_WIDTH), F32),
                        pltpu.VMEM((rows_q + ATTN_BLOCK, GROUP_WIDTH), F32),
                        pltpu.VMEM((rows_q + ATTN_BLOCK, GROUP_WIDTH), F32)],
        compiler_params=_params(),
    )(q, k, k, v, v, do, lse, delta, bias)


def _disc_math(a_re, a_im, ldt, b_re, b_im):
    dt = jnp.exp(ldt)
    mag = jnp.exp(a_re * dt)
    ab_re = mag * jnp.cos(a_im * dt)
    ab_im = mag * jnp.sin(a_im * dt)
    den = a_re * a_re + a_im * a_im
    xr = ab_re - 1.0
    coef_re = (xr * a_re + ab_im * a_im) / den
    coef_im = (ab_im * a_re - xr * a_im) / den
    return ab_re, ab_im, coef_re * b_re - coef_im * b_im, coef_re * b_im + coef_im * b_re


def _block_diag_mask():
    row_g = lax.broadcasted_iota(jnp.int32, (SSM_WIDTH, 2 * NS), 0) // SSM_GROUP
    col = lax.broadcasted_iota(jnp.int32, (SSM_WIDTH, 2 * NS), 1)
    col_g = jnp.where(col >= NS, col - NS, col) // SSM_STATE
    return row_g == col_g


def _disc_fwd(a_re, a_im, ldt, b_re, b_im, c_re, c_im):
    def body(are_ref, aim_ref, ldt_ref, bre_ref, bim_ref, cre_ref, cim_ref, pw_ref, pwr_ref, bd_ref, cdt_ref):
        ab_re, ab_im, bb_re, bb_im = _disc_math(are_ref[...], aim_ref[...], ldt_ref[...], bre_ref[...], bim_ref[...])
        row = lax.broadcasted_iota(jnp.int32, (8, NS), 0)
        pr, pi = ab_re, ab_im
        t_re = jnp.zeros((8, NS), F32)
        t_im = jnp.zeros((8, NS), F32)
        u_re = jnp.zeros((8, NS), F32)
        u_im = jnp.zeros((8, NS), F32)
        for j in range(8):
            t_re = jnp.where(row == j, pr, t_re)
            t_im = jnp.where(row == j, pi, t_im)
            u_re = jnp.where(row == 7 - j, pr, u_re)
            u_im = jnp.where(row == 7 - j, pi, u_im)
            pr, pi = pr * ab_re - pi * ab_im, pr * ab_im + pi * ab_re
        pw_ref[0] = t_re
        pw_ref[1] = t_im
        pwr_ref[0] = u_re
        pwr_ref[1] = u_im
        mask = _block_diag_mask()
        zero = jnp.zeros((SSM_WIDTH, 2 * NS), F32)
        bfull = jnp.concatenate([jnp.concatenate([bb_re] * SSM_GROUPS, axis=0),
                                 jnp.concatenate([bb_im] * SSM_GROUPS, axis=0)], axis=1)
        bd_ref[...] = jnp.where(mask, bfull, zero).astype(BF16)
        cfull = jnp.concatenate([jnp.concatenate([cre_ref[...]] * SSM_GROUPS, axis=0),
                                 jnp.concatenate([-cim_ref[...]] * SSM_GROUPS, axis=0)], axis=1)
        cdt_ref[...] = jnp.where(mask, cfull, zero).astype(BF16)

    return pl.pallas_call(
        body, name="s5_disc_fwd",
        in_specs=[_whole()] * 7, out_specs=[_whole()] * 4,
        out_shape=[jax.ShapeDtypeStruct((2, 8, NS), F32), jax.ShapeDtypeStruct((2, 8, NS), F32),
                   jax.ShapeDtypeStruct((SSM_WIDTH, 2 * NS), BF16), jax.ShapeDtypeStruct((SSM_WIDTH, 2 * NS), BF16)],
        compiler_params=_params(),
    )(a_re, a_im, ldt, b_re, b_im, c_re, c_im)


def _disc_bwd(a_re, a_im, ldt, b_re, b_im, d_bd, d_cdt, d_ab, group_sum):
    def body(are_ref, aim_ref, ldt_ref, bre_ref, bim_ref, dbd_ref, dcdt_ref, dab_ref, gs_ref,
             dare_ref, daim_ref, dldt_ref, dbre_ref, dbim_ref, dcre_ref, dcim_ref):
        col = lax.broadcasted_iota(jnp.int32, (SSM_GROUP, 2 * NS), 1)
        col_g = jnp.where(col >= NS, col - NS, col) // SSM_STATE
        acc_b = jnp.zeros((SSM_GROUP, 2 * NS), F32)
        acc_c = jnp.zeros((SSM_GROUP, 2 * NS), F32)
        for g in range(SSM_GROUPS):
            rows = slice(g * SSM_GROUP, (g + 1) * SSM_GROUP)
            acc_b = acc_b + jnp.where(col_g == g, dbd_ref[rows, :], 0.0)
            acc_c = acc_c + jnp.where(col_g == g, dcdt_ref[rows, :], 0.0)
        dcre_ref[...] = acc_c[:, :NS]
        dcim_ref[...] = -acc_c[:, NS:]
        dab_re = jnp.sum(dab_ref[0], axis=0, keepdims=True)
        dab_im = jnp.sum(dab_ref[1], axis=0, keepdims=True)
        _, vjp = jax.vjp(_disc_math, are_ref[...], aim_ref[...], ldt_ref[...], bre_ref[...], bim_ref[...])
        d_are, d_aim, d_ldt, d_bre, d_bim = vjp((dab_re, dab_im, acc_b[:, :NS], acc_b[:, NS:]))
        dare_ref[...] = d_are
        daim_ref[...] = d_aim
        dbre_ref[...] = d_bre
        dbim_ref[...] = d_bim
        dldt_ref[...] = _dot_exact(jnp.broadcast_to(d_ldt, (8, NS)), gs_ref[...])

    vec = jax.ShapeDtypeStruct((1, NS), F32)
    mat = jax.ShapeDtypeStruct((SSM_GROUP, NS), F32)
    return pl.pallas_call(
        body, name="s5_disc_bwd",
        in_specs=[_whole()] * 9, out_specs=[_whole()] * 7,
        out_shape=[vec, vec, jax.ShapeDtypeStruct((8, 128), F32), mat, mat, mat, mat],
        compiler_params=_params(),
    )(a_re, a_im, ldt, b_re, b_im, d_bd, d_cdt, d_ab, group_sum)


def _scan_blocks(buf, pw_ref, carry_ref, n_blocks, reverse):
    row = lax.broadcasted_iota(jnp.int32, (8, SCAN_LANES), 0)
    for lc in range(NS // SCAN_LANES):
        re_cols = pl.ds(lc * SCAN_LANES, SCAN_LANES)
        im_cols = pl.ds(NS + lc * SCAN_LANES, SCAN_LANES)
        pr = pw_ref[0, :, re_cols]
        pi = pw_ref[1, :, re_cols]
        if reverse:
            pi = -pi
            base = [(7, 1), (6, 2), (4, 4)]
            coef = [(jnp.where(row < 8 - k, pr[j:j + 1], 0.0), jnp.where(row < 8 - k, pi[j:j + 1], 0.0), 8 - k)
                    for j, k in base]
        else:
            base = [(0, 1), (1, 2), (3, 4)]
            coef = [(jnp.where(row >= k, pr[j:j + 1], 0.0), jnp.where(row >= k, pi[j:j + 1], 0.0), k)
                    for j, k in base]

        def step(i, carry, pr=pr, pi=pi, coef=coef, re_cols=re_cols, im_cols=im_cols):
            cr, ci = carry
            blk = (n_blocks - 1 - i) if reverse else i
            rows = pl.ds(pl.multiple_of(blk * 8, 8), 8)
            xr = buf[rows, re_cols]
            xi = buf[rows, im_cols]
            for kr, ki, shift in coef:
                sr = pltpu.roll(xr, shift, 0)
                si = pltpu.roll(xi, shift, 0)
                xr, xi = xr + kr * sr - ki * si, xi + kr * si + ki * sr
            xr, xi = xr + pr * cr - pi * ci, xi + pr * ci + pi * cr
            buf[rows, re_cols] = xr
            buf[rows, im_cols] = xi
            edge = slice(0, 1) if reverse else slice(7, 8)
            return xr[edge], xi[edge]

        cr, ci = lax.fori_loop(0, n_blocks, step, (carry_ref[0:1, re_cols], carry_ref[0:1, im_cols]))
        carry_ref[0:1, re_cols] = cr
        carry_ref[0:1, im_cols] = ci


_SUPER_GROUPS = 16
_SUPER_BLOCKS = [
    (slice(k * _SUPER_GROUPS * SSM_GROUP, (k + 1) * _SUPER_GROUPS * SSM_GROUP),
     [slice(half + k * _SUPER_GROUPS * SSM_STATE, half + (k + 1) * _SUPER_GROUPS * SSM_STATE) for half in (0, NS)])
    for k in range(SSM_GROUPS // _SUPER_GROUPS)]


def _ssm_fwd(u, bd, cdt, d_skip, pw):
    L = u.shape[0]
    tc = min(SSM_CHUNK, L)

    def body(u_ref, bd_ref, cdt_ref, dsk_ref, pw_ref, y_ref, s_ref, carry_ref):
        @pl.when(pl.program_id(0) == 0)
        def _():
            carry_ref[...] = jnp.zeros_like(carry_ref)

        uv = u_ref[...]
        u16 = uv.astype(BF16)
        for ch, states in _SUPER_BLOCKS:
            for st in states:
                s_ref[:, st] = _dot(u16[:, ch], bd_ref[ch, st])
        _scan_blocks(s_ref, pw_ref, carry_ref, tc // 8, reverse=False)
        for ch, states in _SUPER_BLOCKS:
            y_ref[:, ch] = (sum(_dot_nt(s_ref[:, st].astype(BF16), cdt_ref[ch, st]) for st in states)
                            + dsk_ref[:, ch] * uv[:, ch])

    return pl.pallas_call(
        body, name="s5_fwd", grid=(L // tc,),
        in_specs=[_rows(tc, SSM_WIDTH), _whole(), _whole(), _whole(), _whole()],
        out_specs=[_rows(tc, SSM_WIDTH), _rows(tc, 2 * NS)],
        out_shape=[jax.ShapeDtypeStruct((L, SSM_WIDTH), F32), jax.ShapeDtypeStruct((L, 2 * NS), F32)],
        scratch_shapes=[pltpu.VMEM((8, 2 * NS), F32)],
        compiler_params=_params(),
    )(u, bd, cdt, d_skip, pw)


def _ssm_bwd(dy, u, s, bd, cdt, d_skip, pwr):
    L = u.shape[0]
    tc = min(SSM_CHUNK, L)
    nc = L // tc
    blocks = tc // 8

    def body(dy_ref, u_ref, s_ref, sprev_ref, bd_ref, cdt_ref, dsk_ref, pwr_ref,
             du_ref, ddsk_ref, dbd_ref, dcdt_ref, dab_ref, g_ref, sx_ref, carry_ref):
        i = pl.program_id(0)

        @pl.when(i == 0)
        def _():
            carry_ref[...] = jnp.zeros_like(carry_ref)
            ddsk_ref[...] = jnp.zeros_like(ddsk_ref)
            dbd_ref[...] = jnp.zeros_like(dbd_ref)
            dcdt_ref[...] = jnp.zeros_like(dcdt_ref)
            dab_ref[...] = jnp.zeros_like(dab_ref)

        dyv = dy_ref[...]
        uv = u_ref[...]
        dy16 = dyv.astype(BF16)
        u16 = uv.astype(BF16)
        for ch, states in _SUPER_BLOCKS:
            for st in states:
                g_ref[:, st] = _dot(dy16[:, ch], cdt_ref[ch, st])
        _scan_blocks(g_ref, pwr_ref, carry_ref, blocks, reverse=True)
        ddsk_ref[...] += jnp.sum(dyv * uv, axis=0, keepdims=True)
        for ch, states in _SUPER_BLOCKS:
            du = dsk_ref[:, ch] * dyv[:, ch]
            for st in states:
                g16 = g_ref[:, st].astype(BF16)
                du = du + _dot_nt(g16, bd_ref[ch, st])
                dbd_ref[ch, st] += _dot_tn(u16[:, ch], g16)
                dcdt_ref[ch, st] += _dot_tn(dy16[:, ch], s_ref[:, st].astype(BF16))
            du_ref[:, ch] = du

        sx_ref[pl.ds(8, tc), :] = s_ref[...]
        sx_ref[pl.ds(0, 8), :] = jnp.where(i == nc - 1, 0.0, sprev_ref[...])
        row = lax.broadcasted_iota(jnp.int32, (8, SCAN_LANES), 0)
        for lc in range(NS // SCAN_LANES):
            re_cols = pl.ds(lc * SCAN_LANES, SCAN_LANES)
            im_cols = pl.ds(NS + lc * SCAN_LANES, SCAN_LANES)

            def step(b, acc, re_cols=re_cols, im_cols=im_cols):
                ar, ai = acc
                off = pl.multiple_of(b * 8, 8)
                gr = g_ref[pl.ds(off, 8), re_cols]
                gi = g_ref[pl.ds(off, 8), im_cols]
                before = pl.ds(off, 8)
                here = pl.ds(off + 8, 8)
                sr = jnp.where(row == 0, sx_ref[before, re_cols][7:8], pltpu.roll(sx_ref[here, re_cols], 1, 0))
                si = jnp.where(row == 0, sx_ref[before, im_cols][7:8], pltpu.roll(sx_ref[here, im_cols], 1, 0))
                return ar + gr * sr + gi * si, ai + gi * sr - gr * si

            zero = jnp.zeros((8, SCAN_LANES), F32)
            ar, ai = lax.fori_loop(0, blocks, step, (zero, zero))
            dab_ref[0, :, re_cols] += ar
            dab_ref[1, :, re_cols] += ai

    rev = lambda i: (nc - 1 - i, 0)
    sprev = pl.BlockSpec((8, 2 * NS), lambda i: (jnp.maximum((nc - 1 - i) * blocks - 1, 0), 0))
    return pl.pallas_call(
        body, name="s5_bwd", grid=(nc,),
        in_specs=[pl.BlockSpec((tc, SSM_WIDTH), rev), pl.BlockSpec((tc, SSM_WIDTH), rev),
                  pl.BlockSpec((tc, 2 * NS), rev), sprev, _whole(), _whole(), _whole(), _whole()],
        out_specs=[pl.BlockSpec((tc, SSM_WIDTH), rev), _whole(), _whole(), _whole(), _whole()],
        out_shape=[jax.ShapeDtypeStruct((L, SSM_WIDTH), F32), jax.ShapeDtypeStruct((1, SSM_WIDTH), F32),
                   jax.ShapeDtypeStruct((SSM_WIDTH, 2 * NS), F32), jax.ShapeDtypeStruct((SSM_WIDTH, 2 * NS), F32),
                   jax.ShapeDtypeStruct((2, 8, NS), F32)],
        scratch_shapes=[pltpu.VMEM((tc, 2 * NS), F32), pltpu.VMEM((tc + 8, 2 * NS), F32), pltpu.VMEM((8, 2 * NS), F32)],
        compiler_params=_params(),
    )(dy, u, s, s, bd, cdt, d_skip, pwr)


def _branches(o_attn, y, gates, w_ab, w_glu, w_sb):
    ya = _dot(o_attn.astype(BF16), w_ab[...])
    gel = _gelu(y)
    glu = _dot(gel.astype(BF16), w_glu[...])
    p = glu[:, :SSM_WIDTH]
    sg = _sigmoid(glu[:, SSM_WIDTH:])
    ys2 = p * sg
    ysb = _dot(ys2.astype(BF16), w_sb[...])
    ga = gates[:, :D_MODEL]
    gs = gates[:, D_MODEL:]
    return ya, gel, p, sg, ys2, ysb, ga, gs


def _mix_out_fwd(x1, o_g, lse_g, y, gates, w_ab, w_glu, w_sb, w_out):
    L = x1.shape[0]
    tm = min(ROW_TILE, L)

    def body(x_ref, o0, o1, o2, l0, l1, l2, y_ref, gate_ref, wab_ref, wglu_ref, wsb_ref, wout_ref,
             x2_ref, oat_ref, lse0, lse1, lse2, scr):
        la, lb, lc = (_from_residues(ref, scr, d) for ref, d in zip((l0, l1, l2), DILATIONS))
        m = jnp.maximum(jnp.maximum(la, lb), lc)
        ea, eb, ec = jnp.exp(la - m), jnp.exp(lb - m), jnp.exp(lc - m)
        tot = ea + eb + ec
        oa, ob, oc = (_from_residues(ref, scr, d) for ref, d in zip((o0, o1, o2), DILATIONS))
        o_attn = (ea * oa + eb * ob + ec * oc) / tot
        oat_ref[...] = o_attn
        lse = m + jnp.log(tot)
        for ref, d in zip((lse0, lse1, lse2), DILATIONS):
            _to_residues(lse, ref, scr, d)
        ya, _, _, _, _, ysb, ga, gs = _branches(o_attn, y_ref[...], gate_ref[...], wab_ref, wglu_ref, wsb_ref)
        mix = ga * ya + gs * ysb
        x2_ref[...] = x_ref[...] + _dot(mix.astype(BF16), wout_ref[...])

    res = [_residue_spec(d, tm) for d in DILATIONS]
    return pl.pallas_call(
        body, name="mix_out_fwd", grid=(L // tm,),
        in_specs=[_rows(tm, D_MODEL)] + res * 2 + [_rows(tm, SSM_WIDTH), _rows(tm, 2 * D_MODEL)] + [_whole()] * 4,
        out_specs=[_rows(tm, D_MODEL), _rows(tm, GROUP_WIDTH)] + res,
        out_shape=[jax.ShapeDtypeStruct((L, D_MODEL), F32), jax.ShapeDtypeStruct((L, GROUP_WIDTH), F32)]
        + [_residue_shape(d, L, F32) for d in DILATIONS],
        scratch_shapes=[_residue_scratch(tm)],
        compiler_params=_params(),
    )(x1, *o_g, *lse_g, y, gates, w_ab, w_glu, w_sb, w_out)


def _mix_out_bwd(dx2, o_attn, y, gates, w_ab, w_glu, w_sb, w_out, head_sum):
    L = dx2.shape[0]
    tm = min(ROW_TILE, L)

    def body(dx_ref, oat_ref, y_ref, gate_ref, wab_ref, wglu_ref, wsb_ref, wout_ref, hs_ref,
             do0, do1, do2, dl0, dl1, dl2, dy_ref, dgp_ref, mix_ref, dya_ref, dys_ref, ys2_ref, gel_ref, dglu_ref,
             dgb_ref, scr):
        i = pl.program_id(0)
        o_attn = oat_ref[...]
        yv = y_ref[...]
        ya, gel, p, sg, ys2, ysb, ga, gs = _branches(o_attn, yv, gate_ref[...], wab_ref, wglu_ref, wsb_ref)
        mix_ref[...] = (ga * ya + gs * ysb).astype(BF16)
        ys2_ref[...] = ys2.astype(BF16)
        gel_ref[...] = gel.astype(BF16)
        dmix = _dot_nt(dx_ref[...].astype(BF16), wout_ref[...])
        dgp = jnp.concatenate([dmix * ya * ga * (1.0 - ga), dmix * ysb * gs * (1.0 - gs)], axis=1)
        dgp_ref[...] = dgp.astype(BF16)

        @pl.when(i == 0)
        def _():
            dgb_ref[...] = jnp.zeros_like(dgb_ref)

        dgb_ref[...] += jnp.sum(dgp, axis=0, keepdims=True)
        dya = (dmix * ga).astype(BF16)
        dys = (dmix * gs).astype(BF16)
        dya_ref[...] = dya
        dys_ref[...] = dys
        d_o = _dot_nt(dya, wab_ref[...])
        delta = _dot_exact(d_o * o_attn, hs_ref[...])
        for do_ref, dl_ref, d in zip((do0, do1, do2), (dl0, dl1, dl2), DILATIONS):
            _to_residues(d_o, do_ref, scr, d)
            _to_residues(delta, dl_ref, scr, d)
        dys2 = _dot_nt(dys, wsb_ref[...])
        dglu = jnp.concatenate([dys2 * sg, dys2 * p * sg * (1.0 - sg)], axis=1).astype(BF16)
        dglu_ref[...] = dglu
        dy_ref[...] = _dot_nt(dglu, wglu_ref[...]) * _gelu_grad(yv)

    grp = _rows(tm, GROUP_WIDTH)
    wide = _rows(tm, D_MODEL)
    half = _rows(tm, SSM_WIDTH)
    res = [_residue_spec(d, tm) for d in DILATIONS]
    sds = jax.ShapeDtypeStruct
    return pl.pallas_call(
        body, name="mix_out_bwd", grid=(L // tm,),
        in_specs=[wide, grp, half, _rows(tm, 2 * D_MODEL)] + [_whole()] * 5,
        out_specs=res + res + [half, _rows(tm, 2 * D_MODEL), wide, wide, wide, half, half, wide, _acc_row(2 * D_MODEL)],
        out_shape=[_residue_shape(d, L, BF16) for d in DILATIONS] + [_residue_shape(d, L, F32) for d in DILATIONS]
        + [sds((L, SSM_WIDTH), F32),
           sds((L, 2 * D_MODEL), BF16), sds((L, D_MODEL), BF16), sds((L, D_MODEL), BF16),
           sds((L, D_MODEL), BF16), sds((L, SSM_WIDTH), BF16), sds((L, SSM_WIDTH), BF16),
           sds((L, D_MODEL), BF16), sds((1, 2 * D_MODEL), F32)],
        scratch_shapes=[_residue_scratch(tm)],
        compiler_params=_params(),
    )(dx2, o_attn, y, gates, w_ab, w_glu, w_sb, w_out, head_sum)


def _adamw(w, g, m, v, name):
    R, C = w.shape
    tr = _row_tile(R, max(8, ADAMW_BLOCK_BYTES // (4 * C)))

    def body(w_ref, g_ref, m_ref, v_ref, d_ref, mo_ref, vo_ref):
        gv = g_ref[...]
        mn = ADAM_B1 * m_ref[...] + (1.0 - ADAM_B1) * gv
        vn = ADAM_B2 * v_ref[...] + (1.0 - ADAM_B2) * (gv * gv)
        m_hat = mn / (1.0 - ADAM_B1 ** ADAM_STEP)
        v_hat = vn / (1.0 - ADAM_B2 ** ADAM_STEP)
        d_ref[...] = -ADAM_LR * (m_hat / (jnp.sqrt(v_hat) + ADAM_EPS) + ADAM_WD * w_ref[...])
        mo_ref[...] = mn
        vo_ref[...] = vn

    blk = pl.BlockSpec((tr, C), lambda i: (i, 0))
    return pl.pallas_call(
        body, name=name, grid=(R // tr,),
        in_specs=[blk] * 4, out_specs=[blk] * 3,
        out_shape=[jax.ShapeDtypeStruct((R, C), F32)] * 3,
        compiler_params=_params(),
    )(w, g, m, v)


def _sum_chips_into_half(u, t, name):
    S, H, C = u.shape
    tr = _row_tile(H, 512)
    hb = H // tr

    def body(s_ref, t_ref, a_ref, b_ref, c_ref, o_ref):
        me = s_ref[1]
        others = (a_ref[...], b_ref[...], c_ref[...])
        acc = None
        for chip in range(S):
            below = others[min(chip, S - 2)]
            above = others[max(chip - 1, 0)]
            term = jnp.where(me == chip, t_ref[...], jnp.where(me > chip, below, above)).astype(F32)
            acc = term if acc is None else acc + term
        o_ref[...] = acc

    x, y, c = lax.axis_index("x"), lax.axis_index("y"), lax.axis_index("c")
    me = 2 * x + y
    scalars = jnp.stack([c, me] + [j + (j >= me).astype(jnp.int32) for j in range(S - 1)]).astype(jnp.int32)
    blk = (None, tr, C)
    return pl.pallas_call(
        body, name=name,
        grid_spec=pltpu.PrefetchScalarGridSpec(
            num_scalar_prefetch=1, grid=(hb,),
            in_specs=[pl.BlockSpec(blk, lambda i, s: (s[1], i, 0))]
            + [pl.BlockSpec(blk, functools.partial(lambda j, i, s: (s[2 + j], i, 0), j)) for j in range(S - 1)],
            out_specs=pl.BlockSpec((tr, C), lambda i, s: (s[0] * hb + i, 0))),
        out_shape=jax.ShapeDtypeStruct((2 * H, C), F32),
        compiler_params=_params(),
    )(scalars, t, u, u, u)


def _add_halves(g, r1, name):
    S, R, C = g.shape
    H = R // 2
    tr = _row_tile(H, 512)
    hb = H // tr

    def body(c_ref, g_ref, r_ref, o_ref):
        o_ref[...] = (g_ref[...] + r_ref[...]).astype(BF16)

    core = lax.axis_index("c").astype(jnp.int32).reshape(1)
    return pl.pallas_call(
        body, name=name,
        grid_spec=pltpu.PrefetchScalarGridSpec(
            num_scalar_prefetch=1, grid=(S, hb),
            in_specs=[pl.BlockSpec((None, tr, C), lambda j, i, c_ref: (j, c_ref[0] * hb + i, 0)),
                      pl.BlockSpec((None, tr, C), lambda j, i, c_ref: (j, i, 0))],
            out_specs=pl.BlockSpec((None, tr, C), lambda j, i, c_ref: (j, i, 0))),
        out_shape=jax.ShapeDtypeStruct((S, H, C), BF16),
        compiler_params=_params(),
    )(core, g, r1)


_ANY = pl.BlockSpec(memory_space=pl.ANY)


def _place():
    x, y, c = lax.axis_index("x"), lax.axis_index("y"), lax.axis_index("c")
    chips = [(1 - x, y), (x, 1 - y), (1 - x, 1 - y)]
    return x, y, c, chips


def _comm_call(body, name, ins, out_shapes, n_remote, n_local):
    return pl.pallas_call(
        body, name=name,
        in_specs=[_ANY] * len(ins), out_specs=[_ANY] * len(out_shapes), out_shape=out_shapes,
        scratch_shapes=[pltpu.SemaphoreType.DMA((n_remote,)), pltpu.SemaphoreType.DMA((n_remote,)),
                        pltpu.SemaphoreType.DMA((max(n_local, 1),))],
    )(*ins)


def _remote(src, dst, send_sems, recv_sems, k, device):
    return pltpu.make_async_remote_copy(src_ref=src, dst_ref=dst, send_sem=send_sems.at[k], recv_sem=recv_sems.at[k],
                                        device_id=device, device_id_type=MESH)


def _gather_parts(shapes, w_refs, out_refs, send_sems, recv_sems):
    n = len(shapes)
    x, y, c, chips = _place()
    me = 2 * x + y
    sibling = (x, y, 1 - c)

    def half(k, chip_idx, core):
        H = shapes[k][0] // 2
        return out_refs[k].at[chip_idx, pl.ds(core * H, H), :]

    mine = [_remote(w_refs[k], out_refs[k].at[me], send_sems, recv_sems, 6 * n + k, sibling) for k in range(n)]
    first = []
    for k in range(n):
        H = shapes[k][0] // 2
        for j, (cx, cy) in enumerate(chips):
            first.append(_remote(w_refs[k].at[pl.ds(c * H, H), :], half(k, me, c), send_sems, recv_sems,
                                 3 * k + j, (cx, cy, c)))

    def start():
        for cp in mine + first:
            cp.start()

    def finish():
        passed = []
        for k in range(n):
            for j, (cx, cy) in enumerate(chips):
                landed = half(k, 2 * cx + cy, c)
                _remote(landed, landed, send_sems, recv_sems, 3 * k + j, (cx, cy, c)).wait_recv()
                fwd = _remote(landed, landed, send_sems, recv_sems, 3 * n + 3 * k + j, sibling)
                fwd.start()
                passed.append(fwd)
        for k in range(n):
            for j, (cx, cy) in enumerate(chips):
                other = half(k, 2 * cx + cy, 1 - c)
                _remote(other, other, send_sems, recv_sems, 3 * n + 3 * k + j, sibling).wait_recv()
        for cp in mine:
            cp.wait_recv()
        for cp in first + passed + mine:
            cp.wait_send()

    return start, finish


def _gather_weights(shards, name):
    n = len(shards)

    def body(*refs):
        start, finish = _gather_parts([w.shape for w in shards], refs[:n], refs[n:2 * n], *refs[2 * n:2 * n + 2])
        start()
        finish()

    return _comm_call(body, name, shards,
                      [jax.ShapeDtypeStruct((N_SHARD,) + w.shape, w.dtype) for w in shards], 7 * n, 0)


def _handshake(peers):
    barrier = pltpu.get_barrier_semaphore()
    for peer in peers:
        pl.semaphore_signal(barrier, inc=1, device_id=peer, device_id_type=MESH)
    pl.semaphore_wait(barrier, len(peers))


def _sequenced(body, name, ins, out_shapes, n_sems, collective_id):
    return pl.kernel(
        body, out_type=list(out_shapes), mesh=plsc.ScalarSubcoreMesh(axis_name="sequencer", num_cores=1), name=name,
        scratch_types=(pltpu.SemaphoreType.DMA((n_sems,)), pltpu.SemaphoreType.DMA((n_sems,))),
        compiler_params=pltpu.CompilerParams(collective_id=collective_id))(*ins)


def _swap_halves(gs, name, collective_id):
    n = len(gs)

    def body(*refs):
        g_refs, out_refs = refs[:n], refs[n:2 * n]
        send_sems, recv_sems = refs[2 * n:]
        x, y, c, _ = _place()
        _handshake([(x, y, 1 - c)])
        cps = []
        for k in range(n):
            H = gs[k].shape[1] // 2
            cp = _remote(g_refs[k].at[:, pl.ds((1 - c) * H, H), :], out_refs[k], send_sems, recv_sems, k, (x, y, 1 - c))
            cp.start()
            cps.append(cp)
        for cp in cps:
            cp.wait()

    return _sequenced(body, name, gs, [jax.ShapeDtypeStruct((g.shape[0], g.shape[1] // 2, g.shape[2]), g.dtype)
                                       for g in gs], n, collective_id)


def _exchange_chips(ts, name, collective_id):
    n = len(ts)

    def body(*refs):
        t_refs, out_refs = refs[:n], refs[n:2 * n]
        send_sems, recv_sems = refs[2 * n:]
        x, y, c, chips = _place()
        me = 2 * x + y
        _handshake([(cx, cy, c) for cx, cy in chips])
        sent = []
        for k in range(n):
            for j, (cx, cy) in enumerate(chips):
                cp = _remote(t_refs[k].at[2 * cx + cy], out_refs[k].at[me], send_sems, recv_sems, 3 * k + j, (cx, cy, c))
                cp.start()
                sent.append(cp)
        for k in range(n):
            for j, (cx, cy) in enumerate(chips):
                slot = out_refs[k].at[2 * cx + cy]
                _remote(slot, slot, send_sems, recv_sems, 3 * k + j, (cx, cy, c)).wait_recv()
        for cp in sent:
            cp.wait_send()

    return _sequenced(body, name, ts, [jax.ShapeDtypeStruct(t.shape, t.dtype) for t in ts], 3 * n, collective_id)


def _join_halves(fs, name):
    n = len(fs)

    def body(*refs):
        out_refs = refs[n:2 * n]
        send_sems, recv_sems, _ = refs[2 * n:]
        x, y, c, _ = _place()
        sent = []
        for k in range(n):
            H = fs[k].shape[0] // 2
            here = out_refs[k].at[pl.ds(c * H, H), :]
            cp = _remote(here, here, send_sems, recv_sems, k, (x, y, 1 - c))
            cp.start()
            sent.append(cp)
        for k in range(n):
            H = fs[k].shape[0] // 2
            other = out_refs[k].at[pl.ds((1 - c) * H, H), :]
            _remote(other, other, send_sems, recv_sems, k, (x, y, 1 - c)).wait_recv()
        for cp in sent:
            cp.wait_send()

    return pl.pallas_call(
        body, name=name,
        in_specs=[_ANY] * n, out_specs=[_ANY] * n,
        out_shape=[jax.ShapeDtypeStruct(f.shape, f.dtype) for f in fs],
        input_output_aliases={k: k for k in range(n)},
        scratch_shapes=[pltpu.SemaphoreType.DMA((n,)), pltpu.SemaphoreType.DMA((n,)), pltpu.SemaphoreType.DMA((1,))],
    )(*fs)


def _gather_small(v):
    R, C = v.shape

    def body(v_ref, out_ref, send_sems, recv_sems):
        x, y, c, _ = _place()
        me = 4 * x + 2 * y + c
        flips = [(fx, fy, fc) for fx in (0, 1) for fy in (0, 1) for fc in (0, 1)][1:]
        peers = [((1 - x) if fx else x, (1 - y) if fy else y, (1 - c) if fc else c) for fx, fy, fc in flips]
        _handshake(peers)
        sent = []
        for j, peer in enumerate(peers):
            cp = _remote(v_ref, out_ref.at[me], send_sems, recv_sems, j, peer)
            cp.start()
            sent.append(cp)
        for j, peer in enumerate(peers):
            slot = out_ref.at[4 * peer[0] + 2 * peer[1] + peer[2]]
            _remote(slot, slot, send_sems, recv_sems, j, peer).wait_recv()
        for cp in sent:
            cp.wait_send()

    return _sequenced(body, "gather_small", [v], [jax.ShapeDtypeStruct((8, R, C), F32)], 7,
                      COLLECTIVE_IDS["gather_small"])[0]


def _sum_devices(x, own, name):
    S, R, C = x.shape
    tr = _row_tile(R, 2048)

    def body(s_ref, x_ref, own_ref, o_ref):
        me = s_ref[0]
        acc = None
        for k in range(S):
            term = jnp.where(me == k, own_ref[...], x_ref[k])
            acc = term if acc is None else acc + term
        o_ref[...] = acc

    x_, y_, c_ = lax.axis_index("x"), lax.axis_index("y"), lax.axis_index("c")
    me = (4 * x_ + 2 * y_ + c_).astype(jnp.int32).reshape(1)
    return pl.pallas_call(
        body, name=name,
        grid_spec=pltpu.PrefetchScalarGridSpec(
            num_scalar_prefetch=1, grid=(R // tr,),
            in_specs=[pl.BlockSpec((S, tr, C), lambda i, s: (0, i, 0)), pl.BlockSpec((tr, C), lambda i, s: (i, 0))],
            out_specs=pl.BlockSpec((tr, C), lambda i, s: (i, 0))),
        out_shape=jax.ShapeDtypeStruct((R, C), F32),
        compiler_params=_params(),
    )(me, x, own)


def _after(earlier, arrays):
    return lax.optimization_barrier((earlier, arrays))


def _reduce_exchange(gs, names, tag, earlier):
    earlier, gs = _after(earlier, gs)
    r1 = _swap_halves(gs, "reduce_swap_" + tag, COLLECTIVE_IDS["swap_" + tag])
    ts = [_add_halves(g, r, "reduce_add_cores_" + nm) for g, r, nm in zip(gs, r1, names)]
    us = _exchange_chips(ts, "reduce_exchange_" + tag, COLLECTIVE_IDS["exchange_" + tag])
    return us, ts, earlier


def _reduce_finish(us, ts, names, tag):
    fs = [_sum_chips_into_half(u, t, "reduce_add_chips_" + nm) for u, t, nm in zip(us, ts, names)]
    return _join_halves(fs, "reduce_join_" + tag)


BIG = ["ffn1_w_gate", "ffn1_w_up", "ffn1_w_down", "w_in", "ssm_w_glu", "w_attn_branch", "w_ssm_branch",
       "w_out", "ffn2_w_gate", "ffn2_w_up", "ffn2_w_down"]
SMALL = ["ffn1_norm", "mix_norm", "gate_bias", "rel_bias_table", "ssm_a_re", "ssm_a_im", "ssm_log_dt",
         "ssm_b_re", "ssm_b_im", "ssm_c_re", "ssm_c_im", "ssm_d", "ffn2_norm", "final_norm"]
ORDER = ["ffn1_norm", "ffn1_w_gate", "ffn1_w_up", "ffn1_w_down", "mix_norm", "w_in", "gate_bias", "rel_bias_table",
         "ssm_a_re", "ssm_a_im", "ssm_log_dt", "ssm_b_re", "ssm_b_im", "ssm_c_re", "ssm_c_im", "ssm_d",
         "ssm_w_glu", "w_attn_branch", "w_ssm_branch", "w_out", "ffn2_norm", "ffn2_w_gate", "ffn2_w_up",
         "ffn2_w_down", "final_norm"]


_SMALL_TILE = 8 * LANES


def _pack_small(arrays):
    rows = []
    for a in arrays:
        flat = a.reshape(-1).astype(F32)
        rows.append(jnp.pad(flat, (0, (-flat.shape[0]) % _SMALL_TILE)).reshape(-1, LANES))
    return jnp.concatenate(rows, axis=0)


def _unpack_small(packed, shapes):
    out, r0 = [], 0
    for shp in shapes:
        n = math.prod(shp)
        rows = 8 * -(-n // _SMALL_TILE)
        out.append(packed[r0:r0 + rows].reshape(-1)[:n].reshape(shp))
        r0 += rows
    return out


def _split_cols(g):
    K, N = g.shape
    return g.reshape(K, N_SHARD, N // N_SHARD).transpose(1, 0, 2)


def _join_cols(w):
    S, K, n = w.shape
    return w.transpose(1, 0, 2).reshape(K, S * n)


COL_SHARDED = ("ssm_w_glu", "w_attn_branch", "w_ssm_branch")
TRANSPOSED = ("ffn1_w_gate", "ffn1_w_up", "ffn2_w_gate", "ffn2_w_up", "w_in")


def _shard_2d(name, arr):
    two_d = arr.reshape(arr.shape[-2:])
    return two_d.T if name in TRANSPOSED else two_d


def _shard_nd(name, two_d, shape):
    return (two_d.T if name in TRANSPOSED else two_d).reshape(shape)


class _GradSync:
    def __init__(self, weights, moms, vels):
        self.weights, self.moms, self.vels = weights, moms, vels
        self.grads, self.delta, self.new_m, self.new_v = {}, {}, {}, {}
        self.loss = None
        self._earlier = []
        self._exchanged = {}

    def grads_ready(self, tag, gw):
        names = REDUCE_GROUPS[tag]
        gs = []
        for n in names:
            g = gw[n]
            if n in COL_SHARDED:
                g = _split_cols(g)
            elif n in ("w_out", "w_in"):
                g = g.reshape(N_SHARD, g.shape[0] // N_SHARD, g.shape[1])
            gs.append(g)
        us, ts, _ = _reduce_exchange(gs, names, tag, self._earlier)
        self._exchanged[tag] = (us, ts)
        self._earlier = us

    def small_ready(self, gs, loss_blk):
        _, (mine,) = _after(self._earlier, [_pack_small([gs[n] for n in SMALL] + [loss_blk[0:1, :]])])
        others = _gather_small(mine)
        self._exchanged["small"] = (others, mine)
        self._earlier = [others]

    def finish(self, tag):
        made = []
        if tag == "small":
            others, mine = self._exchanged[tag]
            shapes = [self.weights[n].shape for n in SMALL]
            total = _unpack_small(_sum_devices(others, mine, "sum_small"), shapes + [(128,)])
            self.loss = total[-1][0]
            self.grads.update(zip(SMALL, total[:-1]))
            packed = [_pack_small([src[n] for n in SMALL]) for src in (self.weights, self.grads, self.moms, self.vels)]
            for dst, res in zip((self.delta, self.new_m, self.new_v), _adamw(*packed, "adamw_small")):
                dst.update(zip(SMALL, _unpack_small(res, shapes)))
            for n in SMALL:
                made += [self.grads[n], self.delta[n], self.new_m[n], self.new_v[n]]
            return made + [self.loss]
        names = REDUCE_GROUPS[tag]
        us, ts = self._exchanged[tag]
        for n, g in zip(names, _reduce_finish(us, ts, names, tag)):
            shp = self.weights[n].shape
            d, m, v = _adamw(_shard_2d(n, self.weights[n]), g, _shard_2d(n, self.moms[n]), _shard_2d(n, self.vels[n]),
                             "adamw_" + n)
            self.grads[n], self.delta[n] = _shard_nd(n, g, shp), _shard_nd(n, d, shp)
            self.new_m[n], self.new_v[n] = _shard_nd(n, m, shp), _shard_nd(n, v, shp)
            made += [self.grads[n], self.delta[n], self.new_m[n], self.new_v[n]]
        return made

    def finish_all(self):
        for tag in ("ffn2", "mixer", "w_in", "small", "ffn1"):
            self.finish(tag)


def _local_step(x, target, w, later, small, sync):
    L = x.shape[0]
    row = lambda v: v.reshape(1, -1)

    a_re, a_im = small["ssm_a_re"].reshape(1, NS), small["ssm_a_im"].reshape(1, NS)
    ldt = jnp.repeat(small["ssm_log_dt"].reshape(SSM_GROUPS), SSM_STATE).reshape(1, NS)
    to_cn = lambda b: b.reshape(SSM_GROUPS, SSM_STATE, SSM_GROUP).transpose(2, 0, 1).reshape(SSM_GROUP, NS)
    c_to_cn = lambda c: c.reshape(SSM_GROUPS, SSM_GROUP, SSM_STATE).transpose(1, 0, 2).reshape(SSM_GROUP, NS)
    b_re, b_im = to_cn(small["ssm_b_re"]), to_cn(small["ssm_b_im"])
    c_re, c_im = c_to_cn(small["ssm_c_re"]), c_to_cn(small["ssm_c_im"])
    d_skip = row(small["ssm_d"])
    pw, pwr, bd, cdt = _disc_fwd(a_re, a_im, ldt, b_re, b_im, c_re, c_im)

    onehot = _bucket_onehot()
    table_t = small["rel_bias_table"].T.reshape(3, HEADS_PER_GROUP, N_BUCKETS)
    table_t = jnp.pad(table_t, ((0, 0), (0, 8 - HEADS_PER_GROUP), (0, 0)))
    bias = _bias_expand(table_t, onehot)[:, :, :HEADS_PER_GROUP].reshape(
        3, 2, HEADS_PER_GROUP, ATTN_BLOCK, 2 * ATTN_BLOCK)

    n1, nm, n2, nf = row(small["ffn1_norm"]), row(small["mix_norm"]), row(small["ffn2_norm"]), row(small["final_norm"])
    gate_bias = row(small["gate_bias"])

    x1, a1, b1, *later_full = _ffn_fwd(x, n1, w["ffn1_w_gate"], w["ffn1_w_up"], w["ffn1_w_down"], "ffn1_fwd",
                                       carried=list(later.values()))
    w = dict(w, **dict(zip(later, later_full)))
    for n in COL_SHARDED:
        w[n] = _join_cols(w[n])
    w["w_out"] = w["w_out"].reshape(D_MODEL, D_MODEL)
    w["w_in"] = w["w_in"].reshape(IN_WIDTH, D_MODEL)
    *qkv, u, gates = _mix_in_fwd(x1, nm, w["w_in"], gate_bias)
    q, k, v = qkv[0:3], qkv[3:6], qkv[6:9]
    o_g, lse_g = [], []
    for grp in range(3):
        o, lse = _attn_fwd(q[grp], k[grp], v[grp], bias[grp], f"attn_fwd_{grp}")
        o_g.append(o)
        lse_g.append(lse)
    y, s = _ssm_fwd(u, bd, cdt, d_skip, pw)
    x2, o_attn, *lse_tot = _mix_out_fwd(x1, o_g, lse_g, y, gates, w["w_attn_branch"], w["ssm_w_glu"],
                                        w["w_ssm_branch"], w["w_out"])
    x3, a2, b2 = _ffn_fwd(x2, n2, w["ffn2_w_gate"], w["ffn2_w_up"], w["ffn2_w_down"], "ffn2_fwd")
    loss_blk, dx3, d_nf = _loss_fwd_bwd(x3, nf, target)

    gw, gs = {}, {}
    gs["final_norm"] = d_nf

    dx2, da, db, sact, h, d_out, gs["ffn2_norm"] = _ffn_bwd(dx3, x2, n2, a2, b2, w["ffn2_w_gate"], w["ffn2_w_up"],
                                                            w["ffn2_w_down"], "ffn2_bwd")
    gw["ffn2_w_gate"] = _matmul_tn(da, h[None], "ffn2_dw_gate")
    gw["ffn2_w_up"] = _matmul_tn(db, h[None], "ffn2_dw_up")
    gw["ffn2_w_down"] = _matmul_tn(sact, d_out[None], "ffn2_dw_down")
    sync.grads_ready("ffn2", gw)

    head_sum = (jnp.arange(GROUP_WIDTH)[:, None] // HEAD_DIM == jnp.arange(GROUP_WIDTH)[None, :] // HEAD_DIM).astype(F32)
    (*d_o_delta, dy, dgp, mix, dya, dys, ys2, gel, dglu, gs["gate_bias"]) = _mix_out_bwd(
        dx2, o_attn, y, gates, w["w_attn_branch"], w["ssm_w_glu"], w["w_ssm_branch"], w["w_out"], head_sum)
    d_o, delta = d_o_delta[0:3], d_o_delta[3:6]
    gw["w_out"] = _matmul_tn(mix[None], dx2[None], "dw_out")[0]
    gw["w_attn_branch"] = _matmul_tn(o_attn[None], dya[None], "dw_attn_branch")[0]
    gw["w_ssm_branch"] = _matmul_tn(ys2[None], dys[None], "dw_ssm_branch")[0]
    gw["ssm_w_glu"] = _matmul_tn(gel[None], dglu[None], "dw_glu")[0]
    sync.grads_ready("mixer", gw)

    dqs, dks, dvs, dsums = [], [], [], []
    for grp in range(3):
        dq, dk, dv, dsum = _attn_bwd(q[grp], k[grp], v[grp], d_o[grp], lse_tot[grp], delta[grp], bias[grp],
                                     f"attn_bwd_{grp}")
        dqs.append(dq)
        dks.append(dk)
        dvs.append(dv)
        dsums.append(dsum.reshape(HEADS_PER_GROUP, -1))
    dsum_all = jnp.pad(jnp.stack(dsums), ((0, 0), (0, 8 - HEADS_PER_GROUP), (0, 0)))
    d_table = _bias_reduce(dsum_all, onehot)[:, :HEADS_PER_GROUP]
    gs["rel_bias_table"] = d_table.reshape(3 * HEADS_PER_GROUP, N_BUCKETS).T

    du, gs["ssm_d"], d_bd, d_cdt, d_ab = _ssm_bwd(dy, u, s, bd, cdt, d_skip, pwr)
    group_sum = (jnp.arange(NS)[:, None] // SSM_STATE == jnp.arange(128)[None, :]).astype(F32)
    d_are, d_aim, d_ldt, d_bre, d_bim, d_cre, d_cim = _disc_bwd(a_re, a_im, ldt, b_re, b_im, d_bd, d_cdt, d_ab, group_sum)
    gs["ssm_a_re"], gs["ssm_a_im"] = d_are, d_aim
    gs["ssm_log_dt"] = d_ldt[0, :SSM_GROUPS]
    from_cn = lambda t: t.reshape(SSM_GROUP, SSM_GROUPS, SSM_STATE).transpose(1, 2, 0)
    c_from_cn = lambda t: t.reshape(SSM_GROUP, SSM_GROUPS, SSM_STATE).transpose(1, 0, 2)
    gs["ssm_b_re"], gs["ssm_b_im"] = from_cn(d_bre), from_cn(d_bim)
    gs["ssm_c_re"], gs["ssm_c_im"] = c_from_cn(d_cre), c_from_cn(d_cim)

    dx1, hm, dz, gs["mix_norm"] = _mix_in_bwd(dx2, x1, nm, dqs + dks + dvs, du, dgp, w["w_in"])
    gw["w_in"] = _matmul_tn(dz[None], hm[None], "dw_in")[0]
    sync.grads_ready("w_in", gw)

    dx0, da, db, sact, h, d_out, gs["ffn1_norm"] = _ffn_bwd(dx1, x, n1, a1, b1, w["ffn1_w_gate"], w["ffn1_w_up"],
                                                            w["ffn1_w_down"], "ffn1_bwd")
    sync.small_ready(gs, loss_blk)
    gw["ffn1_w_gate"] = _matmul_tn(da, h[None], "ffn1_dw_gate")
    gw["ffn1_w_up"] = _matmul_tn(db, h[None], "ffn1_dw_up")
    gw["ffn1_w_down"] = _matmul_tn(sact, d_out[None], "ffn1_dw_down")
    sync.grads_ready("ffn1", gw)
    return dx0


def kernel(x, ffn1_norm, ffn1_w_gate, ffn1_w_up, ffn1_w_down, mix_norm, w_in, gate_bias, rel_bias_table, ssm_a_re, ssm_a_im, ssm_log_dt, ssm_b_re, ssm_b_im, ssm_c_re, ssm_c_im, ssm_d, ssm_w_glu, w_attn_branch, w_ssm_branch, w_out, ffn2_norm, ffn2_w_gate, ffn2_w_up, ffn2_w_down, final_norm, loss_target, m_ffn1_norm, m_ffn1_w_gate, m_ffn1_w_up, m_ffn1_w_down, m_mix_norm, m_w_in, m_gate_bias, m_rel_bias_table, m_ssm_a_re, m_ssm_a_im, m_ssm_log_dt, m_ssm_b_re, m_ssm_b_im, m_ssm_c_re, m_ssm_c_im, m_ssm_d, m_ssm_w_glu, m_w_attn_branch, m_w_ssm_branch, m_w_out, m_ffn2_norm, m_ffn2_w_gate, m_ffn2_w_up, m_ffn2_w_down, m_final_norm, v_ffn1_norm, v_ffn1_w_gate, v_ffn1_w_up, v_ffn1_w_down, v_mix_norm, v_w_in, v_gate_bias, v_rel_bias_table, v_ssm_a_re, v_ssm_a_im, v_ssm_log_dt, v_ssm_b_re, v_ssm_b_im, v_ssm_c_re, v_ssm_c_im, v_ssm_d, v_ssm_w_glu, v_w_attn_branch, v_w_ssm_branch, v_w_out, v_ffn2_norm, v_ffn2_w_gate, v_ffn2_w_up, v_ffn2_w_down, v_final_norm):
    args = dict(locals())
    weights = {n: args[n] for n in ORDER}
    moms = {n: args["m_" + n] for n in ORDER}
    vels = {n: args["v_" + n] for n in ORDER}

    shard2d = {n: _shard_2d(n, weights[n]) for n in BIG}
    first, rest = BIG[:3], BIG[3:]
    full = dict(zip(first, _gather_weights([shard2d[n].astype(BF16) for n in first], "gather_ffn1_weights")))
    later = {n: shard2d[n].astype(BF16) for n in rest}

    small = {n: weights[n] for n in SMALL}
    sync = _GradSync(weights, moms, vels)
    grad_x = _local_step(x[0], loss_target[0], full, later, small, sync)
    sync.finish_all()
    return (sync.loss, grad_x[None], *[sync.grads[n] for n in ORDER], *[sync.delta[n] for n in ORDER],
            *[sync.new_m[n] for n in ORDER], *[sync.new_v[n] for n in ORDER])
```

```python
import functools
import math

import jax
import jax.numpy as jnp
from jax import lax
from jax.experimental import pallas as pl
from jax.experimental.pallas import tpu as pltpu
from jax.experimental.pallas import tpu_sc as plsc

F32 = jnp.float32
BF16 = jnp.bfloat16
MESH = pl.DeviceIdType.MESH

D_MODEL = 1024
D_FF = 2816
HEAD_DIM = 64
HEADS_PER_GROUP = 4
DILATIONS = (1, 4, 16)
WINDOW_STEPS = 128
ATTN_BLOCK = 128
ATTN_QB = 4
GROUP_WIDTH = HEADS_PER_GROUP * HEAD_DIM
ATTN_WIDTH = 3 * GROUP_WIDTH
N_BUCKETS = 32
MAX_DISTANCE = 2048
NEG_INF = -1e30
SSM_WIDTH = 512
SSM_GROUP = 16
SSM_GROUPS = 32
SSM_STATE = 64
NS = SSM_GROUPS * SSM_STATE
EPS = 1e-6
IN_WIDTH = 3 * ATTN_WIDTH + SSM_WIDTH + 2 * D_MODEL
Q_SCALE = HEAD_DIM ** -0.5
N_SHARD = 4
FF_SHARD = D_FF // N_SHARD
ADAM_LR, ADAM_B1, ADAM_B2, ADAM_EPS, ADAM_WD, ADAM_STEP = 0.001, 0.9, 0.999, 1e-08, 0.01, 10

LANES = 128
VMEM_LIMIT = 56 * 1024 * 1024
ROW_TILE = 512
FFN_FWD_TILE = 1024
FFN_BWD_TILE = 512
SSM_CHUNK = 256
SCAN_LANES = 512
ADAMW_BLOCK_BYTES = 1 << 20
TN_VMEM_BUDGET = 40 * 1024 * 1024
REDUCE_GROUPS = {
    "ffn2": ["ffn2_w_gate", "ffn2_w_up", "ffn2_w_down"],
    "mixer": ["w_out", "w_attn_branch", "w_ssm_branch", "ssm_w_glu"],
    "w_in": ["w_in"],
    "ffn1": ["ffn1_w_gate", "ffn1_w_up", "ffn1_w_down"],
}
COLLECTIVE_IDS = {name: i for i, name in enumerate(
    ["gather_small"] + [stage + "_" + tag for tag in REDUCE_GROUPS for stage in ("swap", "exchange")])}


def _params(**kw):
    return pltpu.CompilerParams(vmem_limit_bytes=VMEM_LIMIT, **kw)


def _dot(a, b):
    return jnp.dot(a, b, preferred_element_type=F32)


def _dot_nt(a, b):
    return lax.dot_general(a, b, (((1,), (1,)), ((), ())), preferred_element_type=F32)


def _dot_tn(a, b):
    return lax.dot_general(a, b, (((0,), (0,)), ((), ())), preferred_element_type=F32)


def _dot_exact(a, b):
    return jnp.dot(a, b, preferred_element_type=F32, precision=lax.Precision.HIGHEST)


def _dot_nt_exact(a, b):
    return lax.dot_general(a, b, (((1,), (1,)), ((), ())), preferred_element_type=F32,
                           precision=lax.Precision.HIGHEST)


def _rms(x):
    r = lax.rsqrt(jnp.mean(x * x, axis=-1, keepdims=True) + EPS)
    return r, x * r


def _rms_bwd(dh, g, r, xhat):
    dxh = dh * g
    return r * (dxh - xhat * jnp.mean(dxh * xhat, axis=-1, keepdims=True))


def _sigmoid(x):
    return 1.0 / (1.0 + jnp.exp(-x))


_GELU_C = math.sqrt(2.0 / math.pi)


def _gelu(x):
    return 0.5 * x * (1.0 + jnp.tanh(_GELU_C * (x + 0.044715 * x * x * x)))


def _gelu_grad(x):
    t = jnp.tanh(_GELU_C * (x + 0.044715 * x * x * x))
    return 0.5 * (1.0 + t) + 0.5 * x * (1.0 - t * t) * _GELU_C * (1.0 + 3 * 0.044715 * x * x)


def _whole():
    return pl.BlockSpec(memory_space=pltpu.VMEM)


def _row_tile(rows, cap):
    if rows <= cap:
        return rows
    return max(t for t in range(8, cap + 1, 8) if rows % t == 0)


def _rows(tm, w):
    return pl.BlockSpec((tm, w), lambda i: (i, 0))


def _acc_row(w):
    return pl.BlockSpec((1, w), lambda i: (0, 0))


def _ffn_fwd(x, g, wg, wu, wd, name, carried=()):
    L = x.shape[0]
    tm = min(FFN_FWD_TILE, L)
    n = len(carried)
    steps = L // tm
    last = N_SHARD - 1

    def body(x_ref, g_ref, wg_ref, wu_ref, wd_ref, *refs):
        shard_refs, (xo_ref, a_ref, b_ref), full_refs = refs[:n], refs[n:n + 3], refs[n + 3:2 * n + 3]
        h_ref, acc_ref, *sems = refs[2 * n + 3:]
        i = pl.program_id(0)
        j = pl.program_id(1)
        if n:
            start, finish = _gather_parts([w.shape for w in carried], shard_refs, full_refs, *sems)
            pl.when((i == 0) & (j == 0))(start)

        @pl.when(j == 0)
        def _():
            r, xhat = _rms(x_ref[...])
            h_ref[...] = (xhat * g_ref[...]).astype(BF16)
            acc_ref[...] = jnp.zeros_like(acc_ref)

        h = h_ref[...]
        a = _dot_nt(h, wg_ref[...])
        b = _dot_nt(h, wu_ref[...])
        a_ref[...] = a.astype(BF16)
        b_ref[...] = b.astype(BF16)
        s = (a * _sigmoid(a) * b).astype(BF16)
        acc_ref[...] += _dot(s, wd_ref[...])

        @pl.when(j == last)
        def _():
            xo_ref[...] = x_ref[...] + 0.5 * acc_ref[...]

        if n:
            pl.when((i == steps - 1) & (j == last))(finish)

    rows = pl.BlockSpec((tm, D_MODEL), lambda i, j: (i, 0))
    act = pl.BlockSpec((None, tm, FF_SHARD), lambda i, j: (j, i, 0))
    wt = pl.BlockSpec((None, FF_SHARD, D_MODEL), lambda i, j: (j, 0, 0))
    return pl.pallas_call(
        body, name=name, grid=(steps, N_SHARD),
        in_specs=[rows, _whole(), wt, wt, wt] + [_ANY] * n,
        out_specs=[rows, act, act] + [_ANY] * n,
        out_shape=[jax.ShapeDtypeStruct((L, D_MODEL), F32),
                   jax.ShapeDtypeStruct((N_SHARD, L, FF_SHARD), BF16),
                   jax.ShapeDtypeStruct((N_SHARD, L, FF_SHARD), BF16)]
        + [jax.ShapeDtypeStruct((N_SHARD,) + w.shape, w.dtype) for w in carried],
        scratch_shapes=[pltpu.VMEM((tm, D_MODEL), BF16), pltpu.VMEM((tm, D_MODEL), F32)]
        + ([pltpu.SemaphoreType.DMA((7 * n,)), pltpu.SemaphoreType.DMA((7 * n,))] if n else []),
        compiler_params=_params(),
    )(x, g, wg, wu, wd, *carried)


def _ffn_bwd(dxo, x, g, a, b, wg, wu, wd, name):
    L = x.shape[0]
    tm = min(FFN_BWD_TILE, L)
    last = N_SHARD - 1

    def body(dxo_ref, x_ref, g_ref, a_ref, b_ref, wg_ref, wu_ref, wd_ref,
             dxi_ref, da_ref, db_ref, s_ref, h_ref, do_ref, dg_ref, dh_ref):
        i = pl.program_id(0)
        j = pl.program_id(1)

        @pl.when(j == 0)
        def _():
            r, xhat = _rms(x_ref[...])
            h_ref[...] = (xhat * g_ref[...]).astype(BF16)
            do_ref[...] = (0.5 * dxo_ref[...]).astype(BF16)
            dh_ref[...] = jnp.zeros_like(dh_ref)

        av = a_ref[...].astype(F32)
        bv = b_ref[...].astype(F32)
        sg = _sigmoid(av)
        sl = av * sg
        ds = _dot_nt(do_ref[...], wd_ref[...])
        dbv = (ds * sl).astype(BF16)
        dav = (ds * bv * (sg * (1.0 + av * (1.0 - sg)))).astype(BF16)
        da_ref[...] = dav
        db_ref[...] = dbv
        s_ref[...] = (sl * bv).astype(BF16)
        dh_ref[...] += _dot(dav, wg_ref[...]) + _dot(dbv, wu_ref[...])

        @pl.when((i == 0) & (j == 0))
        def _():
            dg_ref[...] = jnp.zeros_like(dg_ref)

        @pl.when(j == last)
        def _():
            gv = g_ref[...]
            r, xhat = _rms(x_ref[...])
            dh = dh_ref[...]
            dg_ref[...] += jnp.sum(dh * xhat, axis=0, keepdims=True)
            dxi_ref[...] = dxo_ref[...] + _rms_bwd(dh, gv, r, xhat)

    rows = pl.BlockSpec((tm, D_MODEL), lambda i, j: (i, 0))
    act = pl.BlockSpec((None, tm, FF_SHARD), lambda i, j: (j, i, 0))
    wt = pl.BlockSpec((None, FF_SHARD, D_MODEL), lambda i, j: (j, 0, 0))
    act_shape = jax.ShapeDtypeStruct((N_SHARD, L, FF_SHARD), BF16)
    return pl.pallas_call(
        body, name=name, grid=(L // tm, N_SHARD),
        in_specs=[rows, rows, _whole(), act, act, wt, wt, wt],
        out_specs=[rows, act, act, act, rows, rows, pl.BlockSpec((1, D_MODEL), lambda i, j: (0, 0))],
        out_shape=[jax.ShapeDtypeStruct((L, D_MODEL), F32), act_shape, act_shape, act_shape,
                   jax.ShapeDtypeStruct((L, D_MODEL), BF16), jax.ShapeDtypeStruct((L, D_MODEL), BF16),
                   jax.ShapeDtypeStruct((1, D_MODEL), F32)],
        scratch_shapes=[pltpu.VMEM((tm, D_MODEL), F32)],
        compiler_params=_params(),
    )(dxo, x, g, a, b, wg, wu, wd)


def _matmul_tn(a, b, name):
    ja, L, K = a.shape
    jb, _, N = b.shape
    J = max(ja, jb)
    splits = [s for s in (1, 2, 4, 8) if s == 1 or N % (s * LANES) == 0]
    nsplit = next((s for s in splits if 2 * K * (N // s) * 4 <= TN_VMEM_BUDGET // 2), splits[-1])
    nc = N // nsplit
    left = TN_VMEM_BUDGET - 2 * K * nc * 4
    row_bytes = 2 * (K * a.dtype.itemsize + nc * b.dtype.itemsize)
    tm = next((t for t in (2048, 1024, 512, 256) if L % t == 0 and t * row_bytes <= left), min(128, L))

    def body(a_ref, b_ref, o_ref):
        @pl.when(pl.program_id(2) == 0)
        def _():
            o_ref[...] = jnp.zeros_like(o_ref)

        o_ref[...] += _dot_tn(a_ref[...].astype(BF16), b_ref[...].astype(BF16))

    return pl.pallas_call(
        body, name=name, grid=(J, nsplit, L // tm),
        in_specs=[pl.BlockSpec((None, tm, K), (lambda j, s, i: (j, i, 0)) if ja > 1 else (lambda j, s, i: (0, i, 0))),
                  pl.BlockSpec((None, tm, nc), (lambda j, s, i: (j, i, s)) if jb > 1 else (lambda j, s, i: (0, i, s)))],
        out_specs=pl.BlockSpec((None, K, nc), lambda j, s, i: (j, 0, s)),
        out_shape=jax.ShapeDtypeStruct((J, K, N), F32),
        compiler_params=_params(),
    )(a, b)


def _loss_fwd_bwd(x, g, target):
    L = x.shape[0]
    tm = min(ROW_TILE, L)

    def body(x_ref, g_ref, t_ref, loss_ref, dx_ref, dg_ref):
        i = pl.program_id(0)
        xv = x_ref[...]
        gv = g_ref[...]
        r, xhat = _rms(xv)
        err = xhat * gv - t_ref[...]
        part = 0.5 * jnp.sum(jnp.sum(err * err, axis=1, keepdims=True) * (1.0 / D_MODEL), axis=0, keepdims=True)
        dy = err * (1.0 / D_MODEL)

        @pl.when(i == 0)
        def _():
            dg_ref[...] = jnp.zeros_like(dg_ref)
            loss_ref[...] = jnp.zeros_like(loss_ref)

        loss_ref[...] += jnp.broadcast_to(part, loss_ref.shape)
        dg_ref[...] += jnp.sum(dy * xhat, axis=0, keepdims=True)
        dx_ref[...] = _rms_bwd(dy, gv, r, xhat)

    return pl.pallas_call(
        body, name="loss_fwd_bwd", grid=(L // tm,),
        in_specs=[_rows(tm, D_MODEL), _whole(), _rows(tm, D_MODEL)],
        out_specs=[pl.BlockSpec((8, 128), lambda i: (0, 0)), _rows(tm, D_MODEL), _acc_row(D_MODEL)],
        out_shape=[jax.ShapeDtypeStruct((8, 128), F32), jax.ShapeDtypeStruct((L, D_MODEL), F32),
                   jax.ShapeDtypeStruct((1, D_MODEL), F32)],
        compiler_params=_params(),
    )(x, g, target)


_C_K = ATTN_WIDTH
_C_V = 2 * ATTN_WIDTH
_C_U = 3 * ATTN_WIDTH
_C_G = _C_U + SSM_WIDTH


def _residue_spec(d, tm):
    return pl.BlockSpec((d, tm // d, GROUP_WIDTH), lambda i: (0, i, 0))


def _residue_shape(d, L, dtype):
    return jax.ShapeDtypeStruct((d, L // d, GROUP_WIDTH), dtype)


def _residue_scratch(tm):
    return pltpu.VMEM((GROUP_WIDTH // LANES, tm, LANES), F32)


def _to_residues(val, out_ref, scr, d):
    if d == 1:
        out_ref[0] = val.astype(out_ref.dtype)
        return
    tm = val.shape[0]
    for half in range(GROUP_WIDTH // LANES):
        cols = slice(half * LANES, (half + 1) * LANES)
        scr[half] = val[:, cols]
        for r in range(d):
            out_ref[r, :, cols] = scr[half, pl.ds(r, tm // d, stride=d), :].astype(out_ref.dtype)


def _from_residues(ref, scr, d):
    if d == 1:
        return ref[0].astype(F32)
    rows = ref.shape[1]
    for half in range(GROUP_WIDTH // LANES):
        cols = slice(half * LANES, (half + 1) * LANES)
        for r in range(d):
            scr[half, pl.ds(r, rows, stride=d), :] = ref[r, :, cols].astype(F32)
    return jnp.concatenate([scr[half] for half in range(GROUP_WIDTH // LANES)], axis=1)


def _mix_in_fwd(x, g, w_in, gate_bias):
    L = x.shape[0]
    tm = min(ROW_TILE, L)

    def body(x_ref, g_ref, w_ref, gb_ref, *refs):
        qkv_refs, (u_ref, gate_ref, scr) = refs[:9], refs[9:]
        r, xhat = _rms(x_ref[...])
        h = (xhat * g_ref[...]).astype(BF16)
        for part, (c0, scale) in enumerate(((0, Q_SCALE), (_C_K, 1.0), (_C_V, 1.0))):
            z = _dot_nt(h, w_ref[c0:c0 + ATTN_WIDTH, :]) * scale
            for grp, d in enumerate(DILATIONS):
                _to_residues(z[:, grp * GROUP_WIDTH:(grp + 1) * GROUP_WIDTH], qkv_refs[3 * part + grp], scr, d)
        u_ref[...] = _dot_nt(h, w_ref[_C_U:_C_G, :])
        gate_ref[...] = _sigmoid(_dot_nt(h, w_ref[_C_G:IN_WIDTH, :]) + gb_ref[...])

    return pl.pallas_call(
        body, name="mix_in_fwd", grid=(L // tm,),
        in_specs=[_rows(tm, D_MODEL), _whole(), _whole(), _whole()],
        out_specs=[_residue_spec(d, tm) for d in DILATIONS] * 3 + [_rows(tm, SSM_WIDTH), _rows(tm, 2 * D_MODEL)],
        out_shape=[_residue_shape(d, L, BF16) for d in DILATIONS] * 3
        + [jax.ShapeDtypeStruct((L, SSM_WIDTH), F32), jax.ShapeDtypeStruct((L, 2 * D_MODEL), F32)],
        scratch_shapes=[_residue_scratch(tm)],
        compiler_params=_params(),
    )(x, g, w_in, gate_bias)


def _mix_in_bwd(dx2, x, g, dqkv, du, dgp, w_in):
    L = x.shape[0]
    tm = min(ROW_TILE, L)

    def body(dx2_ref, x_ref, g_ref, *refs):
        piece_refs = refs[:9]
        du_ref, dgp_ref, w_ref, dx1_ref, h_ref, dz_ref, dg_ref, scr = refs[9:]
        i = pl.program_id(0)
        gv = g_ref[...]
        r, xhat = _rms(x_ref[...])
        h_ref[...] = (xhat * gv).astype(BF16)
        for part in range(3):
            for grp, d in enumerate(DILATIONS):
                c0 = part * ATTN_WIDTH + grp * GROUP_WIDTH
                dz_ref[:, c0:c0 + GROUP_WIDTH] = _from_residues(piece_refs[3 * part + grp], scr, d).astype(BF16)
        dz_ref[:, _C_U:_C_G] = du_ref[...].astype(BF16)
        dz_ref[:, _C_G:IN_WIDTH] = dgp_ref[...]
        dh = _dot(dz_ref[...], w_ref[...])

        @pl.when(i == 0)
        def _():
            dg_ref[...] = jnp.zeros_like(dg_ref)

        dg_ref[...] += jnp.sum(dh * xhat, axis=0, keepdims=True)
        dx1_ref[...] = dx2_ref[...] + _rms_bwd(dh, gv, r, xhat)

    return pl.pallas_call(
        body, name="mix_in_bwd", grid=(L // tm,),
        in_specs=[_rows(tm, D_MODEL), _rows(tm, D_MODEL), _whole()] + [_residue_spec(d, tm) for d in DILATIONS] * 3
        + [_rows(tm, SSM_WIDTH), _rows(tm, 2 * D_MODEL), _whole()],
        out_specs=[_rows(tm, D_MODEL), _rows(tm, D_MODEL), _rows(tm, IN_WIDTH), _acc_row(D_MODEL)],
        out_shape=[jax.ShapeDtypeStruct((L, D_MODEL), F32), jax.ShapeDtypeStruct((L, D_MODEL), BF16),
                   jax.ShapeDtypeStruct((L, IN_WIDTH), BF16), jax.ShapeDtypeStruct((1, D_MODEL), F32)],
        scratch_shapes=[_residue_scratch(tm)],
        compiler_params=_params(),
    )(dx2, x, g, *dqkv, du, dgp, w_in)


def _bucket_onehot():
    qi = jnp.arange(ATTN_BLOCK)[:, None]
    kj = jnp.arange(2 * ATTN_BLOCK)[None, :]
    steps = jnp.maximum(qi + ATTN_BLOCK - kj, 0)
    max_exact = N_BUCKETS // 2
    out = []
    for d in DILATIONS:
        dist = steps * d
        df = jnp.maximum(dist, 1).astype(F32)
        large = max_exact + (jnp.log(df / max_exact) / math.log(MAX_DISTANCE / max_exact)
                             * (N_BUCKETS - max_exact)).astype(jnp.int32)
        large = jnp.minimum(large, N_BUCKETS - 1)
        bucket = jnp.where(dist < max_exact, dist, large).reshape(-1)
        out.append((bucket[None, :] == jnp.arange(N_BUCKETS)[:, None]).astype(F32))
    return jnp.stack(out)


def _bias_expand(table_t, onehot):
    n = onehot.shape[-1]

    def body(t_ref, oh_ref, o_ref):
        bias = _dot_exact(t_ref[...], oh_ref[...])
        col = lax.broadcasted_iota(jnp.int32, (8, n), 1)
        qi = col // (2 * ATTN_BLOCK)
        kj = col - qi * (2 * ATTN_BLOCK)
        steps = qi + ATTN_BLOCK - kj
        band = (steps >= 0) & (steps <= WINDOW_STEPS)
        o_ref[0] = jnp.where(band & (kj >= ATTN_BLOCK), bias, NEG_INF)
        o_ref[1] = jnp.where(band, bias, NEG_INF)

    return pl.pallas_call(
        body, name="bias_expand", grid=(3,),
        in_specs=[pl.BlockSpec((None, 8, N_BUCKETS), lambda g: (g, 0, 0)),
                  pl.BlockSpec((None, N_BUCKETS, n), lambda g: (g, 0, 0))],
        out_specs=pl.BlockSpec((None, 2, 8, n), lambda g: (g, 0, 0, 0)),
        out_shape=jax.ShapeDtypeStruct((3, 2, 8, n), F32),
        compiler_params=_params(),
    )(table_t, onehot)


def _bias_reduce(dsum, onehot):
    n = onehot.shape[-1]

    def body(d_ref, oh_ref, o_ref):
        o_ref[...] = _dot_nt_exact(d_ref[...], oh_ref[...])

    return pl.pallas_call(
        body, name="bias_reduce", grid=(3,),
        in_specs=[pl.BlockSpec((None, 8, n), lambda g: (g, 0, 0)),
                  pl.BlockSpec((None, N_BUCKETS, n), lambda g: (g, 0, 0))],
        out_specs=pl.BlockSpec((None, 8, N_BUCKETS), lambda g: (g, 0, 0)),
        out_shape=jax.ShapeDtypeStruct((3, 8, N_BUCKETS), F32),
        compiler_params=_params(),
    )(dsum, onehot)


def _head_of_col(rows):
    return lax.broadcasted_iota(jnp.int32, (rows, GROUP_WIDTH), 1) // HEAD_DIM


def _attn_specs(qb):
    rows = qb * ATTN_BLOCK
    cur = pl.BlockSpec((None, rows, GROUP_WIDTH), lambda r, n: (r, n, 0))
    prev = pl.BlockSpec((None, ATTN_BLOCK, GROUP_WIDTH), lambda r, n: (r, jnp.maximum(n * qb - 1, 0), 0))
    bias = pl.BlockSpec((2, HEADS_PER_GROUP, ATTN_BLOCK, 2 * ATTN_BLOCK), lambda r, n: (0, 0, 0, 0))
    return cur, prev, bias


def _attn_fwd(q, k, v, bias, name):
    d, M, _ = q.shape
    nb = M // ATTN_BLOCK
    qb = min(ATTN_QB, nb)

    def body(q_ref, kp_ref, kc_ref, vp_ref, vc_ref, bias_ref, o_ref, lse_ref):
        n = pl.program_id(1)
        q_head = _head_of_col(ATTN_BLOCK)
        kv_head = _head_of_col(2 * ATTN_BLOCK)
        kwin = jnp.concatenate([kp_ref[...], kc_ref[...]], axis=0)
        vwin = jnp.concatenate([vp_ref[...], vc_ref[...]], axis=0)
        for b in range(qb):
            rows = slice(b * ATTN_BLOCK, (b + 1) * ATTN_BLOCK)
            window = slice(b * ATTN_BLOCK, (b + 2) * ATTN_BLOCK)
            variant = jnp.minimum(n, 1) if b == 0 else 1
            qv = q_ref[rows, :]
            kk = kwin[window]
            vv = vwin[window]
            o_acc = jnp.zeros((ATTN_BLOCK, GROUP_WIDTH), F32)
            lse_acc = jnp.zeros((ATTN_BLOCK, GROUP_WIDTH), F32)
            for hh in range(HEADS_PER_GROUP):
                hm = q_head == hh
                qh = jnp.where(hm, qv, jnp.zeros_like(qv))
                logits = _dot_nt(qh, kk) + bias_ref[variant, hh]
                m = jnp.max(logits, axis=1, keepdims=True)
                p = jnp.exp(logits - m)
                vh = jnp.where(kv_head == hh, vv, jnp.ones_like(vv))
                pv = _dot(p.astype(BF16), vh)
                c_sum = ((hh + 1) % HEADS_PER_GROUP) * HEAD_DIM
                den = pv[:, c_sum:c_sum + 1]
                o_acc = jnp.where(hm, pv * (1.0 / den), o_acc)
                lse_acc = jnp.where(hm, m + jnp.log(den), lse_acc)
            o_ref[rows, :] = o_acc
            lse_ref[rows, :] = lse_acc

    cur, prev, full = _attn_specs(qb)
    return pl.pallas_call(
        body, name=name, grid=(d, nb // qb),
        in_specs=[cur, prev, cur, prev, cur, full],
        out_specs=[cur, cur],
        out_shape=[jax.ShapeDtypeStruct((d, M, GROUP_WIDTH), F32)] * 2,
        compiler_params=_params(),
    )(q, k, k, v, v, bias)


def _attn_bwd(q, k, v, do, lse, delta, bias, name):
    d, M, _ = q.shape
    nb = M // ATTN_BLOCK
    qb = min(ATTN_QB, nb)
    ns = nb // qb
    rows_q = qb * ATTN_BLOCK
    last = slice(rows_q - ATTN_BLOCK, rows_q)

    def body(q_ref, kp_ref, kc_ref, vp_ref, vc_ref, do_ref, lse_ref, dl_ref, bias_ref,
             dq_ref, dk_ref, dv_ref, dsum_ref, pk_ref, pv_ref, wk_ref, wv_ref):
        r = pl.program_id(0)
        n = pl.program_id(1)

        @pl.when((r == 0) & (n == 0))
        def _():
            dsum_ref[...] = jnp.zeros_like(dsum_ref)

        @pl.when(n == 0)
        def _():
            pk_ref[...] = jnp.zeros_like(pk_ref)
            pv_ref[...] = jnp.zeros_like(pv_ref)

        @pl.when(n < ns)
        def _():
            q_head = _head_of_col(ATTN_BLOCK)
            kwin = jnp.concatenate([kp_ref[...], kc_ref[...]], axis=0)
            vwin = jnp.concatenate([vp_ref[...], vc_ref[...]], axis=0)
            wk_ref[...] = jnp.zeros_like(wk_ref)
            wv_ref[...] = jnp.zeros_like(wv_ref)
            for b in range(qb):
                rows = slice(b * ATTN_BLOCK, (b + 1) * ATTN_BLOCK)
                window = slice(b * ATTN_BLOCK, (b + 2) * ATTN_BLOCK)
                variant = jnp.minimum(n, 1) if b == 0 else 1
                qv = q_ref[rows, :]
                dov = do_ref[rows, :]
                kk = kwin[window]
                vv = vwin[window]
                dq_acc = jnp.zeros((ATTN_BLOCK, GROUP_WIDTH), F32)
                dkk = jnp.zeros((2 * ATTN_BLOCK, GROUP_WIDTH), F32)
                dvv = jnp.zeros((2 * ATTN_BLOCK, GROUP_WIDTH), F32)
                for hh in range(HEADS_PER_GROUP):
                    hm = q_head == hh
                    c0 = hh * HEAD_DIM
                    qh = jnp.where(hm, qv, jnp.zeros_like(qv))
                    doh = jnp.where(hm, dov, jnp.zeros_like(dov))
                    logits = _dot_nt(qh, kk) + bias_ref[variant, hh]
                    p = jnp.exp(logits - lse_ref[rows, c0:c0 + 1])
                    dp = _dot_nt(doh, vv)
                    ds = p * (dp - dl_ref[rows, c0:c0 + 1])
                    dsum_ref[hh] += ds
                    ds16 = ds.astype(BF16)
                    dq_acc = jnp.where(hm, _dot(ds16, kk), dq_acc)
                    dkk = dkk + _dot_tn(ds16, qh)
                    dvv = dvv + _dot_tn(p.astype(BF16), doh)
                dq_ref[rows, :] = (dq_acc * Q_SCALE).astype(BF16)
                wk_ref[window, :] += dkk
                wv_ref[window, :] += dvv
            for out_ref, part_ref, win_ref in ((dk_ref, pk_ref, wk_ref), (dv_ref, pv_ref, wv_ref)):
                if qb > 1:
                    out_ref[0:rows_q - ATTN_BLOCK, :] = part_ref[0:rows_q - ATTN_BLOCK, :].astype(BF16)
                out_ref[last, :] = (part_ref[last, :] + win_ref[0:ATTN_BLOCK, :]).astype(BF16)
                part_ref[...] = win_ref[ATTN_BLOCK:, :]

        @pl.when(n == ns)
        def _():
            dk_ref[...] = pk_ref[...].astype(BF16)
            dv_ref[...] = pv_ref[...].astype(BF16)

    def clamp(n):
        return jnp.minimum(n, ns - 1)

    cur = pl.BlockSpec((None, rows_q, GROUP_WIDTH), lambda r, n: (r, clamp(n), 0))
    prev = pl.BlockSpec((None, ATTN_BLOCK, GROUP_WIDTH), lambda r, n: (r, jnp.maximum(clamp(n) * qb - 1, 0), 0))
    lag = pl.BlockSpec((None, rows_q, GROUP_WIDTH), lambda r, n: (r, jnp.maximum(n - 1, 0), 0))
    full = pl.BlockSpec((2, HEADS_PER_GROUP, ATTN_BLOCK, 2 * ATTN_BLOCK), lambda r, n: (0, 0, 0, 0))
    acc = pl.BlockSpec((HEADS_PER_GROUP, ATTN_BLOCK, 2 * ATTN_BLOCK), lambda r, n: (0, 0, 0))
    return pl.pallas_call(
        body, name=name, grid=(d, ns + 1),
        in_specs=[cur, prev, cur, prev, cur, cur, cur, cur, full],
        out_specs=[cur, lag, lag, acc],
        out_shape=[jax.ShapeDtypeStruct((d, M, GROUP_WIDTH), BF16)] * 3
        + [jax.ShapeDtypeStruct((HEADS_PER_GROUP, ATTN_BLOCK, 2 * ATTN_BLOCK), F32)],
        scratch_shapes=[pltpu.VMEM((rows_q, GROUP_WIDTH), F32), pltpu.VMEM((rows_q, GROUP_WIDTH), F32),
                        pltpu.VMEM((rows_q + ATTN_BLOCK, GROUP_WIDTH), F32),
                        pltpu.VMEM((rows_q + ATTN_BLOCK, GROUP_WIDTH), F32)],
        compiler_params=_params(),
    )(q, k, k, v, v, do, lse, delta, bias)


def _disc_math(a_re, a_im, ldt, b_re, b_im):
    dt = jnp.exp(ldt)
    mag = jnp.exp(a_re * dt)
    ab_re = mag * jnp.cos(a_im * dt)
    ab_im = mag * jnp.sin(a_im * dt)
    den = a_re * a_re + a_im * a_im
    xr = ab_re - 1.0
    coef_re = (xr * a_re + ab_im * a_im) / den
    coef_im = (ab_im * a_re - xr * a_im) / den
    return ab_re, ab_im, coef_re * b_re - coef_im * b_im, coef_re * b_im + coef_im * b_re


def _block_diag_mask():
    row_g = lax.broadcasted_iota(jnp.int32, (SSM_WIDTH, 2 * NS), 0) // SSM_GROUP
    col = lax.broadcasted_iota(jnp.int32, (SSM_WIDTH, 2 * NS), 1)
    col_g = jnp.where(col >= NS, col - NS, col) // SSM_STATE
    return row_g == col_g


def _disc_fwd(a_re, a_im, ldt, b_re, b_im, c_re, c_im):
    def body(are_ref, aim_ref, ldt_ref, bre_ref, bim_ref, cre_ref, cim_ref, pw_ref, pwr_ref, bd_ref, cdt_ref):
        ab_re, ab_im, bb_re, bb_im = _disc_math(are_ref[...], aim_ref[...], ldt_ref[...], bre_ref[...], bim_ref[...])
        row = lax.broadcasted_iota(jnp.int32, (8, NS), 0)
        pr, pi = ab_re, ab_im
        t_re = jnp.zeros((8, NS), F32)
        t_im = jnp.zeros((8, NS), F32)
        u_re = jnp.zeros((8, NS), F32)
        u_im = jnp.zeros((8, NS), F32)
        for j in range(8):
            t_re = jnp.where(row == j, pr, t_re)
            t_im = jnp.where(row == j, pi, t_im)
            u_re = jnp.where(row == 7 - j, pr, u_re)
            u_im = jnp.where(row == 7 - j, pi, u_im)
            pr, pi = pr * ab_re - pi * ab_im, pr * ab_im + pi * ab_re
        pw_ref[0] = t_re
        pw_ref[1] = t_im
        pwr_ref[0] = u_re
        pwr_ref[1] = u_im
        mask = _block_diag_mask()
        zero = jnp.zeros((SSM_WIDTH, 2 * NS), F32)
        bfull = jnp.concatenate([jnp.concatenate([bb_re] * SSM_GROUPS, axis=0),
                                 jnp.concatenate([bb_im] * SSM_GROUPS, axis=0)], axis=1)
        bd_ref[...] = jnp.where(mask, bfull, zero).astype(BF16)
        cfull = jnp.concatenate([jnp.concatenate([cre_ref[...]] * SSM_GROUPS, axis=0),
                                 jnp.concatenate([-cim_ref[...]] * SSM_GROUPS, axis=0)], axis=1)
        cdt_ref[...] = jnp.where(mask, cfull, zero).astype(BF16)

    return pl.pallas_call(
        body, name="s5_disc_fwd",
        in_specs=[_whole()] * 7, out_specs=[_whole()] * 4,
        out_shape=[jax.ShapeDtypeStruct((2, 8, NS), F32), jax.ShapeDtypeStruct((2, 8, NS), F32),
                   jax.ShapeDtypeStruct((SSM_WIDTH, 2 * NS), BF16), jax.ShapeDtypeStruct((SSM_WIDTH, 2 * NS), BF16)],
        compiler_params=_params(),
    )(a_re, a_im, ldt, b_re, b_im, c_re, c_im)


def _disc_bwd(a_re, a_im, ldt, b_re, b_im, d_bd, d_cdt, d_ab, group_sum):
    def body(are_ref, aim_ref, ldt_ref, bre_ref, bim_ref, dbd_ref, dcdt_ref, dab_ref, gs_ref,
             dare_ref, daim_ref, dldt_ref, dbre_ref, dbim_ref, dcre_ref, dcim_ref):
        col = lax.broadcasted_iota(jnp.int32, (SSM_GROUP, 2 * NS), 1)
        col_g = jnp.where(col >= NS, col - NS, col) // SSM_STATE
        acc_b = jnp.zeros((SSM_GROUP, 2 * NS), F32)
        acc_c = jnp.zeros((SSM_GROUP, 2 * NS), F32)
        for g in range(SSM_GROUPS):
            rows = slice(g * SSM_GROUP, (g + 1) * SSM_GROUP)
            acc_b = acc_b + jnp.where(col_g == g, dbd_ref[rows, :], 0.0)
            acc_c = acc_c + jnp.where(col_g == g, dcdt_ref[rows, :], 0.0)
        dcre_ref[...] = acc_c[:, :NS]
        dcim_ref[...] = -acc_c[:, NS:]
        dab_re = jnp.sum(dab_ref[0], axis=0, keepdims=True)
        dab_im = jnp.sum(dab_ref[1], axis=0, keepdims=True)
        _, vjp = jax.vjp(_disc_math, are_ref[...], aim_ref[...], ldt_ref[...], bre_ref[...], bim_ref[...])
        d_are, d_aim, d_ldt, d_bre, d_bim = vjp((dab_re, dab_im, acc_b[:, :NS], acc_b[:, NS:]))
        dare_ref[...] = d_are
        daim_ref[...] = d_aim
        dbre_ref[...] = d_bre
        dbim_ref[...] = d_bim
        dldt_ref[...] = _dot_exact(jnp.broadcast_to(d_ldt, (8, NS)), gs_ref[...])

    vec = jax.ShapeDtypeStruct((1, NS), F32)
    mat = jax.ShapeDtypeStruct((SSM_GROUP, NS), F32)
    return pl.pallas_call(
        body, name="s5_disc_bwd",
        in_specs=[_whole()] * 9, out_specs=[_whole()] * 7,
        out_shape=[vec, vec, jax.ShapeDtypeStruct((8, 128), F32), mat, mat, mat, mat],
        compiler_params=_params(),
    )(a_re, a_im, ldt, b_re, b_im, d_bd, d_cdt, d_ab, group_sum)


def _scan_blocks(buf, pw_ref, carry_ref, n_blocks, reverse):
    row = lax.broadcasted_iota(jnp.int32, (8, SCAN_LANES), 0)
    for lc in range(NS // SCAN_LANES):
        re_cols = pl.ds(lc * SCAN_LANES, SCAN_LANES)
        im_cols = pl.ds(NS + lc * SCAN_LANES, SCAN_LANES)
        pr = pw_ref[0, :, re_cols]
        pi = pw_ref[1, :, re_cols]
        if reverse:
            pi = -pi
            base = [(7, 1), (6, 2), (4, 4)]
            coef = [(jnp.where(row < 8 - k, pr[j:j + 1], 0.0), jnp.where(row < 8 - k, pi[j:j + 1], 0.0), 8 - k)
                    for j, k in base]
        else:
            base = [(0, 1), (1, 2), (3, 4)]
            coef = [(jnp.where(row >= k, pr[j:j + 1], 0.0), jnp.where(row >= k, pi[j:j + 1], 0.0), k)
                    for j, k in base]

        def step(i, carry, pr=pr, pi=pi, coef=coef, re_cols=re_cols, im_cols=im_cols):
            cr, ci = carry
            blk = (n_blocks - 1 - i) if reverse else i
            rows = pl.ds(pl.multiple_of(blk * 8, 8), 8)
            xr = buf[rows, re_cols]
            xi = buf[rows, im_cols]
            for kr, ki, shift in coef:
                sr = pltpu.roll(xr, shift, 0)
                si = pltpu.roll(xi, shift, 0)
                xr, xi = xr + kr * sr - ki * si, xi + kr * si + ki * sr
            xr, xi = xr + pr * cr - pi * ci, xi + pr * ci + pi * cr
            buf[rows, re_cols] = xr
            buf[rows, im_cols] = xi
            edge = slice(0, 1) if reverse else slice(7, 8)
            return xr[edge], xi[edge]

        cr, ci = lax.fori_loop(0, n_blocks, step, (carry_ref[0:1, re_cols], carry_ref[0:1, im_cols]))
        carry_ref[0:1, re_cols] = cr
        carry_ref[0:1, im_cols] = ci


_SUPER_GROUPS = 16
_SUPER_BLOCKS = [
    (slice(k * _SUPER_GROUPS * SSM_GROUP, (k + 1) * _SUPER_GROUPS * SSM_GROUP),
     [slice(half + k * _SUPER_GROUPS * SSM_STATE, half + (k + 1) * _SUPER_GROUPS * SSM_STATE) for half in (0, NS)])
    for k in range(SSM_GROUPS // _SUPER_GROUPS)]


def _ssm_fwd(u, bd, cdt, d_skip, pw):
    L = u.shape[0]
    tc = min(SSM_CHUNK, L)

    def body(u_ref, bd_ref, cdt_ref, dsk_ref, pw_ref, y_ref, s_ref, carry_ref):
        @pl.when(pl.program_id(0) == 0)
        def _():
            carry_ref[...] = jnp.zeros_like(carry_ref)

        uv = u_ref[...]
        u16 = uv.astype(BF16)
        for ch, states in _SUPER_BLOCKS:
            for st in states:
                s_ref[:, st] = _dot(u16[:, ch], bd_ref[ch, st])
        _scan_blocks(s_ref, pw_ref, carry_ref, tc // 8, reverse=False)
        for ch, states in _SUPER_BLOCKS:
            y_ref[:, ch] = (sum(_dot_nt(s_ref[:, st].astype(BF16), cdt_ref[ch, st]) for st in states)
                            + dsk_ref[:, ch] * uv[:, ch])

    return pl.pallas_call(
        body, name="s5_fwd", grid=(L // tc,),
        in_specs=[_rows(tc, SSM_WIDTH), _whole(), _whole(), _whole(), _whole()],
        out_specs=[_rows(tc, SSM_WIDTH), _rows(tc, 2 * NS)],
        out_shape=[jax.ShapeDtypeStruct((L, SSM_WIDTH), F32), jax.ShapeDtypeStruct((L, 2 * NS), F32)],
        scratch_shapes=[pltpu.VMEM((8, 2 * NS), F32)],
        compiler_params=_params(),
    )(u, bd, cdt, d_skip, pw)


def _ssm_bwd(dy, u, s, bd, cdt, d_skip, pwr):
    L = u.shape[0]
    tc = min(SSM_CHUNK, L)
    nc = L // tc
    blocks = tc // 8

    def body(dy_ref, u_ref, s_ref, sprev_ref, bd_ref, cdt_ref, dsk_ref, pwr_ref,
             du_ref, ddsk_ref, dbd_ref, dcdt_ref, dab_ref, g_ref, sx_ref, carry_ref):
        i = pl.program_id(0)

        @pl.when(i == 0)
        def _():
            carry_ref[...] = jnp.zeros_like(carry_ref)
            ddsk_ref[...] = jnp.zeros_like(ddsk_ref)
            dbd_ref[...] = jnp.zeros_like(dbd_ref)
            dcdt_ref[...] = jnp.zeros_like(dcdt_ref)
            dab_ref[...] = jnp.zeros_like(dab_ref)

        dyv = dy_ref[...]
        uv = u_ref[...]
        dy16 = dyv.astype(BF16)
        u16 = uv.astype(BF16)
        for ch, states in _SUPER_BLOCKS:
            for st in states:
                g_ref[:, st] = _dot(dy16[:, ch], cdt_ref[ch, st])
        _scan_blocks(g_ref, pwr_ref, carry_ref, blocks, reverse=True)
        ddsk_ref[...] += jnp.sum(dyv * uv, axis=0, keepdims=True)
        for ch, states in _SUPER_BLOCKS:
            du = dsk_ref[:, ch] * dyv[:, ch]
            for st in states:
                g16 = g_ref[:, st].astype(BF16)
                du = du + _dot_nt(g16, bd_ref[ch, st])
                dbd_ref[ch, st] += _dot_tn(u16[:, ch], g16)
                dcdt_ref[ch, st] += _dot_tn(dy16[:, ch], s_ref[:, st].astype(BF16))
            du_ref[:, ch] = du

        sx_ref[pl.ds(8, tc), :] = s_ref[...]
        sx_ref[pl.ds(0, 8), :] = jnp.where(i == nc - 1, 0.0, sprev_ref[...])
        row = lax.broadcasted_iota(jnp.int32, (8, SCAN_LANES), 0)
        for lc in range(NS // SCAN_LANES):
            re_cols = pl.ds(lc * SCAN_LANES, SCAN_LANES)
            im_cols = pl.ds(NS + lc * SCAN_LANES, SCAN_LANES)

            def step(b, acc, re_cols=re_cols, im_cols=im_cols):
                ar, ai = acc
                off = pl.multiple_of(b * 8, 8)
                gr = g_ref[pl.ds(off, 8), re_cols]
                gi = g_ref[pl.ds(off, 8), im_cols]
                before = pl.ds(off, 8)
                here = pl.ds(off + 8, 8)
                sr = jnp.where(row == 0, sx_ref[before, re_cols][7:8], pltpu.roll(sx_ref[here, re_cols], 1, 0))
                si = jnp.where(row == 0, sx_ref[before, im_cols][7:8], pltpu.roll(sx_ref[here, im_cols], 1, 0))
                return ar + gr * sr + gi * si, ai + gi * sr - gr * si

            zero = jnp.zeros((8, SCAN_LANES), F32)
            ar, ai = lax.fori_loop(0, blocks, step, (zero, zero))
            dab_ref[0, :, re_cols] += ar
            dab_ref[1, :, re_cols] += ai

    rev = lambda i: (nc - 1 - i, 0)
    sprev = pl.BlockSpec((8, 2 * NS), lambda i: (jnp.maximum((nc - 1 - i) * blocks - 1, 0), 0))
    return pl.pallas_call(
        body, name="s5_bwd", grid=(nc,),
        in_specs=[pl.BlockSpec((tc, SSM_WIDTH), rev), pl.BlockSpec((tc, SSM_WIDTH), rev),
                  pl.BlockSpec((tc, 2 * NS), rev), sprev, _whole(), _whole(), _whole(), _whole()],
        out_specs=[pl.BlockSpec((tc, SSM_WIDTH), rev), _whole(), _whole(), _whole(), _whole()],
        out_shape=[jax.ShapeDtypeStruct((L, SSM_WIDTH), F32), jax.ShapeDtypeStruct((1, SSM_WIDTH), F32),
                   jax.ShapeDtypeStruct((SSM_WIDTH, 2 * NS), F32), jax.ShapeDtypeStruct((SSM_WIDTH, 2 * NS), F32),
                   jax.ShapeDtypeStruct((2, 8, NS), F32)],
        scratch_shapes=[pltpu.VMEM((tc, 2 * NS), F32), pltpu.VMEM((tc + 8, 2 * NS), F32), pltpu.VMEM((8, 2 * NS), F32)],
        compiler_params=_params(),
    )(dy, u, s, s, bd, cdt, d_skip, pwr)


def _branches(o_attn, y, gates, w_ab, w_glu, w_sb):
    ya = _dot(o_attn.astype(BF16), w_ab[...])
    gel = _gelu(y)
    glu = _dot(gel.astype(BF16), w_glu[...])
    p = glu[:, :SSM_WIDTH]
    sg = _sigmoid(glu[:, SSM_WIDTH:])
    ys2 = p * sg
    ysb = _dot(ys2.astype(BF16), w_sb[...])
    ga = gates[:, :D_MODEL]
    gs = gates[:, D_MODEL:]
    return ya, gel, p, sg, ys2, ysb, ga, gs


def _mix_out_fwd(x1, o_g, lse_g, y, gates, w_ab, w_glu, w_sb, w_out):
    L = x1.shape[0]
    tm = min(ROW_TILE, L)

    def body(x_ref, o0, o1, o2, l0, l1, l2, y_ref, gate_ref, wab_ref, wglu_ref, wsb_ref, wout_ref,
             x2_ref, oat_ref, lse0, lse1, lse2, scr):
        la, lb, lc = (_from_residues(ref, scr, d) for ref, d in zip((l0, l1, l2), DILATIONS))
        m = jnp.maximum(jnp.maximum(la, lb), lc)
        ea, eb, ec = jnp.exp(la - m), jnp.exp(lb - m), jnp.exp(lc - m)
        tot = ea + eb + ec
        oa, ob, oc = (_from_residues(ref, scr, d) for ref, d in zip((o0, o1, o2), DILATIONS))
        o_attn = (ea * oa + eb * ob + ec * oc) / tot
        oat_ref[...] = o_attn
        lse = m + jnp.log(tot)
        for ref, d in zip((lse0, lse1, lse2), DILATIONS):
            _to_residues(lse, ref, scr, d)
        ya, _, _, _, _, ysb, ga, gs = _branches(o_attn, y_ref[...], gate_ref[...], wab_ref, wglu_ref, wsb_ref)
        mix = ga * ya + gs * ysb
        x2_ref[...] = x_ref[...] + _dot(mix.astype(BF16), wout_ref[...])

    res = [_residue_spec(d, tm) for d in DILATIONS]
    return pl.pallas_call(
        body, name="mix_out_fwd", grid=(L // tm,),
        in_specs=[_rows(tm, D_MODEL)] + res * 2 + [_rows(tm, SSM_WIDTH), _rows(tm, 2 * D_MODEL)] + [_whole()] * 4,
        out_specs=[_rows(tm, D_MODEL), _rows(tm, GROUP_WIDTH)] + res,
        out_shape=[jax.ShapeDtypeStruct((L, D_MODEL), F32), jax.ShapeDtypeStruct((L, GROUP_WIDTH), F32)]
        + [_residue_shape(d, L, F32) for d in DILATIONS],
        scratch_shapes=[_residue_scratch(tm)],
        compiler_params=_params(),
    )(x1, *o_g, *lse_g, y, gates, w_ab, w_glu, w_sb, w_out)


def _mix_out_bwd(dx2, o_attn, y, gates, w_ab, w_glu, w_sb, w_out, head_sum):
    L = dx2.shape[0]
    tm = min(ROW_TILE, L)

    def body(dx_ref, oat_ref, y_ref, gate_ref, wab_ref, wglu_ref, wsb_ref, wout_ref, hs_ref,
             do0, do1, do2, dl0, dl1, dl2, dy_ref, dgp_ref, mix_ref, dya_ref, dys_ref, ys2_ref, gel_ref, dglu_ref,
             dgb_ref, scr):
        i = pl.program_id(0)
        o_attn = oat_ref[...]
        yv = y_ref[...]
        ya, gel, p, sg, ys2, ysb, ga, gs = _branches(o_attn, yv, gate_ref[...], wab_ref, wglu_ref, wsb_ref)
        mix_ref[...] = (ga * ya + gs * ysb).astype(BF16)
        ys2_ref[...] = ys2.astype(BF16)
        gel_ref[...] = gel.astype(BF16)
        dmix = _dot_nt(dx_ref[...].astype(BF16), wout_ref[...])
        dgp = jnp.concatenate([dmix * ya * ga * (1.0 - ga), dmix * ysb * gs * (1.0 - gs)], axis=1)
        dgp_ref[...] = dgp.astype(BF16)

        @pl.when(i == 0)
        def _():
            dgb_ref[...] = jnp.zeros_like(dgb_ref)

        dgb_ref[...] += jnp.sum(dgp, axis=0, keepdims=True)
        dya = (dmix * ga).astype(BF16)
        dys = (dmix * gs).astype(BF16)
        dya_ref[...] = dya
        dys_ref[...] = dys
        d_o = _dot_nt(dya, wab_ref[...])
        delta = _dot_exact(d_o * o_attn, hs_ref[...])
        for do_ref, dl_ref, d in zip((do0, do1, do2), (dl0, dl1, dl2), DILATIONS):
            _to_residues(d_o, do_ref, scr, d)
            _to_residues(delta, dl_ref, scr, d)
        dys2 = _dot_nt(dys, wsb_ref[...])
        dglu = jnp.concatenate([dys2 * sg, dys2 * p * sg * (1.0 - sg)], axis=1).astype(BF16)
        dglu_ref[...] = dglu
        dy_ref[...] = _dot_nt(dglu, wglu_ref[...]) * _gelu_grad(yv)

    grp = _rows(tm, GROUP_WIDTH)
    wide = _rows(tm, D_MODEL)
    half = _rows(tm, SSM_WIDTH)
    res = [_residue_spec(d, tm) for d in DILATIONS]
    sds = jax.ShapeDtypeStruct
    return pl.pallas_call(
        body, name="mix_out_bwd", grid=(L // tm,),
        in_specs=[wide, grp, half, _rows(tm, 2 * D_MODEL)] + [_whole()] * 5,
        out_specs=res + res + [half, _rows(tm, 2 * D_MODEL), wide, wide, wide, half, half, wide, _acc_row(2 * D_MODEL)],
        out_shape=[_residue_shape(d, L, BF16) for d in DILATIONS] + [_residue_shape(d, L, F32) for d in DILATIONS]
        + [sds((L, SSM_WIDTH), F32),
           sds((L, 2 * D_MODEL), BF16), sds((L, D_MODEL), BF16), sds((L, D_MODEL), BF16),
           sds((L, D_MODEL), BF16), sds((L, SSM_WIDTH), BF16), sds((L, SSM_WIDTH), BF16),
           sds((L, D_MODEL), BF16), sds((1, 2 * D_MODEL), F32)],
        scratch_shapes=[_residue_scratch(tm)],
        compiler_params=_params(),
    )(dx2, o_attn, y, gates, w_ab, w_glu, w_sb, w_out, head_sum)


def _adamw(w, g, m, v, name):
    R, C = w.shape
    tr = _row_tile(R, max(8, ADAMW_BLOCK_BYTES // (4 * C)))

    def body(w_ref, g_ref, m_ref, v_ref, d_ref, mo_ref, vo_ref):
        gv = g_ref[...]
        mn = ADAM_B1 * m_ref[...] + (1.0 - ADAM_B1) * gv
        vn = ADAM_B2 * v_ref[...] + (1.0 - ADAM_B2) * (gv * gv)
        m_hat = mn / (1.0 - ADAM_B1 ** ADAM_STEP)
        v_hat = vn / (1.0 - ADAM_B2 ** ADAM_STEP)
        d_ref[...] = -ADAM_LR * (m_hat / (jnp.sqrt(v_hat) + ADAM_EPS) + ADAM_WD * w_ref[...])
        mo_ref[...] = mn
        vo_ref[...] = vn

    blk = pl.BlockSpec((tr, C), lambda i: (i, 0))
    return pl.pallas_call(
        body, name=name, grid=(R // tr,),
        in_specs=[blk] * 4, out_specs=[blk] * 3,
        out_shape=[jax.ShapeDtypeStruct((R, C), F32)] * 3,
        compiler_params=_params(),
    )(w, g, m, v)


def _sum_chips_into_half(u, t, name):
    S, H, C = u.shape
    tr = _row_tile(H, 512)
    hb = H // tr

    def body(s_ref, t_ref, a_ref, b_ref, c_ref, o_ref):
        me = s_ref[1]
        others = (a_ref[...], b_ref[...], c_ref[...])
        acc = None
        for chip in range(S):
            below = others[min(chip, S - 2)]
            above = others[max(chip - 1, 0)]
            term = jnp.where(me == chip, t_ref[...], jnp.where(me > chip, below, above)).astype(F32)
            acc = term if acc is None else acc + term
        o_ref[...] = acc

    x, y, c = lax.axis_index("x"), lax.axis_index("y"), lax.axis_index("c")
    me = 2 * x + y
    scalars = jnp.stack([c, me] + [j + (j >= me).astype(jnp.int32) for j in range(S - 1)]).astype(jnp.int32)
    blk = (None, tr, C)
    return pl.pallas_call(
        body, name=name,
        grid_spec=pltpu.PrefetchScalarGridSpec(
            num_scalar_prefetch=1, grid=(hb,),
            in_specs=[pl.BlockSpec(blk, lambda i, s: (s[1], i, 0))]
            + [pl.BlockSpec(blk, functools.partial(lambda j, i, s: (s[2 + j], i, 0), j)) for j in range(S - 1)],
            out_specs=pl.BlockSpec((tr, C), lambda i, s: (s[0] * hb + i, 0))),
        out_shape=jax.ShapeDtypeStruct((2 * H, C), F32),
        compiler_params=_params(),
    )(scalars, t, u, u, u)


def _add_halves(g, r1, name):
    S, R, C = g.shape
    H = R // 2
    tr = _row_tile(H, 512)
    hb = H // tr

    def body(c_ref, g_ref, r_ref, o_ref):
        o_ref[...] = (g_ref[...] + r_ref[...]).astype(BF16)

    core = lax.axis_index("c").astype(jnp.int32).reshape(1)
    return pl.pallas_call(
        body, name=name,
        grid_spec=pltpu.PrefetchScalarGridSpec(
            num_scalar_prefetch=1, grid=(S, hb),
            in_specs=[pl.BlockSpec((None, tr, C), lambda j, i, c_ref: (j, c_ref[0] * hb + i, 0)),
                      pl.BlockSpec((None, tr, C), lambda j, i, c_ref: (j, i, 0))],
            out_specs=pl.BlockSpec((None, tr, C), lambda j, i, c_ref: (j, i, 0))),
        out_shape=jax.ShapeDtypeStruct((S, H, C), BF16),
        compiler_params=_params(),
    )(core, g, r1)


_ANY = pl.BlockSpec(memory_space=pl.ANY)


def _place():
    x, y, c = lax.axis_index("x"), lax.axis_index("y"), lax.axis_index("c")
    chips = [(1 - x, y), (x, 1 - y), (1 - x, 1 - y)]
    return x, y, c, chips


def _comm_call(body, name, ins, out_shapes, n_remote, n_local):
    return pl.pallas_call(
        body, name=name,
        in_specs=[_ANY] * len(ins), out_specs=[_ANY] * len(out_shapes), out_shape=out_shapes,
        scratch_shapes=[pltpu.SemaphoreType.DMA((n_remote,)), pltpu.SemaphoreType.DMA((n_remote,)),
                        pltpu.SemaphoreType.DMA((max(n_local, 1),))],
    )(*ins)


def _remote(src, dst, send_sems, recv_sems, k, device):
    return pltpu.make_async_remote_copy(src_ref=src, dst_ref=dst, send_sem=send_sems.at[k], recv_sem=recv_sems.at[k],
                                        device_id=device, device_id_type=MESH)


def _gather_parts(shapes, w_refs, out_refs, send_sems, recv_sems):
    n = len(shapes)
    x, y, c, chips = _place()
    me = 2 * x + y
    sibling = (x, y, 1 - c)

    def half(k, chip_idx, core):
        H = shapes[k][0] // 2
        return out_refs[k].at[chip_idx, pl.ds(core * H, H), :]

    mine = [_remote(w_refs[k], out_refs[k].at[me], send_sems, recv_sems, 6 * n + k, sibling) for k in range(n)]
    first = []
    for k in range(n):
        H = shapes[k][0] // 2
        for j, (cx, cy) in enumerate(chips):
            first.append(_remote(w_refs[k].at[pl.ds(c * H, H), :], half(k, me, c), send_sems, recv_sems,
                                 3 * k + j, (cx, cy, c)))

    def start():
        for cp in mine + first:
            cp.start()

    def finish():
        passed = []
        for k in range(n):
            for j, (cx, cy) in enumerate(chips):
                landed = half(k, 2 * cx + cy, c)
                _remote(landed, landed, send_sems, recv_sems, 3 * k + j, (cx, cy, c)).wait_recv()
                fwd = _remote(landed, landed, send_sems, recv_sems, 3 * n + 3 * k + j, sibling)
                fwd.start()
                passed.append(fwd)
        for k in range(n):
            for j, (cx, cy) in enumerate(chips):
                other = half(k, 2 * cx + cy, 1 - c)
                _remote(other, other, send_sems, recv_sems, 3 * n + 3 * k + j, sibling).wait_recv()
        for cp in mine:
            cp.wait_recv()
        for cp in first + passed + mine:
            cp.wait_send()

    return start, finish


def _gather_weights(shards, name):
    n = len(shards)

    def body(*refs):
        start, finish = _gather_parts([w.shape for w in shards], refs[:n], refs[n:2 * n], *refs[2 * n:2 * n + 2])
        start()
        finish()

    return _comm_call(body, name, shards,
                      [jax.ShapeDtypeStruct((N_SHARD,) + w.shape, w.dtype) for w in shards], 7 * n, 0)


def _handshake(peers):
    barrier = pltpu.get_barrier_semaphore()
    for peer in peers:
        pl.semaphore_signal(barrier, inc=1, device_id=peer, device_id_type=MESH)
    pl.semaphore_wait(barrier, len(peers))


def _sequenced(body, name, ins, out_shapes, n_sems, collective_id):
    return pl.kernel(
        body, out_type=list(out_shapes), mesh=plsc.ScalarSubcoreMesh(axis_name="sequencer", num_cores=1), name=name,
        scratch_types=(pltpu.SemaphoreType.DMA((n_sems,)), pltpu.SemaphoreType.DMA((n_sems,))),
        compiler_params=pltpu.CompilerParams(collective_id=collective_id))(*ins)


def _swap_halves(gs, name, collective_id):
    n = len(gs)

    def body(*refs):
        g_refs, out_refs = refs[:n], refs[n:2 * n]
        send_sems, recv_sems = refs[2 * n:]
        x, y, c, _ = _place()
        _handshake([(x, y, 1 - c)])
        cps = []
        for k in range(n):
            H = gs[k].shape[1] // 2
            cp = _remote(g_refs[k].at[:, pl.ds((1 - c) * H, H), :], out_refs[k], send_sems, recv_sems, k, (x, y, 1 - c))
            cp.start()
            cps.append(cp)
        for cp in cps:
            cp.wait()

    return _sequenced(body, name, gs, [jax.ShapeDtypeStruct((g.shape[0], g.shape[1] // 2, g.shape[2]), g.dtype)
                                       for g in gs], n, collective_id)


def _exchange_chips(ts, name, collective_id):
    n = len(ts)

    def body(*refs):
        t_refs, out_refs = refs[:n], refs[n:2 * n]
        send_sems, recv_sems = refs[2 * n:]
        x, y, c, chips = _place()
        me = 2 * x + y
        _handshake([(cx, cy, c) for cx, cy in chips])
        sent = []
        for k in range(n):
            for j, (cx, cy) in enumerate(chips):
                cp = _remote(t_refs[k].at[2 * cx + cy], out_refs[k].at[me], send_sems, recv_sems, 3 * k + j, (cx, cy, c))
                cp.start()
                sent.append(cp)
        for k in range(n):
            for j, (cx, cy) in enumerate(chips):
                slot = out_refs[k].at[2 * cx + cy]
                _remote(slot, slot, send_sems, recv_sems, 3 * k + j, (cx, cy, c)).wait_recv()
        for cp in sent:
            cp.wait_send()

    return _sequenced(body, name, ts, [jax.ShapeDtypeStruct(t.shape, t.dtype) for t in ts], 3 * n, collective_id)


def _join_halves(fs, name):
    n = len(fs)

    def body(*refs):
        out_refs = refs[n:2 * n]
        send_sems, recv_sems, _ = refs[2 * n:]
        x, y, c, _ = _place()
        sent = []
        for k in range(n):
            H = fs[k].shape[0] // 2
            here = out_refs[k].at[pl.ds(c * H, H), :]
            cp = _remote(here, here, send_sems, recv_sems, k, (x, y, 1 - c))
            cp.start()
            sent.append(cp)
        for k in range(n):
            H = fs[k].shape[0] // 2
            other = out_refs[k].at[pl.ds((1 - c) * H, H), :]
            _remote(other, other, send_sems, recv_sems, k, (x, y, 1 - c)).wait_recv()
        for cp in sent:
            cp.wait_send()

    return pl.pallas_call(
        body, name=name,
        in_specs=[_ANY] * n, out_specs=[_ANY] * n,
        out_shape=[jax.ShapeDtypeStruct(f.shape, f.dtype) for f in fs],
        input_output_aliases={k: k for k in range(n)},
        scratch_shapes=[pltpu.SemaphoreType.DMA((n,)), pltpu.SemaphoreType.DMA((n,)), pltpu.SemaphoreType.DMA((1,))],
    )(*fs)


def _gather_small(v):
    R, C = v.shape

    def body(v_ref, out_ref, send_sems, recv_sems):
        x, y, c, _ = _place()
        me = 4 * x + 2 * y + c
        flips = [(fx, fy, fc) for fx in (0, 1) for fy in (0, 1) for fc in (0, 1)][1:]
        peers = [((1 - x) if fx else x, (1 - y) if fy else y, (1 - c) if fc else c) for fx, fy, fc in flips]
        _handshake(peers)
        sent = []
        for j, peer in enumerate(peers):
            cp = _remote(v_ref, out_ref.at[me], send_sems, recv_sems, j, peer)
            cp.start()
            sent.append(cp)
        for j, peer in enumerate(peers):
            slot = out_ref.at[4 * peer[0] + 2 * peer[1] + peer[2]]
            _remote(slot, slot, send_sems, recv_sems, j, peer).wait_recv()
        for cp in sent:
            cp.wait_send()

    return _sequenced(body, "gather_small", [v], [jax.ShapeDtypeStruct((8, R, C), F32)], 7,
                      COLLECTIVE_IDS["gather_small"])[0]


def _sum_devices(x, own, name):
    S, R, C = x.shape
    tr = _row_tile(R, 2048)

    def body(s_ref, x_ref, own_ref, o_ref):
        me = s_ref[0]
        acc = None
        for k in range(S):
            term = jnp.where(me == k, own_ref[...], x_ref[k])
            acc = term if acc is None else acc + term
        o_ref[...] = acc

    x_, y_, c_ = lax.axis_index("x"), lax.axis_index("y"), lax.axis_index("c")
    me = (4 * x_ + 2 * y_ + c_).astype(jnp.int32).reshape(1)
    return pl.pallas_call(
        body, name=name,
        grid_spec=pltpu.PrefetchScalarGridSpec(
            num_scalar_prefetch=1, grid=(R // tr,),
            in_specs=[pl.BlockSpec((S, tr, C), lambda i, s: (0, i, 0)), pl.BlockSpec((tr, C), lambda i, s: (i, 0))],
            out_specs=pl.BlockSpec((tr, C), lambda i, s: (i, 0))),
        out_shape=jax.ShapeDtypeStruct((R, C), F32),
        compiler_params=_params(),
    )(me, x, own)


def _after(earlier, arrays):
    return lax.optimization_barrier((earlier, arrays))


def _reduce_exchange(gs, names, tag, earlier):
    earlier, gs = _after(earlier, gs)
    r1 = _swap_halves(gs, "reduce_swap_" + tag, COLLECTIVE_IDS["swap_" + tag])
    ts = [_add_halves(g, r, "reduce_add_cores_" + nm) for g, r, nm in zip(gs, r1, names)]
    us = _exchange_chips(ts, "reduce_exchange_" + tag, COLLECTIVE_IDS["exchange_" + tag])
    return us, ts, earlier


def _reduce_finish(us, ts, names, tag):
    fs = [_sum_chips_into_half(u, t, "reduce_add_chips_" + nm) for u, t, nm in zip(us, ts, names)]
    return _join_halves(fs, "reduce_join_" + tag)


BIG = ["ffn1_w_gate", "ffn1_w_up", "ffn1_w_down", "w_in", "ssm_w_glu", "w_attn_branch", "w_ssm_branch",
       "w_out", "ffn2_w_gate", "ffn2_w_up", "ffn2_w_down"]
SMALL = ["ffn1_norm", "mix_norm", "gate_bias", "rel_bias_table", "ssm_a_re", "ssm_a_im", "ssm_log_dt",
         "ssm_b_re", "ssm_b_im", "ssm_c_re", "ssm_c_im", "ssm_d", "ffn2_norm", "final_norm"]
ORDER = ["ffn1_norm", "ffn1_w_gate", "ffn1_w_up", "ffn1_w_down", "mix_norm", "w_in", "gate_bias", "rel_bias_table",
         "ssm_a_re", "ssm_a_im", "ssm_log_dt", "ssm_b_re", "ssm_b_im", "ssm_c_re", "ssm_c_im", "ssm_d",
         "ssm_w_glu", "w_attn_branch", "w_ssm_branch", "w_out", "ffn2_norm", "ffn2_w_gate", "ffn2_w_up",
         "ffn2_w_down", "final_norm"]


_SMALL_TILE = 8 * LANES


def _pack_small(arrays):
    rows = []
    for a in arrays:
        flat = a.reshape(-1).astype(F32)
        rows.append(jnp.pad(flat, (0, (-flat.shape[0]) % _SMALL_TILE)).reshape(-1, LANES))
    return jnp.concatenate(rows, axis=0)


def _unpack_small(packed, shapes):
    out, r0 = [], 0
    for shp in shapes:
        n = math.prod(shp)
        rows = 8 * -(-n // _SMALL_TILE)
        out.append(packed[r0:r0 + rows].reshape(-1)[:n].reshape(shp))
        r0 += rows
    return out


def _split_cols(g):
    K, N = g.shape
    return g.reshape(K, N_SHARD, N // N_SHARD).transpose(1, 0, 2)


def _join_cols(w):
    S, K, n = w.shape
    return w.transpose(1, 0, 2).reshape(K, S * n)


COL_SHARDED = ("ssm_w_glu", "w_attn_branch", "w_ssm_branch")
TRANSPOSED = ("ffn1_w_gate", "ffn1_w_up", "ffn2_w_gate", "ffn2_w_up", "w_in")


def _shard_2d(name, arr):
    two_d = arr.reshape(arr.shape[-2:])
    return two_d.T if name in TRANSPOSED else two_d


def _shard_nd(name, two_d, shape):
    return (two_d.T if name in TRANSPOSED else two_d).reshape(shape)


class _GradSync:
    def __init__(self, weights, moms, vels):
        self.weights, self.moms, self.vels = weights, moms, vels
        self.grads, self.delta, self.new_m, self.new_v = {}, {}, {}, {}
        self.loss = None
        self._earlier = []
        self._exchanged = {}

    def grads_ready(self, tag, gw):
        names = REDUCE_GROUPS[tag]
        gs = []
        for n in names:
            g = gw[n]
            if n in COL_SHARDED:
                g = _split_cols(g)
            elif n in ("w_out", "w_in"):
                g = g.reshape(N_SHARD, g.shape[0] // N_SHARD, g.shape[1])
            gs.append(g)
        us, ts, _ = _reduce_exchange(gs, names, tag, self._earlier)
        self._exchanged[tag] = (us, ts)
        self._earlier = us

    def small_ready(self, gs, loss_blk):
        _, (mine,) = _after(self._earlier, [_pack_small([gs[n] for n in SMALL] + [loss_blk[0:1, :]])])
        others = _gather_small(mine)
        self._exchanged["small"] = (others, mine)
        self._earlier = [others]

    def finish(self, tag):
        made = []
        if tag == "small":
            others, mine = self._exchanged[tag]
            shapes = [self.weights[n].shape for n in SMALL]
            total = _unpack_small(_sum_devices(others, mine, "sum_small"), shapes + [(128,)])
            self.loss = total[-1][0]
            self.grads.update(zip(SMALL, total[:-1]))
            packed = [_pack_small([src[n] for n in SMALL]) for src in (self.weights, self.grads, self.moms, self.vels)]
            for dst, res in zip((self.delta, self.new_m, self.new_v), _adamw(*packed, "adamw_small")):
                dst.update(zip(SMALL, _unpack_small(res, shapes)))
            for n in SMALL:
                made += [self.grads[n], self.delta[n], self.new_m[n], self.new_v[n]]
            return made + [self.loss]
        names = REDUCE_GROUPS[tag]
        us, ts = self._exchanged[tag]
        for n, g in zip(names, _reduce_finish(us, ts, names, tag)):
            shp = self.weights[n].shape
            d, m, v = _adamw(_shard_2d(n, self.weights[n]), g, _shard_2d(n, self.moms[n]), _shard_2d(n, self.vels[n]),
                             "adamw_" + n)
            self.grads[n], self.delta[n] = _shard_nd(n, g, shp), _shard_nd(n, d, shp)
            self.new_m[n], self.new_v[n] = _shard_nd(n, m, shp), _shard_nd(n, v, shp)
            made += [self.grads[n], self.delta[n], self.new_m[n], self.new_v[n]]
        return made

    def finish_all(self):
        for tag in ("ffn2", "mixer", "w_in", "small", "ffn1"):
            self.finish(tag)


def _local_step(x, target, w, later, small, sync):
    L = x.shape[0]
    row = lambda v: v.reshape(1, -1)

    a_re, a_im = small["ssm_a_re"].reshape(1, NS), small["ssm_a_im"].reshape(1, NS)
    ldt = jnp.repeat(small["ssm_log_dt"].reshape(SSM_GROUPS), SSM_STATE).reshape(1, NS)
    to_cn = lambda b: b.reshape(SSM_GROUPS, SSM_STATE, SSM_GROUP).transpose(2, 0, 1).reshape(SSM_GROUP, NS)
    c_to_cn = lambda c: c.reshape(SSM_GROUPS, SSM_GROUP, SSM_STATE).transpose(1, 0, 2).reshape(SSM_GROUP, NS)
    b_re, b_im = to_cn(small["ssm_b_re"]), to_cn(small["ssm_b_im"])
    c_re, c_im = c_to_cn(small["ssm_c_re"]), c_to_cn(small["ssm_c_im"])
    d_skip = row(small["ssm_d"])
    pw, pwr, bd, cdt = _disc_fwd(a_re, a_im, ldt, b_re, b_im, c_re, c_im)

    onehot = _bucket_onehot()
    table_t = small["rel_bias_table"].T.reshape(3, HEADS_PER_GROUP, N_BUCKETS)
    table_t = jnp.pad(table_t, ((0, 0), (0, 8 - HEADS_PER_GROUP), (0, 0)))
    bias = _bias_expand(table_t, onehot)[:, :, :HEADS_PER_GROUP].reshape(
        3, 2, HEADS_PER_GROUP, ATTN_BLOCK, 2 * ATTN_BLOCK)

    n1, nm, n2, nf = row(small["ffn1_norm"]), row(small["mix_norm"]), row(small["ffn2_norm"]), row(small["final_norm"])
    gate_bias = row(small["gate_bias"])

    x1, a1, b1, *later_full = _ffn_fwd(x, n1, w["ffn1_w_gate"], w["ffn1_w_up"], w["ffn1_w_down"], "ffn1_fwd",
                                       carried=list(later.values()))
    w = dict(w, **dict(zip(later, later_full)))
    for n in COL_SHARDED:
        w[n] = _join_cols(w[n])
    w["w_out"] = w["w_out"].reshape(D_MODEL, D_MODEL)
    w["w_in"] = w["w_in"].reshape(IN_WIDTH, D_MODEL)
    *qkv, u, gates = _mix_in_fwd(x1, nm, w["w_in"], gate_bias)
    q, k, v = qkv[0:3], qkv[3:6], qkv[6:9]
    o_g, lse_g = [], []
    for grp in range(3):
        o, lse = _attn_fwd(q[grp], k[grp], v[grp], bias[grp], f"attn_fwd_{grp}")
        o_g.append(o)
        lse_g.append(lse)
    y, s = _ssm_fwd(u, bd, cdt, d_skip, pw)
    x2, o_attn, *lse_tot = _mix_out_fwd(x1, o_g, lse_g, y, gates, w["w_attn_branch"], w["ssm_w_glu"],
                                        w["w_ssm_branch"], w["w_out"])
    x3, a2, b2 = _ffn_fwd(x2, n2, w["ffn2_w_gate"], w["ffn2_w_up"], w["ffn2_w_down"], "ffn2_fwd")
    loss_blk, dx3, d_nf = _loss_fwd_bwd(x3, nf, target)

    gw, gs = {}, {}
    gs["final_norm"] = d_nf

    dx2, da, db, sact, h, d_out, gs["ffn2_norm"] = _ffn_bwd(dx3, x2, n2, a2, b2, w["ffn2_w_gate"], w["ffn2_w_up"],
                                                            w["ffn2_w_down"], "ffn2_bwd")
    gw["ffn2_w_gate"] = _matmul_tn(da, h[None], "ffn2_dw_gate")
    gw["ffn2_w_up"] = _matmul_tn(db, h[None], "ffn2_dw_up")
    gw["ffn2_w_down"] = _matmul_tn(sact, d_out[None], "ffn2_dw_down")
    sync.grads_ready("ffn2", gw)

    head_sum = (jnp.arange(GROUP_WIDTH)[:, None] // HEAD_DIM == jnp.arange(GROUP_WIDTH)[None, :] // HEAD_DIM).astype(F32)
    (*d_o_delta, dy, dgp, mix, dya, dys, ys2, gel, dglu, gs["gate_bias"]) = _mix_out_bwd(
        dx2, o_attn, y, gates, w["w_attn_branch"], w["ssm_w_glu"], w["w_ssm_branch"], w["w_out"], head_sum)
    d_o, delta = d_o_delta[0:3], d_o_delta[3:6]
    gw["w_out"] = _matmul_tn(mix[None], dx2[None], "dw_out")[0]
    gw["w_attn_branch"] = _matmul_tn(o_attn[None], dya[None], "dw_attn_branch")[0]
    gw["w_ssm_branch"] = _matmul_tn(ys2[None], dys[None], "dw_ssm_branch")[0]
    gw["ssm_w_glu"] = _matmul_tn(gel[None], dglu[None], "dw_glu")[0]
    sync.grads_ready("mixer", gw)

    dqs, dks, dvs, dsums = [], [], [], []
    for grp in range(3):
        dq, dk, dv, dsum = _attn_bwd(q[grp], k[grp], v[grp], d_o[grp], lse_tot[grp], delta[grp], bias[grp],
                                     f"attn_bwd_{grp}")
        dqs.append(dq)
        dks.append(dk)
        dvs.append(dv)
        dsums.append(dsum.reshape(HEADS_PER_GROUP, -1))
    dsum_all = jnp.pad(jnp.stack(dsums), ((0, 0), (0, 8 - HEADS_PER_GROUP), (0, 0)))
    d_table = _bias_reduce(dsum_all, onehot)[:, :HEADS_PER_GROUP]
    gs["rel_bias_table"] = d_table.reshape(3 * HEADS_PER_GROUP, N_BUCKETS).T

    du, gs["ssm_d"], d_bd, d_cdt, d_ab = _ssm_bwd(dy, u, s, bd, cdt, d_skip, pwr)
    group_sum = (jnp.arange(NS)[:, None] // SSM_STATE == jnp.arange(128)[None, :]).astype(F32)
    d_are, d_aim, d_ldt, d_bre, d_bim, d_cre, d_cim = _disc_bwd(a_re, a_im, ldt, b_re, b_im, d_bd, d_cdt, d_ab, group_sum)
    gs["ssm_a_re"], gs["ssm_a_im"] = d_are, d_aim
    gs["ssm_log_dt"] = d_ldt[0, :SSM_GROUPS]
    from_cn = lambda t: t.reshape(SSM_GROUP, SSM_GROUPS, SSM_STATE).transpose(1, 2, 0)
    c_from_cn = lambda t: t.reshape(SSM_GROUP, SSM_GROUPS, SSM_STATE).transpose(1, 0, 2)
    gs["ssm_b_re"], gs["ssm_b_im"] = from_cn(d_bre), from_cn(d_bim)
    gs["ssm_c_re"], gs["ssm_c_im"] = c_from_cn(d_cre), c_from_cn(d_cim)

    dx1, hm, dz, gs["mix_norm"] = _mix_in_bwd(dx2, x1, nm, dqs + dks + dvs, du, dgp, w["w_in"])
    gw["w_in"] = _matmul_tn(dz[None], hm[None], "dw_in")[0]
    sync.grads_ready("w_in", gw)

    dx0, da, db, sact, h, d_out, gs["ffn1_norm"] = _ffn_bwd(dx1, x, n1, a1, b1, w["ffn1_w_gate"], w["ffn1_w_up"],
                                                            w["ffn1_w_down"], "ffn1_bwd")
    sync.small_ready(gs, loss_blk)
    gw["ffn1_w_gate"] = _matmul_tn(da, h[None], "ffn1_dw_gate")
    gw["ffn1_w_up"] = _matmul_tn(db, h[None], "ffn1_dw_up")
    gw["ffn1_w_down"] = _matmul_tn(sact, d_out[None], "ffn1_dw_down")
    sync.grads_ready("ffn1", gw)
    return dx0


def kernel(x, ffn1_norm, ffn1_w_gate, ffn1_w_up, ffn1_w_down, mix_norm, w_in, gate_bias, rel_bias_table, ssm_a_re, ssm_a_im, ssm_log_dt, ssm_b_re, ssm_b_im, ssm_c_re, ssm_c_im, ssm_d, ssm_w_glu, w_attn_branch, w_ssm_branch, w_out, ffn2_norm, ffn2_w_gate, ffn2_w_up, ffn2_w_down, final_norm, loss_target, m_ffn1_norm, m_ffn1_w_gate, m_ffn1_w_up, m_ffn1_w_down, m_mix_norm, m_w_in, m_gate_bias, m_rel_bias_table, m_ssm_a_re, m_ssm_a_im, m_ssm_log_dt, m_ssm_b_re, m_ssm_b_im, m_ssm_c_re, m_ssm_c_im, m_ssm_d, m_ssm_w_glu, m_w_attn_branch, m_w_ssm_branch, m_w_out, m_ffn2_norm, m_ffn2_w_gate, m_ffn2_w_up, m_ffn2_w_down, m_final_norm, v_ffn1_norm, v_ffn1_w_gate, v_ffn1_w_up, v_ffn1_w_down, v_mix_norm, v_w_in, v_gate_bias, v_rel_bias_table, v_ssm_a_re, v_ssm_a_im, v_ssm_log_dt, v_ssm_b_re, v_ssm_b_im, v_ssm_c_re, v_ssm_c_im, v_ssm_d, v_ssm_w_glu, v_w_attn_branch, v_w_ssm_branch, v_w_out, v_ffn2_norm, v_ffn2_w_gate, v_ffn2_w_up, v_ffn2_w_down, v_final_norm):
    args = dict(locals())
    weights = {n: args[n] for n in ORDER}
    moms = {n: args["m_" + n] for n in ORDER}
    vels = {n: args["v_" + n] for n in ORDER}

    shard2d = {n: _shard_2d(n, weights[n]) for n in BIG}
    first, rest = BIG[:3], BIG[3:]
    full = dict(zip(first, _gather_weights([shard2d[n].astype(BF16) for n in first], "gather_ffn1_weights")))
    later = {n: shard2d[n].astype(BF16) for n in rest}

    small = {n: weights[n] for n in SMALL}
    sync = _GradSync(weights, moms, vels)
    grad_x = _local_step(x[0], loss_target[0], full, later, small, sync)
    sync.finish_all()
    return (sync.loss, grad_x[None], *[sync.grads[n] for n in ORDER], *[sync.delta[n] for n in ORDER],
            *[sync.new_m[n] for n in ORDER], *[sync.new_v[n] for n in ORDER])
```

```python
import functools
import math

import jax
import jax.numpy as jnp
from jax import lax
from jax.experimental import pallas as pl
from jax.experimental.pallas import tpu as pltpu
from jax.experimental.pallas import tpu_sc as plsc

F32 = jnp.float32
BF16 = jnp.bfloat16
MESH = pl.DeviceIdType.MESH

D_MODEL = 1024
D_FF = 2816
HEAD_DIM = 64
HEADS_PER_GROUP = 4
DILATIONS = (1, 4, 16)
WINDOW_STEPS = 128
ATTN_BLOCK = 128
ATTN_QB = 4
GROUP_WIDTH = HEADS_PER_GROUP * HEAD_DIM
ATTN_WIDTH = 3 * GROUP_WIDTH
N_BUCKETS = 32
MAX_DISTANCE = 2048
NEG_INF = -1e30
SSM_WIDTH = 512
SSM_GROUP = 16
SSM_GROUPS = 32
SSM_STATE = 64
NS = SSM_GROUPS * SSM_STATE
EPS = 1e-6
IN_WIDTH = 3 * ATTN_WIDTH + SSM_WIDTH + 2 * D_MODEL
Q_SCALE = HEAD_DIM ** -0.5
N_SHARD = 4
FF_SHARD = D_FF // N_SHARD
ADAM_LR, ADAM_B1, ADAM_B2, ADAM_EPS, ADAM_WD, ADAM_STEP = 0.001, 0.9, 0.999, 1e-08, 0.01, 10

LANES = 128
VMEM_LIMIT = 56 * 1024 * 1024
ROW_TILE = 512
FFN_BWD_TILE = 256
SSM_CHUNK = 256
SCAN_LANES = 512
ADAMW_BLOCK_BYTES = 1 << 20
TN_VMEM_BUDGET = 40 * 1024 * 1024
REDUCE_GROUPS = {
    "ffn2": ["ffn2_w_gate", "ffn2_w_up", "ffn2_w_down"],
    "mixer": ["w_out", "w_attn_branch", "w_ssm_branch", "ssm_w_glu"],
    "w_in": ["w_in"],
    "ffn1": ["ffn1_w_gate", "ffn1_w_up", "ffn1_w_down"],
}
COLLECTIVE_IDS = {name: i for i, name in enumerate(
    ["gather_small"] + [stage + "_" + tag for tag in REDUCE_GROUPS for stage in ("swap", "exchange")])}


def _params(**kw):
    return pltpu.CompilerParams(vmem_limit_bytes=VMEM_LIMIT, **kw)


def _dot(a, b):
    return jnp.dot(a, b, preferred_element_type=F32)


def _dot_nt(a, b):
    return lax.dot_general(a, b, (((1,), (1,)), ((), ())), preferred_element_type=F32)


def _dot_tn(a, b):
    return lax.dot_general(a, b, (((0,), (0,)), ((), ())), preferred_element_type=F32)


def _dot_exact(a, b):
    return jnp.dot(a, b, preferred_element_type=F32, precision=lax.Precision.HIGHEST)


def _dot_nt_exact(a, b):
    return lax.dot_general(a, b, (((1,), (1,)), ((), ())), preferred_element_type=F32,
                           precision=lax.Precision.HIGHEST)


def _rms(x):
    r = lax.rsqrt(jnp.mean(x * x, axis=-1, keepdims=True) + EPS)
    return r, x * r


def _rms_bwd(dh, g, r, xhat):
    dxh = dh * g
    return r * (dxh - xhat * jnp.mean(dxh * xhat, axis=-1, keepdims=True))


def _sigmoid(x):
    return 1.0 / (1.0 + jnp.exp(-x))


_GELU_C = math.sqrt(2.0 / math.pi)


def _gelu(x):
    return 0.5 * x * (1.0 + jnp.tanh(_GELU_C * (x + 0.044715 * x * x * x)))


def _gelu_grad(x):
    t = jnp.tanh(_GELU_C * (x + 0.044715 * x * x * x))
    return 0.5 * (1.0 + t) + 0.5 * x * (1.0 - t * t) * _GELU_C * (1.0 + 3 * 0.044715 * x * x)


def _whole():
    return pl.BlockSpec(memory_space=pltpu.VMEM)


def _row_tile(rows, cap):
    if rows <= cap:
        return rows
    return max(t for t in range(8, cap + 1, 8) if rows % t == 0)


def _rows(tm, w):
    return pl.BlockSpec((tm, w), lambda i: (i, 0))


def _acc_row(w):
    return pl.BlockSpec((1, w), lambda i: (0, 0))


def _ffn_fwd(x, g, wg, wu, wd, name, carried=()):
    L = x.shape[0]
    tm = min(ROW_TILE, L)
    n = len(carried)
    steps = L // tm

    def body(x_ref, g_ref, wg_ref, wu_ref, wd_ref, *refs):
        shard_refs, (xo_ref, a_ref, b_ref), full_refs, sems = refs[:n], refs[n:n + 3], refs[n + 3:2 * n + 3], refs[2 * n + 3:]
        if n:
            start, finish = _gather_parts([w.shape for w in carried], shard_refs, full_refs, *sems)
            pl.when(pl.program_id(0) == 0)(start)
        xv = x_ref[...]
        r, xhat = _rms(xv)
        h = (xhat * g_ref[...]).astype(BF16)
        acc = jnp.zeros((tm, D_MODEL), F32)
        for j in range(N_SHARD):
            a = _dot_nt(h, wg_ref[j])
            b = _dot_nt(h, wu_ref[j])
            a_ref[j] = a.astype(BF16)
            b_ref[j] = b.astype(BF16)
            s = (a * _sigmoid(a) * b).astype(BF16)
            acc = acc + _dot(s, wd_ref[j])
        xo_ref[...] = xv + 0.5 * acc
        if n:
            pl.when(pl.program_id(0) == steps - 1)(finish)

    act = pl.BlockSpec((N_SHARD, tm, FF_SHARD), lambda i: (0, i, 0))
    return pl.pallas_call(
        body, name=name, grid=(steps,),
        in_specs=[_rows(tm, D_MODEL), _whole(), _whole(), _whole(), _whole()] + [_ANY] * n,
        out_specs=[_rows(tm, D_MODEL), act, act] + [_ANY] * n,
        out_shape=[jax.ShapeDtypeStruct((L, D_MODEL), F32),
                   jax.ShapeDtypeStruct((N_SHARD, L, FF_SHARD), BF16),
                   jax.ShapeDtypeStruct((N_SHARD, L, FF_SHARD), BF16)]
        + [jax.ShapeDtypeStruct((N_SHARD,) + w.shape, w.dtype) for w in carried],
        scratch_shapes=[pltpu.SemaphoreType.DMA((7 * n,)), pltpu.SemaphoreType.DMA((7 * n,))] if n else [],
        compiler_params=_params(),
    )(x, g, wg, wu, wd, *carried)


def _ffn_bwd(dxo, x, g, a, b, wg, wu, wd, name):
    L = x.shape[0]
    tm = min(FFN_BWD_TILE, L)

    def body(dxo_ref, x_ref, g_ref, a_ref, b_ref, wg_ref, wu_ref, wd_ref,
             dxi_ref, da_ref, db_ref, s_ref, h_ref, do_ref, dg_ref):
        i = pl.program_id(0)
        xv = x_ref[...]
        gv = g_ref[...]
        r, xhat = _rms(xv)
        h_ref[...] = (xhat * gv).astype(BF16)
        dxo_v = dxo_ref[...]
        d_out = (0.5 * dxo_v).astype(BF16)
        do_ref[...] = d_out
        dh = jnp.zeros((tm, D_MODEL), F32)
        for j in range(N_SHARD):
            av = a_ref[j].astype(F32)
            bv = b_ref[j].astype(F32)
            sg = _sigmoid(av)
            sl = av * sg
            ds = _dot_nt(d_out, wd_ref[j])
            dbv = (ds * sl).astype(BF16)
            dav = (ds * bv * (sg * (1.0 + av * (1.0 - sg)))).astype(BF16)
            da_ref[j] = dav
            db_ref[j] = dbv
            s_ref[j] = (sl * bv).astype(BF16)
            dh = dh + _dot(dav, wg_ref[j]) + _dot(dbv, wu_ref[j])

        @pl.when(i == 0)
        def _():
            dg_ref[...] = jnp.zeros_like(dg_ref)

        dg_ref[...] += jnp.sum(dh * xhat, axis=0, keepdims=True)
        dxi_ref[...] = dxo_v + _rms_bwd(dh, gv, r, xhat)

    act = pl.BlockSpec((N_SHARD, tm, FF_SHARD), lambda i: (0, i, 0))
    act_shape = jax.ShapeDtypeStruct((N_SHARD, L, FF_SHARD), BF16)
    return pl.pallas_call(
        body, name=name, grid=(L // tm,),
        in_specs=[_rows(tm, D_MODEL), _rows(tm, D_MODEL), _whole(), act, act, _whole(), _whole(), _whole()],
        out_specs=[_rows(tm, D_MODEL), act, act, act, _rows(tm, D_MODEL), _rows(tm, D_MODEL), _acc_row(D_MODEL)],
        out_shape=[jax.ShapeDtypeStruct((L, D_MODEL), F32), act_shape, act_shape, act_shape,
                   jax.ShapeDtypeStruct((L, D_MODEL), BF16), jax.ShapeDtypeStruct((L, D_MODEL), BF16),
                   jax.ShapeDtypeStruct((1, D_MODEL), F32)],
        compiler_params=_params(),
    )(dxo, x, g, a, b, wg, wu, wd)


def _matmul_tn(a, b, name):
    ja, L, K = a.shape
    jb, _, N = b.shape
    J = max(ja, jb)
    splits = [s for s in (1, 2, 4, 8) if s == 1 or N % (s * LANES) == 0]
    nsplit = next((s for s in splits if 2 * K * (N // s) * 4 <= TN_VMEM_BUDGET // 2), splits[-1])
    nc = N // nsplit
    left = TN_VMEM_BUDGET - 2 * K * nc * 4
    row_bytes = 2 * (K * a.dtype.itemsize + nc * b.dtype.itemsize)
    tm = next((t for t in (2048, 1024, 512, 256) if L % t == 0 and t * row_bytes <= left), min(128, L))

    def body(a_ref, b_ref, o_ref):
        @pl.when(pl.program_id(2) == 0)
        def _():
            o_ref[...] = jnp.zeros_like(o_ref)

        o_ref[...] += _dot_tn(a_ref[...].astype(BF16), b_ref[...].astype(BF16))

    return pl.pallas_call(
        body, name=name, grid=(J, nsplit, L // tm),
        in_specs=[pl.BlockSpec((None, tm, K), (lambda j, s, i: (j, i, 0)) if ja > 1 else (lambda j, s, i: (0, i, 0))),
                  pl.BlockSpec((None, tm, nc), (lambda j, s, i: (j, i, s)) if jb > 1 else (lambda j, s, i: (0, i, s)))],
        out_specs=pl.BlockSpec((None, K, nc), lambda j, s, i: (j, 0, s)),
        out_shape=jax.ShapeDtypeStruct((J, K, N), F32),
        compiler_params=_params(),
    )(a, b)


def _loss_fwd_bwd(x, g, target):
    L = x.shape[0]
    tm = min(ROW_TILE, L)

    def body(x_ref, g_ref, t_ref, loss_ref, dx_ref, dg_ref):
        i = pl.program_id(0)
        xv = x_ref[...]
        gv = g_ref[...]
        r, xhat = _rms(xv)
        err = xhat * gv - t_ref[...]
        part = 0.5 * jnp.sum(jnp.sum(err * err, axis=1, keepdims=True) * (1.0 / D_MODEL), axis=0, keepdims=True)
        dy = err * (1.0 / D_MODEL)

        @pl.when(i == 0)
        def _():
            dg_ref[...] = jnp.zeros_like(dg_ref)
            loss_ref[...] = jnp.zeros_like(loss_ref)

        loss_ref[...] += jnp.broadcast_to(part, loss_ref.shape)
        dg_ref[...] += jnp.sum(dy * xhat, axis=0, keepdims=True)
        dx_ref[...] = _rms_bwd(dy, gv, r, xhat)

    return pl.pallas_call(
        body, name="loss_fwd_bwd", grid=(L // tm,),
        in_specs=[_rows(tm, D_MODEL), _whole(), _rows(tm, D_MODEL)],
        out_specs=[pl.BlockSpec((8, 128), lambda i: (0, 0)), _rows(tm, D_MODEL), _acc_row(D_MODEL)],
        out_shape=[jax.ShapeDtypeStruct((8, 128), F32), jax.ShapeDtypeStruct((L, D_MODEL), F32),
                   jax.ShapeDtypeStruct((1, D_MODEL), F32)],
        compiler_params=_params(),
    )(x, g, target)


_C_K = ATTN_WIDTH
_C_V = 2 * ATTN_WIDTH
_C_U = 3 * ATTN_WIDTH
_C_G = _C_U + SSM_WIDTH


def _residue_spec(d, tm):
    return pl.BlockSpec((d, tm // d, GROUP_WIDTH), lambda i: (0, i, 0))


def _residue_shape(d, L, dtype):
    return jax.ShapeDtypeStruct((d, L // d, GROUP_WIDTH), dtype)


def _residue_scratch(tm):
    return pltpu.VMEM((GROUP_WIDTH // LANES, tm, LANES), F32)


def _to_residues(val, out_ref, scr, d):
    if d == 1:
        out_ref[0] = val.astype(out_ref.dtype)
        return
    tm = val.shape[0]
    for half in range(GROUP_WIDTH // LANES):
        cols = slice(half * LANES, (half + 1) * LANES)
        scr[half] = val[:, cols]
        for r in range(d):
            out_ref[r, :, cols] = scr[half, pl.ds(r, tm // d, stride=d), :].astype(out_ref.dtype)


def _from_residues(ref, scr, d):
    if d == 1:
        return ref[0].astype(F32)
    rows = ref.shape[1]
    for half in range(GROUP_WIDTH // LANES):
        cols = slice(half * LANES, (half + 1) * LANES)
        for r in range(d):
            scr[half, pl.ds(r, rows, stride=d), :] = ref[r, :, cols].astype(F32)
    return jnp.concatenate([scr[half] for half in range(GROUP_WIDTH // LANES)], axis=1)


def _mix_in_fwd(x, g, w_in, gate_bias):
    L = x.shape[0]
    tm = min(ROW_TILE, L)

    def body(x_ref, g_ref, w_ref, gb_ref, *refs):
        qkv_refs, (u_ref, gate_ref, scr) = refs[:9], refs[9:]
        r, xhat = _rms(x_ref[...])
        h = (xhat * g_ref[...]).astype(BF16)
        for part, (c0, scale) in enumerate(((0, Q_SCALE), (_C_K, 1.0), (_C_V, 1.0))):
            z = _dot_nt(h, w_ref[c0:c0 + ATTN_WIDTH, :]) * scale
            for grp, d in enumerate(DILATIONS):
                _to_residues(z[:, grp * GROUP_WIDTH:(grp + 1) * GROUP_WIDTH], qkv_refs[3 * part + grp], scr, d)
        u_ref[...] = _dot_nt(h, w_ref[_C_U:_C_G, :])
        gate_ref[...] = _sigmoid(_dot_nt(h, w_ref[_C_G:IN_WIDTH, :]) + gb_ref[...])

    return pl.pallas_call(
        body, name="mix_in_fwd", grid=(L // tm,),
        in_specs=[_rows(tm, D_MODEL), _whole(), _whole(), _whole()],
        out_specs=[_residue_spec(d, tm) for d in DILATIONS] * 3 + [_rows(tm, SSM_WIDTH), _rows(tm, 2 * D_MODEL)],
        out_shape=[_residue_shape(d, L, BF16) for d in DILATIONS] * 3
        + [jax.ShapeDtypeStruct((L, SSM_WIDTH), F32), jax.ShapeDtypeStruct((L, 2 * D_MODEL), F32)],
        scratch_shapes=[_residue_scratch(tm)],
        compiler_params=_params(),
    )(x, g, w_in, gate_bias)


def _mix_in_bwd(dx2, x, g, dqkv, du, dgp, w_in):
    L = x.shape[0]
    tm = min(ROW_TILE, L)

    def body(dx2_ref, x_ref, g_ref, *refs):
        piece_refs = refs[:9]
        du_ref, dgp_ref, w_ref, dx1_ref, h_ref, dz_ref, dg_ref, scr = refs[9:]
        i = pl.program_id(0)
        gv = g_ref[...]
        r, xhat = _rms(x_ref[...])
        h_ref[...] = (xhat * gv).astype(BF16)
        for part in range(3):
            for grp, d in enumerate(DILATIONS):
                c0 = part * ATTN_WIDTH + grp * GROUP_WIDTH
                dz_ref[:, c0:c0 + GROUP_WIDTH] = _from_residues(piece_refs[3 * part + grp], scr, d).astype(BF16)
        dz_ref[:, _C_U:_C_G] = du_ref[...].astype(BF16)
        dz_ref[:, _C_G:IN_WIDTH] = dgp_ref[...]
        dh = _dot(dz_ref[...], w_ref[...])

        @pl.when(i == 0)
        def _():
            dg_ref[...] = jnp.zeros_like(dg_ref)

        dg_ref[...] += jnp.sum(dh * xhat, axis=0, keepdims=True)
        dx1_ref[...] = dx2_ref[...] + _rms_bwd(dh, gv, r, xhat)

    return pl.pallas_call(
        body, name="mix_in_bwd", grid=(L // tm,),
        in_specs=[_rows(tm, D_MODEL), _rows(tm, D_MODEL), _whole()] + [_residue_spec(d, tm) for d in DILATIONS] * 3
        + [_rows(tm, SSM_WIDTH), _rows(tm, 2 * D_MODEL), _whole()],
        out_specs=[_rows(tm, D_MODEL), _rows(tm, D_MODEL), _rows(tm, IN_WIDTH), _acc_row(D_MODEL)],
        out_shape=[jax.ShapeDtypeStruct((L, D_MODEL), F32), jax.ShapeDtypeStruct((L, D_MODEL), BF16),
                   jax.ShapeDtypeStruct((L, IN_WIDTH), BF16), jax.ShapeDtypeStruct((1, D_MODEL), F32)],
        scratch_shapes=[_residue_scratch(tm)],
        compiler_params=_params(),
    )(dx2, x, g, *dqkv, du, dgp, w_in)


def _bucket_onehot():
    qi = jnp.arange(ATTN_BLOCK)[:, None]
    kj = jnp.arange(2 * ATTN_BLOCK)[None, :]
    steps = jnp.maximum(qi + ATTN_BLOCK - kj, 0)
    max_exact = N_BUCKETS // 2
    out = []
    for d in DILATIONS:
        dist = steps * d
        df = jnp.maximum(dist, 1).astype(F32)
        large = max_exact + (jnp.log(df / max_exact) / math.log(MAX_DISTANCE / max_exact)
                             * (N_BUCKETS - max_exact)).astype(jnp.int32)
        large = jnp.minimum(large, N_BUCKETS - 1)
        bucket = jnp.where(dist < max_exact, dist, large).reshape(-1)
        out.append((bucket[None, :] == jnp.arange(N_BUCKETS)[:, None]).astype(F32))
    return jnp.stack(out)


def _bias_expand(table_t, onehot):
    n = onehot.shape[-1]

    def body(t_ref, oh_ref, o_ref):
        bias = _dot_exact(t_ref[...], oh_ref[...])
        col = lax.broadcasted_iota(jnp.int32, (8, n), 1)
        qi = col // (2 * ATTN_BLOCK)
        kj = col - qi * (2 * ATTN_BLOCK)
        steps = qi + ATTN_BLOCK - kj
        band = (steps >= 0) & (steps <= WINDOW_STEPS)
        o_ref[0] = jnp.where(band & (kj >= ATTN_BLOCK), bias, NEG_INF)
        o_ref[1] = jnp.where(band, bias, NEG_INF)

    return pl.pallas_call(
        body, name="bias_expand", grid=(3,),
        in_specs=[pl.BlockSpec((None, 8, N_BUCKETS), lambda g: (g, 0, 0)),
                  pl.BlockSpec((None, N_BUCKETS, n), lambda g: (g, 0, 0))],
        out_specs=pl.BlockSpec((None, 2, 8, n), lambda g: (g, 0, 0, 0)),
        out_shape=jax.ShapeDtypeStruct((3, 2, 8, n), F32),
        compiler_params=_params(),
    )(table_t, onehot)


def _bias_reduce(dsum, onehot):
    n = onehot.shape[-1]

    def body(d_ref, oh_ref, o_ref):
        o_ref[...] = _dot_nt_exact(d_ref[...], oh_ref[...])

    return pl.pallas_call(
        body, name="bias_reduce", grid=(3,),
        in_specs=[pl.BlockSpec((None, 8, n), lambda g: (g, 0, 0)),
                  pl.BlockSpec((None, N_BUCKETS, n), lambda g: (g, 0, 0))],
        out_specs=pl.BlockSpec((None, 8, N_BUCKETS), lambda g: (g, 0, 0)),
        out_shape=jax.ShapeDtypeStruct((3, 8, N_BUCKETS), F32),
        compiler_params=_params(),
    )(dsum, onehot)


def _head_of_col(rows):
    return lax.broadcasted_iota(jnp.int32, (rows, GROUP_WIDTH), 1) // HEAD_DIM


def _attn_specs(qb):
    rows = qb * ATTN_BLOCK
    cur = pl.BlockSpec((None, rows, GROUP_WIDTH), lambda r, n: (r, n, 0))
    prev = pl.BlockSpec((None, ATTN_BLOCK, GROUP_WIDTH), lambda r, n: (r, jnp.maximum(n * qb - 1, 0), 0))
    bias = pl.BlockSpec((2, HEADS_PER_GROUP, ATTN_BLOCK, 2 * ATTN_BLOCK), lambda r, n: (0, 0, 0, 0))
    return cur, prev, bias


def _attn_fwd(q, k, v, bias, name):
    d, M, _ = q.shape
    nb = M // ATTN_BLOCK
    qb = min(ATTN_QB, nb)

    def body(q_ref, kp_ref, kc_ref, vp_ref, vc_ref, bias_ref, o_ref, lse_ref):
        n = pl.program_id(1)
        q_head = _head_of_col(ATTN_BLOCK)
        kv_head = _head_of_col(2 * ATTN_BLOCK)
        kwin = jnp.concatenate([kp_ref[...], kc_ref[...]], axis=0)
        vwin = jnp.concatenate([vp_ref[...], vc_ref[...]], axis=0)
        for b in range(qb):
            rows = slice(b * ATTN_BLOCK, (b + 1) * ATTN_BLOCK)
            window = slice(b * ATTN_BLOCK, (b + 2) * ATTN_BLOCK)
            variant = jnp.minimum(n, 1) if b == 0 else 1
            qv = q_ref[rows, :]
            kk = kwin[window]
            vv = vwin[window]
            o_acc = jnp.zeros((ATTN_BLOCK, GROUP_WIDTH), F32)
            lse_acc = jnp.zeros((ATTN_BLOCK, GROUP_WIDTH), F32)
            for hh in range(HEADS_PER_GROUP):
                hm = q_head == hh
                qh = jnp.where(hm, qv, jnp.zeros_like(qv))
                logits = _dot_nt(qh, kk) + bias_ref[variant, hh]
                m = jnp.max(logits, axis=1, keepdims=True)
                p = jnp.exp(logits - m)
                vh = jnp.where(kv_head == hh, vv, jnp.ones_like(vv))
                pv = _dot(p.astype(BF16), vh)
                c_sum = ((hh + 1) % HEADS_PER_GROUP) * HEAD_DIM
                den = pv[:, c_sum:c_sum + 1]
                o_acc = jnp.where(hm, pv * (1.0 / den), o_acc)
                lse_acc = jnp.where(hm, m + jnp.log(den), lse_acc)
            o_ref[rows, :] = o_acc
            lse_ref[rows, :] = lse_acc

    cur, prev, full = _attn_specs(qb)
    return pl.pallas_call(
        body, name=name, grid=(d, nb // qb),
        in_specs=[cur, prev, cur, prev, cur, full],
        out_specs=[cur, cur],
        out_shape=[jax.ShapeDtypeStruct((d, M, GROUP_WIDTH), F32)] * 2,
        compiler_params=_params(),
    )(q, k, k, v, v, bias)


def _attn_bwd(q, k, v, do, lse, delta, bias, name):
    d, M, _ = q.shape
    nb = M // ATTN_BLOCK
    qb = min(ATTN_QB, nb)
    ns = nb // qb
    rows_q = qb * ATTN_BLOCK
    last = slice(rows_q - ATTN_BLOCK, rows_q)

    def body(q_ref, kp_ref, kc_ref, vp_ref, vc_ref, do_ref, lse_ref, dl_ref, bias_ref,
             dq_ref, dk_ref, dv_ref, dsum_ref, pk_ref, pv_ref, wk_ref, wv_ref):
        r = pl.program_id(0)
        n = pl.program_id(1)

        @pl.when((r == 0) & (n == 0))
        def _():
            dsum_ref[...] = jnp.zeros_like(dsum_ref)

        @pl.when(n == 0)
        def _():
            pk_ref[...] = jnp.zeros_like(pk_ref)
            pv_ref[...] = jnp.zeros_like(pv_ref)

        @pl.when(n < ns)
        def _():
            q_head = _head_of_col(ATTN_BLOCK)
            kwin = jnp.concatenate([kp_ref[...], kc_ref[...]], axis=0)
            vwin = jnp.concatenate([vp_ref[...], vc_ref[...]], axis=0)
            wk_ref[...] = jnp.zeros_like(wk_ref)
            wv_ref[...] = jnp.zeros_like(wv_ref)
            for b in range(qb):
                rows = slice(b * ATTN_BLOCK, (b + 1) * ATTN_BLOCK)
                window = slice(b * ATTN_BLOCK, (b + 2) * ATTN_BLOCK)
                variant = jnp.minimum(n, 1) if b == 0 else 1
                qv = q_ref[rows, :]
                dov = do_ref[rows, :]
                kk = kwin[window]
                vv = vwin[window]
                dq_acc = jnp.zeros((ATTN_BLOCK, GROUP_WIDTH), F32)
                dkk = jnp.zeros((2 * ATTN_BLOCK, GROUP_WIDTH), F32)
                dvv = jnp.zeros((2 * ATTN_BLOCK, GROUP_WIDTH), F32)
                for hh in range(HEADS_PER_GROUP):
                    hm = q_head == hh
                    c0 = hh * HEAD_DIM
                    qh = jnp.where(hm, qv, jnp.zeros_like(qv))
                    doh = jnp.where(hm, dov, jnp.zeros_like(dov))
                    logits = _dot_nt(qh, kk) + bias_ref[variant, hh]
                    p = jnp.exp(logits - lse_ref[rows, c0:c0 + 1])
                    dp = _dot_nt(doh, vv)
                    ds = p * (dp - dl_ref[rows, c0:c0 + 1])
                    dsum_ref[hh] += ds
                    ds16 = ds.astype(BF16)
                    dq_acc = jnp.where(hm, _dot(ds16, kk), dq_acc)
                    dkk = dkk + _dot_tn(ds16, qh)
                    dvv = dvv + _dot_tn(p.astype(BF16), doh)
                dq_ref[rows, :] = (dq_acc * Q_SCALE).astype(BF16)
                wk_ref[window, :] += dkk
                wv_ref[window, :] += dvv
            for out_ref, part_ref, win_ref in ((dk_ref, pk_ref, wk_ref), (dv_ref, pv_ref, wv_ref)):
                if qb > 1:
                    out_ref[0:rows_q - ATTN_BLOCK, :] = part_ref[0:rows_q - ATTN_BLOCK, :].astype(BF16)
                out_ref[last, :] = (part_ref[last, :] + win_ref[0:ATTN_BLOCK, :]).astype(BF16)
                part_ref[...] = win_ref[ATTN_BLOCK:, :]

        @pl.when(n == ns)
        def _():
            dk_ref[...] = pk_ref[...].astype(BF16)
            dv_ref[...] = pv_ref[...].astype(BF16)

    def clamp(n):
        return jnp.minimum(n, ns - 1)

    cur = pl.BlockSpec((None, rows_q, GROUP_WIDTH), lambda r, n: (r, clamp(n), 0))
    prev = pl.BlockSpec((None, ATTN_BLOCK, GROUP_WIDTH), lambda r, n: (r, jnp.maximum(clamp(n) * qb - 1, 0), 0))
    lag = pl.BlockSpec((None, rows_q, GROUP_WIDTH), lambda r, n: (r, jnp.maximum(n - 1, 0), 0))
    full = pl.BlockSpec((2, HEADS_PER_GROUP, ATTN_BLOCK, 2 * ATTN_BLOCK), lambda r, n: (0, 0, 0, 0))
    acc = pl.BlockSpec((HEADS_PER_GROUP, ATTN_BLOCK, 2 * ATTN_BLOCK), lambda r, n: (0, 0, 0))
    return pl.pallas_call(
        body, name=name, grid=(d, ns + 1),
        in_specs=[cur, prev, cur, prev, cur, cur, cur, cur, full],
        out_specs=[cur, lag, lag, acc],
        out_shape=[jax.ShapeDtypeStruct((d, M, GROUP_WIDTH), BF16)] * 3
        + [jax.ShapeDtypeStruct((HEADS_PER_GROUP, ATTN_BLOCK, 2 * ATTN_BLOCK), F32)],
        scratch_shapes=[pltpu.VMEM((rows_q, GROUP_WIDTH), F32), pltpu.VMEM((rows_q, GROUP_WIDTH), F32),
                        pltpu.VMEM((rows_q + ATTN_BLOCK, GROUP_WIDTH), F32),
                        pltpu.VMEM((rows_q + ATTN_BLOCK, GROUP_WIDTH), F32)],
        compiler_params=_params(),
    )(q, k, k, v, v, do, lse, delta, bias)


def _disc_math(a_re, a_im, ldt, b_re, b_im):
    dt = jnp.exp(ldt)
    mag = jnp.exp(a_re * dt)
    ab_re = mag * jnp.cos(a_im * dt)
    ab_im = mag * jnp.sin(a_im * dt)
    den = a_re * a_re + a_im * a_im
    xr = ab_re - 1.0
    coef_re = (xr * a_re + ab_im * a_im) / den
    coef_im = (ab_im * a_re - xr * a_im) / den
    return ab_re, ab_im, coef_re * b_re - coef_im * b_im, coef_re * b_im + coef_im * b_re


def _block_diag_mask():
    row_g = lax.broadcasted_iota(jnp.int32, (SSM_WIDTH, 2 * NS), 0) // SSM_GROUP
    col = lax.broadcasted_iota(jnp.int32, (SSM_WIDTH, 2 * NS), 1)
    col_g = jnp.where(col >= NS, col - NS, col) // SSM_STATE
    return row_g == col_g


def _disc_fwd(a_re, a_im, ldt, b_re, b_im, c_re, c_im):
    def body(are_ref, aim_ref, ldt_ref, bre_ref, bim_ref, cre_ref, cim_ref, pw_ref, pwr_ref, bd_ref, cdt_ref):
        ab_re, ab_im, bb_re, bb_im = _disc_math(are_ref[...], aim_ref[...], ldt_ref[...], bre_ref[...], bim_ref[...])
        row = lax.broadcasted_iota(jnp.int32, (8, NS), 0)
        pr, pi = ab_re, ab_im
        t_re = jnp.zeros((8, NS), F32)
        t_im = jnp.zeros((8, NS), F32)
        u_re = jnp.zeros((8, NS), F32)
        u_im = jnp.zeros((8, NS), F32)
        for j in range(8):
            t_re = jnp.where(row == j, pr, t_re)
            t_im = jnp.where(row == j, pi, t_im)
            u_re = jnp.where(row == 7 - j, pr, u_re)
            u_im = jnp.where(row == 7 - j, pi, u_im)
            pr, pi = pr * ab_re - pi * ab_im, pr * ab_im + pi * ab_re
        pw_ref[0] = t_re
        pw_ref[1] = t_im
        pwr_ref[0] = u_re
        pwr_ref[1] = u_im
        mask = _block_diag_mask()
        zero = jnp.zeros((SSM_WIDTH, 2 * NS), F32)
        bfull = jnp.concatenate([jnp.concatenate([bb_re] * SSM_GROUPS, axis=0),
                                 jnp.concatenate([bb_im] * SSM_GROUPS, axis=0)], axis=1)
        bd_ref[...] = jnp.where(mask, bfull, zero).astype(BF16)
        cfull = jnp.concatenate([jnp.concatenate([cre_ref[...]] * SSM_GROUPS, axis=0),
                                 jnp.concatenate([-cim_ref[...]] * SSM_GROUPS, axis=0)], axis=1)
        cdt_ref[...] = jnp.where(mask, cfull, zero).astype(BF16)

    return pl.pallas_call(
        body, name="s5_disc_fwd",
        in_specs=[_whole()] * 7, out_specs=[_whole()] * 4,
        out_shape=[jax.ShapeDtypeStruct((2, 8, NS), F32), jax.ShapeDtypeStruct((2, 8, NS), F32),
                   jax.ShapeDtypeStruct((SSM_WIDTH, 2 * NS), BF16), jax.ShapeDtypeStruct((SSM_WIDTH, 2 * NS), BF16)],
        compiler_params=_params(),
    )(a_re, a_im, ldt, b_re, b_im, c_re, c_im)


def _disc_bwd(a_re, a_im, ldt, b_re, b_im, d_bd, d_cdt, d_ab, group_sum):
    def body(are_ref, aim_ref, ldt_ref, bre_ref, bim_ref, dbd_ref, dcdt_ref, dab_ref, gs_ref,
             dare_ref, daim_ref, dldt_ref, dbre_ref, dbim_ref, dcre_ref, dcim_ref):
        col = lax.broadcasted_iota(jnp.int32, (SSM_GROUP, 2 * NS), 1)
        col_g = jnp.where(col >= NS, col - NS, col) // SSM_STATE
        acc_b = jnp.zeros((SSM_GROUP, 2 * NS), F32)
        acc_c = jnp.zeros((SSM_GROUP, 2 * NS), F32)
        for g in range(SSM_GROUPS):
            rows = slice(g * SSM_GROUP, (g + 1) * SSM_GROUP)
            acc_b = acc_b + jnp.where(col_g == g, dbd_ref[rows, :], 0.0)
            acc_c = acc_c + jnp.where(col_g == g, dcdt_ref[rows, :], 0.0)
        dcre_ref[...] = acc_c[:, :NS]
        dcim_ref[...] = -acc_c[:, NS:]
        dab_re = jnp.sum(dab_ref[0], axis=0, keepdims=True)
        dab_im = jnp.sum(dab_ref[1], axis=0, keepdims=True)
        _, vjp = jax.vjp(_disc_math, are_ref[...], aim_ref[...], ldt_ref[...], bre_ref[...], bim_ref[...])
        d_are, d_aim, d_ldt, d_bre, d_bim = vjp((dab_re, dab_im, acc_b[:, :NS], acc_b[:, NS:]))
        dare_ref[...] = d_are
        daim_ref[...] = d_aim
        dbre_ref[...] = d_bre
        dbim_ref[...] = d_bim
        dldt_ref[...] = _dot_exact(jnp.broadcast_to(d_ldt, (8, NS)), gs_ref[...])

    vec = jax.ShapeDtypeStruct((1, NS), F32)
    mat = jax.ShapeDtypeStruct((SSM_GROUP, NS), F32)
    return pl.pallas_call(
        body, name="s5_disc_bwd",
        in_specs=[_whole()] * 9, out_specs=[_whole()] * 7,
        out_shape=[vec, vec, jax.ShapeDtypeStruct((8, 128), F32), mat, mat, mat, mat],
        compiler_params=_params(),
    )(a_re, a_im, ldt, b_re, b_im, d_bd, d_cdt, d_ab, group_sum)


def _scan_blocks(buf, pw_ref, carry_ref, n_blocks, reverse):
    row = lax.broadcasted_iota(jnp.int32, (8, SCAN_LANES), 0)
    for lc in range(NS // SCAN_LANES):
        re_cols = pl.ds(lc * SCAN_LANES, SCAN_LANES)
        im_cols = pl.ds(NS + lc * SCAN_LANES, SCAN_LANES)
        pr = pw_ref[0, :, re_cols]
        pi = pw_ref[1, :, re_cols]
        if reverse:
            pi = -pi
            base = [(7, 1), (6, 2), (4, 4)]
            coef = [(jnp.where(row < 8 - k, pr[j:j + 1], 0.0), jnp.where(row < 8 - k, pi[j:j + 1], 0.0), 8 - k)
                    for j, k in base]
        else:
            base = [(0, 1), (1, 2), (3, 4)]
            coef = [(jnp.where(row >= k, pr[j:j + 1], 0.0), jnp.where(row >= k, pi[j:j + 1], 0.0), k)
                    for j, k in base]

        def step(i, carry, pr=pr, pi=pi, coef=coef, re_cols=re_cols, im_cols=im_cols):
            cr, ci = carry
            blk = (n_blocks - 1 - i) if reverse else i
            rows = pl.ds(pl.multiple_of(blk * 8, 8), 8)
            xr = buf[rows, re_cols]
            xi = buf[rows, im_cols]
            for kr, ki, shift in coef:
                sr = pltpu.roll(xr, shift, 0)
                si = pltpu.roll(xi, shift, 0)
                xr, xi = xr + kr * sr - ki * si, xi + kr * si + ki * sr
            xr, xi = xr + pr * cr - pi * ci, xi + pr * ci + pi * cr
            buf[rows, re_cols] = xr
            buf[rows, im_cols] = xi
            edge = slice(0, 1) if reverse else slice(7, 8)
            return xr[edge], xi[edge]

        cr, ci = lax.fori_loop(0, n_blocks, step, (carry_ref[0:1, re_cols], carry_ref[0:1, im_cols]))
        carry_ref[0:1, re_cols] = cr
        carry_ref[0:1, im_cols] = ci


_SUPER_GROUPS = 16
_SUPER_BLOCKS = [
    (slice(k * _SUPER_GROUPS * SSM_GROUP, (k + 1) * _SUPER_GROUPS * SSM_GROUP),
     [slice(half + k * _SUPER_GROUPS * SSM_STATE, half + (k + 1) * _SUPER_GROUPS * SSM_STATE) for half in (0, NS)])
    for k in range(SSM_GROUPS // _SUPER_GROUPS)]


def _ssm_fwd(u, bd, cdt, d_skip, pw):
    L = u.shape[0]
    tc = min(SSM_CHUNK, L)

    def body(u_ref, bd_ref, cdt_ref, dsk_ref, pw_ref, y_ref, s_ref, carry_ref):
        @pl.when(pl.program_id(0) == 0)
        def _():
            carry_ref[...] = jnp.zeros_like(carry_ref)

        uv = u_ref[...]
        u16 = uv.astype(BF16)
        for ch, states in _SUPER_BLOCKS:
            for st in states:
                s_ref[:, st] = _dot(u16[:, ch], bd_ref[ch, st])
        _scan_blocks(s_ref, pw_ref, carry_ref, tc // 8, reverse=False)
        for ch, states in _SUPER_BLOCKS:
            y_ref[:, ch] = (sum(_dot_nt(s_ref[:, st].astype(BF16), cdt_ref[ch, st]) for st in states)
                            + dsk_ref[:, ch] * uv[:, ch])

    return pl.pallas_call(
        body, name="s5_fwd", grid=(L // tc,),
        in_specs=[_rows(tc, SSM_WIDTH), _whole(), _whole(), _whole(), _whole()],
        out_specs=[_rows(tc, SSM_WIDTH), _rows(tc, 2 * NS)],
        out_shape=[jax.ShapeDtypeStruct((L, SSM_WIDTH), F32), jax.ShapeDtypeStruct((L, 2 * NS), F32)],
        scratch_shapes=[pltpu.VMEM((8, 2 * NS), F32)],
        compiler_params=_params(),
    )(u, bd, cdt, d_skip, pw)


def _ssm_bwd(dy, u, s, bd, cdt, d_skip, pwr):
    L = u.shape[0]
    tc = min(SSM_CHUNK, L)
    nc = L // tc
    blocks = tc // 8

    def body(dy_ref, u_ref, s_ref, sprev_ref, bd_ref, cdt_ref, dsk_ref, pwr_ref,
             du_ref, ddsk_ref, dbd_ref, dcdt_ref, dab_ref, g_ref, sx_ref, carry_ref):
        i = pl.program_id(0)

        @pl.when(i == 0)
        def _():
            carry_ref[...] = jnp.zeros_like(carry_ref)
            ddsk_ref[...] = jnp.zeros_like(ddsk_ref)
            dbd_ref[...] = jnp.zeros_like(dbd_ref)
            dcdt_ref[...] = jnp.zeros_like(dcdt_ref)
            dab_ref[...] = jnp.zeros_like(dab_ref)

        dyv = dy_ref[...]
        uv = u_ref[...]
        dy16 = dyv.astype(BF16)
        u16 = uv.astype(BF16)
        for ch, states in _SUPER_BLOCKS:
            for st in states:
                g_ref[:, st] = _dot(dy16[:, ch], cdt_ref[ch, st])
        _scan_blocks(g_ref, pwr_ref, carry_ref, blocks, reverse=True)
        ddsk_ref[...] += jnp.sum(dyv * uv, axis=0, keepdims=True)
        for ch, states in _SUPER_BLOCKS:
            du = dsk_ref[:, ch] * dyv[:, ch]
            for st in states:
                g16 = g_ref[:, st].astype(BF16)
                du = du + _dot_nt(g16, bd_ref[ch, st])
                dbd_ref[ch, st] += _dot_tn(u16[:, ch], g16)
                dcdt_ref[ch, st] += _dot_tn(dy16[:, ch], s_ref[:, st].astype(BF16))
            du_ref[:, ch] = du

        sx_ref[pl.ds(8, tc), :] = s_ref[...]
        sx_ref[pl.ds(0, 8), :] = jnp.where(i == nc - 1, 0.0, sprev_ref[...])
        row = lax.broadcasted_iota(jnp.int32, (8, SCAN_LANES), 0)
        for lc in range(NS // SCAN_LANES):
            re_cols = pl.ds(lc * SCAN_LANES, SCAN_LANES)
            im_cols = pl.ds(NS + lc * SCAN_LANES, SCAN_LANES)

            def step(b, acc, re_cols=re_cols, im_cols=im_cols):
                ar, ai = acc
                off = pl.multiple_of(b * 8, 8)
                gr = g_ref[pl.ds(off, 8), re_cols]
                gi = g_ref[pl.ds(off, 8), im_cols]
                before = pl.ds(off, 8)
                here = pl.ds(off + 8, 8)
                sr = jnp.where(row == 0, sx_ref[before, re_cols][7:8], pltpu.roll(sx_ref[here, re_cols], 1, 0))
                si = jnp.where(row == 0, sx_ref[before, im_cols][7:8], pltpu.roll(sx_ref[here, im_cols], 1, 0))
                return ar + gr * sr + gi * si, ai + gi * sr - gr * si

            zero = jnp.zeros((8, SCAN_LANES), F32)
            ar, ai = lax.fori_loop(0, blocks, step, (zero, zero))
            dab_ref[0, :, re_cols] += ar
            dab_ref[1, :, re_cols] += ai

    rev = lambda i: (nc - 1 - i, 0)
    sprev = pl.BlockSpec((8, 2 * NS), lambda i: (jnp.maximum((nc - 1 - i) * blocks - 1, 0), 0))
    return pl.pallas_call(
        body, name="s5_bwd", grid=(nc,),
        in_specs=[pl.BlockSpec((tc, SSM_WIDTH), rev), pl.BlockSpec((tc, SSM_WIDTH), rev),
                  pl.BlockSpec((tc, 2 * NS), rev), sprev, _whole(), _whole(), _whole(), _whole()],
        out_specs=[pl.BlockSpec((tc, SSM_WIDTH), rev), _whole(), _whole(), _whole(), _whole()],
        out_shape=[jax.ShapeDtypeStruct((L, SSM_WIDTH), F32), jax.ShapeDtypeStruct((1, SSM_WIDTH), F32),
                   jax.ShapeDtypeStruct((SSM_WIDTH, 2 * NS), F32), jax.ShapeDtypeStruct((SSM_WIDTH, 2 * NS), F32),
                   jax.ShapeDtypeStruct((2, 8, NS), F32)],
        scratch_shapes=[pltpu.VMEM((tc, 2 * NS), F32), pltpu.VMEM((tc + 8, 2 * NS), F32), pltpu.VMEM((8, 2 * NS), F32)],
        compiler_params=_params(),
    )(dy, u, s, s, bd, cdt, d_skip, pwr)


def _branches(o_attn, y, gates, w_ab, w_glu, w_sb):
    ya = _dot(o_attn.astype(BF16), w_ab[...])
    gel = _gelu(y)
    glu = _dot(gel.astype(BF16), w_glu[...])
    p = glu[:, :SSM_WIDTH]
    sg = _sigmoid(glu[:, SSM_WIDTH:])
    ys2 = p * sg
    ysb = _dot(ys2.astype(BF16), w_sb[...])
    ga = gates[:, :D_MODEL]
    gs = gates[:, D_MODEL:]
    return ya, gel, p, sg, ys2, ysb, ga, gs


def _mix_out_fwd(x1, o_g, lse_g, y, gates, w_ab, w_glu, w_sb, w_out):
    L = x1.shape[0]
    tm = min(ROW_TILE, L)

    def body(x_ref, o0, o1, o2, l0, l1, l2, y_ref, gate_ref, wab_ref, wglu_ref, wsb_ref, wout_ref,
             x2_ref, oat_ref, lse0, lse1, lse2, scr):
        la, lb, lc = (_from_residues(ref, scr, d) for ref, d in zip((l0, l1, l2), DILATIONS))
        m = jnp.maximum(jnp.maximum(la, lb), lc)
        ea, eb, ec = jnp.exp(la - m), jnp.exp(lb - m), jnp.exp(lc - m)
        tot = ea + eb + ec
        oa, ob, oc = (_from_residues(ref, scr, d) for ref, d in zip((o0, o1, o2), DILATIONS))
        o_attn = (ea * oa + eb * ob + ec * oc) / tot
        oat_ref[...] = o_attn
        lse = m + jnp.log(tot)
        for ref, d in zip((lse0, lse1, lse2), DILATIONS):
            _to_residues(lse, ref, scr, d)
        ya, _, _, _, _, ysb, ga, gs = _branches(o_attn, y_ref[...], gate_ref[...], wab_ref, wglu_ref, wsb_ref)
        mix = ga * ya + gs * ysb
        x2_ref[...] = x_ref[...] + _dot(mix.astype(BF16), wout_ref[...])

    res = [_residue_spec(d, tm) for d in DILATIONS]
    return pl.pallas_call(
        body, name="mix_out_fwd", grid=(L // tm,),
        in_specs=[_rows(tm, D_MODEL)] + res * 2 + [_rows(tm, SSM_WIDTH), _rows(tm, 2 * D_MODEL)] + [_whole()] * 4,
        out_specs=[_rows(tm, D_MODEL), _rows(tm, GROUP_WIDTH)] + res,
        out_shape=[jax.ShapeDtypeStruct((L, D_MODEL), F32), jax.ShapeDtypeStruct((L, GROUP_WIDTH), F32)]
        + [_residue_shape(d, L, F32) for d in DILATIONS],
        scratch_shapes=[_residue_scratch(tm)],
        compiler_params=_params(),
    )(x1, *o_g, *lse_g, y, gates, w_ab, w_glu, w_sb, w_out)


def _mix_out_bwd(dx2, o_attn, y, gates, w_ab, w_glu, w_sb, w_out, head_sum):
    L = dx2.shape[0]
    tm = min(ROW_TILE, L)

    def body(dx_ref, oat_ref, y_ref, gate_ref, wab_ref, wglu_ref, wsb_ref, wout_ref, hs_ref,
             do0, do1, do2, dl0, dl1, dl2, dy_ref, dgp_ref, mix_ref, dya_ref, dys_ref, ys2_ref, gel_ref, dglu_ref,
             dgb_ref, scr):
        i = pl.program_id(0)
        o_attn = oat_ref[...]
        yv = y_ref[...]
        ya, gel, p, sg, ys2, ysb, ga, gs = _branches(o_attn, yv, gate_ref[...], wab_ref, wglu_ref, wsb_ref)
        mix_ref[...] = (ga * ya + gs * ysb).astype(BF16)
        ys2_ref[...] = ys2.astype(BF16)
        gel_ref[...] = gel.astype(BF16)
        dmix = _dot_nt(dx_ref[...].astype(BF16), wout_ref[...])
        dgp = jnp.concatenate([dmix * ya * ga * (1.0 - ga), dmix * ysb * gs * (1.0 - gs)], axis=1)
        dgp_ref[...] = dgp.astype(BF16)

        @pl.when(i == 0)
        def _():
            dgb_ref[...] = jnp.zeros_like(dgb_ref)

        dgb_ref[...] += jnp.sum(dgp, axis=0, keepdims=True)
        dya = (dmix * ga).astype(BF16)
        dys = (dmix * gs).astype(BF16)
        dya_ref[...] = dya
        dys_ref[...] = dys
        d_o = _dot_nt(dya, wab_ref[...])
        delta = _dot_exact(d_o * o_attn, hs_ref[...])
        for do_ref, dl_ref, d in zip((do0, do1, do2), (dl0, dl1, dl2), DILATIONS):
            _to_residues(d_o, do_ref, scr, d)
            _to_residues(delta, dl_ref, scr, d)
        dys2 = _dot_nt(dys, wsb_ref[...])
        dglu = jnp.concatenate([dys2 * sg, dys2 * p * sg * (1.0 - sg)], axis=1).astype(BF16)
        dglu_ref[...] = dglu
        dy_ref[...] = _dot_nt(dglu, wglu_ref[...]) * _gelu_grad(yv)

    grp = _rows(tm, GROUP_WIDTH)
    wide = _rows(tm, D_MODEL)
    half = _rows(tm, SSM_WIDTH)
    res = [_residue_spec(d, tm) for d in DILATIONS]
    sds = jax.ShapeDtypeStruct
    return pl.pallas_call(
        body, name="mix_out_bwd", grid=(L // tm,),
        in_specs=[wide, grp, half, _rows(tm, 2 * D_MODEL)] + [_whole()] * 5,
        out_specs=res + res + [half, _rows(tm, 2 * D_MODEL), wide, wide, wide, half, half, wide, _acc_row(2 * D_MODEL)],
        out_shape=[_residue_shape(d, L, BF16) for d in DILATIONS] + [_residue_shape(d, L, F32) for d in DILATIONS]
        + [sds((L, SSM_WIDTH), F32),
           sds((L, 2 * D_MODEL), BF16), sds((L, D_MODEL), BF16), sds((L, D_MODEL), BF16),
           sds((L, D_MODEL), BF16), sds((L, SSM_WIDTH), BF16), sds((L, SSM_WIDTH), BF16),
           sds((L, D_MODEL), BF16), sds((1, 2 * D_MODEL), F32)],
        scratch_shapes=[_residue_scratch(tm)],
        compiler_params=_params(),
    )(dx2, o_attn, y, gates, w_ab, w_glu, w_sb, w_out, head_sum)


def _adamw(w, g, m, v, name):
    R, C = w.shape
    tr = _row_tile(R, max(8, ADAMW_BLOCK_BYTES // (4 * C)))

    def body(w_ref, g_ref, m_ref, v_ref, d_ref, mo_ref, vo_ref):
        gv = g_ref[...]
        mn = ADAM_B1 * m_ref[...] + (1.0 - ADAM_B1) * gv
        vn = ADAM_B2 * v_ref[...] + (1.0 - ADAM_B2) * (gv * gv)
        m_hat = mn / (1.0 - ADAM_B1 ** ADAM_STEP)
        v_hat = vn / (1.0 - ADAM_B2 ** ADAM_STEP)
        d_ref[...] = -ADAM_LR * (m_hat / (jnp.sqrt(v_hat) + ADAM_EPS) + ADAM_WD * w_ref[...])
        mo_ref[...] = mn
        vo_ref[...] = vn

    blk = pl.BlockSpec((tr, C), lambda i: (i, 0))
    return pl.pallas_call(
        body, name=name, grid=(R // tr,),
        in_specs=[blk] * 4, out_specs=[blk] * 3,
        out_shape=[jax.ShapeDtypeStruct((R, C), F32)] * 3,
        compiler_params=_params(),
    )(w, g, m, v)


def _sum_chips_into_half(u, t, name):
    S, H, C = u.shape
    tr = _row_tile(H, 512)
    hb = H // tr

    def body(s_ref, t_ref, a_ref, b_ref, c_ref, o_ref):
        me = s_ref[1]
        others = (a_ref[...], b_ref[...], c_ref[...])
        acc = None
        for chip in range(S):
            below = others[min(chip, S - 2)]
            above = others[max(chip - 1, 0)]
            term = jnp.where(me == chip, t_ref[...], jnp.where(me > chip, below, above)).astype(F32)
            acc = term if acc is None else acc + term
        o_ref[...] = acc

    x, y, c = lax.axis_index("x"), lax.axis_index("y"), lax.axis_index("c")
    me = 2 * x + y
    scalars = jnp.stack([c, me] + [j + (j >= me).astype(jnp.int32) for j in range(S - 1)]).astype(jnp.int32)
    blk = (None, tr, C)
    return pl.pallas_call(
        body, name=name,
        grid_spec=pltpu.PrefetchScalarGridSpec(
            num_scalar_prefetch=1, grid=(hb,),
            in_specs=[pl.BlockSpec(blk, lambda i, s: (s[1], i, 0))]
            + [pl.BlockSpec(blk, functools.partial(lambda j, i, s: (s[2 + j], i, 0), j)) for j in range(S - 1)],
            out_specs=pl.BlockSpec((tr, C), lambda i, s: (s[0] * hb + i, 0))),
        out_shape=jax.ShapeDtypeStruct((2 * H, C), F32),
        compiler_params=_params(),
    )(scalars, t, u, u, u)


def _add_halves(g, r1, name):
    S, R, C = g.shape
    H = R // 2
    tr = _row_tile(H, 512)
    hb = H // tr

    def body(c_ref, g_ref, r_ref, o_ref):
        o_ref[...] = (g_ref[...] + r_ref[...]).astype(BF16)

    core = lax.axis_index("c").astype(jnp.int32).reshape(1)
    return pl.pallas_call(
        body, name=name,
        grid_spec=pltpu.PrefetchScalarGridSpec(
            num_scalar_prefetch=1, grid=(S, hb),
            in_specs=[pl.BlockSpec((None, tr, C), lambda j, i, c_ref: (j, c_ref[0] * hb + i, 0)),
                      pl.BlockSpec((None, tr, C), lambda j, i, c_ref: (j, i, 0))],
            out_specs=pl.BlockSpec((None, tr, C), lambda j, i, c_ref: (j, i, 0))),
        out_shape=jax.ShapeDtypeStruct((S, H, C), BF16),
        compiler_params=_params(),
    )(core, g, r1)


_ANY = pl.BlockSpec(memory_space=pl.ANY)


def _place():
    x, y, c = lax.axis_index("x"), lax.axis_index("y"), lax.axis_index("c")
    chips = [(1 - x, y), (x, 1 - y), (1 - x, 1 - y)]
    return x, y, c, chips


def _comm_call(body, name, ins, out_shapes, n_remote, n_local):
    return pl.pallas_call(
        body, name=name,
        in_specs=[_ANY] * len(ins), out_specs=[_ANY] * len(out_shapes), out_shape=out_shapes,
        scratch_shapes=[pltpu.SemaphoreType.DMA((n_remote,)), pltpu.SemaphoreType.DMA((n_remote,)),
                        pltpu.SemaphoreType.DMA((max(n_local, 1),))],
    )(*ins)


def _remote(src, dst, send_sems, recv_sems, k, device):
    return pltpu.make_async_remote_copy(src_ref=src, dst_ref=dst, send_sem=send_sems.at[k], recv_sem=recv_sems.at[k],
                                        device_id=device, device_id_type=MESH)


def _gather_parts(shapes, w_refs, out_refs, send_sems, recv_sems):
    n = len(shapes)
    x, y, c, chips = _place()
    me = 2 * x + y
    sibling = (x, y, 1 - c)

    def half(k, chip_idx, core):
        H = shapes[k][0] // 2
        return out_refs[k].at[chip_idx, pl.ds(core * H, H), :]

    mine = [_remote(w_refs[k], out_refs[k].at[me], send_sems, recv_sems, 6 * n + k, sibling) for k in range(n)]
    first = []
    for k in range(n):
        H = shapes[k][0] // 2
        for j, (cx, cy) in enumerate(chips):
            first.append(_remote(w_refs[k].at[pl.ds(c * H, H), :], half(k, me, c), send_sems, recv_sems,
                                 3 * k + j, (cx, cy, c)))

    def start():
        for cp in mine + first:
            cp.start()

    def finish():
        passed = []
        for k in range(n):
            for j, (cx, cy) in enumerate(chips):
                landed = half(k, 2 * cx + cy, c)
                _remote(landed, landed, send_sems, recv_sems, 3 * k + j, (cx, cy, c)).wait_recv()
                fwd = _remote(landed, landed, send_sems, recv_sems, 3 * n + 3 * k + j, sibling)
                fwd.start()
                passed.append(fwd)
        for k in range(n):
            for j, (cx, cy) in enumerate(chips):
                other = half(k, 2 * cx + cy, 1 - c)
                _remote(other, other, send_sems, recv_sems, 3 * n + 3 * k + j, sibling).wait_recv()
        for cp in mine:
            cp.wait_recv()
        for cp in first + passed + mine:
            cp.wait_send()

    return start, finish


def _gather_weights(shards, name):
    n = len(shards)

    def body(*refs):
        start, finish = _gather_parts([w.shape for w in shards], refs[:n], refs[n:2 * n], *refs[2 * n:2 * n + 2])
        start()
        finish()

    return _comm_call(body, name, shards,
                      [jax.ShapeDtypeStruct((N_SHARD,) + w.shape, w.dtype) for w in shards], 7 * n, 0)


def _handshake(peers):
    barrier = pltpu.get_barrier_semaphore()
    for peer in peers:
        pl.semaphore_signal(barrier, inc=1, device_id=peer, device_id_type=MESH)
    pl.semaphore_wait(barrier, len(peers))


def _sequenced(body, name, ins, out_shapes, n_sems, collective_id):
    return pl.kernel(
        body, out_type=list(out_shapes), mesh=plsc.ScalarSubcoreMesh(axis_name="sequencer", num_cores=1), name=name,
        scratch_types=(pltpu.SemaphoreType.DMA((n_sems,)), pltpu.SemaphoreType.DMA((n_sems,))),
        compiler_params=pltpu.CompilerParams(collective_id=collective_id))(*ins)


def _swap_halves(gs, name, collective_id):
    n = len(gs)

    def body(*refs):
        g_refs, out_refs = refs[:n], refs[n:2 * n]
        send_sems, recv_sems = refs[2 * n:]
        x, y, c, _ = _place()
        _handshake([(x, y, 1 - c)])
        cps = []
        for k in range(n):
            H = gs[k].shape[1] // 2
            cp = _remote(g_refs[k].at[:, pl.ds((1 - c) * H, H), :], out_refs[k], send_sems, recv_sems, k, (x, y, 1 - c))
            cp.start()
            cps.append(cp)
        for cp in cps:
            cp.wait()

    return _sequenced(body, name, gs, [jax.ShapeDtypeStruct((g.shape[0], g.shape[1] // 2, g.shape[2]), g.dtype)
                                       for g in gs], n, collective_id)


def _exchange_chips(ts, name, collective_id):
    n = len(ts)

    def body(*refs):
        t_refs, out_refs = refs[:n], refs[n:2 * n]
        send_sems, recv_sems = refs[2 * n:]
        x, y, c, chips = _place()
        me = 2 * x + y
        _handshake([(cx, cy, c) for cx, cy in chips])
        sent = []
        for k in range(n):
            for j, (cx, cy) in enumerate(chips):
                cp = _remote(t_refs[k].at[2 * cx + cy], out_refs[k].at[me], send_sems, recv_sems, 3 * k + j, (cx, cy, c))
                cp.start()
                sent.append(cp)
        for k in range(n):
            for j, (cx, cy) in enumerate(chips):
                slot = out_refs[k].at[2 * cx + cy]
                _remote(slot, slot, send_sems, recv_sems, 3 * k + j, (cx, cy, c)).wait_recv()
        for cp in sent:
            cp.wait_send()

    return _sequenced(body, name, ts, [jax.ShapeDtypeStruct(t.shape, t.dtype) for t in ts], 3 * n, collective_id)


def _join_halves(fs, name):
    n = len(fs)

    def body(*refs):
        out_refs = refs[n:2 * n]
        send_sems, recv_sems, _ = refs[2 * n:]
        x, y, c, _ = _place()
        sent = []
        for k in range(n):
            H = fs[k].shape[0] // 2
            here = out_refs[k].at[pl.ds(c * H, H), :]
            cp = _remote(here, here, send_sems, recv_sems, k, (x, y, 1 - c))
            cp.start()
            sent.append(cp)
        for k in range(n):
            H = fs[k].shape[0] // 2
            other = out_refs[k].at[pl.ds((1 - c) * H, H), :]
            _remote(other, other, send_sems, recv_sems, k, (x, y, 1 - c)).wait_recv()
        for cp in sent:
            cp.wait_send()

    return pl.pallas_call(
        body, name=name,
        in_specs=[_ANY] * n, out_specs=[_ANY] * n,
        out_shape=[jax.ShapeDtypeStruct(f.shape, f.dtype) for f in fs],
        input_output_aliases={k: k for k in range(n)},
        scratch_shapes=[pltpu.SemaphoreType.DMA((n,)), pltpu.SemaphoreType.DMA((n,)), pltpu.SemaphoreType.DMA((1,))],
    )(*fs)


def _gather_small(v):
    R, C = v.shape

    def body(v_ref, out_ref, send_sems, recv_sems):
        x, y, c, _ = _place()
        me = 4 * x + 2 * y + c
        flips = [(fx, fy, fc) for fx in (0, 1) for fy in (0, 1) for fc in (0, 1)][1:]
        peers = [((1 - x) if fx else x, (1 - y) if fy else y, (1 - c) if fc else c) for fx, fy, fc in flips]
        _handshake(peers)
        sent = []
        for j, peer in enumerate(peers):
            cp = _remote(v_ref, out_ref.at[me], send_sems, recv_sems, j, peer)
            cp.start()
            sent.append(cp)
        for j, peer in enumerate(peers):
            slot = out_ref.at[4 * peer[0] + 2 * peer[1] + peer[2]]
            _remote(slot, slot, send_sems, recv_sems, j, peer).wait_recv()
        for cp in sent:
            cp.wait_send()

    return _sequenced(body, "gather_small", [v], [jax.ShapeDtypeStruct((8, R, C), F32)], 7,
                      COLLECTIVE_IDS["gather_small"])[0]


def _sum_devices(x, own, name):
    S, R, C = x.shape
    tr = _row_tile(R, 2048)

    def body(s_ref, x_ref, own_ref, o_ref):
        me = s_ref[0]
        acc = None
        for k in range(S):
            term = jnp.where(me == k, own_ref[...], x_ref[k])
            acc = term if acc is None else acc + term
        o_ref[...] = acc

    x_, y_, c_ = lax.axis_index("x"), lax.axis_index("y"), lax.axis_index("c")
    me = (4 * x_ + 2 * y_ + c_).astype(jnp.int32).reshape(1)
    return pl.pallas_call(
        body, name=name,
        grid_spec=pltpu.PrefetchScalarGridSpec(
            num_scalar_prefetch=1, grid=(R // tr,),
            in_specs=[pl.BlockSpec((S, tr, C), lambda i, s: (0, i, 0)), pl.BlockSpec((tr, C), lambda i, s: (i, 0))],
            out_specs=pl.BlockSpec((tr, C), lambda i, s: (i, 0))),
        out_shape=jax.ShapeDtypeStruct((R, C), F32),
        compiler_params=_params(),
    )(me, x, own)


def _after(earlier, arrays):
    return lax.optimization_barrier((earlier, arrays))


def _reduce_swap(gs, tag, earlier):
    gs = _after(earlier, gs)[1]
    return gs, _swap_halves(gs, "reduce_swap_" + tag, COLLECTIVE_IDS["swap_" + tag])


def _reduce_exchange(gs, r1, names, tag, later_than):
    r1 = _after(later_than, r1)[1]
    ts = [_add_halves(g, r, "reduce_add_cores_" + nm) for g, r, nm in zip(gs, r1, names)]
    us = _exchange_chips(ts, "reduce_exchange_" + tag, COLLECTIVE_IDS["exchange_" + tag])
    return us, ts


def _reduce_finish(us, ts, names, tag):
    fs = [_sum_chips_into_half(u, t, "reduce_add_chips_" + nm) for u, t, nm in zip(us, ts, names)]
    return _join_halves(fs, "reduce_join_" + tag)


BIG = ["ffn1_w_gate", "ffn1_w_up", "ffn1_w_down", "w_in", "ssm_w_glu", "w_attn_branch", "w_ssm_branch",
       "w_out", "ffn2_w_gate", "ffn2_w_up", "ffn2_w_down"]
SMALL = ["ffn1_norm", "mix_norm", "gate_bias", "rel_bias_table", "ssm_a_re", "ssm_a_im", "ssm_log_dt",
         "ssm_b_re", "ssm_b_im", "ssm_c_re", "ssm_c_im", "ssm_d", "ffn2_norm", "final_norm"]
ORDER = ["ffn1_norm", "ffn1_w_gate", "ffn1_w_up", "ffn1_w_down", "mix_norm", "w_in", "gate_bias", "rel_bias_table",
         "ssm_a_re", "ssm_a_im", "ssm_log_dt", "ssm_b_re", "ssm_b_im", "ssm_c_re", "ssm_c_im", "ssm_d",
         "ssm_w_glu", "w_attn_branch", "w_ssm_branch", "w_out", "ffn2_norm", "ffn2_w_gate", "ffn2_w_up",
         "ffn2_w_down", "final_norm"]


_SMALL_TILE = 8 * LANES


def _pack_small(arrays):
    rows = []
    for a in arrays:
        flat = a.reshape(-1).astype(F32)
        rows.append(jnp.pad(flat, (0, (-flat.shape[0]) % _SMALL_TILE)).reshape(-1, LANES))
    return jnp.concatenate(rows, axis=0)


def _unpack_small(packed, shapes):
    out, r0 = [], 0
    for shp in shapes:
        n = math.prod(shp)
        rows = 8 * -(-n // _SMALL_TILE)
        out.append(packed[r0:r0 + rows].reshape(-1)[:n].reshape(shp))
        r0 += rows
    return out


def _split_cols(g):
    K, N = g.shape
    return g.reshape(K, N_SHARD, N // N_SHARD).transpose(1, 0, 2)


def _join_cols(w):
    S, K, n = w.shape
    return w.transpose(1, 0, 2).reshape(K, S * n)


COL_SHARDED = ("ssm_w_glu", "w_attn_branch", "w_ssm_branch")
TRANSPOSED = ("ffn1_w_gate", "ffn1_w_up", "ffn2_w_gate", "ffn2_w_up", "w_in")


def _shard_2d(name, arr):
    two_d = arr.reshape(arr.shape[-2:])
    return two_d.T if name in TRANSPOSED else two_d


def _shard_nd(name, two_d, shape):
    return (two_d.T if name in TRANSPOSED else two_d).reshape(shape)


class _GradSync:
    def __init__(self, weights, moms, vels):
        self.weights, self.moms, self.vels = weights, moms, vels
        self.grads, self.delta, self.new_m, self.new_v = {}, {}, {}, {}
        self.loss = None
        self._earlier = []
        self._swapped = {}
        self._exchanged = {}

    def swap(self, tag, gw, later_than=()):
        gs = []
        for n in REDUCE_GROUPS[tag]:
            g = gw[n]
            if n in COL_SHARDED:
                g = _split_cols(g)
            elif n in ("w_out", "w_in"):
                g = g.reshape(N_SHARD, g.shape[0] // N_SHARD, g.shape[1])
            gs.append(g)
        self._swapped[tag] = _reduce_swap(gs, tag, list(self._earlier) + list(later_than))
        self._earlier = self._swapped[tag][1]

    def exchange(self, tag, later_than):
        gs, r1 = self._swapped[tag]
        us, ts = _reduce_exchange(gs, r1, REDUCE_GROUPS[tag], tag, later_than)
        self._exchanged[tag] = (us, ts)
        self._earlier = us

    def small_ready(self, gs, loss_blk, later_than=()):
        _, (mine,) = _after(list(self._earlier) + list(later_than),
                            [_pack_small([gs[n] for n in SMALL] + [loss_blk[0:1, :]])])
        others = _gather_small(mine)
        self._exchanged["small"] = (others, mine)
        self._earlier = [others]

    def finish(self, tag):
        made = []
        if tag == "small":
            others, mine = self._exchanged[tag]
            shapes = [self.weights[n].shape for n in SMALL]
            total = _unpack_small(_sum_devices(others, mine, "sum_small"), shapes + [(128,)])
            self.loss = total[-1][0]
            self.grads.update(zip(SMALL, total[:-1]))
            packed = [_pack_small([src[n] for n in SMALL]) for src in (self.weights, self.grads, self.moms, self.vels)]
            for dst, res in zip((self.delta, self.new_m, self.new_v), _adamw(*packed, "adamw_small")):
                dst.update(zip(SMALL, _unpack_small(res, shapes)))
            for n in SMALL:
                made += [self.grads[n], self.delta[n], self.new_m[n], self.new_v[n]]
            return made + [self.loss]
        names = REDUCE_GROUPS[tag]
        us, ts = self._exchanged[tag]
        for n, g in zip(names, _reduce_finish(us, ts, names, tag)):
            shp = self.weights[n].shape
            d, m, v = _adamw(_shard_2d(n, self.weights[n]), g, _shard_2d(n, self.moms[n]), _shard_2d(n, self.vels[n]),
                             "adamw_" + n)
            self.grads[n], self.delta[n] = _shard_nd(n, g, shp), _shard_nd(n, d, shp)
            self.new_m[n], self.new_v[n] = _shard_nd(n, m, shp), _shard_nd(n, v, shp)
            made += [self.grads[n], self.delta[n], self.new_m[n], self.new_v[n]]
        return made

    def finish_all(self):
        self.exchange("ffn1", later_than=self.finish("ffn2"))
        for tag in ("mixer", "w_in", "small", "ffn1"):
            self.finish(tag)


def _local_step(x, target, w, later, small, sync):
    L = x.shape[0]
    row = lambda v: v.reshape(1, -1)

    a_re, a_im = small["ssm_a_re"].reshape(1, NS), small["ssm_a_im"].reshape(1, NS)
    ldt = jnp.repeat(small["ssm_log_dt"].reshape(SSM_GROUPS), SSM_STATE).reshape(1, NS)
    to_cn = lambda b: b.reshape(SSM_GROUPS, SSM_STATE, SSM_GROUP).transpose(2, 0, 1).reshape(SSM_GROUP, NS)
    c_to_cn = lambda c: c.reshape(SSM_GROUPS, SSM_GROUP, SSM_STATE).transpose(1, 0, 2).reshape(SSM_GROUP, NS)
    b_re, b_im = to_cn(small["ssm_b_re"]), to_cn(small["ssm_b_im"])
    c_re, c_im = c_to_cn(small["ssm_c_re"]), c_to_cn(small["ssm_c_im"])
    d_skip = row(small["ssm_d"])
    pw, pwr, bd, cdt = _disc_fwd(a_re, a_im, ldt, b_re, b_im, c_re, c_im)

    onehot = _bucket_onehot()
    table_t = small["rel_bias_table"].T.reshape(3, HEADS_PER_GROUP, N_BUCKETS)
    table_t = jnp.pad(table_t, ((0, 0), (0, 8 - HEADS_PER_GROUP), (0, 0)))
    bias = _bias_expand(table_t, onehot)[:, :, :HEADS_PER_GROUP].reshape(
        3, 2, HEADS_PER_GROUP, ATTN_BLOCK, 2 * ATTN_BLOCK)

    n1, nm, n2, nf = row(small["ffn1_norm"]), row(small["mix_norm"]), row(small["ffn2_norm"]), row(small["final_norm"])
    gate_bias = row(small["gate_bias"])

    x1, a1, b1, *later_full = _ffn_fwd(x, n1, w["ffn1_w_gate"], w["ffn1_w_up"], w["ffn1_w_down"], "ffn1_fwd",
                                       carried=list(later.values()))
    w = dict(w, **dict(zip(later, later_full)))
    for n in COL_SHARDED:
        w[n] = _join_cols(w[n])
    w["w_out"] = w["w_out"].reshape(D_MODEL, D_MODEL)
    w["w_in"] = w["w_in"].reshape(IN_WIDTH, D_MODEL)
    *qkv, u, gates = _mix_in_fwd(x1, nm, w["w_in"], gate_bias)
    q, k, v = qkv[0:3], qkv[3:6], qkv[6:9]
    o_g, lse_g = [], []
    for grp in range(3):
        o, lse = _attn_fwd(q[grp], k[grp], v[grp], bias[grp], f"attn_fwd_{grp}")
        o_g.append(o)
        lse_g.append(lse)
    y, s = _ssm_fwd(u, bd, cdt, d_skip, pw)
    x2, o_attn, *lse_tot = _mix_out_fwd(x1, o_g, lse_g, y, gates, w["w_attn_branch"], w["ssm_w_glu"],
                                        w["w_ssm_branch"], w["w_out"])
    x3, a2, b2 = _ffn_fwd(x2, n2, w["ffn2_w_gate"], w["ffn2_w_up"], w["ffn2_w_down"], "ffn2_fwd")
    loss_blk, dx3, d_nf = _loss_fwd_bwd(x3, nf, target)

    gw, gs = {}, {}
    gs["final_norm"] = d_nf

    dx2, da, db, sact, h, d_out, gs["ffn2_norm"] = _ffn_bwd(dx3, x2, n2, a2, b2, w["ffn2_w_gate"], w["ffn2_w_up"],
                                                            w["ffn2_w_down"], "ffn2_bwd")
    gw["ffn2_w_gate"] = _matmul_tn(da, h[None], "ffn2_dw_gate")
    gw["ffn2_w_up"] = _matmul_tn(db, h[None], "ffn2_dw_up")
    gw["ffn2_w_down"] = _matmul_tn(sact, d_out[None], "ffn2_dw_down")
    sync.swap("ffn2", gw)

    head_sum = (jnp.arange(GROUP_WIDTH)[:, None] // HEAD_DIM == jnp.arange(GROUP_WIDTH)[None, :] // HEAD_DIM).astype(F32)
    (*d_o_delta, dy, dgp, mix, dya, dys, ys2, gel, dglu, gs["gate_bias"]) = _mix_out_bwd(
        dx2, o_attn, y, gates, w["w_attn_branch"], w["ssm_w_glu"], w["w_ssm_branch"], w["w_out"], head_sum)
    sync.exchange("ffn2", later_than=[dy])
    d_o, delta = d_o_delta[0:3], d_o_delta[3:6]
    gw["w_out"] = _matmul_tn(mix[None], dx2[None], "dw_out")[0]
    gw["w_attn_branch"] = _matmul_tn(o_attn[None], dya[None], "dw_attn_branch")[0]
    gw["w_ssm_branch"] = _matmul_tn(ys2[None], dys[None], "dw_ssm_branch")[0]
    gw["ssm_w_glu"] = _matmul_tn(gel[None], dglu[None], "dw_glu")[0]

    dqs, dks, dvs, dsums = [], [], [], []
    for grp in range(3):
        dq, dk, dv, dsum = _attn_bwd(q[grp], k[grp], v[grp], d_o[grp], lse_tot[grp], delta[grp], bias[grp],
                                     f"attn_bwd_{grp}")
        dqs.append(dq)
        dks.append(dk)
        dvs.append(dv)
        dsums.append(dsum.reshape(HEADS_PER_GROUP, -1))
    dsum_all = jnp.pad(jnp.stack(dsums), ((0, 0), (0, 8 - HEADS_PER_GROUP), (0, 0)))
    d_table = _bias_reduce(dsum_all, onehot)[:, :HEADS_PER_GROUP]
    gs["rel_bias_table"] = d_table.reshape(3 * HEADS_PER_GROUP, N_BUCKETS).T

    du, gs["ssm_d"], d_bd, d_cdt, d_ab = _ssm_bwd(dy, u, s, bd, cdt, d_skip, pwr)
    sync.swap("mixer", gw, later_than=[du])
    sync.exchange("mixer", later_than=[dqs[2]])
    group_sum =(jnp.arange(NS)[:, None] // SSM_STATE == jnp.arange(128)[None, :]).astype(F32)
    d_are, d_aim, d_ldt, d_bre, d_bim, d_cre, d_cim = _disc_bwd(a_re, a_im, ldt, b_re, b_im, d_bd, d_cdt, d_ab, group_sum)
    gs["ssm_a_re"], gs["ssm_a_im"] = d_are, d_aim
    gs["ssm_log_dt"] = d_ldt[0, :SSM_GROUPS]
    from_cn = lambda t: t.reshape(SSM_GROUP, SSM_GROUPS, SSM_STATE).transpose(1, 2, 0)
    c_from_cn = lambda t: t.reshape(SSM_GROUP, SSM_GROUPS, SSM_STATE).transpose(1, 0, 2)
    gs["ssm_b_re"], gs["ssm_b_im"] = from_cn(d_bre), from_cn(d_bim)
    gs["ssm_c_re"], gs["ssm_c_im"] = c_from_cn(d_cre), c_from_cn(d_cim)

    dx1, hm, dz, gs["mix_norm"] = _mix_in_bwd(dx2, x1, nm, dqs + dks + dvs, du, dgp, w["w_in"])
    gw["w_in"] = _matmul_tn(dz[None], hm[None], "dw_in")[0]
    sync.swap("w_in", gw)

    dx0, da, db, sact, h, d_out, gs["ffn1_norm"] = _ffn_bwd(dx1, x, n1, a1, b1, w["ffn1_w_gate"], w["ffn1_w_up"],
                                                            w["ffn1_w_down"], "ffn1_bwd")
    sync.exchange("w_in", later_than=[dx0])
    gw["ffn1_w_gate"] = _matmul_tn(da, h[None], "ffn1_dw_gate")
    gw["ffn1_w_up"] = _matmul_tn(db, h[None], "ffn1_dw_up")
    sync.small_ready(gs, loss_blk, later_than=[gw["ffn1_w_up"]])
    gw["ffn1_w_down"] = _matmul_tn(sact, d_out[None], "ffn1_dw_down")
    sync.swap("ffn1", gw)
    return dx0


def kernel(x, ffn1_norm, ffn1_w_gate, ffn1_w_up, ffn1_w_down, mix_norm, w_in, gate_bias, rel_bias_table, ssm_a_re, ssm_a_im, ssm_log_dt, ssm_b_re, ssm_b_im, ssm_c_re, ssm_c_im, ssm_d, ssm_w_glu, w_attn_branch, w_ssm_branch, w_out, ffn2_norm, ffn2_w_gate, ffn2_w_up, ffn2_w_down, final_norm, loss_target, m_ffn1_norm, m_ffn1_w_gate, m_ffn1_w_up, m_ffn1_w_down, m_mix_norm, m_w_in, m_gate_bias, m_rel_bias_table, m_ssm_a_re, m_ssm_a_im, m_ssm_log_dt, m_ssm_b_re, m_ssm_b_im, m_ssm_c_re, m_ssm_c_im, m_ssm_d, m_ssm_w_glu, m_w_attn_branch, m_w_ssm_branch, m_w_out, m_ffn2_norm, m_ffn2_w_gate, m_ffn2_w_up, m_ffn2_w_down, m_final_norm, v_ffn1_norm, v_ffn1_w_gate, v_ffn1_w_up, v_ffn1_w_down, v_mix_norm, v_w_in, v_gate_bias, v_rel_bias_table, v_ssm_a_re, v_ssm_a_im, v_ssm_log_dt, v_ssm_b_re, v_ssm_b_im, v_ssm_c_re, v_ssm_c_im, v_ssm_d, v_ssm_w_glu, v_w_attn_branch, v_w_ssm_branch, v_w_out, v_ffn2_norm, v_ffn2_w_gate, v_ffn2_w_up, v_ffn2_w_down, v_final_norm):
    args = dict(locals())
    weights = {n: args[n] for n in ORDER}
    moms = {n: args["m_" + n] for n in ORDER}
    vels = {n: args["v_" + n] for n in ORDER}

    shard2d = {n: _shard_2d(n, weights[n]) for n in BIG}
    first, rest = BIG[:3], BIG[3:]
    full = dict(zip(first, _gather_weights([shard2d[n].astype(BF16) for n in first], "gather_ffn1_weights")))
    later = {n: shard2d[n].astype(BF16) for n in rest}

    small = {n: weights[n] for n in SMALL}
    sync = _GradSync(weights, moms, vels)
    grad_x = _local_step(x[0], loss_target[0], full, later, small, sync)
    sync.finish_all()
    return (sync.loss, grad_x[None], *[sync.grads[n] for n in ORDER], *[sync.delta[n] for n in ORDER],
            *[sync.new_m[n] for n in ORDER], *[sync.new_v[n] for n in ORDER])
```

```python
import functools
import math

import jax
import jax.numpy as jnp
from jax import lax
from jax.experimental import pallas as pl
from jax.experimental.pallas import tpu as pltpu
from jax.experimental.pallas import tpu_sc as plsc

F32 = jnp.float32
BF16 = jnp.bfloat16
MESH = pl.DeviceIdType.MESH

D_MODEL = 1024
D_FF = 2816
HEAD_DIM = 64
HEADS_PER_GROUP = 4
DILATIONS = (1, 4, 16)
WINDOW_STEPS = 128
ATTN_BLOCK = 128
ATTN_QB = 4
GROUP_WIDTH = HEADS_PER_GROUP * HEAD_DIM
ATTN_WIDTH = 3 * GROUP_WIDTH
N_BUCKETS = 32
MAX_DISTANCE = 2048
NEG_INF = -1e30
SSM_WIDTH = 512
SSM_GROUP = 16
SSM_GROUPS = 32
SSM_STATE = 64
NS = SSM_GROUPS * SSM_STATE
EPS = 1e-6
IN_WIDTH = 3 * ATTN_WIDTH + SSM_WIDTH + 2 * D_MODEL
Q_SCALE = HEAD_DIM ** -0.5
N_SHARD = 4
FF_SHARD = D_FF // N_SHARD
ADAM_LR, ADAM_B1, ADAM_B2, ADAM_EPS, ADAM_WD, ADAM_STEP = 0.001, 0.9, 0.999, 1e-08, 0.01, 10

LANES = 128
VMEM_LIMIT = 56 * 1024 * 1024
ROW_TILE = 512
FFN_BWD_TILE = 512
FFN_BWD_VMEM_LIMIT = 62 * 1024 * 1024
SSM_CHUNK = 256
SCAN_LANES = 512
ADAMW_BLOCK_BYTES = 2 << 20
TN_VMEM_BUDGET = 40 * 1024 * 1024
REDUCE_GROUPS = {
    "ffn2": ["ffn2_w_gate", "ffn2_w_up", "ffn2_w_down"],
    "mixer": ["w_out", "w_attn_branch", "w_ssm_branch", "ssm_w_glu"],
    "w_in": ["w_in"],
    "ffn1": ["ffn1_w_gate", "ffn1_w_up", "ffn1_w_down"],
}
COLLECTIVE_IDS = {name: i for i, name in enumerate(
    ["gather_small"] + [stage + "_" + tag for tag in REDUCE_GROUPS for stage in ("swap", "exchange")])}


def _params(vmem_limit=VMEM_LIMIT, **kw):
    return pltpu.CompilerParams(vmem_limit_bytes=vmem_limit, **kw)


def _dot(a, b):
    return jnp.dot(a, b, preferred_element_type=F32)


def _dot_nt(a, b):
    return lax.dot_general(a, b, (((1,), (1,)), ((), ())), preferred_element_type=F32)


def _dot_tn(a, b):
    return lax.dot_general(a, b, (((0,), (0,)), ((), ())), preferred_element_type=F32)


def _dot_exact(a, b):
    return jnp.dot(a, b, preferred_element_type=F32, precision=lax.Precision.HIGHEST)


def _dot_nt_exact(a, b):
    return lax.dot_general(a, b, (((1,), (1,)), ((), ())), preferred_element_type=F32,
                           precision=lax.Precision.HIGHEST)


def _rms(x):
    r = lax.rsqrt(jnp.mean(x * x, axis=-1, keepdims=True) + EPS)
    return r, x * r


def _rms_bwd(dh, g, r, xhat):
    dxh = dh * g
    return r * (dxh - xhat * jnp.mean(dxh * xhat, axis=-1, keepdims=True))


def _sigmoid(x):
    return 1.0 / (1.0 + jnp.exp(-x))


_GELU_C = math.sqrt(2.0 / math.pi)


def _gelu(x):
    return 0.5 * x * (1.0 + jnp.tanh(_GELU_C * (x + 0.044715 * x * x * x)))


def _gelu_grad(x):
    t = jnp.tanh(_GELU_C * (x + 0.044715 * x * x * x))
    return 0.5 * (1.0 + t) + 0.5 * x * (1.0 - t * t) * _GELU_C * (1.0 + 3 * 0.044715 * x * x)


def _whole():
    return pl.BlockSpec(memory_space=pltpu.VMEM)


def _row_tile(rows, cap):
    if rows <= cap:
        return rows
    return max(t for t in range(8, cap + 1, 8) if rows % t == 0)


def _rows(tm, w):
    return pl.BlockSpec((tm, w), lambda i: (i, 0))


def _acc_row(w):
    return pl.BlockSpec((1, w), lambda i: (0, 0))


def _ffn_fwd(x, g, wg, wu, wd, name, carried=()):
    L = x.shape[0]
    tm = min(ROW_TILE, L)
    n = len(carried)
    steps = L // tm

    def body(x_ref, g_ref, wg_ref, wu_ref, wd_ref, *refs):
        shard_refs, (xo_ref, a_ref, b_ref), full_refs, sems = refs[:n], refs[n:n + 3], refs[n + 3:2 * n + 3], refs[2 * n + 3:]
        if n:
            start, finish = _gather_parts([w.shape for w in carried], shard_refs, full_refs, *sems)
            pl.when(pl.program_id(0) == 0)(start)
        xv = x_ref[...]
        r, xhat = _rms(xv)
        h = (xhat * g_ref[...]).astype(BF16)
        acc = jnp.zeros((tm, D_MODEL), F32)
        for j in range(N_SHARD):
            a = _dot_nt(h, wg_ref[j])
            b = _dot_nt(h, wu_ref[j])
            a_ref[j] = a.astype(BF16)
            b_ref[j] = b.astype(BF16)
            s = (a * _sigmoid(a) * b).astype(BF16)
            acc = acc + _dot(s, wd_ref[j])
        xo_ref[...] = xv + 0.5 * acc
        if n:
            pl.when(pl.program_id(0) == steps - 1)(finish)

    act = pl.BlockSpec((N_SHARD, tm, FF_SHARD), lambda i: (0, i, 0))
    return pl.pallas_call(
        body, name=name, grid=(steps,),
        in_specs=[_rows(tm, D_MODEL), _whole(), _whole(), _whole(), _whole()] + [_ANY] * n,
        out_specs=[_rows(tm, D_MODEL), act, act] + [_ANY] * n,
        out_shape=[jax.ShapeDtypeStruct((L, D_MODEL), F32),
                   jax.ShapeDtypeStruct((N_SHARD, L, FF_SHARD), BF16),
                   jax.ShapeDtypeStruct((N_SHARD, L, FF_SHARD), BF16)]
        + [jax.ShapeDtypeStruct((N_SHARD,) + w.shape, w.dtype) for w in carried],
        scratch_shapes=[pltpu.SemaphoreType.DMA((7 * n,)), pltpu.SemaphoreType.DMA((7 * n,))] if n else [],
        compiler_params=_params(),
    )(x, g, wg, wu, wd, *carried)


def _ffn_bwd(dxo, x, g, a, b, wg, wu, wd, name):
    L = x.shape[0]
    tm = min(FFN_BWD_TILE, L)

    def body(dxo_ref, x_ref, g_ref, a_ref, b_ref, wg_ref, wu_ref, wd_ref,
             dxi_ref, da_ref, db_ref, s_ref, h_ref, do_ref, dg_ref):
        i = pl.program_id(0)
        xv = x_ref[...]
        gv = g_ref[...]
        r, xhat = _rms(xv)
        h_ref[...] = (xhat * gv).astype(BF16)
        dxo_v = dxo_ref[...]
        d_out = (0.5 * dxo_v).astype(BF16)
        do_ref[...] = d_out
        dh = jnp.zeros((tm, D_MODEL), F32)
        for j in range(N_SHARD):
            av = a_ref[j].astype(F32)
            bv = b_ref[j].astype(F32)
            sg = _sigmoid(av)
            sl = av * sg
            ds = _dot_nt(d_out, wd_ref[j])
            dbv = (ds * sl).astype(BF16)
            dav = (ds * bv * (sg * (1.0 + av * (1.0 - sg)))).astype(BF16)
            da_ref[j] = dav
            db_ref[j] = dbv
            s_ref[j] = (sl * bv).astype(BF16)
            dh = dh + _dot(dav, wg_ref[j]) + _dot(dbv, wu_ref[j])

        @pl.when(i == 0)
        def _():
            dg_ref[...] = jnp.zeros_like(dg_ref)

        dg_ref[...] += jnp.sum(dh * xhat, axis=0, keepdims=True)
        dxi_ref[...] = dxo_v + _rms_bwd(dh, gv, r, xhat)

    act = pl.BlockSpec((N_SHARD, tm, FF_SHARD), lambda i: (0, i, 0), pipeline_mode=pl.Buffered(1))
    act_shape = jax.ShapeDtypeStruct((N_SHARD, L, FF_SHARD), BF16)
    return pl.pallas_call(
        body, name=name, grid=(L // tm,),
        in_specs=[_rows(tm, D_MODEL), _rows(tm, D_MODEL), _whole(), act, act, _whole(), _whole(), _whole()],
        out_specs=[_rows(tm, D_MODEL), act, act, act, _rows(tm, D_MODEL), _rows(tm, D_MODEL), _acc_row(D_MODEL)],
        out_shape=[jax.ShapeDtypeStruct((L, D_MODEL), F32), act_shape, act_shape, act_shape,
                   jax.ShapeDtypeStruct((L, D_MODEL), BF16), jax.ShapeDtypeStruct((L, D_MODEL), BF16),
                   jax.ShapeDtypeStruct((1, D_MODEL), F32)],
        compiler_params=_params(vmem_limit=FFN_BWD_VMEM_LIMIT),
    )(dxo, x, g, a, b, wg, wu, wd)


def _matmul_tn(a, b, name):
    ja, L, K = a.shape
    jb, _, N = b.shape
    J = max(ja, jb)
    splits = [s for s in (1, 2, 4, 8) if s == 1 or N % (s * LANES) == 0]
    nsplit = next((s for s in splits if 2 * K * (N // s) * 4 <= TN_VMEM_BUDGET // 2), splits[-1])
    nc = N // nsplit
    left = TN_VMEM_BUDGET - 2 * K * nc * 4
    row_bytes = 2 * (K * a.dtype.itemsize + nc * b.dtype.itemsize)
    tm = next((t for t in (2048, 1024, 512, 256) if L % t == 0 and t * row_bytes <= left), min(128, L))

    def body(a_ref, b_ref, o_ref):
        @pl.when(pl.program_id(2) == 0)
        def _():
            o_ref[...] = jnp.zeros_like(o_ref)

        o_ref[...] += _dot_tn(a_ref[...].astype(BF16), b_ref[...].astype(BF16))

    return pl.pallas_call(
        body, name=name, grid=(J, nsplit, L // tm),
        in_specs=[pl.BlockSpec((None, tm, K), (lambda j, s, i: (j, i, 0)) if ja > 1 else (lambda j, s, i: (0, i, 0))),
                  pl.BlockSpec((None, tm, nc), (lambda j, s, i: (j, i, s)) if jb > 1 else (lambda j, s, i: (0, i, s)))],
        out_specs=pl.BlockSpec((None, K, nc), lambda j, s, i: (j, 0, s)),
        out_shape=jax.ShapeDtypeStruct((J, K, N), F32),
        compiler_params=_params(),
    )(a, b)


def _loss_fwd_bwd(x, g, target):
    L = x.shape[0]
    tm = min(ROW_TILE, L)

    def body(x_ref, g_ref, t_ref, loss_ref, dx_ref, dg_ref):
        i = pl.program_id(0)
        xv = x_ref[...]
        gv = g_ref[...]
        r, xhat = _rms(xv)
        err = xhat * gv - t_ref[...]
        part = 0.5 * jnp.sum(jnp.sum(err * err, axis=1, keepdims=True) * (1.0 / D_MODEL), axis=0, keepdims=True)
        dy = err * (1.0 / D_MODEL)

        @pl.when(i == 0)
        def _():
            dg_ref[...] = jnp.zeros_like(dg_ref)
            loss_ref[...] = jnp.zeros_like(loss_ref)

        loss_ref[...] += jnp.broadcast_to(part, loss_ref.shape)
        dg_ref[...] += jnp.sum(dy * xhat, axis=0, keepdims=True)
        dx_ref[...] = _rms_bwd(dy, gv, r, xhat)

    return pl.pallas_call(
        body, name="loss_fwd_bwd", grid=(L // tm,),
        in_specs=[_rows(tm, D_MODEL), _whole(), _rows(tm, D_MODEL)],
        out_specs=[pl.BlockSpec((8, 128), lambda i: (0, 0)), _rows(tm, D_MODEL), _acc_row(D_MODEL)],
        out_shape=[jax.ShapeDtypeStruct((8, 128), F32), jax.ShapeDtypeStruct((L, D_MODEL), F32),
                   jax.ShapeDtypeStruct((1, D_MODEL), F32)],
        compiler_params=_params(),
    )(x, g, target)


_C_K = ATTN_WIDTH
_C_V = 2 * ATTN_WIDTH
_C_U = 3 * ATTN_WIDTH
_C_G = _C_U + SSM_WIDTH


def _residue_spec(d, tm):
    return pl.BlockSpec((d, tm // d, GROUP_WIDTH), lambda i: (0, i, 0))


def _residue_shape(d, L, dtype):
    return jax.ShapeDtypeStruct((d, L // d, GROUP_WIDTH), dtype)


def _residue_scratch(tm):
    return pltpu.VMEM((GROUP_WIDTH // LANES, tm, LANES), F32)


def _to_residues(val, out_ref, scr, d):
    if d == 1:
        out_ref[0] = val.astype(out_ref.dtype)
        return
    tm = val.shape[0]
    for half in range(GROUP_WIDTH // LANES):
        cols = slice(half * LANES, (half + 1) * LANES)
        scr[half] = val[:, cols]
        for r in range(d):
            out_ref[r, :, cols] = scr[half, pl.ds(r, tm // d, stride=d), :].astype(out_ref.dtype)


def _from_residues(ref, scr, d):
    if d == 1:
        return ref[0].astype(F32)
    rows = ref.shape[1]
    for half in range(GROUP_WIDTH // LANES):
        cols = slice(half * LANES, (half + 1) * LANES)
        for r in range(d):
            scr[half, pl.ds(r, rows, stride=d), :] = ref[r, :, cols].astype(F32)
    return jnp.concatenate([scr[half] for half in range(GROUP_WIDTH // LANES)], axis=1)


def _mix_in_fwd(x, g, w_in, gate_bias):
    L = x.shape[0]
    tm = min(ROW_TILE, L)

    def body(x_ref, g_ref, w_ref, gb_ref, *refs):
        qkv_refs, (u_ref, gate_ref, scr) = refs[:9], refs[9:]
        r, xhat = _rms(x_ref[...])
        h = (xhat * g_ref[...]).astype(BF16)
        for part, (c0, scale) in enumerate(((0, Q_SCALE), (_C_K, 1.0), (_C_V, 1.0))):
            z = _dot_nt(h, w_ref[c0:c0 + ATTN_WIDTH, :]) * scale
            for grp, d in enumerate(DILATIONS):
                _to_residues(z[:, grp * GROUP_WIDTH:(grp + 1) * GROUP_WIDTH], qkv_refs[3 * part + grp], scr, d)
        u_ref[...] = _dot_nt(h, w_ref[_C_U:_C_G, :])
        gate_ref[...] = _sigmoid(_dot_nt(h, w_ref[_C_G:IN_WIDTH, :]) + gb_ref[...])

    return pl.pallas_call(
        body, name="mix_in_fwd", grid=(L // tm,),
        in_specs=[_rows(tm, D_MODEL), _whole(), _whole(), _whole()],
        out_specs=[_residue_spec(d, tm) for d in DILATIONS] * 3 + [_rows(tm, SSM_WIDTH), _rows(tm, 2 * D_MODEL)],
        out_shape=[_residue_shape(d, L, BF16) for d in DILATIONS] * 3
        + [jax.ShapeDtypeStruct((L, SSM_WIDTH), F32), jax.ShapeDtypeStruct((L, 2 * D_MODEL), F32)],
        scratch_shapes=[_residue_scratch(tm)],
        compiler_params=_params(),
    )(x, g, w_in, gate_bias)


def _mix_in_bwd(dx2, x, g, dqkv, du, dgp, w_in):
    L = x.shape[0]
    tm = min(ROW_TILE, L)

    def body(dx2_ref, x_ref, g_ref, *refs):
        piece_refs = refs[:9]
        du_ref, dgp_ref, w_ref, dx1_ref, h_ref, dz_ref, dg_ref, scr = refs[9:]
        i = pl.program_id(0)
        gv = g_ref[...]
        r, xhat = _rms(x_ref[...])
        h_ref[...] = (xhat * gv).astype(BF16)
        for part in range(3):
            for grp, d in enumerate(DILATIONS):
                c0 = part * ATTN_WIDTH + grp * GROUP_WIDTH
                dz_ref[:, c0:c0 + GROUP_WIDTH] = _from_residues(piece_refs[3 * part + grp], scr, d).astype(BF16)
        dz_ref[:, _C_U:_C_G] = du_ref[...].astype(BF16)
        dz_ref[:, _C_G:IN_WIDTH] = dgp_ref[...]
        dh = _dot(dz_ref[...], w_ref[...])

        @pl.when(i == 0)
        def _():
            dg_ref[...] = jnp.zeros_like(dg_ref)

        dg_ref[...] += jnp.sum(dh * xhat, axis=0, keepdims=True)
        dx1_ref[...] = dx2_ref[...] + _rms_bwd(dh, gv, r, xhat)

    return pl.pallas_call(
        body, name="mix_in_bwd", grid=(L // tm,),
        in_specs=[_rows(tm, D_MODEL), _rows(tm, D_MODEL), _whole()] + [_residue_spec(d, tm) for d in DILATIONS] * 3
        + [_rows(tm, SSM_WIDTH), _rows(tm, 2 * D_MODEL), _whole()],
        out_specs=[_rows(tm, D_MODEL), _rows(tm, D_MODEL), _rows(tm, IN_WIDTH), _acc_row(D_MODEL)],
        out_shape=[jax.ShapeDtypeStruct((L, D_MODEL), F32), jax.ShapeDtypeStruct((L, D_MODEL), BF16),
                   jax.ShapeDtypeStruct((L, IN_WIDTH), BF16), jax.ShapeDtypeStruct((1, D_MODEL), F32)],
        scratch_shapes=[_residue_scratch(tm)],
        compiler_params=_params(),
    )(dx2, x, g, *dqkv, du, dgp, w_in)


def _bucket_onehot():
    qi = jnp.arange(ATTN_BLOCK)[:, None]
    kj = jnp.arange(2 * ATTN_BLOCK)[None, :]
    steps = jnp.maximum(qi + ATTN_BLOCK - kj, 0)
    max_exact = N_BUCKETS // 2
    out = []
    for d in DILATIONS:
        dist = steps * d
        df = jnp.maximum(dist, 1).astype(F32)
        large = max_exact + (jnp.log(df / max_exact) / math.log(MAX_DISTANCE / max_exact)
                             * (N_BUCKETS - max_exact)).astype(jnp.int32)
        large = jnp.minimum(large, N_BUCKETS - 1)
        bucket = jnp.where(dist < max_exact, dist, large).reshape(-1)
        out.append((bucket[None, :] == jnp.arange(N_BUCKETS)[:, None]).astype(F32))
    return jnp.stack(out)


def _bias_expand(table_t, onehot):
    n = onehot.shape[-1]

    def body(t_ref, oh_ref, o_ref):
        bias = _dot_exact(t_ref[...], oh_ref[...])
        col = lax.broadcasted_iota(jnp.int32, (8, n), 1)
        qi = col // (2 * ATTN_BLOCK)
        kj = col - qi * (2 * ATTN_BLOCK)
        steps = qi + ATTN_BLOCK - kj
        band = (steps >= 0) & (steps <= WINDOW_STEPS)
        o_ref[0] = jnp.where(band & (kj >= ATTN_BLOCK), bias, NEG_INF)
        o_ref[1] = jnp.where(band, bias, NEG_INF)

    return pl.pallas_call(
        body, name="bias_expand", grid=(3,),
        in_specs=[pl.BlockSpec((None, 8, N_BUCKETS), lambda g: (g, 0, 0)),
                  pl.BlockSpec((None, N_BUCKETS, n), lambda g: (g, 0, 0))],
        out_specs=pl.BlockSpec((None, 2, 8, n), lambda g: (g, 0, 0, 0)),
        out_shape=jax.ShapeDtypeStruct((3, 2, 8, n), F32),
        compiler_params=_params(),
    )(table_t, onehot)


def _bias_reduce(dsum, onehot):
    n = onehot.shape[-1]

    def body(d_ref, oh_ref, o_ref):
        o_ref[...] = _dot_nt_exact(d_ref[...], oh_ref[...])

    return pl.pallas_call(
        body, name="bias_reduce", grid=(3,),
        in_specs=[pl.BlockSpec((None, 8, n), lambda g: (g, 0, 0)),
                  pl.BlockSpec((None, N_BUCKETS, n), lambda g: (g, 0, 0))],
        out_specs=pl.BlockSpec((None, 8, N_BUCKETS), lambda g: (g, 0, 0)),
        out_shape=jax.ShapeDtypeStruct((3, 8, N_BUCKETS), F32),
        compiler_params=_params(),
    )(dsum, onehot)


def _head_of_col(rows):
    return lax.broadcasted_iota(jnp.int32, (rows, GROUP_WIDTH), 1) // HEAD_DIM


def _attn_specs(qb):
    rows = qb * ATTN_BLOCK
    cur = pl.BlockSpec((None, rows, GROUP_WIDTH), lambda r, n: (r, n, 0))
    prev = pl.BlockSpec((None, ATTN_BLOCK, GROUP_WIDTH), lambda r, n: (r, jnp.maximum(n * qb - 1, 0), 0))
    bias = pl.BlockSpec((2, HEADS_PER_GROUP, ATTN_BLOCK, 2 * ATTN_BLOCK), lambda r, n: (0, 0, 0, 0))
    return cur, prev, bias


def _attn_fwd(q, k, v, bias, name):
    d, M, _ = q.shape
    nb = M // ATTN_BLOCK
    qb = min(ATTN_QB, nb)

    def body(q_ref, kp_ref, kc_ref, vp_ref, vc_ref, bias_ref, o_ref, lse_ref):
        n = pl.program_id(1)
        q_head = _head_of_col(ATTN_BLOCK)
        kv_head = _head_of_col(2 * ATTN_BLOCK)
        kwin = jnp.concatenate([kp_ref[...], kc_ref[...]], axis=0)
        vwin = jnp.concatenate([vp_ref[...], vc_ref[...]], axis=0)
        for b in range(qb):
            rows = slice(b * ATTN_BLOCK, (b + 1) * ATTN_BLOCK)
            window = slice(b * ATTN_BLOCK, (b + 2) * ATTN_BLOCK)
            variant = jnp.minimum(n, 1) if b == 0 else 1
            qv = q_ref[rows, :]
            kk = kwin[window]
            vv = vwin[window]
            o_acc = jnp.zeros((ATTN_BLOCK, GROUP_WIDTH), F32)
            lse_acc = jnp.zeros((ATTN_BLOCK, GROUP_WIDTH), F32)
            for hh in range(HEADS_PER_GROUP):
                hm = q_head == hh
                qh = jnp.where(hm, qv, jnp.zeros_like(qv))
                logits = _dot_nt(qh, kk) + bias_ref[variant, hh]
                m = jnp.max(logits, axis=1, keepdims=True)
                p = jnp.exp(logits - m)
                vh = jnp.where(kv_head == hh, vv, jnp.ones_like(vv))
                pv = _dot(p.astype(BF16), vh)
                c_sum = ((hh + 1) % HEADS_PER_GROUP) * HEAD_DIM
                den = pv[:, c_sum:c_sum + 1]
                o_acc = jnp.where(hm, pv * (1.0 / den), o_acc)
                lse_acc = jnp.where(hm, m + jnp.log(den), lse_acc)
            o_ref[rows, :] = o_acc
            lse_ref[rows, :] = lse_acc

    cur, prev, full = _attn_specs(qb)
    return pl.pallas_call(
        body, name=name, grid=(d, nb // qb),
        in_specs=[cur, prev, cur, prev, cur, full],
        out_specs=[cur, cur],
        out_shape=[jax.ShapeDtypeStruct((d, M, GROUP_WIDTH), F32)] * 2,
        compiler_params=_params(),
    )(q, k, k, v, v, bias)


def _attn_bwd(q, k, v, do, lse, delta, bias, name):
    d, M, _ = q.shape
    nb = M // ATTN_BLOCK
    qb = min(ATTN_QB, nb)
    ns = nb // qb
    rows_q = qb * ATTN_BLOCK
    last = slice(rows_q - ATTN_BLOCK, rows_q)

    def body(q_ref, kp_ref, kc_ref, vp_ref, vc_ref, do_ref, lse_ref, dl_ref, bias_ref,
             dq_ref, dk_ref, dv_ref, dsum_ref, pk_ref, pv_ref, wk_ref, wv_ref):
        r = pl.program_id(0)
        n = pl.program_id(1)

        @pl.when((r == 0) & (n == 0))
        def _():
            dsum_ref[...] = jnp.zeros_like(dsum_ref)

        @pl.when(n == 0)
        def _():
            pk_ref[...] = jnp.zeros_like(pk_ref)
            pv_ref[...] = jnp.zeros_like(pv_ref)

        @pl.when(n < ns)
        def _():
            q_head = _head_of_col(ATTN_BLOCK)
            kwin = jnp.concatenate([kp_ref[...], kc_ref[...]], axis=0)
            vwin = jnp.concatenate([vp_ref[...], vc_ref[...]], axis=0)
            wk_ref[...] = jnp.zeros_like(wk_ref)
            wv_ref[...] = jnp.zeros_like(wv_ref)
            for b in range(qb):
                rows = slice(b * ATTN_BLOCK, (b + 1) * ATTN_BLOCK)
                window = slice(b * ATTN_BLOCK, (b + 2) * ATTN_BLOCK)
                variant = jnp.minimum(n, 1) if b == 0 else 1
                qv = q_ref[rows, :]
                dov = do_ref[rows, :]
                kk = kwin[window]
                vv = vwin[window]
                dq_acc = jnp.zeros((ATTN_BLOCK, GROUP_WIDTH), F32)
                dkk = jnp.zeros((2 * ATTN_BLOCK, GROUP_WIDTH), F32)
                dvv = jnp.zeros((2 * ATTN_BLOCK, GROUP_WIDTH), F32)
                for hh in range(HEADS_PER_GROUP):
                    hm = q_head == hh
                    c0 = hh * HEAD_DIM
                    qh = jnp.where(hm, qv, jnp.zeros_like(qv))
                    doh = jnp.where(hm, dov, jnp.zeros_like(dov))
                    logits = _dot_nt(qh, kk) + bias_ref[variant, hh]
                    p = jnp.exp(logits - lse_ref[rows, c0:c0 + 1])
                    dp = _dot_nt(doh, vv)
                    ds = p * (dp - dl_ref[rows, c0:c0 + 1])
                    dsum_ref[hh] += ds
                    ds16 = ds.astype(BF16)
                    dq_acc = jnp.where(hm, _dot(ds16, kk), dq_acc)
                    dkk = dkk + _dot_tn(ds16, qh)
                    dvv = dvv + _dot_tn(p.astype(BF16), doh)
                dq_ref[rows, :] = (dq_acc * Q_SCALE).astype(BF16)
                wk_ref[window, :] += dkk
                wv_ref[window, :] += dvv
            for out_ref, part_ref, win_ref in ((dk_ref, pk_ref, wk_ref), (dv_ref, pv_ref, wv_ref)):
                if qb > 1:
                    out_ref[0:rows_q - ATTN_BLOCK, :] = part_ref[0:rows_q - ATTN_BLOCK, :].astype(BF16)
                out_ref[last, :] = (part_ref[last, :] + win_ref[0:ATTN_BLOCK, :]).astype(BF16)
                part_ref[...] = win_ref[ATTN_BLOCK:, :]

        @pl.when(n == ns)
        def _():
            dk_ref[...] = pk_ref[...].astype(BF16)
            dv_ref[...] = pv_ref[...].astype(BF16)

    def clamp(n):
        return jnp.minimum(n, ns - 1)

    cur = pl.BlockSpec((None, rows_q, GROUP_WIDTH), lambda r, n: (r, clamp(n), 0))
    prev = pl.BlockSpec((None, ATTN_BLOCK, GROUP_WIDTH), lambda r, n: (r, jnp.maximum(clamp(n) * qb - 1, 0), 0))
    lag = pl.BlockSpec((None, rows_q, GROUP_WIDTH), lambda r, n: (r, jnp.maximum(n - 1, 0), 0))
    full = pl.BlockSpec((2, HEADS_PER_GROUP, ATTN_BLOCK, 2 * ATTN_BLOCK), lambda r, n: (0, 0, 0, 0))
    acc = pl.BlockSpec((HEADS_PER_GROUP, ATTN_BLOCK, 2 * ATTN_BLOCK), lambda r, n: (0, 0, 0))
    return pl.pallas_call(
        body, name=name, grid=(d, ns + 1),
        in_specs=[cur, prev, cur, prev, cur, cur, cur, cur, full],
        out_specs=[cur, lag, lag, acc],
        out_shape=[jax.ShapeDtypeStruct((d, M, GROUP_WIDTH), BF16)] * 3
        + [jax.ShapeDtypeStruct((HEADS_PER_GROUP, ATTN_BLOCK, 2 * ATTN_BLOCK), F32)],
        scratch_shapes=[pltpu.VMEM((rows_q, GROUP_WIDTH), F32), pltpu.VMEM((rows_q, GROUP_WIDTH), F32),
                        pltpu.VMEM((rows_q + ATTN_BLOCK, GROUP_WIDTH), F32),
                        pltpu.VMEM((rows_q + ATTN_BLOCK, GROUP_WIDTH), F32)],
        compiler_params=_params(),
    )(q, k, k, v, v, do, lse, delta, bias)


def _disc_math(a_re, a_im, ldt, b_re, b_im):
    dt = jnp.exp(ldt)
    mag = jnp.exp(a_re * dt)
    ab_re = mag * jnp.cos(a_im * dt)
    ab_im = mag * jnp.sin(a_im * dt)
    den = a_re * a_re + a_im * a_im
    xr = ab_re - 1.0
    coef_re = (xr * a_re + ab_im * a_im) / den
    coef_im = (ab_im * a_re - xr * a_im) / den
    return ab_re, ab_im, coef_re * b_re - coef_im * b_im, coef_re * b_im + coef_im * b_re


def _block_diag_mask():
    row_g = lax.broadcasted_iota(jnp.int32, (SSM_WIDTH, 2 * NS), 0) // SSM_GROUP
    col = lax.broadcasted_iota(jnp.int32, (SSM_WIDTH, 2 * NS), 1)
    col_g = jnp.where(col >= NS, col - NS, col) // SSM_STATE
    return row_g == col_g


def _disc_fwd(a_re, a_im, ldt, b_re, b_im, c_re, c_im):
    def body(are_ref, aim_ref, ldt_ref, bre_ref, bim_ref, cre_ref, cim_ref, pw_ref, pwr_ref, bd_ref, cdt_ref):
        ab_re, ab_im, bb_re, bb_im = _disc_math(are_ref[...], aim_ref[...], ldt_ref[...], bre_ref[...], bim_ref[...])
        row = lax.broadcasted_iota(jnp.int32, (8, NS), 0)
        pr, pi = ab_re, ab_im
        t_re = jnp.zeros((8, NS), F32)
        t_im = jnp.zeros((8, NS), F32)
        u_re = jnp.zeros((8, NS), F32)
        u_im = jnp.zeros((8, NS), F32)
        for j in range(8):
            t_re = jnp.where(row == j, pr, t_re)
            t_im = jnp.where(row == j, pi, t_im)
            u_re = jnp.where(row == 7 - j, pr, u_re)
            u_im = jnp.where(row == 7 - j, pi, u_im)
            pr, pi = pr * ab_re - pi * ab_im, pr * ab_im + pi * ab_re
        pw_ref[0] = t_re
        pw_ref[1] = t_im
        pwr_ref[0] = u_re
        pwr_ref[1] = u_im
        mask = _block_diag_mask()
        zero = jnp.zeros((SSM_WIDTH, 2 * NS), F32)
        bfull = jnp.concatenate([jnp.concatenate([bb_re] * SSM_GROUPS, axis=0),
                                 jnp.concatenate([bb_im] * SSM_GROUPS, axis=0)], axis=1)
        bd_ref[...] = jnp.where(mask, bfull, zero).astype(BF16)
        cfull = jnp.concatenate([jnp.concatenate([cre_ref[...]] * SSM_GROUPS, axis=0),
                                 jnp.concatenate([-cim_ref[...]] * SSM_GROUPS, axis=0)], axis=1)
        cdt_ref[...] = jnp.where(mask, cfull, zero).astype(BF16)

    return pl.pallas_call(
        body, name="s5_disc_fwd",
        in_specs=[_whole()] * 7, out_specs=[_whole()] * 4,
        out_shape=[jax.ShapeDtypeStruct((2, 8, NS), F32), jax.ShapeDtypeStruct((2, 8, NS), F32),
                   jax.ShapeDtypeStruct((SSM_WIDTH, 2 * NS), BF16), jax.ShapeDtypeStruct((SSM_WIDTH, 2 * NS), BF16)],
        compiler_params=_params(),
    )(a_re, a_im, ldt, b_re, b_im, c_re, c_im)


def _disc_bwd(a_re, a_im, ldt, b_re, b_im, d_bd, d_cdt, d_ab, group_sum):
    def body(are_ref, aim_ref, ldt_ref, bre_ref, bim_ref, dbd_ref, dcdt_ref, dab_ref, gs_ref,
             dare_ref, daim_ref, dldt_ref, dbre_ref, dbim_ref, dcre_ref, dcim_ref):
        col = lax.broadcasted_iota(jnp.int32, (SSM_GROUP, 2 * NS), 1)
        col_g = jnp.where(col >= NS, col - NS, col) // SSM_STATE
        acc_b = jnp.zeros((SSM_GROUP, 2 * NS), F32)
        acc_c = jnp.zeros((SSM_GROUP, 2 * NS), F32)
        for g in range(SSM_GROUPS):
            rows = slice(g * SSM_GROUP, (g + 1) * SSM_GROUP)
            acc_b = acc_b + jnp.where(col_g == g, dbd_ref[rows, :], 0.0)
            acc_c = acc_c + jnp.where(col_g == g, dcdt_ref[rows, :], 0.0)
        dcre_ref[...] = acc_c[:, :NS]
        dcim_ref[...] = -acc_c[:, NS:]
        dab_re = jnp.sum(dab_ref[0], axis=0, keepdims=True)
        dab_im = jnp.sum(dab_ref[1], axis=0, keepdims=True)
        _, vjp = jax.vjp(_disc_math, are_ref[...], aim_ref[...], ldt_ref[...], bre_ref[...], bim_ref[...])
        d_are, d_aim, d_ldt, d_bre, d_bim = vjp((dab_re, dab_im, acc_b[:, :NS], acc_b[:, NS:]))
        dare_ref[...] = d_are
        daim_ref[...] = d_aim
        dbre_ref[...] = d_bre
        dbim_ref[...] = d_bim
        dldt_ref[...] = _dot_exact(jnp.broadcast_to(d_ldt, (8, NS)), gs_ref[...])

    vec = jax.ShapeDtypeStruct((1, NS), F32)
    mat = jax.ShapeDtypeStruct((SSM_GROUP, NS), F32)
    return pl.pallas_call(
        body, name="s5_disc_bwd",
        in_specs=[_whole()] * 9, out_specs=[_whole()] * 7,
        out_shape=[vec, vec, jax.ShapeDtypeStruct((8, 128), F32), mat, mat, mat, mat],
        compiler_params=_params(),
    )(a_re, a_im, ldt, b_re, b_im, d_bd, d_cdt, d_ab, group_sum)


def _scan_blocks(buf, pw_ref, carry_ref, n_blocks, reverse):
    row = lax.broadcasted_iota(jnp.int32, (8, SCAN_LANES), 0)
    for lc in range(NS // SCAN_LANES):
        re_cols = pl.ds(lc * SCAN_LANES, SCAN_LANES)
        im_cols = pl.ds(NS + lc * SCAN_LANES, SCAN_LANES)
        pr = pw_ref[0, :, re_cols]
        pi = pw_ref[1, :, re_cols]
        if reverse:
            pi = -pi
            base = [(7, 1), (6, 2), (4, 4)]
            coef = [(jnp.where(row < 8 - k, pr[j:j + 1], 0.0), jnp.where(row < 8 - k, pi[j:j + 1], 0.0), 8 - k)
                    for j, k in base]
        else:
            base = [(0, 1), (1, 2), (3, 4)]
            coef = [(jnp.where(row >= k, pr[j:j + 1], 0.0), jnp.where(row >= k, pi[j:j + 1], 0.0), k)
                    for j, k in base]

        def step(i, carry, pr=pr, pi=pi, coef=coef, re_cols=re_cols, im_cols=im_cols):
            cr, ci = carry
            blk = (n_blocks - 1 - i) if reverse else i
            rows = pl.ds(pl.multiple_of(blk * 8, 8), 8)
            xr = buf[rows, re_cols]
            xi = buf[rows, im_cols]
            for kr, ki, shift in coef:
                sr = pltpu.roll(xr, shift, 0)
                si = pltpu.roll(xi, shift, 0)
                xr, xi = xr + kr * sr - ki * si, xi + kr * si + ki * sr
            xr, xi = xr + pr * cr - pi * ci, xi + pr * ci + pi * cr
            buf[rows, re_cols] = xr
            buf[rows, im_cols] = xi
            edge = slice(0, 1) if reverse else slice(7, 8)
            return xr[edge], xi[edge]

        cr, ci = lax.fori_loop(0, n_blocks, step, (carry_ref[0:1, re_cols], carry_ref[0:1, im_cols]))
        carry_ref[0:1, re_cols] = cr
        carry_ref[0:1, im_cols] = ci


_SUPER_GROUPS = 16
_SUPER_BLOCKS = [
    (slice(k * _SUPER_GROUPS * SSM_GROUP, (k + 1) * _SUPER_GROUPS * SSM_GROUP),
     [slice(half + k * _SUPER_GROUPS * SSM_STATE, half + (k + 1) * _SUPER_GROUPS * SSM_STATE) for half in (0, NS)])
    for k in range(SSM_GROUPS // _SUPER_GROUPS)]


def _ssm_fwd(u, bd, cdt, d_skip, pw):
    L = u.shape[0]
    tc = min(SSM_CHUNK, L)

    def body(u_ref, bd_ref, cdt_ref, dsk_ref, pw_ref, y_ref, s_ref, carry_ref):
        @pl.when(pl.program_id(0) == 0)
        def _():
            carry_ref[...] = jnp.zeros_like(carry_ref)

        uv = u_ref[...]
        u16 = uv.astype(BF16)
        for ch, states in _SUPER_BLOCKS:
            for st in states:
                s_ref[:, st] = _dot(u16[:, ch], bd_ref[ch, st])
        _scan_blocks(s_ref, pw_ref, carry_ref, tc // 8, reverse=False)
        for ch, states in _SUPER_BLOCKS:
            y_ref[:, ch] = (sum(_dot_nt(s_ref[:, st].astype(BF16), cdt_ref[ch, st]) for st in states)
                            + dsk_ref[:, ch] * uv[:, ch])

    return pl.pallas_call(
        body, name="s5_fwd", grid=(L // tc,),
        in_specs=[_rows(tc, SSM_WIDTH), _whole(), _whole(), _whole(), _whole()],
        out_specs=[_rows(tc, SSM_WIDTH), _rows(tc, 2 * NS)],
        out_shape=[jax.ShapeDtypeStruct((L, SSM_WIDTH), F32), jax.ShapeDtypeStruct((L, 2 * NS), F32)],
        scratch_shapes=[pltpu.VMEM((8, 2 * NS), F32)],
        compiler_params=_params(),
    )(u, bd, cdt, d_skip, pw)


def _ssm_bwd(dy, u, s, bd, cdt, d_skip, pwr):
    L = u.shape[0]
    tc = min(SSM_CHUNK, L)
    nc = L // tc
    blocks = tc // 8

    def body(dy_ref, u_ref, s_ref, sprev_ref, bd_ref, cdt_ref, dsk_ref, pwr_ref,
             du_ref, ddsk_ref, dbd_ref, dcdt_ref, dab_ref, g_ref, sx_ref, carry_ref):
        i = pl.program_id(0)

        @pl.when(i == 0)
        def _():
            carry_ref[...] = jnp.zeros_like(carry_ref)
            ddsk_ref[...] = jnp.zeros_like(ddsk_ref)
            dbd_ref[...] = jnp.zeros_like(dbd_ref)
            dcdt_ref[...] = jnp.zeros_like(dcdt_ref)
            dab_ref[...] = jnp.zeros_like(dab_ref)

        dyv = dy_ref[...]
        uv = u_ref[...]
        dy16 = dyv.astype(BF16)
        u16 = uv.astype(BF16)
        for ch, states in _SUPER_BLOCKS:
            for st in states:
                g_ref[:, st] = _dot(dy16[:, ch], cdt_ref[ch, st])
        _scan_blocks(g_ref, pwr_ref, carry_ref, blocks, reverse=True)
        ddsk_ref[...] += jnp.sum(dyv * uv, axis=0, keepdims=True)
        for ch, states in _SUPER_BLOCKS:
            du = dsk_ref[:, ch] * dyv[:, ch]
            for st in states:
                g16 = g_ref[:, st].astype(BF16)
                du = du + _dot_nt(g16, bd_ref[ch, st])
                dbd_ref[ch, st] += _dot_tn(u16[:, ch], g16)
                dcdt_ref[ch, st] += _dot_tn(dy16[:, ch], s_ref[:, st].astype(BF16))
            du_ref[:, ch] = du

        sx_ref[pl.ds(8, tc), :] = s_ref[...]
        sx_ref[pl.ds(0, 8), :] = jnp.where(i == nc - 1, 0.0, sprev_ref[...])
        row = lax.broadcasted_iota(jnp.int32, (8, SCAN_LANES), 0)
        for lc in range(NS // SCAN_LANES):
            re_cols = pl.ds(lc * SCAN_LANES, SCAN_LANES)
            im_cols = pl.ds(NS + lc * SCAN_LANES, SCAN_LANES)

            def step(b, acc, re_cols=re_cols, im_cols=im_cols):
                ar, ai = acc
                off = pl.multiple_of(b * 8, 8)
                gr = g_ref[pl.ds(off, 8), re_cols]
                gi = g_ref[pl.ds(off, 8), im_cols]
                before = pl.ds(off, 8)
                here = pl.ds(off + 8, 8)
                sr = jnp.where(row == 0, sx_ref[before, re_cols][7:8], pltpu.roll(sx_ref[here, re_cols], 1, 0))
                si = jnp.where(row == 0, sx_ref[before, im_cols][7:8], pltpu.roll(sx_ref[here, im_cols], 1, 0))
                return ar + gr * sr + gi * si, ai + gi * sr - gr * si

            zero = jnp.zeros((8, SCAN_LANES), F32)
            ar, ai = lax.fori_loop(0, blocks, step, (zero, zero))
            dab_ref[0, :, re_cols] += ar
            dab_ref[1, :, re_cols] += ai

    rev = lambda i: (nc - 1 - i, 0)
    sprev = pl.BlockSpec((8, 2 * NS), lambda i: (jnp.maximum((nc - 1 - i) * blocks - 1, 0), 0))
    return pl.pallas_call(
        body, name="s5_bwd", grid=(nc,),
        in_specs=[pl.BlockSpec((tc, SSM_WIDTH), rev), pl.BlockSpec((tc, SSM_WIDTH), rev),
                  pl.BlockSpec((tc, 2 * NS), rev), sprev, _whole(), _whole(), _whole(), _whole()],
        out_specs=[pl.BlockSpec((tc, SSM_WIDTH), rev), _whole(), _whole(), _whole(), _whole()],
        out_shape=[jax.ShapeDtypeStruct((L, SSM_WIDTH), F32), jax.ShapeDtypeStruct((1, SSM_WIDTH), F32),
                   jax.ShapeDtypeStruct((SSM_WIDTH, 2 * NS), F32), jax.ShapeDtypeStruct((SSM_WIDTH, 2 * NS), F32),
                   jax.ShapeDtypeStruct((2, 8, NS), F32)],
        scratch_shapes=[pltpu.VMEM((tc, 2 * NS), F32), pltpu.VMEM((tc + 8, 2 * NS), F32), pltpu.VMEM((8, 2 * NS), F32)],
        compiler_params=_params(),
    )(dy, u, s, s, bd, cdt, d_skip, pwr)


def _branches(o_attn, y, gates, w_ab, w_glu, w_sb):
    ya = _dot(o_attn.astype(BF16), w_ab[...])
    gel = _gelu(y)
    glu = _dot(gel.astype(BF16), w_glu[...])
    p = glu[:, :SSM_WIDTH]
    sg = _sigmoid(glu[:, SSM_WIDTH:])
    ys2 = p * sg
    ysb = _dot(ys2.astype(BF16), w_sb[...])
    ga = gates[:, :D_MODEL]
    gs = gates[:, D_MODEL:]
    return ya, gel, p, sg, ys2, ysb, ga, gs


def _mix_out_fwd(x1, o_g, lse_g, y, gates, w_ab, w_glu, w_sb, w_out):
    L = x1.shape[0]
    tm = min(ROW_TILE, L)

    def body(x_ref, o0, o1, o2, l0, l1, l2, y_ref, gate_ref, wab_ref, wglu_ref, wsb_ref, wout_ref,
             x2_ref, oat_ref, lse0, lse1, lse2, scr):
        la, lb, lc = (_from_residues(ref, scr, d) for ref, d in zip((l0, l1, l2), DILATIONS))
        m = jnp.maximum(jnp.maximum(la, lb), lc)
        ea, eb, ec = jnp.exp(la - m), jnp.exp(lb - m), jnp.exp(lc - m)
        tot = ea + eb + ec
        oa, ob, oc = (_from_residues(ref, scr, d) for ref, d in zip((o0, o1, o2), DILATIONS))
        o_attn = (ea * oa + eb * ob + ec * oc) / tot
        oat_ref[...] = o_attn
        lse = m + jnp.log(tot)
        for ref, d in zip((lse0, lse1, lse2), DILATIONS):
            _to_residues(lse, ref, scr, d)
        ya, _, _, _, _, ysb, ga, gs = _branches(o_attn, y_ref[...], gate_ref[...], wab_ref, wglu_ref, wsb_ref)
        mix = ga * ya + gs * ysb
        x2_ref[...] = x_ref[...] + _dot(mix.astype(BF16), wout_ref[...])

    res = [_residue_spec(d, tm) for d in DILATIONS]
    return pl.pallas_call(
        body, name="mix_out_fwd", grid=(L // tm,),
        in_specs=[_rows(tm, D_MODEL)] + res * 2 + [_rows(tm, SSM_WIDTH), _rows(tm, 2 * D_MODEL)] + [_whole()] * 4,
        out_specs=[_rows(tm, D_MODEL), _rows(tm, GROUP_WIDTH)] + res,
        out_shape=[jax.ShapeDtypeStruct((L, D_MODEL), F32), jax.ShapeDtypeStruct((L, GROUP_WIDTH), F32)]
        + [_residue_shape(d, L, F32) for d in DILATIONS],
        scratch_shapes=[_residue_scratch(tm)],
        compiler_params=_params(),
    )(x1, *o_g, *lse_g, y, gates, w_ab, w_glu, w_sb, w_out)


def _mix_out_bwd(dx2, o_attn, y, gates, w_ab, w_glu, w_sb, w_out, head_sum):
    L = dx2.shape[0]
    tm = min(ROW_TILE, L)

    def body(dx_ref, oat_ref, y_ref, gate_ref, wab_ref, wglu_ref, wsb_ref, wout_ref, hs_ref,
             do0, do1, do2, dl0, dl1, dl2, dy_ref, dgp_ref, mix_ref, dya_ref, dys_ref, ys2_ref, gel_ref, dglu_ref,
             dgb_ref, scr):
        i = pl.program_id(0)
        o_attn = oat_ref[...]
        yv = y_ref[...]
        ya, gel, p, sg, ys2, ysb, ga, gs = _branches(o_attn, yv, gate_ref[...], wab_ref, wglu_ref, wsb_ref)
        mix_ref[...] = (ga * ya + gs * ysb).astype(BF16)
        ys2_ref[...] = ys2.astype(BF16)
        gel_ref[...] = gel.astype(BF16)
        dmix = _dot_nt(dx_ref[...].astype(BF16), wout_ref[...])
        dgp = jnp.concatenate([dmix * ya * ga * (1.0 - ga), dmix * ysb * gs * (1.0 - gs)], axis=1)
        dgp_ref[...] = dgp.astype(BF16)

        @pl.when(i == 0)
        def _():
            dgb_ref[...] = jnp.zeros_like(dgb_ref)

        dgb_ref[...] += jnp.sum(dgp, axis=0, keepdims=True)
        dya = (dmix * ga).astype(BF16)
        dys = (dmix * gs).astype(BF16)
        dya_ref[...] = dya
        dys_ref[...] = dys
        d_o = _dot_nt(dya, wab_ref[...])
        delta = _dot_exact(d_o * o_attn, hs_ref[...])
        for do_ref, dl_ref, d in zip((do0, do1, do2), (dl0, dl1, dl2), DILATIONS):
            _to_residues(d_o, do_ref, scr, d)
            _to_residues(delta, dl_ref, scr, d)
        dys2 = _dot_nt(dys, wsb_ref[...])
        dglu = jnp.concatenate([dys2 * sg, dys2 * p * sg * (1.0 - sg)], axis=1).astype(BF16)
        dglu_ref[...] = dglu
        dy_ref[...] = _dot_nt(dglu, wglu_ref[...]) * _gelu_grad(yv)

    grp = _rows(tm, GROUP_WIDTH)
    wide = _rows(tm, D_MODEL)
    half = _rows(tm, SSM_WIDTH)
    res = [_residue_spec(d, tm) for d in DILATIONS]
    sds = jax.ShapeDtypeStruct
    return pl.pallas_call(
        body, name="mix_out_bwd", grid=(L // tm,),
        in_specs=[wide, grp, half, _rows(tm, 2 * D_MODEL)] + [_whole()] * 5,
        out_specs=res + res + [half, _rows(tm, 2 * D_MODEL), wide, wide, wide, half, half, wide, _acc_row(2 * D_MODEL)],
        out_shape=[_residue_shape(d, L, BF16) for d in DILATIONS] + [_residue_shape(d, L, F32) for d in DILATIONS]
        + [sds((L, SSM_WIDTH), F32),
           sds((L, 2 * D_MODEL), BF16), sds((L, D_MODEL), BF16), sds((L, D_MODEL), BF16),
           sds((L, D_MODEL), BF16), sds((L, SSM_WIDTH), BF16), sds((L, SSM_WIDTH), BF16),
           sds((L, D_MODEL), BF16), sds((1, 2 * D_MODEL), F32)],
        scratch_shapes=[_residue_scratch(tm)],
        compiler_params=_params(),
    )(dx2, o_attn, y, gates, w_ab, w_glu, w_sb, w_out, head_sum)


def _adamw(w, g, m, v, name):
    R, C = w.shape
    tr = _row_tile(R, max(8, ADAMW_BLOCK_BYTES // (4 * C)))

    def body(w_ref, g_ref, m_ref, v_ref, d_ref, mo_ref, vo_ref):
        gv = g_ref[...]
        mn = ADAM_B1 * m_ref[...] + (1.0 - ADAM_B1) * gv
        vn = ADAM_B2 * v_ref[...] + (1.0 - ADAM_B2) * (gv * gv)
        m_hat = mn / (1.0 - ADAM_B1 ** ADAM_STEP)
        v_hat = vn / (1.0 - ADAM_B2 ** ADAM_STEP)
        d_ref[...] = -ADAM_LR * (m_hat / (jnp.sqrt(v_hat) + ADAM_EPS) + ADAM_WD * w_ref[...])
        mo_ref[...] = mn
        vo_ref[...] = vn

    blk = pl.BlockSpec((tr, C), lambda i: (i, 0))
    return pl.pallas_call(
        body, name=name, grid=(R // tr,),
        in_specs=[blk] * 4, out_specs=[blk] * 3,
        out_shape=[jax.ShapeDtypeStruct((R, C), F32)] * 3,
        compiler_params=_params(),
    )(w, g, m, v)


def _sum_chips_into_half(u, t, name):
    S, H, C = u.shape
    tr = _row_tile(H, 512)
    hb = H // tr

    def body(s_ref, t_ref, a_ref, b_ref, c_ref, o_ref):
        me = s_ref[1]
        others = (a_ref[...], b_ref[...], c_ref[...])
        acc = None
        for chip in range(S):
            below = others[min(chip, S - 2)]
            above = others[max(chip - 1, 0)]
            term = jnp.where(me == chip, t_ref[...], jnp.where(me > chip, below, above)).astype(F32)
            acc = term if acc is None else acc + term
        o_ref[...] = acc

    x, y, c = lax.axis_index("x"), lax.axis_index("y"), lax.axis_index("c")
    me = 2 * x + y
    scalars = jnp.stack([c, me] + [j + (j >= me).astype(jnp.int32) for j in range(S - 1)]).astype(jnp.int32)
    blk = (None, tr, C)
    return pl.pallas_call(
        body, name=name,
        grid_spec=pltpu.PrefetchScalarGridSpec(
            num_scalar_prefetch=1, grid=(hb,),
            in_specs=[pl.BlockSpec(blk, lambda i, s: (s[1], i, 0))]
            + [pl.BlockSpec(blk, functools.partial(lambda j, i, s: (s[2 + j], i, 0), j)) for j in range(S - 1)],
            out_specs=pl.BlockSpec((tr, C), lambda i, s: (s[0] * hb + i, 0))),
        out_shape=jax.ShapeDtypeStruct((2 * H, C), F32),
        compiler_params=_params(),
    )(scalars, t, u, u, u)


def _add_halves(g, r1, name):
    S, R, C = g.shape
    H = R // 2
    tr = _row_tile(H, 512)
    hb = H // tr

    def body(c_ref, g_ref, r_ref, o_ref):
        o_ref[...] = (g_ref[...] + r_ref[...]).astype(BF16)

    core = lax.axis_index("c").astype(jnp.int32).reshape(1)
    return pl.pallas_call(
        body, name=name,
        grid_spec=pltpu.PrefetchScalarGridSpec(
            num_scalar_prefetch=1, grid=(S, hb),
            in_specs=[pl.BlockSpec((None, tr, C), lambda j, i, c_ref: (j, c_ref[0] * hb + i, 0)),
                      pl.BlockSpec((None, tr, C), lambda j, i, c_ref: (j, i, 0))],
            out_specs=pl.BlockSpec((None, tr, C), lambda j, i, c_ref: (j, i, 0))),
        out_shape=jax.ShapeDtypeStruct((S, H, C), BF16),
        compiler_params=_params(),
    )(core, g, r1)


_ANY = pl.BlockSpec(memory_space=pl.ANY)


def _place():
    x, y, c = lax.axis_index("x"), lax.axis_index("y"), lax.axis_index("c")
    chips = [(1 - x, y), (x, 1 - y), (1 - x, 1 - y)]
    return x, y, c, chips


def _comm_call(body, name, ins, out_shapes, n_remote, n_local):
    return pl.pallas_call(
        body, name=name,
        in_specs=[_ANY] * len(ins), out_specs=[_ANY] * len(out_shapes), out_shape=out_shapes,
        scratch_shapes=[pltpu.SemaphoreType.DMA((n_remote,)), pltpu.SemaphoreType.DMA((n_remote,)),
                        pltpu.SemaphoreType.DMA((max(n_local, 1),))],
    )(*ins)


def _remote(src, dst, send_sems, recv_sems, k, device):
    return pltpu.make_async_remote_copy(src_ref=src, dst_ref=dst, send_sem=send_sems.at[k], recv_sem=recv_sems.at[k],
                                        device_id=device, device_id_type=MESH)


def _gather_parts(shapes, w_refs, out_refs, send_sems, recv_sems):
    n = len(shapes)
    x, y, c, chips = _place()
    me = 2 * x + y
    sibling = (x, y, 1 - c)

    def half(k, chip_idx, core):
        H = shapes[k][0] // 2
        return out_refs[k].at[chip_idx, pl.ds(core * H, H), :]

    mine = [_remote(w_refs[k], out_refs[k].at[me], send_sems, recv_sems, 6 * n + k, sibling) for k in range(n)]
    first = []
    for k in range(n):
        H = shapes[k][0] // 2
        for j, (cx, cy) in enumerate(chips):
            first.append(_remote(w_refs[k].at[pl.ds(c * H, H), :], half(k, me, c), send_sems, recv_sems,
                                 3 * k + j, (cx, cy, c)))

    def start():
        for cp in mine + first:
            cp.start()

    def finish():
        passed = []
        for k in range(n):
            for j, (cx, cy) in enumerate(chips):
                landed = half(k, 2 * cx + cy, c)
                _remote(landed, landed, send_sems, recv_sems, 3 * k + j, (cx, cy, c)).wait_recv()
                fwd = _remote(landed, landed, send_sems, recv_sems, 3 * n + 3 * k + j, sibling)
                fwd.start()
                passed.append(fwd)
        for k in range(n):
            for j, (cx, cy) in enumerate(chips):
                other = half(k, 2 * cx + cy, 1 - c)
                _remote(other, other, send_sems, recv_sems, 3 * n + 3 * k + j, sibling).wait_recv()
        for cp in mine:
            cp.wait_recv()
        for cp in first + passed + mine:
            cp.wait_send()

    return start, finish


def _gather_weights(shards, name):
    n = len(shards)

    def body(*refs):
        start, finish = _gather_parts([w.shape for w in shards], refs[:n], refs[n:2 * n], *refs[2 * n:2 * n + 2])
        start()
        finish()

    return _comm_call(body, name, shards,
                      [jax.ShapeDtypeStruct((N_SHARD,) + w.shape, w.dtype) for w in shards], 7 * n, 0)


def _handshake(peers):
    barrier = pltpu.get_barrier_semaphore()
    for peer in peers:
        pl.semaphore_signal(barrier, inc=1, device_id=peer, device_id_type=MESH)
    pl.semaphore_wait(barrier, len(peers))


def _sequenced(body, name, ins, out_shapes, n_sems, collective_id):
    return pl.kernel(
        body, out_type=list(out_shapes), mesh=plsc.ScalarSubcoreMesh(axis_name="sequencer", num_cores=1), name=name,
        scratch_types=(pltpu.SemaphoreType.DMA((n_sems,)), pltpu.SemaphoreType.DMA((n_sems,))),
        compiler_params=pltpu.CompilerParams(collective_id=collective_id))(*ins)


def _swap_halves(gs, name, collective_id):
    n = len(gs)

    def body(*refs):
        g_refs, out_refs = refs[:n], refs[n:2 * n]
        send_sems, recv_sems = refs[2 * n:]
        x, y, c, _ = _place()
        _handshake([(x, y, 1 - c)])
        cps = []
        for k in range(n):
            H = gs[k].shape[1] // 2
            cp = _remote(g_refs[k].at[:, pl.ds((1 - c) * H, H), :], out_refs[k], send_sems, recv_sems, k, (x, y, 1 - c))
            cp.start()
            cps.append(cp)
        for cp in cps:
            cp.wait()

    return _sequenced(body, name, gs, [jax.ShapeDtypeStruct((g.shape[0], g.shape[1] // 2, g.shape[2]), g.dtype)
                                       for g in gs], n, collective_id)


def _exchange_chips(ts, name, collective_id):
    n = len(ts)

    def body(*refs):
        t_refs, out_refs = refs[:n], refs[n:2 * n]
        send_sems, recv_sems = refs[2 * n:]
        x, y, c, chips = _place()
        me = 2 * x + y
        _handshake([(cx, cy, c) for cx, cy in chips])
        sent = []
        for k in range(n):
            for j, (cx, cy) in enumerate(chips):
                cp = _remote(t_refs[k].at[2 * cx + cy], out_refs[k].at[me], send_sems, recv_sems, 3 * k + j, (cx, cy, c))
                cp.start()
                sent.append(cp)
        for k in range(n):
            for j, (cx, cy) in enumerate(chips):
                slot = out_refs[k].at[2 * cx + cy]
                _remote(slot, slot, send_sems, recv_sems, 3 * k + j, (cx, cy, c)).wait_recv()
        for cp in sent:
            cp.wait_send()

    return _sequenced(body, name, ts, [jax.ShapeDtypeStruct(t.shape, t.dtype) for t in ts], 3 * n, collective_id)


def _join_halves(fs, name):
    n = len(fs)

    def body(*refs):
        out_refs = refs[n:2 * n]
        send_sems, recv_sems, _ = refs[2 * n:]
        x, y, c, _ = _place()
        sent = []
        for k in range(n):
            H = fs[k].shape[0] // 2
            here = out_refs[k].at[pl.ds(c * H, H), :]
            cp = _remote(here, here, send_sems, recv_sems, k, (x, y, 1 - c))
            cp.start()
            sent.append(cp)
        for k in range(n):
            H = fs[k].shape[0] // 2
            other = out_refs[k].at[pl.ds((1 - c) * H, H), :]
            _remote(other, other, send_sems, recv_sems, k, (x, y, 1 - c)).wait_recv()
        for cp in sent:
            cp.wait_send()

    return pl.pallas_call(
        body, name=name,
        in_specs=[_ANY] * n, out_specs=[_ANY] * n,
        out_shape=[jax.ShapeDtypeStruct(f.shape, f.dtype) for f in fs],
        input_output_aliases={k: k for k in range(n)},
        scratch_shapes=[pltpu.SemaphoreType.DMA((n,)), pltpu.SemaphoreType.DMA((n,)), pltpu.SemaphoreType.DMA((1,))],
    )(*fs)


def _gather_small(v):
    R, C = v.shape

    def body(v_ref, out_ref, send_sems, recv_sems):
        x, y, c, _ = _place()
        me = 4 * x + 2 * y + c
        flips = [(fx, fy, fc) for fx in (0, 1) for fy in (0, 1) for fc in (0, 1)][1:]
        peers = [((1 - x) if fx else x, (1 - y) if fy else y, (1 - c) if fc else c) for fx, fy, fc in flips]
        _handshake(peers)
        sent = []
        for j, peer in enumerate(peers):
            cp = _remote(v_ref, out_ref.at[me], send_sems, recv_sems, j, peer)
            cp.start()
            sent.append(cp)
        for j, peer in enumerate(peers):
            slot = out_ref.at[4 * peer[0] + 2 * peer[1] + peer[2]]
            _remote(slot, slot, send_sems, recv_sems, j, peer).wait_recv()
        for cp in sent:
            cp.wait_send()

    return _sequenced(body, "gather_small", [v], [jax.ShapeDtypeStruct((8, R, C), F32)], 7,
                      COLLECTIVE_IDS["gather_small"])[0]


def _sum_devices(x, own, name):
    S, R, C = x.shape
    tr = _row_tile(R, 2048)

    def body(s_ref, x_ref, own_ref, o_ref):
        me = s_ref[0]
        acc = None
        for k in range(S):
            term = jnp.where(me == k, own_ref[...], x_ref[k])
            acc = term if acc is None else acc + term
        o_ref[...] = acc

    x_, y_, c_ = lax.axis_index("x"), lax.axis_index("y"), lax.axis_index("c")
    me = (4 * x_ + 2 * y_ + c_).astype(jnp.int32).reshape(1)
    return pl.pallas_call(
        body, name=name,
        grid_spec=pltpu.PrefetchScalarGridSpec(
            num_scalar_prefetch=1, grid=(R // tr,),
            in_specs=[pl.BlockSpec((S, tr, C), lambda i, s: (0, i, 0)), pl.BlockSpec((tr, C), lambda i, s: (i, 0))],
            out_specs=pl.BlockSpec((tr, C), lambda i, s: (i, 0))),
        out_shape=jax.ShapeDtypeStruct((R, C), F32),
        compiler_params=_params(),
    )(me, x, own)


def _after(earlier, arrays):
    return lax.optimization_barrier((earlier, arrays))


def _reduce_swap(gs, tag, earlier):
    gs = _after(earlier, gs)[1]
    return gs, _swap_halves(gs, "reduce_swap_" + tag, COLLECTIVE_IDS["swap_" + tag])


def _reduce_exchange(gs, r1, names, tag, later_than):
    r1 = _after(later_than, r1)[1]
    ts = [_add_halves(g, r, "reduce_add_cores_" + nm) for g, r, nm in zip(gs, r1, names)]
    us = _exchange_chips(ts, "reduce_exchange_" + tag, COLLECTIVE_IDS["exchange_" + tag])
    return us, ts


def _reduce_finish(us, ts, names, tag):
    fs = [_sum_chips_into_half(u, t, "reduce_add_chips_" + nm) for u, t, nm in zip(us, ts, names)]
    return _join_halves(fs, "reduce_join_" + tag)


BIG = ["ffn1_w_gate", "ffn1_w_up", "ffn1_w_down", "w_in", "ssm_w_glu", "w_attn_branch", "w_ssm_branch",
       "w_out", "ffn2_w_gate", "ffn2_w_up", "ffn2_w_down"]
SMALL = ["ffn1_norm", "mix_norm", "gate_bias", "rel_bias_table", "ssm_a_re", "ssm_a_im", "ssm_log_dt",
         "ssm_b_re", "ssm_b_im", "ssm_c_re", "ssm_c_im", "ssm_d", "ffn2_norm", "final_norm"]
ORDER = ["ffn1_norm", "ffn1_w_gate", "ffn1_w_up", "ffn1_w_down", "mix_norm", "w_in", "gate_bias", "rel_bias_table",
         "ssm_a_re", "ssm_a_im", "ssm_log_dt", "ssm_b_re", "ssm_b_im", "ssm_c_re", "ssm_c_im", "ssm_d",
         "ssm_w_glu", "w_attn_branch", "w_ssm_branch", "w_out", "ffn2_norm", "ffn2_w_gate", "ffn2_w_up",
         "ffn2_w_down", "final_norm"]


_SMALL_TILE = 8 * LANES


def _pack_small(arrays):
    rows = []
    for a in arrays:
        flat = a.reshape(-1).astype(F32)
        rows.append(jnp.pad(flat, (0, (-flat.shape[0]) % _SMALL_TILE)).reshape(-1, LANES))
    return jnp.concatenate(rows, axis=0)


def _unpack_small(packed, shapes):
    out, r0 = [], 0
    for shp in shapes:
        n = math.prod(shp)
        rows = 8 * -(-n // _SMALL_TILE)
        out.append(packed[r0:r0 + rows].reshape(-1)[:n].reshape(shp))
        r0 += rows
    return out


def _split_cols(g):
    K, N = g.shape
    return g.reshape(K, N_SHARD, N // N_SHARD).transpose(1, 0, 2)


def _join_cols(w):
    S, K, n = w.shape
    return w.transpose(1, 0, 2).reshape(K, S * n)


COL_SHARDED = ("ssm_w_glu", "w_attn_branch", "w_ssm_branch")
TRANSPOSED = ("ffn1_w_gate", "ffn1_w_up", "ffn2_w_gate", "ffn2_w_up", "w_in")


def _shard_2d(name, arr):
    two_d = arr.reshape(arr.shape[-2:])
    return two_d.T if name in TRANSPOSED else two_d


def _shard_nd(name, two_d, shape):
    return (two_d.T if name in TRANSPOSED else two_d).reshape(shape)


class _GradSync:
    def __init__(self, weights, moms, vels):
        self.weights, self.moms, self.vels = weights, moms, vels
        self.grads, self.delta, self.new_m, self.new_v = {}, {}, {}, {}
        self.loss = None
        self._earlier = []
        self._swapped = {}
        self._exchanged = {}

    def swap(self, tag, gw, later_than=()):
        gs = []
        for n in REDUCE_GROUPS[tag]:
            g = gw[n]
            if n in COL_SHARDED:
                g = _split_cols(g)
            elif n in ("w_out", "w_in"):
                g = g.reshape(N_SHARD, g.shape[0] // N_SHARD, g.shape[1])
            gs.append(g)
        self._swapped[tag] = _reduce_swap(gs, tag, list(self._earlier) + list(later_than))
        self._earlier = self._swapped[tag][1]

    def exchange(self, tag, later_than):
        gs, r1 = self._swapped[tag]
        us, ts = _reduce_exchange(gs, r1, REDUCE_GROUPS[tag], tag, later_than)
        self._exchanged[tag] = (us, ts)
        self._earlier = us

    def small_ready(self, gs, loss_blk, later_than=()):
        _, (mine,) = _after(list(self._earlier) + list(later_than),
                            [_pack_small([gs[n] for n in SMALL] + [loss_blk[0:1, :]])])
        others = _gather_small(mine)
        self._exchanged["small"] = (others, mine)
        self._earlier = [others]

    def finish(self, tag):
        made = []
        if tag == "small":
            others, mine = self._exchanged[tag]
            shapes = [self.weights[n].shape for n in SMALL]
            total = _unpack_small(_sum_devices(others, mine, "sum_small"), shapes + [(128,)])
            self.loss = total[-1][0]
            self.grads.update(zip(SMALL, total[:-1]))
            packed = [_pack_small([src[n] for n in SMALL]) for src in (self.weights, self.grads, self.moms, self.vels)]
            for dst, res in zip((self.delta, self.new_m, self.new_v), _adamw(*packed, "adamw_small")):
                dst.update(zip(SMALL, _unpack_small(res, shapes)))
            for n in SMALL:
                made += [self.grads[n], self.delta[n], self.new_m[n], self.new_v[n]]
            return made + [self.loss]
        names = REDUCE_GROUPS[tag]
        us, ts = self._exchanged[tag]
        for n, g in zip(names, _reduce_finish(us, ts, names, tag)):
            shp = self.weights[n].shape
            d, m, v = _adamw(_shard_2d(n, self.weights[n]), g, _shard_2d(n, self.moms[n]), _shard_2d(n, self.vels[n]),
                             "adamw_" + n)
            self.grads[n], self.delta[n] = _shard_nd(n, g, shp), _shard_nd(n, d, shp)
            self.new_m[n], self.new_v[n] = _shard_nd(n, m, shp), _shard_nd(n, v, shp)
            made += [self.grads[n], self.delta[n], self.new_m[n], self.new_v[n]]
        return made

    def finish_all(self):
        self.exchange("ffn1", later_than=self.finish("ffn2"))
        for tag in ("mixer", "w_in", "small", "ffn1"):
            self.finish(tag)


def _local_step(x, target, w, later, small, sync):
    L = x.shape[0]
    row = lambda v: v.reshape(1, -1)

    a_re, a_im = small["ssm_a_re"].reshape(1, NS), small["ssm_a_im"].reshape(1, NS)
    ldt = jnp.repeat(small["ssm_log_dt"].reshape(SSM_GROUPS), SSM_STATE).reshape(1, NS)
    to_cn = lambda b: b.reshape(SSM_GROUPS, SSM_STATE, SSM_GROUP).transpose(2, 0, 1).reshape(SSM_GROUP, NS)
    c_to_cn = lambda c: c.reshape(SSM_GROUPS, SSM_GROUP, SSM_STATE).transpose(1, 0, 2).reshape(SSM_GROUP, NS)
    b_re, b_im = to_cn(small["ssm_b_re"]), to_cn(small["ssm_b_im"])
    c_re, c_im = c_to_cn(small["ssm_c_re"]), c_to_cn(small["ssm_c_im"])
    d_skip = row(small["ssm_d"])
    pw, pwr, bd, cdt = _disc_fwd(a_re, a_im, ldt, b_re, b_im, c_re, c_im)

    onehot = _bucket_onehot()
    table_t = small["rel_bias_table"].T.reshape(3, HEADS_PER_GROUP, N_BUCKETS)
    table_t = jnp.pad(table_t, ((0, 0), (0, 8 - HEADS_PER_GROUP), (0, 0)))
    bias = _bias_expand(table_t, onehot)[:, :, :HEADS_PER_GROUP].reshape(
        3, 2, HEADS_PER_GROUP, ATTN_BLOCK, 2 * ATTN_BLOCK)

    n1, nm, n2, nf = row(small["ffn1_norm"]), row(small["mix_norm"]), row(small["ffn2_norm"]), row(small["final_norm"])
    gate_bias = row(small["gate_bias"])

    x1, a1, b1, *later_full = _ffn_fwd(x, n1, w["ffn1_w_gate"], w["ffn1_w_up"], w["ffn1_w_down"], "ffn1_fwd",
                                       carried=list(later.values()))
    w = dict(w, **dict(zip(later, later_full)))
    for n in COL_SHARDED:
        w[n] = _join_cols(w[n])
    w["w_out"] = w["w_out"].reshape(D_MODEL, D_MODEL)
    w["w_in"] = w["w_in"].reshape(IN_WIDTH, D_MODEL)
    *qkv, u, gates = _mix_in_fwd(x1, nm, w["w_in"], gate_bias)
    q, k, v = qkv[0:3], qkv[3:6], qkv[6:9]
    o_g, lse_g = [], []
    for grp in range(3):
        o, lse = _attn_fwd(q[grp], k[grp], v[grp], bias[grp], f"attn_fwd_{grp}")
        o_g.append(o)
        lse_g.append(lse)
    y, s = _ssm_fwd(u, bd, cdt, d_skip, pw)
    x2, o_attn, *lse_tot = _mix_out_fwd(x1, o_g, lse_g, y, gates, w["w_attn_branch"], w["ssm_w_glu"],
                                        w["w_ssm_branch"], w["w_out"])
    x3, a2, b2 = _ffn_fwd(x2, n2, w["ffn2_w_gate"], w["ffn2_w_up"], w["ffn2_w_down"], "ffn2_fwd")
    loss_blk, dx3, d_nf = _loss_fwd_bwd(x3, nf, target)

    gw, gs = {}, {}
    gs["final_norm"] = d_nf

    dx2, da, db, sact, h, d_out, gs["ffn2_norm"] = _ffn_bwd(dx3, x2, n2, a2, b2, w["ffn2_w_gate"], w["ffn2_w_up"],
                                                            w["ffn2_w_down"], "ffn2_bwd")
    gw["ffn2_w_gate"] = _matmul_tn(da, h[None], "ffn2_dw_gate")
    gw["ffn2_w_up"] = _matmul_tn(db, h[None], "ffn2_dw_up")
    gw["ffn2_w_down"] = _matmul_tn(sact, d_out[None], "ffn2_dw_down")
    sync.swap("ffn2", gw)

    head_sum = (jnp.arange(GROUP_WIDTH)[:, None] // HEAD_DIM == jnp.arange(GROUP_WIDTH)[None, :] // HEAD_DIM).astype(F32)
    (*d_o_delta, dy, dgp, mix, dya, dys, ys2, gel, dglu, gs["gate_bias"]) = _mix_out_bwd(
        dx2, o_attn, y, gates, w["w_attn_branch"], w["ssm_w_glu"], w["w_ssm_branch"], w["w_out"], head_sum)
    sync.exchange("ffn2", later_than=[dy])
    d_o, delta = d_o_delta[0:3], d_o_delta[3:6]
    gw["w_out"] = _matmul_tn(mix[None], dx2[None], "dw_out")[0]
    gw["w_attn_branch"] = _matmul_tn(o_attn[None], dya[None], "dw_attn_branch")[0]
    gw["w_ssm_branch"] = _matmul_tn(ys2[None], dys[None], "dw_ssm_branch")[0]
    gw["ssm_w_glu"] = _matmul_tn(gel[None], dglu[None], "dw_glu")[0]

    dqs, dks, dvs, dsums = [], [], [], []
    for grp in range(3):
        dq, dk, dv, dsum = _attn_bwd(q[grp], k[grp], v[grp], d_o[grp], lse_tot[grp], delta[grp], bias[grp],
                                     f"attn_bwd_{grp}")
        dqs.append(dq)
        dks.append(dk)
        dvs.append(dv)
        dsums.append(dsum.reshape(HEADS_PER_GROUP, -1))
    dsum_all = jnp.pad(jnp.stack(dsums), ((0, 0), (0, 8 - HEADS_PER_GROUP), (0, 0)))
    d_table = _bias_reduce(dsum_all, onehot)[:, :HEADS_PER_GROUP]
    gs["rel_bias_table"] = d_table.reshape(3 * HEADS_PER_GROUP, N_BUCKETS).T

    du, gs["ssm_d"], d_bd, d_cdt, d_ab = _ssm_bwd(dy, u, s, bd, cdt, d_skip, pwr)
    sync.swap("mixer", gw, later_than=[du])
    sync.exchange("mixer", later_than=[dqs[2]])
    group_sum =(jnp.arange(NS)[:, None] // SSM_STATE == jnp.arange(128)[None, :]).astype(F32)
    d_are, d_aim, d_ldt, d_bre, d_bim, d_cre, d_cim = _disc_bwd(a_re, a_im, ldt, b_re, b_im, d_bd, d_cdt, d_ab, group_sum)
    gs["ssm_a_re"], gs["ssm_a_im"] = d_are, d_aim
    gs["ssm_log_dt"] = d_ldt[0, :SSM_GROUPS]
    from_cn = lambda t: t.reshape(SSM_GROUP, SSM_GROUPS, SSM_STATE).transpose(1, 2, 0)
    c_from_cn = lambda t: t.reshape(SSM_GROUP, SSM_GROUPS, SSM_STATE).transpose(1, 0, 2)
    gs["ssm_b_re"], gs["ssm_b_im"] = from_cn(d_bre), from_cn(d_bim)
    gs["ssm_c_re"], gs["ssm_c_im"] = c_from_cn(d_cre), c_from_cn(d_cim)

    dx1, hm, dz, gs["mix_norm"] = _mix_in_bwd(dx2, x1, nm, dqs + dks + dvs, du, dgp, w["w_in"])
    gw["w_in"] = _matmul_tn(dz[None], hm[None], "dw_in")[0]
    sync.swap("w_in", gw)

    dx0, da, db, sact, h, d_out, gs["ffn1_norm"] = _ffn_bwd(dx1, x, n1, a1, b1, w["ffn1_w_gate"], w["ffn1_w_up"],
                                                            w["ffn1_w_down"], "ffn1_bwd")
    sync.exchange("w_in", later_than=[dx0])
    gw["ffn1_w_gate"] = _matmul_tn(da, h[None], "ffn1_dw_gate")
    gw["ffn1_w_up"] = _matmul_tn(db, h[None], "ffn1_dw_up")
    sync.small_ready(gs, loss_blk, later_than=[gw["ffn1_w_up"]])
    gw["ffn1_w_down"] = _matmul_tn(sact, d_out[None], "ffn1_dw_down")
    sync.swap("ffn1", gw)
    return dx0


def kernel(x, ffn1_norm, ffn1_w_gate, ffn1_w_up, ffn1_w_down, mix_norm, w_in, gate_bias, rel_bias_table, ssm_a_re, ssm_a_im, ssm_log_dt, ssm_b_re, ssm_b_im, ssm_c_re, ssm_c_im, ssm_d, ssm_w_glu, w_attn_branch, w_ssm_branch, w_out, ffn2_norm, ffn2_w_gate, ffn2_w_up, ffn2_w_down, final_norm, loss_target, m_ffn1_norm, m_ffn1_w_gate, m_ffn1_w_up, m_ffn1_w_down, m_mix_norm, m_w_in, m_gate_bias, m_rel_bias_table, m_ssm_a_re, m_ssm_a_im, m_ssm_log_dt, m_ssm_b_re, m_ssm_b_im, m_ssm_c_re, m_ssm_c_im, m_ssm_d, m_ssm_w_glu, m_w_attn_branch, m_w_ssm_branch, m_w_out, m_ffn2_norm, m_ffn2_w_gate, m_ffn2_w_up, m_ffn2_w_down, m_final_norm, v_ffn1_norm, v_ffn1_w_gate, v_ffn1_w_up, v_ffn1_w_down, v_mix_norm, v_w_in, v_gate_bias, v_rel_bias_table, v_ssm_a_re, v_ssm_a_im, v_ssm_log_dt, v_ssm_b_re, v_ssm_b_im, v_ssm_c_re, v_ssm_c_im, v_ssm_d, v_ssm_w_glu, v_w_attn_branch, v_w_ssm_branch, v_w_out, v_ffn2_norm, v_ffn2_w_gate, v_ffn2_w_up, v_ffn2_w_down, v_final_norm):
    args = dict(locals())
    weights = {n: args[n] for n in ORDER}
    moms = {n: args["m_" + n] for n in ORDER}
    vels = {n: args["v_" + n] for n in ORDER}

    shard2d = {n: _shard_2d(n, weights[n]) for n in BIG}
    first, rest = BIG[:3], BIG[3:]
    full = dict(zip(first, _gather_weights([shard2d[n].astype(BF16) for n in first], "gather_ffn1_weights")))
    later = {n: shard2d[n].astype(BF16) for n in rest}

    small = {n: weights[n] for n in SMALL}
    sync = _GradSync(weights, moms, vels)
    grad_x = _local_step(x[0], loss_target[0], full, later, small, sync)
    sync.finish_all()
    return (sync.loss, grad_x[None], *[sync.grads[n] for n in ORDER], *[sync.delta[n] for n in ORDER],
            *[sync.new_m[n] for n in ORDER], *[sync.new_v[n] for n in ORDER])
```

```python
import functools
import math

import jax
import jax.numpy as jnp
from jax import lax
from jax.experimental import pallas as pl
from jax.experimental.pallas import tpu as pltpu
from jax.experimental.pallas import tpu_sc as plsc

F32 = jnp.float32
BF16 = jnp.bfloat16
MESH = pl.DeviceIdType.MESH

D_MODEL = 1024
D_FF = 2816
HEAD_DIM = 64
HEADS_PER_GROUP = 4
DILATIONS = (1, 4, 16)
WINDOW_STEPS = 128
ATTN_BLOCK = 128
ATTN_QB = 4
GROUP_WIDTH = HEADS_PER_GROUP * HEAD_DIM
ATTN_WIDTH = 3 * GROUP_WIDTH
N_BUCKETS = 32
MAX_DISTANCE = 2048
NEG_INF = -1e30
SSM_WIDTH = 512
SSM_GROUP = 16
SSM_GROUPS = 32
SSM_STATE = 64
NS = SSM_GROUPS * SSM_STATE
EPS = 1e-6
IN_WIDTH = 3 * ATTN_WIDTH + SSM_WIDTH + 2 * D_MODEL
Q_SCALE = HEAD_DIM ** -0.5
N_SHARD = 4
FF_SHARD = D_FF // N_SHARD
ADAM_LR, ADAM_B1, ADAM_B2, ADAM_EPS, ADAM_WD, ADAM_STEP = 0.001, 0.9, 0.999, 1e-08, 0.01, 10

LANES = 128
VMEM_LIMIT = 56 * 1024 * 1024
ROW_TILE = 512
FFN_BWD_TILE = 256
SSM_CHUNK = 256
SSM_FWD_CHUNK = 512
SCAN_LANES = 512
ADAMW_BLOCK_BYTES = 2 << 20
TN_VMEM_BUDGET = 40 * 1024 * 1024
REDUCE_GROUPS = {
    "ffn2": ["ffn2_w_gate", "ffn2_w_up", "ffn2_w_down"],
    "mixer": ["w_out", "w_attn_branch", "w_ssm_branch", "ssm_w_glu"],
    "w_in": ["w_in"],
    "ffn1": ["ffn1_w_gate", "ffn1_w_up", "ffn1_w_down"],
}
COLLECTIVE_IDS = {name: i for i, name in enumerate(
    ["gather", "gather_small"] + [stage + "_" + tag for tag in REDUCE_GROUPS for stage in ("swap", "exchange")])}


def _params(**kw):
    return pltpu.CompilerParams(vmem_limit_bytes=VMEM_LIMIT, **kw)


def _dot(a, b):
    return jnp.dot(a, b, preferred_element_type=F32)


def _dot_nt(a, b):
    return lax.dot_general(a, b, (((1,), (1,)), ((), ())), preferred_element_type=F32)


def _dot_tn(a, b):
    return lax.dot_general(a, b, (((0,), (0,)), ((), ())), preferred_element_type=F32)


def _dot_exact(a, b):
    return jnp.dot(a, b, preferred_element_type=F32, precision=lax.Precision.HIGHEST)


def _dot_nt_exact(a, b):
    return lax.dot_general(a, b, (((1,), (1,)), ((), ())), preferred_element_type=F32,
                           precision=lax.Precision.HIGHEST)


def _rms(x):
    r = lax.rsqrt(jnp.mean(x * x, axis=-1, keepdims=True) + EPS)
    return r, x * r


def _rms_bwd(dh, g, r, xhat):
    dxh = dh * g
    return r * (dxh - xhat * jnp.mean(dxh * xhat, axis=-1, keepdims=True))


def _sigmoid(x):
    return 1.0 / (1.0 + jnp.exp(-x))


_GELU_C = math.sqrt(2.0 / math.pi)


def _gelu(x):
    return 0.5 * x * (1.0 + jnp.tanh(_GELU_C * (x + 0.044715 * x * x * x)))


def _gelu_grad(x):
    t = jnp.tanh(_GELU_C * (x + 0.044715 * x * x * x))
    return 0.5 * (1.0 + t) + 0.5 * x * (1.0 - t * t) * _GELU_C * (1.0 + 3 * 0.044715 * x * x)


def _whole():
    return pl.BlockSpec(memory_space=pltpu.VMEM)


def _row_tile(rows, cap):
    if rows <= cap:
        return rows
    return max(t for t in range(8, cap + 1, 8) if rows % t == 0)


def _rows(tm, w):
    return pl.BlockSpec((tm, w), lambda i: (i, 0))


def _acc_row(w):
    return pl.BlockSpec((1, w), lambda i: (0, 0))


def _ffn_fwd(x, g, wg, wu, wd, name, carried=()):
    L = x.shape[0]
    tm = min(ROW_TILE, L)
    n = len(carried)
    steps = L // tm

    def body(x_ref, g_ref, wg_ref, wu_ref, wd_ref, *refs):
        shard_refs, (xo_ref, a_ref, b_ref), full_refs, sems = refs[:n], refs[n:n + 3], refs[n + 3:2 * n + 3], refs[2 * n + 3:]
        if n:
            start, finish = _gather_parts([w.shape for w in carried], shard_refs, full_refs, *sems)
            pl.when(pl.program_id(0) == 0)(start)
        xv = x_ref[...]
        r, xhat = _rms(xv)
        h = (xhat * g_ref[...]).astype(BF16)
        acc = jnp.zeros((tm, D_MODEL), F32)
        for j in range(N_SHARD):
            a = _dot_nt(h, wg_ref[j])
            b = _dot_nt(h, wu_ref[j])
            a_ref[j] = a.astype(BF16)
            b_ref[j] = b.astype(BF16)
            s = (a * _sigmoid(a) * b).astype(BF16)
            acc = acc + _dot(s, wd_ref[j])
        xo_ref[...] = xv + 0.5 * acc
        if n:
            pl.when(pl.program_id(0) == steps - 1)(finish)

    act = pl.BlockSpec((N_SHARD, tm, FF_SHARD), lambda i: (0, i, 0))
    return pl.pallas_call(
        body, name=name, grid=(steps,),
        in_specs=[_rows(tm, D_MODEL), _whole(), _whole(), _whole(), _whole()] + [_ANY] * n,
        out_specs=[_rows(tm, D_MODEL), act, act] + [_ANY] * n,
        out_shape=[jax.ShapeDtypeStruct((L, D_MODEL), F32),
                   jax.ShapeDtypeStruct((N_SHARD, L, FF_SHARD), BF16),
                   jax.ShapeDtypeStruct((N_SHARD, L, FF_SHARD), BF16)]
        + [jax.ShapeDtypeStruct((N_SHARD,) + w.shape, w.dtype) for w in carried],
        scratch_shapes=[pltpu.SemaphoreType.DMA((7 * n,)), pltpu.SemaphoreType.DMA((7 * n,))] if n else [],
        compiler_params=_params(),
    )(x, g, wg, wu, wd, *carried)


def _ffn_bwd(dxo, x, g, a, b, wg, wu, wd, name):
    L = x.shape[0]
    tm = min(FFN_BWD_TILE, L)

    def body(dxo_ref, x_ref, g_ref, a_ref, b_ref, wg_ref, wu_ref, wd_ref,
             dxi_ref, da_ref, db_ref, s_ref, h_ref, do_ref, dg_ref):
        i = pl.program_id(0)
        xv = x_ref[...]
        gv = g_ref[...]
        r, xhat = _rms(xv)
        h_ref[...] = (xhat * gv).astype(BF16)
        dxo_v = dxo_ref[...]
        d_out = (0.5 * dxo_v).astype(BF16)
        do_ref[...] = d_out
        dh = jnp.zeros((tm, D_MODEL), F32)
        for j in range(N_SHARD):
            av = a_ref[j].astype(F32)
            bv = b_ref[j].astype(F32)
            sg = _sigmoid(av)
            sl = av * sg
            ds = _dot_nt(d_out, wd_ref[j])
            dbv = (ds * sl).astype(BF16)
            dav = (ds * bv * (sg * (1.0 + av * (1.0 - sg)))).astype(BF16)
            da_ref[j] = dav
            db_ref[j] = dbv
            s_ref[j] = (sl * bv).astype(BF16)
            dh = dh + _dot(dav, wg_ref[j]) + _dot(dbv, wu_ref[j])

        @pl.when(i == 0)
        def _():
            dg_ref[...] = jnp.zeros_like(dg_ref)

        dg_ref[...] += jnp.sum(dh * xhat, axis=0, keepdims=True)
        dxi_ref[...] = dxo_v + _rms_bwd(dh, gv, r, xhat)

    act = pl.BlockSpec((N_SHARD, tm, FF_SHARD), lambda i: (0, i, 0))
    act_shape = jax.ShapeDtypeStruct((N_SHARD, L, FF_SHARD), BF16)
    return pl.pallas_call(
        body, name=name, grid=(L // tm,),
        in_specs=[_rows(tm, D_MODEL), _rows(tm, D_MODEL), _whole(), act, act, _whole(), _whole(), _whole()],
        out_specs=[_rows(tm, D_MODEL), act, act, act, _rows(tm, D_MODEL), _rows(tm, D_MODEL), _acc_row(D_MODEL)],
        out_shape=[jax.ShapeDtypeStruct((L, D_MODEL), F32), act_shape, act_shape, act_shape,
                   jax.ShapeDtypeStruct((L, D_MODEL), BF16), jax.ShapeDtypeStruct((L, D_MODEL), BF16),
                   jax.ShapeDtypeStruct((1, D_MODEL), F32)],
        compiler_params=_params(),
    )(dxo, x, g, a, b, wg, wu, wd)


def _matmul_tn(a, b, name):
    ja, L, K = a.shape
    jb, _, N = b.shape
    J = max(ja, jb)
    splits = [s for s in (1, 2, 4, 8) if s == 1 or N % (s * LANES) == 0]
    nsplit = next((s for s in splits if 2 * K * (N // s) * 4 <= TN_VMEM_BUDGET // 2), splits[-1])
    nc = N // nsplit
    left = TN_VMEM_BUDGET - 2 * K * nc * 4
    row_bytes = 2 * (K * a.dtype.itemsize + nc * b.dtype.itemsize)
    tm = next((t for t in (2048, 1024, 512, 256) if L % t == 0 and t * row_bytes <= left), min(128, L))

    def body(a_ref, b_ref, o_ref):
        @pl.when(pl.program_id(2) == 0)
        def _():
            o_ref[...] = jnp.zeros_like(o_ref)

        o_ref[...] += _dot_tn(a_ref[...].astype(BF16), b_ref[...].astype(BF16))

    return pl.pallas_call(
        body, name=name, grid=(J, nsplit, L // tm),
        in_specs=[pl.BlockSpec((None, tm, K), (lambda j, s, i: (j, i, 0)) if ja > 1 else (lambda j, s, i: (0, i, 0))),
                  pl.BlockSpec((None, tm, nc), (lambda j, s, i: (j, i, s)) if jb > 1 else (lambda j, s, i: (0, i, s)))],
        out_specs=pl.BlockSpec((None, K, nc), lambda j, s, i: (j, 0, s)),
        out_shape=jax.ShapeDtypeStruct((J, K, N), F32),
        compiler_params=_params(),
    )(a, b)


def _loss_fwd_bwd(x, g, target):
    L = x.shape[0]
    tm = min(ROW_TILE, L)

    def body(x_ref, g_ref, t_ref, loss_ref, dx_ref, dg_ref):
        i = pl.program_id(0)
        xv = x_ref[...]
        gv = g_ref[...]
        r, xhat = _rms(xv)
        err = xhat * gv - t_ref[...]
        part = 0.5 * jnp.sum(jnp.sum(err * err, axis=1, keepdims=True) * (1.0 / D_MODEL), axis=0, keepdims=True)
        dy = err * (1.0 / D_MODEL)

        @pl.when(i == 0)
        def _():
            dg_ref[...] = jnp.zeros_like(dg_ref)
            loss_ref[...] = jnp.zeros_like(loss_ref)

        loss_ref[...] += jnp.broadcast_to(part, loss_ref.shape)
        dg_ref[...] += jnp.sum(dy * xhat, axis=0, keepdims=True)
        dx_ref[...] = _rms_bwd(dy, gv, r, xhat)

    return pl.pallas_call(
        body, name="loss_fwd_bwd", grid=(L // tm,),
        in_specs=[_rows(tm, D_MODEL), _whole(), _rows(tm, D_MODEL)],
        out_specs=[pl.BlockSpec((8, 128), lambda i: (0, 0)), _rows(tm, D_MODEL), _acc_row(D_MODEL)],
        out_shape=[jax.ShapeDtypeStruct((8, 128), F32), jax.ShapeDtypeStruct((L, D_MODEL), F32),
                   jax.ShapeDtypeStruct((1, D_MODEL), F32)],
        compiler_params=_params(),
    )(x, g, target)


_C_K = ATTN_WIDTH
_C_V = 2 * ATTN_WIDTH
_C_U = 3 * ATTN_WIDTH
_C_G = _C_U + SSM_WIDTH


def _residue_spec(d, tm):
    return pl.BlockSpec((d, tm // d, GROUP_WIDTH), lambda i: (0, i, 0))


def _residue_shape(d, L, dtype):
    return jax.ShapeDtypeStruct((d, L // d, GROUP_WIDTH), dtype)


def _residue_scratch(tm):
    return pltpu.VMEM((GROUP_WIDTH // LANES, tm, LANES), F32)


def _to_residues(val, out_ref, scr, d):
    if d == 1:
        out_ref[0] = val.astype(out_ref.dtype)
        return
    tm = val.shape[0]
    for half in range(GROUP_WIDTH // LANES):
        cols = slice(half * LANES, (half + 1) * LANES)
        scr[half] = val[:, cols]
        for r in range(d):
            out_ref[r, :, cols] = scr[half, pl.ds(r, tm // d, stride=d), :].astype(out_ref.dtype)


def _from_residues(ref, scr, d):
    if d == 1:
        return ref[0].astype(F32)
    rows = ref.shape[1]
    for half in range(GROUP_WIDTH // LANES):
        cols = slice(half * LANES, (half + 1) * LANES)
        for r in range(d):
            scr[half, pl.ds(r, rows, stride=d), :] = ref[r, :, cols].astype(F32)
    return jnp.concatenate([scr[half] for half in range(GROUP_WIDTH // LANES)], axis=1)


def _mix_in_fwd(x, g, w_in, gate_bias):
    L = x.shape[0]
    tm = min(ROW_TILE, L)

    def body(x_ref, g_ref, w_ref, gb_ref, *refs):
        qkv_refs, (u_ref, gate_ref, scr) = refs[:9], refs[9:]
        r, xhat = _rms(x_ref[...])
        h = (xhat * g_ref[...]).astype(BF16)
        for part, (c0, scale) in enumerate(((0, Q_SCALE), (_C_K, 1.0), (_C_V, 1.0))):
            z = _dot_nt(h, w_ref[c0:c0 + ATTN_WIDTH, :]) * scale
            for grp, d in enumerate(DILATIONS):
                _to_residues(z[:, grp * GROUP_WIDTH:(grp + 1) * GROUP_WIDTH], qkv_refs[3 * part + grp], scr, d)
        u_ref[...] = _dot_nt(h, w_ref[_C_U:_C_G, :])
        gate_ref[...] = _sigmoid(_dot_nt(h, w_ref[_C_G:IN_WIDTH, :]) + gb_ref[...])

    return pl.pallas_call(
        body, name="mix_in_fwd", grid=(L // tm,),
        in_specs=[_rows(tm, D_MODEL), _whole(), _whole(), _whole()],
        out_specs=[_residue_spec(d, tm) for d in DILATIONS] * 3 + [_rows(tm, SSM_WIDTH), _rows(tm, 2 * D_MODEL)],
        out_shape=[_residue_shape(d, L, BF16) for d in DILATIONS] * 3
        + [jax.ShapeDtypeStruct((L, SSM_WIDTH), F32), jax.ShapeDtypeStruct((L, 2 * D_MODEL), F32)],
        scratch_shapes=[_residue_scratch(tm)],
        compiler_params=_params(),
    )(x, g, w_in, gate_bias)


def _mix_in_bwd(dx2, x, g, dqkv, du, dgp, w_in):
    L = x.shape[0]
    tm = min(ROW_TILE, L)

    def body(dx2_ref, x_ref, g_ref, *refs):
        piece_refs = refs[:9]
        du_ref, dgp_ref, w_ref, dx1_ref, h_ref, dz_ref, dg_ref, scr = refs[9:]
        i = pl.program_id(0)
        gv = g_ref[...]
        r, xhat = _rms(x_ref[...])
        h_ref[...] = (xhat * gv).astype(BF16)
        for part in range(3):
            for grp, d in enumerate(DILATIONS):
                c0 = part * ATTN_WIDTH + grp * GROUP_WIDTH
                dz_ref[:, c0:c0 + GROUP_WIDTH] = _from_residues(piece_refs[3 * part + grp], scr, d).astype(BF16)
        dz_ref[:, _C_U:_C_G] = du_ref[...].astype(BF16)
        dz_ref[:, _C_G:IN_WIDTH] = dgp_ref[...]
        dh = _dot(dz_ref[...], w_ref[...])

        @pl.when(i == 0)
        def _():
            dg_ref[...] = jnp.zeros_like(dg_ref)

        dg_ref[...] += jnp.sum(dh * xhat, axis=0, keepdims=True)
        dx1_ref[...] = dx2_ref[...] + _rms_bwd(dh, gv, r, xhat)

    return pl.pallas_call(
        body, name="mix_in_bwd", grid=(L // tm,),
        in_specs=[_rows(tm, D_MODEL), _rows(tm, D_MODEL), _whole()] + [_residue_spec(d, tm) for d in DILATIONS] * 3
        + [_rows(tm, SSM_WIDTH), _rows(tm, 2 * D_MODEL), _whole()],
        out_specs=[_rows(tm, D_MODEL), _rows(tm, D_MODEL), _rows(tm, IN_WIDTH), _acc_row(D_MODEL)],
        out_shape=[jax.ShapeDtypeStruct((L, D_MODEL), F32), jax.ShapeDtypeStruct((L, D_MODEL), BF16),
                   jax.ShapeDtypeStruct((L, IN_WIDTH), BF16), jax.ShapeDtypeStruct((1, D_MODEL), F32)],
        scratch_shapes=[_residue_scratch(tm)],
        compiler_params=_params(),
    )(dx2, x, g, *dqkv, du, dgp, w_in)


def _bucket_onehot():
    qi = jnp.arange(ATTN_BLOCK)[:, None]
    kj = jnp.arange(2 * ATTN_BLOCK)[None, :]
    steps = jnp.maximum(qi + ATTN_BLOCK - kj, 0)
    max_exact = N_BUCKETS // 2
    out = []
    for d in DILATIONS:
        dist = steps * d
        df = jnp.maximum(dist, 1).astype(F32)
        large = max_exact + (jnp.log(df / max_exact) / math.log(MAX_DISTANCE / max_exact)
                             * (N_BUCKETS - max_exact)).astype(jnp.int32)
        large = jnp.minimum(large, N_BUCKETS - 1)
        bucket = jnp.where(dist < max_exact, dist, large).reshape(-1)
        out.append((bucket[None, :] == jnp.arange(N_BUCKETS)[:, None]).astype(F32))
    return jnp.stack(out)


def _bias_expand(table_t, onehot):
    n = onehot.shape[-1]

    def body(t_ref, oh_ref, o_ref):
        bias = _dot_exact(t_ref[...], oh_ref[...])
        col = lax.broadcasted_iota(jnp.int32, (8, n), 1)
        qi = col // (2 * ATTN_BLOCK)
        kj = col - qi * (2 * ATTN_BLOCK)
        steps = qi + ATTN_BLOCK - kj
        band = (steps >= 0) & (steps <= WINDOW_STEPS)
        o_ref[0] = jnp.where(band & (kj >= ATTN_BLOCK), bias, NEG_INF)
        o_ref[1] = jnp.where(band, bias, NEG_INF)

    return pl.pallas_call(
        body, name="bias_expand", grid=(3,),
        in_specs=[pl.BlockSpec((None, 8, N_BUCKETS), lambda g: (g, 0, 0)),
                  pl.BlockSpec((None, N_BUCKETS, n), lambda g: (g, 0, 0))],
        out_specs=pl.BlockSpec((None, 2, 8, n), lambda g: (g, 0, 0, 0)),
        out_shape=jax.ShapeDtypeStruct((3, 2, 8, n), F32),
        compiler_params=_params(),
    )(table_t, onehot)


def _bias_reduce(dsum, onehot):
    n = onehot.shape[-1]

    def body(d_ref, oh_ref, o_ref):
        o_ref[...] = _dot_nt_exact(d_ref[...], oh_ref[...])

    return pl.pallas_call(
        body, name="bias_reduce", grid=(3,),
        in_specs=[pl.BlockSpec((None, 8, n), lambda g: (g, 0, 0)),
                  pl.BlockSpec((None, N_BUCKETS, n), lambda g: (g, 0, 0))],
        out_specs=pl.BlockSpec((None, 8, N_BUCKETS), lambda g: (g, 0, 0)),
        out_shape=jax.ShapeDtypeStruct((3, 8, N_BUCKETS), F32),
        compiler_params=_params(),
    )(dsum, onehot)


def _head_of_col(rows):
    return lax.broadcasted_iota(jnp.int32, (rows, GROUP_WIDTH), 1) // HEAD_DIM


def _attn_specs(qb):
    rows = qb * ATTN_BLOCK
    cur = pl.BlockSpec((None, rows, GROUP_WIDTH), lambda r, n: (r, n, 0))
    prev = pl.BlockSpec((None, ATTN_BLOCK, GROUP_WIDTH), lambda r, n: (r, jnp.maximum(n * qb - 1, 0), 0))
    bias = pl.BlockSpec((2, HEADS_PER_GROUP, ATTN_BLOCK, 2 * ATTN_BLOCK), lambda r, n: (0, 0, 0, 0))
    return cur, prev, bias


def _attn_fwd(q, k, v, bias, name):
    d, M, _ = q.shape
    nb = M // ATTN_BLOCK
    qb = min(ATTN_QB, nb)

    def body(q_ref, kp_ref, kc_ref, vp_ref, vc_ref, bias_ref, o_ref, lse_ref):
        n = pl.program_id(1)
        q_head = _head_of_col(ATTN_BLOCK)
        kv_head = _head_of_col(2 * ATTN_BLOCK)
        kwin = jnp.concatenate([kp_ref[...], kc_ref[...]], axis=0)
        vwin = jnp.concatenate([vp_ref[...], vc_ref[...]], axis=0)
        for b in range(qb):
            rows = slice(b * ATTN_BLOCK, (b + 1) * ATTN_BLOCK)
            window = slice(b * ATTN_BLOCK, (b + 2) * ATTN_BLOCK)
            variant = jnp.minimum(n, 1) if b == 0 else 1
            qv = q_ref[rows, :]
            kk = kwin[window]
            vv = vwin[window]
            o_acc = jnp.zeros((ATTN_BLOCK, GROUP_WIDTH), F32)
            lse_acc = jnp.zeros((ATTN_BLOCK, GROUP_WIDTH), F32)
            for hh in range(HEADS_PER_GROUP):
                hm = q_head == hh
                qh = jnp.where(hm, qv, jnp.zeros_like(qv))
                logits = _dot_nt(qh, kk) + bias_ref[variant, hh]
                m = jnp.max(logits, axis=1, keepdims=True)
                p = jnp.exp(logits - m)
                vh = jnp.where(kv_head == hh, vv, jnp.ones_like(vv))
                pv = _dot(p.astype(BF16), vh)
                c_sum = ((hh + 1) % HEADS_PER_GROUP) * HEAD_DIM
                den = pv[:, c_sum:c_sum + 1]
                o_acc = jnp.where(hm, pv * (1.0 / den), o_acc)
                lse_acc = jnp.where(hm, m + jnp.log(den), lse_acc)
            o_ref[rows, :] = o_acc
            lse_ref[rows, :] = lse_acc

    cur, prev, full = _attn_specs(qb)
    return pl.pallas_call(
        body, name=name, grid=(d, nb // qb),
        in_specs=[cur, prev, cur, prev, cur, full],
        out_specs=[cur, cur],
        out_shape=[jax.ShapeDtypeStruct((d, M, GROUP_WIDTH), F32)] * 2,
        compiler_params=_params(),
    )(q, k, k, v, v, bias)


def _attn_bwd(q, k, v, do, lse, delta, bias, name):
    d, M, _ = q.shape
    nb = M // ATTN_BLOCK
    qb = min(ATTN_QB, nb)
    ns = nb // qb
    rows_q = qb * ATTN_BLOCK
    last = slice(rows_q - ATTN_BLOCK, rows_q)

    def body(q_ref, kp_ref, kc_ref, vp_ref, vc_ref, do_ref, lse_ref, dl_ref, bias_ref,
             dq_ref, dk_ref, dv_ref, dsum_ref, pk_ref, pv_ref, wk_ref, wv_ref):
        r = pl.program_id(0)
        n = pl.program_id(1)

        @pl.when((r == 0) & (n == 0))
        def _():
            dsum_ref[...] = jnp.zeros_like(dsum_ref)

        @pl.when(n == 0)
        def _():
            pk_ref[...] = jnp.zeros_like(pk_ref)
            pv_ref[...] = jnp.zeros_like(pv_ref)

        @pl.when(n < ns)
        def _():
            q_head = _head_of_col(ATTN_BLOCK)
            kwin = jnp.concatenate([kp_ref[...], kc_ref[...]], axis=0)
            vwin = jnp.concatenate([vp_ref[...], vc_ref[...]], axis=0)
            wk_ref[...] = jnp.zeros_like(wk_ref)
            wv_ref[...] = jnp.zeros_like(wv_ref)
            for b in range(qb):
                rows = slice(b * ATTN_BLOCK, (b + 1) * ATTN_BLOCK)
                window = slice(b * ATTN_BLOCK, (b + 2) * ATTN_BLOCK)
                variant = jnp.minimum(n, 1) if b == 0 else 1
                qv = q_ref[rows, :]
                dov = do_ref[rows, :]
                kk = kwin[window]
                vv = vwin[window]
                dq_acc = jnp.zeros((ATTN_BLOCK, GROUP_WIDTH), F32)
                dkk = jnp.zeros((2 * ATTN_BLOCK, GROUP_WIDTH), F32)
                dvv = jnp.zeros((2 * ATTN_BLOCK, GROUP_WIDTH), F32)
                for hh in range(HEADS_PER_GROUP):
                    hm = q_head == hh
                    c0 = hh * HEAD_DIM
                    qh = jnp.where(hm, qv, jnp.zeros_like(qv))
                    doh = jnp.where(hm, dov, jnp.zeros_like(dov))
                    logits = _dot_nt(qh, kk) + bias_ref[variant, hh]
                    p = jnp.exp(logits - lse_ref[rows, c0:c0 + 1])
                    dp = _dot_nt(doh, vv)
                    ds = p * (dp - dl_ref[rows, c0:c0 + 1])
                    dsum_ref[hh] += ds
                    ds16 = ds.astype(BF16)
                    dq_acc = jnp.where(hm, _dot(ds16, kk), dq_acc)
                    dkk = dkk + _dot_tn(ds16, qh)
                    dvv = dvv + _dot_tn(p.astype(BF16), doh)
                dq_ref[rows, :] = (dq_acc * Q_SCALE).astype(BF16)
                wk_ref[window, :] += dkk
                wv_ref[window, :] += dvv
            for out_ref, part_ref, win_ref in ((dk_ref, pk_ref, wk_ref), (dv_ref, pv_ref, wv_ref)):
                if qb > 1:
                    out_ref[0:rows_q - ATTN_BLOCK, :] = part_ref[0:rows_q - ATTN_BLOCK, :].astype(BF16)
                out_ref[last, :] = (part_ref[last, :] + win_ref[0:ATTN_BLOCK, :]).astype(BF16)
                part_ref[...] = win_ref[ATTN_BLOCK:, :]

        @pl.when(n == ns)
        def _():
            dk_ref[...] = pk_ref[...].astype(BF16)
            dv_ref[...] = pv_ref[...].astype(BF16)

    def clamp(n):
        return jnp.minimum(n, ns - 1)

    cur = pl.BlockSpec((None, rows_q, GROUP_WIDTH), lambda r, n: (r, clamp(n), 0))
    prev = pl.BlockSpec((None, ATTN_BLOCK, GROUP_WIDTH), lambda r, n: (r, jnp.maximum(clamp(n) * qb - 1, 0), 0))
    lag = pl.BlockSpec((None, rows_q, GROUP_WIDTH), lambda r, n: (r, jnp.maximum(n - 1, 0), 0))
    full = pl.BlockSpec((2, HEADS_PER_GROUP, ATTN_BLOCK, 2 * ATTN_BLOCK), lambda r, n: (0, 0, 0, 0))
    acc = pl.BlockSpec((HEADS_PER_GROUP, ATTN_BLOCK, 2 * ATTN_BLOCK), lambda r, n: (0, 0, 0))
    return pl.pallas_call(
        body, name=name, grid=(d, ns + 1),
        in_specs=[cur, prev, cur, prev, cur, cur, cur, cur, full],
        out_specs=[cur, lag, lag, acc],
        out_shape=[jax.ShapeDtypeStruct((d, M, GROUP_WIDTH), BF16)] * 3
        + [jax.ShapeDtypeStruct((HEADS_PER_GROUP, ATTN_BLOCK, 2 * ATTN_BLOCK), F32)],
        scratch_shapes=[pltpu.VMEM((rows_q, GROUP_WIDTH), F32), pltpu.VMEM((rows_q, GROUP_WIDTH), F32),
                        pltpu.VMEM((rows_q + ATTN_BLOCK, GROUP_WIDTH), F32),
                        pltpu.VMEM((rows_q + ATTN_BLOCK, GROUP_WIDTH), F32)],
        compiler_params=_params(),
    )(q, k, k, v, v, do, lse, delta, bias)


def _disc_math(a_re, a_im, ldt, b_re, b_im):
    dt = jnp.exp(ldt)
    mag = jnp.exp(a_re * dt)
    ab_re = mag * jnp.cos(a_im * dt)
    ab_im = mag * jnp.sin(a_im * dt)
    den = a_re * a_re + a_im * a_im
    xr = ab_re - 1.0
    coef_re = (xr * a_re + ab_im * a_im) / den
    coef_im = (ab_im * a_re - xr * a_im) / den
    return ab_re, ab_im, coef_re * b_re - coef_im * b_im, coef_re * b_im + coef_im * b_re


def _block_diag_mask():
    row_g = lax.broadcasted_iota(jnp.int32, (SSM_WIDTH, 2 * NS), 0) // SSM_GROUP
    col = lax.broadcasted_iota(jnp.int32, (SSM_WIDTH, 2 * NS), 1)
    col_g = jnp.where(col >= NS, col - NS, col) // SSM_STATE
    return row_g == col_g


def _disc_fwd(a_re, a_im, ldt, b_re, b_im, c_re, c_im):
    def body(are_ref, aim_ref, ldt_ref, bre_ref, bim_ref, cre_ref, cim_ref, pw_ref, pwr_ref, bd_ref, cdt_ref):
        ab_re, ab_im, bb_re, bb_im = _disc_math(are_ref[...], aim_ref[...], ldt_ref[...], bre_ref[...], bim_ref[...])
        row = lax.broadcasted_iota(jnp.int32, (8, NS), 0)
        pr, pi = ab_re, ab_im
        t_re = jnp.zeros((8, NS), F32)
        t_im = jnp.zeros((8, NS), F32)
        u_re = jnp.zeros((8, NS), F32)
        u_im = jnp.zeros((8, NS), F32)
        for j in range(8):
            t_re = jnp.where(row == j, pr, t_re)
            t_im = jnp.where(row == j, pi, t_im)
            u_re = jnp.where(row == 7 - j, pr, u_re)
            u_im = jnp.where(row == 7 - j, pi, u_im)
            pr, pi = pr * ab_re - pi * ab_im, pr * ab_im + pi * ab_re
        pw_ref[0] = t_re
        pw_ref[1] = t_im
        pwr_ref[0] = u_re
        pwr_ref[1] = u_im
        mask = _block_diag_mask()
        zero = jnp.zeros((SSM_WIDTH, 2 * NS), F32)
        bfull = jnp.concatenate([jnp.concatenate([bb_re] * SSM_GROUPS, axis=0),
                                 jnp.concatenate([bb_im] * SSM_GROUPS, axis=0)], axis=1)
        bd_ref[...] = jnp.where(mask, bfull, zero).astype(BF16)
        cfull = jnp.concatenate([jnp.concatenate([cre_ref[...]] * SSM_GROUPS, axis=0),
                                 jnp.concatenate([-cim_ref[...]] * SSM_GROUPS, axis=0)], axis=1)
        cdt_ref[...] = jnp.where(mask, cfull, zero).astype(BF16)

    return pl.pallas_call(
        body, name="s5_disc_fwd",
        in_specs=[_whole()] * 7, out_specs=[_whole()] * 4,
        out_shape=[jax.ShapeDtypeStruct((2, 8, NS), F32), jax.ShapeDtypeStruct((2, 8, NS), F32),
                   jax.ShapeDtypeStruct((SSM_WIDTH, 2 * NS), BF16), jax.ShapeDtypeStruct((SSM_WIDTH, 2 * NS), BF16)],
        compiler_params=_params(),
    )(a_re, a_im, ldt, b_re, b_im, c_re, c_im)


def _disc_bwd(a_re, a_im, ldt, b_re, b_im, d_bd, d_cdt, d_ab, group_sum):
    def body(are_ref, aim_ref, ldt_ref, bre_ref, bim_ref, dbd_ref, dcdt_ref, dab_ref, gs_ref,
             dare_ref, daim_ref, dldt_ref, dbre_ref, dbim_ref, dcre_ref, dcim_ref):
        col = lax.broadcasted_iota(jnp.int32, (SSM_GROUP, 2 * NS), 1)
        col_g = jnp.where(col >= NS, col - NS, col) // SSM_STATE
        acc_b = jnp.zeros((SSM_GROUP, 2 * NS), F32)
        acc_c = jnp.zeros((SSM_GROUP, 2 * NS), F32)
        for g in range(SSM_GROUPS):
            rows = slice(g * SSM_GROUP, (g + 1) * SSM_GROUP)
            acc_b = acc_b + jnp.where(col_g == g, dbd_ref[rows, :], 0.0)
            acc_c = acc_c + jnp.where(col_g == g, dcdt_ref[rows, :], 0.0)
        dcre_ref[...] = acc_c[:, :NS]
        dcim_ref[...] = -acc_c[:, NS:]
        dab_re = jnp.sum(dab_ref[0], axis=0, keepdims=True)
        dab_im = jnp.sum(dab_ref[1], axis=0, keepdims=True)
        _, vjp = jax.vjp(_disc_math, are_ref[...], aim_ref[...], ldt_ref[...], bre_ref[...], bim_ref[...])
        d_are, d_aim, d_ldt, d_bre, d_bim = vjp((dab_re, dab_im, acc_b[:, :NS], acc_b[:, NS:]))
        dare_ref[...] = d_are
        daim_ref[...] = d_aim
        dbre_ref[...] = d_bre
        dbim_ref[...] = d_bim
        dldt_ref[...] = _dot_exact(jnp.broadcast_to(d_ldt, (8, NS)), gs_ref[...])

    vec = jax.ShapeDtypeStruct((1, NS), F32)
    mat = jax.ShapeDtypeStruct((SSM_GROUP, NS), F32)
    return pl.pallas_call(
        body, name="s5_disc_bwd",
        in_specs=[_whole()] * 9, out_specs=[_whole()] * 7,
        out_shape=[vec, vec, jax.ShapeDtypeStruct((8, 128), F32), mat, mat, mat, mat],
        compiler_params=_params(),
    )(a_re, a_im, ldt, b_re, b_im, d_bd, d_cdt, d_ab, group_sum)


def _scan_blocks(buf, pw_ref, carry_ref, n_blocks, reverse):
    row = lax.broadcasted_iota(jnp.int32, (8, SCAN_LANES), 0)
    for lc in range(NS // SCAN_LANES):
        re_cols = pl.ds(lc * SCAN_LANES, SCAN_LANES)
        im_cols = pl.ds(NS + lc * SCAN_LANES, SCAN_LANES)
        pr = pw_ref[0, :, re_cols]
        pi = pw_ref[1, :, re_cols]
        if reverse:
            pi = -pi
            base = [(7, 1), (6, 2), (4, 4)]
            coef = [(jnp.where(row < 8 - k, pr[j:j + 1], 0.0), jnp.where(row < 8 - k, pi[j:j + 1], 0.0), 8 - k)
                    for j, k in base]
        else:
            base = [(0, 1), (1, 2), (3, 4)]
            coef = [(jnp.where(row >= k, pr[j:j + 1], 0.0), jnp.where(row >= k, pi[j:j + 1], 0.0), k)
                    for j, k in base]

        def step(i, carry, pr=pr, pi=pi, coef=coef, re_cols=re_cols, im_cols=im_cols):
            cr, ci = carry
            blk = (n_blocks - 1 - i) if reverse else i
            rows = pl.ds(pl.multiple_of(blk * 8, 8), 8)
            xr = buf[rows, re_cols]
            xi = buf[rows, im_cols]
            for kr, ki, shift in coef:
                sr = pltpu.roll(xr, shift, 0)
                si = pltpu.roll(xi, shift, 0)
                xr, xi = xr + kr * sr - ki * si, xi + kr * si + ki * sr
            xr, xi = xr + pr * cr - pi * ci, xi + pr * ci + pi * cr
            buf[rows, re_cols] = xr
            buf[rows, im_cols] = xi
            edge = slice(0, 1) if reverse else slice(7, 8)
            return xr[edge], xi[edge]

        cr, ci = lax.fori_loop(0, n_blocks, step, (carry_ref[0:1, re_cols], carry_ref[0:1, im_cols]))
        carry_ref[0:1, re_cols] = cr
        carry_ref[0:1, im_cols] = ci


_SUPER_GROUPS = 16
_SUPER_BLOCKS = [
    (slice(k * _SUPER_GROUPS * SSM_GROUP, (k + 1) * _SUPER_GROUPS * SSM_GROUP),
     [slice(half + k * _SUPER_GROUPS * SSM_STATE, half + (k + 1) * _SUPER_GROUPS * SSM_STATE) for half in (0, NS)])
    for k in range(SSM_GROUPS // _SUPER_GROUPS)]


def _ssm_fwd(u, bd, cdt, d_skip, pw):
    L = u.shape[0]
    tc = min(SSM_FWD_CHUNK, L)

    def body(u_ref, bd_ref, cdt_ref, dsk_ref, pw_ref, y_ref, s_ref, carry_ref):
        @pl.when(pl.program_id(0) == 0)
        def _():
            carry_ref[...] = jnp.zeros_like(carry_ref)

        uv = u_ref[...]
        u16 = uv.astype(BF16)
        for ch, states in _SUPER_BLOCKS:
            for st in states:
                s_ref[:, st] = _dot(u16[:, ch], bd_ref[ch, st])
        _scan_blocks(s_ref, pw_ref, carry_ref, tc // 8, reverse=False)
        for ch, states in _SUPER_BLOCKS:
            y_ref[:, ch] = (sum(_dot_nt(s_ref[:, st].astype(BF16), cdt_ref[ch, st]) for st in states)
                            + dsk_ref[:, ch] * uv[:, ch])

    return pl.pallas_call(
        body, name="s5_fwd", grid=(L // tc,),
        in_specs=[_rows(tc, SSM_WIDTH), _whole(), _whole(), _whole(), _whole()],
        out_specs=[_rows(tc, SSM_WIDTH), _rows(tc, 2 * NS)],
        out_shape=[jax.ShapeDtypeStruct((L, SSM_WIDTH), F32), jax.ShapeDtypeStruct((L, 2 * NS), F32)],
        scratch_shapes=[pltpu.VMEM((8, 2 * NS), F32)],
        compiler_params=_params(),
    )(u, bd, cdt, d_skip, pw)


def _ssm_bwd(dy, u, s, bd, cdt, d_skip, pwr):
    L = u.shape[0]
    tc = min(SSM_CHUNK, L)
    nc = L // tc
    blocks = tc // 8

    def body(dy_ref, u_ref, s_ref, sprev_ref, bd_ref, cdt_ref, dsk_ref, pwr_ref,
             du_ref, ddsk_ref, dbd_ref, dcdt_ref, dab_ref, g_ref, sx_ref, carry_ref):
        i = pl.program_id(0)

        @pl.when(i == 0)
        def _():
            carry_ref[...] = jnp.zeros_like(carry_ref)
            ddsk_ref[...] = jnp.zeros_like(ddsk_ref)
            dbd_ref[...] = jnp.zeros_like(dbd_ref)
            dcdt_ref[...] = jnp.zeros_like(dcdt_ref)
            dab_ref[...] = jnp.zeros_like(dab_ref)

        dyv = dy_ref[...]
        uv = u_ref[...]
        dy16 = dyv.astype(BF16)
        u16 = uv.astype(BF16)
        for ch, states in _SUPER_BLOCKS:
            for st in states:
                g_ref[:, st] = _dot(dy16[:, ch], cdt_ref[ch, st])
        _scan_blocks(g_ref, pwr_ref, carry_ref, blocks, reverse=True)
        ddsk_ref[...] += jnp.sum(dyv * uv, axis=0, keepdims=True)
        for ch, states in _SUPER_BLOCKS:
            du = dsk_ref[:, ch] * dyv[:, ch]
            for st in states:
                g16 = g_ref[:, st].astype(BF16)
                du = du + _dot_nt(g16, bd_ref[ch, st])
                dbd_ref[ch, st] += _dot_tn(u16[:, ch], g16)
                dcdt_ref[ch, st] += _dot_tn(dy16[:, ch], s_ref[:, st].astype(BF16))
            du_ref[:, ch] = du

        sx_ref[pl.ds(8, tc), :] = s_ref[...]
        sx_ref[pl.ds(0, 8), :] = jnp.where(i == nc - 1, 0.0, sprev_ref[...])
        row = lax.broadcasted_iota(jnp.int32, (8, SCAN_LANES), 0)
        for lc in range(NS // SCAN_LANES):
            re_cols = pl.ds(lc * SCAN_LANES, SCAN_LANES)
            im_cols = pl.ds(NS + lc * SCAN_LANES, SCAN_LANES)

            def step(b, acc, re_cols=re_cols, im_cols=im_cols):
                ar, ai = acc
                off = pl.multiple_of(b * 8, 8)
                gr = g_ref[pl.ds(off, 8), re_cols]
                gi = g_ref[pl.ds(off, 8), im_cols]
                before = pl.ds(off, 8)
                here = pl.ds(off + 8, 8)
                sr = jnp.where(row == 0, sx_ref[before, re_cols][7:8], pltpu.roll(sx_ref[here, re_cols], 1, 0))
                si = jnp.where(row == 0, sx_ref[before, im_cols][7:8], pltpu.roll(sx_ref[here, im_cols], 1, 0))
                return ar + gr * sr + gi * si, ai + gi * sr - gr * si

            zero = jnp.zeros((8, SCAN_LANES), F32)
            ar, ai = lax.fori_loop(0, blocks, step, (zero, zero))
            dab_ref[0, :, re_cols] += ar
            dab_ref[1, :, re_cols] += ai

    rev = lambda i: (nc - 1 - i, 0)
    sprev = pl.BlockSpec((8, 2 * NS), lambda i: (jnp.maximum((nc - 1 - i) * blocks - 1, 0), 0))
    return pl.pallas_call(
        body, name="s5_bwd", grid=(nc,),
        in_specs=[pl.BlockSpec((tc, SSM_WIDTH), rev), pl.BlockSpec((tc, SSM_WIDTH), rev),
                  pl.BlockSpec((tc, 2 * NS), rev), sprev, _whole(), _whole(), _whole(), _whole()],
        out_specs=[pl.BlockSpec((tc, SSM_WIDTH), rev), _whole(), _whole(), _whole(), _whole()],
        out_shape=[jax.ShapeDtypeStruct((L, SSM_WIDTH), F32), jax.ShapeDtypeStruct((1, SSM_WIDTH), F32),
                   jax.ShapeDtypeStruct((SSM_WIDTH, 2 * NS), F32), jax.ShapeDtypeStruct((SSM_WIDTH, 2 * NS), F32),
                   jax.ShapeDtypeStruct((2, 8, NS), F32)],
        scratch_shapes=[pltpu.VMEM((tc, 2 * NS), F32), pltpu.VMEM((tc + 8, 2 * NS), F32), pltpu.VMEM((8, 2 * NS), F32)],
        compiler_params=_params(),
    )(dy, u, s, s, bd, cdt, d_skip, pwr)


def _branches(o_attn, y, gates, w_ab, w_glu, w_sb):
    ya = _dot(o_attn.astype(BF16), w_ab[...])
    gel = _gelu(y)
    glu = _dot(gel.astype(BF16), w_glu[...])
    p = glu[:, :SSM_WIDTH]
    sg = _sigmoid(glu[:, SSM_WIDTH:])
    ys2 = p * sg
    ysb = _dot(ys2.astype(BF16), w_sb[...])
    ga = gates[:, :D_MODEL]
    gs = gates[:, D_MODEL:]
    return ya, gel, p, sg, ys2, ysb, ga, gs


def _mix_out_fwd(x1, o_g, lse_g, y, gates, w_ab, w_glu, w_sb, w_out):
    L = x1.shape[0]
    tm = min(ROW_TILE, L)

    def body(x_ref, o0, o1, o2, l0, l1, l2, y_ref, gate_ref, wab_ref, wglu_ref, wsb_ref, wout_ref,
             x2_ref, oat_ref, lse0, lse1, lse2, scr):
        la, lb, lc = (_from_residues(ref, scr, d) for ref, d in zip((l0, l1, l2), DILATIONS))
        m = jnp.maximum(jnp.maximum(la, lb), lc)
        ea, eb, ec = jnp.exp(la - m), jnp.exp(lb - m), jnp.exp(lc - m)
        tot = ea + eb + ec
        oa, ob, oc = (_from_residues(ref, scr, d) for ref, d in zip((o0, o1, o2), DILATIONS))
        o_attn = (ea * oa + eb * ob + ec * oc) / tot
        oat_ref[...] = o_attn
        lse = m + jnp.log(tot)
        for ref, d in zip((lse0, lse1, lse2), DILATIONS):
            _to_residues(lse, ref, scr, d)
        ya, _, _, _, _, ysb, ga, gs = _branches(o_attn, y_ref[...], gate_ref[...], wab_ref, wglu_ref, wsb_ref)
        mix = ga * ya + gs * ysb
        x2_ref[...] = x_ref[...] + _dot(mix.astype(BF16), wout_ref[...])

    res = [_residue_spec(d, tm) for d in DILATIONS]
    return pl.pallas_call(
        body, name="mix_out_fwd", grid=(L // tm,),
        in_specs=[_rows(tm, D_MODEL)] + res * 2 + [_rows(tm, SSM_WIDTH), _rows(tm, 2 * D_MODEL)] + [_whole()] * 4,
        out_specs=[_rows(tm, D_MODEL), _rows(tm, GROUP_WIDTH)] + res,
        out_shape=[jax.ShapeDtypeStruct((L, D_MODEL), F32), jax.ShapeDtypeStruct((L, GROUP_WIDTH), F32)]
        + [_residue_shape(d, L, F32) for d in DILATIONS],
        scratch_shapes=[_residue_scratch(tm)],
        compiler_params=_params(),
    )(x1, *o_g, *lse_g, y, gates, w_ab, w_glu, w_sb, w_out)


def _mix_out_bwd(dx2, o_attn, y, gates, w_ab, w_glu, w_sb, w_out, head_sum):
    L = dx2.shape[0]
    tm = min(ROW_TILE, L)

    def body(dx_ref, oat_ref, y_ref, gate_ref, wab_ref, wglu_ref, wsb_ref, wout_ref, hs_ref,
             do0, do1, do2, dl0, dl1, dl2, dy_ref, dgp_ref, mix_ref, dya_ref, dys_ref, ys2_ref, gel_ref, dglu_ref,
             dgb_ref, scr):
        i = pl.program_id(0)
        o_attn = oat_ref[...]
        yv = y_ref[...]
        ya, gel, p, sg, ys2, ysb, ga, gs = _branches(o_attn, yv, gate_ref[...], wab_ref, wglu_ref, wsb_ref)
        mix_ref[...] = (ga * ya + gs * ysb).astype(BF16)
        ys2_ref[...] = ys2.astype(BF16)
        gel_ref[...] = gel.astype(BF16)
        dmix = _dot_nt(dx_ref[...].astype(BF16), wout_ref[...])
        dgp = jnp.concatenate([dmix * ya * ga * (1.0 - ga), dmix * ysb * gs * (1.0 - gs)], axis=1)
        dgp_ref[...] = dgp.astype(BF16)

        @pl.when(i == 0)
        def _():
            dgb_ref[...] = jnp.zeros_like(dgb_ref)

        dgb_ref[...] += jnp.sum(dgp, axis=0, keepdims=True)
        dya = (dmix * ga).astype(BF16)
        dys = (dmix * gs).astype(BF16)
        dya_ref[...] = dya
        dys_ref[...] = dys
        d_o = _dot_nt(dya, wab_ref[...])
        delta = _dot_exact(d_o * o_attn, hs_ref[...])
        for do_ref, dl_ref, d in zip((do0, do1, do2), (dl0, dl1, dl2), DILATIONS):
            _to_residues(d_o, do_ref, scr, d)
            _to_residues(delta, dl_ref, scr, d)
        dys2 = _dot_nt(dys, wsb_ref[...])
        dglu = jnp.concatenate([dys2 * sg, dys2 * p * sg * (1.0 - sg)], axis=1).astype(BF16)
        dglu_ref[...] = dglu
        dy_ref[...] = _dot_nt(dglu, wglu_ref[...]) * _gelu_grad(yv)

    grp = _rows(tm, GROUP_WIDTH)
    wide = _rows(tm, D_MODEL)
    half = _rows(tm, SSM_WIDTH)
    res = [_residue_spec(d, tm) for d in DILATIONS]
    sds = jax.ShapeDtypeStruct
    return pl.pallas_call(
        body, name="mix_out_bwd", grid=(L // tm,),
        in_specs=[wide, grp, half, _rows(tm, 2 * D_MODEL)] + [_whole()] * 5,
        out_specs=res + res + [half, _rows(tm, 2 * D_MODEL), wide, wide, wide, half, half, wide, _acc_row(2 * D_MODEL)],
        out_shape=[_residue_shape(d, L, BF16) for d in DILATIONS] + [_residue_shape(d, L, F32) for d in DILATIONS]
        + [sds((L, SSM_WIDTH), F32),
           sds((L, 2 * D_MODEL), BF16), sds((L, D_MODEL), BF16), sds((L, D_MODEL), BF16),
           sds((L, D_MODEL), BF16), sds((L, SSM_WIDTH), BF16), sds((L, SSM_WIDTH), BF16),
           sds((L, D_MODEL), BF16), sds((1, 2 * D_MODEL), F32)],
        scratch_shapes=[_residue_scratch(tm)],
        compiler_params=_params(),
    )(dx2, o_attn, y, gates, w_ab, w_glu, w_sb, w_out, head_sum)


def _adamw(w, g, m, v, name):
    R, C = w.shape
    tr = _row_tile(R, max(8, ADAMW_BLOCK_BYTES // (4 * C)))

    def body(w_ref, g_ref, m_ref, v_ref, d_ref, mo_ref, vo_ref):
        gv = g_ref[...]
        mn = ADAM_B1 * m_ref[...] + (1.0 - ADAM_B1) * gv
        vn = ADAM_B2 * v_ref[...] + (1.0 - ADAM_B2) * (gv * gv)
        m_hat = mn / (1.0 - ADAM_B1 ** ADAM_STEP)
        v_hat = vn / (1.0 - ADAM_B2 ** ADAM_STEP)
        d_ref[...] = -ADAM_LR * (m_hat / (jnp.sqrt(v_hat) + ADAM_EPS) + ADAM_WD * w_ref[...])
        mo_ref[...] = mn
        vo_ref[...] = vn

    blk = pl.BlockSpec((tr, C), lambda i: (i, 0))
    return pl.pallas_call(
        body, name=name, grid=(R // tr,),
        in_specs=[blk] * 4, out_specs=[blk] * 3,
        out_shape=[jax.ShapeDtypeStruct((R, C), F32)] * 3,
        compiler_params=_params(),
    )(w, g, m, v)


def _sum_chips_into_half(u, t, name):
    S, H, C = u.shape
    tr = _row_tile(H, 512)
    hb = H // tr

    def body(s_ref, t_ref, a_ref, b_ref, c_ref, o_ref):
        me = s_ref[1]
        others = (a_ref[...], b_ref[...], c_ref[...])
        acc = None
        for chip in range(S):
            below = others[min(chip, S - 2)]
            above = others[max(chip - 1, 0)]
            term = jnp.where(me == chip, t_ref[...], jnp.where(me > chip, below, above)).astype(F32)
            acc = term if acc is None else acc + term
        o_ref[...] = acc

    x, y, c = lax.axis_index("x"), lax.axis_index("y"), lax.axis_index("c")
    me = 2 * x + y
    scalars = jnp.stack([c, me] + [j + (j >= me).astype(jnp.int32) for j in range(S - 1)]).astype(jnp.int32)
    blk = (None, tr, C)
    return pl.pallas_call(
        body, name=name,
        grid_spec=pltpu.PrefetchScalarGridSpec(
            num_scalar_prefetch=1, grid=(hb,),
            in_specs=[pl.BlockSpec(blk, lambda i, s: (s[1], i, 0))]
            + [pl.BlockSpec(blk, functools.partial(lambda j, i, s: (s[2 + j], i, 0), j)) for j in range(S - 1)],
            out_specs=pl.BlockSpec((tr, C), lambda i, s: (s[0] * hb + i, 0))),
        out_shape=jax.ShapeDtypeStruct((2 * H, C), F32),
        compiler_params=_params(),
    )(scalars, t, u, u, u)


def _add_halves(g, r1, name):
    S, R, C = g.shape
    H = R // 2
    tr = _row_tile(H, 512)
    hb = H // tr

    def body(c_ref, g_ref, r_ref, o_ref):
        o_ref[...] = (g_ref[...] + r_ref[...]).astype(BF16)

    core = lax.axis_index("c").astype(jnp.int32).reshape(1)
    return pl.pallas_call(
        body, name=name,
        grid_spec=pltpu.PrefetchScalarGridSpec(
            num_scalar_prefetch=1, grid=(S, hb),
            in_specs=[pl.BlockSpec((None, tr, C), lambda j, i, c_ref: (j, c_ref[0] * hb + i, 0)),
                      pl.BlockSpec((None, tr, C), lambda j, i, c_ref: (j, i, 0))],
            out_specs=pl.BlockSpec((None, tr, C), lambda j, i, c_ref: (j, i, 0))),
        out_shape=jax.ShapeDtypeStruct((S, H, C), BF16),
        compiler_params=_params(),
    )(core, g, r1)


_ANY = pl.BlockSpec(memory_space=pl.ANY)


def _place():
    x, y, c = lax.axis_index("x"), lax.axis_index("y"), lax.axis_index("c")
    chips = [(1 - x, y), (x, 1 - y), (1 - x, 1 - y)]
    return x, y, c, chips


def _comm_call(body, name, ins, out_shapes, n_remote, n_local):
    return pl.pallas_call(
        body, name=name,
        in_specs=[_ANY] * len(ins), out_specs=[_ANY] * len(out_shapes), out_shape=out_shapes,
        scratch_shapes=[pltpu.SemaphoreType.DMA((n_remote,)), pltpu.SemaphoreType.DMA((n_remote,)),
                        pltpu.SemaphoreType.DMA((max(n_local, 1),))],
    )(*ins)


def _remote(src, dst, send_sems, recv_sems, k, device):
    return pltpu.make_async_remote_copy(src_ref=src, dst_ref=dst, send_sem=send_sems.at[k], recv_sem=recv_sems.at[k],
                                        device_id=device, device_id_type=MESH)


def _gather_parts(shapes, w_refs, out_refs, send_sems, recv_sems):
    n = len(shapes)
    x, y, c, chips = _place()
    me = 2 * x + y
    sibling = (x, y, 1 - c)

    def half(k, chip_idx, core):
        H = shapes[k][0] // 2
        return out_refs[k].at[chip_idx, pl.ds(core * H, H), :]

    mine = [_remote(w_refs[k], out_refs[k].at[me], send_sems, recv_sems, 6 * n + k, sibling) for k in range(n)]
    first = []
    for k in range(n):
        H = shapes[k][0] // 2
        for j, (cx, cy) in enumerate(chips):
            first.append(_remote(w_refs[k].at[pl.ds(c * H, H), :], half(k, me, c), send_sems, recv_sems,
                                 3 * k + j, (cx, cy, c)))

    def start():
        for cp in mine + first:
            cp.start()

    def finish():
        passed = []
        for k in range(n):
            for j, (cx, cy) in enumerate(chips):
                landed = half(k, 2 * cx + cy, c)
                _remote(landed, landed, send_sems, recv_sems, 3 * k + j, (cx, cy, c)).wait_recv()
                fwd = _remote(landed, landed, send_sems, recv_sems, 3 * n + 3 * k + j, sibling)
                fwd.start()
                passed.append(fwd)
        for k in range(n):
            for j, (cx, cy) in enumerate(chips):
                other = half(k, 2 * cx + cy, 1 - c)
                _remote(other, other, send_sems, recv_sems, 3 * n + 3 * k + j, sibling).wait_recv()
        for cp in mine:
            cp.wait_recv()
        for cp in first + passed + mine:
            cp.wait_send()

    return start, finish


def _gather_weights(shards, name):
    n = len(shards)

    def body(*refs):
        x, y, c, chips = _place()
        _handshake([(x, y, 1 - c)] + [(cx, cy, c) for cx, cy in chips])
        start, finish = _gather_parts([w.shape for w in shards], refs[:n], refs[n:2 * n], *refs[2 * n:2 * n + 2])
        start()
        finish()

    return _sequenced(body, name, shards, [jax.ShapeDtypeStruct((N_SHARD,) + w.shape, w.dtype) for w in shards],
                      7 * n, COLLECTIVE_IDS["gather"])


def _handshake(peers):
    barrier = pltpu.get_barrier_semaphore()
    for peer in peers:
        pl.semaphore_signal(barrier, inc=1, device_id=peer, device_id_type=MESH)
    pl.semaphore_wait(barrier, len(peers))


def _sequenced(body, name, ins, out_shapes, n_sems, collective_id):
    return pl.kernel(
        body, out_type=list(out_shapes), mesh=plsc.ScalarSubcoreMesh(axis_name="sequencer", num_cores=1), name=name,
        scratch_types=(pltpu.SemaphoreType.DMA((n_sems,)), pltpu.SemaphoreType.DMA((n_sems,))),
        compiler_params=pltpu.CompilerParams(collective_id=collective_id))(*ins)


def _swap_halves(gs, name, collective_id):
    n = len(gs)

    def body(*refs):
        g_refs, out_refs = refs[:n], refs[n:2 * n]
        send_sems, recv_sems = refs[2 * n:]
        x, y, c, _ = _place()
        _handshake([(x, y, 1 - c)])
        cps = []
        for k in range(n):
            H = gs[k].shape[1] // 2
            cp = _remote(g_refs[k].at[:, pl.ds((1 - c) * H, H), :], out_refs[k], send_sems, recv_sems, k, (x, y, 1 - c))
            cp.start()
            cps.append(cp)
        for cp in cps:
            cp.wait()

    return _sequenced(body, name, gs, [jax.ShapeDtypeStruct((g.shape[0], g.shape[1] // 2, g.shape[2]), g.dtype)
                                       for g in gs], n, collective_id)


def _exchange_chips(ts, name, collective_id):
    n = len(ts)

    def body(*refs):
        t_refs, out_refs = refs[:n], refs[n:2 * n]
        send_sems, recv_sems = refs[2 * n:]
        x, y, c, chips = _place()
        me = 2 * x + y
        _handshake([(cx, cy, c) for cx, cy in chips])
        sent = []
        for k in range(n):
            for j, (cx, cy) in enumerate(chips):
                cp = _remote(t_refs[k].at[2 * cx + cy], out_refs[k].at[me], send_sems, recv_sems, 3 * k + j, (cx, cy, c))
                cp.start()
                sent.append(cp)
        for k in range(n):
            for j, (cx, cy) in enumerate(chips):
                slot = out_refs[k].at[2 * cx + cy]
                _remote(slot, slot, send_sems, recv_sems, 3 * k + j, (cx, cy, c)).wait_recv()
        for cp in sent:
            cp.wait_send()

    return _sequenced(body, name, ts, [jax.ShapeDtypeStruct(t.shape, t.dtype) for t in ts], 3 * n, collective_id)


def _join_halves(fs, name):
    n = len(fs)

    def body(*refs):
        out_refs = refs[n:2 * n]
        send_sems, recv_sems, _ = refs[2 * n:]
        x, y, c, _ = _place()
        sent = []
        for k in range(n):
            H = fs[k].shape[0] // 2
            here = out_refs[k].at[pl.ds(c * H, H), :]
            cp = _remote(here, here, send_sems, recv_sems, k, (x, y, 1 - c))
            cp.start()
            sent.append(cp)
        for k in range(n):
            H = fs[k].shape[0] // 2
            other = out_refs[k].at[pl.ds((1 - c) * H, H), :]
            _remote(other, other, send_sems, recv_sems, k, (x, y, 1 - c)).wait_recv()
        for cp in sent:
            cp.wait_send()

    return pl.pallas_call(
        body, name=name,
        in_specs=[_ANY] * n, out_specs=[_ANY] * n,
        out_shape=[jax.ShapeDtypeStruct(f.shape, f.dtype) for f in fs],
        input_output_aliases={k: k for k in range(n)},
        scratch_shapes=[pltpu.SemaphoreType.DMA((n,)), pltpu.SemaphoreType.DMA((n,)), pltpu.SemaphoreType.DMA((1,))],
    )(*fs)


def _gather_small(v):
    R, C = v.shape

    def body(v_ref, out_ref, send_sems, recv_sems):
        x, y, c, _ = _place()
        me = 4 * x + 2 * y + c
        flips = [(fx, fy, fc) for fx in (0, 1) for fy in (0, 1) for fc in (0, 1)][1:]
        peers = [((1 - x) if fx else x, (1 - y) if fy else y, (1 - c) if fc else c) for fx, fy, fc in flips]
        _handshake(peers)
        sent = []
        for j, peer in enumerate(peers):
            cp = _remote(v_ref, out_ref.at[me], send_sems, recv_sems, j, peer)
            cp.start()
            sent.append(cp)
        for j, peer in enumerate(peers):
            slot = out_ref.at[4 * peer[0] + 2 * peer[1] + peer[2]]
            _remote(slot, slot, send_sems, recv_sems, j, peer).wait_recv()
        for cp in sent:
            cp.wait_send()

    return _sequenced(body, "gather_small", [v], [jax.ShapeDtypeStruct((8, R, C), F32)], 7,
                      COLLECTIVE_IDS["gather_small"])[0]


def _sum_devices(x, own, name):
    S, R, C = x.shape
    tr = _row_tile(R, 2048)

    def body(s_ref, x_ref, own_ref, o_ref):
        me = s_ref[0]
        acc = None
        for k in range(S):
            term = jnp.where(me == k, own_ref[...], x_ref[k])
            acc = term if acc is None else acc + term
        o_ref[...] = acc

    x_, y_, c_ = lax.axis_index("x"), lax.axis_index("y"), lax.axis_index("c")
    me = (4 * x_ + 2 * y_ + c_).astype(jnp.int32).reshape(1)
    return pl.pallas_call(
        body, name=name,
        grid_spec=pltpu.PrefetchScalarGridSpec(
            num_scalar_prefetch=1, grid=(R // tr,),
            in_specs=[pl.BlockSpec((S, tr, C), lambda i, s: (0, i, 0)), pl.BlockSpec((tr, C), lambda i, s: (i, 0))],
            out_specs=pl.BlockSpec((tr, C), lambda i, s: (i, 0))),
        out_shape=jax.ShapeDtypeStruct((R, C), F32),
        compiler_params=_params(),
    )(me, x, own)


def _after(earlier, arrays):
    return lax.optimization_barrier((earlier, arrays))


def _reduce_swap(gs, tag, earlier):
    gs = _after(earlier, gs)[1]
    return gs, _swap_halves(gs, "reduce_swap_" + tag, COLLECTIVE_IDS["swap_" + tag])


def _reduce_exchange(gs, r1, names, tag, later_than):
    r1 = _after(later_than, r1)[1]
    ts = [_add_halves(g, r, "reduce_add_cores_" + nm) for g, r, nm in zip(gs, r1, names)]
    us = _exchange_chips(ts, "reduce_exchange_" + tag, COLLECTIVE_IDS["exchange_" + tag])
    return us, ts


def _reduce_finish(us, ts, names, tag):
    fs = [_sum_chips_into_half(u, t, "reduce_add_chips_" + nm) for u, t, nm in zip(us, ts, names)]
    return _join_halves(fs, "reduce_join_" + tag)


BIG = ["ffn1_w_gate", "ffn1_w_up", "ffn1_w_down", "w_in", "ssm_w_glu", "w_attn_branch", "w_ssm_branch",
       "w_out", "ffn2_w_gate", "ffn2_w_up", "ffn2_w_down"]
SMALL = ["ffn1_norm", "mix_norm", "gate_bias", "rel_bias_table", "ssm_a_re", "ssm_a_im", "ssm_log_dt",
         "ssm_b_re", "ssm_b_im", "ssm_c_re", "ssm_c_im", "ssm_d", "ffn2_norm", "final_norm"]
ORDER = ["ffn1_norm", "ffn1_w_gate", "ffn1_w_up", "ffn1_w_down", "mix_norm", "w_in", "gate_bias", "rel_bias_table",
         "ssm_a_re", "ssm_a_im", "ssm_log_dt", "ssm_b_re", "ssm_b_im", "ssm_c_re", "ssm_c_im", "ssm_d",
         "ssm_w_glu", "w_attn_branch", "w_ssm_branch", "w_out", "ffn2_norm", "ffn2_w_gate", "ffn2_w_up",
         "ffn2_w_down", "final_norm"]


_SMALL_TILE = 8 * LANES


def _pack_small(arrays):
    rows = []
    for a in arrays:
        flat = a.reshape(-1).astype(F32)
        rows.append(jnp.pad(flat, (0, (-flat.shape[0]) % _SMALL_TILE)).reshape(-1, LANES))
    return jnp.concatenate(rows, axis=0)


def _unpack_small(packed, shapes):
    out, r0 = [], 0
    for shp in shapes:
        n = math.prod(shp)
        rows = 8 * -(-n // _SMALL_TILE)
        out.append(packed[r0:r0 + rows].reshape(-1)[:n].reshape(shp))
        r0 += rows
    return out


def _split_cols(g):
    K, N = g.shape
    return g.reshape(K, N_SHARD, N // N_SHARD).transpose(1, 0, 2)


def _join_cols(w):
    S, K, n = w.shape
    return w.transpose(1, 0, 2).reshape(K, S * n)


COL_SHARDED = ("ssm_w_glu", "w_attn_branch", "w_ssm_branch")
TRANSPOSED = ("ffn1_w_gate", "ffn1_w_up", "ffn2_w_gate", "ffn2_w_up", "w_in")


def _shard_2d(name, arr):
    two_d = arr.reshape(arr.shape[-2:])
    return two_d.T if name in TRANSPOSED else two_d


def _shard_nd(name, two_d, shape):
    return (two_d.T if name in TRANSPOSED else two_d).reshape(shape)


class _GradSync:
    def __init__(self, weights, moms, vels):
        self.weights, self.moms, self.vels = weights, moms, vels
        self.grads, self.delta, self.new_m, self.new_v = {}, {}, {}, {}
        self.loss = None
        self._earlier = []
        self._swapped = {}
        self._exchanged = {}

    def swap(self, tag, gw, later_than=()):
        gs = []
        for n in REDUCE_GROUPS[tag]:
            g = gw[n]
            if n in COL_SHARDED:
                g = _split_cols(g)
            elif n in ("w_out", "w_in"):
                g = g.reshape(N_SHARD, g.shape[0] // N_SHARD, g.shape[1])
            gs.append(g)
        self._swapped[tag] = _reduce_swap(gs, tag, list(self._earlier) + list(later_than))
        self._earlier = self._swapped[tag][1]

    def exchange(self, tag, later_than):
        gs, r1 = self._swapped[tag]
        us, ts = _reduce_exchange(gs, r1, REDUCE_GROUPS[tag], tag, later_than)
        self._exchanged[tag] = (us, ts)
        self._earlier = us

    def small_ready(self, gs, loss_blk, later_than=()):
        _, (mine,) = _after(list(self._earlier) + list(later_than),
                            [_pack_small([gs[n] for n in SMALL] + [loss_blk[0:1, :]])])
        others = _gather_small(mine)
        self._exchanged["small"] = (others, mine)
        self._earlier = [others]

    def finish(self, tag):
        made = []
        if tag == "small":
            others, mine = self._exchanged[tag]
            shapes = [self.weights[n].shape for n in SMALL]
            total = _unpack_small(_sum_devices(others, mine, "sum_small"), shapes + [(128,)])
            self.loss = total[-1][0]
            self.grads.update(zip(SMALL, total[:-1]))
            packed = [_pack_small([src[n] for n in SMALL]) for src in (self.weights, self.grads, self.moms, self.vels)]
            for dst, res in zip((self.delta, self.new_m, self.new_v), _adamw(*packed, "adamw_small")):
                dst.update(zip(SMALL, _unpack_small(res, shapes)))
            for n in SMALL:
                made += [self.grads[n], self.delta[n], self.new_m[n], self.new_v[n]]
            return made + [self.loss]
        names = REDUCE_GROUPS[tag]
        us, ts = self._exchanged[tag]
        for n, g in zip(names, _reduce_finish(us, ts, names, tag)):
            shp = self.weights[n].shape
            d, m, v = _adamw(_shard_2d(n, self.weights[n]), g, _shard_2d(n, self.moms[n]), _shard_2d(n, self.vels[n]),
                             "adamw_" + n)
            self.grads[n], self.delta[n] = _shard_nd(n, g, shp), _shard_nd(n, d, shp)
            self.new_m[n], self.new_v[n] = _shard_nd(n, m, shp), _shard_nd(n, v, shp)
            made += [self.grads[n], self.delta[n], self.new_m[n], self.new_v[n]]
        return made

    def finish_all(self):
        self.exchange("ffn1", later_than=self.finish("ffn2"))
        for tag in ("mixer", "w_in", "small", "ffn1"):
            self.finish(tag)


def _local_step(x, target, w, later, small, sync):
    L = x.shape[0]
    row = lambda v: v.reshape(1, -1)

    a_re, a_im = small["ssm_a_re"].reshape(1, NS), small["ssm_a_im"].reshape(1, NS)
    ldt = jnp.repeat(small["ssm_log_dt"].reshape(SSM_GROUPS), SSM_STATE).reshape(1, NS)
    to_cn = lambda b: b.reshape(SSM_GROUPS, SSM_STATE, SSM_GROUP).transpose(2, 0, 1).reshape(SSM_GROUP, NS)
    c_to_cn = lambda c: c.reshape(SSM_GROUPS, SSM_GROUP, SSM_STATE).transpose(1, 0, 2).reshape(SSM_GROUP, NS)
    b_re, b_im = to_cn(small["ssm_b_re"]), to_cn(small["ssm_b_im"])
    c_re, c_im = c_to_cn(small["ssm_c_re"]), c_to_cn(small["ssm_c_im"])
    d_skip = row(small["ssm_d"])
    pw, pwr, bd, cdt = _disc_fwd(a_re, a_im, ldt, b_re, b_im, c_re, c_im)

    onehot = _bucket_onehot()
    table_t = small["rel_bias_table"].T.reshape(3, HEADS_PER_GROUP, N_BUCKETS)
    table_t = jnp.pad(table_t, ((0, 0), (0, 8 - HEADS_PER_GROUP), (0, 0)))
    bias = _bias_expand(table_t, onehot)[:, :, :HEADS_PER_GROUP].reshape(
        3, 2, HEADS_PER_GROUP, ATTN_BLOCK, 2 * ATTN_BLOCK)

    n1, nm, n2, nf = row(small["ffn1_norm"]), row(small["mix_norm"]), row(small["ffn2_norm"]), row(small["final_norm"])
    gate_bias = row(small["gate_bias"])

    x1, a1, b1, *later_full = _ffn_fwd(x, n1, w["ffn1_w_gate"], w["ffn1_w_up"], w["ffn1_w_down"], "ffn1_fwd",
                                       carried=list(later.values()))
    w = dict(w, **dict(zip(later, later_full)))
    for n in COL_SHARDED:
        w[n] = _join_cols(w[n])
    w["w_out"] = w["w_out"].reshape(D_MODEL, D_MODEL)
    w["w_in"] = w["w_in"].reshape(IN_WIDTH, D_MODEL)
    *qkv, u, gates = _mix_in_fwd(x1, nm, w["w_in"], gate_bias)
    q, k, v = qkv[0:3], qkv[3:6], qkv[6:9]
    o_g, lse_g = [], []
    for grp in range(3):
        o, lse = _attn_fwd(q[grp], k[grp], v[grp], bias[grp], f"attn_fwd_{grp}")
        o_g.append(o)
        lse_g.append(lse)
    y, s = _ssm_fwd(u, bd, cdt, d_skip, pw)
    x2, o_attn, *lse_tot = _mix_out_fwd(x1, o_g, lse_g, y, gates, w["w_attn_branch"], w["ssm_w_glu"],
                                        w["w_ssm_branch"], w["w_out"])
    x3, a2, b2 = _ffn_fwd(x2, n2, w["ffn2_w_gate"], w["ffn2_w_up"], w["ffn2_w_down"], "ffn2_fwd")
    loss_blk, dx3, d_nf = _loss_fwd_bwd(x3, nf, target)

    gw, gs = {}, {}
    gs["final_norm"] = d_nf

    dx2, da, db, sact, h, d_out, gs["ffn2_norm"] = _ffn_bwd(dx3, x2, n2, a2, b2, w["ffn2_w_gate"], w["ffn2_w_up"],
                                                            w["ffn2_w_down"], "ffn2_bwd")
    gw["ffn2_w_gate"] = _matmul_tn(da, h[None], "ffn2_dw_gate")
    gw["ffn2_w_up"] = _matmul_tn(db, h[None], "ffn2_dw_up")
    gw["ffn2_w_down"] = _matmul_tn(sact, d_out[None], "ffn2_dw_down")
    sync.swap("ffn2", gw)

    head_sum = (jnp.arange(GROUP_WIDTH)[:, None] // HEAD_DIM == jnp.arange(GROUP_WIDTH)[None, :] // HEAD_DIM).astype(F32)
    (*d_o_delta, dy, dgp, mix, dya, dys, ys2, gel, dglu, gs["gate_bias"]) = _mix_out_bwd(
        dx2, o_attn, y, gates, w["w_attn_branch"], w["ssm_w_glu"], w["w_ssm_branch"], w["w_out"], head_sum)
    sync.exchange("ffn2", later_than=[dy])
    d_o, delta = d_o_delta[0:3], d_o_delta[3:6]
    gw["w_out"] = _matmul_tn(mix[None], dx2[None], "dw_out")[0]
    gw["w_attn_branch"] = _matmul_tn(o_attn[None], dya[None], "dw_attn_branch")[0]
    gw["w_ssm_branch"] = _matmul_tn(ys2[None], dys[None], "dw_ssm_branch")[0]
    gw["ssm_w_glu"] = _matmul_tn(gel[None], dglu[None], "dw_glu")[0]

    dqs, dks, dvs, dsums = [], [], [], []
    for grp in range(3):
        dq, dk, dv, dsum = _attn_bwd(q[grp], k[grp], v[grp], d_o[grp], lse_tot[grp], delta[grp], bias[grp],
                                     f"attn_bwd_{grp}")
        dqs.append(dq)
        dks.append(dk)
        dvs.append(dv)
        dsums.append(dsum.reshape(HEADS_PER_GROUP, -1))
    dsum_all = jnp.pad(jnp.stack(dsums), ((0, 0), (0, 8 - HEADS_PER_GROUP), (0, 0)))
    d_table = _bias_reduce(dsum_all, onehot)[:, :HEADS_PER_GROUP]
    gs["rel_bias_table"] = d_table.reshape(3 * HEADS_PER_GROUP, N_BUCKETS).T

    du, gs["ssm_d"], d_bd, d_cdt, d_ab = _ssm_bwd(dy, u, s, bd, cdt, d_skip, pwr)
    sync.swap("mixer", gw, later_than=[du])
    sync.exchange("mixer", later_than=[dqs[2]])
    group_sum =(jnp.arange(NS)[:, None] // SSM_STATE == jnp.arange(128)[None, :]).astype(F32)
    d_are, d_aim, d_ldt, d_bre, d_bim, d_cre, d_cim = _disc_bwd(a_re, a_im, ldt, b_re, b_im, d_bd, d_cdt, d_ab, group_sum)
    gs["ssm_a_re"], gs["ssm_a_im"] = d_are, d_aim
    gs["ssm_log_dt"] = d_ldt[0, :SSM_GROUPS]
    from_cn = lambda t: t.reshape(SSM_GROUP, SSM_GROUPS, SSM_STATE).transpose(1, 2, 0)
    c_from_cn = lambda t: t.reshape(SSM_GROUP, SSM_GROUPS, SSM_STATE).transpose(1, 0, 2)
    gs["ssm_b_re"], gs["ssm_b_im"] = from_cn(d_bre), from_cn(d_bim)
    gs["ssm_c_re"], gs["ssm_c_im"] = c_from_cn(d_cre), c_from_cn(d_cim)

    dx1, hm, dz, gs["mix_norm"] = _mix_in_bwd(dx2, x1, nm, dqs + dks + dvs, du, dgp, w["w_in"])
    gw["w_in"] = _matmul_tn(dz[None], hm[None], "dw_in")[0]
    sync.swap("w_in", gw)

    dx0, da, db, sact, h, d_out, gs["ffn1_norm"] = _ffn_bwd(dx1, x, n1, a1, b1, w["ffn1_w_gate"], w["ffn1_w_up"],
                                                            w["ffn1_w_down"], "ffn1_bwd")
    sync.exchange("w_in", later_than=[dx0])
    gw["ffn1_w_gate"] = _matmul_tn(da, h[None], "ffn1_dw_gate")
    gw["ffn1_w_up"] = _matmul_tn(db, h[None], "ffn1_dw_up")
    sync.small_ready(gs, loss_blk, later_than=[gw["ffn1_w_up"]])
    gw["ffn1_w_down"] = _matmul_tn(sact, d_out[None], "ffn1_dw_down")
    sync.swap("ffn1", gw)
    return dx0


def kernel(x, ffn1_norm, ffn1_w_gate, ffn1_w_up, ffn1_w_down, mix_norm, w_in, gate_bias, rel_bias_table, ssm_a_re, ssm_a_im, ssm_log_dt, ssm_b_re, ssm_b_im, ssm_c_re, ssm_c_im, ssm_d, ssm_w_glu, w_attn_branch, w_ssm_branch, w_out, ffn2_norm, ffn2_w_gate, ffn2_w_up, ffn2_w_down, final_norm, loss_target, m_ffn1_norm, m_ffn1_w_gate, m_ffn1_w_up, m_ffn1_w_down, m_mix_norm, m_w_in, m_gate_bias, m_rel_bias_table, m_ssm_a_re, m_ssm_a_im, m_ssm_log_dt, m_ssm_b_re, m_ssm_b_im, m_ssm_c_re, m_ssm_c_im, m_ssm_d, m_ssm_w_glu, m_w_attn_branch, m_w_ssm_branch, m_w_out, m_ffn2_norm, m_ffn2_w_gate, m_ffn2_w_up, m_ffn2_w_down, m_final_norm, v_ffn1_norm, v_ffn1_w_gate, v_ffn1_w_up, v_ffn1_w_down, v_mix_norm, v_w_in, v_gate_bias, v_rel_bias_table, v_ssm_a_re, v_ssm_a_im, v_ssm_log_dt, v_ssm_b_re, v_ssm_b_im, v_ssm_c_re, v_ssm_c_im, v_ssm_d, v_ssm_w_glu, v_w_attn_branch, v_w_ssm_branch, v_w_out, v_ffn2_norm, v_ffn2_w_gate, v_ffn2_w_up, v_ffn2_w_down, v_final_norm):
    args = dict(locals())
    weights = {n: args[n] for n in ORDER}
    moms = {n: args["m_" + n] for n in ORDER}
    vels = {n: args["v_" + n] for n in ORDER}

    shard2d = {n: _shard_2d(n, weights[n]) for n in BIG}
    first, rest = BIG[:3], BIG[3:]
    full = dict(zip(first, _gather_weights([shard2d[n].astype(BF16) for n in first], "gather_ffn1_weights")))
    later = {n: shard2d[n].astype(BF16) for n in rest}

    small = {n: weights[n] for n in SMALL}
    sync = _GradSync(weights, moms, vels)
    grad_x = _local_step(x[0], loss_target[0], full, later, small, sync)
    sync.finish_all()
    return (sync.loss, grad_x[None], *[sync.grads[n] for n in ORDER], *[sync.delta[n] for n in ORDER],
            *[sync.new_m[n] for n in ORDER], *[sync.new_v[n] for n in ORDER])
```

```python
import functools
import math

import jax
import jax.numpy as jnp
from jax import lax
from jax.experimental import pallas as pl
from jax.experimental.pallas import tpu as pltpu
from jax.experimental.pallas import tpu_sc as plsc

F32 = jnp.float32
BF16 = jnp.bfloat16
MESH = pl.DeviceIdType.MESH

D_MODEL = 1024
D_FF = 2816
HEAD_DIM = 64
HEADS_PER_GROUP = 4
DILATIONS = (1, 4, 16)
WINDOW_STEPS = 128
ATTN_BLOCK = 128
ATTN_QB = 4
GROUP_WIDTH = HEADS_PER_GROUP * HEAD_DIM
ATTN_WIDTH = 3 * GROUP_WIDTH
N_BUCKETS = 32
MAX_DISTANCE = 2048
NEG_INF = -1e30
SSM_WIDTH = 512
SSM_GROUP = 16
SSM_GROUPS = 32
SSM_STATE = 64
NS = SSM_GROUPS * SSM_STATE
EPS = 1e-6
IN_WIDTH = 3 * ATTN_WIDTH + SSM_WIDTH + 2 * D_MODEL
Q_SCALE = HEAD_DIM ** -0.5
N_SHARD = 4
FF_SHARD = D_FF // N_SHARD
ADAM_LR, ADAM_B1, ADAM_B2, ADAM_EPS, ADAM_WD, ADAM_STEP = 0.001, 0.9, 0.999, 1e-08, 0.01, 10

LANES = 128
VMEM_LIMIT = 56 * 1024 * 1024
ROW_TILE = 512
FFN_BWD_TILE = 256
SSM_CHUNK = 256
SSM_FWD_CHUNK = 512
SCAN_LANES = 512
ADAMW_BLOCK_BYTES = 2 << 20
TN_VMEM_BUDGET = 40 * 1024 * 1024
REDUCE_GROUPS = {
    "ffn2": ["ffn2_w_gate", "ffn2_w_up", "ffn2_w_down"],
    "mixer": ["w_out", "w_attn_branch", "w_ssm_branch", "ssm_w_glu"],
    "w_in": ["w_in"],
    "ffn1": ["ffn1_w_gate", "ffn1_w_up", "ffn1_w_down"],
}
COLLECTIVE_IDS = {name: i for i, name in enumerate(
    ["gather", "gather_small"] + [stage + "_" + tag for tag in REDUCE_GROUPS for stage in ("swap", "exchange")])}


def _params(**kw):
    return pltpu.CompilerParams(vmem_limit_bytes=VMEM_LIMIT, **kw)


def _dot(a, b):
    return jnp.dot(a, b, preferred_element_type=F32)


def _dot_nt(a, b):
    return lax.dot_general(a, b, (((1,), (1,)), ((), ())), preferred_element_type=F32)


def _dot_tn(a, b):
    return lax.dot_general(a, b, (((0,), (0,)), ((), ())), preferred_element_type=F32)


def _dot_exact(a, b):
    return jnp.dot(a, b, preferred_element_type=F32, precision=lax.Precision.HIGHEST)


def _dot_nt_exact(a, b):
    return lax.dot_general(a, b, (((1,), (1,)), ((), ())), preferred_element_type=F32,
                           precision=lax.Precision.HIGHEST)


def _rms(x):
    r = lax.rsqrt(jnp.mean(x * x, axis=-1, keepdims=True) + EPS)
    return r, x * r


def _rms_bwd(dh, g, r, xhat):
    dxh = dh * g
    return r * (dxh - xhat * jnp.mean(dxh * xhat, axis=-1, keepdims=True))


def _sigmoid(x):
    return 1.0 / (1.0 + jnp.exp(-x))


_GELU_C = math.sqrt(2.0 / math.pi)


def _gelu(x):
    return 0.5 * x * (1.0 + jnp.tanh(_GELU_C * (x + 0.044715 * x * x * x)))


def _gelu_grad(x):
    t = jnp.tanh(_GELU_C * (x + 0.044715 * x * x * x))
    return 0.5 * (1.0 + t) + 0.5 * x * (1.0 - t * t) * _GELU_C * (1.0 + 3 * 0.044715 * x * x)


def _whole():
    return pl.BlockSpec(memory_space=pltpu.VMEM)


def _row_tile(rows, cap):
    if rows <= cap:
        return rows
    return max(t for t in range(8, cap + 1, 8) if rows % t == 0)


def _rows(tm, w):
    return pl.BlockSpec((tm, w), lambda i: (i, 0))


def _acc_row(w):
    return pl.BlockSpec((1, w), lambda i: (0, 0))


def _ffn_fwd(x, g, wg, wu, wd, name, carried=()):
    L = x.shape[0]
    tm = min(ROW_TILE, L)
    n = len(carried)
    steps = L // tm

    def body(x_ref, g_ref, wg_ref, wu_ref, wd_ref, *refs):
        shard_refs, (xo_ref, a_ref, b_ref), full_refs, sems = refs[:n], refs[n:n + 3], refs[n + 3:2 * n + 3], refs[2 * n + 3:]
        if n:
            start, finish = _gather_parts([w.shape for w in carried], shard_refs, full_refs, *sems)
            pl.when(pl.program_id(0) == 0)(start)
        xv = x_ref[...]
        r, xhat = _rms(xv)
        h = (xhat * g_ref[...]).astype(BF16)
        acc = jnp.zeros((tm, D_MODEL), F32)
        for j in range(N_SHARD):
            a = _dot_nt(h, wg_ref[j])
            b = _dot_nt(h, wu_ref[j])
            a_ref[j] = a.astype(BF16)
            b_ref[j] = b.astype(BF16)
            s = (a * _sigmoid(a) * b).astype(BF16)
            acc = acc + _dot(s, wd_ref[j])
        xo_ref[...] = xv + 0.5 * acc
        if n:
            pl.when(pl.program_id(0) == steps - 1)(finish)

    act = pl.BlockSpec((N_SHARD, tm, FF_SHARD), lambda i: (0, i, 0))
    return pl.pallas_call(
        body, name=name, grid=(steps,),
        in_specs=[_rows(tm, D_MODEL), _whole(), _whole(), _whole(), _whole()] + [_ANY] * n,
        out_specs=[_rows(tm, D_MODEL), act, act] + [_ANY] * n,
        out_shape=[jax.ShapeDtypeStruct((L, D_MODEL), F32),
                   jax.ShapeDtypeStruct((N_SHARD, L, FF_SHARD), BF16),
                   jax.ShapeDtypeStruct((N_SHARD, L, FF_SHARD), BF16)]
        + [jax.ShapeDtypeStruct((N_SHARD,) + w.shape, w.dtype) for w in carried],
        scratch_shapes=[pltpu.SemaphoreType.DMA((7 * n,)), pltpu.SemaphoreType.DMA((7 * n,))] if n else [],
        compiler_params=_params(),
    )(x, g, wg, wu, wd, *carried)


def _ffn_bwd(dxo, x, g, a, b, wg, wu, wd, name):
    L = x.shape[0]
    tm = min(FFN_BWD_TILE, L)

    def body(dxo_ref, x_ref, g_ref, a_ref, b_ref, wg_ref, wu_ref, wd_ref,
             dxi_ref, da_ref, db_ref, s_ref, h_ref, do_ref, dg_ref):
        i = pl.program_id(0)
        xv = x_ref[...]
        gv = g_ref[...]
        r, xhat = _rms(xv)
        h_ref[...] = (xhat * gv).astype(BF16)
        dxo_v = dxo_ref[...]
        d_out = (0.5 * dxo_v).astype(BF16)
        do_ref[...] = d_out
        dh = jnp.zeros((tm, D_MODEL), F32)
        for j in range(N_SHARD):
            av = a_ref[j].astype(F32)
            bv = b_ref[j].astype(F32)
            sg = _sigmoid(av)
            sl = av * sg
            ds = _dot_nt(d_out, wd_ref[j])
            dbv = (ds * sl).astype(BF16)
            dav = (ds * bv * (sg * (1.0 + av * (1.0 - sg)))).astype(BF16)
            da_ref[j] = dav
            db_ref[j] = dbv
            s_ref[j] = (sl * bv).astype(BF16)
            dh = dh + _dot(dav, wg_ref[j]) + _dot(dbv, wu_ref[j])

        @pl.when(i == 0)
        def _():
            dg_ref[...] = jnp.zeros_like(dg_ref)

        dg_ref[...] += jnp.sum(dh * xhat, axis=0, keepdims=True)
        dxi_ref[...] = dxo_v + _rms_bwd(dh, gv, r, xhat)

    act = pl.BlockSpec((N_SHARD, tm, FF_SHARD), lambda i: (0, i, 0))
    act_shape = jax.ShapeDtypeStruct((N_SHARD, L, FF_SHARD), BF16)
    return pl.pallas_call(
        body, name=name, grid=(L // tm,),
        in_specs=[_rows(tm, D_MODEL), _rows(tm, D_MODEL), _whole(), act, act, _whole(), _whole(), _whole()],
        out_specs=[_rows(tm, D_MODEL), act, act, act, _rows(tm, D_MODEL), _rows(tm, D_MODEL), _acc_row(D_MODEL)],
        out_shape=[jax.ShapeDtypeStruct((L, D_MODEL), F32), act_shape, act_shape, act_shape,
                   jax.ShapeDtypeStruct((L, D_MODEL), BF16), jax.ShapeDtypeStruct((L, D_MODEL), BF16),
                   jax.ShapeDtypeStruct((1, D_MODEL), F32)],
        compiler_params=_params(),
    )(dxo, x, g, a, b, wg, wu, wd)


def _matmul_tn(a, b, name):
    ja, L, K = a.shape
    jb, _, N = b.shape
    J = max(ja, jb)
    splits = [s for s in (1, 2, 4, 8) if s == 1 or N % (s * LANES) == 0]
    nsplit = next((s for s in splits if 2 * K * (N // s) * 4 <= TN_VMEM_BUDGET // 2), splits[-1])
    nc = N // nsplit
    left = TN_VMEM_BUDGET - 2 * K * nc * 4
    row_bytes = 2 * (K * a.dtype.itemsize + nc * b.dtype.itemsize)
    tm = next((t for t in (2048, 1024, 512, 256) if L % t == 0 and t * row_bytes <= left), min(128, L))

    def body(a_ref, b_ref, o_ref):
        @pl.when(pl.program_id(2) == 0)
        def _():
            o_ref[...] = jnp.zeros_like(o_ref)

        o_ref[...] += _dot_tn(a_ref[...].astype(BF16), b_ref[...].astype(BF16))

    return pl.pallas_call(
        body, name=name, grid=(J, nsplit, L // tm),
        in_specs=[pl.BlockSpec((None, tm, K), (lambda j, s, i: (j, i, 0)) if ja > 1 else (lambda j, s, i: (0, i, 0))),
                  pl.BlockSpec((None, tm, nc), (lambda j, s, i: (j, i, s)) if jb > 1 else (lambda j, s, i: (0, i, s)))],
        out_specs=pl.BlockSpec((None, K, nc), lambda j, s, i: (j, 0, s)),
        out_shape=jax.ShapeDtypeStruct((J, K, N), F32),
        compiler_params=_params(),
    )(a, b)


def _loss_fwd_bwd(x, g, target):
    L = x.shape[0]
    tm = min(ROW_TILE, L)

    def body(x_ref, g_ref, t_ref, loss_ref, dx_ref, dg_ref):
        i = pl.program_id(0)
        xv = x_ref[...]
        gv = g_ref[...]
        r, xhat = _rms(xv)
        err = xhat * gv - t_ref[...]
        part = 0.5 * jnp.sum(jnp.sum(err * err, axis=1, keepdims=True) * (1.0 / D_MODEL), axis=0, keepdims=True)
        dy = err * (1.0 / D_MODEL)

        @pl.when(i == 0)
        def _():
            dg_ref[...] = jnp.zeros_like(dg_ref)
            loss_ref[...] = jnp.zeros_like(loss_ref)

        loss_ref[...] += jnp.broadcast_to(part, loss_ref.shape)
        dg_ref[...] += jnp.sum(dy * xhat, axis=0, keepdims=True)
        dx_ref[...] = _rms_bwd(dy, gv, r, xhat)

    return pl.pallas_call(
        body, name="loss_fwd_bwd", grid=(L // tm,),
        in_specs=[_rows(tm, D_MODEL), _whole(), _rows(tm, D_MODEL)],
        out_specs=[pl.BlockSpec((8, 128), lambda i: (0, 0)), _rows(tm, D_MODEL), _acc_row(D_MODEL)],
        out_shape=[jax.ShapeDtypeStruct((8, 128), F32), jax.ShapeDtypeStruct((L, D_MODEL), F32),
                   jax.ShapeDtypeStruct((1, D_MODEL), F32)],
        compiler_params=_params(),
    )(x, g, target)


_C_K = ATTN_WIDTH
_C_V = 2 * ATTN_WIDTH
_C_U = 3 * ATTN_WIDTH
_C_G = _C_U + SSM_WIDTH


def _residue_spec(d, tm):
    return pl.BlockSpec((d, tm // d, GROUP_WIDTH), lambda i: (0, i, 0))


def _residue_shape(d, L, dtype):
    return jax.ShapeDtypeStruct((d, L // d, GROUP_WIDTH), dtype)


def _residue_scratch(tm):
    return pltpu.VMEM((GROUP_WIDTH // LANES, tm, LANES), F32)


def _to_residues(val, out_ref, scr, d):
    if d == 1:
        out_ref[0] = val.astype(out_ref.dtype)
        return
    tm = val.shape[0]
    for half in range(GROUP_WIDTH // LANES):
        cols = slice(half * LANES, (half + 1) * LANES)
        scr[half] = val[:, cols]
        for r in range(d):
            out_ref[r, :, cols] = scr[half, pl.ds(r, tm // d, stride=d), :].astype(out_ref.dtype)


def _from_residues(ref, scr, d):
    if d == 1:
        return ref[0].astype(F32)
    rows = ref.shape[1]
    for half in range(GROUP_WIDTH // LANES):
        cols = slice(half * LANES, (half + 1) * LANES)
        for r in range(d):
            scr[half, pl.ds(r, rows, stride=d), :] = ref[r, :, cols].astype(F32)
    return jnp.concatenate([scr[half] for half in range(GROUP_WIDTH // LANES)], axis=1)


def _mix_in_fwd(x, g, w_in, gate_bias):
    L = x.shape[0]
    tm = min(ROW_TILE, L)

    def body(x_ref, g_ref, w_ref, gb_ref, *refs):
        qkv_refs, (u_ref, gate_ref, scr) = refs[:9], refs[9:]
        r, xhat = _rms(x_ref[...])
        h = (xhat * g_ref[...]).astype(BF16)
        for part, (c0, scale) in enumerate(((0, Q_SCALE), (_C_K, 1.0), (_C_V, 1.0))):
            z = _dot_nt(h, w_ref[c0:c0 + ATTN_WIDTH, :]) * scale
            for grp, d in enumerate(DILATIONS):
                _to_residues(z[:, grp * GROUP_WIDTH:(grp + 1) * GROUP_WIDTH], qkv_refs[3 * part + grp], scr, d)
        u_ref[...] = _dot_nt(h, w_ref[_C_U:_C_G, :])
        gate_ref[...] = _sigmoid(_dot_nt(h, w_ref[_C_G:IN_WIDTH, :]) + gb_ref[...])

    return pl.pallas_call(
        body, name="mix_in_fwd", grid=(L // tm,),
        in_specs=[_rows(tm, D_MODEL), _whole(), _whole(), _whole()],
        out_specs=[_residue_spec(d, tm) for d in DILATIONS] * 3 + [_rows(tm, SSM_WIDTH), _rows(tm, 2 * D_MODEL)],
        out_shape=[_residue_shape(d, L, BF16) for d in DILATIONS] * 3
        + [jax.ShapeDtypeStruct((L, SSM_WIDTH), F32), jax.ShapeDtypeStruct((L, 2 * D_MODEL), F32)],
        scratch_shapes=[_residue_scratch(tm)],
        compiler_params=_params(),
    )(x, g, w_in, gate_bias)


def _mix_in_bwd(dx2, x, g, dqkv, du, dgp, w_in):
    L = x.shape[0]
    tm = min(ROW_TILE, L)

    def body(dx2_ref, x_ref, g_ref, *refs):
        piece_refs = refs[:9]
        du_ref, dgp_ref, w_ref, dx1_ref, h_ref, dz_ref, dg_ref, scr = refs[9:]
        i = pl.program_id(0)
        gv = g_ref[...]
        r, xhat = _rms(x_ref[...])
        h_ref[...] = (xhat * gv).astype(BF16)
        for part in range(3):
            for grp, d in enumerate(DILATIONS):
                c0 = part * ATTN_WIDTH + grp * GROUP_WIDTH
                dz_ref[:, c0:c0 + GROUP_WIDTH] = _from_residues(piece_refs[3 * part + grp], scr, d).astype(BF16)
        dz_ref[:, _C_U:_C_G] = du_ref[...].astype(BF16)
        dz_ref[:, _C_G:IN_WIDTH] = dgp_ref[...]
        dh = _dot(dz_ref[...], w_ref[...])

        @pl.when(i == 0)
        def _():
            dg_ref[...] = jnp.zeros_like(dg_ref)

        dg_ref[...] += jnp.sum(dh * xhat, axis=0, keepdims=True)
        dx1_ref[...] = dx2_ref[...] + _rms_bwd(dh, gv, r, xhat)

    return pl.pallas_call(
        body, name="mix_in_bwd", grid=(L // tm,),
        in_specs=[_rows(tm, D_MODEL), _rows(tm, D_MODEL), _whole()] + [_residue_spec(d, tm) for d in DILATIONS] * 3
        + [_rows(tm, SSM_WIDTH), _rows(tm, 2 * D_MODEL), _whole()],
        out_specs=[_rows(tm, D_MODEL), _rows(tm, D_MODEL), _rows(tm, IN_WIDTH), _acc_row(D_MODEL)],
        out_shape=[jax.ShapeDtypeStruct((L, D_MODEL), F32), jax.ShapeDtypeStruct((L, D_MODEL), BF16),
                   jax.ShapeDtypeStruct((L, IN_WIDTH), BF16), jax.ShapeDtypeStruct((1, D_MODEL), F32)],
        scratch_shapes=[_residue_scratch(tm)],
        compiler_params=_params(),
    )(dx2, x, g, *dqkv, du, dgp, w_in)


def _bucket_onehot():
    qi = jnp.arange(ATTN_BLOCK)[:, None]
    kj = jnp.arange(2 * ATTN_BLOCK)[None, :]
    steps = jnp.maximum(qi + ATTN_BLOCK - kj, 0)
    max_exact = N_BUCKETS // 2
    out = []
    for d in DILATIONS:
        dist = steps * d
        df = jnp.maximum(dist, 1).astype(F32)
        large = max_exact + (jnp.log(df / max_exact) / math.log(MAX_DISTANCE / max_exact)
                             * (N_BUCKETS - max_exact)).astype(jnp.int32)
        large = jnp.minimum(large, N_BUCKETS - 1)
        bucket = jnp.where(dist < max_exact, dist, large).reshape(-1)
        out.append((bucket[None, :] == jnp.arange(N_BUCKETS)[:, None]).astype(F32))
    return jnp.stack(out)


def _bias_expand(table_t, onehot):
    n = onehot.shape[-1]

    def body(t_ref, oh_ref, o_ref):
        bias = _dot_exact(t_ref[...], oh_ref[...])
        col = lax.broadcasted_iota(jnp.int32, (8, n), 1)
        qi = col // (2 * ATTN_BLOCK)
        kj = col - qi * (2 * ATTN_BLOCK)
        steps = qi + ATTN_BLOCK - kj
        band = (steps >= 0) & (steps <= WINDOW_STEPS)
        o_ref[0] = jnp.where(band & (kj >= ATTN_BLOCK), bias, NEG_INF)
        o_ref[1] = jnp.where(band, bias, NEG_INF)

    return pl.pallas_call(
        body, name="bias_expand", grid=(3,),
        in_specs=[pl.BlockSpec((None, 8, N_BUCKETS), lambda g: (g, 0, 0)),
                  pl.BlockSpec((None, N_BUCKETS, n), lambda g: (g, 0, 0))],
        out_specs=pl.BlockSpec((None, 2, 8, n), lambda g: (g, 0, 0, 0)),
        out_shape=jax.ShapeDtypeStruct((3, 2, 8, n), F32),
        compiler_params=_params(),
    )(table_t, onehot)


def _bias_reduce(dsum, onehot):
    n = onehot.shape[-1]

    def body(d_ref, oh_ref, o_ref):
        o_ref[...] = _dot_nt_exact(d_ref[...], oh_ref[...])

    return pl.pallas_call(
        body, name="bias_reduce", grid=(3,),
        in_specs=[pl.BlockSpec((None, 8, n), lambda g: (g, 0, 0)),
                  pl.BlockSpec((None, N_BUCKETS, n), lambda g: (g, 0, 0))],
        out_specs=pl.BlockSpec((None, 8, N_BUCKETS), lambda g: (g, 0, 0)),
        out_shape=jax.ShapeDtypeStruct((3, 8, N_BUCKETS), F32),
        compiler_params=_params(),
    )(dsum, onehot)


def _head_of_col(rows):
    return lax.broadcasted_iota(jnp.int32, (rows, GROUP_WIDTH), 1) // HEAD_DIM


_STACK_ROWS = HEADS_PER_GROUP * ATTN_BLOCK


def _stack_heads(x, head_of_col):
    return jnp.concatenate([jnp.where(head_of_col == hh, x, jnp.zeros_like(x)) for hh in range(HEADS_PER_GROUP)],
                           axis=0)


def _attn_specs(qb):
    rows = qb * ATTN_BLOCK
    cur = pl.BlockSpec((None, rows, GROUP_WIDTH), lambda r, n: (r, n, 0))
    prev = pl.BlockSpec((None, ATTN_BLOCK, GROUP_WIDTH), lambda r, n: (r, jnp.maximum(n * qb - 1, 0), 0))
    bias = pl.BlockSpec((2, HEADS_PER_GROUP, ATTN_BLOCK, 2 * ATTN_BLOCK), lambda r, n: (0, 0, 0, 0))
    return cur, prev, bias


def _attn_fwd(q, k, v, bias, name):
    d, M, _ = q.shape
    nb = M // ATTN_BLOCK
    qb = min(ATTN_QB, nb)

    def body(q_ref, kp_ref, kc_ref, vp_ref, vc_ref, bias_ref, o_ref, lse_ref):
        n = pl.program_id(1)
        q_head = _head_of_col(ATTN_BLOCK)
        kwin = jnp.concatenate([kp_ref[...], kc_ref[...]], axis=0)
        vwin = jnp.concatenate([vp_ref[...], vc_ref[...]], axis=0)
        ones = jnp.ones((2 * ATTN_BLOCK, LANES), BF16)
        for b in range(qb):
            rows = slice(b * ATTN_BLOCK, (b + 1) * ATTN_BLOCK)
            window = slice(b * ATTN_BLOCK, (b + 2) * ATTN_BLOCK)
            variant = jnp.minimum(n, 1) if b == 0 else 1
            kk = kwin[window]
            vv = vwin[window]
            q4 = _stack_heads(q_ref[rows, :], q_head)
            logits = _dot_nt(q4, kk) + bias_ref[variant].reshape(_STACK_ROWS, 2 * ATTN_BLOCK)
            m = jnp.max(logits, axis=1, keepdims=True)
            p16 = jnp.exp(logits - m).astype(BF16)
            den = _dot(p16, ones)[:, 0:1]
            out = _dot(p16, vv) * (1.0 / den)
            lse = m + jnp.log(den)
            o_acc = jnp.zeros((ATTN_BLOCK, GROUP_WIDTH), F32)
            lse_acc = jnp.zeros((ATTN_BLOCK, GROUP_WIDTH), F32)
            for hh in range(HEADS_PER_GROUP):
                head_rows = slice(hh * ATTN_BLOCK, (hh + 1) * ATTN_BLOCK)
                o_acc = jnp.where(q_head == hh, out[head_rows], o_acc)
                lse_acc = jnp.where(q_head == hh, lse[head_rows], lse_acc)
            o_ref[rows, :] = o_acc
            lse_ref[rows, :] = lse_acc

    cur, prev, full = _attn_specs(qb)
    return pl.pallas_call(
        body, name=name, grid=(d, nb // qb),
        in_specs=[cur, prev, cur, prev, cur, full],
        out_specs=[cur, cur],
        out_shape=[jax.ShapeDtypeStruct((d, M, GROUP_WIDTH), F32)] * 2,
        compiler_params=_params(),
    )(q, k, k, v, v, bias)


def _attn_bwd(q, k, v, do, lse, delta, bias, name):
    d, M, _ = q.shape
    nb = M // ATTN_BLOCK
    qb = min(ATTN_QB, nb)
    ns = nb // qb
    rows_q = qb * ATTN_BLOCK
    last = slice(rows_q - ATTN_BLOCK, rows_q)

    def body(q_ref, kp_ref, kc_ref, vp_ref, vc_ref, do_ref, lse_ref, dl_ref, bias_ref,
             dq_ref, dk_ref, dv_ref, dsum_ref, pk_ref, pv_ref, wk_ref, wv_ref):
        r = pl.program_id(0)
        n = pl.program_id(1)

        @pl.when((r == 0) & (n == 0))
        def _():
            dsum_ref[...] = jnp.zeros_like(dsum_ref)

        @pl.when(n == 0)
        def _():
            pk_ref[...] = jnp.zeros_like(pk_ref)
            pv_ref[...] = jnp.zeros_like(pv_ref)

        @pl.when(n < ns)
        def _():
            q_head = _head_of_col(ATTN_BLOCK)
            kwin = jnp.concatenate([kp_ref[...], kc_ref[...]], axis=0)
            vwin = jnp.concatenate([vp_ref[...], vc_ref[...]], axis=0)
            wk_ref[...] = jnp.zeros_like(wk_ref)
            wv_ref[...] = jnp.zeros_like(wv_ref)
            for b in range(qb):
                rows = slice(b * ATTN_BLOCK, (b + 1) * ATTN_BLOCK)
                window = slice(b * ATTN_BLOCK, (b + 2) * ATTN_BLOCK)
                variant = jnp.minimum(n, 1) if b == 0 else 1
                kk = kwin[window]
                vv = vwin[window]
                q4 = _stack_heads(q_ref[rows, :], q_head)
                do4 = _stack_heads(do_ref[rows, :], q_head)
                heads = [hh * HEAD_DIM for hh in range(HEADS_PER_GROUP)]
                lse4 = jnp.concatenate([lse_ref[rows, c0:c0 + 1] for c0 in heads], axis=0)
                dl4 = jnp.concatenate([dl_ref[rows, c0:c0 + 1] for c0 in heads], axis=0)
                logits = _dot_nt(q4, kk) + bias_ref[variant].reshape(_STACK_ROWS, 2 * ATTN_BLOCK)
                p = jnp.exp(logits - lse4)
                ds = p * (_dot_nt(do4, vv) - dl4)
                dsum_ref[...] += ds.reshape(HEADS_PER_GROUP, ATTN_BLOCK, 2 * ATTN_BLOCK)
                ds16 = ds.astype(BF16)
                dq4 = _dot(ds16, kk)
                dq_acc = jnp.zeros((ATTN_BLOCK, GROUP_WIDTH), F32)
                for hh in range(HEADS_PER_GROUP):
                    dq_acc = jnp.where(q_head == hh, dq4[hh * ATTN_BLOCK:(hh + 1) * ATTN_BLOCK], dq_acc)
                dq_ref[rows, :] = (dq_acc * Q_SCALE).astype(BF16)
                wk_ref[window, :] += _dot_tn(ds16, q4)
                wv_ref[window, :] += _dot_tn(p.astype(BF16), do4)
            for out_ref, part_ref, win_ref in ((dk_ref, pk_ref, wk_ref), (dv_ref, pv_ref, wv_ref)):
                if qb > 1:
                    out_ref[0:rows_q - ATTN_BLOCK, :] = part_ref[0:rows_q - ATTN_BLOCK, :].astype(BF16)
                out_ref[last, :] = (part_ref[last, :] + win_ref[0:ATTN_BLOCK, :]).astype(BF16)
                part_ref[...] = win_ref[ATTN_BLOCK:, :]

        @pl.when(n == ns)
        def _():
            dk_ref[...] = pk_ref[...].astype(BF16)
            dv_ref[...] = pv_ref[...].astype(BF16)

    def clamp(n):
        return jnp.minimum(n, ns - 1)

    cur = pl.BlockSpec((None, rows_q, GROUP_WIDTH), lambda r, n: (r, clamp(n), 0))
    prev = pl.BlockSpec((None, ATTN_BLOCK, GROUP_WIDTH), lambda r, n: (r, jnp.maximum(clamp(n) * qb - 1, 0), 0))
    lag = pl.BlockSpec((None, rows_q, GROUP_WIDTH), lambda r, n: (r, jnp.maximum(n - 1, 0), 0))
    full = pl.BlockSpec((2, HEADS_PER_GROUP, ATTN_BLOCK, 2 * ATTN_BLOCK), lambda r, n: (0, 0, 0, 0))
    acc = pl.BlockSpec((HEADS_PER_GROUP, ATTN_BLOCK, 2 * ATTN_BLOCK), lambda r, n: (0, 0, 0))
    return pl.pallas_call(
        body, name=name, grid=(d, ns + 1),
        in_specs=[cur, prev, cur, prev, cur, cur, cur, cur, full],
        out_specs=[cur, lag, lag, acc],
        out_shape=[jax.ShapeDtypeStruct((d, M, GROUP_WIDTH), BF16)] * 3
        + [jax.ShapeDtypeStruct((HEADS_PER_GROUP, ATTN_BLOCK, 2 * ATTN_BLOCK), F32)],
        scratch_shapes=[pltpu.VMEM((rows_q, GROUP_WIDTH), F32), pltpu.VMEM((rows_q, GROUP_WIDTH), F32),
                        pltpu.VMEM((rows_q + ATTN_BLOCK, GROUP_WIDTH), F32),
                        pltpu.VMEM((rows_q + ATTN_BLOCK, GROUP_WIDTH), F32)],
        compiler_params=_params(),
    )(q, k, k, v, v, do, lse, delta, bias)


def _disc_math(a_re, a_im, ldt, b_re, b_im):
    dt = jnp.exp(ldt)
    mag = jnp.exp(a_re * dt)
    ab_re = mag * jnp.cos(a_im * dt)
    ab_im = mag * jnp.sin(a_im * dt)
    den = a_re * a_re + a_im * a_im
    xr = ab_re - 1.0
    coef_re = (xr * a_re + ab_im * a_im) / den
    coef_im = (ab_im * a_re - xr * a_im) / den
    return ab_re, ab_im, coef_re * b_re - coef_im * b_im, coef_re * b_im + coef_im * b_re


def _block_diag_mask():
    row_g = lax.broadcasted_iota(jnp.int32, (SSM_WIDTH, 2 * NS), 0) // SSM_GROUP
    col = lax.broadcasted_iota(jnp.int32, (SSM_WIDTH, 2 * NS), 1)
    col_g = jnp.where(col >= NS, col - NS, col) // SSM_STATE
    return row_g == col_g


def _disc_fwd(a_re, a_im, ldt, b_re, b_im, c_re, c_im):
    def body(are_ref, aim_ref, ldt_ref, bre_ref, bim_ref, cre_ref, cim_ref, pw_ref, pwr_ref, bd_ref, cdt_ref):
        ab_re, ab_im, bb_re, bb_im = _disc_math(are_ref[...], aim_ref[...], ldt_ref[...], bre_ref[...], bim_ref[...])
        row = lax.broadcasted_iota(jnp.int32, (8, NS), 0)
        pr, pi = ab_re, ab_im
        t_re = jnp.zeros((8, NS), F32)
        t_im = jnp.zeros((8, NS), F32)
        u_re = jnp.zeros((8, NS), F32)
        u_im = jnp.zeros((8, NS), F32)
        for j in range(8):
            t_re = jnp.where(row == j, pr, t_re)
            t_im = jnp.where(row == j, pi, t_im)
            u_re = jnp.where(row == 7 - j, pr, u_re)
            u_im = jnp.where(row == 7 - j, pi, u_im)
            pr, pi = pr * ab_re - pi * ab_im, pr * ab_im + pi * ab_re
        pw_ref[0] = t_re
        pw_ref[1] = t_im
        pwr_ref[0] = u_re
        pwr_ref[1] = u_im
        mask = _block_diag_mask()
        zero = jnp.zeros((SSM_WIDTH, 2 * NS), F32)
        bfull = jnp.concatenate([jnp.concatenate([bb_re] * SSM_GROUPS, axis=0),
                                 jnp.concatenate([bb_im] * SSM_GROUPS, axis=0)], axis=1)
        bd_ref[...] = jnp.where(mask, bfull, zero).astype(BF16)
        cfull = jnp.concatenate([jnp.concatenate([cre_ref[...]] * SSM_GROUPS, axis=0),
                                 jnp.concatenate([-cim_ref[...]] * SSM_GROUPS, axis=0)], axis=1)
        cdt_ref[...] = jnp.where(mask, cfull, zero).astype(BF16)

    return pl.pallas_call(
        body, name="s5_disc_fwd",
        in_specs=[_whole()] * 7, out_specs=[_whole()] * 4,
        out_shape=[jax.ShapeDtypeStruct((2, 8, NS), F32), jax.ShapeDtypeStruct((2, 8, NS), F32),
                   jax.ShapeDtypeStruct((SSM_WIDTH, 2 * NS), BF16), jax.ShapeDtypeStruct((SSM_WIDTH, 2 * NS), BF16)],
        compiler_params=_params(),
    )(a_re, a_im, ldt, b_re, b_im, c_re, c_im)


def _disc_bwd(a_re, a_im, ldt, b_re, b_im, d_bd, d_cdt, d_ab, group_sum):
    def body(are_ref, aim_ref, ldt_ref, bre_ref, bim_ref, dbd_ref, dcdt_ref, dab_ref, gs_ref,
             dare_ref, daim_ref, dldt_ref, dbre_ref, dbim_ref, dcre_ref, dcim_ref):
        col = lax.broadcasted_iota(jnp.int32, (SSM_GROUP, 2 * NS), 1)
        col_g = jnp.where(col >= NS, col - NS, col) // SSM_STATE
        acc_b = jnp.zeros((SSM_GROUP, 2 * NS), F32)
        acc_c = jnp.zeros((SSM_GROUP, 2 * NS), F32)
        for g in range(SSM_GROUPS):
            rows = slice(g * SSM_GROUP, (g + 1) * SSM_GROUP)
            acc_b = acc_b + jnp.where(col_g == g, dbd_ref[rows, :], 0.0)
            acc_c = acc_c + jnp.where(col_g == g, dcdt_ref[rows, :], 0.0)
        dcre_ref[...] = acc_c[:, :NS]
        dcim_ref[...] = -acc_c[:, NS:]
        dab_re = jnp.sum(dab_ref[0], axis=0, keepdims=True)
        dab_im = jnp.sum(dab_ref[1], axis=0, keepdims=True)
        _, vjp = jax.vjp(_disc_math, are_ref[...], aim_ref[...], ldt_ref[...], bre_ref[...], bim_ref[...])
        d_are, d_aim, d_ldt, d_bre, d_bim = vjp((dab_re, dab_im, acc_b[:, :NS], acc_b[:, NS:]))
        dare_ref[...] = d_are
        daim_ref[...] = d_aim
        dbre_ref[...] = d_bre
        dbim_ref[...] = d_bim
        dldt_ref[...] = _dot_exact(jnp.broadcast_to(d_ldt, (8, NS)), gs_ref[...])

    vec = jax.ShapeDtypeStruct((1, NS), F32)
    mat = jax.ShapeDtypeStruct((SSM_GROUP, NS), F32)
    return pl.pallas_call(
        body, name="s5_disc_bwd",
        in_specs=[_whole()] * 9, out_specs=[_whole()] * 7,
        out_shape=[vec, vec, jax.ShapeDtypeStruct((8, 128), F32), mat, mat, mat, mat],
        compiler_params=_params(),
    )(a_re, a_im, ldt, b_re, b_im, d_bd, d_cdt, d_ab, group_sum)


def _scan_blocks(buf, pw_ref, carry_ref, n_blocks, reverse):
    row = lax.broadcasted_iota(jnp.int32, (8, SCAN_LANES), 0)
    for lc in range(NS // SCAN_LANES):
        re_cols = pl.ds(lc * SCAN_LANES, SCAN_LANES)
        im_cols = pl.ds(NS + lc * SCAN_LANES, SCAN_LANES)
        pr = pw_ref[0, :, re_cols]
        pi = pw_ref[1, :, re_cols]
        if reverse:
            pi = -pi
            base = [(7, 1), (6, 2), (4, 4)]
            coef = [(jnp.where(row < 8 - k, pr[j:j + 1], 0.0), jnp.where(row < 8 - k, pi[j:j + 1], 0.0), 8 - k)
                    for j, k in base]
        else:
            base = [(0, 1), (1, 2), (3, 4)]
            coef = [(jnp.where(row >= k, pr[j:j + 1], 0.0), jnp.where(row >= k, pi[j:j + 1], 0.0), k)
                    for j, k in base]

        def step(i, carry, pr=pr, pi=pi, coef=coef, re_cols=re_cols, im_cols=im_cols):
            cr, ci = carry
            blk = (n_blocks - 1 - i) if reverse else i
            rows = pl.ds(pl.multiple_of(blk * 8, 8), 8)
            xr = buf[rows, re_cols]
            xi = buf[rows, im_cols]
            for kr, ki, shift in coef:
                sr = pltpu.roll(xr, shift, 0)
                si = pltpu.roll(xi, shift, 0)
                xr, xi = xr + kr * sr - ki * si, xi + kr * si + ki * sr
            xr, xi = xr + pr * cr - pi * ci, xi + pr * ci + pi * cr
            buf[rows, re_cols] = xr
            buf[rows, im_cols] = xi
            edge = slice(0, 1) if reverse else slice(7, 8)
            return xr[edge], xi[edge]

        cr, ci = lax.fori_loop(0, n_blocks, step, (carry_ref[0:1, re_cols], carry_ref[0:1, im_cols]))
        carry_ref[0:1, re_cols] = cr
        carry_ref[0:1, im_cols] = ci


_SUPER_GROUPS = 16
_SUPER_BLOCKS = [
    (slice(k * _SUPER_GROUPS * SSM_GROUP, (k + 1) * _SUPER_GROUPS * SSM_GROUP),
     [slice(half + k * _SUPER_GROUPS * SSM_STATE, half + (k + 1) * _SUPER_GROUPS * SSM_STATE) for half in (0, NS)])
    for k in range(SSM_GROUPS // _SUPER_GROUPS)]


def _ssm_fwd(u, bd, cdt, d_skip, pw):
    L = u.shape[0]
    tc = min(SSM_FWD_CHUNK, L)

    def body(u_ref, bd_ref, cdt_ref, dsk_ref, pw_ref, y_ref, s_ref, carry_ref):
        @pl.when(pl.program_id(0) == 0)
        def _():
            carry_ref[...] = jnp.zeros_like(carry_ref)

        uv = u_ref[...]
        u16 = uv.astype(BF16)
        for ch, states in _SUPER_BLOCKS:
            for st in states:
                s_ref[:, st] = _dot(u16[:, ch], bd_ref[ch, st])
        _scan_blocks(s_ref, pw_ref, carry_ref, tc // 8, reverse=False)
        for ch, states in _SUPER_BLOCKS:
            y_ref[:, ch] = (sum(_dot_nt(s_ref[:, st].astype(BF16), cdt_ref[ch, st]) for st in states)
                            + dsk_ref[:, ch] * uv[:, ch])

    return pl.pallas_call(
        body, name="s5_fwd", grid=(L // tc,),
        in_specs=[_rows(tc, SSM_WIDTH), _whole(), _whole(), _whole(), _whole()],
        out_specs=[_rows(tc, SSM_WIDTH), _rows(tc, 2 * NS)],
        out_shape=[jax.ShapeDtypeStruct((L, SSM_WIDTH), F32), jax.ShapeDtypeStruct((L, 2 * NS), F32)],
        scratch_shapes=[pltpu.VMEM((8, 2 * NS), F32)],
        compiler_params=_params(),
    )(u, bd, cdt, d_skip, pw)


def _ssm_bwd(dy, u, s, bd, cdt, d_skip, pwr):
    L = u.shape[0]
    tc = min(SSM_CHUNK, L)
    nc = L // tc
    blocks = tc // 8

    def body(dy_ref, u_ref, s_ref, sprev_ref, bd_ref, cdt_ref, dsk_ref, pwr_ref,
             du_ref, ddsk_ref, dbd_ref, dcdt_ref, dab_ref, g_ref, sx_ref, carry_ref):
        i = pl.program_id(0)

        @pl.when(i == 0)
        def _():
            carry_ref[...] = jnp.zeros_like(carry_ref)
            ddsk_ref[...] = jnp.zeros_like(ddsk_ref)
            dbd_ref[...] = jnp.zeros_like(dbd_ref)
            dcdt_ref[...] = jnp.zeros_like(dcdt_ref)
            dab_ref[...] = jnp.zeros_like(dab_ref)

        dyv = dy_ref[...]
        uv = u_ref[...]
        dy16 = dyv.astype(BF16)
        u16 = uv.astype(BF16)
        for ch, states in _SUPER_BLOCKS:
            for st in states:
                g_ref[:, st] = _dot(dy16[:, ch], cdt_ref[ch, st])
        _scan_blocks(g_ref, pwr_ref, carry_ref, blocks, reverse=True)
        ddsk_ref[...] += jnp.sum(dyv * uv, axis=0, keepdims=True)
        for ch, states in _SUPER_BLOCKS:
            du = dsk_ref[:, ch] * dyv[:, ch]
            for st in states:
                g16 = g_ref[:, st].astype(BF16)
                du = du + _dot_nt(g16, bd_ref[ch, st])
                dbd_ref[ch, st] += _dot_tn(u16[:, ch], g16)
                dcdt_ref[ch, st] += _dot_tn(dy16[:, ch], s_ref[:, st].astype(BF16))
            du_ref[:, ch] = du

        sx_ref[pl.ds(8, tc), :] = s_ref[...]
        sx_ref[pl.ds(0, 8), :] = jnp.where(i == nc - 1, 0.0, sprev_ref[...])
        row = lax.broadcasted_iota(jnp.int32, (8, SCAN_LANES), 0)
        for lc in range(NS // SCAN_LANES):
            re_cols = pl.ds(lc * SCAN_LANES, SCAN_LANES)
            im_cols = pl.ds(NS + lc * SCAN_LANES, SCAN_LANES)

            def step(b, acc, re_cols=re_cols, im_cols=im_cols):
                ar, ai = acc
                off = pl.multiple_of(b * 8, 8)
                gr = g_ref[pl.ds(off, 8), re_cols]
                gi = g_ref[pl.ds(off, 8), im_cols]
                before = pl.ds(off, 8)
                here = pl.ds(off + 8, 8)
                sr = jnp.where(row == 0, sx_ref[before, re_cols][7:8], pltpu.roll(sx_ref[here, re_cols], 1, 0))
                si = jnp.where(row == 0, sx_ref[before, im_cols][7:8], pltpu.roll(sx_ref[here, im_cols], 1, 0))
                return ar + gr * sr + gi * si, ai + gi * sr - gr * si

            zero = jnp.zeros((8, SCAN_LANES), F32)
            ar, ai = lax.fori_loop(0, blocks, step, (zero, zero))
            dab_ref[0, :, re_cols] += ar
            dab_ref[1, :, re_cols] += ai

    rev = lambda i: (nc - 1 - i, 0)
    sprev = pl.BlockSpec((8, 2 * NS), lambda i: (jnp.maximum((nc - 1 - i) * blocks - 1, 0), 0))
    return pl.pallas_call(
        body, name="s5_bwd", grid=(nc,),
        in_specs=[pl.BlockSpec((tc, SSM_WIDTH), rev), pl.BlockSpec((tc, SSM_WIDTH), rev),
                  pl.BlockSpec((tc, 2 * NS), rev), sprev, _whole(), _whole(), _whole(), _whole()],
        out_specs=[pl.BlockSpec((tc, SSM_WIDTH), rev), _whole(), _whole(), _whole(), _whole()],
        out_shape=[jax.ShapeDtypeStruct((L, SSM_WIDTH), F32), jax.ShapeDtypeStruct((1, SSM_WIDTH), F32),
                   jax.ShapeDtypeStruct((SSM_WIDTH, 2 * NS), F32), jax.ShapeDtypeStruct((SSM_WIDTH, 2 * NS), F32),
                   jax.ShapeDtypeStruct((2, 8, NS), F32)],
        scratch_shapes=[pltpu.VMEM((tc, 2 * NS), F32), pltpu.VMEM((tc + 8, 2 * NS), F32), pltpu.VMEM((8, 2 * NS), F32)],
        compiler_params=_params(),
    )(dy, u, s, s, bd, cdt, d_skip, pwr)


def _branches(o_attn, y, gates, w_ab, w_glu, w_sb):
    ya = _dot(o_attn.astype(BF16), w_ab[...])
    gel = _gelu(y)
    glu = _dot(gel.astype(BF16), w_glu[...])
    p = glu[:, :SSM_WIDTH]
    sg = _sigmoid(glu[:, SSM_WIDTH:])
    ys2 = p * sg
    ysb = _dot(ys2.astype(BF16), w_sb[...])
    ga = gates[:, :D_MODEL]
    gs = gates[:, D_MODEL:]
    return ya, gel, p, sg, ys2, ysb, ga, gs


def _mix_out_fwd(x1, o_g, lse_g, y, gates, w_ab, w_glu, w_sb, w_out):
    L = x1.shape[0]
    tm = min(ROW_TILE, L)

    def body(x_ref, o0, o1, o2, l0, l1, l2, y_ref, gate_ref, wab_ref, wglu_ref, wsb_ref, wout_ref,
             x2_ref, oat_ref, lse0, lse1, lse2, scr):
        la, lb, lc = (_from_residues(ref, scr, d) for ref, d in zip((l0, l1, l2), DILATIONS))
        m = jnp.maximum(jnp.maximum(la, lb), lc)
        ea, eb, ec = jnp.exp(la - m), jnp.exp(lb - m), jnp.exp(lc - m)
        tot = ea + eb + ec
        oa, ob, oc = (_from_residues(ref, scr, d) for ref, d in zip((o0, o1, o2), DILATIONS))
        o_attn = (ea * oa + eb * ob + ec * oc) / tot
        oat_ref[...] = o_attn
        lse = m + jnp.log(tot)
        for ref, d in zip((lse0, lse1, lse2), DILATIONS):
            _to_residues(lse, ref, scr, d)
        ya, _, _, _, _, ysb, ga, gs = _branches(o_attn, y_ref[...], gate_ref[...], wab_ref, wglu_ref, wsb_ref)
        mix = ga * ya + gs * ysb
        x2_ref[...] = x_ref[...] + _dot(mix.astype(BF16), wout_ref[...])

    res = [_residue_spec(d, tm) for d in DILATIONS]
    return pl.pallas_call(
        body, name="mix_out_fwd", grid=(L // tm,),
        in_specs=[_rows(tm, D_MODEL)] + res * 2 + [_rows(tm, SSM_WIDTH), _rows(tm, 2 * D_MODEL)] + [_whole()] * 4,
        out_specs=[_rows(tm, D_MODEL), _rows(tm, GROUP_WIDTH)] + res,
        out_shape=[jax.ShapeDtypeStruct((L, D_MODEL), F32), jax.ShapeDtypeStruct((L, GROUP_WIDTH), F32)]
        + [_residue_shape(d, L, F32) for d in DILATIONS],
        scratch_shapes=[_residue_scratch(tm)],
        compiler_params=_params(),
    )(x1, *o_g, *lse_g, y, gates, w_ab, w_glu, w_sb, w_out)


def _mix_out_bwd(dx2, o_attn, y, gates, w_ab, w_glu, w_sb, w_out, head_sum):
    L = dx2.shape[0]
    tm = min(ROW_TILE, L)

    def body(dx_ref, oat_ref, y_ref, gate_ref, wab_ref, wglu_ref, wsb_ref, wout_ref, hs_ref,
             do0, do1, do2, dl0, dl1, dl2, dy_ref, dgp_ref, mix_ref, dya_ref, dys_ref, ys2_ref, gel_ref, dglu_ref,
             dgb_ref, scr):
        i = pl.program_id(0)
        o_attn = oat_ref[...]
        yv = y_ref[...]
        ya, gel, p, sg, ys2, ysb, ga, gs = _branches(o_attn, yv, gate_ref[...], wab_ref, wglu_ref, wsb_ref)
        mix_ref[...] = (ga * ya + gs * ysb).astype(BF16)
        ys2_ref[...] = ys2.astype(BF16)
        gel_ref[...] = gel.astype(BF16)
        dmix = _dot_nt(dx_ref[...].astype(BF16), wout_ref[...])
        dgp = jnp.concatenate([dmix * ya * ga * (1.0 - ga), dmix * ysb * gs * (1.0 - gs)], axis=1)
        dgp_ref[...] = dgp.astype(BF16)

        @pl.when(i == 0)
        def _():
            dgb_ref[...] = jnp.zeros_like(dgb_ref)

        dgb_ref[...] += jnp.sum(dgp, axis=0, keepdims=True)
        dya = (dmix * ga).astype(BF16)
        dys = (dmix * gs).astype(BF16)
        dya_ref[...] = dya
        dys_ref[...] = dys
        d_o = _dot_nt(dya, wab_ref[...])
        delta = _dot_exact(d_o * o_attn, hs_ref[...])
        for do_ref, dl_ref, d in zip((do0, do1, do2), (dl0, dl1, dl2), DILATIONS):
            _to_residues(d_o, do_ref, scr, d)
            _to_residues(delta, dl_ref, scr, d)
        dys2 = _dot_nt(dys, wsb_ref[...])
        dglu = jnp.concatenate([dys2 * sg, dys2 * p * sg * (1.0 - sg)], axis=1).astype(BF16)
        dglu_ref[...] = dglu
        dy_ref[...] = _dot_nt(dglu, wglu_ref[...]) * _gelu_grad(yv)

    grp = _rows(tm, GROUP_WIDTH)
    wide = _rows(tm, D_MODEL)
    half = _rows(tm, SSM_WIDTH)
    res = [_residue_spec(d, tm) for d in DILATIONS]
    sds = jax.ShapeDtypeStruct
    return pl.pallas_call(
        body, name="mix_out_bwd", grid=(L // tm,),
        in_specs=[wide, grp, half, _rows(tm, 2 * D_MODEL)] + [_whole()] * 5,
        out_specs=res + res + [half, _rows(tm, 2 * D_MODEL), wide, wide, wide, half, half, wide, _acc_row(2 * D_MODEL)],
        out_shape=[_residue_shape(d, L, BF16) for d in DILATIONS] + [_residue_shape(d, L, F32) for d in DILATIONS]
        + [sds((L, SSM_WIDTH), F32),
           sds((L, 2 * D_MODEL), BF16), sds((L, D_MODEL), BF16), sds((L, D_MODEL), BF16),
           sds((L, D_MODEL), BF16), sds((L, SSM_WIDTH), BF16), sds((L, SSM_WIDTH), BF16),
           sds((L, D_MODEL), BF16), sds((1, 2 * D_MODEL), F32)],
        scratch_shapes=[_residue_scratch(tm)],
        compiler_params=_params(),
    )(dx2, o_attn, y, gates, w_ab, w_glu, w_sb, w_out, head_sum)


def _adamw(w, g, m, v, name):
    R, C = w.shape
    tr = _row_tile(R, max(8, ADAMW_BLOCK_BYTES // (4 * C)))

    def body(w_ref, g_ref, m_ref, v_ref, d_ref, mo_ref, vo_ref):
        gv = g_ref[...]
        mn = ADAM_B1 * m_ref[...] + (1.0 - ADAM_B1) * gv
        vn = ADAM_B2 * v_ref[...] + (1.0 - ADAM_B2) * (gv * gv)
        m_hat = mn / (1.0 - ADAM_B1 ** ADAM_STEP)
        v_hat = vn / (1.0 - ADAM_B2 ** ADAM_STEP)
        d_ref[...] = -ADAM_LR * (m_hat / (jnp.sqrt(v_hat) + ADAM_EPS) + ADAM_WD * w_ref[...])
        mo_ref[...] = mn
        vo_ref[...] = vn

    blk = pl.BlockSpec((tr, C), lambda i: (i, 0))
    return pl.pallas_call(
        body, name=name, grid=(R // tr,),
        in_specs=[blk] * 4, out_specs=[blk] * 3,
        out_shape=[jax.ShapeDtypeStruct((R, C), F32)] * 3,
        compiler_params=_params(),
    )(w, g, m, v)


def _sum_chips_into_half(u, t, name):
    S, H, C = u.shape
    tr = _row_tile(H, 512)
    hb = H // tr

    def body(s_ref, t_ref, a_ref, b_ref, c_ref, o_ref):
        me = s_ref[1]
        others = (a_ref[...], b_ref[...], c_ref[...])
        acc = None
        for chip in range(S):
            below = others[min(chip, S - 2)]
            above = others[max(chip - 1, 0)]
            term = jnp.where(me == chip, t_ref[...], jnp.where(me > chip, below, above)).astype(F32)
            acc = term if acc is None else acc + term
        o_ref[...] = acc

    x, y, c = lax.axis_index("x"), lax.axis_index("y"), lax.axis_index("c")
    me = 2 * x + y
    scalars = jnp.stack([c, me] + [j + (j >= me).astype(jnp.int32) for j in range(S - 1)]).astype(jnp.int32)
    blk = (None, tr, C)
    return pl.pallas_call(
        body, name=name,
        grid_spec=pltpu.PrefetchScalarGridSpec(
            num_scalar_prefetch=1, grid=(hb,),
            in_specs=[pl.BlockSpec(blk, lambda i, s: (s[1], i, 0))]
            + [pl.BlockSpec(blk, functools.partial(lambda j, i, s: (s[2 + j], i, 0), j)) for j in range(S - 1)],
            out_specs=pl.BlockSpec((tr, C), lambda i, s: (s[0] * hb + i, 0))),
        out_shape=jax.ShapeDtypeStruct((2 * H, C), F32),
        compiler_params=_params(),
    )(scalars, t, u, u, u)


def _add_halves(g, r1, name):
    S, R, C = g.shape
    H = R // 2
    tr = _row_tile(H, 512)
    hb = H // tr

    def body(c_ref, g_ref, r_ref, o_ref):
        o_ref[...] = (g_ref[...] + r_ref[...]).astype(BF16)

    core = lax.axis_index("c").astype(jnp.int32).reshape(1)
    return pl.pallas_call(
        body, name=name,
        grid_spec=pltpu.PrefetchScalarGridSpec(
            num_scalar_prefetch=1, grid=(S, hb),
            in_specs=[pl.BlockSpec((None, tr, C), lambda j, i, c_ref: (j, c_ref[0] * hb + i, 0)),
                      pl.BlockSpec((None, tr, C), lambda j, i, c_ref: (j, i, 0))],
            out_specs=pl.BlockSpec((None, tr, C), lambda j, i, c_ref: (j, i, 0))),
        out_shape=jax.ShapeDtypeStruct((S, H, C), BF16),
        compiler_params=_params(),
    )(core, g, r1)


_ANY = pl.BlockSpec(memory_space=pl.ANY)


def _place():
    x, y, c = lax.axis_index("x"), lax.axis_index("y"), lax.axis_index("c")
    chips = [(1 - x, y), (x, 1 - y), (1 - x, 1 - y)]
    return x, y, c, chips


def _comm_call(body, name, ins, out_shapes, n_remote, n_local):
    return pl.pallas_call(
        body, name=name,
        in_specs=[_ANY] * len(ins), out_specs=[_ANY] * len(out_shapes), out_shape=out_shapes,
        scratch_shapes=[pltpu.SemaphoreType.DMA((n_remote,)), pltpu.SemaphoreType.DMA((n_remote,)),
                        pltpu.SemaphoreType.DMA((max(n_local, 1),))],
    )(*ins)


def _remote(src, dst, send_sems, recv_sems, k, device):
    return pltpu.make_async_remote_copy(src_ref=src, dst_ref=dst, send_sem=send_sems.at[k], recv_sem=recv_sems.at[k],
                                        device_id=device, device_id_type=MESH)


def _gather_parts(shapes, w_refs, out_refs, send_sems, recv_sems):
    n = len(shapes)
    x, y, c, chips = _place()
    me = 2 * x + y
    sibling = (x, y, 1 - c)

    def half(k, chip_idx, core):
        H = shapes[k][0] // 2
        return out_refs[k].at[chip_idx, pl.ds(core * H, H), :]

    mine = [_remote(w_refs[k], out_refs[k].at[me], send_sems, recv_sems, 6 * n + k, sibling) for k in range(n)]
    first = []
    for k in range(n):
        H = shapes[k][0] // 2
        for j, (cx, cy) in enumerate(chips):
            first.append(_remote(w_refs[k].at[pl.ds(c * H, H), :], half(k, me, c), send_sems, recv_sems,
                                 3 * k + j, (cx, cy, c)))

    def start():
        for cp in mine + first:
            cp.start()

    def finish():
        passed = []
        for k in range(n):
            for j, (cx, cy) in enumerate(chips):
                landed = half(k, 2 * cx + cy, c)
                _remote(landed, landed, send_sems, recv_sems, 3 * k + j, (cx, cy, c)).wait_recv()
                fwd = _remote(landed, landed, send_sems, recv_sems, 3 * n + 3 * k + j, sibling)
                fwd.start()
                passed.append(fwd)
        for k in range(n):
            for j, (cx, cy) in enumerate(chips):
                other = half(k, 2 * cx + cy, 1 - c)
                _remote(other, other, send_sems, recv_sems, 3 * n + 3 * k + j, sibling).wait_recv()
        for cp in mine:
            cp.wait_recv()
        for cp in first + passed + mine:
            cp.wait_send()

    return start, finish


def _gather_weights(shards, name):
    n = len(shards)

    def body(*refs):
        x, y, c, chips = _place()
        _handshake([(x, y, 1 - c)] + [(cx, cy, c) for cx, cy in chips])
        start, finish = _gather_parts([w.shape for w in shards], refs[:n], refs[n:2 * n], *refs[2 * n:2 * n + 2])
        start()
        finish()

    return _sequenced(body, name, shards, [jax.ShapeDtypeStruct((N_SHARD,) + w.shape, w.dtype) for w in shards],
                      7 * n, COLLECTIVE_IDS["gather"])


def _handshake(peers):
    barrier = pltpu.get_barrier_semaphore()
    for peer in peers:
        pl.semaphore_signal(barrier, inc=1, device_id=peer, device_id_type=MESH)
    pl.semaphore_wait(barrier, len(peers))


def _sequenced(body, name, ins, out_shapes, n_sems, collective_id):
    return pl.kernel(
        body, out_type=list(out_shapes), mesh=plsc.ScalarSubcoreMesh(axis_name="sequencer", num_cores=1), name=name,
        scratch_types=(pltpu.SemaphoreType.DMA((n_sems,)), pltpu.SemaphoreType.DMA((n_sems,))),
        compiler_params=pltpu.CompilerParams(collective_id=collective_id))(*ins)


def _swap_halves(gs, name, collective_id):
    n = len(gs)

    def body(*refs):
        g_refs, out_refs = refs[:n], refs[n:2 * n]
        send_sems, recv_sems = refs[2 * n:]
        x, y, c, _ = _place()
        _handshake([(x, y, 1 - c)])
        cps = []
        for k in range(n):
            H = gs[k].shape[1] // 2
            cp = _remote(g_refs[k].at[:, pl.ds((1 - c) * H, H), :], out_refs[k], send_sems, recv_sems, k, (x, y, 1 - c))
            cp.start()
            cps.append(cp)
        for cp in cps:
            cp.wait()

    return _sequenced(body, name, gs, [jax.ShapeDtypeStruct((g.shape[0], g.shape[1] // 2, g.shape[2]), g.dtype)
                                       for g in gs], n, collective_id)


def _exchange_chips(ts, name, collective_id):
    n = len(ts)

    def body(*refs):
        t_refs, out_refs = refs[:n], refs[n:2 * n]
        send_sems, recv_sems = refs[2 * n:]
        x, y, c, chips = _place()
        me = 2 * x + y
        _handshake([(cx, cy, c) for cx, cy in chips])
        sent = []
        for k in range(n):
            for j, (cx, cy) in enumerate(chips):
                cp = _remote(t_refs[k].at[2 * cx + cy], out_refs[k].at[me], send_sems, recv_sems, 3 * k + j, (cx, cy, c))
                cp.start()
                sent.append(cp)
        for k in range(n):
            for j, (cx, cy) in enumerate(chips):
                slot = out_refs[k].at[2 * cx + cy]
                _remote(slot, slot, send_sems, recv_sems, 3 * k + j, (cx, cy, c)).wait_recv()
        for cp in sent:
            cp.wait_send()

    return _sequenced(body, name, ts, [jax.ShapeDtypeStruct(t.shape, t.dtype) for t in ts], 3 * n, collective_id)


def _join_halves(fs, name):
    n = len(fs)

    def body(*refs):
        out_refs = refs[n:2 * n]
        send_sems, recv_sems, _ = refs[2 * n:]
        x, y, c, _ = _place()
        sent = []
        for k in range(n):
            H = fs[k].shape[0] // 2
            here = out_refs[k].at[pl.ds(c * H, H), :]
            cp = _remote(here, here, send_sems, recv_sems, k, (x, y, 1 - c))
            cp.start()
            sent.append(cp)
        for k in range(n):
            H = fs[k].shape[0] // 2
            other = out_refs[k].at[pl.ds((1 - c) * H, H), :]
            _remote(other, other, send_sems, recv_sems, k, (x, y, 1 - c)).wait_recv()
        for cp in sent:
            cp.wait_send()

    return pl.pallas_call(
        body, name=name,
        in_specs=[_ANY] * n, out_specs=[_ANY] * n,
        out_shape=[jax.ShapeDtypeStruct(f.shape, f.dtype) for f in fs],
        input_output_aliases={k: k for k in range(n)},
        scratch_shapes=[pltpu.SemaphoreType.DMA((n,)), pltpu.SemaphoreType.DMA((n,)), pltpu.SemaphoreType.DMA((1,))],
    )(*fs)


def _gather_small(v):
    R, C = v.shape

    def body(v_ref, out_ref, send_sems, recv_sems):
        x, y, c, _ = _place()
        me = 4 * x + 2 * y + c
        flips = [(fx, fy, fc) for fx in (0, 1) for fy in (0, 1) for fc in (0, 1)][1:]
        peers = [((1 - x) if fx else x, (1 - y) if fy else y, (1 - c) if fc else c) for fx, fy, fc in flips]
        _handshake(peers)
        sent = []
        for j, peer in enumerate(peers):
            cp = _remote(v_ref, out_ref.at[me], send_sems, recv_sems, j, peer)
            cp.start()
            sent.append(cp)
        for j, peer in enumerate(peers):
            slot = out_ref.at[4 * peer[0] + 2 * peer[1] + peer[2]]
            _remote(slot, slot, send_sems, recv_sems, j, peer).wait_recv()
        for cp in sent:
            cp.wait_send()

    return _sequenced(body, "gather_small", [v], [jax.ShapeDtypeStruct((8, R, C), F32)], 7,
                      COLLECTIVE_IDS["gather_small"])[0]


def _sum_devices(x, own, name):
    S, R, C = x.shape
    tr = _row_tile(R, 2048)

    def body(s_ref, x_ref, own_ref, o_ref):
        me = s_ref[0]
        acc = None
        for k in range(S):
            term = jnp.where(me == k, own_ref[...], x_ref[k])
            acc = term if acc is None else acc + term
        o_ref[...] = acc

    x_, y_, c_ = lax.axis_index("x"), lax.axis_index("y"), lax.axis_index("c")
    me = (4 * x_ + 2 * y_ + c_).astype(jnp.int32).reshape(1)
    return pl.pallas_call(
        body, name=name,
        grid_spec=pltpu.PrefetchScalarGridSpec(
            num_scalar_prefetch=1, grid=(R // tr,),
            in_specs=[pl.BlockSpec((S, tr, C), lambda i, s: (0, i, 0)), pl.BlockSpec((tr, C), lambda i, s: (i, 0))],
            out_specs=pl.BlockSpec((tr, C), lambda i, s: (i, 0))),
        out_shape=jax.ShapeDtypeStruct((R, C), F32),
        compiler_params=_params(),
    )(me, x, own)


def _after(earlier, arrays):
    return lax.optimization_barrier((earlier, arrays))


def _reduce_swap(gs, tag, earlier):
    gs = _after(earlier, gs)[1]
    return gs, _swap_halves(gs, "reduce_swap_" + tag, COLLECTIVE_IDS["swap_" + tag])


def _reduce_exchange(gs, r1, names, tag, later_than):
    r1 = _after(later_than, r1)[1]
    ts = [_add_halves(g, r, "reduce_add_cores_" + nm) for g, r, nm in zip(gs, r1, names)]
    us = _exchange_chips(ts, "reduce_exchange_" + tag, COLLECTIVE_IDS["exchange_" + tag])
    return us, ts


def _reduce_finish(us, ts, names, tag):
    fs = [_sum_chips_into_half(u, t, "reduce_add_chips_" + nm) for u, t, nm in zip(us, ts, names)]
    return _join_halves(fs, "reduce_join_" + tag)


BIG = ["ffn1_w_gate", "ffn1_w_up", "ffn1_w_down", "w_in", "ssm_w_glu", "w_attn_branch", "w_ssm_branch",
       "w_out", "ffn2_w_gate", "ffn2_w_up", "ffn2_w_down"]
SMALL = ["ffn1_norm", "mix_norm", "gate_bias", "rel_bias_table", "ssm_a_re", "ssm_a_im", "ssm_log_dt",
         "ssm_b_re", "ssm_b_im", "ssm_c_re", "ssm_c_im", "ssm_d", "ffn2_norm", "final_norm"]
ORDER = ["ffn1_norm", "ffn1_w_gate", "ffn1_w_up", "ffn1_w_down", "mix_norm", "w_in", "gate_bias", "rel_bias_table",
         "ssm_a_re", "ssm_a_im", "ssm_log_dt", "ssm_b_re", "ssm_b_im", "ssm_c_re", "ssm_c_im", "ssm_d",
         "ssm_w_glu", "w_attn_branch", "w_ssm_branch", "w_out", "ffn2_norm", "ffn2_w_gate", "ffn2_w_up",
         "ffn2_w_down", "final_norm"]


_SMALL_TILE = 8 * LANES


def _pack_small(arrays):
    rows = []
    for a in arrays:
        flat = a.reshape(-1).astype(F32)
        rows.append(jnp.pad(flat, (0, (-flat.shape[0]) % _SMALL_TILE)).reshape(-1, LANES))
    return jnp.concatenate(rows, axis=0)


def _unpack_small(packed, shapes):
    out, r0 = [], 0
    for shp in shapes:
        n = math.prod(shp)
        rows = 8 * -(-n // _SMALL_TILE)
        out.append(packed[r0:r0 + rows].reshape(-1)[:n].reshape(shp))
        r0 += rows
    return out


def _split_cols(g):
    K, N = g.shape
    return g.reshape(K, N_SHARD, N // N_SHARD).transpose(1, 0, 2)


def _join_cols(w):
    S, K, n = w.shape
    return w.transpose(1, 0, 2).reshape(K, S * n)


COL_SHARDED = ("ssm_w_glu", "w_attn_branch", "w_ssm_branch")
TRANSPOSED = ("ffn1_w_gate", "ffn1_w_up", "ffn2_w_gate", "ffn2_w_up", "w_in")


def _shard_2d(name, arr):
    two_d = arr.reshape(arr.shape[-2:])
    return two_d.T if name in TRANSPOSED else two_d


def _shard_nd(name, two_d, shape):
    return (two_d.T if name in TRANSPOSED else two_d).reshape(shape)


class _GradSync:
    def __init__(self, weights, moms, vels):
        self.weights, self.moms, self.vels = weights, moms, vels
        self.grads, self.delta, self.new_m, self.new_v = {}, {}, {}, {}
        self.loss = None
        self._earlier = []
        self._swapped = {}
        self._exchanged = {}

    def swap(self, tag, gw, later_than=()):
        gs = []
        for n in REDUCE_GROUPS[tag]:
            g = gw[n]
            if n in COL_SHARDED:
                g = _split_cols(g)
            elif n in ("w_out", "w_in"):
                g = g.reshape(N_SHARD, g.shape[0] // N_SHARD, g.shape[1])
            gs.append(g)
        self._swapped[tag] = _reduce_swap(gs, tag, list(self._earlier) + list(later_than))
        self._earlier = self._swapped[tag][1]

    def exchange(self, tag, later_than):
        gs, r1 = self._swapped[tag]
        us, ts = _reduce_exchange(gs, r1, REDUCE_GROUPS[tag], tag, later_than)
        self._exchanged[tag] = (us, ts)
        self._earlier = us

    def small_ready(self, gs, loss_blk, later_than=()):
        _, (mine,) = _after(list(self._earlier) + list(later_than),
                            [_pack_small([gs[n] for n in SMALL] + [loss_blk[0:1, :]])])
        others = _gather_small(mine)
        self._exchanged["small"] = (others, mine)
        self._earlier = [others]

    def finish(self, tag):
        made = []
        if tag == "small":
            others, mine = self._exchanged[tag]
            shapes = [self.weights[n].shape for n in SMALL]
            total = _unpack_small(_sum_devices(others, mine, "sum_small"), shapes + [(128,)])
            self.loss = total[-1][0]
            self.grads.update(zip(SMALL, total[:-1]))
            packed = [_pack_small([src[n] for n in SMALL]) for src in (self.weights, self.grads, self.moms, self.vels)]
            for dst, res in zip((self.delta, self.new_m, self.new_v), _adamw(*packed, "adamw_small")):
                dst.update(zip(SMALL, _unpack_small(res, shapes)))
            for n in SMALL:
                made += [self.grads[n], self.delta[n], self.new_m[n], self.new_v[n]]
            return made + [self.loss]
        names = REDUCE_GROUPS[tag]
        us, ts = self._exchanged[tag]
        for n, g in zip(names, _reduce_finish(us, ts, names, tag)):
            shp = self.weights[n].shape
            d, m, v = _adamw(_shard_2d(n, self.weights[n]), g, _shard_2d(n, self.moms[n]), _shard_2d(n, self.vels[n]),
                             "adamw_" + n)
            self.grads[n], self.delta[n] = _shard_nd(n, g, shp), _shard_nd(n, d, shp)
            self.new_m[n], self.new_v[n] = _shard_nd(n, m, shp), _shard_nd(n, v, shp)
            made += [self.grads[n], self.delta[n], self.new_m[n], self.new_v[n]]
        return made

    def finish_all(self):
        self.exchange("ffn1", later_than=self.finish("ffn2"))
        for tag in ("mixer", "w_in", "small", "ffn1"):
            self.finish(tag)


def _local_step(x, target, w, later, small, sync):
    L = x.shape[0]
    row = lambda v: v.reshape(1, -1)

    a_re, a_im = small["ssm_a_re"].reshape(1, NS), small["ssm_a_im"].reshape(1, NS)
    ldt = jnp.repeat(small["ssm_log_dt"].reshape(SSM_GROUPS), SSM_STATE).reshape(1, NS)
    to_cn = lambda b: b.reshape(SSM_GROUPS, SSM_STATE, SSM_GROUP).transpose(2, 0, 1).reshape(SSM_GROUP, NS)
    c_to_cn = lambda c: c.reshape(SSM_GROUPS, SSM_GROUP, SSM_STATE).transpose(1, 0, 2).reshape(SSM_GROUP, NS)
    b_re, b_im = to_cn(small["ssm_b_re"]), to_cn(small["ssm_b_im"])
    c_re, c_im = c_to_cn(small["ssm_c_re"]), c_to_cn(small["ssm_c_im"])
    d_skip = row(small["ssm_d"])
    pw, pwr, bd, cdt = _disc_fwd(a_re, a_im, ldt, b_re, b_im, c_re, c_im)

    onehot = _bucket_onehot()
    table_t = small["rel_bias_table"].T.reshape(3, HEADS_PER_GROUP, N_BUCKETS)
    table_t = jnp.pad(table_t, ((0, 0), (0, 8 - HEADS_PER_GROUP), (0, 0)))
    bias = _bias_expand(table_t, onehot)[:, :, :HEADS_PER_GROUP].reshape(
        3, 2, HEADS_PER_GROUP, ATTN_BLOCK, 2 * ATTN_BLOCK)

    n1, nm, n2, nf = row(small["ffn1_norm"]), row(small["mix_norm"]), row(small["ffn2_norm"]), row(small["final_norm"])
    gate_bias = row(small["gate_bias"])

    x1, a1, b1, *later_full = _ffn_fwd(x, n1, w["ffn1_w_gate"], w["ffn1_w_up"], w["ffn1_w_down"], "ffn1_fwd",
                                       carried=list(later.values()))
    w = dict(w, **dict(zip(later, later_full)))
    for n in COL_SHARDED:
        w[n] = _join_cols(w[n])
    w["w_out"] = w["w_out"].reshape(D_MODEL, D_MODEL)
    w["w_in"] = w["w_in"].reshape(IN_WIDTH, D_MODEL)
    *qkv, u, gates = _mix_in_fwd(x1, nm, w["w_in"], gate_bias)
    q, k, v = qkv[0:3], qkv[3:6], qkv[6:9]
    o_g, lse_g = [], []
    for grp in range(3):
        o, lse = _attn_fwd(q[grp], k[grp], v[grp], bias[grp], f"attn_fwd_{grp}")
        o_g.append(o)
        lse_g.append(lse)
    y, s = _ssm_fwd(u, bd, cdt, d_skip, pw)
    x2, o_attn, *lse_tot = _mix_out_fwd(x1, o_g, lse_g, y, gates, w["w_attn_branch"], w["ssm_w_glu"],
                                        w["w_ssm_branch"], w["w_out"])
    x3, a2, b2 = _ffn_fwd(x2, n2, w["ffn2_w_gate"], w["ffn2_w_up"], w["ffn2_w_down"], "ffn2_fwd")
    loss_blk, dx3, d_nf = _loss_fwd_bwd(x3, nf, target)

    gw, gs = {}, {}
    gs["final_norm"] = d_nf

    dx2, da, db, sact, h, d_out, gs["ffn2_norm"] = _ffn_bwd(dx3, x2, n2, a2, b2, w["ffn2_w_gate"], w["ffn2_w_up"],
                                                            w["ffn2_w_down"], "ffn2_bwd")
    gw["ffn2_w_gate"] = _matmul_tn(da, h[None], "ffn2_dw_gate")
    gw["ffn2_w_up"] = _matmul_tn(db, h[None], "ffn2_dw_up")
    gw["ffn2_w_down"] = _matmul_tn(sact, d_out[None], "ffn2_dw_down")
    sync.swap("ffn2", gw)

    head_sum = (jnp.arange(GROUP_WIDTH)[:, None] // HEAD_DIM == jnp.arange(GROUP_WIDTH)[None, :] // HEAD_DIM).astype(F32)
    (*d_o_delta, dy, dgp, mix, dya, dys, ys2, gel, dglu, gs["gate_bias"]) = _mix_out_bwd(
        dx2, o_attn, y, gates, w["w_attn_branch"], w["ssm_w_glu"], w["w_ssm_branch"], w["w_out"], head_sum)
    sync.exchange("ffn2", later_than=[dy])
    d_o, delta = d_o_delta[0:3], d_o_delta[3:6]
    gw["w_out"] = _matmul_tn(mix[None], dx2[None], "dw_out")[0]
    gw["w_attn_branch"] = _matmul_tn(o_attn[None], dya[None], "dw_attn_branch")[0]
    gw["w_ssm_branch"] = _matmul_tn(ys2[None], dys[None], "dw_ssm_branch")[0]
    gw["ssm_w_glu"] = _matmul_tn(gel[None], dglu[None], "dw_glu")[0]

    dqs, dks, dvs, dsums = [], [], [], []
    for grp in range(3):
        dq, dk, dv, dsum = _attn_bwd(q[grp], k[grp], v[grp], d_o[grp], lse_tot[grp], delta[grp], bias[grp],
                                     f"attn_bwd_{grp}")
        dqs.append(dq)
        dks.append(dk)
        dvs.append(dv)
        dsums.append(dsum.reshape(HEADS_PER_GROUP, -1))
    dsum_all = jnp.pad(jnp.stack(dsums), ((0, 0), (0, 8 - HEADS_PER_GROUP), (0, 0)))
    d_table = _bias_reduce(dsum_all, onehot)[:, :HEADS_PER_GROUP]
    gs["rel_bias_table"] = d_table.reshape(3 * HEADS_PER_GROUP, N_BUCKETS).T

    du, gs["ssm_d"], d_bd, d_cdt, d_ab = _ssm_bwd(dy, u, s, bd, cdt, d_skip, pwr)
    sync.swap("mixer", gw, later_than=[du])
    sync.exchange("mixer", later_than=[dqs[2]])
    group_sum =(jnp.arange(NS)[:, None] // SSM_STATE == jnp.arange(128)[None, :]).astype(F32)
    d_are, d_aim, d_ldt, d_bre, d_bim, d_cre, d_cim = _disc_bwd(a_re, a_im, ldt, b_re, b_im, d_bd, d_cdt, d_ab, group_sum)
    gs["ssm_a_re"], gs["ssm_a_im"] = d_are, d_aim
    gs["ssm_log_dt"] = d_ldt[0, :SSM_GROUPS]
    from_cn = lambda t: t.reshape(SSM_GROUP, SSM_GROUPS, SSM_STATE).transpose(1, 2, 0)
    c_from_cn = lambda t: t.reshape(SSM_GROUP, SSM_GROUPS, SSM_STATE).transpose(1, 0, 2)
    gs["ssm_b_re"], gs["ssm_b_im"] = from_cn(d_bre), from_cn(d_bim)
    gs["ssm_c_re"], gs["ssm_c_im"] = c_from_cn(d_cre), c_from_cn(d_cim)

    dx1, hm, dz, gs["mix_norm"] = _mix_in_bwd(dx2, x1, nm, dqs + dks + dvs, du, dgp, w["w_in"])
    gw["w_in"] = _matmul_tn(dz[None], hm[None], "dw_in")[0]
    sync.swap("w_in", gw)

    dx0, da, db, sact, h, d_out, gs["ffn1_norm"] = _ffn_bwd(dx1, x, n1, a1, b1, w["ffn1_w_gate"], w["ffn1_w_up"],
                                                            w["ffn1_w_down"], "ffn1_bwd")
    sync.exchange("w_in", later_than=[dx0])
    gw["ffn1_w_gate"] = _matmul_tn(da, h[None], "ffn1_dw_gate")
    gw["ffn1_w_up"] = _matmul_tn(db, h[None], "ffn1_dw_up")
    sync.small_ready(gs, loss_blk, later_than=[gw["ffn1_w_up"]])
    gw["ffn1_w_down"] = _matmul_tn(sact, d_out[None], "ffn1_dw_down")
    sync.swap("ffn1", gw)
    return dx0


def kernel(x, ffn1_norm, ffn1_w_gate, ffn1_w_up, ffn1_w_down, mix_norm, w_in, gate_bias, rel_bias_table, ssm_a_re, ssm_a_im, ssm_log_dt, ssm_b_re, ssm_b_im, ssm_c_re, ssm_c_im, ssm_d, ssm_w_glu, w_attn_branch, w_ssm_branch, w_out, ffn2_norm, ffn2_w_gate, ffn2_w_up, ffn2_w_down, final_norm, loss_target, m_ffn1_norm, m_ffn1_w_gate, m_ffn1_w_up, m_ffn1_w_down, m_mix_norm, m_w_in, m_gate_bias, m_rel_bias_table, m_ssm_a_re, m_ssm_a_im, m_ssm_log_dt, m_ssm_b_re, m_ssm_b_im, m_ssm_c_re, m_ssm_c_im, m_ssm_d, m_ssm_w_glu, m_w_attn_branch, m_w_ssm_branch, m_w_out, m_ffn2_norm, m_ffn2_w_gate, m_ffn2_w_up, m_ffn2_w_down, m_final_norm, v_ffn1_norm, v_ffn1_w_gate, v_ffn1_w_up, v_ffn1_w_down, v_mix_norm, v_w_in, v_gate_bias, v_rel_bias_table, v_ssm_a_re, v_ssm_a_im, v_ssm_log_dt, v_ssm_b_re, v_ssm_b_im, v_ssm_c_re, v_ssm_c_im, v_ssm_d, v_ssm_w_glu, v_w_attn_branch, v_w_ssm_branch, v_w_out, v_ffn2_norm, v_ffn2_w_gate, v_ffn2_w_up, v_ffn2_w_down, v_final_norm):
    args = dict(locals())
    weights = {n: args[n] for n in ORDER}
    moms = {n: args["m_" + n] for n in ORDER}
    vels = {n: args["v_" + n] for n in ORDER}

    shard2d = {n: _shard_2d(n, weights[n]) for n in BIG}
    first, rest = BIG[:3], BIG[3:]
    full = dict(zip(first, _gather_weights([shard2d[n].astype(BF16) for n in first], "gather_ffn1_weights")))
    later = {n: shard2d[n].astype(BF16) for n in rest}

    small = {n: weights[n] for n in SMALL}
    sync = _GradSync(weights, moms, vels)
    grad_x = _local_step(x[0], loss_target[0], full, later, small, sync)
    sync.finish_all()
    return (sync.loss, grad_x[None], *[sync.grads[n] for n in ORDER], *[sync.delta[n] for n in ORDER],
            *[sync.new_m[n] for n in ORDER], *[sync.new_v[n] for n in ORDER])
```

```python
import functools
import math

import jax
import jax.numpy as jnp
from jax import lax
from jax.experimental import pallas as pl
from jax.experimental.pallas import tpu as pltpu
from jax.experimental.pallas import tpu_sc as plsc

F32 = jnp.float32
BF16 = jnp.bfloat16
MESH = pl.DeviceIdType.MESH

D_MODEL = 1024
D_FF = 2816
HEAD_DIM = 64
HEADS_PER_GROUP = 4
DILATIONS = (1, 4, 16)
WINDOW_STEPS = 128
ATTN_BLOCK = 128
ATTN_QB = 4
GROUP_WIDTH = HEADS_PER_GROUP * HEAD_DIM
ATTN_WIDTH = 3 * GROUP_WIDTH
N_BUCKETS = 32
MAX_DISTANCE = 2048
NEG_INF = -1e30
SSM_WIDTH = 512
SSM_GROUP = 16
SSM_GROUPS = 32
SSM_STATE = 64
NS = SSM_GROUPS * SSM_STATE
EPS = 1e-6
IN_WIDTH = 3 * ATTN_WIDTH + SSM_WIDTH + 2 * D_MODEL
Q_SCALE = HEAD_DIM ** -0.5
N_SHARD = 4
FF_SHARD = D_FF // N_SHARD
ADAM_LR, ADAM_B1, ADAM_B2, ADAM_EPS, ADAM_WD, ADAM_STEP = 0.001, 0.9, 0.999, 1e-08, 0.01, 10

LANES = 128
VMEM_LIMIT = 56 * 1024 * 1024
ROW_TILE = 512
FFN_BWD_TILE = 512
SSM_CHUNK = 256
SSM_FWD_CHUNK = 512
SCAN_LANES = 512
ADAMW_BLOCK_BYTES = 2 << 20
TN_VMEM_BUDGET = 40 * 1024 * 1024
REDUCE_GROUPS = {
    "ffn2": ["ffn2_w_gate", "ffn2_w_up", "ffn2_w_down"],
    "mixer": ["w_out", "w_attn_branch", "w_ssm_branch", "ssm_w_glu"],
    "w_in": ["w_in"],
    "ffn1": ["ffn1_w_gate", "ffn1_w_up", "ffn1_w_down"],
}
COLLECTIVE_IDS = {name: i for i, name in enumerate(
    ["gather", "gather_small"] + [stage + "_" + tag for tag in REDUCE_GROUPS for stage in ("swap", "exchange")])}


def _params(**kw):
    return pltpu.CompilerParams(vmem_limit_bytes=VMEM_LIMIT, **kw)


def _dot(a, b):
    return jnp.dot(a, b, preferred_element_type=F32)


def _dot_nt(a, b):
    return lax.dot_general(a, b, (((1,), (1,)), ((), ())), preferred_element_type=F32)


def _dot_tn(a, b):
    return lax.dot_general(a, b, (((0,), (0,)), ((), ())), preferred_element_type=F32)


def _dot_exact(a, b):
    return jnp.dot(a, b, preferred_element_type=F32, precision=lax.Precision.HIGHEST)


def _dot_nt_exact(a, b):
    return lax.dot_general(a, b, (((1,), (1,)), ((), ())), preferred_element_type=F32,
                           precision=lax.Precision.HIGHEST)


def _rms(x):
    r = lax.rsqrt(jnp.mean(x * x, axis=-1, keepdims=True) + EPS)
    return r, x * r


def _rms_bwd(dh, g, r, xhat):
    dxh = dh * g
    return r * (dxh - xhat * jnp.mean(dxh * xhat, axis=-1, keepdims=True))


def _sigmoid(x):
    return 1.0 / (1.0 + jnp.exp(-x))


_GELU_C = math.sqrt(2.0 / math.pi)


def _gelu(x):
    return 0.5 * x * (1.0 + jnp.tanh(_GELU_C * (x + 0.044715 * x * x * x)))


def _gelu_grad(x):
    t = jnp.tanh(_GELU_C * (x + 0.044715 * x * x * x))
    return 0.5 * (1.0 + t) + 0.5 * x * (1.0 - t * t) * _GELU_C * (1.0 + 3 * 0.044715 * x * x)


def _whole():
    return pl.BlockSpec(memory_space=pltpu.VMEM)


def _row_tile(rows, cap):
    if rows <= cap:
        return rows
    return max(t for t in range(8, cap + 1, 8) if rows % t == 0)


def _rows(tm, w):
    return pl.BlockSpec((tm, w), lambda i: (i, 0))


def _acc_row(w):
    return pl.BlockSpec((1, w), lambda i: (0, 0))


def _ffn_fwd(x, g, wg, wu, wd, name, carried=()):
    L = x.shape[0]
    tm = min(ROW_TILE, L)
    n = len(carried)
    steps = L // tm

    def body(x_ref, g_ref, wg_ref, wu_ref, wd_ref, *refs):
        shard_refs, (xo_ref, a_ref, b_ref), full_refs, sems = refs[:n], refs[n:n + 3], refs[n + 3:2 * n + 3], refs[2 * n + 3:]
        if n:
            start, finish = _gather_parts([w.shape for w in carried], shard_refs, full_refs, *sems)
            pl.when(pl.program_id(0) == 0)(start)
        xv = x_ref[...]
        r, xhat = _rms(xv)
        h = (xhat * g_ref[...]).astype(BF16)
        acc = jnp.zeros((tm, D_MODEL), F32)
        for j in range(N_SHARD):
            a = _dot_nt(h, wg_ref[j])
            b = _dot_nt(h, wu_ref[j])
            a_ref[j] = a.astype(BF16)
            b_ref[j] = b.astype(BF16)
            s = (a * _sigmoid(a) * b).astype(BF16)
            acc = acc + _dot(s, wd_ref[j])
        xo_ref[...] = xv + 0.5 * acc
        if n:
            pl.when(pl.program_id(0) == steps - 1)(finish)

    act = pl.BlockSpec((N_SHARD, tm, FF_SHARD), lambda i: (0, i, 0))
    return pl.pallas_call(
        body, name=name, grid=(steps,),
        in_specs=[_rows(tm, D_MODEL), _whole(), _whole(), _whole(), _whole()] + [_ANY] * n,
        out_specs=[_rows(tm, D_MODEL), act, act] + [_ANY] * n,
        out_shape=[jax.ShapeDtypeStruct((L, D_MODEL), F32),
                   jax.ShapeDtypeStruct((N_SHARD, L, FF_SHARD), BF16),
                   jax.ShapeDtypeStruct((N_SHARD, L, FF_SHARD), BF16)]
        + [jax.ShapeDtypeStruct((N_SHARD,) + w.shape, w.dtype) for w in carried],
        scratch_shapes=[pltpu.SemaphoreType.DMA((7 * n,)), pltpu.SemaphoreType.DMA((7 * n,))] if n else [],
        compiler_params=_params(),
    )(x, g, wg, wu, wd, *carried)


def _ffn_bwd(dxo, x, g, a, b, wg, wu, wd, name):
    L = x.shape[0]
    tm = min(FFN_BWD_TILE, L)
    act = pl.BlockSpec((N_SHARD, tm, FF_SHARD), lambda i: (0, i, 0))
    act_shape = jax.ShapeDtypeStruct((N_SHARD, L, FF_SHARD), BF16)

    def hidden(dxo_ref, a_ref, b_ref, wd_ref, da_ref, db_ref, s_ref, do_ref):
        d_out = (0.5 * dxo_ref[...]).astype(BF16)
        do_ref[...] = d_out
        for j in range(N_SHARD):
            av = a_ref[j].astype(F32)
            bv = b_ref[j].astype(F32)
            sg = _sigmoid(av)
            sl = av * sg
            ds = _dot_nt(d_out, wd_ref[j])
            db_ref[j] = (ds * sl).astype(BF16)
            da_ref[j] = (ds * bv * (sg * (1.0 + av * (1.0 - sg)))).astype(BF16)
            s_ref[j] = (sl * bv).astype(BF16)

    da, db, s, d_out = pl.pallas_call(
        hidden, name=name + "_hidden", grid=(L // tm,),
        in_specs=[_rows(tm, D_MODEL), act, act, _whole()],
        out_specs=[act, act, act, _rows(tm, D_MODEL)],
        out_shape=[act_shape, act_shape, act_shape, jax.ShapeDtypeStruct((L, D_MODEL), BF16)],
        compiler_params=_params(),
    )(dxo, a, b, wd)

    def to_input(dxo_ref, x_ref, g_ref, da_ref, db_ref, wg_ref, wu_ref, dxi_ref, h_ref, dg_ref):
        i = pl.program_id(0)
        gv = g_ref[...]
        r, xhat = _rms(x_ref[...])
        h_ref[...] = (xhat * gv).astype(BF16)
        dh = jnp.zeros((tm, D_MODEL), F32)
        for j in range(N_SHARD):
            dh = dh + _dot(da_ref[j], wg_ref[j]) + _dot(db_ref[j], wu_ref[j])

        @pl.when(i == 0)
        def _():
            dg_ref[...] = jnp.zeros_like(dg_ref)

        dg_ref[...] += jnp.sum(dh * xhat, axis=0, keepdims=True)
        dxi_ref[...] = dxo_ref[...] + _rms_bwd(dh, gv, r, xhat)

    dxi, h, dg = pl.pallas_call(
        to_input, name=name + "_input", grid=(L // tm,),
        in_specs=[_rows(tm, D_MODEL), _rows(tm, D_MODEL), _whole(), act, act, _whole(), _whole()],
        out_specs=[_rows(tm, D_MODEL), _rows(tm, D_MODEL), _acc_row(D_MODEL)],
        out_shape=[jax.ShapeDtypeStruct((L, D_MODEL), F32), jax.ShapeDtypeStruct((L, D_MODEL), BF16),
                   jax.ShapeDtypeStruct((1, D_MODEL), F32)],
        compiler_params=_params(),
    )(dxo, x, g, da, db, wg, wu)
    return dxi, da, db, s, h, d_out, dg


def _matmul_tn(a, b, name):
    ja, L, K = a.shape
    jb, _, N = b.shape
    J = max(ja, jb)
    splits = [s for s in (1, 2, 4, 8) if s == 1 or N % (s * LANES) == 0]
    nsplit = next((s for s in splits if 2 * K * (N // s) * 4 <= TN_VMEM_BUDGET // 2), splits[-1])
    nc = N // nsplit
    left = TN_VMEM_BUDGET - 2 * K * nc * 4
    row_bytes = 2 * (K * a.dtype.itemsize + nc * b.dtype.itemsize)
    tm = next((t for t in (2048, 1024, 512, 256) if L % t == 0 and t * row_bytes <= left), min(128, L))

    def body(a_ref, b_ref, o_ref):
        @pl.when(pl.program_id(2) == 0)
        def _():
            o_ref[...] = jnp.zeros_like(o_ref)

        o_ref[...] += _dot_tn(a_ref[...].astype(BF16), b_ref[...].astype(BF16))

    return pl.pallas_call(
        body, name=name, grid=(J, nsplit, L // tm),
        in_specs=[pl.BlockSpec((None, tm, K), (lambda j, s, i: (j, i, 0)) if ja > 1 else (lambda j, s, i: (0, i, 0))),
                  pl.BlockSpec((None, tm, nc), (lambda j, s, i: (j, i, s)) if jb > 1 else (lambda j, s, i: (0, i, s)))],
        out_specs=pl.BlockSpec((None, K, nc), lambda j, s, i: (j, 0, s)),
        out_shape=jax.ShapeDtypeStruct((J, K, N), F32),
        compiler_params=_params(),
    )(a, b)


def _loss_fwd_bwd(x, g, target):
    L = x.shape[0]
    tm = min(ROW_TILE, L)

    def body(x_ref, g_ref, t_ref, loss_ref, dx_ref, dg_ref):
        i = pl.program_id(0)
        xv = x_ref[...]
        gv = g_ref[...]
        r, xhat = _rms(xv)
        err = xhat * gv - t_ref[...]
        part = 0.5 * jnp.sum(jnp.sum(err * err, axis=1, keepdims=True) * (1.0 / D_MODEL), axis=0, keepdims=True)
        dy = err * (1.0 / D_MODEL)

        @pl.when(i == 0)
        def _():
            dg_ref[...] = jnp.zeros_like(dg_ref)
            loss_ref[...] = jnp.zeros_like(loss_ref)

        loss_ref[...] += jnp.broadcast_to(part, loss_ref.shape)
        dg_ref[...] += jnp.sum(dy * xhat, axis=0, keepdims=True)
        dx_ref[...] = _rms_bwd(dy, gv, r, xhat)

    return pl.pallas_call(
        body, name="loss_fwd_bwd", grid=(L // tm,),
        in_specs=[_rows(tm, D_MODEL), _whole(), _rows(tm, D_MODEL)],
        out_specs=[pl.BlockSpec((8, 128), lambda i: (0, 0)), _rows(tm, D_MODEL), _acc_row(D_MODEL)],
        out_shape=[jax.ShapeDtypeStruct((8, 128), F32), jax.ShapeDtypeStruct((L, D_MODEL), F32),
                   jax.ShapeDtypeStruct((1, D_MODEL), F32)],
        compiler_params=_params(),
    )(x, g, target)


_C_K = ATTN_WIDTH
_C_V = 2 * ATTN_WIDTH
_C_U = 3 * ATTN_WIDTH
_C_G = _C_U + SSM_WIDTH


def _residue_spec(d, tm):
    return pl.BlockSpec((d, tm // d, GROUP_WIDTH), lambda i: (0, i, 0))


def _residue_shape(d, L, dtype):
    return jax.ShapeDtypeStruct((d, L // d, GROUP_WIDTH), dtype)


def _residue_scratch(tm):
    return pltpu.VMEM((GROUP_WIDTH // LANES, tm, LANES), F32)


def _to_residues(val, out_ref, scr, d):
    if d == 1:
        out_ref[0] = val.astype(out_ref.dtype)
        return
    tm = val.shape[0]
    for half in range(GROUP_WIDTH // LANES):
        cols = slice(half * LANES, (half + 1) * LANES)
        scr[half] = val[:, cols]
        for r in range(d):
            out_ref[r, :, cols] = scr[half, pl.ds(r, tm // d, stride=d), :].astype(out_ref.dtype)


def _from_residues(ref, scr, d):
    if d == 1:
        return ref[0].astype(F32)
    rows = ref.shape[1]
    for half in range(GROUP_WIDTH // LANES):
        cols = slice(half * LANES, (half + 1) * LANES)
        for r in range(d):
            scr[half, pl.ds(r, rows, stride=d), :] = ref[r, :, cols].astype(F32)
    return jnp.concatenate([scr[half] for half in range(GROUP_WIDTH // LANES)], axis=1)


def _mix_in_fwd(x, g, w_in, gate_bias):
    L = x.shape[0]
    tm = min(ROW_TILE, L)

    def body(x_ref, g_ref, w_ref, gb_ref, *refs):
        qkv_refs, (u_ref, gate_ref, scr) = refs[:9], refs[9:]
        r, xhat = _rms(x_ref[...])
        h = (xhat * g_ref[...]).astype(BF16)
        for part, (c0, scale) in enumerate(((0, Q_SCALE), (_C_K, 1.0), (_C_V, 1.0))):
            z = _dot_nt(h, w_ref[c0:c0 + ATTN_WIDTH, :]) * scale
            for grp, d in enumerate(DILATIONS):
                _to_residues(z[:, grp * GROUP_WIDTH:(grp + 1) * GROUP_WIDTH], qkv_refs[3 * part + grp], scr, d)
        u_ref[...] = _dot_nt(h, w_ref[_C_U:_C_G, :])
        gate_ref[...] = _sigmoid(_dot_nt(h, w_ref[_C_G:IN_WIDTH, :]) + gb_ref[...])

    return pl.pallas_call(
        body, name="mix_in_fwd", grid=(L // tm,),
        in_specs=[_rows(tm, D_MODEL), _whole(), _whole(), _whole()],
        out_specs=[_residue_spec(d, tm) for d in DILATIONS] * 3 + [_rows(tm, SSM_WIDTH), _rows(tm, 2 * D_MODEL)],
        out_shape=[_residue_shape(d, L, BF16) for d in DILATIONS] * 3
        + [jax.ShapeDtypeStruct((L, SSM_WIDTH), F32), jax.ShapeDtypeStruct((L, 2 * D_MODEL), F32)],
        scratch_shapes=[_residue_scratch(tm)],
        compiler_params=_params(),
    )(x, g, w_in, gate_bias)


def _mix_in_bwd(dx2, x, g, dqkv, du, dgp, w_in):
    L = x.shape[0]
    tm = min(ROW_TILE, L)

    def body(dx2_ref, x_ref, g_ref, *refs):
        piece_refs = refs[:9]
        du_ref, dgp_ref, w_ref, dx1_ref, h_ref, dz_ref, dg_ref, scr = refs[9:]
        i = pl.program_id(0)
        gv = g_ref[...]
        r, xhat = _rms(x_ref[...])
        h_ref[...] = (xhat * gv).astype(BF16)
        for part in range(3):
            for grp, d in enumerate(DILATIONS):
                c0 = part * ATTN_WIDTH + grp * GROUP_WIDTH
                dz_ref[:, c0:c0 + GROUP_WIDTH] = _from_residues(piece_refs[3 * part + grp], scr, d).astype(BF16)
        dz_ref[:, _C_U:_C_G] = du_ref[...].astype(BF16)
        dz_ref[:, _C_G:IN_WIDTH] = dgp_ref[...]
        dh = _dot(dz_ref[...], w_ref[...])

        @pl.when(i == 0)
        def _():
            dg_ref[...] = jnp.zeros_like(dg_ref)

        dg_ref[...] += jnp.sum(dh * xhat, axis=0, keepdims=True)
        dx1_ref[...] = dx2_ref[...] + _rms_bwd(dh, gv, r, xhat)

    return pl.pallas_call(
        body, name="mix_in_bwd", grid=(L // tm,),
        in_specs=[_rows(tm, D_MODEL), _rows(tm, D_MODEL), _whole()] + [_residue_spec(d, tm) for d in DILATIONS] * 3
        + [_rows(tm, SSM_WIDTH), _rows(tm, 2 * D_MODEL), _whole()],
        out_specs=[_rows(tm, D_MODEL), _rows(tm, D_MODEL), _rows(tm, IN_WIDTH), _acc_row(D_MODEL)],
        out_shape=[jax.ShapeDtypeStruct((L, D_MODEL), F32), jax.ShapeDtypeStruct((L, D_MODEL), BF16),
                   jax.ShapeDtypeStruct((L, IN_WIDTH), BF16), jax.ShapeDtypeStruct((1, D_MODEL), F32)],
        scratch_shapes=[_residue_scratch(tm)],
        compiler_params=_params(),
    )(dx2, x, g, *dqkv, du, dgp, w_in)


def _bucket_onehot():
    qi = jnp.arange(ATTN_BLOCK)[:, None]
    kj = jnp.arange(2 * ATTN_BLOCK)[None, :]
    steps = jnp.maximum(qi + ATTN_BLOCK - kj, 0)
    max_exact = N_BUCKETS // 2
    out = []
    for d in DILATIONS:
        dist = steps * d
        df = jnp.maximum(dist, 1).astype(F32)
        large = max_exact + (jnp.log(df / max_exact) / math.log(MAX_DISTANCE / max_exact)
                             * (N_BUCKETS - max_exact)).astype(jnp.int32)
        large = jnp.minimum(large, N_BUCKETS - 1)
        bucket = jnp.where(dist < max_exact, dist, large).reshape(-1)
        out.append((bucket[None, :] == jnp.arange(N_BUCKETS)[:, None]).astype(F32))
    return jnp.stack(out)


def _bias_expand(table_t, onehot):
    n = onehot.shape[-1]

    def body(t_ref, oh_ref, o_ref):
        bias = _dot_exact(t_ref[...], oh_ref[...])
        col = lax.broadcasted_iota(jnp.int32, (8, n), 1)
        qi = col // (2 * ATTN_BLOCK)
        kj = col - qi * (2 * ATTN_BLOCK)
        steps = qi + ATTN_BLOCK - kj
        band = (steps >= 0) & (steps <= WINDOW_STEPS)
        o_ref[0] = jnp.where(band & (kj >= ATTN_BLOCK), bias, NEG_INF)
        o_ref[1] = jnp.where(band, bias, NEG_INF)

    return pl.pallas_call(
        body, name="bias_expand", grid=(3,),
        in_specs=[pl.BlockSpec((None, 8, N_BUCKETS), lambda g: (g, 0, 0)),
                  pl.BlockSpec((None, N_BUCKETS, n), lambda g: (g, 0, 0))],
        out_specs=pl.BlockSpec((None, 2, 8, n), lambda g: (g, 0, 0, 0)),
        out_shape=jax.ShapeDtypeStruct((3, 2, 8, n), F32),
        compiler_params=_params(),
    )(table_t, onehot)


def _bias_reduce(dsum, onehot):
    n = onehot.shape[-1]

    def body(d_ref, oh_ref, o_ref):
        o_ref[...] = _dot_nt_exact(d_ref[...], oh_ref[...])

    return pl.pallas_call(
        body, name="bias_reduce", grid=(3,),
        in_specs=[pl.BlockSpec((None, 8, n), lambda g: (g, 0, 0)),
                  pl.BlockSpec((None, N_BUCKETS, n), lambda g: (g, 0, 0))],
        out_specs=pl.BlockSpec((None, 8, N_BUCKETS), lambda g: (g, 0, 0)),
        out_shape=jax.ShapeDtypeStruct((3, 8, N_BUCKETS), F32),
        compiler_params=_params(),
    )(dsum, onehot)


def _head_of_col(rows):
    return lax.broadcasted_iota(jnp.int32, (rows, GROUP_WIDTH), 1) // HEAD_DIM


_STACK_ROWS = HEADS_PER_GROUP * ATTN_BLOCK


def _stack_heads(x, head_of_col):
    return jnp.concatenate([jnp.where(head_of_col == hh, x, jnp.zeros_like(x)) for hh in range(HEADS_PER_GROUP)],
                           axis=0)


def _attn_specs(qb):
    rows = qb * ATTN_BLOCK
    cur = pl.BlockSpec((None, rows, GROUP_WIDTH), lambda r, n: (r, n, 0))
    prev = pl.BlockSpec((None, ATTN_BLOCK, GROUP_WIDTH), lambda r, n: (r, jnp.maximum(n * qb - 1, 0), 0))
    bias = pl.BlockSpec((2, HEADS_PER_GROUP, ATTN_BLOCK, 2 * ATTN_BLOCK), lambda r, n: (0, 0, 0, 0))
    return cur, prev, bias


def _attn_fwd(q, k, v, bias, name):
    d, M, _ = q.shape
    nb = M // ATTN_BLOCK
    qb = min(ATTN_QB, nb)

    def body(q_ref, kp_ref, kc_ref, vp_ref, vc_ref, bias_ref, o_ref, lse_ref):
        n = pl.program_id(1)
        q_head = _head_of_col(ATTN_BLOCK)
        kwin = jnp.concatenate([kp_ref[...], kc_ref[...]], axis=0)
        vwin = jnp.concatenate([vp_ref[...], vc_ref[...]], axis=0)
        ones = jnp.ones((2 * ATTN_BLOCK, LANES), BF16)
        for b in range(qb):
            rows = slice(b * ATTN_BLOCK, (b + 1) * ATTN_BLOCK)
            window = slice(b * ATTN_BLOCK, (b + 2) * ATTN_BLOCK)
            variant = jnp.minimum(n, 1) if b == 0 else 1
            kk = kwin[window]
            vv = vwin[window]
            q4 = _stack_heads(q_ref[rows, :], q_head)
            logits = _dot_nt(q4, kk) + bias_ref[variant].reshape(_STACK_ROWS, 2 * ATTN_BLOCK)
            m = jnp.max(logits, axis=1, keepdims=True)
            p16 = jnp.exp(logits - m).astype(BF16)
            den = _dot(p16, ones)[:, 0:1]
            out = _dot(p16, vv) * (1.0 / den)
            lse = m + jnp.log(den)
            o_acc = jnp.zeros((ATTN_BLOCK, GROUP_WIDTH), F32)
            lse_acc = jnp.zeros((ATTN_BLOCK, GROUP_WIDTH), F32)
            for hh in range(HEADS_PER_GROUP):
                head_rows = slice(hh * ATTN_BLOCK, (hh + 1) * ATTN_BLOCK)
                o_acc = jnp.where(q_head == hh, out[head_rows], o_acc)
                lse_acc = jnp.where(q_head == hh, lse[head_rows], lse_acc)
            o_ref[rows, :] = o_acc
            lse_ref[rows, :] = lse_acc

    cur, prev, full = _attn_specs(qb)
    return pl.pallas_call(
        body, name=name, grid=(d, nb // qb),
        in_specs=[cur, prev, cur, prev, cur, full],
        out_specs=[cur, cur],
        out_shape=[jax.ShapeDtypeStruct((d, M, GROUP_WIDTH), F32)] * 2,
        compiler_params=_params(),
    )(q, k, k, v, v, bias)


def _attn_bwd(q, k, v, do, lse, delta, bias, name):
    d, M, _ = q.shape
    nb = M // ATTN_BLOCK
    qb = min(ATTN_QB, nb)
    ns = nb // qb
    rows_q = qb * ATTN_BLOCK
    last = slice(rows_q - ATTN_BLOCK, rows_q)

    def body(q_ref, kp_ref, kc_ref, vp_ref, vc_ref, do_ref, lse_ref, dl_ref, bias_ref,
             dq_ref, dk_ref, dv_ref, dsum_ref, pk_ref, pv_ref, wk_ref, wv_ref):
        r = pl.program_id(0)
        n = pl.program_id(1)

        @pl.when((r == 0) & (n == 0))
        def _():
            dsum_ref[...] = jnp.zeros_like(dsum_ref)

        @pl.when(n == 0)
        def _():
            pk_ref[...] = jnp.zeros_like(pk_ref)
            pv_ref[...] = jnp.zeros_like(pv_ref)

        @pl.when(n < ns)
        def _():
            q_head = _head_of_col(ATTN_BLOCK)
            kwin = jnp.concatenate([kp_ref[...], kc_ref[...]], axis=0)
            vwin = jnp.concatenate([vp_ref[...], vc_ref[...]], axis=0)
            wk_ref[...] = jnp.zeros_like(wk_ref)
            wv_ref[...] = jnp.zeros_like(wv_ref)
            for b in range(qb):
                rows = slice(b * ATTN_BLOCK, (b + 1) * ATTN_BLOCK)
                window = slice(b * ATTN_BLOCK, (b + 2) * ATTN_BLOCK)
                variant = jnp.minimum(n, 1) if b == 0 else 1
                kk = kwin[window]
                vv = vwin[window]
                q4 = _stack_heads(q_ref[rows, :], q_head)
                do4 = _stack_heads(do_ref[rows, :], q_head)
                heads = [hh * HEAD_DIM for hh in range(HEADS_PER_GROUP)]
                lse4 = jnp.concatenate([lse_ref[rows, c0:c0 + 1] for c0 in heads], axis=0)
                dl4 = jnp.concatenate([dl_ref[rows, c0:c0 + 1] for c0 in heads], axis=0)
                logits = _dot_nt(q4, kk) + bias_ref[variant].reshape(_STACK_ROWS, 2 * ATTN_BLOCK)
                p = jnp.exp(logits - lse4)
                ds = p * (_dot_nt(do4, vv) - dl4)
                dsum_ref[...] += ds.reshape(HEADS_PER_GROUP, ATTN_BLOCK, 2 * ATTN_BLOCK)
                ds16 = ds.astype(BF16)
                dq4 = _dot(ds16, kk)
                dq_acc = jnp.zeros((ATTN_BLOCK, GROUP_WIDTH), F32)
                for hh in range(HEADS_PER_GROUP):
                    dq_acc = jnp.where(q_head == hh, dq4[hh * ATTN_BLOCK:(hh + 1) * ATTN_BLOCK], dq_acc)
                dq_ref[rows, :] = (dq_acc * Q_SCALE).astype(BF16)
                wk_ref[window, :] += _dot_tn(ds16, q4)
                wv_ref[window, :] += _dot_tn(p.astype(BF16), do4)
            for out_ref, part_ref, win_ref in ((dk_ref, pk_ref, wk_ref), (dv_ref, pv_ref, wv_ref)):
                if ns == 1:
                    out_ref[...] = win_ref[ATTN_BLOCK:, :].astype(BF16)
                    continue
                if qb > 1:
                    out_ref[0:rows_q - ATTN_BLOCK, :] = part_ref[0:rows_q - ATTN_BLOCK, :].astype(BF16)
                out_ref[last, :] = (part_ref[last, :] + win_ref[0:ATTN_BLOCK, :]).astype(BF16)
                part_ref[...] = win_ref[ATTN_BLOCK:, :]

        if ns > 1:
            @pl.when(n == ns)
            def _():
                dk_ref[...] = pk_ref[...].astype(BF16)
                dv_ref[...] = pv_ref[...].astype(BF16)

    def clamp(n):
        return jnp.minimum(n, ns - 1)

    cur = pl.BlockSpec((None, rows_q, GROUP_WIDTH), lambda r, n: (r, clamp(n), 0))
    prev = pl.BlockSpec((None, ATTN_BLOCK, GROUP_WIDTH), lambda r, n: (r, jnp.maximum(clamp(n) * qb - 1, 0), 0))
    lag = pl.BlockSpec((None, rows_q, GROUP_WIDTH), lambda r, n: (r, jnp.maximum(n - 1, 0), 0))
    full = pl.BlockSpec((2, HEADS_PER_GROUP, ATTN_BLOCK, 2 * ATTN_BLOCK), lambda r, n: (0, 0, 0, 0))
    acc = pl.BlockSpec((HEADS_PER_GROUP, ATTN_BLOCK, 2 * ATTN_BLOCK), lambda r, n: (0, 0, 0))
    return pl.pallas_call(
        body, name=name, grid=(d, ns + 1 if ns > 1 else 1),
        in_specs=[cur, prev, cur, prev, cur, cur, cur, cur, full],
        out_specs=[cur, lag, lag, acc],
        out_shape=[jax.ShapeDtypeStruct((d, M, GROUP_WIDTH), BF16)] * 3
        + [jax.ShapeDtypeStruct((HEADS_PER_GROUP, ATTN_BLOCK, 2 * ATTN_BLOCK), F32)],
        scratch_shapes=[pltpu.VMEM((rows_q, GROUP_WIDTH), F32), pltpu.VMEM((rows_q, GROUP_WIDTH), F32),
                        pltpu.VMEM((rows_q + ATTN_BLOCK, GROUP_WIDTH), F32),
                        pltpu.VMEM((rows_q + ATTN_BLOCK, GROUP_WIDTH), F32)],
        compiler_params=_params(),
    )(q, k, k, v, v, do, lse, delta, bias)


def _disc_math(a_re, a_im, ldt, b_re, b_im):
    dt = jnp.exp(ldt)
    mag = jnp.exp(a_re * dt)
    ab_re = mag * jnp.cos(a_im * dt)
    ab_im = mag * jnp.sin(a_im * dt)
    den = a_re * a_re + a_im * a_im
    xr = ab_re - 1.0
    coef_re = (xr * a_re + ab_im * a_im) / den
    coef_im = (ab_im * a_re - xr * a_im) / den
    return ab_re, ab_im, coef_re * b_re - coef_im * b_im, coef_re * b_im + coef_im * b_re


def _block_diag_mask():
    row_g = lax.broadcasted_iota(jnp.int32, (SSM_WIDTH, 2 * NS), 0) // SSM_GROUP
    col = lax.broadcasted_iota(jnp.int32, (SSM_WIDTH, 2 * NS), 1)
    col_g = jnp.where(col >= NS, col - NS, col) // SSM_STATE
    return row_g == col_g


def _disc_fwd(a_re, a_im, ldt, b_re, b_im, c_re, c_im):
    def body(are_ref, aim_ref, ldt_ref, bre_ref, bim_ref, cre_ref, cim_ref, pw_ref, pwr_ref, bd_ref, cdt_ref):
        ab_re, ab_im, bb_re, bb_im = _disc_math(are_ref[...], aim_ref[...], ldt_ref[...], bre_ref[...], bim_ref[...])
        row = lax.broadcasted_iota(jnp.int32, (8, NS), 0)
        pr, pi = ab_re, ab_im
        t_re = jnp.zeros((8, NS), F32)
        t_im = jnp.zeros((8, NS), F32)
        u_re = jnp.zeros((8, NS), F32)
        u_im = jnp.zeros((8, NS), F32)
        for j in range(8):
            t_re = jnp.where(row == j, pr, t_re)
            t_im = jnp.where(row == j, pi, t_im)
            u_re = jnp.where(row == 7 - j, pr, u_re)
            u_im = jnp.where(row == 7 - j, pi, u_im)
            pr, pi = pr * ab_re - pi * ab_im, pr * ab_im + pi * ab_re
        pw_ref[0] = t_re
        pw_ref[1] = t_im
        pwr_ref[0] = u_re
        pwr_ref[1] = u_im
        mask = _block_diag_mask()
        zero = jnp.zeros((SSM_WIDTH, 2 * NS), F32)
        bfull = jnp.concatenate([jnp.concatenate([bb_re] * SSM_GROUPS, axis=0),
                                 jnp.concatenate([bb_im] * SSM_GROUPS, axis=0)], axis=1)
        bd_ref[...] = jnp.where(mask, bfull, zero).astype(BF16)
        cfull = jnp.concatenate([jnp.concatenate([cre_ref[...]] * SSM_GROUPS, axis=0),
                                 jnp.concatenate([-cim_ref[...]] * SSM_GROUPS, axis=0)], axis=1)
        cdt_ref[...] = jnp.where(mask, cfull, zero).astype(BF16)

    return pl.pallas_call(
        body, name="s5_disc_fwd",
        in_specs=[_whole()] * 7, out_specs=[_whole()] * 4,
        out_shape=[jax.ShapeDtypeStruct((2, 8, NS), F32), jax.ShapeDtypeStruct((2, 8, NS), F32),
                   jax.ShapeDtypeStruct((SSM_WIDTH, 2 * NS), BF16), jax.ShapeDtypeStruct((SSM_WIDTH, 2 * NS), BF16)],
        compiler_params=_params(),
    )(a_re, a_im, ldt, b_re, b_im, c_re, c_im)


def _disc_bwd(a_re, a_im, ldt, b_re, b_im, d_bd, d_cdt, d_ab, group_sum):
    def body(are_ref, aim_ref, ldt_ref, bre_ref, bim_ref, dbd_ref, dcdt_ref, dab_ref, gs_ref,
             dare_ref, daim_ref, dldt_ref, dbre_ref, dbim_ref, dcre_ref, dcim_ref):
        col = lax.broadcasted_iota(jnp.int32, (SSM_GROUP, 2 * NS), 1)
        col_g = jnp.where(col >= NS, col - NS, col) // SSM_STATE
        acc_b = jnp.zeros((SSM_GROUP, 2 * NS), F32)
        acc_c = jnp.zeros((SSM_GROUP, 2 * NS), F32)
        for g in range(SSM_GROUPS):
            rows = slice(g * SSM_GROUP, (g + 1) * SSM_GROUP)
            acc_b = acc_b + jnp.where(col_g == g, dbd_ref[rows, :], 0.0)
            acc_c = acc_c + jnp.where(col_g == g, dcdt_ref[rows, :], 0.0)
        dcre_ref[...] = acc_c[:, :NS]
        dcim_ref[...] = -acc_c[:, NS:]
        dab_re = jnp.sum(dab_ref[0], axis=0, keepdims=True)
        dab_im = jnp.sum(dab_ref[1], axis=0, keepdims=True)
        _, vjp = jax.vjp(_disc_math, are_ref[...], aim_ref[...], ldt_ref[...], bre_ref[...], bim_ref[...])
        d_are, d_aim, d_ldt, d_bre, d_bim = vjp((dab_re, dab_im, acc_b[:, :NS], acc_b[:, NS:]))
        dare_ref[...] = d_are
        daim_ref[...] = d_aim
        dbre_ref[...] = d_bre
        dbim_ref[...] = d_bim
        dldt_ref[...] = _dot_exact(jnp.broadcast_to(d_ldt, (8, NS)), gs_ref[...])

    vec = jax.ShapeDtypeStruct((1, NS), F32)
    mat = jax.ShapeDtypeStruct((SSM_GROUP, NS), F32)
    return pl.pallas_call(
        body, name="s5_disc_bwd",
        in_specs=[_whole()] * 9, out_specs=[_whole()] * 7,
        out_shape=[vec, vec, jax.ShapeDtypeStruct((8, 128), F32), mat, mat, mat, mat],
        compiler_params=_params(),
    )(a_re, a_im, ldt, b_re, b_im, d_bd, d_cdt, d_ab, group_sum)


def _scan_blocks(buf, pw_ref, carry_ref, n_blocks, reverse):
    row = lax.broadcasted_iota(jnp.int32, (8, SCAN_LANES), 0)
    for lc in range(NS // SCAN_LANES):
        re_cols = pl.ds(lc * SCAN_LANES, SCAN_LANES)
        im_cols = pl.ds(NS + lc * SCAN_LANES, SCAN_LANES)
        pr = pw_ref[0, :, re_cols]
        pi = pw_ref[1, :, re_cols]
        if reverse:
            pi = -pi
            base = [(7, 1), (6, 2), (4, 4)]
            coef = [(jnp.where(row < 8 - k, pr[j:j + 1], 0.0), jnp.where(row < 8 - k, pi[j:j + 1], 0.0), 8 - k)
                    for j, k in base]
        else:
            base = [(0, 1), (1, 2), (3, 4)]
            coef = [(jnp.where(row >= k, pr[j:j + 1], 0.0), jnp.where(row >= k, pi[j:j + 1], 0.0), k)
                    for j, k in base]

        def step(i, carry, pr=pr, pi=pi, coef=coef, re_cols=re_cols, im_cols=im_cols):
            cr, ci = carry
            blk = (n_blocks - 1 - i) if reverse else i
            rows = pl.ds(pl.multiple_of(blk * 8, 8), 8)
            xr = buf[rows, re_cols]
            xi = buf[rows, im_cols]
            for kr, ki, shift in coef:
                sr = pltpu.roll(xr, shift, 0)
                si = pltpu.roll(xi, shift, 0)
                xr, xi = xr + kr * sr - ki * si, xi + kr * si + ki * sr
            xr, xi = xr + pr * cr - pi * ci, xi + pr * ci + pi * cr
            buf[rows, re_cols] = xr
            buf[rows, im_cols] = xi
            edge = slice(0, 1) if reverse else slice(7, 8)
            return xr[edge], xi[edge]

        cr, ci = lax.fori_loop(0, n_blocks, step, (carry_ref[0:1, re_cols], carry_ref[0:1, im_cols]))
        carry_ref[0:1, re_cols] = cr
        carry_ref[0:1, im_cols] = ci


_SUPER_GROUPS = 16
_SUPER_BLOCKS = [
    (slice(k * _SUPER_GROUPS * SSM_GROUP, (k + 1) * _SUPER_GROUPS * SSM_GROUP),
     [slice(half + k * _SUPER_GROUPS * SSM_STATE, half + (k + 1) * _SUPER_GROUPS * SSM_STATE) for half in (0, NS)])
    for k in range(SSM_GROUPS // _SUPER_GROUPS)]


def _ssm_fwd(u, bd, cdt, d_skip, pw):
    L = u.shape[0]
    tc = min(SSM_FWD_CHUNK, L)

    def body(u_ref, bd_ref, cdt_ref, dsk_ref, pw_ref, y_ref, s_ref, carry_ref):
        @pl.when(pl.program_id(0) == 0)
        def _():
            carry_ref[...] = jnp.zeros_like(carry_ref)

        uv = u_ref[...]
        u16 = uv.astype(BF16)
        for ch, states in _SUPER_BLOCKS:
            for st in states:
                s_ref[:, st] = _dot(u16[:, ch], bd_ref[ch, st])
        _scan_blocks(s_ref, pw_ref, carry_ref, tc // 8, reverse=False)
        for ch, states in _SUPER_BLOCKS:
            y_ref[:, ch] = (sum(_dot_nt(s_ref[:, st].astype(BF16), cdt_ref[ch, st]) for st in states)
                            + dsk_ref[:, ch] * uv[:, ch])

    return pl.pallas_call(
        body, name="s5_fwd", grid=(L // tc,),
        in_specs=[_rows(tc, SSM_WIDTH), _whole(), _whole(), _whole(), _whole()],
        out_specs=[_rows(tc, SSM_WIDTH), _rows(tc, 2 * NS)],
        out_shape=[jax.ShapeDtypeStruct((L, SSM_WIDTH), F32), jax.ShapeDtypeStruct((L, 2 * NS), F32)],
        scratch_shapes=[pltpu.VMEM((8, 2 * NS), F32)],
        compiler_params=_params(),
    )(u, bd, cdt, d_skip, pw)


def _ssm_bwd(dy, u, s, bd, cdt, d_skip, pwr):
    L = u.shape[0]
    tc = min(SSM_CHUNK, L)
    nc = L // tc
    blocks = tc // 8

    def body(dy_ref, u_ref, s_ref, sprev_ref, bd_ref, cdt_ref, dsk_ref, pwr_ref,
             du_ref, ddsk_ref, dbd_ref, dcdt_ref, dab_ref, g_ref, sx_ref, carry_ref):
        i = pl.program_id(0)

        @pl.when(i == 0)
        def _():
            carry_ref[...] = jnp.zeros_like(carry_ref)
            ddsk_ref[...] = jnp.zeros_like(ddsk_ref)
            dbd_ref[...] = jnp.zeros_like(dbd_ref)
            dcdt_ref[...] = jnp.zeros_like(dcdt_ref)
            dab_ref[...] = jnp.zeros_like(dab_ref)

        dyv = dy_ref[...]
        uv = u_ref[...]
        dy16 = dyv.astype(BF16)
        u16 = uv.astype(BF16)
        for ch, states in _SUPER_BLOCKS:
            for st in states:
                g_ref[:, st] = _dot(dy16[:, ch], cdt_ref[ch, st])
        _scan_blocks(g_ref, pwr_ref, carry_ref, blocks, reverse=True)
        ddsk_ref[...] += jnp.sum(dyv * uv, axis=0, keepdims=True)
        for ch, states in _SUPER_BLOCKS:
            du = dsk_ref[:, ch] * dyv[:, ch]
            for st in states:
                g16 = g_ref[:, st].astype(BF16)
                du = du + _dot_nt(g16, bd_ref[ch, st])
                dbd_ref[ch, st] += _dot_tn(u16[:, ch], g16)
                dcdt_ref[ch, st] += _dot_tn(dy16[:, ch], s_ref[:, st].astype(BF16))
            du_ref[:, ch] = du

        sx_ref[pl.ds(8, tc), :] = s_ref[...]
        sx_ref[pl.ds(0, 8), :] = jnp.where(i == nc - 1, 0.0, sprev_ref[...])
        row = lax.broadcasted_iota(jnp.int32, (8, SCAN_LANES), 0)
        for lc in range(NS // SCAN_LANES):
            re_cols = pl.ds(lc * SCAN_LANES, SCAN_LANES)
            im_cols = pl.ds(NS + lc * SCAN_LANES, SCAN_LANES)

            def step(b, acc, re_cols=re_cols, im_cols=im_cols):
                ar, ai = acc
                off = pl.multiple_of(b * 8, 8)
                gr = g_ref[pl.ds(off, 8), re_cols]
                gi = g_ref[pl.ds(off, 8), im_cols]
                before = pl.ds(off, 8)
                here = pl.ds(off + 8, 8)
                sr = jnp.where(row == 0, sx_ref[before, re_cols][7:8], pltpu.roll(sx_ref[here, re_cols], 1, 0))
                si = jnp.where(row == 0, sx_ref[before, im_cols][7:8], pltpu.roll(sx_ref[here, im_cols], 1, 0))
                return ar + gr * sr + gi * si, ai + gi * sr - gr * si

            zero = jnp.zeros((8, SCAN_LANES), F32)
            ar, ai = lax.fori_loop(0, blocks, step, (zero, zero))
            dab_ref[0, :, re_cols] += ar
            dab_ref[1, :, re_cols] += ai

    rev = lambda i: (nc - 1 - i, 0)
    sprev = pl.BlockSpec((8, 2 * NS), lambda i: (jnp.maximum((nc - 1 - i) * blocks - 1, 0), 0))
    return pl.pallas_call(
        body, name="s5_bwd", grid=(nc,),
        in_specs=[pl.BlockSpec((tc, SSM_WIDTH), rev), pl.BlockSpec((tc, SSM_WIDTH), rev),
                  pl.BlockSpec((tc, 2 * NS), rev), sprev, _whole(), _whole(), _whole(), _whole()],
        out_specs=[pl.BlockSpec((tc, SSM_WIDTH), rev), _whole(), _whole(), _whole(), _whole()],
        out_shape=[jax.ShapeDtypeStruct((L, SSM_WIDTH), F32), jax.ShapeDtypeStruct((1, SSM_WIDTH), F32),
                   jax.ShapeDtypeStruct((SSM_WIDTH, 2 * NS), F32), jax.ShapeDtypeStruct((SSM_WIDTH, 2 * NS), F32),
                   jax.ShapeDtypeStruct((2, 8, NS), F32)],
        scratch_shapes=[pltpu.VMEM((tc, 2 * NS), F32), pltpu.VMEM((tc + 8, 2 * NS), F32), pltpu.VMEM((8, 2 * NS), F32)],
        compiler_params=_params(),
    )(dy, u, s, s, bd, cdt, d_skip, pwr)


def _branches(o_attn, y, gates, w_ab, w_glu, w_sb):
    ya = _dot(o_attn.astype(BF16), w_ab[...])
    gel = _gelu(y)
    glu = _dot(gel.astype(BF16), w_glu[...])
    p = glu[:, :SSM_WIDTH]
    sg = _sigmoid(glu[:, SSM_WIDTH:])
    ys2 = p * sg
    ysb = _dot(ys2.astype(BF16), w_sb[...])
    ga = gates[:, :D_MODEL]
    gs = gates[:, D_MODEL:]
    return ya, gel, p, sg, ys2, ysb, ga, gs


def _mix_out_fwd(x1, o_g, lse_g, y, gates, w_ab, w_glu, w_sb, w_out):
    L = x1.shape[0]
    tm = min(ROW_TILE, L)

    def body(x_ref, o0, o1, o2, l0, l1, l2, y_ref, gate_ref, wab_ref, wglu_ref, wsb_ref, wout_ref,
             x2_ref, oat_ref, lse0, lse1, lse2, scr):
        la, lb, lc = (_from_residues(ref, scr, d) for ref, d in zip((l0, l1, l2), DILATIONS))
        m = jnp.maximum(jnp.maximum(la, lb), lc)
        ea, eb, ec = jnp.exp(la - m), jnp.exp(lb - m), jnp.exp(lc - m)
        tot = ea + eb + ec
        oa, ob, oc = (_from_residues(ref, scr, d) for ref, d in zip((o0, o1, o2), DILATIONS))
        o_attn = (ea * oa + eb * ob + ec * oc) / tot
        oat_ref[...] = o_attn
        lse = m + jnp.log(tot)
        for ref, d in zip((lse0, lse1, lse2), DILATIONS):
            _to_residues(lse, ref, scr, d)
        ya, _, _, _, _, ysb, ga, gs = _branches(o_attn, y_ref[...], gate_ref[...], wab_ref, wglu_ref, wsb_ref)
        mix = ga * ya + gs * ysb
        x2_ref[...] = x_ref[...] + _dot(mix.astype(BF16), wout_ref[...])

    res = [_residue_spec(d, tm) for d in DILATIONS]
    return pl.pallas_call(
        body, name="mix_out_fwd", grid=(L // tm,),
        in_specs=[_rows(tm, D_MODEL)] + res * 2 + [_rows(tm, SSM_WIDTH), _rows(tm, 2 * D_MODEL)] + [_whole()] * 4,
        out_specs=[_rows(tm, D_MODEL), _rows(tm, GROUP_WIDTH)] + res,
        out_shape=[jax.ShapeDtypeStruct((L, D_MODEL), F32), jax.ShapeDtypeStruct((L, GROUP_WIDTH), F32)]
        + [_residue_shape(d, L, F32) for d in DILATIONS],
        scratch_shapes=[_residue_scratch(tm)],
        compiler_params=_params(),
    )(x1, *o_g, *lse_g, y, gates, w_ab, w_glu, w_sb, w_out)


def _mix_out_bwd(dx2, o_attn, y, gates, w_ab, w_glu, w_sb, w_out, head_sum):
    L = dx2.shape[0]
    tm = min(ROW_TILE, L)

    def body(dx_ref, oat_ref, y_ref, gate_ref, wab_ref, wglu_ref, wsb_ref, wout_ref, hs_ref,
             do0, do1, do2, dl0, dl1, dl2, dy_ref, dgp_ref, mix_ref, dya_ref, dys_ref, ys2_ref, gel_ref, dglu_ref,
             dgb_ref, scr):
        i = pl.program_id(0)
        o_attn = oat_ref[...]
        yv = y_ref[...]
        ya, gel, p, sg, ys2, ysb, ga, gs = _branches(o_attn, yv, gate_ref[...], wab_ref, wglu_ref, wsb_ref)
        mix_ref[...] = (ga * ya + gs * ysb).astype(BF16)
        ys2_ref[...] = ys2.astype(BF16)
        gel_ref[...] = gel.astype(BF16)
        dmix = _dot_nt(dx_ref[...].astype(BF16), wout_ref[...])
        dgp = jnp.concatenate([dmix * ya * ga * (1.0 - ga), dmix * ysb * gs * (1.0 - gs)], axis=1)
        dgp_ref[...] = dgp.astype(BF16)

        @pl.when(i == 0)
        def _():
            dgb_ref[...] = jnp.zeros_like(dgb_ref)

        dgb_ref[...] += jnp.sum(dgp, axis=0, keepdims=True)
        dya = (dmix * ga).astype(BF16)
        dys = (dmix * gs).astype(BF16)
        dya_ref[...] = dya
        dys_ref[...] = dys
        d_o = _dot_nt(dya, wab_ref[...])
        delta = _dot_exact(d_o * o_attn, hs_ref[...])
        for do_ref, dl_ref, d in zip((do0, do1, do2), (dl0, dl1, dl2), DILATIONS):
            _to_residues(d_o, do_ref, scr, d)
            _to_residues(delta, dl_ref, scr, d)
        dys2 = _dot_nt(dys, wsb_ref[...])
        dglu = jnp.concatenate([dys2 * sg, dys2 * p * sg * (1.0 - sg)], axis=1).astype(BF16)
        dglu_ref[...] = dglu
        dy_ref[...] = _dot_nt(dglu, wglu_ref[...]) * _gelu_grad(yv)

    grp = _rows(tm, GROUP_WIDTH)
    wide = _rows(tm, D_MODEL)
    half = _rows(tm, SSM_WIDTH)
    res = [_residue_spec(d, tm) for d in DILATIONS]
    sds = jax.ShapeDtypeStruct
    return pl.pallas_call(
        body, name="mix_out_bwd", grid=(L // tm,),
        in_specs=[wide, grp, half, _rows(tm, 2 * D_MODEL)] + [_whole()] * 5,
        out_specs=res + res + [half, _rows(tm, 2 * D_MODEL), wide, wide, wide, half, half, wide, _acc_row(2 * D_MODEL)],
        out_shape=[_residue_shape(d, L, BF16) for d in DILATIONS] + [_residue_shape(d, L, F32) for d in DILATIONS]
        + [sds((L, SSM_WIDTH), F32),
           sds((L, 2 * D_MODEL), BF16), sds((L, D_MODEL), BF16), sds((L, D_MODEL), BF16),
           sds((L, D_MODEL), BF16), sds((L, SSM_WIDTH), BF16), sds((L, SSM_WIDTH), BF16),
           sds((L, D_MODEL), BF16), sds((1, 2 * D_MODEL), F32)],
        scratch_shapes=[_residue_scratch(tm)],
        compiler_params=_params(),
    )(dx2, o_attn, y, gates, w_ab, w_glu, w_sb, w_out, head_sum)


def _adamw(w, g, m, v, name):
    R, C = w.shape
    tr = _row_tile(R, max(8, ADAMW_BLOCK_BYTES // (4 * C)))

    def body(w_ref, g_ref, m_ref, v_ref, d_ref, mo_ref, vo_ref):
        gv = g_ref[...]
        mn = ADAM_B1 * m_ref[...] + (1.0 - ADAM_B1) * gv
        vn = ADAM_B2 * v_ref[...] + (1.0 - ADAM_B2) * (gv * gv)
        m_hat = mn / (1.0 - ADAM_B1 ** ADAM_STEP)
        v_hat = vn / (1.0 - ADAM_B2 ** ADAM_STEP)
        d_ref[...] = -ADAM_LR * (m_hat / (jnp.sqrt(v_hat) + ADAM_EPS) + ADAM_WD * w_ref[...])
        mo_ref[...] = mn
        vo_ref[...] = vn

    blk = pl.BlockSpec((tr, C), lambda i: (i, 0))
    return pl.pallas_call(
        body, name=name, grid=(R // tr,),
        in_specs=[blk] * 4, out_specs=[blk] * 3,
        out_shape=[jax.ShapeDtypeStruct((R, C), F32)] * 3,
        compiler_params=_params(),
    )(w, g, m, v)


def _sum_chips_into_half(u, t, name):
    S, H, C = u.shape
    tr = _row_tile(H, 512)
    hb = H // tr

    def body(s_ref, t_ref, a_ref, b_ref, c_ref, o_ref):
        me = s_ref[1]
        others = (a_ref[...], b_ref[...], c_ref[...])
        acc = None
        for chip in range(S):
            below = others[min(chip, S - 2)]
            above = others[max(chip - 1, 0)]
            term = jnp.where(me == chip, t_ref[...], jnp.where(me > chip, below, above)).astype(F32)
            acc = term if acc is None else acc + term
        o_ref[...] = acc

    x, y, c = lax.axis_index("x"), lax.axis_index("y"), lax.axis_index("c")
    me = 2 * x + y
    scalars = jnp.stack([c, me] + [j + (j >= me).astype(jnp.int32) for j in range(S - 1)]).astype(jnp.int32)
    blk = (None, tr, C)
    return pl.pallas_call(
        body, name=name,
        grid_spec=pltpu.PrefetchScalarGridSpec(
            num_scalar_prefetch=1, grid=(hb,),
            in_specs=[pl.BlockSpec(blk, lambda i, s: (s[1], i, 0))]
            + [pl.BlockSpec(blk, functools.partial(lambda j, i, s: (s[2 + j], i, 0), j)) for j in range(S - 1)],
            out_specs=pl.BlockSpec((tr, C), lambda i, s: (s[0] * hb + i, 0))),
        out_shape=jax.ShapeDtypeStruct((2 * H, C), F32),
        compiler_params=_params(),
    )(scalars, t, u, u, u)


def _add_halves(g, r1, name):
    S, R, C = g.shape
    H = R // 2
    tr = _row_tile(H, 512)
    hb = H // tr

    def body(c_ref, g_ref, r_ref, o_ref):
        o_ref[...] = (g_ref[...] + r_ref[...]).astype(BF16)

    core = lax.axis_index("c").astype(jnp.int32).reshape(1)
    return pl.pallas_call(
        body, name=name,
        grid_spec=pltpu.PrefetchScalarGridSpec(
            num_scalar_prefetch=1, grid=(S, hb),
            in_specs=[pl.BlockSpec((None, tr, C), lambda j, i, c_ref: (j, c_ref[0] * hb + i, 0)),
                      pl.BlockSpec((None, tr, C), lambda j, i, c_ref: (j, i, 0))],
            out_specs=pl.BlockSpec((None, tr, C), lambda j, i, c_ref: (j, i, 0))),
        out_shape=jax.ShapeDtypeStruct((S, H, C), BF16),
        compiler_params=_params(),
    )(core, g, r1)


_ANY = pl.BlockSpec(memory_space=pl.ANY)


def _place():
    x, y, c = lax.axis_index("x"), lax.axis_index("y"), lax.axis_index("c")
    chips = [(1 - x, y), (x, 1 - y), (1 - x, 1 - y)]
    return x, y, c, chips


def _comm_call(body, name, ins, out_shapes, n_remote, n_local):
    return pl.pallas_call(
        body, name=name,
        in_specs=[_ANY] * len(ins), out_specs=[_ANY] * len(out_shapes), out_shape=out_shapes,
        scratch_shapes=[pltpu.SemaphoreType.DMA((n_remote,)), pltpu.SemaphoreType.DMA((n_remote,)),
                        pltpu.SemaphoreType.DMA((max(n_local, 1),))],
    )(*ins)


def _remote(src, dst, send_sems, recv_sems, k, device):
    return pltpu.make_async_remote_copy(src_ref=src, dst_ref=dst, send_sem=send_sems.at[k], recv_sem=recv_sems.at[k],
                                        device_id=device, device_id_type=MESH)


def _gather_parts(shapes, w_refs, out_refs, send_sems, recv_sems):
    n = len(shapes)
    x, y, c, chips = _place()
    me = 2 * x + y
    sibling = (x, y, 1 - c)

    def half(k, chip_idx, core):
        H = shapes[k][0] // 2
        return out_refs[k].at[chip_idx, pl.ds(core * H, H), :]

    mine = [_remote(w_refs[k], out_refs[k].at[me], send_sems, recv_sems, 6 * n + k, sibling) for k in range(n)]
    first = []
    for k in range(n):
        H = shapes[k][0] // 2
        for j, (cx, cy) in enumerate(chips):
            first.append(_remote(w_refs[k].at[pl.ds(c * H, H), :], half(k, me, c), send_sems, recv_sems,
                                 3 * k + j, (cx, cy, c)))

    def start():
        for cp in mine + first:
            cp.start()

    def finish():
        passed = []
        for k in range(n):
            for j, (cx, cy) in enumerate(chips):
                landed = half(k, 2 * cx + cy, c)
                _remote(landed, landed, send_sems, recv_sems, 3 * k + j, (cx, cy, c)).wait_recv()
                fwd = _remote(landed, landed, send_sems, recv_sems, 3 * n + 3 * k + j, sibling)
                fwd.start()
                passed.append(fwd)
        for k in range(n):
            for j, (cx, cy) in enumerate(chips):
                other = half(k, 2 * cx + cy, 1 - c)
                _remote(other, other, send_sems, recv_sems, 3 * n + 3 * k + j, sibling).wait_recv()
        for cp in mine:
            cp.wait_recv()
        for cp in first + passed + mine:
            cp.wait_send()

    return start, finish


def _gather_weights(shards, name):
    n = len(shards)

    def body(*refs):
        x, y, c, chips = _place()
        _handshake([(x, y, 1 - c)] + [(cx, cy, c) for cx, cy in chips])
        start, finish = _gather_parts([w.shape for w in shards], refs[:n], refs[n:2 * n], *refs[2 * n:2 * n + 2])
        start()
        finish()

    return _sequenced(body, name, shards, [jax.ShapeDtypeStruct((N_SHARD,) + w.shape, w.dtype) for w in shards],
                      7 * n, COLLECTIVE_IDS["gather"])


def _handshake(peers):
    barrier = pltpu.get_barrier_semaphore()
    for peer in peers:
        pl.semaphore_signal(barrier, inc=1, device_id=peer, device_id_type=MESH)
    pl.semaphore_wait(barrier, len(peers))


def _sequenced(body, name, ins, out_shapes, n_sems, collective_id):
    return pl.kernel(
        body, out_type=list(out_shapes), mesh=plsc.ScalarSubcoreMesh(axis_name="sequencer", num_cores=1), name=name,
        scratch_types=(pltpu.SemaphoreType.DMA((n_sems,)), pltpu.SemaphoreType.DMA((n_sems,))),
        compiler_params=pltpu.CompilerParams(collective_id=collective_id))(*ins)


def _swap_halves(gs, name, collective_id):
    n = len(gs)

    def body(*refs):
        g_refs, out_refs = refs[:n], refs[n:2 * n]
        send_sems, recv_sems = refs[2 * n:]
        x, y, c, _ = _place()
        _handshake([(x, y, 1 - c)])
        cps = []
        for k in range(n):
            H = gs[k].shape[1] // 2
            cp = _remote(g_refs[k].at[:, pl.ds((1 - c) * H, H), :], out_refs[k], send_sems, recv_sems, k, (x, y, 1 - c))
            cp.start()
            cps.append(cp)
        for cp in cps:
            cp.wait()

    return _sequenced(body, name, gs, [jax.ShapeDtypeStruct((g.shape[0], g.shape[1] // 2, g.shape[2]), g.dtype)
                                       for g in gs], n, collective_id)


def _exchange_chips(ts, name, collective_id):
    n = len(ts)

    def body(*refs):
        t_refs, out_refs = refs[:n], refs[n:2 * n]
        send_sems, recv_sems = refs[2 * n:]
        x, y, c, chips = _place()
        me = 2 * x + y
        _handshake([(cx, cy, c) for cx, cy in chips])
        sent = []
        for k in range(n):
            for j, (cx, cy) in enumerate(chips):
                cp = _remote(t_refs[k].at[2 * cx + cy], out_refs[k].at[me], send_sems, recv_sems, 3 * k + j, (cx, cy, c))
                cp.start()
                sent.append(cp)
        for k in range(n):
            for j, (cx, cy) in enumerate(chips):
                slot = out_refs[k].at[2 * cx + cy]
                _remote(slot, slot, send_sems, recv_sems, 3 * k + j, (cx, cy, c)).wait_recv()
        for cp in sent:
            cp.wait_send()

    return _sequenced(body, name, ts, [jax.ShapeDtypeStruct(t.shape, t.dtype) for t in ts], 3 * n, collective_id)


def _join_halves(fs, name):
    n = len(fs)

    def body(*refs):
        out_refs = refs[n:2 * n]
        send_sems, recv_sems, _ = refs[2 * n:]
        x, y, c, _ = _place()
        sent = []
        for k in range(n):
            H = fs[k].shape[0] // 2
            here = out_refs[k].at[pl.ds(c * H, H), :]
            cp = _remote(here, here, send_sems, recv_sems, k, (x, y, 1 - c))
            cp.start()
            sent.append(cp)
        for k in range(n):
            H = fs[k].shape[0] // 2
            other = out_refs[k].at[pl.ds((1 - c) * H, H), :]
            _remote(other, other, send_sems, recv_sems, k, (x, y, 1 - c)).wait_recv()
        for cp in sent:
            cp.wait_send()

    return pl.pallas_call(
        body, name=name,
        in_specs=[_ANY] * n, out_specs=[_ANY] * n,
        out_shape=[jax.ShapeDtypeStruct(f.shape, f.dtype) for f in fs],
        input_output_aliases={k: k for k in range(n)},
        scratch_shapes=[pltpu.SemaphoreType.DMA((n,)), pltpu.SemaphoreType.DMA((n,)), pltpu.SemaphoreType.DMA((1,))],
    )(*fs)


def _gather_small(v):
    R, C = v.shape

    def body(v_ref, out_ref, send_sems, recv_sems):
        x, y, c, _ = _place()
        me = 4 * x + 2 * y + c
        flips = [(fx, fy, fc) for fx in (0, 1) for fy in (0, 1) for fc in (0, 1)][1:]
        peers = [((1 - x) if fx else x, (1 - y) if fy else y, (1 - c) if fc else c) for fx, fy, fc in flips]
        _handshake(peers)
        sent = []
        for j, peer in enumerate(peers):
            cp = _remote(v_ref, out_ref.at[me], send_sems, recv_sems, j, peer)
            cp.start()
            sent.append(cp)
        for j, peer in enumerate(peers):
            slot = out_ref.at[4 * peer[0] + 2 * peer[1] + peer[2]]
            _remote(slot, slot, send_sems, recv_sems, j, peer).wait_recv()
        for cp in sent:
            cp.wait_send()

    return _sequenced(body, "gather_small", [v], [jax.ShapeDtypeStruct((8, R, C), F32)], 7,
                      COLLECTIVE_IDS["gather_small"])[0]


def _sum_devices(x, own, name):
    S, R, C = x.shape
    tr = _row_tile(R, 2048)

    def body(s_ref, x_ref, own_ref, o_ref):
        me = s_ref[0]
        acc = None
        for k in range(S):
            term = jnp.where(me == k, own_ref[...], x_ref[k])
            acc = term if acc is None else acc + term
        o_ref[...] = acc

    x_, y_, c_ = lax.axis_index("x"), lax.axis_index("y"), lax.axis_index("c")
    me = (4 * x_ + 2 * y_ + c_).astype(jnp.int32).reshape(1)
    return pl.pallas_call(
        body, name=name,
        grid_spec=pltpu.PrefetchScalarGridSpec(
            num_scalar_prefetch=1, grid=(R // tr,),
            in_specs=[pl.BlockSpec((S, tr, C), lambda i, s: (0, i, 0)), pl.BlockSpec((tr, C), lambda i, s: (i, 0))],
            out_specs=pl.BlockSpec((tr, C), lambda i, s: (i, 0))),
        out_shape=jax.ShapeDtypeStruct((R, C), F32),
        compiler_params=_params(),
    )(me, x, own)


def _after(earlier, arrays):
    return lax.optimization_barrier((earlier, arrays))


def _reduce_swap(gs, tag, earlier):
    gs = _after(earlier, gs)[1]
    return gs, _swap_halves(gs, "reduce_swap_" + tag, COLLECTIVE_IDS["swap_" + tag])


def _reduce_exchange(gs, r1, names, tag, later_than):
    r1 = _after(later_than, r1)[1]
    ts = [_add_halves(g, r, "reduce_add_cores_" + nm) for g, r, nm in zip(gs, r1, names)]
    us = _exchange_chips(ts, "reduce_exchange_" + tag, COLLECTIVE_IDS["exchange_" + tag])
    return us, ts


def _reduce_finish(us, ts, names, tag):
    fs = [_sum_chips_into_half(u, t, "reduce_add_chips_" + nm) for u, t, nm in zip(us, ts, names)]
    return _join_halves(fs, "reduce_join_" + tag)


BIG = ["ffn1_w_gate", "ffn1_w_up", "ffn1_w_down", "w_in", "ssm_w_glu", "w_attn_branch", "w_ssm_branch",
       "w_out", "ffn2_w_gate", "ffn2_w_up", "ffn2_w_down"]
SMALL = ["ffn1_norm", "mix_norm", "gate_bias", "rel_bias_table", "ssm_a_re", "ssm_a_im", "ssm_log_dt",
         "ssm_b_re", "ssm_b_im", "ssm_c_re", "ssm_c_im", "ssm_d", "ffn2_norm", "final_norm"]
ORDER = ["ffn1_norm", "ffn1_w_gate", "ffn1_w_up", "ffn1_w_down", "mix_norm", "w_in", "gate_bias", "rel_bias_table",
         "ssm_a_re", "ssm_a_im", "ssm_log_dt", "ssm_b_re", "ssm_b_im", "ssm_c_re", "ssm_c_im", "ssm_d",
         "ssm_w_glu", "w_attn_branch", "w_ssm_branch", "w_out", "ffn2_norm", "ffn2_w_gate", "ffn2_w_up",
         "ffn2_w_down", "final_norm"]


_SMALL_TILE = 8 * LANES


def _pack_small(arrays):
    rows = []
    for a in arrays:
        flat = a.reshape(-1).astype(F32)
        rows.append(jnp.pad(flat, (0, (-flat.shape[0]) % _SMALL_TILE)).reshape(-1, LANES))
    return jnp.concatenate(rows, axis=0)


def _unpack_small(packed, shapes):
    out, r0 = [], 0
    for shp in shapes:
        n = math.prod(shp)
        rows = 8 * -(-n // _SMALL_TILE)
        out.append(packed[r0:r0 + rows].reshape(-1)[:n].reshape(shp))
        r0 += rows
    return out


def _split_cols(g):
    K, N = g.shape
    return g.reshape(K, N_SHARD, N // N_SHARD).transpose(1, 0, 2)


def _join_cols(w):
    S, K, n = w.shape
    return w.transpose(1, 0, 2).reshape(K, S * n)


COL_SHARDED = ("ssm_w_glu", "w_attn_branch", "w_ssm_branch")
TRANSPOSED = ("ffn1_w_gate", "ffn1_w_up", "ffn2_w_gate", "ffn2_w_up", "w_in")


def _shard_2d(name, arr):
    two_d = arr.reshape(arr.shape[-2:])
    return two_d.T if name in TRANSPOSED else two_d


def _shard_nd(name, two_d, shape):
    return (two_d.T if name in TRANSPOSED else two_d).reshape(shape)


class _GradSync:
    def __init__(self, weights, moms, vels):
        self.weights, self.moms, self.vels = weights, moms, vels
        self.grads, self.delta, self.new_m, self.new_v = {}, {}, {}, {}
        self.loss = None
        self._earlier = []
        self._swapped = {}
        self._exchanged = {}

    def swap(self, tag, gw, later_than=()):
        gs = []
        for n in REDUCE_GROUPS[tag]:
            g = gw[n]
            if n in COL_SHARDED:
                g = _split_cols(g)
            elif n in ("w_out", "w_in"):
                g = g.reshape(N_SHARD, g.shape[0] // N_SHARD, g.shape[1])
            gs.append(g)
        self._swapped[tag] = _reduce_swap(gs, tag, list(self._earlier) + list(later_than))
        self._earlier = self._swapped[tag][1]

    def exchange(self, tag, later_than):
        gs, r1 = self._swapped[tag]
        us, ts = _reduce_exchange(gs, r1, REDUCE_GROUPS[tag], tag, later_than)
        self._exchanged[tag] = (us, ts)
        self._earlier = us

    def small_ready(self, gs, loss_blk, later_than=()):
        _, (mine,) = _after(list(self._earlier) + list(later_than),
                            [_pack_small([gs[n] for n in SMALL] + [loss_blk[0:1, :]])])
        others = _gather_small(mine)
        self._exchanged["small"] = (others, mine)
        self._earlier = [others]

    def finish(self, tag):
        made = []
        if tag == "small":
            others, mine = self._exchanged[tag]
            shapes = [self.weights[n].shape for n in SMALL]
            total = _unpack_small(_sum_devices(others, mine, "sum_small"), shapes + [(128,)])
            self.loss = total[-1][0]
            self.grads.update(zip(SMALL, total[:-1]))
            packed = [_pack_small([src[n] for n in SMALL]) for src in (self.weights, self.grads, self.moms, self.vels)]
            for dst, res in zip((self.delta, self.new_m, self.new_v), _adamw(*packed, "adamw_small")):
                dst.update(zip(SMALL, _unpack_small(res, shapes)))
            for n in SMALL:
                made += [self.grads[n], self.delta[n], self.new_m[n], self.new_v[n]]
            return made + [self.loss]
        names = REDUCE_GROUPS[tag]
        us, ts = self._exchanged[tag]
        for n, g in zip(names, _reduce_finish(us, ts, names, tag)):
            shp = self.weights[n].shape
            d, m, v = _adamw(_shard_2d(n, self.weights[n]), g, _shard_2d(n, self.moms[n]), _shard_2d(n, self.vels[n]),
                             "adamw_" + n)
            self.grads[n], self.delta[n] = _shard_nd(n, g, shp), _shard_nd(n, d, shp)
            self.new_m[n], self.new_v[n] = _shard_nd(n, m, shp), _shard_nd(n, v, shp)
            made += [self.grads[n], self.delta[n], self.new_m[n], self.new_v[n]]
        return made

    def finish_all(self):
        self.exchange("ffn1", later_than=self.finish("ffn2"))
        for tag in ("mixer", "w_in", "small", "ffn1"):
            self.finish(tag)


def _local_step(x, target, w, later, small, sync):
    L = x.shape[0]
    row = lambda v: v.reshape(1, -1)

    a_re, a_im = small["ssm_a_re"].reshape(1, NS), small["ssm_a_im"].reshape(1, NS)
    ldt = jnp.repeat(small["ssm_log_dt"].reshape(SSM_GROUPS), SSM_STATE).reshape(1, NS)
    to_cn = lambda b: b.reshape(SSM_GROUPS, SSM_STATE, SSM_GROUP).transpose(2, 0, 1).reshape(SSM_GROUP, NS)
    c_to_cn = lambda c: c.reshape(SSM_GROUPS, SSM_GROUP, SSM_STATE).transpose(1, 0, 2).reshape(SSM_GROUP, NS)
    b_re, b_im = to_cn(small["ssm_b_re"]), to_cn(small["ssm_b_im"])
    c_re, c_im = c_to_cn(small["ssm_c_re"]), c_to_cn(small["ssm_c_im"])
    d_skip = row(small["ssm_d"])
    pw, pwr, bd, cdt = _disc_fwd(a_re, a_im, ldt, b_re, b_im, c_re, c_im)

    onehot = _bucket_onehot()
    table_t = small["rel_bias_table"].T.reshape(3, HEADS_PER_GROUP, N_BUCKETS)
    table_t = jnp.pad(table_t, ((0, 0), (0, 8 - HEADS_PER_GROUP), (0, 0)))
    bias = _bias_expand(table_t, onehot)[:, :, :HEADS_PER_GROUP].reshape(
        3, 2, HEADS_PER_GROUP, ATTN_BLOCK, 2 * ATTN_BLOCK)

    n1, nm, n2, nf = row(small["ffn1_norm"]), row(small["mix_norm"]), row(small["ffn2_norm"]), row(small["final_norm"])
    gate_bias = row(small["gate_bias"])

    x1, a1, b1, *later_full = _ffn_fwd(x, n1, w["ffn1_w_gate"], w["ffn1_w_up"], w["ffn1_w_down"], "ffn1_fwd",
                                       carried=list(later.values()))
    w = dict(w, **dict(zip(later, later_full)))
    for n in COL_SHARDED:
        w[n] = _join_cols(w[n])
    w["w_out"] = w["w_out"].reshape(D_MODEL, D_MODEL)
    w["w_in"] = w["w_in"].reshape(IN_WIDTH, D_MODEL)
    *qkv, u, gates = _mix_in_fwd(x1, nm, w["w_in"], gate_bias)
    q, k, v = qkv[0:3], qkv[3:6], qkv[6:9]
    o_g, lse_g = [], []
    for grp in range(3):
        o, lse = _attn_fwd(q[grp], k[grp], v[grp], bias[grp], f"attn_fwd_{grp}")
        o_g.append(o)
        lse_g.append(lse)
    y, s = _ssm_fwd(u, bd, cdt, d_skip, pw)
    x2, o_attn, *lse_tot = _mix_out_fwd(x1, o_g, lse_g, y, gates, w["w_attn_branch"], w["ssm_w_glu"],
                                        w["w_ssm_branch"], w["w_out"])
    x3, a2, b2 = _ffn_fwd(x2, n2, w["ffn2_w_gate"], w["ffn2_w_up"], w["ffn2_w_down"], "ffn2_fwd")
    loss_blk, dx3, d_nf = _loss_fwd_bwd(x3, nf, target)

    gw, gs = {}, {}
    gs["final_norm"] = d_nf

    dx2, da, db, sact, h, d_out, gs["ffn2_norm"] = _ffn_bwd(dx3, x2, n2, a2, b2, w["ffn2_w_gate"], w["ffn2_w_up"],
                                                            w["ffn2_w_down"], "ffn2_bwd")
    gw["ffn2_w_gate"] = _matmul_tn(da, h[None], "ffn2_dw_gate")
    gw["ffn2_w_up"] = _matmul_tn(db, h[None], "ffn2_dw_up")
    gw["ffn2_w_down"] = _matmul_tn(sact, d_out[None], "ffn2_dw_down")
    sync.swap("ffn2", gw)

    head_sum = (jnp.arange(GROUP_WIDTH)[:, None] // HEAD_DIM == jnp.arange(GROUP_WIDTH)[None, :] // HEAD_DIM).astype(F32)
    (*d_o_delta, dy, dgp, mix, dya, dys, ys2, gel, dglu, gs["gate_bias"]) = _mix_out_bwd(
        dx2, o_attn, y, gates, w["w_attn_branch"], w["ssm_w_glu"], w["w_ssm_branch"], w["w_out"], head_sum)
    sync.exchange("ffn2", later_than=[dy])
    d_o, delta = d_o_delta[0:3], d_o_delta[3:6]
    gw["w_out"] = _matmul_tn(mix[None], dx2[None], "dw_out")[0]
    gw["w_attn_branch"] = _matmul_tn(o_attn[None], dya[None], "dw_attn_branch")[0]
    gw["w_ssm_branch"] = _matmul_tn(ys2[None], dys[None], "dw_ssm_branch")[0]
    gw["ssm_w_glu"] = _matmul_tn(gel[None], dglu[None], "dw_glu")[0]

    dqs, dks, dvs, dsums = [], [], [], []
    for grp in range(3):
        dq, dk, dv, dsum = _attn_bwd(q[grp], k[grp], v[grp], d_o[grp], lse_tot[grp], delta[grp], bias[grp],
                                     f"attn_bwd_{grp}")
        dqs.append(dq)
        dks.append(dk)
        dvs.append(dv)
        dsums.append(dsum.reshape(HEADS_PER_GROUP, -1))
    dsum_all = jnp.pad(jnp.stack(dsums), ((0, 0), (0, 8 - HEADS_PER_GROUP), (0, 0)))
    d_table = _bias_reduce(dsum_all, onehot)[:, :HEADS_PER_GROUP]
    gs["rel_bias_table"] = d_table.reshape(3 * HEADS_PER_GROUP, N_BUCKETS).T

    du, gs["ssm_d"], d_bd, d_cdt, d_ab = _ssm_bwd(dy, u, s, bd, cdt, d_skip, pwr)
    sync.swap("mixer", gw, later_than=[du])
    sync.exchange("mixer", later_than=[dqs[2]])
    group_sum =(jnp.arange(NS)[:, None] // SSM_STATE == jnp.arange(128)[None, :]).astype(F32)
    d_are, d_aim, d_ldt, d_bre, d_bim, d_cre, d_cim = _disc_bwd(a_re, a_im, ldt, b_re, b_im, d_bd, d_cdt, d_ab, group_sum)
    gs["ssm_a_re"], gs["ssm_a_im"] = d_are, d_aim
    gs["ssm_log_dt"] = d_ldt[0, :SSM_GROUPS]
    from_cn = lambda t: t.reshape(SSM_GROUP, SSM_GROUPS, SSM_STATE).transpose(1, 2, 0)
    c_from_cn = lambda t: t.reshape(SSM_GROUP, SSM_GROUPS, SSM_STATE).transpose(1, 0, 2)
    gs["ssm_b_re"], gs["ssm_b_im"] = from_cn(d_bre), from_cn(d_bim)
    gs["ssm_c_re"], gs["ssm_c_im"] = c_from_cn(d_cre), c_from_cn(d_cim)

    dx1, hm, dz, gs["mix_norm"] = _mix_in_bwd(dx2, x1, nm, dqs + dks + dvs, du, dgp, w["w_in"])
    gw["w_in"] = _matmul_tn(dz[None], hm[None], "dw_in")[0]
    sync.swap("w_in", gw)

    dx0, da, db, sact, h, d_out, gs["ffn1_norm"] = _ffn_bwd(dx1, x, n1, a1, b1, w["ffn1_w_gate"], w["ffn1_w_up"],
                                                            w["ffn1_w_down"], "ffn1_bwd")
    sync.exchange("w_in", later_than=[dx0])
    gw["ffn1_w_gate"] = _matmul_tn(da, h[None], "ffn1_dw_gate")
    gw["ffn1_w_up"] = _matmul_tn(db, h[None], "ffn1_dw_up")
    sync.small_ready(gs, loss_blk, later_than=[gw["ffn1_w_up"]])
    gw["ffn1_w_down"] = _matmul_tn(sact, d_out[None], "ffn1_dw_down")
    sync.swap("ffn1", gw)
    return dx0


def kernel(x, ffn1_norm, ffn1_w_gate, ffn1_w_up, ffn1_w_down, mix_norm, w_in, gate_bias, rel_bias_table, ssm_a_re, ssm_a_im, ssm_log_dt, ssm_b_re, ssm_b_im, ssm_c_re, ssm_c_im, ssm_d, ssm_w_glu, w_attn_branch, w_ssm_branch, w_out, ffn2_norm, ffn2_w_gate, ffn2_w_up, ffn2_w_down, final_norm, loss_target, m_ffn1_norm, m_ffn1_w_gate, m_ffn1_w_up, m_ffn1_w_down, m_mix_norm, m_w_in, m_gate_bias, m_rel_bias_table, m_ssm_a_re, m_ssm_a_im, m_ssm_log_dt, m_ssm_b_re, m_ssm_b_im, m_ssm_c_re, m_ssm_c_im, m_ssm_d, m_ssm_w_glu, m_w_attn_branch, m_w_ssm_branch, m_w_out, m_ffn2_norm, m_ffn2_w_gate, m_ffn2_w_up, m_ffn2_w_down, m_final_norm, v_ffn1_norm, v_ffn1_w_gate, v_ffn1_w_up, v_ffn1_w_down, v_mix_norm, v_w_in, v_gate_bias, v_rel_bias_table, v_ssm_a_re, v_ssm_a_im, v_ssm_log_dt, v_ssm_b_re, v_ssm_b_im, v_ssm_c_re, v_ssm_c_im, v_ssm_d, v_ssm_w_glu, v_w_attn_branch, v_w_ssm_branch, v_w_out, v_ffn2_norm, v_ffn2_w_gate, v_ffn2_w_up, v_ffn2_w_down, v_final_norm):
    args = dict(locals())
    weights = {n: args[n] for n in ORDER}
    moms = {n: args["m_" + n] for n in ORDER}
    vels = {n: args["v_" + n] for n in ORDER}

    shard2d = {n: _shard_2d(n, weights[n]) for n in BIG}
    first, rest = BIG[:3], BIG[3:]
    full = dict(zip(first, _gather_weights([shard2d[n].astype(BF16) for n in first], "gather_ffn1_weights")))
    later = {n: shard2d[n].astype(BF16) for n in rest}

    small = {n: weights[n] for n in SMALL}
    sync = _GradSync(weights, moms, vels)
    grad_x = _local_step(x[0], loss_target[0], full, later, small, sync)
    sync.finish_all()
    return (sync.loss, grad_x[None], *[sync.grads[n] for n in ORDER], *[sync.delta[n] for n in ORDER],
            *[sync.new_m[n] for n in ORDER], *[sync.new_v[n] for n in ORDER])
```

```python
import functools
import math

import jax
import jax.numpy as jnp
from jax import lax
from jax.experimental import pallas as pl
from jax.experimental.pallas import tpu as pltpu
from jax.experimental.pallas import tpu_sc as plsc

F32 = jnp.float32
BF16 = jnp.bfloat16
MESH = pl.DeviceIdType.MESH

D_MODEL = 1024
D_FF = 2816
HEAD_DIM = 64
HEADS_PER_GROUP = 4
DILATIONS = (1, 4, 16)
WINDOW_STEPS = 128
ATTN_BLOCK = 128
ATTN_QB = 4
GROUP_WIDTH = HEADS_PER_GROUP * HEAD_DIM
ATTN_WIDTH = 3 * GROUP_WIDTH
N_BUCKETS = 32
MAX_DISTANCE = 2048
NEG_INF = -1e30
SSM_WIDTH = 512
SSM_GROUP = 16
SSM_GROUPS = 32
SSM_STATE = 64
NS = SSM_GROUPS * SSM_STATE
EPS = 1e-6
IN_WIDTH = 3 * ATTN_WIDTH + SSM_WIDTH + 2 * D_MODEL
Q_SCALE = HEAD_DIM ** -0.5
N_SHARD = 4
FF_SHARD = D_FF // N_SHARD
ADAM_LR, ADAM_B1, ADAM_B2, ADAM_EPS, ADAM_WD, ADAM_STEP = 0.001, 0.9, 0.999, 1e-08, 0.01, 10

LANES = 128
VMEM_LIMIT = 56 * 1024 * 1024
ROW_TILE = 512
FFN_BWD_TILE = 256
SSM_CHUNK = 256
SSM_FWD_CHUNK = 512
SCAN_LANES = 512
ADAMW_BLOCK_BYTES = 2 << 20
TN_VMEM_BUDGET = 40 * 1024 * 1024
REDUCE_GROUPS = {
    "ffn2": ["ffn2_w_gate", "ffn2_w_up", "ffn2_w_down"],
    "mixer": ["w_out", "w_attn_branch", "w_ssm_branch", "ssm_w_glu"],
    "w_in": ["w_in"],
    "ffn1": ["ffn1_w_gate", "ffn1_w_up", "ffn1_w_down"],
}
COLLECTIVE_IDS = {name: i for i, name in enumerate(
    ["gather", "gather_small"] + [stage + "_" + tag for tag in REDUCE_GROUPS for stage in ("swap", "exchange")])}


def _params(**kw):
    return pltpu.CompilerParams(vmem_limit_bytes=VMEM_LIMIT, **kw)


def _dot(a, b):
    return jnp.dot(a, b, preferred_element_type=F32)


def _dot_nt(a, b):
    return lax.dot_general(a, b, (((1,), (1,)), ((), ())), preferred_element_type=F32)


def _dot_tn(a, b):
    return lax.dot_general(a, b, (((0,), (0,)), ((), ())), preferred_element_type=F32)


def _dot_exact(a, b):
    return jnp.dot(a, b, preferred_element_type=F32, precision=lax.Precision.HIGHEST)


def _dot_nt_exact(a, b):
    return lax.dot_general(a, b, (((1,), (1,)), ((), ())), preferred_element_type=F32,
                           precision=lax.Precision.HIGHEST)


def _rms(x):
    r = lax.rsqrt(jnp.mean(x * x, axis=-1, keepdims=True) + EPS)
    return r, x * r


def _rms_bwd(dh, g, r, xhat):
    dxh = dh * g
    return r * (dxh - xhat * jnp.mean(dxh * xhat, axis=-1, keepdims=True))


def _sigmoid(x):
    return 0.5 + 0.5 * jnp.tanh(0.5 * x)


_GELU_C = math.sqrt(2.0 / math.pi)


def _gelu(x):
    return 0.5 * x * (1.0 + jnp.tanh(_GELU_C * (x + 0.044715 * x * x * x)))


def _gelu_grad(x):
    t = jnp.tanh(_GELU_C * (x + 0.044715 * x * x * x))
    return 0.5 * (1.0 + t) + 0.5 * x * (1.0 - t * t) * _GELU_C * (1.0 + 3 * 0.044715 * x * x)


def _whole():
    return pl.BlockSpec(memory_space=pltpu.VMEM)


def _row_tile(rows, cap):
    if rows <= cap:
        return rows
    return max(t for t in range(8, cap + 1, 8) if rows % t == 0)


def _rows(tm, w):
    return pl.BlockSpec((tm, w), lambda i: (i, 0))


def _acc_row(w):
    return pl.BlockSpec((1, w), lambda i: (0, 0))


def _ffn_fwd(x, g, wg, wu, wd, name, carried=()):
    L = x.shape[0]
    tm = min(ROW_TILE, L)
    n = len(carried)
    steps = L // tm

    def body(x_ref, g_ref, wg_ref, wu_ref, wd_ref, *refs):
        shard_refs, (xo_ref, a_ref, b_ref), full_refs, sems = refs[:n], refs[n:n + 3], refs[n + 3:2 * n + 3], refs[2 * n + 3:]
        if n:
            start, finish = _gather_parts([w.shape for w in carried], shard_refs, full_refs, *sems)
            pl.when(pl.program_id(0) == 0)(start)
        xv = x_ref[...]
        r, xhat = _rms(xv)
        h = (xhat * g_ref[...]).astype(BF16)
        acc = jnp.zeros((tm, D_MODEL), F32)
        for j in range(N_SHARD):
            a = _dot_nt(h, wg_ref[j])
            b = _dot_nt(h, wu_ref[j])
            a_ref[j] = a.astype(BF16)
            b_ref[j] = b.astype(BF16)
            s = (a * _sigmoid(a) * b).astype(BF16)
            acc = acc + _dot(s, wd_ref[j])
        xo_ref[...] = xv + 0.5 * acc
        if n:
            pl.when(pl.program_id(0) == steps - 1)(finish)

    act = pl.BlockSpec((N_SHARD, tm, FF_SHARD), lambda i: (0, i, 0))
    return pl.pallas_call(
        body, name=name, grid=(steps,),
        in_specs=[_rows(tm, D_MODEL), _whole(), _whole(), _whole(), _whole()] + [_ANY] * n,
        out_specs=[_rows(tm, D_MODEL), act, act] + [_ANY] * n,
        out_shape=[jax.ShapeDtypeStruct((L, D_MODEL), F32),
                   jax.ShapeDtypeStruct((N_SHARD, L, FF_SHARD), BF16),
                   jax.ShapeDtypeStruct((N_SHARD, L, FF_SHARD), BF16)]
        + [jax.ShapeDtypeStruct((N_SHARD,) + w.shape, w.dtype) for w in carried],
        scratch_shapes=[pltpu.SemaphoreType.DMA((7 * n,)), pltpu.SemaphoreType.DMA((7 * n,))] if n else [],
        compiler_params=_params(),
    )(x, g, wg, wu, wd, *carried)


def _ffn_bwd(dxo, x, g, a, b, wg, wu, wd, name):
    L = x.shape[0]
    tm = min(FFN_BWD_TILE, L)

    def body(dxo_ref, x_ref, g_ref, a_ref, b_ref, wg_ref, wu_ref, wd_ref,
             dxi_ref, da_ref, db_ref, s_ref, h_ref, do_ref, dg_ref):
        i = pl.program_id(0)
        xv = x_ref[...]
        gv = g_ref[...]
        r, xhat = _rms(xv)
        h_ref[...] = (xhat * gv).astype(BF16)
        dxo_v = dxo_ref[...]
        d_out = (0.5 * dxo_v).astype(BF16)
        do_ref[...] = d_out
        dh = jnp.zeros((tm, D_MODEL), F32)
        for j in range(N_SHARD):
            av = a_ref[j].astype(F32)
            bv = b_ref[j].astype(F32)
            sg = _sigmoid(av)
            sl = av * sg
            ds = _dot_nt(d_out, wd_ref[j])
            dbv = (ds * sl).astype(BF16)
            dav = (ds * bv * (sg * (1.0 + av * (1.0 - sg)))).astype(BF16)
            da_ref[j] = dav
            db_ref[j] = dbv
            s_ref[j] = (sl * bv).astype(BF16)
            dh = dh + _dot(dav, wg_ref[j]) + _dot(dbv, wu_ref[j])

        @pl.when(i == 0)
        def _():
            dg_ref[...] = jnp.zeros_like(dg_ref)

        dg_ref[...] += jnp.sum(dh * xhat, axis=0, keepdims=True)
        dxi_ref[...] = dxo_v + _rms_bwd(dh, gv, r, xhat)

    act = pl.BlockSpec((N_SHARD, tm, FF_SHARD), lambda i: (0, i, 0))
    act_shape = jax.ShapeDtypeStruct((N_SHARD, L, FF_SHARD), BF16)
    return pl.pallas_call(
        body, name=name, grid=(L // tm,),
        in_specs=[_rows(tm, D_MODEL), _rows(tm, D_MODEL), _whole(), act, act, _whole(), _whole(), _whole()],
        out_specs=[_rows(tm, D_MODEL), act, act, act, _rows(tm, D_MODEL), _rows(tm, D_MODEL), _acc_row(D_MODEL)],
        out_shape=[jax.ShapeDtypeStruct((L, D_MODEL), F32), act_shape, act_shape, act_shape,
                   jax.ShapeDtypeStruct((L, D_MODEL), BF16), jax.ShapeDtypeStruct((L, D_MODEL), BF16),
                   jax.ShapeDtypeStruct((1, D_MODEL), F32)],
        compiler_params=_params(),
    )(dxo, x, g, a, b, wg, wu, wd)


def _matmul_tn(a, b, name):
    ja, L, K = a.shape
    jb, _, N = b.shape
    J = max(ja, jb)
    splits = [s for s in (1, 2, 4, 8) if s == 1 or N % (s * LANES) == 0]
    nsplit = next((s for s in splits if 2 * K * (N // s) * 4 <= TN_VMEM_BUDGET // 2), splits[-1])
    nc = N // nsplit
    left = TN_VMEM_BUDGET - 2 * K * nc * 4
    row_bytes = 2 * (K * a.dtype.itemsize + nc * b.dtype.itemsize)
    tm = next((t for t in (2048, 1024, 512, 256) if L % t == 0 and t * row_bytes <= left), min(128, L))

    def body(a_ref, b_ref, o_ref):
        @pl.when(pl.program_id(2) == 0)
        def _():
            o_ref[...] = jnp.zeros_like(o_ref)

        o_ref[...] += _dot_tn(a_ref[...].astype(BF16), b_ref[...].astype(BF16))

    return pl.pallas_call(
        body, name=name, grid=(J, nsplit, L // tm),
        in_specs=[pl.BlockSpec((None, tm, K), (lambda j, s, i: (j, i, 0)) if ja > 1 else (lambda j, s, i: (0, i, 0))),
                  pl.BlockSpec((None, tm, nc), (lambda j, s, i: (j, i, s)) if jb > 1 else (lambda j, s, i: (0, i, s)))],
        out_specs=pl.BlockSpec((None, K, nc), lambda j, s, i: (j, 0, s)),
        out_shape=jax.ShapeDtypeStruct((J, K, N), F32),
        compiler_params=_params(),
    )(a, b)


def _loss_fwd_bwd(x, g, target):
    L = x.shape[0]
    tm = min(ROW_TILE, L)

    def body(x_ref, g_ref, t_ref, loss_ref, dx_ref, dg_ref):
        i = pl.program_id(0)
        xv = x_ref[...]
        gv = g_ref[...]
        r, xhat = _rms(xv)
        err = xhat * gv - t_ref[...]
        part = 0.5 * jnp.sum(jnp.sum(err * err, axis=1, keepdims=True) * (1.0 / D_MODEL), axis=0, keepdims=True)
        dy = err * (1.0 / D_MODEL)

        @pl.when(i == 0)
        def _():
            dg_ref[...] = jnp.zeros_like(dg_ref)
            loss_ref[...] = jnp.zeros_like(loss_ref)

        loss_ref[...] += jnp.broadcast_to(part, loss_ref.shape)
        dg_ref[...] += jnp.sum(dy * xhat, axis=0, keepdims=True)
        dx_ref[...] = _rms_bwd(dy, gv, r, xhat)

    return pl.pallas_call(
        body, name="loss_fwd_bwd", grid=(L // tm,),
        in_specs=[_rows(tm, D_MODEL), _whole(), _rows(tm, D_MODEL)],
        out_specs=[pl.BlockSpec((8, 128), lambda i: (0, 0)), _rows(tm, D_MODEL), _acc_row(D_MODEL)],
        out_shape=[jax.ShapeDtypeStruct((8, 128), F32), jax.ShapeDtypeStruct((L, D_MODEL), F32),
                   jax.ShapeDtypeStruct((1, D_MODEL), F32)],
        compiler_params=_params(),
    )(x, g, target)


_C_K = ATTN_WIDTH
_C_V = 2 * ATTN_WIDTH
_C_U = 3 * ATTN_WIDTH
_C_G = _C_U + SSM_WIDTH


def _residue_spec(d, tm):
    return pl.BlockSpec((d, tm // d, GROUP_WIDTH), lambda i: (0, i, 0))


def _residue_shape(d, L, dtype):
    return jax.ShapeDtypeStruct((d, L // d, GROUP_WIDTH), dtype)


def _residue_scratch(tm):
    return pltpu.VMEM((GROUP_WIDTH // LANES, tm, LANES), F32)


def _to_residues(val, out_ref, scr, d):
    if d == 1:
        out_ref[0] = val.astype(out_ref.dtype)
        return
    tm = val.shape[0]
    for half in range(GROUP_WIDTH // LANES):
        cols = slice(half * LANES, (half + 1) * LANES)
        scr[half] = val[:, cols]
        for r in range(d):
            out_ref[r, :, cols] = scr[half, pl.ds(r, tm // d, stride=d), :].astype(out_ref.dtype)


def _from_residues(ref, scr, d):
    if d == 1:
        return ref[0].astype(F32)
    rows = ref.shape[1]
    for half in range(GROUP_WIDTH // LANES):
        cols = slice(half * LANES, (half + 1) * LANES)
        for r in range(d):
            scr[half, pl.ds(r, rows, stride=d), :] = ref[r, :, cols].astype(F32)
    return jnp.concatenate([scr[half] for half in range(GROUP_WIDTH // LANES)], axis=1)


def _mix_in_fwd(x, g, w_in, gate_bias):
    L = x.shape[0]
    tm = min(ROW_TILE, L)

    def body(x_ref, g_ref, w_ref, gb_ref, *refs):
        qkv_refs, (u_ref, gate_ref, scr) = refs[:9], refs[9:]
        r, xhat = _rms(x_ref[...])
        h = (xhat * g_ref[...]).astype(BF16)
        for part, (c0, scale) in enumerate(((0, Q_SCALE), (_C_K, 1.0), (_C_V, 1.0))):
            z = _dot_nt(h, w_ref[c0:c0 + ATTN_WIDTH, :]) * scale
            for grp, d in enumerate(DILATIONS):
                _to_residues(z[:, grp * GROUP_WIDTH:(grp + 1) * GROUP_WIDTH], qkv_refs[3 * part + grp], scr, d)
        u_ref[...] = _dot_nt(h, w_ref[_C_U:_C_G, :])
        gate_ref[...] = _sigmoid(_dot_nt(h, w_ref[_C_G:IN_WIDTH, :]) + gb_ref[...])

    return pl.pallas_call(
        body, name="mix_in_fwd", grid=(L // tm,),
        in_specs=[_rows(tm, D_MODEL), _whole(), _whole(), _whole()],
        out_specs=[_residue_spec(d, tm) for d in DILATIONS] * 3 + [_rows(tm, SSM_WIDTH), _rows(tm, 2 * D_MODEL)],
        out_shape=[_residue_shape(d, L, BF16) for d in DILATIONS] * 3
        + [jax.ShapeDtypeStruct((L, SSM_WIDTH), F32), jax.ShapeDtypeStruct((L, 2 * D_MODEL), F32)],
        scratch_shapes=[_residue_scratch(tm)],
        compiler_params=_params(),
    )(x, g, w_in, gate_bias)


def _mix_in_bwd(dx2, x, g, dqkv, du, dgp, w_in):
    L = x.shape[0]
    tm = min(ROW_TILE, L)

    def body(dx2_ref, x_ref, g_ref, *refs):
        piece_refs = refs[:9]
        du_ref, dgp_ref, w_ref, dx1_ref, h_ref, dz_ref, dg_ref, scr = refs[9:]
        i = pl.program_id(0)
        gv = g_ref[...]
        r, xhat = _rms(x_ref[...])
        h_ref[...] = (xhat * gv).astype(BF16)
        for part in range(3):
            for grp, d in enumerate(DILATIONS):
                c0 = part * ATTN_WIDTH + grp * GROUP_WIDTH
                dz_ref[:, c0:c0 + GROUP_WIDTH] = _from_residues(piece_refs[3 * part + grp], scr, d).astype(BF16)
        dz_ref[:, _C_U:_C_G] = du_ref[...].astype(BF16)
        dz_ref[:, _C_G:IN_WIDTH] = dgp_ref[...]
        dh = _dot(dz_ref[...], w_ref[...])

        @pl.when(i == 0)
        def _():
            dg_ref[...] = jnp.zeros_like(dg_ref)

        dg_ref[...] += jnp.sum(dh * xhat, axis=0, keepdims=True)
        dx1_ref[...] = dx2_ref[...] + _rms_bwd(dh, gv, r, xhat)

    return pl.pallas_call(
        body, name="mix_in_bwd", grid=(L // tm,),
        in_specs=[_rows(tm, D_MODEL), _rows(tm, D_MODEL), _whole()] + [_residue_spec(d, tm) for d in DILATIONS] * 3
        + [_rows(tm, SSM_WIDTH), _rows(tm, 2 * D_MODEL), _whole()],
        out_specs=[_rows(tm, D_MODEL), _rows(tm, D_MODEL), _rows(tm, IN_WIDTH), _acc_row(D_MODEL)],
        out_shape=[jax.ShapeDtypeStruct((L, D_MODEL), F32), jax.ShapeDtypeStruct((L, D_MODEL), BF16),
                   jax.ShapeDtypeStruct((L, IN_WIDTH), BF16), jax.ShapeDtypeStruct((1, D_MODEL), F32)],
        scratch_shapes=[_residue_scratch(tm)],
        compiler_params=_params(),
    )(dx2, x, g, *dqkv, du, dgp, w_in)


def _bucket_onehot():
    qi = jnp.arange(ATTN_BLOCK)[:, None]
    kj = jnp.arange(2 * ATTN_BLOCK)[None, :]
    steps = jnp.maximum(qi + ATTN_BLOCK - kj, 0)
    max_exact = N_BUCKETS // 2
    out = []
    for d in DILATIONS:
        dist = steps * d
        df = jnp.maximum(dist, 1).astype(F32)
        large = max_exact + (jnp.log(df / max_exact) / math.log(MAX_DISTANCE / max_exact)
                             * (N_BUCKETS - max_exact)).astype(jnp.int32)
        large = jnp.minimum(large, N_BUCKETS - 1)
        bucket = jnp.where(dist < max_exact, dist, large).reshape(-1)
        out.append((bucket[None, :] == jnp.arange(N_BUCKETS)[:, None]).astype(F32))
    return jnp.stack(out)


def _bias_expand(table_t, onehot):
    n = onehot.shape[-1]

    def body(t_ref, oh_ref, o_ref):
        bias = _dot_exact(t_ref[...], oh_ref[...])
        col = lax.broadcasted_iota(jnp.int32, (8, n), 1)
        qi = col // (2 * ATTN_BLOCK)
        kj = col - qi * (2 * ATTN_BLOCK)
        steps = qi + ATTN_BLOCK - kj
        band = (steps >= 0) & (steps <= WINDOW_STEPS)
        o_ref[0] = jnp.where(band & (kj >= ATTN_BLOCK), bias, NEG_INF)
        o_ref[1] = jnp.where(band, bias, NEG_INF)

    return pl.pallas_call(
        body, name="bias_expand", grid=(3,),
        in_specs=[pl.BlockSpec((None, 8, N_BUCKETS), lambda g: (g, 0, 0)),
                  pl.BlockSpec((None, N_BUCKETS, n), lambda g: (g, 0, 0))],
        out_specs=pl.BlockSpec((None, 2, 8, n), lambda g: (g, 0, 0, 0)),
        out_shape=jax.ShapeDtypeStruct((3, 2, 8, n), F32),
        compiler_params=_params(),
    )(table_t, onehot)


def _bias_reduce(dsum, onehot):
    n = onehot.shape[-1]

    def body(d_ref, oh_ref, o_ref):
        o_ref[...] = _dot_nt_exact(d_ref[...], oh_ref[...])

    return pl.pallas_call(
        body, name="bias_reduce", grid=(3,),
        in_specs=[pl.BlockSpec((None, 8, n), lambda g: (g, 0, 0)),
                  pl.BlockSpec((None, N_BUCKETS, n), lambda g: (g, 0, 0))],
        out_specs=pl.BlockSpec((None, 8, N_BUCKETS), lambda g: (g, 0, 0)),
        out_shape=jax.ShapeDtypeStruct((3, 8, N_BUCKETS), F32),
        compiler_params=_params(),
    )(dsum, onehot)


def _head_of_col(rows):
    return lax.broadcasted_iota(jnp.int32, (rows, GROUP_WIDTH), 1) // HEAD_DIM


_STACK_ROWS = HEADS_PER_GROUP * ATTN_BLOCK


def _stack_heads(x, head_of_col):
    return jnp.concatenate([jnp.where(head_of_col == hh, x, jnp.zeros_like(x)) for hh in range(HEADS_PER_GROUP)],
                           axis=0)


def _attn_specs(qb):
    rows = qb * ATTN_BLOCK
    cur = pl.BlockSpec((None, rows, GROUP_WIDTH), lambda r, n: (r, n, 0))
    prev = pl.BlockSpec((None, ATTN_BLOCK, GROUP_WIDTH), lambda r, n: (r, jnp.maximum(n * qb - 1, 0), 0))
    bias = pl.BlockSpec((2, HEADS_PER_GROUP, ATTN_BLOCK, 2 * ATTN_BLOCK), lambda r, n: (0, 0, 0, 0))
    return cur, prev, bias


def _attn_fwd(q, k, v, bias, name):
    d, M, _ = q.shape
    nb = M // ATTN_BLOCK
    qb = min(ATTN_QB, nb)

    def body(q_ref, kp_ref, kc_ref, vp_ref, vc_ref, bias_ref, o_ref, lse_ref):
        n = pl.program_id(1)
        q_head = _head_of_col(ATTN_BLOCK)
        kwin = jnp.concatenate([kp_ref[...], kc_ref[...]], axis=0)
        vwin = jnp.concatenate([vp_ref[...], vc_ref[...]], axis=0)
        ones = jnp.ones((2 * ATTN_BLOCK, LANES), BF16)
        for b in range(qb):
            rows = slice(b * ATTN_BLOCK, (b + 1) * ATTN_BLOCK)
            window = slice(b * ATTN_BLOCK, (b + 2) * ATTN_BLOCK)
            variant = jnp.minimum(n, 1) if b == 0 else 1
            kk = kwin[window]
            vv = vwin[window]
            q4 = _stack_heads(q_ref[rows, :], q_head)
            logits = _dot_nt(q4, kk) + bias_ref[variant].reshape(_STACK_ROWS, 2 * ATTN_BLOCK)
            m = jnp.max(logits, axis=1, keepdims=True)
            p16 = jnp.exp(logits - m).astype(BF16)
            den = _dot(p16, ones)[:, 0:1]
            out = _dot(p16, vv) * (1.0 / den)
            lse = m + jnp.log(den)
            o_acc = jnp.zeros((ATTN_BLOCK, GROUP_WIDTH), F32)
            lse_acc = jnp.zeros((ATTN_BLOCK, GROUP_WIDTH), F32)
            for hh in range(HEADS_PER_GROUP):
                head_rows = slice(hh * ATTN_BLOCK, (hh + 1) * ATTN_BLOCK)
                o_acc = jnp.where(q_head == hh, out[head_rows], o_acc)
                lse_acc = jnp.where(q_head == hh, lse[head_rows], lse_acc)
            o_ref[rows, :] = o_acc
            lse_ref[rows, :] = lse_acc

    cur, prev, full = _attn_specs(qb)
    return pl.pallas_call(
        body, name=name, grid=(d, nb // qb),
        in_specs=[cur, prev, cur, prev, cur, full],
        out_specs=[cur, cur],
        out_shape=[jax.ShapeDtypeStruct((d, M, GROUP_WIDTH), F32)] * 2,
        compiler_params=_params(),
    )(q, k, k, v, v, bias)


def _attn_bwd(q, k, v, do, lse, delta, bias, name):
    d, M, _ = q.shape
    nb = M // ATTN_BLOCK
    qb = min(ATTN_QB, nb)
    ns = nb // qb
    rows_q = qb * ATTN_BLOCK
    last = slice(rows_q - ATTN_BLOCK, rows_q)

    def body(q_ref, kp_ref, kc_ref, vp_ref, vc_ref, do_ref, lse_ref, dl_ref, bias_ref,
             dq_ref, dk_ref, dv_ref, dsum_ref, pk_ref, pv_ref, wk_ref, wv_ref):
        r = pl.program_id(0)
        n = pl.program_id(1)

        @pl.when((r == 0) & (n == 0))
        def _():
            dsum_ref[...] = jnp.zeros_like(dsum_ref)

        @pl.when(n == 0)
        def _():
            pk_ref[...] = jnp.zeros_like(pk_ref)
            pv_ref[...] = jnp.zeros_like(pv_ref)

        @pl.when(n < ns)
        def _():
            q_head = _head_of_col(ATTN_BLOCK)
            kwin = jnp.concatenate([kp_ref[...], kc_ref[...]], axis=0)
            vwin = jnp.concatenate([vp_ref[...], vc_ref[...]], axis=0)
            wk_ref[...] = jnp.zeros_like(wk_ref)
            wv_ref[...] = jnp.zeros_like(wv_ref)
            for b in range(qb):
                rows = slice(b * ATTN_BLOCK, (b + 1) * ATTN_BLOCK)
                window = slice(b * ATTN_BLOCK, (b + 2) * ATTN_BLOCK)
                variant = jnp.minimum(n, 1) if b == 0 else 1
                kk = kwin[window]
                vv = vwin[window]
                q4 = _stack_heads(q_ref[rows, :], q_head)
                do4 = _stack_heads(do_ref[rows, :], q_head)
                heads = [hh * HEAD_DIM for hh in range(HEADS_PER_GROUP)]
                lse4 = jnp.concatenate([lse_ref[rows, c0:c0 + 1] for c0 in heads], axis=0)
                dl4 = jnp.concatenate([dl_ref[rows, c0:c0 + 1] for c0 in heads], axis=0)
                logits = _dot_nt(q4, kk) + bias_ref[variant].reshape(_STACK_ROWS, 2 * ATTN_BLOCK)
                p = jnp.exp(logits - lse4)
                ds = p * (_dot_nt(do4, vv) - dl4)
                dsum_ref[...] += ds.reshape(HEADS_PER_GROUP, ATTN_BLOCK, 2 * ATTN_BLOCK)
                ds16 = ds.astype(BF16)
                dq4 = _dot(ds16, kk)
                dq_acc = jnp.zeros((ATTN_BLOCK, GROUP_WIDTH), F32)
                for hh in range(HEADS_PER_GROUP):
                    dq_acc = jnp.where(q_head == hh, dq4[hh * ATTN_BLOCK:(hh + 1) * ATTN_BLOCK], dq_acc)
                dq_ref[rows, :] = (dq_acc * Q_SCALE).astype(BF16)
                wk_ref[window, :] += _dot_tn(ds16, q4)
                wv_ref[window, :] += _dot_tn(p.astype(BF16), do4)
            for out_ref, part_ref, win_ref in ((dk_ref, pk_ref, wk_ref), (dv_ref, pv_ref, wv_ref)):
                if ns == 1:
                    out_ref[...] = win_ref[ATTN_BLOCK:, :].astype(BF16)
                    continue
                if qb > 1:
                    out_ref[0:rows_q - ATTN_BLOCK, :] = part_ref[0:rows_q - ATTN_BLOCK, :].astype(BF16)
                out_ref[last, :] = (part_ref[last, :] + win_ref[0:ATTN_BLOCK, :]).astype(BF16)
                part_ref[...] = win_ref[ATTN_BLOCK:, :]

        if ns > 1:
            @pl.when(n == ns)
            def _():
                dk_ref[...] = pk_ref[...].astype(BF16)
                dv_ref[...] = pv_ref[...].astype(BF16)

    def clamp(n):
        return jnp.minimum(n, ns - 1)

    cur = pl.BlockSpec((None, rows_q, GROUP_WIDTH), lambda r, n: (r, clamp(n), 0))
    prev = pl.BlockSpec((None, ATTN_BLOCK, GROUP_WIDTH), lambda r, n: (r, jnp.maximum(clamp(n) * qb - 1, 0), 0))
    lag = pl.BlockSpec((None, rows_q, GROUP_WIDTH), lambda r, n: (r, jnp.maximum(n - 1, 0), 0))
    full = pl.BlockSpec((2, HEADS_PER_GROUP, ATTN_BLOCK, 2 * ATTN_BLOCK), lambda r, n: (0, 0, 0, 0))
    acc = pl.BlockSpec((HEADS_PER_GROUP, ATTN_BLOCK, 2 * ATTN_BLOCK), lambda r, n: (0, 0, 0))
    return pl.pallas_call(
        body, name=name, grid=(d, ns + 1 if ns > 1 else 1),
        in_specs=[cur, prev, cur, prev, cur, cur, cur, cur, full],
        out_specs=[cur, lag, lag, acc],
        out_shape=[jax.ShapeDtypeStruct((d, M, GROUP_WIDTH), BF16)] * 3
        + [jax.ShapeDtypeStruct((HEADS_PER_GROUP, ATTN_BLOCK, 2 * ATTN_BLOCK), F32)],
        scratch_shapes=[pltpu.VMEM((rows_q, GROUP_WIDTH), F32), pltpu.VMEM((rows_q, GROUP_WIDTH), F32),
                        pltpu.VMEM((rows_q + ATTN_BLOCK, GROUP_WIDTH), F32),
                        pltpu.VMEM((rows_q + ATTN_BLOCK, GROUP_WIDTH), F32)],
        compiler_params=_params(),
    )(q, k, k, v, v, do, lse, delta, bias)


def _disc_math(a_re, a_im, ldt, b_re, b_im):
    dt = jnp.exp(ldt)
    mag = jnp.exp(a_re * dt)
    ab_re = mag * jnp.cos(a_im * dt)
    ab_im = mag * jnp.sin(a_im * dt)
    den = a_re * a_re + a_im * a_im
    xr = ab_re - 1.0
    coef_re = (xr * a_re + ab_im * a_im) / den
    coef_im = (ab_im * a_re - xr * a_im) / den
    return ab_re, ab_im, coef_re * b_re - coef_im * b_im, coef_re * b_im + coef_im * b_re


def _block_diag_mask():
    row_g = lax.broadcasted_iota(jnp.int32, (SSM_WIDTH, 2 * NS), 0) // SSM_GROUP
    col = lax.broadcasted_iota(jnp.int32, (SSM_WIDTH, 2 * NS), 1)
    col_g = jnp.where(col >= NS, col - NS, col) // SSM_STATE
    return row_g == col_g


def _disc_fwd(a_re, a_im, ldt, b_re, b_im, c_re, c_im):
    def body(are_ref, aim_ref, ldt_ref, bre_ref, bim_ref, cre_ref, cim_ref, pw_ref, pwr_ref, bd_ref, cdt_ref):
        ab_re, ab_im, bb_re, bb_im = _disc_math(are_ref[...], aim_ref[...], ldt_ref[...], bre_ref[...], bim_ref[...])
        row = lax.broadcasted_iota(jnp.int32, (8, NS), 0)
        pr, pi = ab_re, ab_im
        t_re = jnp.zeros((8, NS), F32)
        t_im = jnp.zeros((8, NS), F32)
        u_re = jnp.zeros((8, NS), F32)
        u_im = jnp.zeros((8, NS), F32)
        for j in range(8):
            t_re = jnp.where(row == j, pr, t_re)
            t_im = jnp.where(row == j, pi, t_im)
            u_re = jnp.where(row == 7 - j, pr, u_re)
            u_im = jnp.where(row == 7 - j, pi, u_im)
            pr, pi = pr * ab_re - pi * ab_im, pr * ab_im + pi * ab_re
        pw_ref[0] = t_re
        pw_ref[1] = t_im
        pwr_ref[0] = u_re
        pwr_ref[1] = u_im
        mask = _block_diag_mask()
        zero = jnp.zeros((SSM_WIDTH, 2 * NS), F32)
        bfull = jnp.concatenate([jnp.concatenate([bb_re] * SSM_GROUPS, axis=0),
                                 jnp.concatenate([bb_im] * SSM_GROUPS, axis=0)], axis=1)
        bd_ref[...] = jnp.where(mask, bfull, zero).astype(BF16)
        cfull = jnp.concatenate([jnp.concatenate([cre_ref[...]] * SSM_GROUPS, axis=0),
                                 jnp.concatenate([-cim_ref[...]] * SSM_GROUPS, axis=0)], axis=1)
        cdt_ref[...] = jnp.where(mask, cfull, zero).astype(BF16)

    return pl.pallas_call(
        body, name="s5_disc_fwd",
        in_specs=[_whole()] * 7, out_specs=[_whole()] * 4,
        out_shape=[jax.ShapeDtypeStruct((2, 8, NS), F32), jax.ShapeDtypeStruct((2, 8, NS), F32),
                   jax.ShapeDtypeStruct((SSM_WIDTH, 2 * NS), BF16), jax.ShapeDtypeStruct((SSM_WIDTH, 2 * NS), BF16)],
        compiler_params=_params(),
    )(a_re, a_im, ldt, b_re, b_im, c_re, c_im)


def _disc_bwd(a_re, a_im, ldt, b_re, b_im, d_bd, d_cdt, d_ab, group_sum):
    def body(are_ref, aim_ref, ldt_ref, bre_ref, bim_ref, dbd_ref, dcdt_ref, dab_ref, gs_ref,
             dare_ref, daim_ref, dldt_ref, dbre_ref, dbim_ref, dcre_ref, dcim_ref):
        col = lax.broadcasted_iota(jnp.int32, (SSM_GROUP, 2 * NS), 1)
        col_g = jnp.where(col >= NS, col - NS, col) // SSM_STATE
        acc_b = jnp.zeros((SSM_GROUP, 2 * NS), F32)
        acc_c = jnp.zeros((SSM_GROUP, 2 * NS), F32)
        for g in range(SSM_GROUPS):
            rows = slice(g * SSM_GROUP, (g + 1) * SSM_GROUP)
            acc_b = acc_b + jnp.where(col_g == g, dbd_ref[rows, :], 0.0)
            acc_c = acc_c + jnp.where(col_g == g, dcdt_ref[rows, :], 0.0)
        dcre_ref[...] = acc_c[:, :NS]
        dcim_ref[...] = -acc_c[:, NS:]
        dab_re = jnp.sum(dab_ref[0], axis=0, keepdims=True)
        dab_im = jnp.sum(dab_ref[1], axis=0, keepdims=True)
        _, vjp = jax.vjp(_disc_math, are_ref[...], aim_ref[...], ldt_ref[...], bre_ref[...], bim_ref[...])
        d_are, d_aim, d_ldt, d_bre, d_bim = vjp((dab_re, dab_im, acc_b[:, :NS], acc_b[:, NS:]))
        dare_ref[...] = d_are
        daim_ref[...] = d_aim
        dbre_ref[...] = d_bre
        dbim_ref[...] = d_bim
        dldt_ref[...] = _dot_exact(jnp.broadcast_to(d_ldt, (8, NS)), gs_ref[...])

    vec = jax.ShapeDtypeStruct((1, NS), F32)
    mat = jax.ShapeDtypeStruct((SSM_GROUP, NS), F32)
    return pl.pallas_call(
        body, name="s5_disc_bwd",
        in_specs=[_whole()] * 9, out_specs=[_whole()] * 7,
        out_shape=[vec, vec, jax.ShapeDtypeStruct((8, 128), F32), mat, mat, mat, mat],
        compiler_params=_params(),
    )(a_re, a_im, ldt, b_re, b_im, d_bd, d_cdt, d_ab, group_sum)


def _scan_blocks(buf, pw_ref, carry_ref, n_blocks, reverse):
    row = lax.broadcasted_iota(jnp.int32, (8, SCAN_LANES), 0)
    for lc in range(NS // SCAN_LANES):
        re_cols = pl.ds(lc * SCAN_LANES, SCAN_LANES)
        im_cols = pl.ds(NS + lc * SCAN_LANES, SCAN_LANES)
        pr = pw_ref[0, :, re_cols]
        pi = pw_ref[1, :, re_cols]
        if reverse:
            pi = -pi
            base = [(7, 1), (6, 2), (4, 4)]
            coef = [(jnp.where(row < 8 - k, pr[j:j + 1], 0.0), jnp.where(row < 8 - k, pi[j:j + 1], 0.0), 8 - k)
                    for j, k in base]
        else:
            base = [(0, 1), (1, 2), (3, 4)]
            coef = [(jnp.where(row >= k, pr[j:j + 1], 0.0), jnp.where(row >= k, pi[j:j + 1], 0.0), k)
                    for j, k in base]

        def step(i, carry, pr=pr, pi=pi, coef=coef, re_cols=re_cols, im_cols=im_cols):
            cr, ci = carry
            blk = (n_blocks - 1 - i) if reverse else i
            rows = pl.ds(pl.multiple_of(blk * 8, 8), 8)
            xr = buf[rows, re_cols]
            xi = buf[rows, im_cols]
            for kr, ki, shift in coef:
                sr = pltpu.roll(xr, shift, 0)
                si = pltpu.roll(xi, shift, 0)
                xr, xi = xr + kr * sr - ki * si, xi + kr * si + ki * sr
            xr, xi = xr + pr * cr - pi * ci, xi + pr * ci + pi * cr
            buf[rows, re_cols] = xr
            buf[rows, im_cols] = xi
            edge = slice(0, 1) if reverse else slice(7, 8)
            return xr[edge], xi[edge]

        cr, ci = lax.fori_loop(0, n_blocks, step, (carry_ref[0:1, re_cols], carry_ref[0:1, im_cols]))
        carry_ref[0:1, re_cols] = cr
        carry_ref[0:1, im_cols] = ci


_SUPER_GROUPS = 16
_SUPER_BLOCKS = [
    (slice(k * _SUPER_GROUPS * SSM_GROUP, (k + 1) * _SUPER_GROUPS * SSM_GROUP),
     [slice(half + k * _SUPER_GROUPS * SSM_STATE, half + (k + 1) * _SUPER_GROUPS * SSM_STATE) for half in (0, NS)])
    for k in range(SSM_GROUPS // _SUPER_GROUPS)]


def _ssm_fwd(u, bd, cdt, d_skip, pw):
    L = u.shape[0]
    tc = min(SSM_FWD_CHUNK, L)

    def body(u_ref, bd_ref, cdt_ref, dsk_ref, pw_ref, y_ref, s_ref, carry_ref):
        @pl.when(pl.program_id(0) == 0)
        def _():
            carry_ref[...] = jnp.zeros_like(carry_ref)

        uv = u_ref[...]
        u16 = uv.astype(BF16)
        for ch, states in _SUPER_BLOCKS:
            for st in states:
                s_ref[:, st] = _dot(u16[:, ch], bd_ref[ch, st])
        _scan_blocks(s_ref, pw_ref, carry_ref, tc // 8, reverse=False)
        for ch, states in _SUPER_BLOCKS:
            y_ref[:, ch] = (sum(_dot_nt(s_ref[:, st].astype(BF16), cdt_ref[ch, st]) for st in states)
                            + dsk_ref[:, ch] * uv[:, ch])

    return pl.pallas_call(
        body, name="s5_fwd", grid=(L // tc,),
        in_specs=[_rows(tc, SSM_WIDTH), _whole(), _whole(), _whole(), _whole()],
        out_specs=[_rows(tc, SSM_WIDTH), _rows(tc, 2 * NS)],
        out_shape=[jax.ShapeDtypeStruct((L, SSM_WIDTH), F32), jax.ShapeDtypeStruct((L, 2 * NS), F32)],
        scratch_shapes=[pltpu.VMEM((8, 2 * NS), F32)],
        compiler_params=_params(),
    )(u, bd, cdt, d_skip, pw)


def _ssm_bwd(dy, u, s, bd, cdt, d_skip, pwr):
    L = u.shape[0]
    tc = min(SSM_CHUNK, L)
    nc = L // tc
    blocks = tc // 8

    def body(dy_ref, u_ref, s_ref, sprev_ref, bd_ref, cdt_ref, dsk_ref, pwr_ref,
             du_ref, ddsk_ref, dbd_ref, dcdt_ref, dab_ref, g_ref, sx_ref, carry_ref):
        i = pl.program_id(0)

        @pl.when(i == 0)
        def _():
            carry_ref[...] = jnp.zeros_like(carry_ref)
            ddsk_ref[...] = jnp.zeros_like(ddsk_ref)
            dbd_ref[...] = jnp.zeros_like(dbd_ref)
            dcdt_ref[...] = jnp.zeros_like(dcdt_ref)
            dab_ref[...] = jnp.zeros_like(dab_ref)

        dyv = dy_ref[...]
        uv = u_ref[...]
        dy16 = dyv.astype(BF16)
        u16 = uv.astype(BF16)
        for ch, states in _SUPER_BLOCKS:
            for st in states:
                g_ref[:, st] = _dot(dy16[:, ch], cdt_ref[ch, st])
        _scan_blocks(g_ref, pwr_ref, carry_ref, blocks, reverse=True)
        ddsk_ref[...] += jnp.sum(dyv * uv, axis=0, keepdims=True)
        for ch, states in _SUPER_BLOCKS:
            du = dsk_ref[:, ch] * dyv[:, ch]
            for st in states:
                g16 = g_ref[:, st].astype(BF16)
                du = du + _dot_nt(g16, bd_ref[ch, st])
                dbd_ref[ch, st] += _dot_tn(u16[:, ch], g16)
                dcdt_ref[ch, st] += _dot_tn(dy16[:, ch], s_ref[:, st].astype(BF16))
            du_ref[:, ch] = du

        sx_ref[pl.ds(8, tc), :] = s_ref[...]
        sx_ref[pl.ds(0, 8), :] = jnp.where(i == nc - 1, 0.0, sprev_ref[...])
        row = lax.broadcasted_iota(jnp.int32, (8, SCAN_LANES), 0)
        for lc in range(NS // SCAN_LANES):
            re_cols = pl.ds(lc * SCAN_LANES, SCAN_LANES)
            im_cols = pl.ds(NS + lc * SCAN_LANES, SCAN_LANES)

            def step(b, acc, re_cols=re_cols, im_cols=im_cols):
                ar, ai = acc
                off = pl.multiple_of(b * 8, 8)
                gr = g_ref[pl.ds(off, 8), re_cols]
                gi = g_ref[pl.ds(off, 8), im_cols]
                before = pl.ds(off, 8)
                here = pl.ds(off + 8, 8)
                sr = jnp.where(row == 0, sx_ref[before, re_cols][7:8], pltpu.roll(sx_ref[here, re_cols], 1, 0))
                si = jnp.where(row == 0, sx_ref[before, im_cols][7:8], pltpu.roll(sx_ref[here, im_cols], 1, 0))
                return ar + gr * sr + gi * si, ai + gi * sr - gr * si

            zero = jnp.zeros((8, SCAN_LANES), F32)
            ar, ai = lax.fori_loop(0, blocks, step, (zero, zero))
            dab_ref[0, :, re_cols] += ar
            dab_ref[1, :, re_cols] += ai

    rev = lambda i: (nc - 1 - i, 0)
    sprev = pl.BlockSpec((8, 2 * NS), lambda i: (jnp.maximum((nc - 1 - i) * blocks - 1, 0), 0))
    return pl.pallas_call(
        body, name="s5_bwd", grid=(nc,),
        in_specs=[pl.BlockSpec((tc, SSM_WIDTH), rev), pl.BlockSpec((tc, SSM_WIDTH), rev),
                  pl.BlockSpec((tc, 2 * NS), rev), sprev, _whole(), _whole(), _whole(), _whole()],
        out_specs=[pl.BlockSpec((tc, SSM_WIDTH), rev), _whole(), _whole(), _whole(), _whole()],
        out_shape=[jax.ShapeDtypeStruct((L, SSM_WIDTH), F32), jax.ShapeDtypeStruct((1, SSM_WIDTH), F32),
                   jax.ShapeDtypeStruct((SSM_WIDTH, 2 * NS), F32), jax.ShapeDtypeStruct((SSM_WIDTH, 2 * NS), F32),
                   jax.ShapeDtypeStruct((2, 8, NS), F32)],
        scratch_shapes=[pltpu.VMEM((tc, 2 * NS), F32), pltpu.VMEM((tc + 8, 2 * NS), F32), pltpu.VMEM((8, 2 * NS), F32)],
        compiler_params=_params(),
    )(dy, u, s, s, bd, cdt, d_skip, pwr)


def _branches(o_attn, y, gates, w_ab, w_glu, w_sb):
    ya = _dot(o_attn.astype(BF16), w_ab[...])
    gel = _gelu(y)
    glu = _dot(gel.astype(BF16), w_glu[...])
    p = glu[:, :SSM_WIDTH]
    sg = _sigmoid(glu[:, SSM_WIDTH:])
    ys2 = p * sg
    ysb = _dot(ys2.astype(BF16), w_sb[...])
    ga = gates[:, :D_MODEL]
    gs = gates[:, D_MODEL:]
    return ya, gel, p, sg, ys2, ysb, ga, gs


def _mix_out_fwd(x1, o_g, lse_g, y, gates, w_ab, w_glu, w_sb, w_out):
    L = x1.shape[0]
    tm = min(ROW_TILE, L)

    def body(x_ref, o0, o1, o2, l0, l1, l2, y_ref, gate_ref, wab_ref, wglu_ref, wsb_ref, wout_ref,
             x2_ref, oat_ref, lse0, lse1, lse2, scr):
        la, lb, lc = (_from_residues(ref, scr, d) for ref, d in zip((l0, l1, l2), DILATIONS))
        m = jnp.maximum(jnp.maximum(la, lb), lc)
        ea, eb, ec = jnp.exp(la - m), jnp.exp(lb - m), jnp.exp(lc - m)
        tot = ea + eb + ec
        oa, ob, oc = (_from_residues(ref, scr, d) for ref, d in zip((o0, o1, o2), DILATIONS))
        o_attn = (ea * oa + eb * ob + ec * oc) / tot
        oat_ref[...] = o_attn
        lse = m + jnp.log(tot)
        for ref, d in zip((lse0, lse1, lse2), DILATIONS):
            _to_residues(lse, ref, scr, d)
        ya, _, _, _, _, ysb, ga, gs = _branches(o_attn, y_ref[...], gate_ref[...], wab_ref, wglu_ref, wsb_ref)
        mix = ga * ya + gs * ysb
        x2_ref[...] = x_ref[...] + _dot(mix.astype(BF16), wout_ref[...])

    res = [_residue_spec(d, tm) for d in DILATIONS]
    return pl.pallas_call(
        body, name="mix_out_fwd", grid=(L // tm,),
        in_specs=[_rows(tm, D_MODEL)] + res * 2 + [_rows(tm, SSM_WIDTH), _rows(tm, 2 * D_MODEL)] + [_whole()] * 4,
        out_specs=[_rows(tm, D_MODEL), _rows(tm, GROUP_WIDTH)] + res,
        out_shape=[jax.ShapeDtypeStruct((L, D_MODEL), F32), jax.ShapeDtypeStruct((L, GROUP_WIDTH), F32)]
        + [_residue_shape(d, L, F32) for d in DILATIONS],
        scratch_shapes=[_residue_scratch(tm)],
        compiler_params=_params(),
    )(x1, *o_g, *lse_g, y, gates, w_ab, w_glu, w_sb, w_out)


def _mix_out_bwd(dx2, o_attn, y, gates, w_ab, w_glu, w_sb, w_out, head_sum):
    L = dx2.shape[0]
    tm = min(ROW_TILE, L)

    def body(dx_ref, oat_ref, y_ref, gate_ref, wab_ref, wglu_ref, wsb_ref, wout_ref, hs_ref,
             do0, do1, do2, dl0, dl1, dl2, dy_ref, dgp_ref, mix_ref, dya_ref, dys_ref, ys2_ref, gel_ref, dglu_ref,
             dgb_ref, scr):
        i = pl.program_id(0)
        o_attn = oat_ref[...]
        yv = y_ref[...]
        ya, gel, p, sg, ys2, ysb, ga, gs = _branches(o_attn, yv, gate_ref[...], wab_ref, wglu_ref, wsb_ref)
        mix_ref[...] = (ga * ya + gs * ysb).astype(BF16)
        ys2_ref[...] = ys2.astype(BF16)
        gel_ref[...] = gel.astype(BF16)
        dmix = _dot_nt(dx_ref[...].astype(BF16), wout_ref[...])
        dgp = jnp.concatenate([dmix * ya * ga * (1.0 - ga), dmix * ysb * gs * (1.0 - gs)], axis=1)
        dgp_ref[...] = dgp.astype(BF16)

        @pl.when(i == 0)
        def _():
            dgb_ref[...] = jnp.zeros_like(dgb_ref)

        dgb_ref[...] += jnp.sum(dgp, axis=0, keepdims=True)
        dya = (dmix * ga).astype(BF16)
        dys = (dmix * gs).astype(BF16)
        dya_ref[...] = dya
        dys_ref[...] = dys
        d_o = _dot_nt(dya, wab_ref[...])
        delta = _dot_exact(d_o * o_attn, hs_ref[...])
        for do_ref, dl_ref, d in zip((do0, do1, do2), (dl0, dl1, dl2), DILATIONS):
            _to_residues(d_o, do_ref, scr, d)
            _to_residues(delta, dl_ref, scr, d)
        dys2 = _dot_nt(dys, wsb_ref[...])
        dglu = jnp.concatenate([dys2 * sg, dys2 * p * sg * (1.0 - sg)], axis=1).astype(BF16)
        dglu_ref[...] = dglu
        dy_ref[...] = _dot_nt(dglu, wglu_ref[...]) * _gelu_grad(yv)

    grp = _rows(tm, GROUP_WIDTH)
    wide = _rows(tm, D_MODEL)
    half = _rows(tm, SSM_WIDTH)
    res = [_residue_spec(d, tm) for d in DILATIONS]
    sds = jax.ShapeDtypeStruct
    return pl.pallas_call(
        body, name="mix_out_bwd", grid=(L // tm,),
        in_specs=[wide, grp, half, _rows(tm, 2 * D_MODEL)] + [_whole()] * 5,
        out_specs=res + res + [half, _rows(tm, 2 * D_MODEL), wide, wide, wide, half, half, wide, _acc_row(2 * D_MODEL)],
        out_shape=[_residue_shape(d, L, BF16) for d in DILATIONS] + [_residue_shape(d, L, F32) for d in DILATIONS]
        + [sds((L, SSM_WIDTH), F32),
           sds((L, 2 * D_MODEL), BF16), sds((L, D_MODEL), BF16), sds((L, D_MODEL), BF16),
           sds((L, D_MODEL), BF16), sds((L, SSM_WIDTH), BF16), sds((L, SSM_WIDTH), BF16),
           sds((L, D_MODEL), BF16), sds((1, 2 * D_MODEL), F32)],
        scratch_shapes=[_residue_scratch(tm)],
        compiler_params=_params(),
    )(dx2, o_attn, y, gates, w_ab, w_glu, w_sb, w_out, head_sum)


def _adamw(w, g, m, v, name):
    R, C = w.shape
    tr = _row_tile(R, max(8, ADAMW_BLOCK_BYTES // (4 * C)))

    def body(w_ref, g_ref, m_ref, v_ref, d_ref, mo_ref, vo_ref):
        gv = g_ref[...]
        mn = ADAM_B1 * m_ref[...] + (1.0 - ADAM_B1) * gv
        vn = ADAM_B2 * v_ref[...] + (1.0 - ADAM_B2) * (gv * gv)
        m_hat = mn / (1.0 - ADAM_B1 ** ADAM_STEP)
        v_hat = vn / (1.0 - ADAM_B2 ** ADAM_STEP)
        d_ref[...] = -ADAM_LR * (m_hat / (jnp.sqrt(v_hat) + ADAM_EPS) + ADAM_WD * w_ref[...])
        mo_ref[...] = mn
        vo_ref[...] = vn

    blk = pl.BlockSpec((tr, C), lambda i: (i, 0))
    return pl.pallas_call(
        body, name=name, grid=(R // tr,),
        in_specs=[blk] * 4, out_specs=[blk] * 3,
        out_shape=[jax.ShapeDtypeStruct((R, C), F32)] * 3,
        compiler_params=_params(),
    )(w, g, m, v)


def _sum_chips_into_half(u, t, name):
    S, H, C = u.shape
    tr = _row_tile(H, 512)
    hb = H // tr

    def body(s_ref, t_ref, a_ref, b_ref, c_ref, o_ref):
        me = s_ref[1]
        others = (a_ref[...], b_ref[...], c_ref[...])
        acc = None
        for chip in range(S):
            below = others[min(chip, S - 2)]
            above = others[max(chip - 1, 0)]
            term = jnp.where(me == chip, t_ref[...], jnp.where(me > chip, below, above)).astype(F32)
            acc = term if acc is None else acc + term
        o_ref[...] = acc

    x, y, c = lax.axis_index("x"), lax.axis_index("y"), lax.axis_index("c")
    me = 2 * x + y
    scalars = jnp.stack([c, me] + [j + (j >= me).astype(jnp.int32) for j in range(S - 1)]).astype(jnp.int32)
    blk = (None, tr, C)
    return pl.pallas_call(
        body, name=name,
        grid_spec=pltpu.PrefetchScalarGridSpec(
            num_scalar_prefetch=1, grid=(hb,),
            in_specs=[pl.BlockSpec(blk, lambda i, s: (s[1], i, 0))]
            + [pl.BlockSpec(blk, functools.partial(lambda j, i, s: (s[2 + j], i, 0), j)) for j in range(S - 1)],
            out_specs=pl.BlockSpec((tr, C), lambda i, s: (s[0] * hb + i, 0))),
        out_shape=jax.ShapeDtypeStruct((2 * H, C), F32),
        compiler_params=_params(),
    )(scalars, t, u, u, u)


def _add_halves(g, r1, name):
    S, R, C = g.shape
    H = R // 2
    tr = _row_tile(H, 512)
    hb = H // tr

    def body(c_ref, g_ref, r_ref, o_ref):
        o_ref[...] = (g_ref[...] + r_ref[...]).astype(BF16)

    core = lax.axis_index("c").astype(jnp.int32).reshape(1)
    return pl.pallas_call(
        body, name=name,
        grid_spec=pltpu.PrefetchScalarGridSpec(
            num_scalar_prefetch=1, grid=(S, hb),
            in_specs=[pl.BlockSpec((None, tr, C), lambda j, i, c_ref: (j, c_ref[0] * hb + i, 0)),
                      pl.BlockSpec((None, tr, C), lambda j, i, c_ref: (j, i, 0))],
            out_specs=pl.BlockSpec((None, tr, C), lambda j, i, c_ref: (j, i, 0))),
        out_shape=jax.ShapeDtypeStruct((S, H, C), BF16),
        compiler_params=_params(),
    )(core, g, r1)


_ANY = pl.BlockSpec(memory_space=pl.ANY)


def _place():
    x, y, c = lax.axis_index("x"), lax.axis_index("y"), lax.axis_index("c")
    chips = [(1 - x, y), (x, 1 - y), (1 - x, 1 - y)]
    return x, y, c, chips


def _comm_call(body, name, ins, out_shapes, n_remote, n_local):
    return pl.pallas_call(
        body, name=name,
        in_specs=[_ANY] * len(ins), out_specs=[_ANY] * len(out_shapes), out_shape=out_shapes,
        scratch_shapes=[pltpu.SemaphoreType.DMA((n_remote,)), pltpu.SemaphoreType.DMA((n_remote,)),
                        pltpu.SemaphoreType.DMA((max(n_local, 1),))],
    )(*ins)


def _remote(src, dst, send_sems, recv_sems, k, device):
    return pltpu.make_async_remote_copy(src_ref=src, dst_ref=dst, send_sem=send_sems.at[k], recv_sem=recv_sems.at[k],
                                        device_id=device, device_id_type=MESH)


def _gather_parts(shapes, w_refs, out_refs, send_sems, recv_sems):
    n = len(shapes)
    x, y, c, chips = _place()
    me = 2 * x + y
    sibling = (x, y, 1 - c)

    def half(k, chip_idx, core):
        H = shapes[k][0] // 2
        return out_refs[k].at[chip_idx, pl.ds(core * H, H), :]

    mine = [_remote(w_refs[k], out_refs[k].at[me], send_sems, recv_sems, 6 * n + k, sibling) for k in range(n)]
    first = []
    for k in range(n):
        H = shapes[k][0] // 2
        for j, (cx, cy) in enumerate(chips):
            first.append(_remote(w_refs[k].at[pl.ds(c * H, H), :], half(k, me, c), send_sems, recv_sems,
                                 3 * k + j, (cx, cy, c)))

    def start():
        for cp in mine + first:
            cp.start()

    def finish():
        passed = []
        for k in range(n):
            for j, (cx, cy) in enumerate(chips):
                landed = half(k, 2 * cx + cy, c)
                _remote(landed, landed, send_sems, recv_sems, 3 * k + j, (cx, cy, c)).wait_recv()
                fwd = _remote(landed, landed, send_sems, recv_sems, 3 * n + 3 * k + j, sibling)
                fwd.start()
                passed.append(fwd)
        for k in range(n):
            for j, (cx, cy) in enumerate(chips):
                other = half(k, 2 * cx + cy, 1 - c)
                _remote(other, other, send_sems, recv_sems, 3 * n + 3 * k + j, sibling).wait_recv()
        for cp in mine:
            cp.wait_recv()
        for cp in first + passed + mine:
            cp.wait_send()

    return start, finish


def _gather_weights(shards, name):
    n = len(shards)

    def body(*refs):
        x, y, c, chips = _place()
        _handshake([(x, y, 1 - c)] + [(cx, cy, c) for cx, cy in chips])
        start, finish = _gather_parts([w.shape for w in shards], refs[:n], refs[n:2 * n], *refs[2 * n:2 * n + 2])
        start()
        finish()

    return _sequenced(body, name, shards, [jax.ShapeDtypeStruct((N_SHARD,) + w.shape, w.dtype) for w in shards],
                      7 * n, COLLECTIVE_IDS["gather"])


def _handshake(peers):
    barrier = pltpu.get_barrier_semaphore()
    for peer in peers:
        pl.semaphore_signal(barrier, inc=1, device_id=peer, device_id_type=MESH)
    pl.semaphore_wait(barrier, len(peers))


def _sequenced(body, name, ins, out_shapes, n_sems, collective_id):
    return pl.kernel(
        body, out_type=list(out_shapes), mesh=plsc.ScalarSubcoreMesh(axis_name="sequencer", num_cores=1), name=name,
        scratch_types=(pltpu.SemaphoreType.DMA((n_sems,)), pltpu.SemaphoreType.DMA((n_sems,))),
        compiler_params=pltpu.CompilerParams(collective_id=collective_id))(*ins)


def _swap_halves(gs, name, collective_id):
    n = len(gs)

    def body(*refs):
        g_refs, out_refs = refs[:n], refs[n:2 * n]
        send_sems, recv_sems = refs[2 * n:]
        x, y, c, _ = _place()
        _handshake([(x, y, 1 - c)])
        cps = []
        for k in range(n):
            H = gs[k].shape[1] // 2
            cp = _remote(g_refs[k].at[:, pl.ds((1 - c) * H, H), :], out_refs[k], send_sems, recv_sems, k, (x, y, 1 - c))
            cp.start()
            cps.append(cp)
        for cp in cps:
            cp.wait()

    return _sequenced(body, name, gs, [jax.ShapeDtypeStruct((g.shape[0], g.shape[1] // 2, g.shape[2]), g.dtype)
                                       for g in gs], n, collective_id)


def _exchange_chips(ts, name, collective_id):
    n = len(ts)

    def body(*refs):
        t_refs, out_refs = refs[:n], refs[n:2 * n]
        send_sems, recv_sems = refs[2 * n:]
        x, y, c, chips = _place()
        me = 2 * x + y
        _handshake([(cx, cy, c) for cx, cy in chips])
        sent = []
        for k in range(n):
            for j, (cx, cy) in enumerate(chips):
                cp = _remote(t_refs[k].at[2 * cx + cy], out_refs[k].at[me], send_sems, recv_sems, 3 * k + j, (cx, cy, c))
                cp.start()
                sent.append(cp)
        for k in range(n):
            for j, (cx, cy) in enumerate(chips):
                slot = out_refs[k].at[2 * cx + cy]
                _remote(slot, slot, send_sems, recv_sems, 3 * k + j, (cx, cy, c)).wait_recv()
        for cp in sent:
            cp.wait_send()

    return _sequenced(body, name, ts, [jax.ShapeDtypeStruct(t.shape, t.dtype) for t in ts], 3 * n, collective_id)


def _join_halves(fs, name):
    n = len(fs)

    def body(*refs):
        out_refs = refs[n:2 * n]
        send_sems, recv_sems, _ = refs[2 * n:]
        x, y, c, _ = _place()
        sent = []
        for k in range(n):
            H = fs[k].shape[0] // 2
            here = out_refs[k].at[pl.ds(c * H, H), :]
            cp = _remote(here, here, send_sems, recv_sems, k, (x, y, 1 - c))
            cp.start()
            sent.append(cp)
        for k in range(n):
            H = fs[k].shape[0] // 2
            other = out_refs[k].at[pl.ds((1 - c) * H, H), :]
            _remote(other, other, send_sems, recv_sems, k, (x, y, 1 - c)).wait_recv()
        for cp in sent:
            cp.wait_send()

    return pl.pallas_call(
        body, name=name,
        in_specs=[_ANY] * n, out_specs=[_ANY] * n,
        out_shape=[jax.ShapeDtypeStruct(f.shape, f.dtype) for f in fs],
        input_output_aliases={k: k for k in range(n)},
        scratch_shapes=[pltpu.SemaphoreType.DMA((n,)), pltpu.SemaphoreType.DMA((n,)), pltpu.SemaphoreType.DMA((1,))],
    )(*fs)


def _gather_small(v):
    R, C = v.shape

    def body(v_ref, out_ref, send_sems, recv_sems):
        x, y, c, _ = _place()
        me = 4 * x + 2 * y + c
        flips = [(fx, fy, fc) for fx in (0, 1) for fy in (0, 1) for fc in (0, 1)][1:]
        peers = [((1 - x) if fx else x, (1 - y) if fy else y, (1 - c) if fc else c) for fx, fy, fc in flips]
        _handshake(peers)
        sent = []
        for j, peer in enumerate(peers):
            cp = _remote(v_ref, out_ref.at[me], send_sems, recv_sems, j, peer)
            cp.start()
            sent.append(cp)
        for j, peer in enumerate(peers):
            slot = out_ref.at[4 * peer[0] + 2 * peer[1] + peer[2]]
            _remote(slot, slot, send_sems, recv_sems, j, peer).wait_recv()
        for cp in sent:
            cp.wait_send()

    return _sequenced(body, "gather_small", [v], [jax.ShapeDtypeStruct((8, R, C), F32)], 7,
                      COLLECTIVE_IDS["gather_small"])[0]


def _sum_devices(x, own, name):
    S, R, C = x.shape
    tr = _row_tile(R, 2048)

    def body(s_ref, x_ref, own_ref, o_ref):
        me = s_ref[0]
        acc = None
        for k in range(S):
            term = jnp.where(me == k, own_ref[...], x_ref[k])
            acc = term if acc is None else acc + term
        o_ref[...] = acc

    x_, y_, c_ = lax.axis_index("x"), lax.axis_index("y"), lax.axis_index("c")
    me = (4 * x_ + 2 * y_ + c_).astype(jnp.int32).reshape(1)
    return pl.pallas_call(
        body, name=name,
        grid_spec=pltpu.PrefetchScalarGridSpec(
            num_scalar_prefetch=1, grid=(R // tr,),
            in_specs=[pl.BlockSpec((S, tr, C), lambda i, s: (0, i, 0)), pl.BlockSpec((tr, C), lambda i, s: (i, 0))],
            out_specs=pl.BlockSpec((tr, C), lambda i, s: (i, 0))),
        out_shape=jax.ShapeDtypeStruct((R, C), F32),
        compiler_params=_params(),
    )(me, x, own)


def _after(earlier, arrays):
    return lax.optimization_barrier((earlier, arrays))


def _reduce_swap(gs, tag, earlier):
    gs = _after(earlier, gs)[1]
    return gs, _swap_halves(gs, "reduce_swap_" + tag, COLLECTIVE_IDS["swap_" + tag])


def _reduce_exchange(gs, r1, names, tag, later_than):
    r1 = _after(later_than, r1)[1]
    ts = [_add_halves(g, r, "reduce_add_cores_" + nm) for g, r, nm in zip(gs, r1, names)]
    us = _exchange_chips(ts, "reduce_exchange_" + tag, COLLECTIVE_IDS["exchange_" + tag])
    return us, ts


def _reduce_finish(us, ts, names, tag):
    fs = [_sum_chips_into_half(u, t, "reduce_add_chips_" + nm) for u, t, nm in zip(us, ts, names)]
    return _join_halves(fs, "reduce_join_" + tag)


BIG = ["ffn1_w_gate", "ffn1_w_up", "ffn1_w_down", "w_in", "ssm_w_glu", "w_attn_branch", "w_ssm_branch",
       "w_out", "ffn2_w_gate", "ffn2_w_up", "ffn2_w_down"]
SMALL = ["ffn1_norm", "mix_norm", "gate_bias", "rel_bias_table", "ssm_a_re", "ssm_a_im", "ssm_log_dt",
         "ssm_b_re", "ssm_b_im", "ssm_c_re", "ssm_c_im", "ssm_d", "ffn2_norm", "final_norm"]
ORDER = ["ffn1_norm", "ffn1_w_gate", "ffn1_w_up", "ffn1_w_down", "mix_norm", "w_in", "gate_bias", "rel_bias_table",
         "ssm_a_re", "ssm_a_im", "ssm_log_dt", "ssm_b_re", "ssm_b_im", "ssm_c_re", "ssm_c_im", "ssm_d",
         "ssm_w_glu", "w_attn_branch", "w_ssm_branch", "w_out", "ffn2_norm", "ffn2_w_gate", "ffn2_w_up",
         "ffn2_w_down", "final_norm"]


_SMALL_TILE = 8 * LANES


def _pack_small(arrays):
    rows = []
    for a in arrays:
        flat = a.reshape(-1).astype(F32)
        rows.append(jnp.pad(flat, (0, (-flat.shape[0]) % _SMALL_TILE)).reshape(-1, LANES))
    return jnp.concatenate(rows, axis=0)


def _unpack_small(packed, shapes):
    out, r0 = [], 0
    for shp in shapes:
        n = math.prod(shp)
        rows = 8 * -(-n // _SMALL_TILE)
        out.append(packed[r0:r0 + rows].reshape(-1)[:n].reshape(shp))
        r0 += rows
    return out


def _split_cols(g):
    K, N = g.shape
    return g.reshape(K, N_SHARD, N // N_SHARD).transpose(1, 0, 2)


def _join_cols(w):
    S, K, n = w.shape
    return w.transpose(1, 0, 2).reshape(K, S * n)


COL_SHARDED = ("ssm_w_glu", "w_attn_branch", "w_ssm_branch")
TRANSPOSED = ("ffn1_w_gate", "ffn1_w_up", "ffn2_w_gate", "ffn2_w_up", "w_in")


def _shard_2d(name, arr):
    two_d = arr.reshape(arr.shape[-2:])
    return two_d.T if name in TRANSPOSED else two_d


def _shard_nd(name, two_d, shape):
    return (two_d.T if name in TRANSPOSED else two_d).reshape(shape)


class _GradSync:
    def __init__(self, weights, moms, vels):
        self.weights, self.moms, self.vels = weights, moms, vels
        self.grads, self.delta, self.new_m, self.new_v = {}, {}, {}, {}
        self.loss = None
        self._earlier = []
        self._swapped = {}
        self._exchanged = {}

    def swap(self, tag, gw, later_than=()):
        gs = []
        for n in REDUCE_GROUPS[tag]:
            g = gw[n]
            if n in COL_SHARDED:
                g = _split_cols(g)
            elif n in ("w_out", "w_in"):
                g = g.reshape(N_SHARD, g.shape[0] // N_SHARD, g.shape[1])
            gs.append(g)
        self._swapped[tag] = _reduce_swap(gs, tag, list(self._earlier) + list(later_than))
        self._earlier = self._swapped[tag][1]

    def exchange(self, tag, later_than):
        gs, r1 = self._swapped[tag]
        us, ts = _reduce_exchange(gs, r1, REDUCE_GROUPS[tag], tag, later_than)
        self._exchanged[tag] = (us, ts)
        self._earlier = us

    def small_ready(self, gs, loss_blk, later_than=()):
        _, (mine,) = _after(list(self._earlier) + list(later_than),
                            [_pack_small([gs[n] for n in SMALL] + [loss_blk[0:1, :]])])
        others = _gather_small(mine)
        self._exchanged["small"] = (others, mine)
        self._earlier = [others]

    def finish(self, tag):
        made = []
        if tag == "small":
            others, mine = self._exchanged[tag]
            shapes = [self.weights[n].shape for n in SMALL]
            total = _unpack_small(_sum_devices(others, mine, "sum_small"), shapes + [(128,)])
            self.loss = total[-1][0]
            self.grads.update(zip(SMALL, total[:-1]))
            packed = [_pack_small([src[n] for n in SMALL]) for src in (self.weights, self.grads, self.moms, self.vels)]
            for dst, res in zip((self.delta, self.new_m, self.new_v), _adamw(*packed, "adamw_small")):
                dst.update(zip(SMALL, _unpack_small(res, shapes)))
            for n in SMALL:
                made += [self.grads[n], self.delta[n], self.new_m[n], self.new_v[n]]
            return made + [self.loss]
        names = REDUCE_GROUPS[tag]
        us, ts = self._exchanged[tag]
        for n, g in zip(names, _reduce_finish(us, ts, names, tag)):
            shp = self.weights[n].shape
            d, m, v = _adamw(_shard_2d(n, self.weights[n]), g, _shard_2d(n, self.moms[n]), _shard_2d(n, self.vels[n]),
                             "adamw_" + n)
            self.grads[n], self.delta[n] = _shard_nd(n, g, shp), _shard_nd(n, d, shp)
            self.new_m[n], self.new_v[n] = _shard_nd(n, m, shp), _shard_nd(n, v, shp)
            made += [self.grads[n], self.delta[n], self.new_m[n], self.new_v[n]]
        return made

    def finish_all(self):
        self.exchange("ffn1", later_than=self.finish("ffn2"))
        for tag in ("mixer", "w_in", "small", "ffn1"):
            self.finish(tag)


def _local_step(x, target, w, later, small, sync):
    L = x.shape[0]
    row = lambda v: v.reshape(1, -1)

    a_re, a_im = small["ssm_a_re"].reshape(1, NS), small["ssm_a_im"].reshape(1, NS)
    ldt = jnp.repeat(small["ssm_log_dt"].reshape(SSM_GROUPS), SSM_STATE).reshape(1, NS)
    to_cn = lambda b: b.reshape(SSM_GROUPS, SSM_STATE, SSM_GROUP).transpose(2, 0, 1).reshape(SSM_GROUP, NS)
    c_to_cn = lambda c: c.reshape(SSM_GROUPS, SSM_GROUP, SSM_STATE).transpose(1, 0, 2).reshape(SSM_GROUP, NS)
    b_re, b_im = to_cn(small["ssm_b_re"]), to_cn(small["ssm_b_im"])
    c_re, c_im = c_to_cn(small["ssm_c_re"]), c_to_cn(small["ssm_c_im"])
    d_skip = row(small["ssm_d"])
    pw, pwr, bd, cdt = _disc_fwd(a_re, a_im, ldt, b_re, b_im, c_re, c_im)

    onehot = _bucket_onehot()
    table_t = small["rel_bias_table"].T.reshape(3, HEADS_PER_GROUP, N_BUCKETS)
    table_t = jnp.pad(table_t, ((0, 0), (0, 8 - HEADS_PER_GROUP), (0, 0)))
    bias = _bias_expand(table_t, onehot)[:, :, :HEADS_PER_GROUP].reshape(
        3, 2, HEADS_PER_GROUP, ATTN_BLOCK, 2 * ATTN_BLOCK)

    n1, nm, n2, nf = row(small["ffn1_norm"]), row(small["mix_norm"]), row(small["ffn2_norm"]), row(small["final_norm"])
    gate_bias = row(small["gate_bias"])

    x1, a1, b1, *later_full = _ffn_fwd(x, n1, w["ffn1_w_gate"], w["ffn1_w_up"], w["ffn1_w_down"], "ffn1_fwd",
                                       carried=list(later.values()))
    w = dict(w, **dict(zip(later, later_full)))
    for n in COL_SHARDED:
        w[n] = _join_cols(w[n])
    w["w_out"] = w["w_out"].reshape(D_MODEL, D_MODEL)
    w["w_in"] = w["w_in"].reshape(IN_WIDTH, D_MODEL)
    *qkv, u, gates = _mix_in_fwd(x1, nm, w["w_in"], gate_bias)
    q, k, v = qkv[0:3], qkv[3:6], qkv[6:9]
    o_g, lse_g = [], []
    for grp in range(3):
        o, lse = _attn_fwd(q[grp], k[grp], v[grp], bias[grp], f"attn_fwd_{grp}")
        o_g.append(o)
        lse_g.append(lse)
    y, s = _ssm_fwd(u, bd, cdt, d_skip, pw)
    x2, o_attn, *lse_tot = _mix_out_fwd(x1, o_g, lse_g, y, gates, w["w_attn_branch"], w["ssm_w_glu"],
                                        w["w_ssm_branch"], w["w_out"])
    x3, a2, b2 = _ffn_fwd(x2, n2, w["ffn2_w_gate"], w["ffn2_w_up"], w["ffn2_w_down"], "ffn2_fwd")
    loss_blk, dx3, d_nf = _loss_fwd_bwd(x3, nf, target)

    gw, gs = {}, {}
    gs["final_norm"] = d_nf

    dx2, da, db, sact, h, d_out, gs["ffn2_norm"] = _ffn_bwd(dx3, x2, n2, a2, b2, w["ffn2_w_gate"], w["ffn2_w_up"],
                                                            w["ffn2_w_down"], "ffn2_bwd")
    gw["ffn2_w_gate"] = _matmul_tn(da, h[None], "ffn2_dw_gate")
    gw["ffn2_w_up"] = _matmul_tn(db, h[None], "ffn2_dw_up")
    gw["ffn2_w_down"] = _matmul_tn(sact, d_out[None], "ffn2_dw_down")
    sync.swap("ffn2", gw)

    head_sum = (jnp.arange(GROUP_WIDTH)[:, None] // HEAD_DIM == jnp.arange(GROUP_WIDTH)[None, :] // HEAD_DIM).astype(F32)
    (*d_o_delta, dy, dgp, mix, dya, dys, ys2, gel, dglu, gs["gate_bias"]) = _mix_out_bwd(
        dx2, o_attn, y, gates, w["w_attn_branch"], w["ssm_w_glu"], w["w_ssm_branch"], w["w_out"], head_sum)
    sync.exchange("ffn2", later_than=[dy])
    d_o, delta = d_o_delta[0:3], d_o_delta[3:6]
    gw["w_out"] = _matmul_tn(mix[None], dx2[None], "dw_out")[0]
    gw["w_attn_branch"] = _matmul_tn(o_attn[None], dya[None], "dw_attn_branch")[0]
    gw["w_ssm_branch"] = _matmul_tn(ys2[None], dys[None], "dw_ssm_branch")[0]
    gw["ssm_w_glu"] = _matmul_tn(gel[None], dglu[None], "dw_glu")[0]

    dqs, dks, dvs, dsums = [], [], [], []
    for grp in range(3):
        dq, dk, dv, dsum = _attn_bwd(q[grp], k[grp], v[grp], d_o[grp], lse_tot[grp], delta[grp], bias[grp],
                                     f"attn_bwd_{grp}")
        dqs.append(dq)
        dks.append(dk)
        dvs.append(dv)
        dsums.append(dsum.reshape(HEADS_PER_GROUP, -1))
    dsum_all = jnp.pad(jnp.stack(dsums), ((0, 0), (0, 8 - HEADS_PER_GROUP), (0, 0)))
    d_table = _bias_reduce(dsum_all, onehot)[:, :HEADS_PER_GROUP]
    gs["rel_bias_table"] = d_table.reshape(3 * HEADS_PER_GROUP, N_BUCKETS).T

    du, gs["ssm_d"], d_bd, d_cdt, d_ab = _ssm_bwd(dy, u, s, bd, cdt, d_skip, pwr)
    sync.swap("mixer", gw, later_than=[du])
    sync.exchange("mixer", later_than=[dqs[2]])
    group_sum =(jnp.arange(NS)[:, None] // SSM_STATE == jnp.arange(128)[None, :]).astype(F32)
    d_are, d_aim, d_ldt, d_bre, d_bim, d_cre, d_cim = _disc_bwd(a_re, a_im, ldt, b_re, b_im, d_bd, d_cdt, d_ab, group_sum)
    gs["ssm_a_re"], gs["ssm_a_im"] = d_are, d_aim
    gs["ssm_log_dt"] = d_ldt[0, :SSM_GROUPS]
    from_cn = lambda t: t.reshape(SSM_GROUP, SSM_GROUPS, SSM_STATE).transpose(1, 2, 0)
    c_from_cn = lambda t: t.reshape(SSM_GROUP, SSM_GROUPS, SSM_STATE).transpose(1, 0, 2)
    gs["ssm_b_re"], gs["ssm_b_im"] = from_cn(d_bre), from_cn(d_bim)
    gs["ssm_c_re"], gs["ssm_c_im"] = c_from_cn(d_cre), c_from_cn(d_cim)

    dx1, hm, dz, gs["mix_norm"] = _mix_in_bwd(dx2, x1, nm, dqs + dks + dvs, du, dgp, w["w_in"])
    gw["w_in"] = _matmul_tn(dz[None], hm[None], "dw_in")[0]
    sync.swap("w_in", gw)

    dx0, da, db, sact, h, d_out, gs["ffn1_norm"] = _ffn_bwd(dx1, x, n1, a1, b1, w["ffn1_w_gate"], w["ffn1_w_up"],
                                                            w["ffn1_w_down"], "ffn1_bwd")
    sync.exchange("w_in", later_than=[dx0])
    gw["ffn1_w_gate"] = _matmul_tn(da, h[None], "ffn1_dw_gate")
    gw["ffn1_w_up"] = _matmul_tn(db, h[None], "ffn1_dw_up")
    sync.small_ready(gs, loss_blk, later_than=[gw["ffn1_w_up"]])
    gw["ffn1_w_down"] = _matmul_tn(sact, d_out[None], "ffn1_dw_down")
    sync.swap("ffn1", gw)
    return dx0


def kernel(x, ffn1_norm, ffn1_w_gate, ffn1_w_up, ffn1_w_down, mix_norm, w_in, gate_bias, rel_bias_table, ssm_a_re, ssm_a_im, ssm_log_dt, ssm_b_re, ssm_b_im, ssm_c_re, ssm_c_im, ssm_d, ssm_w_glu, w_attn_branch, w_ssm_branch, w_out, ffn2_norm, ffn2_w_gate, ffn2_w_up, ffn2_w_down, final_norm, loss_target, m_ffn1_norm, m_ffn1_w_gate, m_ffn1_w_up, m_ffn1_w_down, m_mix_norm, m_w_in, m_gate_bias, m_rel_bias_table, m_ssm_a_re, m_ssm_a_im, m_ssm_log_dt, m_ssm_b_re, m_ssm_b_im, m_ssm_c_re, m_ssm_c_im, m_ssm_d, m_ssm_w_glu, m_w_attn_branch, m_w_ssm_branch, m_w_out, m_ffn2_norm, m_ffn2_w_gate, m_ffn2_w_up, m_ffn2_w_down, m_final_norm, v_ffn1_norm, v_ffn1_w_gate, v_ffn1_w_up, v_ffn1_w_down, v_mix_norm, v_w_in, v_gate_bias, v_rel_bias_table, v_ssm_a_re, v_ssm_a_im, v_ssm_log_dt, v_ssm_b_re, v_ssm_b_im, v_ssm_c_re, v_ssm_c_im, v_ssm_d, v_ssm_w_glu, v_w_attn_branch, v_w_ssm_branch, v_w_out, v_ffn2_norm, v_ffn2_w_gate, v_ffn2_w_up, v_ffn2_w_down, v_final_norm):
    args = dict(locals())
    weights = {n: args[n] for n in ORDER}
    moms = {n: args["m_" + n] for n in ORDER}
    vels = {n: args["v_" + n] for n in ORDER}

    shard2d = {n: _shard_2d(n, weights[n]) for n in BIG}
    first, rest = BIG[:3], BIG[3:]
    full = dict(zip(first, _gather_weights([shard2d[n].astype(BF16) for n in first], "gather_ffn1_weights")))
    later = {n: shard2d[n].astype(BF16) for n in rest}

    small = {n: weights[n] for n in SMALL}
    sync = _GradSync(weights, moms, vels)
    grad_x = _local_step(x[0], loss_target[0], full, later, small, sync)
    sync.finish_all()
    return (sync.loss, grad_x[None], *[sync.grads[n] for n in ORDER], *[sync.delta[n] for n in ORDER],
            *[sync.new_m[n] for n in ORDER], *[sync.new_v[n] for n in ORDER])
```

```python
import functools
import math

import jax
import jax.numpy as jnp
from jax import lax
from jax.experimental import pallas as pl
from jax.experimental.pallas import tpu as pltpu
from jax.experimental.pallas import tpu_sc as plsc

F32 = jnp.float32
BF16 = jnp.bfloat16
MESH = pl.DeviceIdType.MESH

D_MODEL = 1024
D_FF = 2816
HEAD_DIM = 64
HEADS_PER_GROUP = 4
DILATIONS = (1, 4, 16)
WINDOW_STEPS = 128
ATTN_BLOCK = 128
ATTN_QB = 8
GROUP_WIDTH = HEADS_PER_GROUP * HEAD_DIM
ATTN_WIDTH = 3 * GROUP_WIDTH
N_BUCKETS = 32
MAX_DISTANCE = 2048
NEG_INF = -1e30
SSM_WIDTH = 512
SSM_GROUP = 16
SSM_GROUPS = 32
SSM_STATE = 64
NS = SSM_GROUPS * SSM_STATE
EPS = 1e-6
IN_WIDTH = 3 * ATTN_WIDTH + SSM_WIDTH + 2 * D_MODEL
Q_SCALE = HEAD_DIM ** -0.5
N_SHARD = 4
FF_SHARD = D_FF // N_SHARD
ADAM_LR, ADAM_B1, ADAM_B2, ADAM_EPS, ADAM_WD, ADAM_STEP = 0.001, 0.9, 0.999, 1e-08, 0.01, 10

LANES = 128
VMEM_LIMIT = 56 * 1024 * 1024
ROW_TILE = 512
FFN_BWD_TILE = 256
SSM_CHUNK = 256
SSM_FWD_CHUNK = 512
SCAN_LANES = 512
ADAMW_BLOCK_BYTES = 2 << 20
TN_VMEM_BUDGET = 40 * 1024 * 1024
REDUCE_GROUPS = {
    "ffn2": ["ffn2_w_gate", "ffn2_w_up", "ffn2_w_down"],
    "mixer": ["w_out", "w_attn_branch", "w_ssm_branch", "ssm_w_glu"],
    "w_in": ["w_in"],
    "ffn1": ["ffn1_w_gate", "ffn1_w_up", "ffn1_w_down"],
}
COLLECTIVE_IDS = {name: i for i, name in enumerate(
    ["gather", "gather_small"] + [stage + "_" + tag for tag in REDUCE_GROUPS for stage in ("swap", "exchange")])}


def _params(**kw):
    return pltpu.CompilerParams(vmem_limit_bytes=VMEM_LIMIT, **kw)


def _dot(a, b):
    return jnp.dot(a, b, preferred_element_type=F32)


def _dot_nt(a, b):
    return lax.dot_general(a, b, (((1,), (1,)), ((), ())), preferred_element_type=F32)


def _dot_tn(a, b):
    return lax.dot_general(a, b, (((0,), (0,)), ((), ())), preferred_element_type=F32)


def _dot_exact(a, b):
    return jnp.dot(a, b, preferred_element_type=F32, precision=lax.Precision.HIGHEST)


def _dot_nt_exact(a, b):
    return lax.dot_general(a, b, (((1,), (1,)), ((), ())), preferred_element_type=F32,
                           precision=lax.Precision.HIGHEST)


def _rms(x):
    r = lax.rsqrt(jnp.mean(x * x, axis=-1, keepdims=True) + EPS)
    return r, x * r


def _rms_bwd(dh, g, r, xhat):
    dxh = dh * g
    return r * (dxh - xhat * jnp.mean(dxh * xhat, axis=-1, keepdims=True))


def _sigmoid(x):
    return 0.5 + 0.5 * jnp.tanh(0.5 * x)


_GELU_C = math.sqrt(2.0 / math.pi)


def _gelu(x):
    return 0.5 * x * (1.0 + jnp.tanh(_GELU_C * (x + 0.044715 * x * x * x)))


def _gelu_grad(x):
    t = jnp.tanh(_GELU_C * (x + 0.044715 * x * x * x))
    return 0.5 * (1.0 + t) + 0.5 * x * (1.0 - t * t) * _GELU_C * (1.0 + 3 * 0.044715 * x * x)


def _whole():
    return pl.BlockSpec(memory_space=pltpu.VMEM)


def _row_tile(rows, cap):
    if rows <= cap:
        return rows
    return max(t for t in range(8, cap + 1, 8) if rows % t == 0)


def _rows(tm, w):
    return pl.BlockSpec((tm, w), lambda i: (i, 0))


def _acc_row(w):
    return pl.BlockSpec((1, w), lambda i: (0, 0))


def _ffn_fwd(x, g, wg, wu, wd, name, carried=()):
    L = x.shape[0]
    tm = min(ROW_TILE, L)
    n = len(carried)
    steps = L // tm

    def body(x_ref, g_ref, wg_ref, wu_ref, wd_ref, *refs):
        shard_refs, (xo_ref, a_ref, b_ref), full_refs, sems = refs[:n], refs[n:n + 3], refs[n + 3:2 * n + 3], refs[2 * n + 3:]
        if n:
            start, finish = _gather_parts([w.shape for w in carried], shard_refs, full_refs, *sems)
            pl.when(pl.program_id(0) == 0)(start)
        xv = x_ref[...]
        r, xhat = _rms(xv)
        h = (xhat * g_ref[...]).astype(BF16)
        acc = jnp.zeros((tm, D_MODEL), F32)
        for j in range(N_SHARD):
            a = _dot_nt(h, wg_ref[j])
            b = _dot_nt(h, wu_ref[j])
            a_ref[j] = a.astype(BF16)
            b_ref[j] = b.astype(BF16)
            s = (a * _sigmoid(a) * b).astype(BF16)
            acc = acc + _dot(s, wd_ref[j])
        xo_ref[...] = xv + 0.5 * acc
        if n:
            pl.when(pl.program_id(0) == steps - 1)(finish)

    act = pl.BlockSpec((N_SHARD, tm, FF_SHARD), lambda i: (0, i, 0))
    return pl.pallas_call(
        body, name=name, grid=(steps,),
        in_specs=[_rows(tm, D_MODEL), _whole(), _whole(), _whole(), _whole()] + [_ANY] * n,
        out_specs=[_rows(tm, D_MODEL), act, act] + [_ANY] * n,
        out_shape=[jax.ShapeDtypeStruct((L, D_MODEL), F32),
                   jax.ShapeDtypeStruct((N_SHARD, L, FF_SHARD), BF16),
                   jax.ShapeDtypeStruct((N_SHARD, L, FF_SHARD), BF16)]
        + [jax.ShapeDtypeStruct((N_SHARD,) + w.shape, w.dtype) for w in carried],
        scratch_shapes=[pltpu.SemaphoreType.DMA((7 * n,)), pltpu.SemaphoreType.DMA((7 * n,))] if n else [],
        compiler_params=_params(),
    )(x, g, wg, wu, wd, *carried)


def _ffn_bwd(dxo, x, g, a, b, wg, wu, wd, name):
    L = x.shape[0]
    tm = min(FFN_BWD_TILE, L)

    def body(dxo_ref, x_ref, g_ref, a_ref, b_ref, wg_ref, wu_ref, wd_ref,
             dxi_ref, da_ref, db_ref, s_ref, h_ref, do_ref, dg_ref):
        i = pl.program_id(0)
        xv = x_ref[...]
        gv = g_ref[...]
        r, xhat = _rms(xv)
        h_ref[...] = (xhat * gv).astype(BF16)
        dxo_v = dxo_ref[...]
        d_out = (0.5 * dxo_v).astype(BF16)
        do_ref[...] = d_out
        dh = jnp.zeros((tm, D_MODEL), F32)
        for j in range(N_SHARD):
            av = a_ref[j].astype(F32)
            bv = b_ref[j].astype(F32)
            sg = _sigmoid(av)
            sl = av * sg
            ds = _dot_nt(d_out, wd_ref[j])
            dbv = (ds * sl).astype(BF16)
            dav = (ds * bv * (sg * (1.0 + av * (1.0 - sg)))).astype(BF16)
            da_ref[j] = dav
            db_ref[j] = dbv
            s_ref[j] = (sl * bv).astype(BF16)
            dh = dh + _dot(dav, wg_ref[j]) + _dot(dbv, wu_ref[j])

        @pl.when(i == 0)
        def _():
            dg_ref[...] = jnp.zeros_like(dg_ref)

        dg_ref[...] += jnp.sum(dh * xhat, axis=0, keepdims=True)
        dxi_ref[...] = dxo_v + _rms_bwd(dh, gv, r, xhat)

    act = pl.BlockSpec((N_SHARD, tm, FF_SHARD), lambda i: (0, i, 0))
    act_shape = jax.ShapeDtypeStruct((N_SHARD, L, FF_SHARD), BF16)
    return pl.pallas_call(
        body, name=name, grid=(L // tm,),
        in_specs=[_rows(tm, D_MODEL), _rows(tm, D_MODEL), _whole(), act, act, _whole(), _whole(), _whole()],
        out_specs=[_rows(tm, D_MODEL), act, act, act, _rows(tm, D_MODEL), _rows(tm, D_MODEL), _acc_row(D_MODEL)],
        out_shape=[jax.ShapeDtypeStruct((L, D_MODEL), F32), act_shape, act_shape, act_shape,
                   jax.ShapeDtypeStruct((L, D_MODEL), BF16), jax.ShapeDtypeStruct((L, D_MODEL), BF16),
                   jax.ShapeDtypeStruct((1, D_MODEL), F32)],
        compiler_params=_params(),
    )(dxo, x, g, a, b, wg, wu, wd)


def _matmul_tn(a, b, name):
    ja, L, K = a.shape
    jb, _, N = b.shape
    J = max(ja, jb)
    splits = [s for s in (1, 2, 4, 8) if s == 1 or N % (s * LANES) == 0]
    nsplit = next((s for s in splits if 2 * K * (N // s) * 4 <= TN_VMEM_BUDGET // 2), splits[-1])
    nc = N // nsplit
    left = TN_VMEM_BUDGET - 2 * K * nc * 4
    row_bytes = 2 * (K * a.dtype.itemsize + nc * b.dtype.itemsize)
    tm = next((t for t in (2048, 1024, 512, 256) if L % t == 0 and t * row_bytes <= left), min(128, L))

    def body(a_ref, b_ref, o_ref):
        @pl.when(pl.program_id(2) == 0)
        def _():
            o_ref[...] = jnp.zeros_like(o_ref)

        o_ref[...] += _dot_tn(a_ref[...].astype(BF16), b_ref[...].astype(BF16))

    return pl.pallas_call(
        body, name=name, grid=(J, nsplit, L // tm),
        in_specs=[pl.BlockSpec((None, tm, K), (lambda j, s, i: (j, i, 0)) if ja > 1 else (lambda j, s, i: (0, i, 0))),
                  pl.BlockSpec((None, tm, nc), (lambda j, s, i: (j, i, s)) if jb > 1 else (lambda j, s, i: (0, i, s)))],
        out_specs=pl.BlockSpec((None, K, nc), lambda j, s, i: (j, 0, s)),
        out_shape=jax.ShapeDtypeStruct((J, K, N), F32),
        compiler_params=_params(),
    )(a, b)


def _loss_fwd_bwd(x, g, target):
    L = x.shape[0]
    tm = min(ROW_TILE, L)

    def body(x_ref, g_ref, t_ref, loss_ref, dx_ref, dg_ref):
        i = pl.program_id(0)
        xv = x_ref[...]
        gv = g_ref[...]
        r, xhat = _rms(xv)
        err = xhat * gv - t_ref[...]
        part = 0.5 * jnp.sum(jnp.sum(err * err, axis=1, keepdims=True) * (1.0 / D_MODEL), axis=0, keepdims=True)
        dy = err * (1.0 / D_MODEL)

        @pl.when(i == 0)
        def _():
            dg_ref[...] = jnp.zeros_like(dg_ref)
            loss_ref[...] = jnp.zeros_like(loss_ref)

        loss_ref[...] += jnp.broadcast_to(part, loss_ref.shape)
        dg_ref[...] += jnp.sum(dy * xhat, axis=0, keepdims=True)
        dx_ref[...] = _rms_bwd(dy, gv, r, xhat)

    return pl.pallas_call(
        body, name="loss_fwd_bwd", grid=(L // tm,),
        in_specs=[_rows(tm, D_MODEL), _whole(), _rows(tm, D_MODEL)],
        out_specs=[pl.BlockSpec((8, 128), lambda i: (0, 0)), _rows(tm, D_MODEL), _acc_row(D_MODEL)],
        out_shape=[jax.ShapeDtypeStruct((8, 128), F32), jax.ShapeDtypeStruct((L, D_MODEL), F32),
                   jax.ShapeDtypeStruct((1, D_MODEL), F32)],
        compiler_params=_params(),
    )(x, g, target)


_C_K = ATTN_WIDTH
_C_V = 2 * ATTN_WIDTH
_C_U = 3 * ATTN_WIDTH
_C_G = _C_U + SSM_WIDTH


def _residue_spec(d, tm):
    return pl.BlockSpec((d, tm // d, GROUP_WIDTH), lambda i: (0, i, 0))


def _residue_shape(d, L, dtype):
    return jax.ShapeDtypeStruct((d, L // d, GROUP_WIDTH), dtype)


def _residue_scratch(tm):
    return pltpu.VMEM((GROUP_WIDTH // LANES, tm, LANES), F32)


def _to_residues(val, out_ref, scr, d):
    if d == 1:
        out_ref[0] = val.astype(out_ref.dtype)
        return
    tm = val.shape[0]
    for half in range(GROUP_WIDTH // LANES):
        cols = slice(half * LANES, (half + 1) * LANES)
        scr[half] = val[:, cols]
        for r in range(d):
            out_ref[r, :, cols] = scr[half, pl.ds(r, tm // d, stride=d), :].astype(out_ref.dtype)


def _from_residues(ref, scr, d):
    if d == 1:
        return ref[0].astype(F32)
    rows = ref.shape[1]
    for half in range(GROUP_WIDTH // LANES):
        cols = slice(half * LANES, (half + 1) * LANES)
        for r in range(d):
            scr[half, pl.ds(r, rows, stride=d), :] = ref[r, :, cols].astype(F32)
    return jnp.concatenate([scr[half] for half in range(GROUP_WIDTH // LANES)], axis=1)


def _mix_in_fwd(x, g, w_in, gate_bias):
    L = x.shape[0]
    tm = min(ROW_TILE, L)

    def body(x_ref, g_ref, w_ref, gb_ref, *refs):
        qkv_refs, (u_ref, gate_ref, scr) = refs[:9], refs[9:]
        r, xhat = _rms(x_ref[...])
        h = (xhat * g_ref[...]).astype(BF16)
        for part, (c0, scale) in enumerate(((0, Q_SCALE), (_C_K, 1.0), (_C_V, 1.0))):
            z = _dot_nt(h, w_ref[c0:c0 + ATTN_WIDTH, :]) * scale
            for grp, d in enumerate(DILATIONS):
                _to_residues(z[:, grp * GROUP_WIDTH:(grp + 1) * GROUP_WIDTH], qkv_refs[3 * part + grp], scr, d)
        u_ref[...] = _dot_nt(h, w_ref[_C_U:_C_G, :])
        gate_ref[...] = _sigmoid(_dot_nt(h, w_ref[_C_G:IN_WIDTH, :]) + gb_ref[...])

    return pl.pallas_call(
        body, name="mix_in_fwd", grid=(L // tm,),
        in_specs=[_rows(tm, D_MODEL), _whole(), _whole(), _whole()],
        out_specs=[_residue_spec(d, tm) for d in DILATIONS] * 3 + [_rows(tm, SSM_WIDTH), _rows(tm, 2 * D_MODEL)],
        out_shape=[_residue_shape(d, L, BF16) for d in DILATIONS] * 3
        + [jax.ShapeDtypeStruct((L, SSM_WIDTH), F32), jax.ShapeDtypeStruct((L, 2 * D_MODEL), F32)],
        scratch_shapes=[_residue_scratch(tm)],
        compiler_params=_params(),
    )(x, g, w_in, gate_bias)


def _mix_in_bwd(dx2, x, g, dqkv, du, dgp, w_in):
    L = x.shape[0]
    tm = min(ROW_TILE, L)

    def body(dx2_ref, x_ref, g_ref, *refs):
        piece_refs = refs[:9]
        du_ref, dgp_ref, w_ref, dx1_ref, h_ref, dz_ref, dg_ref, scr = refs[9:]
        i = pl.program_id(0)
        gv = g_ref[...]
        r, xhat = _rms(x_ref[...])
        h_ref[...] = (xhat * gv).astype(BF16)
        for part in range(3):
            for grp, d in enumerate(DILATIONS):
                c0 = part * ATTN_WIDTH + grp * GROUP_WIDTH
                dz_ref[:, c0:c0 + GROUP_WIDTH] = _from_residues(piece_refs[3 * part + grp], scr, d).astype(BF16)
        dz_ref[:, _C_U:_C_G] = du_ref[...].astype(BF16)
        dz_ref[:, _C_G:IN_WIDTH] = dgp_ref[...]
        dh = _dot(dz_ref[...], w_ref[...])

        @pl.when(i == 0)
        def _():
            dg_ref[...] = jnp.zeros_like(dg_ref)

        dg_ref[...] += jnp.sum(dh * xhat, axis=0, keepdims=True)
        dx1_ref[...] = dx2_ref[...] + _rms_bwd(dh, gv, r, xhat)

    return pl.pallas_call(
        body, name="mix_in_bwd", grid=(L // tm,),
        in_specs=[_rows(tm, D_MODEL), _rows(tm, D_MODEL), _whole()] + [_residue_spec(d, tm) for d in DILATIONS] * 3
        + [_rows(tm, SSM_WIDTH), _rows(tm, 2 * D_MODEL), _whole()],
        out_specs=[_rows(tm, D_MODEL), _rows(tm, D_MODEL), _rows(tm, IN_WIDTH), _acc_row(D_MODEL)],
        out_shape=[jax.ShapeDtypeStruct((L, D_MODEL), F32), jax.ShapeDtypeStruct((L, D_MODEL), BF16),
                   jax.ShapeDtypeStruct((L, IN_WIDTH), BF16), jax.ShapeDtypeStruct((1, D_MODEL), F32)],
        scratch_shapes=[_residue_scratch(tm)],
        compiler_params=_params(),
    )(dx2, x, g, *dqkv, du, dgp, w_in)


def _bucket_onehot():
    qi = jnp.arange(ATTN_BLOCK)[:, None]
    kj = jnp.arange(2 * ATTN_BLOCK)[None, :]
    steps = jnp.maximum(qi + ATTN_BLOCK - kj, 0)
    max_exact = N_BUCKETS // 2
    out = []
    for d in DILATIONS:
        dist = steps * d
        df = jnp.maximum(dist, 1).astype(F32)
        large = max_exact + (jnp.log(df / max_exact) / math.log(MAX_DISTANCE / max_exact)
                             * (N_BUCKETS - max_exact)).astype(jnp.int32)
        large = jnp.minimum(large, N_BUCKETS - 1)
        bucket = jnp.where(dist < max_exact, dist, large).reshape(-1)
        out.append((bucket[None, :] == jnp.arange(N_BUCKETS)[:, None]).astype(F32))
    return jnp.stack(out)


def _bias_expand(table_t, onehot):
    n = onehot.shape[-1]

    def body(t_ref, oh_ref, o_ref):
        bias = _dot_exact(t_ref[...], oh_ref[...])
        col = lax.broadcasted_iota(jnp.int32, (8, n), 1)
        qi = col // (2 * ATTN_BLOCK)
        kj = col - qi * (2 * ATTN_BLOCK)
        steps = qi + ATTN_BLOCK - kj
        band = (steps >= 0) & (steps <= WINDOW_STEPS)
        o_ref[0] = jnp.where(band & (kj >= ATTN_BLOCK), bias, NEG_INF)
        o_ref[1] = jnp.where(band, bias, NEG_INF)

    return pl.pallas_call(
        body, name="bias_expand", grid=(3,),
        in_specs=[pl.BlockSpec((None, 8, N_BUCKETS), lambda g: (g, 0, 0)),
                  pl.BlockSpec((None, N_BUCKETS, n), lambda g: (g, 0, 0))],
        out_specs=pl.BlockSpec((None, 2, 8, n), lambda g: (g, 0, 0, 0)),
        out_shape=jax.ShapeDtypeStruct((3, 2, 8, n), F32),
        compiler_params=_params(),
    )(table_t, onehot)


def _bias_reduce(dsum, onehot):
    n = onehot.shape[-1]

    def body(d_ref, oh_ref, o_ref):
        o_ref[...] = _dot_nt_exact(d_ref[...], oh_ref[...])

    return pl.pallas_call(
        body, name="bias_reduce", grid=(3,),
        in_specs=[pl.BlockSpec((None, 8, n), lambda g: (g, 0, 0)),
                  pl.BlockSpec((None, N_BUCKETS, n), lambda g: (g, 0, 0))],
        out_specs=pl.BlockSpec((None, 8, N_BUCKETS), lambda g: (g, 0, 0)),
        out_shape=jax.ShapeDtypeStruct((3, 8, N_BUCKETS), F32),
        compiler_params=_params(),
    )(dsum, onehot)


def _head_of_col(rows):
    return lax.broadcasted_iota(jnp.int32, (rows, GROUP_WIDTH), 1) // HEAD_DIM


_STACK_ROWS = HEADS_PER_GROUP * ATTN_BLOCK


def _stack_heads(x, head_of_col):
    return jnp.concatenate([jnp.where(head_of_col == hh, x, jnp.zeros_like(x)) for hh in range(HEADS_PER_GROUP)],
                           axis=0)


def _attn_specs(qb):
    rows = qb * ATTN_BLOCK
    cur = pl.BlockSpec((None, rows, GROUP_WIDTH), lambda r, n: (r, n, 0))
    prev = pl.BlockSpec((None, ATTN_BLOCK, GROUP_WIDTH), lambda r, n: (r, jnp.maximum(n * qb - 1, 0), 0))
    bias = pl.BlockSpec((2, HEADS_PER_GROUP, ATTN_BLOCK, 2 * ATTN_BLOCK), lambda r, n: (0, 0, 0, 0))
    return cur, prev, bias


def _attn_fwd(q, k, v, bias, name):
    d, M, _ = q.shape
    nb = M // ATTN_BLOCK
    qb = min(ATTN_QB, nb)

    def body(q_ref, kp_ref, kc_ref, vp_ref, vc_ref, bias_ref, o_ref, lse_ref):
        n = pl.program_id(1)
        q_head = _head_of_col(ATTN_BLOCK)
        kwin = jnp.concatenate([kp_ref[...], kc_ref[...]], axis=0)
        vwin = jnp.concatenate([vp_ref[...], vc_ref[...]], axis=0)
        ones = jnp.ones((2 * ATTN_BLOCK, LANES), BF16)
        for b in range(qb):
            rows = slice(b * ATTN_BLOCK, (b + 1) * ATTN_BLOCK)
            window = slice(b * ATTN_BLOCK, (b + 2) * ATTN_BLOCK)
            variant = jnp.minimum(n, 1) if b == 0 else 1
            kk = kwin[window]
            vv = vwin[window]
            q4 = _stack_heads(q_ref[rows, :], q_head)
            logits = _dot_nt(q4, kk) + bias_ref[variant].reshape(_STACK_ROWS, 2 * ATTN_BLOCK)
            m = jnp.max(logits, axis=1, keepdims=True)
            p16 = jnp.exp(logits - m).astype(BF16)
            den = _dot(p16, ones)[:, 0:1]
            out = _dot(p16, vv) * (1.0 / den)
            lse = m + jnp.log(den)
            o_acc = jnp.zeros((ATTN_BLOCK, GROUP_WIDTH), F32)
            lse_acc = jnp.zeros((ATTN_BLOCK, GROUP_WIDTH), F32)
            for hh in range(HEADS_PER_GROUP):
                head_rows = slice(hh * ATTN_BLOCK, (hh + 1) * ATTN_BLOCK)
                o_acc = jnp.where(q_head == hh, out[head_rows], o_acc)
                lse_acc = jnp.where(q_head == hh, lse[head_rows], lse_acc)
            o_ref[rows, :] = o_acc
            lse_ref[rows, :] = lse_acc

    cur, prev, full = _attn_specs(qb)
    return pl.pallas_call(
        body, name=name, grid=(d, nb // qb),
        in_specs=[cur, prev, cur, prev, cur, full],
        out_specs=[cur, cur],
        out_shape=[jax.ShapeDtypeStruct((d, M, GROUP_WIDTH), F32)] * 2,
        compiler_params=_params(),
    )(q, k, k, v, v, bias)


def _attn_bwd(q, k, v, do, lse, delta, bias, name):
    d, M, _ = q.shape
    nb = M // ATTN_BLOCK
    qb = min(ATTN_QB, nb)
    ns = nb // qb
    rows_q = qb * ATTN_BLOCK
    last = slice(rows_q - ATTN_BLOCK, rows_q)

    def body(q_ref, kp_ref, kc_ref, vp_ref, vc_ref, do_ref, lse_ref, dl_ref, bias_ref,
             dq_ref, dk_ref, dv_ref, dsum_ref, pk_ref, pv_ref, wk_ref, wv_ref):
        r = pl.program_id(0)
        n = pl.program_id(1)

        @pl.when((r == 0) & (n == 0))
        def _():
            dsum_ref[...] = jnp.zeros_like(dsum_ref)

        @pl.when(n == 0)
        def _():
            pk_ref[...] = jnp.zeros_like(pk_ref)
            pv_ref[...] = jnp.zeros_like(pv_ref)

        @pl.when(n < ns)
        def _():
            q_head = _head_of_col(ATTN_BLOCK)
            kwin = jnp.concatenate([kp_ref[...], kc_ref[...]], axis=0)
            vwin = jnp.concatenate([vp_ref[...], vc_ref[...]], axis=0)
            wk_ref[...] = jnp.zeros_like(wk_ref)
            wv_ref[...] = jnp.zeros_like(wv_ref)
            for b in range(qb):
                rows = slice(b * ATTN_BLOCK, (b + 1) * ATTN_BLOCK)
                window = slice(b * ATTN_BLOCK, (b + 2) * ATTN_BLOCK)
                variant = jnp.minimum(n, 1) if b == 0 else 1
                kk = kwin[window]
                vv = vwin[window]
                q4 = _stack_heads(q_ref[rows, :], q_head)
                do4 = _stack_heads(do_ref[rows, :], q_head)
                heads = [hh * HEAD_DIM for hh in range(HEADS_PER_GROUP)]
                lse4 = jnp.concatenate([lse_ref[rows, c0:c0 + 1] for c0 in heads], axis=0)
                dl4 = jnp.concatenate([dl_ref[rows, c0:c0 + 1] for c0 in heads], axis=0)
                logits = _dot_nt(q4, kk) + bias_ref[variant].reshape(_STACK_ROWS, 2 * ATTN_BLOCK)
                p = jnp.exp(logits - lse4)
                ds = p * (_dot_nt(do4, vv) - dl4)
                dsum_ref[...] += ds.reshape(HEADS_PER_GROUP, ATTN_BLOCK, 2 * ATTN_BLOCK)
                ds16 = ds.astype(BF16)
                dq4 = _dot(ds16, kk)
                dq_acc = jnp.zeros((ATTN_BLOCK, GROUP_WIDTH), F32)
                for hh in range(HEADS_PER_GROUP):
                    dq_acc = jnp.where(q_head == hh, dq4[hh * ATTN_BLOCK:(hh + 1) * ATTN_BLOCK], dq_acc)
                dq_ref[rows, :] = (dq_acc * Q_SCALE).astype(BF16)
                wk_ref[window, :] += _dot_tn(ds16, q4)
                wv_ref[window, :] += _dot_tn(p.astype(BF16), do4)
            for out_ref, part_ref, win_ref in ((dk_ref, pk_ref, wk_ref), (dv_ref, pv_ref, wv_ref)):
                if ns == 1:
                    out_ref[...] = win_ref[ATTN_BLOCK:, :].astype(BF16)
                    continue
                if qb > 1:
                    out_ref[0:rows_q - ATTN_BLOCK, :] = part_ref[0:rows_q - ATTN_BLOCK, :].astype(BF16)
                out_ref[last, :] = (part_ref[last, :] + win_ref[0:ATTN_BLOCK, :]).astype(BF16)
                part_ref[...] = win_ref[ATTN_BLOCK:, :]

        if ns > 1:
            @pl.when(n == ns)
            def _():
                dk_ref[...] = pk_ref[...].astype(BF16)
                dv_ref[...] = pv_ref[...].astype(BF16)

    def clamp(n):
        return jnp.minimum(n, ns - 1)

    cur = pl.BlockSpec((None, rows_q, GROUP_WIDTH), lambda r, n: (r, clamp(n), 0))
    prev = pl.BlockSpec((None, ATTN_BLOCK, GROUP_WIDTH), lambda r, n: (r, jnp.maximum(clamp(n) * qb - 1, 0), 0))
    lag = pl.BlockSpec((None, rows_q, GROUP_WIDTH), lambda r, n: (r, jnp.maximum(n - 1, 0), 0))
    full = pl.BlockSpec((2, HEADS_PER_GROUP, ATTN_BLOCK, 2 * ATTN_BLOCK), lambda r, n: (0, 0, 0, 0))
    acc = pl.BlockSpec((HEADS_PER_GROUP, ATTN_BLOCK, 2 * ATTN_BLOCK), lambda r, n: (0, 0, 0))
    return pl.pallas_call(
        body, name=name, grid=(d, ns + 1 if ns > 1 else 1),
        in_specs=[cur, prev, cur, prev, cur, cur, cur, cur, full],
        out_specs=[cur, lag, lag, acc],
        out_shape=[jax.ShapeDtypeStruct((d, M, GROUP_WIDTH), BF16)] * 3
        + [jax.ShapeDtypeStruct((HEADS_PER_GROUP, ATTN_BLOCK, 2 * ATTN_BLOCK), F32)],
        scratch_shapes=[pltpu.VMEM((rows_q, GROUP_WIDTH), F32), pltpu.VMEM((rows_q, GROUP_WIDTH), F32),
                        pltpu.VMEM((rows_q + ATTN_BLOCK, GROUP_WIDTH), F32),
                        pltpu.VMEM((rows_q + ATTN_BLOCK, GROUP_WIDTH), F32)],
        compiler_params=_params(),
    )(q, k, k, v, v, do, lse, delta, bias)


def _disc_math(a_re, a_im, ldt, b_re, b_im):
    dt = jnp.exp(ldt)
    mag = jnp.exp(a_re * dt)
    ab_re = mag * jnp.cos(a_im * dt)
    ab_im = mag * jnp.sin(a_im * dt)
    den = a_re * a_re + a_im * a_im
    xr = ab_re - 1.0
    coef_re = (xr * a_re + ab_im * a_im) / den
    coef_im = (ab_im * a_re - xr * a_im) / den
    return ab_re, ab_im, coef_re * b_re - coef_im * b_im, coef_re * b_im + coef_im * b_re


def _block_diag_mask():
    row_g = lax.broadcasted_iota(jnp.int32, (SSM_WIDTH, 2 * NS), 0) // SSM_GROUP
    col = lax.broadcasted_iota(jnp.int32, (SSM_WIDTH, 2 * NS), 1)
    col_g = jnp.where(col >= NS, col - NS, col) // SSM_STATE
    return row_g == col_g


def _disc_fwd(a_re, a_im, ldt, b_re, b_im, c_re, c_im):
    def body(are_ref, aim_ref, ldt_ref, bre_ref, bim_ref, cre_ref, cim_ref, pw_ref, pwr_ref, bd_ref, cdt_ref):
        ab_re, ab_im, bb_re, bb_im = _disc_math(are_ref[...], aim_ref[...], ldt_ref[...], bre_ref[...], bim_ref[...])
        row = lax.broadcasted_iota(jnp.int32, (8, NS), 0)
        pr, pi = ab_re, ab_im
        t_re = jnp.zeros((8, NS), F32)
        t_im = jnp.zeros((8, NS), F32)
        u_re = jnp.zeros((8, NS), F32)
        u_im = jnp.zeros((8, NS), F32)
        for j in range(8):
            t_re = jnp.where(row == j, pr, t_re)
            t_im = jnp.where(row == j, pi, t_im)
            u_re = jnp.where(row == 7 - j, pr, u_re)
            u_im = jnp.where(row == 7 - j, pi, u_im)
            pr, pi = pr * ab_re - pi * ab_im, pr * ab_im + pi * ab_re
        pw_ref[0] = t_re
        pw_ref[1] = t_im
        pwr_ref[0] = u_re
        pwr_ref[1] = u_im
        mask = _block_diag_mask()
        zero = jnp.zeros((SSM_WIDTH, 2 * NS), F32)
        bfull = jnp.concatenate([jnp.concatenate([bb_re] * SSM_GROUPS, axis=0),
                                 jnp.concatenate([bb_im] * SSM_GROUPS, axis=0)], axis=1)
        bd_ref[...] = jnp.where(mask, bfull, zero).astype(BF16)
        cfull = jnp.concatenate([jnp.concatenate([cre_ref[...]] * SSM_GROUPS, axis=0),
                                 jnp.concatenate([-cim_ref[...]] * SSM_GROUPS, axis=0)], axis=1)
        cdt_ref[...] = jnp.where(mask, cfull, zero).astype(BF16)

    return pl.pallas_call(
        body, name="s5_disc_fwd",
        in_specs=[_whole()] * 7, out_specs=[_whole()] * 4,
        out_shape=[jax.ShapeDtypeStruct((2, 8, NS), F32), jax.ShapeDtypeStruct((2, 8, NS), F32),
                   jax.ShapeDtypeStruct((SSM_WIDTH, 2 * NS), BF16), jax.ShapeDtypeStruct((SSM_WIDTH, 2 * NS), BF16)],
        compiler_params=_params(),
    )(a_re, a_im, ldt, b_re, b_im, c_re, c_im)


def _disc_bwd(a_re, a_im, ldt, b_re, b_im, d_bd, d_cdt, d_ab, group_sum):
    def body(are_ref, aim_ref, ldt_ref, bre_ref, bim_ref, dbd_ref, dcdt_ref, dab_ref, gs_ref,
             dare_ref, daim_ref, dldt_ref, dbre_ref, dbim_ref, dcre_ref, dcim_ref):
        col = lax.broadcasted_iota(jnp.int32, (SSM_GROUP, 2 * NS), 1)
        col_g = jnp.where(col >= NS, col - NS, col) // SSM_STATE
        acc_b = jnp.zeros((SSM_GROUP, 2 * NS), F32)
        acc_c = jnp.zeros((SSM_GROUP, 2 * NS), F32)
        for g in range(SSM_GROUPS):
            rows = slice(g * SSM_GROUP, (g + 1) * SSM_GROUP)
            acc_b = acc_b + jnp.where(col_g == g, dbd_ref[rows, :], 0.0)
            acc_c = acc_c + jnp.where(col_g == g, dcdt_ref[rows, :], 0.0)
        dcre_ref[...] = acc_c[:, :NS]
        dcim_ref[...] = -acc_c[:, NS:]
        dab_re = jnp.sum(dab_ref[0], axis=0, keepdims=True)
        dab_im = jnp.sum(dab_ref[1], axis=0, keepdims=True)
        _, vjp = jax.vjp(_disc_math, are_ref[...], aim_ref[...], ldt_ref[...], bre_ref[...], bim_ref[...])
        d_are, d_aim, d_ldt, d_bre, d_bim = vjp((dab_re, dab_im, acc_b[:, :NS], acc_b[:, NS:]))
        dare_ref[...] = d_are
        daim_ref[...] = d_aim
        dbre_ref[...] = d_bre
        dbim_ref[...] = d_bim
        dldt_ref[...] = _dot_exact(jnp.broadcast_to(d_ldt, (8, NS)), gs_ref[...])

    vec = jax.ShapeDtypeStruct((1, NS), F32)
    mat = jax.ShapeDtypeStruct((SSM_GROUP, NS), F32)
    return pl.pallas_call(
        body, name="s5_disc_bwd",
        in_specs=[_whole()] * 9, out_specs=[_whole()] * 7,
        out_shape=[vec, vec, jax.ShapeDtypeStruct((8, 128), F32), mat, mat, mat, mat],
        compiler_params=_params(),
    )(a_re, a_im, ldt, b_re, b_im, d_bd, d_cdt, d_ab, group_sum)


def _scan_blocks(buf, pw_ref, carry_ref, n_blocks, reverse):
    row = lax.broadcasted_iota(jnp.int32, (8, SCAN_LANES), 0)
    for lc in range(NS // SCAN_LANES):
        re_cols = pl.ds(lc * SCAN_LANES, SCAN_LANES)
        im_cols = pl.ds(NS + lc * SCAN_LANES, SCAN_LANES)
        pr = pw_ref[0, :, re_cols]
        pi = pw_ref[1, :, re_cols]
        if reverse:
            pi = -pi
            base = [(7, 1), (6, 2), (4, 4)]
            coef = [(jnp.where(row < 8 - k, pr[j:j + 1], 0.0), jnp.where(row < 8 - k, pi[j:j + 1], 0.0), 8 - k)
                    for j, k in base]
        else:
            base = [(0, 1), (1, 2), (3, 4)]
            coef = [(jnp.where(row >= k, pr[j:j + 1], 0.0), jnp.where(row >= k, pi[j:j + 1], 0.0), k)
                    for j, k in base]

        def step(i, carry, pr=pr, pi=pi, coef=coef, re_cols=re_cols, im_cols=im_cols):
            cr, ci = carry
            blk = (n_blocks - 1 - i) if reverse else i
            rows = pl.ds(pl.multiple_of(blk * 8, 8), 8)
            xr = buf[rows, re_cols]
            xi = buf[rows, im_cols]
            for kr, ki, shift in coef:
                sr = pltpu.roll(xr, shift, 0)
                si = pltpu.roll(xi, shift, 0)
                xr, xi = xr + kr * sr - ki * si, xi + kr * si + ki * sr
            xr, xi = xr + pr * cr - pi * ci, xi + pr * ci + pi * cr
            buf[rows, re_cols] = xr
            buf[rows, im_cols] = xi
            edge = slice(0, 1) if reverse else slice(7, 8)
            return xr[edge], xi[edge]

        cr, ci = lax.fori_loop(0, n_blocks, step, (carry_ref[0:1, re_cols], carry_ref[0:1, im_cols]))
        carry_ref[0:1, re_cols] = cr
        carry_ref[0:1, im_cols] = ci


_SUPER_GROUPS = 16
_SUPER_BLOCKS = [
    (slice(k * _SUPER_GROUPS * SSM_GROUP, (k + 1) * _SUPER_GROUPS * SSM_GROUP),
     [slice(half + k * _SUPER_GROUPS * SSM_STATE, half + (k + 1) * _SUPER_GROUPS * SSM_STATE) for half in (0, NS)])
    for k in range(SSM_GROUPS // _SUPER_GROUPS)]


def _ssm_fwd(u, bd, cdt, d_skip, pw):
    L = u.shape[0]
    tc = min(SSM_FWD_CHUNK, L)

    def body(u_ref, bd_ref, cdt_ref, dsk_ref, pw_ref, y_ref, s_ref, carry_ref):
        @pl.when(pl.program_id(0) == 0)
        def _():
            carry_ref[...] = jnp.zeros_like(carry_ref)

        uv = u_ref[...]
        u16 = uv.astype(BF16)
        for ch, states in _SUPER_BLOCKS:
            for st in states:
                s_ref[:, st] = _dot(u16[:, ch], bd_ref[ch, st])
        _scan_blocks(s_ref, pw_ref, carry_ref, tc // 8, reverse=False)
        for ch, states in _SUPER_BLOCKS:
            y_ref[:, ch] = (sum(_dot_nt(s_ref[:, st].astype(BF16), cdt_ref[ch, st]) for st in states)
                            + dsk_ref[:, ch] * uv[:, ch])

    return pl.pallas_call(
        body, name="s5_fwd", grid=(L // tc,),
        in_specs=[_rows(tc, SSM_WIDTH), _whole(), _whole(), _whole(), _whole()],
        out_specs=[_rows(tc, SSM_WIDTH), _rows(tc, 2 * NS)],
        out_shape=[jax.ShapeDtypeStruct((L, SSM_WIDTH), F32), jax.ShapeDtypeStruct((L, 2 * NS), F32)],
        scratch_shapes=[pltpu.VMEM((8, 2 * NS), F32)],
        compiler_params=_params(),
    )(u, bd, cdt, d_skip, pw)


def _ssm_bwd(dy, u, s, bd, cdt, d_skip, pwr):
    L = u.shape[0]
    tc = min(SSM_CHUNK, L)
    nc = L // tc
    blocks = tc // 8

    def body(dy_ref, u_ref, s_ref, sprev_ref, bd_ref, cdt_ref, dsk_ref, pwr_ref,
             du_ref, ddsk_ref, dbd_ref, dcdt_ref, dab_ref, g_ref, sx_ref, carry_ref):
        i = pl.program_id(0)

        @pl.when(i == 0)
        def _():
            carry_ref[...] = jnp.zeros_like(carry_ref)
            ddsk_ref[...] = jnp.zeros_like(ddsk_ref)
            dbd_ref[...] = jnp.zeros_like(dbd_ref)
            dcdt_ref[...] = jnp.zeros_like(dcdt_ref)
            dab_ref[...] = jnp.zeros_like(dab_ref)

        dyv = dy_ref[...]
        uv = u_ref[...]
        dy16 = dyv.astype(BF16)
        u16 = uv.astype(BF16)
        for ch, states in _SUPER_BLOCKS:
            for st in states:
                g_ref[:, st] = _dot(dy16[:, ch], cdt_ref[ch, st])
        _scan_blocks(g_ref, pwr_ref, carry_ref, blocks, reverse=True)
        ddsk_ref[...] += jnp.sum(dyv * uv, axis=0, keepdims=True)
        for ch, states in _SUPER_BLOCKS:
            du = dsk_ref[:, ch] * dyv[:, ch]
            for st in states:
                g16 = g_ref[:, st].astype(BF16)
                du = du + _dot_nt(g16, bd_ref[ch, st])
                dbd_ref[ch, st] += _dot_tn(u16[:, ch], g16)
                dcdt_ref[ch, st] += _dot_tn(dy16[:, ch], s_ref[:, st].astype(BF16))
            du_ref[:, ch] = du

        sx_ref[pl.ds(8, tc), :] = s_ref[...]
        sx_ref[pl.ds(0, 8), :] = jnp.where(i == nc - 1, 0.0, sprev_ref[...])
        row = lax.broadcasted_iota(jnp.int32, (8, SCAN_LANES), 0)
        for lc in range(NS // SCAN_LANES):
            re_cols = pl.ds(lc * SCAN_LANES, SCAN_LANES)
            im_cols = pl.ds(NS + lc * SCAN_LANES, SCAN_LANES)

            def step(b, acc, re_cols=re_cols, im_cols=im_cols):
                ar, ai = acc
                off = pl.multiple_of(b * 8, 8)
                gr = g_ref[pl.ds(off, 8), re_cols]
                gi = g_ref[pl.ds(off, 8), im_cols]
                before = pl.ds(off, 8)
                here = pl.ds(off + 8, 8)
                sr = jnp.where(row == 0, sx_ref[before, re_cols][7:8], pltpu.roll(sx_ref[here, re_cols], 1, 0))
                si = jnp.where(row == 0, sx_ref[before, im_cols][7:8], pltpu.roll(sx_ref[here, im_cols], 1, 0))
                return ar + gr * sr + gi * si, ai + gi * sr - gr * si

            zero = jnp.zeros((8, SCAN_LANES), F32)
            ar, ai = lax.fori_loop(0, blocks, step, (zero, zero))
            dab_ref[0, :, re_cols] += ar
            dab_ref[1, :, re_cols] += ai

    rev = lambda i: (nc - 1 - i, 0)
    sprev = pl.BlockSpec((8, 2 * NS), lambda i: (jnp.maximum((nc - 1 - i) * blocks - 1, 0), 0))
    return pl.pallas_call(
        body, name="s5_bwd", grid=(nc,),
        in_specs=[pl.BlockSpec((tc, SSM_WIDTH), rev), pl.BlockSpec((tc, SSM_WIDTH), rev),
                  pl.BlockSpec((tc, 2 * NS), rev), sprev, _whole(), _whole(), _whole(), _whole()],
        out_specs=[pl.BlockSpec((tc, SSM_WIDTH), rev), _whole(), _whole(), _whole(), _whole()],
        out_shape=[jax.ShapeDtypeStruct((L, SSM_WIDTH), F32), jax.ShapeDtypeStruct((1, SSM_WIDTH), F32),
                   jax.ShapeDtypeStruct((SSM_WIDTH, 2 * NS), F32), jax.ShapeDtypeStruct((SSM_WIDTH, 2 * NS), F32),
                   jax.ShapeDtypeStruct((2, 8, NS), F32)],
        scratch_shapes=[pltpu.VMEM((tc, 2 * NS), F32), pltpu.VMEM((tc + 8, 2 * NS), F32), pltpu.VMEM((8, 2 * NS), F32)],
        compiler_params=_params(),
    )(dy, u, s, s, bd, cdt, d_skip, pwr)


def _branches(o_attn, y, gates, w_ab, w_glu, w_sb):
    ya = _dot(o_attn.astype(BF16), w_ab[...])
    gel = _gelu(y)
    glu = _dot(gel.astype(BF16), w_glu[...])
    p = glu[:, :SSM_WIDTH]
    sg = _sigmoid(glu[:, SSM_WIDTH:])
    ys2 = p * sg
    ysb = _dot(ys2.astype(BF16), w_sb[...])
    ga = gates[:, :D_MODEL]
    gs = gates[:, D_MODEL:]
    return ya, gel, p, sg, ys2, ysb, ga, gs


def _mix_out_fwd(x1, o_g, lse_g, y, gates, w_ab, w_glu, w_sb, w_out):
    L = x1.shape[0]
    tm = min(ROW_TILE, L)

    def body(x_ref, o0, o1, o2, l0, l1, l2, y_ref, gate_ref, wab_ref, wglu_ref, wsb_ref, wout_ref,
             x2_ref, oat_ref, lse0, lse1, lse2, scr):
        la, lb, lc = (_from_residues(ref, scr, d) for ref, d in zip((l0, l1, l2), DILATIONS))
        m = jnp.maximum(jnp.maximum(la, lb), lc)
        ea, eb, ec = jnp.exp(la - m), jnp.exp(lb - m), jnp.exp(lc - m)
        tot = ea + eb + ec
        oa, ob, oc = (_from_residues(ref, scr, d) for ref, d in zip((o0, o1, o2), DILATIONS))
        o_attn = (ea * oa + eb * ob + ec * oc) / tot
        oat_ref[...] = o_attn
        lse = m + jnp.log(tot)
        for ref, d in zip((lse0, lse1, lse2), DILATIONS):
            _to_residues(lse, ref, scr, d)
        ya, _, _, _, _, ysb, ga, gs = _branches(o_attn, y_ref[...], gate_ref[...], wab_ref, wglu_ref, wsb_ref)
        mix = ga * ya + gs * ysb
        x2_ref[...] = x_ref[...] + _dot(mix.astype(BF16), wout_ref[...])

    res = [_residue_spec(d, tm) for d in DILATIONS]
    return pl.pallas_call(
        body, name="mix_out_fwd", grid=(L // tm,),
        in_specs=[_rows(tm, D_MODEL)] + res * 2 + [_rows(tm, SSM_WIDTH), _rows(tm, 2 * D_MODEL)] + [_whole()] * 4,
        out_specs=[_rows(tm, D_MODEL), _rows(tm, GROUP_WIDTH)] + res,
        out_shape=[jax.ShapeDtypeStruct((L, D_MODEL), F32), jax.ShapeDtypeStruct((L, GROUP_WIDTH), F32)]
        + [_residue_shape(d, L, F32) for d in DILATIONS],
        scratch_shapes=[_residue_scratch(tm)],
        compiler_params=_params(),
    )(x1, *o_g, *lse_g, y, gates, w_ab, w_glu, w_sb, w_out)


def _mix_out_bwd(dx2, o_attn, y, gates, w_ab, w_glu, w_sb, w_out, head_sum):
    L = dx2.shape[0]
    tm = min(ROW_TILE, L)

    def body(dx_ref, oat_ref, y_ref, gate_ref, wab_ref, wglu_ref, wsb_ref, wout_ref, hs_ref,
             do0, do1, do2, dl0, dl1, dl2, dy_ref, dgp_ref, mix_ref, dya_ref, dys_ref, ys2_ref, gel_ref, dglu_ref,
             dgb_ref, scr):
        i = pl.program_id(0)
        o_attn = oat_ref[...]
        yv = y_ref[...]
        ya, gel, p, sg, ys2, ysb, ga, gs = _branches(o_attn, yv, gate_ref[...], wab_ref, wglu_ref, wsb_ref)
        mix_ref[...] = (ga * ya + gs * ysb).astype(BF16)
        ys2_ref[...] = ys2.astype(BF16)
        gel_ref[...] = gel.astype(BF16)
        dmix = _dot_nt(dx_ref[...].astype(BF16), wout_ref[...])
        dgp = jnp.concatenate([dmix * ya * ga * (1.0 - ga), dmix * ysb * gs * (1.0 - gs)], axis=1)
        dgp_ref[...] = dgp.astype(BF16)

        @pl.when(i == 0)
        def _():
            dgb_ref[...] = jnp.zeros_like(dgb_ref)

        dgb_ref[...] += jnp.sum(dgp, axis=0, keepdims=True)
        dya = (dmix * ga).astype(BF16)
        dys = (dmix * gs).astype(BF16)
        dya_ref[...] = dya
        dys_ref[...] = dys
        d_o = _dot_nt(dya, wab_ref[...])
        delta = _dot_exact(d_o * o_attn, hs_ref[...])
        for do_ref, dl_ref, d in zip((do0, do1, do2), (dl0, dl1, dl2), DILATIONS):
            _to_residues(d_o, do_ref, scr, d)
            _to_residues(delta, dl_ref, scr, d)
        dys2 = _dot_nt(dys, wsb_ref[...])
        dglu = jnp.concatenate([dys2 * sg, dys2 * p * sg * (1.0 - sg)], axis=1).astype(BF16)
        dglu_ref[...] = dglu
        dy_ref[...] = _dot_nt(dglu, wglu_ref[...]) * _gelu_grad(yv)

    grp = _rows(tm, GROUP_WIDTH)
    wide = _rows(tm, D_MODEL)
    half = _rows(tm, SSM_WIDTH)
    res = [_residue_spec(d, tm) for d in DILATIONS]
    sds = jax.ShapeDtypeStruct
    return pl.pallas_call(
        body, name="mix_out_bwd", grid=(L // tm,),
        in_specs=[wide, grp, half, _rows(tm, 2 * D_MODEL)] + [_whole()] * 5,
        out_specs=res + res + [half, _rows(tm, 2 * D_MODEL), wide, wide, wide, half, half, wide, _acc_row(2 * D_MODEL)],
        out_shape=[_residue_shape(d, L, BF16) for d in DILATIONS] + [_residue_shape(d, L, F32) for d in DILATIONS]
        + [sds((L, SSM_WIDTH), F32),
           sds((L, 2 * D_MODEL), BF16), sds((L, D_MODEL), BF16), sds((L, D_MODEL), BF16),
           sds((L, D_MODEL), BF16), sds((L, SSM_WIDTH), BF16), sds((L, SSM_WIDTH), BF16),
           sds((L, D_MODEL), BF16), sds((1, 2 * D_MODEL), F32)],
        scratch_shapes=[_residue_scratch(tm)],
        compiler_params=_params(),
    )(dx2, o_attn, y, gates, w_ab, w_glu, w_sb, w_out, head_sum)


def _adamw(w, g, m, v, name):
    R, C = w.shape
    tr = _row_tile(R, max(8, ADAMW_BLOCK_BYTES // (4 * C)))

    def body(w_ref, g_ref, m_ref, v_ref, d_ref, mo_ref, vo_ref):
        gv = g_ref[...]
        mn = ADAM_B1 * m_ref[...] + (1.0 - ADAM_B1) * gv
        vn = ADAM_B2 * v_ref[...] + (1.0 - ADAM_B2) * (gv * gv)
        m_hat = mn / (1.0 - ADAM_B1 ** ADAM_STEP)
        v_hat = vn / (1.0 - ADAM_B2 ** ADAM_STEP)
        d_ref[...] = -ADAM_LR * (m_hat / (jnp.sqrt(v_hat) + ADAM_EPS) + ADAM_WD * w_ref[...])
        mo_ref[...] = mn
        vo_ref[...] = vn

    blk = pl.BlockSpec((tr, C), lambda i: (i, 0))
    return pl.pallas_call(
        body, name=name, grid=(R // tr,),
        in_specs=[blk] * 4, out_specs=[blk] * 3,
        out_shape=[jax.ShapeDtypeStruct((R, C), F32)] * 3,
        compiler_params=_params(),
    )(w, g, m, v)


def _sum_chips_into_half(u, t, name):
    S, H, C = u.shape
    tr = _row_tile(H, 512)
    hb = H // tr

    def body(s_ref, t_ref, a_ref, b_ref, c_ref, o_ref):
        me = s_ref[1]
        others = (a_ref[...], b_ref[...], c_ref[...])
        acc = None
        for chip in range(S):
            below = others[min(chip, S - 2)]
            above = others[max(chip - 1, 0)]
            term = jnp.where(me == chip, t_ref[...], jnp.where(me > chip, below, above)).astype(F32)
            acc = term if acc is None else acc + term
        o_ref[...] = acc

    x, y, c = lax.axis_index("x"), lax.axis_index("y"), lax.axis_index("c")
    me = 2 * x + y
    scalars = jnp.stack([c, me] + [j + (j >= me).astype(jnp.int32) for j in range(S - 1)]).astype(jnp.int32)
    blk = (None, tr, C)
    return pl.pallas_call(
        body, name=name,
        grid_spec=pltpu.PrefetchScalarGridSpec(
            num_scalar_prefetch=1, grid=(hb,),
            in_specs=[pl.BlockSpec(blk, lambda i, s: (s[1], i, 0))]
            + [pl.BlockSpec(blk, functools.partial(lambda j, i, s: (s[2 + j], i, 0), j)) for j in range(S - 1)],
            out_specs=pl.BlockSpec((tr, C), lambda i, s: (s[0] * hb + i, 0))),
        out_shape=jax.ShapeDtypeStruct((2 * H, C), F32),
        compiler_params=_params(),
    )(scalars, t, u, u, u)


def _add_halves(g, r1, name):
    S, R, C = g.shape
    H = R // 2
    tr = _row_tile(H, 512)
    hb = H // tr

    def body(c_ref, g_ref, r_ref, o_ref):
        o_ref[...] = (g_ref[...] + r_ref[...]).astype(BF16)

    core = lax.axis_index("c").astype(jnp.int32).reshape(1)
    return pl.pallas_call(
        body, name=name,
        grid_spec=pltpu.PrefetchScalarGridSpec(
            num_scalar_prefetch=1, grid=(S, hb),
            in_specs=[pl.BlockSpec((None, tr, C), lambda j, i, c_ref: (j, c_ref[0] * hb + i, 0)),
                      pl.BlockSpec((None, tr, C), lambda j, i, c_ref: (j, i, 0))],
            out_specs=pl.BlockSpec((None, tr, C), lambda j, i, c_ref: (j, i, 0))),
        out_shape=jax.ShapeDtypeStruct((S, H, C), BF16),
        compiler_params=_params(),
    )(core, g, r1)


_ANY = pl.BlockSpec(memory_space=pl.ANY)


def _place():
    x, y, c = lax.axis_index("x"), lax.axis_index("y"), lax.axis_index("c")
    chips = [(1 - x, y), (x, 1 - y), (1 - x, 1 - y)]
    return x, y, c, chips


def _remote(src, dst, send_sems, recv_sems, k, device):
    return pltpu.make_async_remote_copy(src_ref=src, dst_ref=dst, send_sem=send_sems.at[k], recv_sem=recv_sems.at[k],
                                        device_id=device, device_id_type=MESH)


def _gather_parts(shapes, w_refs, out_refs, send_sems, recv_sems):
    n = len(shapes)
    x, y, c, chips = _place()
    me = 2 * x + y
    sibling = (x, y, 1 - c)

    def half(k, chip_idx, core):
        H = shapes[k][0] // 2
        return out_refs[k].at[chip_idx, pl.ds(core * H, H), :]

    mine = [_remote(w_refs[k], out_refs[k].at[me], send_sems, recv_sems, 6 * n + k, sibling) for k in range(n)]
    first = []
    for k in range(n):
        H = shapes[k][0] // 2
        for j, (cx, cy) in enumerate(chips):
            first.append(_remote(w_refs[k].at[pl.ds(c * H, H), :], half(k, me, c), send_sems, recv_sems,
                                 3 * k + j, (cx, cy, c)))

    def start():
        for cp in mine + first:
            cp.start()

    def finish():
        passed = []
        for k in range(n):
            for j, (cx, cy) in enumerate(chips):
                landed = half(k, 2 * cx + cy, c)
                _remote(landed, landed, send_sems, recv_sems, 3 * k + j, (cx, cy, c)).wait_recv()
                fwd = _remote(landed, landed, send_sems, recv_sems, 3 * n + 3 * k + j, sibling)
                fwd.start()
                passed.append(fwd)
        for k in range(n):
            for j, (cx, cy) in enumerate(chips):
                other = half(k, 2 * cx + cy, 1 - c)
                _remote(other, other, send_sems, recv_sems, 3 * n + 3 * k + j, sibling).wait_recv()
        for cp in mine:
            cp.wait_recv()
        for cp in first + passed + mine:
            cp.wait_send()

    return start, finish


def _gather_weights(shards, name):
    n = len(shards)

    def body(*refs):
        x, y, c, chips = _place()
        _handshake([(x, y, 1 - c)] + [(cx, cy, c) for cx, cy in chips])
        start, finish = _gather_parts([w.shape for w in shards], refs[:n], refs[n:2 * n], *refs[2 * n:2 * n + 2])
        start()
        finish()

    return _sequenced(body, name, shards, [jax.ShapeDtypeStruct((N_SHARD,) + w.shape, w.dtype) for w in shards],
                      7 * n, COLLECTIVE_IDS["gather"])


def _handshake(peers):
    barrier = pltpu.get_barrier_semaphore()
    for peer in peers:
        pl.semaphore_signal(barrier, inc=1, device_id=peer, device_id_type=MESH)
    pl.semaphore_wait(barrier, len(peers))


def _sequenced(body, name, ins, out_shapes, n_sems, collective_id):
    return pl.kernel(
        body, out_type=list(out_shapes), mesh=plsc.ScalarSubcoreMesh(axis_name="sequencer", num_cores=1), name=name,
        scratch_types=(pltpu.SemaphoreType.DMA((n_sems,)), pltpu.SemaphoreType.DMA((n_sems,))),
        compiler_params=pltpu.CompilerParams(collective_id=collective_id))(*ins)


def _swap_halves(gs, name, collective_id):
    n = len(gs)

    def body(*refs):
        g_refs, out_refs = refs[:n], refs[n:2 * n]
        send_sems, recv_sems = refs[2 * n:]
        x, y, c, _ = _place()
        _handshake([(x, y, 1 - c)])
        cps = []
        for k in range(n):
            H = gs[k].shape[1] // 2
            cp = _remote(g_refs[k].at[:, pl.ds((1 - c) * H, H), :], out_refs[k], send_sems, recv_sems, k, (x, y, 1 - c))
            cp.start()
            cps.append(cp)
        for cp in cps:
            cp.wait()

    return _sequenced(body, name, gs, [jax.ShapeDtypeStruct((g.shape[0], g.shape[1] // 2, g.shape[2]), g.dtype)
                                       for g in gs], n, collective_id)


def _exchange_chips(ts, name, collective_id):
    n = len(ts)

    def body(*refs):
        t_refs, out_refs = refs[:n], refs[n:2 * n]
        send_sems, recv_sems = refs[2 * n:]
        x, y, c, chips = _place()
        me = 2 * x + y
        _handshake([(cx, cy, c) for cx, cy in chips])
        sent = []
        for k in range(n):
            for j, (cx, cy) in enumerate(chips):
                cp = _remote(t_refs[k].at[2 * cx + cy], out_refs[k].at[me], send_sems, recv_sems, 3 * k + j, (cx, cy, c))
                cp.start()
                sent.append(cp)
        for k in range(n):
            for j, (cx, cy) in enumerate(chips):
                slot = out_refs[k].at[2 * cx + cy]
                _remote(slot, slot, send_sems, recv_sems, 3 * k + j, (cx, cy, c)).wait_recv()
        for cp in sent:
            cp.wait_send()

    return _sequenced(body, name, ts, [jax.ShapeDtypeStruct(t.shape, t.dtype) for t in ts], 3 * n, collective_id)


def _join_halves(fs, name):
    n = len(fs)

    def body(*refs):
        out_refs = refs[n:2 * n]
        send_sems, recv_sems, _ = refs[2 * n:]
        x, y, c, _ = _place()
        sent = []
        for k in range(n):
            H = fs[k].shape[0] // 2
            here = out_refs[k].at[pl.ds(c * H, H), :]
            cp = _remote(here, here, send_sems, recv_sems, k, (x, y, 1 - c))
            cp.start()
            sent.append(cp)
        for k in range(n):
            H = fs[k].shape[0] // 2
            other = out_refs[k].at[pl.ds((1 - c) * H, H), :]
            _remote(other, other, send_sems, recv_sems, k, (x, y, 1 - c)).wait_recv()
        for cp in sent:
            cp.wait_send()

    return pl.pallas_call(
        body, name=name,
        in_specs=[_ANY] * n, out_specs=[_ANY] * n,
        out_shape=[jax.ShapeDtypeStruct(f.shape, f.dtype) for f in fs],
        input_output_aliases={k: k for k in range(n)},
        scratch_shapes=[pltpu.SemaphoreType.DMA((n,)), pltpu.SemaphoreType.DMA((n,)), pltpu.SemaphoreType.DMA((1,))],
    )(*fs)


def _gather_small(v):
    R, C = v.shape

    def body(v_ref, out_ref, send_sems, recv_sems):
        x, y, c, _ = _place()
        me = 4 * x + 2 * y + c
        flips = [(fx, fy, fc) for fx in (0, 1) for fy in (0, 1) for fc in (0, 1)][1:]
        peers = [((1 - x) if fx else x, (1 - y) if fy else y, (1 - c) if fc else c) for fx, fy, fc in flips]
        _handshake(peers)
        sent = []
        for j, peer in enumerate(peers):
            cp = _remote(v_ref, out_ref.at[me], send_sems, recv_sems, j, peer)
            cp.start()
            sent.append(cp)
        for j, peer in enumerate(peers):
            slot = out_ref.at[4 * peer[0] + 2 * peer[1] + peer[2]]
            _remote(slot, slot, send_sems, recv_sems, j, peer).wait_recv()
        for cp in sent:
            cp.wait_send()

    return _sequenced(body, "gather_small", [v], [jax.ShapeDtypeStruct((8, R, C), F32)], 7,
                      COLLECTIVE_IDS["gather_small"])[0]


def _sum_devices(x, own, name):
    S, R, C = x.shape
    tr = _row_tile(R, 2048)

    def body(s_ref, x_ref, own_ref, o_ref):
        me = s_ref[0]
        acc = None
        for k in range(S):
            term = jnp.where(me == k, own_ref[...], x_ref[k])
            acc = term if acc is None else acc + term
        o_ref[...] = acc

    x_, y_, c_ = lax.axis_index("x"), lax.axis_index("y"), lax.axis_index("c")
    me = (4 * x_ + 2 * y_ + c_).astype(jnp.int32).reshape(1)
    return pl.pallas_call(
        body, name=name,
        grid_spec=pltpu.PrefetchScalarGridSpec(
            num_scalar_prefetch=1, grid=(R // tr,),
            in_specs=[pl.BlockSpec((S, tr, C), lambda i, s: (0, i, 0)), pl.BlockSpec((tr, C), lambda i, s: (i, 0))],
            out_specs=pl.BlockSpec((tr, C), lambda i, s: (i, 0))),
        out_shape=jax.ShapeDtypeStruct((R, C), F32),
        compiler_params=_params(),
    )(me, x, own)


def _after(earlier, arrays):
    return lax.optimization_barrier((earlier, arrays))


def _reduce_swap(gs, tag, earlier):
    gs = _after(earlier, gs)[1]
    return gs, _swap_halves(gs, "reduce_swap_" + tag, COLLECTIVE_IDS["swap_" + tag])


def _reduce_exchange(gs, r1, names, tag, later_than):
    r1 = _after(later_than, r1)[1]
    ts = [_add_halves(g, r, "reduce_add_cores_" + nm) for g, r, nm in zip(gs, r1, names)]
    us = _exchange_chips(ts, "reduce_exchange_" + tag, COLLECTIVE_IDS["exchange_" + tag])
    return us, ts


def _reduce_finish(us, ts, names, tag):
    fs = [_sum_chips_into_half(u, t, "reduce_add_chips_" + nm) for u, t, nm in zip(us, ts, names)]
    return _join_halves(fs, "reduce_join_" + tag)


BIG = ["ffn1_w_gate", "ffn1_w_up", "ffn1_w_down", "w_in", "ssm_w_glu", "w_attn_branch", "w_ssm_branch",
       "w_out", "ffn2_w_gate", "ffn2_w_up", "ffn2_w_down"]
SMALL = ["ffn1_norm", "mix_norm", "gate_bias", "rel_bias_table", "ssm_a_re", "ssm_a_im", "ssm_log_dt",
         "ssm_b_re", "ssm_b_im", "ssm_c_re", "ssm_c_im", "ssm_d", "ffn2_norm", "final_norm"]
ORDER = ["ffn1_norm", "ffn1_w_gate", "ffn1_w_up", "ffn1_w_down", "mix_norm", "w_in", "gate_bias", "rel_bias_table",
         "ssm_a_re", "ssm_a_im", "ssm_log_dt", "ssm_b_re", "ssm_b_im", "ssm_c_re", "ssm_c_im", "ssm_d",
         "ssm_w_glu", "w_attn_branch", "w_ssm_branch", "w_out", "ffn2_norm", "ffn2_w_gate", "ffn2_w_up",
         "ffn2_w_down", "final_norm"]


_SMALL_TILE = 8 * LANES


def _pack_small(arrays):
    rows = []
    for a in arrays:
        flat = a.reshape(-1).astype(F32)
        rows.append(jnp.pad(flat, (0, (-flat.shape[0]) % _SMALL_TILE)).reshape(-1, LANES))
    return jnp.concatenate(rows, axis=0)


def _unpack_small(packed, shapes):
    out, r0 = [], 0
    for shp in shapes:
        n = math.prod(shp)
        rows = 8 * -(-n // _SMALL_TILE)
        out.append(packed[r0:r0 + rows].reshape(-1)[:n].reshape(shp))
        r0 += rows
    return out


def _split_cols(g):
    K, N = g.shape
    return g.reshape(K, N_SHARD, N // N_SHARD).transpose(1, 0, 2)


def _join_cols(w):
    S, K, n = w.shape
    return w.transpose(1, 0, 2).reshape(K, S * n)


COL_SHARDED = ("ssm_w_glu", "w_attn_branch", "w_ssm_branch")
TRANSPOSED = ("ffn1_w_gate", "ffn1_w_up", "ffn2_w_gate", "ffn2_w_up", "w_in")


def _shard_2d(name, arr):
    two_d = arr.reshape(arr.shape[-2:])
    return two_d.T if name in TRANSPOSED else two_d


def _shard_nd(name, two_d, shape):
    return (two_d.T if name in TRANSPOSED else two_d).reshape(shape)


class _GradSync:
    def __init__(self, weights, moms, vels):
        self.weights, self.moms, self.vels = weights, moms, vels
        self.grads, self.delta, self.new_m, self.new_v = {}, {}, {}, {}
        self.loss = None
        self._earlier = []
        self._swapped = {}
        self._exchanged = {}

    def swap(self, tag, gw, later_than=()):
        gs = []
        for n in REDUCE_GROUPS[tag]:
            g = gw[n]
            if n in COL_SHARDED:
                g = _split_cols(g)
            elif n in ("w_out", "w_in"):
                g = g.reshape(N_SHARD, g.shape[0] // N_SHARD, g.shape[1])
            gs.append(g)
        self._swapped[tag] = _reduce_swap(gs, tag, list(self._earlier) + list(later_than))
        self._earlier = self._swapped[tag][1]

    def exchange(self, tag, later_than):
        gs, r1 = self._swapped[tag]
        us, ts = _reduce_exchange(gs, r1, REDUCE_GROUPS[tag], tag, later_than)
        self._exchanged[tag] = (us, ts)
        self._earlier = us

    def small_ready(self, gs, loss_blk, later_than=()):
        _, (mine,) = _after(list(self._earlier) + list(later_than),
                            [_pack_small([gs[n] for n in SMALL] + [loss_blk[0:1, :]])])
        others = _gather_small(mine)
        self._exchanged["small"] = (others, mine)
        self._earlier = [others]

    def finish(self, tag):
        made = []
        if tag == "small":
            others, mine = self._exchanged[tag]
            shapes = [self.weights[n].shape for n in SMALL]
            total = _unpack_small(_sum_devices(others, mine, "sum_small"), shapes + [(128,)])
            self.loss = total[-1][0]
            self.grads.update(zip(SMALL, total[:-1]))
            packed = [_pack_small([src[n] for n in SMALL]) for src in (self.weights, self.grads, self.moms, self.vels)]
            for dst, res in zip((self.delta, self.new_m, self.new_v), _adamw(*packed, "adamw_small")):
                dst.update(zip(SMALL, _unpack_small(res, shapes)))
            for n in SMALL:
                made += [self.grads[n], self.delta[n], self.new_m[n], self.new_v[n]]
            return made + [self.loss]
        names = REDUCE_GROUPS[tag]
        us, ts = self._exchanged[tag]
        for n, g in zip(names, _reduce_finish(us, ts, names, tag)):
            shp = self.weights[n].shape
            d, m, v = _adamw(_shard_2d(n, self.weights[n]), g, _shard_2d(n, self.moms[n]), _shard_2d(n, self.vels[n]),
                             "adamw_" + n)
            self.grads[n], self.delta[n] = _shard_nd(n, g, shp), _shard_nd(n, d, shp)
            self.new_m[n], self.new_v[n] = _shard_nd(n, m, shp), _shard_nd(n, v, shp)
            made += [self.grads[n], self.delta[n], self.new_m[n], self.new_v[n]]
        return made

    def finish_all(self):
        self.exchange("ffn1", later_than=self.finish("ffn2"))
        for tag in ("mixer", "w_in", "small", "ffn1"):
            self.finish(tag)


def _local_step(x, target, w, later, small, sync):
    L = x.shape[0]
    row = lambda v: v.reshape(1, -1)

    a_re, a_im = small["ssm_a_re"].reshape(1, NS), small["ssm_a_im"].reshape(1, NS)
    ldt = jnp.repeat(small["ssm_log_dt"].reshape(SSM_GROUPS), SSM_STATE).reshape(1, NS)
    to_cn = lambda b: b.reshape(SSM_GROUPS, SSM_STATE, SSM_GROUP).transpose(2, 0, 1).reshape(SSM_GROUP, NS)
    c_to_cn = lambda c: c.reshape(SSM_GROUPS, SSM_GROUP, SSM_STATE).transpose(1, 0, 2).reshape(SSM_GROUP, NS)
    b_re, b_im = to_cn(small["ssm_b_re"]), to_cn(small["ssm_b_im"])
    c_re, c_im = c_to_cn(small["ssm_c_re"]), c_to_cn(small["ssm_c_im"])
    d_skip = row(small["ssm_d"])
    pw, pwr, bd, cdt = _disc_fwd(a_re, a_im, ldt, b_re, b_im, c_re, c_im)

    onehot = _bucket_onehot()
    table_t = small["rel_bias_table"].T.reshape(3, HEADS_PER_GROUP, N_BUCKETS)
    table_t = jnp.pad(table_t, ((0, 0), (0, 8 - HEADS_PER_GROUP), (0, 0)))
    bias = _bias_expand(table_t, onehot)[:, :, :HEADS_PER_GROUP].reshape(
        3, 2, HEADS_PER_GROUP, ATTN_BLOCK, 2 * ATTN_BLOCK)

    n1, nm, n2, nf = row(small["ffn1_norm"]), row(small["mix_norm"]), row(small["ffn2_norm"]), row(small["final_norm"])
    gate_bias = row(small["gate_bias"])

    x1, a1, b1, *later_full = _ffn_fwd(x, n1, w["ffn1_w_gate"], w["ffn1_w_up"], w["ffn1_w_down"], "ffn1_fwd",
                                       carried=list(later.values()))
    w = dict(w, **dict(zip(later, later_full)))
    for n in COL_SHARDED:
        w[n] = _join_cols(w[n])
    w["w_out"] = w["w_out"].reshape(D_MODEL, D_MODEL)
    w["w_in"] = w["w_in"].reshape(IN_WIDTH, D_MODEL)
    *qkv, u, gates = _mix_in_fwd(x1, nm, w["w_in"], gate_bias)
    q, k, v = qkv[0:3], qkv[3:6], qkv[6:9]
    o_g, lse_g = [], []
    for grp in range(3):
        o, lse = _attn_fwd(q[grp], k[grp], v[grp], bias[grp], f"attn_fwd_{grp}")
        o_g.append(o)
        lse_g.append(lse)
    y, s = _ssm_fwd(u, bd, cdt, d_skip, pw)
    x2, o_attn, *lse_tot = _mix_out_fwd(x1, o_g, lse_g, y, gates, w["w_attn_branch"], w["ssm_w_glu"],
                                        w["w_ssm_branch"], w["w_out"])
    x3, a2, b2 = _ffn_fwd(x2, n2, w["ffn2_w_gate"], w["ffn2_w_up"], w["ffn2_w_down"], "ffn2_fwd")
    loss_blk, dx3, d_nf = _loss_fwd_bwd(x3, nf, target)

    gw, gs = {}, {}
    gs["final_norm"] = d_nf

    dx2, da, db, sact, h, d_out, gs["ffn2_norm"] = _ffn_bwd(dx3, x2, n2, a2, b2, w["ffn2_w_gate"], w["ffn2_w_up"],
                                                            w["ffn2_w_down"], "ffn2_bwd")
    gw["ffn2_w_gate"] = _matmul_tn(da, h[None], "ffn2_dw_gate")
    gw["ffn2_w_up"] = _matmul_tn(db, h[None], "ffn2_dw_up")
    gw["ffn2_w_down"] = _matmul_tn(sact, d_out[None], "ffn2_dw_down")
    sync.swap("ffn2", gw)

    head_sum = (jnp.arange(GROUP_WIDTH)[:, None] // HEAD_DIM == jnp.arange(GROUP_WIDTH)[None, :] // HEAD_DIM).astype(F32)
    (*d_o_delta, dy, dgp, mix, dya, dys, ys2, gel, dglu, gs["gate_bias"]) = _mix_out_bwd(
        dx2, o_attn, y, gates, w["w_attn_branch"], w["ssm_w_glu"], w["w_ssm_branch"], w["w_out"], head_sum)
    sync.exchange("ffn2", later_than=[dy])
    d_o, delta = d_o_delta[0:3], d_o_delta[3:6]
    gw["w_out"] = _matmul_tn(mix[None], dx2[None], "dw_out")[0]
    gw["w_attn_branch"] = _matmul_tn(o_attn[None], dya[None], "dw_attn_branch")[0]
    gw["w_ssm_branch"] = _matmul_tn(ys2[None], dys[None], "dw_ssm_branch")[0]
    gw["ssm_w_glu"] = _matmul_tn(gel[None], dglu[None], "dw_glu")[0]

    dqs, dks, dvs, dsums = [], [], [], []
    for grp in range(3):
        dq, dk, dv, dsum = _attn_bwd(q[grp], k[grp], v[grp], d_o[grp], lse_tot[grp], delta[grp], bias[grp],
                                     f"attn_bwd_{grp}")
        dqs.append(dq)
        dks.append(dk)
        dvs.append(dv)
        dsums.append(dsum.reshape(HEADS_PER_GROUP, -1))
    dsum_all = jnp.pad(jnp.stack(dsums), ((0, 0), (0, 8 - HEADS_PER_GROUP), (0, 0)))
    d_table = _bias_reduce(dsum_all, onehot)[:, :HEADS_PER_GROUP]
    gs["rel_bias_table"] = d_table.reshape(3 * HEADS_PER_GROUP, N_BUCKETS).T

    du, gs["ssm_d"], d_bd, d_cdt, d_ab = _ssm_bwd(dy, u, s, bd, cdt, d_skip, pwr)
    sync.swap("mixer", gw, later_than=[du])
    sync.exchange("mixer", later_than=[dqs[2]])
    group_sum =(jnp.arange(NS)[:, None] // SSM_STATE == jnp.arange(128)[None, :]).astype(F32)
    d_are, d_aim, d_ldt, d_bre, d_bim, d_cre, d_cim = _disc_bwd(a_re, a_im, ldt, b_re, b_im, d_bd, d_cdt, d_ab, group_sum)
    gs["ssm_a_re"], gs["ssm_a_im"] = d_are, d_aim
    gs["ssm_log_dt"] = d_ldt[0, :SSM_GROUPS]
    from_cn = lambda t: t.reshape(SSM_GROUP, SSM_GROUPS, SSM_STATE).transpose(1, 2, 0)
    c_from_cn = lambda t: t.reshape(SSM_GROUP, SSM_GROUPS, SSM_STATE).transpose(1, 0, 2)
    gs["ssm_b_re"], gs["ssm_b_im"] = from_cn(d_bre), from_cn(d_bim)
    gs["ssm_c_re"], gs["ssm_c_im"] = c_from_cn(d_cre), c_from_cn(d_cim)

    dx1, hm, dz, gs["mix_norm"] = _mix_in_bwd(dx2, x1, nm, dqs + dks + dvs, du, dgp, w["w_in"])
    gw["w_in"] = _matmul_tn(dz[None], hm[None], "dw_in")[0]
    sync.swap("w_in", gw)

    dx0, da, db, sact, h, d_out, gs["ffn1_norm"] = _ffn_bwd(dx1, x, n1, a1, b1, w["ffn1_w_gate"], w["ffn1_w_up"],
                                                            w["ffn1_w_down"], "ffn1_bwd")
    sync.exchange("w_in", later_than=[dx0])
    gw["ffn1_w_gate"] = _matmul_tn(da, h[None], "ffn1_dw_gate")
    gw["ffn1_w_up"] = _matmul_tn(db, h[None], "ffn1_dw_up")
    sync.small_ready(gs, loss_blk, later_than=[gw["ffn1_w_up"]])
    gw["ffn1_w_down"] = _matmul_tn(sact, d_out[None], "ffn1_dw_down")
    sync.swap("ffn1", gw)
    return dx0


def kernel(x, ffn1_norm, ffn1_w_gate, ffn1_w_up, ffn1_w_down, mix_norm, w_in, gate_bias, rel_bias_table, ssm_a_re, ssm_a_im, ssm_log_dt, ssm_b_re, ssm_b_im, ssm_c_re, ssm_c_im, ssm_d, ssm_w_glu, w_attn_branch, w_ssm_branch, w_out, ffn2_norm, ffn2_w_gate, ffn2_w_up, ffn2_w_down, final_norm, loss_target, m_ffn1_norm, m_ffn1_w_gate, m_ffn1_w_up, m_ffn1_w_down, m_mix_norm, m_w_in, m_gate_bias, m_rel_bias_table, m_ssm_a_re, m_ssm_a_im, m_ssm_log_dt, m_ssm_b_re, m_ssm_b_im, m_ssm_c_re, m_ssm_c_im, m_ssm_d, m_ssm_w_glu, m_w_attn_branch, m_w_ssm_branch, m_w_out, m_ffn2_norm, m_ffn2_w_gate, m_ffn2_w_up, m_ffn2_w_down, m_final_norm, v_ffn1_norm, v_ffn1_w_gate, v_ffn1_w_up, v_ffn1_w_down, v_mix_norm, v_w_in, v_gate_bias, v_rel_bias_table, v_ssm_a_re, v_ssm_a_im, v_ssm_log_dt, v_ssm_b_re, v_ssm_b_im, v_ssm_c_re, v_ssm_c_im, v_ssm_d, v_ssm_w_glu, v_w_attn_branch, v_w_ssm_branch, v_w_out, v_ffn2_norm, v_ffn2_w_gate, v_ffn2_w_up, v_ffn2_w_down, v_final_norm):
    args = dict(locals())
    weights = {n: args[n] for n in ORDER}
    moms = {n: args["m_" + n] for n in ORDER}
    vels = {n: args["v_" + n] for n in ORDER}

    shard2d = {n: _shard_2d(n, weights[n]) for n in BIG}
    first, rest = BIG[:3], BIG[3:]
    full = dict(zip(first, _gather_weights([shard2d[n].astype(BF16) for n in first], "gather_ffn1_weights")))
    later = {n: shard2d[n].astype(BF16) for n in rest}

    small = {n: weights[n] for n in SMALL}
    sync = _GradSync(weights, moms, vels)
    grad_x = _local_step(x[0], loss_target[0], full, later, small, sync)
    sync.finish_all()
    return (sync.loss, grad_x[None], *[sync.grads[n] for n in ORDER], *[sync.delta[n] for n in ORDER],
            *[sync.new_m[n] for n in ORDER], *[sync.new_v[n] for n in ORDER])
```

```python
import functools
import math

import jax
import jax.numpy as jnp
from jax import lax
from jax.experimental import pallas as pl
from jax.experimental.pallas import tpu as pltpu
from jax.experimental.pallas import tpu_sc as plsc

F32 = jnp.float32
BF16 = jnp.bfloat16
MESH = pl.DeviceIdType.MESH

D_MODEL = 1024
D_FF = 2816
HEAD_DIM = 64
HEADS_PER_GROUP = 4
DILATIONS = (1, 4, 16)
WINDOW_STEPS = 128
ATTN_BLOCK = 128
ATTN_QB = 8
GROUP_WIDTH = HEADS_PER_GROUP * HEAD_DIM
ATTN_WIDTH = 3 * GROUP_WIDTH
N_BUCKETS = 32
MAX_DISTANCE = 2048
NEG_INF = -1e30
SSM_WIDTH = 512
SSM_GROUP = 16
SSM_GROUPS = 32
SSM_STATE = 64
NS = SSM_GROUPS * SSM_STATE
EPS = 1e-6
IN_WIDTH = 3 * ATTN_WIDTH + SSM_WIDTH + 2 * D_MODEL
Q_SCALE = HEAD_DIM ** -0.5
N_SHARD = 4
FF_SHARD = D_FF // N_SHARD
ADAM_LR, ADAM_B1, ADAM_B2, ADAM_EPS, ADAM_WD, ADAM_STEP = 0.001, 0.9, 0.999, 1e-08, 0.01, 10

LANES = 128
VMEM_LIMIT = 56 * 1024 * 1024
ROW_TILE = 512
FFN_BWD_TILE = 256
SSM_CHUNK = 256
SSM_FWD_CHUNK = 512
SCAN_LANES = 512
SCAN_UNROLL = 4
ADAMW_BLOCK_BYTES = 2 << 20
TN_VMEM_BUDGET = 40 * 1024 * 1024
REDUCE_GROUPS = {
    "ffn2": ["ffn2_w_gate", "ffn2_w_up", "ffn2_w_down"],
    "mixer": ["w_out", "w_attn_branch", "w_ssm_branch", "ssm_w_glu"],
    "w_in": ["w_in"],
    "ffn1": ["ffn1_w_gate", "ffn1_w_up", "ffn1_w_down"],
}
COLLECTIVE_IDS = {name: i for i, name in enumerate(
    ["gather", "gather_small"] + [stage + "_" + tag for tag in REDUCE_GROUPS for stage in ("swap", "exchange")])}


def _params(**kw):
    return pltpu.CompilerParams(vmem_limit_bytes=VMEM_LIMIT, **kw)


def _dot(a, b):
    return jnp.dot(a, b, preferred_element_type=F32)


def _dot_nt(a, b):
    return lax.dot_general(a, b, (((1,), (1,)), ((), ())), preferred_element_type=F32)


def _dot_tn(a, b):
    return lax.dot_general(a, b, (((0,), (0,)), ((), ())), preferred_element_type=F32)


def _dot_exact(a, b):
    return jnp.dot(a, b, preferred_element_type=F32, precision=lax.Precision.HIGHEST)


def _dot_nt_exact(a, b):
    return lax.dot_general(a, b, (((1,), (1,)), ((), ())), preferred_element_type=F32,
                           precision=lax.Precision.HIGHEST)


def _rms(x):
    r = lax.rsqrt(jnp.mean(x * x, axis=-1, keepdims=True) + EPS)
    return r, x * r


def _rms_bwd(dh, g, r, xhat):
    dxh = dh * g
    return r * (dxh - xhat * jnp.mean(dxh * xhat, axis=-1, keepdims=True))


def _sigmoid(x):
    return 0.5 + 0.5 * jnp.tanh(0.5 * x)


_GELU_C = math.sqrt(2.0 / math.pi)


def _gelu(x):
    return 0.5 * x * (1.0 + jnp.tanh(_GELU_C * (x + 0.044715 * x * x * x)))


def _gelu_grad(x):
    t = jnp.tanh(_GELU_C * (x + 0.044715 * x * x * x))
    return 0.5 * (1.0 + t) + 0.5 * x * (1.0 - t * t) * _GELU_C * (1.0 + 3 * 0.044715 * x * x)


def _whole():
    return pl.BlockSpec(memory_space=pltpu.VMEM)


def _row_tile(rows, cap):
    if rows <= cap:
        return rows
    return max(t for t in range(8, cap + 1, 8) if rows % t == 0)


def _rows(tm, w):
    return pl.BlockSpec((tm, w), lambda i: (i, 0))


def _acc_row(w):
    return pl.BlockSpec((1, w), lambda i: (0, 0))


def _ffn_fwd(x, g, wg, wu, wd, name, carried=()):
    L = x.shape[0]
    tm = min(ROW_TILE, L)
    n = len(carried)
    steps = L // tm

    def body(x_ref, g_ref, wg_ref, wu_ref, wd_ref, *refs):
        shard_refs, (xo_ref, a_ref, b_ref), full_refs, sems = refs[:n], refs[n:n + 3], refs[n + 3:2 * n + 3], refs[2 * n + 3:]
        if n:
            start, finish = _gather_parts([w.shape for w in carried], shard_refs, full_refs, *sems)
            pl.when(pl.program_id(0) == 0)(start)
        xv = x_ref[...]
        r, xhat = _rms(xv)
        h = (xhat * g_ref[...]).astype(BF16)
        acc = jnp.zeros((tm, D_MODEL), F32)
        for j in range(N_SHARD):
            a = _dot_nt(h, wg_ref[j])
            b = _dot_nt(h, wu_ref[j])
            a_ref[j] = a.astype(BF16)
            b_ref[j] = b.astype(BF16)
            s = (a * _sigmoid(a) * b).astype(BF16)
            acc = acc + _dot(s, wd_ref[j])
        xo_ref[...] = xv + 0.5 * acc
        if n:
            pl.when(pl.program_id(0) == steps - 1)(finish)

    act = pl.BlockSpec((N_SHARD, tm, FF_SHARD), lambda i: (0, i, 0))
    return pl.pallas_call(
        body, name=name, grid=(steps,),
        in_specs=[_rows(tm, D_MODEL), _whole(), _whole(), _whole(), _whole()] + [_ANY] * n,
        out_specs=[_rows(tm, D_MODEL), act, act] + [_ANY] * n,
        out_shape=[jax.ShapeDtypeStruct((L, D_MODEL), F32),
                   jax.ShapeDtypeStruct((N_SHARD, L, FF_SHARD), BF16),
                   jax.ShapeDtypeStruct((N_SHARD, L, FF_SHARD), BF16)]
        + [jax.ShapeDtypeStruct((N_SHARD,) + w.shape, w.dtype) for w in carried],
        scratch_shapes=[pltpu.SemaphoreType.DMA((7 * n,)), pltpu.SemaphoreType.DMA((7 * n,))] if n else [],
        compiler_params=_params(),
    )(x, g, wg, wu, wd, *carried)


def _ffn_bwd(dxo, x, g, a, b, wg, wu, wd, name):
    L = x.shape[0]
    tm = min(FFN_BWD_TILE, L)

    def body(dxo_ref, x_ref, g_ref, a_ref, b_ref, wg_ref, wu_ref, wd_ref,
             dxi_ref, da_ref, db_ref, s_ref, h_ref, do_ref, dg_ref):
        i = pl.program_id(0)
        xv = x_ref[...]
        gv = g_ref[...]
        r, xhat = _rms(xv)
        h_ref[...] = (xhat * gv).astype(BF16)
        dxo_v = dxo_ref[...]
        d_out = (0.5 * dxo_v).astype(BF16)
        do_ref[...] = d_out
        dh = jnp.zeros((tm, D_MODEL), F32)
        for j in range(N_SHARD):
            av = a_ref[j].astype(F32)
            bv = b_ref[j].astype(F32)
            sg = _sigmoid(av)
            sl = av * sg
            ds = _dot_nt(d_out, wd_ref[j])
            dbv = (ds * sl).astype(BF16)
            dav = (ds * bv * (sg * (1.0 + av * (1.0 - sg)))).astype(BF16)
            da_ref[j] = dav
            db_ref[j] = dbv
            s_ref[j] = (sl * bv).astype(BF16)
            dh = dh + _dot(dav, wg_ref[j]) + _dot(dbv, wu_ref[j])

        @pl.when(i == 0)
        def _():
            dg_ref[...] = jnp.zeros_like(dg_ref)

        dg_ref[...] += jnp.sum(dh * xhat, axis=0, keepdims=True)
        dxi_ref[...] = dxo_v + _rms_bwd(dh, gv, r, xhat)

    act = pl.BlockSpec((N_SHARD, tm, FF_SHARD), lambda i: (0, i, 0))
    act_shape = jax.ShapeDtypeStruct((N_SHARD, L, FF_SHARD), BF16)
    return pl.pallas_call(
        body, name=name, grid=(L // tm,),
        in_specs=[_rows(tm, D_MODEL), _rows(tm, D_MODEL), _whole(), act, act, _whole(), _whole(), _whole()],
        out_specs=[_rows(tm, D_MODEL), act, act, act, _rows(tm, D_MODEL), _rows(tm, D_MODEL), _acc_row(D_MODEL)],
        out_shape=[jax.ShapeDtypeStruct((L, D_MODEL), F32), act_shape, act_shape, act_shape,
                   jax.ShapeDtypeStruct((L, D_MODEL), BF16), jax.ShapeDtypeStruct((L, D_MODEL), BF16),
                   jax.ShapeDtypeStruct((1, D_MODEL), F32)],
        compiler_params=_params(),
    )(dxo, x, g, a, b, wg, wu, wd)


def _matmul_tn(a, b, name):
    ja, L, K = a.shape
    jb, _, N = b.shape
    J = max(ja, jb)
    splits = [s for s in (1, 2, 4, 8) if s == 1 or N % (s * LANES) == 0]
    nsplit = next((s for s in splits if 2 * K * (N // s) * 4 <= TN_VMEM_BUDGET // 2), splits[-1])
    nc = N // nsplit
    left = TN_VMEM_BUDGET - 2 * K * nc * 4
    row_bytes = 2 * (K * a.dtype.itemsize + nc * b.dtype.itemsize)
    tm = next((t for t in (2048, 1024, 512, 256) if L % t == 0 and t * row_bytes <= left), min(128, L))

    def body(a_ref, b_ref, o_ref):
        @pl.when(pl.program_id(2) == 0)
        def _():
            o_ref[...] = jnp.zeros_like(o_ref)

        o_ref[...] += _dot_tn(a_ref[...].astype(BF16), b_ref[...].astype(BF16))

    return pl.pallas_call(
        body, name=name, grid=(J, nsplit, L // tm),
        in_specs=[pl.BlockSpec((None, tm, K), (lambda j, s, i: (j, i, 0)) if ja > 1 else (lambda j, s, i: (0, i, 0))),
                  pl.BlockSpec((None, tm, nc), (lambda j, s, i: (j, i, s)) if jb > 1 else (lambda j, s, i: (0, i, s)))],
        out_specs=pl.BlockSpec((None, K, nc), lambda j, s, i: (j, 0, s)),
        out_shape=jax.ShapeDtypeStruct((J, K, N), F32),
        compiler_params=_params(),
    )(a, b)


def _loss_fwd_bwd(x, g, target):
    L = x.shape[0]
    tm = min(ROW_TILE, L)

    def body(x_ref, g_ref, t_ref, loss_ref, dx_ref, dg_ref):
        i = pl.program_id(0)
        xv = x_ref[...]
        gv = g_ref[...]
        r, xhat = _rms(xv)
        err = xhat * gv - t_ref[...]
        part = 0.5 * jnp.sum(jnp.sum(err * err, axis=1, keepdims=True) * (1.0 / D_MODEL), axis=0, keepdims=True)
        dy = err * (1.0 / D_MODEL)

        @pl.when(i == 0)
        def _():
            dg_ref[...] = jnp.zeros_like(dg_ref)
            loss_ref[...] = jnp.zeros_like(loss_ref)

        loss_ref[...] += jnp.broadcast_to(part, loss_ref.shape)
        dg_ref[...] += jnp.sum(dy * xhat, axis=0, keepdims=True)
        dx_ref[...] = _rms_bwd(dy, gv, r, xhat)

    return pl.pallas_call(
        body, name="loss_fwd_bwd", grid=(L // tm,),
        in_specs=[_rows(tm, D_MODEL), _whole(), _rows(tm, D_MODEL)],
        out_specs=[pl.BlockSpec((8, 128), lambda i: (0, 0)), _rows(tm, D_MODEL), _acc_row(D_MODEL)],
        out_shape=[jax.ShapeDtypeStruct((8, 128), F32), jax.ShapeDtypeStruct((L, D_MODEL), F32),
                   jax.ShapeDtypeStruct((1, D_MODEL), F32)],
        compiler_params=_params(),
    )(x, g, target)


_C_K = ATTN_WIDTH
_C_V = 2 * ATTN_WIDTH
_C_U = 3 * ATTN_WIDTH
_C_G = _C_U + SSM_WIDTH


def _residue_spec(d, tm):
    return pl.BlockSpec((d, tm // d, GROUP_WIDTH), lambda i: (0, i, 0))


def _residue_shape(d, L, dtype):
    return jax.ShapeDtypeStruct((d, L // d, GROUP_WIDTH), dtype)


def _residue_scratch(tm):
    return pltpu.VMEM((GROUP_WIDTH // LANES, tm, LANES), F32)


def _to_residues(val, out_ref, scr, d):
    if d == 1:
        out_ref[0] = val.astype(out_ref.dtype)
        return
    tm = val.shape[0]
    for half in range(GROUP_WIDTH // LANES):
        cols = slice(half * LANES, (half + 1) * LANES)
        scr[half] = val[:, cols]
        for r in range(d):
            out_ref[r, :, cols] = scr[half, pl.ds(r, tm // d, stride=d), :].astype(out_ref.dtype)


def _from_residues(ref, scr, d):
    if d == 1:
        return ref[0].astype(F32)
    rows = ref.shape[1]
    for half in range(GROUP_WIDTH // LANES):
        cols = slice(half * LANES, (half + 1) * LANES)
        for r in range(d):
            scr[half, pl.ds(r, rows, stride=d), :] = ref[r, :, cols].astype(F32)
    return jnp.concatenate([scr[half] for half in range(GROUP_WIDTH // LANES)], axis=1)


def _mix_in_fwd(x, g, w_in, gate_bias):
    L = x.shape[0]
    tm = min(ROW_TILE, L)

    def body(x_ref, g_ref, w_ref, gb_ref, *refs):
        qkv_refs, (u_ref, gate_ref, scr) = refs[:9], refs[9:]
        r, xhat = _rms(x_ref[...])
        h = (xhat * g_ref[...]).astype(BF16)
        for part, (c0, scale) in enumerate(((0, Q_SCALE), (_C_K, 1.0), (_C_V, 1.0))):
            z = _dot_nt(h, w_ref[c0:c0 + ATTN_WIDTH, :]) * scale
            for grp, d in enumerate(DILATIONS):
                _to_residues(z[:, grp * GROUP_WIDTH:(grp + 1) * GROUP_WIDTH], qkv_refs[3 * part + grp], scr, d)
        u_ref[...] = _dot_nt(h, w_ref[_C_U:_C_G, :])
        gate_ref[...] = _sigmoid(_dot_nt(h, w_ref[_C_G:IN_WIDTH, :]) + gb_ref[...])

    return pl.pallas_call(
        body, name="mix_in_fwd", grid=(L // tm,),
        in_specs=[_rows(tm, D_MODEL), _whole(), _whole(), _whole()],
        out_specs=[_residue_spec(d, tm) for d in DILATIONS] * 3 + [_rows(tm, SSM_WIDTH), _rows(tm, 2 * D_MODEL)],
        out_shape=[_residue_shape(d, L, BF16) for d in DILATIONS] * 3
        + [jax.ShapeDtypeStruct((L, SSM_WIDTH), F32), jax.ShapeDtypeStruct((L, 2 * D_MODEL), F32)],
        scratch_shapes=[_residue_scratch(tm)],
        compiler_params=_params(),
    )(x, g, w_in, gate_bias)


def _mix_in_bwd(dx2, x, g, dqkv, du, dgp, w_in):
    L = x.shape[0]
    tm = min(ROW_TILE, L)

    def body(dx2_ref, x_ref, g_ref, *refs):
        piece_refs = refs[:9]
        du_ref, dgp_ref, w_ref, dx1_ref, h_ref, dz_ref, dg_ref, scr = refs[9:]
        i = pl.program_id(0)
        gv = g_ref[...]
        r, xhat = _rms(x_ref[...])
        h_ref[...] = (xhat * gv).astype(BF16)
        for part in range(3):
            for grp, d in enumerate(DILATIONS):
                c0 = part * ATTN_WIDTH + grp * GROUP_WIDTH
                dz_ref[:, c0:c0 + GROUP_WIDTH] = _from_residues(piece_refs[3 * part + grp], scr, d).astype(BF16)
        dz_ref[:, _C_U:_C_G] = du_ref[...].astype(BF16)
        dz_ref[:, _C_G:IN_WIDTH] = dgp_ref[...]
        dh = _dot(dz_ref[...], w_ref[...])

        @pl.when(i == 0)
        def _():
            dg_ref[...] = jnp.zeros_like(dg_ref)

        dg_ref[...] += jnp.sum(dh * xhat, axis=0, keepdims=True)
        dx1_ref[...] = dx2_ref[...] + _rms_bwd(dh, gv, r, xhat)

    return pl.pallas_call(
        body, name="mix_in_bwd", grid=(L // tm,),
        in_specs=[_rows(tm, D_MODEL), _rows(tm, D_MODEL), _whole()] + [_residue_spec(d, tm) for d in DILATIONS] * 3
        + [_rows(tm, SSM_WIDTH), _rows(tm, 2 * D_MODEL), _whole()],
        out_specs=[_rows(tm, D_MODEL), _rows(tm, D_MODEL), _rows(tm, IN_WIDTH), _acc_row(D_MODEL)],
        out_shape=[jax.ShapeDtypeStruct((L, D_MODEL), F32), jax.ShapeDtypeStruct((L, D_MODEL), BF16),
                   jax.ShapeDtypeStruct((L, IN_WIDTH), BF16), jax.ShapeDtypeStruct((1, D_MODEL), F32)],
        scratch_shapes=[_residue_scratch(tm)],
        compiler_params=_params(),
    )(dx2, x, g, *dqkv, du, dgp, w_in)


def _bucket_onehot():
    qi = jnp.arange(ATTN_BLOCK)[:, None]
    kj = jnp.arange(2 * ATTN_BLOCK)[None, :]
    steps = jnp.maximum(qi + ATTN_BLOCK - kj, 0)
    max_exact = N_BUCKETS // 2
    out = []
    for d in DILATIONS:
        dist = steps * d
        df = jnp.maximum(dist, 1).astype(F32)
        large = max_exact + (jnp.log(df / max_exact) / math.log(MAX_DISTANCE / max_exact)
                             * (N_BUCKETS - max_exact)).astype(jnp.int32)
        large = jnp.minimum(large, N_BUCKETS - 1)
        bucket = jnp.where(dist < max_exact, dist, large).reshape(-1)
        out.append((bucket[None, :] == jnp.arange(N_BUCKETS)[:, None]).astype(F32))
    return jnp.stack(out)


def _bias_expand(table_t, onehot):
    n = onehot.shape[-1]

    def body(t_ref, oh_ref, o_ref):
        bias = _dot_exact(t_ref[...], oh_ref[...])
        col = lax.broadcasted_iota(jnp.int32, (8, n), 1)
        qi = col // (2 * ATTN_BLOCK)
        kj = col - qi * (2 * ATTN_BLOCK)
        steps = qi + ATTN_BLOCK - kj
        band = (steps >= 0) & (steps <= WINDOW_STEPS)
        o_ref[0] = jnp.where(band & (kj >= ATTN_BLOCK), bias, NEG_INF)
        o_ref[1] = jnp.where(band, bias, NEG_INF)

    return pl.pallas_call(
        body, name="bias_expand", grid=(3,),
        in_specs=[pl.BlockSpec((None, 8, N_BUCKETS), lambda g: (g, 0, 0)),
                  pl.BlockSpec((None, N_BUCKETS, n), lambda g: (g, 0, 0))],
        out_specs=pl.BlockSpec((None, 2, 8, n), lambda g: (g, 0, 0, 0)),
        out_shape=jax.ShapeDtypeStruct((3, 2, 8, n), F32),
        compiler_params=_params(),
    )(table_t, onehot)


def _bias_reduce(dsum, onehot):
    n = onehot.shape[-1]

    def body(d_ref, oh_ref, o_ref):
        o_ref[...] = _dot_nt_exact(d_ref[...], oh_ref[...])

    return pl.pallas_call(
        body, name="bias_reduce", grid=(3,),
        in_specs=[pl.BlockSpec((None, 8, n), lambda g: (g, 0, 0)),
                  pl.BlockSpec((None, N_BUCKETS, n), lambda g: (g, 0, 0))],
        out_specs=pl.BlockSpec((None, 8, N_BUCKETS), lambda g: (g, 0, 0)),
        out_shape=jax.ShapeDtypeStruct((3, 8, N_BUCKETS), F32),
        compiler_params=_params(),
    )(dsum, onehot)


def _head_of_col(rows):
    return lax.broadcasted_iota(jnp.int32, (rows, GROUP_WIDTH), 1) // HEAD_DIM


_STACK_ROWS = HEADS_PER_GROUP * ATTN_BLOCK


def _stack_heads(x, head_of_col):
    return jnp.concatenate([jnp.where(head_of_col == hh, x, jnp.zeros_like(x)) for hh in range(HEADS_PER_GROUP)],
                           axis=0)


def _attn_specs(qb):
    rows = qb * ATTN_BLOCK
    cur = pl.BlockSpec((None, rows, GROUP_WIDTH), lambda r, n: (r, n, 0))
    prev = pl.BlockSpec((None, ATTN_BLOCK, GROUP_WIDTH), lambda r, n: (r, jnp.maximum(n * qb - 1, 0), 0))
    bias = pl.BlockSpec((2, HEADS_PER_GROUP, ATTN_BLOCK, 2 * ATTN_BLOCK), lambda r, n: (0, 0, 0, 0))
    return cur, prev, bias


def _attn_fwd(q, k, v, bias, name):
    d, M, _ = q.shape
    nb = M // ATTN_BLOCK
    qb = min(ATTN_QB, nb)

    def body(q_ref, kp_ref, kc_ref, vp_ref, vc_ref, bias_ref, o_ref, lse_ref):
        n = pl.program_id(1)
        q_head = _head_of_col(ATTN_BLOCK)
        kwin = jnp.concatenate([kp_ref[...], kc_ref[...]], axis=0)
        vwin = jnp.concatenate([vp_ref[...], vc_ref[...]], axis=0)
        ones = jnp.ones((2 * ATTN_BLOCK, LANES), BF16)
        for b in range(qb):
            rows = slice(b * ATTN_BLOCK, (b + 1) * ATTN_BLOCK)
            window = slice(b * ATTN_BLOCK, (b + 2) * ATTN_BLOCK)
            variant = jnp.minimum(n, 1) if b == 0 else 1
            kk = kwin[window]
            vv = vwin[window]
            q4 = _stack_heads(q_ref[rows, :], q_head)
            logits = _dot_nt(q4, kk) + bias_ref[variant].reshape(_STACK_ROWS, 2 * ATTN_BLOCK)
            m = jnp.max(logits, axis=1, keepdims=True)
            p16 = jnp.exp(logits - m).astype(BF16)
            den = _dot(p16, ones)[:, 0:1]
            out = _dot(p16, vv) * (1.0 / den)
            lse = m + jnp.log(den)
            o_acc = jnp.zeros((ATTN_BLOCK, GROUP_WIDTH), F32)
            lse_acc = jnp.zeros((ATTN_BLOCK, GROUP_WIDTH), F32)
            for hh in range(HEADS_PER_GROUP):
                head_rows = slice(hh * ATTN_BLOCK, (hh + 1) * ATTN_BLOCK)
                o_acc = jnp.where(q_head == hh, out[head_rows], o_acc)
                lse_acc = jnp.where(q_head == hh, lse[head_rows], lse_acc)
            o_ref[rows, :] = o_acc
            lse_ref[rows, :] = lse_acc

    cur, prev, full = _attn_specs(qb)
    return pl.pallas_call(
        body, name=name, grid=(d, nb // qb),
        in_specs=[cur, prev, cur, prev, cur, full],
        out_specs=[cur, cur],
        out_shape=[jax.ShapeDtypeStruct((d, M, GROUP_WIDTH), F32)] * 2,
        compiler_params=_params(),
    )(q, k, k, v, v, bias)


def _attn_bwd(q, k, v, do, lse, delta, bias, name):
    d, M, _ = q.shape
    nb = M // ATTN_BLOCK
    qb = min(ATTN_QB, nb)
    ns = nb // qb
    rows_q = qb * ATTN_BLOCK
    last = slice(rows_q - ATTN_BLOCK, rows_q)

    def body(q_ref, kp_ref, kc_ref, vp_ref, vc_ref, do_ref, lse_ref, dl_ref, bias_ref,
             dq_ref, dk_ref, dv_ref, dsum_ref, pk_ref, pv_ref, wk_ref, wv_ref):
        r = pl.program_id(0)
        n = pl.program_id(1)

        @pl.when((r == 0) & (n == 0))
        def _():
            dsum_ref[...] = jnp.zeros_like(dsum_ref)

        @pl.when(n == 0)
        def _():
            pk_ref[...] = jnp.zeros_like(pk_ref)
            pv_ref[...] = jnp.zeros_like(pv_ref)

        @pl.when(n < ns)
        def _():
            q_head = _head_of_col(ATTN_BLOCK)
            kwin = jnp.concatenate([kp_ref[...], kc_ref[...]], axis=0)
            vwin = jnp.concatenate([vp_ref[...], vc_ref[...]], axis=0)
            wk_ref[...] = jnp.zeros_like(wk_ref)
            wv_ref[...] = jnp.zeros_like(wv_ref)
            for b in range(qb):
                rows = slice(b * ATTN_BLOCK, (b + 1) * ATTN_BLOCK)
                window = slice(b * ATTN_BLOCK, (b + 2) * ATTN_BLOCK)
                variant = jnp.minimum(n, 1) if b == 0 else 1
                kk = kwin[window]
                vv = vwin[window]
                q4 = _stack_heads(q_ref[rows, :], q_head)
                do4 = _stack_heads(do_ref[rows, :], q_head)
                heads = [hh * HEAD_DIM for hh in range(HEADS_PER_GROUP)]
                lse4 = jnp.concatenate([lse_ref[rows, c0:c0 + 1] for c0 in heads], axis=0)
                dl4 = jnp.concatenate([dl_ref[rows, c0:c0 + 1] for c0 in heads], axis=0)
                logits = _dot_nt(q4, kk) + bias_ref[variant].reshape(_STACK_ROWS, 2 * ATTN_BLOCK)
                p = jnp.exp(logits - lse4)
                ds = p * (_dot_nt(do4, vv) - dl4)
                dsum_ref[...] += ds.reshape(HEADS_PER_GROUP, ATTN_BLOCK, 2 * ATTN_BLOCK)
                ds16 = ds.astype(BF16)
                dq4 = _dot(ds16, kk)
                dq_acc = jnp.zeros((ATTN_BLOCK, GROUP_WIDTH), F32)
                for hh in range(HEADS_PER_GROUP):
                    dq_acc = jnp.where(q_head == hh, dq4[hh * ATTN_BLOCK:(hh + 1) * ATTN_BLOCK], dq_acc)
                dq_ref[rows, :] = (dq_acc * Q_SCALE).astype(BF16)
                wk_ref[window, :] += _dot_tn(ds16, q4)
                wv_ref[window, :] += _dot_tn(p.astype(BF16), do4)
            for out_ref, part_ref, win_ref in ((dk_ref, pk_ref, wk_ref), (dv_ref, pv_ref, wv_ref)):
                if ns == 1:
                    out_ref[...] = win_ref[ATTN_BLOCK:, :].astype(BF16)
                    continue
                if qb > 1:
                    out_ref[0:rows_q - ATTN_BLOCK, :] = part_ref[0:rows_q - ATTN_BLOCK, :].astype(BF16)
                out_ref[last, :] = (part_ref[last, :] + win_ref[0:ATTN_BLOCK, :]).astype(BF16)
                part_ref[...] = win_ref[ATTN_BLOCK:, :]

        if ns > 1:
            @pl.when(n == ns)
            def _():
                dk_ref[...] = pk_ref[...].astype(BF16)
                dv_ref[...] = pv_ref[...].astype(BF16)

    def clamp(n):
        return jnp.minimum(n, ns - 1)

    cur = pl.BlockSpec((None, rows_q, GROUP_WIDTH), lambda r, n: (r, clamp(n), 0))
    prev = pl.BlockSpec((None, ATTN_BLOCK, GROUP_WIDTH), lambda r, n: (r, jnp.maximum(clamp(n) * qb - 1, 0), 0))
    lag = pl.BlockSpec((None, rows_q, GROUP_WIDTH), lambda r, n: (r, jnp.maximum(n - 1, 0), 0))
    full = pl.BlockSpec((2, HEADS_PER_GROUP, ATTN_BLOCK, 2 * ATTN_BLOCK), lambda r, n: (0, 0, 0, 0))
    acc = pl.BlockSpec((HEADS_PER_GROUP, ATTN_BLOCK, 2 * ATTN_BLOCK), lambda r, n: (0, 0, 0))
    return pl.pallas_call(
        body, name=name, grid=(d, ns + 1 if ns > 1 else 1),
        in_specs=[cur, prev, cur, prev, cur, cur, cur, cur, full],
        out_specs=[cur, lag, lag, acc],
        out_shape=[jax.ShapeDtypeStruct((d, M, GROUP_WIDTH), BF16)] * 3
        + [jax.ShapeDtypeStruct((HEADS_PER_GROUP, ATTN_BLOCK, 2 * ATTN_BLOCK), F32)],
        scratch_shapes=[pltpu.VMEM((rows_q, GROUP_WIDTH), F32), pltpu.VMEM((rows_q, GROUP_WIDTH), F32),
                        pltpu.VMEM((rows_q + ATTN_BLOCK, GROUP_WIDTH), F32),
                        pltpu.VMEM((rows_q + ATTN_BLOCK, GROUP_WIDTH), F32)],
        compiler_params=_params(),
    )(q, k, k, v, v, do, lse, delta, bias)


def _disc_math(a_re, a_im, ldt, b_re, b_im):
    dt = jnp.exp(ldt)
    mag = jnp.exp(a_re * dt)
    ab_re = mag * jnp.cos(a_im * dt)
    ab_im = mag * jnp.sin(a_im * dt)
    den = a_re * a_re + a_im * a_im
    xr = ab_re - 1.0
    coef_re = (xr * a_re + ab_im * a_im) / den
    coef_im = (ab_im * a_re - xr * a_im) / den
    return ab_re, ab_im, coef_re * b_re - coef_im * b_im, coef_re * b_im + coef_im * b_re


def _block_diag_mask():
    row_g = lax.broadcasted_iota(jnp.int32, (SSM_WIDTH, 2 * NS), 0) // SSM_GROUP
    col = lax.broadcasted_iota(jnp.int32, (SSM_WIDTH, 2 * NS), 1)
    col_g = jnp.where(col >= NS, col - NS, col) // SSM_STATE
    return row_g == col_g


def _disc_fwd(a_re, a_im, ldt, b_re, b_im, c_re, c_im):
    def body(are_ref, aim_ref, ldt_ref, bre_ref, bim_ref, cre_ref, cim_ref, pw_ref, pwr_ref, bd_ref, cdt_ref):
        ab_re, ab_im, bb_re, bb_im = _disc_math(are_ref[...], aim_ref[...], ldt_ref[...], bre_ref[...], bim_ref[...])
        row = lax.broadcasted_iota(jnp.int32, (8, NS), 0)
        pr, pi = ab_re, ab_im
        t_re = jnp.zeros((8, NS), F32)
        t_im = jnp.zeros((8, NS), F32)
        u_re = jnp.zeros((8, NS), F32)
        u_im = jnp.zeros((8, NS), F32)
        for j in range(8):
            t_re = jnp.where(row == j, pr, t_re)
            t_im = jnp.where(row == j, pi, t_im)
            u_re = jnp.where(row == 7 - j, pr, u_re)
            u_im = jnp.where(row == 7 - j, pi, u_im)
            pr, pi = pr * ab_re - pi * ab_im, pr * ab_im + pi * ab_re
        pw_ref[0] = t_re
        pw_ref[1] = t_im
        pwr_ref[0] = u_re
        pwr_ref[1] = u_im
        mask = _block_diag_mask()
        zero = jnp.zeros((SSM_WIDTH, 2 * NS), F32)
        bfull = jnp.concatenate([jnp.concatenate([bb_re] * SSM_GROUPS, axis=0),
                                 jnp.concatenate([bb_im] * SSM_GROUPS, axis=0)], axis=1)
        bd_ref[...] = jnp.where(mask, bfull, zero).astype(BF16)
        cfull = jnp.concatenate([jnp.concatenate([cre_ref[...]] * SSM_GROUPS, axis=0),
                                 jnp.concatenate([-cim_ref[...]] * SSM_GROUPS, axis=0)], axis=1)
        cdt_ref[...] = jnp.where(mask, cfull, zero).astype(BF16)

    return pl.pallas_call(
        body, name="s5_disc_fwd",
        in_specs=[_whole()] * 7, out_specs=[_whole()] * 4,
        out_shape=[jax.ShapeDtypeStruct((2, 8, NS), F32), jax.ShapeDtypeStruct((2, 8, NS), F32),
                   jax.ShapeDtypeStruct((SSM_WIDTH, 2 * NS), BF16), jax.ShapeDtypeStruct((SSM_WIDTH, 2 * NS), BF16)],
        compiler_params=_params(),
    )(a_re, a_im, ldt, b_re, b_im, c_re, c_im)


def _disc_bwd(a_re, a_im, ldt, b_re, b_im, d_bd, d_cdt, d_ab, group_sum):
    def body(are_ref, aim_ref, ldt_ref, bre_ref, bim_ref, dbd_ref, dcdt_ref, dab_ref, gs_ref,
             dare_ref, daim_ref, dldt_ref, dbre_ref, dbim_ref, dcre_ref, dcim_ref):
        col = lax.broadcasted_iota(jnp.int32, (SSM_GROUP, 2 * NS), 1)
        col_g = jnp.where(col >= NS, col - NS, col) // SSM_STATE
        acc_b = jnp.zeros((SSM_GROUP, 2 * NS), F32)
        acc_c = jnp.zeros((SSM_GROUP, 2 * NS), F32)
        for g in range(SSM_GROUPS):
            rows = slice(g * SSM_GROUP, (g + 1) * SSM_GROUP)
            acc_b = acc_b + jnp.where(col_g == g, dbd_ref[rows, :], 0.0)
            acc_c = acc_c + jnp.where(col_g == g, dcdt_ref[rows, :], 0.0)
        dcre_ref[...] = acc_c[:, :NS]
        dcim_ref[...] = -acc_c[:, NS:]
        dab_re = jnp.sum(dab_ref[0], axis=0, keepdims=True)
        dab_im = jnp.sum(dab_ref[1], axis=0, keepdims=True)
        _, vjp = jax.vjp(_disc_math, are_ref[...], aim_ref[...], ldt_ref[...], bre_ref[...], bim_ref[...])
        d_are, d_aim, d_ldt, d_bre, d_bim = vjp((dab_re, dab_im, acc_b[:, :NS], acc_b[:, NS:]))
        dare_ref[...] = d_are
        daim_ref[...] = d_aim
        dbre_ref[...] = d_bre
        dbim_ref[...] = d_bim
        dldt_ref[...] = _dot_exact(jnp.broadcast_to(d_ldt, (8, NS)), gs_ref[...])

    vec = jax.ShapeDtypeStruct((1, NS), F32)
    mat = jax.ShapeDtypeStruct((SSM_GROUP, NS), F32)
    return pl.pallas_call(
        body, name="s5_disc_bwd",
        in_specs=[_whole()] * 9, out_specs=[_whole()] * 7,
        out_shape=[vec, vec, jax.ShapeDtypeStruct((8, 128), F32), mat, mat, mat, mat],
        compiler_params=_params(),
    )(a_re, a_im, ldt, b_re, b_im, d_bd, d_cdt, d_ab, group_sum)


def _scan_blocks(buf, pw_ref, carry_ref, n_blocks, reverse):
    row = lax.broadcasted_iota(jnp.int32, (8, SCAN_LANES), 0)
    for lc in range(NS // SCAN_LANES):
        re_cols = pl.ds(lc * SCAN_LANES, SCAN_LANES)
        im_cols = pl.ds(NS + lc * SCAN_LANES, SCAN_LANES)
        pr = pw_ref[0, :, re_cols]
        pi = pw_ref[1, :, re_cols]
        if reverse:
            pi = -pi
            base = [(7, 1), (6, 2), (4, 4)]
            coef = [(jnp.where(row < 8 - k, pr[j:j + 1], 0.0), jnp.where(row < 8 - k, pi[j:j + 1], 0.0), 8 - k)
                    for j, k in base]
        else:
            base = [(0, 1), (1, 2), (3, 4)]
            coef = [(jnp.where(row >= k, pr[j:j + 1], 0.0), jnp.where(row >= k, pi[j:j + 1], 0.0), k)
                    for j, k in base]

        def step(i, carry, pr=pr, pi=pi, coef=coef, re_cols=re_cols, im_cols=im_cols):
            cr, ci = carry
            blk = (n_blocks - 1 - i) if reverse else i
            rows = pl.ds(pl.multiple_of(blk * 8, 8), 8)
            xr = buf[rows, re_cols]
            xi = buf[rows, im_cols]
            for kr, ki, shift in coef:
                sr = pltpu.roll(xr, shift, 0)
                si = pltpu.roll(xi, shift, 0)
                xr, xi = xr + kr * sr - ki * si, xi + kr * si + ki * sr
            xr, xi = xr + pr * cr - pi * ci, xi + pr * ci + pi * cr
            buf[rows, re_cols] = xr
            buf[rows, im_cols] = xi
            edge = slice(0, 1) if reverse else slice(7, 8)
            return xr[edge], xi[edge]

        cr, ci = lax.fori_loop(0, n_blocks, step, (carry_ref[0:1, re_cols], carry_ref[0:1, im_cols]),
                               unroll=SCAN_UNROLL)
        carry_ref[0:1, re_cols] = cr
        carry_ref[0:1, im_cols] = ci


_SUPER_GROUPS = 16
_SUPER_BLOCKS = [
    (slice(k * _SUPER_GROUPS * SSM_GROUP, (k + 1) * _SUPER_GROUPS * SSM_GROUP),
     [slice(half + k * _SUPER_GROUPS * SSM_STATE, half + (k + 1) * _SUPER_GROUPS * SSM_STATE) for half in (0, NS)])
    for k in range(SSM_GROUPS // _SUPER_GROUPS)]


def _ssm_fwd(u, bd, cdt, d_skip, pw):
    L = u.shape[0]
    tc = min(SSM_FWD_CHUNK, L)

    def body(u_ref, bd_ref, cdt_ref, dsk_ref, pw_ref, y_ref, s_ref, carry_ref):
        @pl.when(pl.program_id(0) == 0)
        def _():
            carry_ref[...] = jnp.zeros_like(carry_ref)

        uv = u_ref[...]
        u16 = uv.astype(BF16)
        for ch, states in _SUPER_BLOCKS:
            for st in states:
                s_ref[:, st] = _dot(u16[:, ch], bd_ref[ch, st])
        _scan_blocks(s_ref, pw_ref, carry_ref, tc // 8, reverse=False)
        for ch, states in _SUPER_BLOCKS:
            y_ref[:, ch] = (sum(_dot_nt(s_ref[:, st].astype(BF16), cdt_ref[ch, st]) for st in states)
                            + dsk_ref[:, ch] * uv[:, ch])

    return pl.pallas_call(
        body, name="s5_fwd", grid=(L // tc,),
        in_specs=[_rows(tc, SSM_WIDTH), _whole(), _whole(), _whole(), _whole()],
        out_specs=[_rows(tc, SSM_WIDTH), _rows(tc, 2 * NS)],
        out_shape=[jax.ShapeDtypeStruct((L, SSM_WIDTH), F32), jax.ShapeDtypeStruct((L, 2 * NS), F32)],
        scratch_shapes=[pltpu.VMEM((8, 2 * NS), F32)],
        compiler_params=_params(),
    )(u, bd, cdt, d_skip, pw)


def _ssm_bwd(dy, u, s, bd, cdt, d_skip, pwr):
    L = u.shape[0]
    tc = min(SSM_CHUNK, L)
    nc = L // tc
    blocks = tc // 8

    def body(dy_ref, u_ref, s_ref, sprev_ref, bd_ref, cdt_ref, dsk_ref, pwr_ref,
             du_ref, ddsk_ref, dbd_ref, dcdt_ref, dab_ref, g_ref, sx_ref, carry_ref):
        i = pl.program_id(0)

        @pl.when(i == 0)
        def _():
            carry_ref[...] = jnp.zeros_like(carry_ref)
            ddsk_ref[...] = jnp.zeros_like(ddsk_ref)
            dbd_ref[...] = jnp.zeros_like(dbd_ref)
            dcdt_ref[...] = jnp.zeros_like(dcdt_ref)
            dab_ref[...] = jnp.zeros_like(dab_ref)

        dyv = dy_ref[...]
        uv = u_ref[...]
        dy16 = dyv.astype(BF16)
        u16 = uv.astype(BF16)
        for ch, states in _SUPER_BLOCKS:
            for st in states:
                g_ref[:, st] = _dot(dy16[:, ch], cdt_ref[ch, st])
        _scan_blocks(g_ref, pwr_ref, carry_ref, blocks, reverse=True)
        ddsk_ref[...] += jnp.sum(dyv * uv, axis=0, keepdims=True)
        for ch, states in _SUPER_BLOCKS:
            du = dsk_ref[:, ch] * dyv[:, ch]
            for st in states:
                g16 = g_ref[:, st].astype(BF16)
                du = du + _dot_nt(g16, bd_ref[ch, st])
                dbd_ref[ch, st] += _dot_tn(u16[:, ch], g16)
                dcdt_ref[ch, st] += _dot_tn(dy16[:, ch], s_ref[:, st].astype(BF16))
            du_ref[:, ch] = du

        sx_ref[pl.ds(8, tc), :] = s_ref[...]
        sx_ref[pl.ds(0, 8), :] = jnp.where(i == nc - 1, 0.0, sprev_ref[...])
        row = lax.broadcasted_iota(jnp.int32, (8, SCAN_LANES), 0)
        for lc in range(NS // SCAN_LANES):
            re_cols = pl.ds(lc * SCAN_LANES, SCAN_LANES)
            im_cols = pl.ds(NS + lc * SCAN_LANES, SCAN_LANES)

            def step(b, acc, re_cols=re_cols, im_cols=im_cols):
                ar, ai = acc
                off = pl.multiple_of(b * 8, 8)
                gr = g_ref[pl.ds(off, 8), re_cols]
                gi = g_ref[pl.ds(off, 8), im_cols]
                before = pl.ds(off, 8)
                here = pl.ds(off + 8, 8)
                sr = jnp.where(row == 0, sx_ref[before, re_cols][7:8], pltpu.roll(sx_ref[here, re_cols], 1, 0))
                si = jnp.where(row == 0, sx_ref[before, im_cols][7:8], pltpu.roll(sx_ref[here, im_cols], 1, 0))
                return ar + gr * sr + gi * si, ai + gi * sr - gr * si

            zero = jnp.zeros((8, SCAN_LANES), F32)
            ar, ai = lax.fori_loop(0, blocks, step, (zero, zero), unroll=SCAN_UNROLL)
            dab_ref[0, :, re_cols] += ar
            dab_ref[1, :, re_cols] += ai

    rev = lambda i: (nc - 1 - i, 0)
    sprev = pl.BlockSpec((8, 2 * NS), lambda i: (jnp.maximum((nc - 1 - i) * blocks - 1, 0), 0))
    return pl.pallas_call(
        body, name="s5_bwd", grid=(nc,),
        in_specs=[pl.BlockSpec((tc, SSM_WIDTH), rev), pl.BlockSpec((tc, SSM_WIDTH), rev),
                  pl.BlockSpec((tc, 2 * NS), rev), sprev, _whole(), _whole(), _whole(), _whole()],
        out_specs=[pl.BlockSpec((tc, SSM_WIDTH), rev), _whole(), _whole(), _whole(), _whole()],
        out_shape=[jax.ShapeDtypeStruct((L, SSM_WIDTH), F32), jax.ShapeDtypeStruct((1, SSM_WIDTH), F32),
                   jax.ShapeDtypeStruct((SSM_WIDTH, 2 * NS), F32), jax.ShapeDtypeStruct((SSM_WIDTH, 2 * NS), F32),
                   jax.ShapeDtypeStruct((2, 8, NS), F32)],
        scratch_shapes=[pltpu.VMEM((tc, 2 * NS), F32), pltpu.VMEM((tc + 8, 2 * NS), F32), pltpu.VMEM((8, 2 * NS), F32)],
        compiler_params=_params(),
    )(dy, u, s, s, bd, cdt, d_skip, pwr)


def _branches(o_attn, y, gates, w_ab, w_glu, w_sb):
    ya = _dot(o_attn.astype(BF16), w_ab[...])
    gel = _gelu(y)
    glu = _dot(gel.astype(BF16), w_glu[...])
    p = glu[:, :SSM_WIDTH]
    sg = _sigmoid(glu[:, SSM_WIDTH:])
    ys2 = p * sg
    ysb = _dot(ys2.astype(BF16), w_sb[...])
    ga = gates[:, :D_MODEL]
    gs = gates[:, D_MODEL:]
    return ya, gel, p, sg, ys2, ysb, ga, gs


def _mix_out_fwd(x1, o_g, lse_g, y, gates, w_ab, w_glu, w_sb, w_out):
    L = x1.shape[0]
    tm = min(ROW_TILE, L)

    def body(x_ref, o0, o1, o2, l0, l1, l2, y_ref, gate_ref, wab_ref, wglu_ref, wsb_ref, wout_ref,
             x2_ref, oat_ref, lse0, lse1, lse2, scr):
        la, lb, lc = (_from_residues(ref, scr, d) for ref, d in zip((l0, l1, l2), DILATIONS))
        m = jnp.maximum(jnp.maximum(la, lb), lc)
        ea, eb, ec = jnp.exp(la - m), jnp.exp(lb - m), jnp.exp(lc - m)
        tot = ea + eb + ec
        oa, ob, oc = (_from_residues(ref, scr, d) for ref, d in zip((o0, o1, o2), DILATIONS))
        o_attn = (ea * oa + eb * ob + ec * oc) / tot
        oat_ref[...] = o_attn
        lse = m + jnp.log(tot)
        for ref, d in zip((lse0, lse1, lse2), DILATIONS):
            _to_residues(lse, ref, scr, d)
        ya, _, _, _, _, ysb, ga, gs = _branches(o_attn, y_ref[...], gate_ref[...], wab_ref, wglu_ref, wsb_ref)
        mix = ga * ya + gs * ysb
        x2_ref[...] = x_ref[...] + _dot(mix.astype(BF16), wout_ref[...])

    res = [_residue_spec(d, tm) for d in DILATIONS]
    return pl.pallas_call(
        body, name="mix_out_fwd", grid=(L // tm,),
        in_specs=[_rows(tm, D_MODEL)] + res * 2 + [_rows(tm, SSM_WIDTH), _rows(tm, 2 * D_MODEL)] + [_whole()] * 4,
        out_specs=[_rows(tm, D_MODEL), _rows(tm, GROUP_WIDTH)] + res,
        out_shape=[jax.ShapeDtypeStruct((L, D_MODEL), F32), jax.ShapeDtypeStruct((L, GROUP_WIDTH), F32)]
        + [_residue_shape(d, L, F32) for d in DILATIONS],
        scratch_shapes=[_residue_scratch(tm)],
        compiler_params=_params(),
    )(x1, *o_g, *lse_g, y, gates, w_ab, w_glu, w_sb, w_out)


def _mix_out_bwd(dx2, o_attn, y, gates, w_ab, w_glu, w_sb, w_out, head_sum):
    L = dx2.shape[0]
    tm = min(ROW_TILE, L)

    def body(dx_ref, oat_ref, y_ref, gate_ref, wab_ref, wglu_ref, wsb_ref, wout_ref, hs_ref,
             do0, do1, do2, dl0, dl1, dl2, dy_ref, dgp_ref, mix_ref, dya_ref, dys_ref, ys2_ref, gel_ref, dglu_ref,
             dgb_ref, scr):
        i = pl.program_id(0)
        o_attn = oat_ref[...]
        yv = y_ref[...]
        ya, gel, p, sg, ys2, ysb, ga, gs = _branches(o_attn, yv, gate_ref[...], wab_ref, wglu_ref, wsb_ref)
        mix_ref[...] = (ga * ya + gs * ysb).astype(BF16)
        ys2_ref[...] = ys2.astype(BF16)
        gel_ref[...] = gel.astype(BF16)
        dmix = _dot_nt(dx_ref[...].astype(BF16), wout_ref[...])
        dgp = jnp.concatenate([dmix * ya * ga * (1.0 - ga), dmix * ysb * gs * (1.0 - gs)], axis=1)
        dgp_ref[...] = dgp.astype(BF16)

        @pl.when(i == 0)
        def _():
            dgb_ref[...] = jnp.zeros_like(dgb_ref)

        dgb_ref[...] += jnp.sum(dgp, axis=0, keepdims=True)
        dya = (dmix * ga).astype(BF16)
        dys = (dmix * gs).astype(BF16)
        dya_ref[...] = dya
        dys_ref[...] = dys
        d_o = _dot_nt(dya, wab_ref[...])
        delta = _dot_exact(d_o * o_attn, hs_ref[...])
        for do_ref, dl_ref, d in zip((do0, do1, do2), (dl0, dl1, dl2), DILATIONS):
            _to_residues(d_o, do_ref, scr, d)
            _to_residues(delta, dl_ref, scr, d)
        dys2 = _dot_nt(dys, wsb_ref[...])
        dglu = jnp.concatenate([dys2 * sg, dys2 * p * sg * (1.0 - sg)], axis=1).astype(BF16)
        dglu_ref[...] = dglu
        dy_ref[...] = _dot_nt(dglu, wglu_ref[...]) * _gelu_grad(yv)

    grp = _rows(tm, GROUP_WIDTH)
    wide = _rows(tm, D_MODEL)
    half = _rows(tm, SSM_WIDTH)
    res = [_residue_spec(d, tm) for d in DILATIONS]
    sds = jax.ShapeDtypeStruct
    return pl.pallas_call(
        body, name="mix_out_bwd", grid=(L // tm,),
        in_specs=[wide, grp, half, _rows(tm, 2 * D_MODEL)] + [_whole()] * 5,
        out_specs=res + res + [half, _rows(tm, 2 * D_MODEL), wide, wide, wide, half, half, wide, _acc_row(2 * D_MODEL)],
        out_shape=[_residue_shape(d, L, BF16) for d in DILATIONS] + [_residue_shape(d, L, F32) for d in DILATIONS]
        + [sds((L, SSM_WIDTH), F32),
           sds((L, 2 * D_MODEL), BF16), sds((L, D_MODEL), BF16), sds((L, D_MODEL), BF16),
           sds((L, D_MODEL), BF16), sds((L, SSM_WIDTH), BF16), sds((L, SSM_WIDTH), BF16),
           sds((L, D_MODEL), BF16), sds((1, 2 * D_MODEL), F32)],
        scratch_shapes=[_residue_scratch(tm)],
        compiler_params=_params(),
    )(dx2, o_attn, y, gates, w_ab, w_glu, w_sb, w_out, head_sum)


def _adamw(w, g, m, v, name):
    R, C = w.shape
    tr = _row_tile(R, max(8, ADAMW_BLOCK_BYTES // (4 * C)))

    def body(w_ref, g_ref, m_ref, v_ref, d_ref, mo_ref, vo_ref):
        gv = g_ref[...]
        mn = ADAM_B1 * m_ref[...] + (1.0 - ADAM_B1) * gv
        vn = ADAM_B2 * v_ref[...] + (1.0 - ADAM_B2) * (gv * gv)
        m_hat = mn / (1.0 - ADAM_B1 ** ADAM_STEP)
        v_hat = vn / (1.0 - ADAM_B2 ** ADAM_STEP)
        d_ref[...] = -ADAM_LR * (m_hat / (jnp.sqrt(v_hat) + ADAM_EPS) + ADAM_WD * w_ref[...])
        mo_ref[...] = mn
        vo_ref[...] = vn

    blk = pl.BlockSpec((tr, C), lambda i: (i, 0))
    return pl.pallas_call(
        body, name=name, grid=(R // tr,),
        in_specs=[blk] * 4, out_specs=[blk] * 3,
        out_shape=[jax.ShapeDtypeStruct((R, C), F32)] * 3,
        compiler_params=_params(),
    )(w, g, m, v)


def _sum_chips_into_half(u, t, name):
    S, H, C = u.shape
    tr = _row_tile(H, 512)
    hb = H // tr

    def body(s_ref, t_ref, a_ref, b_ref, c_ref, o_ref):
        me = s_ref[1]
        others = (a_ref[...], b_ref[...], c_ref[...])
        acc = None
        for chip in range(S):
            below = others[min(chip, S - 2)]
            above = others[max(chip - 1, 0)]
            term = jnp.where(me == chip, t_ref[...], jnp.where(me > chip, below, above)).astype(F32)
            acc = term if acc is None else acc + term
        o_ref[...] = acc

    x, y, c = lax.axis_index("x"), lax.axis_index("y"), lax.axis_index("c")
    me = 2 * x + y
    scalars = jnp.stack([c, me] + [j + (j >= me).astype(jnp.int32) for j in range(S - 1)]).astype(jnp.int32)
    blk = (None, tr, C)
    return pl.pallas_call(
        body, name=name,
        grid_spec=pltpu.PrefetchScalarGridSpec(
            num_scalar_prefetch=1, grid=(hb,),
            in_specs=[pl.BlockSpec(blk, lambda i, s: (s[1], i, 0))]
            + [pl.BlockSpec(blk, functools.partial(lambda j, i, s: (s[2 + j], i, 0), j)) for j in range(S - 1)],
            out_specs=pl.BlockSpec((tr, C), lambda i, s: (s[0] * hb + i, 0))),
        out_shape=jax.ShapeDtypeStruct((2 * H, C), F32),
        compiler_params=_params(),
    )(scalars, t, u, u, u)


def _add_halves(g, r1, name):
    S, R, C = g.shape
    H = R // 2
    tr = _row_tile(H, 512)
    hb = H // tr

    def body(c_ref, g_ref, r_ref, o_ref):
        o_ref[...] = (g_ref[...] + r_ref[...]).astype(BF16)

    core = lax.axis_index("c").astype(jnp.int32).reshape(1)
    return pl.pallas_call(
        body, name=name,
        grid_spec=pltpu.PrefetchScalarGridSpec(
            num_scalar_prefetch=1, grid=(S, hb),
            in_specs=[pl.BlockSpec((None, tr, C), lambda j, i, c_ref: (j, c_ref[0] * hb + i, 0)),
                      pl.BlockSpec((None, tr, C), lambda j, i, c_ref: (j, i, 0))],
            out_specs=pl.BlockSpec((None, tr, C), lambda j, i, c_ref: (j, i, 0))),
        out_shape=jax.ShapeDtypeStruct((S, H, C), BF16),
        compiler_params=_params(),
    )(core, g, r1)


_ANY = pl.BlockSpec(memory_space=pl.ANY)


def _place():
    x, y, c = lax.axis_index("x"), lax.axis_index("y"), lax.axis_index("c")
    chips = [(1 - x, y), (x, 1 - y), (1 - x, 1 - y)]
    return x, y, c, chips


def _remote(src, dst, send_sems, recv_sems, k, device):
    return pltpu.make_async_remote_copy(src_ref=src, dst_ref=dst, send_sem=send_sems.at[k], recv_sem=recv_sems.at[k],
                                        device_id=device, device_id_type=MESH)


def _gather_parts(shapes, w_refs, out_refs, send_sems, recv_sems):
    n = len(shapes)
    x, y, c, chips = _place()
    me = 2 * x + y
    sibling = (x, y, 1 - c)

    def half(k, chip_idx, core):
        H = shapes[k][0] // 2
        return out_refs[k].at[chip_idx, pl.ds(core * H, H), :]

    mine = [_remote(w_refs[k], out_refs[k].at[me], send_sems, recv_sems, 6 * n + k, sibling) for k in range(n)]
    first = []
    for k in range(n):
        H = shapes[k][0] // 2
        for j, (cx, cy) in enumerate(chips):
            first.append(_remote(w_refs[k].at[pl.ds(c * H, H), :], half(k, me, c), send_sems, recv_sems,
                                 3 * k + j, (cx, cy, c)))

    def start():
        for cp in mine + first:
            cp.start()

    def finish():
        passed = []
        for k in range(n):
            for j, (cx, cy) in enumerate(chips):
                landed = half(k, 2 * cx + cy, c)
                _remote(landed, landed, send_sems, recv_sems, 3 * k + j, (cx, cy, c)).wait_recv()
                fwd = _remote(landed, landed, send_sems, recv_sems, 3 * n + 3 * k + j, sibling)
                fwd.start()
                passed.append(fwd)
        for k in range(n):
            for j, (cx, cy) in enumerate(chips):
                other = half(k, 2 * cx + cy, 1 - c)
                _remote(other, other, send_sems, recv_sems, 3 * n + 3 * k + j, sibling).wait_recv()
        for cp in mine:
            cp.wait_recv()
        for cp in first + passed + mine:
            cp.wait_send()

    return start, finish


def _gather_weights(shards, name):
    n = len(shards)

    def body(*refs):
        x, y, c, chips = _place()
        _handshake([(x, y, 1 - c)] + [(cx, cy, c) for cx, cy in chips])
        start, finish = _gather_parts([w.shape for w in shards], refs[:n], refs[n:2 * n], *refs[2 * n:2 * n + 2])
        start()
        finish()

    return _sequenced(body, name, shards, [jax.ShapeDtypeStruct((N_SHARD,) + w.shape, w.dtype) for w in shards],
                      7 * n, COLLECTIVE_IDS["gather"])


def _handshake(peers):
    barrier = pltpu.get_barrier_semaphore()
    for peer in peers:
        pl.semaphore_signal(barrier, inc=1, device_id=peer, device_id_type=MESH)
    pl.semaphore_wait(barrier, len(peers))


def _sequenced(body, name, ins, out_shapes, n_sems, collective_id):
    return pl.kernel(
        body, out_type=list(out_shapes), mesh=plsc.ScalarSubcoreMesh(axis_name="sequencer", num_cores=1), name=name,
        scratch_types=(pltpu.SemaphoreType.DMA((n_sems,)), pltpu.SemaphoreType.DMA((n_sems,))),
        compiler_params=pltpu.CompilerParams(collective_id=collective_id))(*ins)


def _swap_halves(gs, name, collective_id):
    n = len(gs)

    def body(*refs):
        g_refs, out_refs = refs[:n], refs[n:2 * n]
        send_sems, recv_sems = refs[2 * n:]
        x, y, c, _ = _place()
        _handshake([(x, y, 1 - c)])
        cps = []
        for k in range(n):
            H = gs[k].shape[1] // 2
            cp = _remote(g_refs[k].at[:, pl.ds((1 - c) * H, H), :], out_refs[k], send_sems, recv_sems, k, (x, y, 1 - c))
            cp.start()
            cps.append(cp)
        for cp in cps:
            cp.wait()

    return _sequenced(body, name, gs, [jax.ShapeDtypeStruct((g.shape[0], g.shape[1] // 2, g.shape[2]), g.dtype)
                                       for g in gs], n, collective_id)


def _exchange_chips(ts, name, collective_id):
    n = len(ts)

    def body(*refs):
        t_refs, out_refs = refs[:n], refs[n:2 * n]
        send_sems, recv_sems = refs[2 * n:]
        x, y, c, chips = _place()
        me = 2 * x + y
        _handshake([(cx, cy, c) for cx, cy in chips])
        sent = []
        for k in range(n):
            for j, (cx, cy) in enumerate(chips):
                cp = _remote(t_refs[k].at[2 * cx + cy], out_refs[k].at[me], send_sems, recv_sems, 3 * k + j, (cx, cy, c))
                cp.start()
                sent.append(cp)
        for k in range(n):
            for j, (cx, cy) in enumerate(chips):
                slot = out_refs[k].at[2 * cx + cy]
                _remote(slot, slot, send_sems, recv_sems, 3 * k + j, (cx, cy, c)).wait_recv()
        for cp in sent:
            cp.wait_send()

    return _sequenced(body, name, ts, [jax.ShapeDtypeStruct(t.shape, t.dtype) for t in ts], 3 * n, collective_id)


def _join_halves(fs, name):
    n = len(fs)

    def body(*refs):
        out_refs = refs[n:2 * n]
        send_sems, recv_sems, _ = refs[2 * n:]
        x, y, c, _ = _place()
        sent = []
        for k in range(n):
            H = fs[k].shape[0] // 2
            here = out_refs[k].at[pl.ds(c * H, H), :]
            cp = _remote(here, here, send_sems, recv_sems, k, (x, y, 1 - c))
            cp.start()
            sent.append(cp)
        for k in range(n):
            H = fs[k].shape[0] // 2
            other = out_refs[k].at[pl.ds((1 - c) * H, H), :]
            _remote(other, other, send_sems, recv_sems, k, (x, y, 1 - c)).wait_recv()
        for cp in sent:
            cp.wait_send()

    return pl.pallas_call(
        body, name=name,
        in_specs=[_ANY] * n, out_specs=[_ANY] * n,
        out_shape=[jax.ShapeDtypeStruct(f.shape, f.dtype) for f in fs],
        input_output_aliases={k: k for k in range(n)},
        scratch_shapes=[pltpu.SemaphoreType.DMA((n,)), pltpu.SemaphoreType.DMA((n,)), pltpu.SemaphoreType.DMA((1,))],
    )(*fs)


def _gather_small(v):
    R, C = v.shape

    def body(v_ref, out_ref, send_sems, recv_sems):
        x, y, c, _ = _place()
        me = 4 * x + 2 * y + c
        flips = [(fx, fy, fc) for fx in (0, 1) for fy in (0, 1) for fc in (0, 1)][1:]
        peers = [((1 - x) if fx else x, (1 - y) if fy else y, (1 - c) if fc else c) for fx, fy, fc in flips]
        _handshake(peers)
        sent = []
        for j, peer in enumerate(peers):
            cp = _remote(v_ref, out_ref.at[me], send_sems, recv_sems, j, peer)
            cp.start()
            sent.append(cp)
        for j, peer in enumerate(peers):
            slot = out_ref.at[4 * peer[0] + 2 * peer[1] + peer[2]]
            _remote(slot, slot, send_sems, recv_sems, j, peer).wait_recv()
        for cp in sent:
            cp.wait_send()

    return _sequenced(body, "gather_small", [v], [jax.ShapeDtypeStruct((8, R, C), F32)], 7,
                      COLLECTIVE_IDS["gather_small"])[0]


def _sum_devices(x, own, name):
    S, R, C = x.shape
    tr = _row_tile(R, 2048)

    def body(s_ref, x_ref, own_ref, o_ref):
        me = s_ref[0]
        acc = None
        for k in range(S):
            term = jnp.where(me == k, own_ref[...], x_ref[k])
            acc = term if acc is None else acc + term
        o_ref[...] = acc

    x_, y_, c_ = lax.axis_index("x"), lax.axis_index("y"), lax.axis_index("c")
    me = (4 * x_ + 2 * y_ + c_).astype(jnp.int32).reshape(1)
    return pl.pallas_call(
        body, name=name,
        grid_spec=pltpu.PrefetchScalarGridSpec(
            num_scalar_prefetch=1, grid=(R // tr,),
            in_specs=[pl.BlockSpec((S, tr, C), lambda i, s: (0, i, 0)), pl.BlockSpec((tr, C), lambda i, s: (i, 0))],
            out_specs=pl.BlockSpec((tr, C), lambda i, s: (i, 0))),
        out_shape=jax.ShapeDtypeStruct((R, C), F32),
        compiler_params=_params(),
    )(me, x, own)


def _after(earlier, arrays):
    return lax.optimization_barrier((earlier, arrays))


def _reduce_swap(gs, tag, earlier):
    gs = _after(earlier, gs)[1]
    return gs, _swap_halves(gs, "reduce_swap_" + tag, COLLECTIVE_IDS["swap_" + tag])


def _reduce_exchange(gs, r1, names, tag, later_than):
    r1 = _after(later_than, r1)[1]
    ts = [_add_halves(g, r, "reduce_add_cores_" + nm) for g, r, nm in zip(gs, r1, names)]
    us = _exchange_chips(ts, "reduce_exchange_" + tag, COLLECTIVE_IDS["exchange_" + tag])
    return us, ts


def _reduce_finish(us, ts, names, tag):
    fs = [_sum_chips_into_half(u, t, "reduce_add_chips_" + nm) for u, t, nm in zip(us, ts, names)]
    return _join_halves(fs, "reduce_join_" + tag)


BIG = ["ffn1_w_gate", "ffn1_w_up", "ffn1_w_down", "w_in", "ssm_w_glu", "w_attn_branch", "w_ssm_branch",
       "w_out", "ffn2_w_gate", "ffn2_w_up", "ffn2_w_down"]
SMALL = ["ffn1_norm", "mix_norm", "gate_bias", "rel_bias_table", "ssm_a_re", "ssm_a_im", "ssm_log_dt",
         "ssm_b_re", "ssm_b_im", "ssm_c_re", "ssm_c_im", "ssm_d", "ffn2_norm", "final_norm"]
ORDER = ["ffn1_norm", "ffn1_w_gate", "ffn1_w_up", "ffn1_w_down", "mix_norm", "w_in", "gate_bias", "rel_bias_table",
         "ssm_a_re", "ssm_a_im", "ssm_log_dt", "ssm_b_re", "ssm_b_im", "ssm_c_re", "ssm_c_im", "ssm_d",
         "ssm_w_glu", "w_attn_branch", "w_ssm_branch", "w_out", "ffn2_norm", "ffn2_w_gate", "ffn2_w_up",
         "ffn2_w_down", "final_norm"]


_SMALL_TILE = 8 * LANES


def _pack_small(arrays):
    rows = []
    for a in arrays:
        flat = a.reshape(-1).astype(F32)
        rows.append(jnp.pad(flat, (0, (-flat.shape[0]) % _SMALL_TILE)).reshape(-1, LANES))
    return jnp.concatenate(rows, axis=0)


def _unpack_small(packed, shapes):
    out, r0 = [], 0
    for shp in shapes:
        n = math.prod(shp)
        rows = 8 * -(-n // _SMALL_TILE)
        out.append(packed[r0:r0 + rows].reshape(-1)[:n].reshape(shp))
        r0 += rows
    return out


def _split_cols(g):
    K, N = g.shape
    return g.reshape(K, N_SHARD, N // N_SHARD).transpose(1, 0, 2)


def _join_cols(w):
    S, K, n = w.shape
    return w.transpose(1, 0, 2).reshape(K, S * n)


COL_SHARDED = ("ssm_w_glu", "w_attn_branch", "w_ssm_branch")
TRANSPOSED = ("ffn1_w_gate", "ffn1_w_up", "ffn2_w_gate", "ffn2_w_up", "w_in")


def _shard_2d(name, arr):
    two_d = arr.reshape(arr.shape[-2:])
    return two_d.T if name in TRANSPOSED else two_d


def _shard_nd(name, two_d, shape):
    return (two_d.T if name in TRANSPOSED else two_d).reshape(shape)


class _GradSync:
    def __init__(self, weights, moms, vels):
        self.weights, self.moms, self.vels = weights, moms, vels
        self.grads, self.delta, self.new_m, self.new_v = {}, {}, {}, {}
        self.loss = None
        self._earlier = []
        self._swapped = {}
        self._exchanged = {}

    def swap(self, tag, gw, later_than=()):
        gs = []
        for n in REDUCE_GROUPS[tag]:
            g = gw[n]
            if n in COL_SHARDED:
                g = _split_cols(g)
            elif n in ("w_out", "w_in"):
                g = g.reshape(N_SHARD, g.shape[0] // N_SHARD, g.shape[1])
            gs.append(g)
        self._swapped[tag] = _reduce_swap(gs, tag, list(self._earlier) + list(later_than))
        self._earlier = self._swapped[tag][1]

    def exchange(self, tag, later_than):
        gs, r1 = self._swapped[tag]
        us, ts = _reduce_exchange(gs, r1, REDUCE_GROUPS[tag], tag, later_than)
        self._exchanged[tag] = (us, ts)
        self._earlier = us

    def small_ready(self, gs, loss_blk, later_than=()):
        _, (mine,) = _after(list(self._earlier) + list(later_than),
                            [_pack_small([gs[n] for n in SMALL] + [loss_blk[0:1, :]])])
        others = _gather_small(mine)
        self._exchanged["small"] = (others, mine)
        self._earlier = [others]

    def finish(self, tag):
        made = []
        if tag == "small":
            others, mine = self._exchanged[tag]
            shapes = [self.weights[n].shape for n in SMALL]
            total = _unpack_small(_sum_devices(others, mine, "sum_small"), shapes + [(128,)])
            self.loss = total[-1][0]
            self.grads.update(zip(SMALL, total[:-1]))
            packed = [_pack_small([src[n] for n in SMALL]) for src in (self.weights, self.grads, self.moms, self.vels)]
            for dst, res in zip((self.delta, self.new_m, self.new_v), _adamw(*packed, "adamw_small")):
                dst.update(zip(SMALL, _unpack_small(res, shapes)))
            for n in SMALL:
                made += [self.grads[n], self.delta[n], self.new_m[n], self.new_v[n]]
            return made + [self.loss]
        names = REDUCE_GROUPS[tag]
        us, ts = self._exchanged[tag]
        for n, g in zip(names, _reduce_finish(us, ts, names, tag)):
            shp = self.weights[n].shape
            d, m, v = _adamw(_shard_2d(n, self.weights[n]), g, _shard_2d(n, self.moms[n]), _shard_2d(n, self.vels[n]),
                             "adamw_" + n)
            self.grads[n], self.delta[n] = _shard_nd(n, g, shp), _shard_nd(n, d, shp)
            self.new_m[n], self.new_v[n] = _shard_nd(n, m, shp), _shard_nd(n, v, shp)
            made += [self.grads[n], self.delta[n], self.new_m[n], self.new_v[n]]
        return made

    def finish_all(self):
        self.exchange("ffn1", later_than=self.finish("ffn2"))
        for tag in ("mixer", "w_in", "small", "ffn1"):
            self.finish(tag)


def _local_step(x, target, w, later, small, sync):
    L = x.shape[0]
    row = lambda v: v.reshape(1, -1)

    a_re, a_im = small["ssm_a_re"].reshape(1, NS), small["ssm_a_im"].reshape(1, NS)
    ldt = jnp.repeat(small["ssm_log_dt"].reshape(SSM_GROUPS), SSM_STATE).reshape(1, NS)
    to_cn = lambda b: b.reshape(SSM_GROUPS, SSM_STATE, SSM_GROUP).transpose(2, 0, 1).reshape(SSM_GROUP, NS)
    c_to_cn = lambda c: c.reshape(SSM_GROUPS, SSM_GROUP, SSM_STATE).transpose(1, 0, 2).reshape(SSM_GROUP, NS)
    b_re, b_im = to_cn(small["ssm_b_re"]), to_cn(small["ssm_b_im"])
    c_re, c_im = c_to_cn(small["ssm_c_re"]), c_to_cn(small["ssm_c_im"])
    d_skip = row(small["ssm_d"])
    pw, pwr, bd, cdt = _disc_fwd(a_re, a_im, ldt, b_re, b_im, c_re, c_im)

    onehot = _bucket_onehot()
    table_t = small["rel_bias_table"].T.reshape(3, HEADS_PER_GROUP, N_BUCKETS)
    table_t = jnp.pad(table_t, ((0, 0), (0, 8 - HEADS_PER_GROUP), (0, 0)))
    bias = _bias_expand(table_t, onehot)[:, :, :HEADS_PER_GROUP].reshape(
        3, 2, HEADS_PER_GROUP, ATTN_BLOCK, 2 * ATTN_BLOCK)

    n1, nm, n2, nf = row(small["ffn1_norm"]), row(small["mix_norm"]), row(small["ffn2_norm"]), row(small["final_norm"])
    gate_bias = row(small["gate_bias"])

    x1, a1, b1, *later_full = _ffn_fwd(x, n1, w["ffn1_w_gate"], w["ffn1_w_up"], w["ffn1_w_down"], "ffn1_fwd",
                                       carried=list(later.values()))
    w = dict(w, **dict(zip(later, later_full)))
    for n in COL_SHARDED:
        w[n] = _join_cols(w[n])
    w["w_out"] = w["w_out"].reshape(D_MODEL, D_MODEL)
    w["w_in"] = w["w_in"].reshape(IN_WIDTH, D_MODEL)
    *qkv, u, gates = _mix_in_fwd(x1, nm, w["w_in"], gate_bias)
    q, k, v = qkv[0:3], qkv[3:6], qkv[6:9]
    o_g, lse_g = [], []
    for grp in range(3):
        o, lse = _attn_fwd(q[grp], k[grp], v[grp], bias[grp], f"attn_fwd_{grp}")
        o_g.append(o)
        lse_g.append(lse)
    y, s = _ssm_fwd(u, bd, cdt, d_skip, pw)
    x2, o_attn, *lse_tot = _mix_out_fwd(x1, o_g, lse_g, y, gates, w["w_attn_branch"], w["ssm_w_glu"],
                                        w["w_ssm_branch"], w["w_out"])
    x3, a2, b2 = _ffn_fwd(x2, n2, w["ffn2_w_gate"], w["ffn2_w_up"], w["ffn2_w_down"], "ffn2_fwd")
    loss_blk, dx3, d_nf = _loss_fwd_bwd(x3, nf, target)

    gw, gs = {}, {}
    gs["final_norm"] = d_nf

    dx2, da, db, sact, h, d_out, gs["ffn2_norm"] = _ffn_bwd(dx3, x2, n2, a2, b2, w["ffn2_w_gate"], w["ffn2_w_up"],
                                                            w["ffn2_w_down"], "ffn2_bwd")
    gw["ffn2_w_gate"] = _matmul_tn(da, h[None], "ffn2_dw_gate")
    gw["ffn2_w_up"] = _matmul_tn(db, h[None], "ffn2_dw_up")
    gw["ffn2_w_down"] = _matmul_tn(sact, d_out[None], "ffn2_dw_down")
    sync.swap("ffn2", gw)

    head_sum = (jnp.arange(GROUP_WIDTH)[:, None] // HEAD_DIM == jnp.arange(GROUP_WIDTH)[None, :] // HEAD_DIM).astype(F32)
    (*d_o_delta, dy, dgp, mix, dya, dys, ys2, gel, dglu, gs["gate_bias"]) = _mix_out_bwd(
        dx2, o_attn, y, gates, w["w_attn_branch"], w["ssm_w_glu"], w["w_ssm_branch"], w["w_out"], head_sum)
    sync.exchange("ffn2", later_than=[dy])
    d_o, delta = d_o_delta[0:3], d_o_delta[3:6]
    gw["w_out"] = _matmul_tn(mix[None], dx2[None], "dw_out")[0]
    gw["w_attn_branch"] = _matmul_tn(o_attn[None], dya[None], "dw_attn_branch")[0]
    gw["w_ssm_branch"] = _matmul_tn(ys2[None], dys[None], "dw_ssm_branch")[0]
    gw["ssm_w_glu"] = _matmul_tn(gel[None], dglu[None], "dw_glu")[0]

    dqs, dks, dvs, dsums = [], [], [], []
    for grp in range(3):
        dq, dk, dv, dsum = _attn_bwd(q[grp], k[grp], v[grp], d_o[grp], lse_tot[grp], delta[grp], bias[grp],
                                     f"attn_bwd_{grp}")
        dqs.append(dq)
        dks.append(dk)
        dvs.append(dv)
        dsums.append(dsum.reshape(HEADS_PER_GROUP, -1))
    dsum_all = jnp.pad(jnp.stack(dsums), ((0, 0), (0, 8 - HEADS_PER_GROUP), (0, 0)))
    d_table = _bias_reduce(dsum_all, onehot)[:, :HEADS_PER_GROUP]
    gs["rel_bias_table"] = d_table.reshape(3 * HEADS_PER_GROUP, N_BUCKETS).T

    du, gs["ssm_d"], d_bd, d_cdt, d_ab = _ssm_bwd(dy, u, s, bd, cdt, d_skip, pwr)
    sync.swap("mixer", gw, later_than=[du])
    sync.exchange("mixer", later_than=[dqs[2]])
    group_sum =(jnp.arange(NS)[:, None] // SSM_STATE == jnp.arange(128)[None, :]).astype(F32)
    d_are, d_aim, d_ldt, d_bre, d_bim, d_cre, d_cim = _disc_bwd(a_re, a_im, ldt, b_re, b_im, d_bd, d_cdt, d_ab, group_sum)
    gs["ssm_a_re"], gs["ssm_a_im"] = d_are, d_aim
    gs["ssm_log_dt"] = d_ldt[0, :SSM_GROUPS]
    from_cn = lambda t: t.reshape(SSM_GROUP, SSM_GROUPS, SSM_STATE).transpose(1, 2, 0)
    c_from_cn = lambda t: t.reshape(SSM_GROUP, SSM_GROUPS, SSM_STATE).transpose(1, 0, 2)
    gs["ssm_b_re"], gs["ssm_b_im"] = from_cn(d_bre), from_cn(d_bim)
    gs["ssm_c_re"], gs["ssm_c_im"] = c_from_cn(d_cre), c_from_cn(d_cim)

    dx1, hm, dz, gs["mix_norm"] = _mix_in_bwd(dx2, x1, nm, dqs + dks + dvs, du, dgp, w["w_in"])
    gw["w_in"] = _matmul_tn(dz[None], hm[None], "dw_in")[0]
    sync.swap("w_in", gw)

    dx0, da, db, sact, h, d_out, gs["ffn1_norm"] = _ffn_bwd(dx1, x, n1, a1, b1, w["ffn1_w_gate"], w["ffn1_w_up"],
                                                            w["ffn1_w_down"], "ffn1_bwd")
    sync.exchange("w_in", later_than=[dx0])
    gw["ffn1_w_gate"] = _matmul_tn(da, h[None], "ffn1_dw_gate")
    gw["ffn1_w_up"] = _matmul_tn(db, h[None], "ffn1_dw_up")
    sync.small_ready(gs, loss_blk, later_than=[gw["ffn1_w_up"]])
    gw["ffn1_w_down"] = _matmul_tn(sact, d_out[None], "ffn1_dw_down")
    sync.swap("ffn1", gw)
    return dx0


def kernel(x, ffn1_norm, ffn1_w_gate, ffn1_w_up, ffn1_w_down, mix_norm, w_in, gate_bias, rel_bias_table, ssm_a_re, ssm_a_im, ssm_log_dt, ssm_b_re, ssm_b_im, ssm_c_re, ssm_c_im, ssm_d, ssm_w_glu, w_attn_branch, w_ssm_branch, w_out, ffn2_norm, ffn2_w_gate, ffn2_w_up, ffn2_w_down, final_norm, loss_target, m_ffn1_norm, m_ffn1_w_gate, m_ffn1_w_up, m_ffn1_w_down, m_mix_norm, m_w_in, m_gate_bias, m_rel_bias_table, m_ssm_a_re, m_ssm_a_im, m_ssm_log_dt, m_ssm_b_re, m_ssm_b_im, m_ssm_c_re, m_ssm_c_im, m_ssm_d, m_ssm_w_glu, m_w_attn_branch, m_w_ssm_branch, m_w_out, m_ffn2_norm, m_ffn2_w_gate, m_ffn2_w_up, m_ffn2_w_down, m_final_norm, v_ffn1_norm, v_ffn1_w_gate, v_ffn1_w_up, v_ffn1_w_down, v_mix_norm, v_w_in, v_gate_bias, v_rel_bias_table, v_ssm_a_re, v_ssm_a_im, v_ssm_log_dt, v_ssm_b_re, v_ssm_b_im, v_ssm_c_re, v_ssm_c_im, v_ssm_d, v_ssm_w_glu, v_w_attn_branch, v_w_ssm_branch, v_w_out, v_ffn2_norm, v_ffn2_w_gate, v_ffn2_w_up, v_ffn2_w_down, v_final_norm):
    args = dict(locals())
    weights = {n: args[n] for n in ORDER}
    moms = {n: args["m_" + n] for n in ORDER}
    vels = {n: args["v_" + n] for n in ORDER}

    shard2d = {n: _shard_2d(n, weights[n]) for n in BIG}
    first, rest = BIG[:3], BIG[3:]
    full = dict(zip(first, _gather_weights([shard2d[n].astype(BF16) for n in first], "gather_ffn1_weights")))
    later = {n: shard2d[n].astype(BF16) for n in rest}

    small = {n: weights[n] for n in SMALL}
    sync = _GradSync(weights, moms, vels)
    grad_x = _local_step(x[0], loss_target[0], full, later, small, sync)
    sync.finish_all()
    return (sync.loss, grad_x[None], *[sync.grads[n] for n in ORDER], *[sync.delta[n] for n in ORDER],
            *[sync.new_m[n] for n in ORDER], *[sync.new_v[n] for n in ORDER])
```

```python
import functools
import math

import jax
import jax.numpy as jnp
from jax import lax
from jax.experimental import pallas as pl
from jax.experimental.pallas import tpu as pltpu
from jax.experimental.pallas import tpu_sc as plsc

F32 = jnp.float32
BF16 = jnp.bfloat16
MESH = pl.DeviceIdType.MESH

D_MODEL = 1024
D_FF = 2816
HEAD_DIM = 64
HEADS_PER_GROUP = 4
DILATIONS = (1, 4, 16)
WINDOW_STEPS = 128
ATTN_BLOCK = 128
ATTN_QB = 8
GROUP_WIDTH = HEADS_PER_GROUP * HEAD_DIM
ATTN_WIDTH = 3 * GROUP_WIDTH
N_BUCKETS = 32
MAX_DISTANCE = 2048
NEG_INF = -1e30
SSM_WIDTH = 512
SSM_GROUP = 16
SSM_GROUPS = 32
SSM_STATE = 64
NS = SSM_GROUPS * SSM_STATE
EPS = 1e-6
IN_WIDTH = 3 * ATTN_WIDTH + SSM_WIDTH + 2 * D_MODEL
Q_SCALE = HEAD_DIM ** -0.5
N_SHARD = 4
FF_SHARD = D_FF // N_SHARD
ADAM_LR, ADAM_B1, ADAM_B2, ADAM_EPS, ADAM_WD, ADAM_STEP = 0.001, 0.9, 0.999, 1e-08, 0.01, 10

LANES = 128
VMEM_LIMIT = 56 * 1024 * 1024
ROW_TILE = 512
FFN_BWD_TILE = 256
SSM_CHUNK = 256
SSM_FWD_CHUNK = 512
SCAN_LANES = 512
ADAMW_BLOCK_BYTES = 2 << 20
TN_VMEM_BUDGET = 40 * 1024 * 1024
REDUCE_GROUPS = {
    "ffn2": ["ffn2_w_gate", "ffn2_w_up", "ffn2_w_down"],
    "mixer": ["w_out", "w_attn_branch", "w_ssm_branch", "ssm_w_glu"],
    "w_in": ["w_in"],
    "ffn1": ["ffn1_w_gate", "ffn1_w_up", "ffn1_w_down"],
}
COLLECTIVE_IDS = {name: i for i, name in enumerate(
    ["gather", "gather_small"] + [stage + "_" + tag for tag in REDUCE_GROUPS for stage in ("swap", "exchange")])}


def _params(**kw):
    return pltpu.CompilerParams(vmem_limit_bytes=VMEM_LIMIT, **kw)


def _dot(a, b):
    return jnp.dot(a, b, preferred_element_type=F32)


def _dot_nt(a, b):
    return lax.dot_general(a, b, (((1,), (1,)), ((), ())), preferred_element_type=F32)


def _dot_tn(a, b):
    return lax.dot_general(a, b, (((0,), (0,)), ((), ())), preferred_element_type=F32)


def _dot_exact(a, b):
    return jnp.dot(a, b, preferred_element_type=F32, precision=lax.Precision.HIGHEST)


def _dot_nt_exact(a, b):
    return lax.dot_general(a, b, (((1,), (1,)), ((), ())), preferred_element_type=F32,
                           precision=lax.Precision.HIGHEST)


def _rms(x):
    r = lax.rsqrt(jnp.mean(x * x, axis=-1, keepdims=True) + EPS)
    return r, x * r


def _rms_bwd(dh, g, r, xhat):
    dxh = dh * g
    return r * (dxh - xhat * jnp.mean(dxh * xhat, axis=-1, keepdims=True))


def _sigmoid(x):
    return 0.5 + 0.5 * jnp.tanh(0.5 * x)


_GELU_C = math.sqrt(2.0 / math.pi)


def _gelu(x):
    return 0.5 * x * (1.0 + jnp.tanh(_GELU_C * (x + 0.044715 * x * x * x)))


def _gelu_grad(x):
    t = jnp.tanh(_GELU_C * (x + 0.044715 * x * x * x))
    return 0.5 * (1.0 + t) + 0.5 * x * (1.0 - t * t) * _GELU_C * (1.0 + 3 * 0.044715 * x * x)


def _whole():
    return pl.BlockSpec(memory_space=pltpu.VMEM)


def _row_tile(rows, cap):
    if rows <= cap:
        return rows
    return max(t for t in range(8, cap + 1, 8) if rows % t == 0)


def _rows(tm, w):
    return pl.BlockSpec((tm, w), lambda i: (i, 0))


def _acc_row(w):
    return pl.BlockSpec((1, w), lambda i: (0, 0))


def _ffn_fwd(x, g, wg, wu, wd, name, carried=()):
    L = x.shape[0]
    tm = min(ROW_TILE, L)
    n = len(carried)
    steps = L // tm

    def body(x_ref, g_ref, wg_ref, wu_ref, wd_ref, *refs):
        shard_refs, (xo_ref, a_ref, b_ref), full_refs, sems = refs[:n], refs[n:n + 3], refs[n + 3:2 * n + 3], refs[2 * n + 3:]
        if n:
            start, finish = _gather_parts([w.shape for w in carried], shard_refs, full_refs, *sems)
            pl.when(pl.program_id(0) == 0)(start)
        xv = x_ref[...]
        r, xhat = _rms(xv)
        h = (xhat * g_ref[...]).astype(BF16)
        acc = jnp.zeros((tm, D_MODEL), F32)
        for j in range(N_SHARD):
            a = _dot_nt(h, wg_ref[j])
            b = _dot_nt(h, wu_ref[j])
            a_ref[j] = a.astype(BF16)
            b_ref[j] = b.astype(BF16)
            s = (a * _sigmoid(a) * b).astype(BF16)
            acc = acc + _dot(s, wd_ref[j])
        xo_ref[...] = xv + 0.5 * acc
        if n:
            pl.when(pl.program_id(0) == steps - 1)(finish)

    act = pl.BlockSpec((N_SHARD, tm, FF_SHARD), lambda i: (0, i, 0))
    return pl.pallas_call(
        body, name=name, grid=(steps,),
        in_specs=[_rows(tm, D_MODEL), _whole(), _whole(), _whole(), _whole()] + [_ANY] * n,
        out_specs=[_rows(tm, D_MODEL), act, act] + [_ANY] * n,
        out_shape=[jax.ShapeDtypeStruct((L, D_MODEL), F32),
                   jax.ShapeDtypeStruct((N_SHARD, L, FF_SHARD), BF16),
                   jax.ShapeDtypeStruct((N_SHARD, L, FF_SHARD), BF16)]
        + [jax.ShapeDtypeStruct((N_SHARD,) + w.shape, w.dtype) for w in carried],
        scratch_shapes=[pltpu.SemaphoreType.DMA((7 * n,)), pltpu.SemaphoreType.DMA((7 * n,))] if n else [],
        compiler_params=_params(),
    )(x, g, wg, wu, wd, *carried)


def _ffn_bwd(dxo, x, g, a, b, wg, wu, wd, name):
    L = x.shape[0]
    tm = min(FFN_BWD_TILE, L)

    def body(dxo_ref, x_ref, g_ref, a_ref, b_ref, wg_ref, wu_ref, wd_ref,
             dxi_ref, da_ref, db_ref, s_ref, h_ref, do_ref, dg_ref):
        i = pl.program_id(0)
        xv = x_ref[...]
        gv = g_ref[...]
        r, xhat = _rms(xv)
        h_ref[...] = (xhat * gv).astype(BF16)
        dxo_v = dxo_ref[...]
        d_out = (0.5 * dxo_v).astype(BF16)
        do_ref[...] = d_out
        dh = jnp.zeros((tm, D_MODEL), F32)
        for j in range(N_SHARD):
            av = a_ref[j].astype(F32)
            bv = b_ref[j].astype(F32)
            sg = _sigmoid(av)
            sl = av * sg
            ds = _dot_nt(d_out, wd_ref[j])
            dbv = (ds * sl).astype(BF16)
            dav = (ds * bv * (sg * (1.0 + av * (1.0 - sg)))).astype(BF16)
            da_ref[j] = dav
            db_ref[j] = dbv
            s_ref[j] = (sl * bv).astype(BF16)
            dh = dh + _dot(dav, wg_ref[j]) + _dot(dbv, wu_ref[j])

        @pl.when(i == 0)
        def _():
            dg_ref[...] = jnp.zeros_like(dg_ref)

        dg_ref[...] += jnp.sum(dh * xhat, axis=0, keepdims=True)
        dxi_ref[...] = dxo_v + _rms_bwd(dh, gv, r, xhat)

    act = pl.BlockSpec((N_SHARD, tm, FF_SHARD), lambda i: (0, i, 0))
    act_shape = jax.ShapeDtypeStruct((N_SHARD, L, FF_SHARD), BF16)
    return pl.pallas_call(
        body, name=name, grid=(L // tm,),
        in_specs=[_rows(tm, D_MODEL), _rows(tm, D_MODEL), _whole(), act, act, _whole(), _whole(), _whole()],
        out_specs=[_rows(tm, D_MODEL), act, act, act, _rows(tm, D_MODEL), _rows(tm, D_MODEL), _acc_row(D_MODEL)],
        out_shape=[jax.ShapeDtypeStruct((L, D_MODEL), F32), act_shape, act_shape, act_shape,
                   jax.ShapeDtypeStruct((L, D_MODEL), BF16), jax.ShapeDtypeStruct((L, D_MODEL), BF16),
                   jax.ShapeDtypeStruct((1, D_MODEL), F32)],
        compiler_params=_params(),
    )(dxo, x, g, a, b, wg, wu, wd)


def _matmul_tn(a, b, name):
    ja, L, K = a.shape
    jb, _, N = b.shape
    J = max(ja, jb)
    splits = [s for s in (1, 2, 4, 8) if s == 1 or N % (s * LANES) == 0]
    nsplit = next((s for s in splits if 2 * K * (N // s) * 4 <= TN_VMEM_BUDGET // 2), splits[-1])
    nc = N // nsplit
    left = TN_VMEM_BUDGET - 2 * K * nc * 4
    row_bytes = 2 * (K * a.dtype.itemsize + nc * b.dtype.itemsize)
    tm = next((t for t in (4096, 2048, 1024, 512, 256) if L % t == 0 and t * row_bytes <= left), min(128, L))

    def body(a_ref, b_ref, o_ref):
        @pl.when(pl.program_id(2) == 0)
        def _():
            o_ref[...] = jnp.zeros_like(o_ref)

        o_ref[...] += _dot_tn(a_ref[...].astype(BF16), b_ref[...].astype(BF16))

    return pl.pallas_call(
        body, name=name, grid=(J, nsplit, L // tm),
        in_specs=[pl.BlockSpec((None, tm, K), (lambda j, s, i: (j, i, 0)) if ja > 1 else (lambda j, s, i: (0, i, 0))),
                  pl.BlockSpec((None, tm, nc), (lambda j, s, i: (j, i, s)) if jb > 1 else (lambda j, s, i: (0, i, s)))],
        out_specs=pl.BlockSpec((None, K, nc), lambda j, s, i: (j, 0, s)),
        out_shape=jax.ShapeDtypeStruct((J, K, N), F32),
        compiler_params=_params(),
    )(a, b)


def _loss_fwd_bwd(x, g, target):
    L = x.shape[0]
    tm = min(ROW_TILE, L)

    def body(x_ref, g_ref, t_ref, loss_ref, dx_ref, dg_ref):
        i = pl.program_id(0)
        xv = x_ref[...]
        gv = g_ref[...]
        r, xhat = _rms(xv)
        err = xhat * gv - t_ref[...]
        part = 0.5 * jnp.sum(jnp.sum(err * err, axis=1, keepdims=True) * (1.0 / D_MODEL), axis=0, keepdims=True)
        dy = err * (1.0 / D_MODEL)

        @pl.when(i == 0)
        def _():
            dg_ref[...] = jnp.zeros_like(dg_ref)
            loss_ref[...] = jnp.zeros_like(loss_ref)

        loss_ref[...] += jnp.broadcast_to(part, loss_ref.shape)
        dg_ref[...] += jnp.sum(dy * xhat, axis=0, keepdims=True)
        dx_ref[...] = _rms_bwd(dy, gv, r, xhat)

    return pl.pallas_call(
        body, name="loss_fwd_bwd", grid=(L // tm,),
        in_specs=[_rows(tm, D_MODEL), _whole(), _rows(tm, D_MODEL)],
        out_specs=[pl.BlockSpec((8, 128), lambda i: (0, 0)), _rows(tm, D_MODEL), _acc_row(D_MODEL)],
        out_shape=[jax.ShapeDtypeStruct((8, 128), F32), jax.ShapeDtypeStruct((L, D_MODEL), F32),
                   jax.ShapeDtypeStruct((1, D_MODEL), F32)],
        compiler_params=_params(),
    )(x, g, target)


_C_K = ATTN_WIDTH
_C_V = 2 * ATTN_WIDTH
_C_U = 3 * ATTN_WIDTH
_C_G = _C_U + SSM_WIDTH


def _residue_spec(d, tm):
    return pl.BlockSpec((d, tm // d, GROUP_WIDTH), lambda i: (0, i, 0))


def _residue_shape(d, L, dtype):
    return jax.ShapeDtypeStruct((d, L // d, GROUP_WIDTH), dtype)


def _residue_scratch(tm):
    return pltpu.VMEM((GROUP_WIDTH // LANES, tm, LANES), F32)


def _to_residues(val, out_ref, scr, d):
    if d == 1:
        out_ref[0] = val.astype(out_ref.dtype)
        return
    tm = val.shape[0]
    for half in range(GROUP_WIDTH // LANES):
        cols = slice(half * LANES, (half + 1) * LANES)
        scr[half] = val[:, cols]
        for r in range(d):
            out_ref[r, :, cols] = scr[half, pl.ds(r, tm // d, stride=d), :].astype(out_ref.dtype)


def _from_residues(ref, scr, d):
    if d == 1:
        return ref[0].astype(F32)
    rows = ref.shape[1]
    for half in range(GROUP_WIDTH // LANES):
        cols = slice(half * LANES, (half + 1) * LANES)
        for r in range(d):
            scr[half, pl.ds(r, rows, stride=d), :] = ref[r, :, cols].astype(F32)
    return jnp.concatenate([scr[half] for half in range(GROUP_WIDTH // LANES)], axis=1)


def _mix_in_fwd(x, g, w_in, gate_bias):
    L = x.shape[0]
    tm = min(ROW_TILE, L)

    def body(x_ref, g_ref, w_ref, gb_ref, *refs):
        qkv_refs, (u_ref, gate_ref, scr) = refs[:9], refs[9:]
        r, xhat = _rms(x_ref[...])
        h = (xhat * g_ref[...]).astype(BF16)
        for part, (c0, scale) in enumerate(((0, Q_SCALE), (_C_K, 1.0), (_C_V, 1.0))):
            z = _dot_nt(h, w_ref[c0:c0 + ATTN_WIDTH, :]) * scale
            for grp, d in enumerate(DILATIONS):
                _to_residues(z[:, grp * GROUP_WIDTH:(grp + 1) * GROUP_WIDTH], qkv_refs[3 * part + grp], scr, d)
        u_ref[...] = _dot_nt(h, w_ref[_C_U:_C_G, :])
        gate_ref[...] = _sigmoid(_dot_nt(h, w_ref[_C_G:IN_WIDTH, :]) + gb_ref[...])

    return pl.pallas_call(
        body, name="mix_in_fwd", grid=(L // tm,),
        in_specs=[_rows(tm, D_MODEL), _whole(), _whole(), _whole()],
        out_specs=[_residue_spec(d, tm) for d in DILATIONS] * 3 + [_rows(tm, SSM_WIDTH), _rows(tm, 2 * D_MODEL)],
        out_shape=[_residue_shape(d, L, BF16) for d in DILATIONS] * 3
        + [jax.ShapeDtypeStruct((L, SSM_WIDTH), F32), jax.ShapeDtypeStruct((L, 2 * D_MODEL), F32)],
        scratch_shapes=[_residue_scratch(tm)],
        compiler_params=_params(),
    )(x, g, w_in, gate_bias)


def _mix_in_bwd(dx2, x, g, dqkv, du, dgp, w_in):
    L = x.shape[0]
    tm = min(ROW_TILE, L)

    def body(dx2_ref, x_ref, g_ref, *refs):
        piece_refs = refs[:9]
        du_ref, dgp_ref, w_ref, dx1_ref, h_ref, dz_ref, dg_ref, scr = refs[9:]
        i = pl.program_id(0)
        gv = g_ref[...]
        r, xhat = _rms(x_ref[...])
        h_ref[...] = (xhat * gv).astype(BF16)
        for part in range(3):
            for grp, d in enumerate(DILATIONS):
                c0 = part * ATTN_WIDTH + grp * GROUP_WIDTH
                dz_ref[:, c0:c0 + GROUP_WIDTH] = _from_residues(piece_refs[3 * part + grp], scr, d).astype(BF16)
        dz_ref[:, _C_U:_C_G] = du_ref[...].astype(BF16)
        dz_ref[:, _C_G:IN_WIDTH] = dgp_ref[...]
        dh = _dot(dz_ref[...], w_ref[...])

        @pl.when(i == 0)
        def _():
            dg_ref[...] = jnp.zeros_like(dg_ref)

        dg_ref[...] += jnp.sum(dh * xhat, axis=0, keepdims=True)
        dx1_ref[...] = dx2_ref[...] + _rms_bwd(dh, gv, r, xhat)

    return pl.pallas_call(
        body, name="mix_in_bwd", grid=(L // tm,),
        in_specs=[_rows(tm, D_MODEL), _rows(tm, D_MODEL), _whole()] + [_residue_spec(d, tm) for d in DILATIONS] * 3
        + [_rows(tm, SSM_WIDTH), _rows(tm, 2 * D_MODEL), _whole()],
        out_specs=[_rows(tm, D_MODEL), _rows(tm, D_MODEL), _rows(tm, IN_WIDTH), _acc_row(D_MODEL)],
        out_shape=[jax.ShapeDtypeStruct((L, D_MODEL), F32), jax.ShapeDtypeStruct((L, D_MODEL), BF16),
                   jax.ShapeDtypeStruct((L, IN_WIDTH), BF16), jax.ShapeDtypeStruct((1, D_MODEL), F32)],
        scratch_shapes=[_residue_scratch(tm)],
        compiler_params=_params(),
    )(dx2, x, g, *dqkv, du, dgp, w_in)


def _bucket_onehot():
    qi = jnp.arange(ATTN_BLOCK)[:, None]
    kj = jnp.arange(2 * ATTN_BLOCK)[None, :]
    steps = jnp.maximum(qi + ATTN_BLOCK - kj, 0)
    max_exact = N_BUCKETS // 2
    out = []
    for d in DILATIONS:
        dist = steps * d
        df = jnp.maximum(dist, 1).astype(F32)
        large = max_exact + (jnp.log(df / max_exact) / math.log(MAX_DISTANCE / max_exact)
                             * (N_BUCKETS - max_exact)).astype(jnp.int32)
        large = jnp.minimum(large, N_BUCKETS - 1)
        bucket = jnp.where(dist < max_exact, dist, large).reshape(-1)
        out.append((bucket[None, :] == jnp.arange(N_BUCKETS)[:, None]).astype(F32))
    return jnp.stack(out)


def _bias_expand(table_t, onehot):
    n = onehot.shape[-1]

    def body(t_ref, oh_ref, o_ref):
        bias = _dot_exact(t_ref[...], oh_ref[...])
        col = lax.broadcasted_iota(jnp.int32, (8, n), 1)
        qi = col // (2 * ATTN_BLOCK)
        kj = col - qi * (2 * ATTN_BLOCK)
        steps = qi + ATTN_BLOCK - kj
        band = (steps >= 0) & (steps <= WINDOW_STEPS)
        o_ref[0] = jnp.where(band & (kj >= ATTN_BLOCK), bias, NEG_INF)
        o_ref[1] = jnp.where(band, bias, NEG_INF)

    return pl.pallas_call(
        body, name="bias_expand", grid=(3,),
        in_specs=[pl.BlockSpec((None, 8, N_BUCKETS), lambda g: (g, 0, 0)),
                  pl.BlockSpec((None, N_BUCKETS, n), lambda g: (g, 0, 0))],
        out_specs=pl.BlockSpec((None, 2, 8, n), lambda g: (g, 0, 0, 0)),
        out_shape=jax.ShapeDtypeStruct((3, 2, 8, n), F32),
        compiler_params=_params(),
    )(table_t, onehot)


def _bias_reduce(dsum, onehot):
    n = onehot.shape[-1]

    def body(d_ref, oh_ref, o_ref):
        o_ref[...] = _dot_nt_exact(d_ref[...], oh_ref[...])

    return pl.pallas_call(
        body, name="bias_reduce", grid=(3,),
        in_specs=[pl.BlockSpec((None, 8, n), lambda g: (g, 0, 0)),
                  pl.BlockSpec((None, N_BUCKETS, n), lambda g: (g, 0, 0))],
        out_specs=pl.BlockSpec((None, 8, N_BUCKETS), lambda g: (g, 0, 0)),
        out_shape=jax.ShapeDtypeStruct((3, 8, N_BUCKETS), F32),
        compiler_params=_params(),
    )(dsum, onehot)


def _head_of_col(rows):
    return lax.broadcasted_iota(jnp.int32, (rows, GROUP_WIDTH), 1) // HEAD_DIM


_STACK_ROWS = HEADS_PER_GROUP * ATTN_BLOCK


def _stack_heads(x, head_of_col):
    return jnp.concatenate([jnp.where(head_of_col == hh, x, jnp.zeros_like(x)) for hh in range(HEADS_PER_GROUP)],
                           axis=0)


def _attn_specs(qb):
    rows = qb * ATTN_BLOCK
    cur = pl.BlockSpec((None, rows, GROUP_WIDTH), lambda r, n: (r, n, 0))
    prev = pl.BlockSpec((None, ATTN_BLOCK, GROUP_WIDTH), lambda r, n: (r, jnp.maximum(n * qb - 1, 0), 0))
    bias = pl.BlockSpec((2, HEADS_PER_GROUP, ATTN_BLOCK, 2 * ATTN_BLOCK), lambda r, n: (0, 0, 0, 0))
    return cur, prev, bias


def _attn_fwd(q, k, v, bias, name):
    d, M, _ = q.shape
    nb = M // ATTN_BLOCK
    qb = min(ATTN_QB, nb)

    def body(q_ref, kp_ref, kc_ref, vp_ref, vc_ref, bias_ref, o_ref, lse_ref):
        n = pl.program_id(1)
        q_head = _head_of_col(ATTN_BLOCK)
        kwin = jnp.concatenate([kp_ref[...], kc_ref[...]], axis=0)
        vwin = jnp.concatenate([vp_ref[...], vc_ref[...]], axis=0)
        ones = jnp.ones((2 * ATTN_BLOCK, LANES), BF16)
        for b in range(qb):
            rows = slice(b * ATTN_BLOCK, (b + 1) * ATTN_BLOCK)
            window = slice(b * ATTN_BLOCK, (b + 2) * ATTN_BLOCK)
            variant = jnp.minimum(n, 1) if b == 0 else 1
            kk = kwin[window]
            vv = vwin[window]
            q4 = _stack_heads(q_ref[rows, :], q_head)
            logits = _dot_nt(q4, kk) + bias_ref[variant].reshape(_STACK_ROWS, 2 * ATTN_BLOCK)
            m = jnp.max(logits, axis=1, keepdims=True)
            p16 = jnp.exp(logits - m).astype(BF16)
            den = _dot(p16, ones)[:, 0:1]
            out = _dot(p16, vv) * (1.0 / den)
            lse = m + jnp.log(den)
            o_acc = jnp.zeros((ATTN_BLOCK, GROUP_WIDTH), F32)
            lse_acc = jnp.zeros((ATTN_BLOCK, GROUP_WIDTH), F32)
            for hh in range(HEADS_PER_GROUP):
                head_rows = slice(hh * ATTN_BLOCK, (hh + 1) * ATTN_BLOCK)
                o_acc = jnp.where(q_head == hh, out[head_rows], o_acc)
                lse_acc = jnp.where(q_head == hh, lse[head_rows], lse_acc)
            o_ref[rows, :] = o_acc
            lse_ref[rows, :] = lse_acc

    cur, prev, full = _attn_specs(qb)
    return pl.pallas_call(
        body, name=name, grid=(d, nb // qb),
        in_specs=[cur, prev, cur, prev, cur, full],
        out_specs=[cur, cur],
        out_shape=[jax.ShapeDtypeStruct((d, M, GROUP_WIDTH), F32)] * 2,
        compiler_params=_params(),
    )(q, k, k, v, v, bias)


def _attn_bwd(q, k, v, do, lse, delta, bias, name):
    d, M, _ = q.shape
    nb = M // ATTN_BLOCK
    qb = min(ATTN_QB, nb)
    ns = nb // qb
    rows_q = qb * ATTN_BLOCK
    last = slice(rows_q - ATTN_BLOCK, rows_q)

    def body(q_ref, kp_ref, kc_ref, vp_ref, vc_ref, do_ref, lse_ref, dl_ref, bias_ref,
             dq_ref, dk_ref, dv_ref, dsum_ref, pk_ref, pv_ref, wk_ref, wv_ref):
        r = pl.program_id(0)
        n = pl.program_id(1)

        @pl.when((r == 0) & (n == 0))
        def _():
            dsum_ref[...] = jnp.zeros_like(dsum_ref)

        @pl.when(n == 0)
        def _():
            pk_ref[...] = jnp.zeros_like(pk_ref)
            pv_ref[...] = jnp.zeros_like(pv_ref)

        @pl.when(n < ns)
        def _():
            q_head = _head_of_col(ATTN_BLOCK)
            kwin = jnp.concatenate([kp_ref[...], kc_ref[...]], axis=0)
            vwin = jnp.concatenate([vp_ref[...], vc_ref[...]], axis=0)
            wk_ref[...] = jnp.zeros_like(wk_ref)
            wv_ref[...] = jnp.zeros_like(wv_ref)
            for b in range(qb):
                rows = slice(b * ATTN_BLOCK, (b + 1) * ATTN_BLOCK)
                window = slice(b * ATTN_BLOCK, (b + 2) * ATTN_BLOCK)
                variant = jnp.minimum(n, 1) if b == 0 else 1
                kk = kwin[window]
                vv = vwin[window]
                q4 = _stack_heads(q_ref[rows, :], q_head)
                do4 = _stack_heads(do_ref[rows, :], q_head)
                heads = [hh * HEAD_DIM for hh in range(HEADS_PER_GROUP)]
                lse4 = jnp.concatenate([lse_ref[rows, c0:c0 + 1] for c0 in heads], axis=0)
                dl4 = jnp.concatenate([dl_ref[rows, c0:c0 + 1] for c0 in heads], axis=0)
                logits = _dot_nt(q4, kk) + bias_ref[variant].reshape(_STACK_ROWS, 2 * ATTN_BLOCK)
                p = jnp.exp(logits - lse4)
                ds = p * (_dot_nt(do4, vv) - dl4)
                dsum_ref[...] += ds.reshape(HEADS_PER_GROUP, ATTN_BLOCK, 2 * ATTN_BLOCK)
                ds16 = ds.astype(BF16)
                dq4 = _dot(ds16, kk)
                dq_acc = jnp.zeros((ATTN_BLOCK, GROUP_WIDTH), F32)
                for hh in range(HEADS_PER_GROUP):
                    dq_acc = jnp.where(q_head == hh, dq4[hh * ATTN_BLOCK:(hh + 1) * ATTN_BLOCK], dq_acc)
                dq_ref[rows, :] = (dq_acc * Q_SCALE).astype(BF16)
                wk_ref[window, :] += _dot_tn(ds16, q4)
                wv_ref[window, :] += _dot_tn(p.astype(BF16), do4)
            for out_ref, part_ref, win_ref in ((dk_ref, pk_ref, wk_ref), (dv_ref, pv_ref, wv_ref)):
                if ns == 1:
                    out_ref[...] = win_ref[ATTN_BLOCK:, :].astype(BF16)
                    continue
                if qb > 1:
                    out_ref[0:rows_q - ATTN_BLOCK, :] = part_ref[0:rows_q - ATTN_BLOCK, :].astype(BF16)
                out_ref[last, :] = (part_ref[last, :] + win_ref[0:ATTN_BLOCK, :]).astype(BF16)
                part_ref[...] = win_ref[ATTN_BLOCK:, :]

        if ns > 1:
            @pl.when(n == ns)
            def _():
                dk_ref[...] = pk_ref[...].astype(BF16)
                dv_ref[...] = pv_ref[...].astype(BF16)

    def clamp(n):
        return jnp.minimum(n, ns - 1)

    cur = pl.BlockSpec((None, rows_q, GROUP_WIDTH), lambda r, n: (r, clamp(n), 0))
    prev = pl.BlockSpec((None, ATTN_BLOCK, GROUP_WIDTH), lambda r, n: (r, jnp.maximum(clamp(n) * qb - 1, 0), 0))
    lag = pl.BlockSpec((None, rows_q, GROUP_WIDTH), lambda r, n: (r, jnp.maximum(n - 1, 0), 0))
    full = pl.BlockSpec((2, HEADS_PER_GROUP, ATTN_BLOCK, 2 * ATTN_BLOCK), lambda r, n: (0, 0, 0, 0))
    acc = pl.BlockSpec((HEADS_PER_GROUP, ATTN_BLOCK, 2 * ATTN_BLOCK), lambda r, n: (0, 0, 0))
    return pl.pallas_call(
        body, name=name, grid=(d, ns + 1 if ns > 1 else 1),
        in_specs=[cur, prev, cur, prev, cur, cur, cur, cur, full],
        out_specs=[cur, lag, lag, acc],
        out_shape=[jax.ShapeDtypeStruct((d, M, GROUP_WIDTH), BF16)] * 3
        + [jax.ShapeDtypeStruct((HEADS_PER_GROUP, ATTN_BLOCK, 2 * ATTN_BLOCK), F32)],
        scratch_shapes=[pltpu.VMEM((rows_q, GROUP_WIDTH), F32), pltpu.VMEM((rows_q, GROUP_WIDTH), F32),
                        pltpu.VMEM((rows_q + ATTN_BLOCK, GROUP_WIDTH), F32),
                        pltpu.VMEM((rows_q + ATTN_BLOCK, GROUP_WIDTH), F32)],
        compiler_params=_params(),
    )(q, k, k, v, v, do, lse, delta, bias)


def _disc_math(a_re, a_im, ldt, b_re, b_im):
    dt = jnp.exp(ldt)
    mag = jnp.exp(a_re * dt)
    ab_re = mag * jnp.cos(a_im * dt)
    ab_im = mag * jnp.sin(a_im * dt)
    den = a_re * a_re + a_im * a_im
    xr = ab_re - 1.0
    coef_re = (xr * a_re + ab_im * a_im) / den
    coef_im = (ab_im * a_re - xr * a_im) / den
    return ab_re, ab_im, coef_re * b_re - coef_im * b_im, coef_re * b_im + coef_im * b_re


def _block_diag_mask():
    row_g = lax.broadcasted_iota(jnp.int32, (SSM_WIDTH, 2 * NS), 0) // SSM_GROUP
    col = lax.broadcasted_iota(jnp.int32, (SSM_WIDTH, 2 * NS), 1)
    col_g = jnp.where(col >= NS, col - NS, col) // SSM_STATE
    return row_g == col_g


def _disc_fwd(a_re, a_im, ldt, b_re, b_im, c_re, c_im):
    def body(are_ref, aim_ref, ldt_ref, bre_ref, bim_ref, cre_ref, cim_ref, pw_ref, pwr_ref, bd_ref, cdt_ref):
        ab_re, ab_im, bb_re, bb_im = _disc_math(are_ref[...], aim_ref[...], ldt_ref[...], bre_ref[...], bim_ref[...])
        row = lax.broadcasted_iota(jnp.int32, (8, NS), 0)
        pr, pi = ab_re, ab_im
        t_re = jnp.zeros((8, NS), F32)
        t_im = jnp.zeros((8, NS), F32)
        u_re = jnp.zeros((8, NS), F32)
        u_im = jnp.zeros((8, NS), F32)
        for j in range(8):
            t_re = jnp.where(row == j, pr, t_re)
            t_im = jnp.where(row == j, pi, t_im)
            u_re = jnp.where(row == 7 - j, pr, u_re)
            u_im = jnp.where(row == 7 - j, pi, u_im)
            pr, pi = pr * ab_re - pi * ab_im, pr * ab_im + pi * ab_re
        pw_ref[0] = t_re
        pw_ref[1] = t_im
        pwr_ref[0] = u_re
        pwr_ref[1] = u_im
        mask = _block_diag_mask()
        zero = jnp.zeros((SSM_WIDTH, 2 * NS), F32)
        bfull = jnp.concatenate([jnp.concatenate([bb_re] * SSM_GROUPS, axis=0),
                                 jnp.concatenate([bb_im] * SSM_GROUPS, axis=0)], axis=1)
        bd_ref[...] = jnp.where(mask, bfull, zero).astype(BF16)
        cfull = jnp.concatenate([jnp.concatenate([cre_ref[...]] * SSM_GROUPS, axis=0),
                                 jnp.concatenate([-cim_ref[...]] * SSM_GROUPS, axis=0)], axis=1)
        cdt_ref[...] = jnp.where(mask, cfull, zero).astype(BF16)

    return pl.pallas_call(
        body, name="s5_disc_fwd",
        in_specs=[_whole()] * 7, out_specs=[_whole()] * 4,
        out_shape=[jax.ShapeDtypeStruct((2, 8, NS), F32), jax.ShapeDtypeStruct((2, 8, NS), F32),
                   jax.ShapeDtypeStruct((SSM_WIDTH, 2 * NS), BF16), jax.ShapeDtypeStruct((SSM_WIDTH, 2 * NS), BF16)],
        compiler_params=_params(),
    )(a_re, a_im, ldt, b_re, b_im, c_re, c_im)


def _disc_bwd(a_re, a_im, ldt, b_re, b_im, d_bd, d_cdt, d_ab, group_sum):
    def body(are_ref, aim_ref, ldt_ref, bre_ref, bim_ref, dbd_ref, dcdt_ref, dab_ref, gs_ref,
             dare_ref, daim_ref, dldt_ref, dbre_ref, dbim_ref, dcre_ref, dcim_ref):
        col = lax.broadcasted_iota(jnp.int32, (SSM_GROUP, 2 * NS), 1)
        col_g = jnp.where(col >= NS, col - NS, col) // SSM_STATE
        acc_b = jnp.zeros((SSM_GROUP, 2 * NS), F32)
        acc_c = jnp.zeros((SSM_GROUP, 2 * NS), F32)
        for g in range(SSM_GROUPS):
            rows = slice(g * SSM_GROUP, (g + 1) * SSM_GROUP)
            acc_b = acc_b + jnp.where(col_g == g, dbd_ref[rows, :], 0.0)
            acc_c = acc_c + jnp.where(col_g == g, dcdt_ref[rows, :], 0.0)
        dcre_ref[...] = acc_c[:, :NS]
        dcim_ref[...] = -acc_c[:, NS:]
        dab_re = jnp.sum(dab_ref[0], axis=0, keepdims=True)
        dab_im = jnp.sum(dab_ref[1], axis=0, keepdims=True)
        _, vjp = jax.vjp(_disc_math, are_ref[...], aim_ref[...], ldt_ref[...], bre_ref[...], bim_ref[...])
        d_are, d_aim, d_ldt, d_bre, d_bim = vjp((dab_re, dab_im, acc_b[:, :NS], acc_b[:, NS:]))
        dare_ref[...] = d_are
        daim_ref[...] = d_aim
        dbre_ref[...] = d_bre
        dbim_ref[...] = d_bim
        dldt_ref[...] = _dot_exact(jnp.broadcast_to(d_ldt, (8, NS)), gs_ref[...])

    vec = jax.ShapeDtypeStruct((1, NS), F32)
    mat = jax.ShapeDtypeStruct((SSM_GROUP, NS), F32)
    return pl.pallas_call(
        body, name="s5_disc_bwd",
        in_specs=[_whole()] * 9, out_specs=[_whole()] * 7,
        out_shape=[vec, vec, jax.ShapeDtypeStruct((8, 128), F32), mat, mat, mat, mat],
        compiler_params=_params(),
    )(a_re, a_im, ldt, b_re, b_im, d_bd, d_cdt, d_ab, group_sum)


def _scan_blocks(buf, pw_ref, carry_ref, n_blocks, reverse):
    row = lax.broadcasted_iota(jnp.int32, (8, SCAN_LANES), 0)
    for lc in range(NS // SCAN_LANES):
        re_cols = pl.ds(lc * SCAN_LANES, SCAN_LANES)
        im_cols = pl.ds(NS + lc * SCAN_LANES, SCAN_LANES)
        pr = pw_ref[0, :, re_cols]
        pi = pw_ref[1, :, re_cols]
        if reverse:
            pi = -pi
            base = [(7, 1), (6, 2), (4, 4)]
            coef = [(jnp.where(row < 8 - k, pr[j:j + 1], 0.0), jnp.where(row < 8 - k, pi[j:j + 1], 0.0), 8 - k)
                    for j, k in base]
        else:
            base = [(0, 1), (1, 2), (3, 4)]
            coef = [(jnp.where(row >= k, pr[j:j + 1], 0.0), jnp.where(row >= k, pi[j:j + 1], 0.0), k)
                    for j, k in base]

        def step(i, carry, pr=pr, pi=pi, coef=coef, re_cols=re_cols, im_cols=im_cols):
            cr, ci = carry
            blk = (n_blocks - 1 - i) if reverse else i
            rows = pl.ds(pl.multiple_of(blk * 8, 8), 8)
            xr = buf[rows, re_cols]
            xi = buf[rows, im_cols]
            for kr, ki, shift in coef:
                sr = pltpu.roll(xr, shift, 0)
                si = pltpu.roll(xi, shift, 0)
                xr, xi = xr + kr * sr - ki * si, xi + kr * si + ki * sr
            xr, xi = xr + pr * cr - pi * ci, xi + pr * ci + pi * cr
            buf[rows, re_cols] = xr
            buf[rows, im_cols] = xi
            edge = slice(0, 1) if reverse else slice(7, 8)
            return xr[edge], xi[edge]

        cr, ci = lax.fori_loop(0, n_blocks, step, (carry_ref[0:1, re_cols], carry_ref[0:1, im_cols]))
        carry_ref[0:1, re_cols] = cr
        carry_ref[0:1, im_cols] = ci


_SUPER_GROUPS = 16
_SUPER_BLOCKS = [
    (slice(k * _SUPER_GROUPS * SSM_GROUP, (k + 1) * _SUPER_GROUPS * SSM_GROUP),
     [slice(half + k * _SUPER_GROUPS * SSM_STATE, half + (k + 1) * _SUPER_GROUPS * SSM_STATE) for half in (0, NS)])
    for k in range(SSM_GROUPS // _SUPER_GROUPS)]


def _ssm_fwd(u, bd, cdt, d_skip, pw):
    L = u.shape[0]
    tc = min(SSM_FWD_CHUNK, L)

    def body(u_ref, bd_ref, cdt_ref, dsk_ref, pw_ref, y_ref, s_ref, carry_ref):
        @pl.when(pl.program_id(0) == 0)
        def _():
            carry_ref[...] = jnp.zeros_like(carry_ref)

        uv = u_ref[...]
        u16 = uv.astype(BF16)
        for ch, states in _SUPER_BLOCKS:
            for st in states:
                s_ref[:, st] = _dot(u16[:, ch], bd_ref[ch, st])
        _scan_blocks(s_ref, pw_ref, carry_ref, tc // 8, reverse=False)
        for ch, states in _SUPER_BLOCKS:
            y_ref[:, ch] = (sum(_dot_nt(s_ref[:, st].astype(BF16), cdt_ref[ch, st]) for st in states)
                            + dsk_ref[:, ch] * uv[:, ch])

    return pl.pallas_call(
        body, name="s5_fwd", grid=(L // tc,),
        in_specs=[_rows(tc, SSM_WIDTH), _whole(), _whole(), _whole(), _whole()],
        out_specs=[_rows(tc, SSM_WIDTH), _rows(tc, 2 * NS)],
        out_shape=[jax.ShapeDtypeStruct((L, SSM_WIDTH), F32), jax.ShapeDtypeStruct((L, 2 * NS), F32)],
        scratch_shapes=[pltpu.VMEM((8, 2 * NS), F32)],
        compiler_params=_params(),
    )(u, bd, cdt, d_skip, pw)


def _ssm_bwd(dy, u, s, bd, cdt, d_skip, pwr):
    L = u.shape[0]
    tc = min(SSM_CHUNK, L)
    nc = L // tc
    blocks = tc // 8

    def body(dy_ref, u_ref, s_ref, sprev_ref, bd_ref, cdt_ref, dsk_ref, pwr_ref,
             du_ref, ddsk_ref, dbd_ref, dcdt_ref, dab_ref, g_ref, sx_ref, carry_ref):
        i = pl.program_id(0)

        @pl.when(i == 0)
        def _():
            carry_ref[...] = jnp.zeros_like(carry_ref)
            ddsk_ref[...] = jnp.zeros_like(ddsk_ref)
            dbd_ref[...] = jnp.zeros_like(dbd_ref)
            dcdt_ref[...] = jnp.zeros_like(dcdt_ref)
            dab_ref[...] = jnp.zeros_like(dab_ref)

        dyv = dy_ref[...]
        uv = u_ref[...]
        dy16 = dyv.astype(BF16)
        u16 = uv.astype(BF16)
        for ch, states in _SUPER_BLOCKS:
            for st in states:
                g_ref[:, st] = _dot(dy16[:, ch], cdt_ref[ch, st])
        _scan_blocks(g_ref, pwr_ref, carry_ref, blocks, reverse=True)
        ddsk_ref[...] += jnp.sum(dyv * uv, axis=0, keepdims=True)
        for ch, states in _SUPER_BLOCKS:
            du = dsk_ref[:, ch] * dyv[:, ch]
            for st in states:
                g16 = g_ref[:, st].astype(BF16)
                du = du + _dot_nt(g16, bd_ref[ch, st])
                dbd_ref[ch, st] += _dot_tn(u16[:, ch], g16)
                dcdt_ref[ch, st] += _dot_tn(dy16[:, ch], s_ref[:, st].astype(BF16))
            du_ref[:, ch] = du

        sx_ref[pl.ds(8, tc), :] = s_ref[...]
        sx_ref[pl.ds(0, 8), :] = jnp.where(i == nc - 1, 0.0, sprev_ref[...])
        row = lax.broadcasted_iota(jnp.int32, (8, SCAN_LANES), 0)
        for lc in range(NS // SCAN_LANES):
            re_cols = pl.ds(lc * SCAN_LANES, SCAN_LANES)
            im_cols = pl.ds(NS + lc * SCAN_LANES, SCAN_LANES)

            def step(b, acc, re_cols=re_cols, im_cols=im_cols):
                ar, ai = acc
                off = pl.multiple_of(b * 8, 8)
                gr = g_ref[pl.ds(off, 8), re_cols]
                gi = g_ref[pl.ds(off, 8), im_cols]
                before = pl.ds(off, 8)
                here = pl.ds(off + 8, 8)
                sr = jnp.where(row == 0, sx_ref[before, re_cols][7:8], pltpu.roll(sx_ref[here, re_cols], 1, 0))
                si = jnp.where(row == 0, sx_ref[before, im_cols][7:8], pltpu.roll(sx_ref[here, im_cols], 1, 0))
                return ar + gr * sr + gi * si, ai + gi * sr - gr * si

            zero = jnp.zeros((8, SCAN_LANES), F32)
            ar, ai = lax.fori_loop(0, blocks, step, (zero, zero))
            dab_ref[0, :, re_cols] += ar
            dab_ref[1, :, re_cols] += ai

    rev = lambda i: (nc - 1 - i, 0)
    sprev = pl.BlockSpec((8, 2 * NS), lambda i: (jnp.maximum((nc - 1 - i) * blocks - 1, 0), 0))
    return pl.pallas_call(
        body, name="s5_bwd", grid=(nc,),
        in_specs=[pl.BlockSpec((tc, SSM_WIDTH), rev), pl.BlockSpec((tc, SSM_WIDTH), rev),
                  pl.BlockSpec((tc, 2 * NS), rev), sprev, _whole(), _whole(), _whole(), _whole()],
        out_specs=[pl.BlockSpec((tc, SSM_WIDTH), rev), _whole(), _whole(), _whole(), _whole()],
        out_shape=[jax.ShapeDtypeStruct((L, SSM_WIDTH), F32), jax.ShapeDtypeStruct((1, SSM_WIDTH), F32),
                   jax.ShapeDtypeStruct((SSM_WIDTH, 2 * NS), F32), jax.ShapeDtypeStruct((SSM_WIDTH, 2 * NS), F32),
                   jax.ShapeDtypeStruct((2, 8, NS), F32)],
        scratch_shapes=[pltpu.VMEM((tc, 2 * NS), F32), pltpu.VMEM((tc + 8, 2 * NS), F32), pltpu.VMEM((8, 2 * NS), F32)],
        compiler_params=_params(),
    )(dy, u, s, s, bd, cdt, d_skip, pwr)


def _branches(o_attn, y, gates, w_ab, w_glu, w_sb):
    ya = _dot(o_attn.astype(BF16), w_ab[...])
    gel = _gelu(y)
    glu = _dot(gel.astype(BF16), w_glu[...])
    p = glu[:, :SSM_WIDTH]
    sg = _sigmoid(glu[:, SSM_WIDTH:])
    ys2 = p * sg
    ysb = _dot(ys2.astype(BF16), w_sb[...])
    ga = gates[:, :D_MODEL]
    gs = gates[:, D_MODEL:]
    return ya, gel, p, sg, ys2, ysb, ga, gs


def _mix_out_fwd(x1, o_g, lse_g, y, gates, w_ab, w_glu, w_sb, w_out):
    L = x1.shape[0]
    tm = min(ROW_TILE, L)

    def body(x_ref, o0, o1, o2, l0, l1, l2, y_ref, gate_ref, wab_ref, wglu_ref, wsb_ref, wout_ref,
             x2_ref, oat_ref, lse0, lse1, lse2, scr):
        la, lb, lc = (_from_residues(ref, scr, d) for ref, d in zip((l0, l1, l2), DILATIONS))
        m = jnp.maximum(jnp.maximum(la, lb), lc)
        ea, eb, ec = jnp.exp(la - m), jnp.exp(lb - m), jnp.exp(lc - m)
        tot = ea + eb + ec
        oa, ob, oc = (_from_residues(ref, scr, d) for ref, d in zip((o0, o1, o2), DILATIONS))
        o_attn = (ea * oa + eb * ob + ec * oc) / tot
        oat_ref[...] = o_attn
        lse = m + jnp.log(tot)
        for ref, d in zip((lse0, lse1, lse2), DILATIONS):
            _to_residues(lse, ref, scr, d)
        ya, _, _, _, _, ysb, ga, gs = _branches(o_attn, y_ref[...], gate_ref[...], wab_ref, wglu_ref, wsb_ref)
        mix = ga * ya + gs * ysb
        x2_ref[...] = x_ref[...] + _dot(mix.astype(BF16), wout_ref[...])

    res = [_residue_spec(d, tm) for d in DILATIONS]
    return pl.pallas_call(
        body, name="mix_out_fwd", grid=(L // tm,),
        in_specs=[_rows(tm, D_MODEL)] + res * 2 + [_rows(tm, SSM_WIDTH), _rows(tm, 2 * D_MODEL)] + [_whole()] * 4,
        out_specs=[_rows(tm, D_MODEL), _rows(tm, GROUP_WIDTH)] + res,
        out_shape=[jax.ShapeDtypeStruct((L, D_MODEL), F32), jax.ShapeDtypeStruct((L, GROUP_WIDTH), F32)]
        + [_residue_shape(d, L, F32) for d in DILATIONS],
        scratch_shapes=[_residue_scratch(tm)],
        compiler_params=_params(),
    )(x1, *o_g, *lse_g, y, gates, w_ab, w_glu, w_sb, w_out)


def _mix_out_bwd(dx2, o_attn, y, gates, w_ab, w_glu, w_sb, w_out, head_sum):
    L = dx2.shape[0]
    tm = min(ROW_TILE, L)

    def body(dx_ref, oat_ref, y_ref, gate_ref, wab_ref, wglu_ref, wsb_ref, wout_ref, hs_ref,
             do0, do1, do2, dl0, dl1, dl2, dy_ref, dgp_ref, mix_ref, dya_ref, dys_ref, ys2_ref, gel_ref, dglu_ref,
             dgb_ref, scr):
        i = pl.program_id(0)
        o_attn = oat_ref[...]
        yv = y_ref[...]
        ya, gel, p, sg, ys2, ysb, ga, gs = _branches(o_attn, yv, gate_ref[...], wab_ref, wglu_ref, wsb_ref)
        mix_ref[...] = (ga * ya + gs * ysb).astype(BF16)
        ys2_ref[...] = ys2.astype(BF16)
        gel_ref[...] = gel.astype(BF16)
        dmix = _dot_nt(dx_ref[...].astype(BF16), wout_ref[...])
        dgp = jnp.concatenate([dmix * ya * ga * (1.0 - ga), dmix * ysb * gs * (1.0 - gs)], axis=1)
        dgp_ref[...] = dgp.astype(BF16)

        @pl.when(i == 0)
        def _():
            dgb_ref[...] = jnp.zeros_like(dgb_ref)

        dgb_ref[...] += jnp.sum(dgp, axis=0, keepdims=True)
        dya = (dmix * ga).astype(BF16)
        dys = (dmix * gs).astype(BF16)
        dya_ref[...] = dya
        dys_ref[...] = dys
        d_o = _dot_nt(dya, wab_ref[...])
        delta = _dot_exact(d_o * o_attn, hs_ref[...])
        for do_ref, dl_ref, d in zip((do0, do1, do2), (dl0, dl1, dl2), DILATIONS):
            _to_residues(d_o, do_ref, scr, d)
            _to_residues(delta, dl_ref, scr, d)
        dys2 = _dot_nt(dys, wsb_ref[...])
        dglu = jnp.concatenate([dys2 * sg, dys2 * p * sg * (1.0 - sg)], axis=1).astype(BF16)
        dglu_ref[...] = dglu
        dy_ref[...] = _dot_nt(dglu, wglu_ref[...]) * _gelu_grad(yv)

    grp = _rows(tm, GROUP_WIDTH)
    wide = _rows(tm, D_MODEL)
    half = _rows(tm, SSM_WIDTH)
    res = [_residue_spec(d, tm) for d in DILATIONS]
    sds = jax.ShapeDtypeStruct
    return pl.pallas_call(
        body, name="mix_out_bwd", grid=(L // tm,),
        in_specs=[wide, grp, half, _rows(tm, 2 * D_MODEL)] + [_whole()] * 5,
        out_specs=res + res + [half, _rows(tm, 2 * D_MODEL), wide, wide, wide, half, half, wide, _acc_row(2 * D_MODEL)],
        out_shape=[_residue_shape(d, L, BF16) for d in DILATIONS] + [_residue_shape(d, L, F32) for d in DILATIONS]
        + [sds((L, SSM_WIDTH), F32),
           sds((L, 2 * D_MODEL), BF16), sds((L, D_MODEL), BF16), sds((L, D_MODEL), BF16),
           sds((L, D_MODEL), BF16), sds((L, SSM_WIDTH), BF16), sds((L, SSM_WIDTH), BF16),
           sds((L, D_MODEL), BF16), sds((1, 2 * D_MODEL), F32)],
        scratch_shapes=[_residue_scratch(tm)],
        compiler_params=_params(),
    )(dx2, o_attn, y, gates, w_ab, w_glu, w_sb, w_out, head_sum)


def _adamw(w, g, m, v, name):
    R, C = w.shape
    tr = _row_tile(R, max(8, ADAMW_BLOCK_BYTES // (4 * C)))

    def body(w_ref, g_ref, m_ref, v_ref, d_ref, mo_ref, vo_ref):
        gv = g_ref[...]
        mn = ADAM_B1 * m_ref[...] + (1.0 - ADAM_B1) * gv
        vn = ADAM_B2 * v_ref[...] + (1.0 - ADAM_B2) * (gv * gv)
        m_hat = mn / (1.0 - ADAM_B1 ** ADAM_STEP)
        v_hat = vn / (1.0 - ADAM_B2 ** ADAM_STEP)
        d_ref[...] = -ADAM_LR * (m_hat / (jnp.sqrt(v_hat) + ADAM_EPS) + ADAM_WD * w_ref[...])
        mo_ref[...] = mn
        vo_ref[...] = vn

    blk = pl.BlockSpec((tr, C), lambda i: (i, 0))
    return pl.pallas_call(
        body, name=name, grid=(R // tr,),
        in_specs=[blk] * 4, out_specs=[blk] * 3,
        out_shape=[jax.ShapeDtypeStruct((R, C), F32)] * 3,
        compiler_params=_params(),
    )(w, g, m, v)


def _sum_chips_into_half(u, t, name):
    S, H, C = u.shape
    tr = _row_tile(H, 512)
    hb = H // tr

    def body(s_ref, t_ref, a_ref, b_ref, c_ref, o_ref):
        me = s_ref[1]
        others = (a_ref[...], b_ref[...], c_ref[...])
        acc = None
        for chip in range(S):
            below = others[min(chip, S - 2)]
            above = others[max(chip - 1, 0)]
            term = jnp.where(me == chip, t_ref[...], jnp.where(me > chip, below, above)).astype(F32)
            acc = term if acc is None else acc + term
        o_ref[...] = acc

    x, y, c = lax.axis_index("x"), lax.axis_index("y"), lax.axis_index("c")
    me = 2 * x + y
    scalars = jnp.stack([c, me] + [j + (j >= me).astype(jnp.int32) for j in range(S - 1)]).astype(jnp.int32)
    blk = (None, tr, C)
    return pl.pallas_call(
        body, name=name,
        grid_spec=pltpu.PrefetchScalarGridSpec(
            num_scalar_prefetch=1, grid=(hb,),
            in_specs=[pl.BlockSpec(blk, lambda i, s: (s[1], i, 0))]
            + [pl.BlockSpec(blk, functools.partial(lambda j, i, s: (s[2 + j], i, 0), j)) for j in range(S - 1)],
            out_specs=pl.BlockSpec((tr, C), lambda i, s: (s[0] * hb + i, 0))),
        out_shape=jax.ShapeDtypeStruct((2 * H, C), F32),
        compiler_params=_params(),
    )(scalars, t, u, u, u)


def _add_halves(g, r1, name):
    S, R, C = g.shape
    H = R // 2
    tr = _row_tile(H, 512)
    hb = H // tr

    def body(c_ref, g_ref, r_ref, o_ref):
        o_ref[...] = (g_ref[...] + r_ref[...]).astype(BF16)

    core = lax.axis_index("c").astype(jnp.int32).reshape(1)
    return pl.pallas_call(
        body, name=name,
        grid_spec=pltpu.PrefetchScalarGridSpec(
            num_scalar_prefetch=1, grid=(S, hb),
            in_specs=[pl.BlockSpec((None, tr, C), lambda j, i, c_ref: (j, c_ref[0] * hb + i, 0)),
                      pl.BlockSpec((None, tr, C), lambda j, i, c_ref: (j, i, 0))],
            out_specs=pl.BlockSpec((None, tr, C), lambda j, i, c_ref: (j, i, 0))),
        out_shape=jax.ShapeDtypeStruct((S, H, C), BF16),
        compiler_params=_params(),
    )(core, g, r1)


_ANY = pl.BlockSpec(memory_space=pl.ANY)


def _place():
    x, y, c = lax.axis_index("x"), lax.axis_index("y"), lax.axis_index("c")
    chips = [(1 - x, y), (x, 1 - y), (1 - x, 1 - y)]
    return x, y, c, chips


def _remote(src, dst, send_sems, recv_sems, k, device):
    return pltpu.make_async_remote_copy(src_ref=src, dst_ref=dst, send_sem=send_sems.at[k], recv_sem=recv_sems.at[k],
                                        device_id=device, device_id_type=MESH)


def _gather_parts(shapes, w_refs, out_refs, send_sems, recv_sems):
    n = len(shapes)
    x, y, c, chips = _place()
    me = 2 * x + y
    sibling = (x, y, 1 - c)

    def half(k, chip_idx, core):
        H = shapes[k][0] // 2
        return out_refs[k].at[chip_idx, pl.ds(core * H, H), :]

    mine = [_remote(w_refs[k], out_refs[k].at[me], send_sems, recv_sems, 6 * n + k, sibling) for k in range(n)]
    first = []
    for k in range(n):
        H = shapes[k][0] // 2
        for j, (cx, cy) in enumerate(chips):
            first.append(_remote(w_refs[k].at[pl.ds(c * H, H), :], half(k, me, c), send_sems, recv_sems,
                                 3 * k + j, (cx, cy, c)))

    def start():
        for cp in mine + first:
            cp.start()

    def finish():
        passed = []
        for k in range(n):
            for j, (cx, cy) in enumerate(chips):
                landed = half(k, 2 * cx + cy, c)
                _remote(landed, landed, send_sems, recv_sems, 3 * k + j, (cx, cy, c)).wait_recv()
                fwd = _remote(landed, landed, send_sems, recv_sems, 3 * n + 3 * k + j, sibling)
                fwd.start()
                passed.append(fwd)
        for k in range(n):
            for j, (cx, cy) in enumerate(chips):
                other = half(k, 2 * cx + cy, 1 - c)
                _remote(other, other, send_sems, recv_sems, 3 * n + 3 * k + j, sibling).wait_recv()
        for cp in mine:
            cp.wait_recv()
        for cp in first + passed + mine:
            cp.wait_send()

    return start, finish


def _gather_weights(shards, name):
    n = len(shards)

    def body(*refs):
        x, y, c, chips = _place()
        _handshake([(x, y, 1 - c)] + [(cx, cy, c) for cx, cy in chips])
        start, finish = _gather_parts([w.shape for w in shards], refs[:n], refs[n:2 * n], *refs[2 * n:2 * n + 2])
        start()
        finish()

    return _sequenced(body, name, shards, [jax.ShapeDtypeStruct((N_SHARD,) + w.shape, w.dtype) for w in shards],
                      7 * n, COLLECTIVE_IDS["gather"])


def _handshake(peers):
    barrier = pltpu.get_barrier_semaphore()
    for peer in peers:
        pl.semaphore_signal(barrier, inc=1, device_id=peer, device_id_type=MESH)
    pl.semaphore_wait(barrier, len(peers))


def _sequenced(body, name, ins, out_shapes, n_sems, collective_id):
    return pl.kernel(
        body, out_type=list(out_shapes), mesh=plsc.ScalarSubcoreMesh(axis_name="sequencer", num_cores=1), name=name,
        scratch_types=(pltpu.SemaphoreType.DMA((n_sems,)), pltpu.SemaphoreType.DMA((n_sems,))),
        compiler_params=pltpu.CompilerParams(collective_id=collective_id))(*ins)


def _swap_halves(gs, name, collective_id):
    n = len(gs)

    def body(*refs):
        g_refs, out_refs = refs[:n], refs[n:2 * n]
        send_sems, recv_sems = refs[2 * n:]
        x, y, c, _ = _place()
        _handshake([(x, y, 1 - c)])
        cps = []
        for k in range(n):
            H = gs[k].shape[1] // 2
            cp = _remote(g_refs[k].at[:, pl.ds((1 - c) * H, H), :], out_refs[k], send_sems, recv_sems, k, (x, y, 1 - c))
            cp.start()
            cps.append(cp)
        for cp in cps:
            cp.wait()

    return _sequenced(body, name, gs, [jax.ShapeDtypeStruct((g.shape[0], g.shape[1] // 2, g.shape[2]), g.dtype)
                                       for g in gs], n, collective_id)


def _exchange_chips(ts, name, collective_id):
    n = len(ts)

    def body(*refs):
        t_refs, out_refs = refs[:n], refs[n:2 * n]
        send_sems, recv_sems = refs[2 * n:]
        x, y, c, chips = _place()
        me = 2 * x + y
        _handshake([(cx, cy, c) for cx, cy in chips])
        sent = []
        for k in range(n):
            for j, (cx, cy) in enumerate(chips):
                cp = _remote(t_refs[k].at[2 * cx + cy], out_refs[k].at[me], send_sems, recv_sems, 3 * k + j, (cx, cy, c))
                cp.start()
                sent.append(cp)
        for k in range(n):
            for j, (cx, cy) in enumerate(chips):
                slot = out_refs[k].at[2 * cx + cy]
                _remote(slot, slot, send_sems, recv_sems, 3 * k + j, (cx, cy, c)).wait_recv()
        for cp in sent:
            cp.wait_send()

    return _sequenced(body, name, ts, [jax.ShapeDtypeStruct(t.shape, t.dtype) for t in ts], 3 * n, collective_id)


def _join_halves(fs, name):
    n = len(fs)

    def body(*refs):
        out_refs = refs[n:2 * n]
        send_sems, recv_sems, _ = refs[2 * n:]
        x, y, c, _ = _place()
        sent = []
        for k in range(n):
            H = fs[k].shape[0] // 2
            here = out_refs[k].at[pl.ds(c * H, H), :]
            cp = _remote(here, here, send_sems, recv_sems, k, (x, y, 1 - c))
            cp.start()
            sent.append(cp)
        for k in range(n):
            H = fs[k].shape[0] // 2
            other = out_refs[k].at[pl.ds((1 - c) * H, H), :]
            _remote(other, other, send_sems, recv_sems, k, (x, y, 1 - c)).wait_recv()
        for cp in sent:
            cp.wait_send()

    return pl.pallas_call(
        body, name=name,
        in_specs=[_ANY] * n, out_specs=[_ANY] * n,
        out_shape=[jax.ShapeDtypeStruct(f.shape, f.dtype) for f in fs],
        input_output_aliases={k: k for k in range(n)},
        scratch_shapes=[pltpu.SemaphoreType.DMA((n,)), pltpu.SemaphoreType.DMA((n,)), pltpu.SemaphoreType.DMA((1,))],
    )(*fs)


def _gather_small(v):
    R, C = v.shape

    def body(v_ref, out_ref, send_sems, recv_sems):
        x, y, c, _ = _place()
        me = 4 * x + 2 * y + c
        flips = [(fx, fy, fc) for fx in (0, 1) for fy in (0, 1) for fc in (0, 1)][1:]
        peers = [((1 - x) if fx else x, (1 - y) if fy else y, (1 - c) if fc else c) for fx, fy, fc in flips]
        _handshake(peers)
        sent = []
        for j, peer in enumerate(peers):
            cp = _remote(v_ref, out_ref.at[me], send_sems, recv_sems, j, peer)
            cp.start()
            sent.append(cp)
        for j, peer in enumerate(peers):
            slot = out_ref.at[4 * peer[0] + 2 * peer[1] + peer[2]]
            _remote(slot, slot, send_sems, recv_sems, j, peer).wait_recv()
        for cp in sent:
            cp.wait_send()

    return _sequenced(body, "gather_small", [v], [jax.ShapeDtypeStruct((8, R, C), F32)], 7,
                      COLLECTIVE_IDS["gather_small"])[0]


def _sum_devices(x, own, name):
    S, R, C = x.shape
    tr = _row_tile(R, 2048)

    def body(s_ref, x_ref, own_ref, o_ref):
        me = s_ref[0]
        acc = None
        for k in range(S):
            term = jnp.where(me == k, own_ref[...], x_ref[k])
            acc = term if acc is None else acc + term
        o_ref[...] = acc

    x_, y_, c_ = lax.axis_index("x"), lax.axis_index("y"), lax.axis_index("c")
    me = (4 * x_ + 2 * y_ + c_).astype(jnp.int32).reshape(1)
    return pl.pallas_call(
        body, name=name,
        grid_spec=pltpu.PrefetchScalarGridSpec(
            num_scalar_prefetch=1, grid=(R // tr,),
            in_specs=[pl.BlockSpec((S, tr, C), lambda i, s: (0, i, 0)), pl.BlockSpec((tr, C), lambda i, s: (i, 0))],
            out_specs=pl.BlockSpec((tr, C), lambda i, s: (i, 0))),
        out_shape=jax.ShapeDtypeStruct((R, C), F32),
        compiler_params=_params(),
    )(me, x, own)


def _after(earlier, arrays):
    return lax.optimization_barrier((earlier, arrays))


def _reduce_swap(gs, tag, earlier):
    gs = _after(earlier, gs)[1]
    return gs, _swap_halves(gs, "reduce_swap_" + tag, COLLECTIVE_IDS["swap_" + tag])


def _reduce_exchange(gs, r1, names, tag, later_than):
    r1 = _after(later_than, r1)[1]
    ts = [_add_halves(g, r, "reduce_add_cores_" + nm) for g, r, nm in zip(gs, r1, names)]
    us = _exchange_chips(ts, "reduce_exchange_" + tag, COLLECTIVE_IDS["exchange_" + tag])
    return us, ts


def _reduce_finish(us, ts, names, tag):
    fs = [_sum_chips_into_half(u, t, "reduce_add_chips_" + nm) for u, t, nm in zip(us, ts, names)]
    return _join_halves(fs, "reduce_join_" + tag)


BIG = ["ffn1_w_gate", "ffn1_w_up", "ffn1_w_down", "w_in", "ssm_w_glu", "w_attn_branch", "w_ssm_branch",
       "w_out", "ffn2_w_gate", "ffn2_w_up", "ffn2_w_down"]
SMALL = ["ffn1_norm", "mix_norm", "gate_bias", "rel_bias_table", "ssm_a_re", "ssm_a_im", "ssm_log_dt",
         "ssm_b_re", "ssm_b_im", "ssm_c_re", "ssm_c_im", "ssm_d", "ffn2_norm", "final_norm"]
ORDER = ["ffn1_norm", "ffn1_w_gate", "ffn1_w_up", "ffn1_w_down", "mix_norm", "w_in", "gate_bias", "rel_bias_table",
         "ssm_a_re", "ssm_a_im", "ssm_log_dt", "ssm_b_re", "ssm_b_im", "ssm_c_re", "ssm_c_im", "ssm_d",
         "ssm_w_glu", "w_attn_branch", "w_ssm_branch", "w_out", "ffn2_norm", "ffn2_w_gate", "ffn2_w_up",
         "ffn2_w_down", "final_norm"]


_SMALL_TILE = 8 * LANES


def _pack_small(arrays):
    rows = []
    for a in arrays:
        flat = a.reshape(-1).astype(F32)
        rows.append(jnp.pad(flat, (0, (-flat.shape[0]) % _SMALL_TILE)).reshape(-1, LANES))
    return jnp.concatenate(rows, axis=0)


def _unpack_small(packed, shapes):
    out, r0 = [], 0
    for shp in shapes:
        n = math.prod(shp)
        rows = 8 * -(-n // _SMALL_TILE)
        out.append(packed[r0:r0 + rows].reshape(-1)[:n].reshape(shp))
        r0 += rows
    return out


def _split_cols(g):
    K, N = g.shape
    return g.reshape(K, N_SHARD, N // N_SHARD).transpose(1, 0, 2)


def _join_cols(w):
    S, K, n = w.shape
    return w.transpose(1, 0, 2).reshape(K, S * n)


COL_SHARDED = ("ssm_w_glu", "w_attn_branch", "w_ssm_branch")
TRANSPOSED = ("ffn1_w_gate", "ffn1_w_up", "ffn2_w_gate", "ffn2_w_up", "w_in")


def _shard_2d(name, arr):
    two_d = arr.reshape(arr.shape[-2:])
    return two_d.T if name in TRANSPOSED else two_d


def _shard_nd(name, two_d, shape):
    return (two_d.T if name in TRANSPOSED else two_d).reshape(shape)


class _GradSync:
    def __init__(self, weights, moms, vels):
        self.weights, self.moms, self.vels = weights, moms, vels
        self.grads, self.delta, self.new_m, self.new_v = {}, {}, {}, {}
        self.loss = None
        self._earlier = []
        self._swapped = {}
        self._exchanged = {}

    def swap(self, tag, gw, later_than=()):
        gs = []
        for n in REDUCE_GROUPS[tag]:
            g = gw[n]
            if n in COL_SHARDED:
                g = _split_cols(g)
            elif n in ("w_out", "w_in"):
                g = g.reshape(N_SHARD, g.shape[0] // N_SHARD, g.shape[1])
            gs.append(g)
        self._swapped[tag] = _reduce_swap(gs, tag, list(self._earlier) + list(later_than))
        self._earlier = self._swapped[tag][1]

    def exchange(self, tag, later_than):
        gs, r1 = self._swapped[tag]
        us, ts = _reduce_exchange(gs, r1, REDUCE_GROUPS[tag], tag, later_than)
        self._exchanged[tag] = (us, ts)
        self._earlier = us

    def small_ready(self, gs, loss_blk, later_than=()):
        _, (mine,) = _after(list(self._earlier) + list(later_than),
                            [_pack_small([gs[n] for n in SMALL] + [loss_blk[0:1, :]])])
        others = _gather_small(mine)
        self._exchanged["small"] = (others, mine)
        self._earlier = [others]

    def finish(self, tag):
        made = []
        if tag == "small":
            others, mine = self._exchanged[tag]
            shapes = [self.weights[n].shape for n in SMALL]
            total = _unpack_small(_sum_devices(others, mine, "sum_small"), shapes + [(128,)])
            self.loss = total[-1][0]
            self.grads.update(zip(SMALL, total[:-1]))
            packed = [_pack_small([src[n] for n in SMALL]) for src in (self.weights, self.grads, self.moms, self.vels)]
            for dst, res in zip((self.delta, self.new_m, self.new_v), _adamw(*packed, "adamw_small")):
                dst.update(zip(SMALL, _unpack_small(res, shapes)))
            for n in SMALL:
                made += [self.grads[n], self.delta[n], self.new_m[n], self.new_v[n]]
            return made + [self.loss]
        names = REDUCE_GROUPS[tag]
        us, ts = self._exchanged[tag]
        for n, g in zip(names, _reduce_finish(us, ts, names, tag)):
            shp = self.weights[n].shape
            d, m, v = _adamw(_shard_2d(n, self.weights[n]), g, _shard_2d(n, self.moms[n]), _shard_2d(n, self.vels[n]),
                             "adamw_" + n)
            self.grads[n], self.delta[n] = _shard_nd(n, g, shp), _shard_nd(n, d, shp)
            self.new_m[n], self.new_v[n] = _shard_nd(n, m, shp), _shard_nd(n, v, shp)
            made += [self.grads[n], self.delta[n], self.new_m[n], self.new_v[n]]
        return made

    def finish_all(self):
        self.exchange("ffn1", later_than=self.finish("ffn2"))
        for tag in ("mixer", "w_in", "small", "ffn1"):
            self.finish(tag)


def _local_step(x, target, w, later, small, sync):
    L = x.shape[0]
    row = lambda v: v.reshape(1, -1)

    a_re, a_im = small["ssm_a_re"].reshape(1, NS), small["ssm_a_im"].reshape(1, NS)
    ldt = jnp.repeat(small["ssm_log_dt"].reshape(SSM_GROUPS), SSM_STATE).reshape(1, NS)
    to_cn = lambda b: b.reshape(SSM_GROUPS, SSM_STATE, SSM_GROUP).transpose(2, 0, 1).reshape(SSM_GROUP, NS)
    c_to_cn = lambda c: c.reshape(SSM_GROUPS, SSM_GROUP, SSM_STATE).transpose(1, 0, 2).reshape(SSM_GROUP, NS)
    b_re, b_im = to_cn(small["ssm_b_re"]), to_cn(small["ssm_b_im"])
    c_re, c_im = c_to_cn(small["ssm_c_re"]), c_to_cn(small["ssm_c_im"])
    d_skip = row(small["ssm_d"])
    pw, pwr, bd, cdt = _disc_fwd(a_re, a_im, ldt, b_re, b_im, c_re, c_im)

    onehot = _bucket_onehot()
    table_t = small["rel_bias_table"].T.reshape(3, HEADS_PER_GROUP, N_BUCKETS)
    table_t = jnp.pad(table_t, ((0, 0), (0, 8 - HEADS_PER_GROUP), (0, 0)))
    bias = _bias_expand(table_t, onehot)[:, :, :HEADS_PER_GROUP].reshape(
        3, 2, HEADS_PER_GROUP, ATTN_BLOCK, 2 * ATTN_BLOCK)

    n1, nm, n2, nf = row(small["ffn1_norm"]), row(small["mix_norm"]), row(small["ffn2_norm"]), row(small["final_norm"])
    gate_bias = row(small["gate_bias"])

    x1, a1, b1, *later_full = _ffn_fwd(x, n1, w["ffn1_w_gate"], w["ffn1_w_up"], w["ffn1_w_down"], "ffn1_fwd",
                                       carried=list(later.values()))
    w = dict(w, **dict(zip(later, later_full)))
    for n in COL_SHARDED:
        w[n] = _join_cols(w[n])
    w["w_out"] = w["w_out"].reshape(D_MODEL, D_MODEL)
    w["w_in"] = w["w_in"].reshape(IN_WIDTH, D_MODEL)
    *qkv, u, gates = _mix_in_fwd(x1, nm, w["w_in"], gate_bias)
    q, k, v = qkv[0:3], qkv[3:6], qkv[6:9]
    o_g, lse_g = [], []
    for grp in range(3):
        o, lse = _attn_fwd(q[grp], k[grp], v[grp], bias[grp], f"attn_fwd_{grp}")
        o_g.append(o)
        lse_g.append(lse)
    y, s = _ssm_fwd(u, bd, cdt, d_skip, pw)
    x2, o_attn, *lse_tot = _mix_out_fwd(x1, o_g, lse_g, y, gates, w["w_attn_branch"], w["ssm_w_glu"],
                                        w["w_ssm_branch"], w["w_out"])
    x3, a2, b2 = _ffn_fwd(x2, n2, w["ffn2_w_gate"], w["ffn2_w_up"], w["ffn2_w_down"], "ffn2_fwd")
    loss_blk, dx3, d_nf = _loss_fwd_bwd(x3, nf, target)

    gw, gs = {}, {}
    gs["final_norm"] = d_nf

    dx2, da, db, sact, h, d_out, gs["ffn2_norm"] = _ffn_bwd(dx3, x2, n2, a2, b2, w["ffn2_w_gate"], w["ffn2_w_up"],
                                                            w["ffn2_w_down"], "ffn2_bwd")
    gw["ffn2_w_gate"] = _matmul_tn(da, h[None], "ffn2_dw_gate")
    gw["ffn2_w_up"] = _matmul_tn(db, h[None], "ffn2_dw_up")
    gw["ffn2_w_down"] = _matmul_tn(sact, d_out[None], "ffn2_dw_down")
    sync.swap("ffn2", gw)

    head_sum = (jnp.arange(GROUP_WIDTH)[:, None] // HEAD_DIM == jnp.arange(GROUP_WIDTH)[None, :] // HEAD_DIM).astype(F32)
    (*d_o_delta, dy, dgp, mix, dya, dys, ys2, gel, dglu, gs["gate_bias"]) = _mix_out_bwd(
        dx2, o_attn, y, gates, w["w_attn_branch"], w["ssm_w_glu"], w["w_ssm_branch"], w["w_out"], head_sum)
    sync.exchange("ffn2", later_than=[dy])
    d_o, delta = d_o_delta[0:3], d_o_delta[3:6]
    gw["w_out"] = _matmul_tn(mix[None], dx2[None], "dw_out")[0]
    gw["w_attn_branch"] = _matmul_tn(o_attn[None], dya[None], "dw_attn_branch")[0]
    gw["w_ssm_branch"] = _matmul_tn(ys2[None], dys[None], "dw_ssm_branch")[0]
    gw["ssm_w_glu"] = _matmul_tn(gel[None], dglu[None], "dw_glu")[0]

    dqs, dks, dvs, dsums = [], [], [], []
    for grp in range(3):
        dq, dk, dv, dsum = _attn_bwd(q[grp], k[grp], v[grp], d_o[grp], lse_tot[grp], delta[grp], bias[grp],
                                     f"attn_bwd_{grp}")
        dqs.append(dq)
        dks.append(dk)
        dvs.append(dv)
        dsums.append(dsum.reshape(HEADS_PER_GROUP, -1))
    dsum_all = jnp.pad(jnp.stack(dsums), ((0, 0), (0, 8 - HEADS_PER_GROUP), (0, 0)))
    d_table = _bias_reduce(dsum_all, onehot)[:, :HEADS_PER_GROUP]
    gs["rel_bias_table"] = d_table.reshape(3 * HEADS_PER_GROUP, N_BUCKETS).T

    du, gs["ssm_d"], d_bd, d_cdt, d_ab = _ssm_bwd(dy, u, s, bd, cdt, d_skip, pwr)
    sync.swap("mixer", gw, later_than=[du])
    sync.exchange("mixer", later_than=[dqs[2]])
    group_sum =(jnp.arange(NS)[:, None] // SSM_STATE == jnp.arange(128)[None, :]).astype(F32)
    d_are, d_aim, d_ldt, d_bre, d_bim, d_cre, d_cim = _disc_bwd(a_re, a_im, ldt, b_re, b_im, d_bd, d_cdt, d_ab, group_sum)
    gs["ssm_a_re"], gs["ssm_a_im"] = d_are, d_aim
    gs["ssm_log_dt"] = d_ldt[0, :SSM_GROUPS]
    from_cn = lambda t: t.reshape(SSM_GROUP, SSM_GROUPS, SSM_STATE).transpose(1, 2, 0)
    c_from_cn = lambda t: t.reshape(SSM_GROUP, SSM_GROUPS, SSM_STATE).transpose(1, 0, 2)
    gs["ssm_b_re"], gs["ssm_b_im"] = from_cn(d_bre), from_cn(d_bim)
    gs["ssm_c_re"], gs["ssm_c_im"] = c_from_cn(d_cre), c_from_cn(d_cim)

    dx1, hm, dz, gs["mix_norm"] = _mix_in_bwd(dx2, x1, nm, dqs + dks + dvs, du, dgp, w["w_in"])
    gw["w_in"] = _matmul_tn(dz[None], hm[None], "dw_in")[0]
    sync.swap("w_in", gw)

    dx0, da, db, sact, h, d_out, gs["ffn1_norm"] = _ffn_bwd(dx1, x, n1, a1, b1, w["ffn1_w_gate"], w["ffn1_w_up"],
                                                            w["ffn1_w_down"], "ffn1_bwd")
    sync.exchange("w_in", later_than=[dx0])
    gw["ffn1_w_gate"] = _matmul_tn(da, h[None], "ffn1_dw_gate")
    gw["ffn1_w_up"] = _matmul_tn(db, h[None], "ffn1_dw_up")
    sync.small_ready(gs, loss_blk, later_than=[gw["ffn1_w_up"]])
    gw["ffn1_w_down"] = _matmul_tn(sact, d_out[None], "ffn1_dw_down")
    sync.swap("ffn1", gw)
    return dx0


def kernel(x, ffn1_norm, ffn1_w_gate, ffn1_w_up, ffn1_w_down, mix_norm, w_in, gate_bias, rel_bias_table, ssm_a_re, ssm_a_im, ssm_log_dt, ssm_b_re, ssm_b_im, ssm_c_re, ssm_c_im, ssm_d, ssm_w_glu, w_attn_branch, w_ssm_branch, w_out, ffn2_norm, ffn2_w_gate, ffn2_w_up, ffn2_w_down, final_norm, loss_target, m_ffn1_norm, m_ffn1_w_gate, m_ffn1_w_up, m_ffn1_w_down, m_mix_norm, m_w_in, m_gate_bias, m_rel_bias_table, m_ssm_a_re, m_ssm_a_im, m_ssm_log_dt, m_ssm_b_re, m_ssm_b_im, m_ssm_c_re, m_ssm_c_im, m_ssm_d, m_ssm_w_glu, m_w_attn_branch, m_w_ssm_branch, m_w_out, m_ffn2_norm, m_ffn2_w_gate, m_ffn2_w_up, m_ffn2_w_down, m_final_norm, v_ffn1_norm, v_ffn1_w_gate, v_ffn1_w_up, v_ffn1_w_down, v_mix_norm, v_w_in, v_gate_bias, v_rel_bias_table, v_ssm_a_re, v_ssm_a_im, v_ssm_log_dt, v_ssm_b_re, v_ssm_b_im, v_ssm_c_re, v_ssm_c_im, v_ssm_d, v_ssm_w_glu, v_w_attn_branch, v_w_ssm_branch, v_w_out, v_ffn2_norm, v_ffn2_w_gate, v_ffn2_w_up, v_ffn2_w_down, v_final_norm):
    args = dict(locals())
    weights = {n: args[n] for n in ORDER}
    moms = {n: args["m_" + n] for n in ORDER}
    vels = {n: args["v_" + n] for n in ORDER}

    shard2d = {n: _shard_2d(n, weights[n]) for n in BIG}
    first, rest = BIG[:3], BIG[3:]
    full = dict(zip(first, _gather_weights([shard2d[n].astype(BF16) for n in first], "gather_ffn1_weights")))
    later = {n: shard2d[n].astype(BF16) for n in rest}

    small = {n: weights[n] for n in SMALL}
    sync = _GradSync(weights, moms, vels)
    grad_x = _local_step(x[0], loss_target[0], full, later, small, sync)
    sync.finish_all()
    return (sync.loss, grad_x[None], *[sync.grads[n] for n in ORDER], *[sync.delta[n] for n in ORDER],
            *[sync.new_m[n] for n in ORDER], *[sync.new_v[n] for n in ORDER])
```

```python
import functools
import math

import jax
import jax.numpy as jnp
from jax import lax
from jax.experimental import pallas as pl
from jax.experimental.pallas import tpu as pltpu
from jax.experimental.pallas import tpu_sc as plsc

F32 = jnp.float32
BF16 = jnp.bfloat16
MESH = pl.DeviceIdType.MESH

D_MODEL = 1024
D_FF = 2816
HEAD_DIM = 64
HEADS_PER_GROUP = 4
DILATIONS = (1, 4, 16)
WINDOW_STEPS = 128
ATTN_BLOCK = 128
ATTN_QB = 8
GROUP_WIDTH = HEADS_PER_GROUP * HEAD_DIM
ATTN_WIDTH = 3 * GROUP_WIDTH
N_BUCKETS = 32
MAX_DISTANCE = 2048
NEG_INF = -1e30
SSM_WIDTH = 512
SSM_GROUP = 16
SSM_GROUPS = 32
SSM_STATE = 64
NS = SSM_GROUPS * SSM_STATE
EPS = 1e-6
IN_WIDTH = 3 * ATTN_WIDTH + SSM_WIDTH + 2 * D_MODEL
Q_SCALE = HEAD_DIM ** -0.5
N_SHARD = 4
FF_SHARD = D_FF // N_SHARD
ADAM_LR, ADAM_B1, ADAM_B2, ADAM_EPS, ADAM_WD, ADAM_STEP = 0.001, 0.9, 0.999, 1e-08, 0.01, 10

LANES = 128
VMEM_LIMIT = 56 * 1024 * 1024
ROW_TILE = 512
FFN_BWD_TILE = 256
SSM_CHUNK = 256
SSM_FWD_CHUNK = 1024
SCAN_LANES = 512
ADAMW_BLOCK_BYTES = 2 << 20
TN_VMEM_BUDGET = 40 * 1024 * 1024
REDUCE_GROUPS = {
    "ffn2": ["ffn2_w_gate", "ffn2_w_up", "ffn2_w_down"],
    "mixer": ["w_out", "w_attn_branch", "w_ssm_branch", "ssm_w_glu"],
    "w_in": ["w_in"],
    "ffn1": ["ffn1_w_gate", "ffn1_w_up", "ffn1_w_down"],
}
COLLECTIVE_IDS = {name: i for i, name in enumerate(
    ["gather", "gather_small"] + [stage + "_" + tag for tag in REDUCE_GROUPS for stage in ("swap", "exchange")])}


def _params(**kw):
    return pltpu.CompilerParams(vmem_limit_bytes=VMEM_LIMIT, **kw)


def _dot(a, b):
    return jnp.dot(a, b, preferred_element_type=F32)


def _dot_nt(a, b):
    return lax.dot_general(a, b, (((1,), (1,)), ((), ())), preferred_element_type=F32)


def _dot_tn(a, b):
    return lax.dot_general(a, b, (((0,), (0,)), ((), ())), preferred_element_type=F32)


def _dot_exact(a, b):
    return jnp.dot(a, b, preferred_element_type=F32, precision=lax.Precision.HIGHEST)


def _dot_nt_exact(a, b):
    return lax.dot_general(a, b, (((1,), (1,)), ((), ())), preferred_element_type=F32,
                           precision=lax.Precision.HIGHEST)


def _rms(x):
    r = lax.rsqrt(jnp.mean(x * x, axis=-1, keepdims=True) + EPS)
    return r, x * r


def _rms_bwd(dh, g, r, xhat):
    dxh = dh * g
    return r * (dxh - xhat * jnp.mean(dxh * xhat, axis=-1, keepdims=True))


def _sigmoid(x):
    return 0.5 + 0.5 * jnp.tanh(0.5 * x)


_GELU_C = math.sqrt(2.0 / math.pi)


def _gelu(x):
    return 0.5 * x * (1.0 + jnp.tanh(_GELU_C * (x + 0.044715 * x * x * x)))


def _gelu_grad(x):
    t = jnp.tanh(_GELU_C * (x + 0.044715 * x * x * x))
    return 0.5 * (1.0 + t) + 0.5 * x * (1.0 - t * t) * _GELU_C * (1.0 + 3 * 0.044715 * x * x)


def _whole():
    return pl.BlockSpec(memory_space=pltpu.VMEM)


def _row_tile(rows, cap):
    if rows <= cap:
        return rows
    return max(t for t in range(8, cap + 1, 8) if rows % t == 0)


def _rows(tm, w):
    return pl.BlockSpec((tm, w), lambda i: (i, 0))


def _acc_row(w):
    return pl.BlockSpec((1, w), lambda i: (0, 0))


def _ffn_fwd(x, g, wg, wu, wd, name, carried=()):
    L = x.shape[0]
    tm = min(ROW_TILE, L)
    n = len(carried)
    steps = L // tm

    def body(x_ref, g_ref, wg_ref, wu_ref, wd_ref, *refs):
        shard_refs, (xo_ref, a_ref, b_ref), full_refs, sems = refs[:n], refs[n:n + 3], refs[n + 3:2 * n + 3], refs[2 * n + 3:]
        if n:
            start, finish = _gather_parts([w.shape for w in carried], shard_refs, full_refs, *sems)
            pl.when(pl.program_id(0) == 0)(start)
        xv = x_ref[...]
        r, xhat = _rms(xv)
        h = (xhat * g_ref[...]).astype(BF16)
        acc = jnp.zeros((tm, D_MODEL), F32)
        for j in range(N_SHARD):
            a = _dot_nt(h, wg_ref[j])
            b = _dot_nt(h, wu_ref[j])
            a_ref[j] = a.astype(BF16)
            b_ref[j] = b.astype(BF16)
            s = (a * _sigmoid(a) * b).astype(BF16)
            acc = acc + _dot(s, wd_ref[j])
        xo_ref[...] = xv + 0.5 * acc
        if n:
            pl.when(pl.program_id(0) == steps - 1)(finish)

    act = pl.BlockSpec((N_SHARD, tm, FF_SHARD), lambda i: (0, i, 0))
    return pl.pallas_call(
        body, name=name, grid=(steps,),
        in_specs=[_rows(tm, D_MODEL), _whole(), _whole(), _whole(), _whole()] + [_ANY] * n,
        out_specs=[_rows(tm, D_MODEL), act, act] + [_ANY] * n,
        out_shape=[jax.ShapeDtypeStruct((L, D_MODEL), F32),
                   jax.ShapeDtypeStruct((N_SHARD, L, FF_SHARD), BF16),
                   jax.ShapeDtypeStruct((N_SHARD, L, FF_SHARD), BF16)]
        + [jax.ShapeDtypeStruct((N_SHARD,) + w.shape, w.dtype) for w in carried],
        scratch_shapes=[pltpu.SemaphoreType.DMA((7 * n,)), pltpu.SemaphoreType.DMA((7 * n,))] if n else [],
        compiler_params=_params(),
    )(x, g, wg, wu, wd, *carried)


def _ffn_bwd(dxo, x, g, a, b, wg, wu, wd, name):
    L = x.shape[0]
    tm = min(FFN_BWD_TILE, L)

    def body(dxo_ref, x_ref, g_ref, a_ref, b_ref, wg_ref, wu_ref, wd_ref,
             dxi_ref, da_ref, db_ref, s_ref, h_ref, do_ref, dg_ref):
        i = pl.program_id(0)
        xv = x_ref[...]
        gv = g_ref[...]
        r, xhat = _rms(xv)
        h_ref[...] = (xhat * gv).astype(BF16)
        dxo_v = dxo_ref[...]
        d_out = (0.5 * dxo_v).astype(BF16)
        do_ref[...] = d_out
        dh = jnp.zeros((tm, D_MODEL), F32)
        for j in range(N_SHARD):
            av = a_ref[j].astype(F32)
            bv = b_ref[j].astype(F32)
            sg = _sigmoid(av)
            sl = av * sg
            ds = _dot_nt(d_out, wd_ref[j])
            dbv = (ds * sl).astype(BF16)
            dav = (ds * bv * (sg * (1.0 + av * (1.0 - sg)))).astype(BF16)
            da_ref[j] = dav
            db_ref[j] = dbv
            s_ref[j] = (sl * bv).astype(BF16)
            dh = dh + _dot(dav, wg_ref[j]) + _dot(dbv, wu_ref[j])

        @pl.when(i == 0)
        def _():
            dg_ref[...] = jnp.zeros_like(dg_ref)

        dg_ref[...] += jnp.sum(dh * xhat, axis=0, keepdims=True)
        dxi_ref[...] = dxo_v + _rms_bwd(dh, gv, r, xhat)

    act = pl.BlockSpec((N_SHARD, tm, FF_SHARD), lambda i: (0, i, 0))
    act_shape = jax.ShapeDtypeStruct((N_SHARD, L, FF_SHARD), BF16)
    return pl.pallas_call(
        body, name=name, grid=(L // tm,),
        in_specs=[_rows(tm, D_MODEL), _rows(tm, D_MODEL), _whole(), act, act, _whole(), _whole(), _whole()],
        out_specs=[_rows(tm, D_MODEL), act, act, act, _rows(tm, D_MODEL), _rows(tm, D_MODEL), _acc_row(D_MODEL)],
        out_shape=[jax.ShapeDtypeStruct((L, D_MODEL), F32), act_shape, act_shape, act_shape,
                   jax.ShapeDtypeStruct((L, D_MODEL), BF16), jax.ShapeDtypeStruct((L, D_MODEL), BF16),
                   jax.ShapeDtypeStruct((1, D_MODEL), F32)],
        compiler_params=_params(),
    )(dxo, x, g, a, b, wg, wu, wd)


def _matmul_tn(a, b, name):
    ja, L, K = a.shape
    jb, _, N = b.shape
    J = max(ja, jb)
    splits = [s for s in (1, 2, 4, 8) if s == 1 or N % (s * LANES) == 0]
    nsplit = next((s for s in splits if 2 * K * (N // s) * 4 <= TN_VMEM_BUDGET // 2), splits[-1])
    nc = N // nsplit
    left = TN_VMEM_BUDGET - 2 * K * nc * 4
    row_bytes = 2 * (K * a.dtype.itemsize + nc * b.dtype.itemsize)
    tm = next((t for t in (4096, 2048, 1024, 512, 256) if L % t == 0 and t * row_bytes <= left), min(128, L))

    def body(a_ref, b_ref, o_ref):
        @pl.when(pl.program_id(2) == 0)
        def _():
            o_ref[...] = jnp.zeros_like(o_ref)

        o_ref[...] += _dot_tn(a_ref[...].astype(BF16), b_ref[...].astype(BF16))

    return pl.pallas_call(
        body, name=name, grid=(J, nsplit, L // tm),
        in_specs=[pl.BlockSpec((None, tm, K), (lambda j, s, i: (j, i, 0)) if ja > 1 else (lambda j, s, i: (0, i, 0))),
                  pl.BlockSpec((None, tm, nc), (lambda j, s, i: (j, i, s)) if jb > 1 else (lambda j, s, i: (0, i, s)))],
        out_specs=pl.BlockSpec((None, K, nc), lambda j, s, i: (j, 0, s)),
        out_shape=jax.ShapeDtypeStruct((J, K, N), F32),
        compiler_params=_params(),
    )(a, b)


def _loss_fwd_bwd(x, g, target):
    L = x.shape[0]
    tm = min(ROW_TILE, L)

    def body(x_ref, g_ref, t_ref, loss_ref, dx_ref, dg_ref):
        i = pl.program_id(0)
        xv = x_ref[...]
        gv = g_ref[...]
        r, xhat = _rms(xv)
        err = xhat * gv - t_ref[...]
        part = 0.5 * jnp.sum(jnp.sum(err * err, axis=1, keepdims=True) * (1.0 / D_MODEL), axis=0, keepdims=True)
        dy = err * (1.0 / D_MODEL)

        @pl.when(i == 0)
        def _():
            dg_ref[...] = jnp.zeros_like(dg_ref)
            loss_ref[...] = jnp.zeros_like(loss_ref)

        loss_ref[...] += jnp.broadcast_to(part, loss_ref.shape)
        dg_ref[...] += jnp.sum(dy * xhat, axis=0, keepdims=True)
        dx_ref[...] = _rms_bwd(dy, gv, r, xhat)

    return pl.pallas_call(
        body, name="loss_fwd_bwd", grid=(L // tm,),
        in_specs=[_rows(tm, D_MODEL), _whole(), _rows(tm, D_MODEL)],
        out_specs=[pl.BlockSpec((8, 128), lambda i: (0, 0)), _rows(tm, D_MODEL), _acc_row(D_MODEL)],
        out_shape=[jax.ShapeDtypeStruct((8, 128), F32), jax.ShapeDtypeStruct((L, D_MODEL), F32),
                   jax.ShapeDtypeStruct((1, D_MODEL), F32)],
        compiler_params=_params(),
    )(x, g, target)


_C_K = ATTN_WIDTH
_C_V = 2 * ATTN_WIDTH
_C_U = 3 * ATTN_WIDTH
_C_G = _C_U + SSM_WIDTH


def _residue_spec(d, tm):
    return pl.BlockSpec((d, tm // d, GROUP_WIDTH), lambda i: (0, i, 0))


def _residue_shape(d, L, dtype):
    return jax.ShapeDtypeStruct((d, L // d, GROUP_WIDTH), dtype)


def _residue_scratch(tm):
    return pltpu.VMEM((GROUP_WIDTH // LANES, tm, LANES), F32)


def _to_residues(val, out_ref, scr, d):
    if d == 1:
        out_ref[0] = val.astype(out_ref.dtype)
        return
    tm = val.shape[0]
    for half in range(GROUP_WIDTH // LANES):
        cols = slice(half * LANES, (half + 1) * LANES)
        scr[half] = val[:, cols]
        for r in range(d):
            out_ref[r, :, cols] = scr[half, pl.ds(r, tm // d, stride=d), :].astype(out_ref.dtype)


def _from_residues(ref, scr, d):
    if d == 1:
        return ref[0].astype(F32)
    rows = ref.shape[1]
    for half in range(GROUP_WIDTH // LANES):
        cols = slice(half * LANES, (half + 1) * LANES)
        for r in range(d):
            scr[half, pl.ds(r, rows, stride=d), :] = ref[r, :, cols].astype(F32)
    return jnp.concatenate([scr[half] for half in range(GROUP_WIDTH // LANES)], axis=1)


def _mix_in_fwd(x, g, w_in, gate_bias):
    L = x.shape[0]
    tm = min(ROW_TILE, L)

    def body(x_ref, g_ref, w_ref, gb_ref, *refs):
        qkv_refs, (u_ref, gate_ref, scr) = refs[:9], refs[9:]
        r, xhat = _rms(x_ref[...])
        h = (xhat * g_ref[...]).astype(BF16)
        for part, (c0, scale) in enumerate(((0, Q_SCALE), (_C_K, 1.0), (_C_V, 1.0))):
            z = _dot_nt(h, w_ref[c0:c0 + ATTN_WIDTH, :]) * scale
            for grp, d in enumerate(DILATIONS):
                _to_residues(z[:, grp * GROUP_WIDTH:(grp + 1) * GROUP_WIDTH], qkv_refs[3 * part + grp], scr, d)
        u_ref[...] = _dot_nt(h, w_ref[_C_U:_C_G, :])
        gate_ref[...] = _sigmoid(_dot_nt(h, w_ref[_C_G:IN_WIDTH, :]) + gb_ref[...])

    return pl.pallas_call(
        body, name="mix_in_fwd", grid=(L // tm,),
        in_specs=[_rows(tm, D_MODEL), _whole(), _whole(), _whole()],
        out_specs=[_residue_spec(d, tm) for d in DILATIONS] * 3 + [_rows(tm, SSM_WIDTH), _rows(tm, 2 * D_MODEL)],
        out_shape=[_residue_shape(d, L, BF16) for d in DILATIONS] * 3
        + [jax.ShapeDtypeStruct((L, SSM_WIDTH), F32), jax.ShapeDtypeStruct((L, 2 * D_MODEL), F32)],
        scratch_shapes=[_residue_scratch(tm)],
        compiler_params=_params(),
    )(x, g, w_in, gate_bias)


def _mix_in_bwd(dx2, x, g, dqkv, du, dgp, w_in):
    L = x.shape[0]
    tm = min(ROW_TILE, L)

    def body(dx2_ref, x_ref, g_ref, *refs):
        piece_refs = refs[:9]
        du_ref, dgp_ref, w_ref, dx1_ref, h_ref, dz_ref, dg_ref, scr = refs[9:]
        i = pl.program_id(0)
        gv = g_ref[...]
        r, xhat = _rms(x_ref[...])
        h_ref[...] = (xhat * gv).astype(BF16)
        for part in range(3):
            for grp, d in enumerate(DILATIONS):
                c0 = part * ATTN_WIDTH + grp * GROUP_WIDTH
                dz_ref[:, c0:c0 + GROUP_WIDTH] = _from_residues(piece_refs[3 * part + grp], scr, d).astype(BF16)
        dz_ref[:, _C_U:_C_G] = du_ref[...].astype(BF16)
        dz_ref[:, _C_G:IN_WIDTH] = dgp_ref[...]
        dh = _dot(dz_ref[...], w_ref[...])

        @pl.when(i == 0)
        def _():
            dg_ref[...] = jnp.zeros_like(dg_ref)

        dg_ref[...] += jnp.sum(dh * xhat, axis=0, keepdims=True)
        dx1_ref[...] = dx2_ref[...] + _rms_bwd(dh, gv, r, xhat)

    return pl.pallas_call(
        body, name="mix_in_bwd", grid=(L // tm,),
        in_specs=[_rows(tm, D_MODEL), _rows(tm, D_MODEL), _whole()] + [_residue_spec(d, tm) for d in DILATIONS] * 3
        + [_rows(tm, SSM_WIDTH), _rows(tm, 2 * D_MODEL), _whole()],
        out_specs=[_rows(tm, D_MODEL), _rows(tm, D_MODEL), _rows(tm, IN_WIDTH), _acc_row(D_MODEL)],
        out_shape=[jax.ShapeDtypeStruct((L, D_MODEL), F32), jax.ShapeDtypeStruct((L, D_MODEL), BF16),
                   jax.ShapeDtypeStruct((L, IN_WIDTH), BF16), jax.ShapeDtypeStruct((1, D_MODEL), F32)],
        scratch_shapes=[_residue_scratch(tm)],
        compiler_params=_params(),
    )(dx2, x, g, *dqkv, du, dgp, w_in)


def _bucket_onehot():
    qi = jnp.arange(ATTN_BLOCK)[:, None]
    kj = jnp.arange(2 * ATTN_BLOCK)[None, :]
    steps = jnp.maximum(qi + ATTN_BLOCK - kj, 0)
    max_exact = N_BUCKETS // 2
    out = []
    for d in DILATIONS:
        dist = steps * d
        df = jnp.maximum(dist, 1).astype(F32)
        large = max_exact + (jnp.log(df / max_exact) / math.log(MAX_DISTANCE / max_exact)
                             * (N_BUCKETS - max_exact)).astype(jnp.int32)
        large = jnp.minimum(large, N_BUCKETS - 1)
        bucket = jnp.where(dist < max_exact, dist, large).reshape(-1)
        out.append((bucket[None, :] == jnp.arange(N_BUCKETS)[:, None]).astype(F32))
    return jnp.stack(out)


def _bias_expand(table_t, onehot):
    n = onehot.shape[-1]

    def body(t_ref, oh_ref, o_ref):
        bias = _dot_exact(t_ref[...], oh_ref[...])
        col = lax.broadcasted_iota(jnp.int32, (8, n), 1)
        qi = col // (2 * ATTN_BLOCK)
        kj = col - qi * (2 * ATTN_BLOCK)
        steps = qi + ATTN_BLOCK - kj
        band = (steps >= 0) & (steps <= WINDOW_STEPS)
        o_ref[0] = jnp.where(band & (kj >= ATTN_BLOCK), bias, NEG_INF)
        o_ref[1] = jnp.where(band, bias, NEG_INF)

    return pl.pallas_call(
        body, name="bias_expand", grid=(3,),
        in_specs=[pl.BlockSpec((None, 8, N_BUCKETS), lambda g: (g, 0, 0)),
                  pl.BlockSpec((None, N_BUCKETS, n), lambda g: (g, 0, 0))],
        out_specs=pl.BlockSpec((None, 2, 8, n), lambda g: (g, 0, 0, 0)),
        out_shape=jax.ShapeDtypeStruct((3, 2, 8, n), F32),
        compiler_params=_params(),
    )(table_t, onehot)


def _bias_reduce(dsum, onehot):
    n = onehot.shape[-1]

    def body(d_ref, oh_ref, o_ref):
        o_ref[...] = _dot_nt_exact(d_ref[...], oh_ref[...])

    return pl.pallas_call(
        body, name="bias_reduce", grid=(3,),
        in_specs=[pl.BlockSpec((None, 8, n), lambda g: (g, 0, 0)),
                  pl.BlockSpec((None, N_BUCKETS, n), lambda g: (g, 0, 0))],
        out_specs=pl.BlockSpec((None, 8, N_BUCKETS), lambda g: (g, 0, 0)),
        out_shape=jax.ShapeDtypeStruct((3, 8, N_BUCKETS), F32),
        compiler_params=_params(),
    )(dsum, onehot)


def _head_of_col(rows):
    return lax.broadcasted_iota(jnp.int32, (rows, GROUP_WIDTH), 1) // HEAD_DIM


_STACK_ROWS = HEADS_PER_GROUP * ATTN_BLOCK


def _stack_heads(x, head_of_col):
    return jnp.concatenate([jnp.where(head_of_col == hh, x, jnp.zeros_like(x)) for hh in range(HEADS_PER_GROUP)],
                           axis=0)


def _attn_specs(qb):
    rows = qb * ATTN_BLOCK
    cur = pl.BlockSpec((None, rows, GROUP_WIDTH), lambda r, n: (r, n, 0))
    prev = pl.BlockSpec((None, ATTN_BLOCK, GROUP_WIDTH), lambda r, n: (r, jnp.maximum(n * qb - 1, 0), 0))
    bias = pl.BlockSpec((2, HEADS_PER_GROUP, ATTN_BLOCK, 2 * ATTN_BLOCK), lambda r, n: (0, 0, 0, 0))
    return cur, prev, bias


def _attn_fwd(q, k, v, bias, name):
    d, M, _ = q.shape
    nb = M // ATTN_BLOCK
    qb = min(ATTN_QB, nb)

    def body(q_ref, kp_ref, kc_ref, vp_ref, vc_ref, bias_ref, o_ref, lse_ref):
        n = pl.program_id(1)
        q_head = _head_of_col(ATTN_BLOCK)
        kwin = jnp.concatenate([kp_ref[...], kc_ref[...]], axis=0)
        vwin = jnp.concatenate([vp_ref[...], vc_ref[...]], axis=0)
        ones = jnp.ones((2 * ATTN_BLOCK, LANES), BF16)
        for b in range(qb):
            rows = slice(b * ATTN_BLOCK, (b + 1) * ATTN_BLOCK)
            window = slice(b * ATTN_BLOCK, (b + 2) * ATTN_BLOCK)
            variant = jnp.minimum(n, 1) if b == 0 else 1
            kk = kwin[window]
            vv = vwin[window]
            q4 = _stack_heads(q_ref[rows, :], q_head)
            logits = _dot_nt(q4, kk) + bias_ref[variant].reshape(_STACK_ROWS, 2 * ATTN_BLOCK)
            m = jnp.max(logits, axis=1, keepdims=True)
            p16 = jnp.exp(logits - m).astype(BF16)
            den = _dot(p16, ones)[:, 0:1]
            out = _dot(p16, vv) * (1.0 / den)
            lse = m + jnp.log(den)
            o_acc = jnp.zeros((ATTN_BLOCK, GROUP_WIDTH), F32)
            lse_acc = jnp.zeros((ATTN_BLOCK, GROUP_WIDTH), F32)
            for hh in range(HEADS_PER_GROUP):
                head_rows = slice(hh * ATTN_BLOCK, (hh + 1) * ATTN_BLOCK)
                o_acc = jnp.where(q_head == hh, out[head_rows], o_acc)
                lse_acc = jnp.where(q_head == hh, lse[head_rows], lse_acc)
            o_ref[rows, :] = o_acc
            lse_ref[rows, :] = lse_acc

    cur, prev, full = _attn_specs(qb)
    return pl.pallas_call(
        body, name=name, grid=(d, nb // qb),
        in_specs=[cur, prev, cur, prev, cur, full],
        out_specs=[cur, cur],
        out_shape=[jax.ShapeDtypeStruct((d, M, GROUP_WIDTH), F32)] * 2,
        compiler_params=_params(),
    )(q, k, k, v, v, bias)


def _attn_bwd(q, k, v, do, lse, delta, bias, name):
    d, M, _ = q.shape
    nb = M // ATTN_BLOCK
    qb = min(ATTN_QB, nb)
    ns = nb // qb
    rows_q = qb * ATTN_BLOCK
    last = slice(rows_q - ATTN_BLOCK, rows_q)

    def body(q_ref, kp_ref, kc_ref, vp_ref, vc_ref, do_ref, lse_ref, dl_ref, bias_ref,
             dq_ref, dk_ref, dv_ref, dsum_ref, pk_ref, pv_ref, wk_ref, wv_ref):
        r = pl.program_id(0)
        n = pl.program_id(1)

        @pl.when((r == 0) & (n == 0))
        def _():
            dsum_ref[...] = jnp.zeros_like(dsum_ref)

        @pl.when(n == 0)
        def _():
            pk_ref[...] = jnp.zeros_like(pk_ref)
            pv_ref[...] = jnp.zeros_like(pv_ref)

        @pl.when(n < ns)
        def _():
            q_head = _head_of_col(ATTN_BLOCK)
            kwin = jnp.concatenate([kp_ref[...], kc_ref[...]], axis=0)
            vwin = jnp.concatenate([vp_ref[...], vc_ref[...]], axis=0)
            wk_ref[...] = jnp.zeros_like(wk_ref)
            wv_ref[...] = jnp.zeros_like(wv_ref)
            for b in range(qb):
                rows = slice(b * ATTN_BLOCK, (b + 1) * ATTN_BLOCK)
                window = slice(b * ATTN_BLOCK, (b + 2) * ATTN_BLOCK)
                variant = jnp.minimum(n, 1) if b == 0 else 1
                kk = kwin[window]
                vv = vwin[window]
                q4 = _stack_heads(q_ref[rows, :], q_head)
                do4 = _stack_heads(do_ref[rows, :], q_head)
                heads = [hh * HEAD_DIM for hh in range(HEADS_PER_GROUP)]
                lse4 = jnp.concatenate([lse_ref[rows, c0:c0 + 1] for c0 in heads], axis=0)
                dl4 = jnp.concatenate([dl_ref[rows, c0:c0 + 1] for c0 in heads], axis=0)
                logits = _dot_nt(q4, kk) + bias_ref[variant].reshape(_STACK_ROWS, 2 * ATTN_BLOCK)
                p = jnp.exp(logits - lse4)
                ds = p * (_dot_nt(do4, vv) - dl4)
                dsum_ref[...] += ds.reshape(HEADS_PER_GROUP, ATTN_BLOCK, 2 * ATTN_BLOCK)
                ds16 = ds.astype(BF16)
                dq4 = _dot(ds16, kk)
                dq_acc = jnp.zeros((ATTN_BLOCK, GROUP_WIDTH), F32)
                for hh in range(HEADS_PER_GROUP):
                    dq_acc = jnp.where(q_head == hh, dq4[hh * ATTN_BLOCK:(hh + 1) * ATTN_BLOCK], dq_acc)
                dq_ref[rows, :] = (dq_acc * Q_SCALE).astype(BF16)
                wk_ref[window, :] += _dot_tn(ds16, q4)
                wv_ref[window, :] += _dot_tn(p.astype(BF16), do4)
            for out_ref, part_ref, win_ref in ((dk_ref, pk_ref, wk_ref), (dv_ref, pv_ref, wv_ref)):
                if ns == 1:
                    out_ref[...] = win_ref[ATTN_BLOCK:, :].astype(BF16)
                    continue
                if qb > 1:
                    out_ref[0:rows_q - ATTN_BLOCK, :] = part_ref[0:rows_q - ATTN_BLOCK, :].astype(BF16)
                out_ref[last, :] = (part_ref[last, :] + win_ref[0:ATTN_BLOCK, :]).astype(BF16)
                part_ref[...] = win_ref[ATTN_BLOCK:, :]

        if ns > 1:
            @pl.when(n == ns)
            def _():
                dk_ref[...] = pk_ref[...].astype(BF16)
                dv_ref[...] = pv_ref[...].astype(BF16)

    def clamp(n):
        return jnp.minimum(n, ns - 1)

    cur = pl.BlockSpec((None, rows_q, GROUP_WIDTH), lambda r, n: (r, clamp(n), 0))
    prev = pl.BlockSpec((None, ATTN_BLOCK, GROUP_WIDTH), lambda r, n: (r, jnp.maximum(clamp(n) * qb - 1, 0), 0))
    lag = pl.BlockSpec((None, rows_q, GROUP_WIDTH), lambda r, n: (r, jnp.maximum(n - 1, 0), 0))
    full = pl.BlockSpec((2, HEADS_PER_GROUP, ATTN_BLOCK, 2 * ATTN_BLOCK), lambda r, n: (0, 0, 0, 0))
    acc = pl.BlockSpec((HEADS_PER_GROUP, ATTN_BLOCK, 2 * ATTN_BLOCK), lambda r, n: (0, 0, 0))
    return pl.pallas_call(
        body, name=name, grid=(d, ns + 1 if ns > 1 else 1),
        in_specs=[cur, prev, cur, prev, cur, cur, cur, cur, full],
        out_specs=[cur, lag, lag, acc],
        out_shape=[jax.ShapeDtypeStruct((d, M, GROUP_WIDTH), BF16)] * 3
        + [jax.ShapeDtypeStruct((HEADS_PER_GROUP, ATTN_BLOCK, 2 * ATTN_BLOCK), F32)],
        scratch_shapes=[pltpu.VMEM((rows_q, GROUP_WIDTH), F32), pltpu.VMEM((rows_q, GROUP_WIDTH), F32),
                        pltpu.VMEM((rows_q + ATTN_BLOCK, GROUP_WIDTH), F32),
                        pltpu.VMEM((rows_q + ATTN_BLOCK, GROUP_WIDTH), F32)],
        compiler_params=_params(),
    )(q, k, k, v, v, do, lse, delta, bias)


def _disc_math(a_re, a_im, ldt, b_re, b_im):
    dt = jnp.exp(ldt)
    mag = jnp.exp(a_re * dt)
    ab_re = mag * jnp.cos(a_im * dt)
    ab_im = mag * jnp.sin(a_im * dt)
    den = a_re * a_re + a_im * a_im
    xr = ab_re - 1.0
    coef_re = (xr * a_re + ab_im * a_im) / den
    coef_im = (ab_im * a_re - xr * a_im) / den
    return ab_re, ab_im, coef_re * b_re - coef_im * b_im, coef_re * b_im + coef_im * b_re


def _block_diag_mask():
    row_g = lax.broadcasted_iota(jnp.int32, (SSM_WIDTH, 2 * NS), 0) // SSM_GROUP
    col = lax.broadcasted_iota(jnp.int32, (SSM_WIDTH, 2 * NS), 1)
    col_g = jnp.where(col >= NS, col - NS, col) // SSM_STATE
    return row_g == col_g


def _disc_fwd(a_re, a_im, ldt, b_re, b_im, c_re, c_im):
    def body(are_ref, aim_ref, ldt_ref, bre_ref, bim_ref, cre_ref, cim_ref, pw_ref, pwr_ref, bd_ref, cdt_ref):
        ab_re, ab_im, bb_re, bb_im = _disc_math(are_ref[...], aim_ref[...], ldt_ref[...], bre_ref[...], bim_ref[...])
        row = lax.broadcasted_iota(jnp.int32, (8, NS), 0)
        pr, pi = ab_re, ab_im
        t_re = jnp.zeros((8, NS), F32)
        t_im = jnp.zeros((8, NS), F32)
        u_re = jnp.zeros((8, NS), F32)
        u_im = jnp.zeros((8, NS), F32)
        for j in range(8):
            t_re = jnp.where(row == j, pr, t_re)
            t_im = jnp.where(row == j, pi, t_im)
            u_re = jnp.where(row == 7 - j, pr, u_re)
            u_im = jnp.where(row == 7 - j, pi, u_im)
            pr, pi = pr * ab_re - pi * ab_im, pr * ab_im + pi * ab_re
        pw_ref[0] = t_re
        pw_ref[1] = t_im
        pwr_ref[0] = u_re
        pwr_ref[1] = u_im
        mask = _block_diag_mask()
        zero = jnp.zeros((SSM_WIDTH, 2 * NS), F32)
        bfull = jnp.concatenate([jnp.concatenate([bb_re] * SSM_GROUPS, axis=0),
                                 jnp.concatenate([bb_im] * SSM_GROUPS, axis=0)], axis=1)
        bd_ref[...] = jnp.where(mask, bfull, zero).astype(BF16)
        cfull = jnp.concatenate([jnp.concatenate([cre_ref[...]] * SSM_GROUPS, axis=0),
                                 jnp.concatenate([-cim_ref[...]] * SSM_GROUPS, axis=0)], axis=1)
        cdt_ref[...] = jnp.where(mask, cfull, zero).astype(BF16)

    return pl.pallas_call(
        body, name="s5_disc_fwd",
        in_specs=[_whole()] * 7, out_specs=[_whole()] * 4,
        out_shape=[jax.ShapeDtypeStruct((2, 8, NS), F32), jax.ShapeDtypeStruct((2, 8, NS), F32),
                   jax.ShapeDtypeStruct((SSM_WIDTH, 2 * NS), BF16), jax.ShapeDtypeStruct((SSM_WIDTH, 2 * NS), BF16)],
        compiler_params=_params(),
    )(a_re, a_im, ldt, b_re, b_im, c_re, c_im)


def _disc_bwd(a_re, a_im, ldt, b_re, b_im, d_bd, d_cdt, d_ab, group_sum):
    def body(are_ref, aim_ref, ldt_ref, bre_ref, bim_ref, dbd_ref, dcdt_ref, dab_ref, gs_ref,
             dare_ref, daim_ref, dldt_ref, dbre_ref, dbim_ref, dcre_ref, dcim_ref):
        col = lax.broadcasted_iota(jnp.int32, (SSM_GROUP, 2 * NS), 1)
        col_g = jnp.where(col >= NS, col - NS, col) // SSM_STATE
        acc_b = jnp.zeros((SSM_GROUP, 2 * NS), F32)
        acc_c = jnp.zeros((SSM_GROUP, 2 * NS), F32)
        for g in range(SSM_GROUPS):
            rows = slice(g * SSM_GROUP, (g + 1) * SSM_GROUP)
            acc_b = acc_b + jnp.where(col_g == g, dbd_ref[rows, :], 0.0)
            acc_c = acc_c + jnp.where(col_g == g, dcdt_ref[rows, :], 0.0)
        dcre_ref[...] = acc_c[:, :NS]
        dcim_ref[...] = -acc_c[:, NS:]
        dab_re = jnp.sum(dab_ref[0], axis=0, keepdims=True)
        dab_im = jnp.sum(dab_ref[1], axis=0, keepdims=True)
        _, vjp = jax.vjp(_disc_math, are_ref[...], aim_ref[...], ldt_ref[...], bre_ref[...], bim_ref[...])
        d_are, d_aim, d_ldt, d_bre, d_bim = vjp((dab_re, dab_im, acc_b[:, :NS], acc_b[:, NS:]))
        dare_ref[...] = d_are
        daim_ref[...] = d_aim
        dbre_ref[...] = d_bre
        dbim_ref[...] = d_bim
        dldt_ref[...] = _dot_exact(jnp.broadcast_to(d_ldt, (8, NS)), gs_ref[...])

    vec = jax.ShapeDtypeStruct((1, NS), F32)
    mat = jax.ShapeDtypeStruct((SSM_GROUP, NS), F32)
    return pl.pallas_call(
        body, name="s5_disc_bwd",
        in_specs=[_whole()] * 9, out_specs=[_whole()] * 7,
        out_shape=[vec, vec, jax.ShapeDtypeStruct((8, 128), F32), mat, mat, mat, mat],
        compiler_params=_params(),
    )(a_re, a_im, ldt, b_re, b_im, d_bd, d_cdt, d_ab, group_sum)


def _scan_blocks(buf, pw_ref, carry_ref, n_blocks, reverse):
    row = lax.broadcasted_iota(jnp.int32, (8, SCAN_LANES), 0)
    for lc in range(NS // SCAN_LANES):
        re_cols = pl.ds(lc * SCAN_LANES, SCAN_LANES)
        im_cols = pl.ds(NS + lc * SCAN_LANES, SCAN_LANES)
        pr = pw_ref[0, :, re_cols]
        pi = pw_ref[1, :, re_cols]
        if reverse:
            pi = -pi
            base = [(7, 1), (6, 2), (4, 4)]
            coef = [(jnp.where(row < 8 - k, pr[j:j + 1], 0.0), jnp.where(row < 8 - k, pi[j:j + 1], 0.0), 8 - k)
                    for j, k in base]
        else:
            base = [(0, 1), (1, 2), (3, 4)]
            coef = [(jnp.where(row >= k, pr[j:j + 1], 0.0), jnp.where(row >= k, pi[j:j + 1], 0.0), k)
                    for j, k in base]

        def step(i, carry, pr=pr, pi=pi, coef=coef, re_cols=re_cols, im_cols=im_cols):
            cr, ci = carry
            blk = (n_blocks - 1 - i) if reverse else i
            rows = pl.ds(pl.multiple_of(blk * 8, 8), 8)
            xr = buf[rows, re_cols]
            xi = buf[rows, im_cols]
            for kr, ki, shift in coef:
                sr = pltpu.roll(xr, shift, 0)
                si = pltpu.roll(xi, shift, 0)
                xr, xi = xr + kr * sr - ki * si, xi + kr * si + ki * sr
            xr, xi = xr + pr * cr - pi * ci, xi + pr * ci + pi * cr
            buf[rows, re_cols] = xr
            buf[rows, im_cols] = xi
            edge = slice(0, 1) if reverse else slice(7, 8)
            return xr[edge], xi[edge]

        cr, ci = lax.fori_loop(0, n_blocks, step, (carry_ref[0:1, re_cols], carry_ref[0:1, im_cols]))
        carry_ref[0:1, re_cols] = cr
        carry_ref[0:1, im_cols] = ci


_SUPER_GROUPS = 16
_SUPER_BLOCKS = [
    (slice(k * _SUPER_GROUPS * SSM_GROUP, (k + 1) * _SUPER_GROUPS * SSM_GROUP),
     [slice(half + k * _SUPER_GROUPS * SSM_STATE, half + (k + 1) * _SUPER_GROUPS * SSM_STATE) for half in (0, NS)])
    for k in range(SSM_GROUPS // _SUPER_GROUPS)]


def _ssm_fwd(u, bd, cdt, d_skip, pw):
    L = u.shape[0]
    tc = min(SSM_FWD_CHUNK, L)

    def body(u_ref, bd_ref, cdt_ref, dsk_ref, pw_ref, y_ref, s_ref, carry_ref):
        @pl.when(pl.program_id(0) == 0)
        def _():
            carry_ref[...] = jnp.zeros_like(carry_ref)

        uv = u_ref[...]
        u16 = uv.astype(BF16)
        for ch, states in _SUPER_BLOCKS:
            for st in states:
                s_ref[:, st] = _dot(u16[:, ch], bd_ref[ch, st])
        _scan_blocks(s_ref, pw_ref, carry_ref, tc // 8, reverse=False)
        for ch, states in _SUPER_BLOCKS:
            y_ref[:, ch] = (sum(_dot_nt(s_ref[:, st].astype(BF16), cdt_ref[ch, st]) for st in states)
                            + dsk_ref[:, ch] * uv[:, ch])

    return pl.pallas_call(
        body, name="s5_fwd", grid=(L // tc,),
        in_specs=[_rows(tc, SSM_WIDTH), _whole(), _whole(), _whole(), _whole()],
        out_specs=[_rows(tc, SSM_WIDTH), _rows(tc, 2 * NS)],
        out_shape=[jax.ShapeDtypeStruct((L, SSM_WIDTH), F32), jax.ShapeDtypeStruct((L, 2 * NS), F32)],
        scratch_shapes=[pltpu.VMEM((8, 2 * NS), F32)],
        compiler_params=_params(),
    )(u, bd, cdt, d_skip, pw)


def _ssm_bwd(dy, u, s, bd, cdt, d_skip, pwr):
    L = u.shape[0]
    tc = min(SSM_CHUNK, L)
    nc = L // tc
    blocks = tc // 8

    def body(dy_ref, u_ref, s_ref, sprev_ref, bd_ref, cdt_ref, dsk_ref, pwr_ref,
             du_ref, ddsk_ref, dbd_ref, dcdt_ref, dab_ref, g_ref, sx_ref, carry_ref):
        i = pl.program_id(0)

        @pl.when(i == 0)
        def _():
            carry_ref[...] = jnp.zeros_like(carry_ref)
            ddsk_ref[...] = jnp.zeros_like(ddsk_ref)
            dbd_ref[...] = jnp.zeros_like(dbd_ref)
            dcdt_ref[...] = jnp.zeros_like(dcdt_ref)
            dab_ref[...] = jnp.zeros_like(dab_ref)

        dyv = dy_ref[...]
        uv = u_ref[...]
        dy16 = dyv.astype(BF16)
        u16 = uv.astype(BF16)
        for ch, states in _SUPER_BLOCKS:
            for st in states:
                g_ref[:, st] = _dot(dy16[:, ch], cdt_ref[ch, st])
        _scan_blocks(g_ref, pwr_ref, carry_ref, blocks, reverse=True)
        ddsk_ref[...] += jnp.sum(dyv * uv, axis=0, keepdims=True)
        for ch, states in _SUPER_BLOCKS:
            du = dsk_ref[:, ch] * dyv[:, ch]
            for st in states:
                g16 = g_ref[:, st].astype(BF16)
                du = du + _dot_nt(g16, bd_ref[ch, st])
                dbd_ref[ch, st] += _dot_tn(u16[:, ch], g16)
                dcdt_ref[ch, st] += _dot_tn(dy16[:, ch], s_ref[:, st].astype(BF16))
            du_ref[:, ch] = du

        sx_ref[pl.ds(8, tc), :] = s_ref[...]
        sx_ref[pl.ds(0, 8), :] = jnp.where(i == nc - 1, 0.0, sprev_ref[...])
        row = lax.broadcasted_iota(jnp.int32, (8, SCAN_LANES), 0)
        for lc in range(NS // SCAN_LANES):
            re_cols = pl.ds(lc * SCAN_LANES, SCAN_LANES)
            im_cols = pl.ds(NS + lc * SCAN_LANES, SCAN_LANES)

            def step(b, acc, re_cols=re_cols, im_cols=im_cols):
                ar, ai = acc
                off = pl.multiple_of(b * 8, 8)
                gr = g_ref[pl.ds(off, 8), re_cols]
                gi = g_ref[pl.ds(off, 8), im_cols]
                before = pl.ds(off, 8)
                here = pl.ds(off + 8, 8)
                sr = jnp.where(row == 0, sx_ref[before, re_cols][7:8], pltpu.roll(sx_ref[here, re_cols], 1, 0))
                si = jnp.where(row == 0, sx_ref[before, im_cols][7:8], pltpu.roll(sx_ref[here, im_cols], 1, 0))
                return ar + gr * sr + gi * si, ai + gi * sr - gr * si

            zero = jnp.zeros((8, SCAN_LANES), F32)
            ar, ai = lax.fori_loop(0, blocks, step, (zero, zero))
            dab_ref[0, :, re_cols] += ar
            dab_ref[1, :, re_cols] += ai

    rev = lambda i: (nc - 1 - i, 0)
    sprev = pl.BlockSpec((8, 2 * NS), lambda i: (jnp.maximum((nc - 1 - i) * blocks - 1, 0), 0))
    return pl.pallas_call(
        body, name="s5_bwd", grid=(nc,),
        in_specs=[pl.BlockSpec((tc, SSM_WIDTH), rev), pl.BlockSpec((tc, SSM_WIDTH), rev),
                  pl.BlockSpec((tc, 2 * NS), rev), sprev, _whole(), _whole(), _whole(), _whole()],
        out_specs=[pl.BlockSpec((tc, SSM_WIDTH), rev), _whole(), _whole(), _whole(), _whole()],
        out_shape=[jax.ShapeDtypeStruct((L, SSM_WIDTH), F32), jax.ShapeDtypeStruct((1, SSM_WIDTH), F32),
                   jax.ShapeDtypeStruct((SSM_WIDTH, 2 * NS), F32), jax.ShapeDtypeStruct((SSM_WIDTH, 2 * NS), F32),
                   jax.ShapeDtypeStruct((2, 8, NS), F32)],
        scratch_shapes=[pltpu.VMEM((tc, 2 * NS), F32), pltpu.VMEM((tc + 8, 2 * NS), F32), pltpu.VMEM((8, 2 * NS), F32)],
        compiler_params=_params(),
    )(dy, u, s, s, bd, cdt, d_skip, pwr)


def _branches(o_attn, y, gates, w_ab, w_glu, w_sb):
    ya = _dot(o_attn.astype(BF16), w_ab[...])
    gel = _gelu(y)
    glu = _dot(gel.astype(BF16), w_glu[...])
    p = glu[:, :SSM_WIDTH]
    sg = _sigmoid(glu[:, SSM_WIDTH:])
    ys2 = p * sg
    ysb = _dot(ys2.astype(BF16), w_sb[...])
    ga = gates[:, :D_MODEL]
    gs = gates[:, D_MODEL:]
    return ya, gel, p, sg, ys2, ysb, ga, gs


def _mix_out_fwd(x1, o_g, lse_g, y, gates, w_ab, w_glu, w_sb, w_out):
    L = x1.shape[0]
    tm = min(ROW_TILE, L)

    def body(x_ref, o0, o1, o2, l0, l1, l2, y_ref, gate_ref, wab_ref, wglu_ref, wsb_ref, wout_ref,
             x2_ref, oat_ref, lse0, lse1, lse2, scr):
        la, lb, lc = (_from_residues(ref, scr, d) for ref, d in zip((l0, l1, l2), DILATIONS))
        m = jnp.maximum(jnp.maximum(la, lb), lc)
        ea, eb, ec = jnp.exp(la - m), jnp.exp(lb - m), jnp.exp(lc - m)
        tot = ea + eb + ec
        oa, ob, oc = (_from_residues(ref, scr, d) for ref, d in zip((o0, o1, o2), DILATIONS))
        o_attn = (ea * oa + eb * ob + ec * oc) / tot
        oat_ref[...] = o_attn
        lse = m + jnp.log(tot)
        for ref, d in zip((lse0, lse1, lse2), DILATIONS):
            _to_residues(lse, ref, scr, d)
        ya, _, _, _, _, ysb, ga, gs = _branches(o_attn, y_ref[...], gate_ref[...], wab_ref, wglu_ref, wsb_ref)
        mix = ga * ya + gs * ysb
        x2_ref[...] = x_ref[...] + _dot(mix.astype(BF16), wout_ref[...])

    res = [_residue_spec(d, tm) for d in DILATIONS]
    return pl.pallas_call(
        body, name="mix_out_fwd", grid=(L // tm,),
        in_specs=[_rows(tm, D_MODEL)] + res * 2 + [_rows(tm, SSM_WIDTH), _rows(tm, 2 * D_MODEL)] + [_whole()] * 4,
        out_specs=[_rows(tm, D_MODEL), _rows(tm, GROUP_WIDTH)] + res,
        out_shape=[jax.ShapeDtypeStruct((L, D_MODEL), F32), jax.ShapeDtypeStruct((L, GROUP_WIDTH), F32)]
        + [_residue_shape(d, L, F32) for d in DILATIONS],
        scratch_shapes=[_residue_scratch(tm)],
        compiler_params=_params(),
    )(x1, *o_g, *lse_g, y, gates, w_ab, w_glu, w_sb, w_out)


def _mix_out_bwd(dx2, o_attn, y, gates, w_ab, w_glu, w_sb, w_out, head_sum):
    L = dx2.shape[0]
    tm = min(ROW_TILE, L)

    def body(dx_ref, oat_ref, y_ref, gate_ref, wab_ref, wglu_ref, wsb_ref, wout_ref, hs_ref,
             do0, do1, do2, dl0, dl1, dl2, dy_ref, dgp_ref, mix_ref, dya_ref, dys_ref, ys2_ref, gel_ref, dglu_ref,
             dgb_ref, scr):
        i = pl.program_id(0)
        o_attn = oat_ref[...]
        yv = y_ref[...]
        ya, gel, p, sg, ys2, ysb, ga, gs = _branches(o_attn, yv, gate_ref[...], wab_ref, wglu_ref, wsb_ref)
        mix_ref[...] = (ga * ya + gs * ysb).astype(BF16)
        ys2_ref[...] = ys2.astype(BF16)
        gel_ref[...] = gel.astype(BF16)
        dmix = _dot_nt(dx_ref[...].astype(BF16), wout_ref[...])
        dgp = jnp.concatenate([dmix * ya * ga * (1.0 - ga), dmix * ysb * gs * (1.0 - gs)], axis=1)
        dgp_ref[...] = dgp.astype(BF16)

        @pl.when(i == 0)
        def _():
            dgb_ref[...] = jnp.zeros_like(dgb_ref)

        dgb_ref[...] += jnp.sum(dgp, axis=0, keepdims=True)
        dya = (dmix * ga).astype(BF16)
        dys = (dmix * gs).astype(BF16)
        dya_ref[...] = dya
        dys_ref[...] = dys
        d_o = _dot_nt(dya, wab_ref[...])
        delta = _dot_exact(d_o * o_attn, hs_ref[...])
        for do_ref, dl_ref, d in zip((do0, do1, do2), (dl0, dl1, dl2), DILATIONS):
            _to_residues(d_o, do_ref, scr, d)
            _to_residues(delta, dl_ref, scr, d)
        dys2 = _dot_nt(dys, wsb_ref[...])
        dglu = jnp.concatenate([dys2 * sg, dys2 * p * sg * (1.0 - sg)], axis=1).astype(BF16)
        dglu_ref[...] = dglu
        dy_ref[...] = _dot_nt(dglu, wglu_ref[...]) * _gelu_grad(yv)

    grp = _rows(tm, GROUP_WIDTH)
    wide = _rows(tm, D_MODEL)
    half = _rows(tm, SSM_WIDTH)
    res = [_residue_spec(d, tm) for d in DILATIONS]
    sds = jax.ShapeDtypeStruct
    return pl.pallas_call(
        body, name="mix_out_bwd", grid=(L // tm,),
        in_specs=[wide, grp, half, _rows(tm, 2 * D_MODEL)] + [_whole()] * 5,
        out_specs=res + res + [half, _rows(tm, 2 * D_MODEL), wide, wide, wide, half, half, wide, _acc_row(2 * D_MODEL)],
        out_shape=[_residue_shape(d, L, BF16) for d in DILATIONS] + [_residue_shape(d, L, F32) for d in DILATIONS]
        + [sds((L, SSM_WIDTH), F32),
           sds((L, 2 * D_MODEL), BF16), sds((L, D_MODEL), BF16), sds((L, D_MODEL), BF16),
           sds((L, D_MODEL), BF16), sds((L, SSM_WIDTH), BF16), sds((L, SSM_WIDTH), BF16),
           sds((L, D_MODEL), BF16), sds((1, 2 * D_MODEL), F32)],
        scratch_shapes=[_residue_scratch(tm)],
        compiler_params=_params(),
    )(dx2, o_attn, y, gates, w_ab, w_glu, w_sb, w_out, head_sum)


def _adamw(w, g, m, v, name):
    R, C = w.shape
    tr = _row_tile(R, max(8, ADAMW_BLOCK_BYTES // (4 * C)))

    def body(w_ref, g_ref, m_ref, v_ref, d_ref, mo_ref, vo_ref):
        gv = g_ref[...]
        mn = ADAM_B1 * m_ref[...] + (1.0 - ADAM_B1) * gv
        vn = ADAM_B2 * v_ref[...] + (1.0 - ADAM_B2) * (gv * gv)
        m_hat = mn / (1.0 - ADAM_B1 ** ADAM_STEP)
        v_hat = vn / (1.0 - ADAM_B2 ** ADAM_STEP)
        d_ref[...] = -ADAM_LR * (m_hat / (jnp.sqrt(v_hat) + ADAM_EPS) + ADAM_WD * w_ref[...])
        mo_ref[...] = mn
        vo_ref[...] = vn

    blk = pl.BlockSpec((tr, C), lambda i: (i, 0))
    return pl.pallas_call(
        body, name=name, grid=(R // tr,),
        in_specs=[blk] * 4, out_specs=[blk] * 3,
        out_shape=[jax.ShapeDtypeStruct((R, C), F32)] * 3,
        compiler_params=_params(),
    )(w, g, m, v)


def _sum_chips_into_half(u, t, name):
    S, H, C = u.shape
    tr = _row_tile(H, 512)
    hb = H // tr

    def body(s_ref, t_ref, a_ref, b_ref, c_ref, o_ref):
        me = s_ref[1]
        others = (a_ref[...], b_ref[...], c_ref[...])
        acc = None
        for chip in range(S):
            below = others[min(chip, S - 2)]
            above = others[max(chip - 1, 0)]
            term = jnp.where(me == chip, t_ref[...], jnp.where(me > chip, below, above)).astype(F32)
            acc = term if acc is None else acc + term
        o_ref[...] = acc

    x, y, c = lax.axis_index("x"), lax.axis_index("y"), lax.axis_index("c")
    me = 2 * x + y
    scalars = jnp.stack([c, me] + [j + (j >= me).astype(jnp.int32) for j in range(S - 1)]).astype(jnp.int32)
    blk = (None, tr, C)
    return pl.pallas_call(
        body, name=name,
        grid_spec=pltpu.PrefetchScalarGridSpec(
            num_scalar_prefetch=1, grid=(hb,),
            in_specs=[pl.BlockSpec(blk, lambda i, s: (s[1], i, 0))]
            + [pl.BlockSpec(blk, functools.partial(lambda j, i, s: (s[2 + j], i, 0), j)) for j in range(S - 1)],
            out_specs=pl.BlockSpec((tr, C), lambda i, s: (s[0] * hb + i, 0))),
        out_shape=jax.ShapeDtypeStruct((2 * H, C), F32),
        compiler_params=_params(),
    )(scalars, t, u, u, u)


def _add_halves(g, r1, name):
    S, R, C = g.shape
    H = R // 2
    tr = _row_tile(H, 512)
    hb = H // tr

    def body(c_ref, g_ref, r_ref, o_ref):
        o_ref[...] = (g_ref[...] + r_ref[...]).astype(BF16)

    core = lax.axis_index("c").astype(jnp.int32).reshape(1)
    return pl.pallas_call(
        body, name=name,
        grid_spec=pltpu.PrefetchScalarGridSpec(
            num_scalar_prefetch=1, grid=(S, hb),
            in_specs=[pl.BlockSpec((None, tr, C), lambda j, i, c_ref: (j, c_ref[0] * hb + i, 0)),
                      pl.BlockSpec((None, tr, C), lambda j, i, c_ref: (j, i, 0))],
            out_specs=pl.BlockSpec((None, tr, C), lambda j, i, c_ref: (j, i, 0))),
        out_shape=jax.ShapeDtypeStruct((S, H, C), BF16),
        compiler_params=_params(),
    )(core, g, r1)


_ANY = pl.BlockSpec(memory_space=pl.ANY)


def _place():
    x, y, c = lax.axis_index("x"), lax.axis_index("y"), lax.axis_index("c")
    chips = [(1 - x, y), (x, 1 - y), (1 - x, 1 - y)]
    return x, y, c, chips


def _remote(src, dst, send_sems, recv_sems, k, device):
    return pltpu.make_async_remote_copy(src_ref=src, dst_ref=dst, send_sem=send_sems.at[k], recv_sem=recv_sems.at[k],
                                        device_id=device, device_id_type=MESH)


def _gather_parts(shapes, w_refs, out_refs, send_sems, recv_sems):
    n = len(shapes)
    x, y, c, chips = _place()
    me = 2 * x + y
    sibling = (x, y, 1 - c)

    def half(k, chip_idx, core):
        H = shapes[k][0] // 2
        return out_refs[k].at[chip_idx, pl.ds(core * H, H), :]

    mine = [_remote(w_refs[k], out_refs[k].at[me], send_sems, recv_sems, 6 * n + k, sibling) for k in range(n)]
    first = []
    for k in range(n):
        H = shapes[k][0] // 2
        for j, (cx, cy) in enumerate(chips):
            first.append(_remote(w_refs[k].at[pl.ds(c * H, H), :], half(k, me, c), send_sems, recv_sems,
                                 3 * k + j, (cx, cy, c)))

    def start():
        for cp in mine + first:
            cp.start()

    def finish():
        passed = []
        for k in range(n):
            for j, (cx, cy) in enumerate(chips):
                landed = half(k, 2 * cx + cy, c)
                _remote(landed, landed, send_sems, recv_sems, 3 * k + j, (cx, cy, c)).wait_recv()
                fwd = _remote(landed, landed, send_sems, recv_sems, 3 * n + 3 * k + j, sibling)
                fwd.start()
                passed.append(fwd)
        for k in range(n):
            for j, (cx, cy) in enumerate(chips):
                other = half(k, 2 * cx + cy, 1 - c)
                _remote(other, other, send_sems, recv_sems, 3 * n + 3 * k + j, sibling).wait_recv()
        for cp in mine:
            cp.wait_recv()
        for cp in first + passed + mine:
            cp.wait_send()

    return start, finish


def _gather_weights(shards, name):
    n = len(shards)

    def body(*refs):
        x, y, c, chips = _place()
        _handshake([(x, y, 1 - c)] + [(cx, cy, c) for cx, cy in chips])
        start, finish = _gather_parts([w.shape for w in shards], refs[:n], refs[n:2 * n], *refs[2 * n:2 * n + 2])
        start()
        finish()

    return _sequenced(body, name, shards, [jax.ShapeDtypeStruct((N_SHARD,) + w.shape, w.dtype) for w in shards],
                      7 * n, COLLECTIVE_IDS["gather"])


def _handshake(peers):
    barrier = pltpu.get_barrier_semaphore()
    for peer in peers:
        pl.semaphore_signal(barrier, inc=1, device_id=peer, device_id_type=MESH)
    pl.semaphore_wait(barrier, len(peers))


def _sequenced(body, name, ins, out_shapes, n_sems, collective_id):
    return pl.kernel(
        body, out_type=list(out_shapes), mesh=plsc.ScalarSubcoreMesh(axis_name="sequencer", num_cores=1), name=name,
        scratch_types=(pltpu.SemaphoreType.DMA((n_sems,)), pltpu.SemaphoreType.DMA((n_sems,))),
        compiler_params=pltpu.CompilerParams(collective_id=collective_id))(*ins)


def _swap_halves(gs, name, collective_id):
    n = len(gs)

    def body(*refs):
        g_refs, out_refs = refs[:n], refs[n:2 * n]
        send_sems, recv_sems = refs[2 * n:]
        x, y, c, _ = _place()
        _handshake([(x, y, 1 - c)])
        cps = []
        for k in range(n):
            H = gs[k].shape[1] // 2
            cp = _remote(g_refs[k].at[:, pl.ds((1 - c) * H, H), :], out_refs[k], send_sems, recv_sems, k, (x, y, 1 - c))
            cp.start()
            cps.append(cp)
        for cp in cps:
            cp.wait()

    return _sequenced(body, name, gs, [jax.ShapeDtypeStruct((g.shape[0], g.shape[1] // 2, g.shape[2]), g.dtype)
                                       for g in gs], n, collective_id)


def _exchange_chips(ts, name, collective_id):
    n = len(ts)

    def body(*refs):
        t_refs, out_refs = refs[:n], refs[n:2 * n]
        send_sems, recv_sems = refs[2 * n:]
        x, y, c, chips = _place()
        me = 2 * x + y
        _handshake([(cx, cy, c) for cx, cy in chips])
        sent = []
        for k in range(n):
            for j, (cx, cy) in enumerate(chips):
                cp = _remote(t_refs[k].at[2 * cx + cy], out_refs[k].at[me], send_sems, recv_sems, 3 * k + j, (cx, cy, c))
                cp.start()
                sent.append(cp)
        for k in range(n):
            for j, (cx, cy) in enumerate(chips):
                slot = out_refs[k].at[2 * cx + cy]
                _remote(slot, slot, send_sems, recv_sems, 3 * k + j, (cx, cy, c)).wait_recv()
        for cp in sent:
            cp.wait_send()

    return _sequenced(body, name, ts, [jax.ShapeDtypeStruct(t.shape, t.dtype) for t in ts], 3 * n, collective_id)


def _join_halves(fs, name):
    n = len(fs)

    def body(*refs):
        out_refs = refs[n:2 * n]
        send_sems, recv_sems, _ = refs[2 * n:]
        x, y, c, _ = _place()
        sent = []
        for k in range(n):
            H = fs[k].shape[0] // 2
            here = out_refs[k].at[pl.ds(c * H, H), :]
            cp = _remote(here, here, send_sems, recv_sems, k, (x, y, 1 - c))
            cp.start()
            sent.append(cp)
        for k in range(n):
            H = fs[k].shape[0] // 2
            other = out_refs[k].at[pl.ds((1 - c) * H, H), :]
            _remote(other, other, send_sems, recv_sems, k, (x, y, 1 - c)).wait_recv()
        for cp in sent:
            cp.wait_send()

    return pl.pallas_call(
        body, name=name,
        in_specs=[_ANY] * n, out_specs=[_ANY] * n,
        out_shape=[jax.ShapeDtypeStruct(f.shape, f.dtype) for f in fs],
        input_output_aliases={k: k for k in range(n)},
        scratch_shapes=[pltpu.SemaphoreType.DMA((n,)), pltpu.SemaphoreType.DMA((n,)), pltpu.SemaphoreType.DMA((1,))],
    )(*fs)


def _gather_small(v):
    R, C = v.shape

    def body(v_ref, out_ref, send_sems, recv_sems):
        x, y, c, _ = _place()
        me = 4 * x + 2 * y + c
        flips = [(fx, fy, fc) for fx in (0, 1) for fy in (0, 1) for fc in (0, 1)][1:]
        peers = [((1 - x) if fx else x, (1 - y) if fy else y, (1 - c) if fc else c) for fx, fy, fc in flips]
        _handshake(peers)
        sent = []
        for j, peer in enumerate(peers):
            cp = _remote(v_ref, out_ref.at[me], send_sems, recv_sems, j, peer)
            cp.start()
            sent.append(cp)
        for j, peer in enumerate(peers):
            slot = out_ref.at[4 * peer[0] + 2 * peer[1] + peer[2]]
            _remote(slot, slot, send_sems, recv_sems, j, peer).wait_recv()
        for cp in sent:
            cp.wait_send()

    return _sequenced(body, "gather_small", [v], [jax.ShapeDtypeStruct((8, R, C), F32)], 7,
                      COLLECTIVE_IDS["gather_small"])[0]


def _sum_devices(x, own, name):
    S, R, C = x.shape
    tr = _row_tile(R, 2048)

    def body(s_ref, x_ref, own_ref, o_ref):
        me = s_ref[0]
        acc = None
        for k in range(S):
            term = jnp.where(me == k, own_ref[...], x_ref[k])
            acc = term if acc is None else acc + term
        o_ref[...] = acc

    x_, y_, c_ = lax.axis_index("x"), lax.axis_index("y"), lax.axis_index("c")
    me = (4 * x_ + 2 * y_ + c_).astype(jnp.int32).reshape(1)
    return pl.pallas_call(
        body, name=name,
        grid_spec=pltpu.PrefetchScalarGridSpec(
            num_scalar_prefetch=1, grid=(R // tr,),
            in_specs=[pl.BlockSpec((S, tr, C), lambda i, s: (0, i, 0)), pl.BlockSpec((tr, C), lambda i, s: (i, 0))],
            out_specs=pl.BlockSpec((tr, C), lambda i, s: (i, 0))),
        out_shape=jax.ShapeDtypeStruct((R, C), F32),
        compiler_params=_params(),
    )(me, x, own)


def _after(earlier, arrays):
    return lax.optimization_barrier((earlier, arrays))


def _reduce_swap(gs, tag, earlier):
    gs = _after(earlier, gs)[1]
    return gs, _swap_halves(gs, "reduce_swap_" + tag, COLLECTIVE_IDS["swap_" + tag])


def _reduce_exchange(gs, r1, names, tag, later_than):
    r1 = _after(later_than, r1)[1]
    ts = [_add_halves(g, r, "reduce_add_cores_" + nm) for g, r, nm in zip(gs, r1, names)]
    us = _exchange_chips(ts, "reduce_exchange_" + tag, COLLECTIVE_IDS["exchange_" + tag])
    return us, ts


def _reduce_finish(us, ts, names, tag):
    fs = [_sum_chips_into_half(u, t, "reduce_add_chips_" + nm) for u, t, nm in zip(us, ts, names)]
    return _join_halves(fs, "reduce_join_" + tag)


BIG = ["ffn1_w_gate", "ffn1_w_up", "ffn1_w_down", "w_in", "ssm_w_glu", "w_attn_branch", "w_ssm_branch",
       "w_out", "ffn2_w_gate", "ffn2_w_up", "ffn2_w_down"]
SMALL = ["ffn1_norm", "mix_norm", "gate_bias", "rel_bias_table", "ssm_a_re", "ssm_a_im", "ssm_log_dt",
         "ssm_b_re", "ssm_b_im", "ssm_c_re", "ssm_c_im", "ssm_d", "ffn2_norm", "final_norm"]
ORDER = ["ffn1_norm", "ffn1_w_gate", "ffn1_w_up", "ffn1_w_down", "mix_norm", "w_in", "gate_bias", "rel_bias_table",
         "ssm_a_re", "ssm_a_im", "ssm_log_dt", "ssm_b_re", "ssm_b_im", "ssm_c_re", "ssm_c_im", "ssm_d",
         "ssm_w_glu", "w_attn_branch", "w_ssm_branch", "w_out", "ffn2_norm", "ffn2_w_gate", "ffn2_w_up",
         "ffn2_w_down", "final_norm"]


_SMALL_TILE = 8 * LANES


def _pack_small(arrays):
    rows = []
    for a in arrays:
        flat = a.reshape(-1).astype(F32)
        rows.append(jnp.pad(flat, (0, (-flat.shape[0]) % _SMALL_TILE)).reshape(-1, LANES))
    return jnp.concatenate(rows, axis=0)


def _unpack_small(packed, shapes):
    out, r0 = [], 0
    for shp in shapes:
        n = math.prod(shp)
        rows = 8 * -(-n // _SMALL_TILE)
        out.append(packed[r0:r0 + rows].reshape(-1)[:n].reshape(shp))
        r0 += rows
    return out


def _split_cols(g):
    K, N = g.shape
    return g.reshape(K, N_SHARD, N // N_SHARD).transpose(1, 0, 2)


def _join_cols(w):
    S, K, n = w.shape
    return w.transpose(1, 0, 2).reshape(K, S * n)


COL_SHARDED = ("ssm_w_glu", "w_attn_branch", "w_ssm_branch")
TRANSPOSED = ("ffn1_w_gate", "ffn1_w_up", "ffn2_w_gate", "ffn2_w_up", "w_in")


def _shard_2d(name, arr):
    two_d = arr.reshape(arr.shape[-2:])
    return two_d.T if name in TRANSPOSED else two_d


def _shard_nd(name, two_d, shape):
    return (two_d.T if name in TRANSPOSED else two_d).reshape(shape)


class _GradSync:
    def __init__(self, weights, moms, vels):
        self.weights, self.moms, self.vels = weights, moms, vels
        self.grads, self.delta, self.new_m, self.new_v = {}, {}, {}, {}
        self.loss = None
        self._earlier = []
        self._swapped = {}
        self._exchanged = {}

    def swap(self, tag, gw, later_than=()):
        gs = []
        for n in REDUCE_GROUPS[tag]:
            g = gw[n]
            if n in COL_SHARDED:
                g = _split_cols(g)
            elif n in ("w_out", "w_in"):
                g = g.reshape(N_SHARD, g.shape[0] // N_SHARD, g.shape[1])
            gs.append(g)
        self._swapped[tag] = _reduce_swap(gs, tag, list(self._earlier) + list(later_than))
        self._earlier = self._swapped[tag][1]

    def exchange(self, tag, later_than):
        gs, r1 = self._swapped[tag]
        us, ts = _reduce_exchange(gs, r1, REDUCE_GROUPS[tag], tag, later_than)
        self._exchanged[tag] = (us, ts)
        self._earlier = us

    def small_ready(self, gs, loss_blk, later_than=()):
        _, (mine,) = _after(list(self._earlier) + list(later_than),
                            [_pack_small([gs[n] for n in SMALL] + [loss_blk[0:1, :]])])
        others = _gather_small(mine)
        self._exchanged["small"] = (others, mine)
        self._earlier = [others]

    def finish(self, tag):
        made = []
        if tag == "small":
            others, mine = self._exchanged[tag]
            shapes = [self.weights[n].shape for n in SMALL]
            total = _unpack_small(_sum_devices(others, mine, "sum_small"), shapes + [(128,)])
            self.loss = total[-1][0]
            self.grads.update(zip(SMALL, total[:-1]))
            packed = [_pack_small([src[n] for n in SMALL]) for src in (self.weights, self.grads, self.moms, self.vels)]
            for dst, res in zip((self.delta, self.new_m, self.new_v), _adamw(*packed, "adamw_small")):
                dst.update(zip(SMALL, _unpack_small(res, shapes)))
            for n in SMALL:
                made += [self.grads[n], self.delta[n], self.new_m[n], self.new_v[n]]
            return made + [self.loss]
        names = REDUCE_GROUPS[tag]
        us, ts = self._exchanged[tag]
        for n, g in zip(names, _reduce_finish(us, ts, names, tag)):
            shp = self.weights[n].shape
            d, m, v = _adamw(_shard_2d(n, self.weights[n]), g, _shard_2d(n, self.moms[n]), _shard_2d(n, self.vels[n]),
                             "adamw_" + n)
            self.grads[n], self.delta[n] = _shard_nd(n, g, shp), _shard_nd(n, d, shp)
            self.new_m[n], self.new_v[n] = _shard_nd(n, m, shp), _shard_nd(n, v, shp)
            made += [self.grads[n], self.delta[n], self.new_m[n], self.new_v[n]]
        return made

    def finish_all(self):
        self.exchange("ffn1", later_than=self.finish("ffn2"))
        for tag in ("mixer", "w_in", "small", "ffn1"):
            self.finish(tag)


def _local_step(x, target, w, later, small, sync):
    L = x.shape[0]
    row = lambda v: v.reshape(1, -1)

    a_re, a_im = small["ssm_a_re"].reshape(1, NS), small["ssm_a_im"].reshape(1, NS)
    ldt = jnp.repeat(small["ssm_log_dt"].reshape(SSM_GROUPS), SSM_STATE).reshape(1, NS)
    to_cn = lambda b: b.reshape(SSM_GROUPS, SSM_STATE, SSM_GROUP).transpose(2, 0, 1).reshape(SSM_GROUP, NS)
    c_to_cn = lambda c: c.reshape(SSM_GROUPS, SSM_GROUP, SSM_STATE).transpose(1, 0, 2).reshape(SSM_GROUP, NS)
    b_re, b_im = to_cn(small["ssm_b_re"]), to_cn(small["ssm_b_im"])
    c_re, c_im = c_to_cn(small["ssm_c_re"]), c_to_cn(small["ssm_c_im"])
    d_skip = row(small["ssm_d"])
    pw, pwr, bd, cdt = _disc_fwd(a_re, a_im, ldt, b_re, b_im, c_re, c_im)

    onehot = _bucket_onehot()
    table_t = small["rel_bias_table"].T.reshape(3, HEADS_PER_GROUP, N_BUCKETS)
    table_t = jnp.pad(table_t, ((0, 0), (0, 8 - HEADS_PER_GROUP), (0, 0)))
    bias = _bias_expand(table_t, onehot)[:, :, :HEADS_PER_GROUP].reshape(
        3, 2, HEADS_PER_GROUP, ATTN_BLOCK, 2 * ATTN_BLOCK)

    n1, nm, n2, nf = row(small["ffn1_norm"]), row(small["mix_norm"]), row(small["ffn2_norm"]), row(small["final_norm"])
    gate_bias = row(small["gate_bias"])

    x1, a1, b1, *later_full = _ffn_fwd(x, n1, w["ffn1_w_gate"], w["ffn1_w_up"], w["ffn1_w_down"], "ffn1_fwd",
                                       carried=list(later.values()))
    w = dict(w, **dict(zip(later, later_full)))
    for n in COL_SHARDED:
        w[n] = _join_cols(w[n])
    w["w_out"] = w["w_out"].reshape(D_MODEL, D_MODEL)
    w["w_in"] = w["w_in"].reshape(IN_WIDTH, D_MODEL)
    *qkv, u, gates = _mix_in_fwd(x1, nm, w["w_in"], gate_bias)
    q, k, v = qkv[0:3], qkv[3:6], qkv[6:9]
    o_g, lse_g = [], []
    for grp in range(3):
        o, lse = _attn_fwd(q[grp], k[grp], v[grp], bias[grp], f"attn_fwd_{grp}")
        o_g.append(o)
        lse_g.append(lse)
    y, s = _ssm_fwd(u, bd, cdt, d_skip, pw)
    x2, o_attn, *lse_tot = _mix_out_fwd(x1, o_g, lse_g, y, gates, w["w_attn_branch"], w["ssm_w_glu"],
                                        w["w_ssm_branch"], w["w_out"])
    x3, a2, b2 = _ffn_fwd(x2, n2, w["ffn2_w_gate"], w["ffn2_w_up"], w["ffn2_w_down"], "ffn2_fwd")
    loss_blk, dx3, d_nf = _loss_fwd_bwd(x3, nf, target)

    gw, gs = {}, {}
    gs["final_norm"] = d_nf

    dx2, da, db, sact, h, d_out, gs["ffn2_norm"] = _ffn_bwd(dx3, x2, n2, a2, b2, w["ffn2_w_gate"], w["ffn2_w_up"],
                                                            w["ffn2_w_down"], "ffn2_bwd")
    gw["ffn2_w_gate"] = _matmul_tn(da, h[None], "ffn2_dw_gate")
    gw["ffn2_w_up"] = _matmul_tn(db, h[None], "ffn2_dw_up")
    gw["ffn2_w_down"] = _matmul_tn(sact, d_out[None], "ffn2_dw_down")
    sync.swap("ffn2", gw)

    head_sum = (jnp.arange(GROUP_WIDTH)[:, None] // HEAD_DIM == jnp.arange(GROUP_WIDTH)[None, :] // HEAD_DIM).astype(F32)
    (*d_o_delta, dy, dgp, mix, dya, dys, ys2, gel, dglu, gs["gate_bias"]) = _mix_out_bwd(
        dx2, o_attn, y, gates, w["w_attn_branch"], w["ssm_w_glu"], w["w_ssm_branch"], w["w_out"], head_sum)
    sync.exchange("ffn2", later_than=[dy])
    d_o, delta = d_o_delta[0:3], d_o_delta[3:6]
    gw["w_out"] = _matmul_tn(mix[None], dx2[None], "dw_out")[0]
    gw["w_attn_branch"] = _matmul_tn(o_attn[None], dya[None], "dw_attn_branch")[0]
    gw["w_ssm_branch"] = _matmul_tn(ys2[None], dys[None], "dw_ssm_branch")[0]
    gw["ssm_w_glu"] = _matmul_tn(gel[None], dglu[None], "dw_glu")[0]

    dqs, dks, dvs, dsums = [], [], [], []
    for grp in range(3):
        dq, dk, dv, dsum = _attn_bwd(q[grp], k[grp], v[grp], d_o[grp], lse_tot[grp], delta[grp], bias[grp],
                                     f"attn_bwd_{grp}")
        dqs.append(dq)
        dks.append(dk)
        dvs.append(dv)
        dsums.append(dsum.reshape(HEADS_PER_GROUP, -1))
    dsum_all = jnp.pad(jnp.stack(dsums), ((0, 0), (0, 8 - HEADS_PER_GROUP), (0, 0)))
    d_table = _bias_reduce(dsum_all, onehot)[:, :HEADS_PER_GROUP]
    gs["rel_bias_table"] = d_table.reshape(3 * HEADS_PER_GROUP, N_BUCKETS).T

    du, gs["ssm_d"], d_bd, d_cdt, d_ab = _ssm_bwd(dy, u, s, bd, cdt, d_skip, pwr)
    sync.swap("mixer", gw, later_than=[du])
    sync.exchange("mixer", later_than=[dqs[2]])
    group_sum =(jnp.arange(NS)[:, None] // SSM_STATE == jnp.arange(128)[None, :]).astype(F32)
    d_are, d_aim, d_ldt, d_bre, d_bim, d_cre, d_cim = _disc_bwd(a_re, a_im, ldt, b_re, b_im, d_bd, d_cdt, d_ab, group_sum)
    gs["ssm_a_re"], gs["ssm_a_im"] = d_are, d_aim
    gs["ssm_log_dt"] = d_ldt[0, :SSM_GROUPS]
    from_cn = lambda t: t.reshape(SSM_GROUP, SSM_GROUPS, SSM_STATE).transpose(1, 2, 0)
    c_from_cn = lambda t: t.reshape(SSM_GROUP, SSM_GROUPS, SSM_STATE).transpose(1, 0, 2)
    gs["ssm_b_re"], gs["ssm_b_im"] = from_cn(d_bre), from_cn(d_bim)
    gs["ssm_c_re"], gs["ssm_c_im"] = c_from_cn(d_cre), c_from_cn(d_cim)

    dx1, hm, dz, gs["mix_norm"] = _mix_in_bwd(dx2, x1, nm, dqs + dks + dvs, du, dgp, w["w_in"])
    gw["w_in"] = _matmul_tn(dz[None], hm[None], "dw_in")[0]
    sync.swap("w_in", gw)

    dx0, da, db, sact, h, d_out, gs["ffn1_norm"] = _ffn_bwd(dx1, x, n1, a1, b1, w["ffn1_w_gate"], w["ffn1_w_up"],
                                                            w["ffn1_w_down"], "ffn1_bwd")
    sync.exchange("w_in", later_than=[dx0])
    gw["ffn1_w_gate"] = _matmul_tn(da, h[None], "ffn1_dw_gate")
    gw["ffn1_w_up"] = _matmul_tn(db, h[None], "ffn1_dw_up")
    sync.small_ready(gs, loss_blk, later_than=[gw["ffn1_w_up"]])
    gw["ffn1_w_down"] = _matmul_tn(sact, d_out[None], "ffn1_dw_down")
    sync.swap("ffn1", gw)
    return dx0


def kernel(x, ffn1_norm, ffn1_w_gate, ffn1_w_up, ffn1_w_down, mix_norm, w_in, gate_bias, rel_bias_table, ssm_a_re, ssm_a_im, ssm_log_dt, ssm_b_re, ssm_b_im, ssm_c_re, ssm_c_im, ssm_d, ssm_w_glu, w_attn_branch, w_ssm_branch, w_out, ffn2_norm, ffn2_w_gate, ffn2_w_up, ffn2_w_down, final_norm, loss_target, m_ffn1_norm, m_ffn1_w_gate, m_ffn1_w_up, m_ffn1_w_down, m_mix_norm, m_w_in, m_gate_bias, m_rel_bias_table, m_ssm_a_re, m_ssm_a_im, m_ssm_log_dt, m_ssm_b_re, m_ssm_b_im, m_ssm_c_re, m_ssm_c_im, m_ssm_d, m_ssm_w_glu, m_w_attn_branch, m_w_ssm_branch, m_w_out, m_ffn2_norm, m_ffn2_w_gate, m_ffn2_w_up, m_ffn2_w_down, m_final_norm, v_ffn1_norm, v_ffn1_w_gate, v_ffn1_w_up, v_ffn1_w_down, v_mix_norm, v_w_in, v_gate_bias, v_rel_bias_table, v_ssm_a_re, v_ssm_a_im, v_ssm_log_dt, v_ssm_b_re, v_ssm_b_im, v_ssm_c_re, v_ssm_c_im, v_ssm_d, v_ssm_w_glu, v_w_attn_branch, v_w_ssm_branch, v_w_out, v_ffn2_norm, v_ffn2_w_gate, v_ffn2_w_up, v_ffn2_w_down, v_final_norm):
    args = dict(locals())
    weights = {n: args[n] for n in ORDER}
    moms = {n: args["m_" + n] for n in ORDER}
    vels = {n: args["v_" + n] for n in ORDER}

    shard2d = {n: _shard_2d(n, weights[n]) for n in BIG}
    first, rest = BIG[:3], BIG[3:]
    full = dict(zip(first, _gather_weights([shard2d[n].astype(BF16) for n in first], "gather_ffn1_weights")))
    later = {n: shard2d[n].astype(BF16) for n in rest}

    small = {n: weights[n] for n in SMALL}
    sync = _GradSync(weights, moms, vels)
    grad_x = _local_step(x[0], loss_target[0], full, later, small, sync)
    sync.finish_all()
    return (sync.loss, grad_x[None], *[sync.grads[n] for n in ORDER], *[sync.delta[n] for n in ORDER],
            *[sync.new_m[n] for n in ORDER], *[sync.new_v[n] for n in ORDER])
```

```python
import functools
import math

import jax
import jax.numpy as jnp
from jax import lax
from jax.experimental import pallas as pl
from jax.experimental.pallas import tpu as pltpu
from jax.experimental.pallas import tpu_sc as plsc

F32 = jnp.float32
BF16 = jnp.bfloat16
MESH = pl.DeviceIdType.MESH

D_MODEL = 1024
D_FF = 2816
HEAD_DIM = 64
HEADS_PER_GROUP = 4
DILATIONS = (1, 4, 16)
WINDOW_STEPS = 128
ATTN_BLOCK = 128
ATTN_QB = 8
GROUP_WIDTH = HEADS_PER_GROUP * HEAD_DIM
ATTN_WIDTH = 3 * GROUP_WIDTH
N_BUCKETS = 32
MAX_DISTANCE = 2048
NEG_INF = -1e30
SSM_WIDTH = 512
SSM_GROUP = 16
SSM_GROUPS = 32
SSM_STATE = 64
NS = SSM_GROUPS * SSM_STATE
EPS = 1e-6
IN_WIDTH = 3 * ATTN_WIDTH + SSM_WIDTH + 2 * D_MODEL
Q_SCALE = HEAD_DIM ** -0.5
N_SHARD = 4
FF_SHARD = D_FF // N_SHARD
ADAM_LR, ADAM_B1, ADAM_B2, ADAM_EPS, ADAM_WD, ADAM_STEP = 0.001, 0.9, 0.999, 1e-08, 0.01, 10

LANES = 128
VMEM_LIMIT = 56 * 1024 * 1024
ROW_TILE = 512
FFN_BWD_TILE = 256
SSM_CHUNK = 256
SSM_FWD_CHUNK = 1024
SCAN_LANES = 512
ADAMW_BLOCK_BYTES = 2 << 20
TN_VMEM_BUDGET = 40 * 1024 * 1024
REDUCE_GROUPS = {
    "ffn2": ["ffn2_w_gate", "ffn2_w_up", "ffn2_w_down"],
    "mixer": ["w_out", "w_attn_branch", "w_ssm_branch", "ssm_w_glu"],
    "w_in": ["w_in"],
    "ffn1": ["ffn1_w_gate", "ffn1_w_up", "ffn1_w_down"],
}
COLLECTIVE_IDS = {name: i for i, name in enumerate(
    ["gather", "gather_small"] + [stage + "_" + tag for tag in REDUCE_GROUPS for stage in ("swap", "exchange")])}


def _params(**kw):
    return pltpu.CompilerParams(vmem_limit_bytes=VMEM_LIMIT, **kw)


def _dot(a, b):
    return jnp.dot(a, b, preferred_element_type=F32)


def _dot_nt(a, b):
    return lax.dot_general(a, b, (((1,), (1,)), ((), ())), preferred_element_type=F32)


def _dot_tn(a, b):
    return lax.dot_general(a, b, (((0,), (0,)), ((), ())), preferred_element_type=F32)


def _dot_exact(a, b):
    return jnp.dot(a, b, preferred_element_type=F32, precision=lax.Precision.HIGHEST)


def _dot_nt_exact(a, b):
    return lax.dot_general(a, b, (((1,), (1,)), ((), ())), preferred_element_type=F32,
                           precision=lax.Precision.HIGHEST)


def _rms(x):
    r = lax.rsqrt(jnp.mean(x * x, axis=-1, keepdims=True) + EPS)
    return r, x * r


def _rms_bwd(dh, g, r, xhat):
    dxh = dh * g
    return r * (dxh - xhat * jnp.mean(dxh * xhat, axis=-1, keepdims=True))


def _sigmoid(x):
    return 0.5 + 0.5 * jnp.tanh(0.5 * x)


_GELU_C = math.sqrt(2.0 / math.pi)


def _gelu(x):
    return 0.5 * x * (1.0 + jnp.tanh(_GELU_C * (x + 0.044715 * x * x * x)))


def _gelu_grad(x):
    t = jnp.tanh(_GELU_C * (x + 0.044715 * x * x * x))
    return 0.5 * (1.0 + t) + 0.5 * x * (1.0 - t * t) * _GELU_C * (1.0 + 3 * 0.044715 * x * x)


def _whole():
    return pl.BlockSpec(memory_space=pltpu.VMEM)


def _row_tile(rows, cap):
    if rows <= cap:
        return rows
    return max(t for t in range(8, cap + 1, 8) if rows % t == 0)


def _rows(tm, w):
    return pl.BlockSpec((tm, w), lambda i: (i, 0))


def _acc_row(w):
    return pl.BlockSpec((1, w), lambda i: (0, 0))


def _ffn_fwd(x, g, wg, wu, wd, name, carried=()):
    L = x.shape[0]
    tm = min(ROW_TILE, L)
    n = len(carried)
    steps = L // tm

    def body(x_ref, g_ref, wg_ref, wu_ref, wd_ref, *refs):
        shard_refs, (xo_ref, a_ref, b_ref), full_refs, sems = refs[:n], refs[n:n + 3], refs[n + 3:2 * n + 3], refs[2 * n + 3:]
        if n:
            start, finish = _gather_parts([w.shape for w in carried], shard_refs, full_refs, *sems)
            pl.when(pl.program_id(0) == 0)(start)
        xv = x_ref[...]
        r, xhat = _rms(xv)
        h = (xhat * g_ref[...]).astype(BF16)
        acc = jnp.zeros((tm, D_MODEL), F32)
        for j in range(N_SHARD):
            a = _dot_nt(h, wg_ref[j])
            b = _dot_nt(h, wu_ref[j])
            sg = _sigmoid(a)
            sl = a * sg
            a_ref[j] = sl.astype(BF16)
            a_ref[N_SHARD + j] = (sg * (1.0 + a * (1.0 - sg))).astype(BF16)
            b_ref[j] = b.astype(BF16)
            acc = acc + _dot((sl * b).astype(BF16), wd_ref[j])
        xo_ref[...] = xv + 0.5 * acc
        if n:
            pl.when(pl.program_id(0) == steps - 1)(finish)

    act = pl.BlockSpec((N_SHARD, tm, FF_SHARD), lambda i: (0, i, 0))
    act2 = pl.BlockSpec((2 * N_SHARD, tm, FF_SHARD), lambda i: (0, i, 0))
    return pl.pallas_call(
        body, name=name, grid=(steps,),
        in_specs=[_rows(tm, D_MODEL), _whole(), _whole(), _whole(), _whole()] + [_ANY] * n,
        out_specs=[_rows(tm, D_MODEL), act2, act] + [_ANY] * n,
        out_shape=[jax.ShapeDtypeStruct((L, D_MODEL), F32),
                   jax.ShapeDtypeStruct((2 * N_SHARD, L, FF_SHARD), BF16),
                   jax.ShapeDtypeStruct((N_SHARD, L, FF_SHARD), BF16)]
        + [jax.ShapeDtypeStruct((N_SHARD,) + w.shape, w.dtype) for w in carried],
        scratch_shapes=[pltpu.SemaphoreType.DMA((7 * n,)), pltpu.SemaphoreType.DMA((7 * n,))] if n else [],
        compiler_params=_params(),
    )(x, g, wg, wu, wd, *carried)


def _ffn_bwd(dxo, x, g, a, b, wg, wu, wd, name):
    L = x.shape[0]
    tm = min(FFN_BWD_TILE, L)

    def body(dxo_ref, x_ref, g_ref, a_ref, b_ref, wg_ref, wu_ref, wd_ref,
             dxi_ref, da_ref, db_ref, s_ref, h_ref, do_ref, dg_ref):
        i = pl.program_id(0)
        xv = x_ref[...]
        gv = g_ref[...]
        r, xhat = _rms(xv)
        h_ref[...] = (xhat * gv).astype(BF16)
        dxo_v = dxo_ref[...]
        d_out = (0.5 * dxo_v).astype(BF16)
        do_ref[...] = d_out
        dh = jnp.zeros((tm, D_MODEL), F32)
        for j in range(N_SHARD):
            sl = a_ref[j].astype(F32)
            bv = b_ref[j].astype(F32)
            ds = _dot_nt(d_out, wd_ref[j])
            dbv = (ds * sl).astype(BF16)
            dav = (ds * bv * a_ref[N_SHARD + j].astype(F32)).astype(BF16)
            da_ref[j] = dav
            db_ref[j] = dbv
            s_ref[j] = (sl * bv).astype(BF16)
            dh = dh + _dot(dav, wg_ref[j]) + _dot(dbv, wu_ref[j])

        @pl.when(i == 0)
        def _():
            dg_ref[...] = jnp.zeros_like(dg_ref)

        dg_ref[...] += jnp.sum(dh * xhat, axis=0, keepdims=True)
        dxi_ref[...] = dxo_v + _rms_bwd(dh, gv, r, xhat)

    act = pl.BlockSpec((N_SHARD, tm, FF_SHARD), lambda i: (0, i, 0))
    act_shape = jax.ShapeDtypeStruct((N_SHARD, L, FF_SHARD), BF16)
    act2 = pl.BlockSpec((2 * N_SHARD, tm, FF_SHARD), lambda i: (0, i, 0))
    return pl.pallas_call(
        body, name=name, grid=(L // tm,),
        in_specs=[_rows(tm, D_MODEL), _rows(tm, D_MODEL), _whole(), act2, act, _whole(), _whole(), _whole()],
        out_specs=[_rows(tm, D_MODEL), act, act, act, _rows(tm, D_MODEL), _rows(tm, D_MODEL), _acc_row(D_MODEL)],
        out_shape=[jax.ShapeDtypeStruct((L, D_MODEL), F32), act_shape, act_shape, act_shape,
                   jax.ShapeDtypeStruct((L, D_MODEL), BF16), jax.ShapeDtypeStruct((L, D_MODEL), BF16),
                   jax.ShapeDtypeStruct((1, D_MODEL), F32)],
        compiler_params=_params(),
    )(dxo, x, g, a, b, wg, wu, wd)


def _matmul_tn(a, b, name):
    ja, L, K = a.shape
    jb, _, N = b.shape
    J = max(ja, jb)
    splits = [s for s in (1, 2, 4, 8) if s == 1 or N % (s * LANES) == 0]
    nsplit = next((s for s in splits if 2 * K * (N // s) * 4 <= TN_VMEM_BUDGET // 2), splits[-1])
    nc = N // nsplit
    left = TN_VMEM_BUDGET - 2 * K * nc * 4
    row_bytes = 2 * (K * a.dtype.itemsize + nc * b.dtype.itemsize)
    tm = next((t for t in (4096, 2048, 1024, 512, 256) if L % t == 0 and t * row_bytes <= left), min(128, L))

    def body(a_ref, b_ref, o_ref):
        @pl.when(pl.program_id(2) == 0)
        def _():
            o_ref[...] = jnp.zeros_like(o_ref)

        o_ref[...] += _dot_tn(a_ref[...].astype(BF16), b_ref[...].astype(BF16))

    return pl.pallas_call(
        body, name=name, grid=(J, nsplit, L // tm),
        in_specs=[pl.BlockSpec((None, tm, K), (lambda j, s, i: (j, i, 0)) if ja > 1 else (lambda j, s, i: (0, i, 0))),
                  pl.BlockSpec((None, tm, nc), (lambda j, s, i: (j, i, s)) if jb > 1 else (lambda j, s, i: (0, i, s)))],
        out_specs=pl.BlockSpec((None, K, nc), lambda j, s, i: (j, 0, s)),
        out_shape=jax.ShapeDtypeStruct((J, K, N), F32),
        compiler_params=_params(),
    )(a, b)


def _loss_fwd_bwd(x, g, target):
    L = x.shape[0]
    tm = min(ROW_TILE, L)

    def body(x_ref, g_ref, t_ref, loss_ref, dx_ref, dg_ref):
        i = pl.program_id(0)
        xv = x_ref[...]
        gv = g_ref[...]
        r, xhat = _rms(xv)
        err = xhat * gv - t_ref[...]
        part = 0.5 * jnp.sum(jnp.sum(err * err, axis=1, keepdims=True) * (1.0 / D_MODEL), axis=0, keepdims=True)
        dy = err * (1.0 / D_MODEL)

        @pl.when(i == 0)
        def _():
            dg_ref[...] = jnp.zeros_like(dg_ref)
            loss_ref[...] = jnp.zeros_like(loss_ref)

        loss_ref[...] += jnp.broadcast_to(part, loss_ref.shape)
        dg_ref[...] += jnp.sum(dy * xhat, axis=0, keepdims=True)
        dx_ref[...] = _rms_bwd(dy, gv, r, xhat)

    return pl.pallas_call(
        body, name="loss_fwd_bwd", grid=(L // tm,),
        in_specs=[_rows(tm, D_MODEL), _whole(), _rows(tm, D_MODEL)],
        out_specs=[pl.BlockSpec((8, 128), lambda i: (0, 0)), _rows(tm, D_MODEL), _acc_row(D_MODEL)],
        out_shape=[jax.ShapeDtypeStruct((8, 128), F32), jax.ShapeDtypeStruct((L, D_MODEL), F32),
                   jax.ShapeDtypeStruct((1, D_MODEL), F32)],
        compiler_params=_params(),
    )(x, g, target)


_C_K = ATTN_WIDTH
_C_V = 2 * ATTN_WIDTH
_C_U = 3 * ATTN_WIDTH
_C_G = _C_U + SSM_WIDTH


def _residue_spec(d, tm):
    return pl.BlockSpec((d, tm // d, GROUP_WIDTH), lambda i: (0, i, 0))


def _residue_shape(d, L, dtype):
    return jax.ShapeDtypeStruct((d, L // d, GROUP_WIDTH), dtype)


def _residue_scratch(tm):
    return pltpu.VMEM((GROUP_WIDTH // LANES, tm, LANES), F32)


def _to_residues(val, out_ref, scr, d):
    if d == 1:
        out_ref[0] = val.astype(out_ref.dtype)
        return
    tm = val.shape[0]
    for half in range(GROUP_WIDTH // LANES):
        cols = slice(half * LANES, (half + 1) * LANES)
        scr[half] = val[:, cols]
        for r in range(d):
            out_ref[r, :, cols] = scr[half, pl.ds(r, tm // d, stride=d), :].astype(out_ref.dtype)


def _from_residues(ref, scr, d):
    if d == 1:
        return ref[0].astype(F32)
    rows = ref.shape[1]
    for half in range(GROUP_WIDTH // LANES):
        cols = slice(half * LANES, (half + 1) * LANES)
        for r in range(d):
            scr[half, pl.ds(r, rows, stride=d), :] = ref[r, :, cols].astype(F32)
    return jnp.concatenate([scr[half] for half in range(GROUP_WIDTH // LANES)], axis=1)


def _mix_in_fwd(x, g, w_in, gate_bias):
    L = x.shape[0]
    tm = min(ROW_TILE, L)

    def body(x_ref, g_ref, w_ref, gb_ref, *refs):
        qkv_refs, (u_ref, gate_ref, scr) = refs[:9], refs[9:]
        r, xhat = _rms(x_ref[...])
        h = (xhat * g_ref[...]).astype(BF16)
        for part, (c0, scale) in enumerate(((0, Q_SCALE), (_C_K, 1.0), (_C_V, 1.0))):
            z = _dot_nt(h, w_ref[c0:c0 + ATTN_WIDTH, :]) * scale
            for grp, d in enumerate(DILATIONS):
                _to_residues(z[:, grp * GROUP_WIDTH:(grp + 1) * GROUP_WIDTH], qkv_refs[3 * part + grp], scr, d)
        u_ref[...] = _dot_nt(h, w_ref[_C_U:_C_G, :])
        gate_ref[...] = _sigmoid(_dot_nt(h, w_ref[_C_G:IN_WIDTH, :]) + gb_ref[...])

    return pl.pallas_call(
        body, name="mix_in_fwd", grid=(L // tm,),
        in_specs=[_rows(tm, D_MODEL), _whole(), _whole(), _whole()],
        out_specs=[_residue_spec(d, tm) for d in DILATIONS] * 3 + [_rows(tm, SSM_WIDTH), _rows(tm, 2 * D_MODEL)],
        out_shape=[_residue_shape(d, L, BF16) for d in DILATIONS] * 3
        + [jax.ShapeDtypeStruct((L, SSM_WIDTH), F32), jax.ShapeDtypeStruct((L, 2 * D_MODEL), F32)],
        scratch_shapes=[_residue_scratch(tm)],
        compiler_params=_params(),
    )(x, g, w_in, gate_bias)


def _mix_in_bwd(dx2, x, g, dqkv, du, dgp, w_in):
    L = x.shape[0]
    tm = min(ROW_TILE, L)

    def body(dx2_ref, x_ref, g_ref, *refs):
        piece_refs = refs[:9]
        du_ref, dgp_ref, w_ref, dx1_ref, h_ref, dz_ref, dg_ref, scr = refs[9:]
        i = pl.program_id(0)
        gv = g_ref[...]
        r, xhat = _rms(x_ref[...])
        h_ref[...] = (xhat * gv).astype(BF16)
        for part in range(3):
            for grp, d in enumerate(DILATIONS):
                c0 = part * ATTN_WIDTH + grp * GROUP_WIDTH
                dz_ref[:, c0:c0 + GROUP_WIDTH] = _from_residues(piece_refs[3 * part + grp], scr, d).astype(BF16)
        dz_ref[:, _C_U:_C_G] = du_ref[...].astype(BF16)
        dz_ref[:, _C_G:IN_WIDTH] = dgp_ref[...]
        dh = _dot(dz_ref[...], w_ref[...])

        @pl.when(i == 0)
        def _():
            dg_ref[...] = jnp.zeros_like(dg_ref)

        dg_ref[...] += jnp.sum(dh * xhat, axis=0, keepdims=True)
        dx1_ref[...] = dx2_ref[...] + _rms_bwd(dh, gv, r, xhat)

    return pl.pallas_call(
        body, name="mix_in_bwd", grid=(L // tm,),
        in_specs=[_rows(tm, D_MODEL), _rows(tm, D_MODEL), _whole()] + [_residue_spec(d, tm) for d in DILATIONS] * 3
        + [_rows(tm, SSM_WIDTH), _rows(tm, 2 * D_MODEL), _whole()],
        out_specs=[_rows(tm, D_MODEL), _rows(tm, D_MODEL), _rows(tm, IN_WIDTH), _acc_row(D_MODEL)],
        out_shape=[jax.ShapeDtypeStruct((L, D_MODEL), F32), jax.ShapeDtypeStruct((L, D_MODEL), BF16),
                   jax.ShapeDtypeStruct((L, IN_WIDTH), BF16), jax.ShapeDtypeStruct((1, D_MODEL), F32)],
        scratch_shapes=[_residue_scratch(tm)],
        compiler_params=_params(),
    )(dx2, x, g, *dqkv, du, dgp, w_in)


def _bucket_onehot():
    qi = jnp.arange(ATTN_BLOCK)[:, None]
    kj = jnp.arange(2 * ATTN_BLOCK)[None, :]
    steps = jnp.maximum(qi + ATTN_BLOCK - kj, 0)
    max_exact = N_BUCKETS // 2
    out = []
    for d in DILATIONS:
        dist = steps * d
        df = jnp.maximum(dist, 1).astype(F32)
        large = max_exact + (jnp.log(df / max_exact) / math.log(MAX_DISTANCE / max_exact)
                             * (N_BUCKETS - max_exact)).astype(jnp.int32)
        large = jnp.minimum(large, N_BUCKETS - 1)
        bucket = jnp.where(dist < max_exact, dist, large).reshape(-1)
        out.append((bucket[None, :] == jnp.arange(N_BUCKETS)[:, None]).astype(F32))
    return jnp.stack(out)


def _bias_expand(table_t, onehot):
    n = onehot.shape[-1]

    def body(t_ref, oh_ref, o_ref):
        bias = _dot_exact(t_ref[...], oh_ref[...])
        col = lax.broadcasted_iota(jnp.int32, (8, n), 1)
        qi = col // (2 * ATTN_BLOCK)
        kj = col - qi * (2 * ATTN_BLOCK)
        steps = qi + ATTN_BLOCK - kj
        band = (steps >= 0) & (steps <= WINDOW_STEPS)
        o_ref[0] = jnp.where(band & (kj >= ATTN_BLOCK), bias, NEG_INF)
        o_ref[1] = jnp.where(band, bias, NEG_INF)

    return pl.pallas_call(
        body, name="bias_expand", grid=(3,),
        in_specs=[pl.BlockSpec((None, 8, N_BUCKETS), lambda g: (g, 0, 0)),
                  pl.BlockSpec((None, N_BUCKETS, n), lambda g: (g, 0, 0))],
        out_specs=pl.BlockSpec((None, 2, 8, n), lambda g: (g, 0, 0, 0)),
        out_shape=jax.ShapeDtypeStruct((3, 2, 8, n), F32),
        compiler_params=_params(),
    )(table_t, onehot)


def _bias_reduce(dsum, onehot):
    n = onehot.shape[-1]

    def body(d_ref, oh_ref, o_ref):
        o_ref[...] = _dot_nt_exact(d_ref[...], oh_ref[...])

    return pl.pallas_call(
        body, name="bias_reduce", grid=(3,),
        in_specs=[pl.BlockSpec((None, 8, n), lambda g: (g, 0, 0)),
                  pl.BlockSpec((None, N_BUCKETS, n), lambda g: (g, 0, 0))],
        out_specs=pl.BlockSpec((None, 8, N_BUCKETS), lambda g: (g, 0, 0)),
        out_shape=jax.ShapeDtypeStruct((3, 8, N_BUCKETS), F32),
        compiler_params=_params(),
    )(dsum, onehot)


def _head_of_col(rows):
    return lax.broadcasted_iota(jnp.int32, (rows, GROUP_WIDTH), 1) // HEAD_DIM


_STACK_ROWS = HEADS_PER_GROUP * ATTN_BLOCK


def _stack_heads(x, head_of_col):
    return jnp.concatenate([jnp.where(head_of_col == hh, x, jnp.zeros_like(x)) for hh in range(HEADS_PER_GROUP)],
                           axis=0)


def _attn_specs(qb):
    rows = qb * ATTN_BLOCK
    cur = pl.BlockSpec((None, rows, GROUP_WIDTH), lambda r, n: (r, n, 0))
    prev = pl.BlockSpec((None, ATTN_BLOCK, GROUP_WIDTH), lambda r, n: (r, jnp.maximum(n * qb - 1, 0), 0))
    bias = pl.BlockSpec((2, HEADS_PER_GROUP, ATTN_BLOCK, 2 * ATTN_BLOCK), lambda r, n: (0, 0, 0, 0))
    return cur, prev, bias


def _attn_fwd(q, k, v, bias, name):
    d, M, _ = q.shape
    nb = M // ATTN_BLOCK
    qb = min(ATTN_QB, nb)

    def body(q_ref, kp_ref, kc_ref, vp_ref, vc_ref, bias_ref, o_ref, lse_ref):
        n = pl.program_id(1)
        q_head = _head_of_col(ATTN_BLOCK)
        kwin = jnp.concatenate([kp_ref[...], kc_ref[...]], axis=0)
        vwin = jnp.concatenate([vp_ref[...], vc_ref[...]], axis=0)
        ones = jnp.ones((2 * ATTN_BLOCK, LANES), BF16)
        for b in range(qb):
            rows = slice(b * ATTN_BLOCK, (b + 1) * ATTN_BLOCK)
            window = slice(b * ATTN_BLOCK, (b + 2) * ATTN_BLOCK)
            variant = jnp.minimum(n, 1) if b == 0 else 1
            kk = kwin[window]
            vv = vwin[window]
            q4 = _stack_heads(q_ref[rows, :], q_head)
            logits = _dot_nt(q4, kk) + bias_ref[variant].reshape(_STACK_ROWS, 2 * ATTN_BLOCK)
            m = jnp.max(logits, axis=1, keepdims=True)
            p16 = jnp.exp(logits - m).astype(BF16)
            den = _dot(p16, ones)[:, 0:1]
            out = _dot(p16, vv) * (1.0 / den)
            lse = m + jnp.log(den)
            o_acc = jnp.zeros((ATTN_BLOCK, GROUP_WIDTH), F32)
            lse_acc = jnp.zeros((ATTN_BLOCK, GROUP_WIDTH), F32)
            for hh in range(HEADS_PER_GROUP):
                head_rows = slice(hh * ATTN_BLOCK, (hh + 1) * ATTN_BLOCK)
                o_acc = jnp.where(q_head == hh, out[head_rows], o_acc)
                lse_acc = jnp.where(q_head == hh, lse[head_rows], lse_acc)
            o_ref[rows, :] = o_acc
            lse_ref[rows, :] = lse_acc

    cur, prev, full = _attn_specs(qb)
    return pl.pallas_call(
        body, name=name, grid=(d, nb // qb),
        in_specs=[cur, prev, cur, prev, cur, full],
        out_specs=[cur, cur],
        out_shape=[jax.ShapeDtypeStruct((d, M, GROUP_WIDTH), F32)] * 2,
        compiler_params=_params(),
    )(q, k, k, v, v, bias)


def _attn_bwd(q, k, v, do, lse, delta, bias, name):
    d, M, _ = q.shape
    nb = M // ATTN_BLOCK
    qb = min(ATTN_QB, nb)
    ns = nb // qb
    rows_q = qb * ATTN_BLOCK
    last = slice(rows_q - ATTN_BLOCK, rows_q)

    def body(q_ref, kp_ref, kc_ref, vp_ref, vc_ref, do_ref, lse_ref, dl_ref, bias_ref,
             dq_ref, dk_ref, dv_ref, dsum_ref, pk_ref, pv_ref, wk_ref, wv_ref):
        r = pl.program_id(0)
        n = pl.program_id(1)

        @pl.when((r == 0) & (n == 0))
        def _():
            dsum_ref[...] = jnp.zeros_like(dsum_ref)

        @pl.when(n == 0)
        def _():
            pk_ref[...] = jnp.zeros_like(pk_ref)
            pv_ref[...] = jnp.zeros_like(pv_ref)

        @pl.when(n < ns)
        def _():
            q_head = _head_of_col(ATTN_BLOCK)
            kwin = jnp.concatenate([kp_ref[...], kc_ref[...]], axis=0)
            vwin = jnp.concatenate([vp_ref[...], vc_ref[...]], axis=0)
            wk_ref[...] = jnp.zeros_like(wk_ref)
            wv_ref[...] = jnp.zeros_like(wv_ref)
            for b in range(qb):
                rows = slice(b * ATTN_BLOCK, (b + 1) * ATTN_BLOCK)
                window = slice(b * ATTN_BLOCK, (b + 2) * ATTN_BLOCK)
                variant = jnp.minimum(n, 1) if b == 0 else 1
                kk = kwin[window]
                vv = vwin[window]
                q4 = _stack_heads(q_ref[rows, :], q_head)
                do4 = _stack_heads(do_ref[rows, :], q_head)
                heads = [hh * HEAD_DIM for hh in range(HEADS_PER_GROUP)]
                lse4 = jnp.concatenate([lse_ref[rows, c0:c0 + 1] for c0 in heads], axis=0)
                dl4 = jnp.concatenate([dl_ref[rows, c0:c0 + 1] for c0 in heads], axis=0)
                logits = _dot_nt(q4, kk) + bias_ref[variant].reshape(_STACK_ROWS, 2 * ATTN_BLOCK)
                p = jnp.exp(logits - lse4)
                ds = p * (_dot_nt(do4, vv) - dl4)
                dsum_ref[...] += ds.reshape(HEADS_PER_GROUP, ATTN_BLOCK, 2 * ATTN_BLOCK)
                ds16 = ds.astype(BF16)
                dq4 = _dot(ds16, kk)
                dq_acc = jnp.zeros((ATTN_BLOCK, GROUP_WIDTH), F32)
                for hh in range(HEADS_PER_GROUP):
                    dq_acc = jnp.where(q_head == hh, dq4[hh * ATTN_BLOCK:(hh + 1) * ATTN_BLOCK], dq_acc)
                dq_ref[rows, :] = (dq_acc * Q_SCALE).astype(BF16)
                wk_ref[window, :] += _dot_tn(ds16, q4)
                wv_ref[window, :] += _dot_tn(p.astype(BF16), do4)
            for out_ref, part_ref, win_ref in ((dk_ref, pk_ref, wk_ref), (dv_ref, pv_ref, wv_ref)):
                if ns == 1:
                    out_ref[...] = win_ref[ATTN_BLOCK:, :].astype(BF16)
                    continue
                if qb > 1:
                    out_ref[0:rows_q - ATTN_BLOCK, :] = part_ref[0:rows_q - ATTN_BLOCK, :].astype(BF16)
                out_ref[last, :] = (part_ref[last, :] + win_ref[0:ATTN_BLOCK, :]).astype(BF16)
                part_ref[...] = win_ref[ATTN_BLOCK:, :]

        if ns > 1:
            @pl.when(n == ns)
            def _():
                dk_ref[...] = pk_ref[...].astype(BF16)
                dv_ref[...] = pv_ref[...].astype(BF16)

    def clamp(n):
        return jnp.minimum(n, ns - 1)

    cur = pl.BlockSpec((None, rows_q, GROUP_WIDTH), lambda r, n: (r, clamp(n), 0))
    prev = pl.BlockSpec((None, ATTN_BLOCK, GROUP_WIDTH), lambda r, n: (r, jnp.maximum(clamp(n) * qb - 1, 0), 0))
    lag = pl.BlockSpec((None, rows_q, GROUP_WIDTH), lambda r, n: (r, jnp.maximum(n - 1, 0), 0))
    full = pl.BlockSpec((2, HEADS_PER_GROUP, ATTN_BLOCK, 2 * ATTN_BLOCK), lambda r, n: (0, 0, 0, 0))
    acc = pl.BlockSpec((HEADS_PER_GROUP, ATTN_BLOCK, 2 * ATTN_BLOCK), lambda r, n: (0, 0, 0))
    return pl.pallas_call(
        body, name=name, grid=(d, ns + 1 if ns > 1 else 1),
        in_specs=[cur, prev, cur, prev, cur, cur, cur, cur, full],
        out_specs=[cur, lag, lag, acc],
        out_shape=[jax.ShapeDtypeStruct((d, M, GROUP_WIDTH), BF16)] * 3
        + [jax.ShapeDtypeStruct((HEADS_PER_GROUP, ATTN_BLOCK, 2 * ATTN_BLOCK), F32)],
        scratch_shapes=[pltpu.VMEM((rows_q, GROUP_WIDTH), F32), pltpu.VMEM((rows_q, GROUP_WIDTH), F32),
                        pltpu.VMEM((rows_q + ATTN_BLOCK, GROUP_WIDTH), F32),
                        pltpu.VMEM((rows_q + ATTN_BLOCK, GROUP_WIDTH), F32)],
        compiler_params=_params(),
    )(q, k, k, v, v, do, lse, delta, bias)


def _disc_math(a_re, a_im, ldt, b_re, b_im):
    dt = jnp.exp(ldt)
    mag = jnp.exp(a_re * dt)
    ab_re = mag * jnp.cos(a_im * dt)
    ab_im = mag * jnp.sin(a_im * dt)
    den = a_re * a_re + a_im * a_im
    xr = ab_re - 1.0
    coef_re = (xr * a_re + ab_im * a_im) / den
    coef_im = (ab_im * a_re - xr * a_im) / den
    return ab_re, ab_im, coef_re * b_re - coef_im * b_im, coef_re * b_im + coef_im * b_re


def _block_diag_mask():
    row_g = lax.broadcasted_iota(jnp.int32, (SSM_WIDTH, 2 * NS), 0) // SSM_GROUP
    col = lax.broadcasted_iota(jnp.int32, (SSM_WIDTH, 2 * NS), 1)
    col_g = jnp.where(col >= NS, col - NS, col) // SSM_STATE
    return row_g == col_g


def _disc_fwd(a_re, a_im, ldt, b_re, b_im, c_re, c_im):
    def body(are_ref, aim_ref, ldt_ref, bre_ref, bim_ref, cre_ref, cim_ref, pw_ref, pwr_ref, bd_ref, cdt_ref):
        ab_re, ab_im, bb_re, bb_im = _disc_math(are_ref[...], aim_ref[...], ldt_ref[...], bre_ref[...], bim_ref[...])
        row = lax.broadcasted_iota(jnp.int32, (8, NS), 0)
        pr, pi = ab_re, ab_im
        t_re = jnp.zeros((8, NS), F32)
        t_im = jnp.zeros((8, NS), F32)
        u_re = jnp.zeros((8, NS), F32)
        u_im = jnp.zeros((8, NS), F32)
        for j in range(8):
            t_re = jnp.where(row == j, pr, t_re)
            t_im = jnp.where(row == j, pi, t_im)
            u_re = jnp.where(row == 7 - j, pr, u_re)
            u_im = jnp.where(row == 7 - j, pi, u_im)
            pr, pi = pr * ab_re - pi * ab_im, pr * ab_im + pi * ab_re
        pw_ref[0] = t_re
        pw_ref[1] = t_im
        pwr_ref[0] = u_re
        pwr_ref[1] = u_im
        mask = _block_diag_mask()
        zero = jnp.zeros((SSM_WIDTH, 2 * NS), F32)
        bfull = jnp.concatenate([jnp.concatenate([bb_re] * SSM_GROUPS, axis=0),
                                 jnp.concatenate([bb_im] * SSM_GROUPS, axis=0)], axis=1)
        bd_ref[...] = jnp.where(mask, bfull, zero).astype(BF16)
        cfull = jnp.concatenate([jnp.concatenate([cre_ref[...]] * SSM_GROUPS, axis=0),
                                 jnp.concatenate([-cim_ref[...]] * SSM_GROUPS, axis=0)], axis=1)
        cdt_ref[...] = jnp.where(mask, cfull, zero).astype(BF16)

    return pl.pallas_call(
        body, name="s5_disc_fwd",
        in_specs=[_whole()] * 7, out_specs=[_whole()] * 4,
        out_shape=[jax.ShapeDtypeStruct((2, 8, NS), F32), jax.ShapeDtypeStruct((2, 8, NS), F32),
                   jax.ShapeDtypeStruct((SSM_WIDTH, 2 * NS), BF16), jax.ShapeDtypeStruct((SSM_WIDTH, 2 * NS), BF16)],
        compiler_params=_params(),
    )(a_re, a_im, ldt, b_re, b_im, c_re, c_im)


def _disc_bwd(a_re, a_im, ldt, b_re, b_im, d_bd, d_cdt, d_ab, group_sum):
    def body(are_ref, aim_ref, ldt_ref, bre_ref, bim_ref, dbd_ref, dcdt_ref, dab_ref, gs_ref,
             dare_ref, daim_ref, dldt_ref, dbre_ref, dbim_ref, dcre_ref, dcim_ref):
        col = lax.broadcasted_iota(jnp.int32, (SSM_GROUP, 2 * NS), 1)
        col_g = jnp.where(col >= NS, col - NS, col) // SSM_STATE
        acc_b = jnp.zeros((SSM_GROUP, 2 * NS), F32)
        acc_c = jnp.zeros((SSM_GROUP, 2 * NS), F32)
        for g in range(SSM_GROUPS):
            rows = slice(g * SSM_GROUP, (g + 1) * SSM_GROUP)
            acc_b = acc_b + jnp.where(col_g == g, dbd_ref[rows, :], 0.0)
            acc_c = acc_c + jnp.where(col_g == g, dcdt_ref[rows, :], 0.0)
        dcre_ref[...] = acc_c[:, :NS]
        dcim_ref[...] = -acc_c[:, NS:]
        dab_re = jnp.sum(dab_ref[0], axis=0, keepdims=True)
        dab_im = jnp.sum(dab_ref[1], axis=0, keepdims=True)
        _, vjp = jax.vjp(_disc_math, are_ref[...], aim_ref[...], ldt_ref[...], bre_ref[...], bim_ref[...])
        d_are, d_aim, d_ldt, d_bre, d_bim = vjp((dab_re, dab_im, acc_b[:, :NS], acc_b[:, NS:]))
        dare_ref[...] = d_are
        daim_ref[...] = d_aim
        dbre_ref[...] = d_bre
        dbim_ref[...] = d_bim
        dldt_ref[...] = _dot_exact(jnp.broadcast_to(d_ldt, (8, NS)), gs_ref[...])

    vec = jax.ShapeDtypeStruct((1, NS), F32)
    mat = jax.ShapeDtypeStruct((SSM_GROUP, NS), F32)
    return pl.pallas_call(
        body, name="s5_disc_bwd",
        in_specs=[_whole()] * 9, out_specs=[_whole()] * 7,
        out_shape=[vec, vec, jax.ShapeDtypeStruct((8, 128), F32), mat, mat, mat, mat],
        compiler_params=_params(),
    )(a_re, a_im, ldt, b_re, b_im, d_bd, d_cdt, d_ab, group_sum)


def _scan_blocks(buf, pw_ref, carry_ref, n_blocks, reverse):
    row = lax.broadcasted_iota(jnp.int32, (8, SCAN_LANES), 0)
    for lc in range(NS // SCAN_LANES):
        re_cols = pl.ds(lc * SCAN_LANES, SCAN_LANES)
        im_cols = pl.ds(NS + lc * SCAN_LANES, SCAN_LANES)
        pr = pw_ref[0, :, re_cols]
        pi = pw_ref[1, :, re_cols]
        if reverse:
            pi = -pi
            base = [(7, 1), (6, 2), (4, 4)]
            coef = [(jnp.where(row < 8 - k, pr[j:j + 1], 0.0), jnp.where(row < 8 - k, pi[j:j + 1], 0.0), 8 - k)
                    for j, k in base]
        else:
            base = [(0, 1), (1, 2), (3, 4)]
            coef = [(jnp.where(row >= k, pr[j:j + 1], 0.0), jnp.where(row >= k, pi[j:j + 1], 0.0), k)
                    for j, k in base]

        def step(i, carry, pr=pr, pi=pi, coef=coef, re_cols=re_cols, im_cols=im_cols):
            cr, ci = carry
            blk = (n_blocks - 1 - i) if reverse else i
            rows = pl.ds(pl.multiple_of(blk * 8, 8), 8)
            xr = buf[rows, re_cols]
            xi = buf[rows, im_cols]
            for kr, ki, shift in coef:
                sr = pltpu.roll(xr, shift, 0)
                si = pltpu.roll(xi, shift, 0)
                xr, xi = xr + kr * sr - ki * si, xi + kr * si + ki * sr
            xr, xi = xr + pr * cr - pi * ci, xi + pr * ci + pi * cr
            buf[rows, re_cols] = xr
            buf[rows, im_cols] = xi
            edge = slice(0, 1) if reverse else slice(7, 8)
            return xr[edge], xi[edge]

        cr, ci = lax.fori_loop(0, n_blocks, step, (carry_ref[0:1, re_cols], carry_ref[0:1, im_cols]))
        carry_ref[0:1, re_cols] = cr
        carry_ref[0:1, im_cols] = ci


_SUPER_GROUPS = 16
_SUPER_BLOCKS = [
    (slice(k * _SUPER_GROUPS * SSM_GROUP, (k + 1) * _SUPER_GROUPS * SSM_GROUP),
     [slice(half + k * _SUPER_GROUPS * SSM_STATE, half + (k + 1) * _SUPER_GROUPS * SSM_STATE) for half in (0, NS)])
    for k in range(SSM_GROUPS // _SUPER_GROUPS)]


def _ssm_fwd(u, bd, cdt, d_skip, pw):
    L = u.shape[0]
    tc = min(SSM_FWD_CHUNK, L)

    def body(u_ref, bd_ref, cdt_ref, dsk_ref, pw_ref, y_ref, s_ref, carry_ref):
        @pl.when(pl.program_id(0) == 0)
        def _():
            carry_ref[...] = jnp.zeros_like(carry_ref)

        uv = u_ref[...]
        u16 = uv.astype(BF16)
        for ch, states in _SUPER_BLOCKS:
            for st in states:
                s_ref[:, st] = _dot(u16[:, ch], bd_ref[ch, st])
        _scan_blocks(s_ref, pw_ref, carry_ref, tc // 8, reverse=False)
        for ch, states in _SUPER_BLOCKS:
            y_ref[:, ch] = (sum(_dot_nt(s_ref[:, st].astype(BF16), cdt_ref[ch, st]) for st in states)
                            + dsk_ref[:, ch] * uv[:, ch])

    return pl.pallas_call(
        body, name="s5_fwd", grid=(L // tc,),
        in_specs=[_rows(tc, SSM_WIDTH), _whole(), _whole(), _whole(), _whole()],
        out_specs=[_rows(tc, SSM_WIDTH), _rows(tc, 2 * NS)],
        out_shape=[jax.ShapeDtypeStruct((L, SSM_WIDTH), F32), jax.ShapeDtypeStruct((L, 2 * NS), F32)],
        scratch_shapes=[pltpu.VMEM((8, 2 * NS), F32)],
        compiler_params=_params(),
    )(u, bd, cdt, d_skip, pw)


def _ssm_bwd(dy, u, s, bd, cdt, d_skip, pwr):
    L = u.shape[0]
    tc = min(SSM_CHUNK, L)
    nc = L // tc
    blocks = tc // 8

    def body(dy_ref, u_ref, s_ref, sprev_ref, bd_ref, cdt_ref, dsk_ref, pwr_ref,
             du_ref, ddsk_ref, dbd_ref, dcdt_ref, dab_ref, g_ref, sx_ref, carry_ref):
        i = pl.program_id(0)

        @pl.when(i == 0)
        def _():
            carry_ref[...] = jnp.zeros_like(carry_ref)
            ddsk_ref[...] = jnp.zeros_like(ddsk_ref)
            dbd_ref[...] = jnp.zeros_like(dbd_ref)
            dcdt_ref[...] = jnp.zeros_like(dcdt_ref)
            dab_ref[...] = jnp.zeros_like(dab_ref)

        dyv = dy_ref[...]
        uv = u_ref[...]
        dy16 = dyv.astype(BF16)
        u16 = uv.astype(BF16)
        for ch, states in _SUPER_BLOCKS:
            for st in states:
                g_ref[:, st] = _dot(dy16[:, ch], cdt_ref[ch, st])
        _scan_blocks(g_ref, pwr_ref, carry_ref, blocks, reverse=True)
        ddsk_ref[...] += jnp.sum(dyv * uv, axis=0, keepdims=True)
        for ch, states in _SUPER_BLOCKS:
            du = dsk_ref[:, ch] * dyv[:, ch]
            for st in states:
                g16 = g_ref[:, st].astype(BF16)
                du = du + _dot_nt(g16, bd_ref[ch, st])
                dbd_ref[ch, st] += _dot_tn(u16[:, ch], g16)
                dcdt_ref[ch, st] += _dot_tn(dy16[:, ch], s_ref[:, st].astype(BF16))
            du_ref[:, ch] = du

        sx_ref[pl.ds(8, tc), :] = s_ref[...]
        sx_ref[pl.ds(0, 8), :] = jnp.where(i == nc - 1, 0.0, sprev_ref[...])
        row = lax.broadcasted_iota(jnp.int32, (8, SCAN_LANES), 0)
        for lc in range(NS // SCAN_LANES):
            re_cols = pl.ds(lc * SCAN_LANES, SCAN_LANES)
            im_cols = pl.ds(NS + lc * SCAN_LANES, SCAN_LANES)

            def step(b, acc, re_cols=re_cols, im_cols=im_cols):
                ar, ai = acc
                off = pl.multiple_of(b * 8, 8)
                gr = g_ref[pl.ds(off, 8), re_cols]
                gi = g_ref[pl.ds(off, 8), im_cols]
                before = pl.ds(off, 8)
                here = pl.ds(off + 8, 8)
                sr = jnp.where(row == 0, sx_ref[before, re_cols][7:8], pltpu.roll(sx_ref[here, re_cols], 1, 0))
                si = jnp.where(row == 0, sx_ref[before, im_cols][7:8], pltpu.roll(sx_ref[here, im_cols], 1, 0))
                return ar + gr * sr + gi * si, ai + gi * sr - gr * si

            zero = jnp.zeros((8, SCAN_LANES), F32)
            ar, ai = lax.fori_loop(0, blocks, step, (zero, zero))
            dab_ref[0, :, re_cols] += ar
            dab_ref[1, :, re_cols] += ai

    rev = lambda i: (nc - 1 - i, 0)
    sprev = pl.BlockSpec((8, 2 * NS), lambda i: (jnp.maximum((nc - 1 - i) * blocks - 1, 0), 0))
    return pl.pallas_call(
        body, name="s5_bwd", grid=(nc,),
        in_specs=[pl.BlockSpec((tc, SSM_WIDTH), rev), pl.BlockSpec((tc, SSM_WIDTH), rev),
                  pl.BlockSpec((tc, 2 * NS), rev), sprev, _whole(), _whole(), _whole(), _whole()],
        out_specs=[pl.BlockSpec((tc, SSM_WIDTH), rev), _whole(), _whole(), _whole(), _whole()],
        out_shape=[jax.ShapeDtypeStruct((L, SSM_WIDTH), F32), jax.ShapeDtypeStruct((1, SSM_WIDTH), F32),
                   jax.ShapeDtypeStruct((SSM_WIDTH, 2 * NS), F32), jax.ShapeDtypeStruct((SSM_WIDTH, 2 * NS), F32),
                   jax.ShapeDtypeStruct((2, 8, NS), F32)],
        scratch_shapes=[pltpu.VMEM((tc, 2 * NS), F32), pltpu.VMEM((tc + 8, 2 * NS), F32), pltpu.VMEM((8, 2 * NS), F32)],
        compiler_params=_params(),
    )(dy, u, s, s, bd, cdt, d_skip, pwr)


def _branches(o_attn, y, gates, w_ab, w_glu, w_sb):
    ya = _dot(o_attn.astype(BF16), w_ab[...])
    gel = _gelu(y)
    glu = _dot(gel.astype(BF16), w_glu[...])
    p = glu[:, :SSM_WIDTH]
    sg = _sigmoid(glu[:, SSM_WIDTH:])
    ys2 = p * sg
    ysb = _dot(ys2.astype(BF16), w_sb[...])
    ga = gates[:, :D_MODEL]
    gs = gates[:, D_MODEL:]
    return ya, gel, p, sg, ys2, ysb, ga, gs


def _mix_out_fwd(x1, o_g, lse_g, y, gates, w_ab, w_glu, w_sb, w_out):
    L = x1.shape[0]
    tm = min(ROW_TILE, L)

    def body(x_ref, o0, o1, o2, l0, l1, l2, y_ref, gate_ref, wab_ref, wglu_ref, wsb_ref, wout_ref,
             x2_ref, oat_ref, lse0, lse1, lse2, scr):
        la, lb, lc = (_from_residues(ref, scr, d) for ref, d in zip((l0, l1, l2), DILATIONS))
        m = jnp.maximum(jnp.maximum(la, lb), lc)
        ea, eb, ec = jnp.exp(la - m), jnp.exp(lb - m), jnp.exp(lc - m)
        tot = ea + eb + ec
        oa, ob, oc = (_from_residues(ref, scr, d) for ref, d in zip((o0, o1, o2), DILATIONS))
        o_attn = (ea * oa + eb * ob + ec * oc) / tot
        oat_ref[...] = o_attn
        lse = m + jnp.log(tot)
        for ref, d in zip((lse0, lse1, lse2), DILATIONS):
            _to_residues(lse, ref, scr, d)
        ya, _, _, _, _, ysb, ga, gs = _branches(o_attn, y_ref[...], gate_ref[...], wab_ref, wglu_ref, wsb_ref)
        mix = ga * ya + gs * ysb
        x2_ref[...] = x_ref[...] + _dot(mix.astype(BF16), wout_ref[...])

    res = [_residue_spec(d, tm) for d in DILATIONS]
    return pl.pallas_call(
        body, name="mix_out_fwd", grid=(L // tm,),
        in_specs=[_rows(tm, D_MODEL)] + res * 2 + [_rows(tm, SSM_WIDTH), _rows(tm, 2 * D_MODEL)] + [_whole()] * 4,
        out_specs=[_rows(tm, D_MODEL), _rows(tm, GROUP_WIDTH)] + res,
        out_shape=[jax.ShapeDtypeStruct((L, D_MODEL), F32), jax.ShapeDtypeStruct((L, GROUP_WIDTH), F32)]
        + [_residue_shape(d, L, F32) for d in DILATIONS],
        scratch_shapes=[_residue_scratch(tm)],
        compiler_params=_params(),
    )(x1, *o_g, *lse_g, y, gates, w_ab, w_glu, w_sb, w_out)


def _mix_out_bwd(dx2, o_attn, y, gates, w_ab, w_glu, w_sb, w_out, head_sum):
    L = dx2.shape[0]
    tm = min(ROW_TILE, L)

    def body(dx_ref, oat_ref, y_ref, gate_ref, wab_ref, wglu_ref, wsb_ref, wout_ref, hs_ref,
             do0, do1, do2, dl0, dl1, dl2, dy_ref, dgp_ref, mix_ref, dya_ref, dys_ref, ys2_ref, gel_ref, dglu_ref,
             dgb_ref, scr):
        i = pl.program_id(0)
        o_attn = oat_ref[...]
        yv = y_ref[...]
        ya, gel, p, sg, ys2, ysb, ga, gs = _branches(o_attn, yv, gate_ref[...], wab_ref, wglu_ref, wsb_ref)
        mix_ref[...] = (ga * ya + gs * ysb).astype(BF16)
        ys2_ref[...] = ys2.astype(BF16)
        gel_ref[...] = gel.astype(BF16)
        dmix = _dot_nt(dx_ref[...].astype(BF16), wout_ref[...])
        dgp = jnp.concatenate([dmix * ya * ga * (1.0 - ga), dmix * ysb * gs * (1.0 - gs)], axis=1)
        dgp_ref[...] = dgp.astype(BF16)

        @pl.when(i == 0)
        def _():
            dgb_ref[...] = jnp.zeros_like(dgb_ref)

        dgb_ref[...] += jnp.sum(dgp, axis=0, keepdims=True)
        dya = (dmix * ga).astype(BF16)
        dys = (dmix * gs).astype(BF16)
        dya_ref[...] = dya
        dys_ref[...] = dys
        d_o = _dot_nt(dya, wab_ref[...])
        delta = _dot_exact(d_o * o_attn, hs_ref[...])
        for do_ref, dl_ref, d in zip((do0, do1, do2), (dl0, dl1, dl2), DILATIONS):
            _to_residues(d_o, do_ref, scr, d)
            _to_residues(delta, dl_ref, scr, d)
        dys2 = _dot_nt(dys, wsb_ref[...])
        dglu = jnp.concatenate([dys2 * sg, dys2 * p * sg * (1.0 - sg)], axis=1).astype(BF16)
        dglu_ref[...] = dglu
        dy_ref[...] = _dot_nt(dglu, wglu_ref[...]) * _gelu_grad(yv)

    grp = _rows(tm, GROUP_WIDTH)
    wide = _rows(tm, D_MODEL)
    half = _rows(tm, SSM_WIDTH)
    res = [_residue_spec(d, tm) for d in DILATIONS]
    sds = jax.ShapeDtypeStruct
    return pl.pallas_call(
        body, name="mix_out_bwd", grid=(L // tm,),
        in_specs=[wide, grp, half, _rows(tm, 2 * D_MODEL)] + [_whole()] * 5,
        out_specs=res + res + [half, _rows(tm, 2 * D_MODEL), wide, wide, wide, half, half, wide, _acc_row(2 * D_MODEL)],
        out_shape=[_residue_shape(d, L, BF16) for d in DILATIONS] + [_residue_shape(d, L, F32) for d in DILATIONS]
        + [sds((L, SSM_WIDTH), F32),
           sds((L, 2 * D_MODEL), BF16), sds((L, D_MODEL), BF16), sds((L, D_MODEL), BF16),
           sds((L, D_MODEL), BF16), sds((L, SSM_WIDTH), BF16), sds((L, SSM_WIDTH), BF16),
           sds((L, D_MODEL), BF16), sds((1, 2 * D_MODEL), F32)],
        scratch_shapes=[_residue_scratch(tm)],
        compiler_params=_params(),
    )(dx2, o_attn, y, gates, w_ab, w_glu, w_sb, w_out, head_sum)


def _adamw(w, g, m, v, name):
    R, C = w.shape
    tr = _row_tile(R, max(8, ADAMW_BLOCK_BYTES // (4 * C)))

    def body(w_ref, g_ref, m_ref, v_ref, d_ref, mo_ref, vo_ref):
        gv = g_ref[...]
        mn = ADAM_B1 * m_ref[...] + (1.0 - ADAM_B1) * gv
        vn = ADAM_B2 * v_ref[...] + (1.0 - ADAM_B2) * (gv * gv)
        m_hat = mn / (1.0 - ADAM_B1 ** ADAM_STEP)
        v_hat = vn / (1.0 - ADAM_B2 ** ADAM_STEP)
        d_ref[...] = -ADAM_LR * (m_hat / (jnp.sqrt(v_hat) + ADAM_EPS) + ADAM_WD * w_ref[...])
        mo_ref[...] = mn
        vo_ref[...] = vn

    blk = pl.BlockSpec((tr, C), lambda i: (i, 0))
    return pl.pallas_call(
        body, name=name, grid=(R // tr,),
        in_specs=[blk] * 4, out_specs=[blk] * 3,
        out_shape=[jax.ShapeDtypeStruct((R, C), F32)] * 3,
        compiler_params=_params(),
    )(w, g, m, v)


def _sum_chips_into_half(u, t, name):
    S, H, C = u.shape
    tr = _row_tile(H, 512)
    hb = H // tr

    def body(s_ref, t_ref, a_ref, b_ref, c_ref, o_ref):
        me = s_ref[1]
        others = (a_ref[...], b_ref[...], c_ref[...])
        acc = None
        for chip in range(S):
            below = others[min(chip, S - 2)]
            above = others[max(chip - 1, 0)]
            term = jnp.where(me == chip, t_ref[...], jnp.where(me > chip, below, above)).astype(F32)
            acc = term if acc is None else acc + term
        o_ref[...] = acc

    x, y, c = lax.axis_index("x"), lax.axis_index("y"), lax.axis_index("c")
    me = 2 * x + y
    scalars = jnp.stack([c, me] + [j + (j >= me).astype(jnp.int32) for j in range(S - 1)]).astype(jnp.int32)
    blk = (None, tr, C)
    return pl.pallas_call(
        body, name=name,
        grid_spec=pltpu.PrefetchScalarGridSpec(
            num_scalar_prefetch=1, grid=(hb,),
            in_specs=[pl.BlockSpec(blk, lambda i, s: (s[1], i, 0))]
            + [pl.BlockSpec(blk, functools.partial(lambda j, i, s: (s[2 + j], i, 0), j)) for j in range(S - 1)],
            out_specs=pl.BlockSpec((tr, C), lambda i, s: (s[0] * hb + i, 0))),
        out_shape=jax.ShapeDtypeStruct((2 * H, C), F32),
        compiler_params=_params(),
    )(scalars, t, u, u, u)


def _add_halves(g, r1, name):
    S, R, C = g.shape
    H = R // 2
    tr = _row_tile(H, 512)
    hb = H // tr

    def body(c_ref, g_ref, r_ref, o_ref):
        o_ref[...] = (g_ref[...] + r_ref[...]).astype(BF16)

    core = lax.axis_index("c").astype(jnp.int32).reshape(1)
    return pl.pallas_call(
        body, name=name,
        grid_spec=pltpu.PrefetchScalarGridSpec(
            num_scalar_prefetch=1, grid=(S, hb),
            in_specs=[pl.BlockSpec((None, tr, C), lambda j, i, c_ref: (j, c_ref[0] * hb + i, 0)),
                      pl.BlockSpec((None, tr, C), lambda j, i, c_ref: (j, i, 0))],
            out_specs=pl.BlockSpec((None, tr, C), lambda j, i, c_ref: (j, i, 0))),
        out_shape=jax.ShapeDtypeStruct((S, H, C), BF16),
        compiler_params=_params(),
    )(core, g, r1)


_ANY = pl.BlockSpec(memory_space=pl.ANY)


def _place():
    x, y, c = lax.axis_index("x"), lax.axis_index("y"), lax.axis_index("c")
    chips = [(1 - x, y), (x, 1 - y), (1 - x, 1 - y)]
    return x, y, c, chips


def _remote(src, dst, send_sems, recv_sems, k, device):
    return pltpu.make_async_remote_copy(src_ref=src, dst_ref=dst, send_sem=send_sems.at[k], recv_sem=recv_sems.at[k],
                                        device_id=device, device_id_type=MESH)


def _gather_parts(shapes, w_refs, out_refs, send_sems, recv_sems):
    n = len(shapes)
    x, y, c, chips = _place()
    me = 2 * x + y
    sibling = (x, y, 1 - c)

    def half(k, chip_idx, core):
        H = shapes[k][0] // 2
        return out_refs[k].at[chip_idx, pl.ds(core * H, H), :]

    mine = [_remote(w_refs[k], out_refs[k].at[me], send_sems, recv_sems, 6 * n + k, sibling) for k in range(n)]
    first = []
    for k in range(n):
        H = shapes[k][0] // 2
        for j, (cx, cy) in enumerate(chips):
            first.append(_remote(w_refs[k].at[pl.ds(c * H, H), :], half(k, me, c), send_sems, recv_sems,
                                 3 * k + j, (cx, cy, c)))

    def start():
        for cp in mine + first:
            cp.start()

    def finish():
        passed = []
        for k in range(n):
            for j, (cx, cy) in enumerate(chips):
                landed = half(k, 2 * cx + cy, c)
                _remote(landed, landed, send_sems, recv_sems, 3 * k + j, (cx, cy, c)).wait_recv()
                fwd = _remote(landed, landed, send_sems, recv_sems, 3 * n + 3 * k + j, sibling)
                fwd.start()
                passed.append(fwd)
        for k in range(n):
            for j, (cx, cy) in enumerate(chips):
                other = half(k, 2 * cx + cy, 1 - c)
                _remote(other, other, send_sems, recv_sems, 3 * n + 3 * k + j, sibling).wait_recv()
        for cp in mine:
            cp.wait_recv()
        for cp in first + passed + mine:
            cp.wait_send()

    return start, finish


def _gather_weights(shards, name):
    n = len(shards)

    def body(*refs):
        x, y, c, chips = _place()
        _handshake([(x, y, 1 - c)] + [(cx, cy, c) for cx, cy in chips])
        start, finish = _gather_parts([w.shape for w in shards], refs[:n], refs[n:2 * n], *refs[2 * n:2 * n + 2])
        start()
        finish()

    return _sequenced(body, name, shards, [jax.ShapeDtypeStruct((N_SHARD,) + w.shape, w.dtype) for w in shards],
                      7 * n, COLLECTIVE_IDS["gather"])


def _handshake(peers):
    barrier = pltpu.get_barrier_semaphore()
    for peer in peers:
        pl.semaphore_signal(barrier, inc=1, device_id=peer, device_id_type=MESH)
    pl.semaphore_wait(barrier, len(peers))


def _sequenced(body, name, ins, out_shapes, n_sems, collective_id):
    return pl.kernel(
        body, out_type=list(out_shapes), mesh=plsc.ScalarSubcoreMesh(axis_name="sequencer", num_cores=1), name=name,
        scratch_types=(pltpu.SemaphoreType.DMA((n_sems,)), pltpu.SemaphoreType.DMA((n_sems,))),
        compiler_params=pltpu.CompilerParams(collective_id=collective_id))(*ins)


def _swap_halves(gs, name, collective_id):
    n = len(gs)

    def body(*refs):
        g_refs, out_refs = refs[:n], refs[n:2 * n]
        send_sems, recv_sems = refs[2 * n:]
        x, y, c, _ = _place()
        _handshake([(x, y, 1 - c)])
        cps = []
        for k in range(n):
            H = gs[k].shape[1] // 2
            cp = _remote(g_refs[k].at[:, pl.ds((1 - c) * H, H), :], out_refs[k], send_sems, recv_sems, k, (x, y, 1 - c))
            cp.start()
            cps.append(cp)
        for cp in cps:
            cp.wait()

    return _sequenced(body, name, gs, [jax.ShapeDtypeStruct((g.shape[0], g.shape[1] // 2, g.shape[2]), g.dtype)
                                       for g in gs], n, collective_id)


def _exchange_chips(ts, name, collective_id):
    n = len(ts)

    def body(*refs):
        t_refs, out_refs = refs[:n], refs[n:2 * n]
        send_sems, recv_sems = refs[2 * n:]
        x, y, c, chips = _place()
        me = 2 * x + y
        _handshake([(cx, cy, c) for cx, cy in chips])
        sent = []
        for k in range(n):
            for j, (cx, cy) in enumerate(chips):
                cp = _remote(t_refs[k].at[2 * cx + cy], out_refs[k].at[me], send_sems, recv_sems, 3 * k + j, (cx, cy, c))
                cp.start()
                sent.append(cp)
        for k in range(n):
            for j, (cx, cy) in enumerate(chips):
                slot = out_refs[k].at[2 * cx + cy]
                _remote(slot, slot, send_sems, recv_sems, 3 * k + j, (cx, cy, c)).wait_recv()
        for cp in sent:
            cp.wait_send()

    return _sequenced(body, name, ts, [jax.ShapeDtypeStruct(t.shape, t.dtype) for t in ts], 3 * n, collective_id)


def _join_halves(fs, name):
    n = len(fs)

    def body(*refs):
        out_refs = refs[n:2 * n]
        send_sems, recv_sems, _ = refs[2 * n:]
        x, y, c, _ = _place()
        sent = []
        for k in range(n):
            H = fs[k].shape[0] // 2
            here = out_refs[k].at[pl.ds(c * H, H), :]
            cp = _remote(here, here, send_sems, recv_sems, k, (x, y, 1 - c))
            cp.start()
            sent.append(cp)
        for k in range(n):
            H = fs[k].shape[0] // 2
            other = out_refs[k].at[pl.ds((1 - c) * H, H), :]
            _remote(other, other, send_sems, recv_sems, k, (x, y, 1 - c)).wait_recv()
        for cp in sent:
            cp.wait_send()

    return pl.pallas_call(
        body, name=name,
        in_specs=[_ANY] * n, out_specs=[_ANY] * n,
        out_shape=[jax.ShapeDtypeStruct(f.shape, f.dtype) for f in fs],
        input_output_aliases={k: k for k in range(n)},
        scratch_shapes=[pltpu.SemaphoreType.DMA((n,)), pltpu.SemaphoreType.DMA((n,)), pltpu.SemaphoreType.DMA((1,))],
    )(*fs)


def _gather_small(v):
    R, C = v.shape

    def body(v_ref, out_ref, send_sems, recv_sems):
        x, y, c, _ = _place()
        me = 4 * x + 2 * y + c
        flips = [(fx, fy, fc) for fx in (0, 1) for fy in (0, 1) for fc in (0, 1)][1:]
        peers = [((1 - x) if fx else x, (1 - y) if fy else y, (1 - c) if fc else c) for fx, fy, fc in flips]
        _handshake(peers)
        sent = []
        for j, peer in enumerate(peers):
            cp = _remote(v_ref, out_ref.at[me], send_sems, recv_sems, j, peer)
            cp.start()
            sent.append(cp)
        for j, peer in enumerate(peers):
            slot = out_ref.at[4 * peer[0] + 2 * peer[1] + peer[2]]
            _remote(slot, slot, send_sems, recv_sems, j, peer).wait_recv()
        for cp in sent:
            cp.wait_send()

    return _sequenced(body, "gather_small", [v], [jax.ShapeDtypeStruct((8, R, C), F32)], 7,
                      COLLECTIVE_IDS["gather_small"])[0]


def _sum_devices(x, own, name):
    S, R, C = x.shape
    tr = _row_tile(R, 2048)

    def body(s_ref, x_ref, own_ref, o_ref):
        me = s_ref[0]
        acc = None
        for k in range(S):
            term = jnp.where(me == k, own_ref[...], x_ref[k])
            acc = term if acc is None else acc + term
        o_ref[...] = acc

    x_, y_, c_ = lax.axis_index("x"), lax.axis_index("y"), lax.axis_index("c")
    me = (4 * x_ + 2 * y_ + c_).astype(jnp.int32).reshape(1)
    return pl.pallas_call(
        body, name=name,
        grid_spec=pltpu.PrefetchScalarGridSpec(
            num_scalar_prefetch=1, grid=(R // tr,),
            in_specs=[pl.BlockSpec((S, tr, C), lambda i, s: (0, i, 0)), pl.BlockSpec((tr, C), lambda i, s: (i, 0))],
            out_specs=pl.BlockSpec((tr, C), lambda i, s: (i, 0))),
        out_shape=jax.ShapeDtypeStruct((R, C), F32),
        compiler_params=_params(),
    )(me, x, own)


def _after(earlier, arrays):
    return lax.optimization_barrier((earlier, arrays))


def _reduce_swap(gs, tag, earlier):
    gs = _after(earlier, gs)[1]
    return gs, _swap_halves(gs, "reduce_swap_" + tag, COLLECTIVE_IDS["swap_" + tag])


def _reduce_exchange(gs, r1, names, tag, later_than):
    r1 = _after(later_than, r1)[1]
    ts = [_add_halves(g, r, "reduce_add_cores_" + nm) for g, r, nm in zip(gs, r1, names)]
    us = _exchange_chips(ts, "reduce_exchange_" + tag, COLLECTIVE_IDS["exchange_" + tag])
    return us, ts


def _reduce_finish(us, ts, names, tag):
    fs = [_sum_chips_into_half(u, t, "reduce_add_chips_" + nm) for u, t, nm in zip(us, ts, names)]
    return _join_halves(fs, "reduce_join_" + tag)


BIG = ["ffn1_w_gate", "ffn1_w_up", "ffn1_w_down", "w_in", "ssm_w_glu", "w_attn_branch", "w_ssm_branch",
       "w_out", "ffn2_w_gate", "ffn2_w_up", "ffn2_w_down"]
SMALL = ["ffn1_norm", "mix_norm", "gate_bias", "rel_bias_table", "ssm_a_re", "ssm_a_im", "ssm_log_dt",
         "ssm_b_re", "ssm_b_im", "ssm_c_re", "ssm_c_im", "ssm_d", "ffn2_norm", "final_norm"]
ORDER = ["ffn1_norm", "ffn1_w_gate", "ffn1_w_up", "ffn1_w_down", "mix_norm", "w_in", "gate_bias", "rel_bias_table",
         "ssm_a_re", "ssm_a_im", "ssm_log_dt", "ssm_b_re", "ssm_b_im", "ssm_c_re", "ssm_c_im", "ssm_d",
         "ssm_w_glu", "w_attn_branch", "w_ssm_branch", "w_out", "ffn2_norm", "ffn2_w_gate", "ffn2_w_up",
         "ffn2_w_down", "final_norm"]


_SMALL_TILE = 8 * LANES


def _pack_small(arrays):
    rows = []
    for a in arrays:
        flat = a.reshape(-1).astype(F32)
        rows.append(jnp.pad(flat, (0, (-flat.shape[0]) % _SMALL_TILE)).reshape(-1, LANES))
    return jnp.concatenate(rows, axis=0)


def _unpack_small(packed, shapes):
    out, r0 = [], 0
    for shp in shapes:
        n = math.prod(shp)
        rows = 8 * -(-n // _SMALL_TILE)
        out.append(packed[r0:r0 + rows].reshape(-1)[:n].reshape(shp))
        r0 += rows
    return out


def _split_cols(g):
    K, N = g.shape
    return g.reshape(K, N_SHARD, N // N_SHARD).transpose(1, 0, 2)


def _join_cols(w):
    S, K, n = w.shape
    return w.transpose(1, 0, 2).reshape(K, S * n)


COL_SHARDED = ("ssm_w_glu", "w_attn_branch", "w_ssm_branch")
TRANSPOSED = ("ffn1_w_gate", "ffn1_w_up", "ffn2_w_gate", "ffn2_w_up", "w_in")


def _shard_2d(name, arr):
    two_d = arr.reshape(arr.shape[-2:])
    return two_d.T if name in TRANSPOSED else two_d


def _shard_nd(name, two_d, shape):
    return (two_d.T if name in TRANSPOSED else two_d).reshape(shape)


class _GradSync:
    def __init__(self, weights, moms, vels):
        self.weights, self.moms, self.vels = weights, moms, vels
        self.grads, self.delta, self.new_m, self.new_v = {}, {}, {}, {}
        self.loss = None
        self._earlier = []
        self._swapped = {}
        self._exchanged = {}

    def swap(self, tag, gw, later_than=()):
        gs = []
        for n in REDUCE_GROUPS[tag]:
            g = gw[n]
            if n in COL_SHARDED:
                g = _split_cols(g)
            elif n in ("w_out", "w_in"):
                g = g.reshape(N_SHARD, g.shape[0] // N_SHARD, g.shape[1])
            gs.append(g)
        self._swapped[tag] = _reduce_swap(gs, tag, list(self._earlier) + list(later_than))
        self._earlier = self._swapped[tag][1]

    def exchange(self, tag, later_than):
        gs, r1 = self._swapped[tag]
        us, ts = _reduce_exchange(gs, r1, REDUCE_GROUPS[tag], tag, later_than)
        self._exchanged[tag] = (us, ts)
        self._earlier = us

    def small_ready(self, gs, loss_blk, later_than=()):
        _, (mine,) = _after(list(self._earlier) + list(later_than),
                            [_pack_small([gs[n] for n in SMALL] + [loss_blk[0:1, :]])])
        others = _gather_small(mine)
        self._exchanged["small"] = (others, mine)
        self._earlier = [others]

    def finish(self, tag):
        made = []
        if tag == "small":
            others, mine = self._exchanged[tag]
            shapes = [self.weights[n].shape for n in SMALL]
            total = _unpack_small(_sum_devices(others, mine, "sum_small"), shapes + [(128,)])
            self.loss = total[-1][0]
            self.grads.update(zip(SMALL, total[:-1]))
            packed = [_pack_small([src[n] for n in SMALL]) for src in (self.weights, self.grads, self.moms, self.vels)]
            for dst, res in zip((self.delta, self.new_m, self.new_v), _adamw(*packed, "adamw_small")):
                dst.update(zip(SMALL, _unpack_small(res, shapes)))
            for n in SMALL:
                made += [self.grads[n], self.delta[n], self.new_m[n], self.new_v[n]]
            return made + [self.loss]
        names = REDUCE_GROUPS[tag]
        us, ts = self._exchanged[tag]
        for n, g in zip(names, _reduce_finish(us, ts, names, tag)):
            shp = self.weights[n].shape
            d, m, v = _adamw(_shard_2d(n, self.weights[n]), g, _shard_2d(n, self.moms[n]), _shard_2d(n, self.vels[n]),
                             "adamw_" + n)
            self.grads[n], self.delta[n] = _shard_nd(n, g, shp), _shard_nd(n, d, shp)
            self.new_m[n], self.new_v[n] = _shard_nd(n, m, shp), _shard_nd(n, v, shp)
            made += [self.grads[n], self.delta[n], self.new_m[n], self.new_v[n]]
        return made

    def finish_all(self):
        self.exchange("ffn1", later_than=self.finish("ffn2"))
        for tag in ("mixer", "w_in", "small", "ffn1"):
            self.finish(tag)


def _local_step(x, target, w, later, small, sync):
    L = x.shape[0]
    row = lambda v: v.reshape(1, -1)

    a_re, a_im = small["ssm_a_re"].reshape(1, NS), small["ssm_a_im"].reshape(1, NS)
    ldt = jnp.repeat(small["ssm_log_dt"].reshape(SSM_GROUPS), SSM_STATE).reshape(1, NS)
    to_cn = lambda b: b.reshape(SSM_GROUPS, SSM_STATE, SSM_GROUP).transpose(2, 0, 1).reshape(SSM_GROUP, NS)
    c_to_cn = lambda c: c.reshape(SSM_GROUPS, SSM_GROUP, SSM_STATE).transpose(1, 0, 2).reshape(SSM_GROUP, NS)
    b_re, b_im = to_cn(small["ssm_b_re"]), to_cn(small["ssm_b_im"])
    c_re, c_im = c_to_cn(small["ssm_c_re"]), c_to_cn(small["ssm_c_im"])
    d_skip = row(small["ssm_d"])
    pw, pwr, bd, cdt = _disc_fwd(a_re, a_im, ldt, b_re, b_im, c_re, c_im)

    onehot = _bucket_onehot()
    table_t = small["rel_bias_table"].T.reshape(3, HEADS_PER_GROUP, N_BUCKETS)
    table_t = jnp.pad(table_t, ((0, 0), (0, 8 - HEADS_PER_GROUP), (0, 0)))
    bias = _bias_expand(table_t, onehot)[:, :, :HEADS_PER_GROUP].reshape(
        3, 2, HEADS_PER_GROUP, ATTN_BLOCK, 2 * ATTN_BLOCK)

    n1, nm, n2, nf = row(small["ffn1_norm"]), row(small["mix_norm"]), row(small["ffn2_norm"]), row(small["final_norm"])
    gate_bias = row(small["gate_bias"])

    x1, a1, b1, *later_full = _ffn_fwd(x, n1, w["ffn1_w_gate"], w["ffn1_w_up"], w["ffn1_w_down"], "ffn1_fwd",
                                       carried=list(later.values()))
    w = dict(w, **dict(zip(later, later_full)))
    for n in COL_SHARDED:
        w[n] = _join_cols(w[n])
    w["w_out"] = w["w_out"].reshape(D_MODEL, D_MODEL)
    w["w_in"] = w["w_in"].reshape(IN_WIDTH, D_MODEL)
    *qkv, u, gates = _mix_in_fwd(x1, nm, w["w_in"], gate_bias)
    q, k, v = qkv[0:3], qkv[3:6], qkv[6:9]
    o_g, lse_g = [], []
    for grp in range(3):
        o, lse = _attn_fwd(q[grp], k[grp], v[grp], bias[grp], f"attn_fwd_{grp}")
        o_g.append(o)
        lse_g.append(lse)
    y, s = _ssm_fwd(u, bd, cdt, d_skip, pw)
    x2, o_attn, *lse_tot = _mix_out_fwd(x1, o_g, lse_g, y, gates, w["w_attn_branch"], w["ssm_w_glu"],
                                        w["w_ssm_branch"], w["w_out"])
    x3, a2, b2 = _ffn_fwd(x2, n2, w["ffn2_w_gate"], w["ffn2_w_up"], w["ffn2_w_down"], "ffn2_fwd")
    loss_blk, dx3, d_nf = _loss_fwd_bwd(x3, nf, target)

    gw, gs = {}, {}
    gs["final_norm"] = d_nf

    dx2, da, db, sact, h, d_out, gs["ffn2_norm"] = _ffn_bwd(dx3, x2, n2, a2, b2, w["ffn2_w_gate"], w["ffn2_w_up"],
                                                            w["ffn2_w_down"], "ffn2_bwd")
    gw["ffn2_w_gate"] = _matmul_tn(da, h[None], "ffn2_dw_gate")
    gw["ffn2_w_up"] = _matmul_tn(db, h[None], "ffn2_dw_up")
    gw["ffn2_w_down"] = _matmul_tn(sact, d_out[None], "ffn2_dw_down")
    sync.swap("ffn2", gw)

    head_sum = (jnp.arange(GROUP_WIDTH)[:, None] // HEAD_DIM == jnp.arange(GROUP_WIDTH)[None, :] // HEAD_DIM).astype(F32)
    (*d_o_delta, dy, dgp, mix, dya, dys, ys2, gel, dglu, gs["gate_bias"]) = _mix_out_bwd(
        dx2, o_attn, y, gates, w["w_attn_branch"], w["ssm_w_glu"], w["w_ssm_branch"], w["w_out"], head_sum)
    sync.exchange("ffn2", later_than=[dy])
    d_o, delta = d_o_delta[0:3], d_o_delta[3:6]
    gw["w_out"] = _matmul_tn(mix[None], dx2[None], "dw_out")[0]
    gw["w_attn_branch"] = _matmul_tn(o_attn[None], dya[None], "dw_attn_branch")[0]
    gw["w_ssm_branch"] = _matmul_tn(ys2[None], dys[None], "dw_ssm_branch")[0]
    gw["ssm_w_glu"] = _matmul_tn(gel[None], dglu[None], "dw_glu")[0]

    dqs, dks, dvs, dsums = [], [], [], []
    for grp in range(3):
        dq, dk, dv, dsum = _attn_bwd(q[grp], k[grp], v[grp], d_o[grp], lse_tot[grp], delta[grp], bias[grp],
                                     f"attn_bwd_{grp}")
        dqs.append(dq)
        dks.append(dk)
        dvs.append(dv)
        dsums.append(dsum.reshape(HEADS_PER_GROUP, -1))
    dsum_all = jnp.pad(jnp.stack(dsums), ((0, 0), (0, 8 - HEADS_PER_GROUP), (0, 0)))
    d_table = _bias_reduce(dsum_all, onehot)[:, :HEADS_PER_GROUP]
    gs["rel_bias_table"] = d_table.reshape(3 * HEADS_PER_GROUP, N_BUCKETS).T

    du, gs["ssm_d"], d_bd, d_cdt, d_ab = _ssm_bwd(dy, u, s, bd, cdt, d_skip, pwr)
    sync.swap("mixer", gw, later_than=[du])
    sync.exchange("mixer", later_than=[dqs[2]])
    group_sum =(jnp.arange(NS)[:, None] // SSM_STATE == jnp.arange(128)[None, :]).astype(F32)
    d_are, d_aim, d_ldt, d_bre, d_bim, d_cre, d_cim = _disc_bwd(a_re, a_im, ldt, b_re, b_im, d_bd, d_cdt, d_ab, group_sum)
    gs["ssm_a_re"], gs["ssm_a_im"] = d_are, d_aim
    gs["ssm_log_dt"] = d_ldt[0, :SSM_GROUPS]
    from_cn = lambda t: t.reshape(SSM_GROUP, SSM_GROUPS, SSM_STATE).transpose(1, 2, 0)
    c_from_cn = lambda t: t.reshape(SSM_GROUP, SSM_GROUPS, SSM_STATE).transpose(1, 0, 2)
    gs["ssm_b_re"], gs["ssm_b_im"] = from_cn(d_bre), from_cn(d_bim)
    gs["ssm_c_re"], gs["ssm_c_im"] = c_from_cn(d_cre), c_from_cn(d_cim)

    dx1, hm, dz, gs["mix_norm"] = _mix_in_bwd(dx2, x1, nm, dqs + dks + dvs, du, dgp, w["w_in"])
    gw["w_in"] = _matmul_tn(dz[None], hm[None], "dw_in")[0]
    sync.swap("w_in", gw)

    dx0, da, db, sact, h, d_out, gs["ffn1_norm"] = _ffn_bwd(dx1, x, n1, a1, b1, w["ffn1_w_gate"], w["ffn1_w_up"],
                                                            w["ffn1_w_down"], "ffn1_bwd")
    sync.exchange("w_in", later_than=[dx0])
    gw["ffn1_w_gate"] = _matmul_tn(da, h[None], "ffn1_dw_gate")
    gw["ffn1_w_up"] = _matmul_tn(db, h[None], "ffn1_dw_up")
    sync.small_ready(gs, loss_blk, later_than=[gw["ffn1_w_up"]])
    gw["ffn1_w_down"] = _matmul_tn(sact, d_out[None], "ffn1_dw_down")
    sync.swap("ffn1", gw)
    return dx0


def kernel(x, ffn1_norm, ffn1_w_gate, ffn1_w_up, ffn1_w_down, mix_norm, w_in, gate_bias, rel_bias_table, ssm_a_re, ssm_a_im, ssm_log_dt, ssm_b_re, ssm_b_im, ssm_c_re, ssm_c_im, ssm_d, ssm_w_glu, w_attn_branch, w_ssm_branch, w_out, ffn2_norm, ffn2_w_gate, ffn2_w_up, ffn2_w_down, final_norm, loss_target, m_ffn1_norm, m_ffn1_w_gate, m_ffn1_w_up, m_ffn1_w_down, m_mix_norm, m_w_in, m_gate_bias, m_rel_bias_table, m_ssm_a_re, m_ssm_a_im, m_ssm_log_dt, m_ssm_b_re, m_ssm_b_im, m_ssm_c_re, m_ssm_c_im, m_ssm_d, m_ssm_w_glu, m_w_attn_branch, m_w_ssm_branch, m_w_out, m_ffn2_norm, m_ffn2_w_gate, m_ffn2_w_up, m_ffn2_w_down, m_final_norm, v_ffn1_norm, v_ffn1_w_gate, v_ffn1_w_up, v_ffn1_w_down, v_mix_norm, v_w_in, v_gate_bias, v_rel_bias_table, v_ssm_a_re, v_ssm_a_im, v_ssm_log_dt, v_ssm_b_re, v_ssm_b_im, v_ssm_c_re, v_ssm_c_im, v_ssm_d, v_ssm_w_glu, v_w_attn_branch, v_w_ssm_branch, v_w_out, v_ffn2_norm, v_ffn2_w_gate, v_ffn2_w_up, v_ffn2_w_down, v_final_norm):
    args = dict(locals())
    weights = {n: args[n] for n in ORDER}
    moms = {n: args["m_" + n] for n in ORDER}
    vels = {n: args["v_" + n] for n in ORDER}

    shard2d = {n: _shard_2d(n, weights[n]) for n in BIG}
    first, rest = BIG[:3], BIG[3:]
    full = dict(zip(first, _gather_weights([shard2d[n].astype(BF16) for n in first], "gather_ffn1_weights")))
    later = {n: shard2d[n].astype(BF16) for n in rest}

    small = {n: weights[n] for n in SMALL}
    sync = _GradSync(weights, moms, vels)
    grad_x = _local_step(x[0], loss_target[0], full, later, small, sync)
    sync.finish_all()
    return (sync.loss, grad_x[None], *[sync.grads[n] for n in ORDER], *[sync.delta[n] for n in ORDER],
            *[sync.new_m[n] for n in ORDER], *[sync.new_v[n] for n in ORDER])
```

```python
import functools
import math

import jax
import jax.numpy as jnp
from jax import lax
from jax.experimental import pallas as pl
from jax.experimental.pallas import tpu as pltpu
from jax.experimental.pallas import tpu_sc as plsc

F32 = jnp.float32
BF16 = jnp.bfloat16
MESH = pl.DeviceIdType.MESH

D_MODEL = 1024
D_FF = 2816
HEAD_DIM = 64
HEADS_PER_GROUP = 4
DILATIONS = (1, 4, 16)
WINDOW_STEPS = 128
ATTN_BLOCK = 128
ATTN_QB = 16
GROUP_WIDTH = HEADS_PER_GROUP * HEAD_DIM
ATTN_WIDTH = 3 * GROUP_WIDTH
N_BUCKETS = 32
MAX_DISTANCE = 2048
NEG_INF = -1e30
SSM_WIDTH = 512
SSM_GROUP = 16
SSM_GROUPS = 32
SSM_STATE = 64
NS = SSM_GROUPS * SSM_STATE
EPS = 1e-6
IN_WIDTH = 3 * ATTN_WIDTH + SSM_WIDTH + 2 * D_MODEL
Q_SCALE = HEAD_DIM ** -0.5
N_SHARD = 4
FF_SHARD = D_FF // N_SHARD
ADAM_LR, ADAM_B1, ADAM_B2, ADAM_EPS, ADAM_WD, ADAM_STEP = 0.001, 0.9, 0.999, 1e-08, 0.01, 10

LANES = 128
VMEM_LIMIT = 56 * 1024 * 1024
ROW_TILE = 512
FFN_BWD_TILE = 256
SSM_CHUNK = 256
SSM_FWD_CHUNK = 1024
SCAN_LANES = 512
ADAMW_BLOCK_BYTES = 2 << 20
TN_VMEM_BUDGET = 40 * 1024 * 1024
REDUCE_GROUPS = {
    "ffn2": ["ffn2_w_gate", "ffn2_w_up", "ffn2_w_down"],
    "mixer": ["w_out", "w_attn_branch", "w_ssm_branch", "ssm_w_glu"],
    "w_in": ["w_in"],
    "ffn1": ["ffn1_w_gate", "ffn1_w_up", "ffn1_w_down"],
}
COLLECTIVE_IDS = {name: i for i, name in enumerate(
    ["gather", "gather_small"] + [stage + "_" + tag for tag in REDUCE_GROUPS for stage in ("swap", "exchange")])}


def _params(**kw):
    return pltpu.CompilerParams(vmem_limit_bytes=VMEM_LIMIT, **kw)


def _dot(a, b):
    return jnp.dot(a, b, preferred_element_type=F32)


def _dot_nt(a, b):
    return lax.dot_general(a, b, (((1,), (1,)), ((), ())), preferred_element_type=F32)


def _dot_tn(a, b):
    return lax.dot_general(a, b, (((0,), (0,)), ((), ())), preferred_element_type=F32)


def _dot_exact(a, b):
    return jnp.dot(a, b, preferred_element_type=F32, precision=lax.Precision.HIGHEST)


def _dot_nt_exact(a, b):
    return lax.dot_general(a, b, (((1,), (1,)), ((), ())), preferred_element_type=F32,
                           precision=lax.Precision.HIGHEST)


def _rms(x):
    r = lax.rsqrt(jnp.mean(x * x, axis=-1, keepdims=True) + EPS)
    return r, x * r


def _rms_bwd(dh, g, r, xhat):
    dxh = dh * g
    return r * (dxh - xhat * jnp.mean(dxh * xhat, axis=-1, keepdims=True))


def _sigmoid(x):
    return 0.5 + 0.5 * jnp.tanh(0.5 * x)


_GELU_C = math.sqrt(2.0 / math.pi)


def _gelu(x):
    return 0.5 * x * (1.0 + jnp.tanh(_GELU_C * (x + 0.044715 * x * x * x)))


def _gelu_grad(x):
    t = jnp.tanh(_GELU_C * (x + 0.044715 * x * x * x))
    return 0.5 * (1.0 + t) + 0.5 * x * (1.0 - t * t) * _GELU_C * (1.0 + 3 * 0.044715 * x * x)


def _whole():
    return pl.BlockSpec(memory_space=pltpu.VMEM)


def _row_tile(rows, cap):
    if rows <= cap:
        return rows
    return max(t for t in range(8, cap + 1, 8) if rows % t == 0)


def _rows(tm, w):
    return pl.BlockSpec((tm, w), lambda i: (i, 0))


def _acc_row(w):
    return pl.BlockSpec((1, w), lambda i: (0, 0))


def _ffn_fwd(x, g, wg, wu, wd, name, carried=()):
    L = x.shape[0]
    tm = min(ROW_TILE, L)
    n = len(carried)
    steps = L // tm

    def body(x_ref, g_ref, wg_ref, wu_ref, wd_ref, *refs):
        shard_refs, (xo_ref, a_ref, b_ref), full_refs, sems = refs[:n], refs[n:n + 3], refs[n + 3:2 * n + 3], refs[2 * n + 3:]
        if n:
            start, finish = _gather_parts([w.shape for w in carried], shard_refs, full_refs, *sems)
            pl.when(pl.program_id(0) == 0)(start)
        xv = x_ref[...]
        r, xhat = _rms(xv)
        h = (xhat * g_ref[...]).astype(BF16)
        acc = jnp.zeros((tm, D_MODEL), F32)
        for j in range(N_SHARD):
            a = _dot_nt(h, wg_ref[j])
            b = _dot_nt(h, wu_ref[j])
            a_ref[j] = a.astype(BF16)
            b_ref[j] = b.astype(BF16)
            s = (a * _sigmoid(a) * b).astype(BF16)
            acc = acc + _dot(s, wd_ref[j])
        xo_ref[...] = xv + 0.5 * acc
        if n:
            pl.when(pl.program_id(0) == steps - 1)(finish)

    act = pl.BlockSpec((N_SHARD, tm, FF_SHARD), lambda i: (0, i, 0))
    return pl.pallas_call(
        body, name=name, grid=(steps,),
        in_specs=[_rows(tm, D_MODEL), _whole(), _whole(), _whole(), _whole()] + [_ANY] * n,
        out_specs=[_rows(tm, D_MODEL), act, act] + [_ANY] * n,
        out_shape=[jax.ShapeDtypeStruct((L, D_MODEL), F32),
                   jax.ShapeDtypeStruct((N_SHARD, L, FF_SHARD), BF16),
                   jax.ShapeDtypeStruct((N_SHARD, L, FF_SHARD), BF16)]
        + [jax.ShapeDtypeStruct((N_SHARD,) + w.shape, w.dtype) for w in carried],
        scratch_shapes=[pltpu.SemaphoreType.DMA((7 * n,)), pltpu.SemaphoreType.DMA((7 * n,))] if n else [],
        compiler_params=_params(),
    )(x, g, wg, wu, wd, *carried)


def _ffn_bwd(dxo, x, g, a, b, wg, wu, wd, name):
    L = x.shape[0]
    tm = min(FFN_BWD_TILE, L)

    def body(dxo_ref, x_ref, g_ref, a_ref, b_ref, wg_ref, wu_ref, wd_ref,
             dxi_ref, da_ref, db_ref, s_ref, h_ref, do_ref, dg_ref):
        i = pl.program_id(0)
        xv = x_ref[...]
        gv = g_ref[...]
        r, xhat = _rms(xv)
        h_ref[...] = (xhat * gv).astype(BF16)
        dxo_v = dxo_ref[...]
        d_out = (0.5 * dxo_v).astype(BF16)
        do_ref[...] = d_out
        dh = jnp.zeros((tm, D_MODEL), F32)
        for j in range(N_SHARD):
            av = a_ref[j].astype(F32)
            bv = b_ref[j].astype(F32)
            sg = _sigmoid(av)
            sl = av * sg
            ds = _dot_nt(d_out, wd_ref[j])
            dbv = (ds * sl).astype(BF16)
            dav = (ds * bv * (sg * (1.0 + av * (1.0 - sg)))).astype(BF16)
            da_ref[j] = dav
            db_ref[j] = dbv
            s_ref[j] = (sl * bv).astype(BF16)
            dh = dh + _dot(dav, wg_ref[j]) + _dot(dbv, wu_ref[j])

        @pl.when(i == 0)
        def _():
            dg_ref[...] = jnp.zeros_like(dg_ref)

        dg_ref[...] += jnp.sum(dh * xhat, axis=0, keepdims=True)
        dxi_ref[...] = dxo_v + _rms_bwd(dh, gv, r, xhat)

    act = pl.BlockSpec((N_SHARD, tm, FF_SHARD), lambda i: (0, i, 0))
    act_shape = jax.ShapeDtypeStruct((N_SHARD, L, FF_SHARD), BF16)
    return pl.pallas_call(
        body, name=name, grid=(L // tm,),
        in_specs=[_rows(tm, D_MODEL), _rows(tm, D_MODEL), _whole(), act, act, _whole(), _whole(), _whole()],
        out_specs=[_rows(tm, D_MODEL), act, act, act, _rows(tm, D_MODEL), _rows(tm, D_MODEL), _acc_row(D_MODEL)],
        out_shape=[jax.ShapeDtypeStruct((L, D_MODEL), F32), act_shape, act_shape, act_shape,
                   jax.ShapeDtypeStruct((L, D_MODEL), BF16), jax.ShapeDtypeStruct((L, D_MODEL), BF16),
                   jax.ShapeDtypeStruct((1, D_MODEL), F32)],
        compiler_params=_params(),
    )(dxo, x, g, a, b, wg, wu, wd)


def _matmul_tn(a, b, name):
    ja, L, K = a.shape
    jb, _, N = b.shape
    J = max(ja, jb)
    splits = [s for s in (1, 2, 4, 8) if s == 1 or N % (s * LANES) == 0]
    nsplit = next((s for s in splits if 2 * K * (N // s) * 4 <= TN_VMEM_BUDGET // 2), splits[-1])
    nc = N // nsplit
    left = TN_VMEM_BUDGET - 2 * K * nc * 4
    row_bytes = 2 * (K * a.dtype.itemsize + nc * b.dtype.itemsize)
    tm = next((t for t in (4096, 2048, 1024, 512, 256) if L % t == 0 and t * row_bytes <= left), min(128, L))

    def body(a_ref, b_ref, o_ref):
        @pl.when(pl.program_id(2) == 0)
        def _():
            o_ref[...] = jnp.zeros_like(o_ref)

        o_ref[...] += _dot_tn(a_ref[...].astype(BF16), b_ref[...].astype(BF16))

    return pl.pallas_call(
        body, name=name, grid=(J, nsplit, L // tm),
        in_specs=[pl.BlockSpec((None, tm, K), (lambda j, s, i: (j, i, 0)) if ja > 1 else (lambda j, s, i: (0, i, 0))),
                  pl.BlockSpec((None, tm, nc), (lambda j, s, i: (j, i, s)) if jb > 1 else (lambda j, s, i: (0, i, s)))],
        out_specs=pl.BlockSpec((None, K, nc), lambda j, s, i: (j, 0, s)),
        out_shape=jax.ShapeDtypeStruct((J, K, N), F32),
        compiler_params=_params(),
    )(a, b)


def _loss_fwd_bwd(x, g, target):
    L = x.shape[0]
    tm = min(ROW_TILE, L)

    def body(x_ref, g_ref, t_ref, loss_ref, dx_ref, dg_ref):
        i = pl.program_id(0)
        xv = x_ref[...]
        gv = g_ref[...]
        r, xhat = _rms(xv)
        err = xhat * gv - t_ref[...]
        part = 0.5 * jnp.sum(jnp.sum(err * err, axis=1, keepdims=True) * (1.0 / D_MODEL), axis=0, keepdims=True)
        dy = err * (1.0 / D_MODEL)

        @pl.when(i == 0)
        def _():
            dg_ref[...] = jnp.zeros_like(dg_ref)
            loss_ref[...] = jnp.zeros_like(loss_ref)

        loss_ref[...] += jnp.broadcast_to(part, loss_ref.shape)
        dg_ref[...] += jnp.sum(dy * xhat, axis=0, keepdims=True)
        dx_ref[...] = _rms_bwd(dy, gv, r, xhat)

    return pl.pallas_call(
        body, name="loss_fwd_bwd", grid=(L // tm,),
        in_specs=[_rows(tm, D_MODEL), _whole(), _rows(tm, D_MODEL)],
        out_specs=[pl.BlockSpec((8, 128), lambda i: (0, 0)), _rows(tm, D_MODEL), _acc_row(D_MODEL)],
        out_shape=[jax.ShapeDtypeStruct((8, 128), F32), jax.ShapeDtypeStruct((L, D_MODEL), F32),
                   jax.ShapeDtypeStruct((1, D_MODEL), F32)],
        compiler_params=_params(),
    )(x, g, target)


_C_K = ATTN_WIDTH
_C_V = 2 * ATTN_WIDTH
_C_U = 3 * ATTN_WIDTH
_C_G = _C_U + SSM_WIDTH


def _residue_spec(d, tm):
    return pl.BlockSpec((d, tm // d, GROUP_WIDTH), lambda i: (0, i, 0))


def _residue_shape(d, L, dtype):
    return jax.ShapeDtypeStruct((d, L // d, GROUP_WIDTH), dtype)


def _residue_scratch(tm):
    return pltpu.VMEM((GROUP_WIDTH // LANES, tm, LANES), F32)


def _to_residues(val, out_ref, scr, d):
    if d == 1:
        out_ref[0] = val.astype(out_ref.dtype)
        return
    tm = val.shape[0]
    for half in range(GROUP_WIDTH // LANES):
        cols = slice(half * LANES, (half + 1) * LANES)
        scr[half] = val[:, cols]
        for r in range(d):
            out_ref[r, :, cols] = scr[half, pl.ds(r, tm // d, stride=d), :].astype(out_ref.dtype)


def _from_residues(ref, scr, d):
    if d == 1:
        return ref[0].astype(F32)
    rows = ref.shape[1]
    for half in range(GROUP_WIDTH // LANES):
        cols = slice(half * LANES, (half + 1) * LANES)
        for r in range(d):
            scr[half, pl.ds(r, rows, stride=d), :] = ref[r, :, cols].astype(F32)
    return jnp.concatenate([scr[half] for half in range(GROUP_WIDTH // LANES)], axis=1)


def _mix_in_fwd(x, g, w_in, gate_bias):
    L = x.shape[0]
    tm = min(ROW_TILE, L)

    def body(x_ref, g_ref, w_ref, gb_ref, *refs):
        qkv_refs, (u_ref, gate_ref, scr) = refs[:9], refs[9:]
        r, xhat = _rms(x_ref[...])
        h = (xhat * g_ref[...]).astype(BF16)
        for part, (c0, scale) in enumerate(((0, Q_SCALE), (_C_K, 1.0), (_C_V, 1.0))):
            z = _dot_nt(h, w_ref[c0:c0 + ATTN_WIDTH, :]) * scale
            for grp, d in enumerate(DILATIONS):
                _to_residues(z[:, grp * GROUP_WIDTH:(grp + 1) * GROUP_WIDTH], qkv_refs[3 * part + grp], scr, d)
        u_ref[...] = _dot_nt(h, w_ref[_C_U:_C_G, :])
        gate_ref[...] = _sigmoid(_dot_nt(h, w_ref[_C_G:IN_WIDTH, :]) + gb_ref[...])

    return pl.pallas_call(
        body, name="mix_in_fwd", grid=(L // tm,),
        in_specs=[_rows(tm, D_MODEL), _whole(), _whole(), _whole()],
        out_specs=[_residue_spec(d, tm) for d in DILATIONS] * 3 + [_rows(tm, SSM_WIDTH), _rows(tm, 2 * D_MODEL)],
        out_shape=[_residue_shape(d, L, BF16) for d in DILATIONS] * 3
        + [jax.ShapeDtypeStruct((L, SSM_WIDTH), F32), jax.ShapeDtypeStruct((L, 2 * D_MODEL), F32)],
        scratch_shapes=[_residue_scratch(tm)],
        compiler_params=_params(),
    )(x, g, w_in, gate_bias)


def _mix_in_bwd(dx2, x, g, dqkv, du, dgp, w_in):
    L = x.shape[0]
    tm = min(ROW_TILE, L)

    def body(dx2_ref, x_ref, g_ref, *refs):
        piece_refs = refs[:9]
        du_ref, dgp_ref, w_ref, dx1_ref, h_ref, dz_ref, dg_ref, scr = refs[9:]
        i = pl.program_id(0)
        gv = g_ref[...]
        r, xhat = _rms(x_ref[...])
        h_ref[...] = (xhat * gv).astype(BF16)
        for part in range(3):
            for grp, d in enumerate(DILATIONS):
                c0 = part * ATTN_WIDTH + grp * GROUP_WIDTH
                dz_ref[:, c0:c0 + GROUP_WIDTH] = _from_residues(piece_refs[3 * part + grp], scr, d).astype(BF16)
        dz_ref[:, _C_U:_C_G] = du_ref[...].astype(BF16)
        dz_ref[:, _C_G:IN_WIDTH] = dgp_ref[...]
        dh = _dot(dz_ref[...], w_ref[...])

        @pl.when(i == 0)
        def _():
            dg_ref[...] = jnp.zeros_like(dg_ref)

        dg_ref[...] += jnp.sum(dh * xhat, axis=0, keepdims=True)
        dx1_ref[...] = dx2_ref[...] + _rms_bwd(dh, gv, r, xhat)

    return pl.pallas_call(
        body, name="mix_in_bwd", grid=(L // tm,),
        in_specs=[_rows(tm, D_MODEL), _rows(tm, D_MODEL), _whole()] + [_residue_spec(d, tm) for d in DILATIONS] * 3
        + [_rows(tm, SSM_WIDTH), _rows(tm, 2 * D_MODEL), _whole()],
        out_specs=[_rows(tm, D_MODEL), _rows(tm, D_MODEL), _rows(tm, IN_WIDTH), _acc_row(D_MODEL)],
        out_shape=[jax.ShapeDtypeStruct((L, D_MODEL), F32), jax.ShapeDtypeStruct((L, D_MODEL), BF16),
                   jax.ShapeDtypeStruct((L, IN_WIDTH), BF16), jax.ShapeDtypeStruct((1, D_MODEL), F32)],
        scratch_shapes=[_residue_scratch(tm)],
        compiler_params=_params(),
    )(dx2, x, g, *dqkv, du, dgp, w_in)


def _bucket_onehot():
    qi = jnp.arange(ATTN_BLOCK)[:, None]
    kj = jnp.arange(2 * ATTN_BLOCK)[None, :]
    steps = jnp.maximum(qi + ATTN_BLOCK - kj, 0)
    max_exact = N_BUCKETS // 2
    out = []
    for d in DILATIONS:
        dist = steps * d
        df = jnp.maximum(dist, 1).astype(F32)
        large = max_exact + (jnp.log(df / max_exact) / math.log(MAX_DISTANCE / max_exact)
                             * (N_BUCKETS - max_exact)).astype(jnp.int32)
        large = jnp.minimum(large, N_BUCKETS - 1)
        bucket = jnp.where(dist < max_exact, dist, large).reshape(-1)
        out.append((bucket[None, :] == jnp.arange(N_BUCKETS)[:, None]).astype(F32))
    return jnp.stack(out)


def _bias_expand(table_t, onehot):
    n = onehot.shape[-1]

    def body(t_ref, oh_ref, o_ref):
        bias = _dot_exact(t_ref[...], oh_ref[...])
        col = lax.broadcasted_iota(jnp.int32, (8, n), 1)
        qi = col // (2 * ATTN_BLOCK)
        kj = col - qi * (2 * ATTN_BLOCK)
        steps = qi + ATTN_BLOCK - kj
        band = (steps >= 0) & (steps <= WINDOW_STEPS)
        o_ref[0] = jnp.where(band & (kj >= ATTN_BLOCK), bias, NEG_INF)
        o_ref[1] = jnp.where(band, bias, NEG_INF)

    return pl.pallas_call(
        body, name="bias_expand", grid=(3,),
        in_specs=[pl.BlockSpec((None, 8, N_BUCKETS), lambda g: (g, 0, 0)),
                  pl.BlockSpec((None, N_BUCKETS, n), lambda g: (g, 0, 0))],
        out_specs=pl.BlockSpec((None, 2, 8, n), lambda g: (g, 0, 0, 0)),
        out_shape=jax.ShapeDtypeStruct((3, 2, 8, n), F32),
        compiler_params=_params(),
    )(table_t, onehot)


def _bias_reduce(dsum, onehot):
    n = onehot.shape[-1]

    def body(d_ref, oh_ref, o_ref):
        o_ref[...] = _dot_nt_exact(d_ref[...], oh_ref[...])

    return pl.pallas_call(
        body, name="bias_reduce", grid=(3,),
        in_specs=[pl.BlockSpec((None, 8, n), lambda g: (g, 0, 0)),
                  pl.BlockSpec((None, N_BUCKETS, n), lambda g: (g, 0, 0))],
        out_specs=pl.BlockSpec((None, 8, N_BUCKETS), lambda g: (g, 0, 0)),
        out_shape=jax.ShapeDtypeStruct((3, 8, N_BUCKETS), F32),
        compiler_params=_params(),
    )(dsum, onehot)


def _head_of_col(rows):
    return lax.broadcasted_iota(jnp.int32, (rows, GROUP_WIDTH), 1) // HEAD_DIM


_STACK_ROWS = HEADS_PER_GROUP * ATTN_BLOCK


def _stack_heads(x, head_of_col):
    return jnp.concatenate([jnp.where(head_of_col == hh, x, jnp.zeros_like(x)) for hh in range(HEADS_PER_GROUP)],
                           axis=0)


def _attn_specs(qb):
    rows = qb * ATTN_BLOCK
    cur = pl.BlockSpec((None, rows, GROUP_WIDTH), lambda r, n: (r, n, 0))
    prev = pl.BlockSpec((None, ATTN_BLOCK, GROUP_WIDTH), lambda r, n: (r, jnp.maximum(n * qb - 1, 0), 0))
    bias = pl.BlockSpec((2, HEADS_PER_GROUP, ATTN_BLOCK, 2 * ATTN_BLOCK), lambda r, n: (0, 0, 0, 0))
    return cur, prev, bias


def _attn_fwd(q, k, v, bias, name):
    d, M, _ = q.shape
    nb = M // ATTN_BLOCK
    qb = min(ATTN_QB, nb)

    def body(q_ref, kp_ref, kc_ref, vp_ref, vc_ref, bias_ref, o_ref, lse_ref):
        n = pl.program_id(1)
        q_head = _head_of_col(ATTN_BLOCK)
        kwin = jnp.concatenate([kp_ref[...], kc_ref[...]], axis=0)
        vwin = jnp.concatenate([vp_ref[...], vc_ref[...]], axis=0)
        ones = jnp.ones((2 * ATTN_BLOCK, LANES), BF16)
        for b in range(qb):
            rows = slice(b * ATTN_BLOCK, (b + 1) * ATTN_BLOCK)
            window = slice(b * ATTN_BLOCK, (b + 2) * ATTN_BLOCK)
            variant = jnp.minimum(n, 1) if b == 0 else 1
            kk = kwin[window]
            vv = vwin[window]
            q4 = _stack_heads(q_ref[rows, :], q_head)
            logits = _dot_nt(q4, kk) + bias_ref[variant].reshape(_STACK_ROWS, 2 * ATTN_BLOCK)
            m = jnp.max(logits, axis=1, keepdims=True)
            p16 = jnp.exp(logits - m).astype(BF16)
            den = _dot(p16, ones)[:, 0:1]
            out = _dot(p16, vv) * (1.0 / den)
            lse = m + jnp.log(den)
            o_acc = jnp.zeros((ATTN_BLOCK, GROUP_WIDTH), F32)
            lse_acc = jnp.zeros((ATTN_BLOCK, GROUP_WIDTH), F32)
            for hh in range(HEADS_PER_GROUP):
                head_rows = slice(hh * ATTN_BLOCK, (hh + 1) * ATTN_BLOCK)
                o_acc = jnp.where(q_head == hh, out[head_rows], o_acc)
                lse_acc = jnp.where(q_head == hh, lse[head_rows], lse_acc)
            o_ref[rows, :] = o_acc
            lse_ref[rows, :] = lse_acc

    cur, prev, full = _attn_specs(qb)
    return pl.pallas_call(
        body, name=name, grid=(d, nb // qb),
        in_specs=[cur, prev, cur, prev, cur, full],
        out_specs=[cur, cur],
        out_shape=[jax.ShapeDtypeStruct((d, M, GROUP_WIDTH), F32)] * 2,
        compiler_params=_params(),
    )(q, k, k, v, v, bias)


def _attn_bwd(q, k, v, do, lse, delta, bias, name):
    d, M, _ = q.shape
    nb = M // ATTN_BLOCK
    qb = min(ATTN_QB, nb)
    ns = nb // qb
    rows_q = qb * ATTN_BLOCK
    last = slice(rows_q - ATTN_BLOCK, rows_q)

    def body(q_ref, kp_ref, kc_ref, vp_ref, vc_ref, do_ref, lse_ref, dl_ref, bias_ref,
             dq_ref, dk_ref, dv_ref, dsum_ref, pk_ref, pv_ref, wk_ref, wv_ref):
        r = pl.program_id(0)
        n = pl.program_id(1)

        @pl.when((r == 0) & (n == 0))
        def _():
            dsum_ref[...] = jnp.zeros_like(dsum_ref)

        @pl.when(n == 0)
        def _():
            pk_ref[...] = jnp.zeros_like(pk_ref)
            pv_ref[...] = jnp.zeros_like(pv_ref)

        @pl.when(n < ns)
        def _():
            q_head = _head_of_col(ATTN_BLOCK)
            kwin = jnp.concatenate([kp_ref[...], kc_ref[...]], axis=0)
            vwin = jnp.concatenate([vp_ref[...], vc_ref[...]], axis=0)
            wk_ref[...] = jnp.zeros_like(wk_ref)
            wv_ref[...] = jnp.zeros_like(wv_ref)
            for b in range(qb):
                rows = slice(b * ATTN_BLOCK, (b + 1) * ATTN_BLOCK)
                window = slice(b * ATTN_BLOCK, (b + 2) * ATTN_BLOCK)
                variant = jnp.minimum(n, 1) if b == 0 else 1
                kk = kwin[window]
                vv = vwin[window]
                q4 = _stack_heads(q_ref[rows, :], q_head)
                do4 = _stack_heads(do_ref[rows, :], q_head)
                heads = [hh * HEAD_DIM for hh in range(HEADS_PER_GROUP)]
                lse4 = jnp.concatenate([lse_ref[rows, c0:c0 + 1] for c0 in heads], axis=0)
                dl4 = jnp.concatenate([dl_ref[rows, c0:c0 + 1] for c0 in heads], axis=0)
                logits = _dot_nt(q4, kk) + bias_ref[variant].reshape(_STACK_ROWS, 2 * ATTN_BLOCK)
                p = jnp.exp(logits - lse4)
                ds = p * (_dot_nt(do4, vv) - dl4)
                dsum_ref[...] += ds.reshape(HEADS_PER_GROUP, ATTN_BLOCK, 2 * ATTN_BLOCK)
                ds16 = ds.astype(BF16)
                dq4 = _dot(ds16, kk)
                dq_acc = jnp.zeros((ATTN_BLOCK, GROUP_WIDTH), F32)
                for hh in range(HEADS_PER_GROUP):
                    dq_acc = jnp.where(q_head == hh, dq4[hh * ATTN_BLOCK:(hh + 1) * ATTN_BLOCK], dq_acc)
                dq_ref[rows, :] = (dq_acc * Q_SCALE).astype(BF16)
                wk_ref[window, :] += _dot_tn(ds16, q4)
                wv_ref[window, :] += _dot_tn(p.astype(BF16), do4)
            for out_ref, part_ref, win_ref in ((dk_ref, pk_ref, wk_ref), (dv_ref, pv_ref, wv_ref)):
                if ns == 1:
                    out_ref[...] = win_ref[ATTN_BLOCK:, :].astype(BF16)
                    continue
                if qb > 1:
                    out_ref[0:rows_q - ATTN_BLOCK, :] = part_ref[0:rows_q - ATTN_BLOCK, :].astype(BF16)
                out_ref[last, :] = (part_ref[last, :] + win_ref[0:ATTN_BLOCK, :]).astype(BF16)
                part_ref[...] = win_ref[ATTN_BLOCK:, :]

        if ns > 1:
            @pl.when(n == ns)
            def _():
                dk_ref[...] = pk_ref[...].astype(BF16)
                dv_ref[...] = pv_ref[...].astype(BF16)

    def clamp(n):
        return jnp.minimum(n, ns - 1)

    cur = pl.BlockSpec((None, rows_q, GROUP_WIDTH), lambda r, n: (r, clamp(n), 0))
    prev = pl.BlockSpec((None, ATTN_BLOCK, GROUP_WIDTH), lambda r, n: (r, jnp.maximum(clamp(n) * qb - 1, 0), 0))
    lag = pl.BlockSpec((None, rows_q, GROUP_WIDTH), lambda r, n: (r, jnp.maximum(n - 1, 0), 0))
    full = pl.BlockSpec((2, HEADS_PER_GROUP, ATTN_BLOCK, 2 * ATTN_BLOCK), lambda r, n: (0, 0, 0, 0))
    acc = pl.BlockSpec((HEADS_PER_GROUP, ATTN_BLOCK, 2 * ATTN_BLOCK), lambda r, n: (0, 0, 0))
    return pl.pallas_call(
        body, name=name, grid=(d, ns + 1 if ns > 1 else 1),
        in_specs=[cur, prev, cur, prev, cur, cur, cur, cur, full],
        out_specs=[cur, lag, lag, acc],
        out_shape=[jax.ShapeDtypeStruct((d, M, GROUP_WIDTH), BF16)] * 3
        + [jax.ShapeDtypeStruct((HEADS_PER_GROUP, ATTN_BLOCK, 2 * ATTN_BLOCK), F32)],
        scratch_shapes=[pltpu.VMEM((rows_q, GROUP_WIDTH), F32), pltpu.VMEM((rows_q, GROUP_WIDTH), F32),
                        pltpu.VMEM((rows_q + ATTN_BLOCK, GROUP_WIDTH), F32),
                        pltpu.VMEM((rows_q + ATTN_BLOCK, GROUP_WIDTH), F32)],
        compiler_params=_params(),
    )(q, k, k, v, v, do, lse, delta, bias)


def _disc_math(a_re, a_im, ldt, b_re, b_im):
    dt = jnp.exp(ldt)
    mag = jnp.exp(a_re * dt)
    ab_re = mag * jnp.cos(a_im * dt)
    ab_im = mag * jnp.sin(a_im * dt)
    den = a_re * a_re + a_im * a_im
    xr = ab_re - 1.0
    coef_re = (xr * a_re + ab_im * a_im) / den
    coef_im = (ab_im * a_re - xr * a_im) / den
    return ab_re, ab_im, coef_re * b_re - coef_im * b_im, coef_re * b_im + coef_im * b_re


def _block_diag_mask():
    row_g = lax.broadcasted_iota(jnp.int32, (SSM_WIDTH, 2 * NS), 0) // SSM_GROUP
    col = lax.broadcasted_iota(jnp.int32, (SSM_WIDTH, 2 * NS), 1)
    col_g = jnp.where(col >= NS, col - NS, col) // SSM_STATE
    return row_g == col_g


def _disc_fwd(a_re, a_im, ldt, b_re, b_im, c_re, c_im):
    def body(are_ref, aim_ref, ldt_ref, bre_ref, bim_ref, cre_ref, cim_ref, pw_ref, pwr_ref, bd_ref, cdt_ref):
        ab_re, ab_im, bb_re, bb_im = _disc_math(are_ref[...], aim_ref[...], ldt_ref[...], bre_ref[...], bim_ref[...])
        row = lax.broadcasted_iota(jnp.int32, (8, NS), 0)
        pr, pi = ab_re, ab_im
        t_re = jnp.zeros((8, NS), F32)
        t_im = jnp.zeros((8, NS), F32)
        u_re = jnp.zeros((8, NS), F32)
        u_im = jnp.zeros((8, NS), F32)
        for j in range(8):
            t_re = jnp.where(row == j, pr, t_re)
            t_im = jnp.where(row == j, pi, t_im)
            u_re = jnp.where(row == 7 - j, pr, u_re)
            u_im = jnp.where(row == 7 - j, pi, u_im)
            pr, pi = pr * ab_re - pi * ab_im, pr * ab_im + pi * ab_re
        pw_ref[0] = t_re
        pw_ref[1] = t_im
        pwr_ref[0] = u_re
        pwr_ref[1] = u_im
        mask = _block_diag_mask()
        zero = jnp.zeros((SSM_WIDTH, 2 * NS), F32)
        bfull = jnp.concatenate([jnp.concatenate([bb_re] * SSM_GROUPS, axis=0),
                                 jnp.concatenate([bb_im] * SSM_GROUPS, axis=0)], axis=1)
        bd_ref[...] = jnp.where(mask, bfull, zero).astype(BF16)
        cfull = jnp.concatenate([jnp.concatenate([cre_ref[...]] * SSM_GROUPS, axis=0),
                                 jnp.concatenate([-cim_ref[...]] * SSM_GROUPS, axis=0)], axis=1)
        cdt_ref[...] = jnp.where(mask, cfull, zero).astype(BF16)

    return pl.pallas_call(
        body, name="s5_disc_fwd",
        in_specs=[_whole()] * 7, out_specs=[_whole()] * 4,
        out_shape=[jax.ShapeDtypeStruct((2, 8, NS), F32), jax.ShapeDtypeStruct((2, 8, NS), F32),
                   jax.ShapeDtypeStruct((SSM_WIDTH, 2 * NS), BF16), jax.ShapeDtypeStruct((SSM_WIDTH, 2 * NS), BF16)],
        compiler_params=_params(),
    )(a_re, a_im, ldt, b_re, b_im, c_re, c_im)


def _disc_bwd(a_re, a_im, ldt, b_re, b_im, d_bd, d_cdt, d_ab, group_sum):
    def body(are_ref, aim_ref, ldt_ref, bre_ref, bim_ref, dbd_ref, dcdt_ref, dab_ref, gs_ref,
             dare_ref, daim_ref, dldt_ref, dbre_ref, dbim_ref, dcre_ref, dcim_ref):
        col = lax.broadcasted_iota(jnp.int32, (SSM_GROUP, 2 * NS), 1)
        col_g = jnp.where(col >= NS, col - NS, col) // SSM_STATE
        acc_b = jnp.zeros((SSM_GROUP, 2 * NS), F32)
        acc_c = jnp.zeros((SSM_GROUP, 2 * NS), F32)
        for g in range(SSM_GROUPS):
            rows = slice(g * SSM_GROUP, (g + 1) * SSM_GROUP)
            acc_b = acc_b + jnp.where(col_g == g, dbd_ref[rows, :], 0.0)
            acc_c = acc_c + jnp.where(col_g == g, dcdt_ref[rows, :], 0.0)
        dcre_ref[...] = acc_c[:, :NS]
        dcim_ref[...] = -acc_c[:, NS:]
        dab_re = jnp.sum(dab_ref[0], axis=0, keepdims=True)
        dab_im = jnp.sum(dab_ref[1], axis=0, keepdims=True)
        _, vjp = jax.vjp(_disc_math, are_ref[...], aim_ref[...], ldt_ref[...], bre_ref[...], bim_ref[...])
        d_are, d_aim, d_ldt, d_bre, d_bim = vjp((dab_re, dab_im, acc_b[:, :NS], acc_b[:, NS:]))
        dare_ref[...] = d_are
        daim_ref[...] = d_aim
        dbre_ref[...] = d_bre
        dbim_ref[...] = d_bim
        dldt_ref[...] = _dot_exact(jnp.broadcast_to(d_ldt, (8, NS)), gs_ref[...])

    vec = jax.ShapeDtypeStruct((1, NS), F32)
    mat = jax.ShapeDtypeStruct((SSM_GROUP, NS), F32)
    return pl.pallas_call(
        body, name="s5_disc_bwd",
        in_specs=[_whole()] * 9, out_specs=[_whole()] * 7,
        out_shape=[vec, vec, jax.ShapeDtypeStruct((8, 128), F32), mat, mat, mat, mat],
        compiler_params=_params(),
    )(a_re, a_im, ldt, b_re, b_im, d_bd, d_cdt, d_ab, group_sum)


def _scan_blocks(buf, pw_ref, carry_ref, n_blocks, reverse):
    row = lax.broadcasted_iota(jnp.int32, (8, SCAN_LANES), 0)
    for lc in range(NS // SCAN_LANES):
        re_cols = pl.ds(lc * SCAN_LANES, SCAN_LANES)
        im_cols = pl.ds(NS + lc * SCAN_LANES, SCAN_LANES)
        pr = pw_ref[0, :, re_cols]
        pi = pw_ref[1, :, re_cols]
        if reverse:
            pi = -pi
            base = [(7, 1), (6, 2), (4, 4)]
            coef = [(jnp.where(row < 8 - k, pr[j:j + 1], 0.0), jnp.where(row < 8 - k, pi[j:j + 1], 0.0), 8 - k)
                    for j, k in base]
        else:
            base = [(0, 1), (1, 2), (3, 4)]
            coef = [(jnp.where(row >= k, pr[j:j + 1], 0.0), jnp.where(row >= k, pi[j:j + 1], 0.0), k)
                    for j, k in base]

        def step(i, carry, pr=pr, pi=pi, coef=coef, re_cols=re_cols, im_cols=im_cols):
            cr, ci = carry
            blk = (n_blocks - 1 - i) if reverse else i
            rows = pl.ds(pl.multiple_of(blk * 8, 8), 8)
            xr = buf[rows, re_cols]
            xi = buf[rows, im_cols]
            for kr, ki, shift in coef:
                sr = pltpu.roll(xr, shift, 0)
                si = pltpu.roll(xi, shift, 0)
                xr, xi = xr + kr * sr - ki * si, xi + kr * si + ki * sr
            xr, xi = xr + pr * cr - pi * ci, xi + pr * ci + pi * cr
            buf[rows, re_cols] = xr
            buf[rows, im_cols] = xi
            edge = slice(0, 1) if reverse else slice(7, 8)
            return xr[edge], xi[edge]

        cr, ci = lax.fori_loop(0, n_blocks, step, (carry_ref[0:1, re_cols], carry_ref[0:1, im_cols]))
        carry_ref[0:1, re_cols] = cr
        carry_ref[0:1, im_cols] = ci


_SUPER_GROUPS = 16
_SUPER_BLOCKS = [
    (slice(k * _SUPER_GROUPS * SSM_GROUP, (k + 1) * _SUPER_GROUPS * SSM_GROUP),
     [slice(half + k * _SUPER_GROUPS * SSM_STATE, half + (k + 1) * _SUPER_GROUPS * SSM_STATE) for half in (0, NS)])
    for k in range(SSM_GROUPS // _SUPER_GROUPS)]


def _ssm_fwd(u, bd, cdt, d_skip, pw):
    L = u.shape[0]
    tc = min(SSM_FWD_CHUNK, L)

    def body(u_ref, bd_ref, cdt_ref, dsk_ref, pw_ref, y_ref, s_ref, carry_ref):
        @pl.when(pl.program_id(0) == 0)
        def _():
            carry_ref[...] = jnp.zeros_like(carry_ref)

        uv = u_ref[...]
        u16 = uv.astype(BF16)
        for ch, states in _SUPER_BLOCKS:
            for st in states:
                s_ref[:, st] = _dot(u16[:, ch], bd_ref[ch, st])
        _scan_blocks(s_ref, pw_ref, carry_ref, tc // 8, reverse=False)
        for ch, states in _SUPER_BLOCKS:
            y_ref[:, ch] = (sum(_dot_nt(s_ref[:, st].astype(BF16), cdt_ref[ch, st]) for st in states)
                            + dsk_ref[:, ch] * uv[:, ch])

    return pl.pallas_call(
        body, name="s5_fwd", grid=(L // tc,),
        in_specs=[_rows(tc, SSM_WIDTH), _whole(), _whole(), _whole(), _whole()],
        out_specs=[_rows(tc, SSM_WIDTH), _rows(tc, 2 * NS)],
        out_shape=[jax.ShapeDtypeStruct((L, SSM_WIDTH), F32), jax.ShapeDtypeStruct((L, 2 * NS), F32)],
        scratch_shapes=[pltpu.VMEM((8, 2 * NS), F32)],
        compiler_params=_params(),
    )(u, bd, cdt, d_skip, pw)


def _ssm_bwd(dy, u, s, bd, cdt, d_skip, pwr):
    L = u.shape[0]
    tc = min(SSM_CHUNK, L)
    nc = L // tc
    blocks = tc // 8

    def body(dy_ref, u_ref, s_ref, sprev_ref, bd_ref, cdt_ref, dsk_ref, pwr_ref,
             du_ref, ddsk_ref, dbd_ref, dcdt_ref, dab_ref, g_ref, sx_ref, carry_ref):
        i = pl.program_id(0)

        @pl.when(i == 0)
        def _():
            carry_ref[...] = jnp.zeros_like(carry_ref)
            ddsk_ref[...] = jnp.zeros_like(ddsk_ref)
            dbd_ref[...] = jnp.zeros_like(dbd_ref)
            dcdt_ref[...] = jnp.zeros_like(dcdt_ref)
            dab_ref[...] = jnp.zeros_like(dab_ref)

        dyv = dy_ref[...]
        uv = u_ref[...]
        dy16 = dyv.astype(BF16)
        u16 = uv.astype(BF16)
        for ch, states in _SUPER_BLOCKS:
            for st in states:
                g_ref[:, st] = _dot(dy16[:, ch], cdt_ref[ch, st])
        _scan_blocks(g_ref, pwr_ref, carry_ref, blocks, reverse=True)
        ddsk_ref[...] += jnp.sum(dyv * uv, axis=0, keepdims=True)
        for ch, states in _SUPER_BLOCKS:
            du = dsk_ref[:, ch] * dyv[:, ch]
            for st in states:
                g16 = g_ref[:, st].astype(BF16)
                du = du + _dot_nt(g16, bd_ref[ch, st])
                dbd_ref[ch, st] += _dot_tn(u16[:, ch], g16)
                dcdt_ref[ch, st] += _dot_tn(dy16[:, ch], s_ref[:, st].astype(BF16))
            du_ref[:, ch] = du

        sx_ref[pl.ds(8, tc), :] = s_ref[...]
        sx_ref[pl.ds(0, 8), :] = jnp.where(i == nc - 1, 0.0, sprev_ref[...])
        row = lax.broadcasted_iota(jnp.int32, (8, SCAN_LANES), 0)
        for lc in range(NS // SCAN_LANES):
            re_cols = pl.ds(lc * SCAN_LANES, SCAN_LANES)
            im_cols = pl.ds(NS + lc * SCAN_LANES, SCAN_LANES)

            def step(b, acc, re_cols=re_cols, im_cols=im_cols):
                ar, ai = acc
                off = pl.multiple_of(b * 8, 8)
                gr = g_ref[pl.ds(off, 8), re_cols]
                gi = g_ref[pl.ds(off, 8), im_cols]
                before = pl.ds(off, 8)
                here = pl.ds(off + 8, 8)
                sr = jnp.where(row == 0, sx_ref[before, re_cols][7:8], pltpu.roll(sx_ref[here, re_cols], 1, 0))
                si = jnp.where(row == 0, sx_ref[before, im_cols][7:8], pltpu.roll(sx_ref[here, im_cols], 1, 0))
                return ar + gr * sr + gi * si, ai + gi * sr - gr * si

            zero = jnp.zeros((8, SCAN_LANES), F32)
            ar, ai = lax.fori_loop(0, blocks, step, (zero, zero))
            dab_ref[0, :, re_cols] += ar
            dab_ref[1, :, re_cols] += ai

    rev = lambda i: (nc - 1 - i, 0)
    sprev = pl.BlockSpec((8, 2 * NS), lambda i: (jnp.maximum((nc - 1 - i) * blocks - 1, 0), 0))
    return pl.pallas_call(
        body, name="s5_bwd", grid=(nc,),
        in_specs=[pl.BlockSpec((tc, SSM_WIDTH), rev), pl.BlockSpec((tc, SSM_WIDTH), rev),
                  pl.BlockSpec((tc, 2 * NS), rev), sprev, _whole(), _whole(), _whole(), _whole()],
        out_specs=[pl.BlockSpec((tc, SSM_WIDTH), rev), _whole(), _whole(), _whole(), _whole()],
        out_shape=[jax.ShapeDtypeStruct((L, SSM_WIDTH), F32), jax.ShapeDtypeStruct((1, SSM_WIDTH), F32),
                   jax.ShapeDtypeStruct((SSM_WIDTH, 2 * NS), F32), jax.ShapeDtypeStruct((SSM_WIDTH, 2 * NS), F32),
                   jax.ShapeDtypeStruct((2, 8, NS), F32)],
        scratch_shapes=[pltpu.VMEM((tc, 2 * NS), F32), pltpu.VMEM((tc + 8, 2 * NS), F32), pltpu.VMEM((8, 2 * NS), F32)],
        compiler_params=_params(),
    )(dy, u, s, s, bd, cdt, d_skip, pwr)


def _branches(o_attn, y, gates, w_ab, w_glu, w_sb):
    ya = _dot(o_attn.astype(BF16), w_ab[...])
    gel = _gelu(y)
    glu = _dot(gel.astype(BF16), w_glu[...])
    p = glu[:, :SSM_WIDTH]
    sg = _sigmoid(glu[:, SSM_WIDTH:])
    ys2 = p * sg
    ysb = _dot(ys2.astype(BF16), w_sb[...])
    ga = gates[:, :D_MODEL]
    gs = gates[:, D_MODEL:]
    return ya, gel, p, sg, ys2, ysb, ga, gs


def _mix_out_fwd(x1, o_g, lse_g, y, gates, w_ab, w_glu, w_sb, w_out):
    L = x1.shape[0]
    tm = min(ROW_TILE, L)

    def body(x_ref, o0, o1, o2, l0, l1, l2, y_ref, gate_ref, wab_ref, wglu_ref, wsb_ref, wout_ref,
             x2_ref, oat_ref, lse0, lse1, lse2, scr):
        la, lb, lc = (_from_residues(ref, scr, d) for ref, d in zip((l0, l1, l2), DILATIONS))
        m = jnp.maximum(jnp.maximum(la, lb), lc)
        ea, eb, ec = jnp.exp(la - m), jnp.exp(lb - m), jnp.exp(lc - m)
        tot = ea + eb + ec
        oa, ob, oc = (_from_residues(ref, scr, d) for ref, d in zip((o0, o1, o2), DILATIONS))
        o_attn = (ea * oa + eb * ob + ec * oc) / tot
        oat_ref[...] = o_attn
        lse = m + jnp.log(tot)
        for ref, d in zip((lse0, lse1, lse2), DILATIONS):
            _to_residues(lse, ref, scr, d)
        ya, _, _, _, _, ysb, ga, gs = _branches(o_attn, y_ref[...], gate_ref[...], wab_ref, wglu_ref, wsb_ref)
        mix = ga * ya + gs * ysb
        x2_ref[...] = x_ref[...] + _dot(mix.astype(BF16), wout_ref[...])

    res = [_residue_spec(d, tm) for d in DILATIONS]
    return pl.pallas_call(
        body, name="mix_out_fwd", grid=(L // tm,),
        in_specs=[_rows(tm, D_MODEL)] + res * 2 + [_rows(tm, SSM_WIDTH), _rows(tm, 2 * D_MODEL)] + [_whole()] * 4,
        out_specs=[_rows(tm, D_MODEL), _rows(tm, GROUP_WIDTH)] + res,
        out_shape=[jax.ShapeDtypeStruct((L, D_MODEL), F32), jax.ShapeDtypeStruct((L, GROUP_WIDTH), F32)]
        + [_residue_shape(d, L, F32) for d in DILATIONS],
        scratch_shapes=[_residue_scratch(tm)],
        compiler_params=_params(),
    )(x1, *o_g, *lse_g, y, gates, w_ab, w_glu, w_sb, w_out)


def _mix_out_bwd(dx2, o_attn, y, gates, w_ab, w_glu, w_sb, w_out, head_sum):
    L = dx2.shape[0]
    tm = min(ROW_TILE, L)

    def body(dx_ref, oat_ref, y_ref, gate_ref, wab_ref, wglu_ref, wsb_ref, wout_ref, hs_ref,
             do0, do1, do2, dl0, dl1, dl2, dy_ref, dgp_ref, mix_ref, dya_ref, dys_ref, ys2_ref, gel_ref, dglu_ref,
             dgb_ref, scr):
        i = pl.program_id(0)
        o_attn = oat_ref[...]
        yv = y_ref[...]
        ya, gel, p, sg, ys2, ysb, ga, gs = _branches(o_attn, yv, gate_ref[...], wab_ref, wglu_ref, wsb_ref)
        mix_ref[...] = (ga * ya + gs * ysb).astype(BF16)
        ys2_ref[...] = ys2.astype(BF16)
        gel_ref[...] = gel.astype(BF16)
        dmix = _dot_nt(dx_ref[...].astype(BF16), wout_ref[...])
        dgp = jnp.concatenate([dmix * ya * ga * (1.0 - ga), dmix * ysb * gs * (1.0 - gs)], axis=1)
        dgp_ref[...] = dgp.astype(BF16)

        @pl.when(i == 0)
        def _():
            dgb_ref[...] = jnp.zeros_like(dgb_ref)

        dgb_ref[...] += jnp.sum(dgp, axis=0, keepdims=True)
        dya = (dmix * ga).astype(BF16)
        dys = (dmix * gs).astype(BF16)
        dya_ref[...] = dya
        dys_ref[...] = dys
        d_o = _dot_nt(dya, wab_ref[...])
        delta = _dot_exact(d_o * o_attn, hs_ref[...])
        for do_ref, dl_ref, d in zip((do0, do1, do2), (dl0, dl1, dl2), DILATIONS):
            _to_residues(d_o, do_ref, scr, d)
            _to_residues(delta, dl_ref, scr, d)
        dys2 = _dot_nt(dys, wsb_ref[...])
        dglu = jnp.concatenate([dys2 * sg, dys2 * p * sg * (1.0 - sg)], axis=1).astype(BF16)
        dglu_ref[...] = dglu
        dy_ref[...] = _dot_nt(dglu, wglu_ref[...]) * _gelu_grad(yv)

    grp = _rows(tm, GROUP_WIDTH)
    wide = _rows(tm, D_MODEL)
    half = _rows(tm, SSM_WIDTH)
    res = [_residue_spec(d, tm) for d in DILATIONS]
    sds = jax.ShapeDtypeStruct
    return pl.pallas_call(
        body, name="mix_out_bwd", grid=(L // tm,),
        in_specs=[wide, grp, half, _rows(tm, 2 * D_MODEL)] + [_whole()] * 5,
        out_specs=res + res + [half, _rows(tm, 2 * D_MODEL), wide, wide, wide, half, half, wide, _acc_row(2 * D_MODEL)],
        out_shape=[_residue_shape(d, L, BF16) for d in DILATIONS] + [_residue_shape(d, L, F32) for d in DILATIONS]
        + [sds((L, SSM_WIDTH), F32),
           sds((L, 2 * D_MODEL), BF16), sds((L, D_MODEL), BF16), sds((L, D_MODEL), BF16),
           sds((L, D_MODEL), BF16), sds((L, SSM_WIDTH), BF16), sds((L, SSM_WIDTH), BF16),
           sds((L, D_MODEL), BF16), sds((1, 2 * D_MODEL), F32)],
        scratch_shapes=[_residue_scratch(tm)],
        compiler_params=_params(),
    )(dx2, o_attn, y, gates, w_ab, w_glu, w_sb, w_out, head_sum)


def _adamw(w, g, m, v, name):
    R, C = w.shape
    tr = _row_tile(R, max(8, ADAMW_BLOCK_BYTES // (4 * C)))

    def body(w_ref, g_ref, m_ref, v_ref, d_ref, mo_ref, vo_ref):
        gv = g_ref[...]
        mn = ADAM_B1 * m_ref[...] + (1.0 - ADAM_B1) * gv
        vn = ADAM_B2 * v_ref[...] + (1.0 - ADAM_B2) * (gv * gv)
        m_hat = mn / (1.0 - ADAM_B1 ** ADAM_STEP)
        v_hat = vn / (1.0 - ADAM_B2 ** ADAM_STEP)
        d_ref[...] = -ADAM_LR * (m_hat / (jnp.sqrt(v_hat) + ADAM_EPS) + ADAM_WD * w_ref[...])
        mo_ref[...] = mn
        vo_ref[...] = vn

    blk = pl.BlockSpec((tr, C), lambda i: (i, 0))
    return pl.pallas_call(
        body, name=name, grid=(R // tr,),
        in_specs=[blk] * 4, out_specs=[blk] * 3,
        out_shape=[jax.ShapeDtypeStruct((R, C), F32)] * 3,
        compiler_params=_params(),
    )(w, g, m, v)


def _sum_chips_into_half(u, t, name):
    S, H, C = u.shape
    tr = _row_tile(H, 512)
    hb = H // tr

    def body(s_ref, t_ref, a_ref, b_ref, c_ref, o_ref):
        me = s_ref[1]
        others = (a_ref[...], b_ref[...], c_ref[...])
        acc = None
        for chip in range(S):
            below = others[min(chip, S - 2)]
            above = others[max(chip - 1, 0)]
            term = jnp.where(me == chip, t_ref[...], jnp.where(me > chip, below, above)).astype(F32)
            acc = term if acc is None else acc + term
        o_ref[...] = acc

    x, y, c = lax.axis_index("x"), lax.axis_index("y"), lax.axis_index("c")
    me = 2 * x + y
    scalars = jnp.stack([c, me] + [j + (j >= me).astype(jnp.int32) for j in range(S - 1)]).astype(jnp.int32)
    blk = (None, tr, C)
    return pl.pallas_call(
        body, name=name,
        grid_spec=pltpu.PrefetchScalarGridSpec(
            num_scalar_prefetch=1, grid=(hb,),
            in_specs=[pl.BlockSpec(blk, lambda i, s: (s[1], i, 0))]
            + [pl.BlockSpec(blk, functools.partial(lambda j, i, s: (s[2 + j], i, 0), j)) for j in range(S - 1)],
            out_specs=pl.BlockSpec((tr, C), lambda i, s: (s[0] * hb + i, 0))),
        out_shape=jax.ShapeDtypeStruct((2 * H, C), F32),
        compiler_params=_params(),
    )(scalars, t, u, u, u)


def _add_halves(g, r1, name):
    S, R, C = g.shape
    H = R // 2
    tr = _row_tile(H, 512)
    hb = H // tr

    def body(c_ref, g_ref, r_ref, o_ref):
        o_ref[...] = (g_ref[...] + r_ref[...]).astype(BF16)

    core = lax.axis_index("c").astype(jnp.int32).reshape(1)
    return pl.pallas_call(
        body, name=name,
        grid_spec=pltpu.PrefetchScalarGridSpec(
            num_scalar_prefetch=1, grid=(S, hb),
            in_specs=[pl.BlockSpec((None, tr, C), lambda j, i, c_ref: (j, c_ref[0] * hb + i, 0)),
                      pl.BlockSpec((None, tr, C), lambda j, i, c_ref: (j, i, 0))],
            out_specs=pl.BlockSpec((None, tr, C), lambda j, i, c_ref: (j, i, 0))),
        out_shape=jax.ShapeDtypeStruct((S, H, C), BF16),
        compiler_params=_params(),
    )(core, g, r1)


_ANY = pl.BlockSpec(memory_space=pl.ANY)


def _place():
    x, y, c = lax.axis_index("x"), lax.axis_index("y"), lax.axis_index("c")
    chips = [(1 - x, y), (x, 1 - y), (1 - x, 1 - y)]
    return x, y, c, chips


def _remote(src, dst, send_sems, recv_sems, k, device):
    return pltpu.make_async_remote_copy(src_ref=src, dst_ref=dst, send_sem=send_sems.at[k], recv_sem=recv_sems.at[k],
                                        device_id=device, device_id_type=MESH)


def _gather_parts(shapes, w_refs, out_refs, send_sems, recv_sems):
    n = len(shapes)
    x, y, c, chips = _place()
    me = 2 * x + y
    sibling = (x, y, 1 - c)

    def half(k, chip_idx, core):
        H = shapes[k][0] // 2
        return out_refs[k].at[chip_idx, pl.ds(core * H, H), :]

    mine = [_remote(w_refs[k], out_refs[k].at[me], send_sems, recv_sems, 6 * n + k, sibling) for k in range(n)]
    first = []
    for k in range(n):
        H = shapes[k][0] // 2
        for j, (cx, cy) in enumerate(chips):
            first.append(_remote(w_refs[k].at[pl.ds(c * H, H), :], half(k, me, c), send_sems, recv_sems,
                                 3 * k + j, (cx, cy, c)))

    def start():
        for cp in mine + first:
            cp.start()

    def finish():
        passed = []
        for k in range(n):
            for j, (cx, cy) in enumerate(chips):
                landed = half(k, 2 * cx + cy, c)
                _remote(landed, landed, send_sems, recv_sems, 3 * k + j, (cx, cy, c)).wait_recv()
                fwd = _remote(landed, landed, send_sems, recv_sems, 3 * n + 3 * k + j, sibling)
                fwd.start()
                passed.append(fwd)
        for k in range(n):
            for j, (cx, cy) in enumerate(chips):
                other = half(k, 2 * cx + cy, 1 - c)
                _remote(other, other, send_sems, recv_sems, 3 * n + 3 * k + j, sibling).wait_recv()
        for cp in mine:
            cp.wait_recv()
        for cp in first + passed + mine:
            cp.wait_send()

    return start, finish


def _gather_weights(shards, name):
    n = len(shards)

    def body(*refs):
        x, y, c, chips = _place()
        _handshake([(x, y, 1 - c)] + [(cx, cy, c) for cx, cy in chips])
        start, finish = _gather_parts([w.shape for w in shards], refs[:n], refs[n:2 * n], *refs[2 * n:2 * n + 2])
        start()
        finish()

    return _sequenced(body, name, shards, [jax.ShapeDtypeStruct((N_SHARD,) + w.shape, w.dtype) for w in shards],
                      7 * n, COLLECTIVE_IDS["gather"])


def _handshake(peers):
    barrier = pltpu.get_barrier_semaphore()
    for peer in peers:
        pl.semaphore_signal(barrier, inc=1, device_id=peer, device_id_type=MESH)
    pl.semaphore_wait(barrier, len(peers))


def _sequenced(body, name, ins, out_shapes, n_sems, collective_id):
    return pl.kernel(
        body, out_type=list(out_shapes), mesh=plsc.ScalarSubcoreMesh(axis_name="sequencer", num_cores=1), name=name,
        scratch_types=(pltpu.SemaphoreType.DMA((n_sems,)), pltpu.SemaphoreType.DMA((n_sems,))),
        compiler_params=pltpu.CompilerParams(collective_id=collective_id))(*ins)


def _swap_halves(gs, name, collective_id):
    n = len(gs)

    def body(*refs):
        g_refs, out_refs = refs[:n], refs[n:2 * n]
        send_sems, recv_sems = refs[2 * n:]
        x, y, c, _ = _place()
        _handshake([(x, y, 1 - c)])
        cps = []
        for k in range(n):
            H = gs[k].shape[1] // 2
            cp = _remote(g_refs[k].at[:, pl.ds((1 - c) * H, H), :], out_refs[k], send_sems, recv_sems, k, (x, y, 1 - c))
            cp.start()
            cps.append(cp)
        for cp in cps:
            cp.wait()

    return _sequenced(body, name, gs, [jax.ShapeDtypeStruct((g.shape[0], g.shape[1] // 2, g.shape[2]), g.dtype)
                                       for g in gs], n, collective_id)


def _exchange_chips(ts, name, collective_id):
    n = len(ts)

    def body(*refs):
        t_refs, out_refs = refs[:n], refs[n:2 * n]
        send_sems, recv_sems = refs[2 * n:]
        x, y, c, chips = _place()
        me = 2 * x + y
        _handshake([(cx, cy, c) for cx, cy in chips])
        sent = []
        for k in range(n):
            for j, (cx, cy) in enumerate(chips):
                cp = _remote(t_refs[k].at[2 * cx + cy], out_refs[k].at[me], send_sems, recv_sems, 3 * k + j, (cx, cy, c))
                cp.start()
                sent.append(cp)
        for k in range(n):
            for j, (cx, cy) in enumerate(chips):
                slot = out_refs[k].at[2 * cx + cy]
                _remote(slot, slot, send_sems, recv_sems, 3 * k + j, (cx, cy, c)).wait_recv()
        for cp in sent:
            cp.wait_send()

    return _sequenced(body, name, ts, [jax.ShapeDtypeStruct(t.shape, t.dtype) for t in ts], 3 * n, collective_id)


def _join_halves(fs, name):
    n = len(fs)

    def body(*refs):
        out_refs = refs[n:2 * n]
        send_sems, recv_sems, _ = refs[2 * n:]
        x, y, c, _ = _place()
        sent = []
        for k in range(n):
            H = fs[k].shape[0] // 2
            here = out_refs[k].at[pl.ds(c * H, H), :]
            cp = _remote(here, here, send_sems, recv_sems, k, (x, y, 1 - c))
            cp.start()
            sent.append(cp)
        for k in range(n):
            H = fs[k].shape[0] // 2
            other = out_refs[k].at[pl.ds((1 - c) * H, H), :]
            _remote(other, other, send_sems, recv_sems, k, (x, y, 1 - c)).wait_recv()
        for cp in sent:
            cp.wait_send()

    return pl.pallas_call(
        body, name=name,
        in_specs=[_ANY] * n, out_specs=[_ANY] * n,
        out_shape=[jax.ShapeDtypeStruct(f.shape, f.dtype) for f in fs],
        input_output_aliases={k: k for k in range(n)},
        scratch_shapes=[pltpu.SemaphoreType.DMA((n,)), pltpu.SemaphoreType.DMA((n,)), pltpu.SemaphoreType.DMA((1,))],
    )(*fs)


def _gather_small(v):
    R, C = v.shape

    def body(v_ref, out_ref, send_sems, recv_sems):
        x, y, c, _ = _place()
        me = 4 * x + 2 * y + c
        flips = [(fx, fy, fc) for fx in (0, 1) for fy in (0, 1) for fc in (0, 1)][1:]
        peers = [((1 - x) if fx else x, (1 - y) if fy else y, (1 - c) if fc else c) for fx, fy, fc in flips]
        _handshake(peers)
        sent = []
        for j, peer in enumerate(peers):
            cp = _remote(v_ref, out_ref.at[me], send_sems, recv_sems, j, peer)
            cp.start()
            sent.append(cp)
        for j, peer in enumerate(peers):
            slot = out_ref.at[4 * peer[0] + 2 * peer[1] + peer[2]]
            _remote(slot, slot, send_sems, recv_sems, j, peer).wait_recv()
        for cp in sent:
            cp.wait_send()

    return _sequenced(body, "gather_small", [v], [jax.ShapeDtypeStruct((8, R, C), F32)], 7,
                      COLLECTIVE_IDS["gather_small"])[0]


def _sum_devices(x, own, name):
    S, R, C = x.shape
    tr = _row_tile(R, 2048)

    def body(s_ref, x_ref, own_ref, o_ref):
        me = s_ref[0]
        acc = None
        for k in range(S):
            term = jnp.where(me == k, own_ref[...], x_ref[k])
            acc = term if acc is None else acc + term
        o_ref[...] = acc

    x_, y_, c_ = lax.axis_index("x"), lax.axis_index("y"), lax.axis_index("c")
    me = (4 * x_ + 2 * y_ + c_).astype(jnp.int32).reshape(1)
    return pl.pallas_call(
        body, name=name,
        grid_spec=pltpu.PrefetchScalarGridSpec(
            num_scalar_prefetch=1, grid=(R // tr,),
            in_specs=[pl.BlockSpec((S, tr, C), lambda i, s: (0, i, 0)), pl.BlockSpec((tr, C), lambda i, s: (i, 0))],
            out_specs=pl.BlockSpec((tr, C), lambda i, s: (i, 0))),
        out_shape=jax.ShapeDtypeStruct((R, C), F32),
        compiler_params=_params(),
    )(me, x, own)


def _after(earlier, arrays):
    return lax.optimization_barrier((earlier, arrays))


def _reduce_swap(gs, tag, earlier):
    gs = _after(earlier, gs)[1]
    return gs, _swap_halves(gs, "reduce_swap_" + tag, COLLECTIVE_IDS["swap_" + tag])


def _reduce_exchange(gs, r1, names, tag, later_than):
    r1 = _after(later_than, r1)[1]
    ts = [_add_halves(g, r, "reduce_add_cores_" + nm) for g, r, nm in zip(gs, r1, names)]
    us = _exchange_chips(ts, "reduce_exchange_" + tag, COLLECTIVE_IDS["exchange_" + tag])
    return us, ts


def _reduce_finish(us, ts, names, tag):
    fs = [_sum_chips_into_half(u, t, "reduce_add_chips_" + nm) for u, t, nm in zip(us, ts, names)]
    return _join_halves(fs, "reduce_join_" + tag)


BIG = ["ffn1_w_gate", "ffn1_w_up", "ffn1_w_down", "w_in", "ssm_w_glu", "w_attn_branch", "w_ssm_branch",
       "w_out", "ffn2_w_gate", "ffn2_w_up", "ffn2_w_down"]
SMALL = ["ffn1_norm", "mix_norm", "gate_bias", "rel_bias_table", "ssm_a_re", "ssm_a_im", "ssm_log_dt",
         "ssm_b_re", "ssm_b_im", "ssm_c_re", "ssm_c_im", "ssm_d", "ffn2_norm", "final_norm"]
ORDER = ["ffn1_norm", "ffn1_w_gate", "ffn1_w_up", "ffn1_w_down", "mix_norm", "w_in", "gate_bias", "rel_bias_table",
         "ssm_a_re", "ssm_a_im", "ssm_log_dt", "ssm_b_re", "ssm_b_im", "ssm_c_re", "ssm_c_im", "ssm_d",
         "ssm_w_glu", "w_attn_branch", "w_ssm_branch", "w_out", "ffn2_norm", "ffn2_w_gate", "ffn2_w_up",
         "ffn2_w_down", "final_norm"]


_SMALL_TILE = 8 * LANES


def _pack_small(arrays):
    rows = []
    for a in arrays:
        flat = a.reshape(-1).astype(F32)
        rows.append(jnp.pad(flat, (0, (-flat.shape[0]) % _SMALL_TILE)).reshape(-1, LANES))
    return jnp.concatenate(rows, axis=0)


def _unpack_small(packed, shapes):
    out, r0 = [], 0
    for shp in shapes:
        n = math.prod(shp)
        rows = 8 * -(-n // _SMALL_TILE)
        out.append(packed[r0:r0 + rows].reshape(-1)[:n].reshape(shp))
        r0 += rows
    return out


def _split_cols(g):
    K, N = g.shape
    return g.reshape(K, N_SHARD, N // N_SHARD).transpose(1, 0, 2)


def _join_cols(w):
    S, K, n = w.shape
    return w.transpose(1, 0, 2).reshape(K, S * n)


COL_SHARDED = ("ssm_w_glu", "w_attn_branch", "w_ssm_branch")
TRANSPOSED = ("ffn1_w_gate", "ffn1_w_up", "ffn2_w_gate", "ffn2_w_up", "w_in")


def _shard_2d(name, arr):
    two_d = arr.reshape(arr.shape[-2:])
    return two_d.T if name in TRANSPOSED else two_d


def _shard_nd(name, two_d, shape):
    return (two_d.T if name in TRANSPOSED else two_d).reshape(shape)


class _GradSync:
    def __init__(self, weights, moms, vels):
        self.weights, self.moms, self.vels = weights, moms, vels
        self.grads, self.delta, self.new_m, self.new_v = {}, {}, {}, {}
        self.loss = None
        self._earlier = []
        self._swapped = {}
        self._exchanged = {}

    def swap(self, tag, gw, later_than=()):
        gs = []
        for n in REDUCE_GROUPS[tag]:
            g = gw[n]
            if n in COL_SHARDED:
                g = _split_cols(g)
            elif n in ("w_out", "w_in"):
                g = g.reshape(N_SHARD, g.shape[0] // N_SHARD, g.shape[1])
            gs.append(g)
        self._swapped[tag] = _reduce_swap(gs, tag, list(self._earlier) + list(later_than))
        self._earlier = self._swapped[tag][1]

    def exchange(self, tag, later_than):
        gs, r1 = self._swapped[tag]
        us, ts = _reduce_exchange(gs, r1, REDUCE_GROUPS[tag], tag, later_than)
        self._exchanged[tag] = (us, ts)
        self._earlier = us

    def small_ready(self, gs, loss_blk, later_than=()):
        _, (mine,) = _after(list(self._earlier) + list(later_than),
                            [_pack_small([gs[n] for n in SMALL] + [loss_blk[0:1, :]])])
        others = _gather_small(mine)
        self._exchanged["small"] = (others, mine)
        self._earlier = [others]

    def finish(self, tag):
        made = []
        if tag == "small":
            others, mine = self._exchanged[tag]
            shapes = [self.weights[n].shape for n in SMALL]
            total = _unpack_small(_sum_devices(others, mine, "sum_small"), shapes + [(128,)])
            self.loss = total[-1][0]
            self.grads.update(zip(SMALL, total[:-1]))
            packed = [_pack_small([src[n] for n in SMALL]) for src in (self.weights, self.grads, self.moms, self.vels)]
            for dst, res in zip((self.delta, self.new_m, self.new_v), _adamw(*packed, "adamw_small")):
                dst.update(zip(SMALL, _unpack_small(res, shapes)))
            for n in SMALL:
                made += [self.grads[n], self.delta[n], self.new_m[n], self.new_v[n]]
            return made + [self.loss]
        names = REDUCE_GROUPS[tag]
        us, ts = self._exchanged[tag]
        for n, g in zip(names, _reduce_finish(us, ts, names, tag)):
            shp = self.weights[n].shape
            d, m, v = _adamw(_shard_2d(n, self.weights[n]), g, _shard_2d(n, self.moms[n]), _shard_2d(n, self.vels[n]),
                             "adamw_" + n)
            self.grads[n], self.delta[n] = _shard_nd(n, g, shp), _shard_nd(n, d, shp)
            self.new_m[n], self.new_v[n] = _shard_nd(n, m, shp), _shard_nd(n, v, shp)
            made += [self.grads[n], self.delta[n], self.new_m[n], self.new_v[n]]
        return made

    def finish_all(self):
        self.exchange("ffn1", later_than=self.finish("ffn2"))
        for tag in ("mixer", "w_in", "small", "ffn1"):
            self.finish(tag)


def _local_step(x, target, w, later, small, sync):
    L = x.shape[0]
    row = lambda v: v.reshape(1, -1)

    a_re, a_im = small["ssm_a_re"].reshape(1, NS), small["ssm_a_im"].reshape(1, NS)
    ldt = jnp.repeat(small["ssm_log_dt"].reshape(SSM_GROUPS), SSM_STATE).reshape(1, NS)
    to_cn = lambda b: b.reshape(SSM_GROUPS, SSM_STATE, SSM_GROUP).transpose(2, 0, 1).reshape(SSM_GROUP, NS)
    c_to_cn = lambda c: c.reshape(SSM_GROUPS, SSM_GROUP, SSM_STATE).transpose(1, 0, 2).reshape(SSM_GROUP, NS)
    b_re, b_im = to_cn(small["ssm_b_re"]), to_cn(small["ssm_b_im"])
    c_re, c_im = c_to_cn(small["ssm_c_re"]), c_to_cn(small["ssm_c_im"])
    d_skip = row(small["ssm_d"])
    pw, pwr, bd, cdt = _disc_fwd(a_re, a_im, ldt, b_re, b_im, c_re, c_im)

    onehot = _bucket_onehot()
    table_t = small["rel_bias_table"].T.reshape(3, HEADS_PER_GROUP, N_BUCKETS)
    table_t = jnp.pad(table_t, ((0, 0), (0, 8 - HEADS_PER_GROUP), (0, 0)))
    bias = _bias_expand(table_t, onehot)[:, :, :HEADS_PER_GROUP].reshape(
        3, 2, HEADS_PER_GROUP, ATTN_BLOCK, 2 * ATTN_BLOCK)

    n1, nm, n2, nf = row(small["ffn1_norm"]), row(small["mix_norm"]), row(small["ffn2_norm"]), row(small["final_norm"])
    gate_bias = row(small["gate_bias"])

    x1, a1, b1, *later_full = _ffn_fwd(x, n1, w["ffn1_w_gate"], w["ffn1_w_up"], w["ffn1_w_down"], "ffn1_fwd",
                                       carried=list(later.values()))
    w = dict(w, **dict(zip(later, later_full)))
    for n in COL_SHARDED:
        w[n] = _join_cols(w[n])
    w["w_out"] = w["w_out"].reshape(D_MODEL, D_MODEL)
    w["w_in"] = w["w_in"].reshape(IN_WIDTH, D_MODEL)
    *qkv, u, gates = _mix_in_fwd(x1, nm, w["w_in"], gate_bias)
    q, k, v = qkv[0:3], qkv[3:6], qkv[6:9]
    o_g, lse_g = [], []
    for grp in range(3):
        o, lse = _attn_fwd(q[grp], k[grp], v[grp], bias[grp], f"attn_fwd_{grp}")
        o_g.append(o)
        lse_g.append(lse)
    y, s = _ssm_fwd(u, bd, cdt, d_skip, pw)
    x2, o_attn, *lse_tot = _mix_out_fwd(x1, o_g, lse_g, y, gates, w["w_attn_branch"], w["ssm_w_glu"],
                                        w["w_ssm_branch"], w["w_out"])
    x3, a2, b2 = _ffn_fwd(x2, n2, w["ffn2_w_gate"], w["ffn2_w_up"], w["ffn2_w_down"], "ffn2_fwd")
    loss_blk, dx3, d_nf = _loss_fwd_bwd(x3, nf, target)

    gw, gs = {}, {}
    gs["final_norm"] = d_nf

    dx2, da, db, sact, h, d_out, gs["ffn2_norm"] = _ffn_bwd(dx3, x2, n2, a2, b2, w["ffn2_w_gate"], w["ffn2_w_up"],
                                                            w["ffn2_w_down"], "ffn2_bwd")
    gw["ffn2_w_gate"] = _matmul_tn(da, h[None], "ffn2_dw_gate")
    gw["ffn2_w_up"] = _matmul_tn(db, h[None], "ffn2_dw_up")
    gw["ffn2_w_down"] = _matmul_tn(sact, d_out[None], "ffn2_dw_down")
    sync.swap("ffn2", gw)

    head_sum = (jnp.arange(GROUP_WIDTH)[:, None] // HEAD_DIM == jnp.arange(GROUP_WIDTH)[None, :] // HEAD_DIM).astype(F32)
    (*d_o_delta, dy, dgp, mix, dya, dys, ys2, gel, dglu, gs["gate_bias"]) = _mix_out_bwd(
        dx2, o_attn, y, gates, w["w_attn_branch"], w["ssm_w_glu"], w["w_ssm_branch"], w["w_out"], head_sum)
    sync.exchange("ffn2", later_than=[dy])
    d_o, delta = d_o_delta[0:3], d_o_delta[3:6]
    gw["w_out"] = _matmul_tn(mix[None], dx2[None], "dw_out")[0]
    gw["w_attn_branch"] = _matmul_tn(o_attn[None], dya[None], "dw_attn_branch")[0]
    gw["w_ssm_branch"] = _matmul_tn(ys2[None], dys[None], "dw_ssm_branch")[0]
    gw["ssm_w_glu"] = _matmul_tn(gel[None], dglu[None], "dw_glu")[0]

    dqs, dks, dvs, dsums = [], [], [], []
    for grp in range(3):
        dq, dk, dv, dsum = _attn_bwd(q[grp], k[grp], v[grp], d_o[grp], lse_tot[grp], delta[grp], bias[grp],
                                     f"attn_bwd_{grp}")
        dqs.append(dq)
        dks.append(dk)
        dvs.append(dv)
        dsums.append(dsum.reshape(HEADS_PER_GROUP, -1))
    dsum_all = jnp.pad(jnp.stack(dsums), ((0, 0), (0, 8 - HEADS_PER_GROUP), (0, 0)))
    d_table = _bias_reduce(dsum_all, onehot)[:, :HEADS_PER_GROUP]
    gs["rel_bias_table"] = d_table.reshape(3 * HEADS_PER_GROUP, N_BUCKETS).T

    du, gs["ssm_d"], d_bd, d_cdt, d_ab = _ssm_bwd(dy, u, s, bd, cdt, d_skip, pwr)
    sync.swap("mixer", gw, later_than=[du])
    sync.exchange("mixer", later_than=[dqs[2]])
    group_sum =(jnp.arange(NS)[:, None] // SSM_STATE == jnp.arange(128)[None, :]).astype(F32)
    d_are, d_aim, d_ldt, d_bre, d_bim, d_cre, d_cim = _disc_bwd(a_re, a_im, ldt, b_re, b_im, d_bd, d_cdt, d_ab, group_sum)
    gs["ssm_a_re"], gs["ssm_a_im"] = d_are, d_aim
    gs["ssm_log_dt"] = d_ldt[0, :SSM_GROUPS]
    from_cn = lambda t: t.reshape(SSM_GROUP, SSM_GROUPS, SSM_STATE).transpose(1, 2, 0)
    c_from_cn = lambda t: t.reshape(SSM_GROUP, SSM_GROUPS, SSM_STATE).transpose(1, 0, 2)
    gs["ssm_b_re"], gs["ssm_b_im"] = from_cn(d_bre), from_cn(d_bim)
    gs["ssm_c_re"], gs["ssm_c_im"] = c_from_cn(d_cre), c_from_cn(d_cim)

    dx1, hm, dz, gs["mix_norm"] = _mix_in_bwd(dx2, x1, nm, dqs + dks + dvs, du, dgp, w["w_in"])
    gw["w_in"] = _matmul_tn(dz[None], hm[None], "dw_in")[0]
    sync.swap("w_in", gw)

    dx0, da, db, sact, h, d_out, gs["ffn1_norm"] = _ffn_bwd(dx1, x, n1, a1, b1, w["ffn1_w_gate"], w["ffn1_w_up"],
                                                            w["ffn1_w_down"], "ffn1_bwd")
    sync.exchange("w_in", later_than=[dx0])
    gw["ffn1_w_gate"] = _matmul_tn(da, h[None], "ffn1_dw_gate")
    gw["ffn1_w_up"] = _matmul_tn(db, h[None], "ffn1_dw_up")
    sync.small_ready(gs, loss_blk, later_than=[gw["ffn1_w_up"]])
    gw["ffn1_w_down"] = _matmul_tn(sact, d_out[None], "ffn1_dw_down")
    sync.swap("ffn1", gw)
    return dx0


def kernel(x, ffn1_norm, ffn1_w_gate, ffn1_w_up, ffn1_w_down, mix_norm, w_in, gate_bias, rel_bias_table, ssm_a_re, ssm_a_im, ssm_log_dt, ssm_b_re, ssm_b_im, ssm_c_re, ssm_c_im, ssm_d, ssm_w_glu, w_attn_branch, w_ssm_branch, w_out, ffn2_norm, ffn2_w_gate, ffn2_w_up, ffn2_w_down, final_norm, loss_target, m_ffn1_norm, m_ffn1_w_gate, m_ffn1_w_up, m_ffn1_w_down, m_mix_norm, m_w_in, m_gate_bias, m_rel_bias_table, m_ssm_a_re, m_ssm_a_im, m_ssm_log_dt, m_ssm_b_re, m_ssm_b_im, m_ssm_c_re, m_ssm_c_im, m_ssm_d, m_ssm_w_glu, m_w_attn_branch, m_w_ssm_branch, m_w_out, m_ffn2_norm, m_ffn2_w_gate, m_ffn2_w_up, m_ffn2_w_down, m_final_norm, v_ffn1_norm, v_ffn1_w_gate, v_ffn1_w_up, v_ffn1_w_down, v_mix_norm, v_w_in, v_gate_bias, v_rel_bias_table, v_ssm_a_re, v_ssm_a_im, v_ssm_log_dt, v_ssm_b_re, v_ssm_b_im, v_ssm_c_re, v_ssm_c_im, v_ssm_d, v_ssm_w_glu, v_w_attn_branch, v_w_ssm_branch, v_w_out, v_ffn2_norm, v_ffn2_w_gate, v_ffn2_w_up, v_ffn2_w_down, v_final_norm):
    args = dict(locals())
    weights = {n: args[n] for n in ORDER}
    moms = {n: args["m_" + n] for n in ORDER}
    vels = {n: args["v_" + n] for n in ORDER}

    shard2d = {n: _shard_2d(n, weights[n]) for n in BIG}
    first, rest = BIG[:3], BIG[3:]
    full = dict(zip(first, _gather_weights([shard2d[n].astype(BF16) for n in first], "gather_ffn1_weights")))
    later = {n: shard2d[n].astype(BF16) for n in rest}

    small = {n: weights[n] for n in SMALL}
    sync = _GradSync(weights, moms, vels)
    grad_x = _local_step(x[0], loss_target[0], full, later, small, sync)
    sync.finish_all()
    return (sync.loss, grad_x[None], *[sync.grads[n] for n in ORDER], *[sync.delta[n] for n in ORDER],
            *[sync.new_m[n] for n in ORDER], *[sync.new_v[n] for n in ORDER])
```
